```python
import math
import jax
import jax.numpy as jnp
from jax import lax
import numpy as np

D_MODEL = 1024
BATCH = 16
SEQ = 2048
DEPTH = 2

CHUNK = 64
Q_BLOCK = 128
D_FF = 4 * D_MODEL
NORM_EPS = 1e-6

FOX_HEADS = 8
FOX_HEAD_DIM = D_MODEL // 16
FOX_W = FOX_HEADS * FOX_HEAD_DIM
FOX_F_BIAS_MEAN = 2.0
GDN_HEADS = 4
GDN_HEAD_DIM = D_MODEL // 8
GDN_W = GDN_HEADS * GDN_HEAD_DIM
GDN_CONV = 4
EVEN_SPLIT = (FOX_W, FOX_W, FOX_W, FOX_HEADS,
              GDN_W, GDN_W, GDN_W, GDN_HEADS, GDN_HEADS, GDN_W)
EVEN_IN = 3 * FOX_W + FOX_HEADS + 4 * GDN_W + 2 * GDN_HEADS

HGRN_EXPAND = 128
HGRN_HEADS = D_MODEL // HGRN_EXPAND
HGRN_HEAD_DIM = HGRN_EXPAND
ODD_IN = 4 * D_MODEL

kernel_name = "fox_gdn_hgrn2_hybrid_trunk"


def rms_norm(x, gain):
    xf = x.astype(jnp.float32)
    y = xf * lax.rsqrt(jnp.mean(xf * xf, axis=-1, keepdims=True) + NORM_EPS)
    return (y * gain.astype(jnp.float32)).astype(x.dtype)


def l2_normalize(x):
    xf = x.astype(jnp.float32)
    return xf * lax.rsqrt(jnp.sum(xf * xf, axis=-1, keepdims=True) + NORM_EPS)


def split_cols(z, sizes):
    out, start = [], 0
    for s in sizes:
        out.append(z[..., start:start + s])
        start += s
    return out


def causal_depthwise_conv(x, w):
    k_w, c = w.shape
    return lax.conv_general_dilated(
        x, w[:, None, :].astype(x.dtype), window_strides=(1,),
        padding=[(k_w - 1, 0)], dimension_numbers=("NWC", "WIO", "NWC"),
        feature_group_count=c)


def squared_relu_mlp(h, w1, w2):
    a = jax.nn.relu(h @ w1)
    return (a * a) @ w2


def forgetting_attention(q, k, v, f_logit):
    b, t, h, dh = q.shape
    c = jnp.cumsum(jax.nn.log_sigmoid(f_logit.astype(jnp.float32)), axis=1)
    c = c.transpose(0, 2, 1)
    scale = dh ** -0.5
    outs = []
    for i in range(t // Q_BLOCK):
        q0, q1 = i * Q_BLOCK, (i + 1) * Q_BLOCK
        s = jnp.einsum("bqhd,bkhd->bhqk", q[:, q0:q1], k[:, :q1]).astype(jnp.float32) * scale
        s = s + c[:, :, q0:q1, None] - c[:, :, None, :q1]
        mask = (q0 + jnp.arange(Q_BLOCK))[:, None] >= jnp.arange(q1)[None, :]
        s = jnp.where(mask, s, -jnp.inf)
        p = jax.nn.softmax(s, axis=-1).astype(v.dtype)
        outs.append(jnp.einsum("bhqk,bkhd->bqhd", p, v[:, :q1]))
    return jnp.concatenate(outs, axis=1)


def gated_delta_rule(q, k, v, log_alpha, beta):
    b, h, t, dk = q.shape
    dv = v.shape[-1]
    n = t // CHUNK
    rs = lambda a: a.reshape(b, h, n, CHUNK, *a.shape[3:])
    q = rs(q * dk ** -0.5)
    k, v = rs(k), rs(v)
    beta = rs(beta)
    g = jnp.cumsum(rs(log_alpha), axis=-1)
    idx = jnp.arange(CHUNK)
    causal = idx[:, None] >= idx[None, :]
    strict = idx[:, None] > idx[None, :]
    decay = jnp.exp(jnp.where(causal, g[..., :, None] - g[..., None, :], -jnp.inf))
    k_beta = k * beta[..., None]
    a_mat = jnp.where(strict, jnp.einsum("bhnck,bhndk->bhncd", k_beta, k) * decay, 0.0)
    eye = jnp.eye(CHUNK, dtype=jnp.float32)
    rhs = jnp.concatenate([v * beta[..., None], k_beta * jnp.exp(g)[..., None]], axis=-1)
    wu = lax.linalg.triangular_solve(eye + a_mat, rhs, left_side=True, lower=True,
                                     unit_diagonal=True)
    u, w = wu[..., :dv], wu[..., dv:]
    qk = jnp.where(causal, jnp.einsum("bhnck,bhndk->bhncd", q, k) * decay, 0.0)
    q_dec = q * jnp.exp(g)[..., None]
    g_last = g[..., -1]
    k_dec = k * jnp.exp(g_last[..., None] - g)[..., None]

    def step(s_state, xs):
        qk_i, q_dec_i, k_dec_i, u_i, w_i, gl_i = xs
        v_new = u_i - jnp.einsum("bhck,bhkv->bhcv", w_i, s_state)
        o = (jnp.einsum("bhck,bhkv->bhcv", q_dec_i, s_state)
             + jnp.einsum("bhcd,bhdv->bhcv", qk_i, v_new))
        s_state = (s_state * jnp.exp(gl_i)[..., None, None]
                   + jnp.einsum("bhck,bhcv->bhkv", k_dec_i, v_new))
        return s_state, o

    xs = tuple(jnp.moveaxis(a, 2, 0) for a in (qk, q_dec, k_dec, u, w, g_last))
    s0 = jnp.zeros((b, h, dk, dv), jnp.float32)
    _, o = lax.scan(step, s0, xs)
    return jnp.moveaxis(o, 0, 2).reshape(b, h, t, dv)


def hgrn2_recurrence(q, k, i_val, log_f):
    b, h, t, dk = q.shape
    dv = i_val.shape[-1]
    n = t // CHUNK
    rs = lambda a: a.reshape(b, h, n, CHUNK, a.shape[-1])
    q, k, i_val = rs(q), rs(k), rs(i_val)
    bcum = jnp.cumsum(rs(log_f), axis=-2)
    b_last = bcum[..., -1, :]
    q_dec = q * jnp.exp(bcum)
    k_dec = k * jnp.exp(b_last[..., None, :] - bcum)
    idx = jnp.arange(CHUNK)
    causal = (idx[:, None] >= idx[None, :])[:, :, None]

    def step(s_state, xs):
        q_i, k_i, v_i, b_i, q_dec_i, k_dec_i, bl_i = xs
        diff = b_i[:, :, :, None, :] - b_i[:, :, None, :, :]
        dec = jnp.exp(jnp.where(causal, diff, -jnp.inf))
        a_mat = jnp.einsum("bhtk,bhsk,bhtsk->bhts", q_i, k_i, dec)
        o = (jnp.einsum("bhtk,bhkv->bhtv", q_dec_i, s_state)
             + jnp.einsum("bhts,bhsv->bhtv", a_mat, v_i))
        s_state = (s_state * jnp.exp(bl_i)[..., None]
                   + jnp.einsum("bhsk,bhsv->bhkv", k_dec_i, v_i))
        return s_state, o

    xs = tuple(jnp.moveaxis(a, 2, 0) for a in (q, k, i_val, bcum, q_dec, k_dec, b_last))
    s0 = jnp.zeros((b, h, dk, dv), jnp.float32)
    _, o = lax.scan(step, s0, xs)
    return jnp.moveaxis(o, 0, 2).reshape(b, h, t, dv)


def fox_gdn_mixer(h, w_in, fox_q_norm, fox_k_norm, fox_f_bias, gdn_conv, gdn_A_log,
                  gdn_dt_bias, gdn_o_norm, w_out):
    b, t, _ = h.shape
    z = h @ w_in
    fq, fk, fv, ff, gq, gk, gv, gb, ga, gg = split_cols(z, EVEN_SPLIT)
    fq = rms_norm(fq.reshape(b, t, FOX_HEADS, FOX_HEAD_DIM), fox_q_norm)
    fk = rms_norm(fk.reshape(b, t, FOX_HEADS, FOX_HEAD_DIM), fox_k_norm)
    fv = fv.reshape(b, t, FOX_HEADS, FOX_HEAD_DIM)
    fox_o = forgetting_attention(fq, fk, fv, ff + fox_f_bias)
    fox_o = fox_o.reshape(b, t, FOX_W).astype(h.dtype)
    conv = jax.nn.silu(causal_depthwise_conv(jnp.concatenate([gq, gk, gv], axis=-1), gdn_conv))
    cq, ck, cv = split_cols(conv, (GDN_W, GDN_W, GDN_W))
    to_heads = lambda a: a.reshape(b, t, GDN_HEADS, GDN_HEAD_DIM).transpose(0, 2, 1, 3)
    q = to_heads(l2_normalize(cq.reshape(b, t, GDN_HEADS, GDN_HEAD_DIM)).reshape(b, t, GDN_W))
    k = to_heads(l2_normalize(ck.reshape(b, t, GDN_HEADS, GDN_HEAD_DIM)).reshape(b, t, GDN_W))
    v = to_heads(cv.astype(jnp.float32))
    beta = jax.nn.sigmoid(gb.astype(jnp.float32)).transpose(0, 2, 1)
    log_alpha = (-jnp.exp(gdn_A_log.astype(jnp.float32))
                 * jax.nn.softplus(ga.astype(jnp.float32) + gdn_dt_bias.astype(jnp.float32)))
    gdn_o = gated_delta_rule(q, k, v, log_alpha.transpose(0, 2, 1), beta)
    gdn_o = gdn_o.transpose(0, 2, 1, 3)
    gate = jax.nn.silu(gg.astype(jnp.float32)).reshape(b, t, GDN_HEADS, GDN_HEAD_DIM)
    gdn_o = (rms_norm(gdn_o, gdn_o_norm) * gate).reshape(b, t, GDN_W).astype(h.dtype)
    return jnp.concatenate([fox_o, gdn_o], axis=-1) @ w_out


def hgrn2_mixer(h, w_in, lower_bound, o_norm, w_out):
    b, t, _ = h.shape
    z = h @ w_in
    zq, zf, zi, zg = split_cols(z, (D_MODEL, D_MODEL, D_MODEL, D_MODEL))
    q = jax.nn.silu(zq.astype(jnp.float32))
    f = lower_bound + (1.0 - lower_bound) * jax.nn.sigmoid(zf.astype(jnp.float32))
    k = 1.0 - f
    log_f = jnp.log(f)
    to_heads = lambda a: a.reshape(b, t, HGRN_HEADS, HGRN_HEAD_DIM).transpose(0, 2, 1, 3)
    o = hgrn2_recurrence(to_heads(q), to_heads(k), to_heads(zi.astype(jnp.float32)),
                         to_heads(log_f))
    o = o.transpose(0, 2, 1, 3)
    gate = jax.nn.silu(zg.astype(jnp.float32)).reshape(b, t, HGRN_HEADS, HGRN_HEAD_DIM)
    o = (rms_norm(o, o_norm) * gate).reshape(b, t, D_MODEL).astype(h.dtype)
    return o @ w_out


def _fwd_setup_inputs(seed: int = 0) -> dict:
    key = jax.random.key(seed)
    ks = jax.random.split(key, 24)
    f32 = jnp.float32

    def dense(k, fan_in, fan_out):
        return jax.random.normal(k, (fan_in, fan_out), f32) * fan_in ** -0.5

    def gain(k, n):
        return 1.0 + 0.1 * jax.random.normal(k, (n,), f32)

    dt = jnp.exp(jax.random.uniform(ks[8], (GDN_HEADS,), f32, math.log(1e-3), math.log(1e-1)))
    return {
        "x": jax.random.normal(ks[0], (BATCH, SEQ, D_MODEL), f32),
        "l0_mix_norm": gain(ks[1], D_MODEL),
        "l0_w_in": dense(ks[2], D_MODEL, EVEN_IN),
        "l0_fox_q_norm": gain(ks[3], FOX_HEAD_DIM),
        "l0_fox_k_norm": gain(ks[4], FOX_HEAD_DIM),
        "l0_fox_f_bias": FOX_F_BIAS_MEAN + 0.1 * jax.random.normal(ks[5], (FOX_HEADS,), f32),
        "l0_gdn_conv": jax.random.normal(ks[6], (GDN_CONV, 3 * GDN_W), f32) * GDN_CONV ** -0.5,
        "l0_gdn_A_log": jnp.log(jax.random.uniform(ks[7], (GDN_HEADS,), f32, 1.0, 16.0)),
        "l0_gdn_dt_bias": dt + jnp.log(-jnp.expm1(-dt)),
        "l0_gdn_o_norm": gain(ks[9], GDN_HEAD_DIM),
        "l0_w_out": dense(ks[10], FOX_W + GDN_W, D_MODEL),
        "l0_ffn_norm": gain(ks[11], D_MODEL),
        "l0_w_ff1": dense(ks[12], D_MODEL, D_FF),
        "l0_w_ff2": dense(ks[13], D_FF, D_MODEL),
        "l1_mix_norm": gain(ks[14], D_MODEL),
        "l1_w_in": dense(ks[15], D_MODEL, ODD_IN),
        "l1_hgrn_o_norm": gain(ks[16], HGRN_HEAD_DIM),
        "l1_w_out": dense(ks[17], D_MODEL, D_MODEL),
        "l1_ffn_norm": gain(ks[18], D_MODEL),
        "l1_w_ff1": dense(ks[19], D_MODEL, D_FF),
        "l1_w_ff2": dense(ks[20], D_FF, D_MODEL),
        "hgrn_lb_logits": 0.5 * jax.random.normal(ks[21], (DEPTH, D_MODEL), f32),
    }


def _fwd_reference(x, l0_mix_norm, l0_w_in, l0_fox_q_norm, l0_fox_k_norm, l0_fox_f_bias,
              l0_gdn_conv, l0_gdn_A_log, l0_gdn_dt_bias, l0_gdn_o_norm, l0_w_out,
              l0_ffn_norm, l0_w_ff1, l0_w_ff2, l1_mix_norm, l1_w_in, l1_hgrn_o_norm,
              l1_w_out, l1_ffn_norm, l1_w_ff1, l1_w_ff2, hgrn_lb_logits):
    lb_soft = jax.nn.softmax(hgrn_lb_logits.astype(jnp.float32), axis=0)
    lower_bounds = jnp.cumsum(lb_soft, axis=0) - lb_soft[0]
    layers = (
        dict(mix_norm=l0_mix_norm, ffn_norm=l0_ffn_norm, w_ff1=l0_w_ff1, w_ff2=l0_w_ff2),
        dict(mix_norm=l1_mix_norm, ffn_norm=l1_ffn_norm, w_ff1=l1_w_ff1, w_ff2=l1_w_ff2),
    )
    for l in range(DEPTH):
        p = layers[l]
        h = rms_norm(x, p["mix_norm"])
        if l % 2 == 0:
            mix = fox_gdn_mixer(h, l0_w_in, l0_fox_q_norm, l0_fox_k_norm, l0_fox_f_bias,
                                l0_gdn_conv, l0_gdn_A_log, l0_gdn_dt_bias, l0_gdn_o_norm,
                                l0_w_out)
        else:
            mix = hgrn2_mixer(h, l1_w_in, lower_bounds[l], l1_hgrn_o_norm, l1_w_out)
        x = x + mix.astype(x.dtype)
        x = x + squared_relu_mlp(rms_norm(x, p["ffn_norm"]), p["w_ff1"], p["w_ff2"]).astype(x.dtype)
    return x


import jax as _jax
import jax.numpy as _jnp

TWIN_FORMAT = 'train_step'
FWD_PARAMS = ['x', 'l0_mix_norm', 'l0_w_in', 'l0_fox_q_norm', 'l0_fox_k_norm', 'l0_fox_f_bias', 'l0_gdn_conv', 'l0_gdn_A_log', 'l0_gdn_dt_bias', 'l0_gdn_o_norm', 'l0_w_out', 'l0_ffn_norm', 'l0_w_ff1', 'l0_w_ff2', 'l1_mix_norm', 'l1_w_in', 'l1_hgrn_o_norm', 'l1_w_out', 'l1_ffn_norm', 'l1_w_ff1', 'l1_w_ff2', 'hgrn_lb_logits']
TWIN_WEIGHTS = ['l0_mix_norm', 'l0_w_in', 'l0_fox_q_norm', 'l0_fox_k_norm', 'l0_fox_f_bias', 'l0_gdn_conv', 'l0_gdn_A_log', 'l0_gdn_dt_bias', 'l0_gdn_o_norm', 'l0_w_out', 'l0_ffn_norm', 'l0_w_ff1', 'l0_w_ff2', 'l1_mix_norm', 'l1_w_in', 'l1_hgrn_o_norm', 'l1_w_out', 'l1_ffn_norm', 'l1_w_ff1', 'l1_w_ff2', 'hgrn_lb_logits']
TWIN_DIFF_INPUT = 'x'
TWIN_INPUTS = ['x', 'l0_mix_norm', 'l0_w_in', 'l0_fox_q_norm', 'l0_fox_k_norm', 'l0_fox_f_bias', 'l0_gdn_conv', 'l0_gdn_A_log', 'l0_gdn_dt_bias', 'l0_gdn_o_norm', 'l0_w_out', 'l0_ffn_norm', 'l0_w_ff1', 'l0_w_ff2', 'l1_mix_norm', 'l1_w_in', 'l1_hgrn_o_norm', 'l1_w_out', 'l1_ffn_norm', 'l1_w_ff1', 'l1_w_ff2', 'hgrn_lb_logits', 'loss_target', 'm_l0_mix_norm', 'm_l0_w_in', 'm_l0_fox_q_norm', 'm_l0_fox_k_norm', 'm_l0_fox_f_bias', 'm_l0_gdn_conv', 'm_l0_gdn_A_log', 'm_l0_gdn_dt_bias', 'm_l0_gdn_o_norm', 'm_l0_w_out', 'm_l0_ffn_norm', 'm_l0_w_ff1', 'm_l0_w_ff2', 'm_l1_mix_norm', 'm_l1_w_in', 'm_l1_hgrn_o_norm', 'm_l1_w_out', 'm_l1_ffn_norm', 'm_l1_w_ff1', 'm_l1_w_ff2', 'm_hgrn_lb_logits', 'v_l0_mix_norm', 'v_l0_w_in', 'v_l0_fox_q_norm', 'v_l0_fox_k_norm', 'v_l0_fox_f_bias', 'v_l0_gdn_conv', 'v_l0_gdn_A_log', 'v_l0_gdn_dt_bias', 'v_l0_gdn_o_norm', 'v_l0_w_out', 'v_l0_ffn_norm', 'v_l0_w_ff1', 'v_l0_w_ff2', 'v_l1_mix_norm', 'v_l1_w_in', 'v_l1_hgrn_o_norm', 'v_l1_w_out', 'v_l1_ffn_norm', 'v_l1_w_ff1', 'v_l1_w_ff2', 'v_hgrn_lb_logits']
TWIN_OUTPUTS = ['loss', 'grad_x', 'grad_l0_mix_norm', 'grad_l0_w_in', 'grad_l0_fox_q_norm', 'grad_l0_fox_k_norm', 'grad_l0_fox_f_bias', 'grad_l0_gdn_conv', 'grad_l0_gdn_A_log', 'grad_l0_gdn_dt_bias', 'grad_l0_gdn_o_norm', 'grad_l0_w_out', 'grad_l0_ffn_norm', 'grad_l0_w_ff1', 'grad_l0_w_ff2', 'grad_l1_mix_norm', 'grad_l1_w_in', 'grad_l1_hgrn_o_norm', 'grad_l1_w_out', 'grad_l1_ffn_norm', 'grad_l1_w_ff1', 'grad_l1_w_ff2', 'grad_hgrn_lb_logits', 'delta_l0_mix_norm', 'delta_l0_w_in', 'delta_l0_fox_q_norm', 'delta_l0_fox_k_norm', 'delta_l0_fox_f_bias', 'delta_l0_gdn_conv', 'delta_l0_gdn_A_log', 'delta_l0_gdn_dt_bias', 'delta_l0_gdn_o_norm', 'delta_l0_w_out', 'delta_l0_ffn_norm', 'delta_l0_w_ff1', 'delta_l0_w_ff2', 'delta_l1_mix_norm', 'delta_l1_w_in', 'delta_l1_hgrn_o_norm', 'delta_l1_w_out', 'delta_l1_ffn_norm', 'delta_l1_w_ff1', 'delta_l1_w_ff2', 'delta_hgrn_lb_logits', 'new_m_l0_mix_norm', 'new_m_l0_w_in', 'new_m_l0_fox_q_norm', 'new_m_l0_fox_k_norm', 'new_m_l0_fox_f_bias', 'new_m_l0_gdn_conv', 'new_m_l0_gdn_A_log', 'new_m_l0_gdn_dt_bias', 'new_m_l0_gdn_o_norm', 'new_m_l0_w_out', 'new_m_l0_ffn_norm', 'new_m_l0_w_ff1', 'new_m_l0_w_ff2', 'new_m_l1_mix_norm', 'new_m_l1_w_in', 'new_m_l1_hgrn_o_norm', 'new_m_l1_w_out', 'new_m_l1_ffn_norm', 'new_m_l1_w_ff1', 'new_m_l1_w_ff2', 'new_m_hgrn_lb_logits', 'new_v_l0_mix_norm', 'new_v_l0_w_in', 'new_v_l0_fox_q_norm', 'new_v_l0_fox_k_norm', 'new_v_l0_fox_f_bias', 'new_v_l0_gdn_conv', 'new_v_l0_gdn_A_log', 'new_v_l0_gdn_dt_bias', 'new_v_l0_gdn_o_norm', 'new_v_l0_w_out', 'new_v_l0_ffn_norm', 'new_v_l0_w_ff1', 'new_v_l0_w_ff2', 'new_v_l1_mix_norm', 'new_v_l1_w_in', 'new_v_l1_hgrn_o_norm', 'new_v_l1_w_out', 'new_v_l1_ffn_norm', 'new_v_l1_w_ff1', 'new_v_l1_w_ff2', 'new_v_hgrn_lb_logits']
TWIN_LEAF_KINDS = {'loss': 'loss', 'grad_x': 'grad_x', 'grad_l0_mix_norm': 'grad_w', 'grad_l0_w_in': 'grad_w', 'grad_l0_fox_q_norm': 'grad_w', 'grad_l0_fox_k_norm': 'grad_w', 'grad_l0_fox_f_bias': 'grad_w', 'grad_l0_gdn_conv': 'grad_w', 'grad_l0_gdn_A_log': 'grad_w', 'grad_l0_gdn_dt_bias': 'grad_w', 'grad_l0_gdn_o_norm': 'grad_w', 'grad_l0_w_out': 'grad_w', 'grad_l0_ffn_norm': 'grad_w', 'grad_l0_w_ff1': 'grad_w', 'grad_l0_w_ff2': 'grad_w', 'grad_l1_mix_norm': 'grad_w', 'grad_l1_w_in': 'grad_w', 'grad_l1_hgrn_o_norm': 'grad_w', 'grad_l1_w_out': 'grad_w', 'grad_l1_ffn_norm': 'grad_w', 'grad_l1_w_ff1': 'grad_w', 'grad_l1_w_ff2': 'grad_w', 'grad_hgrn_lb_logits': 'grad_w', 'delta_l0_mix_norm': 'delta_w', 'delta_l0_w_in': 'delta_w', 'delta_l0_fox_q_norm': 'delta_w', 'delta_l0_fox_k_norm': 'delta_w', 'delta_l0_fox_f_bias': 'delta_w', 'delta_l0_gdn_conv': 'delta_w', 'delta_l0_gdn_A_log': 'delta_w', 'delta_l0_gdn_dt_bias': 'delta_w', 'delta_l0_gdn_o_norm': 'delta_w', 'delta_l0_w_out': 'delta_w', 'delta_l0_ffn_norm': 'delta_w', 'delta_l0_w_ff1': 'delta_w', 'delta_l0_w_ff2': 'delta_w', 'delta_l1_mix_norm': 'delta_w', 'delta_l1_w_in': 'delta_w', 'delta_l1_hgrn_o_norm': 'delta_w', 'delta_l1_w_out': 'delta_w', 'delta_l1_ffn_norm': 'delta_w', 'delta_l1_w_ff1': 'delta_w', 'delta_l1_w_ff2': 'delta_w', 'delta_hgrn_lb_logits': 'delta_w', 'new_m_l0_mix_norm': 'new_m', 'new_m_l0_w_in': 'new_m', 'new_m_l0_fox_q_norm': 'new_m', 'new_m_l0_fox_k_norm': 'new_m', 'new_m_l0_fox_f_bias': 'new_m', 'new_m_l0_gdn_conv': 'new_m', 'new_m_l0_gdn_A_log': 'new_m', 'new_m_l0_gdn_dt_bias': 'new_m', 'new_m_l0_gdn_o_norm': 'new_m', 'new_m_l0_w_out': 'new_m', 'new_m_l0_ffn_norm': 'new_m', 'new_m_l0_w_ff1': 'new_m', 'new_m_l0_w_ff2': 'new_m', 'new_m_l1_mix_norm': 'new_m', 'new_m_l1_w_in': 'new_m', 'new_m_l1_hgrn_o_norm': 'new_m', 'new_m_l1_w_out': 'new_m', 'new_m_l1_ffn_norm': 'new_m', 'new_m_l1_w_ff1': 'new_m', 'new_m_l1_w_ff2': 'new_m', 'new_m_hgrn_lb_logits': 'new_m', 'new_v_l0_mix_norm': 'new_v', 'new_v_l0_w_in': 'new_v', 'new_v_l0_fox_q_norm': 'new_v', 'new_v_l0_fox_k_norm': 'new_v', 'new_v_l0_fox_f_bias': 'new_v', 'new_v_l0_gdn_conv': 'new_v', 'new_v_l0_gdn_A_log': 'new_v', 'new_v_l0_gdn_dt_bias': 'new_v', 'new_v_l0_gdn_o_norm': 'new_v', 'new_v_l0_w_out': 'new_v', 'new_v_l0_ffn_norm': 'new_v', 'new_v_l0_w_ff1': 'new_v', 'new_v_l0_w_ff2': 'new_v', 'new_v_l1_mix_norm': 'new_v', 'new_v_l1_w_in': 'new_v', 'new_v_l1_hgrn_o_norm': 'new_v', 'new_v_l1_w_out': 'new_v', 'new_v_l1_ffn_norm': 'new_v', 'new_v_l1_w_ff1': 'new_v', 'new_v_l1_w_ff2': 'new_v', 'new_v_hgrn_lb_logits': 'new_v'}


def _forward(args):
    return _fwd_reference(*[args[k] for k in FWD_PARAMS])


def _output_shape():
    out = _jax.eval_shape(lambda: _forward(_fwd_setup_inputs(0)))
    return out.shape, out.dtype

N_MICROBATCH = 1
ADAM_LR = 0.001
ADAM_B1 = 0.9
ADAM_B2 = 0.999
ADAM_EPS = 1e-08
ADAM_WD = 0.01
ADAM_STEP = 10
PER_EXAMPLE_BATCH_AXIS = {'x': 0, 'loss_target': 0}
SHARED_INPUTS = []
_WEIGHT_DTYPES = {'l0_mix_norm': _jnp.float32, 'l0_w_in': _jnp.float32, 'l0_fox_q_norm': _jnp.float32, 'l0_fox_k_norm': _jnp.float32, 'l0_fox_f_bias': _jnp.float32, 'l0_gdn_conv': _jnp.float32, 'l0_gdn_A_log': _jnp.float32, 'l0_gdn_dt_bias': _jnp.float32, 'l0_gdn_o_norm': _jnp.float32, 'l0_w_out': _jnp.float32, 'l0_ffn_norm': _jnp.float32, 'l0_w_ff1': _jnp.float32, 'l0_w_ff2': _jnp.float32, 'l1_mix_norm': _jnp.float32, 'l1_w_in': _jnp.float32, 'l1_hgrn_o_norm': _jnp.float32, 'l1_w_out': _jnp.float32, 'l1_ffn_norm': _jnp.float32, 'l1_w_ff1': _jnp.float32, 'l1_w_ff2': _jnp.float32, 'hgrn_lb_logits': _jnp.float32}
MOMENT_SCALE = {'l0_mix_norm': 8.790846e+00, 'l0_w_in': 6.546598e-01, 'l0_fox_q_norm': 1.671902e+01, 'l0_fox_k_norm': 1.644005e+01, 'l0_fox_f_bias': 9.396052e+01, 'l0_gdn_conv': 2.286706e+00, 'l0_gdn_A_log': 4.389598e+00, 'l0_gdn_dt_bias': 4.163977e+00, 'l0_gdn_o_norm': 5.164749e+01, 'l0_w_out': 3.976934e+00, 'l0_ffn_norm': 9.620757e+01, 'l0_w_ff1': 1.506384e+00, 'l0_w_ff2': 1.731270e+01, 'l1_mix_norm': 1.484719e+01, 'l1_w_in': 3.510581e+00, 'l1_hgrn_o_norm': 9.441195e+01, 'l1_w_out': 4.593999e+00, 'l1_ffn_norm': 9.868948e+01, 'l1_w_ff1': 5.393059e+00, 'l1_w_ff2': 1.385506e+01, 'hgrn_lb_logits': 8.159043e-02}


def _to_microbatches(a, axis):
    t = _jnp.moveaxis(a, axis, 0)
    t = t.reshape((N_MICROBATCH, t.shape[0] // N_MICROBATCH) + t.shape[1:])
    return _jnp.moveaxis(t, 1, axis + 1)


def setup_inputs(seed: int = 0) -> dict:
    inp = _fwd_setup_inputs(seed)
    key = _jax.random.fold_in(_jax.random.key(seed), 7919)
    shape, _ = _output_shape()
    out = dict(inp)
    out["loss_target"] = _jax.random.normal(_jax.random.fold_in(key, 0), shape, _jnp.float32)
    for i, name in enumerate(TWIN_WEIGHTS):
        w = inp[name].astype(_jnp.float32)
        if MOMENT_SCALE is None:
            s = _jnp.sqrt(_jnp.mean(_jnp.square(w)) + 1e-30)
        else:
            s = MOMENT_SCALE[name]
        km, kv = _jax.random.split(_jax.random.fold_in(key, i + 1))
        out[name] = w
        out["m_" + name] = s * _jax.random.normal(km, w.shape, _jnp.float32)
        out["v_" + name] = (s * s) * _jax.random.uniform(kv, w.shape, _jnp.float32, 0.5, 1.5)
    if N_MICROBATCH > 1:
        for name, axis in PER_EXAMPLE_BATCH_AXIS.items():
            out[name] = _to_microbatches(out[name], axis)
    return {'x': out['x'], 'l0_mix_norm': out['l0_mix_norm'], 'l0_w_in': out['l0_w_in'], 'l0_fox_q_norm': out['l0_fox_q_norm'], 'l0_fox_k_norm': out['l0_fox_k_norm'], 'l0_fox_f_bias': out['l0_fox_f_bias'], 'l0_gdn_conv': out['l0_gdn_conv'], 'l0_gdn_A_log': out['l0_gdn_A_log'], 'l0_gdn_dt_bias': out['l0_gdn_dt_bias'], 'l0_gdn_o_norm': out['l0_gdn_o_norm'], 'l0_w_out': out['l0_w_out'], 'l0_ffn_norm': out['l0_ffn_norm'], 'l0_w_ff1': out['l0_w_ff1'], 'l0_w_ff2': out['l0_w_ff2'], 'l1_mix_norm': out['l1_mix_norm'], 'l1_w_in': out['l1_w_in'], 'l1_hgrn_o_norm': out['l1_hgrn_o_norm'], 'l1_w_out': out['l1_w_out'], 'l1_ffn_norm': out['l1_ffn_norm'], 'l1_w_ff1': out['l1_w_ff1'], 'l1_w_ff2': out['l1_w_ff2'], 'hgrn_lb_logits': out['hgrn_lb_logits'], 'loss_target': out['loss_target'], 'm_l0_mix_norm': out['m_l0_mix_norm'], 'm_l0_w_in': out['m_l0_w_in'], 'm_l0_fox_q_norm': out['m_l0_fox_q_norm'], 'm_l0_fox_k_norm': out['m_l0_fox_k_norm'], 'm_l0_fox_f_bias': out['m_l0_fox_f_bias'], 'm_l0_gdn_conv': out['m_l0_gdn_conv'], 'm_l0_gdn_A_log': out['m_l0_gdn_A_log'], 'm_l0_gdn_dt_bias': out['m_l0_gdn_dt_bias'], 'm_l0_gdn_o_norm': out['m_l0_gdn_o_norm'], 'm_l0_w_out': out['m_l0_w_out'], 'm_l0_ffn_norm': out['m_l0_ffn_norm'], 'm_l0_w_ff1': out['m_l0_w_ff1'], 'm_l0_w_ff2': out['m_l0_w_ff2'], 'm_l1_mix_norm': out['m_l1_mix_norm'], 'm_l1_w_in': out['m_l1_w_in'], 'm_l1_hgrn_o_norm': out['m_l1_hgrn_o_norm'], 'm_l1_w_out': out['m_l1_w_out'], 'm_l1_ffn_norm': out['m_l1_ffn_norm'], 'm_l1_w_ff1': out['m_l1_w_ff1'], 'm_l1_w_ff2': out['m_l1_w_ff2'], 'm_hgrn_lb_logits': out['m_hgrn_lb_logits'], 'v_l0_mix_norm': out['v_l0_mix_norm'], 'v_l0_w_in': out['v_l0_w_in'], 'v_l0_fox_q_norm': out['v_l0_fox_q_norm'], 'v_l0_fox_k_norm': out['v_l0_fox_k_norm'], 'v_l0_fox_f_bias': out['v_l0_fox_f_bias'], 'v_l0_gdn_conv': out['v_l0_gdn_conv'], 'v_l0_gdn_A_log': out['v_l0_gdn_A_log'], 'v_l0_gdn_dt_bias': out['v_l0_gdn_dt_bias'], 'v_l0_gdn_o_norm': out['v_l0_gdn_o_norm'], 'v_l0_w_out': out['v_l0_w_out'], 'v_l0_ffn_norm': out['v_l0_ffn_norm'], 'v_l0_w_ff1': out['v_l0_w_ff1'], 'v_l0_w_ff2': out['v_l0_w_ff2'], 'v_l1_mix_norm': out['v_l1_mix_norm'], 'v_l1_w_in': out['v_l1_w_in'], 'v_l1_hgrn_o_norm': out['v_l1_hgrn_o_norm'], 'v_l1_w_out': out['v_l1_w_out'], 'v_l1_ffn_norm': out['v_l1_ffn_norm'], 'v_l1_w_ff1': out['v_l1_w_ff1'], 'v_l1_w_ff2': out['v_l1_w_ff2'], 'v_hgrn_lb_logits': out['v_hgrn_lb_logits']}


def _loss(weights, diff, rest, loss_target):
    with _jax.named_scope("forward"):
        args = {**rest, TWIN_DIFF_INPUT: diff, **{k: w.astype(_WEIGHT_DTYPES[k]) for k, w in weights.items()}}
        y = _forward(args)
    with _jax.named_scope("loss_head"):
        err = _jnp.square(y.astype(_jnp.float32) - loss_target)
        return 0.5 * _jnp.sum(_jnp.mean(err, axis=-1)) if err.ndim else 0.5 * err


def _adamw(w, g, m, v):
    m = ADAM_B1 * m + (1.0 - ADAM_B1) * g
    v = ADAM_B2 * v + (1.0 - ADAM_B2) * _jnp.square(g)
    m_hat = m / (1.0 - ADAM_B1 ** ADAM_STEP)
    v_hat = v / (1.0 - ADAM_B2 ** ADAM_STEP)
    delta = -ADAM_LR * (m_hat / (_jnp.sqrt(v_hat) + ADAM_EPS) + ADAM_WD * w)
    return delta, m, v


def reference(x, l0_mix_norm, l0_w_in, l0_fox_q_norm, l0_fox_k_norm, l0_fox_f_bias, l0_gdn_conv, l0_gdn_A_log, l0_gdn_dt_bias, l0_gdn_o_norm, l0_w_out, l0_ffn_norm, l0_w_ff1, l0_w_ff2, l1_mix_norm, l1_w_in, l1_hgrn_o_norm, l1_w_out, l1_ffn_norm, l1_w_ff1, l1_w_ff2, hgrn_lb_logits, loss_target, m_l0_mix_norm, m_l0_w_in, m_l0_fox_q_norm, m_l0_fox_k_norm, m_l0_fox_f_bias, m_l0_gdn_conv, m_l0_gdn_A_log, m_l0_gdn_dt_bias, m_l0_gdn_o_norm, m_l0_w_out, m_l0_ffn_norm, m_l0_w_ff1, m_l0_w_ff2, m_l1_mix_norm, m_l1_w_in, m_l1_hgrn_o_norm, m_l1_w_out, m_l1_ffn_norm, m_l1_w_ff1, m_l1_w_ff2, m_hgrn_lb_logits, v_l0_mix_norm, v_l0_w_in, v_l0_fox_q_norm, v_l0_fox_k_norm, v_l0_fox_f_bias, v_l0_gdn_conv, v_l0_gdn_A_log, v_l0_gdn_dt_bias, v_l0_gdn_o_norm, v_l0_w_out, v_l0_ffn_norm, v_l0_w_ff1, v_l0_w_ff2, v_l1_mix_norm, v_l1_w_in, v_l1_hgrn_o_norm, v_l1_w_out, v_l1_ffn_norm, v_l1_w_ff1, v_l1_w_ff2, v_hgrn_lb_logits):
    given = dict(x=x, l0_mix_norm=l0_mix_norm, l0_w_in=l0_w_in, l0_fox_q_norm=l0_fox_q_norm, l0_fox_k_norm=l0_fox_k_norm, l0_fox_f_bias=l0_fox_f_bias, l0_gdn_conv=l0_gdn_conv, l0_gdn_A_log=l0_gdn_A_log, l0_gdn_dt_bias=l0_gdn_dt_bias, l0_gdn_o_norm=l0_gdn_o_norm, l0_w_out=l0_w_out, l0_ffn_norm=l0_ffn_norm, l0_w_ff1=l0_w_ff1, l0_w_ff2=l0_w_ff2, l1_mix_norm=l1_mix_norm, l1_w_in=l1_w_in, l1_hgrn_o_norm=l1_hgrn_o_norm, l1_w_out=l1_w_out, l1_ffn_norm=l1_ffn_norm, l1_w_ff1=l1_w_ff1, l1_w_ff2=l1_w_ff2, hgrn_lb_logits=hgrn_lb_logits, loss_target=loss_target, m_l0_mix_norm=m_l0_mix_norm, m_l0_w_in=m_l0_w_in, m_l0_fox_q_norm=m_l0_fox_q_norm, m_l0_fox_k_norm=m_l0_fox_k_norm, m_l0_fox_f_bias=m_l0_fox_f_bias, m_l0_gdn_conv=m_l0_gdn_conv, m_l0_gdn_A_log=m_l0_gdn_A_log, m_l0_gdn_dt_bias=m_l0_gdn_dt_bias, m_l0_gdn_o_norm=m_l0_gdn_o_norm, m_l0_w_out=m_l0_w_out, m_l0_ffn_norm=m_l0_ffn_norm, m_l0_w_ff1=m_l0_w_ff1, m_l0_w_ff2=m_l0_w_ff2, m_l1_mix_norm=m_l1_mix_norm, m_l1_w_in=m_l1_w_in, m_l1_hgrn_o_norm=m_l1_hgrn_o_norm, m_l1_w_out=m_l1_w_out, m_l1_ffn_norm=m_l1_ffn_norm, m_l1_w_ff1=m_l1_w_ff1, m_l1_w_ff2=m_l1_w_ff2, m_hgrn_lb_logits=m_hgrn_lb_logits, v_l0_mix_norm=v_l0_mix_norm, v_l0_w_in=v_l0_w_in, v_l0_fox_q_norm=v_l0_fox_q_norm, v_l0_fox_k_norm=v_l0_fox_k_norm, v_l0_fox_f_bias=v_l0_fox_f_bias, v_l0_gdn_conv=v_l0_gdn_conv, v_l0_gdn_A_log=v_l0_gdn_A_log, v_l0_gdn_dt_bias=v_l0_gdn_dt_bias, v_l0_gdn_o_norm=v_l0_gdn_o_norm, v_l0_w_out=v_l0_w_out, v_l0_ffn_norm=v_l0_ffn_norm, v_l0_w_ff1=v_l0_w_ff1, v_l0_w_ff2=v_l0_w_ff2, v_l1_mix_norm=v_l1_mix_norm, v_l1_w_in=v_l1_w_in, v_l1_hgrn_o_norm=v_l1_hgrn_o_norm, v_l1_w_out=v_l1_w_out, v_l1_ffn_norm=v_l1_ffn_norm, v_l1_w_ff1=v_l1_w_ff1, v_l1_w_ff2=v_l1_w_ff2, v_hgrn_lb_logits=v_hgrn_lb_logits)
    weights = {n: given[n] for n in TWIN_WEIGHTS}
    shared = {n: given[n] for n in SHARED_INPUTS}
    per_example = {n: given[n] for n in ['x']}
    grad_fn = _jax.value_and_grad(_loss, argnums=(0, 1))

    def one_microbatch(ex, loss_target):
        ex = dict(ex)
        diff = ex.pop(TWIN_DIFF_INPUT)
        return grad_fn(weights, diff, {**shared, **ex}, loss_target)

    if N_MICROBATCH == 1:
        loss, (grad_w, grad_x) = one_microbatch(per_example, given["loss_target"])
    else:
        def body(carry, xs):
            loss_sum, grad_sum = carry
            l_k, (gw_k, gx_k) = one_microbatch(xs[0], xs[1])
            with _jax.named_scope("update"):
                return (loss_sum + l_k, _jax.tree.map(_jnp.add, grad_sum, gw_k)), gx_k

        init = (_jnp.zeros((), _jnp.float32), _jax.tree.map(_jnp.zeros_like, weights))
        (loss, grad_w), grad_x = _jax.lax.scan(body, init, (per_example, given["loss_target"]))
    with _jax.named_scope("update"):
        delta_w, new_m, new_v = {}, {}, {}
        for n in TWIN_WEIGHTS:
            delta_w[n], new_m[n], new_v[n] = _adamw(weights[n], grad_w[n], given["m_" + n], given["v_" + n])
    return (loss, grad_x, *[grad_w[n] for n in TWIN_WEIGHTS], *[delta_w[n] for n in TWIN_WEIGHTS],
            *[new_m[n] for n in TWIN_WEIGHTS], *[new_v[n] for n in TWIN_WEIGHTS])
```

```python
import functools

import jax
import jax.numpy as jnp
from jax import lax
from jax.experimental import pallas as pl
from jax.experimental.pallas import tpu as pltpu

f32, bf16 = jnp.float32, jnp.bfloat16
NN = (((1,), (0,)), ((), ()))
NT = (((1,), (1,)), ((), ()))
TN = (((0,), (0,)), ((), ()))
HI = lax.Precision.HIGHEST
MESH = pl.DeviceIdType.MESH
S = jax.ShapeDtypeStruct

EPS = 1e-6
D = 1024
LANES = 128
FOX_H, FOX_D, FOX_W = 8, 64, 512
GDN_H, HD, GDN_W = 4, 128, 512
HG_H = 8
CH = 64
ZW0 = 3840
NDEV = 8
ADAM_LR, ADAM_B1, ADAM_B2, ADAM_EPS, ADAM_WD, ADAM_STEP = 0.001, 0.9, 0.999, 1e-08, 0.01, 10

Z0_FQK, Z0_FV, Z0_GQKV, Z0_GG, Z0_SMALL = 0, 8, 12, 24, 28


def _dot(a, b, dims=NN, prec=None):
    return lax.dot_general(a, b, dims, precision=prec, preferred_element_type=f32)


def _iota2(shape, axis):
    return lax.broadcasted_iota(jnp.int32, shape, axis)


def _tok_specs(rows, consts, tm):
    specs = []
    for (_, w, base) in rows:
        specs.append(pl.BlockSpec((tm, w), functools.partial(lambda j, i, b: (i, b + j), b=base)))
    for (arr, w, base) in consts:
        if w is None:
            specs.append(pl.BlockSpec(arr.shape, lambda j, i: (0, 0)))
        else:
            specs.append(pl.BlockSpec((arr.shape[0], w), functools.partial(lambda j, i, b: (0, b + j), b=base)))
    return specs


def _tok_fwd(name, f, rows, consts, outs, tm, ncb=1, with_j=False):
    n = rows[0][0].shape[0]
    nin = len(rows) + len(consts)

    def body(*refs):
        ins = [r[...] for r in refs[:nin]]
        vals = f(pl.program_id(0), *ins) if with_j else f(*ins)
        for r, v in zip(refs[nin:], vals):
            r[...] = v.astype(r.dtype)

    return pl.pallas_call(
        body, name=name, grid=(ncb, n // tm),
        in_specs=_tok_specs(rows, consts, tm),
        out_specs=[pl.BlockSpec((tm, w), lambda j, i: (i, j)) for (w, _) in outs],
        out_shape=[S((n, w * ncb), dt) for (w, dt) in outs],
        compiler_params=pltpu.CompilerParams(dimension_semantics=("parallel", "parallel")),
    )(*[r[0] for r in rows], *[c[0] for c in consts])


def _tok_bwd(name, f, rows, consts, cots, tm, drow, dconst, ncb=1, with_j=False, addto=None):
    n = rows[0][0].shape[0]
    nr, nc, nct = len(rows), len(consts), len(cots)
    addto = addto or {}
    add_keys = sorted(addto)
    nadd = len(add_keys)

    def body(*refs):
        ins = [r[...] for r in refs[:nr + nc]]
        cot = [r[...] for r in refs[nr + nc:nr + nc + nct]]
        adds = refs[nr + nc + nct:nr + nc + nct + nadd]
        outs = refs[nr + nc + nct + nadd:]
        pos = list(drow) + [nr + k for k in dconst]

        def g(*dargs):
            full = list(ins)
            for p, a in zip(pos, dargs):
                full[p] = a
            return tuple(f(pl.program_id(0), *full) if with_j else f(*full))

        vals, vjp = jax.vjp(g, *[ins[p] for p in pos])
        grads = vjp(tuple(c.astype(v.dtype) for c, v in zip(cot, vals)))
        for k in range(len(drow)):
            gk = grads[k]
            if k in addto:
                gk = gk + adds[add_keys.index(k)][...]
            outs[k][...] = gk.astype(outs[k].dtype)
        first = pl.program_id(1) == 0
        for k in range(len(dconst)):
            ref = outs[len(drow) + k]

            @pl.when(first)
            def _():
                ref[...] = jnp.zeros_like(ref)

            ref[...] += grads[len(drow) + k]

    in_specs = _tok_specs(rows, consts, tm)
    in_specs += [pl.BlockSpec((tm, w), lambda j, i: (i, j)) for (_, w) in cots]
    in_specs += [pl.BlockSpec((tm, rows[drow[k]][1]), lambda j, i: (i, j)) for k in add_keys]
    out_specs = [pl.BlockSpec((tm, rows[k][1]), lambda j, i: (i, j)) for k in drow]
    out_shape = [S((n, rows[k][1] * ncb), f32) for k in drow]
    for k in dconst:
        arr, w, _ = consts[k]
        if w is None:
            out_specs.append(pl.BlockSpec(arr.shape, lambda j, i: (0, 0)))
            out_shape.append(S(arr.shape, f32))
        else:
            out_specs.append(pl.BlockSpec((arr.shape[0], w), lambda j, i: (0, j)))
            out_shape.append(S((arr.shape[0], w * ncb), f32))
    return pl.pallas_call(
        body, name=name, grid=(ncb, n // tm), in_specs=in_specs, out_specs=out_specs, out_shape=out_shape,
        compiler_params=pltpu.CompilerParams(dimension_semantics=("parallel", "arbitrary")),
    )(*[r[0] for r in rows], *[c[0] for c in consts], *[c[0] for c in cots], *[addto[k] for k in add_keys])


def _scan_fwd(name, f, rows, nh, nchunk):
    n = rows[0][0].shape[0]
    groups = (n // (CH * nchunk)) * nh
    nin = len(rows)

    def body(*refs):
        o_ref, hist_ref, st = refs[nin], refs[nin + 1], refs[nin + 2]

        @pl.when(pl.program_id(1) == 0)
        def _():
            st[...] = jnp.zeros_like(st)

        s0 = st[...]
        hist_ref[...] = s0
        o, s1 = f(*[r[...] for r in refs[:nin]], s0)
        o_ref[...] = o
        st[...] = s1

    def rmap(g, c, base):
        return ((g // nh) * nchunk + c, base + g % nh)

    return pl.pallas_call(
        body, name=name, grid=(groups, nchunk),
        in_specs=[pl.BlockSpec((CH, HD), functools.partial(rmap, base=b)) for (_, b) in rows],
        out_specs=[pl.BlockSpec((CH, HD), functools.partial(rmap, base=0)),
                   pl.BlockSpec((HD, HD), lambda g, c: (g * nchunk + c, 0))],
        out_shape=[S((n, nh * HD), f32), S((groups * nchunk * HD, HD), f32)],
        scratch_shapes=[pltpu.VMEM((HD, HD), f32)],
        compiler_params=pltpu.CompilerParams(dimension_semantics=("parallel", "arbitrary")),
    )(*[r[0] for r in rows])


def _scan_bwd(name, f, rows, hist, do, nh, nchunk):
    n = rows[0][0].shape[0]
    groups = (n // (CH * nchunk)) * nh
    nin = len(rows)

    def body(*refs):
        hist_ref, do_ref = refs[nin], refs[nin + 1]
        outs = refs[nin + 2:nin + 2 + nin]
        ds = refs[nin + 2 + nin]

        @pl.when(pl.program_id(1) == 0)
        def _():
            ds[...] = jnp.zeros_like(ds)

        _, vjp = jax.vjp(f, *[r[...] for r in refs[:nin]], hist_ref[...])
        grads = vjp((do_ref[...], ds[...]))
        for r, gk in zip(outs, grads[:nin]):
            r[...] = gk
        ds[...] = grads[nin]

    def rmap(g, c, base):
        return ((g // nh) * nchunk + (nchunk - 1 - c), base + g % nh)

    return pl.pallas_call(
        body, name=name, grid=(groups, nchunk),
        in_specs=[pl.BlockSpec((CH, HD), functools.partial(rmap, base=b)) for (_, b) in rows]
        + [pl.BlockSpec((HD, HD), lambda g, c: (g * nchunk + (nchunk - 1 - c), 0)),
           pl.BlockSpec((CH, HD), functools.partial(rmap, base=0))],
        out_specs=[pl.BlockSpec((CH, HD), functools.partial(rmap, base=0)) for _ in rows],
        out_shape=[S((n, nh * HD), f32) for _ in rows],
        scratch_shapes=[pltpu.VMEM((HD, HD), f32)],
        compiler_params=pltpu.CompilerParams(dimension_semantics=("parallel", "arbitrary")),
    )(*[r[0] for r in rows], hist, do)


def _pick(n, cands):
    for c in cands:
        if n % c == 0:
            return c
    raise ValueError(f"no tile for {n}")


def _mm(name, a, b, dims, out_dtype, a_fn=None, epi=None, extra=()):
    if dims is TN:
        r, m = a.shape
        nn = b.shape[1]
        tm, tn, tk = _pick(m, (512, 256, 128)), _pick(nn, (768, 512, 256, 128)), _pick(r, (512, 256, 128, 64))

        def body(a_ref, b_ref, o_ref, acc):
            k = pl.program_id(2)

            @pl.when(k == 0)
            def _():
                acc[...] = jnp.zeros_like(acc)

            av = a_ref[...]
            if a_fn is not None:
                av = a_fn(av)
            acc[...] += _dot(av.astype(bf16), b_ref[...].astype(bf16), TN)

            @pl.when(k == pl.num_programs(2) - 1)
            def _():
                o_ref[...] = acc[...].astype(o_ref.dtype)

        return pl.pallas_call(
            body, name=name, grid=(m // tm, nn // tn, r // tk),
            in_specs=[pl.BlockSpec((tk, tm), lambda i, j, k: (k, i)), pl.BlockSpec((tk, tn), lambda i, j, k: (k, j))],
            out_specs=pl.BlockSpec((tm, tn), lambda i, j, k: (i, j)),
            out_shape=S((m, nn), out_dtype), scratch_shapes=[pltpu.VMEM((tm, tn), f32)],
            compiler_params=pltpu.CompilerParams(dimension_semantics=("parallel", "parallel", "arbitrary")),
        )(a, b)

    m, kk = a.shape
    nn = b.shape[1] if dims is NN else b.shape[0]
    tm, tn = _pick(m, (512, 256, 128, 64)), _pick(nn, (768, 512, 256, 128))
    nex = len(extra)

    def body(a_ref, b_ref, *rest):
        av = a_ref[...]
        if a_fn is not None:
            av = a_fn(av)
        acc = _dot(av.astype(bf16), b_ref[...].astype(bf16), dims)
        if epi is not None:
            acc = epi(acc, *[r[...] for r in rest[:nex]])
        rest[nex][...] = acc.astype(rest[nex].dtype)

    bspec = pl.BlockSpec((kk, tn), lambda i, j: (0, j)) if dims is NN else pl.BlockSpec((tn, kk), lambda i, j: (j, 0))
    return pl.pallas_call(
        body, name=name, grid=(m // tm, nn // tn),
        in_specs=[pl.BlockSpec((tm, kk), lambda i, j: (i, 0)), bspec]
        + [pl.BlockSpec((tm, tn), lambda i, j: (i, j)) for _ in extra],
        out_specs=pl.BlockSpec((tm, tn), lambda i, j: (i, j)),
        out_shape=S((m, nn), out_dtype),
        compiler_params=pltpu.CompilerParams(dimension_semantics=("parallel", "parallel")),
    )(a, b, *extra)


def _f_norm(x, g):
    return (x * lax.rsqrt(jnp.mean(x * x, axis=-1, keepdims=True) + EPS) * g,)


def _f_foxpre(zqk, gq, gk, pm):
    def nrm(t, g):
        return t * lax.rsqrt(_dot(t * t, pm, prec=HI) + EPS) * g
    return nrm(zqk[:, :FOX_W], gq), nrm(zqk[:, FOX_W:], gk)


def _f_gdngate(zs, eb, ea, alog_b, dt_b):
    beta = jax.nn.sigmoid(_dot(zs, eb, prec=HI))
    la = -jnp.exp(alog_b) * jax.nn.softplus(_dot(zs, ea, prec=HI) + dt_b)
    return beta, la


def _f_conv(j, x, w):
    t = x.shape[0]
    y = x * w[3:4, :]
    for jj in range(3):
        sh = 3 - jj
        xs = jnp.concatenate([jnp.zeros((sh, x.shape[1]), f32), x[:t - sh, :]], axis=0)
        y = y + xs * w[jj:jj + 1, :]
    y = jax.nn.silu(y)
    yn = y * lax.rsqrt(jnp.sum(y * y, axis=-1, keepdims=True) + EPS)
    return (jnp.where(j < 2 * GDN_H, yn, y),)


def _head_rms(o, nh):
    outs = []
    for h in range(nh):
        oh = o[:, HD * h:HD * (h + 1)]
        outs.append(oh * lax.rsqrt(jnp.mean(oh * oh, axis=-1, keepdims=True) + EPS))
    return jnp.concatenate(outs, axis=1)


def _f_post0(fox_o, o, gg, on):
    return (jnp.concatenate([fox_o, _head_rms(o, GDN_H) * on * jax.nn.silu(gg)], axis=1),)


def _f_post1(o, zg, on):
    return (_head_rms(o, HG_H) * on * jax.nn.silu(zg),)


def _f_hpre(zqf, lbl):
    lb = jax.nn.sigmoid(lbl[1:2, :] - lbl[0:1, :])
    fg = lb + (1.0 - lb) * jax.nn.sigmoid(zqf[:, D:])
    return jax.nn.silu(zqf[:, :D]), 1.0 - fg, jnp.log(fg)


def _inv_unit_lower(a):
    n = a.shape[0]
    x = jnp.where(_iota2((n, n), 0) == _iota2((n, n), 1), 1.0, 0.0).astype(f32) - a
    p = a
    for _ in range(max(1, (n - 1).bit_length()) - 1):
        p = _dot(p, p, prec=HI)
        x = x + _dot(x, p, prec=HI)
    return x


def _gdn_intra_head(q, k, v, bb, lab):
    c = q.shape[0]
    r, cc = _iota2((c, c), 0), _iota2((c, c), 1)
    causal, strict = r >= cc, r > cc
    beta, la = bb[:, :1], lab[:, :1]
    gmat = _dot(causal.astype(f32), jnp.broadcast_to(la, (c, c)), prec=HI)
    g = gmat[:, :1]
    decay = jnp.exp(jnp.where(causal, gmat - gmat.T, -jnp.inf))
    kb = k * beta
    amat = jnp.where(strict, _dot(kb, k, NT) * decay, 0.0)
    tinv = _inv_unit_lower(amat)
    u = _dot(tinv, v * beta, prec=HI)
    w = _dot(tinv, kb * jnp.exp(g), prec=HI)
    qs = q * (q.shape[1] ** -0.5)
    qk = jnp.where(causal, _dot(qs, k, NT) * decay, 0.0)
    gl = gmat[c - 1:c, :1]
    return (u, w, jnp.concatenate([qk, jnp.zeros_like(qk)], axis=1), qs * jnp.exp(g), k * jnp.exp(gl - g),
            jnp.broadcast_to(gl, q.shape))


def _f_gdn_intra(q, k, v, bb, lab):
    per_head = [_gdn_intra_head(*[t[:, HD * h:HD * (h + 1)] for t in (q, k, v, bb, lab)]) for h in range(GDN_H)]
    return tuple(jnp.concatenate([ph[i] for ph in per_head], axis=1) for i in range(6))


def _f_gdn_inter(u, w, qkp, qd, kd, glb, st):
    vn = u - _dot(w, st)
    o = _dot(qd, st) + _dot(qkp[:, :CH], vn)
    return o, st * jnp.exp(glb[:1, :1]) + _dot(kd, vn, TN)


def _f_hgrn_chunk(q, k, lf, v, st):
    c = q.shape[0]
    causal = _iota2((c, c), 0) >= _iota2((c, c), 1)
    b = _dot(causal.astype(f32), lf, prec=HI)
    bl, bm = b[c - 1:c, :], b[c // 2 - 1:c // 2, :]
    a = jnp.where(causal, _dot(q * jnp.exp(b - bm), k * jnp.exp(bm - b), NT), 0.0)
    o = _dot(q * jnp.exp(b), st, NT) + _dot(a, v)
    return o, st * jnp.exp(bl) + _dot(v, k * jnp.exp(bl - b), TN)


def _fox_gate_fwd(z0, fbias, t, tc=256):
    n = z0.shape[0]
    nt = t // tc

    def body(zs_ref, b_ref, ccol_ref, crow_ref, carry):
        @pl.when(pl.program_id(1) == 0)
        def _():
            carry[...] = jnp.zeros_like(carry)

        ls = jnp.where(_iota2((tc, LANES), 1) < FOX_H, jax.nn.log_sigmoid(zs_ref[...] + b_ref[...]), 0.0)
        tri = (_iota2((tc, tc), 0) >= _iota2((tc, tc), 1)).astype(f32)
        c = _dot(tri, ls, prec=HI) + carry[...]
        carry[...] = c[tc - 1:tc, :]
        ccol_ref[...] = c
        crow_ref[0] = c.T[:FOX_H, :]

    return pl.pallas_call(
        body, name="fox_gate_fwd", grid=(n // t, nt),
        in_specs=[pl.BlockSpec((tc, LANES), lambda b, i: (b * nt + i, Z0_SMALL)), pl.BlockSpec((1, LANES), lambda b, i: (0, 0))],
        out_specs=[pl.BlockSpec((tc, LANES), lambda b, i: (b * nt + i, 0)), pl.BlockSpec((1, FOX_H, tc), lambda b, i: (b, 0, i))],
        out_shape=[S((n, LANES), f32), S((n // t, FOX_H, t), f32)],
        scratch_shapes=[pltpu.VMEM((1, LANES), f32)],
        compiler_params=pltpu.CompilerParams(dimension_semantics=("parallel", "arbitrary")),
    )(z0, fbias)


def _fox_gate_bwd(z0, fbias, dcq, dck, t, tc=256):
    n = z0.shape[0]
    nt = t // tc

    def body(zs_ref, b_ref, dcq_ref, dck_ref, dz_ref, db_ref, carry):
        first = jnp.logical_and(pl.program_id(0) == 0, pl.program_id(1) == 0)

        @pl.when(pl.program_id(1) == 0)
        def _():
            carry[...] = jnp.zeros_like(carry)

        @pl.when(first)
        def _():
            db_ref[...] = jnp.zeros_like(db_ref)

        dc = dcq_ref[0] + dcq_ref[1] + dcq_ref[2] + dcq_ref[3]
        drow = dck_ref[0, 0] + dck_ref[1, 0] + dck_ref[2, 0] + dck_ref[3, 0]
        eye = (_iota2((FOX_H, LANES), 0) == _iota2((FOX_H, LANES), 1)).astype(f32)
        dc = dc + _dot(drow, eye, TN, prec=HI)
        triu = (_iota2((tc, tc), 0) <= _iota2((tc, tc), 1)).astype(f32)
        dls = _dot(triu, dc, prec=HI) + carry[...]
        carry[...] = dls[0:1, :]
        x = zs_ref[...] + b_ref[...]
        dz = jnp.where(_iota2((tc, LANES), 1) < FOX_H, dls * jax.nn.sigmoid(-x), 0.0)
        dz_ref[...] = dz
        db_ref[...] += jnp.sum(dz, axis=0, keepdims=True)

    def rev(b, i):
        return b * nt + (nt - 1 - i)

    return pl.pallas_call(
        body, name="fox_gate_bwd", grid=(n // t, nt),
        in_specs=[pl.BlockSpec((tc, LANES), lambda b, i: (rev(b, i), Z0_SMALL)), pl.BlockSpec((1, LANES), lambda b, i: (0, 0)),
                  pl.BlockSpec((4, tc, LANES), lambda b, i: (0, rev(b, i), 0)),
                  pl.BlockSpec((4, 1, FOX_H, tc), lambda b, i: (0, b, 0, nt - 1 - i))],
        out_specs=[pl.BlockSpec((tc, LANES), lambda b, i: (rev(b, i), 0)), pl.BlockSpec((1, LANES), lambda b, i: (0, 0))],
        out_shape=[S((n, LANES), f32), S((1, LANES), f32)],
        scratch_shapes=[pltpu.VMEM((1, LANES), f32)],
        compiler_params=pltpu.CompilerParams(dimension_semantics=("arbitrary", "arbitrary")),
    )(z0, fbias, dcq, dck)


def _fox_scores(hh, p, i, tq, t, q, k, ccol, crow):
    lane = _iota2((1, LANES), 1)
    mh = (lane // FOX_D) == hh
    h = 2 * p + hh
    qh = jnp.where(mh, q, 0.0).astype(bf16)
    s = _dot(qh, k, NT) * (FOX_D ** -0.5)
    cq = jnp.sum(jnp.where(lane == h, ccol, 0.0), axis=1, keepdims=True)
    ck = jnp.sum(jnp.where(_iota2((FOX_H, 1), 0) == h, crow, 0.0), axis=0, keepdims=True)
    causal = _iota2((1, t), 1) <= (i * tq + _iota2((tq, 1), 0))
    s = jnp.where(causal, s + cq - ck, -jnp.inf)
    pe = jnp.exp(s - jnp.max(s, axis=1, keepdims=True))
    return mh, qh, pe, jnp.sum(pe, axis=1, keepdims=True)


def _fox_attn_fwd(qn, kn, z0, ccol, crow, t, tq=256):
    n = qn.shape[0]
    nq = t // tq

    def body(q_ref, k_ref, v_ref, ccol_ref, crow_ref, o_ref):
        p, i = pl.program_id(1), pl.program_id(2)
        q, k, v = q_ref[...], k_ref[...].astype(bf16), v_ref[...].astype(bf16)
        acc = jnp.zeros((tq, LANES), f32)
        for hh in range(2):
            mh, _, pe, l = _fox_scores(hh, p, i, tq, t, q, k, ccol_ref[...], crow_ref[0])
            acc = jnp.where(mh, _dot(pe.astype(bf16), v) / l, acc)
        o_ref[...] = acc

    return pl.pallas_call(
        body, name="fox_attn_fwd", grid=(n // t, FOX_H // 2, nq),
        in_specs=[pl.BlockSpec((tq, LANES), lambda b, p, i: (b * nq + i, p)), pl.BlockSpec((t, LANES), lambda b, p, i: (b, p)),
                  pl.BlockSpec((t, LANES), lambda b, p, i: (b, Z0_FV + p)),
                  pl.BlockSpec((tq, LANES), lambda b, p, i: (b * nq + i, 0)), pl.BlockSpec((1, FOX_H, t), lambda b, p, i: (b, 0, 0))],
        out_specs=pl.BlockSpec((tq, LANES), lambda b, p, i: (b * nq + i, p)),
        out_shape=S((n, FOX_W), f32),
        compiler_params=pltpu.CompilerParams(dimension_semantics=("parallel", "parallel", "parallel")),
    )(qn, kn, z0, ccol, crow)


def _fox_attn_bwd(qn, kn, z0, ccol, crow, do, t, tq=256):
    n = qn.shape[0]
    nq = t // tq
    nb = n // t

    def body(q_ref, k_ref, v_ref, ccol_ref, crow_ref, do_ref, dq_ref, dk_ref, dv_ref, dcq_ref, dck_ref):
        p, i = pl.program_id(1), pl.program_id(2)

        @pl.when(i == 0)
        def _():
            dk_ref[...] = jnp.zeros_like(dk_ref)
            dv_ref[...] = jnp.zeros_like(dv_ref)
            dck_ref[...] = jnp.zeros_like(dck_ref)

        q, kf, v, dout = q_ref[...], k_ref[...], v_ref[...].astype(bf16), do_ref[...]
        k = kf.astype(bf16)
        lane = _iota2((1, LANES), 1)
        sub = _iota2((FOX_H, 1), 0)
        scale = FOX_D ** -0.5
        dq = jnp.zeros((tq, LANES), f32)
        dcq = jnp.zeros((tq, LANES), f32)
        for hh in range(2):
            mh, qh, pe, l = _fox_scores(hh, p, i, tq, t, q, k, ccol_ref[...], crow_ref[0])
            pr = pe / l
            doh = jnp.where(mh, dout, 0.0).astype(bf16)
            dp = _dot(doh, v, NT)
            ds = pr * (dp - jnp.sum(pr * dp, axis=1, keepdims=True))
            dsb = ds.astype(bf16)
            dq = dq + _dot(dsb, jnp.where(mh, kf, 0.0).astype(bf16)) * scale
            dk_ref[...] += _dot(dsb, qh, TN) * scale
            dv_ref[...] += _dot(pr.astype(bf16), doh, TN)
            h = 2 * p + hh
            dcq = dcq + jnp.where(lane == h, jnp.sum(ds, axis=1, keepdims=True), 0.0)
            dck_ref[0, 0] += jnp.where(sub == h, -jnp.sum(ds, axis=0, keepdims=True), 0.0)
        dq_ref[...] = dq
        dcq_ref[0] = dcq

    return pl.pallas_call(
        body, name="fox_attn_bwd", grid=(nb, FOX_H // 2, nq),
        in_specs=[pl.BlockSpec((tq, LANES), lambda b, p, i: (b * nq + i, p)), pl.BlockSpec((t, LANES), lambda b, p, i: (b, p)),
                  pl.BlockSpec((t, LANES), lambda b, p, i: (b, Z0_FV + p)),
                  pl.BlockSpec((tq, LANES), lambda b, p, i: (b * nq + i, 0)), pl.BlockSpec((1, FOX_H, t), lambda b, p, i: (b, 0, 0)),
                  pl.BlockSpec((tq, LANES), lambda b, p, i: (b * nq + i, p))],
        out_specs=[pl.BlockSpec((tq, LANES), lambda b, p, i: (b * nq + i, p)), pl.BlockSpec((t, LANES), lambda b, p, i: (b, p)),
                   pl.BlockSpec((t, LANES), lambda b, p, i: (b, p)),
                   pl.BlockSpec((1, tq, LANES), lambda b, p, i: (p, b * nq + i, 0)),
                   pl.BlockSpec((1, 1, FOX_H, t), lambda b, p, i: (p, b, 0, 0))],
        out_shape=[S((n, FOX_W), f32), S((n, FOX_W), f32), S((n, FOX_W), f32), S((4, n, LANES), f32), S((4, nb, FOX_H, t), f32)],
        compiler_params=pltpu.CompilerParams(dimension_semantics=("parallel", "parallel", "arbitrary")),
    )(qn, kn, z0, ccol, crow, do)


def _loss_head(y, tgt, tm=256):
    n = y.shape[0]

    def body(y_ref, t_ref, dy_ref, l_ref):
        @pl.when(pl.program_id(0) == 0)
        def _():
            l_ref[...] = jnp.zeros_like(l_ref)

        e = y_ref[...] - t_ref[...]
        dy_ref[...] = e * (1.0 / D)
        l_ref[...] += jnp.sum(e * e, axis=0, keepdims=True)

    return pl.pallas_call(
        body, name="loss_head", grid=(n // tm,),
        in_specs=[pl.BlockSpec((tm, D), lambda i: (i, 0))] * 2,
        out_specs=[pl.BlockSpec((tm, D), lambda i: (i, 0)), pl.BlockSpec((1, D), lambda i: (0, 0))],
        out_shape=[S((n, D), f32), S((1, D), f32)],
        compiler_params=pltpu.CompilerParams(dimension_semantics=("arbitrary",)),
    )(y, tgt)


def _adamw_math(w, g, m, v):
    m = ADAM_B1 * m + (1.0 - ADAM_B1) * g
    v = ADAM_B2 * v + (1.0 - ADAM_B2) * (g * g)
    m_hat = m / (1.0 - ADAM_B1 ** ADAM_STEP)
    v_hat = v / (1.0 - ADAM_B2 ** ADAM_STEP)
    return -ADAM_LR * (m_hat / (jnp.sqrt(v_hat) + ADAM_EPS) + ADAM_WD * w), m, v


def _adamw_big(idx, gmine, recv1, recv2, w, m, v, tr=256):
    r, c = w.shape

    def body(idx_ref, gm_ref, r1_ref, r2_ref, w_ref, m_ref, v_ref, g_ref, d_ref, nm_ref, nv_ref):
        g = gm_ref[0] + r1_ref[0].astype(f32)
        for k in range(3):
            g = g + r2_ref[k].astype(f32)
        d, nm, nv = _adamw_math(w_ref[...], g, m_ref[...], v_ref[...])
        g_ref[...] = g
        d_ref[...] = d
        nm_ref[...] = nm
        nv_ref[...] = nv

    row = pl.BlockSpec((tr, c), lambda i, s: (i, 0))
    return pl.pallas_call(
        body, name="adamw_big",
        grid_spec=pltpu.PrefetchScalarGridSpec(
            num_scalar_prefetch=1, grid=(r // tr,),
            in_specs=[pl.BlockSpec((1, tr, c), lambda i, s: (s[0], i, 0)), pl.BlockSpec((1, tr, c), lambda i, s: (s[1], i, 0)),
                      pl.BlockSpec((3, tr, c), lambda i, s: (0, i, 0)), row, row, row],
            out_specs=[row, row, row, row]),
        out_shape=[S((r, c), f32)] * 4,
        compiler_params=pltpu.CompilerParams(dimension_semantics=("parallel",)),
    )(idx, gmine, recv1, recv2, w, m, v)


def _pair_sum(idx, gmine, recv1, tr=256):
    _, _, r, c = gmine.shape

    def body(idx_ref, gm_ref, r1_ref, o_ref):
        o_ref[0] = (gm_ref[0, 0] + r1_ref[0].astype(f32)).astype(bf16)

    return pl.pallas_call(
        body, name="rs_pair_sum",
        grid_spec=pltpu.PrefetchScalarGridSpec(
            num_scalar_prefetch=1, grid=(4, r // tr),
            in_specs=[pl.BlockSpec((1, 1, tr, c), lambda ch, i, s: (ch, s[0], i, 0)), pl.BlockSpec((1, tr, c), lambda ch, i, s: (ch, i, 0))],
            out_specs=pl.BlockSpec((1, tr, c), lambda ch, i, s: (ch, i, 0))),
        out_shape=S((4, r, c), bf16),
        compiler_params=pltpu.CompilerParams(dimension_semantics=("parallel", "parallel")),
    )(idx, gmine, recv1)


def _adamw_small(parts, w, m, v, own_mask, own_g):
    def body(p_ref, w_ref, m_ref, v_ref, mask_ref, og_ref, g_ref, d_ref, nm_ref, nv_ref):
        g = p_ref[0]
        for k in range(1, NDEV):
            g = g + p_ref[k]
        g_ref[...] = g
        ge = jnp.where(mask_ref[...] > 0.5, og_ref[...], g)
        d, nm, nv = _adamw_math(w_ref[...], ge, m_ref[...], v_ref[...])
        d_ref[...] = d
        nm_ref[...] = nm
        nv_ref[...] = nv

    return pl.pallas_call(body, name="adamw_small", out_shape=[S(w.shape, f32)] * 4)(parts, w, m, v, own_mask, own_g)


def _sum_parts(parts):
    def body(p_ref, g_ref):
        g = p_ref[0]
        for k in range(1, NDEV):
            g = g + p_ref[k]
        g_ref[...] = g

    return pl.pallas_call(body, name="sum_parts", out_shape=S(parts.shape[1:], f32))(parts)


def _me():
    return lax.axis_index("x"), lax.axis_index("y"), lax.axis_index("c")


def _allgather(name, x):
    def body(x_ref, out_ref, send_sems, recv_sems, local_sem):
        mx, my, mc = _me()
        me, sib = (mx, my, mc), (mx, my, 1 - mc)
        chips = [(1 - mx, my), (mx, 1 - my), (1 - mx, 1 - my)]

        def slab(px, py, pc):
            return out_ref.at[4 * px + 2 * py + pc]

        def copy(k, block, to, src=None):
            return pltpu.make_async_remote_copy(
                src_ref=slab(*block) if src is None else src, dst_ref=slab(*block),
                send_sem=send_sems.at[k], recv_sem=recv_sems.at[k], device_id=to, device_id_type=MESH)

        mine = pltpu.make_async_copy(x_ref, slab(*me), local_sem)
        mine.start()
        first = [copy(0, me, sib, src=x_ref)] + [copy(1 + j, me, (*chip, mc), src=x_ref) for j, chip in enumerate(chips)]
        for cp in first:
            cp.start()
        passed = [copy(4 + j, (*chip, mc), sib) for j, chip in enumerate(chips)]
        for j, chip in enumerate(chips):
            copy(1 + j, (*chip, mc), me).wait_recv()
            passed[j].start()
        copy(0, sib, me).wait_recv()
        for j, chip in enumerate(chips):
            copy(4 + j, (*chip, 1 - mc), me).wait_recv()
        for cp in first + passed:
            cp.wait_send()
        mine.wait()

    return pl.pallas_call(
        body, name=name, out_shape=S((NDEV,) + x.shape, x.dtype),
        in_specs=[pl.BlockSpec(memory_space=pl.ANY)], out_specs=pl.BlockSpec(memory_space=pl.ANY),
        scratch_shapes=[pltpu.SemaphoreType.DMA((7,)), pltpu.SemaphoreType.DMA((7,)), pltpu.SemaphoreType.DMA(())],
    )(x)


def _rs_sibling(g):
    def body(g_ref, out_ref, send_sems, recv_sems):
        mx, my, mc = _me()
        sib = (mx, my, 1 - mc)
        cps = [pltpu.make_async_remote_copy(
            src_ref=g_ref.at[ch, 1 - mc], dst_ref=out_ref.at[ch], send_sem=send_sems.at[ch], recv_sem=recv_sems.at[ch],
            device_id=sib, device_id_type=MESH) for ch in range(4)]
        for cp in cps:
            cp.start()
        for cp in cps:
            cp.wait_recv()
        for cp in cps:
            cp.wait_send()

    return pl.pallas_call(
        body, name="rs_sibling", out_shape=S((4,) + g.shape[2:], g.dtype),
        in_specs=[pl.BlockSpec(memory_space=pl.ANY)], out_specs=pl.BlockSpec(memory_space=pl.ANY),
        scratch_shapes=[pltpu.SemaphoreType.DMA((4,)), pltpu.SemaphoreType.DMA((4,))],
    )(g)


def _rs_chips(p):
    def body(p_ref, out_ref, send_sems, recv_sems):
        mx, my, mc = _me()
        chips = [(1 - mx, my), (mx, 1 - my), (1 - mx, 1 - my)]
        cps = [pltpu.make_async_remote_copy(
            src_ref=p_ref.at[2 * cx + cy], dst_ref=out_ref.at[k], send_sem=send_sems.at[k], recv_sem=recv_sems.at[k],
            device_id=(cx, cy, mc), device_id_type=MESH) for k, (cx, cy) in enumerate(chips)]
        for cp in cps:
            cp.start()
        for cp in cps:
            cp.wait_recv()
        for cp in cps:
            cp.wait_send()

    return pl.pallas_call(
        body, name="rs_chips", out_shape=S((3,) + p.shape[1:], p.dtype),
        in_specs=[pl.BlockSpec(memory_space=pl.ANY)], out_specs=pl.BlockSpec(memory_space=pl.ANY),
        scratch_shapes=[pltpu.SemaphoreType.DMA((3,)), pltpu.SemaphoreType.DMA((3,))],
    )(p)


_COLW = 512
_BIG = (("l0_w_in", "col", 512), ("l0_w_out", "row", 128), ("l0_w_ff1", "col", 512), ("l0_w_ff2", "row", 512),
        ("l1_w_in", "col", 512), ("l1_w_out", "row", 128), ("l1_w_ff1", "col", 512), ("l1_w_ff2", "row", 512))
_BIG_ROWS = sum(r for _, _, r in _BIG)


def _pack_shards(shards, dtype):
    parts = []
    for name, kind, _ in _BIG:
        a = shards[name].astype(dtype)
        if kind == "col":
            a = jnp.pad(a, ((0, 0), (0, _COLW - a.shape[1]))).reshape(-1, D)
        parts.append(a)
    return jnp.concatenate(parts, axis=0)


def _unpack_shard(slab, name, ncols):
    off = 0
    for nm, kind, rows in _BIG:
        if nm == name:
            a = slab[off:off + rows]
            return a.reshape(D, _COLW)[:, :ncols] if kind == "col" else a
        off += rows
    raise KeyError(name)


def _full_from_slabs(slabs, name, ncols):
    off = 0
    for nm, kind, rows in _BIG:
        if nm == name:
            a = slabs[:, off:off + rows]
            if kind == "row":
                return a.reshape(NDEV * rows, D)
            return a.reshape(NDEV, D, _COLW)[:, :, :ncols].transpose(1, 0, 2).reshape(D, NDEV * ncols)
        off += rows
    raise KeyError(name)


def _slabs_from_full(grads, ncols):
    parts = []
    for name, kind, rows in _BIG:
        g = grads[name]
        if kind == "row":
            parts.append(g.reshape(NDEV, rows, D))
        else:
            nc = ncols[name]
            g = g.reshape(D, NDEV, nc).transpose(1, 0, 2)
            g = jnp.pad(g, ((0, 0), (0, 0), (0, _COLW - nc)))
            parts.append(g.reshape(NDEV, rows, D))
    return jnp.concatenate(parts, axis=1)


def _regroup_w_in0(w):
    main = jnp.concatenate([w[:, 0:1536], w[:, 1544:3080], w[:, 3088:3600]], axis=1)
    small = jnp.concatenate([w[:, 1536:1544], w[:, 3080:3088]], axis=1)
    return jnp.concatenate([main, small, jnp.zeros((D, ZW0 - 3584 - 16), w.dtype)], axis=1)


def _ungroup_w_in0(g):
    return jnp.concatenate([g[:, 0:1536], g[:, 3584:3592], g[:, 1536:3072], g[:, 3592:3600], g[:, 3072:3584]], axis=1)


def _mlp_fwd(tag, x, gain, w1, w2):
    (h,) = _tok_fwd(f"{tag}_ffn_norm", _f_norm, [(x, D, 0)], [(gain, None, 0)], [(D, bf16)], 256)
    a = _mm(f"{tag}_ff1", h, w1, NN, bf16, epi=lambda acc: jnp.maximum(acc, 0.0))
    sq = lambda t: jnp.square(t.astype(f32))
    y = _mm(f"{tag}_ff2", a, w2, NN, f32, a_fn=sq, epi=lambda acc, r: acc + r, extra=(x,))
    return y, (h, a)


def _mlp_bwd(tag, x, gain, w1, w2, saved, dy):
    h, a = saved
    sq = lambda t: jnp.square(t.astype(f32))
    da = _mm(f"{tag}_ff2_dx", dy, w2, NT, bf16, epi=lambda acc, av: acc * 2.0 * av.astype(f32), extra=(a,))
    dw2 = _mm(f"{tag}_ff2_dw", a, dy, TN, f32, a_fn=sq)
    dh = _mm(f"{tag}_ff1_dx", da, w1, NT, f32)
    dw1 = _mm(f"{tag}_ff1_dw", h, da, TN, f32)
    dx, dgain = _tok_bwd(f"{tag}_ffn_norm_bwd", _f_norm, [(x, D, 0)], [(gain, None, 0)], [(dh, D)], 256, [0], [0], addto={0: dy})
    return dx, dgain, dw1, dw2


def _row(v):
    return v.reshape(1, -1).astype(f32)


def _local_step(x, tgt, p, t):
    n = x.shape[0]
    nchunk = t // CH
    g = {}

    li = jnp.arange(FOX_W)
    pm = jnp.where((li[:, None] // FOX_D) == (li[None, :] // FOX_D), 1.0 / FOX_D, 0.0).astype(f32)
    lane_head = jnp.arange(GDN_W) // HD
    sel = lambda first: (jnp.arange(LANES)[:, None] == (first + lane_head)[None, :]).astype(f32)
    e_beta, e_alpha = sel(FOX_H), sel(FOX_H + GDN_H)
    alog_b, dt_b = _row(jnp.repeat(p["l0_gdn_A_log"], HD)), _row(jnp.repeat(p["l0_gdn_dt_bias"], HD))
    gq_t, gk_t = _row(jnp.tile(p["l0_fox_q_norm"], FOX_H)), _row(jnp.tile(p["l0_fox_k_norm"], FOX_H))
    on0_t, on1_t = _row(jnp.tile(p["l0_gdn_o_norm"], GDN_H)), _row(jnp.tile(p["l1_hgrn_o_norm"], HG_H))
    fbias = jnp.pad(_row(p["l0_fox_f_bias"]), ((0, 0), (0, LANES - FOX_H)))
    g0m, g0f, g1m, g1f = (_row(p[k]) for k in ("l0_mix_norm", "l0_ffn_norm", "l1_mix_norm", "l1_ffn_norm"))
    wconv = p["l0_gdn_conv"].astype(f32)
    lbl = p["hgrn_lb_logits"].astype(f32)

    (h0,) = _tok_fwd("l0_mix_norm", _f_norm, [(x, D, 0)], [(g0m, None, 0)], [(D, bf16)], 256)
    z0 = _mm("l0_in", h0, p["l0_w_in"], NN, f32)
    qk_rows = [(z0, 2 * FOX_W, 0)]
    qk_consts = [(gq_t, None, 0), (gk_t, None, 0), (pm, None, 0)]
    qn, kn = _tok_fwd("fox_pre", _f_foxpre, qk_rows, qk_consts, [(FOX_W, f32)] * 2, 256)
    ccol, crow = _fox_gate_fwd(z0, fbias, t)
    fox_o = _fox_attn_fwd(qn, kn, z0, ccol, crow, t)
    conv_rows, conv_consts = [(z0, LANES, Z0_GQKV)], [(wconv, LANES, 0)]
    (qkv,) = _tok_fwd("gdn_conv", _f_conv, conv_rows, conv_consts, [(LANES, f32)], t, ncb=12, with_j=True)
    gate_rows = [(z0, LANES, Z0_SMALL)]
    gate_consts = [(e_beta, None, 0), (e_alpha, None, 0), (alog_b, None, 0), (dt_b, None, 0)]
    beta_b, la_b = _tok_fwd("gdn_gate", _f_gdngate, gate_rows, gate_consts, [(GDN_W, f32)] * 2, 256)
    intra_rows = [(qkv, GDN_W, 0), (qkv, GDN_W, 1), (qkv, GDN_W, 2), (beta_b, GDN_W, 0), (la_b, GDN_W, 0)]
    intra = _tok_fwd("gdn_intra", _f_gdn_intra, intra_rows, [], [(GDN_W, f32)] * 6, CH)
    inter_rows = [(a, 0) for a in intra]
    gdn_o, gdn_hist = _scan_fwd("gdn_scan", _f_gdn_inter, inter_rows, GDN_H, nchunk)
    post0_rows, post0_consts = [(fox_o, FOX_W, 0), (gdn_o, GDN_W, 0), (z0, GDN_W, Z0_GG // 4)], [(on0_t, None, 0)]
    (cat0,) = _tok_fwd("l0_post", _f_post0, post0_rows, post0_consts, [(D, bf16)], 256)
    x1 = _mm("l0_out", cat0, p["l0_w_out"], NN, f32, epi=lambda acc, r: acc + r, extra=(x,))
    x2, mlp0 = _mlp_fwd("l0", x1, g0f, p["l0_w_ff1"], p["l0_w_ff2"])

    (h1,) = _tok_fwd("l1_mix_norm", _f_norm, [(x2, D, 0)], [(g1m, None, 0)], [(D, bf16)], 256)
    z1 = _mm("l1_in", h1, p["l1_w_in"], NN, f32)
    hpre_rows, hpre_consts = [(z1, 2 * D, 0)], [(lbl, None, 0)]
    hq, hk, hlf = _tok_fwd("hgrn_pre", _f_hpre, hpre_rows, hpre_consts, [(D, f32)] * 3, 256)
    hg_rows = [(hq, 0), (hk, 0), (hlf, 0), (z1, 2 * D // HD)]
    hg_o, hg_hist = _scan_fwd("hgrn_scan", _f_hgrn_chunk, hg_rows, HG_H, nchunk)
    post1_rows, post1_consts = [(hg_o, D, 0), (z1, D, 3)], [(on1_t, None, 0)]
    (cat1,) = _tok_fwd("l1_post", _f_post1, post1_rows, post1_consts, [(D, bf16)], 256)
    x3 = _mm("l1_out", cat1, p["l1_w_out"], NN, f32, epi=lambda acc, r: acc + r, extra=(x2,))
    y, mlp1 = _mlp_fwd("l1", x3, g1f, p["l1_w_ff1"], p["l1_w_ff2"])

    dy, loss_row = _loss_head(y, tgt)

    dx3, g["l1_ffn_norm"], g["l1_w_ff1"], g["l1_w_ff2"] = _mlp_bwd("l1", x3, g1f, p["l1_w_ff1"], p["l1_w_ff2"], mlp1, dy)
    dcat1 = _mm("l1_out_dx", dx3, p["l1_w_out"], NT, f32)
    g["l1_w_out"] = _mm("l1_out_dw", cat1, dx3, TN, f32)
    dhg_o, dzg, don1 = _tok_bwd("l1_post_bwd", _f_post1, post1_rows, post1_consts, [(dcat1, D)], 256, [0, 1], [0])
    dhq, dhk, dhlf, dzi = _scan_bwd("hgrn_scan_bwd", _f_hgrn_chunk, hg_rows, hg_hist, dhg_o, HG_H, nchunk)
    dzqf, dlbl = _tok_bwd("hgrn_pre_bwd", _f_hpre, hpre_rows, hpre_consts, [(dhq, D), (dhk, D), (dhlf, D)], 256, [0], [0])
    dz1 = jnp.concatenate([dzqf, dzi, dzg], axis=1)
    dh1 = _mm("l1_in_dx", dz1, p["l1_w_in"], NT, f32)
    g["l1_w_in"] = _mm("l1_in_dw", h1, dz1, TN, f32)
    dx2, g["l1_mix_norm"] = _tok_bwd("l1_mix_norm_bwd", _f_norm, [(x2, D, 0)], [(g1m, None, 0)], [(dh1, D)], 256, [0], [0], addto={0: dx3})
    g["l1_hgrn_o_norm"] = don1.reshape(HG_H, HD).sum(0)
    g["hgrn_lb_logits"] = dlbl

    dx1, g["l0_ffn_norm"], g["l0_w_ff1"], g["l0_w_ff2"] = _mlp_bwd("l0", x1, g0f, p["l0_w_ff1"], p["l0_w_ff2"], mlp0, dx2)
    dcat0 = _mm("l0_out_dx", dx1, p["l0_w_out"], NT, f32)
    g["l0_w_out"] = _mm("l0_out_dw", cat0, dx1, TN, f32)
    dfox_o, dgdn_o, dgg, don0 = _tok_bwd("l0_post_bwd", _f_post0, post0_rows, post0_consts, [(dcat0, D)], 256, [0, 1, 2], [0])
    dinter = _scan_bwd("gdn_scan_bwd", _f_gdn_inter, inter_rows, gdn_hist, dgdn_o, GDN_H, nchunk)
    dq_g, dk_g, dv_g, dbeta_b, dla_b = _tok_bwd("gdn_intra_bwd", _f_gdn_intra, intra_rows, [], [(a, GDN_W) for a in dinter], CH, [0, 1, 2, 3, 4], [])
    dqkv = jnp.concatenate([dq_g, dk_g, dv_g], axis=1)
    dzs_g, dalog_b, ddt_b = _tok_bwd("gdn_gate_bwd", _f_gdngate, gate_rows, gate_consts, [(dbeta_b, GDN_W), (dla_b, GDN_W)], 256, [0], [2, 3])
    dgqkv, dwconv = _tok_bwd("gdn_conv_bwd", _f_conv, conv_rows, conv_consts, [(dqkv, LANES)], t, [0], [0], ncb=12, with_j=True)
    dqn, dkn, dfv, dcq, dck = _fox_attn_bwd(qn, kn, z0, ccol, crow, dfox_o, t)
    dzs_f, dfb = _fox_gate_bwd(z0, fbias, dcq, dck, t)
    dzqk, dgq_t, dgk_t = _tok_bwd("fox_pre_bwd", _f_foxpre, qk_rows, qk_consts, [(dqn, FOX_W), (dkn, FOX_W)], 256, [0], [0, 1])
    dz0 = jnp.concatenate([dzqk, dfv, dgqkv, dgg, dzs_g + dzs_f, jnp.zeros((n, ZW0 - 3712), f32)], axis=1)
    dh0 = _mm("l0_in_dx", dz0, p["l0_w_in"], NT, f32)
    g["l0_w_in"] = _mm("l0_in_dw", h0, dz0, TN, f32)
    dx, g["l0_mix_norm"] = _tok_bwd("l0_mix_norm_bwd", _f_norm, [(x, D, 0)], [(g0m, None, 0)], [(dh0, D)], 256, [0], [0], addto={0: dx1})
    g["l0_fox_q_norm"] = dgq_t.reshape(FOX_H, FOX_D).sum(0)
    g["l0_fox_k_norm"] = dgk_t.reshape(FOX_H, FOX_D).sum(0)
    g["l0_fox_f_bias"] = dfb[0, :FOX_H]
    g["l0_gdn_conv"] = dwconv
    g["l0_gdn_A_log"] = dalog_b.reshape(GDN_H, HD).sum(1)
    g["l0_gdn_dt_bias"] = ddt_b.reshape(GDN_H, HD).sum(1)
    g["l0_gdn_o_norm"] = don0.reshape(GDN_H, HD).sum(0)
    return loss_row, dx, g


_NAMES = ("l0_mix_norm", "l0_w_in", "l0_fox_q_norm", "l0_fox_k_norm", "l0_fox_f_bias", "l0_gdn_conv", "l0_gdn_A_log",
          "l0_gdn_dt_bias", "l0_gdn_o_norm", "l0_w_out", "l0_ffn_norm", "l0_w_ff1", "l0_w_ff2", "l1_mix_norm", "l1_w_in",
          "l1_hgrn_o_norm", "l1_w_out", "l1_ffn_norm", "l1_w_ff1", "l1_w_ff2", "hgrn_lb_logits")
_BIG_NAMES = tuple(nm for nm, _, _ in _BIG)
_SMALL_NAMES = tuple(nm for nm in _NAMES if nm not in _BIG_NAMES)
_SMALL_ROWS = 16


def _pack_small(vals, conv_key="l0_gdn_conv"):
    flat = jnp.concatenate([vals[nm].reshape(-1).astype(f32) for nm in _SMALL_NAMES])
    return jnp.pad(flat, (0, _SMALL_ROWS * D - flat.shape[0])).reshape(_SMALL_ROWS, D)


def _unpack_small(packed, shapes):
    flat = packed.reshape(-1)
    out, off = {}, 0
    for nm in _SMALL_NAMES:
        size = 1
        for s in shapes[nm]:
            size *= s
        out[nm] = flat[off:off + size].reshape(shapes[nm])
        off += size
    return out, off


def kernel(x, l0_mix_norm, l0_w_in, l0_fox_q_norm, l0_fox_k_norm, l0_fox_f_bias, l0_gdn_conv, l0_gdn_A_log, l0_gdn_dt_bias, l0_gdn_o_norm, l0_w_out, l0_ffn_norm, l0_w_ff1, l0_w_ff2, l1_mix_norm, l1_w_in, l1_hgrn_o_norm, l1_w_out, l1_ffn_norm, l1_w_ff1, l1_w_ff2, hgrn_lb_logits, loss_target, m_l0_mix_norm, m_l0_w_in, m_l0_fox_q_norm, m_l0_fox_k_norm, m_l0_fox_f_bias, m_l0_gdn_conv, m_l0_gdn_A_log, m_l0_gdn_dt_bias, m_l0_gdn_o_norm, m_l0_w_out, m_l0_ffn_norm, m_l0_w_ff1, m_l0_w_ff2, m_l1_mix_norm, m_l1_w_in, m_l1_hgrn_o_norm, m_l1_w_out, m_l1_ffn_norm, m_l1_w_ff1, m_l1_w_ff2, m_hgrn_lb_logits, v_l0_mix_norm, v_l0_w_in, v_l0_fox_q_norm, v_l0_fox_k_norm, v_l0_fox_f_bias, v_l0_gdn_conv, v_l0_gdn_A_log, v_l0_gdn_dt_bias, v_l0_gdn_o_norm, v_l0_w_out, v_l0_ffn_norm, v_l0_w_ff1, v_l0_w_ff2, v_l1_mix_norm, v_l1_w_in, v_l1_hgrn_o_norm, v_l1_w_out, v_l1_ffn_norm, v_l1_w_ff1, v_l1_w_ff2, v_hgrn_lb_logits):
    args = dict(zip(_NAMES, (l0_mix_norm, l0_w_in, l0_fox_q_norm, l0_fox_k_norm, l0_fox_f_bias, l0_gdn_conv, l0_gdn_A_log, l0_gdn_dt_bias, l0_gdn_o_norm, l0_w_out, l0_ffn_norm, l0_w_ff1, l0_w_ff2, l1_mix_norm, l1_w_in, l1_hgrn_o_norm, l1_w_out, l1_ffn_norm, l1_w_ff1, l1_w_ff2, hgrn_lb_logits)))
    mom = dict(zip(_NAMES, (m_l0_mix_norm, m_l0_w_in, m_l0_fox_q_norm, m_l0_fox_k_norm, m_l0_fox_f_bias, m_l0_gdn_conv, m_l0_gdn_A_log, m_l0_gdn_dt_bias, m_l0_gdn_o_norm, m_l0_w_out, m_l0_ffn_norm, m_l0_w_ff1, m_l0_w_ff2, m_l1_mix_norm, m_l1_w_in, m_l1_hgrn_o_norm, m_l1_w_out, m_l1_ffn_norm, m_l1_w_ff1, m_l1_w_ff2, m_hgrn_lb_logits)))
    var = dict(zip(_NAMES, (v_l0_mix_norm, v_l0_w_in, v_l0_fox_q_norm, v_l0_fox_k_norm, v_l0_fox_f_bias, v_l0_gdn_conv, v_l0_gdn_A_log, v_l0_gdn_dt_bias, v_l0_gdn_o_norm, v_l0_w_out, v_l0_ffn_norm, v_l0_w_ff1, v_l0_w_ff2, v_l1_mix_norm, v_l1_w_in, v_l1_hgrn_o_norm, v_l1_w_out, v_l1_ffn_norm, v_l1_w_ff1, v_l1_w_ff2, v_hgrn_lb_logits)))
    nb, t, _ = x.shape
    mx, my, mc = _me()
    dev = 4 * mx + 2 * my + mc
    chip = 2 * mx + my
    ncols = {nm: args[nm].shape[1] for nm, kind, _ in _BIG if kind == "col"}
    conv_cols = l0_gdn_conv.shape[1]

    slabs = _allgather("ag_weights", _pack_shards(args, bf16))
    conv_pad = jnp.pad(l0_gdn_conv, ((0, 4), (0, LANES * 2 - conv_cols)))
    conv_all = _allgather("ag_conv", conv_pad)
    p = {nm: args[nm] for nm in _SMALL_NAMES}
    p["l0_gdn_conv"] = conv_all[:, :4, :conv_cols].transpose(1, 0, 2).reshape(4, NDEV * conv_cols)
    for nm in _BIG_NAMES:
        p[nm] = _full_from_slabs(slabs, nm, ncols.get(nm, 0))
    p["l0_w_in"] = _regroup_w_in0(p["l0_w_in"])

    loss_row, dx, g = _local_step(x.reshape(nb * t, D), loss_target.reshape(nb * t, D), p, t)
    g["l0_w_in"] = _ungroup_w_in0(g["l0_w_in"])

    gslabs = _slabs_from_full(g, ncols).reshape(4, 2, _BIG_ROWS, D)
    recv1 = _rs_sibling(gslabs.astype(bf16))
    core_idx = jnp.reshape(mc, (1,)).astype(jnp.int32)
    pair = _pair_sum(core_idx, gslabs, recv1)
    recv2 = _rs_chips(pair)
    own_idx = jnp.stack([dev, chip]).astype(jnp.int32)
    gb, db, mb, vb = _adamw_big(own_idx, gslabs.reshape(NDEV, _BIG_ROWS, D), recv1, recv2,
                                _pack_shards(args, f32), _pack_shards(mom, f32), _pack_shards(var, f32))

    shapes = {nm: args[nm].shape for nm in _SMALL_NAMES}
    gsm = dict(g)
    gsm["l0_gdn_conv"] = jnp.zeros(shapes["l0_gdn_conv"], f32)
    packed, used = _pack_small(gsm), None
    flat_extra = jnp.concatenate([jnp.sum(loss_row).reshape(1), g["l0_gdn_conv"].reshape(-1)])
    _, used = _unpack_small(packed, shapes)
    packed = packed.reshape(-1).at[used:used + flat_extra.shape[0]].set(flat_extra).reshape(_SMALL_ROWS, D)
    parts = _allgather("ag_small", packed)
    total = _sum_parts(parts).reshape(-1)
    loss = 0.5 * total[used] / D
    conv_g_full = total[used + 1:used + 1 + 4 * NDEV * conv_cols].reshape(4, NDEV * conv_cols)
    conv_g = lax.dynamic_slice(conv_g_full, (0, dev * conv_cols), (4, conv_cols))
    own_vals = {nm: jnp.zeros(shapes[nm], f32) for nm in _SMALL_NAMES}
    own_vals["l0_gdn_conv"] = conv_g
    own_mask = {nm: jnp.zeros(shapes[nm], f32) for nm in _SMALL_NAMES}
    own_mask["l0_gdn_conv"] = jnp.ones(shapes["l0_gdn_conv"], f32)
    gs, ds, ms, vs = _adamw_small(parts, _pack_small(args), _pack_small(mom), _pack_small(var),
                                  _pack_small(own_mask), _pack_small(own_vals))
    gs_d, _ = _unpack_small(gs, shapes)
    ds_d, _ = _unpack_small(ds, shapes)
    ms_d, _ = _unpack_small(ms, shapes)
    vs_d, _ = _unpack_small(vs, shapes)
    gs_d["l0_gdn_conv"] = conv_g

    def pick(big, small, nm):
        return _unpack_shard(big, nm, ncols.get(nm, 0)) if nm in _BIG_NAMES else small[nm]

    outs = [loss, dx.reshape(nb, t, D)]
    for big, small in ((gb, gs_d), (db, ds_d), (mb, ms_d), (vb, vs_d)):
        outs += [pick(big, small, nm) for nm in _NAMES]
    return tuple(outs)
```

```python
import functools

import jax
import jax.numpy as jnp
from jax import lax
from jax.experimental import pallas as pl
from jax.experimental.pallas import tpu as pltpu

f32, bf16 = jnp.float32, jnp.bfloat16
NN = (((1,), (0,)), ((), ()))
NT = (((1,), (1,)), ((), ()))
TN = (((0,), (0,)), ((), ()))
HI = lax.Precision.HIGHEST
MESH = pl.DeviceIdType.MESH
S = jax.ShapeDtypeStruct

EPS = 1e-6
D = 1024
LANES = 128
FOX_H, FOX_D, FOX_W = 8, 64, 512
GDN_H, HD, GDN_W = 4, 128, 512
HG_H = 8
CH = 64
ZW0 = 3840
NDEV = 8
ADAM_LR, ADAM_B1, ADAM_B2, ADAM_EPS, ADAM_WD, ADAM_STEP = 0.001, 0.9, 0.999, 1e-08, 0.01, 10

Z0_FQK, Z0_FV, Z0_GQKV, Z0_GG, Z0_SMALL = 0, 8, 12, 24, 28


def _dot(a, b, dims=NN, prec=None):
    return lax.dot_general(a, b, dims, precision=prec, preferred_element_type=f32)


def _iota2(shape, axis):
    return lax.broadcasted_iota(jnp.int32, shape, axis)


def _tok_specs(rows, consts, tm):
    specs = []
    for (_, w, base) in rows:
        specs.append(pl.BlockSpec((tm, w), functools.partial(lambda j, i, b: (i, b + j), b=base)))
    for (arr, w, base) in consts:
        if w is None:
            specs.append(pl.BlockSpec(arr.shape, lambda j, i: (0, 0)))
        else:
            specs.append(pl.BlockSpec((arr.shape[0], w), functools.partial(lambda j, i, b: (0, b + j), b=base)))
    return specs


def _tok_fwd(name, f, rows, consts, outs, tm, ncb=1, with_j=False):
    n = rows[0][0].shape[0]
    nin = len(rows) + len(consts)

    def body(*refs):
        ins = [r[...] for r in refs[:nin]]
        vals = f(pl.program_id(0), *ins) if with_j else f(*ins)
        for r, v in zip(refs[nin:], vals):
            r[...] = v.astype(r.dtype)

    return pl.pallas_call(
        body, name=name, grid=(ncb, n // tm),
        in_specs=_tok_specs(rows, consts, tm),
        out_specs=[pl.BlockSpec((tm, w), lambda j, i: (i, j)) for (w, _) in outs],
        out_shape=[S((n, w * ncb), dt) for (w, dt) in outs],
        compiler_params=pltpu.CompilerParams(dimension_semantics=("parallel", "parallel")),
    )(*[r[0] for r in rows], *[c[0] for c in consts])


def _tok_bwd(name, f, rows, consts, cots, tm, drow, dconst, ncb=1, with_j=False, addto=None):
    n = rows[0][0].shape[0]
    nr, nc, nct = len(rows), len(consts), len(cots)
    addto = addto or {}
    add_keys = sorted(addto)
    nadd = len(add_keys)

    def body(*refs):
        ins = [r[...] for r in refs[:nr + nc]]
        cot = [r[...] for r in refs[nr + nc:nr + nc + nct]]
        adds = refs[nr + nc + nct:nr + nc + nct + nadd]
        outs = refs[nr + nc + nct + nadd:]
        pos = list(drow) + [nr + k for k in dconst]

        def g(*dargs):
            full = list(ins)
            for p, a in zip(pos, dargs):
                full[p] = a
            return tuple(f(pl.program_id(0), *full) if with_j else f(*full))

        vals, vjp = jax.vjp(g, *[ins[p] for p in pos])
        grads = vjp(tuple(c.astype(v.dtype) for c, v in zip(cot, vals)))
        for k in range(len(drow)):
            gk = grads[k]
            if k in addto:
                gk = gk + adds[add_keys.index(k)][...]
            outs[k][...] = gk.astype(outs[k].dtype)
        first = pl.program_id(1) == 0
        for k in range(len(dconst)):
            ref = outs[len(drow) + k]

            @pl.when(first)
            def _():
                ref[...] = jnp.zeros_like(ref)

            ref[...] += grads[len(drow) + k]

    in_specs = _tok_specs(rows, consts, tm)
    in_specs += [pl.BlockSpec((tm, w), lambda j, i: (i, j)) for (_, w) in cots]
    in_specs += [pl.BlockSpec((tm, rows[drow[k]][1]), lambda j, i: (i, j)) for k in add_keys]
    out_specs = [pl.BlockSpec((tm, rows[k][1]), lambda j, i: (i, j)) for k in drow]
    out_shape = [S((n, rows[k][1] * ncb), f32) for k in drow]
    for k in dconst:
        arr, w, _ = consts[k]
        if w is None:
            out_specs.append(pl.BlockSpec(arr.shape, lambda j, i: (0, 0)))
            out_shape.append(S(arr.shape, f32))
        else:
            out_specs.append(pl.BlockSpec((arr.shape[0], w), lambda j, i: (0, j)))
            out_shape.append(S((arr.shape[0], w * ncb), f32))
    return pl.pallas_call(
        body, name=name, grid=(ncb, n // tm), in_specs=in_specs, out_specs=out_specs, out_shape=out_shape,
        compiler_params=pltpu.CompilerParams(dimension_semantics=("parallel", "arbitrary")),
    )(*[r[0] for r in rows], *[c[0] for c in consts], *[c[0] for c in cots], *[addto[k] for k in add_keys])


def _scan_fwd(name, f, rows, nh, nchunk):
    n = rows[0][0].shape[0]
    nb = n // (CH * nchunk)
    nin = len(rows)
    w = nh * HD

    def body(*refs):
        o_ref, hist_ref, st = refs[nin], refs[nin + 1], refs[nin + 2]

        @pl.when(pl.program_id(1) == 0)
        def _():
            st[...] = jnp.zeros_like(st)

        s0 = st[...]
        hist_ref[...] = s0
        o, s1 = f(*[r[...] for r in refs[:nin]], s0)
        o_ref[...] = o
        st[...] = s1

    def rmap(b, c, base):
        return (b * nchunk + c, base)

    return pl.pallas_call(
        body, name=name, grid=(nb, nchunk),
        in_specs=[pl.BlockSpec((CH, w), functools.partial(rmap, base=b)) for (_, b) in rows],
        out_specs=[pl.BlockSpec((CH, w), functools.partial(rmap, base=0)),
                   pl.BlockSpec((w, HD), functools.partial(rmap, base=0))],
        out_shape=[S((n, w), f32), S((nb * nchunk * w, HD), f32)],
        scratch_shapes=[pltpu.VMEM((w, HD), f32)],
        compiler_params=pltpu.CompilerParams(dimension_semantics=("parallel", "arbitrary")),
    )(*[r[0] for r in rows])


def _scan_bwd(name, f, rows, hist, do, nh, nchunk):
    n = rows[0][0].shape[0]
    nb = n // (CH * nchunk)
    nin = len(rows)
    w = nh * HD

    def body(*refs):
        hist_ref, do_ref = refs[nin], refs[nin + 1]
        outs = refs[nin + 2:nin + 2 + nin]
        ds = refs[nin + 2 + nin]

        @pl.when(pl.program_id(1) == 0)
        def _():
            ds[...] = jnp.zeros_like(ds)

        _, vjp = jax.vjp(f, *[r[...] for r in refs[:nin]], hist_ref[...])
        grads = vjp((do_ref[...], ds[...]))
        for r, gk in zip(outs, grads[:nin]):
            r[...] = gk
        ds[...] = grads[nin]

    def rmap(b, c, base):
        return (b * nchunk + (nchunk - 1 - c), base)

    return pl.pallas_call(
        body, name=name, grid=(nb, nchunk),
        in_specs=[pl.BlockSpec((CH, w), functools.partial(rmap, base=b)) for (_, b) in rows]
        + [pl.BlockSpec((w, HD), functools.partial(rmap, base=0)), pl.BlockSpec((CH, w), functools.partial(rmap, base=0))],
        out_specs=[pl.BlockSpec((CH, w), functools.partial(rmap, base=0)) for _ in rows],
        out_shape=[S((n, w), f32) for _ in rows],
        scratch_shapes=[pltpu.VMEM((w, HD), f32)],
        compiler_params=pltpu.CompilerParams(dimension_semantics=("parallel", "arbitrary")),
    )(*[r[0] for r in rows], hist, do)


def _pick(n, cands):
    for c in cands:
        if n % c == 0:
            return c
    raise ValueError(f"no tile for {n}")


def _mm(name, a, b, dims, out_dtype, a_fn=None, epi=None, extra=()):
    if dims is TN:
        r, m = a.shape
        nn = b.shape[1]
        tm, tn, tk = _pick(m, (512, 256, 128)), _pick(nn, (768, 512, 256, 128)), _pick(r, (512, 256, 128, 64))

        def body(a_ref, b_ref, o_ref, acc):
            k = pl.program_id(2)

            @pl.when(k == 0)
            def _():
                acc[...] = jnp.zeros_like(acc)

            av = a_ref[...]
            if a_fn is not None:
                av = a_fn(av)
            acc[...] += _dot(av.astype(bf16), b_ref[...].astype(bf16), TN)

            @pl.when(k == pl.num_programs(2) - 1)
            def _():
                o_ref[...] = acc[...].astype(o_ref.dtype)

        return pl.pallas_call(
            body, name=name, grid=(m // tm, nn // tn, r // tk),
            in_specs=[pl.BlockSpec((tk, tm), lambda i, j, k: (k, i)), pl.BlockSpec((tk, tn), lambda i, j, k: (k, j))],
            out_specs=pl.BlockSpec((tm, tn), lambda i, j, k: (i, j)),
            out_shape=S((m, nn), out_dtype), scratch_shapes=[pltpu.VMEM((tm, tn), f32)],
            compiler_params=pltpu.CompilerParams(dimension_semantics=("parallel", "parallel", "arbitrary")),
        )(a, b)

    m, kk = a.shape
    nn = b.shape[1] if dims is NN else b.shape[0]
    tm, tn = _pick(m, (512, 256, 128, 64)), _pick(nn, (768, 512, 256, 128))
    nex = len(extra)

    def body(a_ref, b_ref, *rest):
        av = a_ref[...]
        if a_fn is not None:
            av = a_fn(av)
        acc = _dot(av.astype(bf16), b_ref[...].astype(bf16), dims)
        if epi is not None:
            acc = epi(acc, *[r[...] for r in rest[:nex]])
        rest[nex][...] = acc.astype(rest[nex].dtype)

    bspec = pl.BlockSpec((kk, tn), lambda i, j: (0, j)) if dims is NN else pl.BlockSpec((tn, kk), lambda i, j: (j, 0))
    return pl.pallas_call(
        body, name=name, grid=(m // tm, nn // tn),
        in_specs=[pl.BlockSpec((tm, kk), lambda i, j: (i, 0)), bspec]
        + [pl.BlockSpec((tm, tn), lambda i, j: (i, j)) for _ in extra],
        out_specs=pl.BlockSpec((tm, tn), lambda i, j: (i, j)),
        out_shape=S((m, nn), out_dtype),
        compiler_params=pltpu.CompilerParams(dimension_semantics=("parallel", "parallel")),
    )(a, b, *extra)


def _f_norm(x, g):
    return (x * lax.rsqrt(jnp.mean(x * x, axis=-1, keepdims=True) + EPS) * g,)


def _f_foxpre(zqk, gq, gk, pm):
    def nrm(t, g):
        return t * lax.rsqrt(_dot(t * t, pm, prec=HI) + EPS) * g
    return nrm(zqk[:, :FOX_W], gq), nrm(zqk[:, FOX_W:], gk)


def _chunk_cumsum(x):
    n = x.shape[0]
    r, c = _iota2((n, n), 0), _iota2((n, n), 1)
    tri = jnp.logical_and(r >= c, (r // CH) == (c // CH)).astype(f32)
    return _dot(tri, x, prec=HI)


def _f_gdngate(zs, eb, ea, alog_b, dt_b):
    beta = jax.nn.sigmoid(_dot(zs, eb, prec=HI))
    la = -jnp.exp(alog_b) * jax.nn.softplus(_dot(zs, ea, prec=HI) + dt_b)
    return beta, _chunk_cumsum(la)


def _f_conv(j, x, w):
    t = x.shape[0]
    y = x * w[3:4, :]
    for jj in range(3):
        sh = 3 - jj
        xs = jnp.concatenate([jnp.zeros((sh, x.shape[1]), f32), x[:t - sh, :]], axis=0)
        y = y + xs * w[jj:jj + 1, :]
    y = jax.nn.silu(y)
    yn = y * lax.rsqrt(jnp.sum(y * y, axis=-1, keepdims=True) + EPS)
    return (jnp.where(j < 2 * GDN_H, yn, y),)


def _head_rms(o, nh):
    outs = []
    for h in range(nh):
        oh = o[:, HD * h:HD * (h + 1)]
        outs.append(oh * lax.rsqrt(jnp.mean(oh * oh, axis=-1, keepdims=True) + EPS))
    return jnp.concatenate(outs, axis=1)


def _f_post0(fox_o, o, gg, on):
    return (jnp.concatenate([fox_o, _head_rms(o, GDN_H) * on * jax.nn.silu(gg)], axis=1),)


def _f_post1(o, zg, on):
    return (_head_rms(o, HG_H) * on * jax.nn.silu(zg),)


def _f_hpre(zqf, lbl):
    lb = jax.nn.sigmoid(lbl[1:2, :] - lbl[0:1, :])
    fg = lb + (1.0 - lb) * jax.nn.sigmoid(zqf[:, D:])
    return jax.nn.silu(zqf[:, :D]), 1.0 - fg, _chunk_cumsum(jnp.log(fg))


def _dotb(a, b, dims=NN):
    return _dot(a.astype(bf16), b.astype(bf16), dims)


def _dot3(a, b):
    ah, bh = a.astype(bf16), b.astype(bf16)
    al, bl = (a - ah.astype(f32)).astype(bf16), (b - bh.astype(f32)).astype(bf16)
    return _dot(ah, bh) + (_dot(ah, bl) + _dot(al, bh))


def _split(t, nh):
    return [t[CH * ck:CH * (ck + 1), HD * h:HD * (h + 1)] for ck in range(t.shape[0] // CH) for h in range(nh)]


def _merge(units, nh):
    return jnp.concatenate([jnp.concatenate(units[i:i + nh], axis=1) for i in range(0, len(units), nh)], axis=0)


def _inv_impl(amats):
    n = amats[0].shape[0]
    eye = jnp.where(_iota2((n, n), 0) == _iota2((n, n), 1), 1.0, 0.0).astype(f32)
    xs, ps = [eye - a for a in amats], list(amats)
    for _ in range(max(1, (n - 1).bit_length()) - 1):
        ps = [_dotb(p, p) for p in ps]
        xs = [x + _dotb(x, p) for x, p in zip(xs, ps)]
    for _ in range(2):
        rs = [eye - x - _dot3(a, x) for a, x in zip(amats, xs)]
        xs = [x + _dotb(x, r) for x, r in zip(xs, rs)]
    return tuple(xs)


@jax.custom_vjp
def _inv_unit_lower(amats):
    return _inv_impl(amats)


def _inv_fwd(amats):
    xs = _inv_impl(amats)
    return xs, xs


def _inv_bwd(xs, dxs):
    return (tuple(-_dotb(_dotb(x, dx, TN), x, NT) for x, dx in zip(xs, dxs)),)


_inv_unit_lower.defvjp(_inv_fwd, _inv_bwd)


def _f_gdn_intra(q, k, v, bb, gb):
    qs, ks, vs, bs, gs = (_split(t, GDN_H) for t in (q, k, v, bb, gb))
    r, cc = _iota2((CH, CH), 0), _iota2((CH, CH), 1)
    causal, strict = r >= cc, r > cc
    beta, g, gl = [b[:, :1] for b in bs], [x[:, :1] for x in gs], [x[CH - 1:CH, :1] for x in gs]
    decay = [jnp.exp(jnp.where(causal, x[:, :CH] - x[:, :CH].T, -jnp.inf)) for x in gs]
    kb = [ki * bi for ki, bi in zip(ks, beta)]
    amat = [jnp.where(strict, _dotb(kbi, ki, NT) * di, 0.0) for kbi, ki, di in zip(kb, ks, decay)]
    tinv = _inv_unit_lower(tuple(amat))
    rhs = [jnp.concatenate([vi * bi, kbi * jnp.exp(gi)], axis=1) for vi, bi, kbi, gi in zip(vs, beta, kb, g)]
    uw = [_dotb(ti, ri) for ti, ri in zip(tinv, rhs)]
    qsc = [qi * (HD ** -0.5) for qi in qs]
    qk = [jnp.where(causal, _dotb(qi, ki, NT) * di, 0.0) for qi, ki, di in zip(qsc, ks, decay)]
    outs = ([x[:, :HD] for x in uw], [x[:, HD:] for x in uw],
            [jnp.concatenate([x, jnp.zeros_like(x)], axis=1) for x in qk],
            [qi * jnp.exp(gi) for qi, gi in zip(qsc, g)],
            [ki * jnp.exp(gli - gi) for ki, gli, gi in zip(ks, gl, g)],
            [jnp.broadcast_to(gli, (CH, HD)) for gli in gl])
    return tuple(_merge(o, GDN_H) for o in outs)


def _f_gdn_inter(u, w, qkp, qd, kd, glb, st):
    us, ws, qks, qds, kds, gls = (_split(t, GDN_H) for t in (u, w, qkp, qd, kd, glb))
    sts = [st[HD * h:HD * (h + 1), :] for h in range(GDN_H)]
    vn = [ui - _dotb(wi, si) for ui, wi, si in zip(us, ws, sts)]
    o = [_dotb(qi, si) + _dotb(xi[:, :CH], vi) for qi, si, xi, vi in zip(qds, sts, qks, vn)]
    s2 = [si * jnp.exp(gi[:1, :1]) + _dotb(ki, vi, TN) for si, gi, ki, vi in zip(sts, gls, kds, vn)]
    return jnp.concatenate(o, axis=1), jnp.concatenate(s2, axis=0)


def _f_hgrn_chunk(q, k, b, v, st):
    qs, ks, bs, vs = (_split(t, HG_H) for t in (q, k, b, v))
    sts = [st[HD * h:HD * (h + 1), :] for h in range(HG_H)]
    causal = _iota2((CH, CH), 0) >= _iota2((CH, CH), 1)
    bl, bm = [x[CH - 1:CH, :] for x in bs], [x[CH // 2 - 1:CH // 2, :] for x in bs]
    a = [jnp.where(causal, _dotb(qi * jnp.exp(bi - mi), ki * jnp.exp(mi - bi), NT), 0.0)
         for qi, ki, bi, mi in zip(qs, ks, bs, bm)]
    o = [_dotb(qi * jnp.exp(bi), si, NT) + _dotb(ai, vi) for qi, bi, si, ai, vi in zip(qs, bs, sts, a, vs)]
    s2 = [si * jnp.exp(li) + _dotb(vi, ki * jnp.exp(li - bi), TN) for si, li, vi, ki, bi in zip(sts, bl, vs, ks, bs)]
    return jnp.concatenate(o, axis=1), jnp.concatenate(s2, axis=0)


def _fox_gate_fwd(z0, fbias, t, tc=256):
    n = z0.shape[0]
    nt = t // tc

    def body(zs_ref, b_ref, ccol_ref, crow_ref, carry):
        @pl.when(pl.program_id(1) == 0)
        def _():
            carry[...] = jnp.zeros_like(carry)

        ls = jnp.where(_iota2((tc, LANES), 1) < FOX_H, jax.nn.log_sigmoid(zs_ref[...] + b_ref[...]), 0.0)
        tri = (_iota2((tc, tc), 0) >= _iota2((tc, tc), 1)).astype(f32)
        c = _dot(tri, ls, prec=HI) + carry[...]
        carry[...] = c[tc - 1:tc, :]
        ccol_ref[...] = c
        crow_ref[0] = c.T[:FOX_H, :]

    return pl.pallas_call(
        body, name="fox_gate_fwd", grid=(n // t, nt),
        in_specs=[pl.BlockSpec((tc, LANES), lambda b, i: (b * nt + i, Z0_SMALL)), pl.BlockSpec((1, LANES), lambda b, i: (0, 0))],
        out_specs=[pl.BlockSpec((tc, LANES), lambda b, i: (b * nt + i, 0)), pl.BlockSpec((1, FOX_H, tc), lambda b, i: (b, 0, i))],
        out_shape=[S((n, LANES), f32), S((n // t, FOX_H, t), f32)],
        scratch_shapes=[pltpu.VMEM((1, LANES), f32)],
        compiler_params=pltpu.CompilerParams(dimension_semantics=("parallel", "arbitrary")),
    )(z0, fbias)


def _fox_gate_bwd(z0, fbias, dcq, dck, t, tc=256):
    n = z0.shape[0]
    nt = t // tc

    def body(zs_ref, b_ref, dcq_ref, dck_ref, dz_ref, db_ref, carry):
        first = jnp.logical_and(pl.program_id(0) == 0, pl.program_id(1) == 0)

        @pl.when(pl.program_id(1) == 0)
        def _():
            carry[...] = jnp.zeros_like(carry)

        @pl.when(first)
        def _():
            db_ref[...] = jnp.zeros_like(db_ref)

        dc = dcq_ref[0] + dcq_ref[1] + dcq_ref[2] + dcq_ref[3]
        drow = dck_ref[0, 0] + dck_ref[1, 0] + dck_ref[2, 0] + dck_ref[3, 0]
        eye = (_iota2((FOX_H, LANES), 0) == _iota2((FOX_H, LANES), 1)).astype(f32)
        dc = dc + _dot(drow, eye, TN, prec=HI)
        triu = (_iota2((tc, tc), 0) <= _iota2((tc, tc), 1)).astype(f32)
        dls = _dot(triu, dc, prec=HI) + carry[...]
        carry[...] = dls[0:1, :]
        x = zs_ref[...] + b_ref[...]
        dz = jnp.where(_iota2((tc, LANES), 1) < FOX_H, dls * jax.nn.sigmoid(-x), 0.0)
        dz_ref[...] = dz
        db_ref[...] += jnp.sum(dz, axis=0, keepdims=True)

    def rev(b, i):
        return b * nt + (nt - 1 - i)

    return pl.pallas_call(
        body, name="fox_gate_bwd", grid=(n // t, nt),
        in_specs=[pl.BlockSpec((tc, LANES), lambda b, i: (rev(b, i), Z0_SMALL)), pl.BlockSpec((1, LANES), lambda b, i: (0, 0)),
                  pl.BlockSpec((4, tc, LANES), lambda b, i: (0, rev(b, i), 0)),
                  pl.BlockSpec((4, 1, FOX_H, tc), lambda b, i: (0, b, 0, nt - 1 - i))],
        out_specs=[pl.BlockSpec((tc, LANES), lambda b, i: (rev(b, i), 0)), pl.BlockSpec((1, LANES), lambda b, i: (0, 0))],
        out_shape=[S((n, LANES), f32), S((1, LANES), f32)],
        scratch_shapes=[pltpu.VMEM((1, LANES), f32)],
        compiler_params=pltpu.CompilerParams(dimension_semantics=("arbitrary", "arbitrary")),
    )(z0, fbias, dcq, dck)


def _fox_scores(hh, p, i, tq, t, q, k, ccol, crow):
    lane = _iota2((1, LANES), 1)
    mh = (lane // FOX_D) == hh
    h = 2 * p + hh
    qh = jnp.where(mh, q, 0.0).astype(bf16)
    s = _dot(qh, k, NT) * (FOX_D ** -0.5)
    cq = jnp.sum(jnp.where(lane == h, ccol, 0.0), axis=1, keepdims=True)
    ck = jnp.sum(jnp.where(_iota2((FOX_H, 1), 0) == h, crow, 0.0), axis=0, keepdims=True)
    causal = _iota2((1, t), 1) <= (i * tq + _iota2((tq, 1), 0))
    s = jnp.where(causal, s + cq - ck, -jnp.inf)
    pe = jnp.exp(s - jnp.max(s, axis=1, keepdims=True))
    return mh, qh, pe, jnp.sum(pe, axis=1, keepdims=True)


def _fox_attn_fwd(qn, kn, z0, ccol, crow, t, tq=256):
    n = qn.shape[0]
    nq = t // tq

    def body(q_ref, k_ref, v_ref, ccol_ref, crow_ref, o_ref):
        p, i = pl.program_id(1), pl.program_id(2)
        q, k, v = q_ref[...], k_ref[...].astype(bf16), v_ref[...].astype(bf16)
        acc = jnp.zeros((tq, LANES), f32)
        for hh in range(2):
            mh, _, pe, l = _fox_scores(hh, p, i, tq, t, q, k, ccol_ref[...], crow_ref[0])
            acc = jnp.where(mh, _dot(pe.astype(bf16), v) / l, acc)
        o_ref[...] = acc

    return pl.pallas_call(
        body, name="fox_attn_fwd", grid=(n // t, FOX_H // 2, nq),
        in_specs=[pl.BlockSpec((tq, LANES), lambda b, p, i: (b * nq + i, p)), pl.BlockSpec((t, LANES), lambda b, p, i: (b, p)),
                  pl.BlockSpec((t, LANES), lambda b, p, i: (b, Z0_FV + p)),
                  pl.BlockSpec((tq, LANES), lambda b, p, i: (b * nq + i, 0)), pl.BlockSpec((1, FOX_H, t), lambda b, p, i: (b, 0, 0))],
        out_specs=pl.BlockSpec((tq, LANES), lambda b, p, i: (b * nq + i, p)),
        out_shape=S((n, FOX_W), f32),
        compiler_params=pltpu.CompilerParams(dimension_semantics=("parallel", "parallel", "parallel")),
    )(qn, kn, z0, ccol, crow)


def _fox_attn_bwd(qn, kn, z0, ccol, crow, do, t, tq=256):
    n = qn.shape[0]
    nq = t // tq
    nb = n // t

    def body(q_ref, k_ref, v_ref, ccol_ref, crow_ref, do_ref, dq_ref, dk_ref, dv_ref, dcq_ref, dck_ref):
        p, i = pl.program_id(1), pl.program_id(2)

        @pl.when(i == 0)
        def _():
            dk_ref[...] = jnp.zeros_like(dk_ref)
            dv_ref[...] = jnp.zeros_like(dv_ref)
            dck_ref[...] = jnp.zeros_like(dck_ref)

        q, kf, v, dout = q_ref[...], k_ref[...], v_ref[...].astype(bf16), do_ref[...]
        k = kf.astype(bf16)
        lane = _iota2((1, LANES), 1)
        sub = _iota2((FOX_H, 1), 0)
        scale = FOX_D ** -0.5
        dq = jnp.zeros((tq, LANES), f32)
        dcq = jnp.zeros((tq, LANES), f32)
        for hh in range(2):
            mh, qh, pe, l = _fox_scores(hh, p, i, tq, t, q, k, ccol_ref[...], crow_ref[0])
            pr = pe / l
            doh = jnp.where(mh, dout, 0.0).astype(bf16)
            dp = _dot(doh, v, NT)
            ds = pr * (dp - jnp.sum(pr * dp, axis=1, keepdims=True))
            dsb = ds.astype(bf16)
            dq = dq + _dot(dsb, jnp.where(mh, kf, 0.0).astype(bf16)) * scale
            dk_ref[...] += _dot(dsb, qh, TN) * scale
            dv_ref[...] += _dot(pr.astype(bf16), doh, TN)
            h = 2 * p + hh
            dcq = dcq + jnp.where(lane == h, jnp.sum(ds, axis=1, keepdims=True), 0.0)
            dck_ref[0, 0] += jnp.where(sub == h, -jnp.sum(ds, axis=0, keepdims=True), 0.0)
        dq_ref[...] = dq
        dcq_ref[0] = dcq

    return pl.pallas_call(
        body, name="fox_attn_bwd", grid=(nb, FOX_H // 2, nq),
        in_specs=[pl.BlockSpec((tq, LANES), lambda b, p, i: (b * nq + i, p)), pl.BlockSpec((t, LANES), lambda b, p, i: (b, p)),
                  pl.BlockSpec((t, LANES), lambda b, p, i: (b, Z0_FV + p)),
                  pl.BlockSpec((tq, LANES), lambda b, p, i: (b * nq + i, 0)), pl.BlockSpec((1, FOX_H, t), lambda b, p, i: (b, 0, 0)),
                  pl.BlockSpec((tq, LANES), lambda b, p, i: (b * nq + i, p))],
        out_specs=[pl.BlockSpec((tq, LANES), lambda b, p, i: (b * nq + i, p)), pl.BlockSpec((t, LANES), lambda b, p, i: (b, p)),
                   pl.BlockSpec((t, LANES), lambda b, p, i: (b, p)),
                   pl.BlockSpec((1, tq, LANES), lambda b, p, i: (p, b * nq + i, 0)),
                   pl.BlockSpec((1, 1, FOX_H, t), lambda b, p, i: (p, b, 0, 0))],
        out_shape=[S((n, FOX_W), f32), S((n, FOX_W), f32), S((n, FOX_W), f32), S((4, n, LANES), f32), S((4, nb, FOX_H, t), f32)],
        compiler_params=pltpu.CompilerParams(dimension_semantics=("parallel", "parallel", "arbitrary")),
    )(qn, kn, z0, ccol, crow, do)


def _loss_head(y, tgt, tm=256):
    n = y.shape[0]

    def body(y_ref, t_ref, dy_ref, l_ref):
        @pl.when(pl.program_id(0) == 0)
        def _():
            l_ref[...] = jnp.zeros_like(l_ref)

        e = y_ref[...] - t_ref[...]
        dy_ref[...] = e * (1.0 / D)
        l_ref[...] += jnp.sum(e * e, axis=0, keepdims=True)

    return pl.pallas_call(
        body, name="loss_head", grid=(n // tm,),
        in_specs=[pl.BlockSpec((tm, D), lambda i: (i, 0))] * 2,
        out_specs=[pl.BlockSpec((tm, D), lambda i: (i, 0)), pl.BlockSpec((1, D), lambda i: (0, 0))],
        out_shape=[S((n, D), f32), S((1, D), f32)],
        compiler_params=pltpu.CompilerParams(dimension_semantics=("arbitrary",)),
    )(y, tgt)


def _adamw_math(w, g, m, v):
    m = ADAM_B1 * m + (1.0 - ADAM_B1) * g
    v = ADAM_B2 * v + (1.0 - ADAM_B2) * (g * g)
    m_hat = m / (1.0 - ADAM_B1 ** ADAM_STEP)
    v_hat = v / (1.0 - ADAM_B2 ** ADAM_STEP)
    return -ADAM_LR * (m_hat / (jnp.sqrt(v_hat) + ADAM_EPS) + ADAM_WD * w), m, v


def _adamw_big(idx, gmine, recv1, recv2, w, m, v, tr=256):
    r, c = w.shape

    def body(idx_ref, gm_ref, r1_ref, r2_ref, w_ref, m_ref, v_ref, g_ref, d_ref, nm_ref, nv_ref):
        g = gm_ref[0] + r1_ref[0].astype(f32)
        for k in range(3):
            g = g + r2_ref[k].astype(f32)
        d, nm, nv = _adamw_math(w_ref[...], g, m_ref[...], v_ref[...])
        g_ref[...] = g
        d_ref[...] = d
        nm_ref[...] = nm
        nv_ref[...] = nv

    row = pl.BlockSpec((tr, c), lambda i, s: (i, 0))
    return pl.pallas_call(
        body, name="adamw_big",
        grid_spec=pltpu.PrefetchScalarGridSpec(
            num_scalar_prefetch=1, grid=(r // tr,),
            in_specs=[pl.BlockSpec((1, tr, c), lambda i, s: (s[0], i, 0)), pl.BlockSpec((1, tr, c), lambda i, s: (s[1], i, 0)),
                      pl.BlockSpec((3, tr, c), lambda i, s: (0, i, 0)), row, row, row],
            out_specs=[row, row, row, row]),
        out_shape=[S((r, c), f32)] * 4,
        compiler_params=pltpu.CompilerParams(dimension_semantics=("parallel",)),
    )(idx, gmine, recv1, recv2, w, m, v)


def _pair_sum(idx, gmine, recv1, tr=256):
    _, _, r, c = gmine.shape

    def body(idx_ref, gm_ref, r1_ref, o_ref):
        o_ref[0] = (gm_ref[0, 0] + r1_ref[0].astype(f32)).astype(bf16)

    return pl.pallas_call(
        body, name="rs_pair_sum",
        grid_spec=pltpu.PrefetchScalarGridSpec(
            num_scalar_prefetch=1, grid=(4, r // tr),
            in_specs=[pl.BlockSpec((1, 1, tr, c), lambda ch, i, s: (ch, s[0], i, 0)), pl.BlockSpec((1, tr, c), lambda ch, i, s: (ch, i, 0))],
            out_specs=pl.BlockSpec((1, tr, c), lambda ch, i, s: (ch, i, 0))),
        out_shape=S((4, r, c), bf16),
        compiler_params=pltpu.CompilerParams(dimension_semantics=("parallel", "parallel")),
    )(idx, gmine, recv1)


def _adamw_small(parts, w, m, v, own_mask, own_g):
    def body(p_ref, w_ref, m_ref, v_ref, mask_ref, og_ref, g_ref, d_ref, nm_ref, nv_ref):
        g = p_ref[0]
        for k in range(1, NDEV):
            g = g + p_ref[k]
        g_ref[...] = g
        ge = jnp.where(mask_ref[...] > 0.5, og_ref[...], g)
        d, nm, nv = _adamw_math(w_ref[...], ge, m_ref[...], v_ref[...])
        d_ref[...] = d
        nm_ref[...] = nm
        nv_ref[...] = nv

    return pl.pallas_call(body, name="adamw_small", out_shape=[S(w.shape, f32)] * 4)(parts, w, m, v, own_mask, own_g)


def _sum_parts(parts):
    def body(p_ref, g_ref):
        g = p_ref[0]
        for k in range(1, NDEV):
            g = g + p_ref[k]
        g_ref[...] = g

    return pl.pallas_call(body, name="sum_parts", out_shape=S(parts.shape[1:], f32))(parts)


def _me():
    return lax.axis_index("x"), lax.axis_index("y"), lax.axis_index("c")


def _allgather(name, x):
    def body(x_ref, out_ref, send_sems, recv_sems, local_sem):
        mx, my, mc = _me()
        me, sib = (mx, my, mc), (mx, my, 1 - mc)
        chips = [(1 - mx, my), (mx, 1 - my), (1 - mx, 1 - my)]

        def slab(px, py, pc):
            return out_ref.at[4 * px + 2 * py + pc]

        def copy(k, block, to, src=None):
            return pltpu.make_async_remote_copy(
                src_ref=slab(*block) if src is None else src, dst_ref=slab(*block),
                send_sem=send_sems.at[k], recv_sem=recv_sems.at[k], device_id=to, device_id_type=MESH)

        mine = pltpu.make_async_copy(x_ref, slab(*me), local_sem)
        mine.start()
        first = [copy(0, me, sib, src=x_ref)] + [copy(1 + j, me, (*chip, mc), src=x_ref) for j, chip in enumerate(chips)]
        for cp in first:
            cp.start()
        passed = [copy(4 + j, (*chip, mc), sib) for j, chip in enumerate(chips)]
        for j, chip in enumerate(chips):
            copy(1 + j, (*chip, mc), me).wait_recv()
            passed[j].start()
        copy(0, sib, me).wait_recv()
        for j, chip in enumerate(chips):
            copy(4 + j, (*chip, 1 - mc), me).wait_recv()
        for cp in first + passed:
            cp.wait_send()
        mine.wait()

    return pl.pallas_call(
        body, name=name, out_shape=S((NDEV,) + x.shape, x.dtype),
        in_specs=[pl.BlockSpec(memory_space=pl.ANY)], out_specs=pl.BlockSpec(memory_space=pl.ANY),
        scratch_shapes=[pltpu.SemaphoreType.DMA((7,)), pltpu.SemaphoreType.DMA((7,)), pltpu.SemaphoreType.DMA(())],
    )(x)


def _rs_sibling(g):
    def body(g_ref, out_ref, send_sems, recv_sems):
        mx, my, mc = _me()
        sib = (mx, my, 1 - mc)
        cps = [pltpu.make_async_remote_copy(
            src_ref=g_ref.at[ch, 1 - mc], dst_ref=out_ref.at[ch], send_sem=send_sems.at[ch], recv_sem=recv_sems.at[ch],
            device_id=sib, device_id_type=MESH) for ch in range(4)]
        for cp in cps:
            cp.start()
        for cp in cps:
            cp.wait_recv()
        for cp in cps:
            cp.wait_send()

    return pl.pallas_call(
        body, name="rs_sibling", out_shape=S((4,) + g.shape[2:], g.dtype),
        in_specs=[pl.BlockSpec(memory_space=pl.ANY)], out_specs=pl.BlockSpec(memory_space=pl.ANY),
        scratch_shapes=[pltpu.SemaphoreType.DMA((4,)), pltpu.SemaphoreType.DMA((4,))],
    )(g)


def _rs_chips(p):
    def body(p_ref, out_ref, send_sems, recv_sems):
        mx, my, mc = _me()
        chips = [(1 - mx, my), (mx, 1 - my), (1 - mx, 1 - my)]
        cps = [pltpu.make_async_remote_copy(
            src_ref=p_ref.at[2 * cx + cy], dst_ref=out_ref.at[k], send_sem=send_sems.at[k], recv_sem=recv_sems.at[k],
            device_id=(cx, cy, mc), device_id_type=MESH) for k, (cx, cy) in enumerate(chips)]
        for cp in cps:
            cp.start()
        for cp in cps:
            cp.wait_recv()
        for cp in cps:
            cp.wait_send()

    return pl.pallas_call(
        body, name="rs_chips", out_shape=S((3,) + p.shape[1:], p.dtype),
        in_specs=[pl.BlockSpec(memory_space=pl.ANY)], out_specs=pl.BlockSpec(memory_space=pl.ANY),
        scratch_shapes=[pltpu.SemaphoreType.DMA((3,)), pltpu.SemaphoreType.DMA((3,))],
    )(p)


_COLW = 512
_BIG = (("l0_w_in", "col", 512), ("l0_w_out", "row", 128), ("l0_w_ff1", "col", 512), ("l0_w_ff2", "row", 512),
        ("l1_w_in", "col", 512), ("l1_w_out", "row", 128), ("l1_w_ff1", "col", 512), ("l1_w_ff2", "row", 512))
_BIG_ROWS = sum(r for _, _, r in _BIG)


def _pack_shards(shards, dtype):
    parts = []
    for name, kind, _ in _BIG:
        a = shards[name].astype(dtype)
        if kind == "col":
            a = jnp.pad(a, ((0, 0), (0, _COLW - a.shape[1]))).reshape(-1, D)
        parts.append(a)
    return jnp.concatenate(parts, axis=0)


def _unpack_shard(slab, name, ncols):
    off = 0
    for nm, kind, rows in _BIG:
        if nm == name:
            a = slab[off:off + rows]
            return a.reshape(D, _COLW)[:, :ncols] if kind == "col" else a
        off += rows
    raise KeyError(name)


def _full_from_slabs(slabs, name, ncols):
    off = 0
    for nm, kind, rows in _BIG:
        if nm == name:
            a = slabs[:, off:off + rows]
            if kind == "row":
                return a.reshape(NDEV * rows, D)
            return a.reshape(NDEV, D, _COLW)[:, :, :ncols].transpose(1, 0, 2).reshape(D, NDEV * ncols)
        off += rows
    raise KeyError(name)


def _slabs_from_full(grads, ncols):
    parts = []
    for name, kind, rows in _BIG:
        g = grads[name]
        if kind == "row":
            parts.append(g.reshape(NDEV, rows, D))
        else:
            nc = ncols[name]
            g = g.reshape(D, NDEV, nc).transpose(1, 0, 2)
            g = jnp.pad(g, ((0, 0), (0, 0), (0, _COLW - nc)))
            parts.append(g.reshape(NDEV, rows, D))
    return jnp.concatenate(parts, axis=1)


def _regroup_w_in0(w):
    main = jnp.concatenate([w[:, 0:1536], w[:, 1544:3080], w[:, 3088:3600]], axis=1)
    small = jnp.concatenate([w[:, 1536:1544], w[:, 3080:3088]], axis=1)
    return jnp.concatenate([main, small, jnp.zeros((D, ZW0 - 3584 - 16), w.dtype)], axis=1)


def _ungroup_w_in0(g):
    return jnp.concatenate([g[:, 0:1536], g[:, 3584:3592], g[:, 1536:3072], g[:, 3592:3600], g[:, 3072:3584]], axis=1)


def _mlp_fwd(tag, x, gain, w1, w2):
    (h,) = _tok_fwd(f"{tag}_ffn_norm", _f_norm, [(x, D, 0)], [(gain, None, 0)], [(D, bf16)], 256)
    a = _mm(f"{tag}_ff1", h, w1, NN, bf16, epi=lambda acc: jnp.maximum(acc, 0.0))
    sq = lambda t: jnp.square(t.astype(f32))
    y = _mm(f"{tag}_ff2", a, w2, NN, f32, a_fn=sq, epi=lambda acc, r: acc + r, extra=(x,))
    return y, (h, a)


def _mlp_bwd(tag, x, gain, w1, w2, saved, dy):
    h, a = saved
    sq = lambda t: jnp.square(t.astype(f32))
    da = _mm(f"{tag}_ff2_dx", dy, w2, NT, bf16, epi=lambda acc, av: acc * 2.0 * av.astype(f32), extra=(a,))
    dw2 = _mm(f"{tag}_ff2_dw", a, dy, TN, f32, a_fn=sq)
    dh = _mm(f"{tag}_ff1_dx", da, w1, NT, f32)
    dw1 = _mm(f"{tag}_ff1_dw", h, da, TN, f32)
    dx, dgain = _tok_bwd(f"{tag}_ffn_norm_bwd", _f_norm, [(x, D, 0)], [(gain, None, 0)], [(dh, D)], 256, [0], [0], addto={0: dy})
    return dx, dgain, dw1, dw2


def _row(v):
    return v.reshape(1, -1).astype(f32)


def _local_step(x, tgt, p, t):
    n = x.shape[0]
    nchunk = t // CH
    g = {}

    li = jnp.arange(FOX_W)
    pm = jnp.where((li[:, None] // FOX_D) == (li[None, :] // FOX_D), 1.0 / FOX_D, 0.0).astype(f32)
    lane_head = jnp.arange(GDN_W) // HD
    sel = lambda first: (jnp.arange(LANES)[:, None] == (first + lane_head)[None, :]).astype(f32)
    e_beta, e_alpha = sel(FOX_H), sel(FOX_H + GDN_H)
    alog_b, dt_b = _row(jnp.repeat(p["l0_gdn_A_log"], HD)), _row(jnp.repeat(p["l0_gdn_dt_bias"], HD))
    gq_t, gk_t = _row(jnp.tile(p["l0_fox_q_norm"], FOX_H)), _row(jnp.tile(p["l0_fox_k_norm"], FOX_H))
    on0_t, on1_t = _row(jnp.tile(p["l0_gdn_o_norm"], GDN_H)), _row(jnp.tile(p["l1_hgrn_o_norm"], HG_H))
    fbias = jnp.pad(_row(p["l0_fox_f_bias"]), ((0, 0), (0, LANES - FOX_H)))
    g0m, g0f, g1m, g1f = (_row(p[k]) for k in ("l0_mix_norm", "l0_ffn_norm", "l1_mix_norm", "l1_ffn_norm"))
    wconv = p["l0_gdn_conv"].astype(f32)
    lbl = p["hgrn_lb_logits"].astype(f32)

    (h0,) = _tok_fwd("l0_mix_norm", _f_norm, [(x, D, 0)], [(g0m, None, 0)], [(D, bf16)], 256)
    z0 = _mm("l0_in", h0, p["l0_w_in"], NN, f32)
    qk_rows = [(z0, 2 * FOX_W, 0)]
    qk_consts = [(gq_t, None, 0), (gk_t, None, 0), (pm, None, 0)]
    qn, kn = _tok_fwd("fox_pre", _f_foxpre, qk_rows, qk_consts, [(FOX_W, f32)] * 2, 256)
    ccol, crow = _fox_gate_fwd(z0, fbias, t)
    fox_o = _fox_attn_fwd(qn, kn, z0, ccol, crow, t)
    conv_rows, conv_consts = [(z0, LANES, Z0_GQKV)], [(wconv, LANES, 0)]
    (qkv,) = _tok_fwd("gdn_conv", _f_conv, conv_rows, conv_consts, [(LANES, f32)], t, ncb=12, with_j=True)
    gate_rows = [(z0, LANES, Z0_SMALL)]
    gate_consts = [(e_beta, None, 0), (e_alpha, None, 0), (alog_b, None, 0), (dt_b, None, 0)]
    beta_b, la_b = _tok_fwd("gdn_gate", _f_gdngate, gate_rows, gate_consts, [(GDN_W, f32)] * 2, 256)
    intra_rows = [(qkv, GDN_W, 0), (qkv, GDN_W, 1), (qkv, GDN_W, 2), (beta_b, GDN_W, 0), (la_b, GDN_W, 0)]
    intra = _tok_fwd("gdn_intra", _f_gdn_intra, intra_rows, [], [(GDN_W, f32)] * 6, 2 * CH)
    inter_rows = [(a, 0) for a in intra]
    gdn_o, gdn_hist = _scan_fwd("gdn_scan", _f_gdn_inter, inter_rows, GDN_H, nchunk)
    post0_rows, post0_consts = [(fox_o, FOX_W, 0), (gdn_o, GDN_W, 0), (z0, GDN_W, Z0_GG // 4)], [(on0_t, None, 0)]
    (cat0,) = _tok_fwd("l0_post", _f_post0, post0_rows, post0_consts, [(D, bf16)], 256)
    x1 = _mm("l0_out", cat0, p["l0_w_out"], NN, f32, epi=lambda acc, r: acc + r, extra=(x,))
    x2, mlp0 = _mlp_fwd("l0", x1, g0f, p["l0_w_ff1"], p["l0_w_ff2"])

    (h1,) = _tok_fwd("l1_mix_norm", _f_norm, [(x2, D, 0)], [(g1m, None, 0)], [(D, bf16)], 256)
    z1 = _mm("l1_in", h1, p["l1_w_in"], NN, f32)
    hpre_rows, hpre_consts = [(z1, 2 * D, 0)], [(lbl, None, 0)]
    hq, hk, hlf = _tok_fwd("hgrn_pre", _f_hpre, hpre_rows, hpre_consts, [(D, f32)] * 3, 256)
    hg_rows = [(hq, 0), (hk, 0), (hlf, 0), (z1, 2)]
    hg_o, hg_hist = _scan_fwd("hgrn_scan", _f_hgrn_chunk, hg_rows, HG_H, nchunk)
    post1_rows, post1_consts = [(hg_o, D, 0), (z1, D, 3)], [(on1_t, None, 0)]
    (cat1,) = _tok_fwd("l1_post", _f_post1, post1_rows, post1_consts, [(D, bf16)], 256)
    x3 = _mm("l1_out", cat1, p["l1_w_out"], NN, f32, epi=lambda acc, r: acc + r, extra=(x2,))
    y, mlp1 = _mlp_fwd("l1", x3, g1f, p["l1_w_ff1"], p["l1_w_ff2"])

    dy, loss_row = _loss_head(y, tgt)

    dx3, g["l1_ffn_norm"], g["l1_w_ff1"], g["l1_w_ff2"] = _mlp_bwd("l1", x3, g1f, p["l1_w_ff1"], p["l1_w_ff2"], mlp1, dy)
    dcat1 = _mm("l1_out_dx", dx3, p["l1_w_out"], NT, f32)
    g["l1_w_out"] = _mm("l1_out_dw", cat1, dx3, TN, f32)
    dhg_o, dzg, don1 = _tok_bwd("l1_post_bwd", _f_post1, post1_rows, post1_consts, [(dcat1, D)], 256, [0, 1], [0])
    dhq, dhk, dhlf, dzi = _scan_bwd("hgrn_scan_bwd", _f_hgrn_chunk, hg_rows, hg_hist, dhg_o, HG_H, nchunk)
    dzqf, dlbl = _tok_bwd("hgrn_pre_bwd", _f_hpre, hpre_rows, hpre_consts, [(dhq, D), (dhk, D), (dhlf, D)], 256, [0], [0])
    dz1 = jnp.concatenate([dzqf, dzi, dzg], axis=1)
    dh1 = _mm("l1_in_dx", dz1, p["l1_w_in"], NT, f32)
    g["l1_w_in"] = _mm("l1_in_dw", h1, dz1, TN, f32)
    dx2, g["l1_mix_norm"] = _tok_bwd("l1_mix_norm_bwd", _f_norm, [(x2, D, 0)], [(g1m, None, 0)], [(dh1, D)], 256, [0], [0], addto={0: dx3})
    g["l1_hgrn_o_norm"] = don1.reshape(HG_H, HD).sum(0)
    g["hgrn_lb_logits"] = dlbl

    dx1, g["l0_ffn_norm"], g["l0_w_ff1"], g["l0_w_ff2"] = _mlp_bwd("l0", x1, g0f, p["l0_w_ff1"], p["l0_w_ff2"], mlp0, dx2)
    dcat0 = _mm("l0_out_dx", dx1, p["l0_w_out"], NT, f32)
    g["l0_w_out"] = _mm("l0_out_dw", cat0, dx1, TN, f32)
    dfox_o, dgdn_o, dgg, don0 = _tok_bwd("l0_post_bwd", _f_post0, post0_rows, post0_consts, [(dcat0, D)], 256, [0, 1, 2], [0])
    dinter = _scan_bwd("gdn_scan_bwd", _f_gdn_inter, inter_rows, gdn_hist, dgdn_o, GDN_H, nchunk)
    dq_g, dk_g, dv_g, dbeta_b, dla_b = _tok_bwd("gdn_intra_bwd", _f_gdn_intra, intra_rows, [], [(a, GDN_W) for a in dinter], 2 * CH, [0, 1, 2, 3, 4], [])
    dqkv = jnp.concatenate([dq_g, dk_g, dv_g], axis=1)
    dzs_g, dalog_b, ddt_b = _tok_bwd("gdn_gate_bwd", _f_gdngate, gate_rows, gate_consts, [(dbeta_b, GDN_W), (dla_b, GDN_W)], 256, [0], [2, 3])
    dgqkv, dwconv = _tok_bwd("gdn_conv_bwd", _f_conv, conv_rows, conv_consts, [(dqkv, LANES)], t, [0], [0], ncb=12, with_j=True)
    dqn, dkn, dfv, dcq, dck = _fox_attn_bwd(qn, kn, z0, ccol, crow, dfox_o, t)
    dzs_f, dfb = _fox_gate_bwd(z0, fbias, dcq, dck, t)
    dzqk, dgq_t, dgk_t = _tok_bwd("fox_pre_bwd", _f_foxpre, qk_rows, qk_consts, [(dqn, FOX_W), (dkn, FOX_W)], 256, [0], [0, 1])
    dz0 = jnp.concatenate([dzqk, dfv, dgqkv, dgg, dzs_g + dzs_f, jnp.zeros((n, ZW0 - 3712), f32)], axis=1)
    dh0 = _mm("l0_in_dx", dz0, p["l0_w_in"], NT, f32)
    g["l0_w_in"] = _mm("l0_in_dw", h0, dz0, TN, f32)
    dx, g["l0_mix_norm"] = _tok_bwd("l0_mix_norm_bwd", _f_norm, [(x, D, 0)], [(g0m, None, 0)], [(dh0, D)], 256, [0], [0], addto={0: dx1})
    g["l0_fox_q_norm"] = dgq_t.reshape(FOX_H, FOX_D).sum(0)
    g["l0_fox_k_norm"] = dgk_t.reshape(FOX_H, FOX_D).sum(0)
    g["l0_fox_f_bias"] = dfb[0, :FOX_H]
    g["l0_gdn_conv"] = dwconv
    g["l0_gdn_A_log"] = dalog_b.reshape(GDN_H, HD).sum(1)
    g["l0_gdn_dt_bias"] = ddt_b.reshape(GDN_H, HD).sum(1)
    g["l0_gdn_o_norm"] = don0.reshape(GDN_H, HD).sum(0)
    return loss_row, dx, g


_NAMES = ("l0_mix_norm", "l0_w_in", "l0_fox_q_norm", "l0_fox_k_norm", "l0_fox_f_bias", "l0_gdn_conv", "l0_gdn_A_log",
          "l0_gdn_dt_bias", "l0_gdn_o_norm", "l0_w_out", "l0_ffn_norm", "l0_w_ff1", "l0_w_ff2", "l1_mix_norm", "l1_w_in",
          "l1_hgrn_o_norm", "l1_w_out", "l1_ffn_norm", "l1_w_ff1", "l1_w_ff2", "hgrn_lb_logits")
_BIG_NAMES = tuple(nm for nm, _, _ in _BIG)
_SMALL_NAMES = tuple(nm for nm in _NAMES if nm not in _BIG_NAMES)
_SMALL_ROWS = 16


def _pack_small(vals, conv_key="l0_gdn_conv"):
    flat = jnp.concatenate([vals[nm].reshape(-1).astype(f32) for nm in _SMALL_NAMES])
    return jnp.pad(flat, (0, _SMALL_ROWS * D - flat.shape[0])).reshape(_SMALL_ROWS, D)


def _unpack_small(packed, shapes):
    flat = packed.reshape(-1)
    out, off = {}, 0
    for nm in _SMALL_NAMES:
        size = 1
        for s in shapes[nm]:
            size *= s
        out[nm] = flat[off:off + size].reshape(shapes[nm])
        off += size
    return out, off


def kernel(x, l0_mix_norm, l0_w_in, l0_fox_q_norm, l0_fox_k_norm, l0_fox_f_bias, l0_gdn_conv, l0_gdn_A_log, l0_gdn_dt_bias, l0_gdn_o_norm, l0_w_out, l0_ffn_norm, l0_w_ff1, l0_w_ff2, l1_mix_norm, l1_w_in, l1_hgrn_o_norm, l1_w_out, l1_ffn_norm, l1_w_ff1, l1_w_ff2, hgrn_lb_logits, loss_target, m_l0_mix_norm, m_l0_w_in, m_l0_fox_q_norm, m_l0_fox_k_norm, m_l0_fox_f_bias, m_l0_gdn_conv, m_l0_gdn_A_log, m_l0_gdn_dt_bias, m_l0_gdn_o_norm, m_l0_w_out, m_l0_ffn_norm, m_l0_w_ff1, m_l0_w_ff2, m_l1_mix_norm, m_l1_w_in, m_l1_hgrn_o_norm, m_l1_w_out, m_l1_ffn_norm, m_l1_w_ff1, m_l1_w_ff2, m_hgrn_lb_logits, v_l0_mix_norm, v_l0_w_in, v_l0_fox_q_norm, v_l0_fox_k_norm, v_l0_fox_f_bias, v_l0_gdn_conv, v_l0_gdn_A_log, v_l0_gdn_dt_bias, v_l0_gdn_o_norm, v_l0_w_out, v_l0_ffn_norm, v_l0_w_ff1, v_l0_w_ff2, v_l1_mix_norm, v_l1_w_in, v_l1_hgrn_o_norm, v_l1_w_out, v_l1_ffn_norm, v_l1_w_ff1, v_l1_w_ff2, v_hgrn_lb_logits):
    args = dict(zip(_NAMES, (l0_mix_norm, l0_w_in, l0_fox_q_norm, l0_fox_k_norm, l0_fox_f_bias, l0_gdn_conv, l0_gdn_A_log, l0_gdn_dt_bias, l0_gdn_o_norm, l0_w_out, l0_ffn_norm, l0_w_ff1, l0_w_ff2, l1_mix_norm, l1_w_in, l1_hgrn_o_norm, l1_w_out, l1_ffn_norm, l1_w_ff1, l1_w_ff2, hgrn_lb_logits)))
    mom = dict(zip(_NAMES, (m_l0_mix_norm, m_l0_w_in, m_l0_fox_q_norm, m_l0_fox_k_norm, m_l0_fox_f_bias, m_l0_gdn_conv, m_l0_gdn_A_log, m_l0_gdn_dt_bias, m_l0_gdn_o_norm, m_l0_w_out, m_l0_ffn_norm, m_l0_w_ff1, m_l0_w_ff2, m_l1_mix_norm, m_l1_w_in, m_l1_hgrn_o_norm, m_l1_w_out, m_l1_ffn_norm, m_l1_w_ff1, m_l1_w_ff2, m_hgrn_lb_logits)))
    var = dict(zip(_NAMES, (v_l0_mix_norm, v_l0_w_in, v_l0_fox_q_norm, v_l0_fox_k_norm, v_l0_fox_f_bias, v_l0_gdn_conv, v_l0_gdn_A_log, v_l0_gdn_dt_bias, v_l0_gdn_o_norm, v_l0_w_out, v_l0_ffn_norm, v_l0_w_ff1, v_l0_w_ff2, v_l1_mix_norm, v_l1_w_in, v_l1_hgrn_o_norm, v_l1_w_out, v_l1_ffn_norm, v_l1_w_ff1, v_l1_w_ff2, v_hgrn_lb_logits)))
    nb, t, _ = x.shape
    mx, my, mc = _me()
    dev = 4 * mx + 2 * my + mc
    chip = 2 * mx + my
    ncols = {nm: args[nm].shape[1] for nm, kind, _ in _BIG if kind == "col"}
    conv_cols = l0_gdn_conv.shape[1]

    slabs = _allgather("ag_weights", _pack_shards(args, bf16))
    conv_pad = jnp.pad(l0_gdn_conv, ((0, 4), (0, LANES * 2 - conv_cols)))
    conv_all = _allgather("ag_conv", conv_pad)
    p = {nm: args[nm] for nm in _SMALL_NAMES}
    p["l0_gdn_conv"] = conv_all[:, :4, :conv_cols].transpose(1, 0, 2).reshape(4, NDEV * conv_cols)
    for nm in _BIG_NAMES:
        p[nm] = _full_from_slabs(slabs, nm, ncols.get(nm, 0))
    p["l0_w_in"] = _regroup_w_in0(p["l0_w_in"])

    loss_row, dx, g = _local_step(x.reshape(nb * t, D), loss_target.reshape(nb * t, D), p, t)
    g["l0_w_in"] = _ungroup_w_in0(g["l0_w_in"])

    gslabs = _slabs_from_full(g, ncols).reshape(4, 2, _BIG_ROWS, D)
    recv1 = _rs_sibling(gslabs.astype(bf16))
    core_idx = jnp.reshape(mc, (1,)).astype(jnp.int32)
    pair = _pair_sum(core_idx, gslabs, recv1)
    recv2 = _rs_chips(pair)
    own_idx = jnp.stack([dev, chip]).astype(jnp.int32)
    gb, db, mb, vb = _adamw_big(own_idx, gslabs.reshape(NDEV, _BIG_ROWS, D), recv1, recv2,
                                _pack_shards(args, f32), _pack_shards(mom, f32), _pack_shards(var, f32))

    shapes = {nm: args[nm].shape for nm in _SMALL_NAMES}
    gsm = dict(g)
    gsm["l0_gdn_conv"] = jnp.zeros(shapes["l0_gdn_conv"], f32)
    packed, used = _pack_small(gsm), None
    flat_extra = jnp.concatenate([jnp.sum(loss_row).reshape(1), g["l0_gdn_conv"].reshape(-1)])
    _, used = _unpack_small(packed, shapes)
    packed = packed.reshape(-1).at[used:used + flat_extra.shape[0]].set(flat_extra).reshape(_SMALL_ROWS, D)
    parts = _allgather("ag_small", packed)
    total = _sum_parts(parts).reshape(-1)
    loss = 0.5 * total[used] / D
    conv_g_full = total[used + 1:used + 1 + 4 * NDEV * conv_cols].reshape(4, NDEV * conv_cols)
    conv_g = lax.dynamic_slice(conv_g_full, (0, dev * conv_cols), (4, conv_cols))
    own_vals = {nm: jnp.zeros(shapes[nm], f32) for nm in _SMALL_NAMES}
    own_vals["l0_gdn_conv"] = conv_g
    own_mask = {nm: jnp.zeros(shapes[nm], f32) for nm in _SMALL_NAMES}
    own_mask["l0_gdn_conv"] = jnp.ones(shapes["l0_gdn_conv"], f32)
    gs, ds, ms, vs = _adamw_small(parts, _pack_small(args), _pack_small(mom), _pack_small(var),
                                  _pack_small(own_mask), _pack_small(own_vals))
    gs_d, _ = _unpack_small(gs, shapes)
    ds_d, _ = _unpack_small(ds, shapes)
    ms_d, _ = _unpack_small(ms, shapes)
    vs_d, _ = _unpack_small(vs, shapes)
    gs_d["l0_gdn_conv"] = conv_g

    def pick(big, small, nm):
        return _unpack_shard(big, nm, ncols.get(nm, 0)) if nm in _BIG_NAMES else small[nm]

    outs = [loss, dx.reshape(nb, t, D)]
    for big, small in ((gb, gs_d), (db, ds_d), (mb, ms_d), (vb, vs_d)):
        outs += [pick(big, small, nm) for nm in _NAMES]
    return tuple(outs)
```

```python
import functools

import jax
import jax.numpy as jnp
from jax import lax
from jax.experimental import pallas as pl
from jax.experimental.pallas import tpu as pltpu

f32, bf16 = jnp.float32, jnp.bfloat16
NN = (((1,), (0,)), ((), ()))
NT = (((1,), (1,)), ((), ()))
TN = (((0,), (0,)), ((), ()))
HI = lax.Precision.HIGHEST
MESH = pl.DeviceIdType.MESH
S = jax.ShapeDtypeStruct

EPS = 1e-6
D = 1024
LANES = 128
FOX_H, FOX_D, FOX_W = 8, 64, 512
GDN_H, HD, GDN_W = 4, 128, 512
HG_H = 8
CH = 64
ZW0 = 3840
NDEV = 8
ADAM_LR, ADAM_B1, ADAM_B2, ADAM_EPS, ADAM_WD, ADAM_STEP = 0.001, 0.9, 0.999, 1e-08, 0.01, 10

Z0_FQK, Z0_FV, Z0_GQKV, Z0_GG, Z0_SMALL = 0, 8, 12, 24, 28


def _dot(a, b, dims=NN, prec=None):
    return lax.dot_general(a, b, dims, precision=prec, preferred_element_type=f32)


def _iota2(shape, axis):
    return lax.broadcasted_iota(jnp.int32, shape, axis)


def _tok_specs(rows, consts, tm):
    specs = []
    for (_, w, base) in rows:
        specs.append(pl.BlockSpec((tm, w), functools.partial(lambda j, i, b: (i, b + j), b=base)))
    for (arr, w, base) in consts:
        if w is None:
            specs.append(pl.BlockSpec(arr.shape, lambda j, i: (0, 0)))
        else:
            specs.append(pl.BlockSpec((arr.shape[0], w), functools.partial(lambda j, i, b: (0, b + j), b=base)))
    return specs


def _tok_fwd(name, f, rows, consts, outs, tm, ncb=1, with_j=False, also_t=()):
    n = rows[0][0].shape[0]
    nin = len(rows) + len(consts)
    nout = len(outs)

    def body(*refs):
        ins = [r[...] for r in refs[:nin]]
        vals = f(pl.program_id(0), *ins) if with_j else f(*ins)
        for r, v in zip(refs[nin:nin + nout], vals):
            r[...] = v.astype(r.dtype)
        for r, k in zip(refs[nin + nout:], also_t):
            r[...] = vals[k].T.astype(r.dtype)

    return pl.pallas_call(
        body, name=name, grid=(ncb, n // tm),
        in_specs=_tok_specs(rows, consts, tm),
        out_specs=[pl.BlockSpec((tm, w), lambda j, i: (i, j)) for (w, _) in outs]
        + [pl.BlockSpec((outs[k][0], tm), lambda j, i: (0, i)) for k in also_t],
        out_shape=[S((n, w * ncb), dt) for (w, dt) in outs] + [S((outs[k][0], n), bf16) for k in also_t],
        compiler_params=pltpu.CompilerParams(dimension_semantics=("parallel", "parallel")),
    )(*[r[0] for r in rows], *[c[0] for c in consts])


def _tok_bwd(name, f, rows, consts, cots, tm, drow, dconst, ncb=1, with_j=False, addto=None, also_t=(), drow_dtype=f32):
    n = rows[0][0].shape[0]
    nr, nc, nct = len(rows), len(consts), len(cots)
    addto = addto or {}
    add_keys = sorted(addto)
    nadd = len(add_keys)

    def body(*refs):
        ins = [r[...] for r in refs[:nr + nc]]
        cot = [r[...] for r in refs[nr + nc:nr + nc + nct]]
        adds = refs[nr + nc + nct:nr + nc + nct + nadd]
        outs = refs[nr + nc + nct + nadd:]
        pos = list(drow) + [nr + k for k in dconst]

        def g(*dargs):
            full = list(ins)
            for p, a in zip(pos, dargs):
                full[p] = a
            return tuple(f(pl.program_id(0), *full) if with_j else f(*full))

        vals, vjp = jax.vjp(g, *[ins[p] for p in pos])
        grads = vjp(tuple(c.astype(v.dtype) for c, v in zip(cot, vals)))
        for k in range(len(drow)):
            gk = grads[k]
            if k in addto:
                gk = gk + adds[add_keys.index(k)][...]
            outs[k][...] = gk.astype(outs[k].dtype)
            if k in also_t:
                tref = outs[len(drow) + len(dconst) + list(also_t).index(k)]
                tref[...] = gk.T.astype(tref.dtype)
        first = pl.program_id(1) == 0
        for k in range(len(dconst)):
            ref = outs[len(drow) + k]

            @pl.when(first)
            def _():
                ref[...] = jnp.zeros_like(ref)

            ref[...] += grads[len(drow) + k]

    in_specs = _tok_specs(rows, consts, tm)
    in_specs += [pl.BlockSpec((tm, w), lambda j, i: (i, j)) for (_, w) in cots]
    in_specs += [pl.BlockSpec((tm, rows[drow[k]][1]), lambda j, i: (i, j)) for k in add_keys]
    out_specs = [pl.BlockSpec((tm, rows[k][1]), lambda j, i: (i, j)) for k in drow]
    out_shape = [S((n, rows[k][1] * ncb), drow_dtype) for k in drow]
    for k in dconst:
        arr, w, _ = consts[k]
        if w is None:
            out_specs.append(pl.BlockSpec(arr.shape, lambda j, i: (0, 0)))
            out_shape.append(S(arr.shape, f32))
        else:
            out_specs.append(pl.BlockSpec((arr.shape[0], w), lambda j, i: (0, j)))
            out_shape.append(S((arr.shape[0], w * ncb), f32))
    for k in also_t:
        out_specs.append(pl.BlockSpec((rows[drow[k]][1], tm), lambda j, i: (0, i)))
        out_shape.append(S((rows[drow[k]][1], n), bf16))
    return pl.pallas_call(
        body, name=name, grid=(ncb, n // tm), in_specs=in_specs, out_specs=out_specs, out_shape=out_shape,
        compiler_params=pltpu.CompilerParams(dimension_semantics=("parallel", "arbitrary")),
    )(*[r[0] for r in rows], *[c[0] for c in consts], *[c[0] for c in cots], *[addto[k] for k in add_keys])


def _scan_fwd(name, f, rows, nh, nchunk):
    n = rows[0][0].shape[0]
    nb = n // (CH * nchunk)
    nin = len(rows)
    w = nh * HD

    def body(*refs):
        o_ref, hist_ref, st = refs[nin], refs[nin + 1], refs[nin + 2]

        @pl.when(pl.program_id(1) == 0)
        def _():
            st[...] = jnp.zeros_like(st)

        s0 = st[...]
        hist_ref[...] = s0
        o, s1 = f(*[r[...] for r in refs[:nin]], s0)
        o_ref[...] = o
        st[...] = s1

    def rmap(b, c, base):
        return (b * nchunk + c, base)

    return pl.pallas_call(
        body, name=name, grid=(nb, nchunk),
        in_specs=[pl.BlockSpec((CH, w), functools.partial(rmap, base=b)) for (_, b) in rows],
        out_specs=[pl.BlockSpec((CH, w), functools.partial(rmap, base=0)),
                   pl.BlockSpec((w, HD), functools.partial(rmap, base=0))],
        out_shape=[S((n, w), f32), S((nb * nchunk * w, HD), f32)],
        scratch_shapes=[pltpu.VMEM((w, HD), f32)],
        compiler_params=pltpu.CompilerParams(dimension_semantics=("parallel", "arbitrary")),
    )(*[r[0] for r in rows])


def _scan_bwd(name, f, rows, hist, do, nh, nchunk):
    n = rows[0][0].shape[0]
    nb = n // (CH * nchunk)
    nin = len(rows)
    w = nh * HD

    def body(*refs):
        hist_ref, do_ref = refs[nin], refs[nin + 1]
        outs = refs[nin + 2:nin + 2 + nin]
        ds = refs[nin + 2 + nin]

        @pl.when(pl.program_id(1) == 0)
        def _():
            ds[...] = jnp.zeros_like(ds)

        _, vjp = jax.vjp(f, *[r[...] for r in refs[:nin]], hist_ref[...])
        grads = vjp((do_ref[...], ds[...]))
        for r, gk in zip(outs, grads[:nin]):
            r[...] = gk
        ds[...] = grads[nin]

    def rmap(b, c, base):
        return (b * nchunk + (nchunk - 1 - c), base)

    return pl.pallas_call(
        body, name=name, grid=(nb, nchunk),
        in_specs=[pl.BlockSpec((CH, w), functools.partial(rmap, base=b)) for (_, b) in rows]
        + [pl.BlockSpec((w, HD), functools.partial(rmap, base=0)), pl.BlockSpec((CH, w), functools.partial(rmap, base=0))],
        out_specs=[pl.BlockSpec((CH, w), functools.partial(rmap, base=0)) for _ in rows],
        out_shape=[S((n, w), f32) for _ in rows],
        scratch_shapes=[pltpu.VMEM((w, HD), f32)],
        compiler_params=pltpu.CompilerParams(dimension_semantics=("parallel", "arbitrary")),
    )(*[r[0] for r in rows], hist, do)


_VMEM_LIMIT = 56 * 2 ** 20
_VMEM_TILE_BUDGET = 40 * 2 ** 20


def _mm_tiles(m, n, k, sa, sb, so, sx, a_f32, b_f32, tn_fixed):
    best = None
    for tm in (1024, 512, 256, 128, 64):
        for tn in ((tn_fixed,) if tn_fixed else (1024, 768, 512, 384, 256, 128)):
            if m % tm or n % tn:
                continue
            need = 2 * (tm * k * sa + k * tn * sb + tm * tn * (so + sx)) + tm * tn * 4
            need += tm * k * (2 if sa == 4 else 0) + k * tn * (2 if sb == 4 else 0)
            need += tm * k * (4 if a_f32 else 0) + k * tn * (4 if b_f32 else 0)
            if need <= _VMEM_TILE_BUDGET and (best is None or (tm * tn, tm) > best[0]):
                best = ((tm * tn, tm), tm, tn)
    return best[1], best[2]


def _mm(name, a, b, dims, out_dtype, a_fn=None, b_fn=None, epi=None, extra=(), out_t=False, slab=False):
    m, kk = a.shape
    nn = b.shape[1] if dims is NN else b.shape[0]
    sx = sum(e.dtype.itemsize for e in extra)
    tm, tn = _mm_tiles(m, nn, kk, a.dtype.itemsize, b.dtype.itemsize, jnp.dtype(out_dtype).itemsize, sx,
                       a_fn is not None, b_fn is not None, _COLW if slab else None)
    nex = len(extra)

    def body(a_ref, b_ref, *rest):
        av, bv = a_ref[...], b_ref[...]
        if a_fn is not None:
            av = a_fn(av.astype(f32))
        if b_fn is not None:
            bv = b_fn(bv.astype(f32))
        acc = _dot(av.astype(bf16), bv.astype(bf16), dims)
        if epi is not None:
            acc = epi(acc, *[r[...] for r in rest[:nex]])
        o_ref = rest[nex]
        if out_t:
            o_ref[...] = acc.T.astype(o_ref.dtype)
        elif slab:
            o_ref[0] = acc.astype(o_ref.dtype)
        else:
            o_ref[...] = acc.astype(o_ref.dtype)

    bspec = pl.BlockSpec((kk, tn), lambda i, j: (0, j)) if dims is NN else pl.BlockSpec((tn, kk), lambda i, j: (j, 0))
    if out_t:
        out_spec, out_shape = pl.BlockSpec((tn, tm), lambda i, j: (j, i)), S((nn, m), out_dtype)
    elif slab:
        out_spec, out_shape = pl.BlockSpec((1, tm, tn), lambda i, j: (j, i, 0)), S((nn // tn, m, tn), out_dtype)
    else:
        out_spec, out_shape = pl.BlockSpec((tm, tn), lambda i, j: (i, j)), S((m, nn), out_dtype)
    return pl.pallas_call(
        body, name=name, grid=(m // tm, nn // tn),
        in_specs=[pl.BlockSpec((tm, kk), lambda i, j: (i, 0)), bspec]
        + [pl.BlockSpec((tm, tn), lambda i, j: (i, j)) for _ in extra],
        out_specs=out_spec, out_shape=out_shape,
        compiler_params=pltpu.CompilerParams(dimension_semantics=("parallel", "parallel"), vmem_limit_bytes=_VMEM_LIMIT),
    )(a, b, *extra)


def _f_norm(x, g):
    return (x * lax.rsqrt(jnp.mean(x * x, axis=-1, keepdims=True) + EPS) * g,)


def _f_foxpre(zqk, gq, gk, pm):
    def nrm(t, g):
        return t * lax.rsqrt(_dot(t * t, pm, prec=HI) + EPS) * g
    return nrm(zqk[:, :FOX_W], gq), nrm(zqk[:, FOX_W:], gk)


def _chunk_cumsum(x):
    n = x.shape[0]
    r, c = _iota2((n, n), 0), _iota2((n, n), 1)
    tri = jnp.logical_and(r >= c, (r // CH) == (c // CH)).astype(f32)
    return _dot(tri, x, prec=HI)


def _f_gdngate(zs, eb, ea, alog_b, dt_b):
    beta = jax.nn.sigmoid(_dot(zs, eb, prec=HI))
    la = -jnp.exp(alog_b) * jax.nn.softplus(_dot(zs, ea, prec=HI) + dt_b)
    return beta, _chunk_cumsum(la)


def _f_conv(j, x, w):
    t = x.shape[0]
    y = x * w[3:4, :]
    for jj in range(3):
        sh = 3 - jj
        xs = jnp.concatenate([jnp.zeros((sh, x.shape[1]), f32), x[:t - sh, :]], axis=0)
        y = y + xs * w[jj:jj + 1, :]
    y = jax.nn.silu(y)
    yn = y * lax.rsqrt(jnp.sum(y * y, axis=-1, keepdims=True) + EPS)
    return (jnp.where(j < 2 * GDN_H, yn, y),)


def _head_rms(o, nh):
    outs = []
    for h in range(nh):
        oh = o[:, HD * h:HD * (h + 1)]
        outs.append(oh * lax.rsqrt(jnp.mean(oh * oh, axis=-1, keepdims=True) + EPS))
    return jnp.concatenate(outs, axis=1)


def _f_post0(fox_o, o, gg, on):
    return (jnp.concatenate([fox_o, _head_rms(o, GDN_H) * on * jax.nn.silu(gg)], axis=1),)


def _f_post1(o, zg, on):
    return (_head_rms(o, HG_H) * on * jax.nn.silu(zg),)


def _f_hpre(zqf, lbl):
    lb = jax.nn.sigmoid(lbl[1:2, :] - lbl[0:1, :])
    fg = lb + (1.0 - lb) * jax.nn.sigmoid(zqf[:, D:])
    return jax.nn.silu(zqf[:, :D]), 1.0 - fg, _chunk_cumsum(jnp.log(fg))


def _dotb(a, b, dims=NN):
    return _dot(a.astype(bf16), b.astype(bf16), dims)


def _dot3(a, b):
    ah, bh = a.astype(bf16), b.astype(bf16)
    al, bl = (a - ah.astype(f32)).astype(bf16), (b - bh.astype(f32)).astype(bf16)
    return _dot(ah, bh) + (_dot(ah, bl) + _dot(al, bh))


def _split(t, nh):
    return [t[CH * ck:CH * (ck + 1), HD * h:HD * (h + 1)] for ck in range(t.shape[0] // CH) for h in range(nh)]


def _merge(units, nh):
    return jnp.concatenate([jnp.concatenate(units[i:i + nh], axis=1) for i in range(0, len(units), nh)], axis=0)


def _inv_impl(amats):
    n = amats[0].shape[0]
    eye = jnp.where(_iota2((n, n), 0) == _iota2((n, n), 1), 1.0, 0.0).astype(f32)
    xs, ps = [eye - a for a in amats], list(amats)
    for _ in range(max(1, (n - 1).bit_length()) - 1):
        ps = [_dotb(p, p) for p in ps]
        xs = [x + _dotb(x, p) for x, p in zip(xs, ps)]
    for _ in range(2):
        rs = [eye - x - _dot3(a, x) for a, x in zip(amats, xs)]
        xs = [x + _dotb(x, r) for x, r in zip(xs, rs)]
    return tuple(xs)


@jax.custom_vjp
def _inv_unit_lower(amats):
    return _inv_impl(amats)


def _inv_fwd(amats):
    xs = _inv_impl(amats)
    return xs, xs


def _inv_bwd(xs, dxs):
    return (tuple(-_dotb(_dotb(x, dx, TN), x, NT) for x, dx in zip(xs, dxs)),)


_inv_unit_lower.defvjp(_inv_fwd, _inv_bwd)


def _f_gdn_intra(q, k, v, bb, gb):
    qs, ks, vs, bs, gs = (_split(t, GDN_H) for t in (q, k, v, bb, gb))
    r, cc = _iota2((CH, CH), 0), _iota2((CH, CH), 1)
    causal, strict = r >= cc, r > cc
    beta, g, gl = [b[:, :1] for b in bs], [x[:, :1] for x in gs], [x[CH - 1:CH, :1] for x in gs]
    decay = [jnp.exp(jnp.where(causal, x[:, :CH] - x[:, :CH].T, -jnp.inf)) for x in gs]
    kb = [ki * bi for ki, bi in zip(ks, beta)]
    amat = [jnp.where(strict, _dotb(kbi, ki, NT) * di, 0.0) for kbi, ki, di in zip(kb, ks, decay)]
    tinv = _inv_unit_lower(tuple(amat))
    rhs = [jnp.concatenate([vi * bi, kbi * jnp.exp(gi)], axis=1) for vi, bi, kbi, gi in zip(vs, beta, kb, g)]
    uw = [_dotb(ti, ri) for ti, ri in zip(tinv, rhs)]
    qsc = [qi * (HD ** -0.5) for qi in qs]
    qk = [jnp.where(causal, _dotb(qi, ki, NT) * di, 0.0) for qi, ki, di in zip(qsc, ks, decay)]
    outs = ([x[:, :HD] for x in uw], [x[:, HD:] for x in uw],
            [jnp.concatenate([x, jnp.zeros_like(x)], axis=1) for x in qk],
            [qi * jnp.exp(gi) for qi, gi in zip(qsc, g)],
            [ki * jnp.exp(gli - gi) for ki, gli, gi in zip(ks, gl, g)],
            [jnp.broadcast_to(gli, (CH, HD)) for gli in gl])
    return tuple(_merge(o, GDN_H) for o in outs)


def _f_gdn_inter(u, w, qkp, qd, kd, glb, st):
    us, ws, qks, qds, kds, gls = (_split(t, GDN_H) for t in (u, w, qkp, qd, kd, glb))
    sts = [st[HD * h:HD * (h + 1), :] for h in range(GDN_H)]
    vn = [ui - _dotb(wi, si) for ui, wi, si in zip(us, ws, sts)]
    o = [_dotb(qi, si) + _dotb(xi[:, :CH], vi) for qi, si, xi, vi in zip(qds, sts, qks, vn)]
    s2 = [si * jnp.exp(gi[:1, :1]) + _dotb(ki, vi, TN) for si, gi, ki, vi in zip(sts, gls, kds, vn)]
    return jnp.concatenate(o, axis=1), jnp.concatenate(s2, axis=0)


def _f_hgrn_chunk(q, k, b, v, st):
    qs, ks, bs, vs = (_split(t, HG_H) for t in (q, k, b, v))
    sts = [st[HD * h:HD * (h + 1), :] for h in range(HG_H)]
    causal = _iota2((CH, CH), 0) >= _iota2((CH, CH), 1)
    bl, bm = [x[CH - 1:CH, :] for x in bs], [x[CH // 2 - 1:CH // 2, :] for x in bs]
    a = [jnp.where(causal, _dotb(qi * jnp.exp(bi - mi), ki * jnp.exp(mi - bi), NT), 0.0)
         for qi, ki, bi, mi in zip(qs, ks, bs, bm)]
    o = [_dotb(qi * jnp.exp(bi), si, NT) + _dotb(ai, vi) for qi, bi, si, ai, vi in zip(qs, bs, sts, a, vs)]
    s2 = [si * jnp.exp(li) + _dotb(vi, ki * jnp.exp(li - bi), TN) for si, li, vi, ki, bi in zip(sts, bl, vs, ks, bs)]
    return jnp.concatenate(o, axis=1), jnp.concatenate(s2, axis=0)


def _fox_gate_fwd(z0, fbias, t, tc=256):
    n = z0.shape[0]
    nt = t // tc

    def body(zs_ref, b_ref, ccol_ref, crow_ref, carry):
        @pl.when(pl.program_id(1) == 0)
        def _():
            carry[...] = jnp.zeros_like(carry)

        ls = jnp.where(_iota2((tc, LANES), 1) < FOX_H, jax.nn.log_sigmoid(zs_ref[...] + b_ref[...]), 0.0)
        tri = (_iota2((tc, tc), 0) >= _iota2((tc, tc), 1)).astype(f32)
        c = _dot(tri, ls, prec=HI) + carry[...]
        carry[...] = c[tc - 1:tc, :]
        ccol_ref[...] = c
        crow_ref[0] = c.T[:FOX_H, :]

    return pl.pallas_call(
        body, name="fox_gate_fwd", grid=(n // t, nt),
        in_specs=[pl.BlockSpec((tc, LANES), lambda b, i: (b * nt + i, Z0_SMALL)), pl.BlockSpec((1, LANES), lambda b, i: (0, 0))],
        out_specs=[pl.BlockSpec((tc, LANES), lambda b, i: (b * nt + i, 0)), pl.BlockSpec((1, FOX_H, tc), lambda b, i: (b, 0, i))],
        out_shape=[S((n, LANES), f32), S((n // t, FOX_H, t), f32)],
        scratch_shapes=[pltpu.VMEM((1, LANES), f32)],
        compiler_params=pltpu.CompilerParams(dimension_semantics=("parallel", "arbitrary")),
    )(z0, fbias)


def _fox_gate_bwd(z0, fbias, dcq, dck, t, tc=256):
    n = z0.shape[0]
    nt = t // tc

    def body(zs_ref, b_ref, dcq_ref, dck_ref, dz_ref, db_ref, carry):
        first = jnp.logical_and(pl.program_id(0) == 0, pl.program_id(1) == 0)

        @pl.when(pl.program_id(1) == 0)
        def _():
            carry[...] = jnp.zeros_like(carry)

        @pl.when(first)
        def _():
            db_ref[...] = jnp.zeros_like(db_ref)

        dc = dcq_ref[0] + dcq_ref[1] + dcq_ref[2] + dcq_ref[3]
        drow = dck_ref[0, 0] + dck_ref[1, 0] + dck_ref[2, 0] + dck_ref[3, 0]
        eye = (_iota2((FOX_H, LANES), 0) == _iota2((FOX_H, LANES), 1)).astype(f32)
        dc = dc + _dot(drow, eye, TN, prec=HI)
        triu = (_iota2((tc, tc), 0) <= _iota2((tc, tc), 1)).astype(f32)
        dls = _dot(triu, dc, prec=HI) + carry[...]
        carry[...] = dls[0:1, :]
        x = zs_ref[...] + b_ref[...]
        dz = jnp.where(_iota2((tc, LANES), 1) < FOX_H, dls * jax.nn.sigmoid(-x), 0.0)
        dz_ref[...] = dz
        db_ref[...] += jnp.sum(dz, axis=0, keepdims=True)

    def rev(b, i):
        return b * nt + (nt - 1 - i)

    return pl.pallas_call(
        body, name="fox_gate_bwd", grid=(n // t, nt),
        in_specs=[pl.BlockSpec((tc, LANES), lambda b, i: (rev(b, i), Z0_SMALL)), pl.BlockSpec((1, LANES), lambda b, i: (0, 0)),
                  pl.BlockSpec((4, tc, LANES), lambda b, i: (0, rev(b, i), 0)),
                  pl.BlockSpec((4, 1, FOX_H, tc), lambda b, i: (0, b, 0, nt - 1 - i))],
        out_specs=[pl.BlockSpec((tc, LANES), lambda b, i: (rev(b, i), 0)), pl.BlockSpec((1, LANES), lambda b, i: (0, 0))],
        out_shape=[S((n, LANES), f32), S((1, LANES), f32)],
        scratch_shapes=[pltpu.VMEM((1, LANES), f32)],
        compiler_params=pltpu.CompilerParams(dimension_semantics=("arbitrary", "arbitrary")),
    )(z0, fbias, dcq, dck)


def _fox_scores(hh, p, i, tq, t, q, k, ccol, crow):
    lane = _iota2((1, LANES), 1)
    mh = (lane // FOX_D) == hh
    h = 2 * p + hh
    qh = jnp.where(mh, q, 0.0).astype(bf16)
    s = _dot(qh, k, NT) * (FOX_D ** -0.5)
    cq = jnp.sum(jnp.where(lane == h, ccol, 0.0), axis=1, keepdims=True)
    ck = jnp.sum(jnp.where(_iota2((FOX_H, 1), 0) == h, crow, 0.0), axis=0, keepdims=True)
    causal = _iota2((1, t), 1) <= (i * tq + _iota2((tq, 1), 0))
    s = jnp.where(causal, s + cq - ck, -jnp.inf)
    pe = jnp.exp(s - jnp.max(s, axis=1, keepdims=True))
    return mh, qh, pe, jnp.sum(pe, axis=1, keepdims=True)


def _fox_attn_fwd(qn, kn, z0, ccol, crow, t, tq=256):
    n = qn.shape[0]
    nq = t // tq

    def body(q_ref, k_ref, v_ref, ccol_ref, crow_ref, o_ref):
        p, i = pl.program_id(1), pl.program_id(2)
        q, k, v = q_ref[...], k_ref[...].astype(bf16), v_ref[...].astype(bf16)
        acc = jnp.zeros((tq, LANES), f32)
        for hh in range(2):
            mh, _, pe, l = _fox_scores(hh, p, i, tq, t, q, k, ccol_ref[...], crow_ref[0])
            acc = jnp.where(mh, _dot(pe.astype(bf16), v) / l, acc)
        o_ref[...] = acc

    return pl.pallas_call(
        body, name="fox_attn_fwd", grid=(n // t, FOX_H // 2, nq),
        in_specs=[pl.BlockSpec((tq, LANES), lambda b, p, i: (b * nq + i, p)), pl.BlockSpec((t, LANES), lambda b, p, i: (b, p)),
                  pl.BlockSpec((t, LANES), lambda b, p, i: (b, Z0_FV + p)),
                  pl.BlockSpec((tq, LANES), lambda b, p, i: (b * nq + i, 0)), pl.BlockSpec((1, FOX_H, t), lambda b, p, i: (b, 0, 0))],
        out_specs=pl.BlockSpec((tq, LANES), lambda b, p, i: (b * nq + i, p)),
        out_shape=S((n, FOX_W), f32),
        compiler_params=pltpu.CompilerParams(dimension_semantics=("parallel", "parallel", "parallel")),
    )(qn, kn, z0, ccol, crow)


def _fox_attn_bwd(qn, kn, z0, ccol, crow, do, t, tq=256):
    n = qn.shape[0]
    nq = t // tq
    nb = n // t

    def body(q_ref, k_ref, v_ref, ccol_ref, crow_ref, do_ref, dq_ref, dk_ref, dv_ref, dcq_ref, dck_ref):
        p, i = pl.program_id(1), pl.program_id(2)

        @pl.when(i == 0)
        def _():
            dk_ref[...] = jnp.zeros_like(dk_ref)
            dv_ref[...] = jnp.zeros_like(dv_ref)
            dck_ref[...] = jnp.zeros_like(dck_ref)

        q, kf, v, dout = q_ref[...], k_ref[...], v_ref[...].astype(bf16), do_ref[...]
        k = kf.astype(bf16)
        lane = _iota2((1, LANES), 1)
        sub = _iota2((FOX_H, 1), 0)
        scale = FOX_D ** -0.5
        dq = jnp.zeros((tq, LANES), f32)
        dcq = jnp.zeros((tq, LANES), f32)
        for hh in range(2):
            mh, qh, pe, l = _fox_scores(hh, p, i, tq, t, q, k, ccol_ref[...], crow_ref[0])
            pr = pe / l
            doh = jnp.where(mh, dout, 0.0).astype(bf16)
            dp = _dot(doh, v, NT)
            ds = pr * (dp - jnp.sum(pr * dp, axis=1, keepdims=True))
            dsb = ds.astype(bf16)
            dq = dq + _dot(dsb, jnp.where(mh, kf, 0.0).astype(bf16)) * scale
            dk_ref[...] += _dot(dsb, qh, TN) * scale
            dv_ref[...] += _dot(pr.astype(bf16), doh, TN)
            h = 2 * p + hh
            dcq = dcq + jnp.where(lane == h, jnp.sum(ds, axis=1, keepdims=True), 0.0)
            dck_ref[0, 0] += jnp.where(sub == h, -jnp.sum(ds, axis=0, keepdims=True), 0.0)
        dq_ref[...] = dq
        dcq_ref[0] = dcq

    return pl.pallas_call(
        body, name="fox_attn_bwd", grid=(nb, FOX_H // 2, nq),
        in_specs=[pl.BlockSpec((tq, LANES), lambda b, p, i: (b * nq + i, p)), pl.BlockSpec((t, LANES), lambda b, p, i: (b, p)),
                  pl.BlockSpec((t, LANES), lambda b, p, i: (b, Z0_FV + p)),
                  pl.BlockSpec((tq, LANES), lambda b, p, i: (b * nq + i, 0)), pl.BlockSpec((1, FOX_H, t), lambda b, p, i: (b, 0, 0)),
                  pl.BlockSpec((tq, LANES), lambda b, p, i: (b * nq + i, p))],
        out_specs=[pl.BlockSpec((tq, LANES), lambda b, p, i: (b * nq + i, p)), pl.BlockSpec((t, LANES), lambda b, p, i: (b, p)),
                   pl.BlockSpec((t, LANES), lambda b, p, i: (b, p)),
                   pl.BlockSpec((1, tq, LANES), lambda b, p, i: (p, b * nq + i, 0)),
                   pl.BlockSpec((1, 1, FOX_H, t), lambda b, p, i: (p, b, 0, 0))],
        out_shape=[S((n, FOX_W), f32), S((n, FOX_W), f32), S((n, FOX_W), f32), S((4, n, LANES), f32), S((4, nb, FOX_H, t), f32)],
        compiler_params=pltpu.CompilerParams(dimension_semantics=("parallel", "parallel", "arbitrary")),
    )(qn, kn, z0, ccol, crow, do)


def _loss_head(y, tgt, tm=256):
    n = y.shape[0]

    def body(y_ref, t_ref, dy_ref, l_ref, dyt_ref):
        @pl.when(pl.program_id(0) == 0)
        def _():
            l_ref[...] = jnp.zeros_like(l_ref)

        e = y_ref[...] - t_ref[...]
        dy = e * (1.0 / D)
        dy_ref[...] = dy
        dyt_ref[...] = dy.T.astype(bf16)
        l_ref[...] += jnp.sum(e * e, axis=0, keepdims=True)

    return pl.pallas_call(
        body, name="loss_head", grid=(n // tm,),
        in_specs=[pl.BlockSpec((tm, D), lambda i: (i, 0))] * 2,
        out_specs=[pl.BlockSpec((tm, D), lambda i: (i, 0)), pl.BlockSpec((1, D), lambda i: (0, 0)),
                   pl.BlockSpec((D, tm), lambda i: (0, i))],
        out_shape=[S((n, D), f32), S((1, D), f32), S((D, n), bf16)],
        compiler_params=pltpu.CompilerParams(dimension_semantics=("arbitrary",)),
    )(y, tgt)


def _adamw_math(w, g, m, v):
    m = ADAM_B1 * m + (1.0 - ADAM_B1) * g
    v = ADAM_B2 * v + (1.0 - ADAM_B2) * (g * g)
    m_hat = m / (1.0 - ADAM_B1 ** ADAM_STEP)
    v_hat = v / (1.0 - ADAM_B2 ** ADAM_STEP)
    return -ADAM_LR * (m_hat / (jnp.sqrt(v_hat) + ADAM_EPS) + ADAM_WD * w), m, v


def _adamw_big(name, idx, gmine, recv1, recv2, w, m, v):
    r, wc = w.shape
    c = gmine.shape[2]
    tr = min(r, 256)

    def body(idx_ref, gm_ref, r1_ref, r2_ref, w_ref, m_ref, v_ref, g_ref, d_ref, nm_ref, nv_ref):
        g = gm_ref[0].astype(f32) + r1_ref[0].astype(f32)
        for k in range(3):
            g = g + r2_ref[k].astype(f32)
        g = g[:, :wc]
        d, nm, nv = _adamw_math(w_ref[...], g, m_ref[...], v_ref[...])
        g_ref[...] = g
        d_ref[...] = d
        nm_ref[...] = nm
        nv_ref[...] = nv

    row = pl.BlockSpec((tr, wc), lambda i, s: (i, 0))
    return pl.pallas_call(
        body, name=name,
        grid_spec=pltpu.PrefetchScalarGridSpec(
            num_scalar_prefetch=1, grid=(r // tr,),
            in_specs=[pl.BlockSpec((1, tr, c), lambda i, s: (s[0], i, 0)), pl.BlockSpec((1, tr, c), lambda i, s: (s[1], i, 0)),
                      pl.BlockSpec((3, tr, c), lambda i, s: (0, i, 0)), row, row, row],
            out_specs=[row, row, row, row]),
        out_shape=[S((r, wc), f32)] * 4,
        compiler_params=pltpu.CompilerParams(dimension_semantics=("parallel",)),
    )(idx, gmine, recv1, recv2, w, m, v)


def _pair_sum(name, idx, gmine, recv1):
    _, r, c = gmine.shape
    g4 = gmine.reshape(4, 2, r, c)

    def body(idx_ref, gm_ref, r1_ref, o_ref):
        o_ref[0] = (gm_ref[0, 0].astype(f32) + r1_ref[0].astype(f32)).astype(bf16)

    return pl.pallas_call(
        body, name=name,
        grid_spec=pltpu.PrefetchScalarGridSpec(
            num_scalar_prefetch=1, grid=(4,),
            in_specs=[pl.BlockSpec((1, 1, r, c), lambda ch, s: (ch, s[0], 0, 0)), pl.BlockSpec((1, r, c), lambda ch, s: (ch, 0, 0))],
            out_specs=pl.BlockSpec((1, r, c), lambda ch, s: (ch, 0, 0))),
        out_shape=S((4, r, c), bf16),
        compiler_params=pltpu.CompilerParams(dimension_semantics=("parallel",)),
    )(idx, g4, recv1)


def _adamw_small(parts, w, m, v, own_mask, own_g):
    def body(p_ref, w_ref, m_ref, v_ref, mask_ref, og_ref, g_ref, d_ref, nm_ref, nv_ref):
        g = p_ref[0]
        for k in range(1, NDEV):
            g = g + p_ref[k]
        g_ref[...] = g
        ge = jnp.where(mask_ref[...] > 0.5, og_ref[...], g)
        d, nm, nv = _adamw_math(w_ref[...], ge, m_ref[...], v_ref[...])
        d_ref[...] = d
        nm_ref[...] = nm
        nv_ref[...] = nv

    return pl.pallas_call(body, name="adamw_small", out_shape=[S(w.shape, f32)] * 4)(parts, w, m, v, own_mask, own_g)


def _sum_parts(parts):
    def body(p_ref, g_ref):
        g = p_ref[0]
        for k in range(1, NDEV):
            g = g + p_ref[k]
        g_ref[...] = g

    return pl.pallas_call(body, name="sum_parts", out_shape=S(parts.shape[1:], f32))(parts)


def _me():
    return lax.axis_index("x"), lax.axis_index("y"), lax.axis_index("c")


def _hbm_specs(n):
    return [pl.BlockSpec(memory_space=pl.ANY)] * n


def _allgather(name, xs):
    na = len(xs)

    def body(*refs):
        x_refs, out_refs = refs[:na], refs[na:2 * na]
        send_sems, recv_sems, local_sems = refs[2 * na:]
        mx, my, mc = _me()
        me, sib = (mx, my, mc), (mx, my, 1 - mc)
        chips = [(1 - mx, my), (mx, 1 - my), (1 - mx, 1 - my)]

        def slab(a, px, py, pc):
            return out_refs[a].at[4 * px + 2 * py + pc]

        def copy(a, k, block, to, own=False):
            return pltpu.make_async_remote_copy(
                src_ref=x_refs[a] if own else slab(a, *block), dst_ref=slab(a, *block),
                send_sem=send_sems.at[7 * a + k], recv_sem=recv_sems.at[7 * a + k], device_id=to, device_id_type=MESH)

        mine = [pltpu.make_async_copy(x_refs[a], slab(a, *me), local_sems.at[a]) for a in range(na)]
        first = []
        for a in range(na):
            mine[a].start()
            first += [copy(a, 0, me, sib, own=True)] + [copy(a, 1 + j, me, (*chip, mc), own=True) for j, chip in enumerate(chips)]
        for cp in first:
            cp.start()
        passed = []
        for j, chip in enumerate(chips):
            for a in range(na):
                copy(a, 1 + j, (*chip, mc), me).wait_recv()
                passed.append(copy(a, 4 + j, (*chip, mc), sib))
                passed[-1].start()
        for a in range(na):
            copy(a, 0, sib, me).wait_recv()
            for j, chip in enumerate(chips):
                copy(a, 4 + j, (*chip, 1 - mc), me).wait_recv()
        for cp in first + passed:
            cp.wait_send()
        for cp in mine:
            cp.wait()

    return pl.pallas_call(
        body, name=name, out_shape=[S((NDEV,) + x.shape, x.dtype) for x in xs],
        in_specs=_hbm_specs(na), out_specs=_hbm_specs(na),
        scratch_shapes=[pltpu.SemaphoreType.DMA((7 * na,)), pltpu.SemaphoreType.DMA((7 * na,)), pltpu.SemaphoreType.DMA((na,))],
    )(*xs)


def _rs_sibling(gs):
    na = len(gs)

    def body(*refs):
        g_refs, out_refs, send_sems, recv_sems = refs[:na], refs[na:2 * na], refs[2 * na], refs[2 * na + 1]
        mx, my, mc = _me()
        cps = [pltpu.make_async_remote_copy(
            src_ref=g_refs[a].at[2 * ch + 1 - mc], dst_ref=out_refs[a].at[ch], send_sem=send_sems.at[4 * a + ch],
            recv_sem=recv_sems.at[4 * a + ch], device_id=(mx, my, 1 - mc), device_id_type=MESH)
            for a in range(na) for ch in range(4)]
        for cp in cps:
            cp.start()
        for cp in cps:
            cp.wait_recv()
        for cp in cps:
            cp.wait_send()

    return pl.pallas_call(
        body, name="rs_sibling", out_shape=[S((4,) + g.shape[1:], g.dtype) for g in gs],
        in_specs=_hbm_specs(na), out_specs=_hbm_specs(na),
        scratch_shapes=[pltpu.SemaphoreType.DMA((4 * na,)), pltpu.SemaphoreType.DMA((4 * na,))],
    )(*gs)


def _rs_chips(ps):
    na = len(ps)

    def body(*refs):
        p_refs, out_refs, send_sems, recv_sems = refs[:na], refs[na:2 * na], refs[2 * na], refs[2 * na + 1]
        mx, my, mc = _me()
        chips = [(1 - mx, my), (mx, 1 - my), (1 - mx, 1 - my)]
        cps = [pltpu.make_async_remote_copy(
            src_ref=p_refs[a].at[2 * cx + cy], dst_ref=out_refs[a].at[k], send_sem=send_sems.at[3 * a + k],
            recv_sem=recv_sems.at[3 * a + k], device_id=(cx, cy, mc), device_id_type=MESH)
            for a in range(na) for k, (cx, cy) in enumerate(chips)]
        for cp in cps:
            cp.start()
        for cp in cps:
            cp.wait_recv()
        for cp in cps:
            cp.wait_send()

    return pl.pallas_call(
        body, name="rs_chips", out_shape=[S((3,) + p.shape[1:], p.dtype) for p in ps],
        in_specs=_hbm_specs(na), out_specs=_hbm_specs(na),
        scratch_shapes=[pltpu.SemaphoreType.DMA((3 * na,)), pltpu.SemaphoreType.DMA((3 * na,))],
    )(*ps)


_COLW = 512
_COL_NAMES = ("l0_w_in", "l0_w_ff1", "l1_w_in", "l1_w_ff1")
_ROW_NAMES = ("l0_w_out", "l0_w_ff2", "l1_w_out", "l1_w_ff2")
_BIG_NAMES = _COL_NAMES + _ROW_NAMES


def _full_weight(gathered, name, ncols):
    if name in _ROW_NAMES:
        return gathered.reshape(-1, D)
    return gathered[:, :, :ncols].transpose(1, 0, 2).reshape(D, NDEV * ncols)


def _regroup_w_in0(w):
    main = jnp.concatenate([w[:, 0:1536], w[:, 1544:3080], w[:, 3088:3600]], axis=1)
    small = jnp.concatenate([w[:, 1536:1544], w[:, 3080:3088]], axis=1)
    return jnp.concatenate([main, small, jnp.zeros((D, ZW0 - 3584 - 16), w.dtype)], axis=1)


def _ungroup_w_in0(g):
    return jnp.concatenate([g[:, 0:1536], g[:, 3584:3592], g[:, 1536:3072], g[:, 3592:3600], g[:, 3072:3584]], axis=1)


def _col_slabs(g, ncols):
    g = g.reshape(D, NDEV, ncols).transpose(1, 0, 2)
    return jnp.pad(g, ((0, 0), (0, 0), (0, _COLW - ncols)))


def _sq(t):
    return t * t


def _mlp_fwd(tag, x, gain, w1, w2):
    h, ht = _tok_fwd(f"{tag}_ffn_norm", _f_norm, [(x, D, 0)], [(gain, None, 0)], [(D, bf16)], 256, also_t=(0,))
    a = _mm(f"{tag}_ff1", h, w1, NN, bf16, epi=lambda acc: jnp.maximum(acc, 0.0))
    y = _mm(f"{tag}_ff2", a, w2, NN, f32, a_fn=_sq, epi=lambda acc, r: acc + r, extra=(x,))
    return y, (h, ht, a)


def _mlp_bwd(tag, x, gain, w1, w2, saved, dy, dyt, want_t):
    h, ht, a = saved
    da = _mm(f"{tag}_ff2_dx", dy, w2, NT, bf16, epi=lambda acc, av: acc * 2.0 * av.astype(f32), extra=(a,))
    dw2 = _mm(f"{tag}_ff2_dw", dyt, a, NN, bf16, b_fn=_sq, out_t=True)
    dh = _mm(f"{tag}_ff1_dx", da, w1, NT, f32)
    dw1 = _mm(f"{tag}_ff1_dw", ht, da, NN, bf16, slab=True)
    res = _tok_bwd(f"{tag}_ffn_norm_bwd", _f_norm, [(x, D, 0)], [(gain, None, 0)], [(dh, D)], 256, [0], [0], addto={0: dy},
                   also_t=(0,) if want_t else ())
    return res, dw1, dw2


def _row(v):
    return v.reshape(1, -1).astype(f32)


def _local_step(x, tgt, p, t):
    n = x.shape[0]
    nchunk = t // CH
    g = {}

    li = jnp.arange(FOX_W)
    pm = jnp.where((li[:, None] // FOX_D) == (li[None, :] // FOX_D), 1.0 / FOX_D, 0.0).astype(f32)
    lane_head = jnp.arange(GDN_W) // HD
    sel = lambda first: (jnp.arange(LANES)[:, None] == (first + lane_head)[None, :]).astype(f32)
    e_beta, e_alpha = sel(FOX_H), sel(FOX_H + GDN_H)
    alog_b, dt_b = _row(jnp.repeat(p["l0_gdn_A_log"], HD)), _row(jnp.repeat(p["l0_gdn_dt_bias"], HD))
    gq_t, gk_t = _row(jnp.tile(p["l0_fox_q_norm"], FOX_H)), _row(jnp.tile(p["l0_fox_k_norm"], FOX_H))
    on0_t, on1_t = _row(jnp.tile(p["l0_gdn_o_norm"], GDN_H)), _row(jnp.tile(p["l1_hgrn_o_norm"], HG_H))
    fbias = jnp.pad(_row(p["l0_fox_f_bias"]), ((0, 0), (0, LANES - FOX_H)))
    g0m, g0f, g1m, g1f = (_row(p[k]) for k in ("l0_mix_norm", "l0_ffn_norm", "l1_mix_norm", "l1_ffn_norm"))
    wconv = p["l0_gdn_conv"].astype(f32)
    lbl = p["hgrn_lb_logits"].astype(f32)

    h0, h0t = _tok_fwd("l0_mix_norm", _f_norm, [(x, D, 0)], [(g0m, None, 0)], [(D, bf16)], 256, also_t=(0,))
    z0 = _mm("l0_in", h0, p["l0_w_in"], NN, f32)
    qk_rows = [(z0, 2 * FOX_W, 0)]
    qk_consts = [(gq_t, None, 0), (gk_t, None, 0), (pm, None, 0)]
    qn, kn = _tok_fwd("fox_pre", _f_foxpre, qk_rows, qk_consts, [(FOX_W, f32)] * 2, 256)
    ccol, crow = _fox_gate_fwd(z0, fbias, t)
    fox_o = _fox_attn_fwd(qn, kn, z0, ccol, crow, t)
    conv_rows, conv_consts = [(z0, LANES, Z0_GQKV)], [(wconv, LANES, 0)]
    (qkv,) = _tok_fwd("gdn_conv", _f_conv, conv_rows, conv_consts, [(LANES, f32)], t, ncb=12, with_j=True)
    gate_rows = [(z0, LANES, Z0_SMALL)]
    gate_consts = [(e_beta, None, 0), (e_alpha, None, 0), (alog_b, None, 0), (dt_b, None, 0)]
    beta_b, g_b = _tok_fwd("gdn_gate", _f_gdngate, gate_rows, gate_consts, [(GDN_W, f32)] * 2, 256)
    intra_rows = [(qkv, GDN_W, 0), (qkv, GDN_W, 1), (qkv, GDN_W, 2), (beta_b, GDN_W, 0), (g_b, GDN_W, 0)]
    intra = _tok_fwd("gdn_intra", _f_gdn_intra, intra_rows, [], [(GDN_W, f32)] * 6, 2 * CH)
    inter_rows = [(a, 0) for a in intra]
    gdn_o, gdn_hist = _scan_fwd("gdn_scan", _f_gdn_inter, inter_rows, GDN_H, nchunk)
    post0_rows, post0_consts = [(fox_o, FOX_W, 0), (gdn_o, GDN_W, 0), (z0, GDN_W, Z0_GG // 4)], [(on0_t, None, 0)]
    cat0, cat0t = _tok_fwd("l0_post", _f_post0, post0_rows, post0_consts, [(D, bf16)], 256, also_t=(0,))
    x1 = _mm("l0_out", cat0, p["l0_w_out"], NN, f32, epi=lambda acc, r: acc + r, extra=(x,))
    x2, mlp0 = _mlp_fwd("l0", x1, g0f, p["l0_w_ff1"], p["l0_w_ff2"])

    h1, h1t = _tok_fwd("l1_mix_norm", _f_norm, [(x2, D, 0)], [(g1m, None, 0)], [(D, bf16)], 256, also_t=(0,))
    z1 = _mm("l1_in", h1, p["l1_w_in"], NN, f32)
    hpre_rows, hpre_consts = [(z1, 2 * D, 0)], [(lbl, None, 0)]
    hq, hk, hb = _tok_fwd("hgrn_pre", _f_hpre, hpre_rows, hpre_consts, [(D, f32)] * 3, 256)
    hg_rows = [(hq, 0), (hk, 0), (hb, 0), (z1, 2)]
    hg_o, hg_hist = _scan_fwd("hgrn_scan", _f_hgrn_chunk, hg_rows, HG_H, nchunk)
    post1_rows, post1_consts = [(hg_o, D, 0), (z1, D, 3)], [(on1_t, None, 0)]
    cat1, cat1t = _tok_fwd("l1_post", _f_post1, post1_rows, post1_consts, [(D, bf16)], 256, also_t=(0,))
    x3 = _mm("l1_out", cat1, p["l1_w_out"], NN, f32, epi=lambda acc, r: acc + r, extra=(x2,))
    y, mlp1 = _mlp_fwd("l1", x3, g1f, p["l1_w_ff1"], p["l1_w_ff2"])

    dy, loss_row, dyt = _loss_head(y, tgt)

    (dx3, g["l1_ffn_norm"]), g["l1_w_ff1"], g["l1_w_ff2"] = _mlp_bwd("l1", x3, g1f, p["l1_w_ff1"], p["l1_w_ff2"], mlp1, dy, dyt, False)
    dcat1 = _mm("l1_out_dx", dx3, p["l1_w_out"], NT, f32)
    g["l1_w_out"] = _mm("l1_out_dw", cat1t, dx3, NN, bf16)
    dhg_o, dzg, don1 = _tok_bwd("l1_post_bwd", _f_post1, post1_rows, post1_consts, [(dcat1, D)], 256, [0, 1], [0])
    dhq, dhk, dhb, dzi = _scan_bwd("hgrn_scan_bwd", _f_hgrn_chunk, hg_rows, hg_hist, dhg_o, HG_H, nchunk)
    dzqf, dlbl = _tok_bwd("hgrn_pre_bwd", _f_hpre, hpre_rows, hpre_consts, [(dhq, D), (dhk, D), (dhb, D)], 256, [0], [0])
    dz1 = jnp.concatenate([dzqf, dzi, dzg], axis=1).astype(bf16)
    dh1 = _mm("l1_in_dx", dz1, p["l1_w_in"], NT, f32)
    g["l1_w_in"] = _mm("l1_in_dw", h1t, dz1, NN, bf16, slab=True)
    dx2, g["l1_mix_norm"], dx2t = _tok_bwd("l1_mix_norm_bwd", _f_norm, [(x2, D, 0)], [(g1m, None, 0)], [(dh1, D)], 256, [0], [0],
                                           addto={0: dx3}, also_t=(0,))
    g["l1_hgrn_o_norm"] = don1.reshape(HG_H, HD).sum(0)
    g["hgrn_lb_logits"] = dlbl

    (dx1, g["l0_ffn_norm"]), g["l0_w_ff1"], g["l0_w_ff2"] = _mlp_bwd("l0", x1, g0f, p["l0_w_ff1"], p["l0_w_ff2"], mlp0, dx2, dx2t, False)
    dcat0 = _mm("l0_out_dx", dx1, p["l0_w_out"], NT, f32)
    g["l0_w_out"] = _mm("l0_out_dw", cat0t, dx1, NN, bf16)
    dfox_o, dgdn_o, dgg, don0 = _tok_bwd("l0_post_bwd", _f_post0, post0_rows, post0_consts, [(dcat0, D)], 256, [0, 1, 2], [0])
    dinter = _scan_bwd("gdn_scan_bwd", _f_gdn_inter, inter_rows, gdn_hist, dgdn_o, GDN_H, nchunk)
    dq_g, dk_g, dv_g, dbeta_b, dg_b = _tok_bwd("gdn_intra_bwd", _f_gdn_intra, intra_rows, [], [(a, GDN_W) for a in dinter], 2 * CH, [0, 1, 2, 3, 4], [])
    dqkv = jnp.concatenate([dq_g, dk_g, dv_g], axis=1)
    dzs_g, dalog_b, ddt_b = _tok_bwd("gdn_gate_bwd", _f_gdngate, gate_rows, gate_consts, [(dbeta_b, GDN_W), (dg_b, GDN_W)], 256, [0], [2, 3])
    dgqkv, dwconv = _tok_bwd("gdn_conv_bwd", _f_conv, conv_rows, conv_consts, [(dqkv, LANES)], t, [0], [0], ncb=12, with_j=True)
    dqn, dkn, dfv, dcq, dck = _fox_attn_bwd(qn, kn, z0, ccol, crow, dfox_o, t)
    dzs_f, dfb = _fox_gate_bwd(z0, fbias, dcq, dck, t)
    dzqk, dgq_t, dgk_t = _tok_bwd("fox_pre_bwd", _f_foxpre, qk_rows, qk_consts, [(dqn, FOX_W), (dkn, FOX_W)], 256, [0], [0, 1])
    dz0 = jnp.concatenate([dzqk, dfv, dgqkv, dgg, dzs_g + dzs_f, jnp.zeros((n, ZW0 - 3712), f32)], axis=1).astype(bf16)
    dh0 = _mm("l0_in_dx", dz0, p["l0_w_in"], NT, f32)
    g["l0_w_in"] = _mm("l0_in_dw", h0t, dz0, NN, bf16)
    dx, g["l0_mix_norm"] = _tok_bwd("l0_mix_norm_bwd", _f_norm, [(x, D, 0)], [(g0m, None, 0)], [(dh0, D)], 256, [0], [0], addto={0: dx1})
    g["l0_fox_q_norm"] = dgq_t.reshape(FOX_H, FOX_D).sum(0)
    g["l0_fox_k_norm"] = dgk_t.reshape(FOX_H, FOX_D).sum(0)
    g["l0_fox_f_bias"] = dfb[0, :FOX_H]
    g["l0_gdn_conv"] = dwconv
    g["l0_gdn_A_log"] = dalog_b.reshape(GDN_H, HD).sum(1)
    g["l0_gdn_dt_bias"] = ddt_b.reshape(GDN_H, HD).sum(1)
    g["l0_gdn_o_norm"] = don0.reshape(GDN_H, HD).sum(0)
    return loss_row, dx, g


_NAMES = ("l0_mix_norm", "l0_w_in", "l0_fox_q_norm", "l0_fox_k_norm", "l0_fox_f_bias", "l0_gdn_conv", "l0_gdn_A_log",
          "l0_gdn_dt_bias", "l0_gdn_o_norm", "l0_w_out", "l0_ffn_norm", "l0_w_ff1", "l0_w_ff2", "l1_mix_norm", "l1_w_in",
          "l1_hgrn_o_norm", "l1_w_out", "l1_ffn_norm", "l1_w_ff1", "l1_w_ff2", "hgrn_lb_logits")
_SMALL_NAMES = tuple(nm for nm in _NAMES if nm not in _BIG_NAMES)
_SMALL_ROWS = 16


def _pack_small(vals):
    flat = jnp.concatenate([vals[nm].reshape(-1).astype(f32) for nm in _SMALL_NAMES])
    return jnp.pad(flat, (0, _SMALL_ROWS * D - flat.shape[0])).reshape(_SMALL_ROWS, D)


def _unpack_small(packed, shapes):
    flat = packed.reshape(-1)
    out, off = {}, 0
    for nm in _SMALL_NAMES:
        size = 1
        for s in shapes[nm]:
            size *= s
        out[nm] = flat[off:off + size].reshape(shapes[nm])
        off += size
    return out, off


def kernel(x, l0_mix_norm, l0_w_in, l0_fox_q_norm, l0_fox_k_norm, l0_fox_f_bias, l0_gdn_conv, l0_gdn_A_log, l0_gdn_dt_bias, l0_gdn_o_norm, l0_w_out, l0_ffn_norm, l0_w_ff1, l0_w_ff2, l1_mix_norm, l1_w_in, l1_hgrn_o_norm, l1_w_out, l1_ffn_norm, l1_w_ff1, l1_w_ff2, hgrn_lb_logits, loss_target, m_l0_mix_norm, m_l0_w_in, m_l0_fox_q_norm, m_l0_fox_k_norm, m_l0_fox_f_bias, m_l0_gdn_conv, m_l0_gdn_A_log, m_l0_gdn_dt_bias, m_l0_gdn_o_norm, m_l0_w_out, m_l0_ffn_norm, m_l0_w_ff1, m_l0_w_ff2, m_l1_mix_norm, m_l1_w_in, m_l1_hgrn_o_norm, m_l1_w_out, m_l1_ffn_norm, m_l1_w_ff1, m_l1_w_ff2, m_hgrn_lb_logits, v_l0_mix_norm, v_l0_w_in, v_l0_fox_q_norm, v_l0_fox_k_norm, v_l0_fox_f_bias, v_l0_gdn_conv, v_l0_gdn_A_log, v_l0_gdn_dt_bias, v_l0_gdn_o_norm, v_l0_w_out, v_l0_ffn_norm, v_l0_w_ff1, v_l0_w_ff2, v_l1_mix_norm, v_l1_w_in, v_l1_hgrn_o_norm, v_l1_w_out, v_l1_ffn_norm, v_l1_w_ff1, v_l1_w_ff2, v_hgrn_lb_logits):
    args = dict(zip(_NAMES, (l0_mix_norm, l0_w_in, l0_fox_q_norm, l0_fox_k_norm, l0_fox_f_bias, l0_gdn_conv, l0_gdn_A_log, l0_gdn_dt_bias, l0_gdn_o_norm, l0_w_out, l0_ffn_norm, l0_w_ff1, l0_w_ff2, l1_mix_norm, l1_w_in, l1_hgrn_o_norm, l1_w_out, l1_ffn_norm, l1_w_ff1, l1_w_ff2, hgrn_lb_logits)))
    mom = dict(zip(_NAMES, (m_l0_mix_norm, m_l0_w_in, m_l0_fox_q_norm, m_l0_fox_k_norm, m_l0_fox_f_bias, m_l0_gdn_conv, m_l0_gdn_A_log, m_l0_gdn_dt_bias, m_l0_gdn_o_norm, m_l0_w_out, m_l0_ffn_norm, m_l0_w_ff1, m_l0_w_ff2, m_l1_mix_norm, m_l1_w_in, m_l1_hgrn_o_norm, m_l1_w_out, m_l1_ffn_norm, m_l1_w_ff1, m_l1_w_ff2, m_hgrn_lb_logits)))
    var = dict(zip(_NAMES, (v_l0_mix_norm, v_l0_w_in, v_l0_fox_q_norm, v_l0_fox_k_norm, v_l0_fox_f_bias, v_l0_gdn_conv, v_l0_gdn_A_log, v_l0_gdn_dt_bias, v_l0_gdn_o_norm, v_l0_w_out, v_l0_ffn_norm, v_l0_w_ff1, v_l0_w_ff2, v_l1_mix_norm, v_l1_w_in, v_l1_hgrn_o_norm, v_l1_w_out, v_l1_ffn_norm, v_l1_w_ff1, v_l1_w_ff2, v_hgrn_lb_logits)))
    nb, t, _ = x.shape
    mx, my, mc = _me()
    dev = 4 * mx + 2 * my + mc
    chip = 2 * mx + my
    ncols = {nm: args[nm].shape[1] for nm in _COL_NAMES}
    conv_cols = l0_gdn_conv.shape[1]

    send = [jnp.pad(args[nm].astype(bf16), ((0, 0), (0, _COLW - ncols[nm]))) for nm in _COL_NAMES]
    send += [args[nm].astype(bf16) for nm in _ROW_NAMES]
    send.append(jnp.pad(l0_gdn_conv, ((0, 4), (0, LANES * 2 - conv_cols))))
    gathered = _allgather("ag_weights", send)
    p = {nm: args[nm] for nm in _SMALL_NAMES}
    p["l0_gdn_conv"] = gathered[-1][:, :4, :conv_cols].transpose(1, 0, 2).reshape(4, NDEV * conv_cols)
    for nm, arr in zip(_BIG_NAMES, gathered):
        p[nm] = _full_weight(arr, nm, ncols.get(nm, 0))
    p["l0_w_in"] = _regroup_w_in0(p["l0_w_in"])

    loss_row, dx, g = _local_step(x.reshape(nb * t, D), loss_target.reshape(nb * t, D), p, t)

    g["l0_w_in"] = _col_slabs(_ungroup_w_in0(g["l0_w_in"]), ncols["l0_w_in"])
    gs = [g[nm] if nm in _COL_NAMES else g[nm].reshape(NDEV, -1, D) for nm in _BIG_NAMES]
    recv1 = _rs_sibling(gs)
    core_idx = jnp.reshape(mc, (1,)).astype(jnp.int32)
    pairs = [_pair_sum(f"rs_pair_sum_{nm}", core_idx, ga, r1) for nm, ga, r1 in zip(_BIG_NAMES, gs, recv1)]
    recv2 = _rs_chips(pairs)
    own_idx = jnp.stack([dev, chip]).astype(jnp.int32)
    big = {nm: _adamw_big(f"adamw_{nm}", own_idx, ga, r1, r2, args[nm], mom[nm], var[nm])
           for nm, ga, r1, r2 in zip(_BIG_NAMES, gs, recv1, recv2)}

    shapes = {nm: args[nm].shape for nm in _SMALL_NAMES}
    gsm = dict(g)
    gsm["l0_gdn_conv"] = jnp.zeros(shapes["l0_gdn_conv"], f32)
    packed = _pack_small(gsm)
    _, used = _unpack_small(packed, shapes)
    flat_extra = jnp.concatenate([jnp.sum(loss_row).reshape(1), g["l0_gdn_conv"].reshape(-1)])
    packed = packed.reshape(-1).at[used:used + flat_extra.shape[0]].set(flat_extra).reshape(_SMALL_ROWS, D)
    (parts,) = _allgather("ag_small", [packed])
    total = _sum_parts(parts).reshape(-1)
    loss = 0.5 * total[used] / D
    conv_g_full = total[used + 1:used + 1 + 4 * NDEV * conv_cols].reshape(4, NDEV * conv_cols)
    conv_g = lax.dynamic_slice(conv_g_full, (0, dev * conv_cols), (4, conv_cols))
    own_vals = {nm: jnp.zeros(shapes[nm], f32) for nm in _SMALL_NAMES}
    own_vals["l0_gdn_conv"] = conv_g
    own_mask = {nm: jnp.zeros(shapes[nm], f32) for nm in _SMALL_NAMES}
    own_mask["l0_gdn_conv"] = jnp.ones(shapes["l0_gdn_conv"], f32)
    small = _adamw_small(parts, _pack_small(args), _pack_small(mom), _pack_small(var), _pack_small(own_mask), _pack_small(own_vals))
    small = [_unpack_small(a, shapes)[0] for a in small]
    small[0]["l0_gdn_conv"] = conv_g

    outs = [loss, dx.reshape(nb, t, D)]
    for k in range(4):
        outs += [big[nm][k] if nm in _BIG_NAMES else small[k][nm] for nm in _NAMES]
    return tuple(outs)
```

```python
import functools

import jax
import jax.numpy as jnp
from jax import lax
from jax.experimental import pallas as pl
from jax.experimental.pallas import tpu as pltpu

f32, bf16 = jnp.float32, jnp.bfloat16
NN = (((1,), (0,)), ((), ()))
NT = (((1,), (1,)), ((), ()))
TN = (((0,), (0,)), ((), ()))
HI = lax.Precision.HIGHEST
MESH = pl.DeviceIdType.MESH
S = jax.ShapeDtypeStruct

EPS = 1e-6
D = 1024
LANES = 128
FOX_H, FOX_D, FOX_W = 8, 64, 512
GDN_H, HD, GDN_W = 4, 128, 512
HG_H = 8
CH = 64
ZW0 = 3840
NDEV = 8
ADAM_LR, ADAM_B1, ADAM_B2, ADAM_EPS, ADAM_WD, ADAM_STEP = 0.001, 0.9, 0.999, 1e-08, 0.01, 10

Z0_FQK, Z0_FV, Z0_GQKV, Z0_GG, Z0_SMALL = 0, 8, 12, 24, 28


def _dot(a, b, dims=NN, prec=None):
    return lax.dot_general(a, b, dims, precision=prec, preferred_element_type=f32)


def _iota2(shape, axis):
    return lax.broadcasted_iota(jnp.int32, shape, axis)


class _Side:
    def __init__(self, ins, out_shapes, scratch, start, finish):
        self.ins, self.out_shapes, self.scratch, self.start, self.finish = list(ins), list(out_shapes), list(scratch), start, finish


def _join_sides(sides):
    def split(refs, counts):
        out, off = [], 0
        for c in counts:
            out.append(refs[off:off + c])
            off += c
        return out

    ni, no, ns = ([len(getattr(sd, a)) for sd in sides] for a in ("ins", "out_shapes", "scratch"))

    def run(which):
        def go(ins, outs, sems):
            for sd, i, o, c in zip(sides, split(ins, ni), split(outs, no), split(sems, ns)):
                getattr(sd, which)(i, o, c)
        return go

    return _Side(sum((sd.ins for sd in sides), []), sum((sd.out_shapes for sd in sides), []),
                 sum((sd.scratch for sd in sides), []), run("start"), run("finish"))


def _pcall(side, body, *, name, grid, in_specs, out_specs, out_shape, scratch_shapes=(), compiler_params=None):
    if side is None:
        return pl.pallas_call(body, name=name, grid=grid, in_specs=in_specs, out_specs=out_specs, out_shape=out_shape,
                              scratch_shapes=scratch_shapes, compiler_params=compiler_params)
    single = not isinstance(out_shape, (list, tuple))
    ospecs, oshape = ([out_specs], [out_shape]) if single else (list(out_specs), list(out_shape))
    nin, nout, nscr = len(in_specs), len(ospecs), len(scratch_shapes)
    si, so = len(side.ins), len(side.out_shapes)

    def wrapped(*refs):
        o0 = nin + si
        c0 = o0 + nout + so
        sins, souts, ssems = refs[nin:o0], refs[o0 + nout:c0], refs[c0 + nscr:]
        ids = [pl.program_id(a) for a in range(len(grid))]
        first = functools.reduce(jnp.logical_and, [i == 0 for i in ids])
        last = functools.reduce(jnp.logical_and, [i == g - 1 for i, g in zip(ids, grid)])

        @pl.when(first)
        def _():
            side.start(sins, souts, ssems)

        body(*refs[:nin], *refs[o0:o0 + nout], *refs[c0:c0 + nscr])

        @pl.when(last)
        def _():
            side.finish(sins, souts, ssems)

    call = pl.pallas_call(
        wrapped, name=name, grid=grid, in_specs=list(in_specs) + _hbm_specs(si), out_specs=ospecs + _hbm_specs(so),
        out_shape=oshape + side.out_shapes, scratch_shapes=list(scratch_shapes) + side.scratch,
        compiler_params=pltpu.CompilerParams(dimension_semantics=("arbitrary",) * len(grid),
                                             vmem_limit_bytes=getattr(compiler_params, "vmem_limit_bytes", None)))

    def run(*args):
        res = call(*args, *side.ins)
        return (res[0] if single else list(res[:nout])), list(res[nout:])

    return run


def _tok_specs(rows, consts, tm):
    specs = []
    for (_, w, base) in rows:
        specs.append(pl.BlockSpec((tm, w), functools.partial(lambda j, i, b: (i, b + j), b=base)))
    for (arr, w, base) in consts:
        if w is None:
            specs.append(pl.BlockSpec(arr.shape, lambda j, i: (0, 0)))
        else:
            specs.append(pl.BlockSpec((arr.shape[0], w), functools.partial(lambda j, i, b: (0, b + j), b=base)))
    return specs


def _tok_fwd(name, f, rows, consts, outs, tm, ncb=1, with_j=False, also_t=(), side=None):
    n = rows[0][0].shape[0]
    nin = len(rows) + len(consts)
    nout = len(outs)

    def body(*refs):
        ins = [r[...] for r in refs[:nin]]
        vals = f(pl.program_id(0), *ins) if with_j else f(*ins)
        for r, v in zip(refs[nin:nin + nout], vals):
            r[...] = v.astype(r.dtype)
        for r, k in zip(refs[nin + nout:], also_t):
            r[...] = vals[k].T.astype(r.dtype)

    return _pcall(
        side, body, name=name, grid=(ncb, n // tm),
        in_specs=_tok_specs(rows, consts, tm),
        out_specs=[pl.BlockSpec((tm, w), lambda j, i: (i, j)) for (w, _) in outs]
        + [pl.BlockSpec((outs[k][0], tm), lambda j, i: (0, i)) for k in also_t],
        out_shape=[S((n, w * ncb), dt) for (w, dt) in outs] + [S((outs[k][0], n), bf16) for k in also_t],
        compiler_params=pltpu.CompilerParams(dimension_semantics=("parallel", "parallel")),
    )(*[r[0] for r in rows], *[c[0] for c in consts])


def _tok_bwd(name, f, rows, consts, cots, tm, drow, dconst, ncb=1, with_j=False, addto=None, also_t=(), drow_dtype=f32, side=None):
    n = rows[0][0].shape[0]
    nr, nc, nct = len(rows), len(consts), len(cots)
    addto = addto or {}
    add_keys = sorted(addto)
    nadd = len(add_keys)

    def body(*refs):
        ins = [r[...] for r in refs[:nr + nc]]
        cot = [r[...] for r in refs[nr + nc:nr + nc + nct]]
        adds = refs[nr + nc + nct:nr + nc + nct + nadd]
        outs = refs[nr + nc + nct + nadd:]
        pos = list(drow) + [nr + k for k in dconst]

        def g(*dargs):
            full = list(ins)
            for p, a in zip(pos, dargs):
                full[p] = a
            return tuple(f(pl.program_id(0), *full) if with_j else f(*full))

        vals, vjp = jax.vjp(g, *[ins[p] for p in pos])
        grads = vjp(tuple(c.astype(v.dtype) for c, v in zip(cot, vals)))
        for k in range(len(drow)):
            gk = grads[k]
            if k in addto:
                gk = gk + adds[add_keys.index(k)][...]
            outs[k][...] = gk.astype(outs[k].dtype)
            if k in also_t:
                tref = outs[len(drow) + len(dconst) + list(also_t).index(k)]
                tref[...] = gk.T.astype(tref.dtype)
        first = pl.program_id(1) == 0
        for k in range(len(dconst)):
            ref = outs[len(drow) + k]

            @pl.when(first)
            def _():
                ref[...] = jnp.zeros_like(ref)

            ref[...] += grads[len(drow) + k]

    in_specs = _tok_specs(rows, consts, tm)
    in_specs += [pl.BlockSpec((tm, w), lambda j, i: (i, j)) for (_, w) in cots]
    in_specs += [pl.BlockSpec((tm, rows[drow[k]][1]), lambda j, i: (i, j)) for k in add_keys]
    out_specs = [pl.BlockSpec((tm, rows[k][1]), lambda j, i: (i, j)) for k in drow]
    out_shape = [S((n, rows[k][1] * ncb), drow_dtype) for k in drow]
    for k in dconst:
        arr, w, _ = consts[k]
        if w is None:
            out_specs.append(pl.BlockSpec(arr.shape, lambda j, i: (0, 0)))
            out_shape.append(S(arr.shape, f32))
        else:
            out_specs.append(pl.BlockSpec((arr.shape[0], w), lambda j, i: (0, j)))
            out_shape.append(S((arr.shape[0], w * ncb), f32))
    for k in also_t:
        out_specs.append(pl.BlockSpec((rows[drow[k]][1], tm), lambda j, i: (0, i)))
        out_shape.append(S((rows[drow[k]][1], n), bf16))
    return _pcall(
        side, body, name=name, grid=(ncb, n // tm), in_specs=in_specs, out_specs=out_specs, out_shape=out_shape,
        compiler_params=pltpu.CompilerParams(dimension_semantics=("parallel", "arbitrary")),
    )(*[r[0] for r in rows], *[c[0] for c in consts], *[c[0] for c in cots], *[addto[k] for k in add_keys])


def _scan_fwd(name, f, rows, nh, nchunk, side=None):
    n = rows[0][0].shape[0]
    nb = n // (CH * nchunk)
    nin = len(rows)
    w = nh * HD

    def body(*refs):
        o_ref, hist_ref, st = refs[nin], refs[nin + 1], refs[nin + 2]

        @pl.when(pl.program_id(1) == 0)
        def _():
            st[...] = jnp.zeros_like(st)

        s0 = st[...]
        hist_ref[...] = s0
        o, s1 = f(*[r[...] for r in refs[:nin]], s0)
        o_ref[...] = o
        st[...] = s1

    def rmap(b, c, base):
        return (b * nchunk + c, base)

    return _pcall(
        side, body, name=name, grid=(nb, nchunk),
        in_specs=[pl.BlockSpec((CH, w), functools.partial(rmap, base=b)) for (_, b) in rows],
        out_specs=[pl.BlockSpec((CH, w), functools.partial(rmap, base=0)),
                   pl.BlockSpec((w, HD), functools.partial(rmap, base=0))],
        out_shape=[S((n, w), f32), S((nb * nchunk * w, HD), f32)],
        scratch_shapes=[pltpu.VMEM((w, HD), f32)],
        compiler_params=pltpu.CompilerParams(dimension_semantics=("parallel", "arbitrary")),
    )(*[r[0] for r in rows])


def _scan_bwd(name, f, rows, hist, do, nh, nchunk, side=None):
    n = rows[0][0].shape[0]
    nb = n // (CH * nchunk)
    nin = len(rows)
    w = nh * HD

    def body(*refs):
        hist_ref, do_ref = refs[nin], refs[nin + 1]
        outs = refs[nin + 2:nin + 2 + nin]
        ds = refs[nin + 2 + nin]

        @pl.when(pl.program_id(1) == 0)
        def _():
            ds[...] = jnp.zeros_like(ds)

        _, vjp = jax.vjp(f, *[r[...] for r in refs[:nin]], hist_ref[...])
        grads = vjp((do_ref[...], ds[...]))
        for r, gk in zip(outs, grads[:nin]):
            r[...] = gk
        ds[...] = grads[nin]

    def rmap(b, c, base):
        return (b * nchunk + (nchunk - 1 - c), base)

    return _pcall(
        side, body, name=name, grid=(nb, nchunk),
        in_specs=[pl.BlockSpec((CH, w), functools.partial(rmap, base=b)) for (_, b) in rows]
        + [pl.BlockSpec((w, HD), functools.partial(rmap, base=0)), pl.BlockSpec((CH, w), functools.partial(rmap, base=0))],
        out_specs=[pl.BlockSpec((CH, w), functools.partial(rmap, base=0)) for _ in rows],
        out_shape=[S((n, w), f32) for _ in rows],
        scratch_shapes=[pltpu.VMEM((w, HD), f32)],
        compiler_params=pltpu.CompilerParams(dimension_semantics=("parallel", "arbitrary")),
    )(*[r[0] for r in rows], hist, do)


_VMEM_LIMIT = 56 * 2 ** 20
_VMEM_TILE_BUDGET = 40 * 2 ** 20


def _mm_tiles(m, n, k, sa, sb, so, sx, a_f32, b_f32, tn_fixed):
    best = None
    for tm in (1024, 512, 256, 128, 64):
        for tn in ((tn_fixed,) if tn_fixed else (1024, 768, 512, 384, 256, 128)):
            if m % tm or n % tn:
                continue
            need = 2 * (tm * k * sa + k * tn * sb + tm * tn * (so + sx)) + tm * tn * 4
            need += tm * k * (2 if sa == 4 else 0) + k * tn * (2 if sb == 4 else 0)
            need += tm * k * (4 if a_f32 else 0) + k * tn * (4 if b_f32 else 0)
            if need <= _VMEM_TILE_BUDGET and (best is None or (tm * tn, tm) > best[0]):
                best = ((tm * tn, tm), tm, tn)
    return best[1], best[2]


def _mm(name, a, b, dims, out_dtype, a_fn=None, b_fn=None, epi=None, extra=(), out_t=False, slab=False, side=None):
    m, kk = a.shape
    nn = b.shape[1] if dims is NN else b.shape[0]
    sx = sum(e.dtype.itemsize for e in extra)
    tm, tn = _mm_tiles(m, nn, kk, a.dtype.itemsize, b.dtype.itemsize, jnp.dtype(out_dtype).itemsize, sx,
                       a_fn is not None, b_fn is not None, _COLW if slab else None)
    nex = len(extra)

    def body(a_ref, b_ref, *rest):
        av, bv = a_ref[...], b_ref[...]
        if a_fn is not None:
            av = a_fn(av.astype(f32))
        if b_fn is not None:
            bv = b_fn(bv.astype(f32))
        acc = _dot(av.astype(bf16), bv.astype(bf16), dims)
        if epi is not None:
            acc = epi(acc, *[r[...] for r in rest[:nex]])
        o_ref = rest[nex]
        if out_t:
            o_ref[...] = acc.T.astype(o_ref.dtype)
        elif slab:
            o_ref[0] = acc.astype(o_ref.dtype)
        else:
            o_ref[...] = acc.astype(o_ref.dtype)

    bspec = pl.BlockSpec((kk, tn), lambda i, j: (0, j)) if dims is NN else pl.BlockSpec((tn, kk), lambda i, j: (j, 0))
    if out_t:
        out_spec, out_shape = pl.BlockSpec((tn, tm), lambda i, j: (j, i)), S((nn, m), out_dtype)
    elif slab:
        out_spec, out_shape = pl.BlockSpec((1, tm, tn), lambda i, j: (j, i, 0)), S((nn // tn, m, tn), out_dtype)
    else:
        out_spec, out_shape = pl.BlockSpec((tm, tn), lambda i, j: (i, j)), S((m, nn), out_dtype)
    return _pcall(
        side, body, name=name, grid=(m // tm, nn // tn),
        in_specs=[pl.BlockSpec((tm, kk), lambda i, j: (i, 0)), bspec]
        + [pl.BlockSpec((tm, tn), lambda i, j: (i, j)) for _ in extra],
        out_specs=out_spec, out_shape=out_shape,
        compiler_params=pltpu.CompilerParams(dimension_semantics=("parallel", "parallel"), vmem_limit_bytes=_VMEM_LIMIT),
    )(a, b, *extra)


def _f_norm(x, g):
    return (x * lax.rsqrt(jnp.mean(x * x, axis=-1, keepdims=True) + EPS) * g,)


def _f_foxpre(zqk, gq, gk, pm):
    def nrm(t, g):
        return t * lax.rsqrt(_dot(t * t, pm, prec=HI) + EPS) * g
    return nrm(zqk[:, :FOX_W], gq), nrm(zqk[:, FOX_W:], gk)


def _chunk_cumsum(x):
    n = x.shape[0]
    r, c = _iota2((n, n), 0), _iota2((n, n), 1)
    tri = jnp.logical_and(r >= c, (r // CH) == (c // CH)).astype(f32)
    return _dot(tri, x, prec=HI)


def _f_gdngate(zs, eb, ea, alog_b, dt_b):
    beta = jax.nn.sigmoid(_dot(zs, eb, prec=HI))
    la = -jnp.exp(alog_b) * jax.nn.softplus(_dot(zs, ea, prec=HI) + dt_b)
    return beta, _chunk_cumsum(la)


def _f_conv(j, x, w):
    t = x.shape[0]
    y = x * w[3:4, :]
    for jj in range(3):
        sh = 3 - jj
        xs = jnp.concatenate([jnp.zeros((sh, x.shape[1]), f32), x[:t - sh, :]], axis=0)
        y = y + xs * w[jj:jj + 1, :]
    y = jax.nn.silu(y)
    yn = y * lax.rsqrt(jnp.sum(y * y, axis=-1, keepdims=True) + EPS)
    return (jnp.where(j < 2 * GDN_H, yn, y),)


def _head_rms(o, nh):
    outs = []
    for h in range(nh):
        oh = o[:, HD * h:HD * (h + 1)]
        outs.append(oh * lax.rsqrt(jnp.mean(oh * oh, axis=-1, keepdims=True) + EPS))
    return jnp.concatenate(outs, axis=1)


def _f_post0(fox_o, o, gg, on):
    return (jnp.concatenate([fox_o, _head_rms(o, GDN_H) * on * jax.nn.silu(gg)], axis=1),)


def _f_post1(o, zg, on):
    return (_head_rms(o, HG_H) * on * jax.nn.silu(zg),)


def _f_hpre(zqf, lbl):
    lb = jax.nn.sigmoid(lbl[1:2, :] - lbl[0:1, :])
    fg = lb + (1.0 - lb) * jax.nn.sigmoid(zqf[:, D:])
    return jax.nn.silu(zqf[:, :D]), 1.0 - fg, _chunk_cumsum(jnp.log(fg))


def _dotb(a, b, dims=NN):
    return _dot(a.astype(bf16), b.astype(bf16), dims)


def _dot3(a, b):
    ah, bh = a.astype(bf16), b.astype(bf16)
    al, bl = (a - ah.astype(f32)).astype(bf16), (b - bh.astype(f32)).astype(bf16)
    return _dot(ah, bh) + (_dot(ah, bl) + _dot(al, bh))


def _split(t, nh):
    return [t[CH * ck:CH * (ck + 1), HD * h:HD * (h + 1)] for ck in range(t.shape[0] // CH) for h in range(nh)]


def _merge(units, nh):
    return jnp.concatenate([jnp.concatenate(units[i:i + nh], axis=1) for i in range(0, len(units), nh)], axis=0)


def _inv_impl(amats):
    n = amats[0].shape[0]
    eye = jnp.where(_iota2((n, n), 0) == _iota2((n, n), 1), 1.0, 0.0).astype(f32)
    xs, ps = [eye - a for a in amats], list(amats)
    for _ in range(max(1, (n - 1).bit_length()) - 1):
        ps = [_dotb(p, p) for p in ps]
        xs = [x + _dotb(x, p) for x, p in zip(xs, ps)]
    for _ in range(2):
        rs = [eye - x - _dot3(a, x) for a, x in zip(amats, xs)]
        xs = [x + _dotb(x, r) for x, r in zip(xs, rs)]
    return tuple(xs)


@jax.custom_vjp
def _inv_unit_lower(amats):
    return _inv_impl(amats)


def _inv_fwd(amats):
    xs = _inv_impl(amats)
    return xs, xs


def _inv_bwd(xs, dxs):
    return (tuple(-_dotb(_dotb(x, dx, TN), x, NT) for x, dx in zip(xs, dxs)),)


_inv_unit_lower.defvjp(_inv_fwd, _inv_bwd)


def _f_gdn_intra(q, k, v, bb, gb):
    qs, ks, vs, bs, gs = (_split(t, GDN_H) for t in (q, k, v, bb, gb))
    r, cc = _iota2((CH, CH), 0), _iota2((CH, CH), 1)
    causal, strict = r >= cc, r > cc
    beta, g, gl = [b[:, :1] for b in bs], [x[:, :1] for x in gs], [x[CH - 1:CH, :1] for x in gs]
    decay = [jnp.exp(jnp.where(causal, x[:, :CH] - x[:, :CH].T, -jnp.inf)) for x in gs]
    kb = [ki * bi for ki, bi in zip(ks, beta)]
    amat = [jnp.where(strict, _dotb(kbi, ki, NT) * di, 0.0) for kbi, ki, di in zip(kb, ks, decay)]
    tinv = _inv_unit_lower(tuple(amat))
    rhs = [jnp.concatenate([vi * bi, kbi * jnp.exp(gi)], axis=1) for vi, bi, kbi, gi in zip(vs, beta, kb, g)]
    uw = [_dotb(ti, ri) for ti, ri in zip(tinv, rhs)]
    qsc = [qi * (HD ** -0.5) for qi in qs]
    qk = [jnp.where(causal, _dotb(qi, ki, NT) * di, 0.0) for qi, ki, di in zip(qsc, ks, decay)]
    outs = ([x[:, :HD] for x in uw], [x[:, HD:] for x in uw],
            [jnp.concatenate([x, jnp.zeros_like(x)], axis=1) for x in qk],
            [qi * jnp.exp(gi) for qi, gi in zip(qsc, g)],
            [ki * jnp.exp(gli - gi) for ki, gli, gi in zip(ks, gl, g)],
            [jnp.broadcast_to(gli, (CH, HD)) for gli in gl])
    return tuple(_merge(o, GDN_H) for o in outs)


def _f_gdn_inter(u, w, qkp, qd, kd, glb, st):
    us, ws, qks, qds, kds, gls = (_split(t, GDN_H) for t in (u, w, qkp, qd, kd, glb))
    sts = [st[HD * h:HD * (h + 1), :] for h in range(GDN_H)]
    vn = [ui - _dotb(wi, si) for ui, wi, si in zip(us, ws, sts)]
    o = [_dotb(qi, si) + _dotb(xi[:, :CH], vi) for qi, si, xi, vi in zip(qds, sts, qks, vn)]
    s2 = [si * jnp.exp(gi[:1, :1]) + _dotb(ki, vi, TN) for si, gi, ki, vi in zip(sts, gls, kds, vn)]
    return jnp.concatenate(o, axis=1), jnp.concatenate(s2, axis=0)


def _f_hgrn_chunk(q, k, b, v, st):
    qs, ks, bs, vs = (_split(t, HG_H) for t in (q, k, b, v))
    sts = [st[HD * h:HD * (h + 1), :] for h in range(HG_H)]
    causal = _iota2((CH, CH), 0) >= _iota2((CH, CH), 1)
    bl, bm = [x[CH - 1:CH, :] for x in bs], [x[CH // 2 - 1:CH // 2, :] for x in bs]
    a = [jnp.where(causal, _dotb(qi * jnp.exp(bi - mi), ki * jnp.exp(mi - bi), NT), 0.0)
         for qi, ki, bi, mi in zip(qs, ks, bs, bm)]
    o = [_dotb(qi * jnp.exp(bi), si, NT) + _dotb(ai, vi) for qi, bi, si, ai, vi in zip(qs, bs, sts, a, vs)]
    s2 = [si * jnp.exp(li) + _dotb(vi, ki * jnp.exp(li - bi), TN) for si, li, vi, ki, bi in zip(sts, bl, vs, ks, bs)]
    return jnp.concatenate(o, axis=1), jnp.concatenate(s2, axis=0)


def _fox_gate_fwd(z0, fbias, t, tc=256):
    n = z0.shape[0]
    nt = t // tc

    def body(zs_ref, b_ref, ccol_ref, crow_ref, carry):
        @pl.when(pl.program_id(1) == 0)
        def _():
            carry[...] = jnp.zeros_like(carry)

        ls = jnp.where(_iota2((tc, LANES), 1) < FOX_H, jax.nn.log_sigmoid(zs_ref[...] + b_ref[...]), 0.0)
        tri = (_iota2((tc, tc), 0) >= _iota2((tc, tc), 1)).astype(f32)
        c = _dot(tri, ls, prec=HI) + carry[...]
        carry[...] = c[tc - 1:tc, :]
        ccol_ref[...] = c
        crow_ref[0] = c.T[:FOX_H, :]

    return pl.pallas_call(
        body, name="fox_gate_fwd", grid=(n // t, nt),
        in_specs=[pl.BlockSpec((tc, LANES), lambda b, i: (b * nt + i, Z0_SMALL)), pl.BlockSpec((1, LANES), lambda b, i: (0, 0))],
        out_specs=[pl.BlockSpec((tc, LANES), lambda b, i: (b * nt + i, 0)), pl.BlockSpec((1, FOX_H, tc), lambda b, i: (b, 0, i))],
        out_shape=[S((n, LANES), f32), S((n // t, FOX_H, t), f32)],
        scratch_shapes=[pltpu.VMEM((1, LANES), f32)],
        compiler_params=pltpu.CompilerParams(dimension_semantics=("parallel", "arbitrary")),
    )(z0, fbias)


def _fox_gate_bwd(z0, fbias, dcq, dck, t, tc=256):
    n = z0.shape[0]
    nt = t // tc

    def body(zs_ref, b_ref, dcq_ref, dck_ref, dz_ref, db_ref, carry):
        first = jnp.logical_and(pl.program_id(0) == 0, pl.program_id(1) == 0)

        @pl.when(pl.program_id(1) == 0)
        def _():
            carry[...] = jnp.zeros_like(carry)

        @pl.when(first)
        def _():
            db_ref[...] = jnp.zeros_like(db_ref)

        dc = dcq_ref[0] + dcq_ref[1] + dcq_ref[2] + dcq_ref[3]
        drow = dck_ref[0, 0] + dck_ref[1, 0] + dck_ref[2, 0] + dck_ref[3, 0]
        eye = (_iota2((FOX_H, LANES), 0) == _iota2((FOX_H, LANES), 1)).astype(f32)
        dc = dc + _dot(drow, eye, TN, prec=HI)
        triu = (_iota2((tc, tc), 0) <= _iota2((tc, tc), 1)).astype(f32)
        dls = _dot(triu, dc, prec=HI) + carry[...]
        carry[...] = dls[0:1, :]
        x = zs_ref[...] + b_ref[...]
        dz = jnp.where(_iota2((tc, LANES), 1) < FOX_H, dls * jax.nn.sigmoid(-x), 0.0)
        dz_ref[...] = dz
        db_ref[...] += jnp.sum(dz, axis=0, keepdims=True)

    def rev(b, i):
        return b * nt + (nt - 1 - i)

    return pl.pallas_call(
        body, name="fox_gate_bwd", grid=(n // t, nt),
        in_specs=[pl.BlockSpec((tc, LANES), lambda b, i: (rev(b, i), Z0_SMALL)), pl.BlockSpec((1, LANES), lambda b, i: (0, 0)),
                  pl.BlockSpec((4, tc, LANES), lambda b, i: (0, rev(b, i), 0)),
                  pl.BlockSpec((4, 1, FOX_H, tc), lambda b, i: (0, b, 0, nt - 1 - i))],
        out_specs=[pl.BlockSpec((tc, LANES), lambda b, i: (rev(b, i), 0)), pl.BlockSpec((1, LANES), lambda b, i: (0, 0))],
        out_shape=[S((n, LANES), f32), S((1, LANES), f32)],
        scratch_shapes=[pltpu.VMEM((1, LANES), f32)],
        compiler_params=pltpu.CompilerParams(dimension_semantics=("arbitrary", "arbitrary")),
    )(z0, fbias, dcq, dck)


def _fox_scores(hh, p, i, tq, q, k, ccol, crow):
    kmax = k.shape[0]
    lane = _iota2((1, LANES), 1)
    mh = (lane // FOX_D) == hh
    h = 2 * p + hh
    qh = jnp.where(mh, q, 0.0).astype(bf16)
    s = _dot(qh, k, NT) * (FOX_D ** -0.5)
    cq = jnp.sum(jnp.where(lane == h, ccol, 0.0), axis=1, keepdims=True)
    ck = jnp.sum(jnp.where(_iota2((FOX_H, 1), 0) == h, crow, 0.0), axis=0, keepdims=True)
    causal = _iota2((1, kmax), 1) <= (i * tq + _iota2((tq, 1), 0))
    s = jnp.where(causal, s + cq - ck, -jnp.inf)
    pe = jnp.exp(s - jnp.max(s, axis=1, keepdims=True))
    return mh, qh, pe, jnp.sum(pe, axis=1, keepdims=True)


def _fox_attn_fwd(qn, kn, z0, ccol, crow, t, tq=256, side=None):
    n = qn.shape[0]
    nq = t // tq

    def body(q_ref, k_ref, v_ref, ccol_ref, crow_ref, o_ref):
        p = pl.program_id(1)
        k, v, crow = k_ref[...].astype(bf16), v_ref[...].astype(bf16), crow_ref[0]
        for i in range(nq):
            rows, kmax = pl.ds(i * tq, tq), (i + 1) * tq
            q, cc = q_ref[rows, :], ccol_ref[rows, :]
            acc = jnp.zeros((tq, LANES), f32)
            for hh in range(2):
                mh, _, pe, l = _fox_scores(hh, p, i, tq, q, k[:kmax], cc, crow[:, :kmax])
                acc = jnp.where(mh, _dot(pe.astype(bf16), v[:kmax]) / l, acc)
            o_ref[rows, :] = acc

    seq = lambda b, p: (b, p)
    return _pcall(
        side, body, name="fox_attn_fwd", grid=(n // t, FOX_H // 2),
        in_specs=[pl.BlockSpec((t, LANES), seq), pl.BlockSpec((t, LANES), seq), pl.BlockSpec((t, LANES), lambda b, p: (b, Z0_FV + p)),
                  pl.BlockSpec((t, LANES), lambda b, p: (b, 0)), pl.BlockSpec((1, FOX_H, t), lambda b, p: (b, 0, 0))],
        out_specs=pl.BlockSpec((t, LANES), seq),
        out_shape=S((n, FOX_W), f32),
        compiler_params=pltpu.CompilerParams(dimension_semantics=("parallel", "parallel")),
    )(qn, kn, z0, ccol, crow)


def _fox_attn_bwd(qn, kn, z0, ccol, crow, do, t, tq=256, side=None):
    n = qn.shape[0]
    nq = t // tq
    nb = n // t

    def body(q_ref, k_ref, v_ref, ccol_ref, crow_ref, do_ref, dq_ref, dk_ref, dv_ref, dcq_ref, dck_ref):
        p = pl.program_id(1)
        dk_ref[...] = jnp.zeros_like(dk_ref)
        dv_ref[...] = jnp.zeros_like(dv_ref)
        dck_ref[...] = jnp.zeros_like(dck_ref)
        kf, v, crow = k_ref[...], v_ref[...].astype(bf16), crow_ref[0]
        k = kf.astype(bf16)
        lane = _iota2((1, LANES), 1)
        sub = _iota2((FOX_H, 1), 0)
        scale = FOX_D ** -0.5
        for i in range(nq):
            rows, kmax = pl.ds(i * tq, tq), (i + 1) * tq
            q, cc, dout = q_ref[rows, :], ccol_ref[rows, :], do_ref[rows, :]
            dq = jnp.zeros((tq, LANES), f32)
            dcq = jnp.zeros((tq, LANES), f32)
            for hh in range(2):
                mh, qh, pe, l = _fox_scores(hh, p, i, tq, q, k[:kmax], cc, crow[:, :kmax])
                pr = pe / l
                doh = jnp.where(mh, dout, 0.0).astype(bf16)
                dp = _dot(doh, v[:kmax], NT)
                ds = pr * (dp - jnp.sum(pr * dp, axis=1, keepdims=True))
                dsb = ds.astype(bf16)
                dq = dq + _dot(dsb, jnp.where(mh, kf[:kmax], 0.0).astype(bf16)) * scale
                dk_ref[:kmax, :] += _dot(dsb, qh, TN) * scale
                dv_ref[:kmax, :] += _dot(pr.astype(bf16), doh, TN)
                h = 2 * p + hh
                dcq = dcq + jnp.where(lane == h, jnp.sum(ds, axis=1, keepdims=True), 0.0)
                dck_ref[0, 0, :, :kmax] += jnp.where(sub == h, -jnp.sum(ds, axis=0, keepdims=True), 0.0)
            dq_ref[rows, :] = dq
            dcq_ref[0, rows, :] = dcq

    seq = lambda b, p: (b, p)
    return _pcall(
        side, body, name="fox_attn_bwd", grid=(nb, FOX_H // 2),
        in_specs=[pl.BlockSpec((t, LANES), seq), pl.BlockSpec((t, LANES), seq), pl.BlockSpec((t, LANES), lambda b, p: (b, Z0_FV + p)),
                  pl.BlockSpec((t, LANES), lambda b, p: (b, 0)), pl.BlockSpec((1, FOX_H, t), lambda b, p: (b, 0, 0)),
                  pl.BlockSpec((t, LANES), seq)],
        out_specs=[pl.BlockSpec((t, LANES), seq), pl.BlockSpec((t, LANES), seq), pl.BlockSpec((t, LANES), seq),
                   pl.BlockSpec((1, t, LANES), lambda b, p: (p, b, 0)), pl.BlockSpec((1, 1, FOX_H, t), lambda b, p: (p, b, 0, 0))],
        out_shape=[S((n, FOX_W), f32), S((n, FOX_W), f32), S((n, FOX_W), f32), S((4, n, LANES), f32), S((4, nb, FOX_H, t), f32)],
        compiler_params=pltpu.CompilerParams(dimension_semantics=("parallel", "parallel")),
    )(qn, kn, z0, ccol, crow, do)


def _loss_head(y, tgt, tm=256):
    n = y.shape[0]

    def body(y_ref, t_ref, dy_ref, l_ref, dyt_ref):
        @pl.when(pl.program_id(0) == 0)
        def _():
            l_ref[...] = jnp.zeros_like(l_ref)

        e = y_ref[...] - t_ref[...]
        dy = e * (1.0 / D)
        dy_ref[...] = dy
        dyt_ref[...] = dy.T.astype(bf16)
        l_ref[...] += jnp.sum(e * e, axis=0, keepdims=True)

    return pl.pallas_call(
        body, name="loss_head", grid=(n // tm,),
        in_specs=[pl.BlockSpec((tm, D), lambda i: (i, 0))] * 2,
        out_specs=[pl.BlockSpec((tm, D), lambda i: (i, 0)), pl.BlockSpec((1, D), lambda i: (0, 0)),
                   pl.BlockSpec((D, tm), lambda i: (0, i))],
        out_shape=[S((n, D), f32), S((1, D), f32), S((D, n), bf16)],
        compiler_params=pltpu.CompilerParams(dimension_semantics=("arbitrary",)),
    )(y, tgt)


def _adamw_math(w, g, m, v):
    m = ADAM_B1 * m + (1.0 - ADAM_B1) * g
    v = ADAM_B2 * v + (1.0 - ADAM_B2) * (g * g)
    m_hat = m / (1.0 - ADAM_B1 ** ADAM_STEP)
    v_hat = v / (1.0 - ADAM_B2 ** ADAM_STEP)
    return -ADAM_LR * (m_hat / (jnp.sqrt(v_hat) + ADAM_EPS) + ADAM_WD * w), m, v


def _adamw_big(name, idx, gmine, recv1, recv2, w, m, v):
    r, wc = w.shape
    c = gmine.shape[2]
    tr = min(r, 256)

    def body(idx_ref, gm_ref, r1_ref, r2_ref, w_ref, m_ref, v_ref, g_ref, d_ref, nm_ref, nv_ref):
        g = gm_ref[0].astype(f32) + r1_ref[0].astype(f32)
        for k in range(3):
            g = g + r2_ref[k].astype(f32)
        g = g[:, :wc]
        d, nm, nv = _adamw_math(w_ref[...], g, m_ref[...], v_ref[...])
        g_ref[...] = g
        d_ref[...] = d
        nm_ref[...] = nm
        nv_ref[...] = nv

    row = pl.BlockSpec((tr, wc), lambda i, s: (i, 0))
    return pl.pallas_call(
        body, name=name,
        grid_spec=pltpu.PrefetchScalarGridSpec(
            num_scalar_prefetch=1, grid=(r // tr,),
            in_specs=[pl.BlockSpec((1, tr, c), lambda i, s: (s[0], i, 0)), pl.BlockSpec((1, tr, c), lambda i, s: (s[1], i, 0)),
                      pl.BlockSpec((3, tr, c), lambda i, s: (0, i, 0)), row, row, row],
            out_specs=[row, row, row, row]),
        out_shape=[S((r, wc), f32)] * 4,
        compiler_params=pltpu.CompilerParams(dimension_semantics=("parallel",)),
    )(idx, gmine, recv1, recv2, w, m, v)


def _pair_sum(name, idx, gmine, recv1):
    _, r, c = gmine.shape
    g4 = gmine.reshape(4, 2, r, c)

    def body(idx_ref, gm_ref, r1_ref, o_ref):
        o_ref[0] = (gm_ref[0, 0].astype(f32) + r1_ref[0].astype(f32)).astype(bf16)

    return pl.pallas_call(
        body, name=name,
        grid_spec=pltpu.PrefetchScalarGridSpec(
            num_scalar_prefetch=1, grid=(4,),
            in_specs=[pl.BlockSpec((1, 1, r, c), lambda ch, s: (ch, s[0], 0, 0)), pl.BlockSpec((1, r, c), lambda ch, s: (ch, 0, 0))],
            out_specs=pl.BlockSpec((1, r, c), lambda ch, s: (ch, 0, 0))),
        out_shape=S((4, r, c), bf16),
        compiler_params=pltpu.CompilerParams(dimension_semantics=("parallel",)),
    )(idx, g4, recv1)


def _adamw_small(parts, w, m, v, own_mask, own_g):
    def body(p_ref, w_ref, m_ref, v_ref, mask_ref, og_ref, g_ref, d_ref, nm_ref, nv_ref):
        g = p_ref[0]
        for k in range(1, NDEV):
            g = g + p_ref[k]
        g_ref[...] = g
        ge = jnp.where(mask_ref[...] > 0.5, og_ref[...], g)
        d, nm, nv = _adamw_math(w_ref[...], ge, m_ref[...], v_ref[...])
        d_ref[...] = d
        nm_ref[...] = nm
        nv_ref[...] = nv

    return pl.pallas_call(body, name="adamw_small", out_shape=[S(w.shape, f32)] * 4)(parts, w, m, v, own_mask, own_g)


def _sum_parts(parts):
    def body(p_ref, g_ref):
        g = p_ref[0]
        for k in range(1, NDEV):
            g = g + p_ref[k]
        g_ref[...] = g

    return pl.pallas_call(body, name="sum_parts", out_shape=S(parts.shape[1:], f32))(parts)


def _me():
    return lax.axis_index("x"), lax.axis_index("y"), lax.axis_index("c")


def _hbm_specs(n):
    return [pl.BlockSpec(memory_space=pl.ANY)] * n


def _allgather(name, xs):
    na = len(xs)

    def body(*refs):
        x_refs, out_refs = refs[:na], refs[na:2 * na]
        send_sems, recv_sems, local_sems = refs[2 * na:]
        mx, my, mc = _me()
        me, sib = (mx, my, mc), (mx, my, 1 - mc)
        chips = [(1 - mx, my), (mx, 1 - my), (1 - mx, 1 - my)]

        def slab(a, px, py, pc):
            return out_refs[a].at[4 * px + 2 * py + pc]

        def copy(a, k, block, to, own=False):
            return pltpu.make_async_remote_copy(
                src_ref=x_refs[a] if own else slab(a, *block), dst_ref=slab(a, *block),
                send_sem=send_sems.at[7 * a + k], recv_sem=recv_sems.at[7 * a + k], device_id=to, device_id_type=MESH)

        mine = [pltpu.make_async_copy(x_refs[a], slab(a, *me), local_sems.at[a]) for a in range(na)]
        first = []
        for a in range(na):
            mine[a].start()
            first += [copy(a, 0, me, sib, own=True)] + [copy(a, 1 + j, me, (*chip, mc), own=True) for j, chip in enumerate(chips)]
        for cp in first:
            cp.start()
        passed = []
        for j, chip in enumerate(chips):
            for a in range(na):
                copy(a, 1 + j, (*chip, mc), me).wait_recv()
                passed.append(copy(a, 4 + j, (*chip, mc), sib))
                passed[-1].start()
        for a in range(na):
            copy(a, 0, sib, me).wait_recv()
            for j, chip in enumerate(chips):
                copy(a, 4 + j, (*chip, 1 - mc), me).wait_recv()
        for cp in first + passed:
            cp.wait_send()
        for cp in mine:
            cp.wait()

    return pl.pallas_call(
        body, name=name, out_shape=[S((NDEV,) + x.shape, x.dtype) for x in xs],
        in_specs=_hbm_specs(na), out_specs=_hbm_specs(na),
        scratch_shapes=[pltpu.SemaphoreType.DMA((7 * na,)), pltpu.SemaphoreType.DMA((7 * na,)), pltpu.SemaphoreType.DMA((na,))],
    )(*xs)


def _rs_sibling(gs):
    na = len(gs)

    def body(*refs):
        g_refs, out_refs, send_sems, recv_sems = refs[:na], refs[na:2 * na], refs[2 * na], refs[2 * na + 1]
        mx, my, mc = _me()
        cps = [pltpu.make_async_remote_copy(
            src_ref=g_refs[a].at[2 * ch + 1 - mc], dst_ref=out_refs[a].at[ch], send_sem=send_sems.at[4 * a + ch],
            recv_sem=recv_sems.at[4 * a + ch], device_id=(mx, my, 1 - mc), device_id_type=MESH)
            for a in range(na) for ch in range(4)]
        for cp in cps:
            cp.start()
        for cp in cps:
            cp.wait_recv()
        for cp in cps:
            cp.wait_send()

    return pl.pallas_call(
        body, name="rs_sibling", out_shape=[S((4,) + g.shape[1:], g.dtype) for g in gs],
        in_specs=_hbm_specs(na), out_specs=_hbm_specs(na),
        scratch_shapes=[pltpu.SemaphoreType.DMA((4 * na,)), pltpu.SemaphoreType.DMA((4 * na,))],
    )(*gs)


def _side_allgather(xs):
    na = len(xs)

    def mk(x_refs, out_refs, sems):
        send_sems, recv_sems, local_sems = sems
        mx, my, mc = _me()
        me, sib = (mx, my, mc), (mx, my, 1 - mc)
        chips = [(1 - mx, my), (mx, 1 - my), (1 - mx, 1 - my)]

        def slab(a, px, py, pc):
            return out_refs[a].at[4 * px + 2 * py + pc]

        def copy(a, k, block, to, own=False):
            return pltpu.make_async_remote_copy(
                src_ref=x_refs[a] if own else slab(a, *block), dst_ref=slab(a, *block),
                send_sem=send_sems.at[7 * a + k], recv_sem=recv_sems.at[7 * a + k], device_id=to, device_id_type=MESH)

        mine = [pltpu.make_async_copy(x_refs[a], slab(a, *me), local_sems.at[a]) for a in range(na)]
        first = []
        for a in range(na):
            first += [copy(a, 0, me, sib, own=True)] + [copy(a, 1 + j, me, (*chip, mc), own=True) for j, chip in enumerate(chips)]
        return me, sib, chips, mc, copy, mine, first

    def start(x_refs, out_refs, sems):
        *_, mine, first = mk(x_refs, out_refs, sems)
        for cp in mine + first:
            cp.start()

    def finish(x_refs, out_refs, sems):
        me, sib, chips, mc, copy, mine, first = mk(x_refs, out_refs, sems)
        passed = []
        for j, chip in enumerate(chips):
            for a in range(na):
                copy(a, 1 + j, (*chip, mc), me).wait_recv()
                passed.append(copy(a, 4 + j, (*chip, mc), sib))
                passed[-1].start()
        for a in range(na):
            copy(a, 0, sib, me).wait_recv()
            for j, chip in enumerate(chips):
                copy(a, 4 + j, (*chip, 1 - mc), me).wait_recv()
        for cp in first + passed:
            cp.wait_send()
        for cp in mine:
            cp.wait()

    scratch = [pltpu.SemaphoreType.DMA((7 * na,)), pltpu.SemaphoreType.DMA((7 * na,)), pltpu.SemaphoreType.DMA((na,))]
    return _Side(xs, [S((NDEV,) + x.shape, x.dtype) for x in xs], scratch, start, finish)


def _side_exchange(arrs, nslot, out_slots, route):
    na = len(arrs)

    def copies(in_refs, out_refs, sems):
        send_sems, recv_sems = sems
        return [pltpu.make_async_remote_copy(
            src_ref=in_refs[a].at[src], dst_ref=out_refs[a].at[k], send_sem=send_sems.at[nslot * a + k],
            recv_sem=recv_sems.at[nslot * a + k], device_id=to, device_id_type=MESH)
            for a in range(na) for k, (src, to) in enumerate(route(*_me()))]

    def start(in_refs, out_refs, sems):
        for cp in copies(in_refs, out_refs, sems):
            cp.start()

    def finish(in_refs, out_refs, sems):
        cps = copies(in_refs, out_refs, sems)
        for cp in cps:
            cp.wait_recv()
        for cp in cps:
            cp.wait_send()

    scratch = [pltpu.SemaphoreType.DMA((nslot * na,)), pltpu.SemaphoreType.DMA((nslot * na,))]
    return _Side(arrs, [S((out_slots,) + x.shape[1:], x.dtype) for x in arrs], scratch, start, finish)


def _side_rs_sibling(gs):
    return _side_exchange(gs, 4, 4, lambda mx, my, mc: [(2 * ch + 1 - mc, (mx, my, 1 - mc)) for ch in range(4)])


def _side_rs_chips(ps):
    return _side_exchange(ps, 3, 3, lambda mx, my, mc: [(2 * cx + cy, (cx, cy, mc)) for cx, cy in
                                                        [(1 - mx, my), (mx, 1 - my), (1 - mx, 1 - my)]])


_COLW = 512
_COL_NAMES = ("l0_w_in", "l0_w_ff1", "l1_w_in", "l1_w_ff1")
_ROW_NAMES = ("l0_w_out", "l0_w_ff2", "l1_w_out", "l1_w_ff2")
_BIG_NAMES = _COL_NAMES + _ROW_NAMES


def _full_weight(gathered, name, ncols):
    if name in _ROW_NAMES:
        return gathered.reshape(-1, D)
    return gathered[:, :, :ncols].transpose(1, 0, 2).reshape(D, NDEV * ncols)


def _regroup_w_in0(w):
    main = jnp.concatenate([w[:, 0:1536], w[:, 1544:3080], w[:, 3088:3600]], axis=1)
    small = jnp.concatenate([w[:, 1536:1544], w[:, 3080:3088]], axis=1)
    return jnp.concatenate([main, small, jnp.zeros((D, ZW0 - 3584 - 16), w.dtype)], axis=1)


def _ungroup_w_in0(g):
    return jnp.concatenate([g[:, 0:1536], g[:, 3584:3592], g[:, 1536:3072], g[:, 3592:3600], g[:, 3072:3584]], axis=1)


def _col_slabs(g, ncols):
    g = g.reshape(D, NDEV, ncols).transpose(1, 0, 2)
    return jnp.pad(g, ((0, 0), (0, 0), (0, _COLW - ncols)))


def _sq(t):
    return t * t


def _mlp_fwd(tag, x, gain, w1, w2):
    h, ht = _tok_fwd(f"{tag}_ffn_norm", _f_norm, [(x, D, 0)], [(gain, None, 0)], [(D, bf16)], 256, also_t=(0,))
    a = _mm(f"{tag}_ff1", h, w1, NN, bf16, epi=lambda acc: jnp.maximum(acc, 0.0))
    y = _mm(f"{tag}_ff2", a, w2, NN, f32, a_fn=_sq, epi=lambda acc, r: acc + r, extra=(x,))
    return y, (h, ht, a)


def _mlp_bwd(tag, x, gain, w1, w2, saved, dy, dyt, side=None):
    h, ht, a = saved
    da = _mm(f"{tag}_ff2_dx", dy, w2, NT, bf16, epi=lambda acc, av: acc * 2.0 * av.astype(f32), extra=(a,), side=side)
    if side is not None:
        da, side_res = da
    dw2 = _mm(f"{tag}_ff2_dw", dyt, a, NN, bf16, b_fn=_sq, out_t=True)
    dh = _mm(f"{tag}_ff1_dx", da, w1, NT, f32)
    dw1 = _mm(f"{tag}_ff1_dw", ht, da, NN, bf16, slab=True)
    res = _tok_bwd(f"{tag}_ffn_norm_bwd", _f_norm, [(x, D, 0)], [(gain, None, 0)], [(dh, D)], 256, [0], [0], addto={0: dy})
    return (res, dw1, dw2) if side is None else ((res, dw1, dw2), side_res)


def _row(v):
    return v.reshape(1, -1).astype(f32)


_L0_REST = ("l0_w_ff1", "l0_w_out", "l0_w_ff2")
_L1_MIX = ("l1_w_in", "l1_w_out")
_L1_FFN = ("l1_w_ff1", "l1_w_ff2")
_GRAD_A = ("l1_w_ff1", "l1_w_ff2", "l1_w_out", "l1_w_in")
_GRAD_B = ("l0_w_ff1", "l0_w_ff2", "l0_w_out")


def _train(x, tgt, args, mom, var, t):
    n = x.shape[0]
    nchunk = t // CH
    mx, my, mc = _me()
    dev, chip = 4 * mx + 2 * my + mc, 2 * mx + my
    core_idx = jnp.reshape(mc, (1,)).astype(jnp.int32)
    own_idx = jnp.stack([dev, chip]).astype(jnp.int32)
    ncols = {nm: args[nm].shape[1] for nm in _COL_NAMES}
    conv_cols = args["l0_gdn_conv"].shape[1]
    g, big, w = {}, {}, {}

    def send(nm):
        a = args[nm].astype(bf16)
        return jnp.pad(a, ((0, 0), (0, _COLW - ncols[nm]))) if nm in _COL_NAMES else a

    def take(names, gathered):
        for nm, arr in zip(names, gathered):
            w[nm] = _full_weight(arr, nm, ncols.get(nm, 0))

    def by_dev(nm, ga):
        return ga if nm in _COL_NAMES else ga.reshape(NDEV, -1, D)

    def pair(names, gs, r1s):
        return [_pair_sum(f"rs_pair_sum_{nm}", core_idx, ga, r1) for nm, ga, r1 in zip(names, gs, r1s)]

    def adam(names, gs, r1s, r2s):
        for nm, ga, r1, r2 in zip(names, gs, r1s, r2s):
            big[nm] = _adamw_big(f"adamw_{nm}", own_idx, ga, r1, r2, args[nm], mom[nm], var[nm])

    first = _allgather("ag_first", [send("l0_w_in"), jnp.pad(args["l0_gdn_conv"], ((0, 4), (0, LANES * 2 - conv_cols)))])
    take(("l0_w_in",), first[:1])
    w_in0 = _regroup_w_in0(w["l0_w_in"])
    wconv = first[1][:, :4, :conv_cols].transpose(1, 0, 2).reshape(4, NDEV * conv_cols)

    li = jnp.arange(FOX_W)
    pm = jnp.where((li[:, None] // FOX_D) == (li[None, :] // FOX_D), 1.0 / FOX_D, 0.0).astype(f32)
    lane_head = jnp.arange(GDN_W) // HD
    sel = lambda first_lane: (jnp.arange(LANES)[:, None] == (first_lane + lane_head)[None, :]).astype(f32)
    e_beta, e_alpha = sel(FOX_H), sel(FOX_H + GDN_H)
    alog_b, dt_b = _row(jnp.repeat(args["l0_gdn_A_log"], HD)), _row(jnp.repeat(args["l0_gdn_dt_bias"], HD))
    gq_t, gk_t = _row(jnp.tile(args["l0_fox_q_norm"], FOX_H)), _row(jnp.tile(args["l0_fox_k_norm"], FOX_H))
    on0_t, on1_t = _row(jnp.tile(args["l0_gdn_o_norm"], GDN_H)), _row(jnp.tile(args["l1_hgrn_o_norm"], HG_H))
    fbias = jnp.pad(_row(args["l0_fox_f_bias"]), ((0, 0), (0, LANES - FOX_H)))
    g0m, g0f, g1m, g1f = (_row(args[k]) for k in ("l0_mix_norm", "l0_ffn_norm", "l1_mix_norm", "l1_ffn_norm"))
    lbl = args["hgrn_lb_logits"].astype(f32)

    h0, h0t = _tok_fwd("l0_mix_norm", _f_norm, [(x, D, 0)], [(g0m, None, 0)], [(D, bf16)], 256, also_t=(0,))
    z0 = _mm("l0_in", h0, w_in0, NN, f32)
    qk_rows = [(z0, 2 * FOX_W, 0)]
    qk_consts = [(gq_t, None, 0), (gk_t, None, 0), (pm, None, 0)]
    qn, kn = _tok_fwd("fox_pre", _f_foxpre, qk_rows, qk_consts, [(FOX_W, f32)] * 2, 256)
    ccol, crow = _fox_gate_fwd(z0, fbias, t)
    fox_o, got = _fox_attn_fwd(qn, kn, z0, ccol, crow, t, side=_side_allgather([send(nm) for nm in _L0_REST]))
    take(_L0_REST, got)
    conv_rows, conv_consts = [(z0, LANES, Z0_GQKV)], [(wconv, LANES, 0)]
    (qkv,) = _tok_fwd("gdn_conv", _f_conv, conv_rows, conv_consts, [(LANES, f32)], t, ncb=12, with_j=True)
    gate_rows = [(z0, LANES, Z0_SMALL)]
    gate_consts = [(e_beta, None, 0), (e_alpha, None, 0), (alog_b, None, 0), (dt_b, None, 0)]
    beta_b, g_b = _tok_fwd("gdn_gate", _f_gdngate, gate_rows, gate_consts, [(GDN_W, f32)] * 2, 256)
    intra_rows = [(qkv, GDN_W, 0), (qkv, GDN_W, 1), (qkv, GDN_W, 2), (beta_b, GDN_W, 0), (g_b, GDN_W, 0)]
    intra, got = _tok_fwd("gdn_intra", _f_gdn_intra, intra_rows, [], [(GDN_W, f32)] * 6, 2 * CH,
                          side=_side_allgather([send(nm) for nm in _L1_MIX]))
    take(_L1_MIX, got)
    inter_rows = [(a, 0) for a in intra]
    gdn_o, gdn_hist = _scan_fwd("gdn_scan", _f_gdn_inter, inter_rows, GDN_H, nchunk)
    post0_rows, post0_consts = [(fox_o, FOX_W, 0), (gdn_o, GDN_W, 0), (z0, GDN_W, Z0_GG // 4)], [(on0_t, None, 0)]
    cat0, cat0t = _tok_fwd("l0_post", _f_post0, post0_rows, post0_consts, [(D, bf16)], 256, also_t=(0,))
    x1 = _mm("l0_out", cat0, w["l0_w_out"], NN, f32, epi=lambda acc, r: acc + r, extra=(x,))
    x2, mlp0 = _mlp_fwd("l0", x1, g0f, w["l0_w_ff1"], w["l0_w_ff2"])

    h1, h1t = _tok_fwd("l1_mix_norm", _f_norm, [(x2, D, 0)], [(g1m, None, 0)], [(D, bf16)], 256, also_t=(0,))
    z1 = _mm("l1_in", h1, w["l1_w_in"], NN, f32)
    hpre_rows, hpre_consts = [(z1, 2 * D, 0)], [(lbl, None, 0)]
    hq, hk, hb = _tok_fwd("hgrn_pre", _f_hpre, hpre_rows, hpre_consts, [(D, f32)] * 3, 256)
    hg_rows = [(hq, 0), (hk, 0), (hb, 0), (z1, 2)]
    (hg_o, hg_hist), got = _scan_fwd("hgrn_scan", _f_hgrn_chunk, hg_rows, HG_H, nchunk,
                                     side=_side_allgather([send(nm) for nm in _L1_FFN]))
    take(_L1_FFN, got)
    post1_rows, post1_consts = [(hg_o, D, 0), (z1, D, 3)], [(on1_t, None, 0)]
    cat1, cat1t = _tok_fwd("l1_post", _f_post1, post1_rows, post1_consts, [(D, bf16)], 256, also_t=(0,))
    x3 = _mm("l1_out", cat1, w["l1_w_out"], NN, f32, epi=lambda acc, r: acc + r, extra=(x2,))
    y, mlp1 = _mlp_fwd("l1", x3, g1f, w["l1_w_ff1"], w["l1_w_ff2"])

    dy, loss_row, dyt = _loss_head(y, tgt)

    (dx3, g["l1_ffn_norm"]), ga_ff1, ga_ff2 = _mlp_bwd("l1", x3, g1f, w["l1_w_ff1"], w["l1_w_ff2"], mlp1, dy, dyt)
    dcat1 = _mm("l1_out_dx", dx3, w["l1_w_out"], NT, f32)
    ga_out = _mm("l1_out_dw", cat1t, dx3, NN, bf16)
    dhg_o, dzg, don1 = _tok_bwd("l1_post_bwd", _f_post1, post1_rows, post1_consts, [(dcat1, D)], 256, [0, 1], [0])
    dhq, dhk, dhb, dzi = _scan_bwd("hgrn_scan_bwd", _f_hgrn_chunk, hg_rows, hg_hist, dhg_o, HG_H, nchunk)
    dzqf, dlbl = _tok_bwd("hgrn_pre_bwd", _f_hpre, hpre_rows, hpre_consts, [(dhq, D), (dhk, D), (dhb, D)], 256, [0], [0])
    dz1 = jnp.concatenate([dzqf, dzi, dzg], axis=1).astype(bf16)
    dh1 = _mm("l1_in_dx", dz1, w["l1_w_in"], NT, f32)
    ga_in = _mm("l1_in_dw", h1t, dz1, NN, bf16, slab=True)
    dx2, g["l1_mix_norm"], dx2t = _tok_bwd("l1_mix_norm_bwd", _f_norm, [(x2, D, 0)], [(g1m, None, 0)], [(dh1, D)], 256, [0], [0],
                                           addto={0: dx3}, also_t=(0,))
    g["l1_hgrn_o_norm"] = don1.reshape(HG_H, HD).sum(0)
    g["hgrn_lb_logits"] = dlbl
    gs_a = [by_dev(nm, ga) for nm, ga in zip(_GRAD_A, (ga_ff1, ga_ff2, ga_out, ga_in))]

    ((dx1, g["l0_ffn_norm"]), gb_ff1, gb_ff2), r1_a = _mlp_bwd("l0", x1, g0f, w["l0_w_ff1"], w["l0_w_ff2"], mlp0, dx2, dx2t,
                                                               side=_side_rs_sibling(gs_a))
    pairs_a = pair(_GRAD_A, gs_a, r1_a)
    dcat0 = _mm("l0_out_dx", dx1, w["l0_w_out"], NT, f32)
    gb_out = _mm("l0_out_dw", cat0t, dx1, NN, bf16)
    gs_b = [by_dev(nm, ga) for nm, ga in zip(_GRAD_B, (gb_ff1, gb_ff2, gb_out))]
    dfox_o, dgdn_o, dgg, don0 = _tok_bwd("l0_post_bwd", _f_post0, post0_rows, post0_consts, [(dcat0, D)], 256, [0, 1, 2], [0])
    (dqn, dkn, dfv, dcq, dck), got = _fox_attn_bwd(qn, kn, z0, ccol, crow, dfox_o, t,
                                                   side=_join_sides([_side_rs_chips(pairs_a), _side_rs_sibling(gs_b)]))
    r2_a, r1_b = got[:len(_GRAD_A)], got[len(_GRAD_A):]
    adam(_GRAD_A, gs_a, r1_a, r2_a)
    pairs_b = pair(_GRAD_B, gs_b, r1_b)
    dinter = _scan_bwd("gdn_scan_bwd", _f_gdn_inter, inter_rows, gdn_hist, dgdn_o, GDN_H, nchunk)
    (dq_g, dk_g, dv_g, dbeta_b, dg_b), r2_b = _tok_bwd("gdn_intra_bwd", _f_gdn_intra, intra_rows, [], [(a, GDN_W) for a in dinter],
                                                       2 * CH, [0, 1, 2, 3, 4], [], side=_side_rs_chips(pairs_b))
    adam(_GRAD_B, gs_b, r1_b, r2_b)
    dqkv = jnp.concatenate([dq_g, dk_g, dv_g], axis=1)
    dzs_g, dalog_b, ddt_b = _tok_bwd("gdn_gate_bwd", _f_gdngate, gate_rows, gate_consts, [(dbeta_b, GDN_W), (dg_b, GDN_W)], 256, [0], [2, 3])
    dgqkv, dwconv = _tok_bwd("gdn_conv_bwd", _f_conv, conv_rows, conv_consts, [(dqkv, LANES)], t, [0], [0], ncb=12, with_j=True)
    dzs_f, dfb = _fox_gate_bwd(z0, fbias, dcq, dck, t)
    dzqk, dgq_t, dgk_t = _tok_bwd("fox_pre_bwd", _f_foxpre, qk_rows, qk_consts, [(dqn, FOX_W), (dkn, FOX_W)], 256, [0], [0, 1])
    dz0 = jnp.concatenate([dzqk, dfv, dgqkv, dgg, dzs_g + dzs_f, jnp.zeros((n, ZW0 - 3712), f32)], axis=1).astype(bf16)
    gs_c = [_col_slabs(_ungroup_w_in0(_mm("l0_in_dw", h0t, dz0, NN, bf16)), ncols["l0_w_in"])]
    r1_c = _rs_sibling(gs_c)
    pairs_c = pair(("l0_w_in",), gs_c, r1_c)
    dh0, r2_c = _mm("l0_in_dx", dz0, w_in0, NT, f32, side=_side_rs_chips(pairs_c))
    dx, g["l0_mix_norm"] = _tok_bwd("l0_mix_norm_bwd", _f_norm, [(x, D, 0)], [(g0m, None, 0)], [(dh0, D)], 256, [0], [0], addto={0: dx1})
    adam(("l0_w_in",), gs_c, r1_c, r2_c)
    g["l0_fox_q_norm"] = dgq_t.reshape(FOX_H, FOX_D).sum(0)
    g["l0_fox_k_norm"] = dgk_t.reshape(FOX_H, FOX_D).sum(0)
    g["l0_fox_f_bias"] = dfb[0, :FOX_H]
    g["l0_gdn_conv"] = dwconv
    g["l0_gdn_A_log"] = dalog_b.reshape(GDN_H, HD).sum(1)
    g["l0_gdn_dt_bias"] = ddt_b.reshape(GDN_H, HD).sum(1)
    g["l0_gdn_o_norm"] = don0.reshape(GDN_H, HD).sum(0)
    return loss_row, dx, g, big


_NAMES = ("l0_mix_norm", "l0_w_in", "l0_fox_q_norm", "l0_fox_k_norm", "l0_fox_f_bias", "l0_gdn_conv", "l0_gdn_A_log",
          "l0_gdn_dt_bias", "l0_gdn_o_norm", "l0_w_out", "l0_ffn_norm", "l0_w_ff1", "l0_w_ff2", "l1_mix_norm", "l1_w_in",
          "l1_hgrn_o_norm", "l1_w_out", "l1_ffn_norm", "l1_w_ff1", "l1_w_ff2", "hgrn_lb_logits")
_SMALL_NAMES = tuple(nm for nm in _NAMES if nm not in _BIG_NAMES)
_SMALL_ROWS = 16


def _pack_small(vals):
    flat = jnp.concatenate([vals[nm].reshape(-1).astype(f32) for nm in _SMALL_NAMES])
    return jnp.pad(flat, (0, _SMALL_ROWS * D - flat.shape[0])).reshape(_SMALL_ROWS, D)


def _unpack_small(packed, shapes):
    flat = packed.reshape(-1)
    out, off = {}, 0
    for nm in _SMALL_NAMES:
        size = 1
        for s in shapes[nm]:
            size *= s
        out[nm] = flat[off:off + size].reshape(shapes[nm])
        off += size
    return out, off


def kernel(x, l0_mix_norm, l0_w_in, l0_fox_q_norm, l0_fox_k_norm, l0_fox_f_bias, l0_gdn_conv, l0_gdn_A_log, l0_gdn_dt_bias, l0_gdn_o_norm, l0_w_out, l0_ffn_norm, l0_w_ff1, l0_w_ff2, l1_mix_norm, l1_w_in, l1_hgrn_o_norm, l1_w_out, l1_ffn_norm, l1_w_ff1, l1_w_ff2, hgrn_lb_logits, loss_target, m_l0_mix_norm, m_l0_w_in, m_l0_fox_q_norm, m_l0_fox_k_norm, m_l0_fox_f_bias, m_l0_gdn_conv, m_l0_gdn_A_log, m_l0_gdn_dt_bias, m_l0_gdn_o_norm, m_l0_w_out, m_l0_ffn_norm, m_l0_w_ff1, m_l0_w_ff2, m_l1_mix_norm, m_l1_w_in, m_l1_hgrn_o_norm, m_l1_w_out, m_l1_ffn_norm, m_l1_w_ff1, m_l1_w_ff2, m_hgrn_lb_logits, v_l0_mix_norm, v_l0_w_in, v_l0_fox_q_norm, v_l0_fox_k_norm, v_l0_fox_f_bias, v_l0_gdn_conv, v_l0_gdn_A_log, v_l0_gdn_dt_bias, v_l0_gdn_o_norm, v_l0_w_out, v_l0_ffn_norm, v_l0_w_ff1, v_l0_w_ff2, v_l1_mix_norm, v_l1_w_in, v_l1_hgrn_o_norm, v_l1_w_out, v_l1_ffn_norm, v_l1_w_ff1, v_l1_w_ff2, v_hgrn_lb_logits):
    args = dict(zip(_NAMES, (l0_mix_norm, l0_w_in, l0_fox_q_norm, l0_fox_k_norm, l0_fox_f_bias, l0_gdn_conv, l0_gdn_A_log, l0_gdn_dt_bias, l0_gdn_o_norm, l0_w_out, l0_ffn_norm, l0_w_ff1, l0_w_ff2, l1_mix_norm, l1_w_in, l1_hgrn_o_norm, l1_w_out, l1_ffn_norm, l1_w_ff1, l1_w_ff2, hgrn_lb_logits)))
    mom = dict(zip(_NAMES, (m_l0_mix_norm, m_l0_w_in, m_l0_fox_q_norm, m_l0_fox_k_norm, m_l0_fox_f_bias, m_l0_gdn_conv, m_l0_gdn_A_log, m_l0_gdn_dt_bias, m_l0_gdn_o_norm, m_l0_w_out, m_l0_ffn_norm, m_l0_w_ff1, m_l0_w_ff2, m_l1_mix_norm, m_l1_w_in, m_l1_hgrn_o_norm, m_l1_w_out, m_l1_ffn_norm, m_l1_w_ff1, m_l1_w_ff2, m_hgrn_lb_logits)))
    var = dict(zip(_NAMES, (v_l0_mix_norm, v_l0_w_in, v_l0_fox_q_norm, v_l0_fox_k_norm, v_l0_fox_f_bias, v_l0_gdn_conv, v_l0_gdn_A_log, v_l0_gdn_dt_bias, v_l0_gdn_o_norm, v_l0_w_out, v_l0_ffn_norm, v_l0_w_ff1, v_l0_w_ff2, v_l1_mix_norm, v_l1_w_in, v_l1_hgrn_o_norm, v_l1_w_out, v_l1_ffn_norm, v_l1_w_ff1, v_l1_w_ff2, v_hgrn_lb_logits)))
    nb, t, _ = x.shape
    dev = 4 * lax.axis_index("x") + 2 * lax.axis_index("y") + lax.axis_index("c")
    conv_cols = l0_gdn_conv.shape[1]
    loss_row, dx, g, big = _train(x.reshape(nb * t, D), loss_target.reshape(nb * t, D), args, mom, var, t)

    shapes = {nm: args[nm].shape for nm in _SMALL_NAMES}
    gsm = dict(g)
    gsm["l0_gdn_conv"] = jnp.zeros(shapes["l0_gdn_conv"], f32)
    packed = _pack_small(gsm)
    _, used = _unpack_small(packed, shapes)
    flat_extra = jnp.concatenate([jnp.sum(loss_row).reshape(1), g["l0_gdn_conv"].reshape(-1)])
    packed = packed.reshape(-1).at[used:used + flat_extra.shape[0]].set(flat_extra).reshape(_SMALL_ROWS, D)
    (parts,) = _allgather("ag_small", [packed])
    total = _sum_parts(parts).reshape(-1)
    loss = 0.5 * total[used] / D
    conv_g_full = total[used + 1:used + 1 + 4 * NDEV * conv_cols].reshape(4, NDEV * conv_cols)
    conv_g = lax.dynamic_slice(conv_g_full, (0, dev * conv_cols), (4, conv_cols))
    own_vals = {nm: jnp.zeros(shapes[nm], f32) for nm in _SMALL_NAMES}
    own_vals["l0_gdn_conv"] = conv_g
    own_mask = {nm: jnp.zeros(shapes[nm], f32) for nm in _SMALL_NAMES}
    own_mask["l0_gdn_conv"] = jnp.ones(shapes["l0_gdn_conv"], f32)
    small = _adamw_small(parts, _pack_small(args), _pack_small(mom), _pack_small(var), _pack_small(own_mask), _pack_small(own_vals))
    small = [_unpack_small(a, shapes)[0] for a in small]
    small[0]["l0_gdn_conv"] = conv_g

    outs = [loss, dx.reshape(nb, t, D)]
    for k in range(4):
        outs += [big[nm][k] if nm in _BIG_NAMES else small[k][nm] for nm in _NAMES]
    return tuple(outs)
```

```python
import functools

import jax
import jax.numpy as jnp
from jax import lax
from jax.experimental import pallas as pl
from jax.experimental.pallas import tpu as pltpu

f32, bf16 = jnp.float32, jnp.bfloat16
NN = (((1,), (0,)), ((), ()))
NT = (((1,), (1,)), ((), ()))
TN = (((0,), (0,)), ((), ()))
HI = lax.Precision.HIGHEST
MESH = pl.DeviceIdType.MESH
S = jax.ShapeDtypeStruct

EPS = 1e-6
D = 1024
LANES = 128
FOX_H, FOX_D, FOX_W = 8, 64, 512
GDN_H, HD, GDN_W = 4, 128, 512
HG_H = 8
CH = 64
ZW0 = 3840
NDEV = 8
ADAM_LR, ADAM_B1, ADAM_B2, ADAM_EPS, ADAM_WD, ADAM_STEP = 0.001, 0.9, 0.999, 1e-08, 0.01, 10

Z0_FQK, Z0_FV, Z0_GQKV, Z0_GG, Z0_SMALL = 0, 8, 12, 24, 28


def _dot(a, b, dims=NN, prec=None):
    return lax.dot_general(a, b, dims, precision=prec, preferred_element_type=f32)


def _iota2(shape, axis):
    return lax.broadcasted_iota(jnp.int32, shape, axis)


def _split3(x):
    x1 = x.astype(bf16)
    r = x - x1.astype(f32)
    x2 = r.astype(bf16)
    return x1, x2, (r - x2.astype(f32)).astype(bf16)


def _dot_sel(a, b, dims=NN, exact_lhs=False):
    if exact_lhs:
        return sum(_dot(a.astype(bf16), piece, dims) for piece in _split3(b))
    return sum(_dot(piece, b.astype(bf16), dims) for piece in _split3(a))


class _Side:
    def __init__(self, ins, out_shapes, scratch, start, finish):
        self.ins, self.out_shapes, self.scratch, self.start, self.finish = list(ins), list(out_shapes), list(scratch), start, finish


def _join_sides(sides):
    def split(refs, counts):
        out, off = [], 0
        for c in counts:
            out.append(refs[off:off + c])
            off += c
        return out

    ni, no, ns = ([len(getattr(sd, a)) for sd in sides] for a in ("ins", "out_shapes", "scratch"))

    def run(which):
        def go(ins, outs, sems):
            for sd, i, o, c in zip(sides, split(ins, ni), split(outs, no), split(sems, ns)):
                getattr(sd, which)(i, o, c)
        return go

    return _Side(sum((sd.ins for sd in sides), []), sum((sd.out_shapes for sd in sides), []),
                 sum((sd.scratch for sd in sides), []), run("start"), run("finish"))


def _pcall(side, body, *, name, grid, in_specs, out_specs, out_shape, scratch_shapes=(), compiler_params=None):
    if side is None:
        return pl.pallas_call(body, name=name, grid=grid, in_specs=in_specs, out_specs=out_specs, out_shape=out_shape,
                              scratch_shapes=scratch_shapes, compiler_params=compiler_params)
    single = not isinstance(out_shape, (list, tuple))
    ospecs, oshape = ([out_specs], [out_shape]) if single else (list(out_specs), list(out_shape))
    nin, nout, nscr = len(in_specs), len(ospecs), len(scratch_shapes)
    si, so = len(side.ins), len(side.out_shapes)

    def wrapped(*refs):
        o0 = nin + si
        c0 = o0 + nout + so
        sins, souts, ssems = refs[nin:o0], refs[o0 + nout:c0], refs[c0 + nscr:]
        ids = [pl.program_id(a) for a in range(len(grid))]
        first = functools.reduce(jnp.logical_and, [i == 0 for i in ids])
        last = functools.reduce(jnp.logical_and, [i == g - 1 for i, g in zip(ids, grid)])

        @pl.when(first)
        def _():
            side.start(sins, souts, ssems)

        body(*refs[:nin], *refs[o0:o0 + nout], *refs[c0:c0 + nscr])

        @pl.when(last)
        def _():
            side.finish(sins, souts, ssems)

    call = pl.pallas_call(
        wrapped, name=name, grid=grid, in_specs=list(in_specs) + _hbm_specs(si), out_specs=ospecs + _hbm_specs(so),
        out_shape=oshape + side.out_shapes, scratch_shapes=list(scratch_shapes) + side.scratch,
        compiler_params=pltpu.CompilerParams(dimension_semantics=("arbitrary",) * len(grid),
                                             vmem_limit_bytes=getattr(compiler_params, "vmem_limit_bytes", None)))

    def run(*args):
        res = call(*args, *side.ins)
        return (res[0] if single else list(res[:nout])), list(res[nout:])

    return run


def _tok_specs(rows, consts, tm):
    specs = []
    for (_, w, base) in rows:
        specs.append(pl.BlockSpec((tm, w), functools.partial(lambda j, i, b: (i, b + j), b=base)))
    for (arr, w, base) in consts:
        if w is None:
            specs.append(pl.BlockSpec(arr.shape, lambda j, i: (0, 0)))
        else:
            specs.append(pl.BlockSpec((arr.shape[0], w), functools.partial(lambda j, i, b: (0, b + j), b=base)))
    return specs


def _tok_fwd(name, f, rows, consts, outs, tm, ncb=1, with_j=False, also_t=(), side=None):
    n = rows[0][0].shape[0]
    nin = len(rows) + len(consts)
    nout = len(outs)

    def body(*refs):
        ins = [r[...] for r in refs[:nin]]
        vals = f(pl.program_id(0), *ins) if with_j else f(*ins)
        for r, v in zip(refs[nin:nin + nout], vals):
            r[...] = v.astype(r.dtype)
        for r, k in zip(refs[nin + nout:], also_t):
            r[...] = vals[k].T.astype(r.dtype)

    return _pcall(
        side, body, name=name, grid=(ncb, n // tm),
        in_specs=_tok_specs(rows, consts, tm),
        out_specs=[pl.BlockSpec((tm, w), lambda j, i: (i, j)) for (w, _) in outs]
        + [pl.BlockSpec((outs[k][0], tm), lambda j, i: (0, i)) for k in also_t],
        out_shape=[S((n, w * ncb), dt) for (w, dt) in outs] + [S((outs[k][0], n), bf16) for k in also_t],
        compiler_params=pltpu.CompilerParams(dimension_semantics=("parallel", "parallel")),
    )(*[r[0] for r in rows], *[c[0] for c in consts])


def _tok_bwd(name, f, rows, consts, cots, tm, drow, dconst, ncb=1, with_j=False, addto=None, also_t=(), drow_dtype=f32, side=None):
    n = rows[0][0].shape[0]
    nr, nc, nct = len(rows), len(consts), len(cots)
    addto = addto or {}
    add_keys = sorted(addto)
    nadd = len(add_keys)

    def body(*refs):
        ins = [r[...] for r in refs[:nr + nc]]
        cot = [r[...] for r in refs[nr + nc:nr + nc + nct]]
        adds = refs[nr + nc + nct:nr + nc + nct + nadd]
        outs = refs[nr + nc + nct + nadd:]
        pos = list(drow) + [nr + k for k in dconst]

        def g(*dargs):
            full = list(ins)
            for p, a in zip(pos, dargs):
                full[p] = a
            return tuple(f(pl.program_id(0), *full) if with_j else f(*full))

        vals, vjp = jax.vjp(g, *[ins[p] for p in pos])
        grads = vjp(tuple(c.astype(v.dtype) for c, v in zip(cot, vals)))
        for k in range(len(drow)):
            gk = grads[k]
            if k in addto:
                gk = gk + adds[add_keys.index(k)][...]
            outs[k][...] = gk.astype(outs[k].dtype)
            if k in also_t:
                tref = outs[len(drow) + len(dconst) + list(also_t).index(k)]
                tref[...] = gk.T.astype(tref.dtype)
        first = pl.program_id(1) == 0
        for k in range(len(dconst)):
            ref = outs[len(drow) + k]

            @pl.when(first)
            def _():
                ref[...] = jnp.zeros_like(ref)

            ref[...] += grads[len(drow) + k]

    in_specs = _tok_specs(rows, consts, tm)
    in_specs += [pl.BlockSpec((tm, w), lambda j, i: (i, j)) for (_, w) in cots]
    in_specs += [pl.BlockSpec((tm, rows[drow[k]][1]), lambda j, i: (i, j)) for k in add_keys]
    out_specs = [pl.BlockSpec((tm, rows[k][1]), lambda j, i: (i, j)) for k in drow]
    out_shape = [S((n, rows[k][1] * ncb), drow_dtype) for k in drow]
    for k in dconst:
        arr, w, _ = consts[k]
        if w is None:
            out_specs.append(pl.BlockSpec(arr.shape, lambda j, i: (0, 0)))
            out_shape.append(S(arr.shape, f32))
        else:
            out_specs.append(pl.BlockSpec((arr.shape[0], w), lambda j, i: (0, j)))
            out_shape.append(S((arr.shape[0], w * ncb), f32))
    for k in also_t:
        out_specs.append(pl.BlockSpec((rows[drow[k]][1], tm), lambda j, i: (0, i)))
        out_shape.append(S((rows[drow[k]][1], n), bf16))
    return _pcall(
        side, body, name=name, grid=(ncb, n // tm), in_specs=in_specs, out_specs=out_specs, out_shape=out_shape,
        compiler_params=pltpu.CompilerParams(dimension_semantics=("parallel", "arbitrary")),
    )(*[r[0] for r in rows], *[c[0] for c in consts], *[c[0] for c in cots], *[addto[k] for k in add_keys])


def _scan_fwd(name, f, rows, nh, nchunk, side=None):
    n = rows[0][0].shape[0]
    nb = n // (CH * nchunk)
    nin = len(rows)
    w = nh * HD

    def body(*refs):
        o_ref, hist_ref, st = refs[nin], refs[nin + 1], refs[nin + 2]

        @pl.when(pl.program_id(0) == 0)
        def _():
            st[...] = jnp.zeros_like(st)

        s0 = st[...]
        hist_ref[0] = s0
        o, s1 = f(*[r[...].reshape(nb * CH, w) for r in refs[:nin]], s0)
        o_ref[...] = o.reshape(nb, CH, w)
        st[...] = s1

    seq3 = lambda a: a.reshape(nb, nchunk * CH, a.shape[1])
    res = _pcall(
        side, body, name=name, grid=(nchunk,),
        in_specs=[pl.BlockSpec((nb, CH, w), functools.partial(lambda c, base: (0, c, base), base=b)) for (_, b) in rows],
        out_specs=[pl.BlockSpec((nb, CH, w), lambda c: (0, c, 0)), pl.BlockSpec((1, nb * w, HD), lambda c: (c, 0, 0))],
        out_shape=[S((nb, nchunk * CH, w), f32), S((nchunk, nb * w, HD), f32)],
        scratch_shapes=[pltpu.VMEM((nb * w, HD), f32)],
        compiler_params=pltpu.CompilerParams(dimension_semantics=("arbitrary",)),
    )(*[seq3(r[0]) for r in rows])
    (o, hist), extra = (res, None) if side is None else res
    out = [o.reshape(n, w), hist]
    return out if side is None else (out, extra)


def _scan_bwd(name, f, rows, hist, do, nh, nchunk, side=None):
    n = rows[0][0].shape[0]
    nb = n // (CH * nchunk)
    nin = len(rows)
    w = nh * HD

    def body(*refs):
        hist_ref, do_ref = refs[nin], refs[nin + 1]
        outs = refs[nin + 2:nin + 2 + nin]
        ds = refs[nin + 2 + nin]

        @pl.when(pl.program_id(0) == 0)
        def _():
            ds[...] = jnp.zeros_like(ds)

        _, vjp = jax.vjp(f, *[r[...].reshape(nb * CH, w) for r in refs[:nin]], hist_ref[0])
        grads = vjp((do_ref[...].reshape(nb * CH, w), ds[...]))
        for r, gk in zip(outs, grads[:nin]):
            r[...] = gk.reshape(nb, CH, w)
        ds[...] = grads[nin]

    seq3 = lambda a: a.reshape(nb, nchunk * CH, a.shape[1])
    rev = lambda c, base: (0, nchunk - 1 - c, base)
    res = _pcall(
        side, body, name=name, grid=(nchunk,),
        in_specs=[pl.BlockSpec((nb, CH, w), functools.partial(rev, base=b)) for (_, b) in rows]
        + [pl.BlockSpec((1, nb * w, HD), lambda c: (nchunk - 1 - c, 0, 0)), pl.BlockSpec((nb, CH, w), functools.partial(rev, base=0))],
        out_specs=[pl.BlockSpec((nb, CH, w), functools.partial(rev, base=0)) for _ in rows],
        out_shape=[S((nb, nchunk * CH, w), f32) for _ in rows],
        scratch_shapes=[pltpu.VMEM((nb * w, HD), f32)],
        compiler_params=pltpu.CompilerParams(dimension_semantics=("arbitrary",)),
    )(*[seq3(r[0]) for r in rows], hist, seq3(do))
    outs, extra = (res, None) if side is None else res
    outs = [o.reshape(n, w) for o in outs]
    return outs if side is None else (outs, extra)


_VMEM_LIMIT = 56 * 2 ** 20
_VMEM_TILE_BUDGET = 40 * 2 ** 20


def _mm_tiles(m, n, k, sa, sb, so, sx, a_f32, b_f32, tn_fixed):
    best = None
    for tm in (1024, 512, 256, 128, 64):
        for tn in ((tn_fixed,) if tn_fixed else (1024, 768, 512, 384, 256, 128)):
            if m % tm or n % tn:
                continue
            need = 2 * (tm * k * sa + k * tn * sb + tm * tn * (so + sx)) + tm * tn * 4
            need += tm * k * (2 if sa == 4 else 0) + k * tn * (2 if sb == 4 else 0)
            need += tm * k * (4 if a_f32 else 0) + k * tn * (4 if b_f32 else 0)
            if need <= _VMEM_TILE_BUDGET and (best is None or (tm * tn, tm) > best[0]):
                best = ((tm * tn, tm), tm, tn)
    return best[1], best[2]


def _mm(name, a, b, dims, out_dtype, a_fn=None, b_fn=None, epi=None, extra=(), out_t=False, slab=False, side=None):
    m, kk = a.shape
    nn = b.shape[1] if dims is NN else b.shape[0]
    sx = sum(e.dtype.itemsize for e in extra)
    tm, tn = _mm_tiles(m, nn, kk, a.dtype.itemsize, b.dtype.itemsize, jnp.dtype(out_dtype).itemsize, sx,
                       a_fn is not None, b_fn is not None, _COLW if slab else None)
    nex = len(extra)

    def body(a_ref, b_ref, *rest):
        av, bv = a_ref[...], b_ref[...]
        if a_fn is not None:
            av = a_fn(av.astype(f32))
        if b_fn is not None:
            bv = b_fn(bv.astype(f32))
        acc = _dot(av.astype(bf16), bv.astype(bf16), dims)
        if epi is not None:
            acc = epi(acc, *[r[...] for r in rest[:nex]])
        o_ref = rest[nex]
        if out_t:
            o_ref[...] = acc.T.astype(o_ref.dtype)
        elif slab:
            o_ref[0] = acc.astype(o_ref.dtype)
        else:
            o_ref[...] = acc.astype(o_ref.dtype)

    bspec = pl.BlockSpec((kk, tn), lambda i, j: (0, j)) if dims is NN else pl.BlockSpec((tn, kk), lambda i, j: (j, 0))
    if out_t:
        out_spec, out_shape = pl.BlockSpec((tn, tm), lambda i, j: (j, i)), S((nn, m), out_dtype)
    elif slab:
        out_spec, out_shape = pl.BlockSpec((1, tm, tn), lambda i, j: (j, i, 0)), S((nn // tn, m, tn), out_dtype)
    else:
        out_spec, out_shape = pl.BlockSpec((tm, tn), lambda i, j: (i, j)), S((m, nn), out_dtype)
    return _pcall(
        side, body, name=name, grid=(m // tm, nn // tn),
        in_specs=[pl.BlockSpec((tm, kk), lambda i, j: (i, 0)), bspec]
        + [pl.BlockSpec((tm, tn), lambda i, j: (i, j)) for _ in extra],
        out_specs=out_spec, out_shape=out_shape,
        compiler_params=pltpu.CompilerParams(dimension_semantics=("parallel", "parallel"), vmem_limit_bytes=_VMEM_LIMIT),
    )(a, b, *extra)


def _f_norm(x, g):
    return (x * lax.rsqrt(jnp.mean(x * x, axis=-1, keepdims=True) + EPS) * g,)


def _f_foxpre(zqk, gq, gk, pm):
    def nrm(t, g):
        return t * lax.rsqrt(_dot_sel(t * t, pm) + EPS) * g
    return nrm(zqk[:, :FOX_W], gq), nrm(zqk[:, FOX_W:], gk)


def _chunk_cumsum(x):
    n = x.shape[0]
    r, c = _iota2((n, n), 0), _iota2((n, n), 1)
    tri = jnp.logical_and(r >= c, (r // CH) == (c // CH)).astype(f32)
    return _dot_sel(tri, x, exact_lhs=True)


def _f_gdngate(zs, eb, ea, alog_b, dt_b):
    beta = jax.nn.sigmoid(_dot_sel(zs, eb))
    la = -jnp.exp(alog_b) * jax.nn.softplus(_dot_sel(zs, ea) + dt_b)
    return beta, _chunk_cumsum(la)


def _f_conv(j, x, w):
    t = x.shape[0]
    y = x * w[3:4, :]
    for jj in range(3):
        sh = 3 - jj
        xs = jnp.concatenate([jnp.zeros((sh, x.shape[1]), f32), x[:t - sh, :]], axis=0)
        y = y + xs * w[jj:jj + 1, :]
    y = jax.nn.silu(y)
    yn = y * lax.rsqrt(jnp.sum(y * y, axis=-1, keepdims=True) + EPS)
    return (jnp.where(j < 2 * GDN_H, yn, y),)


def _head_rms(o, nh):
    outs = []
    for h in range(nh):
        oh = o[:, HD * h:HD * (h + 1)]
        outs.append(oh * lax.rsqrt(jnp.mean(oh * oh, axis=-1, keepdims=True) + EPS))
    return jnp.concatenate(outs, axis=1)


def _f_post0(fox_o, o, gg, on):
    return (jnp.concatenate([fox_o, _head_rms(o, GDN_H) * on * jax.nn.silu(gg)], axis=1),)


def _f_post1(o, zg, on):
    return (_head_rms(o, HG_H) * on * jax.nn.silu(zg),)


def _f_hpre(zqf, lbl):
    lb = jax.nn.sigmoid(lbl[1:2, :] - lbl[0:1, :])
    fg = lb + (1.0 - lb) * jax.nn.sigmoid(zqf[:, D:])
    return jax.nn.silu(zqf[:, :D]), 1.0 - fg, _chunk_cumsum(jnp.log(fg))


def _dotb(a, b, dims=NN):
    return _dot(a.astype(bf16), b.astype(bf16), dims)


def _dot3(a, b):
    ah, bh = a.astype(bf16), b.astype(bf16)
    al, bl = (a - ah.astype(f32)).astype(bf16), (b - bh.astype(f32)).astype(bf16)
    return _dot(ah, bh) + (_dot(ah, bl) + _dot(al, bh))


def _split(t, nh):
    return [t[CH * ck:CH * (ck + 1), HD * h:HD * (h + 1)] for ck in range(t.shape[0] // CH) for h in range(nh)]


def _merge(units, nh):
    return jnp.concatenate([jnp.concatenate(units[i:i + nh], axis=1) for i in range(0, len(units), nh)], axis=0)


def _inv_impl(amats):
    n = amats[0].shape[0]
    eye = jnp.where(_iota2((n, n), 0) == _iota2((n, n), 1), 1.0, 0.0).astype(f32)
    xs, ps = [eye - a for a in amats], list(amats)
    for _ in range(max(1, (n - 1).bit_length()) - 1):
        ps = [_dotb(p, p) for p in ps]
        xs = [x + _dotb(x, p) for x, p in zip(xs, ps)]
    for _ in range(2):
        rs = [eye - x - _dot3(a, x) for a, x in zip(amats, xs)]
        xs = [x + _dotb(x, r) for x, r in zip(xs, rs)]
    return tuple(xs)


@jax.custom_vjp
def _inv_unit_lower(amats):
    return _inv_impl(amats)


def _inv_fwd(amats):
    xs = _inv_impl(amats)
    return xs, xs


def _inv_bwd(xs, dxs):
    return (tuple(-_dotb(_dotb(x, dx, TN), x, NT) for x, dx in zip(xs, dxs)),)


_inv_unit_lower.defvjp(_inv_fwd, _inv_bwd)


@jax.custom_vjp
def _inv_given(amats, xs):
    return xs


def _inv_given_fwd(amats, xs):
    return xs, xs


def _inv_given_bwd(xs, dxs):
    return _inv_bwd(xs, dxs)[0], tuple(jnp.zeros_like(x) for x in xs)


_inv_given.defvjp(_inv_given_fwd, _inv_given_bwd)


def _f_gdn_intra(q, k, v, bb, gb, tinv_p=None):
    qs, ks, vs, bs, gs = (_split(t, GDN_H) for t in (q, k, v, bb, gb))
    r, cc = _iota2((CH, CH), 0), _iota2((CH, CH), 1)
    causal, strict = r >= cc, r > cc
    beta, g, gl = [b[:, :1] for b in bs], [x[:, :1] for x in gs], [x[CH - 1:CH, :1] for x in gs]
    decay = [jnp.exp(jnp.where(causal, x[:, :CH] - x[:, :CH].T, -jnp.inf)) for x in gs]
    kb = [ki * bi for ki, bi in zip(ks, beta)]
    amat = [jnp.where(strict, _dotb(kbi, ki, NT) * di, 0.0) for kbi, ki, di in zip(kb, ks, decay)]
    if tinv_p is None:
        tinv = _inv_unit_lower(tuple(amat))
    else:
        tinv = _inv_given(tuple(amat), tuple(x[:, :CH] for x in _split(tinv_p, GDN_H)))
    rhs = [jnp.concatenate([vi * bi, kbi * jnp.exp(gi)], axis=1) for vi, bi, kbi, gi in zip(vs, beta, kb, g)]
    uw = [_dotb(ti, ri) for ti, ri in zip(tinv, rhs)]
    qsc = [qi * (HD ** -0.5) for qi in qs]
    qk = [jnp.where(causal, _dotb(qi, ki, NT) * di, 0.0) for qi, ki, di in zip(qsc, ks, decay)]
    outs = ([x[:, :HD] for x in uw], [x[:, HD:] for x in uw],
            [jnp.concatenate([x, jnp.zeros_like(x)], axis=1) for x in qk],
            [qi * jnp.exp(gi) for qi, gi in zip(qsc, g)],
            [ki * jnp.exp(gli - gi) for ki, gli, gi in zip(ks, gl, g)],
            [jnp.broadcast_to(gli, (CH, HD)) for gli in gl])
    if tinv_p is None:
        outs += ([jnp.concatenate([x, jnp.zeros_like(x)], axis=1) for x in tinv],)
    return tuple(_merge(o, GDN_H) for o in outs)


def _f_gdn_inter(u, w, qkp, qd, kd, glb, st):
    us, ws, qks, qds, kds, gls = (_split(t, GDN_H) for t in (u, w, qkp, qd, kd, glb))
    sts = [st[HD * i:HD * (i + 1), :] for i in range(len(us))]
    vn = [ui - _dotb(wi, si) for ui, wi, si in zip(us, ws, sts)]
    o = [_dotb(qi, si) + _dotb(xi[:, :CH], vi) for qi, si, xi, vi in zip(qds, sts, qks, vn)]
    s2 = [si * jnp.exp(gi[:1, :1]) + _dotb(ki, vi, TN) for si, gi, ki, vi in zip(sts, gls, kds, vn)]
    return _merge(o, GDN_H), jnp.concatenate(s2, axis=0)


def _f_hgrn_chunk(q, k, b, v, st):
    qs, ks, bs, vs = (_split(t, HG_H) for t in (q, k, b, v))
    sts = [st[HD * i:HD * (i + 1), :] for i in range(len(qs))]
    causal = _iota2((CH, CH), 0) >= _iota2((CH, CH), 1)
    bl, bm = [x[CH - 1:CH, :] for x in bs], [x[CH // 2 - 1:CH // 2, :] for x in bs]
    a = [jnp.where(causal, _dotb(qi * jnp.exp(bi - mi), ki * jnp.exp(mi - bi), NT), 0.0)
         for qi, ki, bi, mi in zip(qs, ks, bs, bm)]
    o = [_dotb(qi * jnp.exp(bi), si, NT) + _dotb(ai, vi) for qi, bi, si, ai, vi in zip(qs, bs, sts, a, vs)]
    s2 = [si * jnp.exp(li) + _dotb(vi, ki * jnp.exp(li - bi), TN) for si, li, vi, ki, bi in zip(sts, bl, vs, ks, bs)]
    return _merge(o, HG_H), jnp.concatenate(s2, axis=0)


def _fox_gate_fwd(z0, fbias, t, tc=256):
    n = z0.shape[0]
    nt = t // tc

    def body(zs_ref, b_ref, ccol_ref, crow_ref, carry):
        @pl.when(pl.program_id(1) == 0)
        def _():
            carry[...] = jnp.zeros_like(carry)

        ls = jnp.where(_iota2((tc, LANES), 1) < FOX_H, jax.nn.log_sigmoid(zs_ref[...] + b_ref[...]), 0.0)
        tri = (_iota2((tc, tc), 0) >= _iota2((tc, tc), 1)).astype(f32)
        c = _dot_sel(tri, ls, exact_lhs=True) + carry[...]
        carry[...] = c[tc - 1:tc, :]
        ccol_ref[...] = c
        crow_ref[0] = c.T[:FOX_H, :]

    return pl.pallas_call(
        body, name="fox_gate_fwd", grid=(n // t, nt),
        in_specs=[pl.BlockSpec((tc, LANES), lambda b, i: (b * nt + i, Z0_SMALL)), pl.BlockSpec((1, LANES), lambda b, i: (0, 0))],
        out_specs=[pl.BlockSpec((tc, LANES), lambda b, i: (b * nt + i, 0)), pl.BlockSpec((1, FOX_H, tc), lambda b, i: (b, 0, i))],
        out_shape=[S((n, LANES), f32), S((n // t, FOX_H, t), f32)],
        scratch_shapes=[pltpu.VMEM((1, LANES), f32)],
        compiler_params=pltpu.CompilerParams(dimension_semantics=("parallel", "arbitrary")),
    )(z0, fbias)


def _fox_gate_bwd(z0, fbias, dcq, dck, t, tc=256):
    n = z0.shape[0]
    nt = t // tc

    def body(zs_ref, b_ref, dcq_ref, dck_ref, dz_ref, db_ref, carry):
        first = jnp.logical_and(pl.program_id(0) == 0, pl.program_id(1) == 0)

        @pl.when(pl.program_id(1) == 0)
        def _():
            carry[...] = jnp.zeros_like(carry)

        @pl.when(first)
        def _():
            db_ref[...] = jnp.zeros_like(db_ref)

        dc = dcq_ref[0] + dcq_ref[1] + dcq_ref[2] + dcq_ref[3]
        drow = dck_ref[0, 0] + dck_ref[1, 0] + dck_ref[2, 0] + dck_ref[3, 0]
        eye = (_iota2((FOX_H, LANES), 0) == _iota2((FOX_H, LANES), 1)).astype(f32)
        dc = dc + _dot_sel(drow, eye, TN)
        triu = (_iota2((tc, tc), 0) <= _iota2((tc, tc), 1)).astype(f32)
        dls = _dot_sel(triu, dc, exact_lhs=True) + carry[...]
        carry[...] = dls[0:1, :]
        x = zs_ref[...] + b_ref[...]
        dz = jnp.where(_iota2((tc, LANES), 1) < FOX_H, dls * jax.nn.sigmoid(-x), 0.0)
        dz_ref[...] = dz
        db_ref[...] += jnp.sum(dz, axis=0, keepdims=True)

    def rev(b, i):
        return b * nt + (nt - 1 - i)

    return pl.pallas_call(
        body, name="fox_gate_bwd", grid=(n // t, nt),
        in_specs=[pl.BlockSpec((tc, LANES), lambda b, i: (rev(b, i), Z0_SMALL)), pl.BlockSpec((1, LANES), lambda b, i: (0, 0)),
                  pl.BlockSpec((4, tc, LANES), lambda b, i: (0, rev(b, i), 0)),
                  pl.BlockSpec((4, 1, FOX_H, tc), lambda b, i: (0, b, 0, nt - 1 - i))],
        out_specs=[pl.BlockSpec((tc, LANES), lambda b, i: (rev(b, i), 0)), pl.BlockSpec((1, LANES), lambda b, i: (0, 0))],
        out_shape=[S((n, LANES), f32), S((1, LANES), f32)],
        scratch_shapes=[pltpu.VMEM((1, LANES), f32)],
        compiler_params=pltpu.CompilerParams(dimension_semantics=("arbitrary", "arbitrary")),
    )(z0, fbias, dcq, dck)


def _fox_scores(hh, p, i, tq, q, k, ccol, crow):
    kmax = k.shape[0]
    lane = _iota2((1, LANES), 1)
    mh = (lane // FOX_D) == hh
    h = 2 * p + hh
    qh = jnp.where(mh, q, 0.0).astype(bf16)
    s = _dot(qh, k, NT) * (FOX_D ** -0.5)
    cq = jnp.sum(jnp.where(lane == h, ccol, 0.0), axis=1, keepdims=True)
    ck = jnp.sum(jnp.where(_iota2((FOX_H, 1), 0) == h, crow, 0.0), axis=0, keepdims=True)
    causal = _iota2((1, kmax), 1) <= (i * tq + _iota2((tq, 1), 0))
    s = jnp.where(causal, s + cq - ck, -jnp.inf)
    pe = jnp.exp(s - jnp.max(s, axis=1, keepdims=True))
    return mh, qh, pe, jnp.sum(pe, axis=1, keepdims=True)


def _fox_attn_fwd(qn, kn, z0, ccol, crow, t, tq=256, side=None):
    n = qn.shape[0]
    nq = t // tq

    def body(q_ref, k_ref, v_ref, ccol_ref, crow_ref, o_ref):
        p = pl.program_id(1)
        k, v, crow = k_ref[...].astype(bf16), v_ref[...].astype(bf16), crow_ref[0]
        for i in range(nq):
            rows, kmax = pl.ds(i * tq, tq), (i + 1) * tq
            q, cc = q_ref[rows, :], ccol_ref[rows, :]
            acc = jnp.zeros((tq, LANES), f32)
            for hh in range(2):
                mh, _, pe, l = _fox_scores(hh, p, i, tq, q, k[:kmax], cc, crow[:, :kmax])
                acc = jnp.where(mh, _dot(pe.astype(bf16), v[:kmax]) / l, acc)
            o_ref[rows, :] = acc

    seq = lambda b, p: (b, p)
    return _pcall(
        side, body, name="fox_attn_fwd", grid=(n // t, FOX_H // 2),
        in_specs=[pl.BlockSpec((t, LANES), seq), pl.BlockSpec((t, LANES), seq), pl.BlockSpec((t, LANES), lambda b, p: (b, Z0_FV + p)),
                  pl.BlockSpec((t, LANES), lambda b, p: (b, 0)), pl.BlockSpec((1, FOX_H, t), lambda b, p: (b, 0, 0))],
        out_specs=pl.BlockSpec((t, LANES), seq),
        out_shape=S((n, FOX_W), f32),
        compiler_params=pltpu.CompilerParams(dimension_semantics=("parallel", "parallel")),
    )(qn, kn, z0, ccol, crow)


def _fox_attn_bwd(qn, kn, z0, ccol, crow, do, t, tq=256, side=None):
    n = qn.shape[0]
    nq = t // tq
    nb = n // t

    def body(q_ref, k_ref, v_ref, ccol_ref, crow_ref, do_ref, dq_ref, dk_ref, dv_ref, dcq_ref, dck_ref):
        p = pl.program_id(1)
        dk_ref[...] = jnp.zeros_like(dk_ref)
        dv_ref[...] = jnp.zeros_like(dv_ref)
        dck_ref[...] = jnp.zeros_like(dck_ref)
        kf, v, crow = k_ref[...], v_ref[...].astype(bf16), crow_ref[0]
        k = kf.astype(bf16)
        lane = _iota2((1, LANES), 1)
        sub = _iota2((FOX_H, 1), 0)
        scale = FOX_D ** -0.5
        for i in range(nq):
            rows, kmax = pl.ds(i * tq, tq), (i + 1) * tq
            q, cc, dout = q_ref[rows, :], ccol_ref[rows, :], do_ref[rows, :]
            dq = jnp.zeros((tq, LANES), f32)
            dcq = jnp.zeros((tq, LANES), f32)
            for hh in range(2):
                mh, qh, pe, l = _fox_scores(hh, p, i, tq, q, k[:kmax], cc, crow[:, :kmax])
                pr = pe / l
                doh = jnp.where(mh, dout, 0.0).astype(bf16)
                dp = _dot(doh, v[:kmax], NT)
                ds = pr * (dp - jnp.sum(pr * dp, axis=1, keepdims=True))
                dsb = ds.astype(bf16)
                dq = dq + _dot(dsb, jnp.where(mh, kf[:kmax], 0.0).astype(bf16)) * scale
                dk_ref[:kmax, :] += _dot(dsb, qh, TN) * scale
                dv_ref[:kmax, :] += _dot(pr.astype(bf16), doh, TN)
                h = 2 * p + hh
                dcq = dcq + jnp.where(lane == h, jnp.sum(ds, axis=1, keepdims=True), 0.0)
                dck_ref[0, 0, :, :kmax] += jnp.where(sub == h, -jnp.sum(ds, axis=0, keepdims=True), 0.0)
            dq_ref[rows, :] = dq
            dcq_ref[0, rows, :] = dcq

    seq = lambda b, p: (b, p)
    return _pcall(
        side, body, name="fox_attn_bwd", grid=(nb, FOX_H // 2),
        in_specs=[pl.BlockSpec((t, LANES), seq), pl.BlockSpec((t, LANES), seq), pl.BlockSpec((t, LANES), lambda b, p: (b, Z0_FV + p)),
                  pl.BlockSpec((t, LANES), lambda b, p: (b, 0)), pl.BlockSpec((1, FOX_H, t), lambda b, p: (b, 0, 0)),
                  pl.BlockSpec((t, LANES), seq)],
        out_specs=[pl.BlockSpec((t, LANES), seq), pl.BlockSpec((t, LANES), seq), pl.BlockSpec((t, LANES), seq),
                   pl.BlockSpec((1, t, LANES), lambda b, p: (p, b, 0)), pl.BlockSpec((1, 1, FOX_H, t), lambda b, p: (p, b, 0, 0))],
        out_shape=[S((n, FOX_W), f32), S((n, FOX_W), f32), S((n, FOX_W), f32), S((4, n, LANES), f32), S((4, nb, FOX_H, t), f32)],
        compiler_params=pltpu.CompilerParams(dimension_semantics=("parallel", "parallel")),
    )(qn, kn, z0, ccol, crow, do)


def _loss_head(y, tgt, tm=256):
    n = y.shape[0]

    def body(y_ref, t_ref, dy_ref, l_ref, dyt_ref):
        @pl.when(pl.program_id(0) == 0)
        def _():
            l_ref[...] = jnp.zeros_like(l_ref)

        e = y_ref[...] - t_ref[...]
        dy = e * (1.0 / D)
        dy_ref[...] = dy
        dyt_ref[...] = dy.T.astype(bf16)
        l_ref[...] += jnp.sum(e * e, axis=0, keepdims=True)

    return pl.pallas_call(
        body, name="loss_head", grid=(n // tm,),
        in_specs=[pl.BlockSpec((tm, D), lambda i: (i, 0))] * 2,
        out_specs=[pl.BlockSpec((tm, D), lambda i: (i, 0)), pl.BlockSpec((1, D), lambda i: (0, 0)),
                   pl.BlockSpec((D, tm), lambda i: (0, i))],
        out_shape=[S((n, D), f32), S((1, D), f32), S((D, n), bf16)],
        compiler_params=pltpu.CompilerParams(dimension_semantics=("arbitrary",)),
    )(y, tgt)


def _adamw_math(w, g, m, v):
    m = ADAM_B1 * m + (1.0 - ADAM_B1) * g
    v = ADAM_B2 * v + (1.0 - ADAM_B2) * (g * g)
    m_hat = m / (1.0 - ADAM_B1 ** ADAM_STEP)
    v_hat = v / (1.0 - ADAM_B2 ** ADAM_STEP)
    return -ADAM_LR * (m_hat / (jnp.sqrt(v_hat) + ADAM_EPS) + ADAM_WD * w), m, v


def _adamw_big(name, idx, gmine, recv1, recv2, w, m, v):
    r, wc = w.shape
    c = gmine.shape[2]
    tr = min(r, 256)

    def body(idx_ref, gm_ref, r1_ref, r2_ref, w_ref, m_ref, v_ref, g_ref, d_ref, nm_ref, nv_ref):
        g = gm_ref[0].astype(f32) + r1_ref[0].astype(f32)
        for k in range(3):
            g = g + r2_ref[k].astype(f32)
        g = g[:, :wc]
        d, nm, nv = _adamw_math(w_ref[...], g, m_ref[...], v_ref[...])
        g_ref[...] = g
        d_ref[...] = d
        nm_ref[...] = nm
        nv_ref[...] = nv

    row = pl.BlockSpec((tr, wc), lambda i, s: (i, 0))
    return pl.pallas_call(
        body, name=name,
        grid_spec=pltpu.PrefetchScalarGridSpec(
            num_scalar_prefetch=1, grid=(r // tr,),
            in_specs=[pl.BlockSpec((1, tr, c), lambda i, s: (s[0], i, 0)), pl.BlockSpec((1, tr, c), lambda i, s: (s[1], i, 0)),
                      pl.BlockSpec((3, tr, c), lambda i, s: (0, i, 0)), row, row, row],
            out_specs=[row, row, row, row]),
        out_shape=[S((r, wc), f32)] * 4,
        compiler_params=pltpu.CompilerParams(dimension_semantics=("parallel",)),
    )(idx, gmine, recv1, recv2, w, m, v)


def _pair_sum(name, idx, gmine, recv1):
    _, r, c = gmine.shape
    g4 = gmine.reshape(4, 2, r, c)

    def body(idx_ref, gm_ref, r1_ref, o_ref):
        o_ref[0] = (gm_ref[0, 0].astype(f32) + r1_ref[0].astype(f32)).astype(bf16)

    return pl.pallas_call(
        body, name=name,
        grid_spec=pltpu.PrefetchScalarGridSpec(
            num_scalar_prefetch=1, grid=(4,),
            in_specs=[pl.BlockSpec((1, 1, r, c), lambda ch, s: (ch, s[0], 0, 0)), pl.BlockSpec((1, r, c), lambda ch, s: (ch, 0, 0))],
            out_specs=pl.BlockSpec((1, r, c), lambda ch, s: (ch, 0, 0))),
        out_shape=S((4, r, c), bf16),
        compiler_params=pltpu.CompilerParams(dimension_semantics=("parallel",)),
    )(idx, g4, recv1)


def _adamw_small(parts, w, m, v, own_mask, own_g):
    def body(p_ref, w_ref, m_ref, v_ref, mask_ref, og_ref, g_ref, d_ref, nm_ref, nv_ref):
        g = p_ref[0]
        for k in range(1, NDEV):
            g = g + p_ref[k]
        g_ref[...] = g
        ge = jnp.where(mask_ref[...] > 0.5, og_ref[...], g)
        d, nm, nv = _adamw_math(w_ref[...], ge, m_ref[...], v_ref[...])
        d_ref[...] = d
        nm_ref[...] = nm
        nv_ref[...] = nv

    return pl.pallas_call(body, name="adamw_small", out_shape=[S(w.shape, f32)] * 4)(parts, w, m, v, own_mask, own_g)


def _sum_parts(parts):
    def body(p_ref, g_ref):
        g = p_ref[0]
        for k in range(1, NDEV):
            g = g + p_ref[k]
        g_ref[...] = g

    return pl.pallas_call(body, name="sum_parts", out_shape=S(parts.shape[1:], f32))(parts)


def _me():
    return lax.axis_index("x"), lax.axis_index("y"), lax.axis_index("c")


def _hbm_specs(n):
    return [pl.BlockSpec(memory_space=pl.ANY)] * n


def _allgather(name, xs):
    na = len(xs)

    def body(*refs):
        x_refs, out_refs = refs[:na], refs[na:2 * na]
        send_sems, recv_sems, local_sems = refs[2 * na:]
        mx, my, mc = _me()
        me, sib = (mx, my, mc), (mx, my, 1 - mc)
        chips = [(1 - mx, my), (mx, 1 - my), (1 - mx, 1 - my)]

        def slab(a, px, py, pc):
            return out_refs[a].at[4 * px + 2 * py + pc]

        def copy(a, k, block, to, own=False):
            return pltpu.make_async_remote_copy(
                src_ref=x_refs[a] if own else slab(a, *block), dst_ref=slab(a, *block),
                send_sem=send_sems.at[7 * a + k], recv_sem=recv_sems.at[7 * a + k], device_id=to, device_id_type=MESH)

        mine = [pltpu.make_async_copy(x_refs[a], slab(a, *me), local_sems.at[a]) for a in range(na)]
        first = []
        for a in range(na):
            mine[a].start()
            first += [copy(a, 0, me, sib, own=True)] + [copy(a, 1 + j, me, (*chip, mc), own=True) for j, chip in enumerate(chips)]
        for cp in first:
            cp.start()
        passed = []
        for j, chip in enumerate(chips):
            for a in range(na):
                copy(a, 1 + j, (*chip, mc), me).wait_recv()
                passed.append(copy(a, 4 + j, (*chip, mc), sib))
                passed[-1].start()
        for a in range(na):
            copy(a, 0, sib, me).wait_recv()
            for j, chip in enumerate(chips):
                copy(a, 4 + j, (*chip, 1 - mc), me).wait_recv()
        for cp in first + passed:
            cp.wait_send()
        for cp in mine:
            cp.wait()

    return pl.pallas_call(
        body, name=name, out_shape=[S((NDEV,) + x.shape, x.dtype) for x in xs],
        in_specs=_hbm_specs(na), out_specs=_hbm_specs(na),
        scratch_shapes=[pltpu.SemaphoreType.DMA((7 * na,)), pltpu.SemaphoreType.DMA((7 * na,)), pltpu.SemaphoreType.DMA((na,))],
    )(*xs)


def _rs_sibling(gs):
    na = len(gs)

    def body(*refs):
        g_refs, out_refs, send_sems, recv_sems = refs[:na], refs[na:2 * na], refs[2 * na], refs[2 * na + 1]
        mx, my, mc = _me()
        cps = [pltpu.make_async_remote_copy(
            src_ref=g_refs[a].at[2 * ch + 1 - mc], dst_ref=out_refs[a].at[ch], send_sem=send_sems.at[4 * a + ch],
            recv_sem=recv_sems.at[4 * a + ch], device_id=(mx, my, 1 - mc), device_id_type=MESH)
            for a in range(na) for ch in range(4)]
        for cp in cps:
            cp.start()
        for cp in cps:
            cp.wait_recv()
        for cp in cps:
            cp.wait_send()

    return pl.pallas_call(
        body, name="rs_sibling", out_shape=[S((4,) + g.shape[1:], g.dtype) for g in gs],
        in_specs=_hbm_specs(na), out_specs=_hbm_specs(na),
        scratch_shapes=[pltpu.SemaphoreType.DMA((4 * na,)), pltpu.SemaphoreType.DMA((4 * na,))],
    )(*gs)


def _side_allgather(xs):
    na = len(xs)

    def mk(x_refs, out_refs, sems):
        send_sems, recv_sems, local_sems = sems
        mx, my, mc = _me()
        me, sib = (mx, my, mc), (mx, my, 1 - mc)
        chips = [(1 - mx, my), (mx, 1 - my), (1 - mx, 1 - my)]

        def slab(a, px, py, pc):
            return out_refs[a].at[4 * px + 2 * py + pc]

        def copy(a, k, block, to, own=False):
            return pltpu.make_async_remote_copy(
                src_ref=x_refs[a] if own else slab(a, *block), dst_ref=slab(a, *block),
                send_sem=send_sems.at[7 * a + k], recv_sem=recv_sems.at[7 * a + k], device_id=to, device_id_type=MESH)

        mine = [pltpu.make_async_copy(x_refs[a], slab(a, *me), local_sems.at[a]) for a in range(na)]
        first = []
        for a in range(na):
            first += [copy(a, 0, me, sib, own=True)] + [copy(a, 1 + j, me, (*chip, mc), own=True) for j, chip in enumerate(chips)]
        return me, sib, chips, mc, copy, mine, first

    def start(x_refs, out_refs, sems):
        *_, mine, first = mk(x_refs, out_refs, sems)
        for cp in mine + first:
            cp.start()

    def finish(x_refs, out_refs, sems):
        me, sib, chips, mc, copy, mine, first = mk(x_refs, out_refs, sems)
        passed = []
        for j, chip in enumerate(chips):
            for a in range(na):
                copy(a, 1 + j, (*chip, mc), me).wait_recv()
                passed.append(copy(a, 4 + j, (*chip, mc), sib))
                passed[-1].start()
        for a in range(na):
            copy(a, 0, sib, me).wait_recv()
            for j, chip in enumerate(chips):
                copy(a, 4 + j, (*chip, 1 - mc), me).wait_recv()
        for cp in first + passed:
            cp.wait_send()
        for cp in mine:
            cp.wait()

    scratch = [pltpu.SemaphoreType.DMA((7 * na,)), pltpu.SemaphoreType.DMA((7 * na,)), pltpu.SemaphoreType.DMA((na,))]
    return _Side(xs, [S((NDEV,) + x.shape, x.dtype) for x in xs], scratch, start, finish)


def _side_exchange(arrs, nslot, out_slots, route):
    na = len(arrs)

    def copies(in_refs, out_refs, sems):
        send_sems, recv_sems = sems
        return [pltpu.make_async_remote_copy(
            src_ref=in_refs[a].at[src], dst_ref=out_refs[a].at[k], send_sem=send_sems.at[nslot * a + k],
            recv_sem=recv_sems.at[nslot * a + k], device_id=to, device_id_type=MESH)
            for a in range(na) for k, (src, to) in enumerate(route(*_me()))]

    def start(in_refs, out_refs, sems):
        for cp in copies(in_refs, out_refs, sems):
            cp.start()

    def finish(in_refs, out_refs, sems):
        cps = copies(in_refs, out_refs, sems)
        for cp in cps:
            cp.wait_recv()
        for cp in cps:
            cp.wait_send()

    scratch = [pltpu.SemaphoreType.DMA((nslot * na,)), pltpu.SemaphoreType.DMA((nslot * na,))]
    return _Side(arrs, [S((out_slots,) + x.shape[1:], x.dtype) for x in arrs], scratch, start, finish)


def _side_rs_sibling(gs):
    return _side_exchange(gs, 4, 4, lambda mx, my, mc: [(2 * ch + 1 - mc, (mx, my, 1 - mc)) for ch in range(4)])


def _side_rs_chips(ps):
    return _side_exchange(ps, 3, 3, lambda mx, my, mc: [(2 * cx + cy, (cx, cy, mc)) for cx, cy in
                                                        [(1 - mx, my), (mx, 1 - my), (1 - mx, 1 - my)]])


_COLW = 512
_COL_NAMES = ("l0_w_in", "l0_w_ff1", "l1_w_in", "l1_w_ff1")
_ROW_NAMES = ("l0_w_out", "l0_w_ff2", "l1_w_out", "l1_w_ff2")
_BIG_NAMES = _COL_NAMES + _ROW_NAMES


def _full_weight(gathered, name, ncols):
    if name in _ROW_NAMES:
        return gathered.reshape(-1, D)
    return gathered[:, :, :ncols].transpose(1, 0, 2).reshape(D, NDEV * ncols)


def _regroup_w_in0(w):
    main = jnp.concatenate([w[:, 0:1536], w[:, 1544:3080], w[:, 3088:3600]], axis=1)
    small = jnp.concatenate([w[:, 1536:1544], w[:, 3080:3088]], axis=1)
    return jnp.concatenate([main, small, jnp.zeros((D, ZW0 - 3584 - 16), w.dtype)], axis=1)


def _ungroup_w_in0(g):
    return jnp.concatenate([g[:, 0:1536], g[:, 3584:3592], g[:, 1536:3072], g[:, 3592:3600], g[:, 3072:3584]], axis=1)


def _col_slabs(g, ncols):
    g = g.reshape(D, NDEV, ncols).transpose(1, 0, 2)
    return jnp.pad(g, ((0, 0), (0, 0), (0, _COLW - ncols)))


def _sq(t):
    return t * t


def _mlp_fwd(tag, x, gain, w1, w2):
    h, ht = _tok_fwd(f"{tag}_ffn_norm", _f_norm, [(x, D, 0)], [(gain, None, 0)], [(D, bf16)], 256, also_t=(0,))
    a = _mm(f"{tag}_ff1", h, w1, NN, bf16, epi=lambda acc: jnp.maximum(acc, 0.0))
    y = _mm(f"{tag}_ff2", a, w2, NN, f32, a_fn=_sq, epi=lambda acc, r: acc + r, extra=(x,))
    return y, (h, ht, a)


def _mlp_bwd(tag, x, gain, w1, w2, saved, dy, dyt, side=None):
    h, ht, a = saved
    da = _mm(f"{tag}_ff2_dx", dy, w2, NT, bf16, epi=lambda acc, av: acc * 2.0 * av.astype(f32), extra=(a,), side=side)
    if side is not None:
        da, side_res = da
    dw2 = _mm(f"{tag}_ff2_dw", dyt, a, NN, bf16, b_fn=_sq, out_t=True)
    dh = _mm(f"{tag}_ff1_dx", da, w1, NT, f32)
    dw1 = _mm(f"{tag}_ff1_dw", ht, da, NN, bf16, slab=True)
    res = _tok_bwd(f"{tag}_ffn_norm_bwd", _f_norm, [(x, D, 0)], [(gain, None, 0)], [(dh, D)], 256, [0], [0], addto={0: dy})
    return (res, dw1, dw2) if side is None else ((res, dw1, dw2), side_res)


def _row(v):
    return v.reshape(1, -1).astype(f32)


_L0_REST = ("l0_w_ff1", "l0_w_out", "l0_w_ff2")
_L1_MIX = ("l1_w_in", "l1_w_out")
_L1_FFN = ("l1_w_ff1", "l1_w_ff2")
_GRAD_A = ("l1_w_ff1", "l1_w_ff2", "l1_w_out", "l1_w_in")
_GRAD_B = ("l0_w_ff1", "l0_w_ff2", "l0_w_out")


def _train(x, tgt, args, mom, var, t):
    n = x.shape[0]
    nchunk = t // CH
    mx, my, mc = _me()
    dev, chip = 4 * mx + 2 * my + mc, 2 * mx + my
    core_idx = jnp.reshape(mc, (1,)).astype(jnp.int32)
    own_idx = jnp.stack([dev, chip]).astype(jnp.int32)
    ncols = {nm: args[nm].shape[1] for nm in _COL_NAMES}
    conv_cols = args["l0_gdn_conv"].shape[1]
    g, big, w = {}, {}, {}

    def send(nm):
        a = args[nm].astype(bf16)
        return jnp.pad(a, ((0, 0), (0, _COLW - ncols[nm]))) if nm in _COL_NAMES else a

    def take(names, gathered):
        for nm, arr in zip(names, gathered):
            w[nm] = _full_weight(arr, nm, ncols.get(nm, 0))

    def by_dev(nm, ga):
        return ga if nm in _COL_NAMES else ga.reshape(NDEV, -1, D)

    def pair(names, gs, r1s):
        return [_pair_sum(f"rs_pair_sum_{nm}", core_idx, ga, r1) for nm, ga, r1 in zip(names, gs, r1s)]

    def adam(names, gs, r1s, r2s):
        for nm, ga, r1, r2 in zip(names, gs, r1s, r2s):
            big[nm] = _adamw_big(f"adamw_{nm}", own_idx, ga, r1, r2, args[nm], mom[nm], var[nm])


    li = jnp.arange(FOX_W)
    pm = jnp.where((li[:, None] // FOX_D) == (li[None, :] // FOX_D), 1.0 / FOX_D, 0.0).astype(f32)
    lane_head = jnp.arange(GDN_W) // HD
    sel = lambda first_lane: (jnp.arange(LANES)[:, None] == (first_lane + lane_head)[None, :]).astype(f32)
    e_beta, e_alpha = sel(FOX_H), sel(FOX_H + GDN_H)
    alog_b, dt_b = _row(jnp.repeat(args["l0_gdn_A_log"], HD)), _row(jnp.repeat(args["l0_gdn_dt_bias"], HD))
    gq_t, gk_t = _row(jnp.tile(args["l0_fox_q_norm"], FOX_H)), _row(jnp.tile(args["l0_fox_k_norm"], FOX_H))
    on0_t, on1_t = _row(jnp.tile(args["l0_gdn_o_norm"], GDN_H)), _row(jnp.tile(args["l1_hgrn_o_norm"], HG_H))
    fbias = jnp.pad(_row(args["l0_fox_f_bias"]), ((0, 0), (0, LANES - FOX_H)))
    g0m, g0f, g1m, g1f = (_row(args[k]) for k in ("l0_mix_norm", "l0_ffn_norm", "l1_mix_norm", "l1_ffn_norm"))
    lbl = args["hgrn_lb_logits"].astype(f32)

    first = [send("l0_w_in"), jnp.pad(args["l0_gdn_conv"], ((0, 4), (0, LANES * 2 - conv_cols)))]
    (h0, h0t), first = _tok_fwd("l0_mix_norm", _f_norm, [(x, D, 0)], [(g0m, None, 0)], [(D, bf16)], 256, also_t=(0,),
                                side=_side_allgather(first))
    take(("l0_w_in",), first[:1])
    w_in0 = _regroup_w_in0(w["l0_w_in"])
    wconv = first[1][:, :4, :conv_cols].transpose(1, 0, 2).reshape(4, NDEV * conv_cols)
    z0 = _mm("l0_in", h0, w_in0, NN, f32)
    qk_rows = [(z0, 2 * FOX_W, 0)]
    qk_consts = [(gq_t, None, 0), (gk_t, None, 0), (pm, None, 0)]
    qn, kn = _tok_fwd("fox_pre", _f_foxpre, qk_rows, qk_consts, [(FOX_W, f32)] * 2, 256)
    ccol, crow = _fox_gate_fwd(z0, fbias, t)
    fox_o, got = _fox_attn_fwd(qn, kn, z0, ccol, crow, t, side=_side_allgather([send(nm) for nm in _L0_REST]))
    take(_L0_REST, got)
    conv_rows, conv_consts = [(z0, LANES, Z0_GQKV)], [(wconv, LANES, 0)]
    (qkv,) = _tok_fwd("gdn_conv", _f_conv, conv_rows, conv_consts, [(LANES, f32)], t, ncb=12, with_j=True)
    gate_rows = [(z0, LANES, Z0_SMALL)]
    gate_consts = [(e_beta, None, 0), (e_alpha, None, 0), (alog_b, None, 0), (dt_b, None, 0)]
    beta_b, g_b = _tok_fwd("gdn_gate", _f_gdngate, gate_rows, gate_consts, [(GDN_W, f32)] * 2, 256)
    intra_rows = [(qkv, GDN_W, 0), (qkv, GDN_W, 1), (qkv, GDN_W, 2), (beta_b, GDN_W, 0), (g_b, GDN_W, 0)]
    intra, got = _tok_fwd("gdn_intra", _f_gdn_intra, intra_rows, [], [(GDN_W, f32)] * 7, 2 * CH,
                          side=_side_allgather([send(nm) for nm in _L1_MIX]))
    take(_L1_MIX, got)
    inter_rows = [(a, 0) for a in intra[:6]]
    (gdn_o, gdn_hist), got = _scan_fwd("gdn_scan", _f_gdn_inter, inter_rows, GDN_H, nchunk,
                                       side=_side_allgather([send(nm) for nm in _L1_FFN[1:]]))
    take(_L1_FFN[1:], got)
    post0_rows, post0_consts = [(fox_o, FOX_W, 0), (gdn_o, GDN_W, 0), (z0, GDN_W, Z0_GG // 4)], [(on0_t, None, 0)]
    cat0, cat0t = _tok_fwd("l0_post", _f_post0, post0_rows, post0_consts, [(D, bf16)], 256, also_t=(0,))
    x1 = _mm("l0_out", cat0, w["l0_w_out"], NN, f32, epi=lambda acc, r: acc + r, extra=(x,))
    x2, mlp0 = _mlp_fwd("l0", x1, g0f, w["l0_w_ff1"], w["l0_w_ff2"])

    h1, h1t = _tok_fwd("l1_mix_norm", _f_norm, [(x2, D, 0)], [(g1m, None, 0)], [(D, bf16)], 256, also_t=(0,))
    z1 = _mm("l1_in", h1, w["l1_w_in"], NN, f32)
    hpre_rows, hpre_consts = [(z1, 2 * D, 0)], [(lbl, None, 0)]
    hq, hk, hb = _tok_fwd("hgrn_pre", _f_hpre, hpre_rows, hpre_consts, [(D, f32)] * 3, 256)
    hg_rows = [(hq, 0), (hk, 0), (hb, 0), (z1, 2)]
    (hg_o, hg_hist), got = _scan_fwd("hgrn_scan", _f_hgrn_chunk, hg_rows, HG_H, nchunk,
                                     side=_side_allgather([send(nm) for nm in _L1_FFN[:1]]))
    take(_L1_FFN[:1], got)
    post1_rows, post1_consts = [(hg_o, D, 0), (z1, D, 3)], [(on1_t, None, 0)]
    cat1, cat1t = _tok_fwd("l1_post", _f_post1, post1_rows, post1_consts, [(D, bf16)], 256, also_t=(0,))
    x3 = _mm("l1_out", cat1, w["l1_w_out"], NN, f32, epi=lambda acc, r: acc + r, extra=(x2,))
    y, mlp1 = _mlp_fwd("l1", x3, g1f, w["l1_w_ff1"], w["l1_w_ff2"])

    dy, loss_row, dyt = _loss_head(y, tgt)

    (dx3, g["l1_ffn_norm"]), ga_ff1, ga_ff2 = _mlp_bwd("l1", x3, g1f, w["l1_w_ff1"], w["l1_w_ff2"], mlp1, dy, dyt)
    dcat1 = _mm("l1_out_dx", dx3, w["l1_w_out"], NT, f32)
    ga_out = _mm("l1_out_dw", cat1t, dx3, NN, bf16)
    dhg_o, dzg, don1 = _tok_bwd("l1_post_bwd", _f_post1, post1_rows, post1_consts, [(dcat1, D)], 256, [0, 1], [0])
    dhq, dhk, dhb, dzi = _scan_bwd("hgrn_scan_bwd", _f_hgrn_chunk, hg_rows, hg_hist, dhg_o, HG_H, nchunk)
    dzqf, dlbl = _tok_bwd("hgrn_pre_bwd", _f_hpre, hpre_rows, hpre_consts, [(dhq, D), (dhk, D), (dhb, D)], 256, [0], [0])
    dz1 = jnp.concatenate([dzqf, dzi, dzg], axis=1).astype(bf16)
    dh1 = _mm("l1_in_dx", dz1, w["l1_w_in"], NT, f32)
    ga_in = _mm("l1_in_dw", h1t, dz1, NN, bf16, slab=True)
    dx2, g["l1_mix_norm"], dx2t = _tok_bwd("l1_mix_norm_bwd", _f_norm, [(x2, D, 0)], [(g1m, None, 0)], [(dh1, D)], 256, [0], [0],
                                           addto={0: dx3}, also_t=(0,))
    g["l1_hgrn_o_norm"] = don1.reshape(HG_H, HD).sum(0)
    g["hgrn_lb_logits"] = dlbl
    gs_a = [by_dev(nm, ga) for nm, ga in zip(_GRAD_A, (ga_ff1, ga_ff2, ga_out, ga_in))]

    ((dx1, g["l0_ffn_norm"]), gb_ff1, gb_ff2), r1_a = _mlp_bwd("l0", x1, g0f, w["l0_w_ff1"], w["l0_w_ff2"], mlp0, dx2, dx2t,
                                                               side=_side_rs_sibling(gs_a))
    pairs_a = pair(_GRAD_A, gs_a, r1_a)
    dcat0 = _mm("l0_out_dx", dx1, w["l0_w_out"], NT, f32)
    gb_out = _mm("l0_out_dw", cat0t, dx1, NN, bf16)
    gs_b = [by_dev(nm, ga) for nm, ga in zip(_GRAD_B, (gb_ff1, gb_ff2, gb_out))]
    dfox_o, dgdn_o, dgg, don0 = _tok_bwd("l0_post_bwd", _f_post0, post0_rows, post0_consts, [(dcat0, D)], 256, [0, 1, 2], [0])
    (dqn, dkn, dfv, dcq, dck), got = _fox_attn_bwd(qn, kn, z0, ccol, crow, dfox_o, t,
                                                   side=_join_sides([_side_rs_chips(pairs_a), _side_rs_sibling(gs_b)]))
    r2_a, r1_b = got[:len(_GRAD_A)], got[len(_GRAD_A):]
    adam(_GRAD_A, gs_a, r1_a, r2_a)
    pairs_b = pair(_GRAD_B, gs_b, r1_b)
    dinter = _scan_bwd("gdn_scan_bwd", _f_gdn_inter, inter_rows, gdn_hist, dgdn_o, GDN_H, nchunk)
    (dq_g, dk_g, dv_g, dbeta_b, dg_b), r2_b = _tok_bwd("gdn_intra_bwd", _f_gdn_intra, intra_rows + [(intra[6], GDN_W, 0)], [],
                                                       [(a, GDN_W) for a in dinter], 2 * CH, [0, 1, 2, 3, 4], [],
                                                       side=_side_rs_chips(pairs_b))
    adam(_GRAD_B, gs_b, r1_b, r2_b)
    dqkv = jnp.concatenate([dq_g, dk_g, dv_g], axis=1)
    dzs_g, dalog_b, ddt_b = _tok_bwd("gdn_gate_bwd", _f_gdngate, gate_rows, gate_consts, [(dbeta_b, GDN_W), (dg_b, GDN_W)], 256, [0], [2, 3])
    dgqkv, dwconv = _tok_bwd("gdn_conv_bwd", _f_conv, conv_rows, conv_consts, [(dqkv, LANES)], t, [0], [0], ncb=12, with_j=True)
    dzs_f, dfb = _fox_gate_bwd(z0, fbias, dcq, dck, t)
    dzqk, dgq_t, dgk_t = _tok_bwd("fox_pre_bwd", _f_foxpre, qk_rows, qk_consts, [(dqn, FOX_W), (dkn, FOX_W)], 256, [0], [0, 1])
    dz0 = jnp.concatenate([dzqk, dfv, dgqkv, dgg, dzs_g + dzs_f, jnp.zeros((n, ZW0 - 3712), f32)], axis=1).astype(bf16)
    gs_c = [_col_slabs(_ungroup_w_in0(_mm("l0_in_dw", h0t, dz0, NN, bf16)), ncols["l0_w_in"])]
    r1_c = _rs_sibling(gs_c)
    pairs_c = pair(("l0_w_in",), gs_c, r1_c)
    dh0, r2_c = _mm("l0_in_dx", dz0, w_in0, NT, f32, side=_side_rs_chips(pairs_c))
    dx, g["l0_mix_norm"] = _tok_bwd("l0_mix_norm_bwd", _f_norm, [(x, D, 0)], [(g0m, None, 0)], [(dh0, D)], 256, [0], [0], addto={0: dx1})
    adam(("l0_w_in",), gs_c, r1_c, r2_c)
    g["l0_fox_q_norm"] = dgq_t.reshape(FOX_H, FOX_D).sum(0)
    g["l0_fox_k_norm"] = dgk_t.reshape(FOX_H, FOX_D).sum(0)
    g["l0_fox_f_bias"] = dfb[0, :FOX_H]
    g["l0_gdn_conv"] = dwconv
    g["l0_gdn_A_log"] = dalog_b.reshape(GDN_H, HD).sum(1)
    g["l0_gdn_dt_bias"] = ddt_b.reshape(GDN_H, HD).sum(1)
    g["l0_gdn_o_norm"] = don0.reshape(GDN_H, HD).sum(0)
    return loss_row, dx, g, big


_NAMES = ("l0_mix_norm", "l0_w_in", "l0_fox_q_norm", "l0_fox_k_norm", "l0_fox_f_bias", "l0_gdn_conv", "l0_gdn_A_log",
          "l0_gdn_dt_bias", "l0_gdn_o_norm", "l0_w_out", "l0_ffn_norm", "l0_w_ff1", "l0_w_ff2", "l1_mix_norm", "l1_w_in",
          "l1_hgrn_o_norm", "l1_w_out", "l1_ffn_norm", "l1_w_ff1", "l1_w_ff2", "hgrn_lb_logits")
_SMALL_NAMES = tuple(nm for nm in _NAMES if nm not in _BIG_NAMES)
_SMALL_ROWS = 16


def _pack_small(vals):
    flat = jnp.concatenate([vals[nm].reshape(-1).astype(f32) for nm in _SMALL_NAMES])
    return jnp.pad(flat, (0, _SMALL_ROWS * D - flat.shape[0])).reshape(_SMALL_ROWS, D)


def _unpack_small(packed, shapes):
    flat = packed.reshape(-1)
    out, off = {}, 0
    for nm in _SMALL_NAMES:
        size = 1
        for s in shapes[nm]:
            size *= s
        out[nm] = flat[off:off + size].reshape(shapes[nm])
        off += size
    return out, off


def kernel(x, l0_mix_norm, l0_w_in, l0_fox_q_norm, l0_fox_k_norm, l0_fox_f_bias, l0_gdn_conv, l0_gdn_A_log, l0_gdn_dt_bias, l0_gdn_o_norm, l0_w_out, l0_ffn_norm, l0_w_ff1, l0_w_ff2, l1_mix_norm, l1_w_in, l1_hgrn_o_norm, l1_w_out, l1_ffn_norm, l1_w_ff1, l1_w_ff2, hgrn_lb_logits, loss_target, m_l0_mix_norm, m_l0_w_in, m_l0_fox_q_norm, m_l0_fox_k_norm, m_l0_fox_f_bias, m_l0_gdn_conv, m_l0_gdn_A_log, m_l0_gdn_dt_bias, m_l0_gdn_o_norm, m_l0_w_out, m_l0_ffn_norm, m_l0_w_ff1, m_l0_w_ff2, m_l1_mix_norm, m_l1_w_in, m_l1_hgrn_o_norm, m_l1_w_out, m_l1_ffn_norm, m_l1_w_ff1, m_l1_w_ff2, m_hgrn_lb_logits, v_l0_mix_norm, v_l0_w_in, v_l0_fox_q_norm, v_l0_fox_k_norm, v_l0_fox_f_bias, v_l0_gdn_conv, v_l0_gdn_A_log, v_l0_gdn_dt_bias, v_l0_gdn_o_norm, v_l0_w_out, v_l0_ffn_norm, v_l0_w_ff1, v_l0_w_ff2, v_l1_mix_norm, v_l1_w_in, v_l1_hgrn_o_norm, v_l1_w_out, v_l1_ffn_norm, v_l1_w_ff1, v_l1_w_ff2, v_hgrn_lb_logits):
    args = dict(zip(_NAMES, (l0_mix_norm, l0_w_in, l0_fox_q_norm, l0_fox_k_norm, l0_fox_f_bias, l0_gdn_conv, l0_gdn_A_log, l0_gdn_dt_bias, l0_gdn_o_norm, l0_w_out, l0_ffn_norm, l0_w_ff1, l0_w_ff2, l1_mix_norm, l1_w_in, l1_hgrn_o_norm, l1_w_out, l1_ffn_norm, l1_w_ff1, l1_w_ff2, hgrn_lb_logits)))
    mom = dict(zip(_NAMES, (m_l0_mix_norm, m_l0_w_in, m_l0_fox_q_norm, m_l0_fox_k_norm, m_l0_fox_f_bias, m_l0_gdn_conv, m_l0_gdn_A_log, m_l0_gdn_dt_bias, m_l0_gdn_o_norm, m_l0_w_out, m_l0_ffn_norm, m_l0_w_ff1, m_l0_w_ff2, m_l1_mix_norm, m_l1_w_in, m_l1_hgrn_o_norm, m_l1_w_out, m_l1_ffn_norm, m_l1_w_ff1, m_l1_w_ff2, m_hgrn_lb_logits)))
    var = dict(zip(_NAMES, (v_l0_mix_norm, v_l0_w_in, v_l0_fox_q_norm, v_l0_fox_k_norm, v_l0_fox_f_bias, v_l0_gdn_conv, v_l0_gdn_A_log, v_l0_gdn_dt_bias, v_l0_gdn_o_norm, v_l0_w_out, v_l0_ffn_norm, v_l0_w_ff1, v_l0_w_ff2, v_l1_mix_norm, v_l1_w_in, v_l1_hgrn_o_norm, v_l1_w_out, v_l1_ffn_norm, v_l1_w_ff1, v_l1_w_ff2, v_hgrn_lb_logits)))
    nb, t, _ = x.shape
    dev = 4 * lax.axis_index("x") + 2 * lax.axis_index("y") + lax.axis_index("c")
    conv_cols = l0_gdn_conv.shape[1]
    loss_row, dx, g, big = _train(x.reshape(nb * t, D), loss_target.reshape(nb * t, D), args, mom, var, t)

    shapes = {nm: args[nm].shape for nm in _SMALL_NAMES}
    gsm = dict(g)
    gsm["l0_gdn_conv"] = jnp.zeros(shapes["l0_gdn_conv"], f32)
    packed = _pack_small(gsm)
    _, used = _unpack_small(packed, shapes)
    flat_extra = jnp.concatenate([jnp.sum(loss_row).reshape(1), g["l0_gdn_conv"].reshape(-1)])
    packed = packed.reshape(-1).at[used:used + flat_extra.shape[0]].set(flat_extra).reshape(_SMALL_ROWS, D)
    (parts,) = _allgather("ag_small", [packed])
    total = _sum_parts(parts).reshape(-1)
    loss = 0.5 * total[used] / D
    conv_g_full = total[used + 1:used + 1 + 4 * NDEV * conv_cols].reshape(4, NDEV * conv_cols)
    conv_g = lax.dynamic_slice(conv_g_full, (0, dev * conv_cols), (4, conv_cols))
    own_vals = {nm: jnp.zeros(shapes[nm], f32) for nm in _SMALL_NAMES}
    own_vals["l0_gdn_conv"] = conv_g
    own_mask = {nm: jnp.zeros(shapes[nm], f32) for nm in _SMALL_NAMES}
    own_mask["l0_gdn_conv"] = jnp.ones(shapes["l0_gdn_conv"], f32)
    small = _adamw_small(parts, _pack_small(args), _pack_small(mom), _pack_small(var), _pack_small(own_mask), _pack_small(own_vals))
    small = [_unpack_small(a, shapes)[0] for a in small]
    small[0]["l0_gdn_conv"] = conv_g

    outs = [loss, dx.reshape(nb, t, D)]
    for k in range(4):
        outs += [big[nm][k] if nm in _BIG_NAMES else small[k][nm] for nm in _NAMES]
    return tuple(outs)
```

```python
import functools

import jax
import jax.numpy as jnp
from jax import lax
from jax.experimental import pallas as pl
from jax.experimental.pallas import tpu as pltpu

f32, bf16 = jnp.float32, jnp.bfloat16
NN = (((1,), (0,)), ((), ()))
NT = (((1,), (1,)), ((), ()))
TN = (((0,), (0,)), ((), ()))
HI = lax.Precision.HIGHEST
MESH = pl.DeviceIdType.MESH
S = jax.ShapeDtypeStruct

EPS = 1e-6
D = 1024
LANES = 128
FOX_H, FOX_D, FOX_W = 8, 64, 512
GDN_H, HD, GDN_W = 4, 128, 512
HG_H = 8
CH = 64
ZW0 = 3840
NDEV = 8
ADAM_LR, ADAM_B1, ADAM_B2, ADAM_EPS, ADAM_WD, ADAM_STEP = 0.001, 0.9, 0.999, 1e-08, 0.01, 10

Z0_FQK, Z0_FV, Z0_GQKV, Z0_GG, Z0_SMALL = 0, 8, 12, 24, 28


def _dot(a, b, dims=NN, prec=None):
    return lax.dot_general(a, b, dims, precision=prec, preferred_element_type=f32)


def _iota2(shape, axis):
    return lax.broadcasted_iota(jnp.int32, shape, axis)


def _split3(x):
    x1 = x.astype(bf16)
    r = x - x1.astype(f32)
    x2 = r.astype(bf16)
    return x1, x2, (r - x2.astype(f32)).astype(bf16)


def _dot_sel(a, b, dims=NN, exact_lhs=False):
    if exact_lhs:
        return sum(_dot(a.astype(bf16), piece, dims) for piece in _split3(b))
    return sum(_dot(piece, b.astype(bf16), dims) for piece in _split3(a))


@jax.custom_vjp
def _sel_rhs(a, b):
    return _dot_sel(a, b)


_sel_rhs.defvjp(lambda a, b: (_dot_sel(a, b), b), lambda b, g: (_dot_sel(g, b, NT), jnp.zeros_like(b)))


@jax.custom_vjp
def _sel_lhs(a, x):
    return _dot_sel(a, x, exact_lhs=True)


_sel_lhs.defvjp(lambda a, x: (_dot_sel(a, x, exact_lhs=True), a), lambda a, g: (jnp.zeros_like(a), _dot_sel(a, g, TN, exact_lhs=True)))


class _Side:
    def __init__(self, ins, out_shapes, scratch, start, finish):
        self.ins, self.out_shapes, self.scratch, self.start, self.finish = list(ins), list(out_shapes), list(scratch), start, finish


def _join_sides(sides):
    def split(refs, counts):
        out, off = [], 0
        for c in counts:
            out.append(refs[off:off + c])
            off += c
        return out

    ni, no, ns = ([len(getattr(sd, a)) for sd in sides] for a in ("ins", "out_shapes", "scratch"))

    def run(which):
        def go(ins, outs, sems):
            for sd, i, o, c in zip(sides, split(ins, ni), split(outs, no), split(sems, ns)):
                getattr(sd, which)(i, o, c)
        return go

    return _Side(sum((sd.ins for sd in sides), []), sum((sd.out_shapes for sd in sides), []),
                 sum((sd.scratch for sd in sides), []), run("start"), run("finish"))


def _pcall(side, body, *, name, grid, in_specs, out_specs, out_shape, scratch_shapes=(), compiler_params=None):
    if side is None:
        return pl.pallas_call(body, name=name, grid=grid, in_specs=in_specs, out_specs=out_specs, out_shape=out_shape,
                              scratch_shapes=scratch_shapes, compiler_params=compiler_params)
    single = not isinstance(out_shape, (list, tuple))
    ospecs, oshape = ([out_specs], [out_shape]) if single else (list(out_specs), list(out_shape))
    nin, nout, nscr = len(in_specs), len(ospecs), len(scratch_shapes)
    si, so = len(side.ins), len(side.out_shapes)

    def wrapped(*refs):
        o0 = nin + si
        c0 = o0 + nout + so
        sins, souts, ssems = refs[nin:o0], refs[o0 + nout:c0], refs[c0 + nscr:]
        ids = [pl.program_id(a) for a in range(len(grid))]
        first = functools.reduce(jnp.logical_and, [i == 0 for i in ids])
        last = functools.reduce(jnp.logical_and, [i == g - 1 for i, g in zip(ids, grid)])

        @pl.when(first)
        def _():
            side.start(sins, souts, ssems)

        body(*refs[:nin], *refs[o0:o0 + nout], *refs[c0:c0 + nscr])

        @pl.when(last)
        def _():
            side.finish(sins, souts, ssems)

    call = pl.pallas_call(
        wrapped, name=name, grid=grid, in_specs=list(in_specs) + _hbm_specs(si), out_specs=ospecs + _hbm_specs(so),
        out_shape=oshape + side.out_shapes, scratch_shapes=list(scratch_shapes) + side.scratch,
        compiler_params=pltpu.CompilerParams(dimension_semantics=("arbitrary",) * len(grid),
                                             vmem_limit_bytes=getattr(compiler_params, "vmem_limit_bytes", None)))

    def run(*args):
        res = call(*args, *side.ins)
        return (res[0] if single else list(res[:nout])), list(res[nout:])

    return run


def _tok_specs(rows, consts, tm):
    specs = []
    for (_, w, base) in rows:
        specs.append(pl.BlockSpec((tm, w), functools.partial(lambda j, i, b: (i, b + j), b=base)))
    for (arr, w, base) in consts:
        if w is None:
            specs.append(pl.BlockSpec(arr.shape, lambda j, i: (0, 0)))
        else:
            specs.append(pl.BlockSpec((arr.shape[0], w), functools.partial(lambda j, i, b: (0, b + j), b=base)))
    return specs


def _tok_fwd(name, f, rows, consts, outs, tm, ncb=1, with_j=False, also_t=(), side=None):
    n = rows[0][0].shape[0]
    nin = len(rows) + len(consts)
    nout = len(outs)

    def body(*refs):
        ins = [r[...] for r in refs[:nin]]
        vals = f(pl.program_id(0), *ins) if with_j else f(*ins)
        for r, v in zip(refs[nin:nin + nout], vals):
            r[...] = v.astype(r.dtype)
        for r, k in zip(refs[nin + nout:], also_t):
            r[...] = vals[k].T.astype(r.dtype)

    return _pcall(
        side, body, name=name, grid=(ncb, n // tm),
        in_specs=_tok_specs(rows, consts, tm),
        out_specs=[pl.BlockSpec((tm, w), lambda j, i: (i, j)) for (w, _) in outs]
        + [pl.BlockSpec((outs[k][0], tm), lambda j, i: (0, i)) for k in also_t],
        out_shape=[S((n, w * ncb), dt) for (w, dt) in outs] + [S((outs[k][0], n), bf16) for k in also_t],
        compiler_params=pltpu.CompilerParams(dimension_semantics=("parallel", "parallel")),
    )(*[r[0] for r in rows], *[c[0] for c in consts])


def _tok_bwd(name, f, rows, consts, cots, tm, drow, dconst, ncb=1, with_j=False, addto=None, also_t=(), drow_dtype=f32, side=None):
    n = rows[0][0].shape[0]
    nr, nc, nct = len(rows), len(consts), len(cots)
    addto = addto or {}
    add_keys = sorted(addto)
    nadd = len(add_keys)

    def body(*refs):
        ins = [r[...] for r in refs[:nr + nc]]
        cot = [r[...] for r in refs[nr + nc:nr + nc + nct]]
        adds = refs[nr + nc + nct:nr + nc + nct + nadd]
        outs = refs[nr + nc + nct + nadd:]
        pos = list(drow) + [nr + k for k in dconst]

        def g(*dargs):
            full = list(ins)
            for p, a in zip(pos, dargs):
                full[p] = a
            return tuple(f(pl.program_id(0), *full) if with_j else f(*full))

        vals, vjp = jax.vjp(g, *[ins[p] for p in pos])
        grads = vjp(tuple(c.astype(v.dtype) for c, v in zip(cot, vals)))
        for k in range(len(drow)):
            gk = grads[k]
            if k in addto:
                gk = gk + adds[add_keys.index(k)][...]
            outs[k][...] = gk.astype(outs[k].dtype)
            if k in also_t:
                tref = outs[len(drow) + len(dconst) + list(also_t).index(k)]
                tref[...] = gk.T.astype(tref.dtype)
        first = pl.program_id(1) == 0
        for k in range(len(dconst)):
            ref = outs[len(drow) + k]

            @pl.when(first)
            def _():
                ref[...] = jnp.zeros_like(ref)

            ref[...] += grads[len(drow) + k]

    in_specs = _tok_specs(rows, consts, tm)
    in_specs += [pl.BlockSpec((tm, w), lambda j, i: (i, j)) for (_, w) in cots]
    in_specs += [pl.BlockSpec((tm, rows[drow[k]][1]), lambda j, i: (i, j)) for k in add_keys]
    out_specs = [pl.BlockSpec((tm, rows[k][1]), lambda j, i: (i, j)) for k in drow]
    out_shape = [S((n, rows[k][1] * ncb), drow_dtype) for k in drow]
    for k in dconst:
        arr, w, _ = consts[k]
        if w is None:
            out_specs.append(pl.BlockSpec(arr.shape, lambda j, i: (0, 0)))
            out_shape.append(S(arr.shape, f32))
        else:
            out_specs.append(pl.BlockSpec((arr.shape[0], w), lambda j, i: (0, j)))
            out_shape.append(S((arr.shape[0], w * ncb), f32))
    for k in also_t:
        out_specs.append(pl.BlockSpec((rows[drow[k]][1], tm), lambda j, i: (0, i)))
        out_shape.append(S((rows[drow[k]][1], n), bf16))
    return _pcall(
        side, body, name=name, grid=(ncb, n // tm), in_specs=in_specs, out_specs=out_specs, out_shape=out_shape,
        compiler_params=pltpu.CompilerParams(dimension_semantics=("parallel", "arbitrary")),
    )(*[r[0] for r in rows], *[c[0] for c in consts], *[c[0] for c in cots], *[addto[k] for k in add_keys])


def _scan_fwd(name, f, rows, nh, nchunk, side=None):
    n = rows[0][0].shape[0]
    nb = n // (CH * nchunk)
    nin = len(rows)
    w = nh * HD

    def body(*refs):
        o_ref, hist_ref, st = refs[nin], refs[nin + 1], refs[nin + 2]

        @pl.when(pl.program_id(0) == 0)
        def _():
            st[...] = jnp.zeros_like(st)

        s0 = st[...]
        hist_ref[0] = s0
        o, s1 = f(*[r[...].reshape(nb * CH, w) for r in refs[:nin]], s0)
        o_ref[...] = o.reshape(nb, CH, w)
        st[...] = s1

    seq3 = lambda a: a.reshape(nb, nchunk * CH, a.shape[1])
    res = _pcall(
        side, body, name=name, grid=(nchunk,),
        in_specs=[pl.BlockSpec((nb, CH, w), functools.partial(lambda c, base: (0, c, base), base=b)) for (_, b) in rows],
        out_specs=[pl.BlockSpec((nb, CH, w), lambda c: (0, c, 0)), pl.BlockSpec((1, nb * w, HD), lambda c: (c, 0, 0))],
        out_shape=[S((nb, nchunk * CH, w), f32), S((nchunk, nb * w, HD), f32)],
        scratch_shapes=[pltpu.VMEM((nb * w, HD), f32)],
        compiler_params=pltpu.CompilerParams(dimension_semantics=("arbitrary",)),
    )(*[seq3(r[0]) for r in rows])
    (o, hist), extra = (res, None) if side is None else res
    out = [o.reshape(n, w), hist]
    return out if side is None else (out, extra)


def _scan_bwd(name, f, rows, hist, do, nh, nchunk, side=None):
    n = rows[0][0].shape[0]
    nb = n // (CH * nchunk)
    nin = len(rows)
    w = nh * HD

    def body(*refs):
        hist_ref, do_ref = refs[nin], refs[nin + 1]
        outs = refs[nin + 2:nin + 2 + nin]
        ds = refs[nin + 2 + nin]

        @pl.when(pl.program_id(0) == 0)
        def _():
            ds[...] = jnp.zeros_like(ds)

        _, vjp = jax.vjp(f, *[r[...].reshape(nb * CH, w) for r in refs[:nin]], hist_ref[0])
        grads = vjp((do_ref[...].reshape(nb * CH, w), ds[...]))
        for r, gk in zip(outs, grads[:nin]):
            r[...] = gk.reshape(nb, CH, w)
        ds[...] = grads[nin]

    seq3 = lambda a: a.reshape(nb, nchunk * CH, a.shape[1])
    rev = lambda c, base: (0, nchunk - 1 - c, base)
    res = _pcall(
        side, body, name=name, grid=(nchunk,),
        in_specs=[pl.BlockSpec((nb, CH, w), functools.partial(rev, base=b)) for (_, b) in rows]
        + [pl.BlockSpec((1, nb * w, HD), lambda c: (nchunk - 1 - c, 0, 0)), pl.BlockSpec((nb, CH, w), functools.partial(rev, base=0))],
        out_specs=[pl.BlockSpec((nb, CH, w), functools.partial(rev, base=0)) for _ in rows],
        out_shape=[S((nb, nchunk * CH, w), f32) for _ in rows],
        scratch_shapes=[pltpu.VMEM((nb * w, HD), f32)],
        compiler_params=pltpu.CompilerParams(dimension_semantics=("arbitrary",)),
    )(*[seq3(r[0]) for r in rows], hist, seq3(do))
    outs, extra = (res, None) if side is None else res
    outs = [o.reshape(n, w) for o in outs]
    return outs if side is None else (outs, extra)


_VMEM_LIMIT = 56 * 2 ** 20
_VMEM_TILE_BUDGET = 40 * 2 ** 20


def _mm_tiles(m, n, k, sa, sb, so, sx, a_f32, b_f32, tn_fixed):
    best = None
    for tm in (1024, 512, 256, 128, 64):
        for tn in ((tn_fixed,) if tn_fixed else (1024, 768, 512, 384, 256, 128)):
            if m % tm or n % tn:
                continue
            need = 2 * (tm * k * sa + k * tn * sb + tm * tn * (so + sx)) + tm * tn * 4
            need += tm * k * (2 if sa == 4 else 0) + k * tn * (2 if sb == 4 else 0)
            need += tm * k * (4 if a_f32 else 0) + k * tn * (4 if b_f32 else 0)
            if need <= _VMEM_TILE_BUDGET and (best is None or (tm * tn, tm) > best[0]):
                best = ((tm * tn, tm), tm, tn)
    return best[1], best[2]


def _mm(name, a, b, dims, out_dtype, a_fn=None, b_fn=None, epi=None, extra=(), consts=(), outs=None, out_t=False, slab=False,
        side=None):
    m, kk = a.shape
    nn = b.shape[1] if dims is NN else b.shape[0]
    kinds = [("tile", out_dtype)] if outs is None else list(outs)
    so = sum(jnp.dtype(dt).itemsize for kd, dt in kinds if kd != "rows")
    sx = sum(e.dtype.itemsize for e in extra)
    full_rows = bool(consts) or any(kd == "rows" for kd, _ in kinds)
    tm, tn = _mm_tiles(m, nn, kk, a.dtype.itemsize, b.dtype.itemsize, so, sx,
                       a_fn is not None, b_fn is not None, _COLW if slab else (nn if full_rows else None))
    nex, nco = len(extra), len(consts)

    def body(a_ref, b_ref, *rest):
        av, bv = a_ref[...], b_ref[...]
        if a_fn is not None:
            av = a_fn(av.astype(f32))
        if b_fn is not None:
            bv = b_fn(bv.astype(f32))
        acc = _dot(av.astype(bf16), bv.astype(bf16), dims)
        if epi is not None:
            acc = epi(acc, *[r[...] for r in rest[:nex + nco]])
        vals = acc if isinstance(acc, tuple) else (acc,)
        for (kd, _), o_ref, val in zip(kinds, rest[nex + nco:], vals):
            if kd == "rows":
                @pl.when(pl.program_id(0) == 0)
                def _():
                    o_ref[...] = jnp.zeros_like(o_ref)

                o_ref[...] += val
            elif kd == "tile_t" or out_t:
                o_ref[...] = val.T.astype(o_ref.dtype)
            elif slab:
                o_ref[0] = val.astype(o_ref.dtype)
            else:
                o_ref[...] = val.astype(o_ref.dtype)

    bspec = pl.BlockSpec((kk, tn), lambda i, j: (0, j)) if dims is NN else pl.BlockSpec((tn, kk), lambda i, j: (j, 0))
    out_specs, out_shape = [], []
    for kd, dt in kinds:
        if kd == "rows":
            out_specs.append(pl.BlockSpec((1, nn), lambda i, j: (0, 0)))
            out_shape.append(S((1, nn), dt))
        elif kd == "tile_t" or out_t:
            out_specs.append(pl.BlockSpec((tn, tm), lambda i, j: (j, i)))
            out_shape.append(S((nn, m), dt))
        elif slab:
            out_specs.append(pl.BlockSpec((1, tm, tn), lambda i, j: (j, i, 0)))
            out_shape.append(S((nn // tn, m, tn), dt))
        else:
            out_specs.append(pl.BlockSpec((tm, tn), lambda i, j: (i, j)))
            out_shape.append(S((m, nn), dt))
    if outs is None:
        out_specs, out_shape = out_specs[0], out_shape[0]
    sem = ("arbitrary", "arbitrary") if any(kd == "rows" for kd, _ in kinds) else ("parallel", "parallel")
    return _pcall(
        side, body, name=name, grid=(m // tm, nn // tn),
        in_specs=[pl.BlockSpec((tm, kk), lambda i, j: (i, 0)), bspec]
        + [pl.BlockSpec((tm, tn), lambda i, j: (i, j)) for _ in extra]
        + [pl.BlockSpec(c.shape, lambda i, j: (0, 0)) for c in consts],
        out_specs=out_specs, out_shape=out_shape,
        compiler_params=pltpu.CompilerParams(dimension_semantics=sem, vmem_limit_bytes=_VMEM_LIMIT),
    )(a, b, *extra, *consts)


def _f_norm(x, g):
    return (x * lax.rsqrt(jnp.mean(x * x, axis=-1, keepdims=True) + EPS) * g,)


def _f_foxpre(zqk, gq, gk, pm):
    def nrm(t, g):
        return t * lax.rsqrt(_sel_rhs(t * t, pm) + EPS) * g
    return nrm(zqk[:, :FOX_W], gq), nrm(zqk[:, FOX_W:], gk)


def _chunk_cumsum(x):
    n = x.shape[0]
    r, c = _iota2((n, n), 0), _iota2((n, n), 1)
    tri = jnp.logical_and(r >= c, (r // CH) == (c // CH)).astype(f32)
    return _sel_lhs(tri, x)


def _f_gdngate(zs, eb, ea, alog_b, dt_b):
    beta = jax.nn.sigmoid(_sel_rhs(zs, eb))
    la = -jnp.exp(alog_b) * jax.nn.softplus(_sel_rhs(zs, ea) + dt_b)
    return beta, _chunk_cumsum(la)


def _f_conv(j, x, w):
    t = x.shape[0]
    y = x * w[3:4, :]
    for jj in range(3):
        sh = 3 - jj
        xs = jnp.concatenate([jnp.zeros((sh, x.shape[1]), f32), x[:t - sh, :]], axis=0)
        y = y + xs * w[jj:jj + 1, :]
    y = jax.nn.silu(y)
    yn = y * lax.rsqrt(jnp.sum(y * y, axis=-1, keepdims=True) + EPS)
    return (jnp.where(j < 2 * GDN_H, yn, y),)


def _head_rms(o, nh):
    outs = []
    for h in range(nh):
        oh = o[:, HD * h:HD * (h + 1)]
        outs.append(oh * lax.rsqrt(jnp.mean(oh * oh, axis=-1, keepdims=True) + EPS))
    return jnp.concatenate(outs, axis=1)


def _f_post0(fox_o, o, gg, on):
    return (jnp.concatenate([fox_o, _head_rms(o, GDN_H) * on * jax.nn.silu(gg)], axis=1),)


def _f_post1(o, zg, on):
    return (_head_rms(o, HG_H) * on * jax.nn.silu(zg),)


def _f_hpre(zqf, lbl):
    lb = jax.nn.sigmoid(lbl[1:2, :] - lbl[0:1, :])
    fg = lb + (1.0 - lb) * jax.nn.sigmoid(zqf[:, D:])
    return jax.nn.silu(zqf[:, :D]), 1.0 - fg, _chunk_cumsum(jnp.log(fg))


def _dotb(a, b, dims=NN):
    return _dot(a.astype(bf16), b.astype(bf16), dims)


def _dot3(a, b):
    ah, bh = a.astype(bf16), b.astype(bf16)
    al, bl = (a - ah.astype(f32)).astype(bf16), (b - bh.astype(f32)).astype(bf16)
    return _dot(ah, bh) + (_dot(ah, bl) + _dot(al, bh))


def _split(t, nh):
    return [t[CH * ck:CH * (ck + 1), HD * h:HD * (h + 1)] for ck in range(t.shape[0] // CH) for h in range(nh)]


def _merge(units, nh):
    return jnp.concatenate([jnp.concatenate(units[i:i + nh], axis=1) for i in range(0, len(units), nh)], axis=0)


def _inv_impl(amats):
    n = amats[0].shape[0]
    eye = jnp.where(_iota2((n, n), 0) == _iota2((n, n), 1), 1.0, 0.0).astype(f32)
    xs, ps = [eye - a for a in amats], list(amats)
    for _ in range(max(1, (n - 1).bit_length()) - 1):
        ps = [_dotb(p, p) for p in ps]
        xs = [x + _dotb(x, p) for x, p in zip(xs, ps)]
    for _ in range(2):
        rs = [eye - x - _dot3(a, x) for a, x in zip(amats, xs)]
        xs = [x + _dotb(x, r) for x, r in zip(xs, rs)]
    return tuple(xs)


@jax.custom_vjp
def _inv_unit_lower(amats):
    return _inv_impl(amats)


def _inv_fwd(amats):
    xs = _inv_impl(amats)
    return xs, xs


def _inv_bwd(xs, dxs):
    return (tuple(-_dotb(_dotb(x, dx, TN), x, NT) for x, dx in zip(xs, dxs)),)


_inv_unit_lower.defvjp(_inv_fwd, _inv_bwd)


@jax.custom_vjp
def _inv_given(amats, xs):
    return xs


def _inv_given_fwd(amats, xs):
    return xs, xs


def _inv_given_bwd(xs, dxs):
    return _inv_bwd(xs, dxs)[0], tuple(jnp.zeros_like(x) for x in xs)


_inv_given.defvjp(_inv_given_fwd, _inv_given_bwd)


def _f_gdn_intra(q, k, v, bb, gb, tinv_p=None):
    qs, ks, vs, bs, gs = (_split(t, GDN_H) for t in (q, k, v, bb, gb))
    r, cc = _iota2((CH, CH), 0), _iota2((CH, CH), 1)
    causal, strict = r >= cc, r > cc
    beta, g, gl = [b[:, :1] for b in bs], [x[:, :1] for x in gs], [x[CH - 1:CH, :1] for x in gs]
    decay = [jnp.exp(jnp.where(causal, x[:, :CH] - x[:, :CH].T, -jnp.inf)) for x in gs]
    kb = [ki * bi for ki, bi in zip(ks, beta)]
    amat = [jnp.where(strict, _dotb(kbi, ki, NT) * di, 0.0) for kbi, ki, di in zip(kb, ks, decay)]
    if tinv_p is None:
        tinv = _inv_unit_lower(tuple(amat))
    else:
        tinv = _inv_given(tuple(amat), tuple(x[:, :CH] for x in _split(tinv_p, GDN_H)))
    rhs = [jnp.concatenate([vi * bi, kbi * jnp.exp(gi)], axis=1) for vi, bi, kbi, gi in zip(vs, beta, kb, g)]
    uw = [_dotb(ti, ri) for ti, ri in zip(tinv, rhs)]
    qsc = [qi * (HD ** -0.5) for qi in qs]
    qk = [jnp.where(causal, _dotb(qi, ki, NT) * di, 0.0) for qi, ki, di in zip(qsc, ks, decay)]
    outs = ([x[:, :HD] for x in uw], [x[:, HD:] for x in uw],
            [jnp.concatenate([x, jnp.zeros_like(x)], axis=1) for x in qk],
            [qi * jnp.exp(gi) for qi, gi in zip(qsc, g)],
            [ki * jnp.exp(gli - gi) for ki, gli, gi in zip(ks, gl, g)],
            [jnp.broadcast_to(gli, (CH, HD)) for gli in gl])
    if tinv_p is None:
        outs += ([jnp.concatenate([x, jnp.zeros_like(x)], axis=1) for x in tinv],)
    return tuple(_merge(o, GDN_H) for o in outs)


def _f_gdn_inter(u, w, qkp, qd, kd, glb, st):
    us, ws, qks, qds, kds, gls = (_split(t, GDN_H) for t in (u, w, qkp, qd, kd, glb))
    sts = [st[HD * i:HD * (i + 1), :] for i in range(len(us))]
    vn = [ui - _dotb(wi, si) for ui, wi, si in zip(us, ws, sts)]
    o = [_dotb(qi, si) + _dotb(xi[:, :CH], vi) for qi, si, xi, vi in zip(qds, sts, qks, vn)]
    s2 = [si * jnp.exp(gi[:1, :1]) + _dotb(ki, vi, TN) for si, gi, ki, vi in zip(sts, gls, kds, vn)]
    return _merge(o, GDN_H), jnp.concatenate(s2, axis=0)


def _f_hgrn_chunk(q, k, b, v, st):
    qs, ks, bs, vs = (_split(t, HG_H) for t in (q, k, b, v))
    sts = [st[HD * i:HD * (i + 1), :] for i in range(len(qs))]
    causal = _iota2((CH, CH), 0) >= _iota2((CH, CH), 1)
    bl, bm = [x[CH - 1:CH, :] for x in bs], [x[CH // 2 - 1:CH // 2, :] for x in bs]
    a = [jnp.where(causal, _dotb(qi * jnp.exp(bi - mi), ki * jnp.exp(mi - bi), NT), 0.0)
         for qi, ki, bi, mi in zip(qs, ks, bs, bm)]
    o = [_dotb(qi * jnp.exp(bi), si, NT) + _dotb(ai, vi) for qi, bi, si, ai, vi in zip(qs, bs, sts, a, vs)]
    s2 = [si * jnp.exp(li) + _dotb(vi, ki * jnp.exp(li - bi), TN) for si, li, vi, ki, bi in zip(sts, bl, vs, ks, bs)]
    return _merge(o, HG_H), jnp.concatenate(s2, axis=0)


def _fox_gate_fwd(z0, fbias, t, tc=256):
    n = z0.shape[0]
    nt = t // tc

    def body(zs_ref, b_ref, ccol_ref, crow_ref, carry):
        @pl.when(pl.program_id(1) == 0)
        def _():
            carry[...] = jnp.zeros_like(carry)

        ls = jnp.where(_iota2((tc, LANES), 1) < FOX_H, jax.nn.log_sigmoid(zs_ref[...] + b_ref[...]), 0.0)
        tri = (_iota2((tc, tc), 0) >= _iota2((tc, tc), 1)).astype(f32)
        c = _dot_sel(tri, ls, exact_lhs=True) + carry[...]
        carry[...] = c[tc - 1:tc, :]
        ccol_ref[...] = c
        crow_ref[0] = c.T[:FOX_H, :]

    return pl.pallas_call(
        body, name="fox_gate_fwd", grid=(n // t, nt),
        in_specs=[pl.BlockSpec((tc, LANES), lambda b, i: (b * nt + i, Z0_SMALL)), pl.BlockSpec((1, LANES), lambda b, i: (0, 0))],
        out_specs=[pl.BlockSpec((tc, LANES), lambda b, i: (b * nt + i, 0)), pl.BlockSpec((1, FOX_H, tc), lambda b, i: (b, 0, i))],
        out_shape=[S((n, LANES), f32), S((n // t, FOX_H, t), f32)],
        scratch_shapes=[pltpu.VMEM((1, LANES), f32)],
        compiler_params=pltpu.CompilerParams(dimension_semantics=("parallel", "arbitrary")),
    )(z0, fbias)


def _fox_gate_bwd(z0, fbias, dcq, dck, t, tc=256):
    n = z0.shape[0]
    nt = t // tc

    def body(zs_ref, b_ref, dcq_ref, dck_ref, dz_ref, db_ref, carry):
        first = jnp.logical_and(pl.program_id(0) == 0, pl.program_id(1) == 0)

        @pl.when(pl.program_id(1) == 0)
        def _():
            carry[...] = jnp.zeros_like(carry)

        @pl.when(first)
        def _():
            db_ref[...] = jnp.zeros_like(db_ref)

        dc = dcq_ref[0] + dcq_ref[1] + dcq_ref[2] + dcq_ref[3]
        drow = dck_ref[0, 0] + dck_ref[1, 0] + dck_ref[2, 0] + dck_ref[3, 0]
        eye = (_iota2((FOX_H, LANES), 0) == _iota2((FOX_H, LANES), 1)).astype(f32)
        dc = dc + _dot_sel(drow, eye, TN)
        triu = (_iota2((tc, tc), 0) <= _iota2((tc, tc), 1)).astype(f32)
        dls = _dot_sel(triu, dc, exact_lhs=True) + carry[...]
        carry[...] = dls[0:1, :]
        x = zs_ref[...] + b_ref[...]
        dz = jnp.where(_iota2((tc, LANES), 1) < FOX_H, dls * jax.nn.sigmoid(-x), 0.0)
        dz_ref[...] = dz
        db_ref[...] += jnp.sum(dz, axis=0, keepdims=True)

    def rev(b, i):
        return b * nt + (nt - 1 - i)

    return pl.pallas_call(
        body, name="fox_gate_bwd", grid=(n // t, nt),
        in_specs=[pl.BlockSpec((tc, LANES), lambda b, i: (rev(b, i), Z0_SMALL)), pl.BlockSpec((1, LANES), lambda b, i: (0, 0)),
                  pl.BlockSpec((4, tc, LANES), lambda b, i: (0, rev(b, i), 0)),
                  pl.BlockSpec((4, 1, FOX_H, tc), lambda b, i: (0, b, 0, nt - 1 - i))],
        out_specs=[pl.BlockSpec((tc, LANES), lambda b, i: (rev(b, i), 0)), pl.BlockSpec((1, LANES), lambda b, i: (0, 0))],
        out_shape=[S((n, LANES), f32), S((1, LANES), f32)],
        scratch_shapes=[pltpu.VMEM((1, LANES), f32)],
        compiler_params=pltpu.CompilerParams(dimension_semantics=("arbitrary", "arbitrary")),
    )(z0, fbias, dcq, dck)


def _fox_scores(hh, p, i, tq, q, k, ccol, crow):
    kmax = k.shape[0]
    lane = _iota2((1, LANES), 1)
    mh = (lane // FOX_D) == hh
    h = 2 * p + hh
    qh = jnp.where(mh, q, 0.0).astype(bf16)
    s = _dot(qh, k, NT) * (FOX_D ** -0.5)
    cq = jnp.sum(jnp.where(lane == h, ccol, 0.0), axis=1, keepdims=True)
    ck = jnp.sum(jnp.where(_iota2((FOX_H, 1), 0) == h, crow, 0.0), axis=0, keepdims=True)
    causal = _iota2((1, kmax), 1) <= (i * tq + _iota2((tq, 1), 0))
    s = jnp.where(causal, s + cq - ck, -jnp.inf)
    pe = jnp.exp(s - jnp.max(s, axis=1, keepdims=True))
    return mh, qh, pe, jnp.sum(pe, axis=1, keepdims=True)


def _fox_attn_fwd(qn, kn, z0, ccol, crow, t, tq=256, side=None):
    n = qn.shape[0]
    nq = t // tq

    def body(q_ref, k_ref, v_ref, ccol_ref, crow_ref, o_ref):
        p = pl.program_id(1)
        k, v, crow = k_ref[...].astype(bf16), v_ref[...].astype(bf16), crow_ref[0]
        for i in range(nq):
            rows, kmax = pl.ds(i * tq, tq), (i + 1) * tq
            q, cc = q_ref[rows, :], ccol_ref[rows, :]
            acc = jnp.zeros((tq, LANES), f32)
            for hh in range(2):
                mh, _, pe, l = _fox_scores(hh, p, i, tq, q, k[:kmax], cc, crow[:, :kmax])
                acc = jnp.where(mh, _dot(pe.astype(bf16), v[:kmax]) / l, acc)
            o_ref[rows, :] = acc

    seq = lambda b, p: (b, p)
    return _pcall(
        side, body, name="fox_attn_fwd", grid=(n // t, FOX_H // 2),
        in_specs=[pl.BlockSpec((t, LANES), seq), pl.BlockSpec((t, LANES), seq), pl.BlockSpec((t, LANES), lambda b, p: (b, Z0_FV + p)),
                  pl.BlockSpec((t, LANES), lambda b, p: (b, 0)), pl.BlockSpec((1, FOX_H, t), lambda b, p: (b, 0, 0))],
        out_specs=pl.BlockSpec((t, LANES), seq),
        out_shape=S((n, FOX_W), f32),
        compiler_params=pltpu.CompilerParams(dimension_semantics=("parallel", "parallel")),
    )(qn, kn, z0, ccol, crow)


def _fox_attn_bwd(qn, kn, z0, ccol, crow, do, t, tq=256, side=None):
    n = qn.shape[0]
    nq = t // tq
    nb = n // t

    def body(q_ref, k_ref, v_ref, ccol_ref, crow_ref, do_ref, dq_ref, dk_ref, dv_ref, dcq_ref, dck_ref):
        p = pl.program_id(1)
        dk_ref[...] = jnp.zeros_like(dk_ref)
        dv_ref[...] = jnp.zeros_like(dv_ref)
        dck_ref[...] = jnp.zeros_like(dck_ref)
        kf, v, crow = k_ref[...], v_ref[...].astype(bf16), crow_ref[0]
        k = kf.astype(bf16)
        lane = _iota2((1, LANES), 1)
        sub = _iota2((FOX_H, 1), 0)
        scale = FOX_D ** -0.5
        for i in range(nq):
            rows, kmax = pl.ds(i * tq, tq), (i + 1) * tq
            q, cc, dout = q_ref[rows, :], ccol_ref[rows, :], do_ref[rows, :]
            dq = jnp.zeros((tq, LANES), f32)
            dcq = jnp.zeros((tq, LANES), f32)
            for hh in range(2):
                mh, qh, pe, l = _fox_scores(hh, p, i, tq, q, k[:kmax], cc, crow[:, :kmax])
                pr = pe / l
                doh = jnp.where(mh, dout, 0.0).astype(bf16)
                dp = _dot(doh, v[:kmax], NT)
                ds = pr * (dp - jnp.sum(pr * dp, axis=1, keepdims=True))
                dsb = ds.astype(bf16)
                dq = dq + _dot(dsb, jnp.where(mh, kf[:kmax], 0.0).astype(bf16)) * scale
                dk_ref[:kmax, :] += _dot(dsb, qh, TN) * scale
                dv_ref[:kmax, :] += _dot(pr.astype(bf16), doh, TN)
                h = 2 * p + hh
                dcq = dcq + jnp.where(lane == h, jnp.sum(ds, axis=1, keepdims=True), 0.0)
                dck_ref[0, 0, :, :kmax] += jnp.where(sub == h, -jnp.sum(ds, axis=0, keepdims=True), 0.0)
            dq_ref[rows, :] = dq
            dcq_ref[0, rows, :] = dcq

    seq = lambda b, p: (b, p)
    return _pcall(
        side, body, name="fox_attn_bwd", grid=(nb, FOX_H // 2),
        in_specs=[pl.BlockSpec((t, LANES), seq), pl.BlockSpec((t, LANES), seq), pl.BlockSpec((t, LANES), lambda b, p: (b, Z0_FV + p)),
                  pl.BlockSpec((t, LANES), lambda b, p: (b, 0)), pl.BlockSpec((1, FOX_H, t), lambda b, p: (b, 0, 0)),
                  pl.BlockSpec((t, LANES), seq)],
        out_specs=[pl.BlockSpec((t, LANES), seq), pl.BlockSpec((t, LANES), seq), pl.BlockSpec((t, LANES), seq),
                   pl.BlockSpec((1, t, LANES), lambda b, p: (p, b, 0)), pl.BlockSpec((1, 1, FOX_H, t), lambda b, p: (p, b, 0, 0))],
        out_shape=[S((n, FOX_W), f32), S((n, FOX_W), f32), S((n, FOX_W), f32), S((4, n, LANES), f32), S((4, nb, FOX_H, t), f32)],
        compiler_params=pltpu.CompilerParams(dimension_semantics=("parallel", "parallel")),
    )(qn, kn, z0, ccol, crow, do)


def _adamw_math(w, g, m, v):
    m = ADAM_B1 * m + (1.0 - ADAM_B1) * g
    v = ADAM_B2 * v + (1.0 - ADAM_B2) * (g * g)
    m_hat = m / (1.0 - ADAM_B1 ** ADAM_STEP)
    v_hat = v / (1.0 - ADAM_B2 ** ADAM_STEP)
    return -ADAM_LR * (m_hat / (jnp.sqrt(v_hat) + ADAM_EPS) + ADAM_WD * w), m, v


def _adamw_big(name, idx, gmine, recv1, recv2, w, m, v):
    r, wc = w.shape
    c = gmine.shape[2]
    tr = min(r, 256)

    def body(idx_ref, gm_ref, r1_ref, r2_ref, w_ref, m_ref, v_ref, g_ref, d_ref, nm_ref, nv_ref):
        g = gm_ref[0].astype(f32) + r1_ref[0].astype(f32)
        for k in range(3):
            g = g + r2_ref[k].astype(f32)
        g = g[:, :wc]
        d, nm, nv = _adamw_math(w_ref[...], g, m_ref[...], v_ref[...])
        g_ref[...] = g
        d_ref[...] = d
        nm_ref[...] = nm
        nv_ref[...] = nv

    row = pl.BlockSpec((tr, wc), lambda i, s: (i, 0))
    return pl.pallas_call(
        body, name=name,
        grid_spec=pltpu.PrefetchScalarGridSpec(
            num_scalar_prefetch=1, grid=(r // tr,),
            in_specs=[pl.BlockSpec((1, tr, c), lambda i, s: (s[0], i, 0)), pl.BlockSpec((1, tr, c), lambda i, s: (s[1], i, 0)),
                      pl.BlockSpec((3, tr, c), lambda i, s: (0, i, 0)), row, row, row],
            out_specs=[row, row, row, row]),
        out_shape=[S((r, wc), f32)] * 4,
        compiler_params=pltpu.CompilerParams(dimension_semantics=("parallel",)),
    )(idx, gmine, recv1, recv2, w, m, v)


def _pair_sum(name, idx, gmine, recv1):
    _, r, c = gmine.shape
    g4 = gmine.reshape(4, 2, r, c)

    def body(idx_ref, gm_ref, r1_ref, o_ref):
        o_ref[0] = (gm_ref[0, 0].astype(f32) + r1_ref[0].astype(f32)).astype(bf16)

    return pl.pallas_call(
        body, name=name,
        grid_spec=pltpu.PrefetchScalarGridSpec(
            num_scalar_prefetch=1, grid=(4,),
            in_specs=[pl.BlockSpec((1, 1, r, c), lambda ch, s: (ch, s[0], 0, 0)), pl.BlockSpec((1, r, c), lambda ch, s: (ch, 0, 0))],
            out_specs=pl.BlockSpec((1, r, c), lambda ch, s: (ch, 0, 0))),
        out_shape=S((4, r, c), bf16),
        compiler_params=pltpu.CompilerParams(dimension_semantics=("parallel",)),
    )(idx, g4, recv1)


def _adamw_small(parts, w, m, v, own_mask, own_g):
    def body(p_ref, w_ref, m_ref, v_ref, mask_ref, og_ref, g_ref, d_ref, nm_ref, nv_ref):
        g = p_ref[0]
        for k in range(1, NDEV):
            g = g + p_ref[k]
        g_ref[...] = g
        ge = jnp.where(mask_ref[...] > 0.5, og_ref[...], g)
        d, nm, nv = _adamw_math(w_ref[...], ge, m_ref[...], v_ref[...])
        d_ref[...] = d
        nm_ref[...] = nm
        nv_ref[...] = nv

    return pl.pallas_call(body, name="adamw_small", out_shape=[S(w.shape, f32)] * 4)(parts, w, m, v, own_mask, own_g)


def _sum_parts(parts):
    def body(p_ref, g_ref):
        g = p_ref[0]
        for k in range(1, NDEV):
            g = g + p_ref[k]
        g_ref[...] = g

    return pl.pallas_call(body, name="sum_parts", out_shape=S(parts.shape[1:], f32))(parts)


def _me():
    return lax.axis_index("x"), lax.axis_index("y"), lax.axis_index("c")


def _hbm_specs(n):
    return [pl.BlockSpec(memory_space=pl.ANY)] * n


def _allgather(name, xs):
    na = len(xs)

    def body(*refs):
        x_refs, out_refs = refs[:na], refs[na:2 * na]
        send_sems, recv_sems, local_sems = refs[2 * na:]
        mx, my, mc = _me()
        me, sib = (mx, my, mc), (mx, my, 1 - mc)
        chips = [(1 - mx, my), (mx, 1 - my), (1 - mx, 1 - my)]

        def slab(a, px, py, pc):
            return out_refs[a].at[4 * px + 2 * py + pc]

        def copy(a, k, block, to, own=False):
            return pltpu.make_async_remote_copy(
                src_ref=x_refs[a] if own else slab(a, *block), dst_ref=slab(a, *block),
                send_sem=send_sems.at[7 * a + k], recv_sem=recv_sems.at[7 * a + k], device_id=to, device_id_type=MESH)

        mine = [pltpu.make_async_copy(x_refs[a], slab(a, *me), local_sems.at[a]) for a in range(na)]
        first = []
        for a in range(na):
            mine[a].start()
            first += [copy(a, 0, me, sib, own=True)] + [copy(a, 1 + j, me, (*chip, mc), own=True) for j, chip in enumerate(chips)]
        for cp in first:
            cp.start()
        passed = []
        for j, chip in enumerate(chips):
            for a in range(na):
                copy(a, 1 + j, (*chip, mc), me).wait_recv()
                passed.append(copy(a, 4 + j, (*chip, mc), sib))
                passed[-1].start()
        for a in range(na):
            copy(a, 0, sib, me).wait_recv()
            for j, chip in enumerate(chips):
                copy(a, 4 + j, (*chip, 1 - mc), me).wait_recv()
        for cp in first + passed:
            cp.wait_send()
        for cp in mine:
            cp.wait()

    return pl.pallas_call(
        body, name=name, out_shape=[S((NDEV,) + x.shape, x.dtype) for x in xs],
        in_specs=_hbm_specs(na), out_specs=_hbm_specs(na),
        scratch_shapes=[pltpu.SemaphoreType.DMA((7 * na,)), pltpu.SemaphoreType.DMA((7 * na,)), pltpu.SemaphoreType.DMA((na,))],
    )(*xs)


def _rs_sibling(gs):
    na = len(gs)

    def body(*refs):
        g_refs, out_refs, send_sems, recv_sems = refs[:na], refs[na:2 * na], refs[2 * na], refs[2 * na + 1]
        mx, my, mc = _me()
        cps = [pltpu.make_async_remote_copy(
            src_ref=g_refs[a].at[2 * ch + 1 - mc], dst_ref=out_refs[a].at[ch], send_sem=send_sems.at[4 * a + ch],
            recv_sem=recv_sems.at[4 * a + ch], device_id=(mx, my, 1 - mc), device_id_type=MESH)
            for a in range(na) for ch in range(4)]
        for cp in cps:
            cp.start()
        for cp in cps:
            cp.wait_recv()
        for cp in cps:
            cp.wait_send()

    return pl.pallas_call(
        body, name="rs_sibling", out_shape=[S((4,) + g.shape[1:], g.dtype) for g in gs],
        in_specs=_hbm_specs(na), out_specs=_hbm_specs(na),
        scratch_shapes=[pltpu.SemaphoreType.DMA((4 * na,)), pltpu.SemaphoreType.DMA((4 * na,))],
    )(*gs)


def _side_allgather(xs):
    na = len(xs)

    def mk(x_refs, out_refs, sems):
        send_sems, recv_sems, local_sems = sems
        mx, my, mc = _me()
        me, sib = (mx, my, mc), (mx, my, 1 - mc)
        chips = [(1 - mx, my), (mx, 1 - my), (1 - mx, 1 - my)]

        def slab(a, px, py, pc):
            return out_refs[a].at[4 * px + 2 * py + pc]

        def copy(a, k, block, to, own=False):
            return pltpu.make_async_remote_copy(
                src_ref=x_refs[a] if own else slab(a, *block), dst_ref=slab(a, *block),
                send_sem=send_sems.at[7 * a + k], recv_sem=recv_sems.at[7 * a + k], device_id=to, device_id_type=MESH)

        mine = [pltpu.make_async_copy(x_refs[a], slab(a, *me), local_sems.at[a]) for a in range(na)]
        first = []
        for a in range(na):
            first += [copy(a, 0, me, sib, own=True)] + [copy(a, 1 + j, me, (*chip, mc), own=True) for j, chip in enumerate(chips)]
        return me, sib, chips, mc, copy, mine, first

    def start(x_refs, out_refs, sems):
        *_, mine, first = mk(x_refs, out_refs, sems)
        for cp in mine + first:
            cp.start()

    def finish(x_refs, out_refs, sems):
        me, sib, chips, mc, copy, mine, first = mk(x_refs, out_refs, sems)
        passed = []
        for j, chip in enumerate(chips):
            for a in range(na):
                copy(a, 1 + j, (*chip, mc), me).wait_recv()
                passed.append(copy(a, 4 + j, (*chip, mc), sib))
                passed[-1].start()
        for a in range(na):
            copy(a, 0, sib, me).wait_recv()
            for j, chip in enumerate(chips):
                copy(a, 4 + j, (*chip, 1 - mc), me).wait_recv()
        for cp in first + passed:
            cp.wait_send()
        for cp in mine:
            cp.wait()

    scratch = [pltpu.SemaphoreType.DMA((7 * na,)), pltpu.SemaphoreType.DMA((7 * na,)), pltpu.SemaphoreType.DMA((na,))]
    return _Side(xs, [S((NDEV,) + x.shape, x.dtype) for x in xs], scratch, start, finish)


def _side_exchange(arrs, nslot, out_slots, route):
    na = len(arrs)

    def copies(in_refs, out_refs, sems):
        send_sems, recv_sems = sems
        return [pltpu.make_async_remote_copy(
            src_ref=in_refs[a].at[src], dst_ref=out_refs[a].at[k], send_sem=send_sems.at[nslot * a + k],
            recv_sem=recv_sems.at[nslot * a + k], device_id=to, device_id_type=MESH)
            for a in range(na) for k, (src, to) in enumerate(route(*_me()))]

    def start(in_refs, out_refs, sems):
        for cp in copies(in_refs, out_refs, sems):
            cp.start()

    def finish(in_refs, out_refs, sems):
        cps = copies(in_refs, out_refs, sems)
        for cp in cps:
            cp.wait_recv()
        for cp in cps:
            cp.wait_send()

    scratch = [pltpu.SemaphoreType.DMA((nslot * na,)), pltpu.SemaphoreType.DMA((nslot * na,))]
    return _Side(arrs, [S((out_slots,) + x.shape[1:], x.dtype) for x in arrs], scratch, start, finish)


def _side_rs_sibling(gs):
    return _side_exchange(gs, 4, 4, lambda mx, my, mc: [(2 * ch + 1 - mc, (mx, my, 1 - mc)) for ch in range(4)])


def _side_rs_chips(ps):
    return _side_exchange(ps, 3, 3, lambda mx, my, mc: [(2 * cx + cy, (cx, cy, mc)) for cx, cy in
                                                        [(1 - mx, my), (mx, 1 - my), (1 - mx, 1 - my)]])


_COLW = 512
_COL_NAMES = ("l0_w_in", "l0_w_ff1", "l1_w_in", "l1_w_ff1")
_ROW_NAMES = ("l0_w_out", "l0_w_ff2", "l1_w_out", "l1_w_ff2")
_BIG_NAMES = _COL_NAMES + _ROW_NAMES


def _full_weight(gathered, name, ncols):
    if name in _ROW_NAMES:
        return gathered.reshape(-1, D)
    return gathered[:, :, :ncols].transpose(1, 0, 2).reshape(D, NDEV * ncols)


def _regroup_w_in0(w):
    main = jnp.concatenate([w[:, 0:1536], w[:, 1544:3080], w[:, 3088:3600]], axis=1)
    small = jnp.concatenate([w[:, 1536:1544], w[:, 3080:3088]], axis=1)
    return jnp.concatenate([main, small, jnp.zeros((D, ZW0 - 3584 - 16), w.dtype)], axis=1)


def _ungroup_w_in0(g):
    return jnp.concatenate([g[:, 0:1536], g[:, 3584:3592], g[:, 1536:3072], g[:, 3592:3600], g[:, 3072:3584]], axis=1)


def _col_slabs(g, ncols):
    g = g.reshape(D, NDEV, ncols).transpose(1, 0, 2)
    return jnp.pad(g, ((0, 0), (0, 0), (0, _COLW - ncols)))


def _sq(t):
    return t * t


def _epi_res_norm(acc, res, gain):
    y = acc + res
    h = _f_norm(y, gain)[0]
    return y, h, h


_RES_NORM_OUTS = [("tile", f32), ("tile", bf16), ("tile_t", bf16)]


def _epi_norm_bwd(acc, x, dres, gain):
    _, vjp = jax.vjp(lambda xx, gg: _f_norm(xx, gg)[0], x, gain)
    dx, dgain = vjp(acc)
    dx = dx + dres
    return dx, dgain, dx


_NORM_BWD_OUTS = [("tile", f32), ("rows", f32), ("tile_t", bf16)]


def _epi_loss(acc, res, tgt):
    e = acc + res - tgt
    dy = e * (1.0 / D)
    return dy, dy, jnp.sum(e * e, axis=0, keepdims=True)


_LOSS_OUTS = [("tile", f32), ("tile_t", bf16), ("rows", f32)]


def _mlp_fwd(tag, x, h, w1, w2, epi, extra, consts, outs):
    a = _mm(f"{tag}_ff1", h, w1, NN, bf16, epi=lambda acc: jnp.maximum(acc, 0.0))
    return _mm(f"{tag}_ff2", a, w2, NN, f32, a_fn=_sq, epi=epi, extra=(x,) + tuple(extra), consts=consts, outs=outs), a


def _mlp_bwd(tag, x, gain, w1, w2, ht, a, dy, dyt, side=None):
    da = _mm(f"{tag}_ff2_dx", dy, w2, NT, bf16, epi=lambda acc, av: acc * 2.0 * av.astype(f32), extra=(a,), side=side)
    if side is not None:
        da, side_res = da
    dw2 = _mm(f"{tag}_ff2_dw", dyt, a, NN, bf16, b_fn=_sq, out_t=True)
    res = _mm(f"{tag}_ff1_dx", da, w1, NT, f32, epi=_epi_norm_bwd, extra=(x, dy), consts=(gain,), outs=_NORM_BWD_OUTS)
    dw1 = _mm(f"{tag}_ff1_dw", ht, da, NN, bf16, slab=True)
    return (res, dw1, dw2) if side is None else ((res, dw1, dw2), side_res)


def _row(v):
    return v.reshape(1, -1).astype(f32)


_L0_REST = ("l0_w_ff1", "l0_w_out", "l0_w_ff2")
_L1_MIX = ("l1_w_in", "l1_w_out")
_L1_FFN = ("l1_w_ff1", "l1_w_ff2")
_GRAD_A = ("l1_w_ff1", "l1_w_ff2", "l1_w_out", "l1_w_in")
_GRAD_B = ("l0_w_ff1", "l0_w_ff2", "l0_w_out")


def _train(x, tgt, args, mom, var, t):
    n = x.shape[0]
    nchunk = t // CH
    mx, my, mc = _me()
    dev, chip = 4 * mx + 2 * my + mc, 2 * mx + my
    core_idx = jnp.reshape(mc, (1,)).astype(jnp.int32)
    own_idx = jnp.stack([dev, chip]).astype(jnp.int32)
    ncols = {nm: args[nm].shape[1] for nm in _COL_NAMES}
    conv_cols = args["l0_gdn_conv"].shape[1]
    g, big, w = {}, {}, {}

    def send(nm):
        a = args[nm].astype(bf16)
        return jnp.pad(a, ((0, 0), (0, _COLW - ncols[nm]))) if nm in _COL_NAMES else a

    def take(names, gathered):
        for nm, arr in zip(names, gathered):
            w[nm] = _full_weight(arr, nm, ncols.get(nm, 0))

    def by_dev(nm, ga):
        return ga if nm in _COL_NAMES else ga.reshape(NDEV, -1, D)

    def pair(names, gs, r1s):
        return [_pair_sum(f"rs_pair_sum_{nm}", core_idx, ga, r1) for nm, ga, r1 in zip(names, gs, r1s)]

    def adam(names, gs, r1s, r2s):
        for nm, ga, r1, r2 in zip(names, gs, r1s, r2s):
            big[nm] = _adamw_big(f"adamw_{nm}", own_idx, ga, r1, r2, args[nm], mom[nm], var[nm])


    li = jnp.arange(FOX_W)
    pm = jnp.where((li[:, None] // FOX_D) == (li[None, :] // FOX_D), 1.0 / FOX_D, 0.0).astype(f32)
    lane_head = jnp.arange(GDN_W) // HD
    sel = lambda first_lane: (jnp.arange(LANES)[:, None] == (first_lane + lane_head)[None, :]).astype(f32)
    e_beta, e_alpha = sel(FOX_H), sel(FOX_H + GDN_H)
    alog_b, dt_b = _row(jnp.repeat(args["l0_gdn_A_log"], HD)), _row(jnp.repeat(args["l0_gdn_dt_bias"], HD))
    gq_t, gk_t = _row(jnp.tile(args["l0_fox_q_norm"], FOX_H)), _row(jnp.tile(args["l0_fox_k_norm"], FOX_H))
    on0_t, on1_t = _row(jnp.tile(args["l0_gdn_o_norm"], GDN_H)), _row(jnp.tile(args["l1_hgrn_o_norm"], HG_H))
    fbias = jnp.pad(_row(args["l0_fox_f_bias"]), ((0, 0), (0, LANES - FOX_H)))
    g0m, g0f, g1m, g1f = (_row(args[k]) for k in ("l0_mix_norm", "l0_ffn_norm", "l1_mix_norm", "l1_ffn_norm"))
    lbl = args["hgrn_lb_logits"].astype(f32)

    first = [send("l0_w_in"), jnp.pad(args["l0_gdn_conv"], ((0, 4), (0, LANES * 2 - conv_cols)))]
    (h0, h0t), first = _tok_fwd("l0_mix_norm", _f_norm, [(x, D, 0)], [(g0m, None, 0)], [(D, bf16)], 256, also_t=(0,),
                                side=_side_allgather(first))
    take(("l0_w_in",), first[:1])
    w_in0 = _regroup_w_in0(w["l0_w_in"])
    wconv = first[1][:, :4, :conv_cols].transpose(1, 0, 2).reshape(4, NDEV * conv_cols)
    z0 = _mm("l0_in", h0, w_in0, NN, f32)
    qk_rows = [(z0, 2 * FOX_W, 0)]
    qk_consts = [(gq_t, None, 0), (gk_t, None, 0), (pm, None, 0)]
    qn, kn = _tok_fwd("fox_pre", _f_foxpre, qk_rows, qk_consts, [(FOX_W, f32)] * 2, 256)
    ccol, crow = _fox_gate_fwd(z0, fbias, t)
    fox_o, got = _fox_attn_fwd(qn, kn, z0, ccol, crow, t, side=_side_allgather([send(nm) for nm in _L0_REST]))
    take(_L0_REST, got)
    conv_rows, conv_consts = [(z0, LANES, Z0_GQKV)], [(wconv, LANES, 0)]
    (qkv,) = _tok_fwd("gdn_conv", _f_conv, conv_rows, conv_consts, [(LANES, f32)], t, ncb=12, with_j=True)
    gate_rows = [(z0, LANES, Z0_SMALL)]
    gate_consts = [(e_beta, None, 0), (e_alpha, None, 0), (alog_b, None, 0), (dt_b, None, 0)]
    beta_b, g_b = _tok_fwd("gdn_gate", _f_gdngate, gate_rows, gate_consts, [(GDN_W, f32)] * 2, 256)
    intra_rows = [(qkv, GDN_W, 0), (qkv, GDN_W, 1), (qkv, GDN_W, 2), (beta_b, GDN_W, 0), (g_b, GDN_W, 0)]
    intra, got = _tok_fwd("gdn_intra", _f_gdn_intra, intra_rows, [], [(GDN_W, f32)] * 7, 2 * CH,
                          side=_side_allgather([send(nm) for nm in _L1_MIX]))
    take(_L1_MIX, got)
    inter_rows = [(a, 0) for a in intra[:6]]
    (gdn_o, gdn_hist), got = _scan_fwd("gdn_scan", _f_gdn_inter, inter_rows, GDN_H, nchunk,
                                       side=_side_allgather([send(nm) for nm in _L1_FFN[1:]]))
    take(_L1_FFN[1:], got)
    post0_rows, post0_consts = [(fox_o, FOX_W, 0), (gdn_o, GDN_W, 0), (z0, GDN_W, Z0_GG // 4)], [(on0_t, None, 0)]
    cat0, cat0t = _tok_fwd("l0_post", _f_post0, post0_rows, post0_consts, [(D, bf16)], 256, also_t=(0,))
    x1, hf0, hf0t = _mm("l0_out", cat0, w["l0_w_out"], NN, f32, epi=_epi_res_norm, extra=(x,), consts=(g0f,), outs=_RES_NORM_OUTS)
    (x2, h1, h1t), a0 = _mlp_fwd("l0", x1, hf0, w["l0_w_ff1"], w["l0_w_ff2"], _epi_res_norm, (), (g1m,), _RES_NORM_OUTS)

    z1 = _mm("l1_in", h1, w["l1_w_in"], NN, f32)
    hpre_rows, hpre_consts = [(z1, 2 * D, 0)], [(lbl, None, 0)]
    hq, hk, hb = _tok_fwd("hgrn_pre", _f_hpre, hpre_rows, hpre_consts, [(D, f32)] * 3, 256)
    hg_rows = [(hq, 0), (hk, 0), (hb, 0), (z1, 2)]
    (hg_o, hg_hist), got = _scan_fwd("hgrn_scan", _f_hgrn_chunk, hg_rows, HG_H, nchunk,
                                     side=_side_allgather([send(nm) for nm in _L1_FFN[:1]]))
    take(_L1_FFN[:1], got)
    post1_rows, post1_consts = [(hg_o, D, 0), (z1, D, 3)], [(on1_t, None, 0)]
    cat1, cat1t = _tok_fwd("l1_post", _f_post1, post1_rows, post1_consts, [(D, bf16)], 256, also_t=(0,))
    x3, hf1, hf1t = _mm("l1_out", cat1, w["l1_w_out"], NN, f32, epi=_epi_res_norm, extra=(x2,), consts=(g1f,), outs=_RES_NORM_OUTS)
    (dy, dyt, loss_row), a1 = _mlp_fwd("l1", x3, hf1, w["l1_w_ff1"], w["l1_w_ff2"], _epi_loss, (tgt,), (), _LOSS_OUTS)

    (dx3, g["l1_ffn_norm"], _), ga_ff1, ga_ff2 = _mlp_bwd("l1", x3, g1f, w["l1_w_ff1"], w["l1_w_ff2"], hf1t, a1, dy, dyt)
    dcat1 = _mm("l1_out_dx", dx3, w["l1_w_out"], NT, f32)
    ga_out = _mm("l1_out_dw", cat1t, dx3, NN, bf16)
    dhg_o, dzg, don1 = _tok_bwd("l1_post_bwd", _f_post1, post1_rows, post1_consts, [(dcat1, D)], 256, [0, 1], [0])
    dhq, dhk, dhb, dzi = _scan_bwd("hgrn_scan_bwd", _f_hgrn_chunk, hg_rows, hg_hist, dhg_o, HG_H, nchunk)
    dzqf, dlbl = _tok_bwd("hgrn_pre_bwd", _f_hpre, hpre_rows, hpre_consts, [(dhq, D), (dhk, D), (dhb, D)], 256, [0], [0])
    dz1 = jnp.concatenate([dzqf, dzi, dzg], axis=1).astype(bf16)
    dx2, g["l1_mix_norm"], dx2t = _mm("l1_in_dx", dz1, w["l1_w_in"], NT, f32, epi=_epi_norm_bwd, extra=(x2, dx3), consts=(g1m,),
                                      outs=_NORM_BWD_OUTS)
    ga_in = _mm("l1_in_dw", h1t, dz1, NN, bf16, slab=True)
    g["l1_hgrn_o_norm"] = don1.reshape(HG_H, HD).sum(0)
    g["hgrn_lb_logits"] = dlbl
    gs_a = [by_dev(nm, ga) for nm, ga in zip(_GRAD_A, (ga_ff1, ga_ff2, ga_out, ga_in))]

    ((dx1, g["l0_ffn_norm"], _), gb_ff1, gb_ff2), r1_a = _mlp_bwd("l0", x1, g0f, w["l0_w_ff1"], w["l0_w_ff2"], hf0t, a0, dx2, dx2t,
                                                                  side=_side_rs_sibling(gs_a))
    pairs_a = pair(_GRAD_A, gs_a, r1_a)
    dcat0 = _mm("l0_out_dx", dx1, w["l0_w_out"], NT, f32)
    gb_out = _mm("l0_out_dw", cat0t, dx1, NN, bf16)
    gs_b = [by_dev(nm, ga) for nm, ga in zip(_GRAD_B, (gb_ff1, gb_ff2, gb_out))]
    dfox_o, dgdn_o, dgg, don0 = _tok_bwd("l0_post_bwd", _f_post0, post0_rows, post0_consts, [(dcat0, D)], 256, [0, 1, 2], [0])
    (dqn, dkn, dfv, dcq, dck), got = _fox_attn_bwd(qn, kn, z0, ccol, crow, dfox_o, t,
                                                   side=_join_sides([_side_rs_chips(pairs_a), _side_rs_sibling(gs_b)]))
    r2_a, r1_b = got[:len(_GRAD_A)], got[len(_GRAD_A):]
    adam(_GRAD_A, gs_a, r1_a, r2_a)
    pairs_b = pair(_GRAD_B, gs_b, r1_b)
    dinter = _scan_bwd("gdn_scan_bwd", _f_gdn_inter, inter_rows, gdn_hist, dgdn_o, GDN_H, nchunk)
    (dq_g, dk_g, dv_g, dbeta_b, dg_b), r2_b = _tok_bwd("gdn_intra_bwd", _f_gdn_intra, intra_rows + [(intra[6], GDN_W, 0)], [],
                                                       [(a, GDN_W) for a in dinter], 2 * CH, [0, 1, 2, 3, 4], [],
                                                       side=_side_rs_chips(pairs_b))
    adam(_GRAD_B, gs_b, r1_b, r2_b)
    dqkv = jnp.concatenate([dq_g, dk_g, dv_g], axis=1)
    dzs_g, dalog_b, ddt_b = _tok_bwd("gdn_gate_bwd", _f_gdngate, gate_rows, gate_consts, [(dbeta_b, GDN_W), (dg_b, GDN_W)], 256, [0], [2, 3])
    dgqkv, dwconv = _tok_bwd("gdn_conv_bwd", _f_conv, conv_rows, conv_consts, [(dqkv, LANES)], t, [0], [0], ncb=12, with_j=True)
    dzs_f, dfb = _fox_gate_bwd(z0, fbias, dcq, dck, t)
    dzqk, dgq_t, dgk_t = _tok_bwd("fox_pre_bwd", _f_foxpre, qk_rows, qk_consts, [(dqn, FOX_W), (dkn, FOX_W)], 256, [0], [0, 1])
    dz0 = jnp.concatenate([dzqk, dfv, dgqkv, dgg, dzs_g + dzs_f, jnp.zeros((n, ZW0 - 3712), f32)], axis=1).astype(bf16)
    gs_c = [_col_slabs(_ungroup_w_in0(_mm("l0_in_dw", h0t, dz0, NN, bf16)), ncols["l0_w_in"])]
    r1_c = _rs_sibling(gs_c)
    pairs_c = pair(("l0_w_in",), gs_c, r1_c)
    (dx, g["l0_mix_norm"], _), r2_c = _mm("l0_in_dx", dz0, w_in0, NT, f32, epi=_epi_norm_bwd, extra=(x, dx1), consts=(g0m,),
                                          outs=_NORM_BWD_OUTS, side=_side_rs_chips(pairs_c))
    adam(("l0_w_in",), gs_c, r1_c, r2_c)
    g["l0_fox_q_norm"] = dgq_t.reshape(FOX_H, FOX_D).sum(0)
    g["l0_fox_k_norm"] = dgk_t.reshape(FOX_H, FOX_D).sum(0)
    g["l0_fox_f_bias"] = dfb[0, :FOX_H]
    g["l0_gdn_conv"] = dwconv
    g["l0_gdn_A_log"] = dalog_b.reshape(GDN_H, HD).sum(1)
    g["l0_gdn_dt_bias"] = ddt_b.reshape(GDN_H, HD).sum(1)
    g["l0_gdn_o_norm"] = don0.reshape(GDN_H, HD).sum(0)
    return loss_row, dx, g, big


_NAMES = ("l0_mix_norm", "l0_w_in", "l0_fox_q_norm", "l0_fox_k_norm", "l0_fox_f_bias", "l0_gdn_conv", "l0_gdn_A_log",
          "l0_gdn_dt_bias", "l0_gdn_o_norm", "l0_w_out", "l0_ffn_norm", "l0_w_ff1", "l0_w_ff2", "l1_mix_norm", "l1_w_in",
          "l1_hgrn_o_norm", "l1_w_out", "l1_ffn_norm", "l1_w_ff1", "l1_w_ff2", "hgrn_lb_logits")
_SMALL_NAMES = tuple(nm for nm in _NAMES if nm not in _BIG_NAMES)
_SMALL_ROWS = 16


def _pack_small(vals):
    flat = jnp.concatenate([vals[nm].reshape(-1).astype(f32) for nm in _SMALL_NAMES])
    return jnp.pad(flat, (0, _SMALL_ROWS * D - flat.shape[0])).reshape(_SMALL_ROWS, D)


def _unpack_small(packed, shapes):
    flat = packed.reshape(-1)
    out, off = {}, 0
    for nm in _SMALL_NAMES:
        size = 1
        for s in shapes[nm]:
            size *= s
        out[nm] = flat[off:off + size].reshape(shapes[nm])
        off += size
    return out, off


def kernel(x, l0_mix_norm, l0_w_in, l0_fox_q_norm, l0_fox_k_norm, l0_fox_f_bias, l0_gdn_conv, l0_gdn_A_log, l0_gdn_dt_bias, l0_gdn_o_norm, l0_w_out, l0_ffn_norm, l0_w_ff1, l0_w_ff2, l1_mix_norm, l1_w_in, l1_hgrn_o_norm, l1_w_out, l1_ffn_norm, l1_w_ff1, l1_w_ff2, hgrn_lb_logits, loss_target, m_l0_mix_norm, m_l0_w_in, m_l0_fox_q_norm, m_l0_fox_k_norm, m_l0_fox_f_bias, m_l0_gdn_conv, m_l0_gdn_A_log, m_l0_gdn_dt_bias, m_l0_gdn_o_norm, m_l0_w_out, m_l0_ffn_norm, m_l0_w_ff1, m_l0_w_ff2, m_l1_mix_norm, m_l1_w_in, m_l1_hgrn_o_norm, m_l1_w_out, m_l1_ffn_norm, m_l1_w_ff1, m_l1_w_ff2, m_hgrn_lb_logits, v_l0_mix_norm, v_l0_w_in, v_l0_fox_q_norm, v_l0_fox_k_norm, v_l0_fox_f_bias, v_l0_gdn_conv, v_l0_gdn_A_log, v_l0_gdn_dt_bias, v_l0_gdn_o_norm, v_l0_w_out, v_l0_ffn_norm, v_l0_w_ff1, v_l0_w_ff2, v_l1_mix_norm, v_l1_w_in, v_l1_hgrn_o_norm, v_l1_w_out, v_l1_ffn_norm, v_l1_w_ff1, v_l1_w_ff2, v_hgrn_lb_logits):
    args = dict(zip(_NAMES, (l0_mix_norm, l0_w_in, l0_fox_q_norm, l0_fox_k_norm, l0_fox_f_bias, l0_gdn_conv, l0_gdn_A_log, l0_gdn_dt_bias, l0_gdn_o_norm, l0_w_out, l0_ffn_norm, l0_w_ff1, l0_w_ff2, l1_mix_norm, l1_w_in, l1_hgrn_o_norm, l1_w_out, l1_ffn_norm, l1_w_ff1, l1_w_ff2, hgrn_lb_logits)))
    mom = dict(zip(_NAMES, (m_l0_mix_norm, m_l0_w_in, m_l0_fox_q_norm, m_l0_fox_k_norm, m_l0_fox_f_bias, m_l0_gdn_conv, m_l0_gdn_A_log, m_l0_gdn_dt_bias, m_l0_gdn_o_norm, m_l0_w_out, m_l0_ffn_norm, m_l0_w_ff1, m_l0_w_ff2, m_l1_mix_norm, m_l1_w_in, m_l1_hgrn_o_norm, m_l1_w_out, m_l1_ffn_norm, m_l1_w_ff1, m_l1_w_ff2, m_hgrn_lb_logits)))
    var = dict(zip(_NAMES, (v_l0_mix_norm, v_l0_w_in, v_l0_fox_q_norm, v_l0_fox_k_norm, v_l0_fox_f_bias, v_l0_gdn_conv, v_l0_gdn_A_log, v_l0_gdn_dt_bias, v_l0_gdn_o_norm, v_l0_w_out, v_l0_ffn_norm, v_l0_w_ff1, v_l0_w_ff2, v_l1_mix_norm, v_l1_w_in, v_l1_hgrn_o_norm, v_l1_w_out, v_l1_ffn_norm, v_l1_w_ff1, v_l1_w_ff2, v_hgrn_lb_logits)))
    nb, t, _ = x.shape
    dev = 4 * lax.axis_index("x") + 2 * lax.axis_index("y") + lax.axis_index("c")
    conv_cols = l0_gdn_conv.shape[1]
    loss_row, dx, g, big = _train(x.reshape(nb * t, D), loss_target.reshape(nb * t, D), args, mom, var, t)

    shapes = {nm: args[nm].shape for nm in _SMALL_NAMES}
    gsm = dict(g)
    gsm["l0_gdn_conv"] = jnp.zeros(shapes["l0_gdn_conv"], f32)
    packed = _pack_small(gsm)
    _, used = _unpack_small(packed, shapes)
    flat_extra = jnp.concatenate([jnp.sum(loss_row).reshape(1), g["l0_gdn_conv"].reshape(-1)])
    packed = packed.reshape(-1).at[used:used + flat_extra.shape[0]].set(flat_extra).reshape(_SMALL_ROWS, D)
    (parts,) = _allgather("ag_small", [packed])
    total = _sum_parts(parts).reshape(-1)
    loss = 0.5 * total[used] / D
    conv_g_full = total[used + 1:used + 1 + 4 * NDEV * conv_cols].reshape(4, NDEV * conv_cols)
    conv_g = lax.dynamic_slice(conv_g_full, (0, dev * conv_cols), (4, conv_cols))
    own_vals = {nm: jnp.zeros(shapes[nm], f32) for nm in _SMALL_NAMES}
    own_vals["l0_gdn_conv"] = conv_g
    own_mask = {nm: jnp.zeros(shapes[nm], f32) for nm in _SMALL_NAMES}
    own_mask["l0_gdn_conv"] = jnp.ones(shapes["l0_gdn_conv"], f32)
    small = _adamw_small(parts, _pack_small(args), _pack_small(mom), _pack_small(var), _pack_small(own_mask), _pack_small(own_vals))
    small = [_unpack_small(a, shapes)[0] for a in small]
    small[0]["l0_gdn_conv"] = conv_g

    outs = [loss, dx.reshape(nb, t, D)]
    for k in range(4):
        outs += [big[nm][k] if nm in _BIG_NAMES else small[k][nm] for nm in _NAMES]
    return tuple(outs)
```

```python
import functools

import jax
import jax.numpy as jnp
from jax import lax
from jax.experimental import pallas as pl
from jax.experimental.pallas import tpu as pltpu

f32, bf16 = jnp.float32, jnp.bfloat16
NN = (((1,), (0,)), ((), ()))
NT = (((1,), (1,)), ((), ()))
TN = (((0,), (0,)), ((), ()))
HI = lax.Precision.HIGHEST
MESH = pl.DeviceIdType.MESH
S = jax.ShapeDtypeStruct

EPS = 1e-6
D = 1024
LANES = 128
FOX_H, FOX_D, FOX_W = 8, 64, 512
GDN_H, HD, GDN_W = 4, 128, 512
HG_H = 8
CH = 64
ZW0 = 3840
NDEV = 8
ADAM_LR, ADAM_B1, ADAM_B2, ADAM_EPS, ADAM_WD, ADAM_STEP = 0.001, 0.9, 0.999, 1e-08, 0.01, 10

Z0_FQK, Z0_FV, Z0_GQKV, Z0_GG, Z0_SMALL = 0, 8, 12, 24, 28


def _dot(a, b, dims=NN, prec=None):
    return lax.dot_general(a, b, dims, precision=prec, preferred_element_type=f32)


def _iota2(shape, axis):
    return lax.broadcasted_iota(jnp.int32, shape, axis)


def _split3(x):
    x1 = x.astype(bf16)
    r = x - x1.astype(f32)
    x2 = r.astype(bf16)
    return x1, x2, (r - x2.astype(f32)).astype(bf16)


def _dot_sel(a, b, dims=NN, exact_lhs=False):
    if exact_lhs:
        return sum(_dot(a.astype(bf16), piece, dims) for piece in _split3(b))
    return sum(_dot(piece, b.astype(bf16), dims) for piece in _split3(a))


@jax.custom_vjp
def _sel_rhs(a, b):
    return _dot_sel(a, b)


_sel_rhs.defvjp(lambda a, b: (_dot_sel(a, b), b), lambda b, g: (_dot_sel(g, b, NT), jnp.zeros_like(b)))


@jax.custom_vjp
def _sel_lhs(a, x):
    return _dot_sel(a, x, exact_lhs=True)


_sel_lhs.defvjp(lambda a, x: (_dot_sel(a, x, exact_lhs=True), a), lambda a, g: (jnp.zeros_like(a), _dot_sel(a, g, TN, exact_lhs=True)))


class _Side:
    def __init__(self, ins, out_shapes, scratch, start, finish):
        self.ins, self.out_shapes, self.scratch, self.start, self.finish = list(ins), list(out_shapes), list(scratch), start, finish


def _join_sides(sides):
    def split(refs, counts):
        out, off = [], 0
        for c in counts:
            out.append(refs[off:off + c])
            off += c
        return out

    ni, no, ns = ([len(getattr(sd, a)) for sd in sides] for a in ("ins", "out_shapes", "scratch"))

    def run(which):
        def go(ins, outs, sems):
            for sd, i, o, c in zip(sides, split(ins, ni), split(outs, no), split(sems, ns)):
                getattr(sd, which)(i, o, c)
        return go

    return _Side(sum((sd.ins for sd in sides), []), sum((sd.out_shapes for sd in sides), []),
                 sum((sd.scratch for sd in sides), []), run("start"), run("finish"))


def _pcall(side, body, *, name, grid, in_specs, out_specs, out_shape, scratch_shapes=(), compiler_params=None):
    if side is None:
        return pl.pallas_call(body, name=name, grid=grid, in_specs=in_specs, out_specs=out_specs, out_shape=out_shape,
                              scratch_shapes=scratch_shapes, compiler_params=compiler_params)
    single = not isinstance(out_shape, (list, tuple))
    ospecs, oshape = ([out_specs], [out_shape]) if single else (list(out_specs), list(out_shape))
    nin, nout, nscr = len(in_specs), len(ospecs), len(scratch_shapes)
    si, so = len(side.ins), len(side.out_shapes)

    def wrapped(*refs):
        o0 = nin + si
        c0 = o0 + nout + so
        sins, souts, ssems = refs[nin:o0], refs[o0 + nout:c0], refs[c0 + nscr:]
        ids = [pl.program_id(a) for a in range(len(grid))]
        first = functools.reduce(jnp.logical_and, [i == 0 for i in ids])
        last = functools.reduce(jnp.logical_and, [i == g - 1 for i, g in zip(ids, grid)])

        @pl.when(first)
        def _():
            side.start(sins, souts, ssems)

        body(*refs[:nin], *refs[o0:o0 + nout], *refs[c0:c0 + nscr])

        @pl.when(last)
        def _():
            side.finish(sins, souts, ssems)

    call = pl.pallas_call(
        wrapped, name=name, grid=grid, in_specs=list(in_specs) + _hbm_specs(si), out_specs=ospecs + _hbm_specs(so),
        out_shape=oshape + side.out_shapes, scratch_shapes=list(scratch_shapes) + side.scratch,
        compiler_params=pltpu.CompilerParams(dimension_semantics=("arbitrary",) * len(grid),
                                             vmem_limit_bytes=getattr(compiler_params, "vmem_limit_bytes", None)))

    def run(*args):
        res = call(*args, *side.ins)
        return (res[0] if single else list(res[:nout])), list(res[nout:])

    return run


def _tok_specs(rows, consts, tm):
    specs = []
    for (_, w, base) in rows:
        specs.append(pl.BlockSpec((tm, w), functools.partial(lambda j, i, b: (i, b + j), b=base)))
    for (arr, w, base) in consts:
        if w is None:
            specs.append(pl.BlockSpec(arr.shape, lambda j, i: (0, 0)))
        else:
            specs.append(pl.BlockSpec((arr.shape[0], w), functools.partial(lambda j, i, b: (0, b + j), b=base)))
    return specs


def _tok_fwd(name, f, rows, consts, outs, tm, ncb=1, with_j=False, also_t=(), side=None):
    n = rows[0][0].shape[0]
    nin = len(rows) + len(consts)
    nout = len(outs)

    def body(*refs):
        ins = [r[...] for r in refs[:nin]]
        vals = f(pl.program_id(0), *ins) if with_j else f(*ins)
        for r, v in zip(refs[nin:nin + nout], vals):
            r[...] = v.astype(r.dtype)
        for r, k in zip(refs[nin + nout:], also_t):
            r[...] = vals[k].T.astype(r.dtype)

    return _pcall(
        side, body, name=name, grid=(ncb, n // tm),
        in_specs=_tok_specs(rows, consts, tm),
        out_specs=[pl.BlockSpec((tm, w), lambda j, i: (i, j)) for (w, _) in outs]
        + [pl.BlockSpec((outs[k][0], tm), lambda j, i: (0, i)) for k in also_t],
        out_shape=[S((n, w * ncb), dt) for (w, dt) in outs] + [S((outs[k][0], n), bf16) for k in also_t],
        compiler_params=pltpu.CompilerParams(dimension_semantics=("parallel", "parallel")),
    )(*[r[0] for r in rows], *[c[0] for c in consts])


def _tok_bwd(name, f, rows, consts, cots, tm, drow, dconst, ncb=1, with_j=False, addto=None, also_t=(), drow_dtype=f32, side=None):
    n = rows[0][0].shape[0]
    nr, nc, nct = len(rows), len(consts), len(cots)
    addto = addto or {}
    add_keys = sorted(addto)
    nadd = len(add_keys)

    def body(*refs):
        ins = [r[...] for r in refs[:nr + nc]]
        cot = [r[...] for r in refs[nr + nc:nr + nc + nct]]
        adds = refs[nr + nc + nct:nr + nc + nct + nadd]
        outs = refs[nr + nc + nct + nadd:]
        pos = list(drow) + [nr + k for k in dconst]

        def g(*dargs):
            full = list(ins)
            for p, a in zip(pos, dargs):
                full[p] = a
            return tuple(f(pl.program_id(0), *full) if with_j else f(*full))

        vals, vjp = jax.vjp(g, *[ins[p] for p in pos])
        grads = vjp(tuple(c.astype(v.dtype) for c, v in zip(cot, vals)))
        for k in range(len(drow)):
            gk = grads[k]
            if k in addto:
                gk = gk + adds[add_keys.index(k)][...]
            outs[k][...] = gk.astype(outs[k].dtype)
            if k in also_t:
                tref = outs[len(drow) + len(dconst) + list(also_t).index(k)]
                tref[...] = gk.T.astype(tref.dtype)
        first = pl.program_id(1) == 0
        for k in range(len(dconst)):
            ref = outs[len(drow) + k]

            @pl.when(first)
            def _():
                ref[...] = jnp.zeros_like(ref)

            ref[...] += grads[len(drow) + k]

    in_specs = _tok_specs(rows, consts, tm)
    in_specs += [pl.BlockSpec((tm, w), lambda j, i: (i, j)) for (_, w) in cots]
    in_specs += [pl.BlockSpec((tm, rows[drow[k]][1]), lambda j, i: (i, j)) for k in add_keys]
    out_specs = [pl.BlockSpec((tm, rows[k][1]), lambda j, i: (i, j)) for k in drow]
    dts = drow_dtype if isinstance(drow_dtype, (list, tuple)) else [drow_dtype] * len(drow)
    out_shape = [S((n, rows[k][1] * ncb), dt) for k, dt in zip(drow, dts)]
    for k in dconst:
        arr, w, _ = consts[k]
        if w is None:
            out_specs.append(pl.BlockSpec(arr.shape, lambda j, i: (0, 0)))
            out_shape.append(S(arr.shape, f32))
        else:
            out_specs.append(pl.BlockSpec((arr.shape[0], w), lambda j, i: (0, j)))
            out_shape.append(S((arr.shape[0], w * ncb), f32))
    for k in also_t:
        out_specs.append(pl.BlockSpec((rows[drow[k]][1], tm), lambda j, i: (0, i)))
        out_shape.append(S((rows[drow[k]][1], n), bf16))
    return _pcall(
        side, body, name=name, grid=(ncb, n // tm), in_specs=in_specs, out_specs=out_specs, out_shape=out_shape,
        compiler_params=pltpu.CompilerParams(dimension_semantics=("parallel", "arbitrary")),
    )(*[r[0] for r in rows], *[c[0] for c in consts], *[c[0] for c in cots], *[addto[k] for k in add_keys])


def _scan_fwd(name, f, rows, nh, nchunk, side=None):
    n = rows[0][0].shape[0]
    nb = n // (CH * nchunk)
    nin = len(rows)
    w = nh * HD

    def body(*refs):
        o_ref, hist_ref, st = refs[nin], refs[nin + 1], refs[nin + 2]

        @pl.when(pl.program_id(0) == 0)
        def _():
            st[...] = jnp.zeros_like(st)

        s0 = st[...]
        hist_ref[0] = s0.astype(hist_ref.dtype)
        o, s1 = f(*[r[...].reshape(nb * CH, w) for r in refs[:nin]], s0)
        o_ref[...] = o.reshape(nb, CH, w)
        st[...] = s1

    seq3 = lambda a: a.reshape(nb, nchunk * CH, a.shape[1])
    res = _pcall(
        side, body, name=name, grid=(nchunk,),
        in_specs=[pl.BlockSpec((nb, CH, w), functools.partial(lambda c, base: (0, c, base), base=b)) for (_, b) in rows],
        out_specs=[pl.BlockSpec((nb, CH, w), lambda c: (0, c, 0)), pl.BlockSpec((1, nb * w, HD), lambda c: (c, 0, 0))],
        out_shape=[S((nb, nchunk * CH, w), f32), S((nchunk, nb * w, HD), bf16)],
        scratch_shapes=[pltpu.VMEM((nb * w, HD), f32)],
        compiler_params=pltpu.CompilerParams(dimension_semantics=("arbitrary",)),
    )(*[seq3(r[0]) for r in rows])
    (o, hist), extra = (res, None) if side is None else res
    out = [o.reshape(n, w), hist]
    return out if side is None else (out, extra)


def _scan_bwd(name, f, rows, hist, do, nh, nchunk, side=None, dtypes=None):
    n = rows[0][0].shape[0]
    nb = n // (CH * nchunk)
    nin = len(rows)
    w = nh * HD

    def body(*refs):
        hist_ref, do_ref = refs[nin], refs[nin + 1]
        outs = refs[nin + 2:nin + 2 + nin]
        ds = refs[nin + 2 + nin]

        @pl.when(pl.program_id(0) == 0)
        def _():
            ds[...] = jnp.zeros_like(ds)

        _, vjp = jax.vjp(f, *[r[...].reshape(nb * CH, w) for r in refs[:nin]], hist_ref[0].astype(f32))
        grads = vjp((do_ref[...].reshape(nb * CH, w), ds[...]))
        for r, gk in zip(outs, grads[:nin]):
            r[...] = gk.reshape(nb, CH, w).astype(r.dtype)
        ds[...] = grads[nin]

    seq3 = lambda a: a.reshape(nb, nchunk * CH, a.shape[1])
    rev = lambda c, base: (0, nchunk - 1 - c, base)
    res = _pcall(
        side, body, name=name, grid=(nchunk,),
        in_specs=[pl.BlockSpec((nb, CH, w), functools.partial(rev, base=b)) for (_, b) in rows]
        + [pl.BlockSpec((1, nb * w, HD), lambda c: (nchunk - 1 - c, 0, 0)), pl.BlockSpec((nb, CH, w), functools.partial(rev, base=0))],
        out_specs=[pl.BlockSpec((nb, CH, w), functools.partial(rev, base=0)) for _ in rows],
        out_shape=[S((nb, nchunk * CH, w), dt) for dt in (dtypes or [f32] * nin)],
        scratch_shapes=[pltpu.VMEM((nb * w, HD), f32)],
        compiler_params=pltpu.CompilerParams(dimension_semantics=("arbitrary",)),
    )(*[seq3(r[0]) for r in rows], hist, seq3(do))
    outs, extra = (res, None) if side is None else res
    outs = [o.reshape(n, w) for o in outs]
    return outs if side is None else (outs, extra)


_VMEM_LIMIT = 56 * 2 ** 20
_VMEM_TILE_BUDGET = 40 * 2 ** 20


def _mm_tiles(m, n, k, sa, sb, so, sx, a_f32, b_f32, tn_fixed):
    best = None
    for tm in (1024, 512, 256, 128, 64):
        for tn in ((tn_fixed,) if tn_fixed else (1024, 768, 512, 384, 256, 128)):
            if m % tm or n % tn:
                continue
            need = 2 * (tm * k * sa + k * tn * sb + tm * tn * (so + sx)) + tm * tn * 4
            need += tm * k * (2 if sa == 4 else 0) + k * tn * (2 if sb == 4 else 0)
            need += tm * k * (4 if a_f32 else 0) + k * tn * (4 if b_f32 else 0)
            if need <= _VMEM_TILE_BUDGET and (best is None or (tm * tn, tm) > best[0]):
                best = ((tm * tn, tm), tm, tn)
    return best[1], best[2]


def _mm(name, a, b, dims, out_dtype, a_fn=None, b_fn=None, epi=None, extra=(), consts=(), outs=None, out_t=False, slab=False,
        side=None):
    m, kk = a.shape
    nn = b.shape[1] if dims is NN else b.shape[0]
    kinds = [("tile", out_dtype)] if outs is None else list(outs)
    so = sum(jnp.dtype(dt).itemsize for kd, dt in kinds if kd != "rows")
    sx = sum(e.dtype.itemsize for e in extra)
    full_rows = bool(consts) or any(kd == "rows" for kd, _ in kinds)
    tm, tn = _mm_tiles(m, nn, kk, a.dtype.itemsize, b.dtype.itemsize, so, sx,
                       a_fn is not None, b_fn is not None, _COLW if slab else (nn if full_rows else None))
    nex, nco = len(extra), len(consts)

    def body(a_ref, b_ref, *rest):
        av, bv = a_ref[...], b_ref[...]
        if a_fn is not None:
            av = a_fn(av.astype(f32))
        if b_fn is not None:
            bv = b_fn(bv.astype(f32))
        acc = _dot(av.astype(bf16), bv.astype(bf16), dims)
        if epi is not None:
            acc = epi(acc, *[r[...] for r in rest[:nex + nco]])
        vals = acc if isinstance(acc, tuple) else (acc,)
        for (kd, _), o_ref, val in zip(kinds, rest[nex + nco:], vals):
            if kd == "rows":
                @pl.when(pl.program_id(0) == 0)
                def _():
                    o_ref[...] = jnp.zeros_like(o_ref)

                o_ref[...] += val
            elif kd == "tile_t" or out_t:
                o_ref[...] = val.T.astype(o_ref.dtype)
            elif slab:
                o_ref[0] = val.astype(o_ref.dtype)
            else:
                o_ref[...] = val.astype(o_ref.dtype)

    bspec = pl.BlockSpec((kk, tn), lambda i, j: (0, j)) if dims is NN else pl.BlockSpec((tn, kk), lambda i, j: (j, 0))
    out_specs, out_shape = [], []
    for kd, dt in kinds:
        if kd == "rows":
            out_specs.append(pl.BlockSpec((1, nn), lambda i, j: (0, 0)))
            out_shape.append(S((1, nn), dt))
        elif kd == "tile_t" or out_t:
            out_specs.append(pl.BlockSpec((tn, tm), lambda i, j: (j, i)))
            out_shape.append(S((nn, m), dt))
        elif slab:
            out_specs.append(pl.BlockSpec((1, tm, tn), lambda i, j: (j, i, 0)))
            out_shape.append(S((nn // tn, m, tn), dt))
        else:
            out_specs.append(pl.BlockSpec((tm, tn), lambda i, j: (i, j)))
            out_shape.append(S((m, nn), dt))
    if outs is None:
        out_specs, out_shape = out_specs[0], out_shape[0]
    sem = ("arbitrary", "arbitrary") if any(kd == "rows" for kd, _ in kinds) else ("parallel", "parallel")
    return _pcall(
        side, body, name=name, grid=(m // tm, nn // tn),
        in_specs=[pl.BlockSpec((tm, kk), lambda i, j: (i, 0)), bspec]
        + [pl.BlockSpec((tm, tn), lambda i, j: (i, j)) for _ in extra]
        + [pl.BlockSpec(c.shape, lambda i, j: (0, 0)) for c in consts],
        out_specs=out_specs, out_shape=out_shape,
        compiler_params=pltpu.CompilerParams(dimension_semantics=sem, vmem_limit_bytes=_VMEM_LIMIT),
    )(a, b, *extra, *consts)


def _f_norm(x, g):
    return (x * lax.rsqrt(jnp.mean(x * x, axis=-1, keepdims=True) + EPS) * g,)


def _f_foxpre(zqk, gq, gk, pm):
    def nrm(t, g):
        return t * lax.rsqrt(_sel_rhs(t * t, pm) + EPS) * g
    return nrm(zqk[:, :FOX_W], gq), nrm(zqk[:, FOX_W:], gk)


def _chunk_cumsum(x):
    n = x.shape[0]
    r, c = _iota2((n, n), 0), _iota2((n, n), 1)
    tri = jnp.logical_and(r >= c, (r // CH) == (c // CH)).astype(f32)
    return _sel_lhs(tri, x)


def _f_gdngate(zs, eb, ea, alog_b, dt_b):
    beta = jax.nn.sigmoid(_sel_rhs(zs, eb))
    la = -jnp.exp(alog_b) * jax.nn.softplus(_sel_rhs(zs, ea) + dt_b)
    return beta, _chunk_cumsum(la)


def _f_conv(j, x, w):
    t = x.shape[0]
    y = x * w[3:4, :]
    for jj in range(3):
        sh = 3 - jj
        xs = jnp.concatenate([jnp.zeros((sh, x.shape[1]), f32), x[:t - sh, :]], axis=0)
        y = y + xs * w[jj:jj + 1, :]
    y = jax.nn.silu(y)
    yn = y * lax.rsqrt(jnp.sum(y * y, axis=-1, keepdims=True) + EPS)
    return (jnp.where(j < 2 * GDN_H, yn, y),)


def _head_rms(o, nh):
    outs = []
    for h in range(nh):
        oh = o[:, HD * h:HD * (h + 1)]
        outs.append(oh * lax.rsqrt(jnp.mean(oh * oh, axis=-1, keepdims=True) + EPS))
    return jnp.concatenate(outs, axis=1)


def _f_post0(fox_o, o, gg, on):
    return (jnp.concatenate([fox_o, _head_rms(o, GDN_H) * on * jax.nn.silu(gg)], axis=1),)


def _f_post1(o, zg, on):
    return (_head_rms(o, HG_H) * on * jax.nn.silu(zg),)


def _f_hpre(zqf, lbl):
    lb = jax.nn.sigmoid(lbl[1:2, :] - lbl[0:1, :])
    fg = lb + (1.0 - lb) * jax.nn.sigmoid(zqf[:, D:])
    return jax.nn.silu(zqf[:, :D]), 1.0 - fg, _chunk_cumsum(jnp.log(fg))


def _dotb(a, b, dims=NN):
    return _dot(a.astype(bf16), b.astype(bf16), dims)


def _dot3(a, b):
    ah, bh = a.astype(bf16), b.astype(bf16)
    al, bl = (a - ah.astype(f32)).astype(bf16), (b - bh.astype(f32)).astype(bf16)
    return _dot(ah, bh) + (_dot(ah, bl) + _dot(al, bh))


def _split(t, nh):
    return [t[CH * ck:CH * (ck + 1), HD * h:HD * (h + 1)] for ck in range(t.shape[0] // CH) for h in range(nh)]


def _merge(units, nh):
    return jnp.concatenate([jnp.concatenate(units[i:i + nh], axis=1) for i in range(0, len(units), nh)], axis=0)


def _inv_impl(amats):
    n = amats[0].shape[0]
    eye = jnp.where(_iota2((n, n), 0) == _iota2((n, n), 1), 1.0, 0.0).astype(f32)
    xs, ps = [eye - a for a in amats], list(amats)
    for _ in range(max(1, (n - 1).bit_length()) - 1):
        ps = [_dotb(p, p) for p in ps]
        xs = [x + _dotb(x, p) for x, p in zip(xs, ps)]
    for _ in range(2):
        rs = [eye - x - _dot3(a, x) for a, x in zip(amats, xs)]
        xs = [x + _dotb(x, r) for x, r in zip(xs, rs)]
    return tuple(xs)


@jax.custom_vjp
def _inv_unit_lower(amats):
    return _inv_impl(amats)


def _inv_fwd(amats):
    xs = _inv_impl(amats)
    return xs, xs


def _inv_bwd(xs, dxs):
    return (tuple(-_dotb(_dotb(x, dx, TN), x, NT) for x, dx in zip(xs, dxs)),)


_inv_unit_lower.defvjp(_inv_fwd, _inv_bwd)


@jax.custom_vjp
def _inv_given(amats, xs):
    return xs


def _inv_given_fwd(amats, xs):
    return xs, xs


def _inv_given_bwd(xs, dxs):
    return _inv_bwd(xs, dxs)[0], tuple(jnp.zeros_like(x) for x in xs)


_inv_given.defvjp(_inv_given_fwd, _inv_given_bwd)


def _f_gdn_intra(q, k, v, bb, gb, tinv_p=None):
    qs, ks, vs, bs, gs = (_split(t, GDN_H) for t in (q, k, v, bb, gb))
    r, cc = _iota2((CH, CH), 0), _iota2((CH, CH), 1)
    causal, strict = r >= cc, r > cc
    beta, g, gl = [b[:, :1] for b in bs], [x[:, :1] for x in gs], [x[CH - 1:CH, :1] for x in gs]
    decay = [jnp.exp(jnp.where(causal, x[:, :CH] - x[:, :CH].T, -jnp.inf)) for x in gs]
    kb = [ki * bi for ki, bi in zip(ks, beta)]
    amat = [jnp.where(strict, _dotb(kbi, ki, NT) * di, 0.0) for kbi, ki, di in zip(kb, ks, decay)]
    if tinv_p is None:
        tinv = _inv_unit_lower(tuple(amat))
    else:
        tinv = _inv_given(tuple(amat), tuple(x[:, :CH] for x in _split(tinv_p, GDN_H)))
    rhs = [jnp.concatenate([vi * bi, kbi * jnp.exp(gi)], axis=1) for vi, bi, kbi, gi in zip(vs, beta, kb, g)]
    uw = [_dotb(ti, ri) for ti, ri in zip(tinv, rhs)]
    qsc = [qi * (HD ** -0.5) for qi in qs]
    qk = [jnp.where(causal, _dotb(qi, ki, NT) * di, 0.0) for qi, ki, di in zip(qsc, ks, decay)]
    outs = ([x[:, :HD] for x in uw], [x[:, HD:] for x in uw],
            [jnp.concatenate([x, jnp.zeros_like(x)], axis=1) for x in qk],
            [qi * jnp.exp(gi) for qi, gi in zip(qsc, g)],
            [ki * jnp.exp(gli - gi) for ki, gli, gi in zip(ks, gl, g)],
            [jnp.broadcast_to(gli, (CH, HD)) for gli in gl])
    if tinv_p is None:
        outs += ([jnp.concatenate([x, jnp.zeros_like(x)], axis=1) for x in tinv],)
    return tuple(_merge(o, GDN_H) for o in outs)


def _f_gdn_inter(u, w, qkp, qd, kd, glb, st):
    us, ws, qks, qds, kds, gls = (_split(t, GDN_H) for t in (u, w, qkp, qd, kd, glb))
    sts = [st[HD * i:HD * (i + 1), :] for i in range(len(us))]
    vn = [ui - _dotb(wi, si) for ui, wi, si in zip(us, ws, sts)]
    o = [_dotb(qi, si) + _dotb(xi[:, :CH], vi) for qi, si, xi, vi in zip(qds, sts, qks, vn)]
    s2 = [si * jnp.exp(gi[:1, :1]) + _dotb(ki, vi, TN) for si, gi, ki, vi in zip(sts, gls, kds, vn)]
    return _merge(o, GDN_H), jnp.concatenate(s2, axis=0)


def _f_hgrn_chunk(q, k, b, v, st):
    qs, ks, bs, vs = (_split(t, HG_H) for t in (q, k, b, v))
    sts = [st[HD * i:HD * (i + 1), :] for i in range(len(qs))]
    causal = _iota2((CH, CH), 0) >= _iota2((CH, CH), 1)
    bl, bm = [x[CH - 1:CH, :] for x in bs], [x[CH // 2 - 1:CH // 2, :] for x in bs]
    a = [jnp.where(causal, _dotb(qi * jnp.exp(bi - mi), ki * jnp.exp(mi - bi), NT), 0.0)
         for qi, ki, bi, mi in zip(qs, ks, bs, bm)]
    o = [_dotb(qi * jnp.exp(bi), si, NT) + _dotb(ai, vi) for qi, bi, si, ai, vi in zip(qs, bs, sts, a, vs)]
    s2 = [si * jnp.exp(li) + _dotb(vi, ki * jnp.exp(li - bi), TN) for si, li, vi, ki, bi in zip(sts, bl, vs, ks, bs)]
    return _merge(o, HG_H), jnp.concatenate(s2, axis=0)


def _fox_gate_fwd(z0, fbias, t, tc=256):
    n = z0.shape[0]
    nt = t // tc

    def body(zs_ref, b_ref, ccol_ref, crow_ref, carry):
        @pl.when(pl.program_id(1) == 0)
        def _():
            carry[...] = jnp.zeros_like(carry)

        ls = jnp.where(_iota2((tc, LANES), 1) < FOX_H, jax.nn.log_sigmoid(zs_ref[...] + b_ref[...]), 0.0)
        tri = (_iota2((tc, tc), 0) >= _iota2((tc, tc), 1)).astype(f32)
        c = _dot_sel(tri, ls, exact_lhs=True) + carry[...]
        carry[...] = c[tc - 1:tc, :]
        ccol_ref[...] = c
        crow_ref[0] = c.T[:FOX_H, :]

    return pl.pallas_call(
        body, name="fox_gate_fwd", grid=(n // t, nt),
        in_specs=[pl.BlockSpec((tc, LANES), lambda b, i: (b * nt + i, Z0_SMALL)), pl.BlockSpec((1, LANES), lambda b, i: (0, 0))],
        out_specs=[pl.BlockSpec((tc, LANES), lambda b, i: (b * nt + i, 0)), pl.BlockSpec((1, FOX_H, tc), lambda b, i: (b, 0, i))],
        out_shape=[S((n, LANES), f32), S((n // t, FOX_H, t), f32)],
        scratch_shapes=[pltpu.VMEM((1, LANES), f32)],
        compiler_params=pltpu.CompilerParams(dimension_semantics=("parallel", "arbitrary")),
    )(z0, fbias)


def _fox_gate_bwd(z0, fbias, dcq, dck, t, tc=256):
    n = z0.shape[0]
    nt = t // tc

    def body(zs_ref, b_ref, dcq_ref, dck_ref, dz_ref, db_ref, carry):
        first = jnp.logical_and(pl.program_id(0) == 0, pl.program_id(1) == 0)

        @pl.when(pl.program_id(1) == 0)
        def _():
            carry[...] = jnp.zeros_like(carry)

        @pl.when(first)
        def _():
            db_ref[...] = jnp.zeros_like(db_ref)

        dc = dcq_ref[0] + dcq_ref[1] + dcq_ref[2] + dcq_ref[3]
        drow = dck_ref[0, 0] + dck_ref[1, 0] + dck_ref[2, 0] + dck_ref[3, 0]
        eye = (_iota2((FOX_H, LANES), 0) == _iota2((FOX_H, LANES), 1)).astype(f32)
        dc = dc + _dot_sel(drow, eye, TN)
        triu = (_iota2((tc, tc), 0) <= _iota2((tc, tc), 1)).astype(f32)
        dls = _dot_sel(triu, dc, exact_lhs=True) + carry[...]
        carry[...] = dls[0:1, :]
        x = zs_ref[...] + b_ref[...]
        dz = jnp.where(_iota2((tc, LANES), 1) < FOX_H, dls * jax.nn.sigmoid(-x), 0.0)
        dz_ref[...] = dz
        db_ref[...] += jnp.sum(dz, axis=0, keepdims=True)

    def rev(b, i):
        return b * nt + (nt - 1 - i)

    return pl.pallas_call(
        body, name="fox_gate_bwd", grid=(n // t, nt),
        in_specs=[pl.BlockSpec((tc, LANES), lambda b, i: (rev(b, i), Z0_SMALL)), pl.BlockSpec((1, LANES), lambda b, i: (0, 0)),
                  pl.BlockSpec((4, tc, LANES), lambda b, i: (0, rev(b, i), 0)),
                  pl.BlockSpec((4, 1, FOX_H, tc), lambda b, i: (0, b, 0, nt - 1 - i))],
        out_specs=[pl.BlockSpec((tc, LANES), lambda b, i: (rev(b, i), 0)), pl.BlockSpec((1, LANES), lambda b, i: (0, 0))],
        out_shape=[S((n, LANES), f32), S((1, LANES), f32)],
        scratch_shapes=[pltpu.VMEM((1, LANES), f32)],
        compiler_params=pltpu.CompilerParams(dimension_semantics=("arbitrary", "arbitrary")),
    )(z0, fbias, dcq, dck)


def _fox_scores(hh, p, i, tq, q, k, ccol, crow):
    kmax = k.shape[0]
    lane = _iota2((1, LANES), 1)
    mh = (lane // FOX_D) == hh
    h = 2 * p + hh
    qh = jnp.where(mh, q, 0.0).astype(bf16)
    s = _dot(qh, k, NT) * (FOX_D ** -0.5)
    cq = jnp.sum(jnp.where(lane == h, ccol, 0.0), axis=1, keepdims=True)
    ck = jnp.sum(jnp.where(_iota2((FOX_H, 1), 0) == h, crow, 0.0), axis=0, keepdims=True)
    causal = _iota2((1, kmax), 1) <= (i * tq + _iota2((tq, 1), 0))
    s = jnp.where(causal, s + cq - ck, -jnp.inf)
    pe = jnp.exp(s - jnp.max(s, axis=1, keepdims=True))
    return mh, qh, pe, jnp.sum(pe, axis=1, keepdims=True)


def _fox_attn_fwd(qn, kn, z0, ccol, crow, t, tq=256, side=None):
    n = qn.shape[0]
    nq = t // tq

    def body(q_ref, k_ref, v_ref, ccol_ref, crow_ref, o_ref):
        p = pl.program_id(1)
        k, v, crow = k_ref[...].astype(bf16), v_ref[...].astype(bf16), crow_ref[0]
        for i in range(nq):
            rows, kmax = pl.ds(i * tq, tq), (i + 1) * tq
            q, cc = q_ref[rows, :], ccol_ref[rows, :]
            acc = jnp.zeros((tq, LANES), f32)
            for hh in range(2):
                mh, _, pe, l = _fox_scores(hh, p, i, tq, q, k[:kmax], cc, crow[:, :kmax])
                acc = jnp.where(mh, _dot(pe.astype(bf16), v[:kmax]) / l, acc)
            o_ref[rows, :] = acc

    seq = lambda b, p: (b, p)
    return _pcall(
        side, body, name="fox_attn_fwd", grid=(n // t, FOX_H // 2),
        in_specs=[pl.BlockSpec((t, LANES), seq), pl.BlockSpec((t, LANES), seq), pl.BlockSpec((t, LANES), lambda b, p: (b, Z0_FV + p)),
                  pl.BlockSpec((t, LANES), lambda b, p: (b, 0)), pl.BlockSpec((1, FOX_H, t), lambda b, p: (b, 0, 0))],
        out_specs=pl.BlockSpec((t, LANES), seq),
        out_shape=S((n, FOX_W), f32),
        compiler_params=pltpu.CompilerParams(dimension_semantics=("parallel", "parallel")),
    )(qn, kn, z0, ccol, crow)


def _fox_attn_bwd(qn, kn, z0, ccol, crow, do, t, tq=256, side=None):
    n = qn.shape[0]
    nq = t // tq
    nb = n // t

    def body(q_ref, k_ref, v_ref, ccol_ref, crow_ref, do_ref, dq_ref, dk_ref, dv_ref, dcq_ref, dck_ref):
        p = pl.program_id(1)
        dk_ref[...] = jnp.zeros_like(dk_ref)
        dv_ref[...] = jnp.zeros_like(dv_ref)
        dck_ref[...] = jnp.zeros_like(dck_ref)
        kf, v, crow = k_ref[...], v_ref[...].astype(bf16), crow_ref[0]
        k = kf.astype(bf16)
        lane = _iota2((1, LANES), 1)
        sub = _iota2((FOX_H, 1), 0)
        scale = FOX_D ** -0.5
        for i in range(nq):
            rows, kmax = pl.ds(i * tq, tq), (i + 1) * tq
            q, cc, dout = q_ref[rows, :], ccol_ref[rows, :], do_ref[rows, :]
            dq = jnp.zeros((tq, LANES), f32)
            dcq = jnp.zeros((tq, LANES), f32)
            for hh in range(2):
                mh, qh, pe, l = _fox_scores(hh, p, i, tq, q, k[:kmax], cc, crow[:, :kmax])
                pr = pe / l
                doh = jnp.where(mh, dout, 0.0).astype(bf16)
                dp = _dot(doh, v[:kmax], NT)
                ds = pr * (dp - jnp.sum(pr * dp, axis=1, keepdims=True))
                dsb = ds.astype(bf16)
                dq = dq + _dot(dsb, jnp.where(mh, kf[:kmax], 0.0).astype(bf16)) * scale
                dk_ref[:kmax, :] += _dot(dsb, qh, TN) * scale
                dv_ref[:kmax, :] += _dot(pr.astype(bf16), doh, TN)
                h = 2 * p + hh
                dcq = dcq + jnp.where(lane == h, jnp.sum(ds, axis=1, keepdims=True), 0.0)
                dck_ref[0, 0, :, :kmax] += jnp.where(sub == h, -jnp.sum(ds, axis=0, keepdims=True), 0.0)
            dq_ref[rows, :] = dq
            dcq_ref[0, rows, :] = dcq

    seq = lambda b, p: (b, p)
    return _pcall(
        side, body, name="fox_attn_bwd", grid=(nb, FOX_H // 2),
        in_specs=[pl.BlockSpec((t, LANES), seq), pl.BlockSpec((t, LANES), seq), pl.BlockSpec((t, LANES), lambda b, p: (b, Z0_FV + p)),
                  pl.BlockSpec((t, LANES), lambda b, p: (b, 0)), pl.BlockSpec((1, FOX_H, t), lambda b, p: (b, 0, 0)),
                  pl.BlockSpec((t, LANES), seq)],
        out_specs=[pl.BlockSpec((t, LANES), seq), pl.BlockSpec((t, LANES), seq), pl.BlockSpec((t, LANES), seq),
                   pl.BlockSpec((1, t, LANES), lambda b, p: (p, b, 0)), pl.BlockSpec((1, 1, FOX_H, t), lambda b, p: (p, b, 0, 0))],
        out_shape=[S((n, FOX_W), f32), S((n, FOX_W), f32), S((n, FOX_W), f32), S((4, n, LANES), f32), S((4, nb, FOX_H, t), f32)],
        compiler_params=pltpu.CompilerParams(dimension_semantics=("parallel", "parallel")),
    )(qn, kn, z0, ccol, crow, do)


def _adamw_math(w, g, m, v):
    m = ADAM_B1 * m + (1.0 - ADAM_B1) * g
    v = ADAM_B2 * v + (1.0 - ADAM_B2) * (g * g)
    m_hat = m / (1.0 - ADAM_B1 ** ADAM_STEP)
    v_hat = v / (1.0 - ADAM_B2 ** ADAM_STEP)
    return -ADAM_LR * (m_hat / (jnp.sqrt(v_hat) + ADAM_EPS) + ADAM_WD * w), m, v


def _adamw_big(name, idx, gmine, recv1, recv2, w, m, v):
    r, wc = w.shape
    c = gmine.shape[2]
    tr = min(r, 256)

    def body(idx_ref, gm_ref, r1_ref, r2_ref, w_ref, m_ref, v_ref, g_ref, d_ref, nm_ref, nv_ref):
        g = gm_ref[0].astype(f32) + r1_ref[0].astype(f32)
        for k in range(3):
            g = g + r2_ref[k].astype(f32)
        g = g[:, :wc]
        d, nm, nv = _adamw_math(w_ref[...], g, m_ref[...], v_ref[...])
        g_ref[...] = g
        d_ref[...] = d
        nm_ref[...] = nm
        nv_ref[...] = nv

    row = pl.BlockSpec((tr, wc), lambda i, s: (i, 0))
    return pl.pallas_call(
        body, name=name,
        grid_spec=pltpu.PrefetchScalarGridSpec(
            num_scalar_prefetch=1, grid=(r // tr,),
            in_specs=[pl.BlockSpec((1, tr, c), lambda i, s: (s[0], i, 0)), pl.BlockSpec((1, tr, c), lambda i, s: (s[1], i, 0)),
                      pl.BlockSpec((3, tr, c), lambda i, s: (0, i, 0)), row, row, row],
            out_specs=[row, row, row, row]),
        out_shape=[S((r, wc), f32)] * 4,
        compiler_params=pltpu.CompilerParams(dimension_semantics=("parallel",)),
    )(idx, gmine, recv1, recv2, w, m, v)


def _pair_sum(name, idx, gmine, recv1):
    _, r, c = gmine.shape
    g4 = gmine.reshape(4, 2, r, c)

    def body(idx_ref, gm_ref, r1_ref, o_ref):
        o_ref[0] = (gm_ref[0, 0].astype(f32) + r1_ref[0].astype(f32)).astype(bf16)

    return pl.pallas_call(
        body, name=name,
        grid_spec=pltpu.PrefetchScalarGridSpec(
            num_scalar_prefetch=1, grid=(4,),
            in_specs=[pl.BlockSpec((1, 1, r, c), lambda ch, s: (ch, s[0], 0, 0)), pl.BlockSpec((1, r, c), lambda ch, s: (ch, 0, 0))],
            out_specs=pl.BlockSpec((1, r, c), lambda ch, s: (ch, 0, 0))),
        out_shape=S((4, r, c), bf16),
        compiler_params=pltpu.CompilerParams(dimension_semantics=("parallel",)),
    )(idx, g4, recv1)


def _adamw_small(parts, w, m, v, own_mask, own_g):
    def body(p_ref, w_ref, m_ref, v_ref, mask_ref, og_ref, g_ref, d_ref, nm_ref, nv_ref):
        g = p_ref[0]
        for k in range(1, NDEV):
            g = g + p_ref[k]
        g_ref[...] = g
        ge = jnp.where(mask_ref[...] > 0.5, og_ref[...], g)
        d, nm, nv = _adamw_math(w_ref[...], ge, m_ref[...], v_ref[...])
        d_ref[...] = d
        nm_ref[...] = nm
        nv_ref[...] = nv

    return pl.pallas_call(body, name="adamw_small", out_shape=[S(w.shape, f32)] * 4)(parts, w, m, v, own_mask, own_g)


def _sum_parts(parts):
    def body(p_ref, g_ref):
        g = p_ref[0]
        for k in range(1, NDEV):
            g = g + p_ref[k]
        g_ref[...] = g

    return pl.pallas_call(body, name="sum_parts", out_shape=S(parts.shape[1:], f32))(parts)


def _me():
    return lax.axis_index("x"), lax.axis_index("y"), lax.axis_index("c")


def _hbm_specs(n):
    return [pl.BlockSpec(memory_space=pl.ANY)] * n


def _allgather(name, xs):
    na = len(xs)

    def body(*refs):
        x_refs, out_refs = refs[:na], refs[na:2 * na]
        send_sems, recv_sems, local_sems = refs[2 * na:]
        mx, my, mc = _me()
        me, sib = (mx, my, mc), (mx, my, 1 - mc)
        chips = [(1 - mx, my), (mx, 1 - my), (1 - mx, 1 - my)]

        def slab(a, px, py, pc):
            return out_refs[a].at[4 * px + 2 * py + pc]

        def copy(a, k, block, to, own=False):
            return pltpu.make_async_remote_copy(
                src_ref=x_refs[a] if own else slab(a, *block), dst_ref=slab(a, *block),
                send_sem=send_sems.at[7 * a + k], recv_sem=recv_sems.at[7 * a + k], device_id=to, device_id_type=MESH)

        mine = [pltpu.make_async_copy(x_refs[a], slab(a, *me), local_sems.at[a]) for a in range(na)]
        first = []
        for a in range(na):
            mine[a].start()
            first += [copy(a, 0, me, sib, own=True)] + [copy(a, 1 + j, me, (*chip, mc), own=True) for j, chip in enumerate(chips)]
        for cp in first:
            cp.start()
        passed = []
        for j, chip in enumerate(chips):
            for a in range(na):
                copy(a, 1 + j, (*chip, mc), me).wait_recv()
                passed.append(copy(a, 4 + j, (*chip, mc), sib))
                passed[-1].start()
        for a in range(na):
            copy(a, 0, sib, me).wait_recv()
            for j, chip in enumerate(chips):
                copy(a, 4 + j, (*chip, 1 - mc), me).wait_recv()
        for cp in first + passed:
            cp.wait_send()
        for cp in mine:
            cp.wait()

    return pl.pallas_call(
        body, name=name, out_shape=[S((NDEV,) + x.shape, x.dtype) for x in xs],
        in_specs=_hbm_specs(na), out_specs=_hbm_specs(na),
        scratch_shapes=[pltpu.SemaphoreType.DMA((7 * na,)), pltpu.SemaphoreType.DMA((7 * na,)), pltpu.SemaphoreType.DMA((na,))],
    )(*xs)


def _rs_sibling(gs):
    na = len(gs)

    def body(*refs):
        g_refs, out_refs, send_sems, recv_sems = refs[:na], refs[na:2 * na], refs[2 * na], refs[2 * na + 1]
        mx, my, mc = _me()
        cps = [pltpu.make_async_remote_copy(
            src_ref=g_refs[a].at[2 * ch + 1 - mc], dst_ref=out_refs[a].at[ch], send_sem=send_sems.at[4 * a + ch],
            recv_sem=recv_sems.at[4 * a + ch], device_id=(mx, my, 1 - mc), device_id_type=MESH)
            for a in range(na) for ch in range(4)]
        for cp in cps:
            cp.start()
        for cp in cps:
            cp.wait_recv()
        for cp in cps:
            cp.wait_send()

    return pl.pallas_call(
        body, name="rs_sibling", out_shape=[S((4,) + g.shape[1:], g.dtype) for g in gs],
        in_specs=_hbm_specs(na), out_specs=_hbm_specs(na),
        scratch_shapes=[pltpu.SemaphoreType.DMA((4 * na,)), pltpu.SemaphoreType.DMA((4 * na,))],
    )(*gs)


def _side_allgather(xs):
    na = len(xs)

    def mk(x_refs, out_refs, sems):
        send_sems, recv_sems, local_sems = sems
        mx, my, mc = _me()
        me, sib = (mx, my, mc), (mx, my, 1 - mc)
        chips = [(1 - mx, my), (mx, 1 - my), (1 - mx, 1 - my)]

        def slab(a, px, py, pc):
            return out_refs[a].at[4 * px + 2 * py + pc]

        def copy(a, k, block, to, own=False):
            return pltpu.make_async_remote_copy(
                src_ref=x_refs[a] if own else slab(a, *block), dst_ref=slab(a, *block),
                send_sem=send_sems.at[7 * a + k], recv_sem=recv_sems.at[7 * a + k], device_id=to, device_id_type=MESH)

        mine = [pltpu.make_async_copy(x_refs[a], slab(a, *me), local_sems.at[a]) for a in range(na)]
        first = []
        for a in range(na):
            first += [copy(a, 0, me, sib, own=True)] + [copy(a, 1 + j, me, (*chip, mc), own=True) for j, chip in enumerate(chips)]
        return me, sib, chips, mc, copy, mine, first

    def start(x_refs, out_refs, sems):
        *_, mine, first = mk(x_refs, out_refs, sems)
        for cp in mine + first:
            cp.start()

    def finish(x_refs, out_refs, sems):
        me, sib, chips, mc, copy, mine, first = mk(x_refs, out_refs, sems)
        passed = []
        for j, chip in enumerate(chips):
            for a in range(na):
                copy(a, 1 + j, (*chip, mc), me).wait_recv()
                passed.append(copy(a, 4 + j, (*chip, mc), sib))
                passed[-1].start()
        for a in range(na):
            copy(a, 0, sib, me).wait_recv()
            for j, chip in enumerate(chips):
                copy(a, 4 + j, (*chip, 1 - mc), me).wait_recv()
        for cp in first + passed:
            cp.wait_send()
        for cp in mine:
            cp.wait()

    scratch = [pltpu.SemaphoreType.DMA((7 * na,)), pltpu.SemaphoreType.DMA((7 * na,)), pltpu.SemaphoreType.DMA((na,))]
    return _Side(xs, [S((NDEV,) + x.shape, x.dtype) for x in xs], scratch, start, finish)


def _side_exchange(arrs, nslot, out_slots, route):
    na = len(arrs)

    def copies(in_refs, out_refs, sems):
        send_sems, recv_sems = sems
        return [pltpu.make_async_remote_copy(
            src_ref=in_refs[a].at[src], dst_ref=out_refs[a].at[k], send_sem=send_sems.at[nslot * a + k],
            recv_sem=recv_sems.at[nslot * a + k], device_id=to, device_id_type=MESH)
            for a in range(na) for k, (src, to) in enumerate(route(*_me()))]

    def start(in_refs, out_refs, sems):
        for cp in copies(in_refs, out_refs, sems):
            cp.start()

    def finish(in_refs, out_refs, sems):
        cps = copies(in_refs, out_refs, sems)
        for cp in cps:
            cp.wait_recv()
        for cp in cps:
            cp.wait_send()

    scratch = [pltpu.SemaphoreType.DMA((nslot * na,)), pltpu.SemaphoreType.DMA((nslot * na,))]
    return _Side(arrs, [S((out_slots,) + x.shape[1:], x.dtype) for x in arrs], scratch, start, finish)


def _side_rs_sibling(gs):
    return _side_exchange(gs, 4, 4, lambda mx, my, mc: [(2 * ch + 1 - mc, (mx, my, 1 - mc)) for ch in range(4)])


def _side_rs_chips(ps):
    return _side_exchange(ps, 3, 3, lambda mx, my, mc: [(2 * cx + cy, (cx, cy, mc)) for cx, cy in
                                                        [(1 - mx, my), (mx, 1 - my), (1 - mx, 1 - my)]])


_COLW = 512
_COL_NAMES = ("l0_w_in", "l0_w_ff1", "l1_w_in", "l1_w_ff1")
_ROW_NAMES = ("l0_w_out", "l0_w_ff2", "l1_w_out", "l1_w_ff2")
_BIG_NAMES = _COL_NAMES + _ROW_NAMES


def _full_weight(gathered, name, ncols):
    if name in _ROW_NAMES:
        return gathered.reshape(-1, D)
    return gathered[:, :, :ncols].transpose(1, 0, 2).reshape(D, NDEV * ncols)


def _regroup_w_in0(w):
    main = jnp.concatenate([w[:, 0:1536], w[:, 1544:3080], w[:, 3088:3600]], axis=1)
    small = jnp.concatenate([w[:, 1536:1544], w[:, 3080:3088]], axis=1)
    return jnp.concatenate([main, small, jnp.zeros((D, ZW0 - 3584 - 16), w.dtype)], axis=1)


def _ungroup_w_in0(g):
    return jnp.concatenate([g[:, 0:1536], g[:, 3584:3592], g[:, 1536:3072], g[:, 3592:3600], g[:, 3072:3584]], axis=1)


def _col_slabs(g, ncols):
    g = g.reshape(D, NDEV, ncols).transpose(1, 0, 2)
    return jnp.pad(g, ((0, 0), (0, 0), (0, _COLW - ncols)))


def _sq(t):
    return t * t


def _epi_res_norm(acc, res, gain):
    y = acc + res
    h = _f_norm(y, gain)[0]
    return y, h, h


_RES_NORM_OUTS = [("tile", f32), ("tile", bf16), ("tile_t", bf16)]


def _epi_norm_bwd(acc, x, dres, gain):
    _, vjp = jax.vjp(lambda xx, gg: _f_norm(xx, gg)[0], x, gain)
    dx, dgain = vjp(acc)
    dx = dx + dres
    return dx, dgain, dx


_NORM_BWD_OUTS = [("tile", f32), ("rows", f32), ("tile_t", bf16)]


def _epi_loss(acc, res, tgt):
    e = acc + res - tgt
    dy = e * (1.0 / D)
    return dy, dy, jnp.sum(e * e, axis=0, keepdims=True)


_LOSS_OUTS = [("tile", f32), ("tile_t", bf16), ("rows", f32)]


def _mlp_fwd(tag, x, h, w1, w2, epi, extra, consts, outs):
    a = _mm(f"{tag}_ff1", h, w1, NN, bf16, epi=lambda acc: jnp.maximum(acc, 0.0))
    return _mm(f"{tag}_ff2", a, w2, NN, f32, a_fn=_sq, epi=epi, extra=(x,) + tuple(extra), consts=consts, outs=outs), a


def _mlp_bwd(tag, x, gain, w1, w2, ht, a, dy, dyt, side=None):
    da = _mm(f"{tag}_ff2_dx", dy, w2, NT, bf16, epi=lambda acc, av: acc * 2.0 * av.astype(f32), extra=(a,), side=side)
    if side is not None:
        da, side_res = da
    dw2 = _mm(f"{tag}_ff2_dw", dyt, a, NN, bf16, b_fn=_sq, out_t=True)
    res = _mm(f"{tag}_ff1_dx", da, w1, NT, f32, epi=_epi_norm_bwd, extra=(x, dy), consts=(gain,), outs=_NORM_BWD_OUTS)
    dw1 = _mm(f"{tag}_ff1_dw", ht, da, NN, bf16, slab=True)
    return (res, dw1, dw2) if side is None else ((res, dw1, dw2), side_res)


def _row(v):
    return v.reshape(1, -1).astype(f32)


_L0_REST = ("l0_w_ff1", "l0_w_out", "l0_w_ff2")
_L1_MIX = ("l1_w_in", "l1_w_out")
_L1_FFN = ("l1_w_ff1", "l1_w_ff2")
_GRAD_A = ("l1_w_ff1", "l1_w_ff2", "l1_w_out", "l1_w_in")
_GRAD_B = ("l0_w_ff1", "l0_w_ff2", "l0_w_out")


def _train(x, tgt, args, mom, var, t):
    n = x.shape[0]
    nchunk = t // CH
    mx, my, mc = _me()
    dev, chip = 4 * mx + 2 * my + mc, 2 * mx + my
    core_idx = jnp.reshape(mc, (1,)).astype(jnp.int32)
    own_idx = jnp.stack([dev, chip]).astype(jnp.int32)
    ncols = {nm: args[nm].shape[1] for nm in _COL_NAMES}
    conv_cols = args["l0_gdn_conv"].shape[1]
    g, big, w = {}, {}, {}

    def send(nm):
        a = args[nm].astype(bf16)
        return jnp.pad(a, ((0, 0), (0, _COLW - ncols[nm]))) if nm in _COL_NAMES else a

    def take(names, gathered):
        for nm, arr in zip(names, gathered):
            w[nm] = _full_weight(arr, nm, ncols.get(nm, 0))

    def by_dev(nm, ga):
        return ga if nm in _COL_NAMES else ga.reshape(NDEV, -1, D)

    def pair(names, gs, r1s):
        return [_pair_sum(f"rs_pair_sum_{nm}", core_idx, ga, r1) for nm, ga, r1 in zip(names, gs, r1s)]

    def adam(names, gs, r1s, r2s):
        for nm, ga, r1, r2 in zip(names, gs, r1s, r2s):
            big[nm] = _adamw_big(f"adamw_{nm}", own_idx, ga, r1, r2, args[nm], mom[nm], var[nm])


    li = jnp.arange(FOX_W)
    pm = jnp.where((li[:, None] // FOX_D) == (li[None, :] // FOX_D), 1.0 / FOX_D, 0.0).astype(f32)
    lane_head = jnp.arange(GDN_W) // HD
    sel = lambda first_lane: (jnp.arange(LANES)[:, None] == (first_lane + lane_head)[None, :]).astype(f32)
    e_beta, e_alpha = sel(FOX_H), sel(FOX_H + GDN_H)
    alog_b, dt_b = _row(jnp.repeat(args["l0_gdn_A_log"], HD)), _row(jnp.repeat(args["l0_gdn_dt_bias"], HD))
    gq_t, gk_t = _row(jnp.tile(args["l0_fox_q_norm"], FOX_H)), _row(jnp.tile(args["l0_fox_k_norm"], FOX_H))
    on0_t, on1_t = _row(jnp.tile(args["l0_gdn_o_norm"], GDN_H)), _row(jnp.tile(args["l1_hgrn_o_norm"], HG_H))
    fbias = jnp.pad(_row(args["l0_fox_f_bias"]), ((0, 0), (0, LANES - FOX_H)))
    g0m, g0f, g1m, g1f = (_row(args[k]) for k in ("l0_mix_norm", "l0_ffn_norm", "l1_mix_norm", "l1_ffn_norm"))
    lbl = args["hgrn_lb_logits"].astype(f32)

    first = [send("l0_w_in"), jnp.pad(args["l0_gdn_conv"], ((0, 4), (0, LANES * 2 - conv_cols)))]
    (h0, h0t), first = _tok_fwd("l0_mix_norm", _f_norm, [(x, D, 0)], [(g0m, None, 0)], [(D, bf16)], 256, also_t=(0,),
                                side=_side_allgather(first))
    take(("l0_w_in",), first[:1])
    w_in0 = _regroup_w_in0(w["l0_w_in"])
    wconv = first[1][:, :4, :conv_cols].transpose(1, 0, 2).reshape(4, NDEV * conv_cols)
    z0 = _mm("l0_in", h0, w_in0, NN, f32)
    qk_rows = [(z0, 2 * FOX_W, 0)]
    qk_consts = [(gq_t, None, 0), (gk_t, None, 0), (pm, None, 0)]
    qn, kn = _tok_fwd("fox_pre", _f_foxpre, qk_rows, qk_consts, [(FOX_W, f32)] * 2, 256)
    ccol, crow = _fox_gate_fwd(z0, fbias, t)
    fox_o, got = _fox_attn_fwd(qn, kn, z0, ccol, crow, t, side=_side_allgather([send(nm) for nm in _L0_REST]))
    take(_L0_REST, got)
    conv_rows, conv_consts = [(z0, LANES, Z0_GQKV)], [(wconv, LANES, 0)]
    (qkv,) = _tok_fwd("gdn_conv", _f_conv, conv_rows, conv_consts, [(LANES, f32)], t, ncb=12, with_j=True)
    gate_rows = [(z0, LANES, Z0_SMALL)]
    gate_consts = [(e_beta, None, 0), (e_alpha, None, 0), (alog_b, None, 0), (dt_b, None, 0)]
    beta_b, g_b = _tok_fwd("gdn_gate", _f_gdngate, gate_rows, gate_consts, [(GDN_W, f32)] * 2, 256)
    intra_rows = [(qkv, GDN_W, 0), (qkv, GDN_W, 1), (qkv, GDN_W, 2), (beta_b, GDN_W, 0), (g_b, GDN_W, 0)]
    intra, got = _tok_fwd("gdn_intra", _f_gdn_intra, intra_rows, [], [(GDN_W, f32)] * 7, 2 * CH,
                          side=_side_allgather([send(nm) for nm in _L1_MIX]))
    take(_L1_MIX, got)
    inter_rows = [(a, 0) for a in intra[:6]]
    (gdn_o, gdn_hist), got = _scan_fwd("gdn_scan", _f_gdn_inter, inter_rows, GDN_H, nchunk,
                                       side=_side_allgather([send(nm) for nm in _L1_FFN[1:]]))
    take(_L1_FFN[1:], got)
    post0_rows, post0_consts = [(fox_o, FOX_W, 0), (gdn_o, GDN_W, 0), (z0, GDN_W, Z0_GG // 4)], [(on0_t, None, 0)]
    cat0, cat0t = _tok_fwd("l0_post", _f_post0, post0_rows, post0_consts, [(D, bf16)], 256, also_t=(0,))
    x1, hf0, hf0t = _mm("l0_out", cat0, w["l0_w_out"], NN, f32, epi=_epi_res_norm, extra=(x,), consts=(g0f,), outs=_RES_NORM_OUTS)
    (x2, h1, h1t), a0 = _mlp_fwd("l0", x1, hf0, w["l0_w_ff1"], w["l0_w_ff2"], _epi_res_norm, (), (g1m,), _RES_NORM_OUTS)

    z1 = _mm("l1_in", h1, w["l1_w_in"], NN, f32)
    hpre_rows, hpre_consts = [(z1, 2 * D, 0)], [(lbl, None, 0)]
    hq, hk, hb = _tok_fwd("hgrn_pre", _f_hpre, hpre_rows, hpre_consts, [(D, f32)] * 3, 256)
    hg_rows = [(hq, 0), (hk, 0), (hb, 0), (z1, 2)]
    (hg_o, hg_hist), got = _scan_fwd("hgrn_scan", _f_hgrn_chunk, hg_rows, HG_H, nchunk,
                                     side=_side_allgather([send(nm) for nm in _L1_FFN[:1]]))
    take(_L1_FFN[:1], got)
    post1_rows, post1_consts = [(hg_o, D, 0), (z1, D, 3)], [(on1_t, None, 0)]
    cat1, cat1t = _tok_fwd("l1_post", _f_post1, post1_rows, post1_consts, [(D, bf16)], 256, also_t=(0,))
    x3, hf1, hf1t = _mm("l1_out", cat1, w["l1_w_out"], NN, f32, epi=_epi_res_norm, extra=(x2,), consts=(g1f,), outs=_RES_NORM_OUTS)
    (dy, dyt, loss_row), a1 = _mlp_fwd("l1", x3, hf1, w["l1_w_ff1"], w["l1_w_ff2"], _epi_loss, (tgt,), (), _LOSS_OUTS)

    (dx3, g["l1_ffn_norm"], _), ga_ff1, ga_ff2 = _mlp_bwd("l1", x3, g1f, w["l1_w_ff1"], w["l1_w_ff2"], hf1t, a1, dy, dyt)
    dcat1 = _mm("l1_out_dx", dx3, w["l1_w_out"], NT, f32)
    ga_out = _mm("l1_out_dw", cat1t, dx3, NN, bf16)
    dhg_o, dzg, don1 = _tok_bwd("l1_post_bwd", _f_post1, post1_rows, post1_consts, [(dcat1, D)], 256, [0, 1], [0],
                                drow_dtype=[f32, bf16])
    dhq, dhk, dhb, dzi = _scan_bwd("hgrn_scan_bwd", _f_hgrn_chunk, hg_rows, hg_hist, dhg_o, HG_H, nchunk, dtypes=[f32, f32, f32, bf16])
    dzqf, dlbl = _tok_bwd("hgrn_pre_bwd", _f_hpre, hpre_rows, hpre_consts, [(dhq, D), (dhk, D), (dhb, D)], 256, [0], [0],
                          drow_dtype=bf16)
    dz1 = jnp.concatenate([dzqf, dzi, dzg], axis=1)
    dx2, g["l1_mix_norm"], dx2t = _mm("l1_in_dx", dz1, w["l1_w_in"], NT, f32, epi=_epi_norm_bwd, extra=(x2, dx3), consts=(g1m,),
                                      outs=_NORM_BWD_OUTS)
    ga_in = _mm("l1_in_dw", h1t, dz1, NN, bf16, slab=True)
    g["l1_hgrn_o_norm"] = don1.reshape(HG_H, HD).sum(0)
    g["hgrn_lb_logits"] = dlbl
    gs_a = [by_dev(nm, ga) for nm, ga in zip(_GRAD_A, (ga_ff1, ga_ff2, ga_out, ga_in))]

    ((dx1, g["l0_ffn_norm"], _), gb_ff1, gb_ff2), r1_a = _mlp_bwd("l0", x1, g0f, w["l0_w_ff1"], w["l0_w_ff2"], hf0t, a0, dx2, dx2t,
                                                                  side=_side_rs_sibling(gs_a))
    pairs_a = pair(_GRAD_A, gs_a, r1_a)
    dcat0 = _mm("l0_out_dx", dx1, w["l0_w_out"], NT, f32)
    gb_out = _mm("l0_out_dw", cat0t, dx1, NN, bf16)
    gs_b = [by_dev(nm, ga) for nm, ga in zip(_GRAD_B, (gb_ff1, gb_ff2, gb_out))]
    dfox_o, dgdn_o, dgg, don0 = _tok_bwd("l0_post_bwd", _f_post0, post0_rows, post0_consts, [(dcat0, D)], 256, [0, 1, 2], [0],
                                         drow_dtype=[f32, f32, bf16])
    (dqn, dkn, dfv, dcq, dck), got = _fox_attn_bwd(qn, kn, z0, ccol, crow, dfox_o, t,
                                                   side=_join_sides([_side_rs_chips(pairs_a), _side_rs_sibling(gs_b)]))
    r2_a, r1_b = got[:len(_GRAD_A)], got[len(_GRAD_A):]
    adam(_GRAD_A, gs_a, r1_a, r2_a)
    pairs_b = pair(_GRAD_B, gs_b, r1_b)
    dinter = _scan_bwd("gdn_scan_bwd", _f_gdn_inter, inter_rows, gdn_hist, dgdn_o, GDN_H, nchunk)
    (dq_g, dk_g, dv_g, dbeta_b, dg_b), r2_b = _tok_bwd("gdn_intra_bwd", _f_gdn_intra, intra_rows + [(intra[6], GDN_W, 0)], [],
                                                       [(a, GDN_W) for a in dinter], 2 * CH, [0, 1, 2, 3, 4], [],
                                                       side=_side_rs_chips(pairs_b))
    adam(_GRAD_B, gs_b, r1_b, r2_b)
    dqkv = jnp.concatenate([dq_g, dk_g, dv_g], axis=1)
    dzs_g, dalog_b, ddt_b = _tok_bwd("gdn_gate_bwd", _f_gdngate, gate_rows, gate_consts, [(dbeta_b, GDN_W), (dg_b, GDN_W)], 256, [0], [2, 3])
    dgqkv, dwconv = _tok_bwd("gdn_conv_bwd", _f_conv, conv_rows, conv_consts, [(dqkv, LANES)], t, [0], [0], ncb=12, with_j=True,
                             drow_dtype=bf16)
    dzs_f, dfb = _fox_gate_bwd(z0, fbias, dcq, dck, t)
    dzqk, dgq_t, dgk_t = _tok_bwd("fox_pre_bwd", _f_foxpre, qk_rows, qk_consts, [(dqn, FOX_W), (dkn, FOX_W)], 256, [0], [0, 1],
                                  drow_dtype=bf16)
    dz0 = jnp.concatenate([dzqk, dfv.astype(bf16), dgqkv, dgg, (dzs_g + dzs_f).astype(bf16), jnp.zeros((n, ZW0 - 3712), bf16)], axis=1)
    gs_c = [_col_slabs(_ungroup_w_in0(_mm("l0_in_dw", h0t, dz0, NN, bf16)), ncols["l0_w_in"])]
    r1_c = _rs_sibling(gs_c)
    pairs_c = pair(("l0_w_in",), gs_c, r1_c)
    (dx, g["l0_mix_norm"], _), r2_c = _mm("l0_in_dx", dz0, w_in0, NT, f32, epi=_epi_norm_bwd, extra=(x, dx1), consts=(g0m,),
                                          outs=_NORM_BWD_OUTS, side=_side_rs_chips(pairs_c))
    adam(("l0_w_in",), gs_c, r1_c, r2_c)
    g["l0_fox_q_norm"] = dgq_t.reshape(FOX_H, FOX_D).sum(0)
    g["l0_fox_k_norm"] = dgk_t.reshape(FOX_H, FOX_D).sum(0)
    g["l0_fox_f_bias"] = dfb[0, :FOX_H]
    g["l0_gdn_conv"] = dwconv
    g["l0_gdn_A_log"] = dalog_b.reshape(GDN_H, HD).sum(1)
    g["l0_gdn_dt_bias"] = ddt_b.reshape(GDN_H, HD).sum(1)
    g["l0_gdn_o_norm"] = don0.reshape(GDN_H, HD).sum(0)
    return loss_row, dx, g, big


_NAMES = ("l0_mix_norm", "l0_w_in", "l0_fox_q_norm", "l0_fox_k_norm", "l0_fox_f_bias", "l0_gdn_conv", "l0_gdn_A_log",
          "l0_gdn_dt_bias", "l0_gdn_o_norm", "l0_w_out", "l0_ffn_norm", "l0_w_ff1", "l0_w_ff2", "l1_mix_norm", "l1_w_in",
          "l1_hgrn_o_norm", "l1_w_out", "l1_ffn_norm", "l1_w_ff1", "l1_w_ff2", "hgrn_lb_logits")
_SMALL_NAMES = tuple(nm for nm in _NAMES if nm not in _BIG_NAMES)
_SMALL_ROWS = 16


def _pack_small(vals):
    flat = jnp.concatenate([vals[nm].reshape(-1).astype(f32) for nm in _SMALL_NAMES])
    return jnp.pad(flat, (0, _SMALL_ROWS * D - flat.shape[0])).reshape(_SMALL_ROWS, D)


def _unpack_small(packed, shapes):
    flat = packed.reshape(-1)
    out, off = {}, 0
    for nm in _SMALL_NAMES:
        size = 1
        for s in shapes[nm]:
            size *= s
        out[nm] = flat[off:off + size].reshape(shapes[nm])
        off += size
    return out, off


def kernel(x, l0_mix_norm, l0_w_in, l0_fox_q_norm, l0_fox_k_norm, l0_fox_f_bias, l0_gdn_conv, l0_gdn_A_log, l0_gdn_dt_bias, l0_gdn_o_norm, l0_w_out, l0_ffn_norm, l0_w_ff1, l0_w_ff2, l1_mix_norm, l1_w_in, l1_hgrn_o_norm, l1_w_out, l1_ffn_norm, l1_w_ff1, l1_w_ff2, hgrn_lb_logits, loss_target, m_l0_mix_norm, m_l0_w_in, m_l0_fox_q_norm, m_l0_fox_k_norm, m_l0_fox_f_bias, m_l0_gdn_conv, m_l0_gdn_A_log, m_l0_gdn_dt_bias, m_l0_gdn_o_norm, m_l0_w_out, m_l0_ffn_norm, m_l0_w_ff1, m_l0_w_ff2, m_l1_mix_norm, m_l1_w_in, m_l1_hgrn_o_norm, m_l1_w_out, m_l1_ffn_norm, m_l1_w_ff1, m_l1_w_ff2, m_hgrn_lb_logits, v_l0_mix_norm, v_l0_w_in, v_l0_fox_q_norm, v_l0_fox_k_norm, v_l0_fox_f_bias, v_l0_gdn_conv, v_l0_gdn_A_log, v_l0_gdn_dt_bias, v_l0_gdn_o_norm, v_l0_w_out, v_l0_ffn_norm, v_l0_w_ff1, v_l0_w_ff2, v_l1_mix_norm, v_l1_w_in, v_l1_hgrn_o_norm, v_l1_w_out, v_l1_ffn_norm, v_l1_w_ff1, v_l1_w_ff2, v_hgrn_lb_logits):
    args = dict(zip(_NAMES, (l0_mix_norm, l0_w_in, l0_fox_q_norm, l0_fox_k_norm, l0_fox_f_bias, l0_gdn_conv, l0_gdn_A_log, l0_gdn_dt_bias, l0_gdn_o_norm, l0_w_out, l0_ffn_norm, l0_w_ff1, l0_w_ff2, l1_mix_norm, l1_w_in, l1_hgrn_o_norm, l1_w_out, l1_ffn_norm, l1_w_ff1, l1_w_ff2, hgrn_lb_logits)))
    mom = dict(zip(_NAMES, (m_l0_mix_norm, m_l0_w_in, m_l0_fox_q_norm, m_l0_fox_k_norm, m_l0_fox_f_bias, m_l0_gdn_conv, m_l0_gdn_A_log, m_l0_gdn_dt_bias, m_l0_gdn_o_norm, m_l0_w_out, m_l0_ffn_norm, m_l0_w_ff1, m_l0_w_ff2, m_l1_mix_norm, m_l1_w_in, m_l1_hgrn_o_norm, m_l1_w_out, m_l1_ffn_norm, m_l1_w_ff1, m_l1_w_ff2, m_hgrn_lb_logits)))
    var = dict(zip(_NAMES, (v_l0_mix_norm, v_l0_w_in, v_l0_fox_q_norm, v_l0_fox_k_norm, v_l0_fox_f_bias, v_l0_gdn_conv, v_l0_gdn_A_log, v_l0_gdn_dt_bias, v_l0_gdn_o_norm, v_l0_w_out, v_l0_ffn_norm, v_l0_w_ff1, v_l0_w_ff2, v_l1_mix_norm, v_l1_w_in, v_l1_hgrn_o_norm, v_l1_w_out, v_l1_ffn_norm, v_l1_w_ff1, v_l1_w_ff2, v_hgrn_lb_logits)))
    nb, t, _ = x.shape
    dev = 4 * lax.axis_index("x") + 2 * lax.axis_index("y") + lax.axis_index("c")
    conv_cols = l0_gdn_conv.shape[1]
    loss_row, dx, g, big = _train(x.reshape(nb * t, D), loss_target.reshape(nb * t, D), args, mom, var, t)

    shapes = {nm: args[nm].shape for nm in _SMALL_NAMES}
    gsm = dict(g)
    gsm["l0_gdn_conv"] = jnp.zeros(shapes["l0_gdn_conv"], f32)
    packed = _pack_small(gsm)
    _, used = _unpack_small(packed, shapes)
    flat_extra = jnp.concatenate([jnp.sum(loss_row).reshape(1), g["l0_gdn_conv"].reshape(-1)])
    packed = packed.reshape(-1).at[used:used + flat_extra.shape[0]].set(flat_extra).reshape(_SMALL_ROWS, D)
    (parts,) = _allgather("ag_small", [packed])
    total = _sum_parts(parts).reshape(-1)
    loss = 0.5 * total[used] / D
    conv_g_full = total[used + 1:used + 1 + 4 * NDEV * conv_cols].reshape(4, NDEV * conv_cols)
    conv_g = lax.dynamic_slice(conv_g_full, (0, dev * conv_cols), (4, conv_cols))
    own_vals = {nm: jnp.zeros(shapes[nm], f32) for nm in _SMALL_NAMES}
    own_vals["l0_gdn_conv"] = conv_g
    own_mask = {nm: jnp.zeros(shapes[nm], f32) for nm in _SMALL_NAMES}
    own_mask["l0_gdn_conv"] = jnp.ones(shapes["l0_gdn_conv"], f32)
    small = _adamw_small(parts, _pack_small(args), _pack_small(mom), _pack_small(var), _pack_small(own_mask), _pack_small(own_vals))
    small = [_unpack_small(a, shapes)[0] for a in small]
    small[0]["l0_gdn_conv"] = conv_g

    outs = [loss, dx.reshape(nb, t, D)]
    for k in range(4):
        outs += [big[nm][k] if nm in _BIG_NAMES else small[k][nm] for nm in _NAMES]
    return tuple(outs)
```

```python
import functools

import jax
import jax.numpy as jnp
from jax import lax
from jax.experimental import pallas as pl
from jax.experimental.pallas import tpu as pltpu

f32, bf16 = jnp.float32, jnp.bfloat16
NN = (((1,), (0,)), ((), ()))
NT = (((1,), (1,)), ((), ()))
TN = (((0,), (0,)), ((), ()))
HI = lax.Precision.HIGHEST
MESH = pl.DeviceIdType.MESH
S = jax.ShapeDtypeStruct

EPS = 1e-6
D = 1024
LANES = 128
FOX_H, FOX_D, FOX_W = 8, 64, 512
GDN_H, HD, GDN_W = 4, 128, 512
HG_H = 8
CH = 64
ZW0 = 3840
NDEV = 8
ADAM_LR, ADAM_B1, ADAM_B2, ADAM_EPS, ADAM_WD, ADAM_STEP = 0.001, 0.9, 0.999, 1e-08, 0.01, 10

Z0_FQK, Z0_FV, Z0_GQKV, Z0_GG, Z0_SMALL = 0, 8, 12, 24, 28


def _dot(a, b, dims=NN, prec=None):
    return lax.dot_general(a, b, dims, precision=prec, preferred_element_type=f32)


def _iota2(shape, axis):
    return lax.broadcasted_iota(jnp.int32, shape, axis)


def _split3(x):
    x1 = x.astype(bf16)
    r = x - x1.astype(f32)
    x2 = r.astype(bf16)
    return x1, x2, (r - x2.astype(f32)).astype(bf16)


def _dot_sel(a, b, dims=NN, exact_lhs=False):
    if exact_lhs:
        return sum(_dot(a.astype(bf16), piece, dims) for piece in _split3(b))
    return sum(_dot(piece, b.astype(bf16), dims) for piece in _split3(a))


@jax.custom_vjp
def _sel_rhs(a, b):
    return _dot_sel(a, b)


_sel_rhs.defvjp(lambda a, b: (_dot_sel(a, b), b), lambda b, g: (_dot_sel(g, b, NT), jnp.zeros_like(b)))


@jax.custom_vjp
def _sel_lhs(a, x):
    return _dot_sel(a, x, exact_lhs=True)


_sel_lhs.defvjp(lambda a, x: (_dot_sel(a, x, exact_lhs=True), a), lambda a, g: (jnp.zeros_like(a), _dot_sel(a, g, TN, exact_lhs=True)))


class _Side:
    def __init__(self, ins, out_shapes, scratch, start, finish):
        self.ins, self.out_shapes, self.scratch, self.start, self.finish = list(ins), list(out_shapes), list(scratch), start, finish


def _join_sides(sides):
    def split(refs, counts):
        out, off = [], 0
        for c in counts:
            out.append(refs[off:off + c])
            off += c
        return out

    ni, no, ns = ([len(getattr(sd, a)) for sd in sides] for a in ("ins", "out_shapes", "scratch"))

    def run(which):
        def go(ins, outs, sems):
            for sd, i, o, c in zip(sides, split(ins, ni), split(outs, no), split(sems, ns)):
                getattr(sd, which)(i, o, c)
        return go

    return _Side(sum((sd.ins for sd in sides), []), sum((sd.out_shapes for sd in sides), []),
                 sum((sd.scratch for sd in sides), []), run("start"), run("finish"))


def _pcall(side, body, *, name, grid, in_specs, out_specs, out_shape, scratch_shapes=(), compiler_params=None):
    if side is None:
        return pl.pallas_call(body, name=name, grid=grid, in_specs=in_specs, out_specs=out_specs, out_shape=out_shape,
                              scratch_shapes=scratch_shapes, compiler_params=compiler_params)
    single = not isinstance(out_shape, (list, tuple))
    ospecs, oshape = ([out_specs], [out_shape]) if single else (list(out_specs), list(out_shape))
    nin, nout, nscr = len(in_specs), len(ospecs), len(scratch_shapes)
    si, so = len(side.ins), len(side.out_shapes)

    def wrapped(*refs):
        o0 = nin + si
        c0 = o0 + nout + so
        sins, souts, ssems = refs[nin:o0], refs[o0 + nout:c0], refs[c0 + nscr:]
        ids = [pl.program_id(a) for a in range(len(grid))]
        first = functools.reduce(jnp.logical_and, [i == 0 for i in ids])
        last = functools.reduce(jnp.logical_and, [i == g - 1 for i, g in zip(ids, grid)])

        @pl.when(first)
        def _():
            side.start(sins, souts, ssems)

        body(*refs[:nin], *refs[o0:o0 + nout], *refs[c0:c0 + nscr])

        @pl.when(last)
        def _():
            side.finish(sins, souts, ssems)

    call = pl.pallas_call(
        wrapped, name=name, grid=grid, in_specs=list(in_specs) + _hbm_specs(si), out_specs=ospecs + _hbm_specs(so),
        out_shape=oshape + side.out_shapes, scratch_shapes=list(scratch_shapes) + side.scratch,
        compiler_params=pltpu.CompilerParams(dimension_semantics=("arbitrary",) * len(grid),
                                             vmem_limit_bytes=getattr(compiler_params, "vmem_limit_bytes", None)))

    def run(*args):
        res = call(*args, *side.ins)
        return (res[0] if single else list(res[:nout])), list(res[nout:])

    return run


def _tok_specs(rows, consts, tm):
    specs = []
    for (_, w, base) in rows:
        specs.append(pl.BlockSpec((tm, w), functools.partial(lambda j, i, b: (i, b + j), b=base)))
    for (arr, w, base) in consts:
        if w is None:
            specs.append(pl.BlockSpec(arr.shape, lambda j, i: (0, 0)))
        else:
            specs.append(pl.BlockSpec((arr.shape[0], w), functools.partial(lambda j, i, b: (0, b + j), b=base)))
    return specs


def _tok_fwd(name, f, rows, consts, outs, tm, ncb=1, with_j=False, also_t=(), side=None):
    n = rows[0][0].shape[0]
    nin = len(rows) + len(consts)
    nout = len(outs)

    def body(*refs):
        ins = [r[...] for r in refs[:nin]]
        vals = f(pl.program_id(0), *ins) if with_j else f(*ins)
        for r, v in zip(refs[nin:nin + nout], vals):
            r[...] = v.astype(r.dtype)
        for r, k in zip(refs[nin + nout:], also_t):
            r[...] = vals[k].T.astype(r.dtype)

    return _pcall(
        side, body, name=name, grid=(ncb, n // tm),
        in_specs=_tok_specs(rows, consts, tm),
        out_specs=[pl.BlockSpec((tm, w), lambda j, i: (i, j)) for (w, _) in outs]
        + [pl.BlockSpec((outs[k][0], tm), lambda j, i: (0, i)) for k in also_t],
        out_shape=[S((n, w * ncb), dt) for (w, dt) in outs] + [S((outs[k][0], n), bf16) for k in also_t],
        compiler_params=pltpu.CompilerParams(dimension_semantics=("parallel", "parallel")),
    )(*[r[0] for r in rows], *[c[0] for c in consts])


def _tok_bwd(name, f, rows, consts, cots, tm, drow, dconst, ncb=1, with_j=False, addto=None, also_t=(), drow_dtype=f32, side=None):
    n = rows[0][0].shape[0]
    nr, nc, nct = len(rows), len(consts), len(cots)
    addto = addto or {}
    add_keys = sorted(addto)
    nadd = len(add_keys)

    def body(*refs):
        ins = [r[...] for r in refs[:nr + nc]]
        cot = [r[...] for r in refs[nr + nc:nr + nc + nct]]
        adds = refs[nr + nc + nct:nr + nc + nct + nadd]
        outs = refs[nr + nc + nct + nadd:]
        pos = list(drow) + [nr + k for k in dconst]

        def g(*dargs):
            full = list(ins)
            for p, a in zip(pos, dargs):
                full[p] = a
            return tuple(f(pl.program_id(0), *full) if with_j else f(*full))

        vals, vjp = jax.vjp(g, *[ins[p] for p in pos])
        grads = vjp(tuple(c.astype(v.dtype) for c, v in zip(cot, vals)))
        for k in range(len(drow)):
            gk = grads[k]
            if k in addto:
                gk = gk + adds[add_keys.index(k)][...]
            outs[k][...] = gk.astype(outs[k].dtype)
            if k in also_t:
                tref = outs[len(drow) + len(dconst) + list(also_t).index(k)]
                tref[...] = gk.T.astype(tref.dtype)
        first = pl.program_id(1) == 0
        for k in range(len(dconst)):
            ref = outs[len(drow) + k]

            @pl.when(first)
            def _():
                ref[...] = jnp.zeros_like(ref)

            ref[...] += grads[len(drow) + k]

    in_specs = _tok_specs(rows, consts, tm)
    in_specs += [pl.BlockSpec((tm, w), lambda j, i: (i, j)) for (_, w) in cots]
    in_specs += [pl.BlockSpec((tm, rows[drow[k]][1]), lambda j, i: (i, j)) for k in add_keys]
    out_specs = [pl.BlockSpec((tm, rows[k][1]), lambda j, i: (i, j)) for k in drow]
    dts = drow_dtype if isinstance(drow_dtype, (list, tuple)) else [drow_dtype] * len(drow)
    out_shape = [S((n, rows[k][1] * ncb), dt) for k, dt in zip(drow, dts)]
    for k in dconst:
        arr, w, _ = consts[k]
        if w is None:
            out_specs.append(pl.BlockSpec(arr.shape, lambda j, i: (0, 0)))
            out_shape.append(S(arr.shape, f32))
        else:
            out_specs.append(pl.BlockSpec((arr.shape[0], w), lambda j, i: (0, j)))
            out_shape.append(S((arr.shape[0], w * ncb), f32))
    for k in also_t:
        out_specs.append(pl.BlockSpec((rows[drow[k]][1], tm), lambda j, i: (0, i)))
        out_shape.append(S((rows[drow[k]][1], n), bf16))
    return _pcall(
        side, body, name=name, grid=(ncb, n // tm), in_specs=in_specs, out_specs=out_specs, out_shape=out_shape,
        compiler_params=pltpu.CompilerParams(dimension_semantics=("parallel", "arbitrary")),
    )(*[r[0] for r in rows], *[c[0] for c in consts], *[c[0] for c in cots], *[addto[k] for k in add_keys])


def _scan_fwd(name, f, rows, nh, nchunk, side=None, consts=()):
    n = rows[0][0].shape[0]
    nb = n // (CH * nchunk)
    nin, nco = len(rows), len(consts)
    w = nh * HD

    def body(*refs):
        o_ref, hist_ref, st = refs[nin + nco], refs[nin + nco + 1], refs[nin + nco + 2]

        @pl.when(pl.program_id(0) == 0)
        def _():
            st[...] = jnp.zeros_like(st)

        s0 = st[...]
        hist_ref[0] = s0.astype(hist_ref.dtype)
        tiles = [r[...].reshape(nb * CH, r.shape[2]) for r in refs[:nin]]
        o, s1 = f(*tiles, *[r[...] for r in refs[nin:nin + nco]], s0)
        o_ref[...] = o.reshape(nb, CH, w)
        st[...] = s1

    seq3 = lambda a: a.reshape(nb, nchunk * CH, a.shape[1])
    res = _pcall(
        side, body, name=name, grid=(nchunk,),
        in_specs=[pl.BlockSpec((nb, CH, k * w), functools.partial(lambda c, base: (0, c, base), base=b)) for (_, b, k) in rows]
        + [pl.BlockSpec(c.shape, lambda c: (0, 0)) for c in consts],
        out_specs=[pl.BlockSpec((nb, CH, w), lambda c: (0, c, 0)), pl.BlockSpec((1, nb * w, HD), lambda c: (c, 0, 0))],
        out_shape=[S((nb, nchunk * CH, w), f32), S((nchunk, nb * w, HD), bf16)],
        scratch_shapes=[pltpu.VMEM((nb * w, HD), f32)],
        compiler_params=pltpu.CompilerParams(dimension_semantics=("arbitrary",)),
    )(*[seq3(r[0]) for r in rows], *consts)
    (o, hist), extra = (res, None) if side is None else res
    out = [o.reshape(n, w), hist]
    return out if side is None else (out, extra)


def _scan_bwd(name, f, rows, hist, do, nh, nchunk, side=None, dtypes=None, consts=()):
    n = rows[0][0].shape[0]
    nb = n // (CH * nchunk)
    nin, nco = len(rows), len(consts)
    w = nh * HD

    def body(*refs):
        hist_ref, do_ref = refs[nin + nco], refs[nin + nco + 1]
        outs = refs[nin + nco + 2:nin + nco + 2 + nin]
        couts = refs[nin + nco + 2 + nin:nin + nco + 2 + nin + nco]
        ds = refs[nin + nco + 2 + nin + nco]

        @pl.when(pl.program_id(0) == 0)
        def _():
            ds[...] = jnp.zeros_like(ds)
            for c in couts:
                c[...] = jnp.zeros_like(c)

        tiles = [r[...].reshape(nb * CH, r.shape[2]) for r in refs[:nin]]
        _, vjp = jax.vjp(f, *tiles, *[r[...] for r in refs[nin:nin + nco]], hist_ref[0].astype(f32))
        grads = vjp((do_ref[...].reshape(nb * CH, w), ds[...]))
        for r, gk in zip(outs, grads[:nin]):
            r[...] = gk.reshape(r.shape).astype(r.dtype)
        for c, gk in zip(couts, grads[nin:nin + nco]):
            c[...] += gk
        ds[...] = grads[nin + nco]

    seq3 = lambda a: a.reshape(nb, nchunk * CH, a.shape[1])
    rev = lambda c, base: (0, nchunk - 1 - c, base)
    res = _pcall(
        side, body, name=name, grid=(nchunk,),
        in_specs=[pl.BlockSpec((nb, CH, k * w), functools.partial(rev, base=b)) for (_, b, k) in rows]
        + [pl.BlockSpec(c.shape, lambda c: (0, 0)) for c in consts]
        + [pl.BlockSpec((1, nb * w, HD), lambda c: (nchunk - 1 - c, 0, 0)), pl.BlockSpec((nb, CH, w), functools.partial(rev, base=0))],
        out_specs=[pl.BlockSpec((nb, CH, k * w), functools.partial(rev, base=0)) for (_, _, k) in rows]
        + [pl.BlockSpec(c.shape, lambda c: (0, 0)) for c in consts],
        out_shape=[S((nb, nchunk * CH, k * w), dt) for (_, _, k), dt in zip(rows, dtypes or [f32] * nin)]
        + [S(c.shape, f32) for c in consts],
        scratch_shapes=[pltpu.VMEM((nb * w, HD), f32)],
        compiler_params=pltpu.CompilerParams(dimension_semantics=("arbitrary",)),
    )(*[seq3(r[0]) for r in rows], *consts, hist, seq3(do))
    outs, extra = (res, None) if side is None else res
    outs = [o.reshape(n, o.shape[2]) for o in outs[:nin]] + list(outs[nin:])
    return outs if side is None else (outs, extra)


_VMEM_LIMIT = 56 * 2 ** 20
_VMEM_TILE_BUDGET = 40 * 2 ** 20


def _mm_tiles(m, n, k, sa, sb, so, sx, a_f32, b_f32, tn_fixed):
    best = None
    for tm in (1024, 512, 256, 128, 64):
        for tn in ((tn_fixed,) if tn_fixed else (1024, 768, 512, 384, 256, 128)):
            if m % tm or n % tn:
                continue
            need = 2 * (tm * k * sa + k * tn * sb + tm * tn * (so + sx)) + tm * tn * 4
            need += tm * k * (2 if sa == 4 else 0) + k * tn * (2 if sb == 4 else 0)
            need += tm * k * (4 if a_f32 else 0) + k * tn * (4 if b_f32 else 0)
            if need <= _VMEM_TILE_BUDGET and (best is None or (tm * tn, tm) > best[0]):
                best = ((tm * tn, tm), tm, tn)
    return best[1], best[2]


def _mm(name, a, b, dims, out_dtype, a_fn=None, b_fn=None, epi=None, extra=(), consts=(), outs=None, out_t=False, slab=False,
        side=None):
    m, kk = a.shape
    nn = b.shape[1] if dims is NN else b.shape[0]
    kinds = [("tile", out_dtype)] if outs is None else list(outs)
    so = sum(jnp.dtype(dt).itemsize for kd, dt in kinds if kd != "rows")
    sx = sum(e.dtype.itemsize for e in extra)
    full_rows = bool(consts) or any(kd == "rows" for kd, _ in kinds)
    tm, tn = _mm_tiles(m, nn, kk, a.dtype.itemsize, b.dtype.itemsize, so, sx,
                       a_fn is not None, b_fn is not None, _COLW if slab else (nn if full_rows else None))
    nex, nco = len(extra), len(consts)

    def body(a_ref, b_ref, *rest):
        av, bv = a_ref[...], b_ref[...]
        if a_fn is not None:
            av = a_fn(av.astype(f32))
        if b_fn is not None:
            bv = b_fn(bv.astype(f32))
        acc = _dot(av.astype(bf16), bv.astype(bf16), dims)
        if epi is not None:
            acc = epi(acc, *[r[...] for r in rest[:nex + nco]])
        vals = acc if isinstance(acc, tuple) else (acc,)
        for (kd, _), o_ref, val in zip(kinds, rest[nex + nco:], vals):
            if kd == "rows":
                @pl.when(pl.program_id(0) == 0)
                def _():
                    o_ref[...] = jnp.zeros_like(o_ref)

                o_ref[...] += val
            elif kd == "tile_t" or out_t:
                o_ref[...] = val.T.astype(o_ref.dtype)
            elif slab:
                o_ref[0] = val.astype(o_ref.dtype)
            else:
                o_ref[...] = val.astype(o_ref.dtype)

    bspec = pl.BlockSpec((kk, tn), lambda i, j: (0, j)) if dims is NN else pl.BlockSpec((tn, kk), lambda i, j: (j, 0))
    out_specs, out_shape = [], []
    for kd, dt in kinds:
        if kd == "rows":
            out_specs.append(pl.BlockSpec((1, nn), lambda i, j: (0, 0)))
            out_shape.append(S((1, nn), dt))
        elif kd == "tile_t" or out_t:
            out_specs.append(pl.BlockSpec((tn, tm), lambda i, j: (j, i)))
            out_shape.append(S((nn, m), dt))
        elif slab:
            out_specs.append(pl.BlockSpec((1, tm, tn), lambda i, j: (j, i, 0)))
            out_shape.append(S((nn // tn, m, tn), dt))
        else:
            out_specs.append(pl.BlockSpec((tm, tn), lambda i, j: (i, j)))
            out_shape.append(S((m, nn), dt))
    if outs is None:
        out_specs, out_shape = out_specs[0], out_shape[0]
    sem = ("arbitrary", "arbitrary") if any(kd == "rows" for kd, _ in kinds) else ("parallel", "parallel")
    return _pcall(
        side, body, name=name, grid=(m // tm, nn // tn),
        in_specs=[pl.BlockSpec((tm, kk), lambda i, j: (i, 0)), bspec]
        + [pl.BlockSpec((tm, tn), lambda i, j: (i, j)) for _ in extra]
        + [pl.BlockSpec(c.shape, lambda i, j: (0, 0)) for c in consts],
        out_specs=out_specs, out_shape=out_shape,
        compiler_params=pltpu.CompilerParams(dimension_semantics=sem, vmem_limit_bytes=_VMEM_LIMIT),
    )(a, b, *extra, *consts)


def _f_norm(x, g):
    return (x * lax.rsqrt(jnp.mean(x * x, axis=-1, keepdims=True) + EPS) * g,)


def _f_foxpre(zqk, gq, gk, pm):
    def nrm(t, g):
        return t * lax.rsqrt(_sel_rhs(t * t, pm) + EPS) * g
    return nrm(zqk[:, :FOX_W], gq), nrm(zqk[:, FOX_W:], gk)


def _chunk_cumsum(x):
    n = x.shape[0]
    r, c = _iota2((n, n), 0), _iota2((n, n), 1)
    tri = jnp.logical_and(r >= c, (r // CH) == (c // CH)).astype(f32)
    return _sel_lhs(tri, x)


def _f_gdngate(zs, eb, ea, alog_b, dt_b):
    beta = jax.nn.sigmoid(_sel_rhs(zs, eb))
    la = -jnp.exp(alog_b) * jax.nn.softplus(_sel_rhs(zs, ea) + dt_b)
    return beta, _chunk_cumsum(la)


def _f_conv(j, x, w):
    t = x.shape[0]
    y = x * w[3:4, :]
    for jj in range(3):
        sh = 3 - jj
        xs = jnp.concatenate([jnp.zeros((sh, x.shape[1]), f32), x[:t - sh, :]], axis=0)
        y = y + xs * w[jj:jj + 1, :]
    y = jax.nn.silu(y)
    yn = y * lax.rsqrt(jnp.sum(y * y, axis=-1, keepdims=True) + EPS)
    return (jnp.where(j < 2 * GDN_H, yn, y),)


def _head_rms(o, nh):
    outs = []
    for h in range(nh):
        oh = o[:, HD * h:HD * (h + 1)]
        outs.append(oh * lax.rsqrt(jnp.mean(oh * oh, axis=-1, keepdims=True) + EPS))
    return jnp.concatenate(outs, axis=1)


def _f_post0(fox_o, o, gg, on):
    return (jnp.concatenate([fox_o, _head_rms(o, GDN_H) * on * jax.nn.silu(gg)], axis=1),)


def _f_post1(o, zg, on):
    return (_head_rms(o, HG_H) * on * jax.nn.silu(zg),)


def _f_hpre(zqf, lbl):
    lb = jax.nn.sigmoid(lbl[1:2, :] - lbl[0:1, :])
    fg = lb + (1.0 - lb) * jax.nn.sigmoid(zqf[:, D:])
    return jax.nn.silu(zqf[:, :D]), 1.0 - fg, _chunk_cumsum(jnp.log(fg))


def _dotb(a, b, dims=NN):
    return _dot(a.astype(bf16), b.astype(bf16), dims)


def _dot3(a, b):
    ah, bh = a.astype(bf16), b.astype(bf16)
    al, bl = (a - ah.astype(f32)).astype(bf16), (b - bh.astype(f32)).astype(bf16)
    return _dot(ah, bh) + (_dot(ah, bl) + _dot(al, bh))


def _split(t, nh):
    return [t[CH * ck:CH * (ck + 1), HD * h:HD * (h + 1)] for ck in range(t.shape[0] // CH) for h in range(nh)]


def _merge(units, nh):
    return jnp.concatenate([jnp.concatenate(units[i:i + nh], axis=1) for i in range(0, len(units), nh)], axis=0)


def _inv_impl(amats):
    n = amats[0].shape[0]
    eye = jnp.where(_iota2((n, n), 0) == _iota2((n, n), 1), 1.0, 0.0).astype(f32)
    xs, ps = [eye - a for a in amats], list(amats)
    for _ in range(max(1, (n - 1).bit_length()) - 1):
        ps = [_dotb(p, p) for p in ps]
        xs = [x + _dotb(x, p) for x, p in zip(xs, ps)]
    for _ in range(3):
        rs = [eye - x - _dot3(a, x) for a, x in zip(amats, xs)]
        xs = [x + _dotb(x, r) for x, r in zip(xs, rs)]
    return tuple(xs)


@jax.custom_vjp
def _inv_unit_lower(amats):
    return _inv_impl(amats)


def _inv_fwd(amats):
    xs = _inv_impl(amats)
    return xs, xs


def _inv_bwd(xs, dxs):
    return (tuple(-_dotb(_dotb(x, dx, TN), x, NT) for x, dx in zip(xs, dxs)),)


_inv_unit_lower.defvjp(_inv_fwd, _inv_bwd)


@jax.custom_vjp
def _inv_given(amats, xs):
    return xs


def _inv_given_fwd(amats, xs):
    return xs, xs


def _inv_given_bwd(xs, dxs):
    return _inv_bwd(xs, dxs)[0], tuple(jnp.zeros_like(x) for x in xs)


_inv_given.defvjp(_inv_given_fwd, _inv_given_bwd)


def _f_gdn_intra(q, k, v, bb, gb, tinv_p=None):
    qs, ks, vs, bs, gs = (_split(t, GDN_H) for t in (q, k, v, bb, gb))
    r, cc = _iota2((CH, CH), 0), _iota2((CH, CH), 1)
    causal, strict = r >= cc, r > cc
    beta, g, gl = [b[:, :1] for b in bs], [x[:, :1] for x in gs], [x[CH - 1:CH, :1] for x in gs]
    decay = [jnp.exp(jnp.where(causal, x[:, :CH] - x[:, :CH].T, -jnp.inf)) for x in gs]
    kb = [ki * bi for ki, bi in zip(ks, beta)]
    amat = [jnp.where(strict, _dotb(kbi, ki, NT) * di, 0.0) for kbi, ki, di in zip(kb, ks, decay)]
    if tinv_p is None:
        tinv = _inv_unit_lower(tuple(amat))
    else:
        tinv = _inv_given(tuple(amat), tuple(x[:, :CH] for x in _split(tinv_p, GDN_H)))
    rhs = [jnp.concatenate([vi * bi, kbi * jnp.exp(gi)], axis=1) for vi, bi, kbi, gi in zip(vs, beta, kb, g)]
    uw = [_dotb(ti, ri) for ti, ri in zip(tinv, rhs)]
    qsc = [qi * (HD ** -0.5) for qi in qs]
    qk = [jnp.where(causal, _dotb(qi, ki, NT) * di, 0.0) for qi, ki, di in zip(qsc, ks, decay)]
    outs = ([x[:, :HD] for x in uw], [x[:, HD:] for x in uw],
            [jnp.concatenate([x, jnp.zeros_like(x)], axis=1) for x in qk],
            [qi * jnp.exp(gi) for qi, gi in zip(qsc, g)],
            [ki * jnp.exp(gli - gi) for ki, gli, gi in zip(ks, gl, g)],
            [jnp.broadcast_to(gli, (CH, HD)) for gli in gl])
    if tinv_p is None:
        outs += ([jnp.concatenate([x, jnp.zeros_like(x)], axis=1) for x in tinv],)
    return tuple(_merge(o, GDN_H) for o in outs)


def _f_gdn_inter(u, w, qkp, qd, kd, glb, st):
    us, ws, qks, qds, kds, gls = (_split(t, GDN_H) for t in (u, w, qkp, qd, kd, glb))
    sts = [st[HD * i:HD * (i + 1), :] for i in range(len(us))]
    vn = [ui - _dotb(wi, si) for ui, wi, si in zip(us, ws, sts)]
    o = [_dotb(qi, si) + _dotb(xi[:, :CH], vi) for qi, si, xi, vi in zip(qds, sts, qks, vn)]
    s2 = [si * jnp.exp(gi[:1, :1]) + _dotb(ki, vi, TN) for si, gi, ki, vi in zip(sts, gls, kds, vn)]
    return _merge(o, GDN_H), jnp.concatenate(s2, axis=0)


def _f_hgrn_chunk(q, k, b, v, st):
    qs, ks, bs, vs = (_split(t, HG_H) for t in (q, k, b, v))
    sts = [st[HD * i:HD * (i + 1), :] for i in range(len(qs))]
    causal = _iota2((CH, CH), 0) >= _iota2((CH, CH), 1)
    bl, bm = [x[CH - 1:CH, :] for x in bs], [x[CH // 2 - 1:CH // 2, :] for x in bs]
    a = [jnp.where(causal, _dotb(qi * jnp.exp(bi - mi), ki * jnp.exp(mi - bi), NT), 0.0)
         for qi, ki, bi, mi in zip(qs, ks, bs, bm)]
    o = [_dotb(qi * jnp.exp(bi), si, NT) + _dotb(ai, vi) for qi, bi, si, ai, vi in zip(qs, bs, sts, a, vs)]
    s2 = [si * jnp.exp(li) + _dotb(vi, ki * jnp.exp(li - bi), TN) for si, li, vi, ki, bi in zip(sts, bl, vs, ks, bs)]
    return _merge(o, HG_H), jnp.concatenate(s2, axis=0)


def _f_hgrn_gated(zqf, v, lbl, st):
    return _f_hgrn_chunk(*_f_hpre(zqf, lbl), v, st)


def _fox_gate_fwd(z0, fbias, t, tc=256):
    n = z0.shape[0]
    nt = t // tc

    def body(zs_ref, b_ref, ccol_ref, crow_ref, carry):
        @pl.when(pl.program_id(1) == 0)
        def _():
            carry[...] = jnp.zeros_like(carry)

        ls = jnp.where(_iota2((tc, LANES), 1) < FOX_H, jax.nn.log_sigmoid(zs_ref[...] + b_ref[...]), 0.0)
        tri = (_iota2((tc, tc), 0) >= _iota2((tc, tc), 1)).astype(f32)
        c = _dot_sel(tri, ls, exact_lhs=True) + carry[...]
        carry[...] = c[tc - 1:tc, :]
        ccol_ref[...] = c
        crow_ref[0] = c.T[:FOX_H, :]

    return pl.pallas_call(
        body, name="fox_gate_fwd", grid=(n // t, nt),
        in_specs=[pl.BlockSpec((tc, LANES), lambda b, i: (b * nt + i, Z0_SMALL)), pl.BlockSpec((1, LANES), lambda b, i: (0, 0))],
        out_specs=[pl.BlockSpec((tc, LANES), lambda b, i: (b * nt + i, 0)), pl.BlockSpec((1, FOX_H, tc), lambda b, i: (b, 0, i))],
        out_shape=[S((n, LANES), f32), S((n // t, FOX_H, t), f32)],
        scratch_shapes=[pltpu.VMEM((1, LANES), f32)],
        compiler_params=pltpu.CompilerParams(dimension_semantics=("parallel", "arbitrary")),
    )(z0, fbias)


def _fox_gate_bwd(z0, fbias, dcq, dck, t, tc=256):
    n = z0.shape[0]
    nt = t // tc

    def body(zs_ref, b_ref, dcq_ref, dck_ref, dz_ref, db_ref, carry):
        first = jnp.logical_and(pl.program_id(0) == 0, pl.program_id(1) == 0)

        @pl.when(pl.program_id(1) == 0)
        def _():
            carry[...] = jnp.zeros_like(carry)

        @pl.when(first)
        def _():
            db_ref[...] = jnp.zeros_like(db_ref)

        dc = dcq_ref[0] + dcq_ref[1] + dcq_ref[2] + dcq_ref[3]
        drow = dck_ref[0, 0] + dck_ref[1, 0] + dck_ref[2, 0] + dck_ref[3, 0]
        eye = (_iota2((FOX_H, LANES), 0) == _iota2((FOX_H, LANES), 1)).astype(f32)
        dc = dc + _dot_sel(drow, eye, TN)
        triu = (_iota2((tc, tc), 0) <= _iota2((tc, tc), 1)).astype(f32)
        dls = _dot_sel(triu, dc, exact_lhs=True) + carry[...]
        carry[...] = dls[0:1, :]
        x = zs_ref[...] + b_ref[...]
        dz = jnp.where(_iota2((tc, LANES), 1) < FOX_H, dls * jax.nn.sigmoid(-x), 0.0)
        dz_ref[...] = dz
        db_ref[...] += jnp.sum(dz, axis=0, keepdims=True)

    def rev(b, i):
        return b * nt + (nt - 1 - i)

    return pl.pallas_call(
        body, name="fox_gate_bwd", grid=(n // t, nt),
        in_specs=[pl.BlockSpec((tc, LANES), lambda b, i: (rev(b, i), Z0_SMALL)), pl.BlockSpec((1, LANES), lambda b, i: (0, 0)),
                  pl.BlockSpec((4, tc, LANES), lambda b, i: (0, rev(b, i), 0)),
                  pl.BlockSpec((4, 1, FOX_H, tc), lambda b, i: (0, b, 0, nt - 1 - i))],
        out_specs=[pl.BlockSpec((tc, LANES), lambda b, i: (rev(b, i), 0)), pl.BlockSpec((1, LANES), lambda b, i: (0, 0))],
        out_shape=[S((n, LANES), f32), S((1, LANES), f32)],
        scratch_shapes=[pltpu.VMEM((1, LANES), f32)],
        compiler_params=pltpu.CompilerParams(dimension_semantics=("arbitrary", "arbitrary")),
    )(z0, fbias, dcq, dck)


def _fox_scores(hh, p, i, tq, q, k, ccol, crow):
    kmax = k.shape[0]
    lane = _iota2((1, LANES), 1)
    mh = (lane // FOX_D) == hh
    h = 2 * p + hh
    qh = jnp.where(mh, q, 0.0).astype(bf16)
    s = _dot(qh, k, NT) * (FOX_D ** -0.5)
    cq = jnp.sum(jnp.where(lane == h, ccol, 0.0), axis=1, keepdims=True)
    ck = jnp.sum(jnp.where(_iota2((FOX_H, 1), 0) == h, crow, 0.0), axis=0, keepdims=True)
    causal = _iota2((1, kmax), 1) <= (i * tq + _iota2((tq, 1), 0))
    s = jnp.where(causal, s + cq - ck, -jnp.inf)
    pe = jnp.exp(s - jnp.max(s, axis=1, keepdims=True))
    return mh, qh, pe, jnp.sum(pe, axis=1, keepdims=True)


def _fox_attn_fwd(qn, kn, z0, ccol, crow, t, tq=256, side=None):
    n = qn.shape[0]
    nq = t // tq

    def body(q_ref, k_ref, v_ref, ccol_ref, crow_ref, o_ref):
        p = pl.program_id(1)
        k, v, crow = k_ref[...].astype(bf16), v_ref[...].astype(bf16), crow_ref[0]
        for i in range(nq):
            rows, kmax = pl.ds(i * tq, tq), (i + 1) * tq
            q, cc = q_ref[rows, :], ccol_ref[rows, :]
            acc = jnp.zeros((tq, LANES), f32)
            for hh in range(2):
                mh, _, pe, l = _fox_scores(hh, p, i, tq, q, k[:kmax], cc, crow[:, :kmax])
                acc = jnp.where(mh, _dot(pe.astype(bf16), v[:kmax]) / l, acc)
            o_ref[rows, :] = acc

    seq = lambda b, p: (b, p)
    return _pcall(
        side, body, name="fox_attn_fwd", grid=(n // t, FOX_H // 2),
        in_specs=[pl.BlockSpec((t, LANES), seq), pl.BlockSpec((t, LANES), seq), pl.BlockSpec((t, LANES), lambda b, p: (b, Z0_FV + p)),
                  pl.BlockSpec((t, LANES), lambda b, p: (b, 0)), pl.BlockSpec((1, FOX_H, t), lambda b, p: (b, 0, 0))],
        out_specs=pl.BlockSpec((t, LANES), seq),
        out_shape=S((n, FOX_W), f32),
        compiler_params=pltpu.CompilerParams(dimension_semantics=("parallel", "parallel")),
    )(qn, kn, z0, ccol, crow)


def _fox_attn_bwd(qn, kn, z0, ccol, crow, do, t, tq=256, side=None):
    n = qn.shape[0]
    nq = t // tq
    nb = n // t

    def body(q_ref, k_ref, v_ref, ccol_ref, crow_ref, do_ref, dq_ref, dk_ref, dv_ref, dcq_ref, dck_ref):
        p = pl.program_id(1)
        dk_ref[...] = jnp.zeros_like(dk_ref)
        dv_ref[...] = jnp.zeros_like(dv_ref)
        dck_ref[...] = jnp.zeros_like(dck_ref)
        kf, v, crow = k_ref[...], v_ref[...].astype(bf16), crow_ref[0]
        k = kf.astype(bf16)
        lane = _iota2((1, LANES), 1)
        sub = _iota2((FOX_H, 1), 0)
        scale = FOX_D ** -0.5
        for i in range(nq):
            rows, kmax = pl.ds(i * tq, tq), (i + 1) * tq
            q, cc, dout = q_ref[rows, :], ccol_ref[rows, :], do_ref[rows, :]
            dq = jnp.zeros((tq, LANES), f32)
            dcq = jnp.zeros((tq, LANES), f32)
            for hh in range(2):
                mh, qh, pe, l = _fox_scores(hh, p, i, tq, q, k[:kmax], cc, crow[:, :kmax])
                pr = pe / l
                doh = jnp.where(mh, dout, 0.0).astype(bf16)
                dp = _dot(doh, v[:kmax], NT)
                ds = pr * (dp - jnp.sum(pr * dp, axis=1, keepdims=True))
                dsb = ds.astype(bf16)
                dq = dq + _dot(dsb, jnp.where(mh, kf[:kmax], 0.0).astype(bf16)) * scale
                dk_ref[:kmax, :] += _dot(dsb, qh, TN) * scale
                dv_ref[:kmax, :] += _dot(pr.astype(bf16), doh, TN)
                h = 2 * p + hh
                dcq = dcq + jnp.where(lane == h, jnp.sum(ds, axis=1, keepdims=True), 0.0)
                dck_ref[0, 0, :, :kmax] += jnp.where(sub == h, -jnp.sum(ds, axis=0, keepdims=True), 0.0)
            dq_ref[rows, :] = dq
            dcq_ref[0, rows, :] = dcq

    seq = lambda b, p: (b, p)
    return _pcall(
        side, body, name="fox_attn_bwd", grid=(nb, FOX_H // 2),
        in_specs=[pl.BlockSpec((t, LANES), seq), pl.BlockSpec((t, LANES), seq), pl.BlockSpec((t, LANES), lambda b, p: (b, Z0_FV + p)),
                  pl.BlockSpec((t, LANES), lambda b, p: (b, 0)), pl.BlockSpec((1, FOX_H, t), lambda b, p: (b, 0, 0)),
                  pl.BlockSpec((t, LANES), seq)],
        out_specs=[pl.BlockSpec((t, LANES), seq), pl.BlockSpec((t, LANES), seq), pl.BlockSpec((t, LANES), seq),
                   pl.BlockSpec((1, t, LANES), lambda b, p: (p, b, 0)), pl.BlockSpec((1, 1, FOX_H, t), lambda b, p: (p, b, 0, 0))],
        out_shape=[S((n, FOX_W), f32), S((n, FOX_W), f32), S((n, FOX_W), f32), S((4, n, LANES), f32), S((4, nb, FOX_H, t), f32)],
        compiler_params=pltpu.CompilerParams(dimension_semantics=("parallel", "parallel")),
    )(qn, kn, z0, ccol, crow, do)


def _adamw_math(w, g, m, v):
    m = ADAM_B1 * m + (1.0 - ADAM_B1) * g
    v = ADAM_B2 * v + (1.0 - ADAM_B2) * (g * g)
    m_hat = m / (1.0 - ADAM_B1 ** ADAM_STEP)
    v_hat = v / (1.0 - ADAM_B2 ** ADAM_STEP)
    return -ADAM_LR * (m_hat / (jnp.sqrt(v_hat) + ADAM_EPS) + ADAM_WD * w), m, v


def _adamw_big(name, idx, gmine, recv1, recv2, w, m, v):
    r, wc = w.shape
    c = gmine.shape[2]
    tr = min(r, 256)

    def body(idx_ref, gm_ref, r1_ref, r2_ref, w_ref, m_ref, v_ref, g_ref, d_ref, nm_ref, nv_ref):
        g = gm_ref[0].astype(f32) + r1_ref[0].astype(f32)
        for k in range(3):
            g = g + r2_ref[k].astype(f32)
        g = g[:, :wc]
        d, nm, nv = _adamw_math(w_ref[...], g, m_ref[...], v_ref[...])
        g_ref[...] = g
        d_ref[...] = d
        nm_ref[...] = nm
        nv_ref[...] = nv

    row = pl.BlockSpec((tr, wc), lambda i, s: (i, 0))
    return pl.pallas_call(
        body, name=name,
        grid_spec=pltpu.PrefetchScalarGridSpec(
            num_scalar_prefetch=1, grid=(r // tr,),
            in_specs=[pl.BlockSpec((1, tr, c), lambda i, s: (s[0], i, 0)), pl.BlockSpec((1, tr, c), lambda i, s: (s[1], i, 0)),
                      pl.BlockSpec((3, tr, c), lambda i, s: (0, i, 0)), row, row, row],
            out_specs=[row, row, row, row]),
        out_shape=[S((r, wc), f32)] * 4,
        compiler_params=pltpu.CompilerParams(dimension_semantics=("parallel",)),
    )(idx, gmine, recv1, recv2, w, m, v)


def _pair_sum(name, idx, gmine, recv1):
    _, r, c = gmine.shape
    g4 = gmine.reshape(4, 2, r, c)

    def body(idx_ref, gm_ref, r1_ref, o_ref):
        o_ref[0] = (gm_ref[0, 0].astype(f32) + r1_ref[0].astype(f32)).astype(bf16)

    return pl.pallas_call(
        body, name=name,
        grid_spec=pltpu.PrefetchScalarGridSpec(
            num_scalar_prefetch=1, grid=(4,),
            in_specs=[pl.BlockSpec((1, 1, r, c), lambda ch, s: (ch, s[0], 0, 0)), pl.BlockSpec((1, r, c), lambda ch, s: (ch, 0, 0))],
            out_specs=pl.BlockSpec((1, r, c), lambda ch, s: (ch, 0, 0))),
        out_shape=S((4, r, c), bf16),
        compiler_params=pltpu.CompilerParams(dimension_semantics=("parallel",)),
    )(idx, g4, recv1)


def _adamw_small(parts, w, m, v, own_mask, own_g):
    def body(p_ref, w_ref, m_ref, v_ref, mask_ref, og_ref, g_ref, d_ref, nm_ref, nv_ref):
        g = p_ref[0]
        for k in range(1, NDEV):
            g = g + p_ref[k]
        g_ref[...] = g
        ge = jnp.where(mask_ref[...] > 0.5, og_ref[...], g)
        d, nm, nv = _adamw_math(w_ref[...], ge, m_ref[...], v_ref[...])
        d_ref[...] = d
        nm_ref[...] = nm
        nv_ref[...] = nv

    return pl.pallas_call(body, name="adamw_small", out_shape=[S(w.shape, f32)] * 4)(parts, w, m, v, own_mask, own_g)


def _sum_parts(parts):
    def body(p_ref, g_ref):
        g = p_ref[0]
        for k in range(1, NDEV):
            g = g + p_ref[k]
        g_ref[...] = g

    return pl.pallas_call(body, name="sum_parts", out_shape=S(parts.shape[1:], f32))(parts)


def _me():
    return lax.axis_index("x"), lax.axis_index("y"), lax.axis_index("c")


def _hbm_specs(n):
    return [pl.BlockSpec(memory_space=pl.ANY)] * n


def _allgather(name, xs):
    na = len(xs)

    def body(*refs):
        x_refs, out_refs = refs[:na], refs[na:2 * na]
        send_sems, recv_sems, local_sems = refs[2 * na:]
        mx, my, mc = _me()
        me, sib = (mx, my, mc), (mx, my, 1 - mc)
        chips = [(1 - mx, my), (mx, 1 - my), (1 - mx, 1 - my)]

        def slab(a, px, py, pc):
            return out_refs[a].at[4 * px + 2 * py + pc]

        def copy(a, k, block, to, own=False):
            return pltpu.make_async_remote_copy(
                src_ref=x_refs[a] if own else slab(a, *block), dst_ref=slab(a, *block),
                send_sem=send_sems.at[7 * a + k], recv_sem=recv_sems.at[7 * a + k], device_id=to, device_id_type=MESH)

        mine = [pltpu.make_async_copy(x_refs[a], slab(a, *me), local_sems.at[a]) for a in range(na)]
        first = []
        for a in range(na):
            mine[a].start()
            first += [copy(a, 0, me, sib, own=True)] + [copy(a, 1 + j, me, (*chip, mc), own=True) for j, chip in enumerate(chips)]
        for cp in first:
            cp.start()
        passed = []
        for j, chip in enumerate(chips):
            for a in range(na):
                copy(a, 1 + j, (*chip, mc), me).wait_recv()
                passed.append(copy(a, 4 + j, (*chip, mc), sib))
                passed[-1].start()
        for a in range(na):
            copy(a, 0, sib, me).wait_recv()
            for j, chip in enumerate(chips):
                copy(a, 4 + j, (*chip, 1 - mc), me).wait_recv()
        for cp in first + passed:
            cp.wait_send()
        for cp in mine:
            cp.wait()

    return pl.pallas_call(
        body, name=name, out_shape=[S((NDEV,) + x.shape, x.dtype) for x in xs],
        in_specs=_hbm_specs(na), out_specs=_hbm_specs(na),
        scratch_shapes=[pltpu.SemaphoreType.DMA((7 * na,)), pltpu.SemaphoreType.DMA((7 * na,)), pltpu.SemaphoreType.DMA((na,))],
    )(*xs)


def _rs_sibling(gs):
    na = len(gs)

    def body(*refs):
        g_refs, out_refs, send_sems, recv_sems = refs[:na], refs[na:2 * na], refs[2 * na], refs[2 * na + 1]
        mx, my, mc = _me()
        cps = [pltpu.make_async_remote_copy(
            src_ref=g_refs[a].at[2 * ch + 1 - mc], dst_ref=out_refs[a].at[ch], send_sem=send_sems.at[4 * a + ch],
            recv_sem=recv_sems.at[4 * a + ch], device_id=(mx, my, 1 - mc), device_id_type=MESH)
            for a in range(na) for ch in range(4)]
        for cp in cps:
            cp.start()
        for cp in cps:
            cp.wait_recv()
        for cp in cps:
            cp.wait_send()

    return pl.pallas_call(
        body, name="rs_sibling", out_shape=[S((4,) + g.shape[1:], g.dtype) for g in gs],
        in_specs=_hbm_specs(na), out_specs=_hbm_specs(na),
        scratch_shapes=[pltpu.SemaphoreType.DMA((4 * na,)), pltpu.SemaphoreType.DMA((4 * na,))],
    )(*gs)


def _side_allgather(xs):
    na = len(xs)

    def mk(x_refs, out_refs, sems):
        send_sems, recv_sems, local_sems = sems
        mx, my, mc = _me()
        me, sib = (mx, my, mc), (mx, my, 1 - mc)
        chips = [(1 - mx, my), (mx, 1 - my), (1 - mx, 1 - my)]

        def slab(a, px, py, pc):
            return out_refs[a].at[4 * px + 2 * py + pc]

        def copy(a, k, block, to, own=False):
            return pltpu.make_async_remote_copy(
                src_ref=x_refs[a] if own else slab(a, *block), dst_ref=slab(a, *block),
                send_sem=send_sems.at[7 * a + k], recv_sem=recv_sems.at[7 * a + k], device_id=to, device_id_type=MESH)

        mine = [pltpu.make_async_copy(x_refs[a], slab(a, *me), local_sems.at[a]) for a in range(na)]
        first = []
        for a in range(na):
            first += [copy(a, 0, me, sib, own=True)] + [copy(a, 1 + j, me, (*chip, mc), own=True) for j, chip in enumerate(chips)]
        return me, sib, chips, mc, copy, mine, first

    def start(x_refs, out_refs, sems):
        *_, mine, first = mk(x_refs, out_refs, sems)
        for cp in mine + first:
            cp.start()

    def finish(x_refs, out_refs, sems):
        me, sib, chips, mc, copy, mine, first = mk(x_refs, out_refs, sems)
        passed = []
        for j, chip in enumerate(chips):
            for a in range(na):
                copy(a, 1 + j, (*chip, mc), me).wait_recv()
                passed.append(copy(a, 4 + j, (*chip, mc), sib))
                passed[-1].start()
        for a in range(na):
            copy(a, 0, sib, me).wait_recv()
            for j, chip in enumerate(chips):
                copy(a, 4 + j, (*chip, 1 - mc), me).wait_recv()
        for cp in first + passed:
            cp.wait_send()
        for cp in mine:
            cp.wait()

    scratch = [pltpu.SemaphoreType.DMA((7 * na,)), pltpu.SemaphoreType.DMA((7 * na,)), pltpu.SemaphoreType.DMA((na,))]
    return _Side(xs, [S((NDEV,) + x.shape, x.dtype) for x in xs], scratch, start, finish)


def _side_exchange(arrs, nslot, out_slots, route):
    na = len(arrs)

    def copies(in_refs, out_refs, sems):
        send_sems, recv_sems = sems
        return [pltpu.make_async_remote_copy(
            src_ref=in_refs[a].at[src], dst_ref=out_refs[a].at[k], send_sem=send_sems.at[nslot * a + k],
            recv_sem=recv_sems.at[nslot * a + k], device_id=to, device_id_type=MESH)
            for a in range(na) for k, (src, to) in enumerate(route(*_me()))]

    def start(in_refs, out_refs, sems):
        for cp in copies(in_refs, out_refs, sems):
            cp.start()

    def finish(in_refs, out_refs, sems):
        cps = copies(in_refs, out_refs, sems)
        for cp in cps:
            cp.wait_recv()
        for cp in cps:
            cp.wait_send()

    scratch = [pltpu.SemaphoreType.DMA((nslot * na,)), pltpu.SemaphoreType.DMA((nslot * na,))]
    return _Side(arrs, [S((out_slots,) + x.shape[1:], x.dtype) for x in arrs], scratch, start, finish)


def _side_rs_sibling(gs):
    return _side_exchange(gs, 4, 4, lambda mx, my, mc: [(2 * ch + 1 - mc, (mx, my, 1 - mc)) for ch in range(4)])


def _side_rs_chips(ps):
    return _side_exchange(ps, 3, 3, lambda mx, my, mc: [(2 * cx + cy, (cx, cy, mc)) for cx, cy in
                                                        [(1 - mx, my), (mx, 1 - my), (1 - mx, 1 - my)]])


_COLW = 512
_COL_NAMES = ("l0_w_in", "l0_w_ff1", "l1_w_in", "l1_w_ff1")
_ROW_NAMES = ("l0_w_out", "l0_w_ff2", "l1_w_out", "l1_w_ff2")
_BIG_NAMES = _COL_NAMES + _ROW_NAMES


def _full_weight(gathered, name, ncols):
    if name in _ROW_NAMES:
        return gathered.reshape(-1, D)
    return gathered[:, :, :ncols].transpose(1, 0, 2).reshape(D, NDEV * ncols)


def _regroup_w_in0(w):
    main = jnp.concatenate([w[:, 0:1536], w[:, 1544:3080], w[:, 3088:3600]], axis=1)
    small = jnp.concatenate([w[:, 1536:1544], w[:, 3080:3088]], axis=1)
    return jnp.concatenate([main, small, jnp.zeros((D, ZW0 - 3584 - 16), w.dtype)], axis=1)


def _ungroup_w_in0(g):
    return jnp.concatenate([g[:, 0:1536], g[:, 3584:3592], g[:, 1536:3072], g[:, 3592:3600], g[:, 3072:3584]], axis=1)


def _col_slabs(g, ncols):
    g = g.reshape(D, NDEV, ncols).transpose(1, 0, 2)
    return jnp.pad(g, ((0, 0), (0, 0), (0, _COLW - ncols)))


def _sq(t):
    return t * t


def _epi_res_norm(acc, res, gain):
    y = acc + res
    h = _f_norm(y, gain)[0]
    return y, h, h


_RES_NORM_OUTS = [("tile", f32), ("tile", bf16), ("tile_t", bf16)]


def _epi_norm_bwd(acc, x, dres, gain):
    _, vjp = jax.vjp(lambda xx, gg: _f_norm(xx, gg)[0], x, gain)
    dx, dgain = vjp(acc)
    dx = dx + dres
    return dx, dgain, dx


_NORM_BWD_OUTS = [("tile", f32), ("rows", f32), ("tile_t", bf16)]


def _epi_loss(acc, res, tgt):
    e = acc + res - tgt
    dy = e * (1.0 / D)
    return dy, dy, jnp.sum(e * e, axis=0, keepdims=True)


_LOSS_OUTS = [("tile", f32), ("tile_t", bf16), ("rows", f32)]


def _mlp_fwd(tag, x, h, w1, w2, epi, extra, consts, outs):
    a = _mm(f"{tag}_ff1", h, w1, NN, bf16, epi=lambda acc: jnp.maximum(acc, 0.0))
    return _mm(f"{tag}_ff2", a, w2, NN, f32, a_fn=_sq, epi=epi, extra=(x,) + tuple(extra), consts=consts, outs=outs), a


def _mlp_bwd(tag, x, gain, w1, w2, ht, a, dy, dyt, side=None):
    da = _mm(f"{tag}_ff2_dx", dy, w2, NT, bf16, epi=lambda acc, av: acc * 2.0 * av.astype(f32), extra=(a,), side=side)
    if side is not None:
        da, side_res = da
    dw2 = _mm(f"{tag}_ff2_dw", dyt, a, NN, bf16, b_fn=_sq, out_t=True)
    res = _mm(f"{tag}_ff1_dx", da, w1, NT, f32, epi=_epi_norm_bwd, extra=(x, dy), consts=(gain,), outs=_NORM_BWD_OUTS)
    dw1 = _mm(f"{tag}_ff1_dw", ht, da, NN, bf16, slab=True)
    return (res, dw1, dw2) if side is None else ((res, dw1, dw2), side_res)


def _row(v):
    return v.reshape(1, -1).astype(f32)


_L0_REST = ("l0_w_ff1", "l0_w_out", "l0_w_ff2")
_L1_MIX = ("l1_w_in", "l1_w_out")
_L1_FFN = ("l1_w_ff1", "l1_w_ff2")
_GRAD_A = ("l1_w_ff1", "l1_w_ff2", "l1_w_out", "l1_w_in")
_GRAD_B = ("l0_w_ff1", "l0_w_ff2", "l0_w_out")


def _train(x, tgt, args, mom, var, t):
    n = x.shape[0]
    nchunk = t // CH
    mx, my, mc = _me()
    dev, chip = 4 * mx + 2 * my + mc, 2 * mx + my
    core_idx = jnp.reshape(mc, (1,)).astype(jnp.int32)
    own_idx = jnp.stack([dev, chip]).astype(jnp.int32)
    ncols = {nm: args[nm].shape[1] for nm in _COL_NAMES}
    conv_cols = args["l0_gdn_conv"].shape[1]
    g, big, w = {}, {}, {}

    def send(nm):
        a = args[nm].astype(bf16)
        return jnp.pad(a, ((0, 0), (0, _COLW - ncols[nm]))) if nm in _COL_NAMES else a

    def take(names, gathered):
        for nm, arr in zip(names, gathered):
            w[nm] = _full_weight(arr, nm, ncols.get(nm, 0))

    def by_dev(nm, ga):
        return ga if nm in _COL_NAMES else ga.reshape(NDEV, -1, D)

    def pair(names, gs, r1s):
        return [_pair_sum(f"rs_pair_sum_{nm}", core_idx, ga, r1) for nm, ga, r1 in zip(names, gs, r1s)]

    def adam(names, gs, r1s, r2s):
        for nm, ga, r1, r2 in zip(names, gs, r1s, r2s):
            big[nm] = _adamw_big(f"adamw_{nm}", own_idx, ga, r1, r2, args[nm], mom[nm], var[nm])


    li = jnp.arange(FOX_W)
    pm = jnp.where((li[:, None] // FOX_D) == (li[None, :] // FOX_D), 1.0 / FOX_D, 0.0).astype(f32)
    lane_head = jnp.arange(GDN_W) // HD
    sel = lambda first_lane: (jnp.arange(LANES)[:, None] == (first_lane + lane_head)[None, :]).astype(f32)
    e_beta, e_alpha = sel(FOX_H), sel(FOX_H + GDN_H)
    alog_b, dt_b = _row(jnp.repeat(args["l0_gdn_A_log"], HD)), _row(jnp.repeat(args["l0_gdn_dt_bias"], HD))
    gq_t, gk_t = _row(jnp.tile(args["l0_fox_q_norm"], FOX_H)), _row(jnp.tile(args["l0_fox_k_norm"], FOX_H))
    on0_t, on1_t = _row(jnp.tile(args["l0_gdn_o_norm"], GDN_H)), _row(jnp.tile(args["l1_hgrn_o_norm"], HG_H))
    fbias = jnp.pad(_row(args["l0_fox_f_bias"]), ((0, 0), (0, LANES - FOX_H)))
    g0m, g0f, g1m, g1f = (_row(args[k]) for k in ("l0_mix_norm", "l0_ffn_norm", "l1_mix_norm", "l1_ffn_norm"))
    lbl = args["hgrn_lb_logits"].astype(f32)

    first = [send("l0_w_in"), jnp.pad(args["l0_gdn_conv"], ((0, 4), (0, LANES * 2 - conv_cols)))]
    (h0, h0t), first = _tok_fwd("l0_mix_norm", _f_norm, [(x, D, 0)], [(g0m, None, 0)], [(D, bf16)], 256, also_t=(0,),
                                side=_side_allgather(first))
    take(("l0_w_in",), first[:1])
    w_in0 = _regroup_w_in0(w["l0_w_in"])
    wconv = first[1][:, :4, :conv_cols].transpose(1, 0, 2).reshape(4, NDEV * conv_cols)
    z0 = _mm("l0_in", h0, w_in0, NN, f32)
    qk_rows = [(z0, 2 * FOX_W, 0)]
    qk_consts = [(gq_t, None, 0), (gk_t, None, 0), (pm, None, 0)]
    qn, kn = _tok_fwd("fox_pre", _f_foxpre, qk_rows, qk_consts, [(FOX_W, f32)] * 2, 256)
    ccol, crow = _fox_gate_fwd(z0, fbias, t)
    fox_o, got = _fox_attn_fwd(qn, kn, z0, ccol, crow, t, side=_side_allgather([send(nm) for nm in _L0_REST]))
    take(_L0_REST, got)
    conv_rows, conv_consts = [(z0, LANES, Z0_GQKV)], [(wconv, LANES, 0)]
    (qkv,) = _tok_fwd("gdn_conv", _f_conv, conv_rows, conv_consts, [(LANES, f32)], t, ncb=12, with_j=True)
    gate_rows = [(z0, LANES, Z0_SMALL)]
    gate_consts = [(e_beta, None, 0), (e_alpha, None, 0), (alog_b, None, 0), (dt_b, None, 0)]
    beta_b, g_b = _tok_fwd("gdn_gate", _f_gdngate, gate_rows, gate_consts, [(GDN_W, f32)] * 2, 256)
    intra_rows = [(qkv, GDN_W, 0), (qkv, GDN_W, 1), (qkv, GDN_W, 2), (beta_b, GDN_W, 0), (g_b, GDN_W, 0)]
    intra, got = _tok_fwd("gdn_intra", _f_gdn_intra, intra_rows, [], [(GDN_W, f32)] * 7, 2 * CH,
                          side=_side_allgather([send(nm) for nm in _L1_MIX]))
    take(_L1_MIX, got)
    inter_rows = [(a, 0, 1) for a in intra[:6]]
    (gdn_o, gdn_hist), got = _scan_fwd("gdn_scan", _f_gdn_inter, inter_rows, GDN_H, nchunk,
                                       side=_side_allgather([send(nm) for nm in _L1_FFN[1:]]))
    take(_L1_FFN[1:], got)
    post0_rows, post0_consts = [(fox_o, FOX_W, 0), (gdn_o, GDN_W, 0), (z0, GDN_W, Z0_GG // 4)], [(on0_t, None, 0)]
    cat0, cat0t = _tok_fwd("l0_post", _f_post0, post0_rows, post0_consts, [(D, bf16)], 256, also_t=(0,))
    x1, hf0, hf0t = _mm("l0_out", cat0, w["l0_w_out"], NN, f32, epi=_epi_res_norm, extra=(x,), consts=(g0f,), outs=_RES_NORM_OUTS)
    (x2, h1, h1t), a0 = _mlp_fwd("l0", x1, hf0, w["l0_w_ff1"], w["l0_w_ff2"], _epi_res_norm, (), (g1m,), _RES_NORM_OUTS)

    z1 = _mm("l1_in", h1, w["l1_w_in"], NN, f32)
    hg_rows = [(z1, 0, 2), (z1, 2, 1)]
    (hg_o, hg_hist), got = _scan_fwd("hgrn_scan", _f_hgrn_gated, hg_rows, HG_H, nchunk, consts=(lbl,),
                                     side=_side_allgather([send(nm) for nm in _L1_FFN[:1]]))
    take(_L1_FFN[:1], got)
    post1_rows, post1_consts = [(hg_o, D, 0), (z1, D, 3)], [(on1_t, None, 0)]
    cat1, cat1t = _tok_fwd("l1_post", _f_post1, post1_rows, post1_consts, [(D, bf16)], 256, also_t=(0,))
    x3, hf1, hf1t = _mm("l1_out", cat1, w["l1_w_out"], NN, f32, epi=_epi_res_norm, extra=(x2,), consts=(g1f,), outs=_RES_NORM_OUTS)
    (dy, dyt, loss_row), a1 = _mlp_fwd("l1", x3, hf1, w["l1_w_ff1"], w["l1_w_ff2"], _epi_loss, (tgt,), (), _LOSS_OUTS)

    (dx3, g["l1_ffn_norm"], _), ga_ff1, ga_ff2 = _mlp_bwd("l1", x3, g1f, w["l1_w_ff1"], w["l1_w_ff2"], hf1t, a1, dy, dyt)
    dcat1 = _mm("l1_out_dx", dx3, w["l1_w_out"], NT, f32)
    ga_out = _mm("l1_out_dw", cat1t, dx3, NN, bf16)
    dhg_o, dzg, don1 = _tok_bwd("l1_post_bwd", _f_post1, post1_rows, post1_consts, [(dcat1, D)], 256, [0, 1], [0],
                                drow_dtype=[f32, bf16])
    dzqf, dzi, dlbl = _scan_bwd("hgrn_scan_bwd", _f_hgrn_gated, hg_rows, hg_hist, dhg_o, HG_H, nchunk, dtypes=[bf16, bf16],
                                consts=(lbl,))
    dz1 = jnp.concatenate([dzqf, dzi, dzg], axis=1)
    dx2, g["l1_mix_norm"], dx2t = _mm("l1_in_dx", dz1, w["l1_w_in"], NT, f32, epi=_epi_norm_bwd, extra=(x2, dx3), consts=(g1m,),
                                      outs=_NORM_BWD_OUTS)
    ga_in = _mm("l1_in_dw", h1t, dz1, NN, bf16, slab=True)
    g["l1_hgrn_o_norm"] = don1.reshape(HG_H, HD).sum(0)
    g["hgrn_lb_logits"] = dlbl
    gs_a = [by_dev(nm, ga) for nm, ga in zip(_GRAD_A, (ga_ff1, ga_ff2, ga_out, ga_in))]

    ((dx1, g["l0_ffn_norm"], _), gb_ff1, gb_ff2), r1_a = _mlp_bwd("l0", x1, g0f, w["l0_w_ff1"], w["l0_w_ff2"], hf0t, a0, dx2, dx2t,
                                                                  side=_side_rs_sibling(gs_a))
    pairs_a = pair(_GRAD_A, gs_a, r1_a)
    dcat0 = _mm("l0_out_dx", dx1, w["l0_w_out"], NT, f32)
    gb_out = _mm("l0_out_dw", cat0t, dx1, NN, bf16)
    gs_b = [by_dev(nm, ga) for nm, ga in zip(_GRAD_B, (gb_ff1, gb_ff2, gb_out))]
    dfox_o, dgdn_o, dgg, don0 = _tok_bwd("l0_post_bwd", _f_post0, post0_rows, post0_consts, [(dcat0, D)], 256, [0, 1, 2], [0],
                                         drow_dtype=[f32, f32, bf16])
    (dqn, dkn, dfv, dcq, dck), got = _fox_attn_bwd(qn, kn, z0, ccol, crow, dfox_o, t,
                                                   side=_join_sides([_side_rs_chips(pairs_a), _side_rs_sibling(gs_b)]))
    r2_a, r1_b = got[:len(_GRAD_A)], got[len(_GRAD_A):]
    adam(_GRAD_A, gs_a, r1_a, r2_a)
    pairs_b = pair(_GRAD_B, gs_b, r1_b)
    dinter = _scan_bwd("gdn_scan_bwd", _f_gdn_inter, inter_rows, gdn_hist, dgdn_o, GDN_H, nchunk)
    (dq_g, dk_g, dv_g, dbeta_b, dg_b), r2_b = _tok_bwd("gdn_intra_bwd", _f_gdn_intra, intra_rows + [(intra[6], GDN_W, 0)], [],
                                                       [(a, GDN_W) for a in dinter], 2 * CH, [0, 1, 2, 3, 4], [],
                                                       side=_side_rs_chips(pairs_b))
    adam(_GRAD_B, gs_b, r1_b, r2_b)
    dqkv = jnp.concatenate([dq_g, dk_g, dv_g], axis=1)
    dzs_g, dalog_b, ddt_b = _tok_bwd("gdn_gate_bwd", _f_gdngate, gate_rows, gate_consts, [(dbeta_b, GDN_W), (dg_b, GDN_W)], 256, [0], [2, 3])
    dgqkv, dwconv = _tok_bwd("gdn_conv_bwd", _f_conv, conv_rows, conv_consts, [(dqkv, LANES)], t, [0], [0], ncb=12, with_j=True,
                             drow_dtype=bf16)
    dzs_f, dfb = _fox_gate_bwd(z0, fbias, dcq, dck, t)
    dzqk, dgq_t, dgk_t = _tok_bwd("fox_pre_bwd", _f_foxpre, qk_rows, qk_consts, [(dqn, FOX_W), (dkn, FOX_W)], 256, [0], [0, 1],
                                  drow_dtype=bf16)
    dz0 = jnp.concatenate([dzqk, dfv.astype(bf16), dgqkv, dgg, (dzs_g + dzs_f).astype(bf16), jnp.zeros((n, ZW0 - 3712), bf16)], axis=1)
    gs_c = [_col_slabs(_ungroup_w_in0(_mm("l0_in_dw", h0t, dz0, NN, bf16)), ncols["l0_w_in"])]
    r1_c = _rs_sibling(gs_c)
    pairs_c = pair(("l0_w_in",), gs_c, r1_c)
    (dx, g["l0_mix_norm"], _), r2_c = _mm("l0_in_dx", dz0, w_in0, NT, f32, epi=_epi_norm_bwd, extra=(x, dx1), consts=(g0m,),
                                          outs=_NORM_BWD_OUTS, side=_side_rs_chips(pairs_c))
    adam(("l0_w_in",), gs_c, r1_c, r2_c)
    g["l0_fox_q_norm"] = dgq_t.reshape(FOX_H, FOX_D).sum(0)
    g["l0_fox_k_norm"] = dgk_t.reshape(FOX_H, FOX_D).sum(0)
    g["l0_fox_f_bias"] = dfb[0, :FOX_H]
    g["l0_gdn_conv"] = dwconv
    g["l0_gdn_A_log"] = dalog_b.reshape(GDN_H, HD).sum(1)
    g["l0_gdn_dt_bias"] = ddt_b.reshape(GDN_H, HD).sum(1)
    g["l0_gdn_o_norm"] = don0.reshape(GDN_H, HD).sum(0)
    return loss_row, dx, g, big


_NAMES = ("l0_mix_norm", "l0_w_in", "l0_fox_q_norm", "l0_fox_k_norm", "l0_fox_f_bias", "l0_gdn_conv", "l0_gdn_A_log",
          "l0_gdn_dt_bias", "l0_gdn_o_norm", "l0_w_out", "l0_ffn_norm", "l0_w_ff1", "l0_w_ff2", "l1_mix_norm", "l1_w_in",
          "l1_hgrn_o_norm", "l1_w_out", "l1_ffn_norm", "l1_w_ff1", "l1_w_ff2", "hgrn_lb_logits")
_SMALL_NAMES = tuple(nm for nm in _NAMES if nm not in _BIG_NAMES)
_SMALL_ROWS = 16


def _pack_small(vals):
    flat = jnp.concatenate([vals[nm].reshape(-1).astype(f32) for nm in _SMALL_NAMES])
    return jnp.pad(flat, (0, _SMALL_ROWS * D - flat.shape[0])).reshape(_SMALL_ROWS, D)


def _unpack_small(packed, shapes):
    flat = packed.reshape(-1)
    out, off = {}, 0
    for nm in _SMALL_NAMES:
        size = 1
        for s in shapes[nm]:
            size *= s
        out[nm] = flat[off:off + size].reshape(shapes[nm])
        off += size
    return out, off


def kernel(x, l0_mix_norm, l0_w_in, l0_fox_q_norm, l0_fox_k_norm, l0_fox_f_bias, l0_gdn_conv, l0_gdn_A_log, l0_gdn_dt_bias, l0_gdn_o_norm, l0_w_out, l0_ffn_norm, l0_w_ff1, l0_w_ff2, l1_mix_norm, l1_w_in, l1_hgrn_o_norm, l1_w_out, l1_ffn_norm, l1_w_ff1, l1_w_ff2, hgrn_lb_logits, loss_target, m_l0_mix_norm, m_l0_w_in, m_l0_fox_q_norm, m_l0_fox_k_norm, m_l0_fox_f_bias, m_l0_gdn_conv, m_l0_gdn_A_log, m_l0_gdn_dt_bias, m_l0_gdn_o_norm, m_l0_w_out, m_l0_ffn_norm, m_l0_w_ff1, m_l0_w_ff2, m_l1_mix_norm, m_l1_w_in, m_l1_hgrn_o_norm, m_l1_w_out, m_l1_ffn_norm, m_l1_w_ff1, m_l1_w_ff2, m_hgrn_lb_logits, v_l0_mix_norm, v_l0_w_in, v_l0_fox_q_norm, v_l0_fox_k_norm, v_l0_fox_f_bias, v_l0_gdn_conv, v_l0_gdn_A_log, v_l0_gdn_dt_bias, v_l0_gdn_o_norm, v_l0_w_out, v_l0_ffn_norm, v_l0_w_ff1, v_l0_w_ff2, v_l1_mix_norm, v_l1_w_in, v_l1_hgrn_o_norm, v_l1_w_out, v_l1_ffn_norm, v_l1_w_ff1, v_l1_w_ff2, v_hgrn_lb_logits):
    args = dict(zip(_NAMES, (l0_mix_norm, l0_w_in, l0_fox_q_norm, l0_fox_k_norm, l0_fox_f_bias, l0_gdn_conv, l0_gdn_A_log, l0_gdn_dt_bias, l0_gdn_o_norm, l0_w_out, l0_ffn_norm, l0_w_ff1, l0_w_ff2, l1_mix_norm, l1_w_in, l1_hgrn_o_norm, l1_w_out, l1_ffn_norm, l1_w_ff1, l1_w_ff2, hgrn_lb_logits)))
    mom = dict(zip(_NAMES, (m_l0_mix_norm, m_l0_w_in, m_l0_fox_q_norm, m_l0_fox_k_norm, m_l0_fox_f_bias, m_l0_gdn_conv, m_l0_gdn_A_log, m_l0_gdn_dt_bias, m_l0_gdn_o_norm, m_l0_w_out, m_l0_ffn_norm, m_l0_w_ff1, m_l0_w_ff2, m_l1_mix_norm, m_l1_w_in, m_l1_hgrn_o_norm, m_l1_w_out, m_l1_ffn_norm, m_l1_w_ff1, m_l1_w_ff2, m_hgrn_lb_logits)))
    var = dict(zip(_NAMES, (v_l0_mix_norm, v_l0_w_in, v_l0_fox_q_norm, v_l0_fox_k_norm, v_l0_fox_f_bias, v_l0_gdn_conv, v_l0_gdn_A_log, v_l0_gdn_dt_bias, v_l0_gdn_o_norm, v_l0_w_out, v_l0_ffn_norm, v_l0_w_ff1, v_l0_w_ff2, v_l1_mix_norm, v_l1_w_in, v_l1_hgrn_o_norm, v_l1_w_out, v_l1_ffn_norm, v_l1_w_ff1, v_l1_w_ff2, v_hgrn_lb_logits)))
    nb, t, _ = x.shape
    dev = 4 * lax.axis_index("x") + 2 * lax.axis_index("y") + lax.axis_index("c")
    conv_cols = l0_gdn_conv.shape[1]
    loss_row, dx, g, big = _train(x.reshape(nb * t, D), loss_target.reshape(nb * t, D), args, mom, var, t)

    shapes = {nm: args[nm].shape for nm in _SMALL_NAMES}
    gsm = dict(g)
    gsm["l0_gdn_conv"] = jnp.zeros(shapes["l0_gdn_conv"], f32)
    packed = _pack_small(gsm)
    _, used = _unpack_small(packed, shapes)
    flat_extra = jnp.concatenate([jnp.sum(loss_row).reshape(1), g["l0_gdn_conv"].reshape(-1)])
    packed = packed.reshape(-1).at[used:used + flat_extra.shape[0]].set(flat_extra).reshape(_SMALL_ROWS, D)
    (parts,) = _allgather("ag_small", [packed])
    total = _sum_parts(parts).reshape(-1)
    loss = 0.5 * total[used] / D
    conv_g_full = total[used + 1:used + 1 + 4 * NDEV * conv_cols].reshape(4, NDEV * conv_cols)
    conv_g = lax.dynamic_slice(conv_g_full, (0, dev * conv_cols), (4, conv_cols))
    own_vals = {nm: jnp.zeros(shapes[nm], f32) for nm in _SMALL_NAMES}
    own_vals["l0_gdn_conv"] = conv_g
    own_mask = {nm: jnp.zeros(shapes[nm], f32) for nm in _SMALL_NAMES}
    own_mask["l0_gdn_conv"] = jnp.ones(shapes["l0_gdn_conv"], f32)
    small = _adamw_small(parts, _pack_small(args), _pack_small(mom), _pack_small(var), _pack_small(own_mask), _pack_small(own_vals))
    small = [_unpack_small(a, shapes)[0] for a in small]
    small[0]["l0_gdn_conv"] = conv_g

    outs = [loss, dx.reshape(nb, t, D)]
    for k in range(4):
        outs += [big[nm][k] if nm in _BIG_NAMES else small[k][nm] for nm in _NAMES]
    return tuple(outs)
```

```python
import functools

import jax
import jax.numpy as jnp
from jax import lax
from jax.experimental import pallas as pl
from jax.experimental.pallas import tpu as pltpu

f32, bf16 = jnp.float32, jnp.bfloat16
NN = (((1,), (0,)), ((), ()))
NT = (((1,), (1,)), ((), ()))
TN = (((0,), (0,)), ((), ()))
HI = lax.Precision.HIGHEST
MESH = pl.DeviceIdType.MESH
S = jax.ShapeDtypeStruct

EPS = 1e-6
D = 1024
LANES = 128
FOX_H, FOX_D, FOX_W = 8, 64, 512
GDN_H, HD, GDN_W = 4, 128, 512
HG_H = 8
CH = 64
ZW0 = 3840
NDEV = 8
ADAM_LR, ADAM_B1, ADAM_B2, ADAM_EPS, ADAM_WD, ADAM_STEP = 0.001, 0.9, 0.999, 1e-08, 0.01, 10

Z0_FQK, Z0_FV, Z0_GQKV, Z0_GG, Z0_SMALL = 0, 8, 12, 24, 28


def _dot(a, b, dims=NN, prec=None):
    return lax.dot_general(a, b, dims, precision=prec, preferred_element_type=f32)


def _iota2(shape, axis):
    return lax.broadcasted_iota(jnp.int32, shape, axis)


def _split3(x):
    x1 = x.astype(bf16)
    r = x - x1.astype(f32)
    x2 = r.astype(bf16)
    return x1, x2, (r - x2.astype(f32)).astype(bf16)


def _dot_sel(a, b, dims=NN, exact_lhs=False):
    if exact_lhs:
        return sum(_dot(a.astype(bf16), piece, dims) for piece in _split3(b))
    return sum(_dot(piece, b.astype(bf16), dims) for piece in _split3(a))


@jax.custom_vjp
def _sel_rhs(a, b):
    return _dot_sel(a, b)


_sel_rhs.defvjp(lambda a, b: (_dot_sel(a, b), b), lambda b, g: (_dot_sel(g, b, NT), jnp.zeros_like(b)))


@jax.custom_vjp
def _sel_lhs(a, x):
    return _dot_sel(a, x, exact_lhs=True)


_sel_lhs.defvjp(lambda a, x: (_dot_sel(a, x, exact_lhs=True), a), lambda a, g: (jnp.zeros_like(a), _dot_sel(a, g, TN, exact_lhs=True)))


class _Side:
    def __init__(self, ins, out_shapes, scratch, start, finish):
        self.ins, self.out_shapes, self.scratch, self.start, self.finish = list(ins), list(out_shapes), list(scratch), start, finish


def _join_sides(sides):
    def split(refs, counts):
        out, off = [], 0
        for c in counts:
            out.append(refs[off:off + c])
            off += c
        return out

    ni, no, ns = ([len(getattr(sd, a)) for sd in sides] for a in ("ins", "out_shapes", "scratch"))

    def run(which):
        def go(ins, outs, sems):
            for sd, i, o, c in zip(sides, split(ins, ni), split(outs, no), split(sems, ns)):
                getattr(sd, which)(i, o, c)
        return go

    return _Side(sum((sd.ins for sd in sides), []), sum((sd.out_shapes for sd in sides), []),
                 sum((sd.scratch for sd in sides), []), run("start"), run("finish"))


def _pcall(side, body, *, name, grid, in_specs, out_specs, out_shape, scratch_shapes=(), compiler_params=None):
    if side is None:
        return pl.pallas_call(body, name=name, grid=grid, in_specs=in_specs, out_specs=out_specs, out_shape=out_shape,
                              scratch_shapes=scratch_shapes, compiler_params=compiler_params)
    single = not isinstance(out_shape, (list, tuple))
    ospecs, oshape = ([out_specs], [out_shape]) if single else (list(out_specs), list(out_shape))
    nin, nout, nscr = len(in_specs), len(ospecs), len(scratch_shapes)
    si, so = len(side.ins), len(side.out_shapes)

    def wrapped(*refs):
        o0 = nin + si
        c0 = o0 + nout + so
        sins, souts, ssems = refs[nin:o0], refs[o0 + nout:c0], refs[c0 + nscr:]
        ids = [pl.program_id(a) for a in range(len(grid))]
        first = functools.reduce(jnp.logical_and, [i == 0 for i in ids])
        last = functools.reduce(jnp.logical_and, [i == g - 1 for i, g in zip(ids, grid)])

        @pl.when(first)
        def _():
            side.start(sins, souts, ssems)

        body(*refs[:nin], *refs[o0:o0 + nout], *refs[c0:c0 + nscr])

        @pl.when(last)
        def _():
            side.finish(sins, souts, ssems)

    call = pl.pallas_call(
        wrapped, name=name, grid=grid, in_specs=list(in_specs) + _hbm_specs(si), out_specs=ospecs + _hbm_specs(so),
        out_shape=oshape + side.out_shapes, scratch_shapes=list(scratch_shapes) + side.scratch,
        compiler_params=pltpu.CompilerParams(dimension_semantics=("arbitrary",) * len(grid),
                                             vmem_limit_bytes=getattr(compiler_params, "vmem_limit_bytes", None)))

    def run(*args):
        res = call(*args, *side.ins)
        return (res[0] if single else list(res[:nout])), list(res[nout:])

    return run


def _tok_specs(rows, consts, tm):
    specs = []
    for (_, w, base) in rows:
        specs.append(pl.BlockSpec((tm, w), functools.partial(lambda j, i, b: (i, b + j), b=base)))
    for (arr, w, base) in consts:
        if w is None:
            specs.append(pl.BlockSpec(arr.shape, lambda j, i: (0, 0)))
        else:
            specs.append(pl.BlockSpec((arr.shape[0], w), functools.partial(lambda j, i, b: (0, b + j), b=base)))
    return specs


def _tok_fwd(name, f, rows, consts, outs, tm, ncb=1, with_j=False, also_t=(), side=None):
    n = rows[0][0].shape[0]
    nin = len(rows) + len(consts)
    nout = len(outs)

    def body(*refs):
        ins = [r[...] for r in refs[:nin]]
        vals = f(pl.program_id(0), *ins) if with_j else f(*ins)
        for r, v in zip(refs[nin:nin + nout], vals):
            r[...] = v.astype(r.dtype)
        for r, k in zip(refs[nin + nout:], also_t):
            r[...] = vals[k].T.astype(r.dtype)

    return _pcall(
        side, body, name=name, grid=(ncb, n // tm),
        in_specs=_tok_specs(rows, consts, tm),
        out_specs=[pl.BlockSpec((tm, w), lambda j, i: (i, j)) for (w, _) in outs]
        + [pl.BlockSpec((outs[k][0], tm), lambda j, i: (0, i)) for k in also_t],
        out_shape=[S((n, w * ncb), dt) for (w, dt) in outs] + [S((outs[k][0], n), bf16) for k in also_t],
        compiler_params=pltpu.CompilerParams(dimension_semantics=("parallel", "parallel")),
    )(*[r[0] for r in rows], *[c[0] for c in consts])


def _tok_bwd(name, f, rows, consts, cots, tm, drow, dconst, ncb=1, with_j=False, addto=None, also_t=(), drow_dtype=f32, side=None, ncat=0):
    n = rows[0][0].shape[0]
    nr, nc, nct = len(rows), len(consts), len(cots)
    addto = addto or {}
    add_keys = sorted(addto)
    nadd = len(add_keys)

    def body(*refs):
        ins = [r[...] for r in refs[:nr + nc]]
        cot = [r[...] for r in refs[nr + nc:nr + nc + nct]]
        adds = refs[nr + nc + nct:nr + nc + nct + nadd]
        outs = refs[nr + nc + nct + nadd:]
        pos = list(drow) + [nr + k for k in dconst]

        def g(*dargs):
            full = list(ins)
            for p, a in zip(pos, dargs):
                full[p] = a
            return tuple(f(pl.program_id(0), *full) if with_j else f(*full))

        vals, vjp = jax.vjp(g, *[ins[p] for p in pos])
        grads = vjp(tuple(c.astype(v.dtype) for c, v in zip(cot, vals)))
        off = 0
        for k in range(len(drow)):
            gk = grads[k]
            if k in addto:
                gk = gk + adds[add_keys.index(k)][...]
            if k < ncat:
                outs[0][:, off:off + gk.shape[1]] = gk.astype(outs[0].dtype)
                off += gk.shape[1]
            else:
                outs[k - skip][...] = gk.astype(outs[k - skip].dtype)
            if k in also_t:
                tref = outs[len(drow) - skip + len(dconst) + list(also_t).index(k)]
                tref[...] = gk.T.astype(tref.dtype)
        first = pl.program_id(1) == 0
        for k in range(len(dconst)):
            ref = outs[len(drow) - skip + k]

            @pl.when(first)
            def _():
                ref[...] = jnp.zeros_like(ref)

            ref[...] += grads[len(drow) + k]

    skip = max(ncat - 1, 0)
    in_specs = _tok_specs(rows, consts, tm)
    in_specs += [pl.BlockSpec((tm, w), lambda j, i: (i, j)) for (_, w) in cots]
    in_specs += [pl.BlockSpec((tm, rows[drow[k]][1]), lambda j, i: (i, j)) for k in add_keys]
    dts = drow_dtype if isinstance(drow_dtype, (list, tuple)) else [drow_dtype] * len(drow)
    widths = [rows[k][1] for k in drow]
    if ncat:
        widths, dts = [sum(widths[:ncat])] + widths[ncat:], [dts[0]] + list(dts[ncat:])
    out_specs = [pl.BlockSpec((tm, wd), lambda j, i: (i, j)) for wd in widths]
    out_shape = [S((n, wd * ncb), dt) for wd, dt in zip(widths, dts)]
    for k in dconst:
        arr, w, _ = consts[k]
        if w is None:
            out_specs.append(pl.BlockSpec(arr.shape, lambda j, i: (0, 0)))
            out_shape.append(S(arr.shape, f32))
        else:
            out_specs.append(pl.BlockSpec((arr.shape[0], w), lambda j, i: (0, j)))
            out_shape.append(S((arr.shape[0], w * ncb), f32))
    for k in also_t:
        out_specs.append(pl.BlockSpec((rows[drow[k]][1], tm), lambda j, i: (0, i)))
        out_shape.append(S((rows[drow[k]][1], n), bf16))
    return _pcall(
        side, body, name=name, grid=(ncb, n // tm), in_specs=in_specs, out_specs=out_specs, out_shape=out_shape,
        compiler_params=pltpu.CompilerParams(dimension_semantics=("parallel", "arbitrary")),
    )(*[r[0] for r in rows], *[c[0] for c in consts], *[c[0] for c in cots], *[addto[k] for k in add_keys])


def _scan_fwd(name, f, rows, nh, nchunk, side=None, consts=()):
    n = rows[0][0].shape[0]
    nb = n // (CH * nchunk)
    nin, nco = len(rows), len(consts)
    w = nh * HD

    def body(*refs):
        o_ref, hist_ref, st = refs[nin + nco], refs[nin + nco + 1], refs[nin + nco + 2]

        @pl.when(pl.program_id(0) == 0)
        def _():
            st[...] = jnp.zeros_like(st)

        s0 = st[...]
        hist_ref[0] = s0.astype(hist_ref.dtype)
        tiles = [r[...].reshape(nb * CH, r.shape[2]) for r in refs[:nin]]
        o, s1 = f(*tiles, *[r[...] for r in refs[nin:nin + nco]], s0)
        o_ref[...] = o.reshape(nb, CH, w)
        st[...] = s1

    seq3 = lambda a: a.reshape(nb, nchunk * CH, a.shape[1])
    res = _pcall(
        side, body, name=name, grid=(nchunk,),
        in_specs=[pl.BlockSpec((nb, CH, k * w), functools.partial(lambda c, base: (0, c, base), base=b)) for (_, b, k) in rows]
        + [pl.BlockSpec(c.shape, lambda c: (0, 0)) for c in consts],
        out_specs=[pl.BlockSpec((nb, CH, w), lambda c: (0, c, 0)), pl.BlockSpec((1, nb * w, HD), lambda c: (c, 0, 0))],
        out_shape=[S((nb, nchunk * CH, w), f32), S((nchunk, nb * w, HD), bf16)],
        scratch_shapes=[pltpu.VMEM((nb * w, HD), f32)],
        compiler_params=pltpu.CompilerParams(dimension_semantics=("arbitrary",)),
    )(*[seq3(r[0]) for r in rows], *consts)
    (o, hist), extra = (res, None) if side is None else res
    out = [o.reshape(n, w), hist]
    return out if side is None else (out, extra)


def _scan_bwd(name, f, rows, hist, do, nh, nchunk, side=None, dtypes=None, consts=()):
    n = rows[0][0].shape[0]
    nb = n // (CH * nchunk)
    nin, nco = len(rows), len(consts)
    w = nh * HD

    def body(*refs):
        hist_ref, do_ref = refs[nin + nco], refs[nin + nco + 1]
        outs = refs[nin + nco + 2:nin + nco + 2 + nin]
        couts = refs[nin + nco + 2 + nin:nin + nco + 2 + nin + nco]
        ds = refs[nin + nco + 2 + nin + nco]

        @pl.when(pl.program_id(0) == 0)
        def _():
            ds[...] = jnp.zeros_like(ds)
            for c in couts:
                c[...] = jnp.zeros_like(c)

        tiles = [r[...].reshape(nb * CH, r.shape[2]) for r in refs[:nin]]
        _, vjp = jax.vjp(f, *tiles, *[r[...] for r in refs[nin:nin + nco]], hist_ref[0].astype(f32))
        grads = vjp((do_ref[...].reshape(nb * CH, w), ds[...]))
        for r, gk in zip(outs, grads[:nin]):
            r[...] = gk.reshape(r.shape).astype(r.dtype)
        for c, gk in zip(couts, grads[nin:nin + nco]):
            c[...] += gk
        ds[...] = grads[nin + nco]

    seq3 = lambda a: a.reshape(nb, nchunk * CH, a.shape[1])
    rev = lambda c, base: (0, nchunk - 1 - c, base)
    res = _pcall(
        side, body, name=name, grid=(nchunk,),
        in_specs=[pl.BlockSpec((nb, CH, k * w), functools.partial(rev, base=b)) for (_, b, k) in rows]
        + [pl.BlockSpec(c.shape, lambda c: (0, 0)) for c in consts]
        + [pl.BlockSpec((1, nb * w, HD), lambda c: (nchunk - 1 - c, 0, 0)), pl.BlockSpec((nb, CH, w), functools.partial(rev, base=0))],
        out_specs=[pl.BlockSpec((nb, CH, k * w), functools.partial(rev, base=0)) for (_, _, k) in rows]
        + [pl.BlockSpec(c.shape, lambda c: (0, 0)) for c in consts],
        out_shape=[S((nb, nchunk * CH, k * w), dt) for (_, _, k), dt in zip(rows, dtypes or [f32] * nin)]
        + [S(c.shape, f32) for c in consts],
        scratch_shapes=[pltpu.VMEM((nb * w, HD), f32)],
        compiler_params=pltpu.CompilerParams(dimension_semantics=("arbitrary",)),
    )(*[seq3(r[0]) for r in rows], *consts, hist, seq3(do))
    outs, extra = (res, None) if side is None else res
    outs = [o.reshape(n, o.shape[2]) for o in outs[:nin]] + list(outs[nin:])
    return outs if side is None else (outs, extra)


_VMEM_LIMIT = 56 * 2 ** 20
_VMEM_TILE_BUDGET = 40 * 2 ** 20


def _mm_tiles(m, n, k, sa, sb, so, sx, a_f32, b_f32, tn_fixed):
    best = None
    for tm in (1024, 512, 256, 128, 64):
        for tn in ((tn_fixed,) if tn_fixed else (1024, 768, 512, 384, 256, 128)):
            if m % tm or n % tn:
                continue
            need = 2 * (tm * k * sa + k * tn * sb + tm * tn * (so + sx)) + tm * tn * 4
            need += tm * k * (2 if sa == 4 else 0) + k * tn * (2 if sb == 4 else 0)
            need += tm * k * (4 if a_f32 else 0) + k * tn * (4 if b_f32 else 0)
            if need <= _VMEM_TILE_BUDGET and (best is None or (tm * tn, tm) > best[0]):
                best = ((tm * tn, tm), tm, tn)
    return best[1], best[2]


def _mm(name, a, b, dims, out_dtype, a_fn=None, b_fn=None, epi=None, extra=(), consts=(), outs=None, out_t=False, slab=False,
        side=None):
    m, kk = a.shape
    nn = b.shape[1] if dims is NN else b.shape[0]
    kinds = [("tile", out_dtype)] if outs is None else list(outs)
    so = sum(jnp.dtype(dt).itemsize for kd, dt in kinds if kd != "rows")
    sx = sum(e.dtype.itemsize for e in extra)
    full_rows = bool(consts) or any(kd == "rows" for kd, _ in kinds)
    tm, tn = _mm_tiles(m, nn, kk, a.dtype.itemsize, b.dtype.itemsize, so, sx,
                       a_fn is not None, b_fn is not None, _COLW if slab else (nn if full_rows else None))
    nex, nco = len(extra), len(consts)

    def body(a_ref, b_ref, *rest):
        av, bv = a_ref[...], b_ref[...]
        if a_fn is not None:
            av = a_fn(av.astype(f32))
        if b_fn is not None:
            bv = b_fn(bv.astype(f32))
        acc = _dot(av.astype(bf16), bv.astype(bf16), dims)
        if epi is not None:
            acc = epi(acc, *[r[...] for r in rest[:nex + nco]])
        vals = acc if isinstance(acc, tuple) else (acc,)
        for (kd, _), o_ref, val in zip(kinds, rest[nex + nco:], vals):
            if kd == "rows":
                @pl.when(pl.program_id(0) == 0)
                def _():
                    o_ref[...] = jnp.zeros_like(o_ref)

                o_ref[...] += val
            elif kd == "tile_t" or out_t:
                o_ref[...] = val.T.astype(o_ref.dtype)
            elif slab:
                o_ref[0] = val.astype(o_ref.dtype)
            else:
                o_ref[...] = val.astype(o_ref.dtype)

    bspec = pl.BlockSpec((kk, tn), lambda i, j: (0, j)) if dims is NN else pl.BlockSpec((tn, kk), lambda i, j: (j, 0))
    out_specs, out_shape = [], []
    for kd, dt in kinds:
        if kd == "rows":
            out_specs.append(pl.BlockSpec((1, nn), lambda i, j: (0, 0)))
            out_shape.append(S((1, nn), dt))
        elif kd == "tile_t" or out_t:
            out_specs.append(pl.BlockSpec((tn, tm), lambda i, j: (j, i)))
            out_shape.append(S((nn, m), dt))
        elif slab:
            out_specs.append(pl.BlockSpec((1, tm, tn), lambda i, j: (j, i, 0)))
            out_shape.append(S((nn // tn, m, tn), dt))
        else:
            out_specs.append(pl.BlockSpec((tm, tn), lambda i, j: (i, j)))
            out_shape.append(S((m, nn), dt))
    if outs is None:
        out_specs, out_shape = out_specs[0], out_shape[0]
    sem = ("arbitrary", "arbitrary") if any(kd == "rows" for kd, _ in kinds) else ("parallel", "parallel")
    return _pcall(
        side, body, name=name, grid=(m // tm, nn // tn),
        in_specs=[pl.BlockSpec((tm, kk), lambda i, j: (i, 0)), bspec]
        + [pl.BlockSpec((tm, tn), lambda i, j: (i, j)) for _ in extra]
        + [pl.BlockSpec(c.shape, lambda i, j: (0, 0)) for c in consts],
        out_specs=out_specs, out_shape=out_shape,
        compiler_params=pltpu.CompilerParams(dimension_semantics=sem, vmem_limit_bytes=_VMEM_LIMIT),
    )(a, b, *extra, *consts)


def _f_norm(x, g):
    return (x * lax.rsqrt(jnp.mean(x * x, axis=-1, keepdims=True) + EPS) * g,)


def _f_foxpre(zqk, gq, gk, pm):
    def nrm(t, g):
        return t * lax.rsqrt(_sel_rhs(t * t, pm) + EPS) * g
    return nrm(zqk[:, :FOX_W], gq), nrm(zqk[:, FOX_W:], gk)


def _chunk_cumsum(x):
    n = x.shape[0]
    r, c = _iota2((n, n), 0), _iota2((n, n), 1)
    tri = jnp.logical_and(r >= c, (r // CH) == (c // CH)).astype(f32)
    return _sel_lhs(tri, x)


def _f_gdngate(zs, eb, ea, alog_b, dt_b):
    beta = jax.nn.sigmoid(_sel_rhs(zs, eb))
    la = -jnp.exp(alog_b) * jax.nn.softplus(_sel_rhs(zs, ea) + dt_b)
    return beta, _chunk_cumsum(la)


def _f_conv(j, x, w):
    t = x.shape[0]
    y = x * w[3:4, :]
    for jj in range(3):
        sh = 3 - jj
        xs = jnp.concatenate([jnp.zeros((sh, x.shape[1]), f32), x[:t - sh, :]], axis=0)
        y = y + xs * w[jj:jj + 1, :]
    y = jax.nn.silu(y)
    yn = y * lax.rsqrt(jnp.sum(y * y, axis=-1, keepdims=True) + EPS)
    return (jnp.where(j < 2 * GDN_H, yn, y),)


def _head_rms(o, nh):
    outs = []
    for h in range(nh):
        oh = o[:, HD * h:HD * (h + 1)]
        outs.append(oh * lax.rsqrt(jnp.mean(oh * oh, axis=-1, keepdims=True) + EPS))
    return jnp.concatenate(outs, axis=1)


def _f_post0(fox_o, o, gg, on):
    return (jnp.concatenate([fox_o, _head_rms(o, GDN_H) * on * jax.nn.silu(gg)], axis=1),)


def _f_post1(o, zg, on):
    return (_head_rms(o, HG_H) * on * jax.nn.silu(zg),)


def _f_hpre(zqf, lbl):
    lb = jax.nn.sigmoid(lbl[1:2, :] - lbl[0:1, :])
    fg = lb + (1.0 - lb) * jax.nn.sigmoid(zqf[:, D:])
    return jax.nn.silu(zqf[:, :D]), 1.0 - fg, _chunk_cumsum(jnp.log(fg))


def _dotb(a, b, dims=NN):
    return _dot(a.astype(bf16), b.astype(bf16), dims)


def _dot3(a, b):
    ah, bh = a.astype(bf16), b.astype(bf16)
    al, bl = (a - ah.astype(f32)).astype(bf16), (b - bh.astype(f32)).astype(bf16)
    return _dot(ah, bh) + (_dot(ah, bl) + _dot(al, bh))


def _split(t, nh):
    return [t[CH * ck:CH * (ck + 1), HD * h:HD * (h + 1)] for ck in range(t.shape[0] // CH) for h in range(nh)]


def _merge(units, nh):
    return jnp.concatenate([jnp.concatenate(units[i:i + nh], axis=1) for i in range(0, len(units), nh)], axis=0)


def _inv_impl(amats):
    n = amats[0].shape[0]
    eye = jnp.where(_iota2((n, n), 0) == _iota2((n, n), 1), 1.0, 0.0).astype(f32)
    xs, ps = [eye - a for a in amats], list(amats)
    for _ in range(max(1, (n - 1).bit_length()) - 1):
        ps = [_dotb(p, p) for p in ps]
        xs = [x + _dotb(x, p) for x, p in zip(xs, ps)]
    for _ in range(3):
        rs = [eye - x - _dot3(a, x) for a, x in zip(amats, xs)]
        xs = [x + _dotb(x, r) for x, r in zip(xs, rs)]
    return tuple(xs)


@jax.custom_vjp
def _inv_unit_lower(amats):
    return _inv_impl(amats)


def _inv_fwd(amats):
    xs = _inv_impl(amats)
    return xs, xs


def _inv_bwd(xs, dxs):
    return (tuple(-_dotb(_dotb(x, dx, TN), x, NT) for x, dx in zip(xs, dxs)),)


_inv_unit_lower.defvjp(_inv_fwd, _inv_bwd)


@jax.custom_vjp
def _inv_given(amats, xs):
    return xs


def _inv_given_fwd(amats, xs):
    return xs, xs


def _inv_given_bwd(xs, dxs):
    return _inv_bwd(xs, dxs)[0], tuple(jnp.zeros_like(x) for x in xs)


_inv_given.defvjp(_inv_given_fwd, _inv_given_bwd)


def _f_gdn_intra(q, k, v, bb, gb, tinv_p=None):
    qs, ks, vs, bs, gs = (_split(t, GDN_H) for t in (q, k, v, bb, gb))
    r, cc = _iota2((CH, CH), 0), _iota2((CH, CH), 1)
    causal, strict = r >= cc, r > cc
    beta, g, gl = [b[:, :1] for b in bs], [x[:, :1] for x in gs], [x[CH - 1:CH, :1] for x in gs]
    decay = [jnp.exp(jnp.where(causal, x[:, :CH] - x[:, :CH].T, -jnp.inf)) for x in gs]
    kb = [ki * bi for ki, bi in zip(ks, beta)]
    amat = [jnp.where(strict, _dotb(kbi, ki, NT) * di, 0.0) for kbi, ki, di in zip(kb, ks, decay)]
    if tinv_p is None:
        tinv = _inv_unit_lower(tuple(amat))
    else:
        tinv = _inv_given(tuple(amat), tuple(x[:, :CH] for x in _split(tinv_p, GDN_H)))
    rhs = [jnp.concatenate([vi * bi, kbi * jnp.exp(gi)], axis=1) for vi, bi, kbi, gi in zip(vs, beta, kb, g)]
    uw = [_dotb(ti, ri) for ti, ri in zip(tinv, rhs)]
    qsc = [qi * (HD ** -0.5) for qi in qs]
    qk = [jnp.where(causal, _dotb(qi, ki, NT) * di, 0.0) for qi, ki, di in zip(qsc, ks, decay)]
    outs = ([x[:, :HD] for x in uw], [x[:, HD:] for x in uw],
            [jnp.concatenate([x, jnp.zeros_like(x)], axis=1) for x in qk],
            [qi * jnp.exp(gi) for qi, gi in zip(qsc, g)],
            [ki * jnp.exp(gli - gi) for ki, gli, gi in zip(ks, gl, g)],
            [jnp.broadcast_to(gli, (CH, HD)) for gli in gl])
    if tinv_p is None:
        outs += ([jnp.concatenate([x, jnp.zeros_like(x)], axis=1) for x in tinv],)
    return tuple(_merge(o, GDN_H) for o in outs)


def _f_gdn_inter(u, w, qkp, qd, kd, glb, st):
    us, ws, qks, qds, kds, gls = (_split(t, GDN_H) for t in (u, w, qkp, qd, kd, glb))
    sts = [st[HD * i:HD * (i + 1), :] for i in range(len(us))]
    vn = [ui - _dotb(wi, si) for ui, wi, si in zip(us, ws, sts)]
    o = [_dotb(qi, si) + _dotb(xi[:, :CH], vi) for qi, si, xi, vi in zip(qds, sts, qks, vn)]
    s2 = [si * jnp.exp(gi[:1, :1]) + _dotb(ki, vi, TN) for si, gi, ki, vi in zip(sts, gls, kds, vn)]
    return _merge(o, GDN_H), jnp.concatenate(s2, axis=0)


def _f_hgrn_chunk(q, k, b, v, st):
    qs, ks, bs, vs = (_split(t, HG_H) for t in (q, k, b, v))
    sts = [st[HD * i:HD * (i + 1), :] for i in range(len(qs))]
    causal = _iota2((CH, CH), 0) >= _iota2((CH, CH), 1)
    bl, bm = [x[CH - 1:CH, :] for x in bs], [x[CH // 2 - 1:CH // 2, :] for x in bs]
    a = [jnp.where(causal, _dotb(qi * jnp.exp(bi - mi), ki * jnp.exp(mi - bi), NT), 0.0)
         for qi, ki, bi, mi in zip(qs, ks, bs, bm)]
    o = [_dotb(qi * jnp.exp(bi), si, NT) + _dotb(ai, vi) for qi, bi, si, ai, vi in zip(qs, bs, sts, a, vs)]
    s2 = [si * jnp.exp(li) + _dotb(vi, ki * jnp.exp(li - bi), TN) for si, li, vi, ki, bi in zip(sts, bl, vs, ks, bs)]
    return _merge(o, HG_H), jnp.concatenate(s2, axis=0)


def _f_hgrn_gated(zqf, v, lbl, st):
    return _f_hgrn_chunk(*_f_hpre(zqf, lbl), v, st)


def _f_hgrn_full(z, lbl, on, st):
    o, s2 = _f_hgrn_chunk(*_f_hpre(z[:, :2 * D], lbl), z[:, 2 * D:3 * D], st)
    return _f_post1(o, z[:, 3 * D:], on)[0], s2


def _fox_gate_fwd(z0, fbias, t, tc=256):
    n = z0.shape[0]
    nt = t // tc

    def body(zs_ref, b_ref, ccol_ref, crow_ref, carry):
        @pl.when(pl.program_id(1) == 0)
        def _():
            carry[...] = jnp.zeros_like(carry)

        ls = jnp.where(_iota2((tc, LANES), 1) < FOX_H, jax.nn.log_sigmoid(zs_ref[...] + b_ref[...]), 0.0)
        tri = (_iota2((tc, tc), 0) >= _iota2((tc, tc), 1)).astype(f32)
        c = _dot_sel(tri, ls, exact_lhs=True) + carry[...]
        carry[...] = c[tc - 1:tc, :]
        ccol_ref[...] = c
        crow_ref[0] = c.T[:FOX_H, :]

    return pl.pallas_call(
        body, name="fox_gate_fwd", grid=(n // t, nt),
        in_specs=[pl.BlockSpec((tc, LANES), lambda b, i: (b * nt + i, Z0_SMALL)), pl.BlockSpec((1, LANES), lambda b, i: (0, 0))],
        out_specs=[pl.BlockSpec((tc, LANES), lambda b, i: (b * nt + i, 0)), pl.BlockSpec((1, FOX_H, tc), lambda b, i: (b, 0, i))],
        out_shape=[S((n, LANES), f32), S((n // t, FOX_H, t), f32)],
        scratch_shapes=[pltpu.VMEM((1, LANES), f32)],
        compiler_params=pltpu.CompilerParams(dimension_semantics=("parallel", "arbitrary")),
    )(z0, fbias)


def _fox_gate_bwd(z0, fbias, dcq, dck, t, tc=256):
    n = z0.shape[0]
    nt = t // tc

    def body(zs_ref, b_ref, dcq_ref, dck_ref, dz_ref, db_ref, carry):
        first = jnp.logical_and(pl.program_id(0) == 0, pl.program_id(1) == 0)

        @pl.when(pl.program_id(1) == 0)
        def _():
            carry[...] = jnp.zeros_like(carry)

        @pl.when(first)
        def _():
            db_ref[...] = jnp.zeros_like(db_ref)

        dc = dcq_ref[0] + dcq_ref[1] + dcq_ref[2] + dcq_ref[3]
        drow = dck_ref[0, 0] + dck_ref[1, 0] + dck_ref[2, 0] + dck_ref[3, 0]
        eye = (_iota2((FOX_H, LANES), 0) == _iota2((FOX_H, LANES), 1)).astype(f32)
        dc = dc + _dot_sel(drow, eye, TN)
        triu = (_iota2((tc, tc), 0) <= _iota2((tc, tc), 1)).astype(f32)
        dls = _dot_sel(triu, dc, exact_lhs=True) + carry[...]
        carry[...] = dls[0:1, :]
        x = zs_ref[...] + b_ref[...]
        dz = jnp.where(_iota2((tc, LANES), 1) < FOX_H, dls * jax.nn.sigmoid(-x), 0.0)
        dz_ref[...] = dz
        db_ref[...] += jnp.sum(dz, axis=0, keepdims=True)

    def rev(b, i):
        return b * nt + (nt - 1 - i)

    return pl.pallas_call(
        body, name="fox_gate_bwd", grid=(n // t, nt),
        in_specs=[pl.BlockSpec((tc, LANES), lambda b, i: (rev(b, i), Z0_SMALL)), pl.BlockSpec((1, LANES), lambda b, i: (0, 0)),
                  pl.BlockSpec((4, tc, LANES), lambda b, i: (0, rev(b, i), 0)),
                  pl.BlockSpec((4, 1, FOX_H, tc), lambda b, i: (0, b, 0, nt - 1 - i))],
        out_specs=[pl.BlockSpec((tc, LANES), lambda b, i: (rev(b, i), 0)), pl.BlockSpec((1, LANES), lambda b, i: (0, 0))],
        out_shape=[S((n, LANES), f32), S((1, LANES), f32)],
        scratch_shapes=[pltpu.VMEM((1, LANES), f32)],
        compiler_params=pltpu.CompilerParams(dimension_semantics=("arbitrary", "arbitrary")),
    )(z0, fbias, dcq, dck)


def _fox_scores(hh, p, i, tq, q, k, ccol, crow):
    kmax = k.shape[0]
    lane = _iota2((1, LANES), 1)
    mh = (lane // FOX_D) == hh
    h = 2 * p + hh
    qh = jnp.where(mh, q, 0.0).astype(bf16)
    s = _dot(qh, k, NT) * (FOX_D ** -0.5)
    cq = jnp.sum(jnp.where(lane == h, ccol, 0.0), axis=1, keepdims=True)
    ck = jnp.sum(jnp.where(_iota2((FOX_H, 1), 0) == h, crow, 0.0), axis=0, keepdims=True)
    causal = _iota2((1, kmax), 1) <= (i * tq + _iota2((tq, 1), 0))
    s = jnp.where(causal, s + cq - ck, -jnp.inf)
    pe = jnp.exp(s - jnp.max(s, axis=1, keepdims=True))
    return mh, qh, pe, jnp.sum(pe, axis=1, keepdims=True)


def _fox_attn_fwd(qn, kn, z0, ccol, crow, t, tq=256, side=None):
    n = qn.shape[0]
    nq = t // tq

    def body(q_ref, k_ref, v_ref, ccol_ref, crow_ref, o_ref):
        p = pl.program_id(1)
        k, v, crow = k_ref[...].astype(bf16), v_ref[...].astype(bf16), crow_ref[0]
        for i in range(nq):
            rows, kmax = pl.ds(i * tq, tq), (i + 1) * tq
            q, cc = q_ref[rows, :], ccol_ref[rows, :]
            acc = jnp.zeros((tq, LANES), f32)
            for hh in range(2):
                mh, _, pe, l = _fox_scores(hh, p, i, tq, q, k[:kmax], cc, crow[:, :kmax])
                acc = jnp.where(mh, _dot(pe.astype(bf16), v[:kmax]) / l, acc)
            o_ref[rows, :] = acc

    seq = lambda b, p: (b, p)
    return _pcall(
        side, body, name="fox_attn_fwd", grid=(n // t, FOX_H // 2),
        in_specs=[pl.BlockSpec((t, LANES), seq), pl.BlockSpec((t, LANES), seq), pl.BlockSpec((t, LANES), lambda b, p: (b, Z0_FV + p)),
                  pl.BlockSpec((t, LANES), lambda b, p: (b, 0)), pl.BlockSpec((1, FOX_H, t), lambda b, p: (b, 0, 0))],
        out_specs=pl.BlockSpec((t, LANES), seq),
        out_shape=S((n, FOX_W), f32),
        compiler_params=pltpu.CompilerParams(dimension_semantics=("parallel", "parallel")),
    )(qn, kn, z0, ccol, crow)


def _fox_attn_bwd(qn, kn, z0, ccol, crow, do, t, tq=256, side=None):
    n = qn.shape[0]
    nq = t // tq
    nb = n // t

    def body(q_ref, k_ref, v_ref, ccol_ref, crow_ref, do_ref, dq_ref, dk_ref, dv_ref, dcq_ref, dck_ref):
        p = pl.program_id(1)
        dk_ref[...] = jnp.zeros_like(dk_ref)
        dv_ref[...] = jnp.zeros_like(dv_ref)
        dck_ref[...] = jnp.zeros_like(dck_ref)
        kf, v, crow = k_ref[...], v_ref[...].astype(bf16), crow_ref[0]
        k = kf.astype(bf16)
        lane = _iota2((1, LANES), 1)
        sub = _iota2((FOX_H, 1), 0)
        scale = FOX_D ** -0.5
        for i in range(nq):
            rows, kmax = pl.ds(i * tq, tq), (i + 1) * tq
            q, cc, dout = q_ref[rows, :], ccol_ref[rows, :], do_ref[rows, :]
            dq = jnp.zeros((tq, LANES), f32)
            dcq = jnp.zeros((tq, LANES), f32)
            for hh in range(2):
                mh, qh, pe, l = _fox_scores(hh, p, i, tq, q, k[:kmax], cc, crow[:, :kmax])
                pr = pe / l
                doh = jnp.where(mh, dout, 0.0).astype(bf16)
                dp = _dot(doh, v[:kmax], NT)
                ds = pr * (dp - jnp.sum(pr * dp, axis=1, keepdims=True))
                dsb = ds.astype(bf16)
                dq = dq + _dot(dsb, jnp.where(mh, kf[:kmax], 0.0).astype(bf16)) * scale
                dk_ref[:kmax, :] += _dot(dsb, qh, TN) * scale
                dv_ref[:kmax, :] += _dot(pr.astype(bf16), doh, TN)
                h = 2 * p + hh
                dcq = dcq + jnp.where(lane == h, jnp.sum(ds, axis=1, keepdims=True), 0.0)
                dck_ref[0, 0, :, :kmax] += jnp.where(sub == h, -jnp.sum(ds, axis=0, keepdims=True), 0.0)
            dq_ref[rows, :] = dq
            dcq_ref[0, rows, :] = dcq

    seq = lambda b, p: (b, p)
    return _pcall(
        side, body, name="fox_attn_bwd", grid=(nb, FOX_H // 2),
        in_specs=[pl.BlockSpec((t, LANES), seq), pl.BlockSpec((t, LANES), seq), pl.BlockSpec((t, LANES), lambda b, p: (b, Z0_FV + p)),
                  pl.BlockSpec((t, LANES), lambda b, p: (b, 0)), pl.BlockSpec((1, FOX_H, t), lambda b, p: (b, 0, 0)),
                  pl.BlockSpec((t, LANES), seq)],
        out_specs=[pl.BlockSpec((t, LANES), seq), pl.BlockSpec((t, LANES), seq), pl.BlockSpec((t, LANES), seq),
                   pl.BlockSpec((1, t, LANES), lambda b, p: (p, b, 0)), pl.BlockSpec((1, 1, FOX_H, t), lambda b, p: (p, b, 0, 0))],
        out_shape=[S((n, FOX_W), f32), S((n, FOX_W), f32), S((n, FOX_W), f32), S((4, n, LANES), f32), S((4, nb, FOX_H, t), f32)],
        compiler_params=pltpu.CompilerParams(dimension_semantics=("parallel", "parallel")),
    )(qn, kn, z0, ccol, crow, do)


def _adamw_math(w, g, m, v):
    m = ADAM_B1 * m + (1.0 - ADAM_B1) * g
    v = ADAM_B2 * v + (1.0 - ADAM_B2) * (g * g)
    m_hat = m / (1.0 - ADAM_B1 ** ADAM_STEP)
    v_hat = v / (1.0 - ADAM_B2 ** ADAM_STEP)
    return -ADAM_LR * (m_hat / (jnp.sqrt(v_hat) + ADAM_EPS) + ADAM_WD * w), m, v


def _adamw_big(name, idx, gmine, recv1, recv2, w, m, v):
    r, wc = w.shape
    c = gmine.shape[2]
    tr = min(r, 256)

    def body(idx_ref, gm_ref, r1_ref, r2_ref, w_ref, m_ref, v_ref, g_ref, d_ref, nm_ref, nv_ref):
        g = gm_ref[0].astype(f32) + r1_ref[0].astype(f32)
        for k in range(3):
            g = g + r2_ref[k].astype(f32)
        g = g[:, :wc]
        d, nm, nv = _adamw_math(w_ref[...], g, m_ref[...], v_ref[...])
        g_ref[...] = g
        d_ref[...] = d
        nm_ref[...] = nm
        nv_ref[...] = nv

    row = pl.BlockSpec((tr, wc), lambda i, s: (i, 0))
    return pl.pallas_call(
        body, name=name,
        grid_spec=pltpu.PrefetchScalarGridSpec(
            num_scalar_prefetch=1, grid=(r // tr,),
            in_specs=[pl.BlockSpec((1, tr, c), lambda i, s: (s[0], i, 0)), pl.BlockSpec((1, tr, c), lambda i, s: (s[1], i, 0)),
                      pl.BlockSpec((3, tr, c), lambda i, s: (0, i, 0)), row, row, row],
            out_specs=[row, row, row, row]),
        out_shape=[S((r, wc), f32)] * 4,
        compiler_params=pltpu.CompilerParams(dimension_semantics=("parallel",)),
    )(idx, gmine, recv1, recv2, w, m, v)


def _pair_sum(name, idx, gmine, recv1):
    _, r, c = gmine.shape
    g4 = gmine.reshape(4, 2, r, c)

    def body(idx_ref, gm_ref, r1_ref, o_ref):
        o_ref[0] = (gm_ref[0, 0].astype(f32) + r1_ref[0].astype(f32)).astype(bf16)

    return pl.pallas_call(
        body, name=name,
        grid_spec=pltpu.PrefetchScalarGridSpec(
            num_scalar_prefetch=1, grid=(4,),
            in_specs=[pl.BlockSpec((1, 1, r, c), lambda ch, s: (ch, s[0], 0, 0)), pl.BlockSpec((1, r, c), lambda ch, s: (ch, 0, 0))],
            out_specs=pl.BlockSpec((1, r, c), lambda ch, s: (ch, 0, 0))),
        out_shape=S((4, r, c), bf16),
        compiler_params=pltpu.CompilerParams(dimension_semantics=("parallel",)),
    )(idx, g4, recv1)


def _adamw_small(parts, w, m, v, own_mask, own_g):
    def body(p_ref, w_ref, m_ref, v_ref, mask_ref, og_ref, g_ref, d_ref, nm_ref, nv_ref):
        g = p_ref[0]
        for k in range(1, NDEV):
            g = g + p_ref[k]
        g_ref[...] = g
        ge = jnp.where(mask_ref[...] > 0.5, og_ref[...], g)
        d, nm, nv = _adamw_math(w_ref[...], ge, m_ref[...], v_ref[...])
        d_ref[...] = d
        nm_ref[...] = nm
        nv_ref[...] = nv

    return pl.pallas_call(body, name="adamw_small", out_shape=[S(w.shape, f32)] * 4)(parts, w, m, v, own_mask, own_g)


def _sum_parts(parts):
    def body(p_ref, g_ref):
        g = p_ref[0]
        for k in range(1, NDEV):
            g = g + p_ref[k]
        g_ref[...] = g

    return pl.pallas_call(body, name="sum_parts", out_shape=S(parts.shape[1:], f32))(parts)


def _me():
    return lax.axis_index("x"), lax.axis_index("y"), lax.axis_index("c")


def _hbm_specs(n):
    return [pl.BlockSpec(memory_space=pl.ANY)] * n


def _allgather(name, xs):
    na = len(xs)

    def body(*refs):
        x_refs, out_refs = refs[:na], refs[na:2 * na]
        send_sems, recv_sems, local_sems = refs[2 * na:]
        mx, my, mc = _me()
        me, sib = (mx, my, mc), (mx, my, 1 - mc)
        chips = [(1 - mx, my), (mx, 1 - my), (1 - mx, 1 - my)]

        def slab(a, px, py, pc):
            return out_refs[a].at[4 * px + 2 * py + pc]

        def copy(a, k, block, to, own=False):
            return pltpu.make_async_remote_copy(
                src_ref=x_refs[a] if own else slab(a, *block), dst_ref=slab(a, *block),
                send_sem=send_sems.at[7 * a + k], recv_sem=recv_sems.at[7 * a + k], device_id=to, device_id_type=MESH)

        mine = [pltpu.make_async_copy(x_refs[a], slab(a, *me), local_sems.at[a]) for a in range(na)]
        first = []
        for a in range(na):
            mine[a].start()
            first += [copy(a, 0, me, sib, own=True)] + [copy(a, 1 + j, me, (*chip, mc), own=True) for j, chip in enumerate(chips)]
        for cp in first:
            cp.start()
        passed = []
        for j, chip in enumerate(chips):
            for a in range(na):
                copy(a, 1 + j, (*chip, mc), me).wait_recv()
                passed.append(copy(a, 4 + j, (*chip, mc), sib))
                passed[-1].start()
        for a in range(na):
            copy(a, 0, sib, me).wait_recv()
            for j, chip in enumerate(chips):
                copy(a, 4 + j, (*chip, 1 - mc), me).wait_recv()
        for cp in first + passed:
            cp.wait_send()
        for cp in mine:
            cp.wait()

    return pl.pallas_call(
        body, name=name, out_shape=[S((NDEV,) + x.shape, x.dtype) for x in xs],
        in_specs=_hbm_specs(na), out_specs=_hbm_specs(na),
        scratch_shapes=[pltpu.SemaphoreType.DMA((7 * na,)), pltpu.SemaphoreType.DMA((7 * na,)), pltpu.SemaphoreType.DMA((na,))],
    )(*xs)


def _rs_sibling(gs):
    na = len(gs)

    def body(*refs):
        g_refs, out_refs, send_sems, recv_sems = refs[:na], refs[na:2 * na], refs[2 * na], refs[2 * na + 1]
        mx, my, mc = _me()
        cps = [pltpu.make_async_remote_copy(
            src_ref=g_refs[a].at[2 * ch + 1 - mc], dst_ref=out_refs[a].at[ch], send_sem=send_sems.at[4 * a + ch],
            recv_sem=recv_sems.at[4 * a + ch], device_id=(mx, my, 1 - mc), device_id_type=MESH)
            for a in range(na) for ch in range(4)]
        for cp in cps:
            cp.start()
        for cp in cps:
            cp.wait_recv()
        for cp in cps:
            cp.wait_send()

    return pl.pallas_call(
        body, name="rs_sibling", out_shape=[S((4,) + g.shape[1:], g.dtype) for g in gs],
        in_specs=_hbm_specs(na), out_specs=_hbm_specs(na),
        scratch_shapes=[pltpu.SemaphoreType.DMA((4 * na,)), pltpu.SemaphoreType.DMA((4 * na,))],
    )(*gs)


def _side_allgather(xs):
    na = len(xs)

    def mk(x_refs, out_refs, sems):
        send_sems, recv_sems, local_sems = sems
        mx, my, mc = _me()
        me, sib = (mx, my, mc), (mx, my, 1 - mc)
        chips = [(1 - mx, my), (mx, 1 - my), (1 - mx, 1 - my)]

        def slab(a, px, py, pc):
            return out_refs[a].at[4 * px + 2 * py + pc]

        def copy(a, k, block, to, own=False):
            return pltpu.make_async_remote_copy(
                src_ref=x_refs[a] if own else slab(a, *block), dst_ref=slab(a, *block),
                send_sem=send_sems.at[7 * a + k], recv_sem=recv_sems.at[7 * a + k], device_id=to, device_id_type=MESH)

        mine = [pltpu.make_async_copy(x_refs[a], slab(a, *me), local_sems.at[a]) for a in range(na)]
        first = []
        for a in range(na):
            first += [copy(a, 0, me, sib, own=True)] + [copy(a, 1 + j, me, (*chip, mc), own=True) for j, chip in enumerate(chips)]
        return me, sib, chips, mc, copy, mine, first

    def start(x_refs, out_refs, sems):
        *_, mine, first = mk(x_refs, out_refs, sems)
        for cp in mine + first:
            cp.start()

    def finish(x_refs, out_refs, sems):
        me, sib, chips, mc, copy, mine, first = mk(x_refs, out_refs, sems)
        passed = []
        for j, chip in enumerate(chips):
            for a in range(na):
                copy(a, 1 + j, (*chip, mc), me).wait_recv()
                passed.append(copy(a, 4 + j, (*chip, mc), sib))
                passed[-1].start()
        for a in range(na):
            copy(a, 0, sib, me).wait_recv()
            for j, chip in enumerate(chips):
                copy(a, 4 + j, (*chip, 1 - mc), me).wait_recv()
        for cp in first + passed:
            cp.wait_send()
        for cp in mine:
            cp.wait()

    scratch = [pltpu.SemaphoreType.DMA((7 * na,)), pltpu.SemaphoreType.DMA((7 * na,)), pltpu.SemaphoreType.DMA((na,))]
    return _Side(xs, [S((NDEV,) + x.shape, x.dtype) for x in xs], scratch, start, finish)


def _side_exchange(arrs, nslot, out_slots, route):
    na = len(arrs)

    def copies(in_refs, out_refs, sems):
        send_sems, recv_sems = sems
        return [pltpu.make_async_remote_copy(
            src_ref=in_refs[a].at[src], dst_ref=out_refs[a].at[k], send_sem=send_sems.at[nslot * a + k],
            recv_sem=recv_sems.at[nslot * a + k], device_id=to, device_id_type=MESH)
            for a in range(na) for k, (src, to) in enumerate(route(*_me()))]

    def start(in_refs, out_refs, sems):
        for cp in copies(in_refs, out_refs, sems):
            cp.start()

    def finish(in_refs, out_refs, sems):
        cps = copies(in_refs, out_refs, sems)
        for cp in cps:
            cp.wait_recv()
        for cp in cps:
            cp.wait_send()

    scratch = [pltpu.SemaphoreType.DMA((nslot * na,)), pltpu.SemaphoreType.DMA((nslot * na,))]
    return _Side(arrs, [S((out_slots,) + x.shape[1:], x.dtype) for x in arrs], scratch, start, finish)


def _side_rs_sibling(gs):
    return _side_exchange(gs, 4, 4, lambda mx, my, mc: [(2 * ch + 1 - mc, (mx, my, 1 - mc)) for ch in range(4)])


def _side_rs_chips(ps):
    return _side_exchange(ps, 3, 3, lambda mx, my, mc: [(2 * cx + cy, (cx, cy, mc)) for cx, cy in
                                                        [(1 - mx, my), (mx, 1 - my), (1 - mx, 1 - my)]])


_COLW = 512
_COL_NAMES = ("l0_w_in", "l0_w_ff1", "l1_w_in", "l1_w_ff1")
_ROW_NAMES = ("l0_w_out", "l0_w_ff2", "l1_w_out", "l1_w_ff2")
_BIG_NAMES = _COL_NAMES + _ROW_NAMES


def _full_weight(gathered, name, ncols):
    if name in _ROW_NAMES:
        return gathered.reshape(-1, D)
    return gathered[:, :, :ncols].transpose(1, 0, 2).reshape(D, NDEV * ncols)


def _regroup_w_in0(w):
    main = jnp.concatenate([w[:, 0:1536], w[:, 1544:3080], w[:, 3088:3600]], axis=1)
    small = jnp.concatenate([w[:, 1536:1544], w[:, 3080:3088]], axis=1)
    return jnp.concatenate([main, small, jnp.zeros((D, ZW0 - 3584 - 16), w.dtype)], axis=1)


def _ungroup_w_in0(g):
    return jnp.concatenate([g[:, 0:1536], g[:, 3584:3592], g[:, 1536:3072], g[:, 3592:3600], g[:, 3072:3584]], axis=1)


def _col_slabs(g, ncols):
    g = g.reshape(D, NDEV, ncols).transpose(1, 0, 2)
    return jnp.pad(g, ((0, 0), (0, 0), (0, _COLW - ncols)))


def _sq(t):
    return t * t


def _epi_res_norm(acc, res, gain):
    y = acc + res
    h = _f_norm(y, gain)[0]
    return y, h, h


_RES_NORM_OUTS = [("tile", f32), ("tile", bf16), ("tile_t", bf16)]


def _epi_norm_bwd(acc, x, dres, gain):
    _, vjp = jax.vjp(lambda xx, gg: _f_norm(xx, gg)[0], x, gain)
    dx, dgain = vjp(acc)
    dx = dx + dres
    return dx, dgain, dx


_NORM_BWD_OUTS = [("tile", f32), ("rows", f32), ("tile_t", bf16)]


def _epi_loss(acc, res, tgt):
    e = acc + res - tgt
    dy = e * (1.0 / D)
    return dy, dy, jnp.sum(e * e, axis=0, keepdims=True)


_LOSS_OUTS = [("tile", f32), ("tile_t", bf16), ("rows", f32)]


def _mlp_fwd(tag, x, h, w1, w2, epi, extra, consts, outs):
    a = _mm(f"{tag}_ff1", h, w1, NN, bf16, epi=lambda acc: jnp.maximum(acc, 0.0))
    return _mm(f"{tag}_ff2", a, w2, NN, f32, a_fn=_sq, epi=epi, extra=(x,) + tuple(extra), consts=consts, outs=outs), a


def _mlp_bwd(tag, x, gain, w1, w2, ht, a, dy, dyt, side=None):
    da = _mm(f"{tag}_ff2_dx", dy, w2, NT, bf16, epi=lambda acc, av: acc * 2.0 * av.astype(f32), extra=(a,), side=side)
    if side is not None:
        da, side_res = da
    dw2 = _mm(f"{tag}_ff2_dw", dyt, a, NN, bf16, b_fn=_sq, out_t=True)
    res = _mm(f"{tag}_ff1_dx", da, w1, NT, f32, epi=_epi_norm_bwd, extra=(x, dy), consts=(gain,), outs=_NORM_BWD_OUTS)
    dw1 = _mm(f"{tag}_ff1_dw", ht, da, NN, bf16, slab=True)
    return (res, dw1, dw2) if side is None else ((res, dw1, dw2), side_res)


def _row(v):
    return v.reshape(1, -1).astype(f32)


_L0_REST = ("l0_w_ff1", "l0_w_out", "l0_w_ff2")
_L1_MIX = ("l1_w_in", "l1_w_out")
_L1_FFN = ("l1_w_ff1", "l1_w_ff2")
_GRAD_A = ("l1_w_ff1", "l1_w_ff2", "l1_w_out", "l1_w_in")
_GRAD_B = ("l0_w_ff1", "l0_w_ff2", "l0_w_out")


def _train(x, tgt, args, mom, var, t):
    n = x.shape[0]
    nchunk = t // CH
    mx, my, mc = _me()
    dev, chip = 4 * mx + 2 * my + mc, 2 * mx + my
    core_idx = jnp.reshape(mc, (1,)).astype(jnp.int32)
    own_idx = jnp.stack([dev, chip]).astype(jnp.int32)
    ncols = {nm: args[nm].shape[1] for nm in _COL_NAMES}
    conv_cols = args["l0_gdn_conv"].shape[1]
    g, big, w = {}, {}, {}

    def send(nm):
        a = args[nm].astype(bf16)
        return jnp.pad(a, ((0, 0), (0, _COLW - ncols[nm]))) if nm in _COL_NAMES else a

    def take(names, gathered):
        for nm, arr in zip(names, gathered):
            w[nm] = _full_weight(arr, nm, ncols.get(nm, 0))

    def by_dev(nm, ga):
        return ga if nm in _COL_NAMES else ga.reshape(NDEV, -1, D)

    def pair(names, gs, r1s):
        return [_pair_sum(f"rs_pair_sum_{nm}", core_idx, ga, r1) for nm, ga, r1 in zip(names, gs, r1s)]

    def adam(names, gs, r1s, r2s):
        for nm, ga, r1, r2 in zip(names, gs, r1s, r2s):
            big[nm] = _adamw_big(f"adamw_{nm}", own_idx, ga, r1, r2, args[nm], mom[nm], var[nm])


    li = jnp.arange(FOX_W)
    pm = jnp.where((li[:, None] // FOX_D) == (li[None, :] // FOX_D), 1.0 / FOX_D, 0.0).astype(f32)
    lane_head = jnp.arange(GDN_W) // HD
    sel = lambda first_lane: (jnp.arange(LANES)[:, None] == (first_lane + lane_head)[None, :]).astype(f32)
    e_beta, e_alpha = sel(FOX_H), sel(FOX_H + GDN_H)
    alog_b, dt_b = _row(jnp.repeat(args["l0_gdn_A_log"], HD)), _row(jnp.repeat(args["l0_gdn_dt_bias"], HD))
    gq_t, gk_t = _row(jnp.tile(args["l0_fox_q_norm"], FOX_H)), _row(jnp.tile(args["l0_fox_k_norm"], FOX_H))
    on0_t, on1_t = _row(jnp.tile(args["l0_gdn_o_norm"], GDN_H)), _row(jnp.tile(args["l1_hgrn_o_norm"], HG_H))
    fbias = jnp.pad(_row(args["l0_fox_f_bias"]), ((0, 0), (0, LANES - FOX_H)))
    g0m, g0f, g1m, g1f = (_row(args[k]) for k in ("l0_mix_norm", "l0_ffn_norm", "l1_mix_norm", "l1_ffn_norm"))
    lbl = args["hgrn_lb_logits"].astype(f32)

    first = [send("l0_w_in"), jnp.pad(args["l0_gdn_conv"], ((0, 4), (0, LANES * 2 - conv_cols)))]
    (h0, h0t), first = _tok_fwd("l0_mix_norm", _f_norm, [(x, D, 0)], [(g0m, None, 0)], [(D, bf16)], 256, also_t=(0,),
                                side=_side_allgather(first))
    take(("l0_w_in",), first[:1])
    w_in0 = _regroup_w_in0(w["l0_w_in"])
    wconv = first[1][:, :4, :conv_cols].transpose(1, 0, 2).reshape(4, NDEV * conv_cols)
    z0 = _mm("l0_in", h0, w_in0, NN, f32)
    qk_rows = [(z0, 2 * FOX_W, 0)]
    qk_consts = [(gq_t, None, 0), (gk_t, None, 0), (pm, None, 0)]
    qn, kn = _tok_fwd("fox_pre", _f_foxpre, qk_rows, qk_consts, [(FOX_W, f32)] * 2, 256)
    ccol, crow = _fox_gate_fwd(z0, fbias, t)
    fox_o, got = _fox_attn_fwd(qn, kn, z0, ccol, crow, t, side=_side_allgather([send(nm) for nm in _L0_REST]))
    take(_L0_REST, got)
    conv_rows, conv_consts = [(z0, LANES, Z0_GQKV)], [(wconv, LANES, 0)]
    (qkv,) = _tok_fwd("gdn_conv", _f_conv, conv_rows, conv_consts, [(LANES, f32)], t, ncb=12, with_j=True)
    gate_rows = [(z0, LANES, Z0_SMALL)]
    gate_consts = [(e_beta, None, 0), (e_alpha, None, 0), (alog_b, None, 0), (dt_b, None, 0)]
    beta_b, g_b = _tok_fwd("gdn_gate", _f_gdngate, gate_rows, gate_consts, [(GDN_W, f32)] * 2, 256)
    intra_rows = [(qkv, GDN_W, 0), (qkv, GDN_W, 1), (qkv, GDN_W, 2), (beta_b, GDN_W, 0), (g_b, GDN_W, 0)]
    intra, got = _tok_fwd("gdn_intra", _f_gdn_intra, intra_rows, [], [(GDN_W, f32)] * 7, 2 * CH,
                          side=_side_allgather([send(nm) for nm in _L1_MIX]))
    take(_L1_MIX, got)
    inter_rows = [(a, 0, 1) for a in intra[:6]]
    (gdn_o, gdn_hist), got = _scan_fwd("gdn_scan", _f_gdn_inter, inter_rows, GDN_H, nchunk,
                                       side=_side_allgather([send(nm) for nm in _L1_FFN[1:]]))
    take(_L1_FFN[1:], got)
    post0_rows, post0_consts = [(fox_o, FOX_W, 0), (gdn_o, GDN_W, 0), (z0, GDN_W, Z0_GG // 4)], [(on0_t, None, 0)]
    cat0, cat0t = _tok_fwd("l0_post", _f_post0, post0_rows, post0_consts, [(D, bf16)], 256, also_t=(0,))
    x1, hf0, hf0t = _mm("l0_out", cat0, w["l0_w_out"], NN, f32, epi=_epi_res_norm, extra=(x,), consts=(g0f,), outs=_RES_NORM_OUTS)
    (x2, h1, h1t), a0 = _mlp_fwd("l0", x1, hf0, w["l0_w_ff1"], w["l0_w_ff2"], _epi_res_norm, (), (g1m,), _RES_NORM_OUTS)

    z1 = _mm("l1_in", h1, w["l1_w_in"], NN, f32)
    hg_rows = [(z1, 0, 2), (z1, 2, 1)]
    (hg_o, hg_hist), got = _scan_fwd("hgrn_scan", _f_hgrn_gated, hg_rows, HG_H, nchunk, consts=(lbl,),
                                     side=_side_allgather([send(nm) for nm in _L1_FFN[:1]]))
    take(_L1_FFN[:1], got)
    post1_rows, post1_consts = [(hg_o, D, 0), (z1, D, 3)], [(on1_t, None, 0)]
    cat1, cat1t = _tok_fwd("l1_post", _f_post1, post1_rows, post1_consts, [(D, bf16)], 256, also_t=(0,))
    x3, hf1, hf1t = _mm("l1_out", cat1, w["l1_w_out"], NN, f32, epi=_epi_res_norm, extra=(x2,), consts=(g1f,), outs=_RES_NORM_OUTS)
    (dy, dyt, loss_row), a1 = _mlp_fwd("l1", x3, hf1, w["l1_w_ff1"], w["l1_w_ff2"], _epi_loss, (tgt,), (), _LOSS_OUTS)

    (dx3, g["l1_ffn_norm"], _), ga_ff1, ga_ff2 = _mlp_bwd("l1", x3, g1f, w["l1_w_ff1"], w["l1_w_ff2"], hf1t, a1, dy, dyt)
    dcat1 = _mm("l1_out_dx", dx3, w["l1_w_out"], NT, f32)
    ga_out = _mm("l1_out_dw", cat1t, dx3, NN, bf16)
    dz1, dlbl, don1 = _scan_bwd("hgrn_scan_bwd", _f_hgrn_full, [(z1, 0, 4)], hg_hist, dcat1, HG_H, nchunk, dtypes=[bf16],
                                consts=(lbl, on1_t))
    dx2, g["l1_mix_norm"], dx2t = _mm("l1_in_dx", dz1, w["l1_w_in"], NT, f32, epi=_epi_norm_bwd, extra=(x2, dx3), consts=(g1m,),
                                      outs=_NORM_BWD_OUTS)
    ga_in = _mm("l1_in_dw", h1t, dz1, NN, bf16, slab=True)
    g["l1_hgrn_o_norm"] = don1.reshape(HG_H, HD).sum(0)
    g["hgrn_lb_logits"] = dlbl
    gs_a = [by_dev(nm, ga) for nm, ga in zip(_GRAD_A, (ga_ff1, ga_ff2, ga_out, ga_in))]

    ((dx1, g["l0_ffn_norm"], _), gb_ff1, gb_ff2), r1_a = _mlp_bwd("l0", x1, g0f, w["l0_w_ff1"], w["l0_w_ff2"], hf0t, a0, dx2, dx2t,
                                                                  side=_side_rs_sibling(gs_a))
    pairs_a = pair(_GRAD_A, gs_a, r1_a)
    dcat0 = _mm("l0_out_dx", dx1, w["l0_w_out"], NT, f32)
    gb_out = _mm("l0_out_dw", cat0t, dx1, NN, bf16)
    gs_b = [by_dev(nm, ga) for nm, ga in zip(_GRAD_B, (gb_ff1, gb_ff2, gb_out))]
    dfox_o, dgdn_o, dgg, don0 = _tok_bwd("l0_post_bwd", _f_post0, post0_rows, post0_consts, [(dcat0, D)], 256, [0, 1, 2], [0],
                                         drow_dtype=[f32, f32, bf16])
    (dqn, dkn, dfv, dcq, dck), got = _fox_attn_bwd(qn, kn, z0, ccol, crow, dfox_o, t,
                                                   side=_join_sides([_side_rs_chips(pairs_a), _side_rs_sibling(gs_b)]))
    r2_a, r1_b = got[:len(_GRAD_A)], got[len(_GRAD_A):]
    adam(_GRAD_A, gs_a, r1_a, r2_a)
    pairs_b = pair(_GRAD_B, gs_b, r1_b)
    dinter = _scan_bwd("gdn_scan_bwd", _f_gdn_inter, inter_rows, gdn_hist, dgdn_o, GDN_H, nchunk)
    (dqkv, dbeta_b, dg_b), r2_b = _tok_bwd("gdn_intra_bwd", _f_gdn_intra, intra_rows + [(intra[6], GDN_W, 0)], [],
                                           [(a, GDN_W) for a in dinter], 2 * CH, [0, 1, 2, 3, 4], [], ncat=3,
                                           side=_side_rs_chips(pairs_b))
    adam(_GRAD_B, gs_b, r1_b, r2_b)
    dzs_g, dalog_b, ddt_b = _tok_bwd("gdn_gate_bwd", _f_gdngate, gate_rows, gate_consts, [(dbeta_b, GDN_W), (dg_b, GDN_W)], 256, [0], [2, 3])
    dgqkv, dwconv = _tok_bwd("gdn_conv_bwd", _f_conv, conv_rows, conv_consts, [(dqkv, LANES)], t, [0], [0], ncb=12, with_j=True,
                             drow_dtype=bf16)
    dzs_f, dfb = _fox_gate_bwd(z0, fbias, dcq, dck, t)
    dzqk, dgq_t, dgk_t = _tok_bwd("fox_pre_bwd", _f_foxpre, qk_rows, qk_consts, [(dqn, FOX_W), (dkn, FOX_W)], 256, [0], [0, 1],
                                  drow_dtype=bf16)
    dz0 = jnp.concatenate([dzqk, dfv.astype(bf16), dgqkv, dgg, (dzs_g + dzs_f).astype(bf16), jnp.zeros((n, ZW0 - 3712), bf16)], axis=1)
    gs_c = [_col_slabs(_ungroup_w_in0(_mm("l0_in_dw", h0t, dz0, NN, bf16)), ncols["l0_w_in"])]
    r1_c = _rs_sibling(gs_c)
    pairs_c = pair(("l0_w_in",), gs_c, r1_c)
    (dx, g["l0_mix_norm"], _), r2_c = _mm("l0_in_dx", dz0, w_in0, NT, f32, epi=_epi_norm_bwd, extra=(x, dx1), consts=(g0m,),
                                          outs=_NORM_BWD_OUTS, side=_side_rs_chips(pairs_c))
    adam(("l0_w_in",), gs_c, r1_c, r2_c)
    g["l0_fox_q_norm"] = dgq_t.reshape(FOX_H, FOX_D).sum(0)
    g["l0_fox_k_norm"] = dgk_t.reshape(FOX_H, FOX_D).sum(0)
    g["l0_fox_f_bias"] = dfb[0, :FOX_H]
    g["l0_gdn_conv"] = dwconv
    g["l0_gdn_A_log"] = dalog_b.reshape(GDN_H, HD).sum(1)
    g["l0_gdn_dt_bias"] = ddt_b.reshape(GDN_H, HD).sum(1)
    g["l0_gdn_o_norm"] = don0.reshape(GDN_H, HD).sum(0)
    return loss_row, dx, g, big


_NAMES = ("l0_mix_norm", "l0_w_in", "l0_fox_q_norm", "l0_fox_k_norm", "l0_fox_f_bias", "l0_gdn_conv", "l0_gdn_A_log",
          "l0_gdn_dt_bias", "l0_gdn_o_norm", "l0_w_out", "l0_ffn_norm", "l0_w_ff1", "l0_w_ff2", "l1_mix_norm", "l1_w_in",
          "l1_hgrn_o_norm", "l1_w_out", "l1_ffn_norm", "l1_w_ff1", "l1_w_ff2", "hgrn_lb_logits")
_SMALL_NAMES = tuple(nm for nm in _NAMES if nm not in _BIG_NAMES)
_SMALL_ROWS = 16


def _pack_small(vals):
    flat = jnp.concatenate([vals[nm].reshape(-1).astype(f32) for nm in _SMALL_NAMES])
    return jnp.pad(flat, (0, _SMALL_ROWS * D - flat.shape[0])).reshape(_SMALL_ROWS, D)


def _unpack_small(packed, shapes):
    flat = packed.reshape(-1)
    out, off = {}, 0
    for nm in _SMALL_NAMES:
        size = 1
        for s in shapes[nm]:
            size *= s
        out[nm] = flat[off:off + size].reshape(shapes[nm])
        off += size
    return out, off


def kernel(x, l0_mix_norm, l0_w_in, l0_fox_q_norm, l0_fox_k_norm, l0_fox_f_bias, l0_gdn_conv, l0_gdn_A_log, l0_gdn_dt_bias, l0_gdn_o_norm, l0_w_out, l0_ffn_norm, l0_w_ff1, l0_w_ff2, l1_mix_norm, l1_w_in, l1_hgrn_o_norm, l1_w_out, l1_ffn_norm, l1_w_ff1, l1_w_ff2, hgrn_lb_logits, loss_target, m_l0_mix_norm, m_l0_w_in, m_l0_fox_q_norm, m_l0_fox_k_norm, m_l0_fox_f_bias, m_l0_gdn_conv, m_l0_gdn_A_log, m_l0_gdn_dt_bias, m_l0_gdn_o_norm, m_l0_w_out, m_l0_ffn_norm, m_l0_w_ff1, m_l0_w_ff2, m_l1_mix_norm, m_l1_w_in, m_l1_hgrn_o_norm, m_l1_w_out, m_l1_ffn_norm, m_l1_w_ff1, m_l1_w_ff2, m_hgrn_lb_logits, v_l0_mix_norm, v_l0_w_in, v_l0_fox_q_norm, v_l0_fox_k_norm, v_l0_fox_f_bias, v_l0_gdn_conv, v_l0_gdn_A_log, v_l0_gdn_dt_bias, v_l0_gdn_o_norm, v_l0_w_out, v_l0_ffn_norm, v_l0_w_ff1, v_l0_w_ff2, v_l1_mix_norm, v_l1_w_in, v_l1_hgrn_o_norm, v_l1_w_out, v_l1_ffn_norm, v_l1_w_ff1, v_l1_w_ff2, v_hgrn_lb_logits):
    args = dict(zip(_NAMES, (l0_mix_norm, l0_w_in, l0_fox_q_norm, l0_fox_k_norm, l0_fox_f_bias, l0_gdn_conv, l0_gdn_A_log, l0_gdn_dt_bias, l0_gdn_o_norm, l0_w_out, l0_ffn_norm, l0_w_ff1, l0_w_ff2, l1_mix_norm, l1_w_in, l1_hgrn_o_norm, l1_w_out, l1_ffn_norm, l1_w_ff1, l1_w_ff2, hgrn_lb_logits)))
    mom = dict(zip(_NAMES, (m_l0_mix_norm, m_l0_w_in, m_l0_fox_q_norm, m_l0_fox_k_norm, m_l0_fox_f_bias, m_l0_gdn_conv, m_l0_gdn_A_log, m_l0_gdn_dt_bias, m_l0_gdn_o_norm, m_l0_w_out, m_l0_ffn_norm, m_l0_w_ff1, m_l0_w_ff2, m_l1_mix_norm, m_l1_w_in, m_l1_hgrn_o_norm, m_l1_w_out, m_l1_ffn_norm, m_l1_w_ff1, m_l1_w_ff2, m_hgrn_lb_logits)))
    var = dict(zip(_NAMES, (v_l0_mix_norm, v_l0_w_in, v_l0_fox_q_norm, v_l0_fox_k_norm, v_l0_fox_f_bias, v_l0_gdn_conv, v_l0_gdn_A_log, v_l0_gdn_dt_bias, v_l0_gdn_o_norm, v_l0_w_out, v_l0_ffn_norm, v_l0_w_ff1, v_l0_w_ff2, v_l1_mix_norm, v_l1_w_in, v_l1_hgrn_o_norm, v_l1_w_out, v_l1_ffn_norm, v_l1_w_ff1, v_l1_w_ff2, v_hgrn_lb_logits)))
    nb, t, _ = x.shape
    dev = 4 * lax.axis_index("x") + 2 * lax.axis_index("y") + lax.axis_index("c")
    conv_cols = l0_gdn_conv.shape[1]
    loss_row, dx, g, big = _train(x.reshape(nb * t, D), loss_target.reshape(nb * t, D), args, mom, var, t)

    shapes = {nm: args[nm].shape for nm in _SMALL_NAMES}
    gsm = dict(g)
    gsm["l0_gdn_conv"] = jnp.zeros(shapes["l0_gdn_conv"], f32)
    packed = _pack_small(gsm)
    _, used = _unpack_small(packed, shapes)
    flat_extra = jnp.concatenate([jnp.sum(loss_row).reshape(1), g["l0_gdn_conv"].reshape(-1)])
    packed = packed.reshape(-1).at[used:used + flat_extra.shape[0]].set(flat_extra).reshape(_SMALL_ROWS, D)
    (parts,) = _allgather("ag_small", [packed])
    total = _sum_parts(parts).reshape(-1)
    loss = 0.5 * total[used] / D
    conv_g_full = total[used + 1:used + 1 + 4 * NDEV * conv_cols].reshape(4, NDEV * conv_cols)
    conv_g = lax.dynamic_slice(conv_g_full, (0, dev * conv_cols), (4, conv_cols))
    own_vals = {nm: jnp.zeros(shapes[nm], f32) for nm in _SMALL_NAMES}
    own_vals["l0_gdn_conv"] = conv_g
    own_mask = {nm: jnp.zeros(shapes[nm], f32) for nm in _SMALL_NAMES}
    own_mask["l0_gdn_conv"] = jnp.ones(shapes["l0_gdn_conv"], f32)
    small = _adamw_small(parts, _pack_small(args), _pack_small(mom), _pack_small(var), _pack_small(own_mask), _pack_small(own_vals))
    small = [_unpack_small(a, shapes)[0] for a in small]
    small[0]["l0_gdn_conv"] = conv_g

    outs = [loss, dx.reshape(nb, t, D)]
    for k in range(4):
        outs += [big[nm][k] if nm in _BIG_NAMES else small[k][nm] for nm in _NAMES]
    return tuple(outs)
```

```python
import functools

import jax
import jax.numpy as jnp
from jax import lax
from jax.experimental import pallas as pl
from jax.experimental.pallas import tpu as pltpu

f32, bf16 = jnp.float32, jnp.bfloat16
NN = (((1,), (0,)), ((), ()))
NT = (((1,), (1,)), ((), ()))
TN = (((0,), (0,)), ((), ()))
HI = lax.Precision.HIGHEST
MESH = pl.DeviceIdType.MESH
S = jax.ShapeDtypeStruct

EPS = 1e-6
D = 1024
LANES = 128
FOX_H, FOX_D, FOX_W = 8, 64, 512
GDN_H, HD, GDN_W = 4, 128, 512
HG_H = 8
CH = 64
ZW0 = 3840
NDEV = 8
ADAM_LR, ADAM_B1, ADAM_B2, ADAM_EPS, ADAM_WD, ADAM_STEP = 0.001, 0.9, 0.999, 1e-08, 0.01, 10

Z0_FQK, Z0_FV, Z0_GQKV, Z0_GG, Z0_SMALL = 0, 8, 12, 24, 28


def _dot(a, b, dims=NN, prec=None):
    return lax.dot_general(a, b, dims, precision=prec, preferred_element_type=f32)


def _iota2(shape, axis):
    return lax.broadcasted_iota(jnp.int32, shape, axis)


def _split3(x):
    x1 = x.astype(bf16)
    r = x - x1.astype(f32)
    x2 = r.astype(bf16)
    return x1, x2, (r - x2.astype(f32)).astype(bf16)


def _dot_sel(a, b, dims=NN, exact_lhs=False):
    if exact_lhs:
        return sum(_dot(a.astype(bf16), piece, dims) for piece in _split3(b))
    return sum(_dot(piece, b.astype(bf16), dims) for piece in _split3(a))


@jax.custom_vjp
def _sel_rhs(a, b):
    return _dot_sel(a, b)


_sel_rhs.defvjp(lambda a, b: (_dot_sel(a, b), b), lambda b, g: (_dot_sel(g, b, NT), jnp.zeros_like(b)))


@jax.custom_vjp
def _sel_lhs(a, x):
    return _dot_sel(a, x, exact_lhs=True)


_sel_lhs.defvjp(lambda a, x: (_dot_sel(a, x, exact_lhs=True), a), lambda a, g: (jnp.zeros_like(a), _dot_sel(a, g, TN, exact_lhs=True)))


class _Side:
    def __init__(self, ins, out_shapes, scratch, start, finish):
        self.ins, self.out_shapes, self.scratch, self.start, self.finish = list(ins), list(out_shapes), list(scratch), start, finish


def _join_sides(sides):
    def split(refs, counts):
        out, off = [], 0
        for c in counts:
            out.append(refs[off:off + c])
            off += c
        return out

    ni, no, ns = ([len(getattr(sd, a)) for sd in sides] for a in ("ins", "out_shapes", "scratch"))

    def run(which):
        def go(ins, outs, sems):
            for sd, i, o, c in zip(sides, split(ins, ni), split(outs, no), split(sems, ns)):
                getattr(sd, which)(i, o, c)
        return go

    return _Side(sum((sd.ins for sd in sides), []), sum((sd.out_shapes for sd in sides), []),
                 sum((sd.scratch for sd in sides), []), run("start"), run("finish"))


def _pcall(side, body, *, name, grid, in_specs, out_specs, out_shape, scratch_shapes=(), compiler_params=None):
    if side is None:
        return pl.pallas_call(body, name=name, grid=grid, in_specs=in_specs, out_specs=out_specs, out_shape=out_shape,
                              scratch_shapes=scratch_shapes, compiler_params=compiler_params)
    single = not isinstance(out_shape, (list, tuple))
    ospecs, oshape = ([out_specs], [out_shape]) if single else (list(out_specs), list(out_shape))
    nin, nout, nscr = len(in_specs), len(ospecs), len(scratch_shapes)
    si, so = len(side.ins), len(side.out_shapes)

    def wrapped(*refs):
        o0 = nin + si
        c0 = o0 + nout + so
        sins, souts, ssems = refs[nin:o0], refs[o0 + nout:c0], refs[c0 + nscr:]
        ids = [pl.program_id(a) for a in range(len(grid))]
        first = functools.reduce(jnp.logical_and, [i == 0 for i in ids])
        last = functools.reduce(jnp.logical_and, [i == g - 1 for i, g in zip(ids, grid)])

        @pl.when(first)
        def _():
            side.start(sins, souts, ssems)

        body(*refs[:nin], *refs[o0:o0 + nout], *refs[c0:c0 + nscr])

        @pl.when(last)
        def _():
            side.finish(sins, souts, ssems)

    call = pl.pallas_call(
        wrapped, name=name, grid=grid, in_specs=list(in_specs) + _hbm_specs(si), out_specs=ospecs + _hbm_specs(so),
        out_shape=oshape + side.out_shapes, scratch_shapes=list(scratch_shapes) + side.scratch,
        compiler_params=pltpu.CompilerParams(dimension_semantics=("arbitrary",) * len(grid),
                                             vmem_limit_bytes=getattr(compiler_params, "vmem_limit_bytes", None)))

    def run(*args):
        res = call(*args, *side.ins)
        return (res[0] if single else list(res[:nout])), list(res[nout:])

    return run


def _tok_specs(rows, consts, tm):
    specs = []
    for (_, w, base) in rows:
        specs.append(pl.BlockSpec((tm, w), functools.partial(lambda j, i, b: (i, b + j), b=base)))
    for (arr, w, base) in consts:
        if w is None:
            specs.append(pl.BlockSpec(arr.shape, lambda j, i: (0, 0)))
        else:
            specs.append(pl.BlockSpec((arr.shape[0], w), functools.partial(lambda j, i, b: (0, b + j), b=base)))
    return specs


def _tok_fwd(name, f, rows, consts, outs, tm, ncb=1, with_j=False, also_t=(), side=None):
    n = rows[0][0].shape[0]
    nin = len(rows) + len(consts)
    nout = len(outs)

    def body(*refs):
        ins = [r[...] for r in refs[:nin]]
        vals = f(pl.program_id(0), *ins) if with_j else f(*ins)
        for r, v in zip(refs[nin:nin + nout], vals):
            r[...] = v.astype(r.dtype)
        for r, k in zip(refs[nin + nout:], also_t):
            r[...] = vals[k].T.astype(r.dtype)

    return _pcall(
        side, body, name=name, grid=(ncb, n // tm),
        in_specs=_tok_specs(rows, consts, tm),
        out_specs=[pl.BlockSpec((tm, w), lambda j, i: (i, j)) for (w, _) in outs]
        + [pl.BlockSpec((outs[k][0], tm), lambda j, i: (0, i)) for k in also_t],
        out_shape=[S((n, w * ncb), dt) for (w, dt) in outs] + [S((outs[k][0], n), bf16) for k in also_t],
        compiler_params=pltpu.CompilerParams(dimension_semantics=("parallel", "parallel")),
    )(*[r[0] for r in rows], *[c[0] for c in consts])


def _tok_bwd(name, f, rows, consts, cots, tm, drow, dconst, ncb=1, with_j=False, addto=None, also_t=(), drow_dtype=f32, side=None, ncat=0):
    n = rows[0][0].shape[0]
    nr, nc, nct = len(rows), len(consts), len(cots)
    addto = addto or {}
    add_keys = sorted(addto)
    nadd = len(add_keys)

    def body(*refs):
        ins = [r[...] for r in refs[:nr + nc]]
        cot = [r[...] for r in refs[nr + nc:nr + nc + nct]]
        adds = refs[nr + nc + nct:nr + nc + nct + nadd]
        outs = refs[nr + nc + nct + nadd:]
        pos = list(drow) + [nr + k for k in dconst]

        def g(*dargs):
            full = list(ins)
            for p, a in zip(pos, dargs):
                full[p] = a
            return tuple(f(pl.program_id(0), *full) if with_j else f(*full))

        vals, vjp = jax.vjp(g, *[ins[p] for p in pos])
        grads = vjp(tuple(c.astype(v.dtype) for c, v in zip(cot, vals)))
        off = 0
        for k in range(len(drow)):
            gk = grads[k]
            if k in addto:
                gk = gk + adds[add_keys.index(k)][...]
            if k < ncat:
                outs[0][:, off:off + gk.shape[1]] = gk.astype(outs[0].dtype)
                off += gk.shape[1]
            else:
                outs[k - skip][...] = gk.astype(outs[k - skip].dtype)
            if k in also_t:
                tref = outs[len(drow) - skip + len(dconst) + list(also_t).index(k)]
                tref[...] = gk.T.astype(tref.dtype)
        first = pl.program_id(1) == 0
        for k in range(len(dconst)):
            ref = outs[len(drow) - skip + k]

            @pl.when(first)
            def _():
                ref[...] = jnp.zeros_like(ref)

            ref[...] += grads[len(drow) + k]

    skip = max(ncat - 1, 0)
    in_specs = _tok_specs(rows, consts, tm)
    in_specs += [pl.BlockSpec((tm, w), lambda j, i: (i, j)) for (_, w) in cots]
    in_specs += [pl.BlockSpec((tm, rows[drow[k]][1]), lambda j, i: (i, j)) for k in add_keys]
    dts = drow_dtype if isinstance(drow_dtype, (list, tuple)) else [drow_dtype] * len(drow)
    widths = [rows[k][1] for k in drow]
    if ncat:
        widths, dts = [sum(widths[:ncat])] + widths[ncat:], [dts[0]] + list(dts[ncat:])
    out_specs = [pl.BlockSpec((tm, wd), lambda j, i: (i, j)) for wd in widths]
    out_shape = [S((n, wd * ncb), dt) for wd, dt in zip(widths, dts)]
    for k in dconst:
        arr, w, _ = consts[k]
        if w is None:
            out_specs.append(pl.BlockSpec(arr.shape, lambda j, i: (0, 0)))
            out_shape.append(S(arr.shape, f32))
        else:
            out_specs.append(pl.BlockSpec((arr.shape[0], w), lambda j, i: (0, j)))
            out_shape.append(S((arr.shape[0], w * ncb), f32))
    for k in also_t:
        out_specs.append(pl.BlockSpec((rows[drow[k]][1], tm), lambda j, i: (0, i)))
        out_shape.append(S((rows[drow[k]][1], n), bf16))
    return _pcall(
        side, body, name=name, grid=(ncb, n // tm), in_specs=in_specs, out_specs=out_specs, out_shape=out_shape,
        compiler_params=pltpu.CompilerParams(dimension_semantics=("parallel", "arbitrary")),
    )(*[r[0] for r in rows], *[c[0] for c in consts], *[c[0] for c in cots], *[addto[k] for k in add_keys])


def _scan_fwd(name, f, rows, nh, nchunk, side=None, consts=(), out_dtype=f32, out_k=1):
    n = rows[0][0].shape[0]
    nb = n // (CH * nchunk)
    nin, nco = len(rows), len(consts)
    w = nh * HD

    def body(*refs):
        o_ref, hist_ref, st = refs[nin + nco], refs[nin + nco + 1], refs[nin + nco + 2]

        @pl.when(pl.program_id(0) == 0)
        def _():
            st[...] = jnp.zeros_like(st)

        s0 = st[...]
        hist_ref[0] = s0.astype(hist_ref.dtype)
        tiles = [r[...].reshape(nb * CH, r.shape[2]) for r in refs[:nin]]
        o, s1 = f(*tiles, *[r[...] for r in refs[nin:nin + nco]], s0)
        o_ref[...] = o.reshape(o_ref.shape).astype(o_ref.dtype)
        st[...] = s1

    seq3 = lambda a: a.reshape(nb, nchunk * CH, a.shape[1])
    res = _pcall(
        side, body, name=name, grid=(nchunk,),
        in_specs=[pl.BlockSpec((nb, CH, k * w), functools.partial(lambda c, base: (0, c, base), base=b)) for (_, b, k) in rows]
        + [pl.BlockSpec(c.shape, lambda c: (0, 0)) for c in consts],
        out_specs=[pl.BlockSpec((nb, CH, out_k * w), lambda c: (0, c, 0)), pl.BlockSpec((1, nb * w, HD), lambda c: (c, 0, 0))],
        out_shape=[S((nb, nchunk * CH, out_k * w), out_dtype), S((nchunk, nb * w, HD), bf16)],
        scratch_shapes=[pltpu.VMEM((nb * w, HD), f32)],
        compiler_params=pltpu.CompilerParams(dimension_semantics=("arbitrary",)),
    )(*[seq3(r[0]) for r in rows], *consts)
    (o, hist), extra = (res, None) if side is None else res
    out = [o.reshape(n, out_k * w), hist]
    return out if side is None else (out, extra)


def _scan_bwd(name, f, rows, hist, do, nh, nchunk, side=None, dtypes=None, consts=(), out_k=1):
    n = rows[0][0].shape[0]
    nb = n // (CH * nchunk)
    nin, nco = len(rows), len(consts)
    w = nh * HD

    def body(*refs):
        hist_ref, do_ref = refs[nin + nco], refs[nin + nco + 1]
        outs = refs[nin + nco + 2:nin + nco + 2 + nin]
        couts = refs[nin + nco + 2 + nin:nin + nco + 2 + nin + nco]
        ds = refs[nin + nco + 2 + nin + nco]

        @pl.when(pl.program_id(0) == 0)
        def _():
            ds[...] = jnp.zeros_like(ds)
            for c in couts:
                c[...] = jnp.zeros_like(c)

        tiles = [r[...].reshape(nb * CH, r.shape[2]) for r in refs[:nin]]
        _, vjp = jax.vjp(f, *tiles, *[r[...] for r in refs[nin:nin + nco]], hist_ref[0].astype(f32))
        grads = vjp((do_ref[...].reshape(nb * CH, out_k * w), ds[...]))
        for r, gk in zip(outs, grads[:nin]):
            r[...] = gk.reshape(r.shape).astype(r.dtype)
        for c, gk in zip(couts, grads[nin:nin + nco]):
            c[...] += gk
        ds[...] = grads[nin + nco]

    seq3 = lambda a: a.reshape(nb, nchunk * CH, a.shape[1])
    rev = lambda c, base: (0, nchunk - 1 - c, base)
    res = _pcall(
        side, body, name=name, grid=(nchunk,),
        in_specs=[pl.BlockSpec((nb, CH, k * w), functools.partial(rev, base=b)) for (_, b, k) in rows]
        + [pl.BlockSpec(c.shape, lambda c: (0, 0)) for c in consts]
        + [pl.BlockSpec((1, nb * w, HD), lambda c: (nchunk - 1 - c, 0, 0)), pl.BlockSpec((nb, CH, out_k * w), functools.partial(rev, base=0))],
        out_specs=[pl.BlockSpec((nb, CH, k * w), functools.partial(rev, base=0)) for (_, _, k) in rows]
        + [pl.BlockSpec(c.shape, lambda c: (0, 0)) for c in consts],
        out_shape=[S((nb, nchunk * CH, k * w), dt) for (_, _, k), dt in zip(rows, dtypes or [f32] * nin)]
        + [S(c.shape, f32) for c in consts],
        scratch_shapes=[pltpu.VMEM((nb * w, HD), f32)],
        compiler_params=pltpu.CompilerParams(dimension_semantics=("arbitrary",)),
    )(*[seq3(r[0]) for r in rows], *consts, hist, seq3(do))
    outs, extra = (res, None) if side is None else res
    outs = [o.reshape(n, o.shape[2]) for o in outs[:nin]] + list(outs[nin:])
    return outs if side is None else (outs, extra)


_VMEM_LIMIT = 56 * 2 ** 20
_VMEM_TILE_BUDGET = 40 * 2 ** 20


def _mm_tiles(m, n, k, sa, sb, so, sx, a_f32, b_f32, tn_fixed):
    best = None
    for tm in (1024, 512, 256, 128, 64):
        for tn in ((tn_fixed,) if tn_fixed else (1024, 768, 512, 384, 256, 128)):
            if m % tm or n % tn:
                continue
            need = 2 * (tm * k * sa + k * tn * sb + tm * tn * (so + sx)) + tm * tn * 4
            need += tm * k * (2 if sa == 4 else 0) + k * tn * (2 if sb == 4 else 0)
            need += tm * k * (4 if a_f32 else 0) + k * tn * (4 if b_f32 else 0)
            if need <= _VMEM_TILE_BUDGET and (best is None or (tm * tn, tm) > best[0]):
                best = ((tm * tn, tm), tm, tn)
    return best[1], best[2]


def _mm(name, a, b, dims, out_dtype, a_fn=None, b_fn=None, epi=None, extra=(), consts=(), outs=None, out_t=False, slab=False,
        side=None):
    m, kk = a.shape
    nn = b.shape[1] if dims is NN else b.shape[0]
    kinds = [("tile", out_dtype)] if outs is None else list(outs)
    so = sum(jnp.dtype(dt).itemsize for kd, dt in kinds if kd != "rows")
    sx = sum(e.dtype.itemsize for e in extra)
    full_rows = bool(consts) or any(kd == "rows" for kd, _ in kinds)
    tm, tn = _mm_tiles(m, nn, kk, a.dtype.itemsize, b.dtype.itemsize, so, sx,
                       a_fn is not None, b_fn is not None, _COLW if slab else (nn if full_rows else None))
    nex, nco = len(extra), len(consts)

    def body(a_ref, b_ref, *rest):
        av, bv = a_ref[...], b_ref[...]
        if a_fn is not None:
            av = a_fn(av.astype(f32))
        if b_fn is not None:
            bv = b_fn(bv.astype(f32))
        acc = _dot(av.astype(bf16), bv.astype(bf16), dims)
        if epi is not None:
            acc = epi(acc, *[r[...] for r in rest[:nex + nco]])
        vals = acc if isinstance(acc, tuple) else (acc,)
        for (kd, _), o_ref, val in zip(kinds, rest[nex + nco:], vals):
            if kd == "rows":
                @pl.when(pl.program_id(0) == 0)
                def _():
                    o_ref[...] = jnp.zeros_like(o_ref)

                o_ref[...] += val
            elif kd == "tile_t" or out_t:
                o_ref[...] = val.T.astype(o_ref.dtype)
            elif slab:
                o_ref[0] = val.astype(o_ref.dtype)
            else:
                o_ref[...] = val.astype(o_ref.dtype)

    bspec = pl.BlockSpec((kk, tn), lambda i, j: (0, j)) if dims is NN else pl.BlockSpec((tn, kk), lambda i, j: (j, 0))
    out_specs, out_shape = [], []
    for kd, dt in kinds:
        if kd == "rows":
            out_specs.append(pl.BlockSpec((1, nn), lambda i, j: (0, 0)))
            out_shape.append(S((1, nn), dt))
        elif kd == "tile_t" or out_t:
            out_specs.append(pl.BlockSpec((tn, tm), lambda i, j: (j, i)))
            out_shape.append(S((nn, m), dt))
        elif slab:
            out_specs.append(pl.BlockSpec((1, tm, tn), lambda i, j: (j, i, 0)))
            out_shape.append(S((nn // tn, m, tn), dt))
        else:
            out_specs.append(pl.BlockSpec((tm, tn), lambda i, j: (i, j)))
            out_shape.append(S((m, nn), dt))
    if outs is None:
        out_specs, out_shape = out_specs[0], out_shape[0]
    sem = ("arbitrary", "arbitrary") if any(kd == "rows" for kd, _ in kinds) else ("parallel", "parallel")
    return _pcall(
        side, body, name=name, grid=(m // tm, nn // tn),
        in_specs=[pl.BlockSpec((tm, kk), lambda i, j: (i, 0)), bspec]
        + [pl.BlockSpec((tm, tn), lambda i, j: (i, j)) for _ in extra]
        + [pl.BlockSpec(c.shape, lambda i, j: (0, 0)) for c in consts],
        out_specs=out_specs, out_shape=out_shape,
        compiler_params=pltpu.CompilerParams(dimension_semantics=sem, vmem_limit_bytes=_VMEM_LIMIT),
    )(a, b, *extra, *consts)


def _f_norm(x, g):
    return (x * lax.rsqrt(jnp.mean(x * x, axis=-1, keepdims=True) + EPS) * g,)


def _f_foxpre(zqk, gq, gk, pm):
    def nrm(t, g):
        return t * lax.rsqrt(_sel_rhs(t * t, pm) + EPS) * g
    return nrm(zqk[:, :FOX_W], gq), nrm(zqk[:, FOX_W:], gk)


def _chunk_cumsum(x):
    n = x.shape[0]
    r, c = _iota2((n, n), 0), _iota2((n, n), 1)
    tri = jnp.logical_and(r >= c, (r // CH) == (c // CH)).astype(f32)
    return _sel_lhs(tri, x)


def _f_gdngate(zs, eb, ea, alog_b, dt_b):
    beta = jax.nn.sigmoid(_sel_rhs(zs, eb))
    la = -jnp.exp(alog_b) * jax.nn.softplus(_sel_rhs(zs, ea) + dt_b)
    return beta, _chunk_cumsum(la)


def _f_conv(j, x, w):
    t = x.shape[0]
    y = x * w[3:4, :]
    for jj in range(3):
        sh = 3 - jj
        xs = jnp.concatenate([jnp.zeros((sh, x.shape[1]), f32), x[:t - sh, :]], axis=0)
        y = y + xs * w[jj:jj + 1, :]
    y = jax.nn.silu(y)
    yn = y * lax.rsqrt(jnp.sum(y * y, axis=-1, keepdims=True) + EPS)
    return (jnp.where(j < 2 * GDN_H, yn, y),)


def _head_rms(o, nh):
    outs = []
    for h in range(nh):
        oh = o[:, HD * h:HD * (h + 1)]
        outs.append(oh * lax.rsqrt(jnp.mean(oh * oh, axis=-1, keepdims=True) + EPS))
    return jnp.concatenate(outs, axis=1)


def _f_post0(fox_o, o, gg, on):
    return (jnp.concatenate([fox_o, _head_rms(o, GDN_H) * on * jax.nn.silu(gg)], axis=1),)


def _f_post1(o, zg, on):
    return (_head_rms(o, HG_H) * on * jax.nn.silu(zg),)


def _f_hpre(zqf, lbl):
    lb = jax.nn.sigmoid(lbl[1:2, :] - lbl[0:1, :])
    fg = lb + (1.0 - lb) * jax.nn.sigmoid(zqf[:, D:])
    return jax.nn.silu(zqf[:, :D]), 1.0 - fg, _chunk_cumsum(jnp.log(fg))


def _dotb(a, b, dims=NN):
    return _dot(a.astype(bf16), b.astype(bf16), dims)


def _dot3(a, b):
    ah, bh = a.astype(bf16), b.astype(bf16)
    al, bl = (a - ah.astype(f32)).astype(bf16), (b - bh.astype(f32)).astype(bf16)
    return _dot(ah, bh) + (_dot(ah, bl) + _dot(al, bh))


def _split(t, nh):
    return [t[CH * ck:CH * (ck + 1), HD * h:HD * (h + 1)] for ck in range(t.shape[0] // CH) for h in range(nh)]


def _merge(units, nh):
    return jnp.concatenate([jnp.concatenate(units[i:i + nh], axis=1) for i in range(0, len(units), nh)], axis=0)


def _inv_impl(amats):
    n = amats[0].shape[0]
    eye = jnp.where(_iota2((n, n), 0) == _iota2((n, n), 1), 1.0, 0.0).astype(f32)
    xs, ps = [eye - a for a in amats], list(amats)
    for _ in range(max(1, (n - 1).bit_length()) - 1):
        ps = [_dotb(p, p) for p in ps]
        xs = [x + _dotb(x, p) for x, p in zip(xs, ps)]
    for _ in range(3):
        rs = [eye - x - _dot3(a, x) for a, x in zip(amats, xs)]
        xs = [x + _dotb(x, r) for x, r in zip(xs, rs)]
    return tuple(xs)


@jax.custom_vjp
def _inv_unit_lower(amats):
    return _inv_impl(amats)


def _inv_fwd(amats):
    xs = _inv_impl(amats)
    return xs, xs


def _inv_bwd(xs, dxs):
    return (tuple(-_dotb(_dotb(x, dx, TN), x, NT) for x, dx in zip(xs, dxs)),)


_inv_unit_lower.defvjp(_inv_fwd, _inv_bwd)


@jax.custom_vjp
def _inv_given(amats, xs):
    return xs


def _inv_given_fwd(amats, xs):
    return xs, xs


def _inv_given_bwd(xs, dxs):
    return _inv_bwd(xs, dxs)[0], tuple(jnp.zeros_like(x) for x in xs)


_inv_given.defvjp(_inv_given_fwd, _inv_given_bwd)


def _f_gdn_intra(q, k, v, bb, gb, tinv_p=None):
    qs, ks, vs, bs, gs = (_split(t, GDN_H) for t in (q, k, v, bb, gb))
    r, cc = _iota2((CH, CH), 0), _iota2((CH, CH), 1)
    causal, strict = r >= cc, r > cc
    beta, g, gl = [b[:, :1] for b in bs], [x[:, :1] for x in gs], [x[CH - 1:CH, :1] for x in gs]
    decay = [jnp.exp(jnp.where(causal, x[:, :CH] - x[:, :CH].T, -jnp.inf)) for x in gs]
    kb = [ki * bi for ki, bi in zip(ks, beta)]
    amat = [jnp.where(strict, _dotb(kbi, ki, NT) * di, 0.0) for kbi, ki, di in zip(kb, ks, decay)]
    if tinv_p is None:
        tinv = _inv_unit_lower(tuple(amat))
    else:
        tinv = _inv_given(tuple(amat), tuple(x[:, :CH] for x in _split(tinv_p, GDN_H)))
    rhs = [jnp.concatenate([vi * bi, kbi * jnp.exp(gi)], axis=1) for vi, bi, kbi, gi in zip(vs, beta, kb, g)]
    uw = [_dotb(ti, ri) for ti, ri in zip(tinv, rhs)]
    qsc = [qi * (HD ** -0.5) for qi in qs]
    qk = [jnp.where(causal, _dotb(qi, ki, NT) * di, 0.0) for qi, ki, di in zip(qsc, ks, decay)]
    outs = ([x[:, :HD] for x in uw], [x[:, HD:] for x in uw],
            [jnp.concatenate([x, jnp.zeros_like(x)], axis=1) for x in qk],
            [qi * jnp.exp(gi) for qi, gi in zip(qsc, g)],
            [ki * jnp.exp(gli - gi) for ki, gli, gi in zip(ks, gl, g)],
            [jnp.broadcast_to(gli, (CH, HD)) for gli in gl])
    if tinv_p is None:
        outs += ([jnp.concatenate([x, jnp.zeros_like(x)], axis=1) for x in tinv],)
    return tuple(_merge(o, GDN_H) for o in outs)


def _f_gdn_inter(u, w, qkp, qd, kd, glb, st):
    us, ws, qks, qds, kds, gls = (_split(t, GDN_H) for t in (u, w, qkp, qd, kd, glb))
    sts = [st[HD * i:HD * (i + 1), :] for i in range(len(us))]
    vn = [ui - _dotb(wi, si) for ui, wi, si in zip(us, ws, sts)]
    o = [_dotb(qi, si) + _dotb(xi[:, :CH], vi) for qi, si, xi, vi in zip(qds, sts, qks, vn)]
    s2 = [si * jnp.exp(gi[:1, :1]) + _dotb(ki, vi, TN) for si, gi, ki, vi in zip(sts, gls, kds, vn)]
    return _merge(o, GDN_H), jnp.concatenate(s2, axis=0)


def _f_gdn_full(u, w, qkp, qd, kd, glb, fox_o, gg, on, st):
    o, s2 = _f_gdn_inter(u, w, qkp, qd, kd, glb, st)
    return _f_post0(fox_o, o, gg, on)[0], s2


def _f_hgrn_chunk(q, k, b, v, st):
    qs, ks, bs, vs = (_split(t, HG_H) for t in (q, k, b, v))
    sts = [st[HD * i:HD * (i + 1), :] for i in range(len(qs))]
    causal = _iota2((CH, CH), 0) >= _iota2((CH, CH), 1)
    bl, bm = [x[CH - 1:CH, :] for x in bs], [x[CH // 2 - 1:CH // 2, :] for x in bs]
    a = [jnp.where(causal, _dotb(qi * jnp.exp(bi - mi), ki * jnp.exp(mi - bi), NT), 0.0)
         for qi, ki, bi, mi in zip(qs, ks, bs, bm)]
    o = [_dotb(qi * jnp.exp(bi), si, NT) + _dotb(ai, vi) for qi, bi, si, ai, vi in zip(qs, bs, sts, a, vs)]
    s2 = [si * jnp.exp(li) + _dotb(vi, ki * jnp.exp(li - bi), TN) for si, li, vi, ki, bi in zip(sts, bl, vs, ks, bs)]
    return _merge(o, HG_H), jnp.concatenate(s2, axis=0)


def _f_hgrn_full(z, lbl, on, st):
    o, s2 = _f_hgrn_chunk(*_f_hpre(z[:, :2 * D], lbl), z[:, 2 * D:3 * D], st)
    return _f_post1(o, z[:, 3 * D:], on)[0], s2


def _fox_gate_fwd(z0, fbias, t, tc=256):
    n = z0.shape[0]
    nt = t // tc

    def body(zs_ref, b_ref, ccol_ref, crow_ref, carry):
        @pl.when(pl.program_id(1) == 0)
        def _():
            carry[...] = jnp.zeros_like(carry)

        ls = jnp.where(_iota2((tc, LANES), 1) < FOX_H, jax.nn.log_sigmoid(zs_ref[...] + b_ref[...]), 0.0)
        tri = (_iota2((tc, tc), 0) >= _iota2((tc, tc), 1)).astype(f32)
        c = _dot_sel(tri, ls, exact_lhs=True) + carry[...]
        carry[...] = c[tc - 1:tc, :]
        ccol_ref[...] = c
        crow_ref[0] = c.T[:FOX_H, :]

    return pl.pallas_call(
        body, name="fox_gate_fwd", grid=(n // t, nt),
        in_specs=[pl.BlockSpec((tc, LANES), lambda b, i: (b * nt + i, Z0_SMALL)), pl.BlockSpec((1, LANES), lambda b, i: (0, 0))],
        out_specs=[pl.BlockSpec((tc, LANES), lambda b, i: (b * nt + i, 0)), pl.BlockSpec((1, FOX_H, tc), lambda b, i: (b, 0, i))],
        out_shape=[S((n, LANES), f32), S((n // t, FOX_H, t), f32)],
        scratch_shapes=[pltpu.VMEM((1, LANES), f32)],
        compiler_params=pltpu.CompilerParams(dimension_semantics=("parallel", "arbitrary")),
    )(z0, fbias)


def _fox_gate_bwd(z0, fbias, dcq, dck, t, tc=256):
    n = z0.shape[0]
    nt = t // tc

    def body(zs_ref, b_ref, dcq_ref, dck_ref, dz_ref, db_ref, carry):
        first = jnp.logical_and(pl.program_id(0) == 0, pl.program_id(1) == 0)

        @pl.when(pl.program_id(1) == 0)
        def _():
            carry[...] = jnp.zeros_like(carry)

        @pl.when(first)
        def _():
            db_ref[...] = jnp.zeros_like(db_ref)

        dc = dcq_ref[0] + dcq_ref[1] + dcq_ref[2] + dcq_ref[3]
        drow = dck_ref[0, 0] + dck_ref[1, 0] + dck_ref[2, 0] + dck_ref[3, 0]
        eye = (_iota2((FOX_H, LANES), 0) == _iota2((FOX_H, LANES), 1)).astype(f32)
        dc = dc + _dot_sel(drow, eye, TN)
        triu = (_iota2((tc, tc), 0) <= _iota2((tc, tc), 1)).astype(f32)
        dls = _dot_sel(triu, dc, exact_lhs=True) + carry[...]
        carry[...] = dls[0:1, :]
        x = zs_ref[...] + b_ref[...]
        dz = jnp.where(_iota2((tc, LANES), 1) < FOX_H, dls * jax.nn.sigmoid(-x), 0.0)
        dz_ref[...] = dz
        db_ref[...] += jnp.sum(dz, axis=0, keepdims=True)

    def rev(b, i):
        return b * nt + (nt - 1 - i)

    return pl.pallas_call(
        body, name="fox_gate_bwd", grid=(n // t, nt),
        in_specs=[pl.BlockSpec((tc, LANES), lambda b, i: (rev(b, i), Z0_SMALL)), pl.BlockSpec((1, LANES), lambda b, i: (0, 0)),
                  pl.BlockSpec((4, tc, LANES), lambda b, i: (0, rev(b, i), 0)),
                  pl.BlockSpec((4, 1, FOX_H, tc), lambda b, i: (0, b, 0, nt - 1 - i))],
        out_specs=[pl.BlockSpec((tc, LANES), lambda b, i: (rev(b, i), 0)), pl.BlockSpec((1, LANES), lambda b, i: (0, 0))],
        out_shape=[S((n, LANES), f32), S((1, LANES), f32)],
        scratch_shapes=[pltpu.VMEM((1, LANES), f32)],
        compiler_params=pltpu.CompilerParams(dimension_semantics=("arbitrary", "arbitrary")),
    )(z0, fbias, dcq, dck)


def _fox_scores(hh, p, i, tq, q, k, ccol, crow):
    kmax = k.shape[0]
    lane = _iota2((1, LANES), 1)
    mh = (lane // FOX_D) == hh
    h = 2 * p + hh
    qh = jnp.where(mh, q, 0.0).astype(bf16)
    s = _dot(qh, k, NT) * (FOX_D ** -0.5)
    cq = jnp.sum(jnp.where(lane == h, ccol, 0.0), axis=1, keepdims=True)
    ck = jnp.sum(jnp.where(_iota2((FOX_H, 1), 0) == h, crow, 0.0), axis=0, keepdims=True)
    causal = _iota2((1, kmax), 1) <= (i * tq + _iota2((tq, 1), 0))
    s = jnp.where(causal, s + cq - ck, -jnp.inf)
    pe = jnp.exp(s - jnp.max(s, axis=1, keepdims=True))
    return mh, qh, pe, jnp.sum(pe, axis=1, keepdims=True)


def _fox_attn_fwd(qn, kn, z0, ccol, crow, t, tq=256, side=None):
    n = qn.shape[0]
    nq = t // tq

    def body(q_ref, k_ref, v_ref, ccol_ref, crow_ref, o_ref):
        p = pl.program_id(1)
        k, v, crow = k_ref[...].astype(bf16), v_ref[...].astype(bf16), crow_ref[0]
        for i in range(nq):
            rows, kmax = pl.ds(i * tq, tq), (i + 1) * tq
            q, cc = q_ref[rows, :], ccol_ref[rows, :]
            acc = jnp.zeros((tq, LANES), f32)
            for hh in range(2):
                mh, _, pe, l = _fox_scores(hh, p, i, tq, q, k[:kmax], cc, crow[:, :kmax])
                acc = jnp.where(mh, _dot(pe.astype(bf16), v[:kmax]) / l, acc)
            o_ref[rows, :] = acc

    seq = lambda b, p: (b, p)
    return _pcall(
        side, body, name="fox_attn_fwd", grid=(n // t, FOX_H // 2),
        in_specs=[pl.BlockSpec((t, LANES), seq), pl.BlockSpec((t, LANES), seq), pl.BlockSpec((t, LANES), lambda b, p: (b, Z0_FV + p)),
                  pl.BlockSpec((t, LANES), lambda b, p: (b, 0)), pl.BlockSpec((1, FOX_H, t), lambda b, p: (b, 0, 0))],
        out_specs=pl.BlockSpec((t, LANES), seq),
        out_shape=S((n, FOX_W), f32),
        compiler_params=pltpu.CompilerParams(dimension_semantics=("parallel", "parallel")),
    )(qn, kn, z0, ccol, crow)


def _fox_attn_bwd(qn, kn, z0, ccol, crow, do, t, tq=256, side=None):
    n = qn.shape[0]
    nq = t // tq
    nb = n // t

    def body(q_ref, k_ref, v_ref, ccol_ref, crow_ref, do_ref, dq_ref, dk_ref, dv_ref, dcq_ref, dck_ref):
        p = pl.program_id(1)
        dk_ref[...] = jnp.zeros_like(dk_ref)
        dv_ref[...] = jnp.zeros_like(dv_ref)
        dck_ref[...] = jnp.zeros_like(dck_ref)
        kf, v, crow = k_ref[...], v_ref[...].astype(bf16), crow_ref[0]
        k = kf.astype(bf16)
        lane = _iota2((1, LANES), 1)
        sub = _iota2((FOX_H, 1), 0)
        scale = FOX_D ** -0.5
        for i in range(nq):
            rows, kmax = pl.ds(i * tq, tq), (i + 1) * tq
            q, cc, dout = q_ref[rows, :], ccol_ref[rows, :], do_ref[rows, :]
            dq = jnp.zeros((tq, LANES), f32)
            dcq = jnp.zeros((tq, LANES), f32)
            for hh in range(2):
                mh, qh, pe, l = _fox_scores(hh, p, i, tq, q, k[:kmax], cc, crow[:, :kmax])
                pr = pe / l
                doh = jnp.where(mh, dout, 0.0).astype(bf16)
                dp = _dot(doh, v[:kmax], NT)
                ds = pr * (dp - jnp.sum(pr * dp, axis=1, keepdims=True))
                dsb = ds.astype(bf16)
                dq = dq + _dot(dsb, jnp.where(mh, kf[:kmax], 0.0).astype(bf16)) * scale
                dk_ref[:kmax, :] += _dot(dsb, qh, TN) * scale
                dv_ref[:kmax, :] += _dot(pr.astype(bf16), doh, TN)
                h = 2 * p + hh
                dcq = dcq + jnp.where(lane == h, jnp.sum(ds, axis=1, keepdims=True), 0.0)
                dck_ref[0, 0, :, :kmax] += jnp.where(sub == h, -jnp.sum(ds, axis=0, keepdims=True), 0.0)
            dq_ref[rows, :] = dq
            dcq_ref[0, rows, :] = dcq

    seq = lambda b, p: (b, p)
    return _pcall(
        side, body, name="fox_attn_bwd", grid=(nb, FOX_H // 2),
        in_specs=[pl.BlockSpec((t, LANES), seq), pl.BlockSpec((t, LANES), seq), pl.BlockSpec((t, LANES), lambda b, p: (b, Z0_FV + p)),
                  pl.BlockSpec((t, LANES), lambda b, p: (b, 0)), pl.BlockSpec((1, FOX_H, t), lambda b, p: (b, 0, 0)),
                  pl.BlockSpec((t, LANES), seq)],
        out_specs=[pl.BlockSpec((t, LANES), seq), pl.BlockSpec((t, LANES), seq), pl.BlockSpec((t, LANES), seq),
                   pl.BlockSpec((1, t, LANES), lambda b, p: (p, b, 0)), pl.BlockSpec((1, 1, FOX_H, t), lambda b, p: (p, b, 0, 0))],
        out_shape=[S((n, FOX_W), f32), S((n, FOX_W), f32), S((n, FOX_W), f32), S((4, n, LANES), f32), S((4, nb, FOX_H, t), f32)],
        compiler_params=pltpu.CompilerParams(dimension_semantics=("parallel", "parallel")),
    )(qn, kn, z0, ccol, crow, do)


def _adamw_math(w, g, m, v):
    m = ADAM_B1 * m + (1.0 - ADAM_B1) * g
    v = ADAM_B2 * v + (1.0 - ADAM_B2) * (g * g)
    m_hat = m / (1.0 - ADAM_B1 ** ADAM_STEP)
    v_hat = v / (1.0 - ADAM_B2 ** ADAM_STEP)
    return -ADAM_LR * (m_hat / (jnp.sqrt(v_hat) + ADAM_EPS) + ADAM_WD * w), m, v


def _adamw_big(name, idx, gmine, recv1, recv2, w, m, v):
    r, wc = w.shape
    c = gmine.shape[2]
    tr = min(r, 256)

    def body(idx_ref, gm_ref, r1_ref, r2_ref, w_ref, m_ref, v_ref, g_ref, d_ref, nm_ref, nv_ref):
        g = gm_ref[0].astype(f32) + r1_ref[0].astype(f32)
        for k in range(3):
            g = g + r2_ref[k].astype(f32)
        g = g[:, :wc]
        d, nm, nv = _adamw_math(w_ref[...], g, m_ref[...], v_ref[...])
        g_ref[...] = g
        d_ref[...] = d
        nm_ref[...] = nm
        nv_ref[...] = nv

    row = pl.BlockSpec((tr, wc), lambda i, s: (i, 0))
    return pl.pallas_call(
        body, name=name,
        grid_spec=pltpu.PrefetchScalarGridSpec(
            num_scalar_prefetch=1, grid=(r // tr,),
            in_specs=[pl.BlockSpec((1, tr, c), lambda i, s: (s[0], i, 0)), pl.BlockSpec((1, tr, c), lambda i, s: (s[1], i, 0)),
                      pl.BlockSpec((3, tr, c), lambda i, s: (0, i, 0)), row, row, row],
            out_specs=[row, row, row, row]),
        out_shape=[S((r, wc), f32)] * 4,
        compiler_params=pltpu.CompilerParams(dimension_semantics=("parallel",)),
    )(idx, gmine, recv1, recv2, w, m, v)


def _pair_sum(name, idx, gmine, recv1):
    _, r, c = gmine.shape
    g4 = gmine.reshape(4, 2, r, c)

    def body(idx_ref, gm_ref, r1_ref, o_ref):
        o_ref[0] = (gm_ref[0, 0].astype(f32) + r1_ref[0].astype(f32)).astype(bf16)

    return pl.pallas_call(
        body, name=name,
        grid_spec=pltpu.PrefetchScalarGridSpec(
            num_scalar_prefetch=1, grid=(4,),
            in_specs=[pl.BlockSpec((1, 1, r, c), lambda ch, s: (ch, s[0], 0, 0)), pl.BlockSpec((1, r, c), lambda ch, s: (ch, 0, 0))],
            out_specs=pl.BlockSpec((1, r, c), lambda ch, s: (ch, 0, 0))),
        out_shape=S((4, r, c), bf16),
        compiler_params=pltpu.CompilerParams(dimension_semantics=("parallel",)),
    )(idx, g4, recv1)


def _adamw_small(parts, w, m, v, own_mask, own_g):
    def body(p_ref, w_ref, m_ref, v_ref, mask_ref, og_ref, g_ref, d_ref, nm_ref, nv_ref):
        g = p_ref[0]
        for k in range(1, NDEV):
            g = g + p_ref[k]
        g_ref[...] = g
        ge = jnp.where(mask_ref[...] > 0.5, og_ref[...], g)
        d, nm, nv = _adamw_math(w_ref[...], ge, m_ref[...], v_ref[...])
        d_ref[...] = d
        nm_ref[...] = nm
        nv_ref[...] = nv

    return pl.pallas_call(body, name="adamw_small", out_shape=[S(w.shape, f32)] * 4)(parts, w, m, v, own_mask, own_g)


def _sum_parts(parts):
    def body(p_ref, g_ref):
        g = p_ref[0]
        for k in range(1, NDEV):
            g = g + p_ref[k]
        g_ref[...] = g

    return pl.pallas_call(body, name="sum_parts", out_shape=S(parts.shape[1:], f32))(parts)


def _me():
    return lax.axis_index("x"), lax.axis_index("y"), lax.axis_index("c")


def _hbm_specs(n):
    return [pl.BlockSpec(memory_space=pl.ANY)] * n


def _allgather(name, xs):
    na = len(xs)

    def body(*refs):
        x_refs, out_refs = refs[:na], refs[na:2 * na]
        send_sems, recv_sems, local_sems = refs[2 * na:]
        mx, my, mc = _me()
        me, sib = (mx, my, mc), (mx, my, 1 - mc)
        chips = [(1 - mx, my), (mx, 1 - my), (1 - mx, 1 - my)]

        def slab(a, px, py, pc):
            return out_refs[a].at[4 * px + 2 * py + pc]

        def copy(a, k, block, to, own=False):
            return pltpu.make_async_remote_copy(
                src_ref=x_refs[a] if own else slab(a, *block), dst_ref=slab(a, *block),
                send_sem=send_sems.at[7 * a + k], recv_sem=recv_sems.at[7 * a + k], device_id=to, device_id_type=MESH)

        mine = [pltpu.make_async_copy(x_refs[a], slab(a, *me), local_sems.at[a]) for a in range(na)]
        first = []
        for a in range(na):
            mine[a].start()
            first += [copy(a, 0, me, sib, own=True)] + [copy(a, 1 + j, me, (*chip, mc), own=True) for j, chip in enumerate(chips)]
        for cp in first:
            cp.start()
        passed = []
        for j, chip in enumerate(chips):
            for a in range(na):
                copy(a, 1 + j, (*chip, mc), me).wait_recv()
                passed.append(copy(a, 4 + j, (*chip, mc), sib))
                passed[-1].start()
        for a in range(na):
            copy(a, 0, sib, me).wait_recv()
            for j, chip in enumerate(chips):
                copy(a, 4 + j, (*chip, 1 - mc), me).wait_recv()
        for cp in first + passed:
            cp.wait_send()
        for cp in mine:
            cp.wait()

    return pl.pallas_call(
        body, name=name, out_shape=[S((NDEV,) + x.shape, x.dtype) for x in xs],
        in_specs=_hbm_specs(na), out_specs=_hbm_specs(na),
        scratch_shapes=[pltpu.SemaphoreType.DMA((7 * na,)), pltpu.SemaphoreType.DMA((7 * na,)), pltpu.SemaphoreType.DMA((na,))],
    )(*xs)


def _rs_sibling(gs):
    na = len(gs)

    def body(*refs):
        g_refs, out_refs, send_sems, recv_sems = refs[:na], refs[na:2 * na], refs[2 * na], refs[2 * na + 1]
        mx, my, mc = _me()
        cps = [pltpu.make_async_remote_copy(
            src_ref=g_refs[a].at[2 * ch + 1 - mc], dst_ref=out_refs[a].at[ch], send_sem=send_sems.at[4 * a + ch],
            recv_sem=recv_sems.at[4 * a + ch], device_id=(mx, my, 1 - mc), device_id_type=MESH)
            for a in range(na) for ch in range(4)]
        for cp in cps:
            cp.start()
        for cp in cps:
            cp.wait_recv()
        for cp in cps:
            cp.wait_send()

    return pl.pallas_call(
        body, name="rs_sibling", out_shape=[S((4,) + g.shape[1:], g.dtype) for g in gs],
        in_specs=_hbm_specs(na), out_specs=_hbm_specs(na),
        scratch_shapes=[pltpu.SemaphoreType.DMA((4 * na,)), pltpu.SemaphoreType.DMA((4 * na,))],
    )(*gs)


def _side_allgather(xs):
    na = len(xs)

    def mk(x_refs, out_refs, sems):
        send_sems, recv_sems, local_sems = sems
        mx, my, mc = _me()
        me, sib = (mx, my, mc), (mx, my, 1 - mc)
        chips = [(1 - mx, my), (mx, 1 - my), (1 - mx, 1 - my)]

        def slab(a, px, py, pc):
            return out_refs[a].at[4 * px + 2 * py + pc]

        def copy(a, k, block, to, own=False):
            return pltpu.make_async_remote_copy(
                src_ref=x_refs[a] if own else slab(a, *block), dst_ref=slab(a, *block),
                send_sem=send_sems.at[7 * a + k], recv_sem=recv_sems.at[7 * a + k], device_id=to, device_id_type=MESH)

        mine = [pltpu.make_async_copy(x_refs[a], slab(a, *me), local_sems.at[a]) for a in range(na)]
        first = []
        for a in range(na):
            first += [copy(a, 0, me, sib, own=True)] + [copy(a, 1 + j, me, (*chip, mc), own=True) for j, chip in enumerate(chips)]
        return me, sib, chips, mc, copy, mine, first

    def start(x_refs, out_refs, sems):
        *_, mine, first = mk(x_refs, out_refs, sems)
        for cp in mine + first:
            cp.start()

    def finish(x_refs, out_refs, sems):
        me, sib, chips, mc, copy, mine, first = mk(x_refs, out_refs, sems)
        passed = []
        for j, chip in enumerate(chips):
            for a in range(na):
                copy(a, 1 + j, (*chip, mc), me).wait_recv()
                passed.append(copy(a, 4 + j, (*chip, mc), sib))
                passed[-1].start()
        for a in range(na):
            copy(a, 0, sib, me).wait_recv()
            for j, chip in enumerate(chips):
                copy(a, 4 + j, (*chip, 1 - mc), me).wait_recv()
        for cp in first + passed:
            cp.wait_send()
        for cp in mine:
            cp.wait()

    scratch = [pltpu.SemaphoreType.DMA((7 * na,)), pltpu.SemaphoreType.DMA((7 * na,)), pltpu.SemaphoreType.DMA((na,))]
    return _Side(xs, [S((NDEV,) + x.shape, x.dtype) for x in xs], scratch, start, finish)


def _side_exchange(arrs, nslot, out_slots, route):
    na = len(arrs)

    def copies(in_refs, out_refs, sems):
        send_sems, recv_sems = sems
        return [pltpu.make_async_remote_copy(
            src_ref=in_refs[a].at[src], dst_ref=out_refs[a].at[k], send_sem=send_sems.at[nslot * a + k],
            recv_sem=recv_sems.at[nslot * a + k], device_id=to, device_id_type=MESH)
            for a in range(na) for k, (src, to) in enumerate(route(*_me()))]

    def start(in_refs, out_refs, sems):
        for cp in copies(in_refs, out_refs, sems):
            cp.start()

    def finish(in_refs, out_refs, sems):
        cps = copies(in_refs, out_refs, sems)
        for cp in cps:
            cp.wait_recv()
        for cp in cps:
            cp.wait_send()

    scratch = [pltpu.SemaphoreType.DMA((nslot * na,)), pltpu.SemaphoreType.DMA((nslot * na,))]
    return _Side(arrs, [S((out_slots,) + x.shape[1:], x.dtype) for x in arrs], scratch, start, finish)


def _side_rs_sibling(gs):
    return _side_exchange(gs, 4, 4, lambda mx, my, mc: [(2 * ch + 1 - mc, (mx, my, 1 - mc)) for ch in range(4)])


def _side_rs_chips(ps):
    return _side_exchange(ps, 3, 3, lambda mx, my, mc: [(2 * cx + cy, (cx, cy, mc)) for cx, cy in
                                                        [(1 - mx, my), (mx, 1 - my), (1 - mx, 1 - my)]])


_COLW = 512
_COL_NAMES = ("l0_w_in", "l0_w_ff1", "l1_w_in", "l1_w_ff1")
_ROW_NAMES = ("l0_w_out", "l0_w_ff2", "l1_w_out", "l1_w_ff2")
_BIG_NAMES = _COL_NAMES + _ROW_NAMES


def _full_weight(gathered, name, ncols):
    if name in _ROW_NAMES:
        return gathered.reshape(-1, D)
    return gathered[:, :, :ncols].transpose(1, 0, 2).reshape(D, NDEV * ncols)


def _regroup_w_in0(w):
    main = jnp.concatenate([w[:, 0:1536], w[:, 1544:3080], w[:, 3088:3600]], axis=1)
    small = jnp.concatenate([w[:, 1536:1544], w[:, 3080:3088]], axis=1)
    return jnp.concatenate([main, small, jnp.zeros((D, ZW0 - 3584 - 16), w.dtype)], axis=1)


def _ungroup_w_in0(g):
    return jnp.concatenate([g[:, 0:1536], g[:, 3584:3592], g[:, 1536:3072], g[:, 3592:3600], g[:, 3072:3584]], axis=1)


def _col_slabs(g, ncols):
    g = g.reshape(D, NDEV, ncols).transpose(1, 0, 2)
    return jnp.pad(g, ((0, 0), (0, 0), (0, _COLW - ncols)))


def _sq(t):
    return t * t


def _epi_res_norm(acc, res, gain):
    y = acc + res
    h = _f_norm(y, gain)[0]
    return y, h, h


_RES_NORM_OUTS = [("tile", f32), ("tile", bf16), ("tile_t", bf16)]


def _epi_norm_bwd(acc, x, dres, gain):
    _, vjp = jax.vjp(lambda xx, gg: _f_norm(xx, gg)[0], x, gain)
    dx, dgain = vjp(acc)
    dx = dx + dres
    return dx, dgain, dx


_NORM_BWD_OUTS = [("tile", f32), ("rows", f32), ("tile_t", bf16)]


def _epi_loss(acc, res, tgt):
    e = acc + res - tgt
    dy = e * (1.0 / D)
    return dy, dy, jnp.sum(e * e, axis=0, keepdims=True)


_LOSS_OUTS = [("tile", f32), ("tile_t", bf16), ("rows", f32)]


def _mlp_fwd(tag, x, h, w1, w2, epi, extra, consts, outs):
    a = _mm(f"{tag}_ff1", h, w1, NN, bf16, epi=lambda acc: jnp.maximum(acc, 0.0))
    return _mm(f"{tag}_ff2", a, w2, NN, f32, a_fn=_sq, epi=epi, extra=(x,) + tuple(extra), consts=consts, outs=outs), a


def _mlp_bwd(tag, x, gain, w1, w2, ht, a, dy, dyt, side=None):
    da = _mm(f"{tag}_ff2_dx", dy, w2, NT, bf16, epi=lambda acc, av: acc * 2.0 * av.astype(f32), extra=(a,), side=side)
    if side is not None:
        da, side_res = da
    dw2 = _mm(f"{tag}_ff2_dw", dyt, a, NN, bf16, b_fn=_sq, out_t=True)
    res = _mm(f"{tag}_ff1_dx", da, w1, NT, f32, epi=_epi_norm_bwd, extra=(x, dy), consts=(gain,), outs=_NORM_BWD_OUTS)
    dw1 = _mm(f"{tag}_ff1_dw", ht, da, NN, bf16, slab=True)
    return (res, dw1, dw2) if side is None else ((res, dw1, dw2), side_res)


def _row(v):
    return v.reshape(1, -1).astype(f32)


_L0_REST = ("l0_w_ff1", "l0_w_out", "l0_w_ff2")
_L1_MIX = ("l1_w_in", "l1_w_out")
_L1_FFN = ("l1_w_ff1", "l1_w_ff2")
_GRAD_A = ("l1_w_ff1", "l1_w_ff2", "l1_w_out", "l1_w_in")
_GRAD_B = ("l0_w_ff1", "l0_w_ff2", "l0_w_out")


def _train(x, tgt, args, mom, var, t):
    n = x.shape[0]
    nchunk = t // CH
    mx, my, mc = _me()
    dev, chip = 4 * mx + 2 * my + mc, 2 * mx + my
    core_idx = jnp.reshape(mc, (1,)).astype(jnp.int32)
    own_idx = jnp.stack([dev, chip]).astype(jnp.int32)
    ncols = {nm: args[nm].shape[1] for nm in _COL_NAMES}
    conv_cols = args["l0_gdn_conv"].shape[1]
    g, big, w = {}, {}, {}

    def send(nm):
        a = args[nm].astype(bf16)
        return jnp.pad(a, ((0, 0), (0, _COLW - ncols[nm]))) if nm in _COL_NAMES else a

    def take(names, gathered):
        for nm, arr in zip(names, gathered):
            w[nm] = _full_weight(arr, nm, ncols.get(nm, 0))

    def by_dev(nm, ga):
        return ga if nm in _COL_NAMES else ga.reshape(NDEV, -1, D)

    def pair(names, gs, r1s):
        return [_pair_sum(f"rs_pair_sum_{nm}", core_idx, ga, r1) for nm, ga, r1 in zip(names, gs, r1s)]

    def adam(names, gs, r1s, r2s):
        for nm, ga, r1, r2 in zip(names, gs, r1s, r2s):
            big[nm] = _adamw_big(f"adamw_{nm}", own_idx, ga, r1, r2, args[nm], mom[nm], var[nm])


    li = jnp.arange(FOX_W)
    pm = jnp.where((li[:, None] // FOX_D) == (li[None, :] // FOX_D), 1.0 / FOX_D, 0.0).astype(f32)
    lane_head = jnp.arange(GDN_W) // HD
    sel = lambda first_lane: (jnp.arange(LANES)[:, None] == (first_lane + lane_head)[None, :]).astype(f32)
    e_beta, e_alpha = sel(FOX_H), sel(FOX_H + GDN_H)
    alog_b, dt_b = _row(jnp.repeat(args["l0_gdn_A_log"], HD)), _row(jnp.repeat(args["l0_gdn_dt_bias"], HD))
    gq_t, gk_t = _row(jnp.tile(args["l0_fox_q_norm"], FOX_H)), _row(jnp.tile(args["l0_fox_k_norm"], FOX_H))
    on0_t, on1_t = _row(jnp.tile(args["l0_gdn_o_norm"], GDN_H)), _row(jnp.tile(args["l1_hgrn_o_norm"], HG_H))
    fbias = jnp.pad(_row(args["l0_fox_f_bias"]), ((0, 0), (0, LANES - FOX_H)))
    g0m, g0f, g1m, g1f = (_row(args[k]) for k in ("l0_mix_norm", "l0_ffn_norm", "l1_mix_norm", "l1_ffn_norm"))
    lbl = args["hgrn_lb_logits"].astype(f32)

    first = [send("l0_w_in"), jnp.pad(args["l0_gdn_conv"], ((0, 4), (0, LANES * 2 - conv_cols)))]
    (h0, h0t), first = _tok_fwd("l0_mix_norm", _f_norm, [(x, D, 0)], [(g0m, None, 0)], [(D, bf16)], 256, also_t=(0,),
                                side=_side_allgather(first))
    take(("l0_w_in",), first[:1])
    w_in0 = _regroup_w_in0(w["l0_w_in"])
    wconv = first[1][:, :4, :conv_cols].transpose(1, 0, 2).reshape(4, NDEV * conv_cols)
    z0 = _mm("l0_in", h0, w_in0, NN, f32)
    qk_rows = [(z0, 2 * FOX_W, 0)]
    qk_consts = [(gq_t, None, 0), (gk_t, None, 0), (pm, None, 0)]
    qn, kn = _tok_fwd("fox_pre", _f_foxpre, qk_rows, qk_consts, [(FOX_W, f32)] * 2, 256)
    ccol, crow = _fox_gate_fwd(z0, fbias, t)
    fox_o, got = _fox_attn_fwd(qn, kn, z0, ccol, crow, t, side=_side_allgather([send(nm) for nm in _L0_REST]))
    take(_L0_REST, got)
    conv_rows, conv_consts = [(z0, LANES, Z0_GQKV)], [(wconv, LANES, 0)]
    (qkv,) = _tok_fwd("gdn_conv", _f_conv, conv_rows, conv_consts, [(LANES, f32)], t, ncb=12, with_j=True)
    gate_rows = [(z0, LANES, Z0_SMALL)]
    gate_consts = [(e_beta, None, 0), (e_alpha, None, 0), (alog_b, None, 0), (dt_b, None, 0)]
    beta_b, g_b = _tok_fwd("gdn_gate", _f_gdngate, gate_rows, gate_consts, [(GDN_W, f32)] * 2, 256)
    intra_rows = [(qkv, GDN_W, 0), (qkv, GDN_W, 1), (qkv, GDN_W, 2), (beta_b, GDN_W, 0), (g_b, GDN_W, 0)]
    intra, got = _tok_fwd("gdn_intra", _f_gdn_intra, intra_rows, [], [(GDN_W, f32)] * 7, 2 * CH,
                          side=_side_allgather([send(nm) for nm in _L1_MIX]))
    take(_L1_MIX, got)
    inter_rows = [(a, 0, 1) for a in intra[:6]]
    full_rows = inter_rows + [(fox_o, 0, 1), (z0, Z0_GG // 4, 1)]
    (cat0, gdn_hist), got = _scan_fwd("gdn_scan", _f_gdn_full, full_rows, GDN_H, nchunk, consts=(on0_t,), out_dtype=bf16, out_k=2,
                                      side=_side_allgather([send(nm) for nm in _L1_FFN[1:]]))
    take(_L1_FFN[1:], got)
    x1, hf0, hf0t = _mm("l0_out", cat0, w["l0_w_out"], NN, f32, epi=_epi_res_norm, extra=(x,), consts=(g0f,), outs=_RES_NORM_OUTS)
    (x2, h1, h1t), a0 = _mlp_fwd("l0", x1, hf0, w["l0_w_ff1"], w["l0_w_ff2"], _epi_res_norm, (), (g1m,), _RES_NORM_OUTS)

    z1 = _mm("l1_in", h1, w["l1_w_in"], NN, f32)
    (cat1, hg_hist), got = _scan_fwd("hgrn_scan", _f_hgrn_full, [(z1, 0, 4)], HG_H, nchunk, consts=(lbl, on1_t), out_dtype=bf16,
                                     side=_side_allgather([send(nm) for nm in _L1_FFN[:1]]))
    take(_L1_FFN[:1], got)
    x3, hf1, hf1t = _mm("l1_out", cat1, w["l1_w_out"], NN, f32, epi=_epi_res_norm, extra=(x2,), consts=(g1f,), outs=_RES_NORM_OUTS)
    (dy, dyt, loss_row), a1 = _mlp_fwd("l1", x3, hf1, w["l1_w_ff1"], w["l1_w_ff2"], _epi_loss, (tgt,), (), _LOSS_OUTS)

    (dx3, g["l1_ffn_norm"], dx3t), ga_ff1, ga_ff2 = _mlp_bwd("l1", x3, g1f, w["l1_w_ff1"], w["l1_w_ff2"], hf1t, a1, dy, dyt)
    dcat1 = _mm("l1_out_dx", dx3, w["l1_w_out"], NT, f32)
    ga_out = _mm("l1_out_dw", dx3t, cat1, NN, bf16, out_t=True)
    dz1, dlbl, don1 = _scan_bwd("hgrn_scan_bwd", _f_hgrn_full, [(z1, 0, 4)], hg_hist, dcat1, HG_H, nchunk, dtypes=[bf16],
                                consts=(lbl, on1_t))
    dx2, g["l1_mix_norm"], dx2t = _mm("l1_in_dx", dz1, w["l1_w_in"], NT, f32, epi=_epi_norm_bwd, extra=(x2, dx3), consts=(g1m,),
                                      outs=_NORM_BWD_OUTS)
    ga_in = _mm("l1_in_dw", h1t, dz1, NN, bf16, slab=True)
    g["l1_hgrn_o_norm"] = don1.reshape(HG_H, HD).sum(0)
    g["hgrn_lb_logits"] = dlbl
    gs_a = [by_dev(nm, ga) for nm, ga in zip(_GRAD_A, (ga_ff1, ga_ff2, ga_out, ga_in))]

    ((dx1, g["l0_ffn_norm"], dx1t), gb_ff1, gb_ff2), r1_a = _mlp_bwd("l0", x1, g0f, w["l0_w_ff1"], w["l0_w_ff2"], hf0t, a0, dx2, dx2t,
                                                                  side=_side_rs_sibling(gs_a))
    pairs_a = pair(_GRAD_A, gs_a, r1_a)
    dcat0 = _mm("l0_out_dx", dx1, w["l0_w_out"], NT, f32)
    gb_out = _mm("l0_out_dw", dx1t, cat0, NN, bf16, out_t=True)
    gs_b = [by_dev(nm, ga) for nm, ga in zip(_GRAD_B, (gb_ff1, gb_ff2, gb_out))]
    *dinter, dfox_o, dgg, don0 = _scan_bwd("gdn_scan_bwd", _f_gdn_full, full_rows, gdn_hist, dcat0, GDN_H, nchunk, consts=(on0_t,),
                                           out_k=2, dtypes=[f32] * 7 + [bf16])
    (dqn, dkn, dfv, dcq, dck), got = _fox_attn_bwd(qn, kn, z0, ccol, crow, dfox_o, t,
                                                   side=_join_sides([_side_rs_chips(pairs_a), _side_rs_sibling(gs_b)]))
    r2_a, r1_b = got[:len(_GRAD_A)], got[len(_GRAD_A):]
    adam(_GRAD_A, gs_a, r1_a, r2_a)
    pairs_b = pair(_GRAD_B, gs_b, r1_b)
    (dqkv, dbeta_b, dg_b), r2_b = _tok_bwd("gdn_intra_bwd", _f_gdn_intra, intra_rows + [(intra[6], GDN_W, 0)], [],
                                           [(a, GDN_W) for a in dinter], 2 * CH, [0, 1, 2, 3, 4], [], ncat=3,
                                           side=_side_rs_chips(pairs_b))
    adam(_GRAD_B, gs_b, r1_b, r2_b)
    dzs_g, dalog_b, ddt_b = _tok_bwd("gdn_gate_bwd", _f_gdngate, gate_rows, gate_consts, [(dbeta_b, GDN_W), (dg_b, GDN_W)], 256, [0], [2, 3])
    dgqkv, dwconv = _tok_bwd("gdn_conv_bwd", _f_conv, conv_rows, conv_consts, [(dqkv, LANES)], t, [0], [0], ncb=12, with_j=True,
                             drow_dtype=bf16)
    dzs_f, dfb = _fox_gate_bwd(z0, fbias, dcq, dck, t)
    dzqk, dgq_t, dgk_t = _tok_bwd("fox_pre_bwd", _f_foxpre, qk_rows, qk_consts, [(dqn, FOX_W), (dkn, FOX_W)], 256, [0], [0, 1],
                                  drow_dtype=bf16)
    dz0 = jnp.concatenate([dzqk, dfv.astype(bf16), dgqkv, dgg, (dzs_g + dzs_f).astype(bf16), jnp.zeros((n, ZW0 - 3712), bf16)], axis=1)
    gs_c = [_col_slabs(_ungroup_w_in0(_mm("l0_in_dw", h0t, dz0, NN, bf16)), ncols["l0_w_in"])]
    r1_c = _rs_sibling(gs_c)
    pairs_c = pair(("l0_w_in",), gs_c, r1_c)
    (dx, g["l0_mix_norm"], _), r2_c = _mm("l0_in_dx", dz0, w_in0, NT, f32, epi=_epi_norm_bwd, extra=(x, dx1), consts=(g0m,),
                                          outs=_NORM_BWD_OUTS, side=_side_rs_chips(pairs_c))
    adam(("l0_w_in",), gs_c, r1_c, r2_c)
    g["l0_fox_q_norm"] = dgq_t.reshape(FOX_H, FOX_D).sum(0)
    g["l0_fox_k_norm"] = dgk_t.reshape(FOX_H, FOX_D).sum(0)
    g["l0_fox_f_bias"] = dfb[0, :FOX_H]
    g["l0_gdn_conv"] = dwconv
    g["l0_gdn_A_log"] = dalog_b.reshape(GDN_H, HD).sum(1)
    g["l0_gdn_dt_bias"] = ddt_b.reshape(GDN_H, HD).sum(1)
    g["l0_gdn_o_norm"] = don0.reshape(GDN_H, HD).sum(0)
    return loss_row, dx, g, big


_NAMES = ("l0_mix_norm", "l0_w_in", "l0_fox_q_norm", "l0_fox_k_norm", "l0_fox_f_bias", "l0_gdn_conv", "l0_gdn_A_log",
          "l0_gdn_dt_bias", "l0_gdn_o_norm", "l0_w_out", "l0_ffn_norm", "l0_w_ff1", "l0_w_ff2", "l1_mix_norm", "l1_w_in",
          "l1_hgrn_o_norm", "l1_w_out", "l1_ffn_norm", "l1_w_ff1", "l1_w_ff2", "hgrn_lb_logits")
_SMALL_NAMES = tuple(nm for nm in _NAMES if nm not in _BIG_NAMES)
_SMALL_ROWS = 16


def _pack_small(vals):
    flat = jnp.concatenate([vals[nm].reshape(-1).astype(f32) for nm in _SMALL_NAMES])
    return jnp.pad(flat, (0, _SMALL_ROWS * D - flat.shape[0])).reshape(_SMALL_ROWS, D)


def _unpack_small(packed, shapes):
    flat = packed.reshape(-1)
    out, off = {}, 0
    for nm in _SMALL_NAMES:
        size = 1
        for s in shapes[nm]:
            size *= s
        out[nm] = flat[off:off + size].reshape(shapes[nm])
        off += size
    return out, off


def kernel(x, l0_mix_norm, l0_w_in, l0_fox_q_norm, l0_fox_k_norm, l0_fox_f_bias, l0_gdn_conv, l0_gdn_A_log, l0_gdn_dt_bias, l0_gdn_o_norm, l0_w_out, l0_ffn_norm, l0_w_ff1, l0_w_ff2, l1_mix_norm, l1_w_in, l1_hgrn_o_norm, l1_w_out, l1_ffn_norm, l1_w_ff1, l1_w_ff2, hgrn_lb_logits, loss_target, m_l0_mix_norm, m_l0_w_in, m_l0_fox_q_norm, m_l0_fox_k_norm, m_l0_fox_f_bias, m_l0_gdn_conv, m_l0_gdn_A_log, m_l0_gdn_dt_bias, m_l0_gdn_o_norm, m_l0_w_out, m_l0_ffn_norm, m_l0_w_ff1, m_l0_w_ff2, m_l1_mix_norm, m_l1_w_in, m_l1_hgrn_o_norm, m_l1_w_out, m_l1_ffn_norm, m_l1_w_ff1, m_l1_w_ff2, m_hgrn_lb_logits, v_l0_mix_norm, v_l0_w_in, v_l0_fox_q_norm, v_l0_fox_k_norm, v_l0_fox_f_bias, v_l0_gdn_conv, v_l0_gdn_A_log, v_l0_gdn_dt_bias, v_l0_gdn_o_norm, v_l0_w_out, v_l0_ffn_norm, v_l0_w_ff1, v_l0_w_ff2, v_l1_mix_norm, v_l1_w_in, v_l1_hgrn_o_norm, v_l1_w_out, v_l1_ffn_norm, v_l1_w_ff1, v_l1_w_ff2, v_hgrn_lb_logits):
    args = dict(zip(_NAMES, (l0_mix_norm, l0_w_in, l0_fox_q_norm, l0_fox_k_norm, l0_fox_f_bias, l0_gdn_conv, l0_gdn_A_log, l0_gdn_dt_bias, l0_gdn_o_norm, l0_w_out, l0_ffn_norm, l0_w_ff1, l0_w_ff2, l1_mix_norm, l1_w_in, l1_hgrn_o_norm, l1_w_out, l1_ffn_norm, l1_w_ff1, l1_w_ff2, hgrn_lb_logits)))
    mom = dict(zip(_NAMES, (m_l0_mix_norm, m_l0_w_in, m_l0_fox_q_norm, m_l0_fox_k_norm, m_l0_fox_f_bias, m_l0_gdn_conv, m_l0_gdn_A_log, m_l0_gdn_dt_bias, m_l0_gdn_o_norm, m_l0_w_out, m_l0_ffn_norm, m_l0_w_ff1, m_l0_w_ff2, m_l1_mix_norm, m_l1_w_in, m_l1_hgrn_o_norm, m_l1_w_out, m_l1_ffn_norm, m_l1_w_ff1, m_l1_w_ff2, m_hgrn_lb_logits)))
    var = dict(zip(_NAMES, (v_l0_mix_norm, v_l0_w_in, v_l0_fox_q_norm, v_l0_fox_k_norm, v_l0_fox_f_bias, v_l0_gdn_conv, v_l0_gdn_A_log, v_l0_gdn_dt_bias, v_l0_gdn_o_norm, v_l0_w_out, v_l0_ffn_norm, v_l0_w_ff1, v_l0_w_ff2, v_l1_mix_norm, v_l1_w_in, v_l1_hgrn_o_norm, v_l1_w_out, v_l1_ffn_norm, v_l1_w_ff1, v_l1_w_ff2, v_hgrn_lb_logits)))
    nb, t, _ = x.shape
    dev = 4 * lax.axis_index("x") + 2 * lax.axis_index("y") + lax.axis_index("c")
    conv_cols = l0_gdn_conv.shape[1]
    loss_row, dx, g, big = _train(x.reshape(nb * t, D), loss_target.reshape(nb * t, D), args, mom, var, t)

    shapes = {nm: args[nm].shape for nm in _SMALL_NAMES}
    gsm = dict(g)
    gsm["l0_gdn_conv"] = jnp.zeros(shapes["l0_gdn_conv"], f32)
    packed = _pack_small(gsm)
    _, used = _unpack_small(packed, shapes)
    flat_extra = jnp.concatenate([jnp.sum(loss_row).reshape(1), g["l0_gdn_conv"].reshape(-1)])
    packed = packed.reshape(-1).at[used:used + flat_extra.shape[0]].set(flat_extra).reshape(_SMALL_ROWS, D)
    (parts,) = _allgather("ag_small", [packed])
    total = _sum_parts(parts).reshape(-1)
    loss = 0.5 * total[used] / D
    conv_g_full = total[used + 1:used + 1 + 4 * NDEV * conv_cols].reshape(4, NDEV * conv_cols)
    conv_g = lax.dynamic_slice(conv_g_full, (0, dev * conv_cols), (4, conv_cols))
    own_vals = {nm: jnp.zeros(shapes[nm], f32) for nm in _SMALL_NAMES}
    own_vals["l0_gdn_conv"] = conv_g
    own_mask = {nm: jnp.zeros(shapes[nm], f32) for nm in _SMALL_NAMES}
    own_mask["l0_gdn_conv"] = jnp.ones(shapes["l0_gdn_conv"], f32)
    small = _adamw_small(parts, _pack_small(args), _pack_small(mom), _pack_small(var), _pack_small(own_mask), _pack_small(own_vals))
    small = [_unpack_small(a, shapes)[0] for a in small]
    small[0]["l0_gdn_conv"] = conv_g

    outs = [loss, dx.reshape(nb, t, D)]
    for k in range(4):
        outs += [big[nm][k] if nm in _BIG_NAMES else small[k][nm] for nm in _NAMES]
    return tuple(outs)
```

```python
import functools

import jax
import jax.numpy as jnp
from jax import lax
from jax.experimental import pallas as pl
from jax.experimental.pallas import tpu as pltpu

f32, bf16 = jnp.float32, jnp.bfloat16
NN = (((1,), (0,)), ((), ()))
NT = (((1,), (1,)), ((), ()))
TN = (((0,), (0,)), ((), ()))
HI = lax.Precision.HIGHEST
MESH = pl.DeviceIdType.MESH
S = jax.ShapeDtypeStruct

EPS = 1e-6
D = 1024
LANES = 128
FOX_H, FOX_D, FOX_W = 8, 64, 512
GDN_H, HD, GDN_W = 4, 128, 512
HG_H = 8
CH = 64
ZW0 = 3840
NDEV = 8
ADAM_LR, ADAM_B1, ADAM_B2, ADAM_EPS, ADAM_WD, ADAM_STEP = 0.001, 0.9, 0.999, 1e-08, 0.01, 10

Z0_FQK, Z0_FV, Z0_GQKV, Z0_GG, Z0_SMALL = 0, 8, 12, 24, 28


def _dot(a, b, dims=NN, prec=None):
    return lax.dot_general(a, b, dims, precision=prec, preferred_element_type=f32)


def _iota2(shape, axis):
    return lax.broadcasted_iota(jnp.int32, shape, axis)


def _split3(x):
    x1 = x.astype(bf16)
    r = x - x1.astype(f32)
    x2 = r.astype(bf16)
    return x1, x2, (r - x2.astype(f32)).astype(bf16)


def _dot_sel(a, b, dims=NN, exact_lhs=False):
    if exact_lhs:
        return sum(_dot(a.astype(bf16), piece, dims) for piece in _split3(b))
    return sum(_dot(piece, b.astype(bf16), dims) for piece in _split3(a))


@jax.custom_vjp
def _sel_rhs(a, b):
    return _dot_sel(a, b)


_sel_rhs.defvjp(lambda a, b: (_dot_sel(a, b), b), lambda b, g: (_dot_sel(g, b, NT), jnp.zeros_like(b)))


@jax.custom_vjp
def _sel_lhs(a, x):
    return _dot_sel(a, x, exact_lhs=True)


_sel_lhs.defvjp(lambda a, x: (_dot_sel(a, x, exact_lhs=True), a), lambda a, g: (jnp.zeros_like(a), _dot_sel(a, g, TN, exact_lhs=True)))


class _Side:
    def __init__(self, ins, out_shapes, scratch, start, finish):
        self.ins, self.out_shapes, self.scratch, self.start, self.finish = list(ins), list(out_shapes), list(scratch), start, finish


def _join_sides(sides):
    def split(refs, counts):
        out, off = [], 0
        for c in counts:
            out.append(refs[off:off + c])
            off += c
        return out

    ni, no, ns = ([len(getattr(sd, a)) for sd in sides] for a in ("ins", "out_shapes", "scratch"))

    def run(which):
        def go(ins, outs, sems):
            for sd, i, o, c in zip(sides, split(ins, ni), split(outs, no), split(sems, ns)):
                getattr(sd, which)(i, o, c)
        return go

    return _Side(sum((sd.ins for sd in sides), []), sum((sd.out_shapes for sd in sides), []),
                 sum((sd.scratch for sd in sides), []), run("start"), run("finish"))


def _pcall(side, body, *, name, grid, in_specs, out_specs, out_shape, scratch_shapes=(), compiler_params=None):
    if side is None:
        return pl.pallas_call(body, name=name, grid=grid, in_specs=in_specs, out_specs=out_specs, out_shape=out_shape,
                              scratch_shapes=scratch_shapes, compiler_params=compiler_params)
    single = not isinstance(out_shape, (list, tuple))
    ospecs, oshape = ([out_specs], [out_shape]) if single else (list(out_specs), list(out_shape))
    nin, nout, nscr = len(in_specs), len(ospecs), len(scratch_shapes)
    si, so = len(side.ins), len(side.out_shapes)

    def wrapped(*refs):
        o0 = nin + si
        c0 = o0 + nout + so
        sins, souts, ssems = refs[nin:o0], refs[o0 + nout:c0], refs[c0 + nscr:]
        ids = [pl.program_id(a) for a in range(len(grid))]
        first = functools.reduce(jnp.logical_and, [i == 0 for i in ids])
        last = functools.reduce(jnp.logical_and, [i == g - 1 for i, g in zip(ids, grid)])

        @pl.when(first)
        def _():
            side.start(sins, souts, ssems)

        body(*refs[:nin], *refs[o0:o0 + nout], *refs[c0:c0 + nscr])

        @pl.when(last)
        def _():
            side.finish(sins, souts, ssems)

    call = pl.pallas_call(
        wrapped, name=name, grid=grid, in_specs=list(in_specs) + _hbm_specs(si), out_specs=ospecs + _hbm_specs(so),
        out_shape=oshape + side.out_shapes, scratch_shapes=list(scratch_shapes) + side.scratch,
        compiler_params=pltpu.CompilerParams(dimension_semantics=("arbitrary",) * len(grid),
                                             vmem_limit_bytes=getattr(compiler_params, "vmem_limit_bytes", None)))

    def run(*args):
        res = call(*args, *side.ins)
        return (res[0] if single else list(res[:nout])), list(res[nout:])

    return run


def _tok_specs(rows, consts, tm):
    specs = []
    for (_, w, base) in rows:
        specs.append(pl.BlockSpec((tm, w), functools.partial(lambda j, i, b: (i, b + j), b=base)))
    for (arr, w, base) in consts:
        if w is None:
            specs.append(pl.BlockSpec(arr.shape, lambda j, i: (0, 0)))
        else:
            specs.append(pl.BlockSpec((arr.shape[0], w), functools.partial(lambda j, i, b: (0, b + j), b=base)))
    return specs


def _tok_fwd(name, f, rows, consts, outs, tm, ncb=1, with_j=False, also_t=(), side=None):
    n = rows[0][0].shape[0]
    nin = len(rows) + len(consts)
    nout = len(outs)

    def body(*refs):
        ins = [r[...] for r in refs[:nin]]
        vals = f(pl.program_id(0), *ins) if with_j else f(*ins)
        for r, v in zip(refs[nin:nin + nout], vals):
            r[...] = v.astype(r.dtype)
        for r, k in zip(refs[nin + nout:], also_t):
            r[...] = vals[k].T.astype(r.dtype)

    return _pcall(
        side, body, name=name, grid=(ncb, n // tm),
        in_specs=_tok_specs(rows, consts, tm),
        out_specs=[pl.BlockSpec((tm, w), lambda j, i: (i, j)) for (w, _) in outs]
        + [pl.BlockSpec((outs[k][0], tm), lambda j, i: (0, i)) for k in also_t],
        out_shape=[S((n, w * ncb), dt) for (w, dt) in outs] + [S((outs[k][0], n), bf16) for k in also_t],
        compiler_params=pltpu.CompilerParams(dimension_semantics=("parallel", "parallel")),
    )(*[r[0] for r in rows], *[c[0] for c in consts])


def _tok_bwd(name, f, rows, consts, cots, tm, drow, dconst, ncb=1, with_j=False, addto=None, also_t=(), drow_dtype=f32, side=None, ncat=0):
    n = rows[0][0].shape[0]
    nr, nc, nct = len(rows), len(consts), len(cots)
    addto = addto or {}
    add_keys = sorted(addto)
    nadd = len(add_keys)

    def body(*refs):
        ins = [r[...] for r in refs[:nr + nc]]
        cot = [r[...] for r in refs[nr + nc:nr + nc + nct]]
        adds = refs[nr + nc + nct:nr + nc + nct + nadd]
        outs = refs[nr + nc + nct + nadd:]
        pos = list(drow) + [nr + k for k in dconst]

        def g(*dargs):
            full = list(ins)
            for p, a in zip(pos, dargs):
                full[p] = a
            return tuple(f(pl.program_id(0), *full) if with_j else f(*full))

        vals, vjp = jax.vjp(g, *[ins[p] for p in pos])
        grads = vjp(tuple(c.astype(v.dtype) for c, v in zip(cot, vals)))
        off = 0
        for k in range(len(drow)):
            gk = grads[k]
            if k in addto:
                gk = gk + adds[add_keys.index(k)][...]
            if k < ncat:
                outs[0][:, off:off + gk.shape[1]] = gk.astype(outs[0].dtype)
                off += gk.shape[1]
            else:
                outs[k - skip][...] = gk.astype(outs[k - skip].dtype)
            if k in also_t:
                tref = outs[len(drow) - skip + len(dconst) + list(also_t).index(k)]
                tref[...] = gk.T.astype(tref.dtype)
        first = pl.program_id(1) == 0
        for k in range(len(dconst)):
            ref = outs[len(drow) - skip + k]

            @pl.when(first)
            def _():
                ref[...] = jnp.zeros_like(ref)

            ref[...] += grads[len(drow) + k]

    skip = max(ncat - 1, 0)
    in_specs = _tok_specs(rows, consts, tm)
    in_specs += [pl.BlockSpec((tm, w), lambda j, i: (i, j)) for (_, w) in cots]
    in_specs += [pl.BlockSpec((tm, rows[drow[k]][1]), lambda j, i: (i, j)) for k in add_keys]
    dts = drow_dtype if isinstance(drow_dtype, (list, tuple)) else [drow_dtype] * len(drow)
    widths = [rows[k][1] for k in drow]
    if ncat:
        widths, dts = [sum(widths[:ncat])] + widths[ncat:], [dts[0]] + list(dts[ncat:])
    out_specs = [pl.BlockSpec((tm, wd), lambda j, i: (i, j)) for wd in widths]
    out_shape = [S((n, wd * ncb), dt) for wd, dt in zip(widths, dts)]
    for k in dconst:
        arr, w, _ = consts[k]
        if w is None:
            out_specs.append(pl.BlockSpec(arr.shape, lambda j, i: (0, 0)))
            out_shape.append(S(arr.shape, f32))
        else:
            out_specs.append(pl.BlockSpec((arr.shape[0], w), lambda j, i: (0, j)))
            out_shape.append(S((arr.shape[0], w * ncb), f32))
    for k in also_t:
        out_specs.append(pl.BlockSpec((rows[drow[k]][1], tm), lambda j, i: (0, i)))
        out_shape.append(S((rows[drow[k]][1], n), bf16))
    return _pcall(
        side, body, name=name, grid=(ncb, n // tm), in_specs=in_specs, out_specs=out_specs, out_shape=out_shape,
        compiler_params=pltpu.CompilerParams(dimension_semantics=("parallel", "arbitrary")),
    )(*[r[0] for r in rows], *[c[0] for c in consts], *[c[0] for c in cots], *[addto[k] for k in add_keys])


def _scan_fwd(name, f, rows, nh, nchunk, side=None, consts=(), out_dtype=f32, out_k=1):
    n = rows[0][0].shape[0]
    nb = n // (CH * nchunk)
    nin, nco = len(rows), len(consts)
    w = nh * HD

    def body(*refs):
        o_ref, hist_ref, st = refs[nin + nco], refs[nin + nco + 1], refs[nin + nco + 2]

        @pl.when(pl.program_id(0) == 0)
        def _():
            st[...] = jnp.zeros_like(st)

        s0 = st[...]
        hist_ref[0] = s0.astype(hist_ref.dtype)
        tiles = [r[...].reshape(nb * CH, r.shape[2]) for r in refs[:nin]]
        o, s1 = f(*tiles, *[r[...] for r in refs[nin:nin + nco]], s0)
        o_ref[...] = o.reshape(o_ref.shape).astype(o_ref.dtype)
        st[...] = s1

    seq3 = lambda a: a.reshape(nb, nchunk * CH, a.shape[1])
    res = _pcall(
        side, body, name=name, grid=(nchunk,),
        in_specs=[pl.BlockSpec((nb, CH, k * w), functools.partial(lambda c, base: (0, c, base), base=b)) for (_, b, k) in rows]
        + [pl.BlockSpec(c.shape, lambda c: (0, 0)) for c in consts],
        out_specs=[pl.BlockSpec((nb, CH, out_k * w), lambda c: (0, c, 0)), pl.BlockSpec((1, nb * w, HD), lambda c: (c, 0, 0))],
        out_shape=[S((nb, nchunk * CH, out_k * w), out_dtype), S((nchunk, nb * w, HD), bf16)],
        scratch_shapes=[pltpu.VMEM((nb * w, HD), f32)],
        compiler_params=pltpu.CompilerParams(dimension_semantics=("arbitrary",)),
    )(*[seq3(r[0]) for r in rows], *consts)
    (o, hist), extra = (res, None) if side is None else res
    out = [o.reshape(n, out_k * w), hist]
    return out if side is None else (out, extra)


def _scan_bwd(name, f, rows, hist, do, nh, nchunk, side=None, dtypes=None, consts=(), out_k=1):
    n = rows[0][0].shape[0]
    nb = n // (CH * nchunk)
    nin, nco = len(rows), len(consts)
    w = nh * HD

    def body(*refs):
        hist_ref, do_ref = refs[nin + nco], refs[nin + nco + 1]
        outs = refs[nin + nco + 2:nin + nco + 2 + nin]
        couts = refs[nin + nco + 2 + nin:nin + nco + 2 + nin + nco]
        ds = refs[nin + nco + 2 + nin + nco]

        @pl.when(pl.program_id(0) == 0)
        def _():
            ds[...] = jnp.zeros_like(ds)
            for c in couts:
                c[...] = jnp.zeros_like(c)

        tiles = [r[...].reshape(nb * CH, r.shape[2]) for r in refs[:nin]]
        _, vjp = jax.vjp(f, *tiles, *[r[...] for r in refs[nin:nin + nco]], hist_ref[0].astype(f32))
        grads = vjp((do_ref[...].reshape(nb * CH, out_k * w), ds[...]))
        for r, gk in zip(outs, grads[:nin]):
            r[...] = gk.reshape(r.shape).astype(r.dtype)
        for c, gk in zip(couts, grads[nin:nin + nco]):
            c[...] += gk
        ds[...] = grads[nin + nco]

    seq3 = lambda a: a.reshape(nb, nchunk * CH, a.shape[1])
    rev = lambda c, base: (0, nchunk - 1 - c, base)
    res = _pcall(
        side, body, name=name, grid=(nchunk,),
        in_specs=[pl.BlockSpec((nb, CH, k * w), functools.partial(rev, base=b)) for (_, b, k) in rows]
        + [pl.BlockSpec(c.shape, lambda c: (0, 0)) for c in consts]
        + [pl.BlockSpec((1, nb * w, HD), lambda c: (nchunk - 1 - c, 0, 0)), pl.BlockSpec((nb, CH, out_k * w), functools.partial(rev, base=0))],
        out_specs=[pl.BlockSpec((nb, CH, k * w), functools.partial(rev, base=0)) for (_, _, k) in rows]
        + [pl.BlockSpec(c.shape, lambda c: (0, 0)) for c in consts],
        out_shape=[S((nb, nchunk * CH, k * w), dt) for (_, _, k), dt in zip(rows, dtypes or [f32] * nin)]
        + [S(c.shape, f32) for c in consts],
        scratch_shapes=[pltpu.VMEM((nb * w, HD), f32)],
        compiler_params=pltpu.CompilerParams(dimension_semantics=("arbitrary",)),
    )(*[seq3(r[0]) for r in rows], *consts, hist, seq3(do))
    outs, extra = (res, None) if side is None else res
    outs = [o.reshape(n, o.shape[2]) for o in outs[:nin]] + list(outs[nin:])
    return outs if side is None else (outs, extra)


_VMEM_LIMIT = 56 * 2 ** 20
_VMEM_TILE_BUDGET = 40 * 2 ** 20


def _mm_tiles(m, n, k, sa, sb, so, sx, a_f32, b_f32, tn_fixed):
    best = None
    for tm in (1024, 512, 256, 128, 64):
        for tn in ((tn_fixed,) if tn_fixed else (1024, 768, 512, 384, 256, 128)):
            if m % tm or n % tn:
                continue
            need = 2 * (tm * k * sa + k * tn * sb + tm * tn * (so + sx)) + tm * tn * 4
            need += tm * k * (2 if sa == 4 else 0) + k * tn * (2 if sb == 4 else 0)
            need += tm * k * (4 if a_f32 else 0) + k * tn * (4 if b_f32 else 0)
            if need <= _VMEM_TILE_BUDGET and (best is None or (tm * tn, tm) > best[0]):
                best = ((tm * tn, tm), tm, tn)
    return best[1], best[2]


def _mm(name, a, b, dims, out_dtype, a_fn=None, b_fn=None, epi=None, extra=(), consts=(), outs=None, out_t=False, slab=False,
        side=None):
    m, kk = a.shape
    nn = b.shape[1] if dims is NN else b.shape[0]
    kinds = [("tile", out_dtype)] if outs is None else list(outs)
    so = sum(jnp.dtype(dt).itemsize for kd, dt in kinds if kd != "rows")
    sx = sum(e.dtype.itemsize for e in extra)
    full_rows = bool(consts) or any(kd == "rows" for kd, _ in kinds)
    tm, tn = _mm_tiles(m, nn, kk, a.dtype.itemsize, b.dtype.itemsize, so, sx,
                       a_fn is not None, b_fn is not None, _COLW if slab else (nn if full_rows else None))
    nex, nco = len(extra), len(consts)

    def body(a_ref, b_ref, *rest):
        av, bv = a_ref[...], b_ref[...]
        if a_fn is not None:
            av = a_fn(av.astype(f32))
        if b_fn is not None:
            bv = b_fn(bv.astype(f32))
        acc = _dot(av.astype(bf16), bv.astype(bf16), dims)
        if epi is not None:
            acc = epi(acc, *[r[...] for r in rest[:nex + nco]])
        vals = acc if isinstance(acc, tuple) else (acc,)
        for (kd, _), o_ref, val in zip(kinds, rest[nex + nco:], vals):
            if kd == "rows":
                @pl.when(pl.program_id(0) == 0)
                def _():
                    o_ref[...] = jnp.zeros_like(o_ref)

                o_ref[...] += val
            elif kd == "tile_t" or out_t:
                o_ref[...] = val.T.astype(o_ref.dtype)
            elif slab:
                o_ref[0] = val.astype(o_ref.dtype)
            else:
                o_ref[...] = val.astype(o_ref.dtype)

    bspec = pl.BlockSpec((kk, tn), lambda i, j: (0, j)) if dims is NN else pl.BlockSpec((tn, kk), lambda i, j: (j, 0))
    out_specs, out_shape = [], []
    for kd, dt in kinds:
        if kd == "rows":
            out_specs.append(pl.BlockSpec((1, nn), lambda i, j: (0, 0)))
            out_shape.append(S((1, nn), dt))
        elif kd == "tile_t" or out_t:
            out_specs.append(pl.BlockSpec((tn, tm), lambda i, j: (j, i)))
            out_shape.append(S((nn, m), dt))
        elif slab:
            out_specs.append(pl.BlockSpec((1, tm, tn), lambda i, j: (j, i, 0)))
            out_shape.append(S((nn // tn, m, tn), dt))
        else:
            out_specs.append(pl.BlockSpec((tm, tn), lambda i, j: (i, j)))
            out_shape.append(S((m, nn), dt))
    if outs is None:
        out_specs, out_shape = out_specs[0], out_shape[0]
    sem = ("arbitrary", "arbitrary") if any(kd == "rows" for kd, _ in kinds) else ("parallel", "parallel")
    return _pcall(
        side, body, name=name, grid=(m // tm, nn // tn),
        in_specs=[pl.BlockSpec((tm, kk), lambda i, j: (i, 0)), bspec]
        + [pl.BlockSpec((tm, tn), lambda i, j: (i, j)) for _ in extra]
        + [pl.BlockSpec(c.shape, lambda i, j: (0, 0)) for c in consts],
        out_specs=out_specs, out_shape=out_shape,
        compiler_params=pltpu.CompilerParams(dimension_semantics=sem, vmem_limit_bytes=_VMEM_LIMIT),
    )(a, b, *extra, *consts)


def _f_norm(x, g):
    return (x * lax.rsqrt(jnp.mean(x * x, axis=-1, keepdims=True) + EPS) * g,)


def _f_foxpre(zqk, gq, gk, pm):
    def nrm(t, g):
        return t * lax.rsqrt(_sel_rhs(t * t, pm) + EPS) * g
    return nrm(zqk[:, :FOX_W], gq), nrm(zqk[:, FOX_W:], gk)


def _chunk_cumsum(x):
    n = x.shape[0]
    r, c = _iota2((n, n), 0), _iota2((n, n), 1)
    tri = jnp.logical_and(r >= c, (r // CH) == (c // CH)).astype(f32)
    return _sel_lhs(tri, x)


def _f_gdngate(zs, eb, ea, alog_b, dt_b):
    beta = jax.nn.sigmoid(_sel_rhs(zs, eb))
    la = -jnp.exp(alog_b) * jax.nn.softplus(_sel_rhs(zs, ea) + dt_b)
    return beta, _chunk_cumsum(la)


def _f_conv(j, x, w):
    t = x.shape[0]
    y = x * w[3:4, :]
    for jj in range(3):
        sh = 3 - jj
        xs = jnp.concatenate([jnp.zeros((sh, x.shape[1]), f32), x[:t - sh, :]], axis=0)
        y = y + xs * w[jj:jj + 1, :]
    y = jax.nn.silu(y)
    yn = y * lax.rsqrt(jnp.sum(y * y, axis=-1, keepdims=True) + EPS)
    return (jnp.where(j < 2 * GDN_H, yn, y),)


def _head_rms(o, nh):
    outs = []
    for h in range(nh):
        oh = o[:, HD * h:HD * (h + 1)]
        outs.append(oh * lax.rsqrt(jnp.mean(oh * oh, axis=-1, keepdims=True) + EPS))
    return jnp.concatenate(outs, axis=1)


def _f_post0(fox_o, o, gg, on):
    return (jnp.concatenate([fox_o, _head_rms(o, GDN_H) * on * jax.nn.silu(gg)], axis=1),)


def _f_post1(o, zg, on):
    return (_head_rms(o, HG_H) * on * jax.nn.silu(zg),)


def _f_hpre(zqf, lbl):
    lb = jax.nn.sigmoid(lbl[1:2, :] - lbl[0:1, :])
    fg = lb + (1.0 - lb) * jax.nn.sigmoid(zqf[:, D:])
    return jax.nn.silu(zqf[:, :D]), 1.0 - fg, _chunk_cumsum(jnp.log(fg))


def _dotb(a, b, dims=NN):
    return _dot(a.astype(bf16), b.astype(bf16), dims)


def _dot3(a, b):
    ah, bh = a.astype(bf16), b.astype(bf16)
    al, bl = (a - ah.astype(f32)).astype(bf16), (b - bh.astype(f32)).astype(bf16)
    return _dot(ah, bh) + (_dot(ah, bl) + _dot(al, bh))


def _split(t, nh):
    return [t[CH * ck:CH * (ck + 1), HD * h:HD * (h + 1)] for ck in range(t.shape[0] // CH) for h in range(nh)]


def _merge(units, nh):
    return jnp.concatenate([jnp.concatenate(units[i:i + nh], axis=1) for i in range(0, len(units), nh)], axis=0)


def _inv_impl(amats):
    n = amats[0].shape[0]
    eye = jnp.where(_iota2((n, n), 0) == _iota2((n, n), 1), 1.0, 0.0).astype(f32)
    xs, ps = [eye - a for a in amats], list(amats)
    for _ in range(max(1, (n - 1).bit_length()) - 1):
        ps = [_dotb(p, p) for p in ps]
        xs = [x + _dotb(x, p) for x, p in zip(xs, ps)]
    for _ in range(3):
        rs = [eye - x - _dot3(a, x) for a, x in zip(amats, xs)]
        xs = [x + _dotb(x, r) for x, r in zip(xs, rs)]
    return tuple(xs)


@jax.custom_vjp
def _inv_unit_lower(amats):
    return _inv_impl(amats)


def _inv_fwd(amats):
    xs = _inv_impl(amats)
    return xs, xs


def _inv_bwd(xs, dxs):
    return (tuple(-_dotb(_dotb(x, dx, TN), x, NT) for x, dx in zip(xs, dxs)),)


_inv_unit_lower.defvjp(_inv_fwd, _inv_bwd)


@jax.custom_vjp
def _inv_given(amats, xs):
    return xs


def _inv_given_fwd(amats, xs):
    return xs, xs


def _inv_given_bwd(xs, dxs):
    return _inv_bwd(xs, dxs)[0], tuple(jnp.zeros_like(x) for x in xs)


_inv_given.defvjp(_inv_given_fwd, _inv_given_bwd)


def _f_gdn_intra(q, k, v, bb, gb, tinv_p=None):
    qs, ks, vs, bs, gs = (_split(t, GDN_H) for t in (q, k, v, bb, gb))
    r, cc = _iota2((CH, CH), 0), _iota2((CH, CH), 1)
    causal, strict = r >= cc, r > cc
    beta, g, gl = [b[:, :1] for b in bs], [x[:, :1] for x in gs], [x[CH - 1:CH, :1] for x in gs]
    decay = [jnp.exp(jnp.where(causal, x[:, :CH] - x[:, :CH].T, -jnp.inf)) for x in gs]
    kb = [ki * bi for ki, bi in zip(ks, beta)]
    amat = [jnp.where(strict, _dotb(kbi, ki, NT) * di, 0.0) for kbi, ki, di in zip(kb, ks, decay)]
    if tinv_p is None:
        tinv = _inv_unit_lower(tuple(amat))
    else:
        tinv = _inv_given(tuple(amat), tuple(x[:, :CH] for x in _split(tinv_p, GDN_H)))
    rhs = [jnp.concatenate([vi * bi, kbi * jnp.exp(gi)], axis=1) for vi, bi, kbi, gi in zip(vs, beta, kb, g)]
    uw = [_dotb(ti, ri) for ti, ri in zip(tinv, rhs)]
    qsc = [qi * (HD ** -0.5) for qi in qs]
    qk = [jnp.where(causal, _dotb(qi, ki, NT) * di, 0.0) for qi, ki, di in zip(qsc, ks, decay)]
    outs = ([x[:, :HD] for x in uw], [x[:, HD:] for x in uw],
            [jnp.concatenate([x, jnp.zeros_like(x)], axis=1) for x in qk],
            [qi * jnp.exp(gi) for qi, gi in zip(qsc, g)],
            [ki * jnp.exp(gli - gi) for ki, gli, gi in zip(ks, gl, g)],
            [jnp.broadcast_to(gli, (CH, HD)) for gli in gl])
    if tinv_p is None:
        outs += ([jnp.concatenate([x, jnp.zeros_like(x)], axis=1) for x in tinv],)
    return tuple(_merge(o, GDN_H) for o in outs)


def _f_gdn_gated(q, k, v, zs, eb, ea, alog_b, dt_b):
    return _f_gdn_intra(q, k, v, *_f_gdngate(zs, eb, ea, alog_b, dt_b))


def _f_gdn_gated_given(q, k, v, zs, tinv_p, eb, ea, alog_b, dt_b):
    return _f_gdn_intra(q, k, v, *_f_gdngate(zs, eb, ea, alog_b, dt_b), tinv_p=tinv_p)


def _f_gdn_inter(u, w, qkp, qd, kd, glb, st):
    us, ws, qks, qds, kds, gls = (_split(t, GDN_H) for t in (u, w, qkp, qd, kd, glb))
    sts = [st[HD * i:HD * (i + 1), :] for i in range(len(us))]
    vn = [ui - _dotb(wi, si) for ui, wi, si in zip(us, ws, sts)]
    o = [_dotb(qi, si) + _dotb(xi[:, :CH], vi) for qi, si, xi, vi in zip(qds, sts, qks, vn)]
    s2 = [si * jnp.exp(gi[:1, :1]) + _dotb(ki, vi, TN) for si, gi, ki, vi in zip(sts, gls, kds, vn)]
    return _merge(o, GDN_H), jnp.concatenate(s2, axis=0)


def _f_gdn_full(u, w, qkp, qd, kd, glb, fox_o, gg, on, st):
    o, s2 = _f_gdn_inter(u, w, qkp, qd, kd, glb, st)
    return _f_post0(fox_o, o, gg, on)[0], s2


def _f_hgrn_chunk(q, k, b, v, st):
    qs, ks, bs, vs = (_split(t, HG_H) for t in (q, k, b, v))
    sts = [st[HD * i:HD * (i + 1), :] for i in range(len(qs))]
    causal = _iota2((CH, CH), 0) >= _iota2((CH, CH), 1)
    bl, bm = [x[CH - 1:CH, :] for x in bs], [x[CH // 2 - 1:CH // 2, :] for x in bs]
    a = [jnp.where(causal, _dotb(qi * jnp.exp(bi - mi), ki * jnp.exp(mi - bi), NT), 0.0)
         for qi, ki, bi, mi in zip(qs, ks, bs, bm)]
    o = [_dotb(qi * jnp.exp(bi), si, NT) + _dotb(ai, vi) for qi, bi, si, ai, vi in zip(qs, bs, sts, a, vs)]
    s2 = [si * jnp.exp(li) + _dotb(vi, ki * jnp.exp(li - bi), TN) for si, li, vi, ki, bi in zip(sts, bl, vs, ks, bs)]
    return _merge(o, HG_H), jnp.concatenate(s2, axis=0)


def _f_hgrn_full(z, lbl, on, st):
    o, s2 = _f_hgrn_chunk(*_f_hpre(z[:, :2 * D], lbl), z[:, 2 * D:3 * D], st)
    return _f_post1(o, z[:, 3 * D:], on)[0], s2


def _fox_gate_fwd(z0, fbias, t, tc=256):
    n = z0.shape[0]
    nt = t // tc

    def body(zs_ref, b_ref, ccol_ref, crow_ref, carry):
        @pl.when(pl.program_id(1) == 0)
        def _():
            carry[...] = jnp.zeros_like(carry)

        ls = jnp.where(_iota2((tc, LANES), 1) < FOX_H, jax.nn.log_sigmoid(zs_ref[...] + b_ref[...]), 0.0)
        tri = (_iota2((tc, tc), 0) >= _iota2((tc, tc), 1)).astype(f32)
        c = _dot_sel(tri, ls, exact_lhs=True) + carry[...]
        carry[...] = c[tc - 1:tc, :]
        ccol_ref[...] = c
        crow_ref[0] = c.T[:FOX_H, :]

    return pl.pallas_call(
        body, name="fox_gate_fwd", grid=(n // t, nt),
        in_specs=[pl.BlockSpec((tc, LANES), lambda b, i: (b * nt + i, Z0_SMALL)), pl.BlockSpec((1, LANES), lambda b, i: (0, 0))],
        out_specs=[pl.BlockSpec((tc, LANES), lambda b, i: (b * nt + i, 0)), pl.BlockSpec((1, FOX_H, tc), lambda b, i: (b, 0, i))],
        out_shape=[S((n, LANES), f32), S((n // t, FOX_H, t), f32)],
        scratch_shapes=[pltpu.VMEM((1, LANES), f32)],
        compiler_params=pltpu.CompilerParams(dimension_semantics=("parallel", "arbitrary")),
    )(z0, fbias)


def _fox_gate_bwd(z0, fbias, dcq, dck, t, tc=256):
    n = z0.shape[0]
    nt = t // tc

    def body(zs_ref, b_ref, dcq_ref, dck_ref, dz_ref, db_ref, carry):
        first = jnp.logical_and(pl.program_id(0) == 0, pl.program_id(1) == 0)

        @pl.when(pl.program_id(1) == 0)
        def _():
            carry[...] = jnp.zeros_like(carry)

        @pl.when(first)
        def _():
            db_ref[...] = jnp.zeros_like(db_ref)

        dc = dcq_ref[0] + dcq_ref[1] + dcq_ref[2] + dcq_ref[3]
        drow = dck_ref[0, 0] + dck_ref[1, 0] + dck_ref[2, 0] + dck_ref[3, 0]
        eye = (_iota2((FOX_H, LANES), 0) == _iota2((FOX_H, LANES), 1)).astype(f32)
        dc = dc + _dot_sel(drow, eye, TN)
        triu = (_iota2((tc, tc), 0) <= _iota2((tc, tc), 1)).astype(f32)
        dls = _dot_sel(triu, dc, exact_lhs=True) + carry[...]
        carry[...] = dls[0:1, :]
        x = zs_ref[...] + b_ref[...]
        dz = jnp.where(_iota2((tc, LANES), 1) < FOX_H, dls * jax.nn.sigmoid(-x), 0.0)
        dz_ref[...] = dz
        db_ref[...] += jnp.sum(dz, axis=0, keepdims=True)

    def rev(b, i):
        return b * nt + (nt - 1 - i)

    return pl.pallas_call(
        body, name="fox_gate_bwd", grid=(n // t, nt),
        in_specs=[pl.BlockSpec((tc, LANES), lambda b, i: (rev(b, i), Z0_SMALL)), pl.BlockSpec((1, LANES), lambda b, i: (0, 0)),
                  pl.BlockSpec((4, tc, LANES), lambda b, i: (0, rev(b, i), 0)),
                  pl.BlockSpec((4, 1, FOX_H, tc), lambda b, i: (0, b, 0, nt - 1 - i))],
        out_specs=[pl.BlockSpec((tc, LANES), lambda b, i: (rev(b, i), 0)), pl.BlockSpec((1, LANES), lambda b, i: (0, 0))],
        out_shape=[S((n, LANES), f32), S((1, LANES), f32)],
        scratch_shapes=[pltpu.VMEM((1, LANES), f32)],
        compiler_params=pltpu.CompilerParams(dimension_semantics=("arbitrary", "arbitrary")),
    )(z0, fbias, dcq, dck)


def _fox_scores(hh, p, i, tq, q, k, ccol, crow):
    kmax = k.shape[0]
    lane = _iota2((1, LANES), 1)
    mh = (lane // FOX_D) == hh
    h = 2 * p + hh
    qh = jnp.where(mh, q, 0.0).astype(bf16)
    s = _dot(qh, k, NT) * (FOX_D ** -0.5)
    cq = jnp.sum(jnp.where(lane == h, ccol, 0.0), axis=1, keepdims=True)
    ck = jnp.sum(jnp.where(_iota2((FOX_H, 1), 0) == h, crow, 0.0), axis=0, keepdims=True)
    causal = _iota2((1, kmax), 1) <= (i * tq + _iota2((tq, 1), 0))
    s = jnp.where(causal, s + cq - ck, -jnp.inf)
    pe = jnp.exp(s - jnp.max(s, axis=1, keepdims=True))
    return mh, qh, pe, jnp.sum(pe, axis=1, keepdims=True)


def _fox_attn_fwd(qn, kn, z0, ccol, crow, t, tq=256, side=None):
    n = qn.shape[0]
    nq = t // tq

    def body(q_ref, k_ref, v_ref, ccol_ref, crow_ref, o_ref):
        p = pl.program_id(1)
        k, v, crow = k_ref[...].astype(bf16), v_ref[...].astype(bf16), crow_ref[0]
        for i in range(nq):
            rows, kmax = pl.ds(i * tq, tq), (i + 1) * tq
            q, cc = q_ref[rows, :], ccol_ref[rows, :]
            acc = jnp.zeros((tq, LANES), f32)
            for hh in range(2):
                mh, _, pe, l = _fox_scores(hh, p, i, tq, q, k[:kmax], cc, crow[:, :kmax])
                acc = jnp.where(mh, _dot(pe.astype(bf16), v[:kmax]) / l, acc)
            o_ref[rows, :] = acc

    seq = lambda b, p: (b, p)
    return _pcall(
        side, body, name="fox_attn_fwd", grid=(n // t, FOX_H // 2),
        in_specs=[pl.BlockSpec((t, LANES), seq), pl.BlockSpec((t, LANES), seq), pl.BlockSpec((t, LANES), lambda b, p: (b, Z0_FV + p)),
                  pl.BlockSpec((t, LANES), lambda b, p: (b, 0)), pl.BlockSpec((1, FOX_H, t), lambda b, p: (b, 0, 0))],
        out_specs=pl.BlockSpec((t, LANES), seq),
        out_shape=S((n, FOX_W), f32),
        compiler_params=pltpu.CompilerParams(dimension_semantics=("parallel", "parallel")),
    )(qn, kn, z0, ccol, crow)


def _fox_attn_bwd(qn, kn, z0, ccol, crow, do, t, tq=256, side=None):
    n = qn.shape[0]
    nq = t // tq
    nb = n // t

    def body(q_ref, k_ref, v_ref, ccol_ref, crow_ref, do_ref, dq_ref, dk_ref, dv_ref, dcq_ref, dck_ref):
        p = pl.program_id(1)
        dk_ref[...] = jnp.zeros_like(dk_ref)
        dv_ref[...] = jnp.zeros_like(dv_ref)
        dck_ref[...] = jnp.zeros_like(dck_ref)
        kf, v, crow = k_ref[...], v_ref[...].astype(bf16), crow_ref[0]
        k = kf.astype(bf16)
        lane = _iota2((1, LANES), 1)
        sub = _iota2((FOX_H, 1), 0)
        scale = FOX_D ** -0.5
        for i in range(nq):
            rows, kmax = pl.ds(i * tq, tq), (i + 1) * tq
            q, cc, dout = q_ref[rows, :], ccol_ref[rows, :], do_ref[rows, :]
            dq = jnp.zeros((tq, LANES), f32)
            dcq = jnp.zeros((tq, LANES), f32)
            for hh in range(2):
                mh, qh, pe, l = _fox_scores(hh, p, i, tq, q, k[:kmax], cc, crow[:, :kmax])
                pr = pe / l
                doh = jnp.where(mh, dout, 0.0).astype(bf16)
                dp = _dot(doh, v[:kmax], NT)
                ds = pr * (dp - jnp.sum(pr * dp, axis=1, keepdims=True))
                dsb = ds.astype(bf16)
                dq = dq + _dot(dsb, jnp.where(mh, kf[:kmax], 0.0).astype(bf16)) * scale
                dk_ref[:kmax, :] += _dot(dsb, qh, TN) * scale
                dv_ref[:kmax, :] += _dot(pr.astype(bf16), doh, TN)
                h = 2 * p + hh
                dcq = dcq + jnp.where(lane == h, jnp.sum(ds, axis=1, keepdims=True), 0.0)
                dck_ref[0, 0, :, :kmax] += jnp.where(sub == h, -jnp.sum(ds, axis=0, keepdims=True), 0.0)
            dq_ref[rows, :] = dq
            dcq_ref[0, rows, :] = dcq

    seq = lambda b, p: (b, p)
    return _pcall(
        side, body, name="fox_attn_bwd", grid=(nb, FOX_H // 2),
        in_specs=[pl.BlockSpec((t, LANES), seq), pl.BlockSpec((t, LANES), seq), pl.BlockSpec((t, LANES), lambda b, p: (b, Z0_FV + p)),
                  pl.BlockSpec((t, LANES), lambda b, p: (b, 0)), pl.BlockSpec((1, FOX_H, t), lambda b, p: (b, 0, 0)),
                  pl.BlockSpec((t, LANES), seq)],
        out_specs=[pl.BlockSpec((t, LANES), seq), pl.BlockSpec((t, LANES), seq), pl.BlockSpec((t, LANES), seq),
                   pl.BlockSpec((1, t, LANES), lambda b, p: (p, b, 0)), pl.BlockSpec((1, 1, FOX_H, t), lambda b, p: (p, b, 0, 0))],
        out_shape=[S((n, FOX_W), f32), S((n, FOX_W), f32), S((n, FOX_W), f32), S((4, n, LANES), f32), S((4, nb, FOX_H, t), f32)],
        compiler_params=pltpu.CompilerParams(dimension_semantics=("parallel", "parallel")),
    )(qn, kn, z0, ccol, crow, do)


def _adamw_math(w, g, m, v):
    m = ADAM_B1 * m + (1.0 - ADAM_B1) * g
    v = ADAM_B2 * v + (1.0 - ADAM_B2) * (g * g)
    m_hat = m / (1.0 - ADAM_B1 ** ADAM_STEP)
    v_hat = v / (1.0 - ADAM_B2 ** ADAM_STEP)
    return -ADAM_LR * (m_hat / (jnp.sqrt(v_hat) + ADAM_EPS) + ADAM_WD * w), m, v


def _adamw_big(name, idx, gmine, recv1, recv2, w, m, v):
    r, wc = w.shape
    c = gmine.shape[2]
    tr = min(r, 256)

    def body(idx_ref, gm_ref, r1_ref, r2_ref, w_ref, m_ref, v_ref, g_ref, d_ref, nm_ref, nv_ref):
        g = gm_ref[0].astype(f32) + r1_ref[0].astype(f32)
        for k in range(3):
            g = g + r2_ref[k].astype(f32)
        g = g[:, :wc]
        d, nm, nv = _adamw_math(w_ref[...], g, m_ref[...], v_ref[...])
        g_ref[...] = g
        d_ref[...] = d
        nm_ref[...] = nm
        nv_ref[...] = nv

    row = pl.BlockSpec((tr, wc), lambda i, s: (i, 0))
    return pl.pallas_call(
        body, name=name,
        grid_spec=pltpu.PrefetchScalarGridSpec(
            num_scalar_prefetch=1, grid=(r // tr,),
            in_specs=[pl.BlockSpec((1, tr, c), lambda i, s: (s[0], i, 0)), pl.BlockSpec((1, tr, c), lambda i, s: (s[1], i, 0)),
                      pl.BlockSpec((3, tr, c), lambda i, s: (0, i, 0)), row, row, row],
            out_specs=[row, row, row, row]),
        out_shape=[S((r, wc), f32)] * 4,
        compiler_params=pltpu.CompilerParams(dimension_semantics=("parallel",)),
    )(idx, gmine, recv1, recv2, w, m, v)


def _pair_sum(name, idx, gmine, recv1):
    _, r, c = gmine.shape
    g4 = gmine.reshape(4, 2, r, c)

    def body(idx_ref, gm_ref, r1_ref, o_ref):
        o_ref[0] = (gm_ref[0, 0].astype(f32) + r1_ref[0].astype(f32)).astype(bf16)

    return pl.pallas_call(
        body, name=name,
        grid_spec=pltpu.PrefetchScalarGridSpec(
            num_scalar_prefetch=1, grid=(4,),
            in_specs=[pl.BlockSpec((1, 1, r, c), lambda ch, s: (ch, s[0], 0, 0)), pl.BlockSpec((1, r, c), lambda ch, s: (ch, 0, 0))],
            out_specs=pl.BlockSpec((1, r, c), lambda ch, s: (ch, 0, 0))),
        out_shape=S((4, r, c), bf16),
        compiler_params=pltpu.CompilerParams(dimension_semantics=("parallel",)),
    )(idx, g4, recv1)


def _adamw_small(parts, w, m, v, own_mask, own_g):
    def body(p_ref, w_ref, m_ref, v_ref, mask_ref, og_ref, g_ref, d_ref, nm_ref, nv_ref):
        g = p_ref[0]
        for k in range(1, NDEV):
            g = g + p_ref[k]
        g_ref[...] = g
        ge = jnp.where(mask_ref[...] > 0.5, og_ref[...], g)
        d, nm, nv = _adamw_math(w_ref[...], ge, m_ref[...], v_ref[...])
        d_ref[...] = d
        nm_ref[...] = nm
        nv_ref[...] = nv

    return pl.pallas_call(body, name="adamw_small", out_shape=[S(w.shape, f32)] * 4)(parts, w, m, v, own_mask, own_g)


def _sum_parts(parts):
    def body(p_ref, g_ref):
        g = p_ref[0]
        for k in range(1, NDEV):
            g = g + p_ref[k]
        g_ref[...] = g

    return pl.pallas_call(body, name="sum_parts", out_shape=S(parts.shape[1:], f32))(parts)


def _me():
    return lax.axis_index("x"), lax.axis_index("y"), lax.axis_index("c")


def _hbm_specs(n):
    return [pl.BlockSpec(memory_space=pl.ANY)] * n


def _allgather(name, xs):
    na = len(xs)

    def body(*refs):
        x_refs, out_refs = refs[:na], refs[na:2 * na]
        send_sems, recv_sems, local_sems = refs[2 * na:]
        mx, my, mc = _me()
        me, sib = (mx, my, mc), (mx, my, 1 - mc)
        chips = [(1 - mx, my), (mx, 1 - my), (1 - mx, 1 - my)]

        def slab(a, px, py, pc):
            return out_refs[a].at[4 * px + 2 * py + pc]

        def copy(a, k, block, to, own=False):
            return pltpu.make_async_remote_copy(
                src_ref=x_refs[a] if own else slab(a, *block), dst_ref=slab(a, *block),
                send_sem=send_sems.at[7 * a + k], recv_sem=recv_sems.at[7 * a + k], device_id=to, device_id_type=MESH)

        mine = [pltpu.make_async_copy(x_refs[a], slab(a, *me), local_sems.at[a]) for a in range(na)]
        first = []
        for a in range(na):
            mine[a].start()
            first += [copy(a, 0, me, sib, own=True)] + [copy(a, 1 + j, me, (*chip, mc), own=True) for j, chip in enumerate(chips)]
        for cp in first:
            cp.start()
        passed = []
        for j, chip in enumerate(chips):
            for a in range(na):
                copy(a, 1 + j, (*chip, mc), me).wait_recv()
                passed.append(copy(a, 4 + j, (*chip, mc), sib))
                passed[-1].start()
        for a in range(na):
            copy(a, 0, sib, me).wait_recv()
            for j, chip in enumerate(chips):
                copy(a, 4 + j, (*chip, 1 - mc), me).wait_recv()
        for cp in first + passed:
            cp.wait_send()
        for cp in mine:
            cp.wait()

    return pl.pallas_call(
        body, name=name, out_shape=[S((NDEV,) + x.shape, x.dtype) for x in xs],
        in_specs=_hbm_specs(na), out_specs=_hbm_specs(na),
        scratch_shapes=[pltpu.SemaphoreType.DMA((7 * na,)), pltpu.SemaphoreType.DMA((7 * na,)), pltpu.SemaphoreType.DMA((na,))],
    )(*xs)


def _rs_sibling(gs):
    na = len(gs)

    def body(*refs):
        g_refs, out_refs, send_sems, recv_sems = refs[:na], refs[na:2 * na], refs[2 * na], refs[2 * na + 1]
        mx, my, mc = _me()
        cps = [pltpu.make_async_remote_copy(
            src_ref=g_refs[a].at[2 * ch + 1 - mc], dst_ref=out_refs[a].at[ch], send_sem=send_sems.at[4 * a + ch],
            recv_sem=recv_sems.at[4 * a + ch], device_id=(mx, my, 1 - mc), device_id_type=MESH)
            for a in range(na) for ch in range(4)]
        for cp in cps:
            cp.start()
        for cp in cps:
            cp.wait_recv()
        for cp in cps:
            cp.wait_send()

    return pl.pallas_call(
        body, name="rs_sibling", out_shape=[S((4,) + g.shape[1:], g.dtype) for g in gs],
        in_specs=_hbm_specs(na), out_specs=_hbm_specs(na),
        scratch_shapes=[pltpu.SemaphoreType.DMA((4 * na,)), pltpu.SemaphoreType.DMA((4 * na,))],
    )(*gs)


def _side_allgather(xs):
    na = len(xs)

    def mk(x_refs, out_refs, sems):
        send_sems, recv_sems, local_sems = sems
        mx, my, mc = _me()
        me, sib = (mx, my, mc), (mx, my, 1 - mc)
        chips = [(1 - mx, my), (mx, 1 - my), (1 - mx, 1 - my)]

        def slab(a, px, py, pc):
            return out_refs[a].at[4 * px + 2 * py + pc]

        def copy(a, k, block, to, own=False):
            return pltpu.make_async_remote_copy(
                src_ref=x_refs[a] if own else slab(a, *block), dst_ref=slab(a, *block),
                send_sem=send_sems.at[7 * a + k], recv_sem=recv_sems.at[7 * a + k], device_id=to, device_id_type=MESH)

        mine = [pltpu.make_async_copy(x_refs[a], slab(a, *me), local_sems.at[a]) for a in range(na)]
        first = []
        for a in range(na):
            first += [copy(a, 0, me, sib, own=True)] + [copy(a, 1 + j, me, (*chip, mc), own=True) for j, chip in enumerate(chips)]
        return me, sib, chips, mc, copy, mine, first

    def start(x_refs, out_refs, sems):
        *_, mine, first = mk(x_refs, out_refs, sems)
        for cp in mine + first:
            cp.start()

    def finish(x_refs, out_refs, sems):
        me, sib, chips, mc, copy, mine, first = mk(x_refs, out_refs, sems)
        passed = []
        for j, chip in enumerate(chips):
            for a in range(na):
                copy(a, 1 + j, (*chip, mc), me).wait_recv()
                passed.append(copy(a, 4 + j, (*chip, mc), sib))
                passed[-1].start()
        for a in range(na):
            copy(a, 0, sib, me).wait_recv()
            for j, chip in enumerate(chips):
                copy(a, 4 + j, (*chip, 1 - mc), me).wait_recv()
        for cp in first + passed:
            cp.wait_send()
        for cp in mine:
            cp.wait()

    scratch = [pltpu.SemaphoreType.DMA((7 * na,)), pltpu.SemaphoreType.DMA((7 * na,)), pltpu.SemaphoreType.DMA((na,))]
    return _Side(xs, [S((NDEV,) + x.shape, x.dtype) for x in xs], scratch, start, finish)


def _side_exchange(arrs, nslot, out_slots, route):
    na = len(arrs)

    def copies(in_refs, out_refs, sems):
        send_sems, recv_sems = sems
        return [pltpu.make_async_remote_copy(
            src_ref=in_refs[a].at[src], dst_ref=out_refs[a].at[k], send_sem=send_sems.at[nslot * a + k],
            recv_sem=recv_sems.at[nslot * a + k], device_id=to, device_id_type=MESH)
            for a in range(na) for k, (src, to) in enumerate(route(*_me()))]

    def start(in_refs, out_refs, sems):
        for cp in copies(in_refs, out_refs, sems):
            cp.start()

    def finish(in_refs, out_refs, sems):
        cps = copies(in_refs, out_refs, sems)
        for cp in cps:
            cp.wait_recv()
        for cp in cps:
            cp.wait_send()

    scratch = [pltpu.SemaphoreType.DMA((nslot * na,)), pltpu.SemaphoreType.DMA((nslot * na,))]
    return _Side(arrs, [S((out_slots,) + x.shape[1:], x.dtype) for x in arrs], scratch, start, finish)


def _side_rs_sibling(gs):
    return _side_exchange(gs, 4, 4, lambda mx, my, mc: [(2 * ch + 1 - mc, (mx, my, 1 - mc)) for ch in range(4)])


def _side_rs_chips(ps):
    return _side_exchange(ps, 3, 3, lambda mx, my, mc: [(2 * cx + cy, (cx, cy, mc)) for cx, cy in
                                                        [(1 - mx, my), (mx, 1 - my), (1 - mx, 1 - my)]])


_COLW = 512
_COL_NAMES = ("l0_w_in", "l0_w_ff1", "l1_w_in", "l1_w_ff1")
_ROW_NAMES = ("l0_w_out", "l0_w_ff2", "l1_w_out", "l1_w_ff2")
_BIG_NAMES = _COL_NAMES + _ROW_NAMES


def _full_weight(gathered, name, ncols):
    if name in _ROW_NAMES:
        return gathered.reshape(-1, D)
    return gathered[:, :, :ncols].transpose(1, 0, 2).reshape(D, NDEV * ncols)


def _regroup_w_in0(w):
    main = jnp.concatenate([w[:, 0:1536], w[:, 1544:3080], w[:, 3088:3600]], axis=1)
    small = jnp.concatenate([w[:, 1536:1544], w[:, 3080:3088]], axis=1)
    return jnp.concatenate([main, small, jnp.zeros((D, ZW0 - 3584 - 16), w.dtype)], axis=1)


def _ungroup_w_in0(g):
    return jnp.concatenate([g[:, 0:1536], g[:, 3584:3592], g[:, 1536:3072], g[:, 3592:3600], g[:, 3072:3584]], axis=1)


def _col_slabs(g, ncols):
    g = g.reshape(D, NDEV, ncols).transpose(1, 0, 2)
    return jnp.pad(g, ((0, 0), (0, 0), (0, _COLW - ncols)))


def _sq(t):
    return t * t


def _epi_res_norm(acc, res, gain):
    y = acc + res
    h = _f_norm(y, gain)[0]
    return y, h, h


_RES_NORM_OUTS = [("tile", f32), ("tile", bf16), ("tile_t", bf16)]


def _epi_norm_bwd(acc, x, dres, gain):
    _, vjp = jax.vjp(lambda xx, gg: _f_norm(xx, gg)[0], x, gain)
    dx, dgain = vjp(acc)
    dx = dx + dres
    return dx, dgain, dx


_NORM_BWD_OUTS = [("tile", f32), ("rows", f32), ("tile_t", bf16)]


def _epi_loss(acc, res, tgt):
    e = acc + res - tgt
    dy = e * (1.0 / D)
    return dy, dy, jnp.sum(e * e, axis=0, keepdims=True)


_LOSS_OUTS = [("tile", f32), ("tile_t", bf16), ("rows", f32)]


def _mlp_fwd(tag, x, h, w1, w2, epi, extra, consts, outs):
    a = _mm(f"{tag}_ff1", h, w1, NN, bf16, epi=lambda acc: jnp.maximum(acc, 0.0))
    return _mm(f"{tag}_ff2", a, w2, NN, f32, a_fn=_sq, epi=epi, extra=(x,) + tuple(extra), consts=consts, outs=outs), a


def _mlp_bwd(tag, x, gain, w1, w2, ht, a, dy, dyt, side=None):
    da = _mm(f"{tag}_ff2_dx", dy, w2, NT, bf16, epi=lambda acc, av: acc * 2.0 * av.astype(f32), extra=(a,), side=side)
    if side is not None:
        da, side_res = da
    dw2 = _mm(f"{tag}_ff2_dw", dyt, a, NN, bf16, b_fn=_sq, out_t=True)
    res = _mm(f"{tag}_ff1_dx", da, w1, NT, f32, epi=_epi_norm_bwd, extra=(x, dy), consts=(gain,), outs=_NORM_BWD_OUTS)
    dw1 = _mm(f"{tag}_ff1_dw", ht, da, NN, bf16, slab=True)
    return (res, dw1, dw2) if side is None else ((res, dw1, dw2), side_res)


def _row(v):
    return v.reshape(1, -1).astype(f32)


_L0_REST = ("l0_w_ff1", "l0_w_out", "l0_w_ff2")
_L1_MIX = ("l1_w_in", "l1_w_out")
_L1_FFN = ("l1_w_ff1", "l1_w_ff2")
_GRAD_A = ("l1_w_ff1", "l1_w_ff2", "l1_w_out", "l1_w_in")
_GRAD_B = ("l0_w_ff1", "l0_w_ff2", "l0_w_out")


def _train(x, tgt, args, mom, var, t):
    n = x.shape[0]
    nchunk = t // CH
    mx, my, mc = _me()
    dev, chip = 4 * mx + 2 * my + mc, 2 * mx + my
    core_idx = jnp.reshape(mc, (1,)).astype(jnp.int32)
    own_idx = jnp.stack([dev, chip]).astype(jnp.int32)
    ncols = {nm: args[nm].shape[1] for nm in _COL_NAMES}
    conv_cols = args["l0_gdn_conv"].shape[1]
    g, big, w = {}, {}, {}

    def send(nm):
        a = args[nm].astype(bf16)
        return jnp.pad(a, ((0, 0), (0, _COLW - ncols[nm]))) if nm in _COL_NAMES else a

    def take(names, gathered):
        for nm, arr in zip(names, gathered):
            w[nm] = _full_weight(arr, nm, ncols.get(nm, 0))

    def by_dev(nm, ga):
        return ga if nm in _COL_NAMES else ga.reshape(NDEV, -1, D)

    def pair(names, gs, r1s):
        return [_pair_sum(f"rs_pair_sum_{nm}", core_idx, ga, r1) for nm, ga, r1 in zip(names, gs, r1s)]

    def adam(names, gs, r1s, r2s):
        for nm, ga, r1, r2 in zip(names, gs, r1s, r2s):
            big[nm] = _adamw_big(f"adamw_{nm}", own_idx, ga, r1, r2, args[nm], mom[nm], var[nm])


    li = jnp.arange(FOX_W)
    pm = jnp.where((li[:, None] // FOX_D) == (li[None, :] // FOX_D), 1.0 / FOX_D, 0.0).astype(f32)
    lane_head = jnp.arange(GDN_W) // HD
    sel = lambda first_lane: (jnp.arange(LANES)[:, None] == (first_lane + lane_head)[None, :]).astype(f32)
    e_beta, e_alpha = sel(FOX_H), sel(FOX_H + GDN_H)
    alog_b, dt_b = _row(jnp.repeat(args["l0_gdn_A_log"], HD)), _row(jnp.repeat(args["l0_gdn_dt_bias"], HD))
    gq_t, gk_t = _row(jnp.tile(args["l0_fox_q_norm"], FOX_H)), _row(jnp.tile(args["l0_fox_k_norm"], FOX_H))
    on0_t, on1_t = _row(jnp.tile(args["l0_gdn_o_norm"], GDN_H)), _row(jnp.tile(args["l1_hgrn_o_norm"], HG_H))
    fbias = jnp.pad(_row(args["l0_fox_f_bias"]), ((0, 0), (0, LANES - FOX_H)))
    g0m, g0f, g1m, g1f = (_row(args[k]) for k in ("l0_mix_norm", "l0_ffn_norm", "l1_mix_norm", "l1_ffn_norm"))
    lbl = args["hgrn_lb_logits"].astype(f32)

    first = [send("l0_w_in"), jnp.pad(args["l0_gdn_conv"], ((0, 4), (0, LANES * 2 - conv_cols)))]
    (h0, h0t), first = _tok_fwd("l0_mix_norm", _f_norm, [(x, D, 0)], [(g0m, None, 0)], [(D, bf16)], 256, also_t=(0,),
                                side=_side_allgather(first))
    take(("l0_w_in",), first[:1])
    w_in0 = _regroup_w_in0(w["l0_w_in"])
    wconv = first[1][:, :4, :conv_cols].transpose(1, 0, 2).reshape(4, NDEV * conv_cols)
    z0 = _mm("l0_in", h0, w_in0, NN, f32)
    qk_rows = [(z0, 2 * FOX_W, 0)]
    qk_consts = [(gq_t, None, 0), (gk_t, None, 0), (pm, None, 0)]
    qn, kn = _tok_fwd("fox_pre", _f_foxpre, qk_rows, qk_consts, [(FOX_W, f32)] * 2, 256)
    ccol, crow = _fox_gate_fwd(z0, fbias, t)
    fox_o, got = _fox_attn_fwd(qn, kn, z0, ccol, crow, t, side=_side_allgather([send(nm) for nm in _L0_REST]))
    take(_L0_REST, got)
    conv_rows, conv_consts = [(z0, LANES, Z0_GQKV)], [(wconv, LANES, 0)]
    (qkv,) = _tok_fwd("gdn_conv", _f_conv, conv_rows, conv_consts, [(LANES, f32)], t, ncb=12, with_j=True)
    gate_consts = [(e_beta, None, 0), (e_alpha, None, 0), (alog_b, None, 0), (dt_b, None, 0)]
    intra_rows = [(qkv, GDN_W, 0), (qkv, GDN_W, 1), (qkv, GDN_W, 2), (z0, LANES, Z0_SMALL)]
    intra, got = _tok_fwd("gdn_intra", _f_gdn_gated, intra_rows, gate_consts, [(GDN_W, f32)] * 7, 2 * CH,
                          side=_side_allgather([send(nm) for nm in _L1_MIX]))
    take(_L1_MIX, got)
    inter_rows = [(a, 0, 1) for a in intra[:6]]
    full_rows = inter_rows + [(fox_o, 0, 1), (z0, Z0_GG // 4, 1)]
    (cat0, gdn_hist), got = _scan_fwd("gdn_scan", _f_gdn_full, full_rows, GDN_H, nchunk, consts=(on0_t,), out_dtype=bf16, out_k=2,
                                      side=_side_allgather([send(nm) for nm in _L1_FFN[1:]]))
    take(_L1_FFN[1:], got)
    x1, hf0, hf0t = _mm("l0_out", cat0, w["l0_w_out"], NN, f32, epi=_epi_res_norm, extra=(x,), consts=(g0f,), outs=_RES_NORM_OUTS)
    (x2, h1, h1t), a0 = _mlp_fwd("l0", x1, hf0, w["l0_w_ff1"], w["l0_w_ff2"], _epi_res_norm, (), (g1m,), _RES_NORM_OUTS)

    z1 = _mm("l1_in", h1, w["l1_w_in"], NN, f32)
    (cat1, hg_hist), got = _scan_fwd("hgrn_scan", _f_hgrn_full, [(z1, 0, 4)], HG_H, nchunk, consts=(lbl, on1_t), out_dtype=bf16,
                                     side=_side_allgather([send(nm) for nm in _L1_FFN[:1]]))
    take(_L1_FFN[:1], got)
    x3, hf1, hf1t = _mm("l1_out", cat1, w["l1_w_out"], NN, f32, epi=_epi_res_norm, extra=(x2,), consts=(g1f,), outs=_RES_NORM_OUTS)
    (dy, dyt, loss_row), a1 = _mlp_fwd("l1", x3, hf1, w["l1_w_ff1"], w["l1_w_ff2"], _epi_loss, (tgt,), (), _LOSS_OUTS)

    (dx3, g["l1_ffn_norm"], dx3t), ga_ff1, ga_ff2 = _mlp_bwd("l1", x3, g1f, w["l1_w_ff1"], w["l1_w_ff2"], hf1t, a1, dy, dyt)
    dcat1 = _mm("l1_out_dx", dx3, w["l1_w_out"], NT, f32)
    ga_out = _mm("l1_out_dw", dx3t, cat1, NN, bf16, out_t=True)
    dz1, dlbl, don1 = _scan_bwd("hgrn_scan_bwd", _f_hgrn_full, [(z1, 0, 4)], hg_hist, dcat1, HG_H, nchunk, dtypes=[bf16],
                                consts=(lbl, on1_t))
    dx2, g["l1_mix_norm"], dx2t = _mm("l1_in_dx", dz1, w["l1_w_in"], NT, f32, epi=_epi_norm_bwd, extra=(x2, dx3), consts=(g1m,),
                                      outs=_NORM_BWD_OUTS)
    ga_in = _mm("l1_in_dw", h1t, dz1, NN, bf16, slab=True)
    g["l1_hgrn_o_norm"] = don1.reshape(HG_H, HD).sum(0)
    g["hgrn_lb_logits"] = dlbl
    gs_a = [by_dev(nm, ga) for nm, ga in zip(_GRAD_A, (ga_ff1, ga_ff2, ga_out, ga_in))]

    ((dx1, g["l0_ffn_norm"], dx1t), gb_ff1, gb_ff2), r1_a = _mlp_bwd("l0", x1, g0f, w["l0_w_ff1"], w["l0_w_ff2"], hf0t, a0, dx2, dx2t,
                                                                  side=_side_rs_sibling(gs_a))
    pairs_a = pair(_GRAD_A, gs_a, r1_a)
    dcat0 = _mm("l0_out_dx", dx1, w["l0_w_out"], NT, f32)
    gb_out = _mm("l0_out_dw", dx1t, cat0, NN, bf16, out_t=True)
    gs_b = [by_dev(nm, ga) for nm, ga in zip(_GRAD_B, (gb_ff1, gb_ff2, gb_out))]
    *dinter, dfox_o, dgg, don0 = _scan_bwd("gdn_scan_bwd", _f_gdn_full, full_rows, gdn_hist, dcat0, GDN_H, nchunk, consts=(on0_t,),
                                           out_k=2, dtypes=[f32] * 7 + [bf16])
    (dqn, dkn, dfv, dcq, dck), got = _fox_attn_bwd(qn, kn, z0, ccol, crow, dfox_o, t,
                                                   side=_join_sides([_side_rs_chips(pairs_a), _side_rs_sibling(gs_b)]))
    r2_a, r1_b = got[:len(_GRAD_A)], got[len(_GRAD_A):]
    adam(_GRAD_A, gs_a, r1_a, r2_a)
    pairs_b = pair(_GRAD_B, gs_b, r1_b)
    (dqkv, dzs_g, dalog_b, ddt_b), r2_b = _tok_bwd(
        "gdn_intra_bwd", _f_gdn_gated_given, intra_rows + [(intra[6], GDN_W, 0)], gate_consts, [(a, GDN_W) for a in dinter],
        2 * CH, [0, 1, 2, 3], [2, 3], ncat=3, side=_side_rs_chips(pairs_b))
    adam(_GRAD_B, gs_b, r1_b, r2_b)
    dgqkv, dwconv = _tok_bwd("gdn_conv_bwd", _f_conv, conv_rows, conv_consts, [(dqkv, LANES)], t, [0], [0], ncb=12, with_j=True,
                             drow_dtype=bf16)
    dzs_f, dfb = _fox_gate_bwd(z0, fbias, dcq, dck, t)
    dzqk, dgq_t, dgk_t = _tok_bwd("fox_pre_bwd", _f_foxpre, qk_rows, qk_consts, [(dqn, FOX_W), (dkn, FOX_W)], 256, [0], [0, 1],
                                  drow_dtype=bf16)
    dz0 = jnp.concatenate([dzqk, dfv.astype(bf16), dgqkv, dgg, (dzs_g + dzs_f).astype(bf16), jnp.zeros((n, ZW0 - 3712), bf16)], axis=1)
    gs_c = [_col_slabs(_ungroup_w_in0(_mm("l0_in_dw", h0t, dz0, NN, bf16)), ncols["l0_w_in"])]
    r1_c = _rs_sibling(gs_c)
    pairs_c = pair(("l0_w_in",), gs_c, r1_c)
    (dx, g["l0_mix_norm"], _), r2_c = _mm("l0_in_dx", dz0, w_in0, NT, f32, epi=_epi_norm_bwd, extra=(x, dx1), consts=(g0m,),
                                          outs=_NORM_BWD_OUTS, side=_side_rs_chips(pairs_c))
    adam(("l0_w_in",), gs_c, r1_c, r2_c)
    g["l0_fox_q_norm"] = dgq_t.reshape(FOX_H, FOX_D).sum(0)
    g["l0_fox_k_norm"] = dgk_t.reshape(FOX_H, FOX_D).sum(0)
    g["l0_fox_f_bias"] = dfb[0, :FOX_H]
    g["l0_gdn_conv"] = dwconv
    g["l0_gdn_A_log"] = dalog_b.reshape(GDN_H, HD).sum(1)
    g["l0_gdn_dt_bias"] = ddt_b.reshape(GDN_H, HD).sum(1)
    g["l0_gdn_o_norm"] = don0.reshape(GDN_H, HD).sum(0)
    return loss_row, dx, g, big


_NAMES = ("l0_mix_norm", "l0_w_in", "l0_fox_q_norm", "l0_fox_k_norm", "l0_fox_f_bias", "l0_gdn_conv", "l0_gdn_A_log",
          "l0_gdn_dt_bias", "l0_gdn_o_norm", "l0_w_out", "l0_ffn_norm", "l0_w_ff1", "l0_w_ff2", "l1_mix_norm", "l1_w_in",
          "l1_hgrn_o_norm", "l1_w_out", "l1_ffn_norm", "l1_w_ff1", "l1_w_ff2", "hgrn_lb_logits")
_SMALL_NAMES = tuple(nm for nm in _NAMES if nm not in _BIG_NAMES)
_SMALL_ROWS = 16


def _pack_small(vals):
    flat = jnp.concatenate([vals[nm].reshape(-1).astype(f32) for nm in _SMALL_NAMES])
    return jnp.pad(flat, (0, _SMALL_ROWS * D - flat.shape[0])).reshape(_SMALL_ROWS, D)


def _unpack_small(packed, shapes):
    flat = packed.reshape(-1)
    out, off = {}, 0
    for nm in _SMALL_NAMES:
        size = 1
        for s in shapes[nm]:
            size *= s
        out[nm] = flat[off:off + size].reshape(shapes[nm])
        off += size
    return out, off


def kernel(x, l0_mix_norm, l0_w_in, l0_fox_q_norm, l0_fox_k_norm, l0_fox_f_bias, l0_gdn_conv, l0_gdn_A_log, l0_gdn_dt_bias, l0_gdn_o_norm, l0_w_out, l0_ffn_norm, l0_w_ff1, l0_w_ff2, l1_mix_norm, l1_w_in, l1_hgrn_o_norm, l1_w_out, l1_ffn_norm, l1_w_ff1, l1_w_ff2, hgrn_lb_logits, loss_target, m_l0_mix_norm, m_l0_w_in, m_l0_fox_q_norm, m_l0_fox_k_norm, m_l0_fox_f_bias, m_l0_gdn_conv, m_l0_gdn_A_log, m_l0_gdn_dt_bias, m_l0_gdn_o_norm, m_l0_w_out, m_l0_ffn_norm, m_l0_w_ff1, m_l0_w_ff2, m_l1_mix_norm, m_l1_w_in, m_l1_hgrn_o_norm, m_l1_w_out, m_l1_ffn_norm, m_l1_w_ff1, m_l1_w_ff2, m_hgrn_lb_logits, v_l0_mix_norm, v_l0_w_in, v_l0_fox_q_norm, v_l0_fox_k_norm, v_l0_fox_f_bias, v_l0_gdn_conv, v_l0_gdn_A_log, v_l0_gdn_dt_bias, v_l0_gdn_o_norm, v_l0_w_out, v_l0_ffn_norm, v_l0_w_ff1, v_l0_w_ff2, v_l1_mix_norm, v_l1_w_in, v_l1_hgrn_o_norm, v_l1_w_out, v_l1_ffn_norm, v_l1_w_ff1, v_l1_w_ff2, v_hgrn_lb_logits):
    args = dict(zip(_NAMES, (l0_mix_norm, l0_w_in, l0_fox_q_norm, l0_fox_k_norm, l0_fox_f_bias, l0_gdn_conv, l0_gdn_A_log, l0_gdn_dt_bias, l0_gdn_o_norm, l0_w_out, l0_ffn_norm, l0_w_ff1, l0_w_ff2, l1_mix_norm, l1_w_in, l1_hgrn_o_norm, l1_w_out, l1_ffn_norm, l1_w_ff1, l1_w_ff2, hgrn_lb_logits)))
    mom = dict(zip(_NAMES, (m_l0_mix_norm, m_l0_w_in, m_l0_fox_q_norm, m_l0_fox_k_norm, m_l0_fox_f_bias, m_l0_gdn_conv, m_l0_gdn_A_log, m_l0_gdn_dt_bias, m_l0_gdn_o_norm, m_l0_w_out, m_l0_ffn_norm, m_l0_w_ff1, m_l0_w_ff2, m_l1_mix_norm, m_l1_w_in, m_l1_hgrn_o_norm, m_l1_w_out, m_l1_ffn_norm, m_l1_w_ff1, m_l1_w_ff2, m_hgrn_lb_logits)))
    var = dict(zip(_NAMES, (v_l0_mix_norm, v_l0_w_in, v_l0_fox_q_norm, v_l0_fox_k_norm, v_l0_fox_f_bias, v_l0_gdn_conv, v_l0_gdn_A_log, v_l0_gdn_dt_bias, v_l0_gdn_o_norm, v_l0_w_out, v_l0_ffn_norm, v_l0_w_ff1, v_l0_w_ff2, v_l1_mix_norm, v_l1_w_in, v_l1_hgrn_o_norm, v_l1_w_out, v_l1_ffn_norm, v_l1_w_ff1, v_l1_w_ff2, v_hgrn_lb_logits)))
    nb, t, _ = x.shape
    dev = 4 * lax.axis_index("x") + 2 * lax.axis_index("y") + lax.axis_index("c")
    conv_cols = l0_gdn_conv.shape[1]
    loss_row, dx, g, big = _train(x.reshape(nb * t, D), loss_target.reshape(nb * t, D), args, mom, var, t)

    shapes = {nm: args[nm].shape for nm in _SMALL_NAMES}
    gsm = dict(g)
    gsm["l0_gdn_conv"] = jnp.zeros(shapes["l0_gdn_conv"], f32)
    packed = _pack_small(gsm)
    _, used = _unpack_small(packed, shapes)
    flat_extra = jnp.concatenate([jnp.sum(loss_row).reshape(1), g["l0_gdn_conv"].reshape(-1)])
    packed = packed.reshape(-1).at[used:used + flat_extra.shape[0]].set(flat_extra).reshape(_SMALL_ROWS, D)
    (parts,) = _allgather("ag_small", [packed])
    total = _sum_parts(parts).reshape(-1)
    loss = 0.5 * total[used] / D
    conv_g_full = total[used + 1:used + 1 + 4 * NDEV * conv_cols].reshape(4, NDEV * conv_cols)
    conv_g = lax.dynamic_slice(conv_g_full, (0, dev * conv_cols), (4, conv_cols))
    own_vals = {nm: jnp.zeros(shapes[nm], f32) for nm in _SMALL_NAMES}
    own_vals["l0_gdn_conv"] = conv_g
    own_mask = {nm: jnp.zeros(shapes[nm], f32) for nm in _SMALL_NAMES}
    own_mask["l0_gdn_conv"] = jnp.ones(shapes["l0_gdn_conv"], f32)
    small = _adamw_small(parts, _pack_small(args), _pack_small(mom), _pack_small(var), _pack_small(own_mask), _pack_small(own_vals))
    small = [_unpack_small(a, shapes)[0] for a in small]
    small[0]["l0_gdn_conv"] = conv_g

    outs = [loss, dx.reshape(nb, t, D)]
    for k in range(4):
        outs += [big[nm][k] if nm in _BIG_NAMES else small[k][nm] for nm in _NAMES]
    return tuple(outs)
```

```python
import functools

import jax
import jax.numpy as jnp
from jax import lax
from jax.experimental import pallas as pl
from jax.experimental.pallas import tpu as pltpu

f32, bf16 = jnp.float32, jnp.bfloat16
NN = (((1,), (0,)), ((), ()))
NT = (((1,), (1,)), ((), ()))
TN = (((0,), (0,)), ((), ()))
HI = lax.Precision.HIGHEST
MESH = pl.DeviceIdType.MESH
S = jax.ShapeDtypeStruct

EPS = 1e-6
D = 1024
LANES = 128
FOX_H, FOX_D, FOX_W = 8, 64, 512
GDN_H, HD, GDN_W = 4, 128, 512
HG_H = 8
CH = 64
ZW0 = 3840
NDEV = 8
ADAM_LR, ADAM_B1, ADAM_B2, ADAM_EPS, ADAM_WD, ADAM_STEP = 0.001, 0.9, 0.999, 1e-08, 0.01, 10

Z0_FQK, Z0_FV, Z0_GQKV, Z0_GG, Z0_SMALL = 0, 8, 12, 24, 28


def _dot(a, b, dims=NN, prec=None):
    return lax.dot_general(a, b, dims, precision=prec, preferred_element_type=f32)


def _iota2(shape, axis):
    return lax.broadcasted_iota(jnp.int32, shape, axis)


def _split3(x):
    x1 = x.astype(bf16)
    r = x - x1.astype(f32)
    x2 = r.astype(bf16)
    return x1, x2, (r - x2.astype(f32)).astype(bf16)


def _dot_sel(a, b, dims=NN, exact_lhs=False):
    if exact_lhs:
        return sum(_dot(a.astype(bf16), piece, dims) for piece in _split3(b))
    return sum(_dot(piece, b.astype(bf16), dims) for piece in _split3(a))


@jax.custom_vjp
def _sel_rhs(a, b):
    return _dot_sel(a, b)


_sel_rhs.defvjp(lambda a, b: (_dot_sel(a, b), b), lambda b, g: (_dot_sel(g, b, NT), jnp.zeros_like(b)))


@jax.custom_vjp
def _sel_lhs(a, x):
    return _dot_sel(a, x, exact_lhs=True)


_sel_lhs.defvjp(lambda a, x: (_dot_sel(a, x, exact_lhs=True), a), lambda a, g: (jnp.zeros_like(a), _dot_sel(a, g, TN, exact_lhs=True)))


class _Side:
    def __init__(self, ins, out_shapes, scratch, start, finish):
        self.ins, self.out_shapes, self.scratch, self.start, self.finish = list(ins), list(out_shapes), list(scratch), start, finish


def _join_sides(sides):
    def split(refs, counts):
        out, off = [], 0
        for c in counts:
            out.append(refs[off:off + c])
            off += c
        return out

    ni, no, ns = ([len(getattr(sd, a)) for sd in sides] for a in ("ins", "out_shapes", "scratch"))

    def run(which):
        def go(ins, outs, sems):
            for sd, i, o, c in zip(sides, split(ins, ni), split(outs, no), split(sems, ns)):
                getattr(sd, which)(i, o, c)
        return go

    return _Side(sum((sd.ins for sd in sides), []), sum((sd.out_shapes for sd in sides), []),
                 sum((sd.scratch for sd in sides), []), run("start"), run("finish"))


def _pcall(side, body, *, name, grid, in_specs, out_specs, out_shape, scratch_shapes=(), compiler_params=None):
    if side is None:
        return pl.pallas_call(body, name=name, grid=grid, in_specs=in_specs, out_specs=out_specs, out_shape=out_shape,
                              scratch_shapes=scratch_shapes, compiler_params=compiler_params)
    single = not isinstance(out_shape, (list, tuple))
    ospecs, oshape = ([out_specs], [out_shape]) if single else (list(out_specs), list(out_shape))
    nin, nout, nscr = len(in_specs), len(ospecs), len(scratch_shapes)
    si, so = len(side.ins), len(side.out_shapes)

    def wrapped(*refs):
        o0 = nin + si
        c0 = o0 + nout + so
        sins, souts, ssems = refs[nin:o0], refs[o0 + nout:c0], refs[c0 + nscr:]
        ids = [pl.program_id(a) for a in range(len(grid))]
        first = functools.reduce(jnp.logical_and, [i == 0 for i in ids])
        last = functools.reduce(jnp.logical_and, [i == g - 1 for i, g in zip(ids, grid)])

        @pl.when(first)
        def _():
            side.start(sins, souts, ssems)

        body(*refs[:nin], *refs[o0:o0 + nout], *refs[c0:c0 + nscr])

        @pl.when(last)
        def _():
            side.finish(sins, souts, ssems)

    call = pl.pallas_call(
        wrapped, name=name, grid=grid, in_specs=list(in_specs) + _hbm_specs(si), out_specs=ospecs + _hbm_specs(so),
        out_shape=oshape + side.out_shapes, scratch_shapes=list(scratch_shapes) + side.scratch,
        compiler_params=pltpu.CompilerParams(dimension_semantics=("arbitrary",) * len(grid),
                                             vmem_limit_bytes=getattr(compiler_params, "vmem_limit_bytes", None)))

    def run(*args):
        res = call(*args, *side.ins)
        return (res[0] if single else list(res[:nout])), list(res[nout:])

    return run


def _tok_specs(rows, consts, tm):
    specs = []
    for (_, w, base) in rows:
        specs.append(pl.BlockSpec((tm, w), functools.partial(lambda j, i, b: (i, b + j), b=base)))
    for (arr, w, base) in consts:
        if w is None:
            specs.append(pl.BlockSpec(arr.shape, lambda j, i: (0, 0)))
        else:
            specs.append(pl.BlockSpec((arr.shape[0], w), functools.partial(lambda j, i, b: (0, b + j), b=base)))
    return specs


def _tok_fwd(name, f, rows, consts, outs, tm, ncb=1, with_j=False, also_t=(), side=None):
    n = rows[0][0].shape[0]
    nin = len(rows) + len(consts)
    nout = len(outs)

    def body(*refs):
        ins = [r[...] for r in refs[:nin]]
        vals = f(pl.program_id(0), *ins) if with_j else f(*ins)
        for r, v in zip(refs[nin:nin + nout], vals):
            r[...] = v.astype(r.dtype)
        for r, k in zip(refs[nin + nout:], also_t):
            r[...] = vals[k].T.astype(r.dtype)

    return _pcall(
        side, body, name=name, grid=(ncb, n // tm),
        in_specs=_tok_specs(rows, consts, tm),
        out_specs=[pl.BlockSpec((tm, w), lambda j, i: (i, j)) for (w, _) in outs]
        + [pl.BlockSpec((outs[k][0], tm), lambda j, i: (0, i)) for k in also_t],
        out_shape=[S((n, w * ncb), dt) for (w, dt) in outs] + [S((outs[k][0], n), bf16) for k in also_t],
        compiler_params=pltpu.CompilerParams(dimension_semantics=("parallel", "parallel")),
    )(*[r[0] for r in rows], *[c[0] for c in consts])


def _tok_bwd(name, f, rows, consts, cots, tm, drow, dconst, ncb=1, with_j=False, addto=None, also_t=(), drow_dtype=f32, side=None, ncat=0):
    n = rows[0][0].shape[0]
    nr, nc, nct = len(rows), len(consts), len(cots)
    addto = addto or {}
    add_keys = sorted(addto)
    nadd = len(add_keys)

    def body(*refs):
        ins = [r[...] for r in refs[:nr + nc]]
        cot = [r[...] for r in refs[nr + nc:nr + nc + nct]]
        adds = refs[nr + nc + nct:nr + nc + nct + nadd]
        outs = refs[nr + nc + nct + nadd:]
        pos = list(drow) + [nr + k for k in dconst]

        def g(*dargs):
            full = list(ins)
            for p, a in zip(pos, dargs):
                full[p] = a
            return tuple(f(pl.program_id(0), *full) if with_j else f(*full))

        vals, vjp = jax.vjp(g, *[ins[p] for p in pos])
        grads = vjp(tuple(c.astype(v.dtype) for c, v in zip(cot, vals)))
        off = 0
        for k in range(len(drow)):
            gk = grads[k]
            if k in addto:
                gk = gk + adds[add_keys.index(k)][...]
            if k < ncat:
                outs[0][:, off:off + gk.shape[1]] = gk.astype(outs[0].dtype)
                off += gk.shape[1]
            else:
                outs[k - skip][...] = gk.astype(outs[k - skip].dtype)
            if k in also_t:
                tref = outs[len(drow) - skip + len(dconst) + list(also_t).index(k)]
                tref[...] = gk.T.astype(tref.dtype)
        first = pl.program_id(1) == 0
        for k in range(len(dconst)):
            ref = outs[len(drow) - skip + k]

            @pl.when(first)
            def _():
                ref[...] = jnp.zeros_like(ref)

            ref[...] += grads[len(drow) + k]

    skip = max(ncat - 1, 0)
    in_specs = _tok_specs(rows, consts, tm)
    in_specs += [pl.BlockSpec((tm, w), lambda j, i: (i, j)) for (_, w) in cots]
    in_specs += [pl.BlockSpec((tm, rows[drow[k]][1]), lambda j, i: (i, j)) for k in add_keys]
    dts = drow_dtype if isinstance(drow_dtype, (list, tuple)) else [drow_dtype] * len(drow)
    widths = [rows[k][1] for k in drow]
    if ncat:
        widths, dts = [sum(widths[:ncat])] + widths[ncat:], [dts[0]] + list(dts[ncat:])
    out_specs = [pl.BlockSpec((tm, wd), lambda j, i: (i, j)) for wd in widths]
    out_shape = [S((n, wd * ncb), dt) for wd, dt in zip(widths, dts)]
    for k in dconst:
        arr, w, _ = consts[k]
        if w is None:
            out_specs.append(pl.BlockSpec(arr.shape, lambda j, i: (0, 0)))
            out_shape.append(S(arr.shape, f32))
        else:
            out_specs.append(pl.BlockSpec((arr.shape[0], w), lambda j, i: (0, j)))
            out_shape.append(S((arr.shape[0], w * ncb), f32))
    for k in also_t:
        out_specs.append(pl.BlockSpec((rows[drow[k]][1], tm), lambda j, i: (0, i)))
        out_shape.append(S((rows[drow[k]][1], n), bf16))
    return _pcall(
        side, body, name=name, grid=(ncb, n // tm), in_specs=in_specs, out_specs=out_specs, out_shape=out_shape,
        compiler_params=pltpu.CompilerParams(dimension_semantics=("parallel", "arbitrary")),
    )(*[r[0] for r in rows], *[c[0] for c in consts], *[c[0] for c in cots], *[addto[k] for k in add_keys])


def _scan_fwd(name, f, rows, nh, nchunk, side=None, consts=(), out_dtype=f32, out_k=1):
    n = rows[0][0].shape[0]
    nb = n // (CH * nchunk)
    nin, nco = len(rows), len(consts)
    w = nh * HD

    def body(*refs):
        o_ref, hist_ref, st = refs[nin + nco], refs[nin + nco + 1], refs[nin + nco + 2]

        @pl.when(pl.program_id(0) == 0)
        def _():
            st[...] = jnp.zeros_like(st)

        s0 = st[...]
        hist_ref[0] = s0.astype(hist_ref.dtype)
        tiles = [r[...].reshape(nb * CH, r.shape[2]) for r in refs[:nin]]
        o, s1 = f(*tiles, *[r[...] for r in refs[nin:nin + nco]], s0)
        o_ref[...] = o.reshape(o_ref.shape).astype(o_ref.dtype)
        st[...] = s1

    seq3 = lambda a: a.reshape(nb, nchunk * CH, a.shape[1])
    res = _pcall(
        side, body, name=name, grid=(nchunk,),
        in_specs=[pl.BlockSpec((nb, CH, k * w), functools.partial(lambda c, base: (0, c, base), base=b)) for (_, b, k) in rows]
        + [pl.BlockSpec(c.shape, lambda c: (0, 0)) for c in consts],
        out_specs=[pl.BlockSpec((nb, CH, out_k * w), lambda c: (0, c, 0)), pl.BlockSpec((1, nb * w, HD), lambda c: (c, 0, 0))],
        out_shape=[S((nb, nchunk * CH, out_k * w), out_dtype), S((nchunk, nb * w, HD), bf16)],
        scratch_shapes=[pltpu.VMEM((nb * w, HD), f32)],
        compiler_params=pltpu.CompilerParams(dimension_semantics=("arbitrary",)),
    )(*[seq3(r[0]) for r in rows], *consts)
    (o, hist), extra = (res, None) if side is None else res
    out = [o.reshape(n, out_k * w), hist]
    return out if side is None else (out, extra)


def _scan_bwd(name, f, rows, hist, do, nh, nchunk, side=None, dtypes=None, consts=(), out_k=1):
    n = rows[0][0].shape[0]
    nb = n // (CH * nchunk)
    nin, nco = len(rows), len(consts)
    w = nh * HD

    def body(*refs):
        hist_ref, do_ref = refs[nin + nco], refs[nin + nco + 1]
        outs = refs[nin + nco + 2:nin + nco + 2 + nin]
        couts = refs[nin + nco + 2 + nin:nin + nco + 2 + nin + nco]
        ds = refs[nin + nco + 2 + nin + nco]

        @pl.when(pl.program_id(0) == 0)
        def _():
            ds[...] = jnp.zeros_like(ds)
            for c in couts:
                c[...] = jnp.zeros_like(c)

        tiles = [r[...].reshape(nb * CH, r.shape[2]) for r in refs[:nin]]
        _, vjp = jax.vjp(f, *tiles, *[r[...] for r in refs[nin:nin + nco]], hist_ref[0].astype(f32))
        grads = vjp((do_ref[...].reshape(nb * CH, out_k * w), ds[...]))
        for r, gk in zip(outs, grads[:nin]):
            r[...] = gk.reshape(r.shape).astype(r.dtype)
        for c, gk in zip(couts, grads[nin:nin + nco]):
            c[...] += gk
        ds[...] = grads[nin + nco]

    seq3 = lambda a: a.reshape(nb, nchunk * CH, a.shape[1])
    rev = lambda c, base: (0, nchunk - 1 - c, base)
    res = _pcall(
        side, body, name=name, grid=(nchunk,),
        in_specs=[pl.BlockSpec((nb, CH, k * w), functools.partial(rev, base=b)) for (_, b, k) in rows]
        + [pl.BlockSpec(c.shape, lambda c: (0, 0)) for c in consts]
        + [pl.BlockSpec((1, nb * w, HD), lambda c: (nchunk - 1 - c, 0, 0)), pl.BlockSpec((nb, CH, out_k * w), functools.partial(rev, base=0))],
        out_specs=[pl.BlockSpec((nb, CH, k * w), functools.partial(rev, base=0)) for (_, _, k) in rows]
        + [pl.BlockSpec(c.shape, lambda c: (0, 0)) for c in consts],
        out_shape=[S((nb, nchunk * CH, k * w), dt) for (_, _, k), dt in zip(rows, dtypes or [f32] * nin)]
        + [S(c.shape, f32) for c in consts],
        scratch_shapes=[pltpu.VMEM((nb * w, HD), f32)],
        compiler_params=pltpu.CompilerParams(dimension_semantics=("arbitrary",)),
    )(*[seq3(r[0]) for r in rows], *consts, hist, seq3(do))
    outs, extra = (res, None) if side is None else res
    outs = [o.reshape(n, o.shape[2]) for o in outs[:nin]] + list(outs[nin:])
    return outs if side is None else (outs, extra)


_VMEM_LIMIT = 56 * 2 ** 20
_VMEM_TILE_BUDGET = 40 * 2 ** 20


def _mm_tiles(m, n, k, sa, sb, so, sx, a_f32, b_f32, tn_fixed):
    best = None
    for tm in (1024, 512, 256, 128, 64):
        for tn in ((tn_fixed,) if tn_fixed else (1024, 768, 512, 384, 256, 128)):
            if m % tm or n % tn:
                continue
            need = 2 * (tm * k * sa + k * tn * sb + tm * tn * (so + sx)) + tm * tn * 4
            need += tm * k * (2 if sa == 4 else 0) + k * tn * (2 if sb == 4 else 0)
            need += tm * k * (4 if a_f32 else 0) + k * tn * (4 if b_f32 else 0)
            if need <= _VMEM_TILE_BUDGET and (best is None or (tm * tn, tm) > best[0]):
                best = ((tm * tn, tm), tm, tn)
    return best[1], best[2]


def _mm(name, a, b, dims, out_dtype, a_fn=None, b_fn=None, epi=None, extra=(), consts=(), outs=None, out_t=False, slab=False,
        side=None):
    m, kk = a.shape
    gathered = b.ndim == 3
    if gathered:
        nn = NDEV * _COLW if dims is NN else b.shape[1]
    else:
        nn = b.shape[1] if dims is NN else b.shape[0]
    kinds = [("tile", out_dtype)] if outs is None else list(outs)
    so = sum(jnp.dtype(dt).itemsize for kd, dt in kinds if kd != "rows")
    sx = sum(e.dtype.itemsize for e in extra)
    full_rows = bool(consts) or any(kd == "rows" for kd, _ in kinds)
    tm, tn = _mm_tiles(m, nn, kk, a.dtype.itemsize, b.dtype.itemsize, so, sx,
                       a_fn is not None, b_fn is not None,
                       _COLW if (slab or (gathered and dims is NN)) else (nn if full_rows else None))
    nex, nco = len(extra), len(consts)

    def body(a_ref, b_ref, *rest):
        av = a_ref[...]
        if a_fn is not None:
            av = a_fn(av.astype(f32))
        av = av.astype(bf16)
        if gathered and dims is NT:
            acc = sum(_dot(av[:, _COLW * d:_COLW * (d + 1)], b_ref[d], NT) for d in range(NDEV))
        else:
            bv = b_ref[0] if gathered else b_ref[...]
            if b_fn is not None:
                bv = b_fn(bv.astype(f32))
            acc = _dot(av, bv.astype(bf16), dims)
        if epi is not None:
            acc = epi(acc, *[r[...] for r in rest[:nex + nco]])
        vals = acc if isinstance(acc, tuple) else (acc,)
        for (kd, _), o_ref, val in zip(kinds, rest[nex + nco:], vals):
            if kd == "rows":
                @pl.when(pl.program_id(0) == 0)
                def _():
                    o_ref[...] = jnp.zeros_like(o_ref)

                o_ref[...] += val
            elif kd == "tile_t" or out_t:
                o_ref[...] = val.T.astype(o_ref.dtype)
            elif slab:
                o_ref[0] = val.astype(o_ref.dtype)
            else:
                o_ref[...] = val.astype(o_ref.dtype)

    if gathered:
        bspec = (pl.BlockSpec((1, kk, tn), lambda i, j: (j, 0, 0)) if dims is NN
                 else pl.BlockSpec((NDEV, tn, _COLW), lambda i, j: (0, j, 0)))
    else:
        bspec = pl.BlockSpec((kk, tn), lambda i, j: (0, j)) if dims is NN else pl.BlockSpec((tn, kk), lambda i, j: (j, 0))
    out_specs, out_shape = [], []
    for kd, dt in kinds:
        if kd == "rows":
            out_specs.append(pl.BlockSpec((1, nn), lambda i, j: (0, 0)))
            out_shape.append(S((1, nn), dt))
        elif kd == "tile_t" or out_t:
            out_specs.append(pl.BlockSpec((tn, tm), lambda i, j: (j, i)))
            out_shape.append(S((nn, m), dt))
        elif slab:
            out_specs.append(pl.BlockSpec((1, tm, tn), lambda i, j: (j, i, 0)))
            out_shape.append(S((nn // tn, m, tn), dt))
        else:
            out_specs.append(pl.BlockSpec((tm, tn), lambda i, j: (i, j)))
            out_shape.append(S((m, nn), dt))
    if outs is None:
        out_specs, out_shape = out_specs[0], out_shape[0]
    sem = ("arbitrary", "arbitrary") if any(kd == "rows" for kd, _ in kinds) else ("parallel", "parallel")
    return _pcall(
        side, body, name=name, grid=(m // tm, nn // tn),
        in_specs=[pl.BlockSpec((tm, kk), lambda i, j: (i, 0)), bspec]
        + [pl.BlockSpec((tm, tn), lambda i, j: (i, j)) for _ in extra]
        + [pl.BlockSpec(c.shape, lambda i, j: (0, 0)) for c in consts],
        out_specs=out_specs, out_shape=out_shape,
        compiler_params=pltpu.CompilerParams(dimension_semantics=sem, vmem_limit_bytes=_VMEM_LIMIT),
    )(a, b, *extra, *consts)


def _f_norm(x, g):
    return (x * lax.rsqrt(jnp.mean(x * x, axis=-1, keepdims=True) + EPS) * g,)


def _f_foxpre(zqk, gq, gk, pm):
    def nrm(t, g):
        return t * lax.rsqrt(_sel_rhs(t * t, pm) + EPS) * g
    return nrm(zqk[:, :FOX_W], gq), nrm(zqk[:, FOX_W:], gk)


def _chunk_cumsum(x):
    n = x.shape[0]
    r, c = _iota2((n, n), 0), _iota2((n, n), 1)
    tri = jnp.logical_and(r >= c, (r // CH) == (c // CH)).astype(f32)
    return _sel_lhs(tri, x)


def _f_gdngate(zs, eb, ea, alog_b, dt_b):
    beta = jax.nn.sigmoid(_sel_rhs(zs, eb))
    la = -jnp.exp(alog_b) * jax.nn.softplus(_sel_rhs(zs, ea) + dt_b)
    return beta, _chunk_cumsum(la)


def _f_conv(j, x, w):
    t = x.shape[0]
    y = x * w[3:4, :]
    for jj in range(3):
        sh = 3 - jj
        xs = jnp.concatenate([jnp.zeros((sh, x.shape[1]), f32), x[:t - sh, :]], axis=0)
        y = y + xs * w[jj:jj + 1, :]
    y = jax.nn.silu(y)
    yn = y * lax.rsqrt(jnp.sum(y * y, axis=-1, keepdims=True) + EPS)
    return (jnp.where(j < 2 * GDN_H, yn, y),)


def _head_rms(o, nh):
    outs = []
    for h in range(nh):
        oh = o[:, HD * h:HD * (h + 1)]
        outs.append(oh * lax.rsqrt(jnp.mean(oh * oh, axis=-1, keepdims=True) + EPS))
    return jnp.concatenate(outs, axis=1)


def _f_post0(fox_o, o, gg, on):
    return (jnp.concatenate([fox_o, _head_rms(o, GDN_H) * on * jax.nn.silu(gg)], axis=1),)


def _f_post1(o, zg, on):
    return (_head_rms(o, HG_H) * on * jax.nn.silu(zg),)


def _f_hpre(zqf, lbl):
    lb = jax.nn.sigmoid(lbl[1:2, :] - lbl[0:1, :])
    fg = lb + (1.0 - lb) * jax.nn.sigmoid(zqf[:, D:])
    return jax.nn.silu(zqf[:, :D]), 1.0 - fg, _chunk_cumsum(jnp.log(fg))


def _dotb(a, b, dims=NN):
    return _dot(a.astype(bf16), b.astype(bf16), dims)


def _dot3(a, b):
    ah, bh = a.astype(bf16), b.astype(bf16)
    al, bl = (a - ah.astype(f32)).astype(bf16), (b - bh.astype(f32)).astype(bf16)
    return _dot(ah, bh) + (_dot(ah, bl) + _dot(al, bh))


def _split(t, nh):
    return [t[CH * ck:CH * (ck + 1), HD * h:HD * (h + 1)] for ck in range(t.shape[0] // CH) for h in range(nh)]


def _merge(units, nh):
    return jnp.concatenate([jnp.concatenate(units[i:i + nh], axis=1) for i in range(0, len(units), nh)], axis=0)


def _inv_impl(amats):
    n = amats[0].shape[0]
    eye = jnp.where(_iota2((n, n), 0) == _iota2((n, n), 1), 1.0, 0.0).astype(f32)
    xs, ps = [eye - a for a in amats], list(amats)
    for _ in range(max(1, (n - 1).bit_length()) - 1):
        ps = [_dotb(p, p) for p in ps]
        xs = [x + _dotb(x, p) for x, p in zip(xs, ps)]
    for _ in range(3):
        rs = [eye - x - _dot3(a, x) for a, x in zip(amats, xs)]
        xs = [x + _dotb(x, r) for x, r in zip(xs, rs)]
    return tuple(xs)


@jax.custom_vjp
def _inv_unit_lower(amats):
    return _inv_impl(amats)


def _inv_fwd(amats):
    xs = _inv_impl(amats)
    return xs, xs


def _inv_bwd(xs, dxs):
    return (tuple(-_dotb(_dotb(x, dx, TN), x, NT) for x, dx in zip(xs, dxs)),)


_inv_unit_lower.defvjp(_inv_fwd, _inv_bwd)


@jax.custom_vjp
def _inv_given(amats, xs):
    return xs


def _inv_given_fwd(amats, xs):
    return xs, xs


def _inv_given_bwd(xs, dxs):
    return _inv_bwd(xs, dxs)[0], tuple(jnp.zeros_like(x) for x in xs)


_inv_given.defvjp(_inv_given_fwd, _inv_given_bwd)


def _f_gdn_intra(q, k, v, bb, gb, tinv_p=None):
    qs, ks, vs, bs, gs = (_split(t, GDN_H) for t in (q, k, v, bb, gb))
    r, cc = _iota2((CH, CH), 0), _iota2((CH, CH), 1)
    causal, strict = r >= cc, r > cc
    beta, g, gl = [b[:, :1] for b in bs], [x[:, :1] for x in gs], [x[CH - 1:CH, :1] for x in gs]
    decay = [jnp.exp(jnp.where(causal, x[:, :CH] - x[:, :CH].T, -jnp.inf)) for x in gs]
    kb = [ki * bi for ki, bi in zip(ks, beta)]
    amat = [jnp.where(strict, _dotb(kbi, ki, NT) * di, 0.0) for kbi, ki, di in zip(kb, ks, decay)]
    if tinv_p is None:
        tinv = _inv_unit_lower(tuple(amat))
    else:
        tinv = _inv_given(tuple(amat), tuple(x[:, :CH] for x in _split(tinv_p, GDN_H)))
    rhs = [jnp.concatenate([vi * bi, kbi * jnp.exp(gi)], axis=1) for vi, bi, kbi, gi in zip(vs, beta, kb, g)]
    uw = [_dotb(ti, ri) for ti, ri in zip(tinv, rhs)]
    qsc = [qi * (HD ** -0.5) for qi in qs]
    qk = [jnp.where(causal, _dotb(qi, ki, NT) * di, 0.0) for qi, ki, di in zip(qsc, ks, decay)]
    outs = ([x[:, :HD] for x in uw], [x[:, HD:] for x in uw],
            [jnp.concatenate([x, jnp.zeros_like(x)], axis=1) for x in qk],
            [qi * jnp.exp(gi) for qi, gi in zip(qsc, g)],
            [ki * jnp.exp(gli - gi) for ki, gli, gi in zip(ks, gl, g)],
            [jnp.broadcast_to(gli, (CH, HD)) for gli in gl])
    if tinv_p is None:
        outs += ([jnp.concatenate([x, jnp.zeros_like(x)], axis=1) for x in tinv],)
    return tuple(_merge(o, GDN_H) for o in outs)


def _f_gdn_gated(q, k, v, zs, eb, ea, alog_b, dt_b):
    return _f_gdn_intra(q, k, v, *_f_gdngate(zs, eb, ea, alog_b, dt_b))


def _f_gdn_gated_given(q, k, v, zs, tinv_p, eb, ea, alog_b, dt_b):
    return _f_gdn_intra(q, k, v, *_f_gdngate(zs, eb, ea, alog_b, dt_b), tinv_p=tinv_p)


def _f_gdn_inter(u, w, qkp, qd, kd, glb, st):
    us, ws, qks, qds, kds, gls = (_split(t, GDN_H) for t in (u, w, qkp, qd, kd, glb))
    sts = [st[HD * i:HD * (i + 1), :] for i in range(len(us))]
    vn = [ui - _dotb(wi, si) for ui, wi, si in zip(us, ws, sts)]
    o = [_dotb(qi, si) + _dotb(xi[:, :CH], vi) for qi, si, xi, vi in zip(qds, sts, qks, vn)]
    s2 = [si * jnp.exp(gi[:1, :1]) + _dotb(ki, vi, TN) for si, gi, ki, vi in zip(sts, gls, kds, vn)]
    return _merge(o, GDN_H), jnp.concatenate(s2, axis=0)


def _f_gdn_full(u, w, qkp, qd, kd, glb, fox_o, gg, on, st):
    o, s2 = _f_gdn_inter(u, w, qkp, qd, kd, glb, st)
    return _f_post0(fox_o, o, gg, on)[0], s2


def _f_hgrn_chunk(q, k, b, v, st):
    qs, ks, bs, vs = (_split(t, HG_H) for t in (q, k, b, v))
    sts = [st[HD * i:HD * (i + 1), :] for i in range(len(qs))]
    causal = _iota2((CH, CH), 0) >= _iota2((CH, CH), 1)
    bl, bm = [x[CH - 1:CH, :] for x in bs], [x[CH // 2 - 1:CH // 2, :] for x in bs]
    a = [jnp.where(causal, _dotb(qi * jnp.exp(bi - mi), ki * jnp.exp(mi - bi), NT), 0.0)
         for qi, ki, bi, mi in zip(qs, ks, bs, bm)]
    o = [_dotb(qi * jnp.exp(bi), si, NT) + _dotb(ai, vi) for qi, bi, si, ai, vi in zip(qs, bs, sts, a, vs)]
    s2 = [si * jnp.exp(li) + _dotb(vi, ki * jnp.exp(li - bi), TN) for si, li, vi, ki, bi in zip(sts, bl, vs, ks, bs)]
    return _merge(o, HG_H), jnp.concatenate(s2, axis=0)


def _f_hgrn_full(z, lbl, on, st):
    o, s2 = _f_hgrn_chunk(*_f_hpre(z[:, :2 * D], lbl), z[:, 2 * D:3 * D], st)
    return _f_post1(o, z[:, 3 * D:], on)[0], s2


def _fox_gate_fwd(z0, fbias, t, tc=256):
    n = z0.shape[0]
    nt = t // tc

    def body(zs_ref, b_ref, ccol_ref, crow_ref, carry):
        @pl.when(pl.program_id(1) == 0)
        def _():
            carry[...] = jnp.zeros_like(carry)

        ls = jnp.where(_iota2((tc, LANES), 1) < FOX_H, jax.nn.log_sigmoid(zs_ref[...] + b_ref[...]), 0.0)
        tri = (_iota2((tc, tc), 0) >= _iota2((tc, tc), 1)).astype(f32)
        c = _dot_sel(tri, ls, exact_lhs=True) + carry[...]
        carry[...] = c[tc - 1:tc, :]
        ccol_ref[...] = c
        crow_ref[0] = c.T[:FOX_H, :]

    return pl.pallas_call(
        body, name="fox_gate_fwd", grid=(n // t, nt),
        in_specs=[pl.BlockSpec((tc, LANES), lambda b, i: (b * nt + i, Z0_SMALL)), pl.BlockSpec((1, LANES), lambda b, i: (0, 0))],
        out_specs=[pl.BlockSpec((tc, LANES), lambda b, i: (b * nt + i, 0)), pl.BlockSpec((1, FOX_H, tc), lambda b, i: (b, 0, i))],
        out_shape=[S((n, LANES), f32), S((n // t, FOX_H, t), f32)],
        scratch_shapes=[pltpu.VMEM((1, LANES), f32)],
        compiler_params=pltpu.CompilerParams(dimension_semantics=("parallel", "arbitrary")),
    )(z0, fbias)


def _fox_gate_bwd(z0, fbias, dcq, dck, t, tc=256):
    n = z0.shape[0]
    nt = t // tc

    def body(zs_ref, b_ref, dcq_ref, dck_ref, dz_ref, db_ref, carry):
        first = jnp.logical_and(pl.program_id(0) == 0, pl.program_id(1) == 0)

        @pl.when(pl.program_id(1) == 0)
        def _():
            carry[...] = jnp.zeros_like(carry)

        @pl.when(first)
        def _():
            db_ref[...] = jnp.zeros_like(db_ref)

        dc = dcq_ref[0] + dcq_ref[1] + dcq_ref[2] + dcq_ref[3]
        drow = dck_ref[0, 0] + dck_ref[1, 0] + dck_ref[2, 0] + dck_ref[3, 0]
        eye = (_iota2((FOX_H, LANES), 0) == _iota2((FOX_H, LANES), 1)).astype(f32)
        dc = dc + _dot_sel(drow, eye, TN)
        triu = (_iota2((tc, tc), 0) <= _iota2((tc, tc), 1)).astype(f32)
        dls = _dot_sel(triu, dc, exact_lhs=True) + carry[...]
        carry[...] = dls[0:1, :]
        x = zs_ref[...] + b_ref[...]
        dz = jnp.where(_iota2((tc, LANES), 1) < FOX_H, dls * jax.nn.sigmoid(-x), 0.0)
        dz_ref[...] = dz
        db_ref[...] += jnp.sum(dz, axis=0, keepdims=True)

    def rev(b, i):
        return b * nt + (nt - 1 - i)

    return pl.pallas_call(
        body, name="fox_gate_bwd", grid=(n // t, nt),
        in_specs=[pl.BlockSpec((tc, LANES), lambda b, i: (rev(b, i), Z0_SMALL)), pl.BlockSpec((1, LANES), lambda b, i: (0, 0)),
                  pl.BlockSpec((4, tc, LANES), lambda b, i: (0, rev(b, i), 0)),
                  pl.BlockSpec((4, 1, FOX_H, tc), lambda b, i: (0, b, 0, nt - 1 - i))],
        out_specs=[pl.BlockSpec((tc, LANES), lambda b, i: (rev(b, i), 0)), pl.BlockSpec((1, LANES), lambda b, i: (0, 0))],
        out_shape=[S((n, LANES), f32), S((1, LANES), f32)],
        scratch_shapes=[pltpu.VMEM((1, LANES), f32)],
        compiler_params=pltpu.CompilerParams(dimension_semantics=("arbitrary", "arbitrary")),
    )(z0, fbias, dcq, dck)


def _fox_scores(hh, p, i, tq, q, k, ccol, crow):
    kmax = k.shape[0]
    lane = _iota2((1, LANES), 1)
    mh = (lane // FOX_D) == hh
    h = 2 * p + hh
    qh = jnp.where(mh, q, 0.0).astype(bf16)
    s = _dot(qh, k, NT) * (FOX_D ** -0.5)
    cq = jnp.sum(jnp.where(lane == h, ccol, 0.0), axis=1, keepdims=True)
    ck = jnp.sum(jnp.where(_iota2((FOX_H, 1), 0) == h, crow, 0.0), axis=0, keepdims=True)
    causal = _iota2((1, kmax), 1) <= (i * tq + _iota2((tq, 1), 0))
    s = jnp.where(causal, s + cq - ck, -jnp.inf)
    pe = jnp.exp(s - jnp.max(s, axis=1, keepdims=True))
    return mh, qh, pe, jnp.sum(pe, axis=1, keepdims=True)


def _fox_attn_fwd(qn, kn, z0, ccol, crow, t, tq=256, side=None):
    n = qn.shape[0]
    nq = t // tq

    def body(q_ref, k_ref, v_ref, ccol_ref, crow_ref, o_ref):
        p = pl.program_id(1)
        k, v, crow = k_ref[...].astype(bf16), v_ref[...].astype(bf16), crow_ref[0]
        for i in range(nq):
            rows, kmax = pl.ds(i * tq, tq), (i + 1) * tq
            q, cc = q_ref[rows, :], ccol_ref[rows, :]
            acc = jnp.zeros((tq, LANES), f32)
            for hh in range(2):
                mh, _, pe, l = _fox_scores(hh, p, i, tq, q, k[:kmax], cc, crow[:, :kmax])
                acc = jnp.where(mh, _dot(pe.astype(bf16), v[:kmax]) / l, acc)
            o_ref[rows, :] = acc

    seq = lambda b, p: (b, p)
    return _pcall(
        side, body, name="fox_attn_fwd", grid=(n // t, FOX_H // 2),
        in_specs=[pl.BlockSpec((t, LANES), seq), pl.BlockSpec((t, LANES), seq), pl.BlockSpec((t, LANES), lambda b, p: (b, Z0_FV + p)),
                  pl.BlockSpec((t, LANES), lambda b, p: (b, 0)), pl.BlockSpec((1, FOX_H, t), lambda b, p: (b, 0, 0))],
        out_specs=pl.BlockSpec((t, LANES), seq),
        out_shape=S((n, FOX_W), f32),
        compiler_params=pltpu.CompilerParams(dimension_semantics=("parallel", "parallel")),
    )(qn, kn, z0, ccol, crow)


def _fox_attn_bwd(qn, kn, z0, ccol, crow, do, t, tq=256, side=None):
    n = qn.shape[0]
    nq = t // tq
    nb = n // t

    def body(q_ref, k_ref, v_ref, ccol_ref, crow_ref, do_ref, dq_ref, dk_ref, dv_ref, dcq_ref, dck_ref):
        p = pl.program_id(1)
        dk_ref[...] = jnp.zeros_like(dk_ref)
        dv_ref[...] = jnp.zeros_like(dv_ref)
        dck_ref[...] = jnp.zeros_like(dck_ref)
        kf, v, crow = k_ref[...], v_ref[...].astype(bf16), crow_ref[0]
        k = kf.astype(bf16)
        lane = _iota2((1, LANES), 1)
        sub = _iota2((FOX_H, 1), 0)
        scale = FOX_D ** -0.5
        for i in range(nq):
            rows, kmax = pl.ds(i * tq, tq), (i + 1) * tq
            q, cc, dout = q_ref[rows, :], ccol_ref[rows, :], do_ref[rows, :]
            dq = jnp.zeros((tq, LANES), f32)
            dcq = jnp.zeros((tq, LANES), f32)
            for hh in range(2):
                mh, qh, pe, l = _fox_scores(hh, p, i, tq, q, k[:kmax], cc, crow[:, :kmax])
                pr = pe / l
                doh = jnp.where(mh, dout, 0.0).astype(bf16)
                dp = _dot(doh, v[:kmax], NT)
                ds = pr * (dp - jnp.sum(pr * dp, axis=1, keepdims=True))
                dsb = ds.astype(bf16)
                dq = dq + _dot(dsb, jnp.where(mh, kf[:kmax], 0.0).astype(bf16)) * scale
                dk_ref[:kmax, :] += _dot(dsb, qh, TN) * scale
                dv_ref[:kmax, :] += _dot(pr.astype(bf16), doh, TN)
                h = 2 * p + hh
                dcq = dcq + jnp.where(lane == h, jnp.sum(ds, axis=1, keepdims=True), 0.0)
                dck_ref[0, 0, :, :kmax] += jnp.where(sub == h, -jnp.sum(ds, axis=0, keepdims=True), 0.0)
            dq_ref[rows, :] = dq
            dcq_ref[0, rows, :] = dcq

    seq = lambda b, p: (b, p)
    return _pcall(
        side, body, name="fox_attn_bwd", grid=(nb, FOX_H // 2),
        in_specs=[pl.BlockSpec((t, LANES), seq), pl.BlockSpec((t, LANES), seq), pl.BlockSpec((t, LANES), lambda b, p: (b, Z0_FV + p)),
                  pl.BlockSpec((t, LANES), lambda b, p: (b, 0)), pl.BlockSpec((1, FOX_H, t), lambda b, p: (b, 0, 0)),
                  pl.BlockSpec((t, LANES), seq)],
        out_specs=[pl.BlockSpec((t, LANES), seq), pl.BlockSpec((t, LANES), seq), pl.BlockSpec((t, LANES), seq),
                   pl.BlockSpec((1, t, LANES), lambda b, p: (p, b, 0)), pl.BlockSpec((1, 1, FOX_H, t), lambda b, p: (p, b, 0, 0))],
        out_shape=[S((n, FOX_W), f32), S((n, FOX_W), f32), S((n, FOX_W), f32), S((4, n, LANES), f32), S((4, nb, FOX_H, t), f32)],
        compiler_params=pltpu.CompilerParams(dimension_semantics=("parallel", "parallel")),
    )(qn, kn, z0, ccol, crow, do)


def _adamw_math(w, g, m, v):
    m = ADAM_B1 * m + (1.0 - ADAM_B1) * g
    v = ADAM_B2 * v + (1.0 - ADAM_B2) * (g * g)
    m_hat = m / (1.0 - ADAM_B1 ** ADAM_STEP)
    v_hat = v / (1.0 - ADAM_B2 ** ADAM_STEP)
    return -ADAM_LR * (m_hat / (jnp.sqrt(v_hat) + ADAM_EPS) + ADAM_WD * w), m, v


def _adamw_big(name, idx, gmine, recv1, recv2, w, m, v):
    r, wc = w.shape
    c = gmine.shape[2]
    tr = min(r, 256)

    def body(idx_ref, gm_ref, r1_ref, r2_ref, w_ref, m_ref, v_ref, g_ref, d_ref, nm_ref, nv_ref):
        g = gm_ref[0].astype(f32) + r1_ref[0].astype(f32)
        for k in range(3):
            g = g + r2_ref[k].astype(f32)
        g = g[:, :wc]
        d, nm, nv = _adamw_math(w_ref[...], g, m_ref[...], v_ref[...])
        g_ref[...] = g
        d_ref[...] = d
        nm_ref[...] = nm
        nv_ref[...] = nv

    row = pl.BlockSpec((tr, wc), lambda i, s: (i, 0))
    return pl.pallas_call(
        body, name=name,
        grid_spec=pltpu.PrefetchScalarGridSpec(
            num_scalar_prefetch=1, grid=(r // tr,),
            in_specs=[pl.BlockSpec((1, tr, c), lambda i, s: (s[0], i, 0)), pl.BlockSpec((1, tr, c), lambda i, s: (s[1], i, 0)),
                      pl.BlockSpec((3, tr, c), lambda i, s: (0, i, 0)), row, row, row],
            out_specs=[row, row, row, row]),
        out_shape=[S((r, wc), f32)] * 4,
        compiler_params=pltpu.CompilerParams(dimension_semantics=("parallel",)),
    )(idx, gmine, recv1, recv2, w, m, v)


def _pair_sum(name, idx, gmine, recv1):
    _, r, c = gmine.shape
    g4 = gmine.reshape(4, 2, r, c)

    def body(idx_ref, gm_ref, r1_ref, o_ref):
        o_ref[0] = (gm_ref[0, 0].astype(f32) + r1_ref[0].astype(f32)).astype(bf16)

    return pl.pallas_call(
        body, name=name,
        grid_spec=pltpu.PrefetchScalarGridSpec(
            num_scalar_prefetch=1, grid=(4,),
            in_specs=[pl.BlockSpec((1, 1, r, c), lambda ch, s: (ch, s[0], 0, 0)), pl.BlockSpec((1, r, c), lambda ch, s: (ch, 0, 0))],
            out_specs=pl.BlockSpec((1, r, c), lambda ch, s: (ch, 0, 0))),
        out_shape=S((4, r, c), bf16),
        compiler_params=pltpu.CompilerParams(dimension_semantics=("parallel",)),
    )(idx, g4, recv1)


def _adamw_small(parts, w, m, v, own_mask, own_g):
    def body(p_ref, w_ref, m_ref, v_ref, mask_ref, og_ref, g_ref, d_ref, nm_ref, nv_ref):
        g = p_ref[0]
        for k in range(1, NDEV):
            g = g + p_ref[k]
        g_ref[...] = g
        ge = jnp.where(mask_ref[...] > 0.5, og_ref[...], g)
        d, nm, nv = _adamw_math(w_ref[...], ge, m_ref[...], v_ref[...])
        d_ref[...] = d
        nm_ref[...] = nm
        nv_ref[...] = nv

    return pl.pallas_call(body, name="adamw_small", out_shape=[S(w.shape, f32)] * 4)(parts, w, m, v, own_mask, own_g)


def _sum_parts(parts):
    def body(p_ref, g_ref):
        g = p_ref[0]
        for k in range(1, NDEV):
            g = g + p_ref[k]
        g_ref[...] = g

    return pl.pallas_call(body, name="sum_parts", out_shape=S(parts.shape[1:], f32))(parts)


def _me():
    return lax.axis_index("x"), lax.axis_index("y"), lax.axis_index("c")


def _hbm_specs(n):
    return [pl.BlockSpec(memory_space=pl.ANY)] * n


def _allgather(name, xs):
    na = len(xs)

    def body(*refs):
        x_refs, out_refs = refs[:na], refs[na:2 * na]
        send_sems, recv_sems, local_sems = refs[2 * na:]
        mx, my, mc = _me()
        me, sib = (mx, my, mc), (mx, my, 1 - mc)
        chips = [(1 - mx, my), (mx, 1 - my), (1 - mx, 1 - my)]

        def slab(a, px, py, pc):
            return out_refs[a].at[4 * px + 2 * py + pc]

        def copy(a, k, block, to, own=False):
            return pltpu.make_async_remote_copy(
                src_ref=x_refs[a] if own else slab(a, *block), dst_ref=slab(a, *block),
                send_sem=send_sems.at[7 * a + k], recv_sem=recv_sems.at[7 * a + k], device_id=to, device_id_type=MESH)

        mine = [pltpu.make_async_copy(x_refs[a], slab(a, *me), local_sems.at[a]) for a in range(na)]
        first = []
        for a in range(na):
            mine[a].start()
            first += [copy(a, 0, me, sib, own=True)] + [copy(a, 1 + j, me, (*chip, mc), own=True) for j, chip in enumerate(chips)]
        for cp in first:
            cp.start()
        passed = []
        for j, chip in enumerate(chips):
            for a in range(na):
                copy(a, 1 + j, (*chip, mc), me).wait_recv()
                passed.append(copy(a, 4 + j, (*chip, mc), sib))
                passed[-1].start()
        for a in range(na):
            copy(a, 0, sib, me).wait_recv()
            for j, chip in enumerate(chips):
                copy(a, 4 + j, (*chip, 1 - mc), me).wait_recv()
        for cp in first + passed:
            cp.wait_send()
        for cp in mine:
            cp.wait()

    return pl.pallas_call(
        body, name=name, out_shape=[S((NDEV,) + x.shape, x.dtype) for x in xs],
        in_specs=_hbm_specs(na), out_specs=_hbm_specs(na),
        scratch_shapes=[pltpu.SemaphoreType.DMA((7 * na,)), pltpu.SemaphoreType.DMA((7 * na,)), pltpu.SemaphoreType.DMA((na,))],
    )(*xs)


def _rs_sibling(gs):
    na = len(gs)

    def body(*refs):
        g_refs, out_refs, send_sems, recv_sems = refs[:na], refs[na:2 * na], refs[2 * na], refs[2 * na + 1]
        mx, my, mc = _me()
        cps = [pltpu.make_async_remote_copy(
            src_ref=g_refs[a].at[2 * ch + 1 - mc], dst_ref=out_refs[a].at[ch], send_sem=send_sems.at[4 * a + ch],
            recv_sem=recv_sems.at[4 * a + ch], device_id=(mx, my, 1 - mc), device_id_type=MESH)
            for a in range(na) for ch in range(4)]
        for cp in cps:
            cp.start()
        for cp in cps:
            cp.wait_recv()
        for cp in cps:
            cp.wait_send()

    return pl.pallas_call(
        body, name="rs_sibling", out_shape=[S((4,) + g.shape[1:], g.dtype) for g in gs],
        in_specs=_hbm_specs(na), out_specs=_hbm_specs(na),
        scratch_shapes=[pltpu.SemaphoreType.DMA((4 * na,)), pltpu.SemaphoreType.DMA((4 * na,))],
    )(*gs)


def _side_allgather(xs):
    na = len(xs)

    def mk(x_refs, out_refs, sems):
        send_sems, recv_sems, local_sems = sems
        mx, my, mc = _me()
        me, sib = (mx, my, mc), (mx, my, 1 - mc)
        chips = [(1 - mx, my), (mx, 1 - my), (1 - mx, 1 - my)]

        def slab(a, px, py, pc):
            return out_refs[a].at[4 * px + 2 * py + pc]

        def copy(a, k, block, to, own=False):
            return pltpu.make_async_remote_copy(
                src_ref=x_refs[a] if own else slab(a, *block), dst_ref=slab(a, *block),
                send_sem=send_sems.at[7 * a + k], recv_sem=recv_sems.at[7 * a + k], device_id=to, device_id_type=MESH)

        mine = [pltpu.make_async_copy(x_refs[a], slab(a, *me), local_sems.at[a]) for a in range(na)]
        first = []
        for a in range(na):
            first += [copy(a, 0, me, sib, own=True)] + [copy(a, 1 + j, me, (*chip, mc), own=True) for j, chip in enumerate(chips)]
        return me, sib, chips, mc, copy, mine, first

    def start(x_refs, out_refs, sems):
        *_, mine, first = mk(x_refs, out_refs, sems)
        for cp in mine + first:
            cp.start()

    def finish(x_refs, out_refs, sems):
        me, sib, chips, mc, copy, mine, first = mk(x_refs, out_refs, sems)
        passed = []
        for j, chip in enumerate(chips):
            for a in range(na):
                copy(a, 1 + j, (*chip, mc), me).wait_recv()
                passed.append(copy(a, 4 + j, (*chip, mc), sib))
                passed[-1].start()
        for a in range(na):
            copy(a, 0, sib, me).wait_recv()
            for j, chip in enumerate(chips):
                copy(a, 4 + j, (*chip, 1 - mc), me).wait_recv()
        for cp in first + passed:
            cp.wait_send()
        for cp in mine:
            cp.wait()

    scratch = [pltpu.SemaphoreType.DMA((7 * na,)), pltpu.SemaphoreType.DMA((7 * na,)), pltpu.SemaphoreType.DMA((na,))]
    return _Side(xs, [S((NDEV,) + x.shape, x.dtype) for x in xs], scratch, start, finish)


def _side_exchange(arrs, nslot, out_slots, route):
    na = len(arrs)

    def copies(in_refs, out_refs, sems):
        send_sems, recv_sems = sems
        return [pltpu.make_async_remote_copy(
            src_ref=in_refs[a].at[src], dst_ref=out_refs[a].at[k], send_sem=send_sems.at[nslot * a + k],
            recv_sem=recv_sems.at[nslot * a + k], device_id=to, device_id_type=MESH)
            for a in range(na) for k, (src, to) in enumerate(route(*_me()))]

    def start(in_refs, out_refs, sems):
        for cp in copies(in_refs, out_refs, sems):
            cp.start()

    def finish(in_refs, out_refs, sems):
        cps = copies(in_refs, out_refs, sems)
        for cp in cps:
            cp.wait_recv()
        for cp in cps:
            cp.wait_send()

    scratch = [pltpu.SemaphoreType.DMA((nslot * na,)), pltpu.SemaphoreType.DMA((nslot * na,))]
    return _Side(arrs, [S((out_slots,) + x.shape[1:], x.dtype) for x in arrs], scratch, start, finish)


def _side_rs_sibling(gs):
    return _side_exchange(gs, 4, 4, lambda mx, my, mc: [(2 * ch + 1 - mc, (mx, my, 1 - mc)) for ch in range(4)])


def _side_rs_chips(ps):
    return _side_exchange(ps, 3, 3, lambda mx, my, mc: [(2 * cx + cy, (cx, cy, mc)) for cx, cy in
                                                        [(1 - mx, my), (mx, 1 - my), (1 - mx, 1 - my)]])


_COLW = 512
_COL_NAMES = ("l0_w_in", "l0_w_ff1", "l1_w_in", "l1_w_ff1")
_ROW_NAMES = ("l0_w_out", "l0_w_ff2", "l1_w_out", "l1_w_ff2")
_BIG_NAMES = _COL_NAMES + _ROW_NAMES


def _full_weight(gathered, name, ncols):
    if name in _ROW_NAMES:
        return gathered.reshape(-1, D)
    return gathered[:, :, :ncols].transpose(1, 0, 2).reshape(D, NDEV * ncols)


def _regroup_w_in0(w):
    main = jnp.concatenate([w[:, 0:1536], w[:, 1544:3080], w[:, 3088:3600]], axis=1)
    small = jnp.concatenate([w[:, 1536:1544], w[:, 3080:3088]], axis=1)
    return jnp.concatenate([main, small, jnp.zeros((D, ZW0 - 3584 - 16), w.dtype)], axis=1)


def _ungroup_w_in0(g):
    return jnp.concatenate([g[:, 0:1536], g[:, 3584:3592], g[:, 1536:3072], g[:, 3592:3600], g[:, 3072:3584]], axis=1)


def _col_slabs(g, ncols):
    g = g.reshape(D, NDEV, ncols).transpose(1, 0, 2)
    return jnp.pad(g, ((0, 0), (0, 0), (0, _COLW - ncols)))


def _sq(t):
    return t * t


def _epi_res_norm(acc, res, gain):
    y = acc + res
    h = _f_norm(y, gain)[0]
    return y, h, h


_RES_NORM_OUTS = [("tile", f32), ("tile", bf16), ("tile_t", bf16)]


def _epi_norm_bwd(acc, x, dres, gain):
    _, vjp = jax.vjp(lambda xx, gg: _f_norm(xx, gg)[0], x, gain)
    dx, dgain = vjp(acc)
    dx = dx + dres
    return dx, dgain, dx


_NORM_BWD_OUTS = [("tile", f32), ("rows", f32), ("tile_t", bf16)]


def _epi_loss(acc, res, tgt):
    e = acc + res - tgt
    dy = e * (1.0 / D)
    return dy, dy, jnp.sum(e * e, axis=0, keepdims=True)


_LOSS_OUTS = [("tile", f32), ("tile_t", bf16), ("rows", f32)]


def _mlp_fwd(tag, x, h, w1, w2, epi, extra, consts, outs):
    a = _mm(f"{tag}_ff1", h, w1, NN, bf16, epi=lambda acc: jnp.maximum(acc, 0.0))
    return _mm(f"{tag}_ff2", a, w2, NN, f32, a_fn=_sq, epi=epi, extra=(x,) + tuple(extra), consts=consts, outs=outs), a


def _mlp_bwd(tag, x, gain, w1, w2, ht, a, dy, dyt, side=None):
    da = _mm(f"{tag}_ff2_dx", dy, w2, NT, bf16, epi=lambda acc, av: acc * 2.0 * av.astype(f32), extra=(a,), side=side)
    if side is not None:
        da, side_res = da
    dw2 = _mm(f"{tag}_ff2_dw", dyt, a, NN, bf16, b_fn=_sq, out_t=True)
    res = _mm(f"{tag}_ff1_dx", da, w1, NT, f32, epi=_epi_norm_bwd, extra=(x, dy), consts=(gain,), outs=_NORM_BWD_OUTS)
    dw1 = _mm(f"{tag}_ff1_dw", ht, da, NN, bf16, slab=True)
    return (res, dw1, dw2) if side is None else ((res, dw1, dw2), side_res)


def _row(v):
    return v.reshape(1, -1).astype(f32)


_L0_REST = ("l0_w_ff1", "l0_w_out", "l0_w_ff2")
_L1_MIX = ("l1_w_in", "l1_w_out")
_L1_FFN = ("l1_w_ff1", "l1_w_ff2")
_GRAD_A = ("l1_w_ff1", "l1_w_ff2", "l1_w_out", "l1_w_in")
_GRAD_B = ("l0_w_ff1", "l0_w_ff2", "l0_w_out")
_IN_PLACE = ("l0_w_ff1", "l1_w_in", "l1_w_ff1")


def _train(x, tgt, args, mom, var, t):
    n = x.shape[0]
    nchunk = t // CH
    mx, my, mc = _me()
    dev, chip = 4 * mx + 2 * my + mc, 2 * mx + my
    core_idx = jnp.reshape(mc, (1,)).astype(jnp.int32)
    own_idx = jnp.stack([dev, chip]).astype(jnp.int32)
    ncols = {nm: args[nm].shape[1] for nm in _COL_NAMES}
    conv_cols = args["l0_gdn_conv"].shape[1]
    g, big, w = {}, {}, {}

    def send(nm):
        a = args[nm].astype(bf16)
        return jnp.pad(a, ((0, 0), (0, _COLW - ncols[nm]))) if nm in _COL_NAMES else a

    def take(names, gathered):
        for nm, arr in zip(names, gathered):
            w[nm] = arr if nm in _IN_PLACE else _full_weight(arr, nm, ncols.get(nm, 0))

    def by_dev(nm, ga):
        return ga if nm in _COL_NAMES else ga.reshape(NDEV, -1, D)

    def pair(names, gs, r1s):
        return [_pair_sum(f"rs_pair_sum_{nm}", core_idx, ga, r1) for nm, ga, r1 in zip(names, gs, r1s)]

    def adam(names, gs, r1s, r2s):
        for nm, ga, r1, r2 in zip(names, gs, r1s, r2s):
            big[nm] = _adamw_big(f"adamw_{nm}", own_idx, ga, r1, r2, args[nm], mom[nm], var[nm])


    li = jnp.arange(FOX_W)
    pm = jnp.where((li[:, None] // FOX_D) == (li[None, :] // FOX_D), 1.0 / FOX_D, 0.0).astype(f32)
    lane_head = jnp.arange(GDN_W) // HD
    sel = lambda first_lane: (jnp.arange(LANES)[:, None] == (first_lane + lane_head)[None, :]).astype(f32)
    e_beta, e_alpha = sel(FOX_H), sel(FOX_H + GDN_H)
    alog_b, dt_b = _row(jnp.repeat(args["l0_gdn_A_log"], HD)), _row(jnp.repeat(args["l0_gdn_dt_bias"], HD))
    gq_t, gk_t = _row(jnp.tile(args["l0_fox_q_norm"], FOX_H)), _row(jnp.tile(args["l0_fox_k_norm"], FOX_H))
    on0_t, on1_t = _row(jnp.tile(args["l0_gdn_o_norm"], GDN_H)), _row(jnp.tile(args["l1_hgrn_o_norm"], HG_H))
    fbias = jnp.pad(_row(args["l0_fox_f_bias"]), ((0, 0), (0, LANES - FOX_H)))
    g0m, g0f, g1m, g1f = (_row(args[k]) for k in ("l0_mix_norm", "l0_ffn_norm", "l1_mix_norm", "l1_ffn_norm"))
    lbl = args["hgrn_lb_logits"].astype(f32)

    first = [send("l0_w_in"), jnp.pad(args["l0_gdn_conv"], ((0, 4), (0, LANES * 2 - conv_cols)))]
    (h0, h0t), first = _tok_fwd("l0_mix_norm", _f_norm, [(x, D, 0)], [(g0m, None, 0)], [(D, bf16)], 256, also_t=(0,),
                                side=_side_allgather(first))
    take(("l0_w_in",), first[:1])
    w_in0 = _regroup_w_in0(w["l0_w_in"])
    wconv = first[1][:, :4, :conv_cols].transpose(1, 0, 2).reshape(4, NDEV * conv_cols)
    z0 = _mm("l0_in", h0, w_in0, NN, f32)
    qk_rows = [(z0, 2 * FOX_W, 0)]
    qk_consts = [(gq_t, None, 0), (gk_t, None, 0), (pm, None, 0)]
    qn, kn = _tok_fwd("fox_pre", _f_foxpre, qk_rows, qk_consts, [(FOX_W, f32)] * 2, 256)
    ccol, crow = _fox_gate_fwd(z0, fbias, t)
    fox_o, got = _fox_attn_fwd(qn, kn, z0, ccol, crow, t, side=_side_allgather([send(nm) for nm in _L0_REST]))
    take(_L0_REST, got)
    conv_rows, conv_consts = [(z0, LANES, Z0_GQKV)], [(wconv, LANES, 0)]
    (qkv,) = _tok_fwd("gdn_conv", _f_conv, conv_rows, conv_consts, [(LANES, f32)], t, ncb=12, with_j=True)
    gate_consts = [(e_beta, None, 0), (e_alpha, None, 0), (alog_b, None, 0), (dt_b, None, 0)]
    intra_rows = [(qkv, GDN_W, 0), (qkv, GDN_W, 1), (qkv, GDN_W, 2), (z0, LANES, Z0_SMALL)]
    intra, got = _tok_fwd("gdn_intra", _f_gdn_gated, intra_rows, gate_consts, [(GDN_W, f32)] * 7, 2 * CH,
                          side=_side_allgather([send(nm) for nm in _L1_MIX]))
    take(_L1_MIX, got)
    inter_rows = [(a, 0, 1) for a in intra[:6]]
    full_rows = inter_rows + [(fox_o, 0, 1), (z0, Z0_GG // 4, 1)]
    (cat0, gdn_hist), got = _scan_fwd("gdn_scan", _f_gdn_full, full_rows, GDN_H, nchunk, consts=(on0_t,), out_dtype=bf16, out_k=2,
                                      side=_side_allgather([send(nm) for nm in _L1_FFN[1:]]))
    take(_L1_FFN[1:], got)
    x1, hf0, hf0t = _mm("l0_out", cat0, w["l0_w_out"], NN, f32, epi=_epi_res_norm, extra=(x,), consts=(g0f,), outs=_RES_NORM_OUTS)
    (x2, h1, h1t), a0 = _mlp_fwd("l0", x1, hf0, w["l0_w_ff1"], w["l0_w_ff2"], _epi_res_norm, (), (g1m,), _RES_NORM_OUTS)

    z1 = _mm("l1_in", h1, w["l1_w_in"], NN, f32)
    (cat1, hg_hist), got = _scan_fwd("hgrn_scan", _f_hgrn_full, [(z1, 0, 4)], HG_H, nchunk, consts=(lbl, on1_t), out_dtype=bf16,
                                     side=_side_allgather([send(nm) for nm in _L1_FFN[:1]]))
    take(_L1_FFN[:1], got)
    x3, hf1, hf1t = _mm("l1_out", cat1, w["l1_w_out"], NN, f32, epi=_epi_res_norm, extra=(x2,), consts=(g1f,), outs=_RES_NORM_OUTS)
    (dy, dyt, loss_row), a1 = _mlp_fwd("l1", x3, hf1, w["l1_w_ff1"], w["l1_w_ff2"], _epi_loss, (tgt,), (), _LOSS_OUTS)

    (dx3, g["l1_ffn_norm"], dx3t), ga_ff1, ga_ff2 = _mlp_bwd("l1", x3, g1f, w["l1_w_ff1"], w["l1_w_ff2"], hf1t, a1, dy, dyt)
    dcat1 = _mm("l1_out_dx", dx3, w["l1_w_out"], NT, f32)
    ga_out = _mm("l1_out_dw", dx3t, cat1, NN, bf16, out_t=True)
    dz1, dlbl, don1 = _scan_bwd("hgrn_scan_bwd", _f_hgrn_full, [(z1, 0, 4)], hg_hist, dcat1, HG_H, nchunk, dtypes=[bf16],
                                consts=(lbl, on1_t))
    dx2, g["l1_mix_norm"], dx2t = _mm("l1_in_dx", dz1, w["l1_w_in"], NT, f32, epi=_epi_norm_bwd, extra=(x2, dx3), consts=(g1m,),
                                      outs=_NORM_BWD_OUTS)
    ga_in = _mm("l1_in_dw", h1t, dz1, NN, bf16, slab=True)
    g["l1_hgrn_o_norm"] = don1.reshape(HG_H, HD).sum(0)
    g["hgrn_lb_logits"] = dlbl
    gs_a = [by_dev(nm, ga) for nm, ga in zip(_GRAD_A, (ga_ff1, ga_ff2, ga_out, ga_in))]

    ((dx1, g["l0_ffn_norm"], dx1t), gb_ff1, gb_ff2), r1_a = _mlp_bwd("l0", x1, g0f, w["l0_w_ff1"], w["l0_w_ff2"], hf0t, a0, dx2, dx2t,
                                                                  side=_side_rs_sibling(gs_a))
    pairs_a = pair(_GRAD_A, gs_a, r1_a)
    dcat0 = _mm("l0_out_dx", dx1, w["l0_w_out"], NT, f32)
    gb_out = _mm("l0_out_dw", dx1t, cat0, NN, bf16, out_t=True)
    gs_b = [by_dev(nm, ga) for nm, ga in zip(_GRAD_B, (gb_ff1, gb_ff2, gb_out))]
    *dinter, dfox_o, dgg, don0 = _scan_bwd("gdn_scan_bwd", _f_gdn_full, full_rows, gdn_hist, dcat0, GDN_H, nchunk, consts=(on0_t,),
                                           out_k=2, dtypes=[f32] * 7 + [bf16])
    (dqn, dkn, dfv, dcq, dck), got = _fox_attn_bwd(qn, kn, z0, ccol, crow, dfox_o, t,
                                                   side=_join_sides([_side_rs_chips(pairs_a), _side_rs_sibling(gs_b)]))
    r2_a, r1_b = got[:len(_GRAD_A)], got[len(_GRAD_A):]
    adam(_GRAD_A, gs_a, r1_a, r2_a)
    pairs_b = pair(_GRAD_B, gs_b, r1_b)
    (dqkv, dzs_g, dalog_b, ddt_b), r2_b = _tok_bwd(
        "gdn_intra_bwd", _f_gdn_gated_given, intra_rows + [(intra[6], GDN_W, 0)], gate_consts, [(a, GDN_W) for a in dinter],
        2 * CH, [0, 1, 2, 3], [2, 3], ncat=3, side=_side_rs_chips(pairs_b))
    adam(_GRAD_B, gs_b, r1_b, r2_b)
    dgqkv, dwconv = _tok_bwd("gdn_conv_bwd", _f_conv, conv_rows, conv_consts, [(dqkv, LANES)], t, [0], [0], ncb=12, with_j=True,
                             drow_dtype=bf16)
    dzs_f, dfb = _fox_gate_bwd(z0, fbias, dcq, dck, t)
    dzqk, dgq_t, dgk_t = _tok_bwd("fox_pre_bwd", _f_foxpre, qk_rows, qk_consts, [(dqn, FOX_W), (dkn, FOX_W)], 256, [0], [0, 1],
                                  drow_dtype=bf16)
    dz0 = jnp.concatenate([dzqk, dfv.astype(bf16), dgqkv, dgg, (dzs_g + dzs_f).astype(bf16), jnp.zeros((n, ZW0 - 3712), bf16)], axis=1)
    gs_c = [_col_slabs(_ungroup_w_in0(_mm("l0_in_dw", h0t, dz0, NN, bf16)), ncols["l0_w_in"])]
    r1_c = _rs_sibling(gs_c)
    pairs_c = pair(("l0_w_in",), gs_c, r1_c)
    (dx, g["l0_mix_norm"], _), r2_c = _mm("l0_in_dx", dz0, w_in0, NT, f32, epi=_epi_norm_bwd, extra=(x, dx1), consts=(g0m,),
                                          outs=_NORM_BWD_OUTS, side=_side_rs_chips(pairs_c))
    adam(("l0_w_in",), gs_c, r1_c, r2_c)
    g["l0_fox_q_norm"] = dgq_t.reshape(FOX_H, FOX_D).sum(0)
    g["l0_fox_k_norm"] = dgk_t.reshape(FOX_H, FOX_D).sum(0)
    g["l0_fox_f_bias"] = dfb[0, :FOX_H]
    g["l0_gdn_conv"] = dwconv
    g["l0_gdn_A_log"] = dalog_b.reshape(GDN_H, HD).sum(1)
    g["l0_gdn_dt_bias"] = ddt_b.reshape(GDN_H, HD).sum(1)
    g["l0_gdn_o_norm"] = don0.reshape(GDN_H, HD).sum(0)
    return loss_row, dx, g, big


_NAMES = ("l0_mix_norm", "l0_w_in", "l0_fox_q_norm", "l0_fox_k_norm", "l0_fox_f_bias", "l0_gdn_conv", "l0_gdn_A_log",
          "l0_gdn_dt_bias", "l0_gdn_o_norm", "l0_w_out", "l0_ffn_norm", "l0_w_ff1", "l0_w_ff2", "l1_mix_norm", "l1_w_in",
          "l1_hgrn_o_norm", "l1_w_out", "l1_ffn_norm", "l1_w_ff1", "l1_w_ff2", "hgrn_lb_logits")
_SMALL_NAMES = tuple(nm for nm in _NAMES if nm not in _BIG_NAMES)
_SMALL_ROWS = 16


def _pack_small(vals):
    flat = jnp.concatenate([vals[nm].reshape(-1).astype(f32) for nm in _SMALL_NAMES])
    return jnp.pad(flat, (0, _SMALL_ROWS * D - flat.shape[0])).reshape(_SMALL_ROWS, D)


def _unpack_small(packed, shapes):
    flat = packed.reshape(-1)
    out, off = {}, 0
    for nm in _SMALL_NAMES:
        size = 1
        for s in shapes[nm]:
            size *= s
        out[nm] = flat[off:off + size].reshape(shapes[nm])
        off += size
    return out, off


def kernel(x, l0_mix_norm, l0_w_in, l0_fox_q_norm, l0_fox_k_norm, l0_fox_f_bias, l0_gdn_conv, l0_gdn_A_log, l0_gdn_dt_bias, l0_gdn_o_norm, l0_w_out, l0_ffn_norm, l0_w_ff1, l0_w_ff2, l1_mix_norm, l1_w_in, l1_hgrn_o_norm, l1_w_out, l1_ffn_norm, l1_w_ff1, l1_w_ff2, hgrn_lb_logits, loss_target, m_l0_mix_norm, m_l0_w_in, m_l0_fox_q_norm, m_l0_fox_k_norm, m_l0_fox_f_bias, m_l0_gdn_conv, m_l0_gdn_A_log, m_l0_gdn_dt_bias, m_l0_gdn_o_norm, m_l0_w_out, m_l0_ffn_norm, m_l0_w_ff1, m_l0_w_ff2, m_l1_mix_norm, m_l1_w_in, m_l1_hgrn_o_norm, m_l1_w_out, m_l1_ffn_norm, m_l1_w_ff1, m_l1_w_ff2, m_hgrn_lb_logits, v_l0_mix_norm, v_l0_w_in, v_l0_fox_q_norm, v_l0_fox_k_norm, v_l0_fox_f_bias, v_l0_gdn_conv, v_l0_gdn_A_log, v_l0_gdn_dt_bias, v_l0_gdn_o_norm, v_l0_w_out, v_l0_ffn_norm, v_l0_w_ff1, v_l0_w_ff2, v_l1_mix_norm, v_l1_w_in, v_l1_hgrn_o_norm, v_l1_w_out, v_l1_ffn_norm, v_l1_w_ff1, v_l1_w_ff2, v_hgrn_lb_logits):
    args = dict(zip(_NAMES, (l0_mix_norm, l0_w_in, l0_fox_q_norm, l0_fox_k_norm, l0_fox_f_bias, l0_gdn_conv, l0_gdn_A_log, l0_gdn_dt_bias, l0_gdn_o_norm, l0_w_out, l0_ffn_norm, l0_w_ff1, l0_w_ff2, l1_mix_norm, l1_w_in, l1_hgrn_o_norm, l1_w_out, l1_ffn_norm, l1_w_ff1, l1_w_ff2, hgrn_lb_logits)))
    mom = dict(zip(_NAMES, (m_l0_mix_norm, m_l0_w_in, m_l0_fox_q_norm, m_l0_fox_k_norm, m_l0_fox_f_bias, m_l0_gdn_conv, m_l0_gdn_A_log, m_l0_gdn_dt_bias, m_l0_gdn_o_norm, m_l0_w_out, m_l0_ffn_norm, m_l0_w_ff1, m_l0_w_ff2, m_l1_mix_norm, m_l1_w_in, m_l1_hgrn_o_norm, m_l1_w_out, m_l1_ffn_norm, m_l1_w_ff1, m_l1_w_ff2, m_hgrn_lb_logits)))
    var = dict(zip(_NAMES, (v_l0_mix_norm, v_l0_w_in, v_l0_fox_q_norm, v_l0_fox_k_norm, v_l0_fox_f_bias, v_l0_gdn_conv, v_l0_gdn_A_log, v_l0_gdn_dt_bias, v_l0_gdn_o_norm, v_l0_w_out, v_l0_ffn_norm, v_l0_w_ff1, v_l0_w_ff2, v_l1_mix_norm, v_l1_w_in, v_l1_hgrn_o_norm, v_l1_w_out, v_l1_ffn_norm, v_l1_w_ff1, v_l1_w_ff2, v_hgrn_lb_logits)))
    nb, t, _ = x.shape
    dev = 4 * lax.axis_index("x") + 2 * lax.axis_index("y") + lax.axis_index("c")
    conv_cols = l0_gdn_conv.shape[1]
    loss_row, dx, g, big = _train(x.reshape(nb * t, D), loss_target.reshape(nb * t, D), args, mom, var, t)

    shapes = {nm: args[nm].shape for nm in _SMALL_NAMES}
    gsm = dict(g)
    gsm["l0_gdn_conv"] = jnp.zeros(shapes["l0_gdn_conv"], f32)
    packed = _pack_small(gsm)
    _, used = _unpack_small(packed, shapes)
    flat_extra = jnp.concatenate([jnp.sum(loss_row).reshape(1), g["l0_gdn_conv"].reshape(-1)])
    packed = packed.reshape(-1).at[used:used + flat_extra.shape[0]].set(flat_extra).reshape(_SMALL_ROWS, D)
    (parts,) = _allgather("ag_small", [packed])
    total = _sum_parts(parts).reshape(-1)
    loss = 0.5 * total[used] / D
    conv_g_full = total[used + 1:used + 1 + 4 * NDEV * conv_cols].reshape(4, NDEV * conv_cols)
    conv_g = lax.dynamic_slice(conv_g_full, (0, dev * conv_cols), (4, conv_cols))
    own_vals = {nm: jnp.zeros(shapes[nm], f32) for nm in _SMALL_NAMES}
    own_vals["l0_gdn_conv"] = conv_g
    own_mask = {nm: jnp.zeros(shapes[nm], f32) for nm in _SMALL_NAMES}
    own_mask["l0_gdn_conv"] = jnp.ones(shapes["l0_gdn_conv"], f32)
    small = _adamw_small(parts, _pack_small(args), _pack_small(mom), _pack_small(var), _pack_small(own_mask), _pack_small(own_vals))
    small = [_unpack_small(a, shapes)[0] for a in small]
    small[0]["l0_gdn_conv"] = conv_g

    outs = [loss, dx.reshape(nb, t, D)]
    for k in range(4):
        outs += [big[nm][k] if nm in _BIG_NAMES else small[k][nm] for nm in _NAMES]
    return tuple(outs)
```

```python
import functools

import jax
import jax.numpy as jnp
from jax import lax
from jax.experimental import pallas as pl
from jax.experimental.pallas import tpu as pltpu

f32, bf16 = jnp.float32, jnp.bfloat16
NN = (((1,), (0,)), ((), ()))
NT = (((1,), (1,)), ((), ()))
TN = (((0,), (0,)), ((), ()))
HI = lax.Precision.HIGHEST
MESH = pl.DeviceIdType.MESH
S = jax.ShapeDtypeStruct

EPS = 1e-6
D = 1024
LANES = 128
FOX_H, FOX_D, FOX_W = 8, 64, 512
GDN_H, HD, GDN_W = 4, 128, 512
HG_H = 8
CH = 64
ZW0 = 3840
NDEV = 8
ADAM_LR, ADAM_B1, ADAM_B2, ADAM_EPS, ADAM_WD, ADAM_STEP = 0.001, 0.9, 0.999, 1e-08, 0.01, 10

Z0_FQK, Z0_FV, Z0_GQKV, Z0_GG, Z0_SMALL = 0, 8, 12, 24, 28


def _dot(a, b, dims=NN, prec=None):
    return lax.dot_general(a, b, dims, precision=prec, preferred_element_type=f32)


def _iota2(shape, axis):
    return lax.broadcasted_iota(jnp.int32, shape, axis)


def _split3(x):
    x1 = x.astype(bf16)
    r = x - x1.astype(f32)
    x2 = r.astype(bf16)
    return x1, x2, (r - x2.astype(f32)).astype(bf16)


def _dot_sel(a, b, dims=NN, exact_lhs=False):
    if exact_lhs:
        return sum(_dot(a.astype(bf16), piece, dims) for piece in _split3(b))
    return sum(_dot(piece, b.astype(bf16), dims) for piece in _split3(a))


@jax.custom_vjp
def _sel_rhs(a, b):
    return _dot_sel(a, b)


_sel_rhs.defvjp(lambda a, b: (_dot_sel(a, b), b), lambda b, g: (_dot_sel(g, b, NT), jnp.zeros_like(b)))


@jax.custom_vjp
def _sel_lhs(a, x):
    return _dot_sel(a, x, exact_lhs=True)


_sel_lhs.defvjp(lambda a, x: (_dot_sel(a, x, exact_lhs=True), a), lambda a, g: (jnp.zeros_like(a), _dot_sel(a, g, TN, exact_lhs=True)))


class _Side:
    def __init__(self, ins, out_shapes, scratch, start, finish):
        self.ins, self.out_shapes, self.scratch, self.start, self.finish = list(ins), list(out_shapes), list(scratch), start, finish


def _join_sides(sides):
    def split(refs, counts):
        out, off = [], 0
        for c in counts:
            out.append(refs[off:off + c])
            off += c
        return out

    ni, no, ns = ([len(getattr(sd, a)) for sd in sides] for a in ("ins", "out_shapes", "scratch"))

    def run(which):
        def go(ins, outs, sems):
            for sd, i, o, c in zip(sides, split(ins, ni), split(outs, no), split(sems, ns)):
                getattr(sd, which)(i, o, c)
        return go

    return _Side(sum((sd.ins for sd in sides), []), sum((sd.out_shapes for sd in sides), []),
                 sum((sd.scratch for sd in sides), []), run("start"), run("finish"))


def _pcall(side, body, *, name, grid, in_specs, out_specs, out_shape, scratch_shapes=(), compiler_params=None):
    if side is None:
        return pl.pallas_call(body, name=name, grid=grid, in_specs=in_specs, out_specs=out_specs, out_shape=out_shape,
                              scratch_shapes=scratch_shapes, compiler_params=compiler_params)
    single = not isinstance(out_shape, (list, tuple))
    ospecs, oshape = ([out_specs], [out_shape]) if single else (list(out_specs), list(out_shape))
    nin, nout, nscr = len(in_specs), len(ospecs), len(scratch_shapes)
    si, so = len(side.ins), len(side.out_shapes)

    def wrapped(*refs):
        o0 = nin + si
        c0 = o0 + nout + so
        sins, souts, ssems = refs[nin:o0], refs[o0 + nout:c0], refs[c0 + nscr:]
        ids = [pl.program_id(a) for a in range(len(grid))]
        first = functools.reduce(jnp.logical_and, [i == 0 for i in ids])
        last = functools.reduce(jnp.logical_and, [i == g - 1 for i, g in zip(ids, grid)])

        @pl.when(first)
        def _():
            side.start(sins, souts, ssems)

        body(*refs[:nin], *refs[o0:o0 + nout], *refs[c0:c0 + nscr])

        @pl.when(last)
        def _():
            side.finish(sins, souts, ssems)

    call = pl.pallas_call(
        wrapped, name=name, grid=grid, in_specs=list(in_specs) + _hbm_specs(si), out_specs=ospecs + _hbm_specs(so),
        out_shape=oshape + side.out_shapes, scratch_shapes=list(scratch_shapes) + side.scratch,
        compiler_params=pltpu.CompilerParams(dimension_semantics=("arbitrary",) * len(grid),
                                             vmem_limit_bytes=getattr(compiler_params, "vmem_limit_bytes", None)))

    def run(*args):
        res = call(*args, *side.ins)
        return (res[0] if single else list(res[:nout])), list(res[nout:])

    return run


def _tok_specs(rows, consts, tm):
    specs = []
    for (_, w, base) in rows:
        specs.append(pl.BlockSpec((tm, w), functools.partial(lambda j, i, b: (i, b + j), b=base)))
    for (arr, w, base) in consts:
        if w is None:
            specs.append(pl.BlockSpec(arr.shape, lambda j, i: (0, 0)))
        else:
            specs.append(pl.BlockSpec((arr.shape[0], w), functools.partial(lambda j, i, b: (0, b + j), b=base)))
    return specs


def _tok_fwd(name, f, rows, consts, outs, tm, ncb=1, with_j=False, also_t=(), side=None):
    n = rows[0][0].shape[0]
    nin = len(rows) + len(consts)
    nout = len(outs)

    def body(*refs):
        ins = [r[...] for r in refs[:nin]]
        vals = f(pl.program_id(0), *ins) if with_j else f(*ins)
        for r, v in zip(refs[nin:nin + nout], vals):
            r[...] = v.astype(r.dtype)
        for r, k in zip(refs[nin + nout:], also_t):
            r[...] = vals[k].T.astype(r.dtype)

    return _pcall(
        side, body, name=name, grid=(ncb, n // tm),
        in_specs=_tok_specs(rows, consts, tm),
        out_specs=[pl.BlockSpec((tm, w), lambda j, i: (i, j)) for (w, _) in outs]
        + [pl.BlockSpec((outs[k][0], tm), lambda j, i: (0, i)) for k in also_t],
        out_shape=[S((n, w * ncb), dt) for (w, dt) in outs] + [S((outs[k][0], n), bf16) for k in also_t],
        compiler_params=pltpu.CompilerParams(dimension_semantics=("parallel", "parallel")),
    )(*[r[0] for r in rows], *[c[0] for c in consts])


def _tok_bwd(name, f, rows, consts, cots, tm, drow, dconst, ncb=1, with_j=False, addto=None, also_t=(), drow_dtype=f32, side=None, ncat=0):
    n = rows[0][0].shape[0]
    nr, nc, nct = len(rows), len(consts), len(cots)
    addto = addto or {}
    add_keys = sorted(addto)
    nadd = len(add_keys)

    def body(*refs):
        ins = [r[...] for r in refs[:nr + nc]]
        cot = [r[...] for r in refs[nr + nc:nr + nc + nct]]
        adds = refs[nr + nc + nct:nr + nc + nct + nadd]
        outs = refs[nr + nc + nct + nadd:]
        pos = list(drow) + [nr + k for k in dconst]

        def g(*dargs):
            full = list(ins)
            for p, a in zip(pos, dargs):
                full[p] = a
            return tuple(f(pl.program_id(0), *full) if with_j else f(*full))

        vals, vjp = jax.vjp(g, *[ins[p] for p in pos])
        grads = vjp(tuple(c.astype(v.dtype) for c, v in zip(cot, vals)))
        off = 0
        for k in range(len(drow)):
            gk = grads[k]
            if k in addto:
                gk = gk + adds[add_keys.index(k)][...]
            if k < ncat:
                outs[0][:, off:off + gk.shape[1]] = gk.astype(outs[0].dtype)
                off += gk.shape[1]
            else:
                outs[k - skip][...] = gk.astype(outs[k - skip].dtype)
            if k in also_t:
                tref = outs[len(drow) - skip + len(dconst) + list(also_t).index(k)]
                tref[...] = gk.T.astype(tref.dtype)
        first = pl.program_id(1) == 0
        for k in range(len(dconst)):
            ref = outs[len(drow) - skip + k]

            @pl.when(first)
            def _():
                ref[...] = jnp.zeros_like(ref)

            ref[...] += grads[len(drow) + k]

    skip = max(ncat - 1, 0)
    in_specs = _tok_specs(rows, consts, tm)
    in_specs += [pl.BlockSpec((tm, w), lambda j, i: (i, j)) for (_, w) in cots]
    in_specs += [pl.BlockSpec((tm, rows[drow[k]][1]), lambda j, i: (i, j)) for k in add_keys]
    dts = drow_dtype if isinstance(drow_dtype, (list, tuple)) else [drow_dtype] * len(drow)
    widths = [rows[k][1] for k in drow]
    if ncat:
        widths, dts = [sum(widths[:ncat])] + widths[ncat:], [dts[0]] + list(dts[ncat:])
    out_specs = [pl.BlockSpec((tm, wd), lambda j, i: (i, j)) for wd in widths]
    out_shape = [S((n, wd * ncb), dt) for wd, dt in zip(widths, dts)]
    for k in dconst:
        arr, w, _ = consts[k]
        if w is None:
            out_specs.append(pl.BlockSpec(arr.shape, lambda j, i: (0, 0)))
            out_shape.append(S(arr.shape, f32))
        else:
            out_specs.append(pl.BlockSpec((arr.shape[0], w), lambda j, i: (0, j)))
            out_shape.append(S((arr.shape[0], w * ncb), f32))
    for k in also_t:
        out_specs.append(pl.BlockSpec((rows[drow[k]][1], tm), lambda j, i: (0, i)))
        out_shape.append(S((rows[drow[k]][1], n), bf16))
    return _pcall(
        side, body, name=name, grid=(ncb, n // tm), in_specs=in_specs, out_specs=out_specs, out_shape=out_shape,
        compiler_params=pltpu.CompilerParams(dimension_semantics=("parallel", "arbitrary")),
    )(*[r[0] for r in rows], *[c[0] for c in consts], *[c[0] for c in cots], *[addto[k] for k in add_keys])


def _scan_fwd(name, f, rows, nh, nchunk, side=None, consts=(), out_dtype=f32, out_k=1):
    n = rows[0][0].shape[0]
    nb = n // (CH * nchunk)
    nin, nco = len(rows), len(consts)
    w = nh * HD

    def body(*refs):
        o_ref, hist_ref, st = refs[nin + nco], refs[nin + nco + 1], refs[nin + nco + 2]

        @pl.when(pl.program_id(0) == 0)
        def _():
            st[...] = jnp.zeros_like(st)

        s0 = st[...]
        hist_ref[0] = s0.astype(hist_ref.dtype)
        tiles = [r[...].reshape(nb * CH, r.shape[2]) for r in refs[:nin]]
        o, s1 = f(*tiles, *[r[...] for r in refs[nin:nin + nco]], s0)
        o_ref[...] = o.reshape(o_ref.shape).astype(o_ref.dtype)
        st[...] = s1

    seq3 = lambda a: a.reshape(nb, nchunk * CH, a.shape[1])
    res = _pcall(
        side, body, name=name, grid=(nchunk,),
        in_specs=[pl.BlockSpec((nb, CH, k * w), functools.partial(lambda c, base: (0, c, base), base=b)) for (_, b, k) in rows]
        + [pl.BlockSpec(c.shape, lambda c: (0, 0)) for c in consts],
        out_specs=[pl.BlockSpec((nb, CH, out_k * w), lambda c: (0, c, 0)), pl.BlockSpec((1, nb * w, HD), lambda c: (c, 0, 0))],
        out_shape=[S((nb, nchunk * CH, out_k * w), out_dtype), S((nchunk, nb * w, HD), bf16)],
        scratch_shapes=[pltpu.VMEM((nb * w, HD), f32)],
        compiler_params=pltpu.CompilerParams(dimension_semantics=("arbitrary",)),
    )(*[seq3(r[0]) for r in rows], *consts)
    (o, hist), extra = (res, None) if side is None else res
    out = [o.reshape(n, out_k * w), hist]
    return out if side is None else (out, extra)


def _scan_bwd(name, f, rows, hist, do, nh, nchunk, side=None, dtypes=None, consts=(), out_k=1):
    n = rows[0][0].shape[0]
    nb = n // (CH * nchunk)
    nin, nco = len(rows), len(consts)
    w = nh * HD

    def body(*refs):
        hist_ref, do_ref = refs[nin + nco], refs[nin + nco + 1]
        outs = refs[nin + nco + 2:nin + nco + 2 + nin]
        couts = refs[nin + nco + 2 + nin:nin + nco + 2 + nin + nco]
        ds = refs[nin + nco + 2 + nin + nco]

        @pl.when(pl.program_id(0) == 0)
        def _():
            ds[...] = jnp.zeros_like(ds)
            for c in couts:
                c[...] = jnp.zeros_like(c)

        tiles = [r[...].reshape(nb * CH, r.shape[2]) for r in refs[:nin]]
        _, vjp = jax.vjp(f, *tiles, *[r[...] for r in refs[nin:nin + nco]], hist_ref[0].astype(f32))
        grads = vjp((do_ref[...].reshape(nb * CH, out_k * w), ds[...]))
        for r, gk in zip(outs, grads[:nin]):
            r[...] = gk.reshape(r.shape).astype(r.dtype)
        for c, gk in zip(couts, grads[nin:nin + nco]):
            c[...] += gk
        ds[...] = grads[nin + nco]

    seq3 = lambda a: a.reshape(nb, nchunk * CH, a.shape[1])
    rev = lambda c, base: (0, nchunk - 1 - c, base)
    res = _pcall(
        side, body, name=name, grid=(nchunk,),
        in_specs=[pl.BlockSpec((nb, CH, k * w), functools.partial(rev, base=b)) for (_, b, k) in rows]
        + [pl.BlockSpec(c.shape, lambda c: (0, 0)) for c in consts]
        + [pl.BlockSpec((1, nb * w, HD), lambda c: (nchunk - 1 - c, 0, 0)), pl.BlockSpec((nb, CH, out_k * w), functools.partial(rev, base=0))],
        out_specs=[pl.BlockSpec((nb, CH, k * w), functools.partial(rev, base=0)) for (_, _, k) in rows]
        + [pl.BlockSpec(c.shape, lambda c: (0, 0)) for c in consts],
        out_shape=[S((nb, nchunk * CH, k * w), dt) for (_, _, k), dt in zip(rows, dtypes or [f32] * nin)]
        + [S(c.shape, f32) for c in consts],
        scratch_shapes=[pltpu.VMEM((nb * w, HD), f32)],
        compiler_params=pltpu.CompilerParams(dimension_semantics=("arbitrary",)),
    )(*[seq3(r[0]) for r in rows], *consts, hist, seq3(do))
    outs, extra = (res, None) if side is None else res
    outs = [o.reshape(n, o.shape[2]) for o in outs[:nin]] + list(outs[nin:])
    return outs if side is None else (outs, extra)


_VMEM_LIMIT = 56 * 2 ** 20
_VMEM_TILE_BUDGET = 40 * 2 ** 20


def _mm_tiles(m, n, k, sa, sb, so, sx, a_f32, b_f32, tn_fixed):
    best = None
    for tm in (1024, 512, 256, 128, 64):
        for tn in ((tn_fixed,) if tn_fixed else (1024, 768, 512, 384, 256, 128)):
            if m % tm or n % tn:
                continue
            need = 2 * (tm * k * sa + k * tn * sb + tm * tn * (so + sx)) + tm * tn * 4
            need += tm * k * (2 if sa == 4 else 0) + k * tn * (2 if sb == 4 else 0)
            need += tm * k * (4 if a_f32 else 0) + k * tn * (4 if b_f32 else 0)
            if need <= _VMEM_TILE_BUDGET and (best is None or (tm * tn, tm) > best[0]):
                best = ((tm * tn, tm), tm, tn)
    return best[1], best[2]


def _mm(name, a, b, dims, out_dtype, a_fn=None, b_fn=None, epi=None, extra=(), consts=(), outs=None, out_t=False, slab=False,
        side=None):
    m, kk = a.shape
    gathered = b.ndim == 3
    if gathered:
        nn = NDEV * _COLW if dims is NN else b.shape[1]
    else:
        nn = b.shape[1] if dims is NN else b.shape[0]
    kinds = [("tile", out_dtype)] if outs is None else list(outs)
    so = sum(jnp.dtype(dt).itemsize for kd, dt in kinds if kd != "rows")
    sx = sum(e.dtype.itemsize for e in extra)
    full_rows = bool(consts) or any(kd == "rows" for kd, _ in kinds)
    tm, tn = _mm_tiles(m, nn, kk, a.dtype.itemsize, b.dtype.itemsize, so, sx,
                       a_fn is not None, b_fn is not None,
                       _COLW if slab else (nn if full_rows else None))
    gblocks = tn // _COLW if (gathered and dims is NN) else 0
    assert not gblocks or (outs is None and not extra and not out_t and not slab and b_fn is None)
    nex, nco = len(extra), len(consts)

    def body(a_ref, b_ref, *rest):
        av = a_ref[...]
        if a_fn is not None:
            av = a_fn(av.astype(f32))
        av = av.astype(bf16)
        if gblocks:
            for gb in range(gblocks):
                part = _dot(av, b_ref[gb], NN)
                rest[nex + nco][:, _COLW * gb:_COLW * (gb + 1)] = (part if epi is None else epi(part)).astype(rest[nex + nco].dtype)
            return
        if gathered:
            acc = sum(_dot(av[:, _COLW * d:_COLW * (d + 1)], b_ref[d], NT) for d in range(NDEV))
        else:
            bv = b_ref[...]
            if b_fn is not None:
                bv = b_fn(bv.astype(f32))
            acc = _dot(av, bv.astype(bf16), dims)
        if epi is not None:
            acc = epi(acc, *[r[...] for r in rest[:nex + nco]])
        vals = acc if isinstance(acc, tuple) else (acc,)
        for (kd, _), o_ref, val in zip(kinds, rest[nex + nco:], vals):
            if kd == "rows":
                @pl.when(pl.program_id(0) == 0)
                def _():
                    o_ref[...] = jnp.zeros_like(o_ref)

                o_ref[...] += val
            elif kd == "tile_t" or out_t:
                o_ref[...] = val.T.astype(o_ref.dtype)
            elif slab:
                o_ref[0] = val.astype(o_ref.dtype)
            else:
                o_ref[...] = val.astype(o_ref.dtype)

    if gathered:
        bspec = (pl.BlockSpec((gblocks, kk, _COLW), lambda i, j: (j, 0, 0)) if dims is NN
                 else pl.BlockSpec((NDEV, tn, _COLW), lambda i, j: (0, j, 0)))
    else:
        bspec = pl.BlockSpec((kk, tn), lambda i, j: (0, j)) if dims is NN else pl.BlockSpec((tn, kk), lambda i, j: (j, 0))
    out_specs, out_shape = [], []
    for kd, dt in kinds:
        if kd == "rows":
            out_specs.append(pl.BlockSpec((1, nn), lambda i, j: (0, 0)))
            out_shape.append(S((1, nn), dt))
        elif kd == "tile_t" or out_t:
            out_specs.append(pl.BlockSpec((tn, tm), lambda i, j: (j, i)))
            out_shape.append(S((nn, m), dt))
        elif slab:
            out_specs.append(pl.BlockSpec((1, tm, tn), lambda i, j: (j, i, 0)))
            out_shape.append(S((nn // tn, m, tn), dt))
        else:
            out_specs.append(pl.BlockSpec((tm, tn), lambda i, j: (i, j)))
            out_shape.append(S((m, nn), dt))
    if outs is None:
        out_specs, out_shape = out_specs[0], out_shape[0]
    sem = ("arbitrary", "arbitrary") if any(kd == "rows" for kd, _ in kinds) else ("parallel", "parallel")
    return _pcall(
        side, body, name=name, grid=(m // tm, nn // tn),
        in_specs=[pl.BlockSpec((tm, kk), lambda i, j: (i, 0)), bspec]
        + [pl.BlockSpec((tm, tn), lambda i, j: (i, j)) for _ in extra]
        + [pl.BlockSpec(c.shape, lambda i, j: (0, 0)) for c in consts],
        out_specs=out_specs, out_shape=out_shape,
        compiler_params=pltpu.CompilerParams(dimension_semantics=sem, vmem_limit_bytes=_VMEM_LIMIT),
    )(a, b, *extra, *consts)


def _f_norm(x, g):
    return (x * lax.rsqrt(jnp.mean(x * x, axis=-1, keepdims=True) + EPS) * g,)


def _f_foxpre(zqk, gq, gk, pm):
    def nrm(t, g):
        return t * lax.rsqrt(_sel_rhs(t * t, pm) + EPS) * g
    return nrm(zqk[:, :FOX_W], gq), nrm(zqk[:, FOX_W:], gk)


def _chunk_cumsum(x):
    n = x.shape[0]
    r, c = _iota2((n, n), 0), _iota2((n, n), 1)
    tri = jnp.logical_and(r >= c, (r // CH) == (c // CH)).astype(f32)
    return _sel_lhs(tri, x)


def _f_gdngate(zs, eb, ea, alog_b, dt_b):
    beta = jax.nn.sigmoid(_sel_rhs(zs, eb))
    la = -jnp.exp(alog_b) * jax.nn.softplus(_sel_rhs(zs, ea) + dt_b)
    return beta, _chunk_cumsum(la)


def _f_conv(j, x, w):
    t = x.shape[0]
    y = x * w[3:4, :]
    for jj in range(3):
        sh = 3 - jj
        xs = jnp.concatenate([jnp.zeros((sh, x.shape[1]), f32), x[:t - sh, :]], axis=0)
        y = y + xs * w[jj:jj + 1, :]
    y = jax.nn.silu(y)
    yn = y * lax.rsqrt(jnp.sum(y * y, axis=-1, keepdims=True) + EPS)
    return (jnp.where(j < 2 * GDN_H, yn, y),)


def _head_rms(o, nh):
    outs = []
    for h in range(nh):
        oh = o[:, HD * h:HD * (h + 1)]
        outs.append(oh * lax.rsqrt(jnp.mean(oh * oh, axis=-1, keepdims=True) + EPS))
    return jnp.concatenate(outs, axis=1)


def _f_post0(fox_o, o, gg, on):
    return (jnp.concatenate([fox_o, _head_rms(o, GDN_H) * on * jax.nn.silu(gg)], axis=1),)


def _f_post1(o, zg, on):
    return (_head_rms(o, HG_H) * on * jax.nn.silu(zg),)


def _f_hpre(zqf, lbl):
    lb = jax.nn.sigmoid(lbl[1:2, :] - lbl[0:1, :])
    fg = lb + (1.0 - lb) * jax.nn.sigmoid(zqf[:, D:])
    return jax.nn.silu(zqf[:, :D]), 1.0 - fg, _chunk_cumsum(jnp.log(fg))


def _dotb(a, b, dims=NN):
    return _dot(a.astype(bf16), b.astype(bf16), dims)


def _dot3(a, b):
    ah, bh = a.astype(bf16), b.astype(bf16)
    al, bl = (a - ah.astype(f32)).astype(bf16), (b - bh.astype(f32)).astype(bf16)
    return _dot(ah, bh) + (_dot(ah, bl) + _dot(al, bh))


def _split(t, nh):
    return [t[CH * ck:CH * (ck + 1), HD * h:HD * (h + 1)] for ck in range(t.shape[0] // CH) for h in range(nh)]


def _merge(units, nh):
    return jnp.concatenate([jnp.concatenate(units[i:i + nh], axis=1) for i in range(0, len(units), nh)], axis=0)


def _inv_impl(amats):
    n = amats[0].shape[0]
    eye = jnp.where(_iota2((n, n), 0) == _iota2((n, n), 1), 1.0, 0.0).astype(f32)
    xs, ps = [eye - a for a in amats], list(amats)
    for _ in range(max(1, (n - 1).bit_length()) - 1):
        ps = [_dotb(p, p) for p in ps]
        xs = [x + _dotb(x, p) for x, p in zip(xs, ps)]
    for _ in range(3):
        rs = [eye - x - _dot3(a, x) for a, x in zip(amats, xs)]
        xs = [x + _dotb(x, r) for x, r in zip(xs, rs)]
    return tuple(xs)


@jax.custom_vjp
def _inv_unit_lower(amats):
    return _inv_impl(amats)


def _inv_fwd(amats):
    xs = _inv_impl(amats)
    return xs, xs


def _inv_bwd(xs, dxs):
    return (tuple(-_dotb(_dotb(x, dx, TN), x, NT) for x, dx in zip(xs, dxs)),)


_inv_unit_lower.defvjp(_inv_fwd, _inv_bwd)


@jax.custom_vjp
def _inv_given(amats, xs):
    return xs


def _inv_given_fwd(amats, xs):
    return xs, xs


def _inv_given_bwd(xs, dxs):
    return _inv_bwd(xs, dxs)[0], tuple(jnp.zeros_like(x) for x in xs)


_inv_given.defvjp(_inv_given_fwd, _inv_given_bwd)


def _f_gdn_intra(q, k, v, bb, gb, tinv_p=None):
    qs, ks, vs, bs, gs = (_split(t, GDN_H) for t in (q, k, v, bb, gb))
    r, cc = _iota2((CH, CH), 0), _iota2((CH, CH), 1)
    causal, strict = r >= cc, r > cc
    beta, g, gl = [b[:, :1] for b in bs], [x[:, :1] for x in gs], [x[CH - 1:CH, :1] for x in gs]
    decay = [jnp.exp(jnp.where(causal, x[:, :CH] - x[:, :CH].T, -jnp.inf)) for x in gs]
    kb = [ki * bi for ki, bi in zip(ks, beta)]
    amat = [jnp.where(strict, _dotb(kbi, ki, NT) * di, 0.0) for kbi, ki, di in zip(kb, ks, decay)]
    if tinv_p is None:
        tinv = _inv_unit_lower(tuple(amat))
    else:
        tinv = _inv_given(tuple(amat), tuple(x[:, :CH] for x in _split(tinv_p, GDN_H)))
    rhs = [jnp.concatenate([vi * bi, kbi * jnp.exp(gi)], axis=1) for vi, bi, kbi, gi in zip(vs, beta, kb, g)]
    uw = [_dotb(ti, ri) for ti, ri in zip(tinv, rhs)]
    qsc = [qi * (HD ** -0.5) for qi in qs]
    qk = [jnp.where(causal, _dotb(qi, ki, NT) * di, 0.0) for qi, ki, di in zip(qsc, ks, decay)]
    outs = ([x[:, :HD] for x in uw], [x[:, HD:] for x in uw],
            [jnp.concatenate([x, jnp.zeros_like(x)], axis=1) for x in qk],
            [qi * jnp.exp(gi) for qi, gi in zip(qsc, g)],
            [ki * jnp.exp(gli - gi) for ki, gli, gi in zip(ks, gl, g)],
            [jnp.broadcast_to(gli, (CH, HD)) for gli in gl])
    if tinv_p is None:
        outs += ([jnp.concatenate([x, jnp.zeros_like(x)], axis=1) for x in tinv],)
    return tuple(_merge(o, GDN_H) for o in outs)


def _f_gdn_gated(q, k, v, zs, eb, ea, alog_b, dt_b):
    return _f_gdn_intra(q, k, v, *_f_gdngate(zs, eb, ea, alog_b, dt_b))


def _f_gdn_gated_given(q, k, v, zs, tinv_p, eb, ea, alog_b, dt_b):
    return _f_gdn_intra(q, k, v, *_f_gdngate(zs, eb, ea, alog_b, dt_b), tinv_p=tinv_p)


def _f_gdn_inter(u, w, qkp, qd, kd, glb, st):
    us, ws, qks, qds, kds, gls = (_split(t, GDN_H) for t in (u, w, qkp, qd, kd, glb))
    sts = [st[HD * i:HD * (i + 1), :] for i in range(len(us))]
    vn = [ui - _dotb(wi, si) for ui, wi, si in zip(us, ws, sts)]
    o = [_dotb(qi, si) + _dotb(xi[:, :CH], vi) for qi, si, xi, vi in zip(qds, sts, qks, vn)]
    s2 = [si * jnp.exp(gi[:1, :1]) + _dotb(ki, vi, TN) for si, gi, ki, vi in zip(sts, gls, kds, vn)]
    return _merge(o, GDN_H), jnp.concatenate(s2, axis=0)


def _f_gdn_full(u, w, qkp, qd, kd, glb, fox_o, gg, on, st):
    o, s2 = _f_gdn_inter(u, w, qkp, qd, kd, glb, st)
    return _f_post0(fox_o, o, gg, on)[0], s2


def _f_hgrn_chunk(q, k, b, v, st):
    qs, ks, bs, vs = (_split(t, HG_H) for t in (q, k, b, v))
    sts = [st[HD * i:HD * (i + 1), :] for i in range(len(qs))]
    causal = _iota2((CH, CH), 0) >= _iota2((CH, CH), 1)
    bl, bm = [x[CH - 1:CH, :] for x in bs], [x[CH // 2 - 1:CH // 2, :] for x in bs]
    a = [jnp.where(causal, _dotb(qi * jnp.exp(bi - mi), ki * jnp.exp(mi - bi), NT), 0.0)
         for qi, ki, bi, mi in zip(qs, ks, bs, bm)]
    o = [_dotb(qi * jnp.exp(bi), si, NT) + _dotb(ai, vi) for qi, bi, si, ai, vi in zip(qs, bs, sts, a, vs)]
    s2 = [si * jnp.exp(li) + _dotb(vi, ki * jnp.exp(li - bi), TN) for si, li, vi, ki, bi in zip(sts, bl, vs, ks, bs)]
    return _merge(o, HG_H), jnp.concatenate(s2, axis=0)


def _f_hgrn_full(z, lbl, on, st):
    o, s2 = _f_hgrn_chunk(*_f_hpre(z[:, :2 * D], lbl), z[:, 2 * D:3 * D], st)
    return _f_post1(o, z[:, 3 * D:], on)[0], s2


def _fox_gate_fwd(z0, fbias, t, tc=256):
    n = z0.shape[0]
    nt = t // tc

    def body(zs_ref, b_ref, ccol_ref, crow_ref, carry):
        @pl.when(pl.program_id(1) == 0)
        def _():
            carry[...] = jnp.zeros_like(carry)

        ls = jnp.where(_iota2((tc, LANES), 1) < FOX_H, jax.nn.log_sigmoid(zs_ref[...] + b_ref[...]), 0.0)
        tri = (_iota2((tc, tc), 0) >= _iota2((tc, tc), 1)).astype(f32)
        c = _dot_sel(tri, ls, exact_lhs=True) + carry[...]
        carry[...] = c[tc - 1:tc, :]
        ccol_ref[...] = c
        crow_ref[0] = c.T[:FOX_H, :]

    return pl.pallas_call(
        body, name="fox_gate_fwd", grid=(n // t, nt),
        in_specs=[pl.BlockSpec((tc, LANES), lambda b, i: (b * nt + i, Z0_SMALL)), pl.BlockSpec((1, LANES), lambda b, i: (0, 0))],
        out_specs=[pl.BlockSpec((tc, LANES), lambda b, i: (b * nt + i, 0)), pl.BlockSpec((1, FOX_H, tc), lambda b, i: (b, 0, i))],
        out_shape=[S((n, LANES), f32), S((n // t, FOX_H, t), f32)],
        scratch_shapes=[pltpu.VMEM((1, LANES), f32)],
        compiler_params=pltpu.CompilerParams(dimension_semantics=("parallel", "arbitrary")),
    )(z0, fbias)


def _fox_gate_bwd(z0, fbias, dcq, dck, t, tc=256):
    n = z0.shape[0]
    nt = t // tc

    def body(zs_ref, b_ref, dcq_ref, dck_ref, dz_ref, db_ref, carry):
        first = jnp.logical_and(pl.program_id(0) == 0, pl.program_id(1) == 0)

        @pl.when(pl.program_id(1) == 0)
        def _():
            carry[...] = jnp.zeros_like(carry)

        @pl.when(first)
        def _():
            db_ref[...] = jnp.zeros_like(db_ref)

        dc = dcq_ref[0] + dcq_ref[1] + dcq_ref[2] + dcq_ref[3]
        drow = dck_ref[0, 0] + dck_ref[1, 0] + dck_ref[2, 0] + dck_ref[3, 0]
        eye = (_iota2((FOX_H, LANES), 0) == _iota2((FOX_H, LANES), 1)).astype(f32)
        dc = dc + _dot_sel(drow, eye, TN)
        triu = (_iota2((tc, tc), 0) <= _iota2((tc, tc), 1)).astype(f32)
        dls = _dot_sel(triu, dc, exact_lhs=True) + carry[...]
        carry[...] = dls[0:1, :]
        x = zs_ref[...] + b_ref[...]
        dz = jnp.where(_iota2((tc, LANES), 1) < FOX_H, dls * jax.nn.sigmoid(-x), 0.0)
        dz_ref[...] = dz
        db_ref[...] += jnp.sum(dz, axis=0, keepdims=True)

    def rev(b, i):
        return b * nt + (nt - 1 - i)

    return pl.pallas_call(
        body, name="fox_gate_bwd", grid=(n // t, nt),
        in_specs=[pl.BlockSpec((tc, LANES), lambda b, i: (rev(b, i), Z0_SMALL)), pl.BlockSpec((1, LANES), lambda b, i: (0, 0)),
                  pl.BlockSpec((4, tc, LANES), lambda b, i: (0, rev(b, i), 0)),
                  pl.BlockSpec((4, 1, FOX_H, tc), lambda b, i: (0, b, 0, nt - 1 - i))],
        out_specs=[pl.BlockSpec((tc, LANES), lambda b, i: (rev(b, i), 0)), pl.BlockSpec((1, LANES), lambda b, i: (0, 0))],
        out_shape=[S((n, LANES), f32), S((1, LANES), f32)],
        scratch_shapes=[pltpu.VMEM((1, LANES), f32)],
        compiler_params=pltpu.CompilerParams(dimension_semantics=("arbitrary", "arbitrary")),
    )(z0, fbias, dcq, dck)


def _fox_scores(hh, p, i, tq, q, k, ccol, crow):
    kmax = k.shape[0]
    lane = _iota2((1, LANES), 1)
    mh = (lane // FOX_D) == hh
    h = 2 * p + hh
    qh = jnp.where(mh, q, 0.0).astype(bf16)
    s = _dot(qh, k, NT) * (FOX_D ** -0.5)
    cq = jnp.sum(jnp.where(lane == h, ccol, 0.0), axis=1, keepdims=True)
    ck = jnp.sum(jnp.where(_iota2((FOX_H, 1), 0) == h, crow, 0.0), axis=0, keepdims=True)
    causal = _iota2((1, kmax), 1) <= (i * tq + _iota2((tq, 1), 0))
    s = jnp.where(causal, s + cq - ck, -jnp.inf)
    pe = jnp.exp(s - jnp.max(s, axis=1, keepdims=True))
    return mh, qh, pe, jnp.sum(pe, axis=1, keepdims=True)


def _fox_attn_fwd(qn, kn, z0, ccol, crow, t, tq=256, side=None):
    n = qn.shape[0]
    nq = t // tq

    def body(q_ref, k_ref, v_ref, ccol_ref, crow_ref, o_ref):
        p = pl.program_id(1)
        k, v, crow = k_ref[...].astype(bf16), v_ref[...].astype(bf16), crow_ref[0]
        for i in range(nq):
            rows, kmax = pl.ds(i * tq, tq), (i + 1) * tq
            q, cc = q_ref[rows, :], ccol_ref[rows, :]
            acc = jnp.zeros((tq, LANES), f32)
            for hh in range(2):
                mh, _, pe, l = _fox_scores(hh, p, i, tq, q, k[:kmax], cc, crow[:, :kmax])
                acc = jnp.where(mh, _dot(pe.astype(bf16), v[:kmax]) / l, acc)
            o_ref[rows, :] = acc

    seq = lambda b, p: (b, p)
    return _pcall(
        side, body, name="fox_attn_fwd", grid=(n // t, FOX_H // 2),
        in_specs=[pl.BlockSpec((t, LANES), seq), pl.BlockSpec((t, LANES), seq), pl.BlockSpec((t, LANES), lambda b, p: (b, Z0_FV + p)),
                  pl.BlockSpec((t, LANES), lambda b, p: (b, 0)), pl.BlockSpec((1, FOX_H, t), lambda b, p: (b, 0, 0))],
        out_specs=pl.BlockSpec((t, LANES), seq),
        out_shape=S((n, FOX_W), f32),
        compiler_params=pltpu.CompilerParams(dimension_semantics=("parallel", "parallel")),
    )(qn, kn, z0, ccol, crow)


def _fox_attn_bwd(qn, kn, z0, ccol, crow, do, t, tq=256, side=None):
    n = qn.shape[0]
    nq = t // tq
    nb = n // t

    def body(q_ref, k_ref, v_ref, ccol_ref, crow_ref, do_ref, dq_ref, dk_ref, dv_ref, dcq_ref, dck_ref):
        p = pl.program_id(1)
        dk_ref[...] = jnp.zeros_like(dk_ref)
        dv_ref[...] = jnp.zeros_like(dv_ref)
        dck_ref[...] = jnp.zeros_like(dck_ref)
        kf, v, crow = k_ref[...], v_ref[...].astype(bf16), crow_ref[0]
        k = kf.astype(bf16)
        lane = _iota2((1, LANES), 1)
        sub = _iota2((FOX_H, 1), 0)
        scale = FOX_D ** -0.5
        for i in range(nq):
            rows, kmax = pl.ds(i * tq, tq), (i + 1) * tq
            q, cc, dout = q_ref[rows, :], ccol_ref[rows, :], do_ref[rows, :]
            dq = jnp.zeros((tq, LANES), f32)
            dcq = jnp.zeros((tq, LANES), f32)
            for hh in range(2):
                mh, qh, pe, l = _fox_scores(hh, p, i, tq, q, k[:kmax], cc, crow[:, :kmax])
                pr = pe / l
                doh = jnp.where(mh, dout, 0.0).astype(bf16)
                dp = _dot(doh, v[:kmax], NT)
                ds = pr * (dp - jnp.sum(pr * dp, axis=1, keepdims=True))
                dsb = ds.astype(bf16)
                dq = dq + _dot(dsb, jnp.where(mh, kf[:kmax], 0.0).astype(bf16)) * scale
                dk_ref[:kmax, :] += _dot(dsb, qh, TN) * scale
                dv_ref[:kmax, :] += _dot(pr.astype(bf16), doh, TN)
                h = 2 * p + hh
                dcq = dcq + jnp.where(lane == h, jnp.sum(ds, axis=1, keepdims=True), 0.0)
                dck_ref[0, 0, :, :kmax] += jnp.where(sub == h, -jnp.sum(ds, axis=0, keepdims=True), 0.0)
            dq_ref[rows, :] = dq
            dcq_ref[0, rows, :] = dcq

    seq = lambda b, p: (b, p)
    return _pcall(
        side, body, name="fox_attn_bwd", grid=(nb, FOX_H // 2),
        in_specs=[pl.BlockSpec((t, LANES), seq), pl.BlockSpec((t, LANES), seq), pl.BlockSpec((t, LANES), lambda b, p: (b, Z0_FV + p)),
                  pl.BlockSpec((t, LANES), lambda b, p: (b, 0)), pl.BlockSpec((1, FOX_H, t), lambda b, p: (b, 0, 0)),
                  pl.BlockSpec((t, LANES), seq)],
        out_specs=[pl.BlockSpec((t, LANES), seq), pl.BlockSpec((t, LANES), seq), pl.BlockSpec((t, LANES), seq),
                   pl.BlockSpec((1, t, LANES), lambda b, p: (p, b, 0)), pl.BlockSpec((1, 1, FOX_H, t), lambda b, p: (p, b, 0, 0))],
        out_shape=[S((n, FOX_W), f32), S((n, FOX_W), f32), S((n, FOX_W), f32), S((4, n, LANES), f32), S((4, nb, FOX_H, t), f32)],
        compiler_params=pltpu.CompilerParams(dimension_semantics=("parallel", "parallel")),
    )(qn, kn, z0, ccol, crow, do)


def _adamw_math(w, g, m, v):
    m = ADAM_B1 * m + (1.0 - ADAM_B1) * g
    v = ADAM_B2 * v + (1.0 - ADAM_B2) * (g * g)
    m_hat = m / (1.0 - ADAM_B1 ** ADAM_STEP)
    v_hat = v / (1.0 - ADAM_B2 ** ADAM_STEP)
    return -ADAM_LR * (m_hat / (jnp.sqrt(v_hat) + ADAM_EPS) + ADAM_WD * w), m, v


def _adamw_big(name, idx, gmine, recv1, recv2, w, m, v):
    r, wc = w.shape
    c = gmine.shape[2]
    tr = min(r, 256)

    def body(idx_ref, gm_ref, r1_ref, r2_ref, w_ref, m_ref, v_ref, g_ref, d_ref, nm_ref, nv_ref):
        g = gm_ref[0].astype(f32) + r1_ref[0].astype(f32)
        for k in range(3):
            g = g + r2_ref[k].astype(f32)
        g = g[:, :wc]
        d, nm, nv = _adamw_math(w_ref[...], g, m_ref[...], v_ref[...])
        g_ref[...] = g
        d_ref[...] = d
        nm_ref[...] = nm
        nv_ref[...] = nv

    row = pl.BlockSpec((tr, wc), lambda i, s: (i, 0))
    return pl.pallas_call(
        body, name=name,
        grid_spec=pltpu.PrefetchScalarGridSpec(
            num_scalar_prefetch=1, grid=(r // tr,),
            in_specs=[pl.BlockSpec((1, tr, c), lambda i, s: (s[0], i, 0)), pl.BlockSpec((1, tr, c), lambda i, s: (s[1], i, 0)),
                      pl.BlockSpec((3, tr, c), lambda i, s: (0, i, 0)), row, row, row],
            out_specs=[row, row, row, row]),
        out_shape=[S((r, wc), f32)] * 4,
        compiler_params=pltpu.CompilerParams(dimension_semantics=("parallel",)),
    )(idx, gmine, recv1, recv2, w, m, v)


def _pair_sum(name, idx, gmine, recv1):
    _, r, c = gmine.shape
    g4 = gmine.reshape(4, 2, r, c)

    def body(idx_ref, gm_ref, r1_ref, o_ref):
        o_ref[0] = (gm_ref[0, 0].astype(f32) + r1_ref[0].astype(f32)).astype(bf16)

    return pl.pallas_call(
        body, name=name,
        grid_spec=pltpu.PrefetchScalarGridSpec(
            num_scalar_prefetch=1, grid=(4,),
            in_specs=[pl.BlockSpec((1, 1, r, c), lambda ch, s: (ch, s[0], 0, 0)), pl.BlockSpec((1, r, c), lambda ch, s: (ch, 0, 0))],
            out_specs=pl.BlockSpec((1, r, c), lambda ch, s: (ch, 0, 0))),
        out_shape=S((4, r, c), bf16),
        compiler_params=pltpu.CompilerParams(dimension_semantics=("parallel",)),
    )(idx, g4, recv1)


def _adamw_small(parts, w, m, v, own_mask, own_g):
    def body(p_ref, w_ref, m_ref, v_ref, mask_ref, og_ref, g_ref, d_ref, nm_ref, nv_ref):
        g = p_ref[0]
        for k in range(1, NDEV):
            g = g + p_ref[k]
        g_ref[...] = g
        ge = jnp.where(mask_ref[...] > 0.5, og_ref[...], g)
        d, nm, nv = _adamw_math(w_ref[...], ge, m_ref[...], v_ref[...])
        d_ref[...] = d
        nm_ref[...] = nm
        nv_ref[...] = nv

    return pl.pallas_call(body, name="adamw_small", out_shape=[S(w.shape, f32)] * 4)(parts, w, m, v, own_mask, own_g)


def _sum_parts(parts):
    def body(p_ref, g_ref):
        g = p_ref[0]
        for k in range(1, NDEV):
            g = g + p_ref[k]
        g_ref[...] = g

    return pl.pallas_call(body, name="sum_parts", out_shape=S(parts.shape[1:], f32))(parts)


def _me():
    return lax.axis_index("x"), lax.axis_index("y"), lax.axis_index("c")


def _hbm_specs(n):
    return [pl.BlockSpec(memory_space=pl.ANY)] * n


def _allgather(name, xs):
    na = len(xs)

    def body(*refs):
        x_refs, out_refs = refs[:na], refs[na:2 * na]
        send_sems, recv_sems, local_sems = refs[2 * na:]
        mx, my, mc = _me()
        me, sib = (mx, my, mc), (mx, my, 1 - mc)
        chips = [(1 - mx, my), (mx, 1 - my), (1 - mx, 1 - my)]

        def slab(a, px, py, pc):
            return out_refs[a].at[4 * px + 2 * py + pc]

        def copy(a, k, block, to, own=False):
            return pltpu.make_async_remote_copy(
                src_ref=x_refs[a] if own else slab(a, *block), dst_ref=slab(a, *block),
                send_sem=send_sems.at[7 * a + k], recv_sem=recv_sems.at[7 * a + k], device_id=to, device_id_type=MESH)

        mine = [pltpu.make_async_copy(x_refs[a], slab(a, *me), local_sems.at[a]) for a in range(na)]
        first = []
        for a in range(na):
            mine[a].start()
            first += [copy(a, 0, me, sib, own=True)] + [copy(a, 1 + j, me, (*chip, mc), own=True) for j, chip in enumerate(chips)]
        for cp in first:
            cp.start()
        passed = []
        for j, chip in enumerate(chips):
            for a in range(na):
                copy(a, 1 + j, (*chip, mc), me).wait_recv()
                passed.append(copy(a, 4 + j, (*chip, mc), sib))
                passed[-1].start()
        for a in range(na):
            copy(a, 0, sib, me).wait_recv()
            for j, chip in enumerate(chips):
                copy(a, 4 + j, (*chip, 1 - mc), me).wait_recv()
        for cp in first + passed:
            cp.wait_send()
        for cp in mine:
            cp.wait()

    return pl.pallas_call(
        body, name=name, out_shape=[S((NDEV,) + x.shape, x.dtype) for x in xs],
        in_specs=_hbm_specs(na), out_specs=_hbm_specs(na),
        scratch_shapes=[pltpu.SemaphoreType.DMA((7 * na,)), pltpu.SemaphoreType.DMA((7 * na,)), pltpu.SemaphoreType.DMA((na,))],
    )(*xs)


def _rs_sibling(gs):
    na = len(gs)

    def body(*refs):
        g_refs, out_refs, send_sems, recv_sems = refs[:na], refs[na:2 * na], refs[2 * na], refs[2 * na + 1]
        mx, my, mc = _me()
        cps = [pltpu.make_async_remote_copy(
            src_ref=g_refs[a].at[2 * ch + 1 - mc], dst_ref=out_refs[a].at[ch], send_sem=send_sems.at[4 * a + ch],
            recv_sem=recv_sems.at[4 * a + ch], device_id=(mx, my, 1 - mc), device_id_type=MESH)
            for a in range(na) for ch in range(4)]
        for cp in cps:
            cp.start()
        for cp in cps:
            cp.wait_recv()
        for cp in cps:
            cp.wait_send()

    return pl.pallas_call(
        body, name="rs_sibling", out_shape=[S((4,) + g.shape[1:], g.dtype) for g in gs],
        in_specs=_hbm_specs(na), out_specs=_hbm_specs(na),
        scratch_shapes=[pltpu.SemaphoreType.DMA((4 * na,)), pltpu.SemaphoreType.DMA((4 * na,))],
    )(*gs)


def _side_allgather(xs):
    na = len(xs)

    def mk(x_refs, out_refs, sems):
        send_sems, recv_sems, local_sems = sems
        mx, my, mc = _me()
        me, sib = (mx, my, mc), (mx, my, 1 - mc)
        chips = [(1 - mx, my), (mx, 1 - my), (1 - mx, 1 - my)]

        def slab(a, px, py, pc):
            return out_refs[a].at[4 * px + 2 * py + pc]

        def copy(a, k, block, to, own=False):
            return pltpu.make_async_remote_copy(
                src_ref=x_refs[a] if own else slab(a, *block), dst_ref=slab(a, *block),
                send_sem=send_sems.at[7 * a + k], recv_sem=recv_sems.at[7 * a + k], device_id=to, device_id_type=MESH)

        mine = [pltpu.make_async_copy(x_refs[a], slab(a, *me), local_sems.at[a]) for a in range(na)]
        first = []
        for a in range(na):
            first += [copy(a, 0, me, sib, own=True)] + [copy(a, 1 + j, me, (*chip, mc), own=True) for j, chip in enumerate(chips)]
        return me, sib, chips, mc, copy, mine, first

    def start(x_refs, out_refs, sems):
        *_, mine, first = mk(x_refs, out_refs, sems)
        for cp in mine + first:
            cp.start()

    def finish(x_refs, out_refs, sems):
        me, sib, chips, mc, copy, mine, first = mk(x_refs, out_refs, sems)
        passed = []
        for j, chip in enumerate(chips):
            for a in range(na):
                copy(a, 1 + j, (*chip, mc), me).wait_recv()
                passed.append(copy(a, 4 + j, (*chip, mc), sib))
                passed[-1].start()
        for a in range(na):
            copy(a, 0, sib, me).wait_recv()
            for j, chip in enumerate(chips):
                copy(a, 4 + j, (*chip, 1 - mc), me).wait_recv()
        for cp in first + passed:
            cp.wait_send()
        for cp in mine:
            cp.wait()

    scratch = [pltpu.SemaphoreType.DMA((7 * na,)), pltpu.SemaphoreType.DMA((7 * na,)), pltpu.SemaphoreType.DMA((na,))]
    return _Side(xs, [S((NDEV,) + x.shape, x.dtype) for x in xs], scratch, start, finish)


def _side_exchange(arrs, nslot, out_slots, route):
    na = len(arrs)

    def copies(in_refs, out_refs, sems):
        send_sems, recv_sems = sems
        return [pltpu.make_async_remote_copy(
            src_ref=in_refs[a].at[src], dst_ref=out_refs[a].at[k], send_sem=send_sems.at[nslot * a + k],
            recv_sem=recv_sems.at[nslot * a + k], device_id=to, device_id_type=MESH)
            for a in range(na) for k, (src, to) in enumerate(route(*_me()))]

    def start(in_refs, out_refs, sems):
        for cp in copies(in_refs, out_refs, sems):
            cp.start()

    def finish(in_refs, out_refs, sems):
        cps = copies(in_refs, out_refs, sems)
        for cp in cps:
            cp.wait_recv()
        for cp in cps:
            cp.wait_send()

    scratch = [pltpu.SemaphoreType.DMA((nslot * na,)), pltpu.SemaphoreType.DMA((nslot * na,))]
    return _Side(arrs, [S((out_slots,) + x.shape[1:], x.dtype) for x in arrs], scratch, start, finish)


def _side_rs_sibling(gs):
    return _side_exchange(gs, 4, 4, lambda mx, my, mc: [(2 * ch + 1 - mc, (mx, my, 1 - mc)) for ch in range(4)])


def _side_rs_chips(ps):
    return _side_exchange(ps, 3, 3, lambda mx, my, mc: [(2 * cx + cy, (cx, cy, mc)) for cx, cy in
                                                        [(1 - mx, my), (mx, 1 - my), (1 - mx, 1 - my)]])


_COLW = 512
_COL_NAMES = ("l0_w_in", "l0_w_ff1", "l1_w_in", "l1_w_ff1")
_ROW_NAMES = ("l0_w_out", "l0_w_ff2", "l1_w_out", "l1_w_ff2")
_BIG_NAMES = _COL_NAMES + _ROW_NAMES


def _full_weight(gathered, name, ncols):
    if name in _ROW_NAMES:
        return gathered.reshape(-1, D)
    return gathered[:, :, :ncols].transpose(1, 0, 2).reshape(D, NDEV * ncols)


def _regroup_w_in0(w):
    main = jnp.concatenate([w[:, 0:1536], w[:, 1544:3080], w[:, 3088:3600]], axis=1)
    small = jnp.concatenate([w[:, 1536:1544], w[:, 3080:3088]], axis=1)
    return jnp.concatenate([main, small, jnp.zeros((D, ZW0 - 3584 - 16), w.dtype)], axis=1)


def _ungroup_w_in0(g):
    return jnp.concatenate([g[:, 0:1536], g[:, 3584:3592], g[:, 1536:3072], g[:, 3592:3600], g[:, 3072:3584]], axis=1)


def _col_slabs(g, ncols):
    g = g.reshape(D, NDEV, ncols).transpose(1, 0, 2)
    return jnp.pad(g, ((0, 0), (0, 0), (0, _COLW - ncols)))


def _sq(t):
    return t * t


def _epi_res_norm(acc, res, gain):
    y = acc + res
    h = _f_norm(y, gain)[0]
    return y, h, h


_RES_NORM_OUTS = [("tile", f32), ("tile", bf16), ("tile_t", bf16)]


def _epi_norm_bwd(acc, x, dres, gain):
    _, vjp = jax.vjp(lambda xx, gg: _f_norm(xx, gg)[0], x, gain)
    dx, dgain = vjp(acc)
    dx = dx + dres
    return dx, dgain, dx


_NORM_BWD_OUTS = [("tile", f32), ("rows", f32), ("tile_t", bf16)]


def _epi_loss(acc, res, tgt):
    e = acc + res - tgt
    dy = e * (1.0 / D)
    return dy, dy, jnp.sum(e * e, axis=0, keepdims=True)


_LOSS_OUTS = [("tile", f32), ("tile_t", bf16), ("rows", f32)]


def _mlp_fwd(tag, x, h, w1, w2, epi, extra, consts, outs):
    a = _mm(f"{tag}_ff1", h, w1, NN, bf16, epi=lambda acc: jnp.maximum(acc, 0.0))
    return _mm(f"{tag}_ff2", a, w2, NN, f32, a_fn=_sq, epi=epi, extra=(x,) + tuple(extra), consts=consts, outs=outs), a


def _mlp_bwd(tag, x, gain, w1, w2, ht, a, dy, dyt, side=None):
    da = _mm(f"{tag}_ff2_dx", dy, w2, NT, bf16, epi=lambda acc, av: acc * 2.0 * av.astype(f32), extra=(a,), side=side)
    if side is not None:
        da, side_res = da
    dw2 = _mm(f"{tag}_ff2_dw", dyt, a, NN, bf16, b_fn=_sq, out_t=True)
    res = _mm(f"{tag}_ff1_dx", da, w1, NT, f32, epi=_epi_norm_bwd, extra=(x, dy), consts=(gain,), outs=_NORM_BWD_OUTS)
    dw1 = _mm(f"{tag}_ff1_dw", ht, da, NN, bf16, slab=True)
    return (res, dw1, dw2) if side is None else ((res, dw1, dw2), side_res)


def _row(v):
    return v.reshape(1, -1).astype(f32)


_L0_REST = ("l0_w_ff1", "l0_w_out", "l0_w_ff2")
_L1_MIX = ("l1_w_in", "l1_w_out")
_L1_FFN = ("l1_w_ff1", "l1_w_ff2")
_GRAD_A = ("l1_w_ff1", "l1_w_ff2", "l1_w_out", "l1_w_in")
_GRAD_B = ("l0_w_ff1", "l0_w_ff2", "l0_w_out")
_IN_PLACE = ("l0_w_ff1", "l1_w_in", "l1_w_ff1")


def _train(x, tgt, args, mom, var, t):
    n = x.shape[0]
    nchunk = t // CH
    mx, my, mc = _me()
    dev, chip = 4 * mx + 2 * my + mc, 2 * mx + my
    core_idx = jnp.reshape(mc, (1,)).astype(jnp.int32)
    own_idx = jnp.stack([dev, chip]).astype(jnp.int32)
    ncols = {nm: args[nm].shape[1] for nm in _COL_NAMES}
    conv_cols = args["l0_gdn_conv"].shape[1]
    g, big, w = {}, {}, {}

    def send(nm):
        a = args[nm].astype(bf16)
        return jnp.pad(a, ((0, 0), (0, _COLW - ncols[nm]))) if nm in _COL_NAMES else a

    def take(names, gathered):
        for nm, arr in zip(names, gathered):
            w[nm] = arr if nm in _IN_PLACE else _full_weight(arr, nm, ncols.get(nm, 0))

    def by_dev(nm, ga):
        return ga if nm in _COL_NAMES else ga.reshape(NDEV, -1, D)

    def pair(names, gs, r1s):
        return [_pair_sum(f"rs_pair_sum_{nm}", core_idx, ga, r1) for nm, ga, r1 in zip(names, gs, r1s)]

    def adam(names, gs, r1s, r2s):
        for nm, ga, r1, r2 in zip(names, gs, r1s, r2s):
            big[nm] = _adamw_big(f"adamw_{nm}", own_idx, ga, r1, r2, args[nm], mom[nm], var[nm])


    li = jnp.arange(FOX_W)
    pm = jnp.where((li[:, None] // FOX_D) == (li[None, :] // FOX_D), 1.0 / FOX_D, 0.0).astype(f32)
    lane_head = jnp.arange(GDN_W) // HD
    sel = lambda first_lane: (jnp.arange(LANES)[:, None] == (first_lane + lane_head)[None, :]).astype(f32)
    e_beta, e_alpha = sel(FOX_H), sel(FOX_H + GDN_H)
    alog_b, dt_b = _row(jnp.repeat(args["l0_gdn_A_log"], HD)), _row(jnp.repeat(args["l0_gdn_dt_bias"], HD))
    gq_t, gk_t = _row(jnp.tile(args["l0_fox_q_norm"], FOX_H)), _row(jnp.tile(args["l0_fox_k_norm"], FOX_H))
    on0_t, on1_t = _row(jnp.tile(args["l0_gdn_o_norm"], GDN_H)), _row(jnp.tile(args["l1_hgrn_o_norm"], HG_H))
    fbias = jnp.pad(_row(args["l0_fox_f_bias"]), ((0, 0), (0, LANES - FOX_H)))
    g0m, g0f, g1m, g1f = (_row(args[k]) for k in ("l0_mix_norm", "l0_ffn_norm", "l1_mix_norm", "l1_ffn_norm"))
    lbl = args["hgrn_lb_logits"].astype(f32)

    first = [send("l0_w_in"), jnp.pad(args["l0_gdn_conv"], ((0, 4), (0, LANES * 2 - conv_cols)))]
    (h0, h0t), first = _tok_fwd("l0_mix_norm", _f_norm, [(x, D, 0)], [(g0m, None, 0)], [(D, bf16)], 256, also_t=(0,),
                                side=_side_allgather(first))
    take(("l0_w_in",), first[:1])
    w_in0 = _regroup_w_in0(w["l0_w_in"])
    wconv = first[1][:, :4, :conv_cols].transpose(1, 0, 2).reshape(4, NDEV * conv_cols)
    z0 = _mm("l0_in", h0, w_in0, NN, f32)
    qk_rows = [(z0, 2 * FOX_W, 0)]
    qk_consts = [(gq_t, None, 0), (gk_t, None, 0), (pm, None, 0)]
    qn, kn = _tok_fwd("fox_pre", _f_foxpre, qk_rows, qk_consts, [(FOX_W, f32)] * 2, 256)
    ccol, crow = _fox_gate_fwd(z0, fbias, t)
    fox_o, got = _fox_attn_fwd(qn, kn, z0, ccol, crow, t, side=_side_allgather([send(nm) for nm in _L0_REST]))
    take(_L0_REST, got)
    conv_rows, conv_consts = [(z0, LANES, Z0_GQKV)], [(wconv, LANES, 0)]
    (qkv,) = _tok_fwd("gdn_conv", _f_conv, conv_rows, conv_consts, [(LANES, f32)], t, ncb=12, with_j=True)
    gate_consts = [(e_beta, None, 0), (e_alpha, None, 0), (alog_b, None, 0), (dt_b, None, 0)]
    intra_rows = [(qkv, GDN_W, 0), (qkv, GDN_W, 1), (qkv, GDN_W, 2), (z0, LANES, Z0_SMALL)]
    intra, got = _tok_fwd("gdn_intra", _f_gdn_gated, intra_rows, gate_consts, [(GDN_W, f32)] * 7, 2 * CH,
                          side=_side_allgather([send(nm) for nm in _L1_MIX]))
    take(_L1_MIX, got)
    inter_rows = [(a, 0, 1) for a in intra[:6]]
    full_rows = inter_rows + [(fox_o, 0, 1), (z0, Z0_GG // 4, 1)]
    (cat0, gdn_hist), got = _scan_fwd("gdn_scan", _f_gdn_full, full_rows, GDN_H, nchunk, consts=(on0_t,), out_dtype=bf16, out_k=2,
                                      side=_side_allgather([send(nm) for nm in _L1_FFN[1:]]))
    take(_L1_FFN[1:], got)
    x1, hf0, hf0t = _mm("l0_out", cat0, w["l0_w_out"], NN, f32, epi=_epi_res_norm, extra=(x,), consts=(g0f,), outs=_RES_NORM_OUTS)
    (x2, h1, h1t), a0 = _mlp_fwd("l0", x1, hf0, w["l0_w_ff1"], w["l0_w_ff2"], _epi_res_norm, (), (g1m,), _RES_NORM_OUTS)

    z1 = _mm("l1_in", h1, w["l1_w_in"], NN, f32)
    (cat1, hg_hist), got = _scan_fwd("hgrn_scan", _f_hgrn_full, [(z1, 0, 4)], HG_H, nchunk, consts=(lbl, on1_t), out_dtype=bf16,
                                     side=_side_allgather([send(nm) for nm in _L1_FFN[:1]]))
    take(_L1_FFN[:1], got)
    x3, hf1, hf1t = _mm("l1_out", cat1, w["l1_w_out"], NN, f32, epi=_epi_res_norm, extra=(x2,), consts=(g1f,), outs=_RES_NORM_OUTS)
    (dy, dyt, loss_row), a1 = _mlp_fwd("l1", x3, hf1, w["l1_w_ff1"], w["l1_w_ff2"], _epi_loss, (tgt,), (), _LOSS_OUTS)

    (dx3, g["l1_ffn_norm"], dx3t), ga_ff1, ga_ff2 = _mlp_bwd("l1", x3, g1f, w["l1_w_ff1"], w["l1_w_ff2"], hf1t, a1, dy, dyt)
    dcat1 = _mm("l1_out_dx", dx3, w["l1_w_out"], NT, f32)
    ga_out = _mm("l1_out_dw", dx3t, cat1, NN, bf16, out_t=True)
    dz1, dlbl, don1 = _scan_bwd("hgrn_scan_bwd", _f_hgrn_full, [(z1, 0, 4)], hg_hist, dcat1, HG_H, nchunk, dtypes=[bf16],
                                consts=(lbl, on1_t))
    dx2, g["l1_mix_norm"], dx2t = _mm("l1_in_dx", dz1, w["l1_w_in"], NT, f32, epi=_epi_norm_bwd, extra=(x2, dx3), consts=(g1m,),
                                      outs=_NORM_BWD_OUTS)
    ga_in = _mm("l1_in_dw", h1t, dz1, NN, bf16, slab=True)
    g["l1_hgrn_o_norm"] = don1.reshape(HG_H, HD).sum(0)
    g["hgrn_lb_logits"] = dlbl
    gs_a = [by_dev(nm, ga) for nm, ga in zip(_GRAD_A, (ga_ff1, ga_ff2, ga_out, ga_in))]

    ((dx1, g["l0_ffn_norm"], dx1t), gb_ff1, gb_ff2), r1_a = _mlp_bwd("l0", x1, g0f, w["l0_w_ff1"], w["l0_w_ff2"], hf0t, a0, dx2, dx2t,
                                                                  side=_side_rs_sibling(gs_a))
    pairs_a = pair(_GRAD_A, gs_a, r1_a)
    dcat0 = _mm("l0_out_dx", dx1, w["l0_w_out"], NT, f32)
    gb_out = _mm("l0_out_dw", dx1t, cat0, NN, bf16, out_t=True)
    gs_b = [by_dev(nm, ga) for nm, ga in zip(_GRAD_B, (gb_ff1, gb_ff2, gb_out))]
    *dinter, dfox_o, dgg, don0 = _scan_bwd("gdn_scan_bwd", _f_gdn_full, full_rows, gdn_hist, dcat0, GDN_H, nchunk, consts=(on0_t,),
                                           out_k=2, dtypes=[f32] * 7 + [bf16])
    (dqn, dkn, dfv, dcq, dck), got = _fox_attn_bwd(qn, kn, z0, ccol, crow, dfox_o, t,
                                                   side=_join_sides([_side_rs_chips(pairs_a), _side_rs_sibling(gs_b)]))
    r2_a, r1_b = got[:len(_GRAD_A)], got[len(_GRAD_A):]
    adam(_GRAD_A, gs_a, r1_a, r2_a)
    pairs_b = pair(_GRAD_B, gs_b, r1_b)
    (dqkv, dzs_g, dalog_b, ddt_b), r2_b = _tok_bwd(
        "gdn_intra_bwd", _f_gdn_gated_given, intra_rows + [(intra[6], GDN_W, 0)], gate_consts, [(a, GDN_W) for a in dinter],
        2 * CH, [0, 1, 2, 3], [2, 3], ncat=3, side=_side_rs_chips(pairs_b))
    adam(_GRAD_B, gs_b, r1_b, r2_b)
    dgqkv, dwconv = _tok_bwd("gdn_conv_bwd", _f_conv, conv_rows, conv_consts, [(dqkv, LANES)], t, [0], [0], ncb=12, with_j=True,
                             drow_dtype=bf16)
    dzs_f, dfb = _fox_gate_bwd(z0, fbias, dcq, dck, t)
    dzqk, dgq_t, dgk_t = _tok_bwd("fox_pre_bwd", _f_foxpre, qk_rows, qk_consts, [(dqn, FOX_W), (dkn, FOX_W)], 256, [0], [0, 1],
                                  drow_dtype=bf16)
    dz0 = jnp.concatenate([dzqk, dfv.astype(bf16), dgqkv, dgg, (dzs_g + dzs_f).astype(bf16), jnp.zeros((n, ZW0 - 3712), bf16)], axis=1)
    gs_c = [_col_slabs(_ungroup_w_in0(_mm("l0_in_dw", h0t, dz0, NN, bf16)), ncols["l0_w_in"])]
    r1_c = _rs_sibling(gs_c)
    pairs_c = pair(("l0_w_in",), gs_c, r1_c)
    (dx, g["l0_mix_norm"], _), r2_c = _mm("l0_in_dx", dz0, w_in0, NT, f32, epi=_epi_norm_bwd, extra=(x, dx1), consts=(g0m,),
                                          outs=_NORM_BWD_OUTS, side=_side_rs_chips(pairs_c))
    adam(("l0_w_in",), gs_c, r1_c, r2_c)
    g["l0_fox_q_norm"] = dgq_t.reshape(FOX_H, FOX_D).sum(0)
    g["l0_fox_k_norm"] = dgk_t.reshape(FOX_H, FOX_D).sum(0)
    g["l0_fox_f_bias"] = dfb[0, :FOX_H]
    g["l0_gdn_conv"] = dwconv
    g["l0_gdn_A_log"] = dalog_b.reshape(GDN_H, HD).sum(1)
    g["l0_gdn_dt_bias"] = ddt_b.reshape(GDN_H, HD).sum(1)
    g["l0_gdn_o_norm"] = don0.reshape(GDN_H, HD).sum(0)
    return loss_row, dx, g, big


_NAMES = ("l0_mix_norm", "l0_w_in", "l0_fox_q_norm", "l0_fox_k_norm", "l0_fox_f_bias", "l0_gdn_conv", "l0_gdn_A_log",
          "l0_gdn_dt_bias", "l0_gdn_o_norm", "l0_w_out", "l0_ffn_norm", "l0_w_ff1", "l0_w_ff2", "l1_mix_norm", "l1_w_in",
          "l1_hgrn_o_norm", "l1_w_out", "l1_ffn_norm", "l1_w_ff1", "l1_w_ff2", "hgrn_lb_logits")
_SMALL_NAMES = tuple(nm for nm in _NAMES if nm not in _BIG_NAMES)
_SMALL_ROWS = 16


def _pack_small(vals):
    flat = jnp.concatenate([vals[nm].reshape(-1).astype(f32) for nm in _SMALL_NAMES])
    return jnp.pad(flat, (0, _SMALL_ROWS * D - flat.shape[0])).reshape(_SMALL_ROWS, D)


def _unpack_small(packed, shapes):
    flat = packed.reshape(-1)
    out, off = {}, 0
    for nm in _SMALL_NAMES:
        size = 1
        for s in shapes[nm]:
            size *= s
        out[nm] = flat[off:off + size].reshape(shapes[nm])
        off += size
    return out, off


def kernel(x, l0_mix_norm, l0_w_in, l0_fox_q_norm, l0_fox_k_norm, l0_fox_f_bias, l0_gdn_conv, l0_gdn_A_log, l0_gdn_dt_bias, l0_gdn_o_norm, l0_w_out, l0_ffn_norm, l0_w_ff1, l0_w_ff2, l1_mix_norm, l1_w_in, l1_hgrn_o_norm, l1_w_out, l1_ffn_norm, l1_w_ff1, l1_w_ff2, hgrn_lb_logits, loss_target, m_l0_mix_norm, m_l0_w_in, m_l0_fox_q_norm, m_l0_fox_k_norm, m_l0_fox_f_bias, m_l0_gdn_conv, m_l0_gdn_A_log, m_l0_gdn_dt_bias, m_l0_gdn_o_norm, m_l0_w_out, m_l0_ffn_norm, m_l0_w_ff1, m_l0_w_ff2, m_l1_mix_norm, m_l1_w_in, m_l1_hgrn_o_norm, m_l1_w_out, m_l1_ffn_norm, m_l1_w_ff1, m_l1_w_ff2, m_hgrn_lb_logits, v_l0_mix_norm, v_l0_w_in, v_l0_fox_q_norm, v_l0_fox_k_norm, v_l0_fox_f_bias, v_l0_gdn_conv, v_l0_gdn_A_log, v_l0_gdn_dt_bias, v_l0_gdn_o_norm, v_l0_w_out, v_l0_ffn_norm, v_l0_w_ff1, v_l0_w_ff2, v_l1_mix_norm, v_l1_w_in, v_l1_hgrn_o_norm, v_l1_w_out, v_l1_ffn_norm, v_l1_w_ff1, v_l1_w_ff2, v_hgrn_lb_logits):
    args = dict(zip(_NAMES, (l0_mix_norm, l0_w_in, l0_fox_q_norm, l0_fox_k_norm, l0_fox_f_bias, l0_gdn_conv, l0_gdn_A_log, l0_gdn_dt_bias, l0_gdn_o_norm, l0_w_out, l0_ffn_norm, l0_w_ff1, l0_w_ff2, l1_mix_norm, l1_w_in, l1_hgrn_o_norm, l1_w_out, l1_ffn_norm, l1_w_ff1, l1_w_ff2, hgrn_lb_logits)))
    mom = dict(zip(_NAMES, (m_l0_mix_norm, m_l0_w_in, m_l0_fox_q_norm, m_l0_fox_k_norm, m_l0_fox_f_bias, m_l0_gdn_conv, m_l0_gdn_A_log, m_l0_gdn_dt_bias, m_l0_gdn_o_norm, m_l0_w_out, m_l0_ffn_norm, m_l0_w_ff1, m_l0_w_ff2, m_l1_mix_norm, m_l1_w_in, m_l1_hgrn_o_norm, m_l1_w_out, m_l1_ffn_norm, m_l1_w_ff1, m_l1_w_ff2, m_hgrn_lb_logits)))
    var = dict(zip(_NAMES, (v_l0_mix_norm, v_l0_w_in, v_l0_fox_q_norm, v_l0_fox_k_norm, v_l0_fox_f_bias, v_l0_gdn_conv, v_l0_gdn_A_log, v_l0_gdn_dt_bias, v_l0_gdn_o_norm, v_l0_w_out, v_l0_ffn_norm, v_l0_w_ff1, v_l0_w_ff2, v_l1_mix_norm, v_l1_w_in, v_l1_hgrn_o_norm, v_l1_w_out, v_l1_ffn_norm, v_l1_w_ff1, v_l1_w_ff2, v_hgrn_lb_logits)))
    nb, t, _ = x.shape
    dev = 4 * lax.axis_index("x") + 2 * lax.axis_index("y") + lax.axis_index("c")
    conv_cols = l0_gdn_conv.shape[1]
    loss_row, dx, g, big = _train(x.reshape(nb * t, D), loss_target.reshape(nb * t, D), args, mom, var, t)

    shapes = {nm: args[nm].shape for nm in _SMALL_NAMES}
    gsm = dict(g)
    gsm["l0_gdn_conv"] = jnp.zeros(shapes["l0_gdn_conv"], f32)
    packed = _pack_small(gsm)
    _, used = _unpack_small(packed, shapes)
    flat_extra = jnp.concatenate([jnp.sum(loss_row).reshape(1), g["l0_gdn_conv"].reshape(-1)])
    packed = packed.reshape(-1).at[used:used + flat_extra.shape[0]].set(flat_extra).reshape(_SMALL_ROWS, D)
    (parts,) = _allgather("ag_small", [packed])
    total = _sum_parts(parts).reshape(-1)
    loss = 0.5 * total[used] / D
    conv_g_full = total[used + 1:used + 1 + 4 * NDEV * conv_cols].reshape(4, NDEV * conv_cols)
    conv_g = lax.dynamic_slice(conv_g_full, (0, dev * conv_cols), (4, conv_cols))
    own_vals = {nm: jnp.zeros(shapes[nm], f32) for nm in _SMALL_NAMES}
    own_vals["l0_gdn_conv"] = conv_g
    own_mask = {nm: jnp.zeros(shapes[nm], f32) for nm in _SMALL_NAMES}
    own_mask["l0_gdn_conv"] = jnp.ones(shapes["l0_gdn_conv"], f32)
    small = _adamw_small(parts, _pack_small(args), _pack_small(mom), _pack_small(var), _pack_small(own_mask), _pack_small(own_vals))
    small = [_unpack_small(a, shapes)[0] for a in small]
    small[0]["l0_gdn_conv"] = conv_g

    outs = [loss, dx.reshape(nb, t, D)]
    for k in range(4):
        outs += [big[nm][k] if nm in _BIG_NAMES else small[k][nm] for nm in _NAMES]
    return tuple(outs)
```

```python
import functools

import jax
import jax.numpy as jnp
from jax import lax
from jax.experimental import pallas as pl
from jax.experimental.pallas import tpu as pltpu

f32, bf16 = jnp.float32, jnp.bfloat16
NN = (((1,), (0,)), ((), ()))
NT = (((1,), (1,)), ((), ()))
TN = (((0,), (0,)), ((), ()))
HI = lax.Precision.HIGHEST
MESH = pl.DeviceIdType.MESH
S = jax.ShapeDtypeStruct

EPS = 1e-6
D = 1024
LANES = 128
FOX_H, FOX_D, FOX_W = 8, 64, 512
GDN_H, HD, GDN_W = 4, 128, 512
HG_H = 8
CH = 64
ZW0 = 3840
NDEV = 8
ADAM_LR, ADAM_B1, ADAM_B2, ADAM_EPS, ADAM_WD, ADAM_STEP = 0.001, 0.9, 0.999, 1e-08, 0.01, 10

Z0_FQK, Z0_FV, Z0_GQKV, Z0_GG, Z0_SMALL = 0, 8, 12, 24, 28


def _dot(a, b, dims=NN, prec=None):
    return lax.dot_general(a, b, dims, precision=prec, preferred_element_type=f32)


def _iota2(shape, axis):
    return lax.broadcasted_iota(jnp.int32, shape, axis)


def _split3(x):
    x1 = x.astype(bf16)
    r = x - x1.astype(f32)
    x2 = r.astype(bf16)
    return x1, x2, (r - x2.astype(f32)).astype(bf16)


def _dot_sel(a, b, dims=NN, exact_lhs=False):
    if exact_lhs:
        return sum(_dot(a.astype(bf16), piece, dims) for piece in _split3(b))
    return sum(_dot(piece, b.astype(bf16), dims) for piece in _split3(a))


@jax.custom_vjp
def _sel_rhs(a, b):
    return _dot_sel(a, b)


_sel_rhs.defvjp(lambda a, b: (_dot_sel(a, b), b), lambda b, g: (_dot_sel(g, b, NT), jnp.zeros_like(b)))


@jax.custom_vjp
def _sel_lhs(a, x):
    return _dot_sel(a, x, exact_lhs=True)


_sel_lhs.defvjp(lambda a, x: (_dot_sel(a, x, exact_lhs=True), a), lambda a, g: (jnp.zeros_like(a), _dot_sel(a, g, TN, exact_lhs=True)))


class _Side:
    def __init__(self, ins, out_shapes, scratch, start, finish):
        self.ins, self.out_shapes, self.scratch, self.start, self.finish = list(ins), list(out_shapes), list(scratch), start, finish


def _join_sides(sides):
    def split(refs, counts):
        out, off = [], 0
        for c in counts:
            out.append(refs[off:off + c])
            off += c
        return out

    ni, no, ns = ([len(getattr(sd, a)) for sd in sides] for a in ("ins", "out_shapes", "scratch"))

    def run(which):
        def go(ins, outs, sems):
            for sd, i, o, c in zip(sides, split(ins, ni), split(outs, no), split(sems, ns)):
                getattr(sd, which)(i, o, c)
        return go

    return _Side(sum((sd.ins for sd in sides), []), sum((sd.out_shapes for sd in sides), []),
                 sum((sd.scratch for sd in sides), []), run("start"), run("finish"))


def _pcall(side, body, *, name, grid, in_specs, out_specs, out_shape, scratch_shapes=(), compiler_params=None):
    if side is None:
        return pl.pallas_call(body, name=name, grid=grid, in_specs=in_specs, out_specs=out_specs, out_shape=out_shape,
                              scratch_shapes=scratch_shapes, compiler_params=compiler_params)
    single = not isinstance(out_shape, (list, tuple))
    ospecs, oshape = ([out_specs], [out_shape]) if single else (list(out_specs), list(out_shape))
    nin, nout, nscr = len(in_specs), len(ospecs), len(scratch_shapes)
    si, so = len(side.ins), len(side.out_shapes)

    def wrapped(*refs):
        o0 = nin + si
        c0 = o0 + nout + so
        sins, souts, ssems = refs[nin:o0], refs[o0 + nout:c0], refs[c0 + nscr:]
        ids = [pl.program_id(a) for a in range(len(grid))]
        first = functools.reduce(jnp.logical_and, [i == 0 for i in ids])
        last = functools.reduce(jnp.logical_and, [i == g - 1 for i, g in zip(ids, grid)])

        @pl.when(first)
        def _():
            side.start(sins, souts, ssems)

        body(*refs[:nin], *refs[o0:o0 + nout], *refs[c0:c0 + nscr])

        @pl.when(last)
        def _():
            side.finish(sins, souts, ssems)

    call = pl.pallas_call(
        wrapped, name=name, grid=grid, in_specs=list(in_specs) + _hbm_specs(si), out_specs=ospecs + _hbm_specs(so),
        out_shape=oshape + side.out_shapes, scratch_shapes=list(scratch_shapes) + side.scratch,
        compiler_params=pltpu.CompilerParams(dimension_semantics=("arbitrary",) * len(grid),
                                             vmem_limit_bytes=getattr(compiler_params, "vmem_limit_bytes", None)))

    def run(*args):
        res = call(*args, *side.ins)
        return (res[0] if single else list(res[:nout])), list(res[nout:])

    return run


def _tok_specs(rows, consts, tm):
    specs = []
    for (_, w, base) in rows:
        specs.append(pl.BlockSpec((tm, w), functools.partial(lambda j, i, b: (i, b + j), b=base)))
    for (arr, w, base) in consts:
        if w is None:
            specs.append(pl.BlockSpec(arr.shape, lambda j, i: (0, 0)))
        else:
            specs.append(pl.BlockSpec((arr.shape[0], w), functools.partial(lambda j, i, b: (0, b + j), b=base)))
    return specs


def _tok_fwd(name, f, rows, consts, outs, tm, ncb=1, with_j=False, also_t=(), side=None):
    n = rows[0][0].shape[0]
    nin = len(rows) + len(consts)
    nout = len(outs)

    def body(*refs):
        ins = [r[...] for r in refs[:nin]]
        vals = f(pl.program_id(0), *ins) if with_j else f(*ins)
        for r, v in zip(refs[nin:nin + nout], vals):
            r[...] = v.astype(r.dtype)
        for r, k in zip(refs[nin + nout:], also_t):
            r[...] = vals[k].T.astype(r.dtype)

    return _pcall(
        side, body, name=name, grid=(ncb, n // tm),
        in_specs=_tok_specs(rows, consts, tm),
        out_specs=[pl.BlockSpec((tm, w), lambda j, i: (i, j)) for (w, _) in outs]
        + [pl.BlockSpec((outs[k][0], tm), lambda j, i: (0, i)) for k in also_t],
        out_shape=[S((n, w * ncb), dt) for (w, dt) in outs] + [S((outs[k][0], n), bf16) for k in also_t],
        compiler_params=pltpu.CompilerParams(dimension_semantics=("parallel", "parallel")),
    )(*[r[0] for r in rows], *[c[0] for c in consts])


def _tok_bwd(name, f, rows, consts, cots, tm, drow, dconst, ncb=1, with_j=False, addto=None, also_t=(), drow_dtype=f32, side=None, ncat=0):
    n = rows[0][0].shape[0]
    nr, nc, nct = len(rows), len(consts), len(cots)
    addto = addto or {}
    add_keys = sorted(addto)
    nadd = len(add_keys)

    def body(*refs):
        ins = [r[...] for r in refs[:nr + nc]]
        cot = [r[...] for r in refs[nr + nc:nr + nc + nct]]
        adds = refs[nr + nc + nct:nr + nc + nct + nadd]
        outs = refs[nr + nc + nct + nadd:]
        pos = list(drow) + [nr + k for k in dconst]

        def g(*dargs):
            full = list(ins)
            for p, a in zip(pos, dargs):
                full[p] = a
            return tuple(f(pl.program_id(0), *full) if with_j else f(*full))

        vals, vjp = jax.vjp(g, *[ins[p] for p in pos])
        grads = vjp(tuple(c.astype(v.dtype) for c, v in zip(cot, vals)))
        off = 0
        for k in range(len(drow)):
            gk = grads[k]
            if k in addto:
                gk = gk + adds[add_keys.index(k)][...]
            if k < ncat:
                outs[0][:, off:off + gk.shape[1]] = gk.astype(outs[0].dtype)
                off += gk.shape[1]
            else:
                outs[k - skip][...] = gk.astype(outs[k - skip].dtype)
            if k in also_t:
                tref = outs[len(drow) - skip + len(dconst) + list(also_t).index(k)]
                tref[...] = gk.T.astype(tref.dtype)
        first = pl.program_id(1) == 0
        for k in range(len(dconst)):
            ref = outs[len(drow) - skip + k]

            @pl.when(first)
            def _():
                ref[...] = jnp.zeros_like(ref)

            ref[...] += grads[len(drow) + k]

    skip = max(ncat - 1, 0)
    in_specs = _tok_specs(rows, consts, tm)
    in_specs += [pl.BlockSpec((tm, w), lambda j, i: (i, j)) for (_, w) in cots]
    in_specs += [pl.BlockSpec((tm, rows[drow[k]][1]), lambda j, i: (i, j)) for k in add_keys]
    dts = drow_dtype if isinstance(drow_dtype, (list, tuple)) else [drow_dtype] * len(drow)
    widths = [rows[k][1] for k in drow]
    if ncat:
        widths, dts = [sum(widths[:ncat])] + widths[ncat:], [dts[0]] + list(dts[ncat:])
    out_specs = [pl.BlockSpec((tm, wd), lambda j, i: (i, j)) for wd in widths]
    out_shape = [S((n, wd * ncb), dt) for wd, dt in zip(widths, dts)]
    for k in dconst:
        arr, w, _ = consts[k]
        if w is None:
            out_specs.append(pl.BlockSpec(arr.shape, lambda j, i: (0, 0)))
            out_shape.append(S(arr.shape, f32))
        else:
            out_specs.append(pl.BlockSpec((arr.shape[0], w), lambda j, i: (0, j)))
            out_shape.append(S((arr.shape[0], w * ncb), f32))
    for k in also_t:
        out_specs.append(pl.BlockSpec((rows[drow[k]][1], tm), lambda j, i: (0, i)))
        out_shape.append(S((rows[drow[k]][1], n), bf16))
    return _pcall(
        side, body, name=name, grid=(ncb, n // tm), in_specs=in_specs, out_specs=out_specs, out_shape=out_shape,
        compiler_params=pltpu.CompilerParams(dimension_semantics=("parallel", "arbitrary")),
    )(*[r[0] for r in rows], *[c[0] for c in consts], *[c[0] for c in cots], *[addto[k] for k in add_keys])


def _scan_fwd(name, f, rows, nh, nchunk, side=None, consts=(), out_dtype=f32, out_k=1):
    n = rows[0][0].shape[0]
    nb = n // (CH * nchunk)
    nin, nco = len(rows), len(consts)
    w = nh * HD

    def body(*refs):
        o_ref, hist_ref, st = refs[nin + nco], refs[nin + nco + 1], refs[nin + nco + 2]

        @pl.when(pl.program_id(0) == 0)
        def _():
            st[...] = jnp.zeros_like(st)

        s0 = st[...]
        hist_ref[0] = s0.astype(hist_ref.dtype)
        tiles = [r[...].reshape(nb * CH, r.shape[2]) for r in refs[:nin]]
        o, s1 = f(*tiles, *[r[...] for r in refs[nin:nin + nco]], s0)
        o_ref[...] = o.reshape(o_ref.shape).astype(o_ref.dtype)
        st[...] = s1

    seq3 = lambda a: a.reshape(nb, nchunk * CH, a.shape[1])
    res = _pcall(
        side, body, name=name, grid=(nchunk,),
        in_specs=[pl.BlockSpec((nb, CH, k * w), functools.partial(lambda c, base: (0, c, base), base=b)) for (_, b, k) in rows]
        + [pl.BlockSpec(c.shape, lambda c: (0, 0)) for c in consts],
        out_specs=[pl.BlockSpec((nb, CH, out_k * w), lambda c: (0, c, 0)), pl.BlockSpec((1, nb * w, HD), lambda c: (c, 0, 0))],
        out_shape=[S((nb, nchunk * CH, out_k * w), out_dtype), S((nchunk, nb * w, HD), bf16)],
        scratch_shapes=[pltpu.VMEM((nb * w, HD), f32)],
        compiler_params=pltpu.CompilerParams(dimension_semantics=("arbitrary",)),
    )(*[seq3(r[0]) for r in rows], *consts)
    (o, hist), extra = (res, None) if side is None else res
    out = [o.reshape(n, out_k * w), hist]
    return out if side is None else (out, extra)


def _scan_bwd(name, f, rows, hist, do, nh, nchunk, side=None, dtypes=None, consts=(), out_k=1):
    n = rows[0][0].shape[0]
    nb = n // (CH * nchunk)
    nin, nco = len(rows), len(consts)
    w = nh * HD

    def body(*refs):
        hist_ref, do_ref = refs[nin + nco], refs[nin + nco + 1]
        outs = refs[nin + nco + 2:nin + nco + 2 + nin]
        couts = refs[nin + nco + 2 + nin:nin + nco + 2 + nin + nco]
        ds = refs[nin + nco + 2 + nin + nco]

        @pl.when(pl.program_id(0) == 0)
        def _():
            ds[...] = jnp.zeros_like(ds)
            for c in couts:
                c[...] = jnp.zeros_like(c)

        tiles = [r[...].reshape(nb * CH, r.shape[2]) for r in refs[:nin]]
        _, vjp = jax.vjp(f, *tiles, *[r[...] for r in refs[nin:nin + nco]], hist_ref[0].astype(f32))
        grads = vjp((do_ref[...].reshape(nb * CH, out_k * w), ds[...]))
        for r, gk in zip(outs, grads[:nin]):
            r[...] = gk.reshape(r.shape).astype(r.dtype)
        for c, gk in zip(couts, grads[nin:nin + nco]):
            c[...] += gk
        ds[...] = grads[nin + nco]

    seq3 = lambda a: a.reshape(nb, nchunk * CH, a.shape[1])
    rev = lambda c, base: (0, nchunk - 1 - c, base)
    res = _pcall(
        side, body, name=name, grid=(nchunk,),
        in_specs=[pl.BlockSpec((nb, CH, k * w), functools.partial(rev, base=b)) for (_, b, k) in rows]
        + [pl.BlockSpec(c.shape, lambda c: (0, 0)) for c in consts]
        + [pl.BlockSpec((1, nb * w, HD), lambda c: (nchunk - 1 - c, 0, 0)), pl.BlockSpec((nb, CH, out_k * w), functools.partial(rev, base=0))],
        out_specs=[pl.BlockSpec((nb, CH, k * w), functools.partial(rev, base=0)) for (_, _, k) in rows]
        + [pl.BlockSpec(c.shape, lambda c: (0, 0)) for c in consts],
        out_shape=[S((nb, nchunk * CH, k * w), dt) for (_, _, k), dt in zip(rows, dtypes or [f32] * nin)]
        + [S(c.shape, f32) for c in consts],
        scratch_shapes=[pltpu.VMEM((nb * w, HD), f32)],
        compiler_params=pltpu.CompilerParams(dimension_semantics=("arbitrary",)),
    )(*[seq3(r[0]) for r in rows], *consts, hist, seq3(do))
    outs, extra = (res, None) if side is None else res
    outs = [o.reshape(n, o.shape[2]) for o in outs[:nin]] + list(outs[nin:])
    return outs if side is None else (outs, extra)


_VMEM_LIMIT = 56 * 2 ** 20
_VMEM_TILE_BUDGET = 40 * 2 ** 20


def _mm_tiles(m, n, k, sa, sb, so, sx, a_f32, b_f32, tn_fixed):
    best = None
    for tm in (1024, 512, 256, 128, 64):
        for tn in ((tn_fixed,) if tn_fixed else (1024, 768, 512, 384, 256, 128)):
            if m % tm or n % tn:
                continue
            need = 2 * (tm * k * sa + k * tn * sb + tm * tn * (so + sx)) + tm * tn * 4
            need += tm * k * (2 if sa == 4 else 0) + k * tn * (2 if sb == 4 else 0)
            need += tm * k * (4 if a_f32 else 0) + k * tn * (4 if b_f32 else 0)
            if need <= _VMEM_TILE_BUDGET and (best is None or (tm * tn, tm) > best[0]):
                best = ((tm * tn, tm), tm, tn)
    return best[1], best[2]


def _mm(name, a, b, dims, out_dtype, a_fn=None, b_fn=None, epi=None, extra=(), consts=(), outs=None, out_t=False, slab=False,
        side=None):
    m, kk = a.shape
    gathered = b.ndim == 3
    if gathered:
        nn = NDEV * _COLW if dims is NN else b.shape[1]
    else:
        nn = b.shape[1] if dims is NN else b.shape[0]
    kinds = [("tile", out_dtype)] if outs is None else list(outs)
    so = sum(jnp.dtype(dt).itemsize for kd, dt in kinds if kd != "rows")
    sx = sum(e.dtype.itemsize for e in extra)
    full_rows = bool(consts) or any(kd == "rows" for kd, _ in kinds)
    tm, tn = _mm_tiles(m, nn, kk, a.dtype.itemsize, b.dtype.itemsize, so, sx,
                       a_fn is not None, b_fn is not None,
                       _COLW if slab else (nn if full_rows else None))
    gblocks = tn // _COLW if (gathered and dims is NN) else 0
    assert not gblocks or (outs is None and not extra and not out_t and not slab and b_fn is None)
    nex, nco = len(extra), len(consts)

    def body(a_ref, b_ref, *rest):
        av = a_ref[...]
        if a_fn is not None:
            av = a_fn(av.astype(f32))
        av = av.astype(bf16)
        if gblocks:
            for gb in range(gblocks):
                part = _dot(av, b_ref[gb], NN)
                rest[nex + nco][:, _COLW * gb:_COLW * (gb + 1)] = (part if epi is None else epi(part)).astype(rest[nex + nco].dtype)
            return
        if gathered:
            acc = sum(_dot(av[:, _COLW * d:_COLW * (d + 1)], b_ref[d], NT) for d in range(NDEV))
        else:
            bv = b_ref[...]
            if b_fn is not None:
                bv = b_fn(bv.astype(f32))
            acc = _dot(av, bv.astype(bf16), dims)
        if epi is not None:
            acc = epi(acc, *[r[...] for r in rest[:nex + nco]])
        vals = acc if isinstance(acc, tuple) else (acc,)
        for (kd, _), o_ref, val in zip(kinds, rest[nex + nco:], vals):
            if kd == "rows":
                @pl.when(pl.program_id(0) == 0)
                def _():
                    o_ref[...] = jnp.zeros_like(o_ref)

                o_ref[...] += val
            elif kd == "tile_t" or out_t:
                o_ref[...] = val.T.astype(o_ref.dtype)
            elif slab:
                o_ref[0] = val.astype(o_ref.dtype)
            else:
                o_ref[...] = val.astype(o_ref.dtype)

    if gathered:
        bspec = (pl.BlockSpec((gblocks, kk, _COLW), lambda i, j: (j, 0, 0)) if dims is NN
                 else pl.BlockSpec((NDEV, tn, _COLW), lambda i, j: (0, j, 0)))
    else:
        bspec = pl.BlockSpec((kk, tn), lambda i, j: (0, j)) if dims is NN else pl.BlockSpec((tn, kk), lambda i, j: (j, 0))
    out_specs, out_shape = [], []
    for kd, dt in kinds:
        if kd == "rows":
            out_specs.append(pl.BlockSpec((1, nn), lambda i, j: (0, 0)))
            out_shape.append(S((1, nn), dt))
        elif kd == "tile_t" or out_t:
            out_specs.append(pl.BlockSpec((tn, tm), lambda i, j: (j, i)))
            out_shape.append(S((nn, m), dt))
        elif slab:
            out_specs.append(pl.BlockSpec((1, tm, tn), lambda i, j: (j, i, 0)))
            out_shape.append(S((nn // tn, m, tn), dt))
        else:
            out_specs.append(pl.BlockSpec((tm, tn), lambda i, j: (i, j)))
            out_shape.append(S((m, nn), dt))
    if outs is None:
        out_specs, out_shape = out_specs[0], out_shape[0]
    sem = ("arbitrary", "arbitrary") if any(kd == "rows" for kd, _ in kinds) else ("parallel", "parallel")
    return _pcall(
        side, body, name=name, grid=(m // tm, nn // tn),
        in_specs=[pl.BlockSpec((tm, kk), lambda i, j: (i, 0)), bspec]
        + [pl.BlockSpec((tm, tn), lambda i, j: (i, j)) for _ in extra]
        + [pl.BlockSpec(c.shape, lambda i, j: (0, 0)) for c in consts],
        out_specs=out_specs, out_shape=out_shape,
        compiler_params=pltpu.CompilerParams(dimension_semantics=sem, vmem_limit_bytes=_VMEM_LIMIT),
    )(a, b, *extra, *consts)


def _f_norm(x, g):
    return (x * lax.rsqrt(jnp.mean(x * x, axis=-1, keepdims=True) + EPS) * g,)


def _f_foxpre(zqk, gq, gk, pm):
    def nrm(t, g):
        return t * lax.rsqrt(_sel_rhs(t * t, pm) + EPS) * g
    return nrm(zqk[:, :FOX_W], gq), nrm(zqk[:, FOX_W:], gk)


def _chunk_cumsum(x):
    n = x.shape[0]
    r, c = _iota2((n, n), 0), _iota2((n, n), 1)
    tri = jnp.logical_and(r >= c, (r // CH) == (c // CH)).astype(f32)
    return _sel_lhs(tri, x)


def _f_gdngate(zs, eb, ea, alog_b, dt_b):
    beta = jax.nn.sigmoid(_sel_rhs(zs, eb))
    la = -jnp.exp(alog_b) * jax.nn.softplus(_sel_rhs(zs, ea) + dt_b)
    return beta, _chunk_cumsum(la)


def _f_conv(j, x, w):
    t = x.shape[0]
    y = x * w[3:4, :]
    for jj in range(3):
        sh = 3 - jj
        xs = jnp.concatenate([jnp.zeros((sh, x.shape[1]), f32), x[:t - sh, :]], axis=0)
        y = y + xs * w[jj:jj + 1, :]
    y = jax.nn.silu(y)
    yn = y * lax.rsqrt(jnp.sum(y * y, axis=-1, keepdims=True) + EPS)
    return (jnp.where(j < 2 * GDN_H, yn, y),)


def _head_rms(o, nh):
    outs = []
    for h in range(nh):
        oh = o[:, HD * h:HD * (h + 1)]
        outs.append(oh * lax.rsqrt(jnp.mean(oh * oh, axis=-1, keepdims=True) + EPS))
    return jnp.concatenate(outs, axis=1)


def _f_post0(fox_o, o, gg, on):
    return (jnp.concatenate([fox_o, _head_rms(o, GDN_H) * on * jax.nn.silu(gg)], axis=1),)


def _f_post1(o, zg, on):
    return (_head_rms(o, HG_H) * on * jax.nn.silu(zg),)


def _f_hpre(zqf, lbl):
    lb = jax.nn.sigmoid(lbl[1:2, :] - lbl[0:1, :])
    fg = lb + (1.0 - lb) * jax.nn.sigmoid(zqf[:, D:])
    return jax.nn.silu(zqf[:, :D]), 1.0 - fg, _chunk_cumsum(jnp.log(fg))


def _dotb(a, b, dims=NN):
    return _dot(a.astype(bf16), b.astype(bf16), dims)


def _dot3(a, b):
    ah, bh = a.astype(bf16), b.astype(bf16)
    al, bl = (a - ah.astype(f32)).astype(bf16), (b - bh.astype(f32)).astype(bf16)
    return _dot(ah, bh) + (_dot(ah, bl) + _dot(al, bh))


def _split(t, nh):
    return [t[CH * ck:CH * (ck + 1), HD * h:HD * (h + 1)] for ck in range(t.shape[0] // CH) for h in range(nh)]


def _merge(units, nh):
    return jnp.concatenate([jnp.concatenate(units[i:i + nh], axis=1) for i in range(0, len(units), nh)], axis=0)


def _inv_impl(amats):
    n = amats[0].shape[0]
    eye = jnp.where(_iota2((n, n), 0) == _iota2((n, n), 1), 1.0, 0.0).astype(f32)
    xs, ps = [eye - a for a in amats], list(amats)
    for _ in range(max(1, (n - 1).bit_length()) - 1):
        ps = [_dotb(p, p) for p in ps]
        xs = [x + _dotb(x, p) for x, p in zip(xs, ps)]
    for _ in range(3):
        rs = [eye - x - _dot3(a, x) for a, x in zip(amats, xs)]
        xs = [x + _dotb(x, r) for x, r in zip(xs, rs)]
    return tuple(xs)


@jax.custom_vjp
def _inv_unit_lower(amats):
    return _inv_impl(amats)


def _inv_fwd(amats):
    xs = _inv_impl(amats)
    return xs, xs


def _inv_bwd(xs, dxs):
    return (tuple(-_dotb(_dotb(x, dx, TN), x, NT) for x, dx in zip(xs, dxs)),)


_inv_unit_lower.defvjp(_inv_fwd, _inv_bwd)


@jax.custom_vjp
def _inv_given(amats, xs):
    return xs


def _inv_given_fwd(amats, xs):
    return xs, xs


def _inv_given_bwd(xs, dxs):
    return _inv_bwd(xs, dxs)[0], tuple(jnp.zeros_like(x) for x in xs)


_inv_given.defvjp(_inv_given_fwd, _inv_given_bwd)


def _f_gdn_intra(q, k, v, bb, gb, tinv_p=None):
    qs, ks, vs, bs, gs = (_split(t, GDN_H) for t in (q, k, v, bb, gb))
    r, cc = _iota2((CH, CH), 0), _iota2((CH, CH), 1)
    causal, strict = r >= cc, r > cc
    beta, g, gl = [b[:, :1] for b in bs], [x[:, :1] for x in gs], [x[CH - 1:CH, :1] for x in gs]
    decay = [jnp.exp(jnp.where(causal, x[:, :CH] - x[:, :CH].T, -jnp.inf)) for x in gs]
    kb = [ki * bi for ki, bi in zip(ks, beta)]
    amat = [jnp.where(strict, _dotb(kbi, ki, NT) * di, 0.0) for kbi, ki, di in zip(kb, ks, decay)]
    if tinv_p is None:
        tinv = _inv_unit_lower(tuple(amat))
    else:
        tinv = _inv_given(tuple(amat), tuple(x[:, :CH] for x in _split(tinv_p, GDN_H)))
    rhs = [jnp.concatenate([vi * bi, kbi * jnp.exp(gi)], axis=1) for vi, bi, kbi, gi in zip(vs, beta, kb, g)]
    uw = [_dotb(ti, ri) for ti, ri in zip(tinv, rhs)]
    qsc = [qi * (HD ** -0.5) for qi in qs]
    qk = [jnp.where(causal, _dotb(qi, ki, NT) * di, 0.0) for qi, ki, di in zip(qsc, ks, decay)]
    outs = ([x[:, :HD] for x in uw], [x[:, HD:] for x in uw],
            [jnp.concatenate([x, jnp.zeros_like(x)], axis=1) for x in qk],
            [qi * jnp.exp(gi) for qi, gi in zip(qsc, g)],
            [ki * jnp.exp(gli - gi) for ki, gli, gi in zip(ks, gl, g)],
            [jnp.broadcast_to(gli, (CH, HD)) for gli in gl])
    if tinv_p is None:
        outs += ([jnp.concatenate([x, jnp.zeros_like(x)], axis=1) for x in tinv],)
    return tuple(_merge(o, GDN_H) for o in outs)


def _f_gdn_gated(q, k, v, zs, eb, ea, alog_b, dt_b):
    return _f_gdn_intra(q, k, v, *_f_gdngate(zs, eb, ea, alog_b, dt_b))


def _f_gdn_gated_given(q, k, v, zs, tinv_p, eb, ea, alog_b, dt_b):
    return _f_gdn_intra(q, k, v, *_f_gdngate(zs, eb, ea, alog_b, dt_b), tinv_p=tinv_p)


def _f_gdn_inter(u, w, qkp, qd, kd, glb, st):
    us, ws, qks, qds, kds, gls = (_split(t, GDN_H) for t in (u, w, qkp, qd, kd, glb))
    sts = [st[HD * i:HD * (i + 1), :] for i in range(len(us))]
    vn = [ui - _dotb(wi, si) for ui, wi, si in zip(us, ws, sts)]
    o = [_dotb(qi, si) + _dotb(xi[:, :CH], vi) for qi, si, xi, vi in zip(qds, sts, qks, vn)]
    s2 = [si * jnp.exp(gi[:1, :1]) + _dotb(ki, vi, TN) for si, gi, ki, vi in zip(sts, gls, kds, vn)]
    return _merge(o, GDN_H), jnp.concatenate(s2, axis=0)


def _f_gdn_full(u, w, qkp, qd, kd, glb, fox_o, gg, on, st):
    o, s2 = _f_gdn_inter(u, w, qkp, qd, kd, glb, st)
    return _f_post0(fox_o, o, gg, on)[0], s2


def _f_hgrn_chunk(q, k, b, v, st):
    qs, ks, bs, vs = (_split(t, HG_H) for t in (q, k, b, v))
    sts = [st[HD * i:HD * (i + 1), :] for i in range(len(qs))]
    causal = _iota2((CH, CH), 0) >= _iota2((CH, CH), 1)
    bl, bm = [x[CH - 1:CH, :] for x in bs], [x[CH // 2 - 1:CH // 2, :] for x in bs]
    a = [jnp.where(causal, _dotb(qi * jnp.exp(bi - mi), ki * jnp.exp(mi - bi), NT), 0.0)
         for qi, ki, bi, mi in zip(qs, ks, bs, bm)]
    o = [_dotb(qi * jnp.exp(bi), si, NT) + _dotb(ai, vi) for qi, bi, si, ai, vi in zip(qs, bs, sts, a, vs)]
    s2 = [si * jnp.exp(li) + _dotb(vi, ki * jnp.exp(li - bi), TN) for si, li, vi, ki, bi in zip(sts, bl, vs, ks, bs)]
    return _merge(o, HG_H), jnp.concatenate(s2, axis=0)


def _f_hgrn_full(z, lbl, on, st):
    o, s2 = _f_hgrn_chunk(*_f_hpre(z[:, :2 * D], lbl), z[:, 2 * D:3 * D], st)
    return _f_post1(o, z[:, 3 * D:], on)[0], s2


def _fox_gate_fwd(z0, fbias, t, tc=256):
    n = z0.shape[0]
    nt = t // tc

    def body(zs_ref, b_ref, ccol_ref, crow_ref, carry):
        @pl.when(pl.program_id(1) == 0)
        def _():
            carry[...] = jnp.zeros_like(carry)

        ls = jnp.where(_iota2((tc, LANES), 1) < FOX_H, jax.nn.log_sigmoid(zs_ref[...] + b_ref[...]), 0.0)
        tri = (_iota2((tc, tc), 0) >= _iota2((tc, tc), 1)).astype(f32)
        c = _dot_sel(tri, ls, exact_lhs=True) + carry[...]
        carry[...] = c[tc - 1:tc, :]
        ccol_ref[...] = c
        crow_ref[0] = c.T[:FOX_H, :]

    return pl.pallas_call(
        body, name="fox_gate_fwd", grid=(n // t, nt),
        in_specs=[pl.BlockSpec((tc, LANES), lambda b, i: (b * nt + i, Z0_SMALL)), pl.BlockSpec((1, LANES), lambda b, i: (0, 0))],
        out_specs=[pl.BlockSpec((tc, LANES), lambda b, i: (b * nt + i, 0)), pl.BlockSpec((1, FOX_H, tc), lambda b, i: (b, 0, i))],
        out_shape=[S((n, LANES), f32), S((n // t, FOX_H, t), f32)],
        scratch_shapes=[pltpu.VMEM((1, LANES), f32)],
        compiler_params=pltpu.CompilerParams(dimension_semantics=("parallel", "arbitrary")),
    )(z0, fbias)


def _fox_gate_bwd(z0, fbias, dcq, dck, t, tc=256):
    n = z0.shape[0]
    nt = t // tc

    def body(zs_ref, b_ref, dcq_ref, dck_ref, dz_ref, db_ref, carry):
        first = jnp.logical_and(pl.program_id(0) == 0, pl.program_id(1) == 0)

        @pl.when(pl.program_id(1) == 0)
        def _():
            carry[...] = jnp.zeros_like(carry)

        @pl.when(first)
        def _():
            db_ref[...] = jnp.zeros_like(db_ref)

        dc = dcq_ref[0] + dcq_ref[1] + dcq_ref[2] + dcq_ref[3]
        drow = dck_ref[0, 0] + dck_ref[1, 0] + dck_ref[2, 0] + dck_ref[3, 0]
        eye = (_iota2((FOX_H, LANES), 0) == _iota2((FOX_H, LANES), 1)).astype(f32)
        dc = dc + _dot_sel(drow, eye, TN)
        triu = (_iota2((tc, tc), 0) <= _iota2((tc, tc), 1)).astype(f32)
        dls = _dot_sel(triu, dc, exact_lhs=True) + carry[...]
        carry[...] = dls[0:1, :]
        x = zs_ref[...] + b_ref[...]
        dz = jnp.where(_iota2((tc, LANES), 1) < FOX_H, dls * jax.nn.sigmoid(-x), 0.0)
        dz_ref[...] = dz
        db_ref[...] += jnp.sum(dz, axis=0, keepdims=True)

    def rev(b, i):
        return b * nt + (nt - 1 - i)

    return pl.pallas_call(
        body, name="fox_gate_bwd", grid=(n // t, nt),
        in_specs=[pl.BlockSpec((tc, LANES), lambda b, i: (rev(b, i), Z0_SMALL)), pl.BlockSpec((1, LANES), lambda b, i: (0, 0)),
                  pl.BlockSpec((4, tc, LANES), lambda b, i: (0, rev(b, i), 0)),
                  pl.BlockSpec((4, 1, FOX_H, tc), lambda b, i: (0, b, 0, nt - 1 - i))],
        out_specs=[pl.BlockSpec((tc, LANES), lambda b, i: (rev(b, i), 0)), pl.BlockSpec((1, LANES), lambda b, i: (0, 0))],
        out_shape=[S((n, LANES), f32), S((1, LANES), f32)],
        scratch_shapes=[pltpu.VMEM((1, LANES), f32)],
        compiler_params=pltpu.CompilerParams(dimension_semantics=("arbitrary", "arbitrary")),
    )(z0, fbias, dcq, dck)


def _fox_scores(hh, p, i, tq, q, k, ccol, crow):
    kmax = k.shape[0]
    lane = _iota2((1, LANES), 1)
    mh = (lane // FOX_D) == hh
    h = 2 * p + hh
    qh = jnp.where(mh, q, 0.0).astype(bf16)
    s = _dot(qh, k, NT) * (FOX_D ** -0.5)
    cq = jnp.sum(jnp.where(lane == h, ccol, 0.0), axis=1, keepdims=True)
    ck = jnp.sum(jnp.where(_iota2((FOX_H, 1), 0) == h, crow, 0.0), axis=0, keepdims=True)
    causal = _iota2((1, kmax), 1) <= (i * tq + _iota2((tq, 1), 0))
    s = jnp.where(causal, s + cq - ck, -jnp.inf)
    pe = jnp.exp(s - jnp.max(s, axis=1, keepdims=True))
    return mh, qh, pe, jnp.sum(pe, axis=1, keepdims=True)


def _fox_attn_fwd(qn, kn, z0, ccol, crow, t, tq=256, side=None):
    n = qn.shape[0]
    nq = t // tq

    def body(q_ref, k_ref, v_ref, ccol_ref, crow_ref, o_ref):
        p = pl.program_id(1)
        k, v, crow = k_ref[...].astype(bf16), v_ref[...].astype(bf16), crow_ref[0]
        for i in range(nq):
            rows, kmax = pl.ds(i * tq, tq), (i + 1) * tq
            q, cc = q_ref[rows, :], ccol_ref[rows, :]
            acc = jnp.zeros((tq, LANES), f32)
            for hh in range(2):
                mh, _, pe, l = _fox_scores(hh, p, i, tq, q, k[:kmax], cc, crow[:, :kmax])
                acc = jnp.where(mh, _dot(pe.astype(bf16), v[:kmax]) / l, acc)
            o_ref[rows, :] = acc

    seq = lambda b, p: (b, p)
    return _pcall(
        side, body, name="fox_attn_fwd", grid=(n // t, FOX_H // 2),
        in_specs=[pl.BlockSpec((t, LANES), seq), pl.BlockSpec((t, LANES), seq), pl.BlockSpec((t, LANES), lambda b, p: (b, Z0_FV + p)),
                  pl.BlockSpec((t, LANES), lambda b, p: (b, 0)), pl.BlockSpec((1, FOX_H, t), lambda b, p: (b, 0, 0))],
        out_specs=pl.BlockSpec((t, LANES), seq),
        out_shape=S((n, FOX_W), f32),
        compiler_params=pltpu.CompilerParams(dimension_semantics=("parallel", "parallel")),
    )(qn, kn, z0, ccol, crow)


def _fox_attn_bwd(qn, kn, z0, ccol, crow, do, t, tq=256, side=None):
    n = qn.shape[0]
    nq = t // tq
    nb = n // t

    def body(q_ref, k_ref, v_ref, ccol_ref, crow_ref, do_ref, dq_ref, dk_ref, dv_ref, dcq_ref, dck_ref):
        p = pl.program_id(1)
        dk_ref[...] = jnp.zeros_like(dk_ref)
        dv_ref[...] = jnp.zeros_like(dv_ref)
        dck_ref[...] = jnp.zeros_like(dck_ref)
        kf, v, crow = k_ref[...], v_ref[...].astype(bf16), crow_ref[0]
        k = kf.astype(bf16)
        lane = _iota2((1, LANES), 1)
        sub = _iota2((FOX_H, 1), 0)
        scale = FOX_D ** -0.5
        for i in range(nq):
            rows, kmax = pl.ds(i * tq, tq), (i + 1) * tq
            q, cc, dout = q_ref[rows, :], ccol_ref[rows, :], do_ref[rows, :]
            dq = jnp.zeros((tq, LANES), f32)
            dcq = jnp.zeros((tq, LANES), f32)
            for hh in range(2):
                mh, qh, pe, l = _fox_scores(hh, p, i, tq, q, k[:kmax], cc, crow[:, :kmax])
                pr = pe / l
                doh = jnp.where(mh, dout, 0.0).astype(bf16)
                dp = _dot(doh, v[:kmax], NT)
                ds = pr * (dp - jnp.sum(pr * dp, axis=1, keepdims=True))
                dsb = ds.astype(bf16)
                dq = dq + _dot(dsb, jnp.where(mh, kf[:kmax], 0.0).astype(bf16)) * scale
                dk_ref[:kmax, :] += _dot(dsb, qh, TN) * scale
                dv_ref[:kmax, :] += _dot(pr.astype(bf16), doh, TN)
                h = 2 * p + hh
                dcq = dcq + jnp.where(lane == h, jnp.sum(ds, axis=1, keepdims=True), 0.0)
                dck_ref[0, 0, :, :kmax] += jnp.where(sub == h, -jnp.sum(ds, axis=0, keepdims=True), 0.0)
            dq_ref[rows, :] = dq
            dcq_ref[0, rows, :] = dcq

    seq = lambda b, p: (b, p)
    return _pcall(
        side, body, name="fox_attn_bwd", grid=(nb, FOX_H // 2),
        in_specs=[pl.BlockSpec((t, LANES), seq), pl.BlockSpec((t, LANES), seq), pl.BlockSpec((t, LANES), lambda b, p: (b, Z0_FV + p)),
                  pl.BlockSpec((t, LANES), lambda b, p: (b, 0)), pl.BlockSpec((1, FOX_H, t), lambda b, p: (b, 0, 0)),
                  pl.BlockSpec((t, LANES), seq)],
        out_specs=[pl.BlockSpec((t, LANES), seq), pl.BlockSpec((t, LANES), seq), pl.BlockSpec((t, LANES), seq),
                   pl.BlockSpec((1, t, LANES), lambda b, p: (p, b, 0)), pl.BlockSpec((1, 1, FOX_H, t), lambda b, p: (p, b, 0, 0))],
        out_shape=[S((n, FOX_W), f32), S((n, FOX_W), f32), S((n, FOX_W), f32), S((4, n, LANES), f32), S((4, nb, FOX_H, t), f32)],
        compiler_params=pltpu.CompilerParams(dimension_semantics=("parallel", "parallel")),
    )(qn, kn, z0, ccol, crow, do)


def _adamw_math(w, g, m, v):
    m = ADAM_B1 * m + (1.0 - ADAM_B1) * g
    v = ADAM_B2 * v + (1.0 - ADAM_B2) * (g * g)
    m_hat = m / (1.0 - ADAM_B1 ** ADAM_STEP)
    v_hat = v / (1.0 - ADAM_B2 ** ADAM_STEP)
    return -ADAM_LR * (m_hat / (jnp.sqrt(v_hat) + ADAM_EPS) + ADAM_WD * w), m, v


def _adamw_big(name, idx, gmine, recv1, recv2, w, m, v):
    r, wc = w.shape
    c = gmine.shape[2]
    tr = min(r, 256)

    def body(idx_ref, gm_ref, r1_ref, r2_ref, w_ref, m_ref, v_ref, g_ref, d_ref, nm_ref, nv_ref):
        g = gm_ref[0].astype(f32) + r1_ref[0].astype(f32)
        for k in range(3):
            g = g + r2_ref[k].astype(f32)
        g = g[:, :wc]
        d, nm, nv = _adamw_math(w_ref[...], g, m_ref[...], v_ref[...])
        g_ref[...] = g
        d_ref[...] = d
        nm_ref[...] = nm
        nv_ref[...] = nv

    row = pl.BlockSpec((tr, wc), lambda i, s: (i, 0))
    return pl.pallas_call(
        body, name=name,
        grid_spec=pltpu.PrefetchScalarGridSpec(
            num_scalar_prefetch=1, grid=(r // tr,),
            in_specs=[pl.BlockSpec((1, tr, c), lambda i, s: (s[0], i, 0)), pl.BlockSpec((1, tr, c), lambda i, s: (s[1], i, 0)),
                      pl.BlockSpec((3, tr, c), lambda i, s: (0, i, 0)), row, row, row],
            out_specs=[row, row, row, row]),
        out_shape=[S((r, wc), f32)] * 4,
        compiler_params=pltpu.CompilerParams(dimension_semantics=("parallel",)),
    )(idx, gmine, recv1, recv2, w, m, v)


def _pair_sum(name, idx, gmine, recv1):
    _, r, c = gmine.shape
    g4 = gmine.reshape(4, 2, r, c)

    def body(idx_ref, gm_ref, r1_ref, o_ref):
        o_ref[0] = (gm_ref[0, 0].astype(f32) + r1_ref[0].astype(f32)).astype(bf16)

    return pl.pallas_call(
        body, name=name,
        grid_spec=pltpu.PrefetchScalarGridSpec(
            num_scalar_prefetch=1, grid=(4,),
            in_specs=[pl.BlockSpec((1, 1, r, c), lambda ch, s: (ch, s[0], 0, 0)), pl.BlockSpec((1, r, c), lambda ch, s: (ch, 0, 0))],
            out_specs=pl.BlockSpec((1, r, c), lambda ch, s: (ch, 0, 0))),
        out_shape=S((4, r, c), bf16),
        compiler_params=pltpu.CompilerParams(dimension_semantics=("parallel",)),
    )(idx, g4, recv1)


def _adamw_small(parts, w, m, v, own_mask, own_g):
    def body(p_ref, w_ref, m_ref, v_ref, mask_ref, og_ref, g_ref, d_ref, nm_ref, nv_ref):
        g = p_ref[0]
        for k in range(1, NDEV):
            g = g + p_ref[k]
        g_ref[...] = g
        ge = jnp.where(mask_ref[...] > 0.5, og_ref[...], g)
        d, nm, nv = _adamw_math(w_ref[...], ge, m_ref[...], v_ref[...])
        d_ref[...] = d
        nm_ref[...] = nm
        nv_ref[...] = nv

    return pl.pallas_call(body, name="adamw_small", out_shape=[S(w.shape, f32)] * 4)(parts, w, m, v, own_mask, own_g)


def _sum_parts(parts):
    def body(p_ref, g_ref):
        g = p_ref[0]
        for k in range(1, NDEV):
            g = g + p_ref[k]
        g_ref[...] = g

    return pl.pallas_call(body, name="sum_parts", out_shape=S(parts.shape[1:], f32))(parts)


def _me():
    return lax.axis_index("x"), lax.axis_index("y"), lax.axis_index("c")


def _hbm_specs(n):
    return [pl.BlockSpec(memory_space=pl.ANY)] * n


def _allgather(name, xs):
    na = len(xs)

    def body(*refs):
        x_refs, out_refs = refs[:na], refs[na:2 * na]
        send_sems, recv_sems, local_sems = refs[2 * na:]
        mx, my, mc = _me()
        me, sib = (mx, my, mc), (mx, my, 1 - mc)
        chips = [(1 - mx, my), (mx, 1 - my), (1 - mx, 1 - my)]

        def slab(a, px, py, pc):
            return out_refs[a].at[4 * px + 2 * py + pc]

        def copy(a, k, block, to, own=False):
            return pltpu.make_async_remote_copy(
                src_ref=x_refs[a] if own else slab(a, *block), dst_ref=slab(a, *block),
                send_sem=send_sems.at[7 * a + k], recv_sem=recv_sems.at[7 * a + k], device_id=to, device_id_type=MESH)

        mine = [pltpu.make_async_copy(x_refs[a], slab(a, *me), local_sems.at[a]) for a in range(na)]
        first = []
        for a in range(na):
            mine[a].start()
            first += [copy(a, 0, me, sib, own=True)] + [copy(a, 1 + j, me, (*chip, mc), own=True) for j, chip in enumerate(chips)]
        for cp in first:
            cp.start()
        passed = []
        for j, chip in enumerate(chips):
            for a in range(na):
                copy(a, 1 + j, (*chip, mc), me).wait_recv()
                passed.append(copy(a, 4 + j, (*chip, mc), sib))
                passed[-1].start()
        for a in range(na):
            copy(a, 0, sib, me).wait_recv()
            for j, chip in enumerate(chips):
                copy(a, 4 + j, (*chip, 1 - mc), me).wait_recv()
        for cp in first + passed:
            cp.wait_send()
        for cp in mine:
            cp.wait()

    return pl.pallas_call(
        body, name=name, out_shape=[S((NDEV,) + x.shape, x.dtype) for x in xs],
        in_specs=_hbm_specs(na), out_specs=_hbm_specs(na),
        scratch_shapes=[pltpu.SemaphoreType.DMA((7 * na,)), pltpu.SemaphoreType.DMA((7 * na,)), pltpu.SemaphoreType.DMA((na,))],
    )(*xs)


def _rs_sibling(gs):
    na = len(gs)

    def body(*refs):
        g_refs, out_refs, send_sems, recv_sems = refs[:na], refs[na:2 * na], refs[2 * na], refs[2 * na + 1]
        mx, my, mc = _me()
        cps = [pltpu.make_async_remote_copy(
            src_ref=g_refs[a].at[2 * ch + 1 - mc], dst_ref=out_refs[a].at[ch], send_sem=send_sems.at[4 * a + ch],
            recv_sem=recv_sems.at[4 * a + ch], device_id=(mx, my, 1 - mc), device_id_type=MESH)
            for a in range(na) for ch in range(4)]
        for cp in cps:
            cp.start()
        for cp in cps:
            cp.wait_recv()
        for cp in cps:
            cp.wait_send()

    return pl.pallas_call(
        body, name="rs_sibling", out_shape=[S((4,) + g.shape[1:], g.dtype) for g in gs],
        in_specs=_hbm_specs(na), out_specs=_hbm_specs(na),
        scratch_shapes=[pltpu.SemaphoreType.DMA((4 * na,)), pltpu.SemaphoreType.DMA((4 * na,))],
    )(*gs)


def _side_allgather(xs):
    na = len(xs)

    def mk(x_refs, out_refs, sems):
        send_sems, recv_sems, local_sems = sems
        mx, my, mc = _me()
        me, sib = (mx, my, mc), (mx, my, 1 - mc)
        chips = [(1 - mx, my), (mx, 1 - my), (1 - mx, 1 - my)]

        def slab(a, px, py, pc):
            return out_refs[a].at[4 * px + 2 * py + pc]

        def copy(a, k, block, to, own=False):
            return pltpu.make_async_remote_copy(
                src_ref=x_refs[a] if own else slab(a, *block), dst_ref=slab(a, *block),
                send_sem=send_sems.at[7 * a + k], recv_sem=recv_sems.at[7 * a + k], device_id=to, device_id_type=MESH)

        mine = [pltpu.make_async_copy(x_refs[a], slab(a, *me), local_sems.at[a]) for a in range(na)]
        first = []
        for a in range(na):
            first += [copy(a, 0, me, sib, own=True)] + [copy(a, 1 + j, me, (*chip, mc), own=True) for j, chip in enumerate(chips)]
        return me, sib, chips, mc, copy, mine, first

    def start(x_refs, out_refs, sems):
        *_, mine, first = mk(x_refs, out_refs, sems)
        for cp in mine + first:
            cp.start()

    def finish(x_refs, out_refs, sems):
        me, sib, chips, mc, copy, mine, first = mk(x_refs, out_refs, sems)
        passed = []
        for j, chip in enumerate(chips):
            for a in range(na):
                copy(a, 1 + j, (*chip, mc), me).wait_recv()
                passed.append(copy(a, 4 + j, (*chip, mc), sib))
                passed[-1].start()
        for a in range(na):
            copy(a, 0, sib, me).wait_recv()
            for j, chip in enumerate(chips):
                copy(a, 4 + j, (*chip, 1 - mc), me).wait_recv()
        for cp in first + passed:
            cp.wait_send()
        for cp in mine:
            cp.wait()

    scratch = [pltpu.SemaphoreType.DMA((7 * na,)), pltpu.SemaphoreType.DMA((7 * na,)), pltpu.SemaphoreType.DMA((na,))]
    return _Side(xs, [S((NDEV,) + x.shape, x.dtype) for x in xs], scratch, start, finish)


def _side_exchange(arrs, nslot, out_slots, route):
    na = len(arrs)

    def copies(in_refs, out_refs, sems):
        send_sems, recv_sems = sems
        return [pltpu.make_async_remote_copy(
            src_ref=in_refs[a].at[src], dst_ref=out_refs[a].at[k], send_sem=send_sems.at[nslot * a + k],
            recv_sem=recv_sems.at[nslot * a + k], device_id=to, device_id_type=MESH)
            for a in range(na) for k, (src, to) in enumerate(route(*_me()))]

    def start(in_refs, out_refs, sems):
        for cp in copies(in_refs, out_refs, sems):
            cp.start()

    def finish(in_refs, out_refs, sems):
        cps = copies(in_refs, out_refs, sems)
        for cp in cps:
            cp.wait_recv()
        for cp in cps:
            cp.wait_send()

    scratch = [pltpu.SemaphoreType.DMA((nslot * na,)), pltpu.SemaphoreType.DMA((nslot * na,))]
    return _Side(arrs, [S((out_slots,) + x.shape[1:], x.dtype) for x in arrs], scratch, start, finish)


def _side_rs_sibling(gs):
    return _side_exchange(gs, 4, 4, lambda mx, my, mc: [(2 * ch + 1 - mc, (mx, my, 1 - mc)) for ch in range(4)])


def _side_rs_chips(ps):
    return _side_exchange(ps, 3, 3, lambda mx, my, mc: [(2 * cx + cy, (cx, cy, mc)) for cx, cy in
                                                        [(1 - mx, my), (mx, 1 - my), (1 - mx, 1 - my)]])


_COLW = 512
_COL_NAMES = ("l0_w_in", "l0_w_ff1", "l1_w_in", "l1_w_ff1")
_ROW_NAMES = ("l0_w_out", "l0_w_ff2", "l1_w_out", "l1_w_ff2")
_BIG_NAMES = _COL_NAMES + _ROW_NAMES


def _full_weight(gathered, name, ncols):
    if name in _ROW_NAMES:
        return gathered.reshape(-1, D)
    return gathered[:, :, :ncols].transpose(1, 0, 2).reshape(D, NDEV * ncols)


def _regroup_w_in0(w):
    main = jnp.concatenate([w[:, 0:1536], w[:, 1544:3080], w[:, 3088:3600]], axis=1)
    small = jnp.concatenate([w[:, 1536:1544], w[:, 3080:3088]], axis=1)
    return jnp.concatenate([main, small, jnp.zeros((D, ZW0 - 3584 - 16), w.dtype)], axis=1)


def _ungroup_w_in0(g):
    return jnp.concatenate([g[:, 0:1536], g[:, 3584:3592], g[:, 1536:3072], g[:, 3592:3600], g[:, 3072:3584]], axis=1)


def _col_slabs(g, ncols):
    g = g.reshape(D, NDEV, ncols).transpose(1, 0, 2)
    return jnp.pad(g, ((0, 0), (0, 0), (0, _COLW - ncols)))


def _sq(t):
    return t * t


def _epi_res_norm(acc, res, gain):
    y = acc + res
    h = _f_norm(y, gain)[0]
    return y, h, h


_RES_NORM_OUTS = [("tile", f32), ("tile", bf16), ("tile_t", bf16)]


def _epi_norm_bwd(acc, x, dres, gain):
    _, vjp = jax.vjp(lambda xx, gg: _f_norm(xx, gg)[0], x, gain)
    dx, dgain = vjp(acc)
    dx = dx + dres
    return dx, dgain, dx


_NORM_BWD_OUTS = [("tile", f32), ("rows", f32), ("tile_t", bf16)]


def _epi_loss(acc, res, tgt):
    e = acc + res - tgt
    dy = e * (1.0 / D)
    return dy, dy, jnp.sum(e * e, axis=0, keepdims=True)


_LOSS_OUTS = [("tile", f32), ("tile_t", bf16), ("rows", f32)]


def _mlp_fwd(tag, x, h, w1, w2, epi, extra, consts, outs):
    a = _mm(f"{tag}_ff1", h, w1, NN, bf16, epi=lambda acc: jnp.maximum(acc, 0.0))
    return _mm(f"{tag}_ff2", a, w2, NN, f32, a_fn=_sq, epi=epi, extra=(x,) + tuple(extra), consts=consts, outs=outs), a


def _mlp_bwd(tag, x, gain, w1, w2, ht, a, dy, dyt, side=None):
    da = _mm(f"{tag}_ff2_dx", dy, w2, NT, bf16, epi=lambda acc, av: acc * 2.0 * av.astype(f32), extra=(a,), side=side)
    if side is not None:
        da, side_res = da
    dw2 = _mm(f"{tag}_ff2_dw", dyt, a, NN, bf16, b_fn=_sq, out_t=True)
    res = _mm(f"{tag}_ff1_dx", da, w1, NT, f32, epi=_epi_norm_bwd, extra=(x, dy), consts=(gain,), outs=_NORM_BWD_OUTS)
    dw1 = _mm(f"{tag}_ff1_dw", ht, da, NN, bf16, slab=True)
    return (res, dw1, dw2) if side is None else ((res, dw1, dw2), side_res)


def _row(v):
    return v.reshape(1, -1).astype(f32)


_L0_REST = ("l0_w_ff1", "l0_w_out")
_L1_MIX = ("l0_w_ff2", "l1_w_in", "l1_w_out")
_L1_FFN = ("l1_w_ff1", "l1_w_ff2")
_GRAD_A = ("l1_w_ff1", "l1_w_ff2", "l1_w_out", "l1_w_in")
_GRAD_B = ("l0_w_ff1", "l0_w_ff2", "l0_w_out")
_IN_PLACE = ("l0_w_ff1", "l1_w_in", "l1_w_ff1")


def _train(x, tgt, args, mom, var, t):
    n = x.shape[0]
    nchunk = t // CH
    mx, my, mc = _me()
    dev, chip = 4 * mx + 2 * my + mc, 2 * mx + my
    core_idx = jnp.reshape(mc, (1,)).astype(jnp.int32)
    own_idx = jnp.stack([dev, chip]).astype(jnp.int32)
    ncols = {nm: args[nm].shape[1] for nm in _COL_NAMES}
    conv_cols = args["l0_gdn_conv"].shape[1]
    g, big, w = {}, {}, {}

    def send(nm):
        a = args[nm].astype(bf16)
        return jnp.pad(a, ((0, 0), (0, _COLW - ncols[nm]))) if nm in _COL_NAMES else a

    def take(names, gathered):
        for nm, arr in zip(names, gathered):
            w[nm] = arr if nm in _IN_PLACE else _full_weight(arr, nm, ncols.get(nm, 0))

    def by_dev(nm, ga):
        return ga if nm in _COL_NAMES else ga.reshape(NDEV, -1, D)

    def pair(names, gs, r1s):
        return [_pair_sum(f"rs_pair_sum_{nm}", core_idx, ga, r1) for nm, ga, r1 in zip(names, gs, r1s)]

    def adam(names, gs, r1s, r2s):
        for nm, ga, r1, r2 in zip(names, gs, r1s, r2s):
            big[nm] = _adamw_big(f"adamw_{nm}", own_idx, ga, r1, r2, args[nm], mom[nm], var[nm])


    li = jnp.arange(FOX_W)
    pm = jnp.where((li[:, None] // FOX_D) == (li[None, :] // FOX_D), 1.0 / FOX_D, 0.0).astype(f32)
    lane_head = jnp.arange(GDN_W) // HD
    sel = lambda first_lane: (jnp.arange(LANES)[:, None] == (first_lane + lane_head)[None, :]).astype(f32)
    e_beta, e_alpha = sel(FOX_H), sel(FOX_H + GDN_H)
    alog_b, dt_b = _row(jnp.repeat(args["l0_gdn_A_log"], HD)), _row(jnp.repeat(args["l0_gdn_dt_bias"], HD))
    gq_t, gk_t = _row(jnp.tile(args["l0_fox_q_norm"], FOX_H)), _row(jnp.tile(args["l0_fox_k_norm"], FOX_H))
    on0_t, on1_t = _row(jnp.tile(args["l0_gdn_o_norm"], GDN_H)), _row(jnp.tile(args["l1_hgrn_o_norm"], HG_H))
    fbias = jnp.pad(_row(args["l0_fox_f_bias"]), ((0, 0), (0, LANES - FOX_H)))
    g0m, g0f, g1m, g1f = (_row(args[k]) for k in ("l0_mix_norm", "l0_ffn_norm", "l1_mix_norm", "l1_ffn_norm"))
    lbl = args["hgrn_lb_logits"].astype(f32)

    first = [send("l0_w_in"), jnp.pad(args["l0_gdn_conv"], ((0, 4), (0, LANES * 2 - conv_cols)))]
    (h0, h0t), first = _tok_fwd("l0_mix_norm", _f_norm, [(x, D, 0)], [(g0m, None, 0)], [(D, bf16)], 256, also_t=(0,),
                                side=_side_allgather(first))
    take(("l0_w_in",), first[:1])
    w_in0 = _regroup_w_in0(w["l0_w_in"])
    wconv = first[1][:, :4, :conv_cols].transpose(1, 0, 2).reshape(4, NDEV * conv_cols)
    z0 = _mm("l0_in", h0, w_in0, NN, f32)
    qk_rows = [(z0, 2 * FOX_W, 0)]
    qk_consts = [(gq_t, None, 0), (gk_t, None, 0), (pm, None, 0)]
    qn, kn = _tok_fwd("fox_pre", _f_foxpre, qk_rows, qk_consts, [(FOX_W, f32)] * 2, 256)
    ccol, crow = _fox_gate_fwd(z0, fbias, t)
    fox_o, got = _fox_attn_fwd(qn, kn, z0, ccol, crow, t, side=_side_allgather([send(nm) for nm in _L0_REST]))
    take(_L0_REST, got)
    conv_rows, conv_consts = [(z0, LANES, Z0_GQKV)], [(wconv, LANES, 0)]
    (qkv,) = _tok_fwd("gdn_conv", _f_conv, conv_rows, conv_consts, [(LANES, f32)], t, ncb=12, with_j=True)
    gate_consts = [(e_beta, None, 0), (e_alpha, None, 0), (alog_b, None, 0), (dt_b, None, 0)]
    intra_rows = [(qkv, GDN_W, 0), (qkv, GDN_W, 1), (qkv, GDN_W, 2), (z0, LANES, Z0_SMALL)]
    intra, got = _tok_fwd("gdn_intra", _f_gdn_gated, intra_rows, gate_consts, [(GDN_W, f32)] * 7, 2 * CH,
                          side=_side_allgather([send(nm) for nm in _L1_MIX]))
    take(_L1_MIX, got)
    inter_rows = [(a, 0, 1) for a in intra[:6]]
    full_rows = inter_rows + [(fox_o, 0, 1), (z0, Z0_GG // 4, 1)]
    (cat0, gdn_hist), got = _scan_fwd("gdn_scan", _f_gdn_full, full_rows, GDN_H, nchunk, consts=(on0_t,), out_dtype=bf16, out_k=2,
                                      side=_side_allgather([send(nm) for nm in _L1_FFN[1:]]))
    take(_L1_FFN[1:], got)
    x1, hf0, hf0t = _mm("l0_out", cat0, w["l0_w_out"], NN, f32, epi=_epi_res_norm, extra=(x,), consts=(g0f,), outs=_RES_NORM_OUTS)
    (x2, h1, h1t), a0 = _mlp_fwd("l0", x1, hf0, w["l0_w_ff1"], w["l0_w_ff2"], _epi_res_norm, (), (g1m,), _RES_NORM_OUTS)

    z1 = _mm("l1_in", h1, w["l1_w_in"], NN, f32)
    (cat1, hg_hist), got = _scan_fwd("hgrn_scan", _f_hgrn_full, [(z1, 0, 4)], HG_H, nchunk, consts=(lbl, on1_t), out_dtype=bf16,
                                     side=_side_allgather([send(nm) for nm in _L1_FFN[:1]]))
    take(_L1_FFN[:1], got)
    x3, hf1, hf1t = _mm("l1_out", cat1, w["l1_w_out"], NN, f32, epi=_epi_res_norm, extra=(x2,), consts=(g1f,), outs=_RES_NORM_OUTS)
    (dy, dyt, loss_row), a1 = _mlp_fwd("l1", x3, hf1, w["l1_w_ff1"], w["l1_w_ff2"], _epi_loss, (tgt,), (), _LOSS_OUTS)

    (dx3, g["l1_ffn_norm"], dx3t), ga_ff1, ga_ff2 = _mlp_bwd("l1", x3, g1f, w["l1_w_ff1"], w["l1_w_ff2"], hf1t, a1, dy, dyt)
    dcat1 = _mm("l1_out_dx", dx3, w["l1_w_out"], NT, f32)
    ga_out = _mm("l1_out_dw", dx3t, cat1, NN, bf16, out_t=True)
    dz1, dlbl, don1 = _scan_bwd("hgrn_scan_bwd", _f_hgrn_full, [(z1, 0, 4)], hg_hist, dcat1, HG_H, nchunk, dtypes=[bf16],
                                consts=(lbl, on1_t))
    dx2, g["l1_mix_norm"], dx2t = _mm("l1_in_dx", dz1, w["l1_w_in"], NT, f32, epi=_epi_norm_bwd, extra=(x2, dx3), consts=(g1m,),
                                      outs=_NORM_BWD_OUTS)
    ga_in = _mm("l1_in_dw", h1t, dz1, NN, bf16, slab=True)
    g["l1_hgrn_o_norm"] = don1.reshape(HG_H, HD).sum(0)
    g["hgrn_lb_logits"] = dlbl
    gs_a = [by_dev(nm, ga) for nm, ga in zip(_GRAD_A, (ga_ff1, ga_ff2, ga_out, ga_in))]

    ((dx1, g["l0_ffn_norm"], dx1t), gb_ff1, gb_ff2), r1_a = _mlp_bwd("l0", x1, g0f, w["l0_w_ff1"], w["l0_w_ff2"], hf0t, a0, dx2, dx2t,
                                                                  side=_side_rs_sibling(gs_a))
    pairs_a = pair(_GRAD_A, gs_a, r1_a)
    dcat0 = _mm("l0_out_dx", dx1, w["l0_w_out"], NT, f32)
    gb_out = _mm("l0_out_dw", dx1t, cat0, NN, bf16, out_t=True)
    gs_b = [by_dev(nm, ga) for nm, ga in zip(_GRAD_B, (gb_ff1, gb_ff2, gb_out))]
    *dinter, dfox_o, dgg, don0 = _scan_bwd("gdn_scan_bwd", _f_gdn_full, full_rows, gdn_hist, dcat0, GDN_H, nchunk, consts=(on0_t,),
                                           out_k=2, dtypes=[f32] * 7 + [bf16])
    (dqn, dkn, dfv, dcq, dck), got = _fox_attn_bwd(qn, kn, z0, ccol, crow, dfox_o, t,
                                                   side=_join_sides([_side_rs_chips(pairs_a), _side_rs_sibling(gs_b)]))
    r2_a, r1_b = got[:len(_GRAD_A)], got[len(_GRAD_A):]
    adam(_GRAD_A, gs_a, r1_a, r2_a)
    pairs_b = pair(_GRAD_B, gs_b, r1_b)
    (dqkv, dzs_g, dalog_b, ddt_b), r2_b = _tok_bwd(
        "gdn_intra_bwd", _f_gdn_gated_given, intra_rows + [(intra[6], GDN_W, 0)], gate_consts, [(a, GDN_W) for a in dinter],
        2 * CH, [0, 1, 2, 3], [2, 3], ncat=3, side=_side_rs_chips(pairs_b))
    adam(_GRAD_B, gs_b, r1_b, r2_b)
    dgqkv, dwconv = _tok_bwd("gdn_conv_bwd", _f_conv, conv_rows, conv_consts, [(dqkv, LANES)], t, [0], [0], ncb=12, with_j=True,
                             drow_dtype=bf16)
    dzs_f, dfb = _fox_gate_bwd(z0, fbias, dcq, dck, t)
    dzqk, dgq_t, dgk_t = _tok_bwd("fox_pre_bwd", _f_foxpre, qk_rows, qk_consts, [(dqn, FOX_W), (dkn, FOX_W)], 256, [0], [0, 1],
                                  drow_dtype=bf16)
    dz0 = jnp.concatenate([dzqk, dfv.astype(bf16), dgqkv, dgg, (dzs_g + dzs_f).astype(bf16), jnp.zeros((n, ZW0 - 3712), bf16)], axis=1)
    gs_c = [_col_slabs(_ungroup_w_in0(_mm("l0_in_dw", h0t, dz0, NN, bf16)), ncols["l0_w_in"])]
    r1_c = _rs_sibling(gs_c)
    pairs_c = pair(("l0_w_in",), gs_c, r1_c)
    (dx, g["l0_mix_norm"], _), r2_c = _mm("l0_in_dx", dz0, w_in0, NT, f32, epi=_epi_norm_bwd, extra=(x, dx1), consts=(g0m,),
                                          outs=_NORM_BWD_OUTS, side=_side_rs_chips(pairs_c))
    adam(("l0_w_in",), gs_c, r1_c, r2_c)
    g["l0_fox_q_norm"] = dgq_t.reshape(FOX_H, FOX_D).sum(0)
    g["l0_fox_k_norm"] = dgk_t.reshape(FOX_H, FOX_D).sum(0)
    g["l0_fox_f_bias"] = dfb[0, :FOX_H]
    g["l0_gdn_conv"] = dwconv
    g["l0_gdn_A_log"] = dalog_b.reshape(GDN_H, HD).sum(1)
    g["l0_gdn_dt_bias"] = ddt_b.reshape(GDN_H, HD).sum(1)
    g["l0_gdn_o_norm"] = don0.reshape(GDN_H, HD).sum(0)
    return loss_row, dx, g, big


_NAMES = ("l0_mix_norm", "l0_w_in", "l0_fox_q_norm", "l0_fox_k_norm", "l0_fox_f_bias", "l0_gdn_conv", "l0_gdn_A_log",
          "l0_gdn_dt_bias", "l0_gdn_o_norm", "l0_w_out", "l0_ffn_norm", "l0_w_ff1", "l0_w_ff2", "l1_mix_norm", "l1_w_in",
          "l1_hgrn_o_norm", "l1_w_out", "l1_ffn_norm", "l1_w_ff1", "l1_w_ff2", "hgrn_lb_logits")
_SMALL_NAMES = tuple(nm for nm in _NAMES if nm not in _BIG_NAMES)
_SMALL_ROWS = 16


def _pack_small(vals):
    flat = jnp.concatenate([vals[nm].reshape(-1).astype(f32) for nm in _SMALL_NAMES])
    return jnp.pad(flat, (0, _SMALL_ROWS * D - flat.shape[0])).reshape(_SMALL_ROWS, D)


def _unpack_small(packed, shapes):
    flat = packed.reshape(-1)
    out, off = {}, 0
    for nm in _SMALL_NAMES:
        size = 1
        for s in shapes[nm]:
            size *= s
        out[nm] = flat[off:off + size].reshape(shapes[nm])
        off += size
    return out, off


def kernel(x, l0_mix_norm, l0_w_in, l0_fox_q_norm, l0_fox_k_norm, l0_fox_f_bias, l0_gdn_conv, l0_gdn_A_log, l0_gdn_dt_bias, l0_gdn_o_norm, l0_w_out, l0_ffn_norm, l0_w_ff1, l0_w_ff2, l1_mix_norm, l1_w_in, l1_hgrn_o_norm, l1_w_out, l1_ffn_norm, l1_w_ff1, l1_w_ff2, hgrn_lb_logits, loss_target, m_l0_mix_norm, m_l0_w_in, m_l0_fox_q_norm, m_l0_fox_k_norm, m_l0_fox_f_bias, m_l0_gdn_conv, m_l0_gdn_A_log, m_l0_gdn_dt_bias, m_l0_gdn_o_norm, m_l0_w_out, m_l0_ffn_norm, m_l0_w_ff1, m_l0_w_ff2, m_l1_mix_norm, m_l1_w_in, m_l1_hgrn_o_norm, m_l1_w_out, m_l1_ffn_norm, m_l1_w_ff1, m_l1_w_ff2, m_hgrn_lb_logits, v_l0_mix_norm, v_l0_w_in, v_l0_fox_q_norm, v_l0_fox_k_norm, v_l0_fox_f_bias, v_l0_gdn_conv, v_l0_gdn_A_log, v_l0_gdn_dt_bias, v_l0_gdn_o_norm, v_l0_w_out, v_l0_ffn_norm, v_l0_w_ff1, v_l0_w_ff2, v_l1_mix_norm, v_l1_w_in, v_l1_hgrn_o_norm, v_l1_w_out, v_l1_ffn_norm, v_l1_w_ff1, v_l1_w_ff2, v_hgrn_lb_logits):
    args = dict(zip(_NAMES, (l0_mix_norm, l0_w_in, l0_fox_q_norm, l0_fox_k_norm, l0_fox_f_bias, l0_gdn_conv, l0_gdn_A_log, l0_gdn_dt_bias, l0_gdn_o_norm, l0_w_out, l0_ffn_norm, l0_w_ff1, l0_w_ff2, l1_mix_norm, l1_w_in, l1_hgrn_o_norm, l1_w_out, l1_ffn_norm, l1_w_ff1, l1_w_ff2, hgrn_lb_logits)))
    mom = dict(zip(_NAMES, (m_l0_mix_norm, m_l0_w_in, m_l0_fox_q_norm, m_l0_fox_k_norm, m_l0_fox_f_bias, m_l0_gdn_conv, m_l0_gdn_A_log, m_l0_gdn_dt_bias, m_l0_gdn_o_norm, m_l0_w_out, m_l0_ffn_norm, m_l0_w_ff1, m_l0_w_ff2, m_l1_mix_norm, m_l1_w_in, m_l1_hgrn_o_norm, m_l1_w_out, m_l1_ffn_norm, m_l1_w_ff1, m_l1_w_ff2, m_hgrn_lb_logits)))
    var = dict(zip(_NAMES, (v_l0_mix_norm, v_l0_w_in, v_l0_fox_q_norm, v_l0_fox_k_norm, v_l0_fox_f_bias, v_l0_gdn_conv, v_l0_gdn_A_log, v_l0_gdn_dt_bias, v_l0_gdn_o_norm, v_l0_w_out, v_l0_ffn_norm, v_l0_w_ff1, v_l0_w_ff2, v_l1_mix_norm, v_l1_w_in, v_l1_hgrn_o_norm, v_l1_w_out, v_l1_ffn_norm, v_l1_w_ff1, v_l1_w_ff2, v_hgrn_lb_logits)))
    nb, t, _ = x.shape
    dev = 4 * lax.axis_index("x") + 2 * lax.axis_index("y") + lax.axis_index("c")
    conv_cols = l0_gdn_conv.shape[1]
    loss_row, dx, g, big = _train(x.reshape(nb * t, D), loss_target.reshape(nb * t, D), args, mom, var, t)

    shapes = {nm: args[nm].shape for nm in _SMALL_NAMES}
    gsm = dict(g)
    gsm["l0_gdn_conv"] = jnp.zeros(shapes["l0_gdn_conv"], f32)
    packed = _pack_small(gsm)
    _, used = _unpack_small(packed, shapes)
    flat_extra = jnp.concatenate([jnp.sum(loss_row).reshape(1), g["l0_gdn_conv"].reshape(-1)])
    packed = packed.reshape(-1).at[used:used + flat_extra.shape[0]].set(flat_extra).reshape(_SMALL_ROWS, D)
    (parts,) = _allgather("ag_small", [packed])
    total = _sum_parts(parts).reshape(-1)
    loss = 0.5 * total[used] / D
    conv_g_full = total[used + 1:used + 1 + 4 * NDEV * conv_cols].reshape(4, NDEV * conv_cols)
    conv_g = lax.dynamic_slice(conv_g_full, (0, dev * conv_cols), (4, conv_cols))
    own_vals = {nm: jnp.zeros(shapes[nm], f32) for nm in _SMALL_NAMES}
    own_vals["l0_gdn_conv"] = conv_g
    own_mask = {nm: jnp.zeros(shapes[nm], f32) for nm in _SMALL_NAMES}
    own_mask["l0_gdn_conv"] = jnp.ones(shapes["l0_gdn_conv"], f32)
    small = _adamw_small(parts, _pack_small(args), _pack_small(mom), _pack_small(var), _pack_small(own_mask), _pack_small(own_vals))
    small = [_unpack_small(a, shapes)[0] for a in small]
    small[0]["l0_gdn_conv"] = conv_g

    outs = [loss, dx.reshape(nb, t, D)]
    for k in range(4):
        outs += [big[nm][k] if nm in _BIG_NAMES else small[k][nm] for nm in _NAMES]
    return tuple(outs)
```

```python
import functools

import jax
import jax.numpy as jnp
from jax import lax
from jax.experimental import pallas as pl
from jax.experimental.pallas import tpu as pltpu

f32, bf16 = jnp.float32, jnp.bfloat16
NN = (((1,), (0,)), ((), ()))
NT = (((1,), (1,)), ((), ()))
TN = (((0,), (0,)), ((), ()))
HI = lax.Precision.HIGHEST
MESH = pl.DeviceIdType.MESH
S = jax.ShapeDtypeStruct

EPS = 1e-6
D = 1024
LANES = 128
FOX_H, FOX_D, FOX_W = 8, 64, 512
GDN_H, HD, GDN_W = 4, 128, 512
HG_H = 8
CH = 64
ZW0 = 3840
NDEV = 8
ADAM_LR, ADAM_B1, ADAM_B2, ADAM_EPS, ADAM_WD, ADAM_STEP = 0.001, 0.9, 0.999, 1e-08, 0.01, 10

Z0_FQK, Z0_FV, Z0_GQKV, Z0_GG, Z0_SMALL = 0, 8, 12, 24, 28


def _dot(a, b, dims=NN, prec=None):
    return lax.dot_general(a, b, dims, precision=prec, preferred_element_type=f32)


def _iota2(shape, axis):
    return lax.broadcasted_iota(jnp.int32, shape, axis)


def _split3(x):
    x1 = x.astype(bf16)
    r = x - x1.astype(f32)
    x2 = r.astype(bf16)
    return x1, x2, (r - x2.astype(f32)).astype(bf16)


def _dot_sel(a, b, dims=NN, exact_lhs=False):
    if exact_lhs:
        return sum(_dot(a.astype(bf16), piece, dims) for piece in _split3(b))
    return sum(_dot(piece, b.astype(bf16), dims) for piece in _split3(a))


@jax.custom_vjp
def _sel_rhs(a, b):
    return _dot_sel(a, b)


_sel_rhs.defvjp(lambda a, b: (_dot_sel(a, b), b), lambda b, g: (_dot_sel(g, b, NT), jnp.zeros_like(b)))


@jax.custom_vjp
def _sel_lhs(a, x):
    return _dot_sel(a, x, exact_lhs=True)


_sel_lhs.defvjp(lambda a, x: (_dot_sel(a, x, exact_lhs=True), a), lambda a, g: (jnp.zeros_like(a), _dot_sel(a, g, TN, exact_lhs=True)))


class _Side:
    def __init__(self, ins, out_shapes, scratch, start, finish):
        self.ins, self.out_shapes, self.scratch, self.start, self.finish = list(ins), list(out_shapes), list(scratch), start, finish


def _join_sides(sides):
    def split(refs, counts):
        out, off = [], 0
        for c in counts:
            out.append(refs[off:off + c])
            off += c
        return out

    ni, no, ns = ([len(getattr(sd, a)) for sd in sides] for a in ("ins", "out_shapes", "scratch"))

    def run(which):
        def go(ins, outs, sems):
            for sd, i, o, c in zip(sides, split(ins, ni), split(outs, no), split(sems, ns)):
                getattr(sd, which)(i, o, c)
        return go

    return _Side(sum((sd.ins for sd in sides), []), sum((sd.out_shapes for sd in sides), []),
                 sum((sd.scratch for sd in sides), []), run("start"), run("finish"))


def _pcall(side, body, *, name, grid, in_specs, out_specs, out_shape, scratch_shapes=(), compiler_params=None):
    if side is None:
        return pl.pallas_call(body, name=name, grid=grid, in_specs=in_specs, out_specs=out_specs, out_shape=out_shape,
                              scratch_shapes=scratch_shapes, compiler_params=compiler_params)
    single = not isinstance(out_shape, (list, tuple))
    ospecs, oshape = ([out_specs], [out_shape]) if single else (list(out_specs), list(out_shape))
    nin, nout, nscr = len(in_specs), len(ospecs), len(scratch_shapes)
    si, so = len(side.ins), len(side.out_shapes)

    def wrapped(*refs):
        o0 = nin + si
        c0 = o0 + nout + so
        sins, souts, ssems = refs[nin:o0], refs[o0 + nout:c0], refs[c0 + nscr:]
        ids = [pl.program_id(a) for a in range(len(grid))]
        first = functools.reduce(jnp.logical_and, [i == 0 for i in ids])
        last = functools.reduce(jnp.logical_and, [i == g - 1 for i, g in zip(ids, grid)])

        @pl.when(first)
        def _():
            side.start(sins, souts, ssems)

        body(*refs[:nin], *refs[o0:o0 + nout], *refs[c0:c0 + nscr])

        @pl.when(last)
        def _():
            side.finish(sins, souts, ssems)

    call = pl.pallas_call(
        wrapped, name=name, grid=grid, in_specs=list(in_specs) + _hbm_specs(si), out_specs=ospecs + _hbm_specs(so),
        out_shape=oshape + side.out_shapes, scratch_shapes=list(scratch_shapes) + side.scratch,
        compiler_params=pltpu.CompilerParams(dimension_semantics=("arbitrary",) * len(grid),
                                             vmem_limit_bytes=getattr(compiler_params, "vmem_limit_bytes", None)))

    def run(*args):
        res = call(*args, *side.ins)
        return (res[0] if single else list(res[:nout])), list(res[nout:])

    return run


def _tok_specs(rows, consts, tm):
    specs = []
    for (_, w, base) in rows:
        specs.append(pl.BlockSpec((tm, w), functools.partial(lambda j, i, b: (i, b + j), b=base)))
    for (arr, w, base) in consts:
        if w is None:
            specs.append(pl.BlockSpec(arr.shape, lambda j, i: (0, 0)))
        else:
            specs.append(pl.BlockSpec((arr.shape[0], w), functools.partial(lambda j, i, b: (0, b + j), b=base)))
    return specs


def _tok_fwd(name, f, rows, consts, outs, tm, ncb=1, with_j=False, also_t=(), side=None):
    n = rows[0][0].shape[0]
    nin = len(rows) + len(consts)
    nout = len(outs)

    def body(*refs):
        ins = [r[...] for r in refs[:nin]]
        vals = f(pl.program_id(0), *ins) if with_j else f(*ins)
        for r, v in zip(refs[nin:nin + nout], vals):
            r[...] = v.astype(r.dtype)
        for r, k in zip(refs[nin + nout:], also_t):
            r[...] = vals[k].T.astype(r.dtype)

    return _pcall(
        side, body, name=name, grid=(ncb, n // tm),
        in_specs=_tok_specs(rows, consts, tm),
        out_specs=[pl.BlockSpec((tm, w), lambda j, i: (i, j)) for (w, _) in outs]
        + [pl.BlockSpec((outs[k][0], tm), lambda j, i: (0, i)) for k in also_t],
        out_shape=[S((n, w * ncb), dt) for (w, dt) in outs] + [S((outs[k][0], n), bf16) for k in also_t],
        compiler_params=pltpu.CompilerParams(dimension_semantics=("parallel", "parallel")),
    )(*[r[0] for r in rows], *[c[0] for c in consts])


def _tok_bwd(name, f, rows, consts, cots, tm, drow, dconst, ncb=1, with_j=False, addto=None, also_t=(), drow_dtype=f32, side=None, ncat=0):
    n = rows[0][0].shape[0]
    nr, nc, nct = len(rows), len(consts), len(cots)
    addto = addto or {}
    add_keys = sorted(addto)
    nadd = len(add_keys)

    def body(*refs):
        ins = [r[...] for r in refs[:nr + nc]]
        cot = [r[...] for r in refs[nr + nc:nr + nc + nct]]
        adds = refs[nr + nc + nct:nr + nc + nct + nadd]
        outs = refs[nr + nc + nct + nadd:]
        pos = list(drow) + [nr + k for k in dconst]

        def g(*dargs):
            full = list(ins)
            for p, a in zip(pos, dargs):
                full[p] = a
            return tuple(f(pl.program_id(0), *full) if with_j else f(*full))

        vals, vjp = jax.vjp(g, *[ins[p] for p in pos])
        grads = vjp(tuple(c.astype(v.dtype) for c, v in zip(cot, vals)))
        off = 0
        for k in range(len(drow)):
            gk = grads[k]
            if k in addto:
                gk = gk + adds[add_keys.index(k)][...]
            if k < ncat:
                outs[0][:, off:off + gk.shape[1]] = gk.astype(outs[0].dtype)
                off += gk.shape[1]
            else:
                outs[k - skip][...] = gk.astype(outs[k - skip].dtype)
            if k in also_t:
                tref = outs[len(drow) - skip + len(dconst) + list(also_t).index(k)]
                tref[...] = gk.T.astype(tref.dtype)
        first = pl.program_id(1) == 0
        for k in range(len(dconst)):
            ref = outs[len(drow) - skip + k]

            @pl.when(first)
            def _():
                ref[...] = jnp.zeros_like(ref)

            ref[...] += grads[len(drow) + k]

    skip = max(ncat - 1, 0)
    in_specs = _tok_specs(rows, consts, tm)
    in_specs += [pl.BlockSpec((tm, w), lambda j, i: (i, j)) for (_, w) in cots]
    in_specs += [pl.BlockSpec((tm, rows[drow[k]][1]), lambda j, i: (i, j)) for k in add_keys]
    dts = drow_dtype if isinstance(drow_dtype, (list, tuple)) else [drow_dtype] * len(drow)
    widths = [rows[k][1] for k in drow]
    if ncat:
        widths, dts = [sum(widths[:ncat])] + widths[ncat:], [dts[0]] + list(dts[ncat:])
    out_specs = [pl.BlockSpec((tm, wd), lambda j, i: (i, j)) for wd in widths]
    out_shape = [S((n, wd * ncb), dt) for wd, dt in zip(widths, dts)]
    for k in dconst:
        arr, w, _ = consts[k]
        if w is None:
            out_specs.append(pl.BlockSpec(arr.shape, lambda j, i: (0, 0)))
            out_shape.append(S(arr.shape, f32))
        else:
            out_specs.append(pl.BlockSpec((arr.shape[0], w), lambda j, i: (0, j)))
            out_shape.append(S((arr.shape[0], w * ncb), f32))
    for k in also_t:
        out_specs.append(pl.BlockSpec((rows[drow[k]][1], tm), lambda j, i: (0, i)))
        out_shape.append(S((rows[drow[k]][1], n), bf16))
    return _pcall(
        side, body, name=name, grid=(ncb, n // tm), in_specs=in_specs, out_specs=out_specs, out_shape=out_shape,
        compiler_params=pltpu.CompilerParams(dimension_semantics=("parallel", "arbitrary")),
    )(*[r[0] for r in rows], *[c[0] for c in consts], *[c[0] for c in cots], *[addto[k] for k in add_keys])


def _scan_fwd(name, f, rows, nh, nchunk, side=None, consts=(), out_dtype=f32, out_k=1):
    n = rows[0][0].shape[0]
    nb = n // (CH * nchunk)
    nin, nco = len(rows), len(consts)
    w = nh * HD

    def body(*refs):
        o_ref, hist_ref, st = refs[nin + nco], refs[nin + nco + 1], refs[nin + nco + 2]

        @pl.when(pl.program_id(0) == 0)
        def _():
            st[...] = jnp.zeros_like(st)

        s0 = st[...]
        hist_ref[0] = s0.astype(hist_ref.dtype)
        tiles = [r[...].reshape(nb * CH, r.shape[2]) for r in refs[:nin]]
        o, s1 = f(*tiles, *[r[...] for r in refs[nin:nin + nco]], s0)
        o_ref[...] = o.reshape(o_ref.shape).astype(o_ref.dtype)
        st[...] = s1

    seq3 = lambda a: a.reshape(nb, nchunk * CH, a.shape[1])
    res = _pcall(
        side, body, name=name, grid=(nchunk,),
        in_specs=[pl.BlockSpec((nb, CH, k * w), functools.partial(lambda c, base: (0, c, base), base=b)) for (_, b, k) in rows]
        + [pl.BlockSpec(c.shape, lambda c: (0, 0)) for c in consts],
        out_specs=[pl.BlockSpec((nb, CH, out_k * w), lambda c: (0, c, 0)), pl.BlockSpec((1, nb * w, HD), lambda c: (c, 0, 0))],
        out_shape=[S((nb, nchunk * CH, out_k * w), out_dtype), S((nchunk, nb * w, HD), bf16)],
        scratch_shapes=[pltpu.VMEM((nb * w, HD), f32)],
        compiler_params=pltpu.CompilerParams(dimension_semantics=("arbitrary",)),
    )(*[seq3(r[0]) for r in rows], *consts)
    (o, hist), extra = (res, None) if side is None else res
    out = [o.reshape(n, out_k * w), hist]
    return out if side is None else (out, extra)


def _scan_bwd(name, f, rows, hist, do, nh, nchunk, side=None, dtypes=None, consts=(), out_k=1):
    n = rows[0][0].shape[0]
    nb = n // (CH * nchunk)
    nin, nco = len(rows), len(consts)
    w = nh * HD

    def body(*refs):
        hist_ref, do_ref = refs[nin + nco], refs[nin + nco + 1]
        outs = refs[nin + nco + 2:nin + nco + 2 + nin]
        couts = refs[nin + nco + 2 + nin:nin + nco + 2 + nin + nco]
        ds = refs[nin + nco + 2 + nin + nco]

        @pl.when(pl.program_id(0) == 0)
        def _():
            ds[...] = jnp.zeros_like(ds)
            for c in couts:
                c[...] = jnp.zeros_like(c)

        tiles = [r[...].reshape(nb * CH, r.shape[2]) for r in refs[:nin]]
        _, vjp = jax.vjp(f, *tiles, *[r[...] for r in refs[nin:nin + nco]], hist_ref[0].astype(f32))
        grads = vjp((do_ref[...].reshape(nb * CH, out_k * w), ds[...]))
        for r, gk in zip(outs, grads[:nin]):
            r[...] = gk.reshape(r.shape).astype(r.dtype)
        for c, gk in zip(couts, grads[nin:nin + nco]):
            c[...] += gk
        ds[...] = grads[nin + nco]

    seq3 = lambda a: a.reshape(nb, nchunk * CH, a.shape[1])
    rev = lambda c, base: (0, nchunk - 1 - c, base)
    res = _pcall(
        side, body, name=name, grid=(nchunk,),
        in_specs=[pl.BlockSpec((nb, CH, k * w), functools.partial(rev, base=b)) for (_, b, k) in rows]
        + [pl.BlockSpec(c.shape, lambda c: (0, 0)) for c in consts]
        + [pl.BlockSpec((1, nb * w, HD), lambda c: (nchunk - 1 - c, 0, 0)), pl.BlockSpec((nb, CH, out_k * w), functools.partial(rev, base=0))],
        out_specs=[pl.BlockSpec((nb, CH, k * w), functools.partial(rev, base=0)) for (_, _, k) in rows]
        + [pl.BlockSpec(c.shape, lambda c: (0, 0)) for c in consts],
        out_shape=[S((nb, nchunk * CH, k * w), dt) for (_, _, k), dt in zip(rows, dtypes or [f32] * nin)]
        + [S(c.shape, f32) for c in consts],
        scratch_shapes=[pltpu.VMEM((nb * w, HD), f32)],
        compiler_params=pltpu.CompilerParams(dimension_semantics=("arbitrary",)),
    )(*[seq3(r[0]) for r in rows], *consts, hist, seq3(do))
    outs, extra = (res, None) if side is None else res
    outs = [o.reshape(n, o.shape[2]) for o in outs[:nin]] + list(outs[nin:])
    return outs if side is None else (outs, extra)


_VMEM_LIMIT = 56 * 2 ** 20
_VMEM_TILE_BUDGET = 40 * 2 ** 20


def _mm_tiles(m, n, k, sa, sb, so, sx, a_f32, b_f32, tn_fixed):
    best = None
    for tm in (1024, 512, 256, 128, 64):
        for tn in ((tn_fixed,) if tn_fixed else (1024, 768, 512, 384, 256, 128)):
            if m % tm or n % tn:
                continue
            need = 2 * (tm * k * sa + k * tn * sb + tm * tn * (so + sx)) + tm * tn * 4
            need += tm * k * (2 if sa == 4 else 0) + k * tn * (2 if sb == 4 else 0)
            need += tm * k * (4 if a_f32 else 0) + k * tn * (4 if b_f32 else 0)
            if need <= _VMEM_TILE_BUDGET and (best is None or (tm * tn, tm) > best[0]):
                best = ((tm * tn, tm), tm, tn)
    return best[1], best[2]


def _mm(name, a, b, dims, out_dtype, a_fn=None, b_fn=None, epi=None, extra=(), consts=(), outs=None, out_t=False, slab=False,
        side=None):
    m, kk = a.shape
    gathered = b.ndim == 3
    if gathered:
        nn = NDEV * _COLW if dims is NN else b.shape[1]
    else:
        nn = b.shape[1] if dims is NN else b.shape[0]
    kinds = [("tile", out_dtype)] if outs is None else list(outs)
    so = sum(jnp.dtype(dt).itemsize for kd, dt in kinds if kd != "rows")
    sx = sum(e.dtype.itemsize for e in extra)
    full_rows = bool(consts) or any(kd == "rows" for kd, _ in kinds)
    tm, tn = _mm_tiles(m, nn, kk, a.dtype.itemsize, b.dtype.itemsize, so, sx,
                       a_fn is not None, b_fn is not None,
                       _COLW if slab else (nn if full_rows else None))
    gblocks = tn // _COLW if (gathered and dims is NN) else 0
    assert not gblocks or (outs is None and not extra and not out_t and not slab and b_fn is None)
    nex, nco = len(extra), len(consts)

    def body(a_ref, b_ref, *rest):
        av = a_ref[...]
        if a_fn is not None:
            av = a_fn(av.astype(f32))
        av = av.astype(bf16)
        if gblocks:
            for gb in range(gblocks):
                part = _dot(av, b_ref[gb], NN)
                rest[nex + nco][:, _COLW * gb:_COLW * (gb + 1)] = (part if epi is None else epi(part)).astype(rest[nex + nco].dtype)
            return
        if gathered:
            acc = sum(_dot(av[:, _COLW * d:_COLW * (d + 1)], b_ref[d], NT) for d in range(NDEV))
        else:
            bv = b_ref[...]
            if b_fn is not None:
                bv = b_fn(bv.astype(f32))
            acc = _dot(av, bv.astype(bf16), dims)
        if epi is not None:
            acc = epi(acc, *[r[...] for r in rest[:nex + nco]])
        vals = acc if isinstance(acc, tuple) else (acc,)
        for (kd, _), o_ref, val in zip(kinds, rest[nex + nco:], vals):
            if kd == "rows":
                @pl.when(pl.program_id(0) == 0)
                def _():
                    o_ref[...] = jnp.zeros_like(o_ref)

                o_ref[...] += val
            elif kd == "tile_t" or out_t:
                o_ref[...] = val.T.astype(o_ref.dtype)
            elif slab:
                o_ref[0] = val.astype(o_ref.dtype)
            else:
                o_ref[...] = val.astype(o_ref.dtype)

    if gathered:
        bspec = (pl.BlockSpec((gblocks, kk, _COLW), lambda i, j: (j, 0, 0)) if dims is NN
                 else pl.BlockSpec((NDEV, tn, _COLW), lambda i, j: (0, j, 0)))
    else:
        bspec = pl.BlockSpec((kk, tn), lambda i, j: (0, j)) if dims is NN else pl.BlockSpec((tn, kk), lambda i, j: (j, 0))
    out_specs, out_shape = [], []
    for kd, dt in kinds:
        if kd == "rows":
            out_specs.append(pl.BlockSpec((1, nn), lambda i, j: (0, 0)))
            out_shape.append(S((1, nn), dt))
        elif kd == "tile_t" or out_t:
            out_specs.append(pl.BlockSpec((tn, tm), lambda i, j: (j, i)))
            out_shape.append(S((nn, m), dt))
        elif slab:
            out_specs.append(pl.BlockSpec((1, tm, tn), lambda i, j: (j, i, 0)))
            out_shape.append(S((nn // tn, m, tn), dt))
        else:
            out_specs.append(pl.BlockSpec((tm, tn), lambda i, j: (i, j)))
            out_shape.append(S((m, nn), dt))
    if outs is None:
        out_specs, out_shape = out_specs[0], out_shape[0]
    sem = ("arbitrary", "arbitrary") if any(kd == "rows" for kd, _ in kinds) else ("parallel", "parallel")
    return _pcall(
        side, body, name=name, grid=(m // tm, nn // tn),
        in_specs=[pl.BlockSpec((tm, kk), lambda i, j: (i, 0)), bspec]
        + [pl.BlockSpec((tm, tn), lambda i, j: (i, j)) for _ in extra]
        + [pl.BlockSpec(c.shape, lambda i, j: (0, 0)) for c in consts],
        out_specs=out_specs, out_shape=out_shape,
        compiler_params=pltpu.CompilerParams(dimension_semantics=sem, vmem_limit_bytes=_VMEM_LIMIT),
    )(a, b, *extra, *consts)


def _f_norm(x, g):
    return (x * lax.rsqrt(jnp.mean(x * x, axis=-1, keepdims=True) + EPS) * g,)


def _f_foxpre(zqk, gq, gk, pm):
    def nrm(t, g):
        return t * lax.rsqrt(_sel_rhs(t * t, pm) + EPS) * g
    return nrm(zqk[:, :FOX_W], gq), nrm(zqk[:, FOX_W:], gk)


def _chunk_cumsum(x):
    n = x.shape[0]
    r, c = _iota2((n, n), 0), _iota2((n, n), 1)
    tri = jnp.logical_and(r >= c, (r // CH) == (c // CH)).astype(f32)
    return _sel_lhs(tri, x)


def _f_gdngate(zs, eb, ea, alog_b, dt_b):
    beta = jax.nn.sigmoid(_sel_rhs(zs, eb))
    la = -jnp.exp(alog_b) * jax.nn.softplus(_sel_rhs(zs, ea) + dt_b)
    return beta, _chunk_cumsum(la)


def _f_conv(j, x, w):
    t = x.shape[0]
    y = x * w[3:4, :]
    for jj in range(3):
        sh = 3 - jj
        xs = jnp.concatenate([jnp.zeros((sh, x.shape[1]), f32), x[:t - sh, :]], axis=0)
        y = y + xs * w[jj:jj + 1, :]
    y = jax.nn.silu(y)
    yn = y * lax.rsqrt(jnp.sum(y * y, axis=-1, keepdims=True) + EPS)
    return (jnp.where(j < 2 * GDN_H, yn, y),)


def _head_rms(o, nh):
    outs = []
    for h in range(nh):
        oh = o[:, HD * h:HD * (h + 1)]
        outs.append(oh * lax.rsqrt(jnp.mean(oh * oh, axis=-1, keepdims=True) + EPS))
    return jnp.concatenate(outs, axis=1)


def _f_post0(fox_o, o, gg, on):
    return (jnp.concatenate([fox_o, _head_rms(o, GDN_H) * on * jax.nn.silu(gg)], axis=1),)


def _f_post1(o, zg, on):
    return (_head_rms(o, HG_H) * on * jax.nn.silu(zg),)


def _f_hpre(zqf, lbl):
    lb = jax.nn.sigmoid(lbl[1:2, :] - lbl[0:1, :])
    fg = lb + (1.0 - lb) * jax.nn.sigmoid(zqf[:, D:])
    return jax.nn.silu(zqf[:, :D]), 1.0 - fg, _chunk_cumsum(jnp.log(fg))


def _dotb(a, b, dims=NN):
    return _dot(a.astype(bf16), b.astype(bf16), dims)


def _dot3(a, b):
    ah, bh = a.astype(bf16), b.astype(bf16)
    al, bl = (a - ah.astype(f32)).astype(bf16), (b - bh.astype(f32)).astype(bf16)
    return _dot(ah, bh) + (_dot(ah, bl) + _dot(al, bh))


def _split(t, nh):
    return [t[CH * ck:CH * (ck + 1), HD * h:HD * (h + 1)] for ck in range(t.shape[0] // CH) for h in range(nh)]


def _merge(units, nh):
    return jnp.concatenate([jnp.concatenate(units[i:i + nh], axis=1) for i in range(0, len(units), nh)], axis=0)


def _inv_impl(amats):
    n = amats[0].shape[0]
    eye = jnp.where(_iota2((n, n), 0) == _iota2((n, n), 1), 1.0, 0.0).astype(f32)
    xs, ps = [eye - a for a in amats], list(amats)
    for _ in range(max(1, (n - 1).bit_length()) - 1):
        ps = [_dotb(p, p) for p in ps]
        xs = [x + _dotb(x, p) for x, p in zip(xs, ps)]
    for _ in range(3):
        rs = [eye - x - _dot3(a, x) for a, x in zip(amats, xs)]
        xs = [x + _dotb(x, r) for x, r in zip(xs, rs)]
    return tuple(xs)


@jax.custom_vjp
def _inv_unit_lower(amats):
    return _inv_impl(amats)


def _inv_fwd(amats):
    xs = _inv_impl(amats)
    return xs, xs


def _inv_bwd(xs, dxs):
    return (tuple(-_dotb(_dotb(x, dx, TN), x, NT) for x, dx in zip(xs, dxs)),)


_inv_unit_lower.defvjp(_inv_fwd, _inv_bwd)


@jax.custom_vjp
def _inv_given(amats, xs):
    return xs


def _inv_given_fwd(amats, xs):
    return xs, xs


def _inv_given_bwd(xs, dxs):
    return _inv_bwd(xs, dxs)[0], tuple(jnp.zeros_like(x) for x in xs)


_inv_given.defvjp(_inv_given_fwd, _inv_given_bwd)


def _f_gdn_intra(q, k, v, bb, gb, tinv_p=None):
    qs, ks, vs, bs, gs = (_split(t, GDN_H) for t in (q, k, v, bb, gb))
    r, cc = _iota2((CH, CH), 0), _iota2((CH, CH), 1)
    causal, strict = r >= cc, r > cc
    beta, g, gl = [b[:, :1] for b in bs], [x[:, :1] for x in gs], [x[CH - 1:CH, :1] for x in gs]
    decay = [jnp.exp(jnp.where(causal, x[:, :CH] - x[:, :CH].T, -jnp.inf)) for x in gs]
    kb = [ki * bi for ki, bi in zip(ks, beta)]
    amat = [jnp.where(strict, _dotb(kbi, ki, NT) * di, 0.0) for kbi, ki, di in zip(kb, ks, decay)]
    if tinv_p is None:
        tinv = _inv_unit_lower(tuple(amat))
    else:
        tinv = _inv_given(tuple(amat), tuple(x[:, :CH] for x in _split(tinv_p, GDN_H)))
    rhs = [jnp.concatenate([vi * bi, kbi * jnp.exp(gi)], axis=1) for vi, bi, kbi, gi in zip(vs, beta, kb, g)]
    uw = [_dotb(ti, ri) for ti, ri in zip(tinv, rhs)]
    qsc = [qi * (HD ** -0.5) for qi in qs]
    qk = [jnp.where(causal, _dotb(qi, ki, NT) * di, 0.0) for qi, ki, di in zip(qsc, ks, decay)]
    outs = ([x[:, :HD] for x in uw], [x[:, HD:] for x in uw],
            [jnp.concatenate([x, jnp.zeros_like(x)], axis=1) for x in qk],
            [qi * jnp.exp(gi) for qi, gi in zip(qsc, g)],
            [ki * jnp.exp(gli - gi) for ki, gli, gi in zip(ks, gl, g)],
            [jnp.broadcast_to(gli, (CH, HD)) for gli in gl])
    if tinv_p is None:
        outs += ([jnp.concatenate([x, jnp.zeros_like(x)], axis=1) for x in tinv],)
    return tuple(_merge(o, GDN_H) for o in outs)


def _f_gdn_gated(q, k, v, zs, eb, ea, alog_b, dt_b):
    return _f_gdn_intra(q, k, v, *_f_gdngate(zs, eb, ea, alog_b, dt_b))


def _f_gdn_gated_given(q, k, v, zs, tinv_p, eb, ea, alog_b, dt_b):
    return _f_gdn_intra(q, k, v, *_f_gdngate(zs, eb, ea, alog_b, dt_b), tinv_p=tinv_p)


def _f_gdn_inter(u, w, qkp, qd, kd, glb, st):
    us, ws, qks, qds, kds, gls = (_split(t, GDN_H) for t in (u, w, qkp, qd, kd, glb))
    sts = [st[HD * i:HD * (i + 1), :] for i in range(len(us))]
    vn = [ui - _dotb(wi, si) for ui, wi, si in zip(us, ws, sts)]
    o = [_dotb(qi, si) + _dotb(xi[:, :CH], vi) for qi, si, xi, vi in zip(qds, sts, qks, vn)]
    s2 = [si * jnp.exp(gi[:1, :1]) + _dotb(ki, vi, TN) for si, gi, ki, vi in zip(sts, gls, kds, vn)]
    return _merge(o, GDN_H), jnp.concatenate(s2, axis=0)


def _f_gdn_full(u, w, qkp, qd, kd, glb, fox_o, gg, on, st):
    o, s2 = _f_gdn_inter(u, w, qkp, qd, kd, glb, st)
    return _f_post0(fox_o, o, gg, on)[0], s2


def _f_hgrn_chunk(q, k, b, v, st):
    qs, ks, bs, vs = (_split(t, HG_H) for t in (q, k, b, v))
    sts = [st[HD * i:HD * (i + 1), :] for i in range(len(qs))]
    causal = _iota2((CH, CH), 0) >= _iota2((CH, CH), 1)
    bl, bm = [x[CH - 1:CH, :] for x in bs], [x[CH // 2 - 1:CH // 2, :] for x in bs]
    a = [jnp.where(causal, _dotb(qi * jnp.exp(bi - mi), ki * jnp.exp(mi - bi), NT), 0.0)
         for qi, ki, bi, mi in zip(qs, ks, bs, bm)]
    o = [_dotb(qi * jnp.exp(bi), si, NT) + _dotb(ai, vi) for qi, bi, si, ai, vi in zip(qs, bs, sts, a, vs)]
    s2 = [si * jnp.exp(li) + _dotb(vi, ki * jnp.exp(li - bi), TN) for si, li, vi, ki, bi in zip(sts, bl, vs, ks, bs)]
    return _merge(o, HG_H), jnp.concatenate(s2, axis=0)


def _f_hgrn_full(z, lbl, on, st):
    o, s2 = _f_hgrn_chunk(*_f_hpre(z[:, :2 * D], lbl), z[:, 2 * D:3 * D], st)
    return _f_post1(o, z[:, 3 * D:], on)[0], s2


def _fox_gate_fwd(z0, fbias, t, tc=256):
    n = z0.shape[0]
    nt = t // tc

    def body(zs_ref, b_ref, ccol_ref, crow_ref, carry):
        @pl.when(pl.program_id(1) == 0)
        def _():
            carry[...] = jnp.zeros_like(carry)

        ls = jnp.where(_iota2((tc, LANES), 1) < FOX_H, jax.nn.log_sigmoid(zs_ref[...] + b_ref[...]), 0.0)
        tri = (_iota2((tc, tc), 0) >= _iota2((tc, tc), 1)).astype(f32)
        c = _dot_sel(tri, ls, exact_lhs=True) + carry[...]
        carry[...] = c[tc - 1:tc, :]
        ccol_ref[...] = c
        crow_ref[0] = c.T[:FOX_H, :]

    return pl.pallas_call(
        body, name="fox_gate_fwd", grid=(n // t, nt),
        in_specs=[pl.BlockSpec((tc, LANES), lambda b, i: (b * nt + i, Z0_SMALL)), pl.BlockSpec((1, LANES), lambda b, i: (0, 0))],
        out_specs=[pl.BlockSpec((tc, LANES), lambda b, i: (b * nt + i, 0)), pl.BlockSpec((1, FOX_H, tc), lambda b, i: (b, 0, i))],
        out_shape=[S((n, LANES), f32), S((n // t, FOX_H, t), f32)],
        scratch_shapes=[pltpu.VMEM((1, LANES), f32)],
        compiler_params=pltpu.CompilerParams(dimension_semantics=("parallel", "arbitrary")),
    )(z0, fbias)


def _fox_gate_bwd(z0, fbias, dcq, dck, t, tc=256):
    n = z0.shape[0]
    nt = t // tc

    def body(zs_ref, b_ref, dcq_ref, dck_ref, dz_ref, db_ref, carry):
        first = jnp.logical_and(pl.program_id(0) == 0, pl.program_id(1) == 0)

        @pl.when(pl.program_id(1) == 0)
        def _():
            carry[...] = jnp.zeros_like(carry)

        @pl.when(first)
        def _():
            db_ref[...] = jnp.zeros_like(db_ref)

        dc = dcq_ref[0] + dcq_ref[1] + dcq_ref[2] + dcq_ref[3]
        drow = dck_ref[0, 0] + dck_ref[1, 0] + dck_ref[2, 0] + dck_ref[3, 0]
        eye = (_iota2((FOX_H, LANES), 0) == _iota2((FOX_H, LANES), 1)).astype(f32)
        dc = dc + _dot_sel(drow, eye, TN)
        triu = (_iota2((tc, tc), 0) <= _iota2((tc, tc), 1)).astype(f32)
        dls = _dot_sel(triu, dc, exact_lhs=True) + carry[...]
        carry[...] = dls[0:1, :]
        x = zs_ref[...] + b_ref[...]
        dz = jnp.where(_iota2((tc, LANES), 1) < FOX_H, dls * jax.nn.sigmoid(-x), 0.0)
        dz_ref[...] = dz
        db_ref[...] += jnp.sum(dz, axis=0, keepdims=True)

    def rev(b, i):
        return b * nt + (nt - 1 - i)

    return pl.pallas_call(
        body, name="fox_gate_bwd", grid=(n // t, nt),
        in_specs=[pl.BlockSpec((tc, LANES), lambda b, i: (rev(b, i), Z0_SMALL)), pl.BlockSpec((1, LANES), lambda b, i: (0, 0)),
                  pl.BlockSpec((4, tc, LANES), lambda b, i: (0, rev(b, i), 0)),
                  pl.BlockSpec((4, 1, FOX_H, tc), lambda b, i: (0, b, 0, nt - 1 - i))],
        out_specs=[pl.BlockSpec((tc, LANES), lambda b, i: (rev(b, i), 0)), pl.BlockSpec((1, LANES), lambda b, i: (0, 0))],
        out_shape=[S((n, LANES), f32), S((1, LANES), f32)],
        scratch_shapes=[pltpu.VMEM((1, LANES), f32)],
        compiler_params=pltpu.CompilerParams(dimension_semantics=("arbitrary", "arbitrary")),
    )(z0, fbias, dcq, dck)


def _fox_scores(hh, p, i, tq, q, k, ccol, crow):
    kmax = k.shape[0]
    lane = _iota2((1, LANES), 1)
    mh = (lane // FOX_D) == hh
    h = 2 * p + hh
    qh = jnp.where(mh, q, 0.0).astype(bf16)
    s = _dot(qh, k, NT) * (FOX_D ** -0.5)
    cq = jnp.sum(jnp.where(lane == h, ccol, 0.0), axis=1, keepdims=True)
    ck = jnp.sum(jnp.where(_iota2((FOX_H, 1), 0) == h, crow, 0.0), axis=0, keepdims=True)
    causal = _iota2((1, kmax), 1) <= (i * tq + _iota2((tq, 1), 0))
    s = jnp.where(causal, s + cq - ck, -jnp.inf)
    pe = jnp.exp(s - jnp.max(s, axis=1, keepdims=True))
    return mh, qh, pe, jnp.sum(pe, axis=1, keepdims=True)


def _fox_attn_fwd(qn, kn, z0, ccol, crow, t, tq=256, side=None):
    n = qn.shape[0]
    nq = t // tq

    def body(q_ref, k_ref, v_ref, ccol_ref, crow_ref, o_ref):
        p = pl.program_id(1)
        k, v, crow = k_ref[...].astype(bf16), v_ref[...].astype(bf16), crow_ref[0]
        for i in range(nq):
            rows, kmax = pl.ds(i * tq, tq), (i + 1) * tq
            q, cc = q_ref[rows, :], ccol_ref[rows, :]
            acc = jnp.zeros((tq, LANES), f32)
            for hh in range(2):
                mh, _, pe, l = _fox_scores(hh, p, i, tq, q, k[:kmax], cc, crow[:, :kmax])
                acc = jnp.where(mh, _dot(pe.astype(bf16), v[:kmax]) / l, acc)
            o_ref[rows, :] = acc

    seq = lambda b, p: (b, p)
    return _pcall(
        side, body, name="fox_attn_fwd", grid=(n // t, FOX_H // 2),
        in_specs=[pl.BlockSpec((t, LANES), seq), pl.BlockSpec((t, LANES), seq), pl.BlockSpec((t, LANES), lambda b, p: (b, Z0_FV + p)),
                  pl.BlockSpec((t, LANES), lambda b, p: (b, 0)), pl.BlockSpec((1, FOX_H, t), lambda b, p: (b, 0, 0))],
        out_specs=pl.BlockSpec((t, LANES), seq),
        out_shape=S((n, FOX_W), f32),
        compiler_params=pltpu.CompilerParams(dimension_semantics=("parallel", "parallel")),
    )(qn, kn, z0, ccol, crow)


def _fox_attn_bwd(qn, kn, z0, ccol, crow, do, t, tq=256, side=None):
    n = qn.shape[0]
    nq = t // tq
    nb = n // t

    def body(q_ref, k_ref, v_ref, ccol_ref, crow_ref, do_ref, dq_ref, dk_ref, dv_ref, dcq_ref, dck_ref):
        p = pl.program_id(1)
        dk_ref[...] = jnp.zeros_like(dk_ref)
        dv_ref[...] = jnp.zeros_like(dv_ref)
        dck_ref[...] = jnp.zeros_like(dck_ref)
        kf, v, crow = k_ref[...], v_ref[...].astype(bf16), crow_ref[0]
        k = kf.astype(bf16)
        lane = _iota2((1, LANES), 1)
        sub = _iota2((FOX_H, 1), 0)
        scale = FOX_D ** -0.5
        for i in range(nq):
            rows, kmax = pl.ds(i * tq, tq), (i + 1) * tq
            q, cc, dout = q_ref[rows, :], ccol_ref[rows, :], do_ref[rows, :]
            dq = jnp.zeros((tq, LANES), f32)
            dcq = jnp.zeros((tq, LANES), f32)
            for hh in range(2):
                mh, qh, pe, l = _fox_scores(hh, p, i, tq, q, k[:kmax], cc, crow[:, :kmax])
                pr = pe / l
                doh = jnp.where(mh, dout, 0.0).astype(bf16)
                dp = _dot(doh, v[:kmax], NT)
                ds = pr * (dp - jnp.sum(pr * dp, axis=1, keepdims=True))
                dsb = ds.astype(bf16)
                dq = dq + _dot(dsb, jnp.where(mh, kf[:kmax], 0.0).astype(bf16)) * scale
                dk_ref[:kmax, :] += _dot(dsb, qh, TN) * scale
                dv_ref[:kmax, :] += _dot(pr.astype(bf16), doh, TN)
                h = 2 * p + hh
                dcq = dcq + jnp.where(lane == h, jnp.sum(ds, axis=1, keepdims=True), 0.0)
                dck_ref[0, 0, :, :kmax] += jnp.where(sub == h, -jnp.sum(ds, axis=0, keepdims=True), 0.0)
            dq_ref[rows, :] = dq
            dcq_ref[0, rows, :] = dcq

    seq = lambda b, p: (b, p)
    return _pcall(
        side, body, name="fox_attn_bwd", grid=(nb, FOX_H // 2),
        in_specs=[pl.BlockSpec((t, LANES), seq), pl.BlockSpec((t, LANES), seq), pl.BlockSpec((t, LANES), lambda b, p: (b, Z0_FV + p)),
                  pl.BlockSpec((t, LANES), lambda b, p: (b, 0)), pl.BlockSpec((1, FOX_H, t), lambda b, p: (b, 0, 0)),
                  pl.BlockSpec((t, LANES), seq)],
        out_specs=[pl.BlockSpec((t, LANES), seq), pl.BlockSpec((t, LANES), seq), pl.BlockSpec((t, LANES), seq),
                   pl.BlockSpec((1, t, LANES), lambda b, p: (p, b, 0)), pl.BlockSpec((1, 1, FOX_H, t), lambda b, p: (p, b, 0, 0))],
        out_shape=[S((n, FOX_W), f32), S((n, FOX_W), f32), S((n, FOX_W), f32), S((4, n, LANES), f32), S((4, nb, FOX_H, t), f32)],
        compiler_params=pltpu.CompilerParams(dimension_semantics=("parallel", "parallel")),
    )(qn, kn, z0, ccol, crow, do)


def _adamw_math(w, g, m, v):
    m = ADAM_B1 * m + (1.0 - ADAM_B1) * g
    v = ADAM_B2 * v + (1.0 - ADAM_B2) * (g * g)
    m_hat = m / (1.0 - ADAM_B1 ** ADAM_STEP)
    v_hat = v / (1.0 - ADAM_B2 ** ADAM_STEP)
    return -ADAM_LR * (m_hat / (jnp.sqrt(v_hat) + ADAM_EPS) + ADAM_WD * w), m, v


def _adamw_big(name, idx, gmine, recv1, recv2, w, m, v):
    r, wc = w.shape
    c = gmine.shape[2]
    tr = min(r, 256)

    def body(idx_ref, gm_ref, r1_ref, r2_ref, w_ref, m_ref, v_ref, g_ref, d_ref, nm_ref, nv_ref):
        g = gm_ref[0].astype(f32) + r1_ref[0].astype(f32)
        for k in range(3):
            g = g + r2_ref[k].astype(f32)
        g = g[:, :wc]
        d, nm, nv = _adamw_math(w_ref[...], g, m_ref[...], v_ref[...])
        g_ref[...] = g
        d_ref[...] = d
        nm_ref[...] = nm
        nv_ref[...] = nv

    row = pl.BlockSpec((tr, wc), lambda i, s: (i, 0))
    return pl.pallas_call(
        body, name=name,
        grid_spec=pltpu.PrefetchScalarGridSpec(
            num_scalar_prefetch=1, grid=(r // tr,),
            in_specs=[pl.BlockSpec((1, tr, c), lambda i, s: (s[0], i, 0)), pl.BlockSpec((1, tr, c), lambda i, s: (s[1], i, 0)),
                      pl.BlockSpec((3, tr, c), lambda i, s: (0, i, 0)), row, row, row],
            out_specs=[row, row, row, row]),
        out_shape=[S((r, wc), f32)] * 4,
        compiler_params=pltpu.CompilerParams(dimension_semantics=("parallel",)),
    )(idx, gmine, recv1, recv2, w, m, v)


def _pair_sum(name, idx, gmine, recv1):
    _, r, c = gmine.shape
    g4 = gmine.reshape(4, 2, r, c)

    def body(idx_ref, gm_ref, r1_ref, o_ref):
        o_ref[0] = (gm_ref[0, 0].astype(f32) + r1_ref[0].astype(f32)).astype(bf16)

    return pl.pallas_call(
        body, name=name,
        grid_spec=pltpu.PrefetchScalarGridSpec(
            num_scalar_prefetch=1, grid=(4,),
            in_specs=[pl.BlockSpec((1, 1, r, c), lambda ch, s: (ch, s[0], 0, 0)), pl.BlockSpec((1, r, c), lambda ch, s: (ch, 0, 0))],
            out_specs=pl.BlockSpec((1, r, c), lambda ch, s: (ch, 0, 0))),
        out_shape=S((4, r, c), bf16),
        compiler_params=pltpu.CompilerParams(dimension_semantics=("parallel",)),
    )(idx, g4, recv1)


def _adamw_small(parts, w, m, v, own_mask, own_g):
    def body(p_ref, w_ref, m_ref, v_ref, mask_ref, og_ref, g_ref, d_ref, nm_ref, nv_ref):
        g = p_ref[0]
        for k in range(1, NDEV):
            g = g + p_ref[k]
        g_ref[...] = g
        ge = jnp.where(mask_ref[...] > 0.5, og_ref[...], g)
        d, nm, nv = _adamw_math(w_ref[...], ge, m_ref[...], v_ref[...])
        d_ref[...] = d
        nm_ref[...] = nm
        nv_ref[...] = nv

    return pl.pallas_call(body, name="adamw_small", out_shape=[S(w.shape, f32)] * 4)(parts, w, m, v, own_mask, own_g)


def _sum_parts(parts):
    def body(p_ref, g_ref):
        g = p_ref[0]
        for k in range(1, NDEV):
            g = g + p_ref[k]
        g_ref[...] = g

    return pl.pallas_call(body, name="sum_parts", out_shape=S(parts.shape[1:], f32))(parts)


def _me():
    return lax.axis_index("x"), lax.axis_index("y"), lax.axis_index("c")


def _hbm_specs(n):
    return [pl.BlockSpec(memory_space=pl.ANY)] * n


def _allgather(name, xs):
    na = len(xs)

    def body(*refs):
        x_refs, out_refs = refs[:na], refs[na:2 * na]
        send_sems, recv_sems, local_sems = refs[2 * na:]
        mx, my, mc = _me()
        me, sib = (mx, my, mc), (mx, my, 1 - mc)
        chips = [(1 - mx, my), (mx, 1 - my), (1 - mx, 1 - my)]

        def slab(a, px, py, pc):
            return out_refs[a].at[4 * px + 2 * py + pc]

        def copy(a, k, block, to, own=False):
            return pltpu.make_async_remote_copy(
                src_ref=x_refs[a] if own else slab(a, *block), dst_ref=slab(a, *block),
                send_sem=send_sems.at[7 * a + k], recv_sem=recv_sems.at[7 * a + k], device_id=to, device_id_type=MESH)

        mine = [pltpu.make_async_copy(x_refs[a], slab(a, *me), local_sems.at[a]) for a in range(na)]
        first = []
        for a in range(na):
            mine[a].start()
            first += [copy(a, 0, me, sib, own=True)] + [copy(a, 1 + j, me, (*chip, mc), own=True) for j, chip in enumerate(chips)]
        for cp in first:
            cp.start()
        passed = []
        for j, chip in enumerate(chips):
            for a in range(na):
                copy(a, 1 + j, (*chip, mc), me).wait_recv()
                passed.append(copy(a, 4 + j, (*chip, mc), sib))
                passed[-1].start()
        for a in range(na):
            copy(a, 0, sib, me).wait_recv()
            for j, chip in enumerate(chips):
                copy(a, 4 + j, (*chip, 1 - mc), me).wait_recv()
        for cp in first + passed:
            cp.wait_send()
        for cp in mine:
            cp.wait()

    return pl.pallas_call(
        body, name=name, out_shape=[S((NDEV,) + x.shape, x.dtype) for x in xs],
        in_specs=_hbm_specs(na), out_specs=_hbm_specs(na),
        scratch_shapes=[pltpu.SemaphoreType.DMA((7 * na,)), pltpu.SemaphoreType.DMA((7 * na,)), pltpu.SemaphoreType.DMA((na,))],
    )(*xs)


def _rs_sibling(gs):
    na = len(gs)

    def body(*refs):
        g_refs, out_refs, send_sems, recv_sems = refs[:na], refs[na:2 * na], refs[2 * na], refs[2 * na + 1]
        mx, my, mc = _me()
        cps = [pltpu.make_async_remote_copy(
            src_ref=g_refs[a].at[2 * ch + 1 - mc], dst_ref=out_refs[a].at[ch], send_sem=send_sems.at[4 * a + ch],
            recv_sem=recv_sems.at[4 * a + ch], device_id=(mx, my, 1 - mc), device_id_type=MESH)
            for a in range(na) for ch in range(4)]
        for cp in cps:
            cp.start()
        for cp in cps:
            cp.wait_recv()
        for cp in cps:
            cp.wait_send()

    return pl.pallas_call(
        body, name="rs_sibling", out_shape=[S((4,) + g.shape[1:], g.dtype) for g in gs],
        in_specs=_hbm_specs(na), out_specs=_hbm_specs(na),
        scratch_shapes=[pltpu.SemaphoreType.DMA((4 * na,)), pltpu.SemaphoreType.DMA((4 * na,))],
    )(*gs)


def _side_allgather(xs):
    na = len(xs)

    def mk(x_refs, out_refs, sems):
        send_sems, recv_sems, local_sems = sems
        mx, my, mc = _me()
        me, sib = (mx, my, mc), (mx, my, 1 - mc)
        chips = [(1 - mx, my), (mx, 1 - my), (1 - mx, 1 - my)]

        def slab(a, px, py, pc):
            return out_refs[a].at[4 * px + 2 * py + pc]

        def copy(a, k, block, to, own=False):
            return pltpu.make_async_remote_copy(
                src_ref=x_refs[a] if own else slab(a, *block), dst_ref=slab(a, *block),
                send_sem=send_sems.at[7 * a + k], recv_sem=recv_sems.at[7 * a + k], device_id=to, device_id_type=MESH)

        mine = [pltpu.make_async_copy(x_refs[a], slab(a, *me), local_sems.at[a]) for a in range(na)]
        first = []
        for a in range(na):
            first += [copy(a, 0, me, sib, own=True)] + [copy(a, 1 + j, me, (*chip, mc), own=True) for j, chip in enumerate(chips)]
        return me, sib, chips, mc, copy, mine, first

    def start(x_refs, out_refs, sems):
        *_, mine, first = mk(x_refs, out_refs, sems)
        for cp in mine + first:
            cp.start()

    def finish(x_refs, out_refs, sems):
        me, sib, chips, mc, copy, mine, first = mk(x_refs, out_refs, sems)
        passed = []
        for j, chip in enumerate(chips):
            for a in range(na):
                copy(a, 1 + j, (*chip, mc), me).wait_recv()
                passed.append(copy(a, 4 + j, (*chip, mc), sib))
                passed[-1].start()
        for a in range(na):
            copy(a, 0, sib, me).wait_recv()
            for j, chip in enumerate(chips):
                copy(a, 4 + j, (*chip, 1 - mc), me).wait_recv()
        for cp in first + passed:
            cp.wait_send()
        for cp in mine:
            cp.wait()

    scratch = [pltpu.SemaphoreType.DMA((7 * na,)), pltpu.SemaphoreType.DMA((7 * na,)), pltpu.SemaphoreType.DMA((na,))]
    return _Side(xs, [S((NDEV,) + x.shape, x.dtype) for x in xs], scratch, start, finish)


def _side_exchange(arrs, nslot, out_slots, route):
    na = len(arrs)

    def copies(in_refs, out_refs, sems):
        send_sems, recv_sems = sems
        return [pltpu.make_async_remote_copy(
            src_ref=in_refs[a].at[src], dst_ref=out_refs[a].at[k], send_sem=send_sems.at[nslot * a + k],
            recv_sem=recv_sems.at[nslot * a + k], device_id=to, device_id_type=MESH)
            for a in range(na) for k, (src, to) in enumerate(route(*_me()))]

    def start(in_refs, out_refs, sems):
        for cp in copies(in_refs, out_refs, sems):
            cp.start()

    def finish(in_refs, out_refs, sems):
        cps = copies(in_refs, out_refs, sems)
        for cp in cps:
            cp.wait_recv()
        for cp in cps:
            cp.wait_send()

    scratch = [pltpu.SemaphoreType.DMA((nslot * na,)), pltpu.SemaphoreType.DMA((nslot * na,))]
    return _Side(arrs, [S((out_slots,) + x.shape[1:], x.dtype) for x in arrs], scratch, start, finish)


def _side_rs_sibling(gs):
    return _side_exchange(gs, 4, 4, lambda mx, my, mc: [(2 * ch + 1 - mc, (mx, my, 1 - mc)) for ch in range(4)])


def _side_rs_chips(ps):
    return _side_exchange(ps, 3, 3, lambda mx, my, mc: [(2 * cx + cy, (cx, cy, mc)) for cx, cy in
                                                        [(1 - mx, my), (mx, 1 - my), (1 - mx, 1 - my)]])


_COLW = 512
_COL_NAMES = ("l0_w_in", "l0_w_ff1", "l1_w_in", "l1_w_ff1")
_ROW_NAMES = ("l0_w_out", "l0_w_ff2", "l1_w_out", "l1_w_ff2")
_BIG_NAMES = _COL_NAMES + _ROW_NAMES


def _full_weight(gathered, name, ncols):
    if name in _ROW_NAMES:
        return gathered.reshape(-1, D)
    return gathered[:, :, :ncols].transpose(1, 0, 2).reshape(D, NDEV * ncols)


def _regroup_w_in0(w):
    main = jnp.concatenate([w[:, 0:1536], w[:, 1544:3080], w[:, 3088:3600]], axis=1)
    small = jnp.concatenate([w[:, 1536:1544], w[:, 3080:3088]], axis=1)
    return jnp.concatenate([main, small, jnp.zeros((D, ZW0 - 3584 - 16), w.dtype)], axis=1)


def _ungroup_w_in0(g):
    return jnp.concatenate([g[:, 0:1536], g[:, 3584:3592], g[:, 1536:3072], g[:, 3592:3600], g[:, 3072:3584]], axis=1)


def _col_slabs(g, ncols):
    g = g.reshape(D, NDEV, ncols).transpose(1, 0, 2)
    return jnp.pad(g, ((0, 0), (0, 0), (0, _COLW - ncols)))


def _sq(t):
    return t * t


def _epi_res_norm(acc, res, gain):
    y = acc + res
    h = _f_norm(y, gain)[0]
    return y, h, h


_RES_NORM_OUTS = [("tile", f32), ("tile", bf16), ("tile_t", bf16)]


def _epi_norm_bwd(acc, x, dres, gain):
    _, vjp = jax.vjp(lambda xx, gg: _f_norm(xx, gg)[0], x, gain)
    dx, dgain = vjp(acc)
    dx = dx + dres
    return dx, dgain, dx


_NORM_BWD_OUTS = [("tile", f32), ("rows", f32), ("tile_t", bf16)]


def _epi_loss(acc, res, tgt):
    e = acc + res - tgt
    dy = e * (1.0 / D)
    return dy, dy, jnp.sum(e * e, axis=0, keepdims=True)


_LOSS_OUTS = [("tile", f32), ("tile_t", bf16), ("rows", f32)]


def _mlp_fwd(tag, x, h, w1, w2, epi, extra, consts, outs):
    a = _mm(f"{tag}_ff1", h, w1, NN, bf16, epi=lambda acc: jnp.maximum(acc, 0.0))
    return _mm(f"{tag}_ff2", a, w2, NN, f32, a_fn=_sq, epi=epi, extra=(x,) + tuple(extra), consts=consts, outs=outs), a


def _mlp_bwd(tag, x, gain, w1, w2, ht, a, dy, dyt, side=None):
    da = _mm(f"{tag}_ff2_dx", dy, w2, NT, bf16, epi=lambda acc, av: acc * 2.0 * av.astype(f32), extra=(a,), side=side)
    if side is not None:
        da, side_res = da
    dw2 = _mm(f"{tag}_ff2_dw", dyt, a, NN, bf16, b_fn=_sq, out_t=True)
    res = _mm(f"{tag}_ff1_dx", da, w1, NT, f32, epi=_epi_norm_bwd, extra=(x, dy), consts=(gain,), outs=_NORM_BWD_OUTS)
    dw1 = _mm(f"{tag}_ff1_dw", ht, da, NN, bf16, slab=True)
    return (res, dw1, dw2) if side is None else ((res, dw1, dw2), side_res)


def _row(v):
    return v.reshape(1, -1).astype(f32)


_L0_REST = ("l0_w_ff1", "l0_w_out")
_L1_MIX = ("l0_w_ff2", "l1_w_in", "l1_w_out")
_L1_FFN = ("l1_w_ff1", "l1_w_ff2")
_GRAD_A = ("l1_w_ff1", "l1_w_ff2", "l1_w_out", "l1_w_in")
_GRAD_B = ("l0_w_ff1", "l0_w_ff2", "l0_w_out")
_IN_PLACE = ("l0_w_ff1", "l1_w_in", "l1_w_ff1")


def _train(x, tgt, args, mom, var, t):
    n = x.shape[0]
    nchunk = t // CH
    mx, my, mc = _me()
    dev, chip = 4 * mx + 2 * my + mc, 2 * mx + my
    core_idx = jnp.reshape(mc, (1,)).astype(jnp.int32)
    own_idx = jnp.stack([dev, chip]).astype(jnp.int32)
    ncols = {nm: args[nm].shape[1] for nm in _COL_NAMES}
    conv_cols = args["l0_gdn_conv"].shape[1]
    g, big, w = {}, {}, {}

    def send(nm):
        a = args[nm].astype(bf16)
        return jnp.pad(a, ((0, 0), (0, _COLW - ncols[nm]))) if nm in _COL_NAMES else a

    def take(names, gathered):
        for nm, arr in zip(names, gathered):
            w[nm] = arr if nm in _IN_PLACE else _full_weight(arr, nm, ncols.get(nm, 0))

    def by_dev(nm, ga):
        return ga if nm in _COL_NAMES else ga.reshape(NDEV, -1, D)

    def pair(names, gs, r1s):
        return [_pair_sum(f"rs_pair_sum_{nm}", core_idx, ga, r1) for nm, ga, r1 in zip(names, gs, r1s)]

    def adam(names, gs, r1s, r2s):
        for nm, ga, r1, r2 in zip(names, gs, r1s, r2s):
            big[nm] = _adamw_big(f"adamw_{nm}", own_idx, ga, r1, r2, args[nm], mom[nm], var[nm])


    li = jnp.arange(FOX_W)
    pm = jnp.where((li[:, None] // FOX_D) == (li[None, :] // FOX_D), 1.0 / FOX_D, 0.0).astype(f32)
    lane_head = jnp.arange(GDN_W) // HD
    sel = lambda first_lane: (jnp.arange(LANES)[:, None] == (first_lane + lane_head)[None, :]).astype(f32)
    e_beta, e_alpha = sel(FOX_H), sel(FOX_H + GDN_H)
    alog_b, dt_b = _row(jnp.repeat(args["l0_gdn_A_log"], HD)), _row(jnp.repeat(args["l0_gdn_dt_bias"], HD))
    gq_t, gk_t = _row(jnp.tile(args["l0_fox_q_norm"], FOX_H)), _row(jnp.tile(args["l0_fox_k_norm"], FOX_H))
    on0_t, on1_t = _row(jnp.tile(args["l0_gdn_o_norm"], GDN_H)), _row(jnp.tile(args["l1_hgrn_o_norm"], HG_H))
    fbias = jnp.pad(_row(args["l0_fox_f_bias"]), ((0, 0), (0, LANES - FOX_H)))
    g0m, g0f, g1m, g1f = (_row(args[k]) for k in ("l0_mix_norm", "l0_ffn_norm", "l1_mix_norm", "l1_ffn_norm"))
    lbl = args["hgrn_lb_logits"].astype(f32)

    first = [send("l0_w_in"), jnp.pad(args["l0_gdn_conv"], ((0, 4), (0, LANES * 2 - conv_cols)))]
    (h0, h0t), first = _tok_fwd("l0_mix_norm", _f_norm, [(x, D, 0)], [(g0m, None, 0)], [(D, bf16)], 256, also_t=(0,),
                                side=_side_allgather(first))
    take(("l0_w_in",), first[:1])
    w_in0 = _regroup_w_in0(w["l0_w_in"])
    wconv = first[1][:, :4, :conv_cols].transpose(1, 0, 2).reshape(4, NDEV * conv_cols)
    z0 = _mm("l0_in", h0, w_in0, NN, f32)
    qk_rows = [(z0, 2 * FOX_W, 0)]
    qk_consts = [(gq_t, None, 0), (gk_t, None, 0), (pm, None, 0)]
    qn, kn = _tok_fwd("fox_pre", _f_foxpre, qk_rows, qk_consts, [(FOX_W, f32)] * 2, 256)
    ccol, crow = _fox_gate_fwd(z0, fbias, t)
    fox_o, got = _fox_attn_fwd(qn, kn, z0, ccol, crow, t, side=_side_allgather([send(nm) for nm in _L0_REST]))
    take(_L0_REST, got)
    conv_rows, conv_consts = [(z0, LANES, Z0_GQKV)], [(wconv, LANES, 0)]
    (qkv,) = _tok_fwd("gdn_conv", _f_conv, conv_rows, conv_consts, [(LANES, f32)], t, ncb=12, with_j=True)
    gate_consts = [(e_beta, None, 0), (e_alpha, None, 0), (alog_b, None, 0), (dt_b, None, 0)]
    intra_rows = [(qkv, GDN_W, 0), (qkv, GDN_W, 1), (qkv, GDN_W, 2), (z0, LANES, Z0_SMALL)]
    intra, got = _tok_fwd("gdn_intra", _f_gdn_gated, intra_rows, gate_consts, [(GDN_W, f32)] * 7, 2 * CH,
                          side=_side_allgather([send(nm) for nm in _L1_MIX]))
    take(_L1_MIX, got)
    inter_rows = [(a, 0, 1) for a in intra[:6]]
    full_rows = inter_rows + [(fox_o, 0, 1), (z0, Z0_GG // 4, 1)]
    (cat0, gdn_hist), got = _scan_fwd("gdn_scan", _f_gdn_full, full_rows, GDN_H, nchunk, consts=(on0_t,), out_dtype=bf16, out_k=2,
                                      side=_side_allgather([send(nm) for nm in _L1_FFN[1:]]))
    take(_L1_FFN[1:], got)
    x1, hf0, hf0t = _mm("l0_out", cat0, w["l0_w_out"], NN, f32, epi=_epi_res_norm, extra=(x,), consts=(g0f,), outs=_RES_NORM_OUTS)
    (x2, h1, h1t), a0 = _mlp_fwd("l0", x1, hf0, w["l0_w_ff1"], w["l0_w_ff2"], _epi_res_norm, (), (g1m,), _RES_NORM_OUTS)

    z1 = _mm("l1_in", h1, w["l1_w_in"], NN, f32)
    (cat1, hg_hist), got = _scan_fwd("hgrn_scan", _f_hgrn_full, [(z1, 0, 4)], HG_H, nchunk, consts=(lbl, on1_t), out_dtype=bf16,
                                     side=_side_allgather([send(nm) for nm in _L1_FFN[:1]]))
    take(_L1_FFN[:1], got)
    x3, hf1, hf1t = _mm("l1_out", cat1, w["l1_w_out"], NN, f32, epi=_epi_res_norm, extra=(x2,), consts=(g1f,), outs=_RES_NORM_OUTS)
    (dy, dyt, loss_row), a1 = _mlp_fwd("l1", x3, hf1, w["l1_w_ff1"], w["l1_w_ff2"], _epi_loss, (tgt,), (), _LOSS_OUTS)

    (dx3, g["l1_ffn_norm"], dx3t), ga_ff1, ga_ff2 = _mlp_bwd("l1", x3, g1f, w["l1_w_ff1"], w["l1_w_ff2"], hf1t, a1, dy, dyt)
    dcat1 = _mm("l1_out_dx", dx3, w["l1_w_out"], NT, f32)
    ga_out = _mm("l1_out_dw", dx3t, cat1, NN, bf16, out_t=True)
    dz1, dlbl, don1 = _scan_bwd("hgrn_scan_bwd", _f_hgrn_full, [(z1, 0, 4)], hg_hist, dcat1, HG_H, nchunk, dtypes=[bf16],
                                consts=(lbl, on1_t))
    dx2, g["l1_mix_norm"], dx2t = _mm("l1_in_dx", dz1, w["l1_w_in"], NT, f32, epi=_epi_norm_bwd, extra=(x2, dx3), consts=(g1m,),
                                      outs=_NORM_BWD_OUTS)
    ga_in = _mm("l1_in_dw", h1t, dz1, NN, bf16, slab=True)
    g["l1_hgrn_o_norm"] = don1.reshape(HG_H, HD).sum(0)
    g["hgrn_lb_logits"] = dlbl
    gs_a = [by_dev(nm, ga) for nm, ga in zip(_GRAD_A, (ga_ff1, ga_ff2, ga_out, ga_in))]

    (dx1, g["l0_ffn_norm"], dx1t), gb_ff1, gb_ff2 = _mlp_bwd("l0", x1, g0f, w["l0_w_ff1"], w["l0_w_ff2"], hf0t, a0, dx2, dx2t)
    dcat0 = _mm("l0_out_dx", dx1, w["l0_w_out"], NT, f32)
    gb_out = _mm("l0_out_dw", dx1t, cat0, NN, bf16, out_t=True)
    gs_b = [by_dev(nm, ga) for nm, ga in zip(_GRAD_B, (gb_ff1, gb_ff2, gb_out))]
    (*dinter, dfox_o, dgg, don0), r1_a = _scan_bwd("gdn_scan_bwd", _f_gdn_full, full_rows, gdn_hist, dcat0, GDN_H, nchunk,
                                                   consts=(on0_t,), out_k=2, dtypes=[f32] * 7 + [bf16], side=_side_rs_sibling(gs_a))
    pairs_a = pair(_GRAD_A, gs_a, r1_a)
    (dqn, dkn, dfv, dcq, dck), got = _fox_attn_bwd(qn, kn, z0, ccol, crow, dfox_o, t,
                                                   side=_join_sides([_side_rs_chips(pairs_a), _side_rs_sibling(gs_b)]))
    r2_a, r1_b = got[:len(_GRAD_A)], got[len(_GRAD_A):]
    adam(_GRAD_A, gs_a, r1_a, r2_a)
    pairs_b = pair(_GRAD_B, gs_b, r1_b)
    (dqkv, dzs_g, dalog_b, ddt_b), r2_b = _tok_bwd(
        "gdn_intra_bwd", _f_gdn_gated_given, intra_rows + [(intra[6], GDN_W, 0)], gate_consts, [(a, GDN_W) for a in dinter],
        2 * CH, [0, 1, 2, 3], [2, 3], ncat=3, side=_side_rs_chips(pairs_b))
    adam(_GRAD_B, gs_b, r1_b, r2_b)
    dgqkv, dwconv = _tok_bwd("gdn_conv_bwd", _f_conv, conv_rows, conv_consts, [(dqkv, LANES)], t, [0], [0], ncb=12, with_j=True,
                             drow_dtype=bf16)
    dzs_f, dfb = _fox_gate_bwd(z0, fbias, dcq, dck, t)
    dzqk, dgq_t, dgk_t = _tok_bwd("fox_pre_bwd", _f_foxpre, qk_rows, qk_consts, [(dqn, FOX_W), (dkn, FOX_W)], 256, [0], [0, 1],
                                  drow_dtype=bf16)
    dz0 = jnp.concatenate([dzqk, dfv.astype(bf16), dgqkv, dgg, (dzs_g + dzs_f).astype(bf16), jnp.zeros((n, ZW0 - 3712), bf16)], axis=1)
    gs_c = [_col_slabs(_ungroup_w_in0(_mm("l0_in_dw", h0t, dz0, NN, bf16)), ncols["l0_w_in"])]
    r1_c = _rs_sibling(gs_c)
    pairs_c = pair(("l0_w_in",), gs_c, r1_c)
    (dx, g["l0_mix_norm"], _), r2_c = _mm("l0_in_dx", dz0, w_in0, NT, f32, epi=_epi_norm_bwd, extra=(x, dx1), consts=(g0m,),
                                          outs=_NORM_BWD_OUTS, side=_side_rs_chips(pairs_c))
    adam(("l0_w_in",), gs_c, r1_c, r2_c)
    g["l0_fox_q_norm"] = dgq_t.reshape(FOX_H, FOX_D).sum(0)
    g["l0_fox_k_norm"] = dgk_t.reshape(FOX_H, FOX_D).sum(0)
    g["l0_fox_f_bias"] = dfb[0, :FOX_H]
    g["l0_gdn_conv"] = dwconv
    g["l0_gdn_A_log"] = dalog_b.reshape(GDN_H, HD).sum(1)
    g["l0_gdn_dt_bias"] = ddt_b.reshape(GDN_H, HD).sum(1)
    g["l0_gdn_o_norm"] = don0.reshape(GDN_H, HD).sum(0)
    return loss_row, dx, g, big


_NAMES = ("l0_mix_norm", "l0_w_in", "l0_fox_q_norm", "l0_fox_k_norm", "l0_fox_f_bias", "l0_gdn_conv", "l0_gdn_A_log",
          "l0_gdn_dt_bias", "l0_gdn_o_norm", "l0_w_out", "l0_ffn_norm", "l0_w_ff1", "l0_w_ff2", "l1_mix_norm", "l1_w_in",
          "l1_hgrn_o_norm", "l1_w_out", "l1_ffn_norm", "l1_w_ff1", "l1_w_ff2", "hgrn_lb_logits")
_SMALL_NAMES = tuple(nm for nm in _NAMES if nm not in _BIG_NAMES)
_SMALL_ROWS = 16


def _pack_small(vals):
    flat = jnp.concatenate([vals[nm].reshape(-1).astype(f32) for nm in _SMALL_NAMES])
    return jnp.pad(flat, (0, _SMALL_ROWS * D - flat.shape[0])).reshape(_SMALL_ROWS, D)


def _unpack_small(packed, shapes):
    flat = packed.reshape(-1)
    out, off = {}, 0
    for nm in _SMALL_NAMES:
        size = 1
        for s in shapes[nm]:
            size *= s
        out[nm] = flat[off:off + size].reshape(shapes[nm])
        off += size
    return out, off


def kernel(x, l0_mix_norm, l0_w_in, l0_fox_q_norm, l0_fox_k_norm, l0_fox_f_bias, l0_gdn_conv, l0_gdn_A_log, l0_gdn_dt_bias, l0_gdn_o_norm, l0_w_out, l0_ffn_norm, l0_w_ff1, l0_w_ff2, l1_mix_norm, l1_w_in, l1_hgrn_o_norm, l1_w_out, l1_ffn_norm, l1_w_ff1, l1_w_ff2, hgrn_lb_logits, loss_target, m_l0_mix_norm, m_l0_w_in, m_l0_fox_q_norm, m_l0_fox_k_norm, m_l0_fox_f_bias, m_l0_gdn_conv, m_l0_gdn_A_log, m_l0_gdn_dt_bias, m_l0_gdn_o_norm, m_l0_w_out, m_l0_ffn_norm, m_l0_w_ff1, m_l0_w_ff2, m_l1_mix_norm, m_l1_w_in, m_l1_hgrn_o_norm, m_l1_w_out, m_l1_ffn_norm, m_l1_w_ff1, m_l1_w_ff2, m_hgrn_lb_logits, v_l0_mix_norm, v_l0_w_in, v_l0_fox_q_norm, v_l0_fox_k_norm, v_l0_fox_f_bias, v_l0_gdn_conv, v_l0_gdn_A_log, v_l0_gdn_dt_bias, v_l0_gdn_o_norm, v_l0_w_out, v_l0_ffn_norm, v_l0_w_ff1, v_l0_w_ff2, v_l1_mix_norm, v_l1_w_in, v_l1_hgrn_o_norm, v_l1_w_out, v_l1_ffn_norm, v_l1_w_ff1, v_l1_w_ff2, v_hgrn_lb_logits):
    args = dict(zip(_NAMES, (l0_mix_norm, l0_w_in, l0_fox_q_norm, l0_fox_k_norm, l0_fox_f_bias, l0_gdn_conv, l0_gdn_A_log, l0_gdn_dt_bias, l0_gdn_o_norm, l0_w_out, l0_ffn_norm, l0_w_ff1, l0_w_ff2, l1_mix_norm, l1_w_in, l1_hgrn_o_norm, l1_w_out, l1_ffn_norm, l1_w_ff1, l1_w_ff2, hgrn_lb_logits)))
    mom = dict(zip(_NAMES, (m_l0_mix_norm, m_l0_w_in, m_l0_fox_q_norm, m_l0_fox_k_norm, m_l0_fox_f_bias, m_l0_gdn_conv, m_l0_gdn_A_log, m_l0_gdn_dt_bias, m_l0_gdn_o_norm, m_l0_w_out, m_l0_ffn_norm, m_l0_w_ff1, m_l0_w_ff2, m_l1_mix_norm, m_l1_w_in, m_l1_hgrn_o_norm, m_l1_w_out, m_l1_ffn_norm, m_l1_w_ff1, m_l1_w_ff2, m_hgrn_lb_logits)))
    var = dict(zip(_NAMES, (v_l0_mix_norm, v_l0_w_in, v_l0_fox_q_norm, v_l0_fox_k_norm, v_l0_fox_f_bias, v_l0_gdn_conv, v_l0_gdn_A_log, v_l0_gdn_dt_bias, v_l0_gdn_o_norm, v_l0_w_out, v_l0_ffn_norm, v_l0_w_ff1, v_l0_w_ff2, v_l1_mix_norm, v_l1_w_in, v_l1_hgrn_o_norm, v_l1_w_out, v_l1_ffn_norm, v_l1_w_ff1, v_l1_w_ff2, v_hgrn_lb_logits)))
    nb, t, _ = x.shape
    dev = 4 * lax.axis_index("x") + 2 * lax.axis_index("y") + lax.axis_index("c")
    conv_cols = l0_gdn_conv.shape[1]
    loss_row, dx, g, big = _train(x.reshape(nb * t, D), loss_target.reshape(nb * t, D), args, mom, var, t)

    shapes = {nm: args[nm].shape for nm in _SMALL_NAMES}
    gsm = dict(g)
    gsm["l0_gdn_conv"] = jnp.zeros(shapes["l0_gdn_conv"], f32)
    packed = _pack_small(gsm)
    _, used = _unpack_small(packed, shapes)
    flat_extra = jnp.concatenate([jnp.sum(loss_row).reshape(1), g["l0_gdn_conv"].reshape(-1)])
    packed = packed.reshape(-1).at[used:used + flat_extra.shape[0]].set(flat_extra).reshape(_SMALL_ROWS, D)
    (parts,) = _allgather("ag_small", [packed])
    total = _sum_parts(parts).reshape(-1)
    loss = 0.5 * total[used] / D
    conv_g_full = total[used + 1:used + 1 + 4 * NDEV * conv_cols].reshape(4, NDEV * conv_cols)
    conv_g = lax.dynamic_slice(conv_g_full, (0, dev * conv_cols), (4, conv_cols))
    own_vals = {nm: jnp.zeros(shapes[nm], f32) for nm in _SMALL_NAMES}
    own_vals["l0_gdn_conv"] = conv_g
    own_mask = {nm: jnp.zeros(shapes[nm], f32) for nm in _SMALL_NAMES}
    own_mask["l0_gdn_conv"] = jnp.ones(shapes["l0_gdn_conv"], f32)
    small = _adamw_small(parts, _pack_small(args), _pack_small(mom), _pack_small(var), _pack_small(own_mask), _pack_small(own_vals))
    small = [_unpack_small(a, shapes)[0] for a in small]
    small[0]["l0_gdn_conv"] = conv_g

    outs = [loss, dx.reshape(nb, t, D)]
    for k in range(4):
        outs += [big[nm][k] if nm in _BIG_NAMES else small[k][nm] for nm in _NAMES]
    return tuple(outs)
```

```python
import functools

import jax
import jax.numpy as jnp
from jax import lax
from jax.experimental import pallas as pl
from jax.experimental.pallas import tpu as pltpu

f32, bf16 = jnp.float32, jnp.bfloat16
NN = (((1,), (0,)), ((), ()))
NT = (((1,), (1,)), ((), ()))
TN = (((0,), (0,)), ((), ()))
MESH = pl.DeviceIdType.MESH
S = jax.ShapeDtypeStruct

EPS = 1e-6
D = 1024
LANES = 128
FOX_H, FOX_D, FOX_W = 8, 64, 512
GDN_H, HD, GDN_W = 4, 128, 512
HG_H = 8
CH = 64
TOK_ROWS = 256
ZW0 = 3840
NDEV = 8
ADAM_LR, ADAM_B1, ADAM_B2, ADAM_EPS, ADAM_WD, ADAM_STEP = 0.001, 0.9, 0.999, 1e-08, 0.01, 10

Z0_FV, Z0_GQKV, Z0_GG, Z0_SMALL = 8, 12, 24, 28


def _dot(a, b, dims=NN):
    return lax.dot_general(a, b, dims, preferred_element_type=f32)


def _iota2(shape, axis):
    return lax.broadcasted_iota(jnp.int32, shape, axis)


def _split3(x):
    x1 = x.astype(bf16)
    r = x - x1.astype(f32)
    x2 = r.astype(bf16)
    return x1, x2, (r - x2.astype(f32)).astype(bf16)


def _dot_sel(a, b, dims=NN, exact_lhs=False):
    if exact_lhs:
        return sum(_dot(a.astype(bf16), piece, dims) for piece in _split3(b))
    return sum(_dot(piece, b.astype(bf16), dims) for piece in _split3(a))


@jax.custom_vjp
def _sel_rhs(a, b):
    return _dot_sel(a, b)


_sel_rhs.defvjp(lambda a, b: (_dot_sel(a, b), b), lambda b, g: (_dot_sel(g, b, NT), jnp.zeros_like(b)))


@jax.custom_vjp
def _sel_lhs(a, x):
    return _dot_sel(a, x, exact_lhs=True)


_sel_lhs.defvjp(lambda a, x: (_dot_sel(a, x, exact_lhs=True), a), lambda a, g: (jnp.zeros_like(a), _dot_sel(a, g, TN, exact_lhs=True)))


class _Side:
    def __init__(self, ins, out_shapes, scratch, start, finish):
        self.ins, self.out_shapes, self.scratch, self.start, self.finish = list(ins), list(out_shapes), list(scratch), start, finish


def _join_sides(sides):
    def split(refs, counts):
        out, off = [], 0
        for c in counts:
            out.append(refs[off:off + c])
            off += c
        return out

    ni, no, ns = ([len(getattr(sd, a)) for sd in sides] for a in ("ins", "out_shapes", "scratch"))

    def run(which):
        def go(ins, outs, sems):
            for sd, i, o, c in zip(sides, split(ins, ni), split(outs, no), split(sems, ns)):
                getattr(sd, which)(i, o, c)
        return go

    return _Side(sum((sd.ins for sd in sides), []), sum((sd.out_shapes for sd in sides), []),
                 sum((sd.scratch for sd in sides), []), run("start"), run("finish"))


def _pcall(side, body, *, name, grid, in_specs, out_specs, out_shape, scratch_shapes=(), compiler_params=None):
    if side is None:
        return pl.pallas_call(body, name=name, grid=grid, in_specs=in_specs, out_specs=out_specs, out_shape=out_shape,
                              scratch_shapes=scratch_shapes, compiler_params=compiler_params)
    single = not isinstance(out_shape, (list, tuple))
    ospecs, oshape = ([out_specs], [out_shape]) if single else (list(out_specs), list(out_shape))
    nin, nout, nscr = len(in_specs), len(ospecs), len(scratch_shapes)
    si, so = len(side.ins), len(side.out_shapes)

    def wrapped(*refs):
        o0 = nin + si
        c0 = o0 + nout + so
        sins, souts, ssems = refs[nin:o0], refs[o0 + nout:c0], refs[c0 + nscr:]
        ids = [pl.program_id(a) for a in range(len(grid))]
        first = functools.reduce(jnp.logical_and, [i == 0 for i in ids])
        last = functools.reduce(jnp.logical_and, [i == g - 1 for i, g in zip(ids, grid)])

        @pl.when(first)
        def _():
            side.start(sins, souts, ssems)

        body(*refs[:nin], *refs[o0:o0 + nout], *refs[c0:c0 + nscr])

        @pl.when(last)
        def _():
            side.finish(sins, souts, ssems)

    call = pl.pallas_call(
        wrapped, name=name, grid=grid, in_specs=list(in_specs) + _hbm_specs(si), out_specs=ospecs + _hbm_specs(so),
        out_shape=oshape + side.out_shapes, scratch_shapes=list(scratch_shapes) + side.scratch,
        compiler_params=pltpu.CompilerParams(dimension_semantics=("arbitrary",) * len(grid),
                                             vmem_limit_bytes=getattr(compiler_params, "vmem_limit_bytes", None)))

    def run(*args):
        res = call(*args, *side.ins)
        return (res[0] if single else list(res[:nout])), list(res[nout:])

    return run


def _tok_specs(rows, consts, tm):
    specs = []
    for (_, w, base) in rows:
        specs.append(pl.BlockSpec((tm, w), functools.partial(lambda j, i, b: (i, b + j), b=base)))
    for (arr, w, base) in consts:
        if w is None:
            specs.append(pl.BlockSpec(arr.shape, lambda j, i: (0, 0)))
        else:
            specs.append(pl.BlockSpec((arr.shape[0], w), functools.partial(lambda j, i, b: (0, b + j), b=base)))
    return specs


def _tok_fwd(name, f, rows, consts, outs, tm, ncb=1, with_j=False, also_t=(), side=None):
    n = rows[0][0].shape[0]
    nin = len(rows) + len(consts)
    nout = len(outs)

    def body(*refs):
        ins = [r[...] for r in refs[:nin]]
        vals = f(pl.program_id(0), *ins) if with_j else f(*ins)
        for r, v in zip(refs[nin:nin + nout], vals):
            r[...] = v.astype(r.dtype)
        for r, k in zip(refs[nin + nout:], also_t):
            r[...] = vals[k].T.astype(r.dtype)

    return _pcall(
        side, body, name=name, grid=(ncb, n // tm),
        in_specs=_tok_specs(rows, consts, tm),
        out_specs=[pl.BlockSpec((tm, w), lambda j, i: (i, j)) for (w, _) in outs]
        + [pl.BlockSpec((outs[k][0], tm), lambda j, i: (0, i)) for k in also_t],
        out_shape=[S((n, w * ncb), dt) for (w, dt) in outs] + [S((outs[k][0], n), bf16) for k in also_t],
        compiler_params=pltpu.CompilerParams(dimension_semantics=("parallel", "parallel")),
    )(*[r[0] for r in rows], *[c[0] for c in consts])


def _tok_bwd(name, f, rows, consts, cots, tm, drow, dconst, ncb=1, with_j=False, addto=None, also_t=(), drow_dtype=f32, side=None, ncat=0):
    n = rows[0][0].shape[0]
    nr, nc, nct = len(rows), len(consts), len(cots)
    addto = addto or {}
    add_keys = sorted(addto)
    nadd = len(add_keys)

    def body(*refs):
        ins = [r[...] for r in refs[:nr + nc]]
        cot = [r[...] for r in refs[nr + nc:nr + nc + nct]]
        adds = refs[nr + nc + nct:nr + nc + nct + nadd]
        outs = refs[nr + nc + nct + nadd:]
        pos = list(drow) + [nr + k for k in dconst]

        def g(*dargs):
            full = list(ins)
            for p, a in zip(pos, dargs):
                full[p] = a
            return tuple(f(pl.program_id(0), *full) if with_j else f(*full))

        vals, vjp = jax.vjp(g, *[ins[p] for p in pos])
        grads = vjp(tuple(c.astype(v.dtype) for c, v in zip(cot, vals)))
        off = 0
        for k in range(len(drow)):
            gk = grads[k]
            if k in addto:
                gk = gk + adds[add_keys.index(k)][...]
            if k < ncat:
                outs[0][:, off:off + gk.shape[1]] = gk.astype(outs[0].dtype)
                off += gk.shape[1]
            else:
                outs[k - skip][...] = gk.astype(outs[k - skip].dtype)
            if k in also_t:
                tref = outs[len(drow) - skip + len(dconst) + list(also_t).index(k)]
                tref[...] = gk.T.astype(tref.dtype)
        first = pl.program_id(1) == 0
        for k in range(len(dconst)):
            ref = outs[len(drow) - skip + k]

            @pl.when(first)
            def _():
                ref[...] = jnp.zeros_like(ref)

            ref[...] += grads[len(drow) + k]

    skip = max(ncat - 1, 0)
    in_specs = _tok_specs(rows, consts, tm)
    in_specs += [pl.BlockSpec((tm, w), lambda j, i: (i, j)) for (_, w) in cots]
    in_specs += [pl.BlockSpec((tm, rows[drow[k]][1]), lambda j, i: (i, j)) for k in add_keys]
    dts = drow_dtype if isinstance(drow_dtype, (list, tuple)) else [drow_dtype] * len(drow)
    widths = [rows[k][1] for k in drow]
    if ncat:
        widths, dts = [sum(widths[:ncat])] + widths[ncat:], [dts[0]] + list(dts[ncat:])
    out_specs = [pl.BlockSpec((tm, wd), lambda j, i: (i, j)) for wd in widths]
    out_shape = [S((n, wd * ncb), dt) for wd, dt in zip(widths, dts)]
    for k in dconst:
        arr, w, _ = consts[k]
        if w is None:
            out_specs.append(pl.BlockSpec(arr.shape, lambda j, i: (0, 0)))
            out_shape.append(S(arr.shape, f32))
        else:
            out_specs.append(pl.BlockSpec((arr.shape[0], w), lambda j, i: (0, j)))
            out_shape.append(S((arr.shape[0], w * ncb), f32))
    for k in also_t:
        out_specs.append(pl.BlockSpec((rows[drow[k]][1], tm), lambda j, i: (0, i)))
        out_shape.append(S((rows[drow[k]][1], n), bf16))
    return _pcall(
        side, body, name=name, grid=(ncb, n // tm), in_specs=in_specs, out_specs=out_specs, out_shape=out_shape,
        compiler_params=pltpu.CompilerParams(dimension_semantics=("parallel", "arbitrary")),
    )(*[r[0] for r in rows], *[c[0] for c in consts], *[c[0] for c in cots], *[addto[k] for k in add_keys])


SCAN_CHUNKS = 2


def _scan_fwd(name, f, rows, nh, nchunk, side=None, consts=(), out_dtype=f32, out_k=1):
    n = rows[0][0].shape[0]
    nb = n // (CH * nchunk)
    nin, nco = len(rows), len(consts)
    w = nh * HD
    cps = SCAN_CHUNKS

    def body(*refs):
        o_ref, hist_ref, st = refs[nin + nco], refs[nin + nco + 1], refs[nin + nco + 2]

        @pl.when(pl.program_id(0) == 0)
        def _():
            st[...] = jnp.zeros_like(st)

        cvals = [r[...] for r in refs[nin:nin + nco]]
        state = st[...]
        for sub in range(cps):
            rr = pl.ds(sub * CH, CH)
            hist_ref[sub] = state.astype(hist_ref.dtype)
            tiles = [r[:, rr, :].reshape(nb * CH, r.shape[2]) for r in refs[:nin]]
            o, state = f(*tiles, *cvals, state)
            o_ref[:, rr, :] = o.reshape(nb, CH, out_k * w).astype(o_ref.dtype)
        st[...] = state

    seq3 = lambda a: a.reshape(nb, nchunk * CH, a.shape[1])
    res = _pcall(
        side, body, name=name, grid=(nchunk // cps,),
        in_specs=[pl.BlockSpec((nb, cps * CH, k * w), functools.partial(lambda c, base: (0, c, base), base=b)) for (_, b, k) in rows]
        + [pl.BlockSpec(c.shape, lambda c: (0, 0)) for c in consts],
        out_specs=[pl.BlockSpec((nb, cps * CH, out_k * w), lambda c: (0, c, 0)), pl.BlockSpec((cps, nb * w, HD), lambda c: (c, 0, 0))],
        out_shape=[S((nb, nchunk * CH, out_k * w), out_dtype), S((nchunk, nb * w, HD), bf16)],
        scratch_shapes=[pltpu.VMEM((nb * w, HD), f32)],
        compiler_params=pltpu.CompilerParams(dimension_semantics=("arbitrary",)),
    )(*[seq3(r[0]) for r in rows], *consts)
    (o, hist), extra = (res, None) if side is None else res
    out = [o.reshape(n, out_k * w), hist]
    return out if side is None else (out, extra)


def _scan_bwd(name, f, rows, hist, do, nh, nchunk, side=None, dtypes=None, consts=(), out_k=1):
    n = rows[0][0].shape[0]
    nb = n // (CH * nchunk)
    nin, nco = len(rows), len(consts)
    w = nh * HD
    cps = SCAN_CHUNKS
    nstep = nchunk // cps

    def body(*refs):
        hist_ref, do_ref = refs[nin + nco], refs[nin + nco + 1]
        outs = refs[nin + nco + 2:nin + nco + 2 + nin]
        couts = refs[nin + nco + 2 + nin:nin + nco + 2 + nin + nco]
        ds = refs[nin + nco + 2 + nin + nco]

        @pl.when(pl.program_id(0) == 0)
        def _():
            ds[...] = jnp.zeros_like(ds)
            for c in couts:
                c[...] = jnp.zeros_like(c)

        cvals = [r[...] for r in refs[nin:nin + nco]]
        dstate = ds[...]
        for sub in reversed(range(cps)):
            rr = pl.ds(sub * CH, CH)
            tiles = [r[:, rr, :].reshape(nb * CH, r.shape[2]) for r in refs[:nin]]
            _, vjp = jax.vjp(f, *tiles, *cvals, hist_ref[sub].astype(f32))
            grads = vjp((do_ref[:, rr, :].reshape(nb * CH, out_k * w), dstate))
            for r, gk in zip(outs, grads[:nin]):
                r[:, rr, :] = gk.reshape(nb, CH, r.shape[2]).astype(r.dtype)
            for c, gk in zip(couts, grads[nin:nin + nco]):
                c[...] += gk
            dstate = grads[nin + nco]
        ds[...] = dstate

    seq3 = lambda a: a.reshape(nb, nchunk * CH, a.shape[1])
    rev = lambda c, base: (0, nstep - 1 - c, base)
    res = _pcall(
        side, body, name=name, grid=(nstep,),
        in_specs=[pl.BlockSpec((nb, cps * CH, k * w), functools.partial(rev, base=b)) for (_, b, k) in rows]
        + [pl.BlockSpec(c.shape, lambda c: (0, 0)) for c in consts]
        + [pl.BlockSpec((cps, nb * w, HD), lambda c: (nstep - 1 - c, 0, 0)),
           pl.BlockSpec((nb, cps * CH, out_k * w), functools.partial(rev, base=0))],
        out_specs=[pl.BlockSpec((nb, cps * CH, k * w), functools.partial(rev, base=0)) for (_, _, k) in rows]
        + [pl.BlockSpec(c.shape, lambda c: (0, 0)) for c in consts],
        out_shape=[S((nb, nchunk * CH, k * w), dt) for (_, _, k), dt in zip(rows, dtypes or [f32] * nin)]
        + [S(c.shape, f32) for c in consts],
        scratch_shapes=[pltpu.VMEM((nb * w, HD), f32)],
        compiler_params=pltpu.CompilerParams(dimension_semantics=("arbitrary",)),
    )(*[seq3(r[0]) for r in rows], *consts, hist, seq3(do))
    outs, extra = (res, None) if side is None else res
    outs = [o.reshape(n, o.shape[2]) for o in outs[:nin]] + list(outs[nin:])
    return outs if side is None else (outs, extra)


_VMEM_LIMIT = 56 * 2 ** 20
_VMEM_TILE_BUDGET = 40 * 2 ** 20


def _mm_tiles(m, n, k, sa, sb, so, sx, a_f32, b_f32, tn_fixed):
    best = None
    for tm in (1024, 512, 256, 128, 64):
        for tn in ((tn_fixed,) if tn_fixed else (1024, 768, 512, 384, 256, 128)):
            if m % tm or n % tn:
                continue
            need = 2 * (tm * k * sa + k * tn * sb + tm * tn * (so + sx)) + tm * tn * 4
            need += tm * k * (2 if sa == 4 else 0) + k * tn * (2 if sb == 4 else 0)
            need += tm * k * (4 if a_f32 else 0) + k * tn * (4 if b_f32 else 0)
            if need <= _VMEM_TILE_BUDGET and (best is None or (tm * tn, tm) > best[0]):
                best = ((tm * tn, tm), tm, tn)
    return best[1], best[2]


def _mm(name, a, b, dims, out_dtype, a_fn=None, b_fn=None, epi=None, extra=(), consts=(), outs=None, out_t=False, slab=False,
        side=None):
    m, kk = a.shape
    gathered = b.ndim == 3
    if gathered:
        nn = NDEV * _COLW if dims is NN else b.shape[1]
    else:
        nn = b.shape[1] if dims is NN else b.shape[0]
    kinds = [("tile", out_dtype)] if outs is None else list(outs)
    so = sum(jnp.dtype(dt).itemsize for kd, dt in kinds if kd != "rows")
    sx = sum(e.dtype.itemsize for e in extra)
    full_rows = bool(consts) or any(kd == "rows" for kd, _ in kinds)
    tm, tn = _mm_tiles(m, nn, kk, a.dtype.itemsize, b.dtype.itemsize, so, sx,
                       a_fn is not None, b_fn is not None,
                       _COLW if slab else (nn if full_rows else None))
    gblocks = tn // _COLW if (gathered and dims is NN) else 0
    assert not gblocks or (outs is None and not extra and not out_t and not slab and b_fn is None)
    nex, nco = len(extra), len(consts)

    def body(a_ref, b_ref, *rest):
        av = a_ref[...]
        if a_fn is not None:
            av = a_fn(av.astype(f32))
        av = av.astype(bf16)
        if gblocks:
            for gb in range(gblocks):
                part = _dot(av, b_ref[gb], NN)
                rest[nex + nco][:, _COLW * gb:_COLW * (gb + 1)] = (part if epi is None else epi(part)).astype(rest[nex + nco].dtype)
            return
        if gathered:
            acc = sum(_dot(av[:, _COLW * d:_COLW * (d + 1)], b_ref[d], NT) for d in range(NDEV))
        else:
            bv = b_ref[...]
            if b_fn is not None:
                bv = b_fn(bv.astype(f32))
            acc = _dot(av, bv.astype(bf16), dims)
        if epi is not None:
            acc = epi(acc, *[r[...] for r in rest[:nex + nco]])
        vals = acc if isinstance(acc, tuple) else (acc,)
        for (kd, _), o_ref, val in zip(kinds, rest[nex + nco:], vals):
            if kd == "rows":
                @pl.when(pl.program_id(0) == 0)
                def _():
                    o_ref[...] = jnp.zeros_like(o_ref)

                o_ref[...] += val
            elif kd == "tile_t" or out_t:
                o_ref[...] = val.T.astype(o_ref.dtype)
            elif slab:
                o_ref[0] = val.astype(o_ref.dtype)
            else:
                o_ref[...] = val.astype(o_ref.dtype)

    if gathered:
        bspec = (pl.BlockSpec((gblocks, kk, _COLW), lambda i, j: (j, 0, 0)) if dims is NN
                 else pl.BlockSpec((NDEV, tn, _COLW), lambda i, j: (0, j, 0)))
    else:
        bspec = pl.BlockSpec((kk, tn), lambda i, j: (0, j)) if dims is NN else pl.BlockSpec((tn, kk), lambda i, j: (j, 0))
    out_specs, out_shape = [], []
    for kd, dt in kinds:
        if kd == "rows":
            out_specs.append(pl.BlockSpec((1, nn), lambda i, j: (0, 0)))
            out_shape.append(S((1, nn), dt))
        elif kd == "tile_t" or out_t:
            out_specs.append(pl.BlockSpec((tn, tm), lambda i, j: (j, i)))
            out_shape.append(S((nn, m), dt))
        elif slab:
            out_specs.append(pl.BlockSpec((1, tm, tn), lambda i, j: (j, i, 0)))
            out_shape.append(S((nn // tn, m, tn), dt))
        else:
            out_specs.append(pl.BlockSpec((tm, tn), lambda i, j: (i, j)))
            out_shape.append(S((m, nn), dt))
    if outs is None:
        out_specs, out_shape = out_specs[0], out_shape[0]
    sem = ("arbitrary", "arbitrary") if any(kd == "rows" for kd, _ in kinds) else ("parallel", "parallel")
    return _pcall(
        side, body, name=name, grid=(m // tm, nn // tn),
        in_specs=[pl.BlockSpec((tm, kk), lambda i, j: (i, 0)), bspec]
        + [pl.BlockSpec((tm, tn), lambda i, j: (i, j)) for _ in extra]
        + [pl.BlockSpec(c.shape, lambda i, j: (0, 0)) for c in consts],
        out_specs=out_specs, out_shape=out_shape,
        compiler_params=pltpu.CompilerParams(dimension_semantics=sem, vmem_limit_bytes=_VMEM_LIMIT),
    )(a, b, *extra, *consts)


def _f_norm(x, g):
    return (x * lax.rsqrt(jnp.mean(x * x, axis=-1, keepdims=True) + EPS) * g,)


def _f_foxpre(zqk, gq, gk, pm):
    def nrm(t, g):
        return t * lax.rsqrt(_sel_rhs(t * t, pm) + EPS) * g
    return nrm(zqk[:, :FOX_W], gq), nrm(zqk[:, FOX_W:], gk)


def _chunk_cumsum(x):
    n = x.shape[0]
    r, c = _iota2((n, n), 0), _iota2((n, n), 1)
    tri = jnp.logical_and(r >= c, (r // CH) == (c // CH)).astype(f32)
    return _sel_lhs(tri, x)


def _f_gdngate(zs, eb, ea, alog_b, dt_b):
    beta = jax.nn.sigmoid(_sel_rhs(zs, eb))
    la = -jnp.exp(alog_b) * jax.nn.softplus(_sel_rhs(zs, ea) + dt_b)
    return beta, _chunk_cumsum(la)


def _f_conv(j, x, w):
    t = x.shape[0]
    y = x * w[3:4, :]
    for jj in range(3):
        sh = 3 - jj
        xs = jnp.concatenate([jnp.zeros((sh, x.shape[1]), f32), x[:t - sh, :]], axis=0)
        y = y + xs * w[jj:jj + 1, :]
    y = jax.nn.silu(y)
    yn = y * lax.rsqrt(jnp.sum(y * y, axis=-1, keepdims=True) + EPS)
    return (jnp.where(j < 2 * GDN_H, yn, y),)


def _head_rms(o, nh):
    outs = []
    for h in range(nh):
        oh = o[:, HD * h:HD * (h + 1)]
        outs.append(oh * lax.rsqrt(jnp.mean(oh * oh, axis=-1, keepdims=True) + EPS))
    return jnp.concatenate(outs, axis=1)


def _f_post0(fox_o, o, gg, on):
    return (jnp.concatenate([fox_o, _head_rms(o, GDN_H) * on * jax.nn.silu(gg)], axis=1),)


def _f_post1(o, zg, on):
    return (_head_rms(o, HG_H) * on * jax.nn.silu(zg),)


def _f_hpre(zqf, lbl):
    lb = jax.nn.sigmoid(lbl[1:2, :] - lbl[0:1, :])
    fg = lb + (1.0 - lb) * jax.nn.sigmoid(zqf[:, D:])
    return jax.nn.silu(zqf[:, :D]), 1.0 - fg, _chunk_cumsum(jnp.log(fg))


def _dotb(a, b, dims=NN):
    return _dot(a.astype(bf16), b.astype(bf16), dims)


def _dot3(a, b):
    ah, bh = a.astype(bf16), b.astype(bf16)
    al, bl = (a - ah.astype(f32)).astype(bf16), (b - bh.astype(f32)).astype(bf16)
    return _dot(ah, bh) + (_dot(ah, bl) + _dot(al, bh))


def _split(t, nh):
    return [t[CH * ck:CH * (ck + 1), HD * h:HD * (h + 1)] for ck in range(t.shape[0] // CH) for h in range(nh)]


def _merge(units, nh):
    return jnp.concatenate([jnp.concatenate(units[i:i + nh], axis=1) for i in range(0, len(units), nh)], axis=0)


def _inv_impl(amats):
    n = amats[0].shape[0]
    eye = jnp.where(_iota2((n, n), 0) == _iota2((n, n), 1), 1.0, 0.0).astype(f32)
    xs, ps = [eye - a for a in amats], list(amats)
    for _ in range(max(1, (n - 1).bit_length()) - 1):
        ps = [_dotb(p, p) for p in ps]
        xs = [x + _dotb(x, p) for x, p in zip(xs, ps)]
    for _ in range(3):
        rs = [eye - x - _dot3(a, x) for a, x in zip(amats, xs)]
        xs = [x + _dotb(x, r) for x, r in zip(xs, rs)]
    return tuple(xs)


@jax.custom_vjp
def _inv_unit_lower(amats):
    return _inv_impl(amats)


def _inv_fwd(amats):
    xs = _inv_impl(amats)
    return xs, xs


def _inv_bwd(xs, dxs):
    return (tuple(-_dotb(_dotb(x, dx, TN), x, NT) for x, dx in zip(xs, dxs)),)


_inv_unit_lower.defvjp(_inv_fwd, _inv_bwd)


@jax.custom_vjp
def _inv_given(amats, xs):
    return xs


def _inv_given_fwd(amats, xs):
    return xs, xs


def _inv_given_bwd(xs, dxs):
    return _inv_bwd(xs, dxs)[0], tuple(jnp.zeros_like(x) for x in xs)


_inv_given.defvjp(_inv_given_fwd, _inv_given_bwd)


def _f_gdn_intra(q, k, v, bb, gb, tinv_p=None):
    qs, ks, vs, bs, gs = (_split(t, GDN_H) for t in (q, k, v, bb, gb))
    r, cc = _iota2((CH, CH), 0), _iota2((CH, CH), 1)
    causal, strict = r >= cc, r > cc
    beta, g, gl = [b[:, :1] for b in bs], [x[:, :1] for x in gs], [x[CH - 1:CH, :1] for x in gs]
    decay = [jnp.exp(jnp.where(causal, x[:, :CH] - x[:, :CH].T, -jnp.inf)) for x in gs]
    kb = [ki * bi for ki, bi in zip(ks, beta)]
    amat = [jnp.where(strict, _dotb(kbi, ki, NT) * di, 0.0) for kbi, ki, di in zip(kb, ks, decay)]
    if tinv_p is None:
        tinv = _inv_unit_lower(tuple(amat))
    else:
        tinv = _inv_given(tuple(amat), tuple(x[:, :CH] for x in _split(tinv_p, GDN_H)))
    rhs = [jnp.concatenate([vi * bi, kbi * jnp.exp(gi)], axis=1) for vi, bi, kbi, gi in zip(vs, beta, kb, g)]
    uw = [_dotb(ti, ri) for ti, ri in zip(tinv, rhs)]
    qsc = [qi * (HD ** -0.5) for qi in qs]
    qk = [jnp.where(causal, _dotb(qi, ki, NT) * di, 0.0) for qi, ki, di in zip(qsc, ks, decay)]
    outs = ([x[:, :HD] for x in uw], [x[:, HD:] for x in uw],
            [jnp.concatenate([x, jnp.zeros_like(x)], axis=1) for x in qk],
            [qi * jnp.exp(gi) for qi, gi in zip(qsc, g)],
            [ki * jnp.exp(gli - gi) for ki, gli, gi in zip(ks, gl, g)],
            [jnp.broadcast_to(gli, (CH, HD)) for gli in gl])
    if tinv_p is None:
        outs += ([jnp.concatenate([x, jnp.zeros_like(x)], axis=1) for x in tinv],)
    return tuple(_merge(o, GDN_H) for o in outs)


def _f_gdn_gated(q, k, v, zs, eb, ea, alog_b, dt_b):
    return _f_gdn_intra(q, k, v, *_f_gdngate(zs, eb, ea, alog_b, dt_b))


def _f_gdn_gated_given(q, k, v, zs, tinv_p, eb, ea, alog_b, dt_b):
    return _f_gdn_intra(q, k, v, *_f_gdngate(zs, eb, ea, alog_b, dt_b), tinv_p=tinv_p)


def _f_gdn_inter(u, w, qkp, qd, kd, glb, st):
    us, ws, qks, qds, kds, gls = (_split(t, GDN_H) for t in (u, w, qkp, qd, kd, glb))
    sts = [st[HD * i:HD * (i + 1), :] for i in range(len(us))]
    vn = [ui - _dotb(wi, si) for ui, wi, si in zip(us, ws, sts)]
    o = [_dotb(qi, si) + _dotb(xi[:, :CH], vi) for qi, si, xi, vi in zip(qds, sts, qks, vn)]
    s2 = [si * jnp.exp(gi[:1, :1]) + _dotb(ki, vi, TN) for si, gi, ki, vi in zip(sts, gls, kds, vn)]
    return _merge(o, GDN_H), jnp.concatenate(s2, axis=0)


def _f_gdn_full(u, w, qkp, qd, kd, glb, fox_o, gg, on, st):
    o, s2 = _f_gdn_inter(u, w, qkp, qd, kd, glb, st)
    return _f_post0(fox_o, o, gg, on)[0], s2


def _f_hgrn_chunk(q, k, b, v, st):
    qs, ks, bs, vs = (_split(t, HG_H) for t in (q, k, b, v))
    sts = [st[HD * i:HD * (i + 1), :] for i in range(len(qs))]
    causal = _iota2((CH, CH), 0) >= _iota2((CH, CH), 1)
    bl, bm = [x[CH - 1:CH, :] for x in bs], [x[CH // 2 - 1:CH // 2, :] for x in bs]
    a = [jnp.where(causal, _dotb(qi * jnp.exp(bi - mi), ki * jnp.exp(mi - bi), NT), 0.0)
         for qi, ki, bi, mi in zip(qs, ks, bs, bm)]
    o = [_dotb(qi * jnp.exp(bi), si, NT) + _dotb(ai, vi) for qi, bi, si, ai, vi in zip(qs, bs, sts, a, vs)]
    s2 = [si * jnp.exp(li) + _dotb(vi, ki * jnp.exp(li - bi), TN) for si, li, vi, ki, bi in zip(sts, bl, vs, ks, bs)]
    return _merge(o, HG_H), jnp.concatenate(s2, axis=0)


def _f_hgrn_full(z, lbl, on, st):
    o, s2 = _f_hgrn_chunk(*_f_hpre(z[:, :2 * D], lbl), z[:, 2 * D:3 * D], st)
    return _f_post1(o, z[:, 3 * D:], on)[0], s2


def _fox_gate_fwd(z0, fbias, t, tc=256):
    n = z0.shape[0]
    nt = t // tc

    def body(zs_ref, b_ref, ccol_ref, crow_ref, carry):
        @pl.when(pl.program_id(1) == 0)
        def _():
            carry[...] = jnp.zeros_like(carry)

        ls = jnp.where(_iota2((tc, LANES), 1) < FOX_H, jax.nn.log_sigmoid(zs_ref[...] + b_ref[...]), 0.0)
        tri = (_iota2((tc, tc), 0) >= _iota2((tc, tc), 1)).astype(f32)
        c = _dot_sel(tri, ls, exact_lhs=True) + carry[...]
        carry[...] = c[tc - 1:tc, :]
        ccol_ref[...] = c
        crow_ref[0] = c.T[:FOX_H, :]

    return pl.pallas_call(
        body, name="fox_gate_fwd", grid=(n // t, nt),
        in_specs=[pl.BlockSpec((tc, LANES), lambda b, i: (b * nt + i, Z0_SMALL)), pl.BlockSpec((1, LANES), lambda b, i: (0, 0))],
        out_specs=[pl.BlockSpec((tc, LANES), lambda b, i: (b * nt + i, 0)), pl.BlockSpec((1, FOX_H, tc), lambda b, i: (b, 0, i))],
        out_shape=[S((n, LANES), f32), S((n // t, FOX_H, t), f32)],
        scratch_shapes=[pltpu.VMEM((1, LANES), f32)],
        compiler_params=pltpu.CompilerParams(dimension_semantics=("parallel", "arbitrary")),
    )(z0, fbias)


def _fox_gate_bwd(z0, fbias, dcq, dck, t, tc=256):
    n = z0.shape[0]
    nt = t // tc

    def body(zs_ref, b_ref, dcq_ref, dck_ref, dz_ref, db_ref, carry):
        first = jnp.logical_and(pl.program_id(0) == 0, pl.program_id(1) == 0)

        @pl.when(pl.program_id(1) == 0)
        def _():
            carry[...] = jnp.zeros_like(carry)

        @pl.when(first)
        def _():
            db_ref[...] = jnp.zeros_like(db_ref)

        dc = dcq_ref[0] + dcq_ref[1] + dcq_ref[2] + dcq_ref[3]
        drow = dck_ref[0, 0] + dck_ref[1, 0] + dck_ref[2, 0] + dck_ref[3, 0]
        eye = (_iota2((FOX_H, LANES), 0) == _iota2((FOX_H, LANES), 1)).astype(f32)
        dc = dc + _dot_sel(drow, eye, TN)
        triu = (_iota2((tc, tc), 0) <= _iota2((tc, tc), 1)).astype(f32)
        dls = _dot_sel(triu, dc, exact_lhs=True) + carry[...]
        carry[...] = dls[0:1, :]
        x = zs_ref[...] + b_ref[...]
        dz = jnp.where(_iota2((tc, LANES), 1) < FOX_H, dls * jax.nn.sigmoid(-x), 0.0)
        dz_ref[...] = dz
        db_ref[...] += jnp.sum(dz, axis=0, keepdims=True)

    def rev(b, i):
        return b * nt + (nt - 1 - i)

    return pl.pallas_call(
        body, name="fox_gate_bwd", grid=(n // t, nt),
        in_specs=[pl.BlockSpec((tc, LANES), lambda b, i: (rev(b, i), Z0_SMALL)), pl.BlockSpec((1, LANES), lambda b, i: (0, 0)),
                  pl.BlockSpec((4, tc, LANES), lambda b, i: (0, rev(b, i), 0)),
                  pl.BlockSpec((4, 1, FOX_H, tc), lambda b, i: (0, b, 0, nt - 1 - i))],
        out_specs=[pl.BlockSpec((tc, LANES), lambda b, i: (rev(b, i), 0)), pl.BlockSpec((1, LANES), lambda b, i: (0, 0))],
        out_shape=[S((n, LANES), f32), S((1, LANES), f32)],
        scratch_shapes=[pltpu.VMEM((1, LANES), f32)],
        compiler_params=pltpu.CompilerParams(dimension_semantics=("arbitrary", "arbitrary")),
    )(z0, fbias, dcq, dck)


def _fox_scores(hh, p, i, tq, q, k, ccol, crow):
    kmax = k.shape[0]
    lane = _iota2((1, LANES), 1)
    mh = (lane // FOX_D) == hh
    h = 2 * p + hh
    qh = jnp.where(mh, q, 0.0).astype(bf16)
    s = _dot(qh, k, NT) * (FOX_D ** -0.5)
    cq = jnp.sum(jnp.where(lane == h, ccol, 0.0), axis=1, keepdims=True)
    ck = jnp.sum(jnp.where(_iota2((FOX_H, 1), 0) == h, crow, 0.0), axis=0, keepdims=True)
    causal = _iota2((1, kmax), 1) <= (i * tq + _iota2((tq, 1), 0))
    s = jnp.where(causal, s + cq - ck, -jnp.inf)
    pe = jnp.exp(s - jnp.max(s, axis=1, keepdims=True))
    return mh, qh, pe, jnp.sum(pe, axis=1, keepdims=True)


def _fox_attn_fwd(qn, kn, z0, ccol, crow, t, tq=256, side=None):
    n = qn.shape[0]
    nq = t // tq

    def body(q_ref, k_ref, v_ref, ccol_ref, crow_ref, o_ref):
        p = pl.program_id(1)
        k, v, crow = k_ref[...].astype(bf16), v_ref[...].astype(bf16), crow_ref[0]
        for i in range(nq):
            rows, kmax = pl.ds(i * tq, tq), (i + 1) * tq
            q, cc = q_ref[rows, :], ccol_ref[rows, :]
            acc = jnp.zeros((tq, LANES), f32)
            for hh in range(2):
                mh, _, pe, l = _fox_scores(hh, p, i, tq, q, k[:kmax], cc, crow[:, :kmax])
                acc = jnp.where(mh, _dot(pe.astype(bf16), v[:kmax]) / l, acc)
            o_ref[rows, :] = acc

    seq = lambda b, p: (b, p)
    return _pcall(
        side, body, name="fox_attn_fwd", grid=(n // t, FOX_H // 2),
        in_specs=[pl.BlockSpec((t, LANES), seq), pl.BlockSpec((t, LANES), seq), pl.BlockSpec((t, LANES), lambda b, p: (b, Z0_FV + p)),
                  pl.BlockSpec((t, LANES), lambda b, p: (b, 0)), pl.BlockSpec((1, FOX_H, t), lambda b, p: (b, 0, 0))],
        out_specs=pl.BlockSpec((t, LANES), seq),
        out_shape=S((n, FOX_W), f32),
        compiler_params=pltpu.CompilerParams(dimension_semantics=("parallel", "parallel")),
    )(qn, kn, z0, ccol, crow)


def _fox_attn_bwd(qn, kn, z0, ccol, crow, do, t, tq=256, side=None):
    n = qn.shape[0]
    nq = t // tq
    nb = n // t

    def body(q_ref, k_ref, v_ref, ccol_ref, crow_ref, do_ref, dq_ref, dk_ref, dv_ref, dcq_ref, dck_ref):
        p = pl.program_id(1)
        dk_ref[...] = jnp.zeros_like(dk_ref)
        dv_ref[...] = jnp.zeros_like(dv_ref)
        dck_ref[...] = jnp.zeros_like(dck_ref)
        kf, v, crow = k_ref[...], v_ref[...].astype(bf16), crow_ref[0]
        k = kf.astype(bf16)
        lane = _iota2((1, LANES), 1)
        sub = _iota2((FOX_H, 1), 0)
        scale = FOX_D ** -0.5
        for i in range(nq):
            rows, kmax = pl.ds(i * tq, tq), (i + 1) * tq
            q, cc, dout = q_ref[rows, :], ccol_ref[rows, :], do_ref[rows, :]
            dq = jnp.zeros((tq, LANES), f32)
            dcq = jnp.zeros((tq, LANES), f32)
            for hh in range(2):
                mh, qh, pe, l = _fox_scores(hh, p, i, tq, q, k[:kmax], cc, crow[:, :kmax])
                pr = pe / l
                doh = jnp.where(mh, dout, 0.0).astype(bf16)
                dp = _dot(doh, v[:kmax], NT)
                ds = pr * (dp - jnp.sum(pr * dp, axis=1, keepdims=True))
                dsb = ds.astype(bf16)
                dq = dq + _dot(dsb, jnp.where(mh, kf[:kmax], 0.0).astype(bf16)) * scale
                dk_ref[:kmax, :] += _dot(dsb, qh, TN) * scale
                dv_ref[:kmax, :] += _dot(pr.astype(bf16), doh, TN)
                h = 2 * p + hh
                dcq = dcq + jnp.where(lane == h, jnp.sum(ds, axis=1, keepdims=True), 0.0)
                dck_ref[0, 0, :, :kmax] += jnp.where(sub == h, -jnp.sum(ds, axis=0, keepdims=True), 0.0)
            dq_ref[rows, :] = dq
            dcq_ref[0, rows, :] = dcq

    seq = lambda b, p: (b, p)
    return _pcall(
        side, body, name="fox_attn_bwd", grid=(nb, FOX_H // 2),
        in_specs=[pl.BlockSpec((t, LANES), seq), pl.BlockSpec((t, LANES), seq), pl.BlockSpec((t, LANES), lambda b, p: (b, Z0_FV + p)),
                  pl.BlockSpec((t, LANES), lambda b, p: (b, 0)), pl.BlockSpec((1, FOX_H, t), lambda b, p: (b, 0, 0)),
                  pl.BlockSpec((t, LANES), seq)],
        out_specs=[pl.BlockSpec((t, LANES), seq), pl.BlockSpec((t, LANES), seq), pl.BlockSpec((t, LANES), seq),
                   pl.BlockSpec((1, t, LANES), lambda b, p: (p, b, 0)), pl.BlockSpec((1, 1, FOX_H, t), lambda b, p: (p, b, 0, 0))],
        out_shape=[S((n, FOX_W), f32), S((n, FOX_W), f32), S((n, FOX_W), f32), S((4, n, LANES), f32), S((4, nb, FOX_H, t), f32)],
        compiler_params=pltpu.CompilerParams(dimension_semantics=("parallel", "parallel")),
    )(qn, kn, z0, ccol, crow, do)


def _adamw_math(w, g, m, v):
    m = ADAM_B1 * m + (1.0 - ADAM_B1) * g
    v = ADAM_B2 * v + (1.0 - ADAM_B2) * (g * g)
    m_hat = m / (1.0 - ADAM_B1 ** ADAM_STEP)
    v_hat = v / (1.0 - ADAM_B2 ** ADAM_STEP)
    return -ADAM_LR * (m_hat / (jnp.sqrt(v_hat) + ADAM_EPS) + ADAM_WD * w), m, v


def _adamw_big(name, idx, gmine, recv1, recv2, w, m, v):
    r, wc = w.shape
    c = gmine.shape[2]
    tr = min(r, 256)

    def body(idx_ref, gm_ref, r1_ref, r2_ref, w_ref, m_ref, v_ref, g_ref, d_ref, nm_ref, nv_ref):
        g = gm_ref[0].astype(f32) + r1_ref[0].astype(f32)
        for k in range(3):
            g = g + r2_ref[k].astype(f32)
        g = g[:, :wc]
        d, nm, nv = _adamw_math(w_ref[...], g, m_ref[...], v_ref[...])
        g_ref[...] = g
        d_ref[...] = d
        nm_ref[...] = nm
        nv_ref[...] = nv

    row = pl.BlockSpec((tr, wc), lambda i, s: (i, 0))
    return pl.pallas_call(
        body, name=name,
        grid_spec=pltpu.PrefetchScalarGridSpec(
            num_scalar_prefetch=1, grid=(r // tr,),
            in_specs=[pl.BlockSpec((1, tr, c), lambda i, s: (s[0], i, 0)), pl.BlockSpec((1, tr, c), lambda i, s: (s[1], i, 0)),
                      pl.BlockSpec((3, tr, c), lambda i, s: (0, i, 0)), row, row, row],
            out_specs=[row, row, row, row]),
        out_shape=[S((r, wc), f32)] * 4,
        compiler_params=pltpu.CompilerParams(dimension_semantics=("parallel",)),
    )(idx, gmine, recv1, recv2, w, m, v)


def _pair_sum(name, idx, gmine, recv1):
    _, r, c = gmine.shape
    g4 = gmine.reshape(4, 2, r, c)

    def body(idx_ref, gm_ref, r1_ref, o_ref):
        o_ref[0] = (gm_ref[0, 0].astype(f32) + r1_ref[0].astype(f32)).astype(bf16)

    return pl.pallas_call(
        body, name=name,
        grid_spec=pltpu.PrefetchScalarGridSpec(
            num_scalar_prefetch=1, grid=(4,),
            in_specs=[pl.BlockSpec((1, 1, r, c), lambda ch, s: (ch, s[0], 0, 0)), pl.BlockSpec((1, r, c), lambda ch, s: (ch, 0, 0))],
            out_specs=pl.BlockSpec((1, r, c), lambda ch, s: (ch, 0, 0))),
        out_shape=S((4, r, c), bf16),
        compiler_params=pltpu.CompilerParams(dimension_semantics=("parallel",)),
    )(idx, g4, recv1)


def _adamw_small(parts, w, m, v, own_mask, own_g):
    def body(p_ref, w_ref, m_ref, v_ref, mask_ref, og_ref, g_ref, d_ref, nm_ref, nv_ref):
        g = p_ref[0]
        for k in range(1, NDEV):
            g = g + p_ref[k]
        g_ref[...] = g
        ge = jnp.where(mask_ref[...] > 0.5, og_ref[...], g)
        d, nm, nv = _adamw_math(w_ref[...], ge, m_ref[...], v_ref[...])
        d_ref[...] = d
        nm_ref[...] = nm
        nv_ref[...] = nv

    return pl.pallas_call(body, name="adamw_small", out_shape=[S(w.shape, f32)] * 4)(parts, w, m, v, own_mask, own_g)


def _sum_parts(parts):
    def body(p_ref, g_ref):
        g = p_ref[0]
        for k in range(1, NDEV):
            g = g + p_ref[k]
        g_ref[...] = g

    return pl.pallas_call(body, name="sum_parts", out_shape=S(parts.shape[1:], f32))(parts)


def _me():
    return lax.axis_index("x"), lax.axis_index("y"), lax.axis_index("c")


def _hbm_specs(n):
    return [pl.BlockSpec(memory_space=pl.ANY)] * n


def _allgather(name, xs):
    na = len(xs)

    def body(*refs):
        x_refs, out_refs = refs[:na], refs[na:2 * na]
        send_sems, recv_sems, local_sems = refs[2 * na:]
        mx, my, mc = _me()
        me, sib = (mx, my, mc), (mx, my, 1 - mc)
        chips = [(1 - mx, my), (mx, 1 - my), (1 - mx, 1 - my)]

        def slab(a, px, py, pc):
            return out_refs[a].at[4 * px + 2 * py + pc]

        def copy(a, k, block, to, own=False):
            return pltpu.make_async_remote_copy(
                src_ref=x_refs[a] if own else slab(a, *block), dst_ref=slab(a, *block),
                send_sem=send_sems.at[7 * a + k], recv_sem=recv_sems.at[7 * a + k], device_id=to, device_id_type=MESH)

        mine = [pltpu.make_async_copy(x_refs[a], slab(a, *me), local_sems.at[a]) for a in range(na)]
        first = []
        for a in range(na):
            mine[a].start()
            first += [copy(a, 0, me, sib, own=True)] + [copy(a, 1 + j, me, (*chip, mc), own=True) for j, chip in enumerate(chips)]
        for cp in first:
            cp.start()
        passed = []
        for j, chip in enumerate(chips):
            for a in range(na):
                copy(a, 1 + j, (*chip, mc), me).wait_recv()
                passed.append(copy(a, 4 + j, (*chip, mc), sib))
                passed[-1].start()
        for a in range(na):
            copy(a, 0, sib, me).wait_recv()
            for j, chip in enumerate(chips):
                copy(a, 4 + j, (*chip, 1 - mc), me).wait_recv()
        for cp in first + passed:
            cp.wait_send()
        for cp in mine:
            cp.wait()

    return pl.pallas_call(
        body, name=name, out_shape=[S((NDEV,) + x.shape, x.dtype) for x in xs],
        in_specs=_hbm_specs(na), out_specs=_hbm_specs(na),
        scratch_shapes=[pltpu.SemaphoreType.DMA((7 * na,)), pltpu.SemaphoreType.DMA((7 * na,)), pltpu.SemaphoreType.DMA((na,))],
    )(*xs)


def _rs_sibling(gs):
    na = len(gs)

    def body(*refs):
        g_refs, out_refs, send_sems, recv_sems = refs[:na], refs[na:2 * na], refs[2 * na], refs[2 * na + 1]
        mx, my, mc = _me()
        cps = [pltpu.make_async_remote_copy(
            src_ref=g_refs[a].at[2 * ch + 1 - mc], dst_ref=out_refs[a].at[ch], send_sem=send_sems.at[4 * a + ch],
            recv_sem=recv_sems.at[4 * a + ch], device_id=(mx, my, 1 - mc), device_id_type=MESH)
            for a in range(na) for ch in range(4)]
        for cp in cps:
            cp.start()
        for cp in cps:
            cp.wait_recv()
        for cp in cps:
            cp.wait_send()

    return pl.pallas_call(
        body, name="rs_sibling", out_shape=[S((4,) + g.shape[1:], g.dtype) for g in gs],
        in_specs=_hbm_specs(na), out_specs=_hbm_specs(na),
        scratch_shapes=[pltpu.SemaphoreType.DMA((4 * na,)), pltpu.SemaphoreType.DMA((4 * na,))],
    )(*gs)


def _side_allgather(xs):
    na = len(xs)

    def mk(x_refs, out_refs, sems):
        send_sems, recv_sems, local_sems = sems
        mx, my, mc = _me()
        me, sib = (mx, my, mc), (mx, my, 1 - mc)
        chips = [(1 - mx, my), (mx, 1 - my), (1 - mx, 1 - my)]

        def slab(a, px, py, pc):
            return out_refs[a].at[4 * px + 2 * py + pc]

        def copy(a, k, block, to, own=False):
            return pltpu.make_async_remote_copy(
                src_ref=x_refs[a] if own else slab(a, *block), dst_ref=slab(a, *block),
                send_sem=send_sems.at[7 * a + k], recv_sem=recv_sems.at[7 * a + k], device_id=to, device_id_type=MESH)

        mine = [pltpu.make_async_copy(x_refs[a], slab(a, *me), local_sems.at[a]) for a in range(na)]
        first = []
        for a in range(na):
            first += [copy(a, 0, me, sib, own=True)] + [copy(a, 1 + j, me, (*chip, mc), own=True) for j, chip in enumerate(chips)]
        return me, sib, chips, mc, copy, mine, first

    def start(x_refs, out_refs, sems):
        *_, mine, first = mk(x_refs, out_refs, sems)
        for cp in mine + first:
            cp.start()

    def finish(x_refs, out_refs, sems):
        me, sib, chips, mc, copy, mine, first = mk(x_refs, out_refs, sems)
        passed = []
        for j, chip in enumerate(chips):
            for a in range(na):
                copy(a, 1 + j, (*chip, mc), me).wait_recv()
                passed.append(copy(a, 4 + j, (*chip, mc), sib))
                passed[-1].start()
        for a in range(na):
            copy(a, 0, sib, me).wait_recv()
            for j, chip in enumerate(chips):
                copy(a, 4 + j, (*chip, 1 - mc), me).wait_recv()
        for cp in first + passed:
            cp.wait_send()
        for cp in mine:
            cp.wait()

    scratch = [pltpu.SemaphoreType.DMA((7 * na,)), pltpu.SemaphoreType.DMA((7 * na,)), pltpu.SemaphoreType.DMA((na,))]
    return _Side(xs, [S((NDEV,) + x.shape, x.dtype) for x in xs], scratch, start, finish)


def _side_exchange(arrs, nslot, out_slots, route):
    na = len(arrs)

    def copies(in_refs, out_refs, sems):
        send_sems, recv_sems = sems
        return [pltpu.make_async_remote_copy(
            src_ref=in_refs[a].at[src], dst_ref=out_refs[a].at[k], send_sem=send_sems.at[nslot * a + k],
            recv_sem=recv_sems.at[nslot * a + k], device_id=to, device_id_type=MESH)
            for a in range(na) for k, (src, to) in enumerate(route(*_me()))]

    def start(in_refs, out_refs, sems):
        for cp in copies(in_refs, out_refs, sems):
            cp.start()

    def finish(in_refs, out_refs, sems):
        cps = copies(in_refs, out_refs, sems)
        for cp in cps:
            cp.wait_recv()
        for cp in cps:
            cp.wait_send()

    scratch = [pltpu.SemaphoreType.DMA((nslot * na,)), pltpu.SemaphoreType.DMA((nslot * na,))]
    return _Side(arrs, [S((out_slots,) + x.shape[1:], x.dtype) for x in arrs], scratch, start, finish)


def _side_rs_sibling(gs):
    return _side_exchange(gs, 4, 4, lambda mx, my, mc: [(2 * ch + 1 - mc, (mx, my, 1 - mc)) for ch in range(4)])


def _side_rs_chips(ps):
    return _side_exchange(ps, 3, 3, lambda mx, my, mc: [(2 * cx + cy, (cx, cy, mc)) for cx, cy in
                                                        [(1 - mx, my), (mx, 1 - my), (1 - mx, 1 - my)]])


_COLW = 512
_COL_NAMES = ("l0_w_in", "l0_w_ff1", "l1_w_in", "l1_w_ff1")
_ROW_NAMES = ("l0_w_out", "l0_w_ff2", "l1_w_out", "l1_w_ff2")
_BIG_NAMES = _COL_NAMES + _ROW_NAMES


def _full_weight(gathered, name, ncols):
    if name in _ROW_NAMES:
        return gathered.reshape(-1, D)
    return gathered[:, :, :ncols].transpose(1, 0, 2).reshape(D, NDEV * ncols)


def _regroup_w_in0(w):
    main = jnp.concatenate([w[:, 0:1536], w[:, 1544:3080], w[:, 3088:3600]], axis=1)
    small = jnp.concatenate([w[:, 1536:1544], w[:, 3080:3088]], axis=1)
    return jnp.concatenate([main, small, jnp.zeros((D, ZW0 - 3584 - 16), w.dtype)], axis=1)


def _ungroup_w_in0(g):
    return jnp.concatenate([g[:, 0:1536], g[:, 3584:3592], g[:, 1536:3072], g[:, 3592:3600], g[:, 3072:3584]], axis=1)


def _col_slabs(g, ncols):
    g = g.reshape(D, NDEV, ncols).transpose(1, 0, 2)
    return jnp.pad(g, ((0, 0), (0, 0), (0, _COLW - ncols)))


def _sq(t):
    return t * t


def _epi_res_norm(acc, res, gain):
    y = acc + res
    h = _f_norm(y, gain)[0]
    return y, h, h


_RES_NORM_OUTS = [("tile", f32), ("tile", bf16), ("tile_t", bf16)]


def _epi_norm_bwd(acc, x, dres, gain):
    _, vjp = jax.vjp(lambda xx, gg: _f_norm(xx, gg)[0], x, gain)
    dx, dgain = vjp(acc)
    dx = dx + dres
    return dx, dgain, dx


_NORM_BWD_OUTS = [("tile", f32), ("rows", f32), ("tile_t", bf16)]


def _epi_loss(acc, res, tgt):
    e = acc + res - tgt
    dy = e * (1.0 / D)
    return dy, dy, jnp.sum(e * e, axis=0, keepdims=True)


_LOSS_OUTS = [("tile", f32), ("tile_t", bf16), ("rows", f32)]


def _mlp_fwd(tag, x, h, w1, w2, epi, extra, consts, outs):
    a = _mm(f"{tag}_ff1", h, w1, NN, bf16, epi=lambda acc: jnp.maximum(acc, 0.0))
    return _mm(f"{tag}_ff2", a, w2, NN, f32, a_fn=_sq, epi=epi, extra=(x,) + tuple(extra), consts=consts, outs=outs), a


def _mlp_bwd(tag, x, gain, w1, w2, ht, a, dy, dyt, side=None):
    da = _mm(f"{tag}_ff2_dx", dy, w2, NT, bf16, epi=lambda acc, av: acc * 2.0 * av.astype(f32), extra=(a,), side=side)
    if side is not None:
        da, side_res = da
    dw2 = _mm(f"{tag}_ff2_dw", dyt, a, NN, bf16, b_fn=_sq, out_t=True)
    res = _mm(f"{tag}_ff1_dx", da, w1, NT, f32, epi=_epi_norm_bwd, extra=(x, dy), consts=(gain,), outs=_NORM_BWD_OUTS)
    dw1 = _mm(f"{tag}_ff1_dw", ht, da, NN, bf16, slab=True)
    return (res, dw1, dw2) if side is None else ((res, dw1, dw2), side_res)


def _row(v):
    return v.reshape(1, -1).astype(f32)


_L0_REST = ("l0_w_ff1", "l0_w_out")
_L1_MIX = ("l0_w_ff2", "l1_w_in", "l1_w_out")
_L1_FFN = ("l1_w_ff1", "l1_w_ff2")
_GRAD_A = ("l1_w_ff1", "l1_w_ff2", "l1_w_out", "l1_w_in")
_GRAD_B = ("l0_w_ff1", "l0_w_ff2", "l0_w_out")
_IN_PLACE = ("l0_w_ff1", "l1_w_in", "l1_w_ff1")


def _train(x, tgt, args, mom, var, t):
    n = x.shape[0]
    nchunk = t // CH
    mx, my, mc = _me()
    dev, chip = 4 * mx + 2 * my + mc, 2 * mx + my
    core_idx = jnp.reshape(mc, (1,)).astype(jnp.int32)
    own_idx = jnp.stack([dev, chip]).astype(jnp.int32)
    ncols = {nm: args[nm].shape[1] for nm in _COL_NAMES}
    conv_cols = args["l0_gdn_conv"].shape[1]
    g, big, w = {}, {}, {}

    def send(nm):
        a = args[nm].astype(bf16)
        return jnp.pad(a, ((0, 0), (0, _COLW - ncols[nm]))) if nm in _COL_NAMES else a

    def take(names, gathered):
        for nm, arr in zip(names, gathered):
            w[nm] = arr if nm in _IN_PLACE else _full_weight(arr, nm, ncols.get(nm, 0))

    def by_dev(nm, ga):
        return ga if nm in _COL_NAMES else ga.reshape(NDEV, -1, D)

    def pair(names, gs, r1s):
        return [_pair_sum(f"rs_pair_sum_{nm}", core_idx, ga, r1) for nm, ga, r1 in zip(names, gs, r1s)]

    def adam(names, gs, r1s, r2s):
        for nm, ga, r1, r2 in zip(names, gs, r1s, r2s):
            big[nm] = _adamw_big(f"adamw_{nm}", own_idx, ga, r1, r2, args[nm], mom[nm], var[nm])


    li = jnp.arange(FOX_W)
    pm = jnp.where((li[:, None] // FOX_D) == (li[None, :] // FOX_D), 1.0 / FOX_D, 0.0).astype(f32)
    lane_head = jnp.arange(GDN_W) // HD
    sel = lambda first_lane: (jnp.arange(LANES)[:, None] == (first_lane + lane_head)[None, :]).astype(f32)
    e_beta, e_alpha = sel(FOX_H), sel(FOX_H + GDN_H)
    alog_b, dt_b = _row(jnp.repeat(args["l0_gdn_A_log"], HD)), _row(jnp.repeat(args["l0_gdn_dt_bias"], HD))
    gq_t, gk_t = _row(jnp.tile(args["l0_fox_q_norm"], FOX_H)), _row(jnp.tile(args["l0_fox_k_norm"], FOX_H))
    on0_t, on1_t = _row(jnp.tile(args["l0_gdn_o_norm"], GDN_H)), _row(jnp.tile(args["l1_hgrn_o_norm"], HG_H))
    fbias = jnp.pad(_row(args["l0_fox_f_bias"]), ((0, 0), (0, LANES - FOX_H)))
    g0m, g0f, g1m, g1f = (_row(args[k]) for k in ("l0_mix_norm", "l0_ffn_norm", "l1_mix_norm", "l1_ffn_norm"))
    lbl = args["hgrn_lb_logits"].astype(f32)

    first = [send("l0_w_in"), jnp.pad(args["l0_gdn_conv"], ((0, 4), (0, LANES * 2 - conv_cols)))]
    (h0, h0t), first = _tok_fwd("l0_mix_norm", _f_norm, [(x, D, 0)], [(g0m, None, 0)], [(D, bf16)], TOK_ROWS, also_t=(0,),
                                side=_side_allgather(first))
    take(("l0_w_in",), first[:1])
    w_in0 = _regroup_w_in0(w["l0_w_in"])
    wconv = first[1][:, :4, :conv_cols].transpose(1, 0, 2).reshape(4, NDEV * conv_cols)
    z0 = _mm("l0_in", h0, w_in0, NN, f32)
    qk_rows = [(z0, 2 * FOX_W, 0)]
    qk_consts = [(gq_t, None, 0), (gk_t, None, 0), (pm, None, 0)]
    qn, kn = _tok_fwd("fox_pre", _f_foxpre, qk_rows, qk_consts, [(FOX_W, f32)] * 2, TOK_ROWS)
    ccol, crow = _fox_gate_fwd(z0, fbias, t)
    fox_o, got = _fox_attn_fwd(qn, kn, z0, ccol, crow, t, side=_side_allgather([send(nm) for nm in _L0_REST]))
    take(_L0_REST, got)
    conv_rows, conv_consts = [(z0, LANES, Z0_GQKV)], [(wconv, LANES, 0)]
    (qkv,) = _tok_fwd("gdn_conv", _f_conv, conv_rows, conv_consts, [(LANES, f32)], t, ncb=12, with_j=True)
    gate_consts = [(e_beta, None, 0), (e_alpha, None, 0), (alog_b, None, 0), (dt_b, None, 0)]
    intra_rows = [(qkv, GDN_W, 0), (qkv, GDN_W, 1), (qkv, GDN_W, 2), (z0, LANES, Z0_SMALL)]
    intra, got = _tok_fwd("gdn_intra", _f_gdn_gated, intra_rows, gate_consts, [(GDN_W, f32)] * 7, 2 * CH,
                          side=_side_allgather([send(nm) for nm in _L1_MIX]))
    take(_L1_MIX, got)
    inter_rows = [(a, 0, 1) for a in intra[:6]]
    full_rows = inter_rows + [(fox_o, 0, 1), (z0, Z0_GG // 4, 1)]
    (cat0, gdn_hist), got = _scan_fwd("gdn_scan", _f_gdn_full, full_rows, GDN_H, nchunk, consts=(on0_t,), out_dtype=bf16, out_k=2,
                                      side=_side_allgather([send(nm) for nm in _L1_FFN[1:]]))
    take(_L1_FFN[1:], got)
    x1, hf0, hf0t = _mm("l0_out", cat0, w["l0_w_out"], NN, f32, epi=_epi_res_norm, extra=(x,), consts=(g0f,), outs=_RES_NORM_OUTS)
    (x2, h1, h1t), a0 = _mlp_fwd("l0", x1, hf0, w["l0_w_ff1"], w["l0_w_ff2"], _epi_res_norm, (), (g1m,), _RES_NORM_OUTS)

    z1 = _mm("l1_in", h1, w["l1_w_in"], NN, f32)
    (cat1, hg_hist), got = _scan_fwd("hgrn_scan", _f_hgrn_full, [(z1, 0, 4)], HG_H, nchunk, consts=(lbl, on1_t), out_dtype=bf16,
                                     side=_side_allgather([send(nm) for nm in _L1_FFN[:1]]))
    take(_L1_FFN[:1], got)
    x3, hf1, hf1t = _mm("l1_out", cat1, w["l1_w_out"], NN, f32, epi=_epi_res_norm, extra=(x2,), consts=(g1f,), outs=_RES_NORM_OUTS)
    (dy, dyt, loss_row), a1 = _mlp_fwd("l1", x3, hf1, w["l1_w_ff1"], w["l1_w_ff2"], _epi_loss, (tgt,), (), _LOSS_OUTS)

    (dx3, g["l1_ffn_norm"], dx3t), ga_ff1, ga_ff2 = _mlp_bwd("l1", x3, g1f, w["l1_w_ff1"], w["l1_w_ff2"], hf1t, a1, dy, dyt)
    dcat1 = _mm("l1_out_dx", dx3, w["l1_w_out"], NT, f32)
    ga_out = _mm("l1_out_dw", dx3t, cat1, NN, bf16, out_t=True)
    dz1, dlbl, don1 = _scan_bwd("hgrn_scan_bwd", _f_hgrn_full, [(z1, 0, 4)], hg_hist, dcat1, HG_H, nchunk, dtypes=[bf16],
                                consts=(lbl, on1_t))
    dx2, g["l1_mix_norm"], dx2t = _mm("l1_in_dx", dz1, w["l1_w_in"], NT, f32, epi=_epi_norm_bwd, extra=(x2, dx3), consts=(g1m,),
                                      outs=_NORM_BWD_OUTS)
    ga_in = _mm("l1_in_dw", h1t, dz1, NN, bf16, slab=True)
    g["l1_hgrn_o_norm"] = don1.reshape(HG_H, HD).sum(0)
    g["hgrn_lb_logits"] = dlbl
    gs_a = [by_dev(nm, ga) for nm, ga in zip(_GRAD_A, (ga_ff1, ga_ff2, ga_out, ga_in))]

    (dx1, g["l0_ffn_norm"], dx1t), gb_ff1, gb_ff2 = _mlp_bwd("l0", x1, g0f, w["l0_w_ff1"], w["l0_w_ff2"], hf0t, a0, dx2, dx2t)
    dcat0 = _mm("l0_out_dx", dx1, w["l0_w_out"], NT, f32)
    gb_out = _mm("l0_out_dw", dx1t, cat0, NN, bf16, out_t=True)
    gs_b = [by_dev(nm, ga) for nm, ga in zip(_GRAD_B, (gb_ff1, gb_ff2, gb_out))]
    (*dinter, dfox_o, dgg, don0), r1_a = _scan_bwd("gdn_scan_bwd", _f_gdn_full, full_rows, gdn_hist, dcat0, GDN_H, nchunk,
                                                   consts=(on0_t,), out_k=2, dtypes=[f32] * 7 + [bf16], side=_side_rs_sibling(gs_a))
    pairs_a = pair(_GRAD_A, gs_a, r1_a)
    (dqn, dkn, dfv, dcq, dck), got = _fox_attn_bwd(qn, kn, z0, ccol, crow, dfox_o, t,
                                                   side=_join_sides([_side_rs_chips(pairs_a), _side_rs_sibling(gs_b)]))
    r2_a, r1_b = got[:len(_GRAD_A)], got[len(_GRAD_A):]
    adam(_GRAD_A, gs_a, r1_a, r2_a)
    pairs_b = pair(_GRAD_B, gs_b, r1_b)
    (dqkv, dzs_g, dalog_b, ddt_b), r2_b = _tok_bwd(
        "gdn_intra_bwd", _f_gdn_gated_given, intra_rows + [(intra[6], GDN_W, 0)], gate_consts, [(a, GDN_W) for a in dinter],
        2 * CH, [0, 1, 2, 3], [2, 3], ncat=3, side=_side_rs_chips(pairs_b))
    adam(_GRAD_B, gs_b, r1_b, r2_b)
    dgqkv, dwconv = _tok_bwd("gdn_conv_bwd", _f_conv, conv_rows, conv_consts, [(dqkv, LANES)], t, [0], [0], ncb=12, with_j=True,
                             drow_dtype=bf16)
    dzs_f, dfb = _fox_gate_bwd(z0, fbias, dcq, dck, t)
    dzqk, dgq_t, dgk_t = _tok_bwd("fox_pre_bwd", _f_foxpre, qk_rows, qk_consts, [(dqn, FOX_W), (dkn, FOX_W)], TOK_ROWS, [0], [0, 1],
                                  drow_dtype=bf16)
    dz0 = jnp.concatenate([dzqk, dfv.astype(bf16), dgqkv, dgg, (dzs_g + dzs_f).astype(bf16), jnp.zeros((n, ZW0 - 3712), bf16)], axis=1)
    gs_c = [_col_slabs(_ungroup_w_in0(_mm("l0_in_dw", h0t, dz0, NN, bf16)), ncols["l0_w_in"])]
    r1_c = _rs_sibling(gs_c)
    pairs_c = pair(("l0_w_in",), gs_c, r1_c)
    (dx, g["l0_mix_norm"], _), r2_c = _mm("l0_in_dx", dz0, w_in0, NT, f32, epi=_epi_norm_bwd, extra=(x, dx1), consts=(g0m,),
                                          outs=_NORM_BWD_OUTS, side=_side_rs_chips(pairs_c))
    adam(("l0_w_in",), gs_c, r1_c, r2_c)
    g["l0_fox_q_norm"] = dgq_t.reshape(FOX_H, FOX_D).sum(0)
    g["l0_fox_k_norm"] = dgk_t.reshape(FOX_H, FOX_D).sum(0)
    g["l0_fox_f_bias"] = dfb[0, :FOX_H]
    g["l0_gdn_conv"] = dwconv
    g["l0_gdn_A_log"] = dalog_b.reshape(GDN_H, HD).sum(1)
    g["l0_gdn_dt_bias"] = ddt_b.reshape(GDN_H, HD).sum(1)
    g["l0_gdn_o_norm"] = don0.reshape(GDN_H, HD).sum(0)
    return loss_row, dx, g, big


_NAMES = ("l0_mix_norm", "l0_w_in", "l0_fox_q_norm", "l0_fox_k_norm", "l0_fox_f_bias", "l0_gdn_conv", "l0_gdn_A_log",
          "l0_gdn_dt_bias", "l0_gdn_o_norm", "l0_w_out", "l0_ffn_norm", "l0_w_ff1", "l0_w_ff2", "l1_mix_norm", "l1_w_in",
          "l1_hgrn_o_norm", "l1_w_out", "l1_ffn_norm", "l1_w_ff1", "l1_w_ff2", "hgrn_lb_logits")
_SMALL_NAMES = tuple(nm for nm in _NAMES if nm not in _BIG_NAMES)
_SMALL_ROWS = 16


def _pack_small(vals):
    flat = jnp.concatenate([vals[nm].reshape(-1).astype(f32) for nm in _SMALL_NAMES])
    return jnp.pad(flat, (0, _SMALL_ROWS * D - flat.shape[0])).reshape(_SMALL_ROWS, D)


def _unpack_small(packed, shapes):
    flat = packed.reshape(-1)
    out, off = {}, 0
    for nm in _SMALL_NAMES:
        size = 1
        for s in shapes[nm]:
            size *= s
        out[nm] = flat[off:off + size].reshape(shapes[nm])
        off += size
    return out, off


def kernel(x, l0_mix_norm, l0_w_in, l0_fox_q_norm, l0_fox_k_norm, l0_fox_f_bias, l0_gdn_conv, l0_gdn_A_log, l0_gdn_dt_bias, l0_gdn_o_norm, l0_w_out, l0_ffn_norm, l0_w_ff1, l0_w_ff2, l1_mix_norm, l1_w_in, l1_hgrn_o_norm, l1_w_out, l1_ffn_norm, l1_w_ff1, l1_w_ff2, hgrn_lb_logits, loss_target, m_l0_mix_norm, m_l0_w_in, m_l0_fox_q_norm, m_l0_fox_k_norm, m_l0_fox_f_bias, m_l0_gdn_conv, m_l0_gdn_A_log, m_l0_gdn_dt_bias, m_l0_gdn_o_norm, m_l0_w_out, m_l0_ffn_norm, m_l0_w_ff1, m_l0_w_ff2, m_l1_mix_norm, m_l1_w_in, m_l1_hgrn_o_norm, m_l1_w_out, m_l1_ffn_norm, m_l1_w_ff1, m_l1_w_ff2, m_hgrn_lb_logits, v_l0_mix_norm, v_l0_w_in, v_l0_fox_q_norm, v_l0_fox_k_norm, v_l0_fox_f_bias, v_l0_gdn_conv, v_l0_gdn_A_log, v_l0_gdn_dt_bias, v_l0_gdn_o_norm, v_l0_w_out, v_l0_ffn_norm, v_l0_w_ff1, v_l0_w_ff2, v_l1_mix_norm, v_l1_w_in, v_l1_hgrn_o_norm, v_l1_w_out, v_l1_ffn_norm, v_l1_w_ff1, v_l1_w_ff2, v_hgrn_lb_logits):
    args = dict(zip(_NAMES, (l0_mix_norm, l0_w_in, l0_fox_q_norm, l0_fox_k_norm, l0_fox_f_bias, l0_gdn_conv, l0_gdn_A_log, l0_gdn_dt_bias, l0_gdn_o_norm, l0_w_out, l0_ffn_norm, l0_w_ff1, l0_w_ff2, l1_mix_norm, l1_w_in, l1_hgrn_o_norm, l1_w_out, l1_ffn_norm, l1_w_ff1, l1_w_ff2, hgrn_lb_logits)))
    mom = dict(zip(_NAMES, (m_l0_mix_norm, m_l0_w_in, m_l0_fox_q_norm, m_l0_fox_k_norm, m_l0_fox_f_bias, m_l0_gdn_conv, m_l0_gdn_A_log, m_l0_gdn_dt_bias, m_l0_gdn_o_norm, m_l0_w_out, m_l0_ffn_norm, m_l0_w_ff1, m_l0_w_ff2, m_l1_mix_norm, m_l1_w_in, m_l1_hgrn_o_norm, m_l1_w_out, m_l1_ffn_norm, m_l1_w_ff1, m_l1_w_ff2, m_hgrn_lb_logits)))
    var = dict(zip(_NAMES, (v_l0_mix_norm, v_l0_w_in, v_l0_fox_q_norm, v_l0_fox_k_norm, v_l0_fox_f_bias, v_l0_gdn_conv, v_l0_gdn_A_log, v_l0_gdn_dt_bias, v_l0_gdn_o_norm, v_l0_w_out, v_l0_ffn_norm, v_l0_w_ff1, v_l0_w_ff2, v_l1_mix_norm, v_l1_w_in, v_l1_hgrn_o_norm, v_l1_w_out, v_l1_ffn_norm, v_l1_w_ff1, v_l1_w_ff2, v_hgrn_lb_logits)))
    nb, t, _ = x.shape
    dev = 4 * lax.axis_index("x") + 2 * lax.axis_index("y") + lax.axis_index("c")
    conv_cols = l0_gdn_conv.shape[1]
    loss_row, dx, g, big = _train(x.reshape(nb * t, D), loss_target.reshape(nb * t, D), args, mom, var, t)

    shapes = {nm: args[nm].shape for nm in _SMALL_NAMES}
    gsm = dict(g)
    gsm["l0_gdn_conv"] = jnp.zeros(shapes["l0_gdn_conv"], f32)
    packed = _pack_small(gsm)
    _, used = _unpack_small(packed, shapes)
    flat_extra = jnp.concatenate([jnp.sum(loss_row).reshape(1), g["l0_gdn_conv"].reshape(-1)])
    packed = packed.reshape(-1).at[used:used + flat_extra.shape[0]].set(flat_extra).reshape(_SMALL_ROWS, D)
    (parts,) = _allgather("ag_small", [packed])
    total = _sum_parts(parts).reshape(-1)
    loss = 0.5 * total[used] / D
    conv_g_full = total[used + 1:used + 1 + 4 * NDEV * conv_cols].reshape(4, NDEV * conv_cols)
    conv_g = lax.dynamic_slice(conv_g_full, (0, dev * conv_cols), (4, conv_cols))
    own_vals = {nm: jnp.zeros(shapes[nm], f32) for nm in _SMALL_NAMES}
    own_vals["l0_gdn_conv"] = conv_g
    own_mask = {nm: jnp.zeros(shapes[nm], f32) for nm in _SMALL_NAMES}
    own_mask["l0_gdn_conv"] = jnp.ones(shapes["l0_gdn_conv"], f32)
    small = _adamw_small(parts, _pack_small(args), _pack_small(mom), _pack_small(var), _pack_small(own_mask), _pack_small(own_vals))
    small = [_unpack_small(a, shapes)[0] for a in small]
    small[0]["l0_gdn_conv"] = conv_g

    outs = [loss, dx.reshape(nb, t, D)]
    for k in range(4):
        outs += [big[nm][k] if nm in _BIG_NAMES else small[k][nm] for nm in _NAMES]
    return tuple(outs)
```

```python
import functools

import jax
import jax.numpy as jnp
from jax import lax
from jax.experimental import pallas as pl
from jax.experimental.pallas import tpu as pltpu

f32, bf16 = jnp.float32, jnp.bfloat16
NN = (((1,), (0,)), ((), ()))
NT = (((1,), (1,)), ((), ()))
TN = (((0,), (0,)), ((), ()))
MESH = pl.DeviceIdType.MESH
S = jax.ShapeDtypeStruct

EPS = 1e-6
D = 1024
LANES = 128
FOX_H, FOX_D, FOX_W = 8, 64, 512
GDN_H, HD, GDN_W = 4, 128, 512
HG_H = 8
CH = 64
TOK_ROWS = 256
ZW0 = 3840
NDEV = 8
ADAM_LR, ADAM_B1, ADAM_B2, ADAM_EPS, ADAM_WD, ADAM_STEP = 0.001, 0.9, 0.999, 1e-08, 0.01, 10

Z0_FV, Z0_GQKV, Z0_GG, Z0_SMALL = 8, 12, 24, 28


def _dot(a, b, dims=NN):
    return lax.dot_general(a, b, dims, preferred_element_type=f32)


def _iota2(shape, axis):
    return lax.broadcasted_iota(jnp.int32, shape, axis)


def _split3(x):
    x1 = x.astype(bf16)
    r = x - x1.astype(f32)
    x2 = r.astype(bf16)
    return x1, x2, (r - x2.astype(f32)).astype(bf16)


def _dot_sel(a, b, dims=NN, exact_lhs=False):
    if exact_lhs:
        return sum(_dot(a.astype(bf16), piece, dims) for piece in _split3(b))
    return sum(_dot(piece, b.astype(bf16), dims) for piece in _split3(a))


@jax.custom_vjp
def _sel_rhs(a, b):
    return _dot_sel(a, b)


_sel_rhs.defvjp(lambda a, b: (_dot_sel(a, b), b), lambda b, g: (_dot_sel(g, b, NT), jnp.zeros_like(b)))


@jax.custom_vjp
def _sel_lhs(a, x):
    return _dot_sel(a, x, exact_lhs=True)


_sel_lhs.defvjp(lambda a, x: (_dot_sel(a, x, exact_lhs=True), a), lambda a, g: (jnp.zeros_like(a), _dot_sel(a, g, TN, exact_lhs=True)))


class _Side:
    def __init__(self, ins, out_shapes, scratch, start, finish):
        self.ins, self.out_shapes, self.scratch, self.start, self.finish = list(ins), list(out_shapes), list(scratch), start, finish


def _join_sides(sides):
    def split(refs, counts):
        out, off = [], 0
        for c in counts:
            out.append(refs[off:off + c])
            off += c
        return out

    ni, no, ns = ([len(getattr(sd, a)) for sd in sides] for a in ("ins", "out_shapes", "scratch"))

    def run(which):
        def go(ins, outs, sems):
            for sd, i, o, c in zip(sides, split(ins, ni), split(outs, no), split(sems, ns)):
                getattr(sd, which)(i, o, c)
        return go

    return _Side(sum((sd.ins for sd in sides), []), sum((sd.out_shapes for sd in sides), []),
                 sum((sd.scratch for sd in sides), []), run("start"), run("finish"))


def _pcall(side, body, *, name, grid, in_specs, out_specs, out_shape, scratch_shapes=(), compiler_params=None):
    if side is None:
        return pl.pallas_call(body, name=name, grid=grid, in_specs=in_specs, out_specs=out_specs, out_shape=out_shape,
                              scratch_shapes=scratch_shapes, compiler_params=compiler_params)
    single = not isinstance(out_shape, (list, tuple))
    ospecs, oshape = ([out_specs], [out_shape]) if single else (list(out_specs), list(out_shape))
    nin, nout, nscr = len(in_specs), len(ospecs), len(scratch_shapes)
    si, so = len(side.ins), len(side.out_shapes)

    def wrapped(*refs):
        o0 = nin + si
        c0 = o0 + nout + so
        sins, souts, ssems = refs[nin:o0], refs[o0 + nout:c0], refs[c0 + nscr:]
        ids = [pl.program_id(a) for a in range(len(grid))]
        first = functools.reduce(jnp.logical_and, [i == 0 for i in ids])
        last = functools.reduce(jnp.logical_and, [i == g - 1 for i, g in zip(ids, grid)])

        @pl.when(first)
        def _():
            side.start(sins, souts, ssems)

        body(*refs[:nin], *refs[o0:o0 + nout], *refs[c0:c0 + nscr])

        @pl.when(last)
        def _():
            side.finish(sins, souts, ssems)

    call = pl.pallas_call(
        wrapped, name=name, grid=grid, in_specs=list(in_specs) + _hbm_specs(si), out_specs=ospecs + _hbm_specs(so),
        out_shape=oshape + side.out_shapes, scratch_shapes=list(scratch_shapes) + side.scratch,
        compiler_params=pltpu.CompilerParams(dimension_semantics=("arbitrary",) * len(grid),
                                             vmem_limit_bytes=getattr(compiler_params, "vmem_limit_bytes", None)))

    def run(*args):
        res = call(*args, *side.ins)
        return (res[0] if single else list(res[:nout])), list(res[nout:])

    return run


def _tok_specs(rows, consts, tm):
    specs = []
    for (_, w, base) in rows:
        specs.append(pl.BlockSpec((tm, w), functools.partial(lambda j, i, b: (i, b + j), b=base)))
    for (arr, w, base) in consts:
        if w is None:
            specs.append(pl.BlockSpec(arr.shape, lambda j, i: (0, 0)))
        else:
            specs.append(pl.BlockSpec((arr.shape[0], w), functools.partial(lambda j, i, b: (0, b + j), b=base)))
    return specs


def _tok_fwd(name, f, rows, consts, outs, tm, ncb=1, with_j=False, also_t=(), side=None):
    n = rows[0][0].shape[0]
    nin = len(rows) + len(consts)
    nout = len(outs)

    def body(*refs):
        ins = [r[...] for r in refs[:nin]]
        vals = f(pl.program_id(0), *ins) if with_j else f(*ins)
        for r, v in zip(refs[nin:nin + nout], vals):
            r[...] = v.astype(r.dtype)
        for r, k in zip(refs[nin + nout:], also_t):
            r[...] = vals[k].T.astype(r.dtype)

    return _pcall(
        side, body, name=name, grid=(ncb, n // tm),
        in_specs=_tok_specs(rows, consts, tm),
        out_specs=[pl.BlockSpec((tm, w), lambda j, i: (i, j)) for (w, _) in outs]
        + [pl.BlockSpec((outs[k][0], tm), lambda j, i: (0, i)) for k in also_t],
        out_shape=[S((n, w * ncb), dt) for (w, dt) in outs] + [S((outs[k][0], n), bf16) for k in also_t],
        compiler_params=pltpu.CompilerParams(dimension_semantics=("parallel", "parallel")),
    )(*[r[0] for r in rows], *[c[0] for c in consts])


def _tok_bwd(name, f, rows, consts, cots, tm, drow, dconst, ncb=1, with_j=False, addto=None, also_t=(), drow_dtype=f32, side=None, ncat=0):
    n = rows[0][0].shape[0]
    nr, nc, nct = len(rows), len(consts), len(cots)
    addto = addto or {}
    add_keys = sorted(addto)
    nadd = len(add_keys)

    def body(*refs):
        ins = [r[...] for r in refs[:nr + nc]]
        cot = [r[...] for r in refs[nr + nc:nr + nc + nct]]
        adds = refs[nr + nc + nct:nr + nc + nct + nadd]
        outs = refs[nr + nc + nct + nadd:]
        pos = list(drow) + [nr + k for k in dconst]

        def g(*dargs):
            full = list(ins)
            for p, a in zip(pos, dargs):
                full[p] = a
            return tuple(f(pl.program_id(0), *full) if with_j else f(*full))

        vals, vjp = jax.vjp(g, *[ins[p] for p in pos])
        grads = vjp(tuple(c.astype(v.dtype) for c, v in zip(cot, vals)))
        off = 0
        for k in range(len(drow)):
            gk = grads[k]
            if k in addto:
                gk = gk + adds[add_keys.index(k)][...]
            if k < ncat:
                outs[0][:, off:off + gk.shape[1]] = gk.astype(outs[0].dtype)
                off += gk.shape[1]
            else:
                outs[k - skip][...] = gk.astype(outs[k - skip].dtype)
            if k in also_t:
                tref = outs[len(drow) - skip + len(dconst) + list(also_t).index(k)]
                tref[...] = gk.T.astype(tref.dtype)
        first = pl.program_id(1) == 0
        for k in range(len(dconst)):
            ref = outs[len(drow) - skip + k]

            @pl.when(first)
            def _():
                ref[...] = jnp.zeros_like(ref)

            ref[...] += grads[len(drow) + k]

    skip = max(ncat - 1, 0)
    in_specs = _tok_specs(rows, consts, tm)
    in_specs += [pl.BlockSpec((tm, w), lambda j, i: (i, j)) for (_, w) in cots]
    in_specs += [pl.BlockSpec((tm, rows[drow[k]][1]), lambda j, i: (i, j)) for k in add_keys]
    dts = drow_dtype if isinstance(drow_dtype, (list, tuple)) else [drow_dtype] * len(drow)
    widths = [rows[k][1] for k in drow]
    if ncat:
        widths, dts = [sum(widths[:ncat])] + widths[ncat:], [dts[0]] + list(dts[ncat:])
    out_specs = [pl.BlockSpec((tm, wd), lambda j, i: (i, j)) for wd in widths]
    out_shape = [S((n, wd * ncb), dt) for wd, dt in zip(widths, dts)]
    for k in dconst:
        arr, w, _ = consts[k]
        if w is None:
            out_specs.append(pl.BlockSpec(arr.shape, lambda j, i: (0, 0)))
            out_shape.append(S(arr.shape, f32))
        else:
            out_specs.append(pl.BlockSpec((arr.shape[0], w), lambda j, i: (0, j)))
            out_shape.append(S((arr.shape[0], w * ncb), f32))
    for k in also_t:
        out_specs.append(pl.BlockSpec((rows[drow[k]][1], tm), lambda j, i: (0, i)))
        out_shape.append(S((rows[drow[k]][1], n), bf16))
    return _pcall(
        side, body, name=name, grid=(ncb, n // tm), in_specs=in_specs, out_specs=out_specs, out_shape=out_shape,
        compiler_params=pltpu.CompilerParams(dimension_semantics=("parallel", "arbitrary")),
    )(*[r[0] for r in rows], *[c[0] for c in consts], *[c[0] for c in cots], *[addto[k] for k in add_keys])


SCAN_CHUNKS = 2


def _scan_fwd(name, f, rows, nh, nchunk, side=None, consts=(), out_dtype=f32, out_k=1):
    n = rows[0][0].shape[0]
    nb = n // (CH * nchunk)
    nin, nco = len(rows), len(consts)
    w = nh * HD
    cps = SCAN_CHUNKS

    def body(*refs):
        o_ref, hist_ref, st = refs[nin + nco], refs[nin + nco + 1], refs[nin + nco + 2]

        @pl.when(pl.program_id(0) == 0)
        def _():
            st[...] = jnp.zeros_like(st)

        cvals = [r[...] for r in refs[nin:nin + nco]]
        state = st[...]
        for sub in range(cps):
            rr = pl.ds(sub * CH, CH)
            hist_ref[sub] = state.astype(hist_ref.dtype)
            tiles = [r[:, rr, :].reshape(nb * CH, r.shape[2]) for r in refs[:nin]]
            o, state = f(*tiles, *cvals, state)
            o_ref[:, rr, :] = o.reshape(nb, CH, out_k * w).astype(o_ref.dtype)
        st[...] = state

    seq3 = lambda a: a.reshape(nb, nchunk * CH, a.shape[1])
    res = _pcall(
        side, body, name=name, grid=(nchunk // cps,),
        in_specs=[pl.BlockSpec((nb, cps * CH, k * w), functools.partial(lambda c, base: (0, c, base), base=b)) for (_, b, k) in rows]
        + [pl.BlockSpec(c.shape, lambda c: (0, 0)) for c in consts],
        out_specs=[pl.BlockSpec((nb, cps * CH, out_k * w), lambda c: (0, c, 0)), pl.BlockSpec((cps, nb * w, HD), lambda c: (c, 0, 0))],
        out_shape=[S((nb, nchunk * CH, out_k * w), out_dtype), S((nchunk, nb * w, HD), bf16)],
        scratch_shapes=[pltpu.VMEM((nb * w, HD), f32)],
        compiler_params=pltpu.CompilerParams(dimension_semantics=("arbitrary",)),
    )(*[seq3(r[0]) for r in rows], *consts)
    (o, hist), extra = (res, None) if side is None else res
    out = [o.reshape(n, out_k * w), hist]
    return out if side is None else (out, extra)


def _scan_bwd(name, f, rows, hist, do, nh, nchunk, side=None, dtypes=None, consts=(), out_k=1, cps=SCAN_CHUNKS):
    n = rows[0][0].shape[0]
    nb = n // (CH * nchunk)
    nin, nco = len(rows), len(consts)
    w = nh * HD
    nstep = nchunk // cps

    def body(*refs):
        hist_ref, do_ref = refs[nin + nco], refs[nin + nco + 1]
        outs = refs[nin + nco + 2:nin + nco + 2 + nin]
        couts = refs[nin + nco + 2 + nin:nin + nco + 2 + nin + nco]
        ds = refs[nin + nco + 2 + nin + nco]

        @pl.when(pl.program_id(0) == 0)
        def _():
            ds[...] = jnp.zeros_like(ds)
            for c in couts:
                c[...] = jnp.zeros_like(c)

        cvals = [r[...] for r in refs[nin:nin + nco]]
        dstate = ds[...]
        for sub in reversed(range(cps)):
            rr = pl.ds(sub * CH, CH)
            tiles = [r[:, rr, :].reshape(nb * CH, r.shape[2]) for r in refs[:nin]]
            _, vjp = jax.vjp(f, *tiles, *cvals, hist_ref[sub].astype(f32))
            grads = vjp((do_ref[:, rr, :].reshape(nb * CH, out_k * w), dstate))
            for r, gk in zip(outs, grads[:nin]):
                r[:, rr, :] = gk.reshape(nb, CH, r.shape[2]).astype(r.dtype)
            for c, gk in zip(couts, grads[nin:nin + nco]):
                c[...] += gk
            dstate = grads[nin + nco]
        ds[...] = dstate

    seq3 = lambda a: a.reshape(nb, nchunk * CH, a.shape[1])
    rev = lambda c, base: (0, nstep - 1 - c, base)
    res = _pcall(
        side, body, name=name, grid=(nstep,),
        in_specs=[pl.BlockSpec((nb, cps * CH, k * w), functools.partial(rev, base=b)) for (_, b, k) in rows]
        + [pl.BlockSpec(c.shape, lambda c: (0, 0)) for c in consts]
        + [pl.BlockSpec((cps, nb * w, HD), lambda c: (nstep - 1 - c, 0, 0)),
           pl.BlockSpec((nb, cps * CH, out_k * w), functools.partial(rev, base=0))],
        out_specs=[pl.BlockSpec((nb, cps * CH, k * w), functools.partial(rev, base=0)) for (_, _, k) in rows]
        + [pl.BlockSpec(c.shape, lambda c: (0, 0)) for c in consts],
        out_shape=[S((nb, nchunk * CH, k * w), dt) for (_, _, k), dt in zip(rows, dtypes or [f32] * nin)]
        + [S(c.shape, f32) for c in consts],
        scratch_shapes=[pltpu.VMEM((nb * w, HD), f32)],
        compiler_params=pltpu.CompilerParams(dimension_semantics=("arbitrary",)),
    )(*[seq3(r[0]) for r in rows], *consts, hist, seq3(do))
    outs, extra = (res, None) if side is None else res
    outs = [o.reshape(n, o.shape[2]) for o in outs[:nin]] + list(outs[nin:])
    return outs if side is None else (outs, extra)


_VMEM_LIMIT = 56 * 2 ** 20
_VMEM_TILE_BUDGET = 40 * 2 ** 20


def _mm_tiles(m, n, k, sa, sb, so, sx, a_f32, b_f32, tn_fixed):
    best = None
    for tm in (1024, 512, 256, 128, 64):
        for tn in ((tn_fixed,) if tn_fixed else (1024, 768, 512, 384, 256, 128)):
            if m % tm or n % tn:
                continue
            need = 2 * (tm * k * sa + k * tn * sb + tm * tn * (so + sx)) + tm * tn * 4
            need += tm * k * (2 if sa == 4 else 0) + k * tn * (2 if sb == 4 else 0)
            need += tm * k * (4 if a_f32 else 0) + k * tn * (4 if b_f32 else 0)
            if need <= _VMEM_TILE_BUDGET and (best is None or (tm * tn, tm) > best[0]):
                best = ((tm * tn, tm), tm, tn)
    return best[1], best[2]


def _mm(name, a, b, dims, out_dtype, a_fn=None, b_fn=None, epi=None, extra=(), consts=(), outs=None, out_t=False, slab=False,
        side=None):
    m, kk = a.shape
    gathered = b.ndim == 3
    if gathered:
        nn = NDEV * _COLW if dims is NN else b.shape[1]
    else:
        nn = b.shape[1] if dims is NN else b.shape[0]
    kinds = [("tile", out_dtype)] if outs is None else list(outs)
    so = sum(jnp.dtype(dt).itemsize for kd, dt in kinds if kd != "rows")
    sx = sum(e.dtype.itemsize for e in extra)
    full_rows = bool(consts) or any(kd == "rows" for kd, _ in kinds)
    tm, tn = _mm_tiles(m, nn, kk, a.dtype.itemsize, b.dtype.itemsize, so, sx,
                       a_fn is not None, b_fn is not None,
                       _COLW if slab else (nn if full_rows else None))
    gblocks = tn // _COLW if (gathered and dims is NN) else 0
    assert not gblocks or (outs is None and not extra and not out_t and not slab and b_fn is None)
    nex, nco = len(extra), len(consts)

    def body(a_ref, b_ref, *rest):
        av = a_ref[...]
        if a_fn is not None:
            av = a_fn(av.astype(f32))
        av = av.astype(bf16)
        if gblocks:
            for gb in range(gblocks):
                part = _dot(av, b_ref[gb], NN)
                rest[nex + nco][:, _COLW * gb:_COLW * (gb + 1)] = (part if epi is None else epi(part)).astype(rest[nex + nco].dtype)
            return
        if gathered:
            acc = sum(_dot(av[:, _COLW * d:_COLW * (d + 1)], b_ref[d], NT) for d in range(NDEV))
        else:
            bv = b_ref[...]
            if b_fn is not None:
                bv = b_fn(bv.astype(f32))
            acc = _dot(av, bv.astype(bf16), dims)
        if epi is not None:
            acc = epi(acc, *[r[...] for r in rest[:nex + nco]])
        vals = acc if isinstance(acc, tuple) else (acc,)
        for (kd, _), o_ref, val in zip(kinds, rest[nex + nco:], vals):
            if kd == "rows":
                @pl.when(pl.program_id(0) == 0)
                def _():
                    o_ref[...] = jnp.zeros_like(o_ref)

                o_ref[...] += val
            elif kd == "tile_t" or out_t:
                o_ref[...] = val.T.astype(o_ref.dtype)
            elif slab:
                o_ref[0] = val.astype(o_ref.dtype)
            else:
                o_ref[...] = val.astype(o_ref.dtype)

    if gathered:
        bspec = (pl.BlockSpec((gblocks, kk, _COLW), lambda i, j: (j, 0, 0)) if dims is NN
                 else pl.BlockSpec((NDEV, tn, _COLW), lambda i, j: (0, j, 0)))
    else:
        bspec = pl.BlockSpec((kk, tn), lambda i, j: (0, j)) if dims is NN else pl.BlockSpec((tn, kk), lambda i, j: (j, 0))
    out_specs, out_shape = [], []
    for kd, dt in kinds:
        if kd == "rows":
            out_specs.append(pl.BlockSpec((1, nn), lambda i, j: (0, 0)))
            out_shape.append(S((1, nn), dt))
        elif kd == "tile_t" or out_t:
            out_specs.append(pl.BlockSpec((tn, tm), lambda i, j: (j, i)))
            out_shape.append(S((nn, m), dt))
        elif slab:
            out_specs.append(pl.BlockSpec((1, tm, tn), lambda i, j: (j, i, 0)))
            out_shape.append(S((nn // tn, m, tn), dt))
        else:
            out_specs.append(pl.BlockSpec((tm, tn), lambda i, j: (i, j)))
            out_shape.append(S((m, nn), dt))
    if outs is None:
        out_specs, out_shape = out_specs[0], out_shape[0]
    sem = ("arbitrary", "arbitrary") if any(kd == "rows" for kd, _ in kinds) else ("parallel", "parallel")
    return _pcall(
        side, body, name=name, grid=(m // tm, nn // tn),
        in_specs=[pl.BlockSpec((tm, kk), lambda i, j: (i, 0)), bspec]
        + [pl.BlockSpec((tm, tn), lambda i, j: (i, j)) for _ in extra]
        + [pl.BlockSpec(c.shape, lambda i, j: (0, 0)) for c in consts],
        out_specs=out_specs, out_shape=out_shape,
        compiler_params=pltpu.CompilerParams(dimension_semantics=sem, vmem_limit_bytes=_VMEM_LIMIT),
    )(a, b, *extra, *consts)


def _f_norm(x, g):
    return (x * lax.rsqrt(jnp.mean(x * x, axis=-1, keepdims=True) + EPS) * g,)


def _f_foxpre(zqk, gq, gk, pm):
    def nrm(t, g):
        return t * lax.rsqrt(_sel_rhs(t * t, pm) + EPS) * g
    return nrm(zqk[:, :FOX_W], gq), nrm(zqk[:, FOX_W:], gk)


def _chunk_cumsum(x):
    n = x.shape[0]
    r, c = _iota2((n, n), 0), _iota2((n, n), 1)
    tri = jnp.logical_and(r >= c, (r // CH) == (c // CH)).astype(f32)
    return _sel_lhs(tri, x)


def _f_gdngate(zs, eb, ea, alog_b, dt_b):
    beta = jax.nn.sigmoid(_sel_rhs(zs, eb))
    la = -jnp.exp(alog_b) * jax.nn.softplus(_sel_rhs(zs, ea) + dt_b)
    return beta, _chunk_cumsum(la)


def _f_conv(j, x, w):
    t = x.shape[0]
    y = x * w[3:4, :]
    for jj in range(3):
        sh = 3 - jj
        xs = jnp.concatenate([jnp.zeros((sh, x.shape[1]), f32), x[:t - sh, :]], axis=0)
        y = y + xs * w[jj:jj + 1, :]
    y = jax.nn.silu(y)
    yn = y * lax.rsqrt(jnp.sum(y * y, axis=-1, keepdims=True) + EPS)
    return (jnp.where(j < 2 * GDN_H, yn, y),)


def _head_rms(o, nh):
    outs = []
    for h in range(nh):
        oh = o[:, HD * h:HD * (h + 1)]
        outs.append(oh * lax.rsqrt(jnp.mean(oh * oh, axis=-1, keepdims=True) + EPS))
    return jnp.concatenate(outs, axis=1)


def _f_post0(fox_o, o, gg, on):
    return (jnp.concatenate([fox_o, _head_rms(o, GDN_H) * on * jax.nn.silu(gg)], axis=1),)


def _f_post1(o, zg, on):
    return (_head_rms(o, HG_H) * on * jax.nn.silu(zg),)


def _f_hpre(zqf, lbl):
    lb = jax.nn.sigmoid(lbl[1:2, :] - lbl[0:1, :])
    fg = lb + (1.0 - lb) * jax.nn.sigmoid(zqf[:, D:])
    return jax.nn.silu(zqf[:, :D]), 1.0 - fg, _chunk_cumsum(jnp.log(fg))


def _dotb(a, b, dims=NN):
    return _dot(a.astype(bf16), b.astype(bf16), dims)


def _dot3(a, b):
    ah, bh = a.astype(bf16), b.astype(bf16)
    al, bl = (a - ah.astype(f32)).astype(bf16), (b - bh.astype(f32)).astype(bf16)
    return _dot(ah, bh) + (_dot(ah, bl) + _dot(al, bh))


def _split(t, nh):
    return [t[CH * ck:CH * (ck + 1), HD * h:HD * (h + 1)] for ck in range(t.shape[0] // CH) for h in range(nh)]


def _merge(units, nh):
    return jnp.concatenate([jnp.concatenate(units[i:i + nh], axis=1) for i in range(0, len(units), nh)], axis=0)


def _inv_impl(amats):
    n = amats[0].shape[0]
    eye = jnp.where(_iota2((n, n), 0) == _iota2((n, n), 1), 1.0, 0.0).astype(f32)
    xs, ps = [eye - a for a in amats], list(amats)
    for _ in range(max(1, (n - 1).bit_length()) - 1):
        ps = [_dotb(p, p) for p in ps]
        xs = [x + _dotb(x, p) for x, p in zip(xs, ps)]
    for _ in range(3):
        rs = [eye - x - _dot3(a, x) for a, x in zip(amats, xs)]
        xs = [x + _dotb(x, r) for x, r in zip(xs, rs)]
    return tuple(xs)


@jax.custom_vjp
def _inv_unit_lower(amats):
    return _inv_impl(amats)


def _inv_fwd(amats):
    xs = _inv_impl(amats)
    return xs, xs


def _inv_bwd(xs, dxs):
    return (tuple(-_dotb(_dotb(x, dx, TN), x, NT) for x, dx in zip(xs, dxs)),)


_inv_unit_lower.defvjp(_inv_fwd, _inv_bwd)


@jax.custom_vjp
def _inv_given(amats, xs):
    return xs


def _inv_given_fwd(amats, xs):
    return xs, xs


def _inv_given_bwd(xs, dxs):
    return _inv_bwd(xs, dxs)[0], tuple(jnp.zeros_like(x) for x in xs)


_inv_given.defvjp(_inv_given_fwd, _inv_given_bwd)


def _f_gdn_intra(q, k, v, bb, gb, tinv_p=None):
    qs, ks, vs, bs, gs = (_split(t, GDN_H) for t in (q, k, v, bb, gb))
    r, cc = _iota2((CH, CH), 0), _iota2((CH, CH), 1)
    causal, strict = r >= cc, r > cc
    beta, g, gl = [b[:, :1] for b in bs], [x[:, :1] for x in gs], [x[CH - 1:CH, :1] for x in gs]
    decay = [jnp.exp(jnp.where(causal, x[:, :CH] - x[:, :CH].T, -jnp.inf)) for x in gs]
    kb = [ki * bi for ki, bi in zip(ks, beta)]
    amat = [jnp.where(strict, _dotb(kbi, ki, NT) * di, 0.0) for kbi, ki, di in zip(kb, ks, decay)]
    if tinv_p is None:
        tinv = _inv_unit_lower(tuple(amat))
    else:
        tinv = _inv_given(tuple(amat), tuple(x[:, :CH] for x in _split(tinv_p, GDN_H)))
    rhs = [jnp.concatenate([vi * bi, kbi * jnp.exp(gi)], axis=1) for vi, bi, kbi, gi in zip(vs, beta, kb, g)]
    uw = [_dotb(ti, ri) for ti, ri in zip(tinv, rhs)]
    qsc = [qi * (HD ** -0.5) for qi in qs]
    qk = [jnp.where(causal, _dotb(qi, ki, NT) * di, 0.0) for qi, ki, di in zip(qsc, ks, decay)]
    outs = ([x[:, :HD] for x in uw], [x[:, HD:] for x in uw],
            [jnp.concatenate([x, jnp.zeros_like(x)], axis=1) for x in qk],
            [qi * jnp.exp(gi) for qi, gi in zip(qsc, g)],
            [ki * jnp.exp(gli - gi) for ki, gli, gi in zip(ks, gl, g)],
            [jnp.broadcast_to(gli, (CH, HD)) for gli in gl])
    if tinv_p is None:
        outs += ([jnp.concatenate([x, jnp.zeros_like(x)], axis=1) for x in tinv],)
    return tuple(_merge(o, GDN_H) for o in outs)


def _f_gdn_gated(q, k, v, zs, eb, ea, alog_b, dt_b):
    return _f_gdn_intra(q, k, v, *_f_gdngate(zs, eb, ea, alog_b, dt_b))


def _f_gdn_gated_given(q, k, v, zs, tinv_p, eb, ea, alog_b, dt_b):
    return _f_gdn_intra(q, k, v, *_f_gdngate(zs, eb, ea, alog_b, dt_b), tinv_p=tinv_p)


def _f_gdn_inter(u, w, qkp, qd, kd, glb, st):
    us, ws, qks, qds, kds, gls = (_split(t, GDN_H) for t in (u, w, qkp, qd, kd, glb))
    sts = [st[HD * i:HD * (i + 1), :] for i in range(len(us))]
    vn = [ui - _dotb(wi, si) for ui, wi, si in zip(us, ws, sts)]
    o = [_dotb(qi, si) + _dotb(xi[:, :CH], vi) for qi, si, xi, vi in zip(qds, sts, qks, vn)]
    s2 = [si * jnp.exp(gi[:1, :1]) + _dotb(ki, vi, TN) for si, gi, ki, vi in zip(sts, gls, kds, vn)]
    return _merge(o, GDN_H), jnp.concatenate(s2, axis=0)


def _f_gdn_full(u, w, qkp, qd, kd, glb, fox_o, gg, on, st):
    o, s2 = _f_gdn_inter(u, w, qkp, qd, kd, glb, st)
    return _f_post0(fox_o, o, gg, on)[0], s2


def _f_hgrn_chunk(q, k, b, v, st):
    qs, ks, bs, vs = (_split(t, HG_H) for t in (q, k, b, v))
    sts = [st[HD * i:HD * (i + 1), :] for i in range(len(qs))]
    causal = _iota2((CH, CH), 0) >= _iota2((CH, CH), 1)
    bl, bm = [x[CH - 1:CH, :] for x in bs], [x[CH // 2 - 1:CH // 2, :] for x in bs]
    a = [jnp.where(causal, _dotb(qi * jnp.exp(bi - mi), ki * jnp.exp(mi - bi), NT), 0.0)
         for qi, ki, bi, mi in zip(qs, ks, bs, bm)]
    o = [_dotb(qi * jnp.exp(bi), si, NT) + _dotb(ai, vi) for qi, bi, si, ai, vi in zip(qs, bs, sts, a, vs)]
    s2 = [si * jnp.exp(li) + _dotb(vi, ki * jnp.exp(li - bi), TN) for si, li, vi, ki, bi in zip(sts, bl, vs, ks, bs)]
    return _merge(o, HG_H), jnp.concatenate(s2, axis=0)


def _f_hgrn_full(z, lbl, on, st):
    o, s2 = _f_hgrn_chunk(*_f_hpre(z[:, :2 * D], lbl), z[:, 2 * D:3 * D], st)
    return _f_post1(o, z[:, 3 * D:], on)[0], s2


def _fox_gate_fwd(z0, fbias, t, tc=256):
    n = z0.shape[0]
    nt = t // tc

    def body(zs_ref, b_ref, ccol_ref, crow_ref, carry):
        @pl.when(pl.program_id(1) == 0)
        def _():
            carry[...] = jnp.zeros_like(carry)

        ls = jnp.where(_iota2((tc, LANES), 1) < FOX_H, jax.nn.log_sigmoid(zs_ref[...] + b_ref[...]), 0.0)
        tri = (_iota2((tc, tc), 0) >= _iota2((tc, tc), 1)).astype(f32)
        c = _dot_sel(tri, ls, exact_lhs=True) + carry[...]
        carry[...] = c[tc - 1:tc, :]
        ccol_ref[...] = c
        crow_ref[0] = c.T[:FOX_H, :]

    return pl.pallas_call(
        body, name="fox_gate_fwd", grid=(n // t, nt),
        in_specs=[pl.BlockSpec((tc, LANES), lambda b, i: (b * nt + i, Z0_SMALL)), pl.BlockSpec((1, LANES), lambda b, i: (0, 0))],
        out_specs=[pl.BlockSpec((tc, LANES), lambda b, i: (b * nt + i, 0)), pl.BlockSpec((1, FOX_H, tc), lambda b, i: (b, 0, i))],
        out_shape=[S((n, LANES), f32), S((n // t, FOX_H, t), f32)],
        scratch_shapes=[pltpu.VMEM((1, LANES), f32)],
        compiler_params=pltpu.CompilerParams(dimension_semantics=("parallel", "arbitrary")),
    )(z0, fbias)


def _fox_gate_bwd(z0, fbias, dcq, dck, t, tc=256):
    n = z0.shape[0]
    nt = t // tc

    def body(zs_ref, b_ref, dcq_ref, dck_ref, dz_ref, db_ref, carry):
        first = jnp.logical_and(pl.program_id(0) == 0, pl.program_id(1) == 0)

        @pl.when(pl.program_id(1) == 0)
        def _():
            carry[...] = jnp.zeros_like(carry)

        @pl.when(first)
        def _():
            db_ref[...] = jnp.zeros_like(db_ref)

        dc = dcq_ref[0] + dcq_ref[1] + dcq_ref[2] + dcq_ref[3]
        drow = dck_ref[0, 0] + dck_ref[1, 0] + dck_ref[2, 0] + dck_ref[3, 0]
        eye = (_iota2((FOX_H, LANES), 0) == _iota2((FOX_H, LANES), 1)).astype(f32)
        dc = dc + _dot_sel(drow, eye, TN)
        triu = (_iota2((tc, tc), 0) <= _iota2((tc, tc), 1)).astype(f32)
        dls = _dot_sel(triu, dc, exact_lhs=True) + carry[...]
        carry[...] = dls[0:1, :]
        x = zs_ref[...] + b_ref[...]
        dz = jnp.where(_iota2((tc, LANES), 1) < FOX_H, dls * jax.nn.sigmoid(-x), 0.0)
        dz_ref[...] = dz
        db_ref[...] += jnp.sum(dz, axis=0, keepdims=True)

    def rev(b, i):
        return b * nt + (nt - 1 - i)

    return pl.pallas_call(
        body, name="fox_gate_bwd", grid=(n // t, nt),
        in_specs=[pl.BlockSpec((tc, LANES), lambda b, i: (rev(b, i), Z0_SMALL)), pl.BlockSpec((1, LANES), lambda b, i: (0, 0)),
                  pl.BlockSpec((4, tc, LANES), lambda b, i: (0, rev(b, i), 0)),
                  pl.BlockSpec((4, 1, FOX_H, tc), lambda b, i: (0, b, 0, nt - 1 - i))],
        out_specs=[pl.BlockSpec((tc, LANES), lambda b, i: (rev(b, i), 0)), pl.BlockSpec((1, LANES), lambda b, i: (0, 0))],
        out_shape=[S((n, LANES), f32), S((1, LANES), f32)],
        scratch_shapes=[pltpu.VMEM((1, LANES), f32)],
        compiler_params=pltpu.CompilerParams(dimension_semantics=("arbitrary", "arbitrary")),
    )(z0, fbias, dcq, dck)


def _fox_scores(hh, p, i, tq, q, k, ccol, crow):
    kmax = k.shape[0]
    lane = _iota2((1, LANES), 1)
    mh = (lane // FOX_D) == hh
    h = 2 * p + hh
    qh = jnp.where(mh, q, 0.0).astype(bf16)
    s = _dot(qh, k, NT) * (FOX_D ** -0.5)
    cq = jnp.sum(jnp.where(lane == h, ccol, 0.0), axis=1, keepdims=True)
    ck = jnp.sum(jnp.where(_iota2((FOX_H, 1), 0) == h, crow, 0.0), axis=0, keepdims=True)
    causal = _iota2((1, kmax), 1) <= (i * tq + _iota2((tq, 1), 0))
    s = jnp.where(causal, s + cq - ck, -jnp.inf)
    pe = jnp.exp(s - jnp.max(s, axis=1, keepdims=True))
    return mh, qh, pe, jnp.sum(pe, axis=1, keepdims=True)


def _fox_attn_fwd(qn, kn, z0, ccol, crow, t, tq=256, side=None):
    n = qn.shape[0]
    nq = t // tq

    def body(q_ref, k_ref, v_ref, ccol_ref, crow_ref, o_ref):
        p = pl.program_id(1)
        k, v, crow = k_ref[...].astype(bf16), v_ref[...].astype(bf16), crow_ref[0]
        for i in range(nq):
            rows, kmax = pl.ds(i * tq, tq), (i + 1) * tq
            q, cc = q_ref[rows, :], ccol_ref[rows, :]
            acc = jnp.zeros((tq, LANES), f32)
            for hh in range(2):
                mh, _, pe, l = _fox_scores(hh, p, i, tq, q, k[:kmax], cc, crow[:, :kmax])
                acc = jnp.where(mh, _dot(pe.astype(bf16), v[:kmax]) / l, acc)
            o_ref[rows, :] = acc

    seq = lambda b, p: (b, p)
    return _pcall(
        side, body, name="fox_attn_fwd", grid=(n // t, FOX_H // 2),
        in_specs=[pl.BlockSpec((t, LANES), seq), pl.BlockSpec((t, LANES), seq), pl.BlockSpec((t, LANES), lambda b, p: (b, Z0_FV + p)),
                  pl.BlockSpec((t, LANES), lambda b, p: (b, 0)), pl.BlockSpec((1, FOX_H, t), lambda b, p: (b, 0, 0))],
        out_specs=pl.BlockSpec((t, LANES), seq),
        out_shape=S((n, FOX_W), f32),
        compiler_params=pltpu.CompilerParams(dimension_semantics=("parallel", "parallel")),
    )(qn, kn, z0, ccol, crow)


def _fox_attn_bwd(qn, kn, z0, ccol, crow, do, t, tq=256, side=None):
    n = qn.shape[0]
    nq = t // tq
    nb = n // t

    def body(q_ref, k_ref, v_ref, ccol_ref, crow_ref, do_ref, dq_ref, dk_ref, dv_ref, dcq_ref, dck_ref):
        p = pl.program_id(1)
        dk_ref[...] = jnp.zeros_like(dk_ref)
        dv_ref[...] = jnp.zeros_like(dv_ref)
        dck_ref[...] = jnp.zeros_like(dck_ref)
        kf, v, crow = k_ref[...], v_ref[...].astype(bf16), crow_ref[0]
        k = kf.astype(bf16)
        lane = _iota2((1, LANES), 1)
        sub = _iota2((FOX_H, 1), 0)
        scale = FOX_D ** -0.5
        for i in range(nq):
            rows, kmax = pl.ds(i * tq, tq), (i + 1) * tq
            q, cc, dout = q_ref[rows, :], ccol_ref[rows, :], do_ref[rows, :]
            dq = jnp.zeros((tq, LANES), f32)
            dcq = jnp.zeros((tq, LANES), f32)
            for hh in range(2):
                mh, qh, pe, l = _fox_scores(hh, p, i, tq, q, k[:kmax], cc, crow[:, :kmax])
                pr = pe / l
                doh = jnp.where(mh, dout, 0.0).astype(bf16)
                dp = _dot(doh, v[:kmax], NT)
                ds = pr * (dp - jnp.sum(pr * dp, axis=1, keepdims=True))
                dsb = ds.astype(bf16)
                dq = dq + _dot(dsb, jnp.where(mh, kf[:kmax], 0.0).astype(bf16)) * scale
                dk_ref[:kmax, :] += _dot(dsb, qh, TN) * scale
                dv_ref[:kmax, :] += _dot(pr.astype(bf16), doh, TN)
                h = 2 * p + hh
                dcq = dcq + jnp.where(lane == h, jnp.sum(ds, axis=1, keepdims=True), 0.0)
                dck_ref[0, 0, :, :kmax] += jnp.where(sub == h, -jnp.sum(ds, axis=0, keepdims=True), 0.0)
            dq_ref[rows, :] = dq
            dcq_ref[0, rows, :] = dcq

    seq = lambda b, p: (b, p)
    return _pcall(
        side, body, name="fox_attn_bwd", grid=(nb, FOX_H // 2),
        in_specs=[pl.BlockSpec((t, LANES), seq), pl.BlockSpec((t, LANES), seq), pl.BlockSpec((t, LANES), lambda b, p: (b, Z0_FV + p)),
                  pl.BlockSpec((t, LANES), lambda b, p: (b, 0)), pl.BlockSpec((1, FOX_H, t), lambda b, p: (b, 0, 0)),
                  pl.BlockSpec((t, LANES), seq)],
        out_specs=[pl.BlockSpec((t, LANES), seq), pl.BlockSpec((t, LANES), seq), pl.BlockSpec((t, LANES), seq),
                   pl.BlockSpec((1, t, LANES), lambda b, p: (p, b, 0)), pl.BlockSpec((1, 1, FOX_H, t), lambda b, p: (p, b, 0, 0))],
        out_shape=[S((n, FOX_W), f32), S((n, FOX_W), f32), S((n, FOX_W), f32), S((4, n, LANES), f32), S((4, nb, FOX_H, t), f32)],
        compiler_params=pltpu.CompilerParams(dimension_semantics=("parallel", "parallel")),
    )(qn, kn, z0, ccol, crow, do)


def _adamw_math(w, g, m, v):
    m = ADAM_B1 * m + (1.0 - ADAM_B1) * g
    v = ADAM_B2 * v + (1.0 - ADAM_B2) * (g * g)
    m_hat = m / (1.0 - ADAM_B1 ** ADAM_STEP)
    v_hat = v / (1.0 - ADAM_B2 ** ADAM_STEP)
    return -ADAM_LR * (m_hat / (jnp.sqrt(v_hat) + ADAM_EPS) + ADAM_WD * w), m, v


def _adamw_big(name, idx, gmine, recv1, recv2, w, m, v):
    r, wc = w.shape
    c = gmine.shape[2]
    tr = min(r, 256)

    def body(idx_ref, gm_ref, r1_ref, r2_ref, w_ref, m_ref, v_ref, g_ref, d_ref, nm_ref, nv_ref):
        g = gm_ref[0].astype(f32) + r1_ref[0].astype(f32)
        for k in range(3):
            g = g + r2_ref[k].astype(f32)
        g = g[:, :wc]
        d, nm, nv = _adamw_math(w_ref[...], g, m_ref[...], v_ref[...])
        g_ref[...] = g
        d_ref[...] = d
        nm_ref[...] = nm
        nv_ref[...] = nv

    row = pl.BlockSpec((tr, wc), lambda i, s: (i, 0))
    return pl.pallas_call(
        body, name=name,
        grid_spec=pltpu.PrefetchScalarGridSpec(
            num_scalar_prefetch=1, grid=(r // tr,),
            in_specs=[pl.BlockSpec((1, tr, c), lambda i, s: (s[0], i, 0)), pl.BlockSpec((1, tr, c), lambda i, s: (s[1], i, 0)),
                      pl.BlockSpec((3, tr, c), lambda i, s: (0, i, 0)), row, row, row],
            out_specs=[row, row, row, row]),
        out_shape=[S((r, wc), f32)] * 4,
        compiler_params=pltpu.CompilerParams(dimension_semantics=("parallel",)),
    )(idx, gmine, recv1, recv2, w, m, v)


def _pair_sum(name, idx, gmine, recv1):
    _, r, c = gmine.shape
    g4 = gmine.reshape(4, 2, r, c)

    def body(idx_ref, gm_ref, r1_ref, o_ref):
        o_ref[0] = (gm_ref[0, 0].astype(f32) + r1_ref[0].astype(f32)).astype(bf16)

    return pl.pallas_call(
        body, name=name,
        grid_spec=pltpu.PrefetchScalarGridSpec(
            num_scalar_prefetch=1, grid=(4,),
            in_specs=[pl.BlockSpec((1, 1, r, c), lambda ch, s: (ch, s[0], 0, 0)), pl.BlockSpec((1, r, c), lambda ch, s: (ch, 0, 0))],
            out_specs=pl.BlockSpec((1, r, c), lambda ch, s: (ch, 0, 0))),
        out_shape=S((4, r, c), bf16),
        compiler_params=pltpu.CompilerParams(dimension_semantics=("parallel",)),
    )(idx, g4, recv1)


def _adamw_small(parts, w, m, v, own_mask, own_g):
    def body(p_ref, w_ref, m_ref, v_ref, mask_ref, og_ref, g_ref, d_ref, nm_ref, nv_ref):
        g = p_ref[0]
        for k in range(1, NDEV):
            g = g + p_ref[k]
        g_ref[...] = g
        ge = jnp.where(mask_ref[...] > 0.5, og_ref[...], g)
        d, nm, nv = _adamw_math(w_ref[...], ge, m_ref[...], v_ref[...])
        d_ref[...] = d
        nm_ref[...] = nm
        nv_ref[...] = nv

    return pl.pallas_call(body, name="adamw_small", out_shape=[S(w.shape, f32)] * 4)(parts, w, m, v, own_mask, own_g)


def _sum_parts(parts):
    def body(p_ref, g_ref):
        g = p_ref[0]
        for k in range(1, NDEV):
            g = g + p_ref[k]
        g_ref[...] = g

    return pl.pallas_call(body, name="sum_parts", out_shape=S(parts.shape[1:], f32))(parts)


def _me():
    return lax.axis_index("x"), lax.axis_index("y"), lax.axis_index("c")


def _hbm_specs(n):
    return [pl.BlockSpec(memory_space=pl.ANY)] * n


def _allgather(name, xs):
    na = len(xs)

    def body(*refs):
        x_refs, out_refs = refs[:na], refs[na:2 * na]
        send_sems, recv_sems, local_sems = refs[2 * na:]
        mx, my, mc = _me()
        me, sib = (mx, my, mc), (mx, my, 1 - mc)
        chips = [(1 - mx, my), (mx, 1 - my), (1 - mx, 1 - my)]

        def slab(a, px, py, pc):
            return out_refs[a].at[4 * px + 2 * py + pc]

        def copy(a, k, block, to, own=False):
            return pltpu.make_async_remote_copy(
                src_ref=x_refs[a] if own else slab(a, *block), dst_ref=slab(a, *block),
                send_sem=send_sems.at[7 * a + k], recv_sem=recv_sems.at[7 * a + k], device_id=to, device_id_type=MESH)

        mine = [pltpu.make_async_copy(x_refs[a], slab(a, *me), local_sems.at[a]) for a in range(na)]
        first = []
        for a in range(na):
            mine[a].start()
            first += [copy(a, 0, me, sib, own=True)] + [copy(a, 1 + j, me, (*chip, mc), own=True) for j, chip in enumerate(chips)]
        for cp in first:
            cp.start()
        passed = []
        for j, chip in enumerate(chips):
            for a in range(na):
                copy(a, 1 + j, (*chip, mc), me).wait_recv()
                passed.append(copy(a, 4 + j, (*chip, mc), sib))
                passed[-1].start()
        for a in range(na):
            copy(a, 0, sib, me).wait_recv()
            for j, chip in enumerate(chips):
                copy(a, 4 + j, (*chip, 1 - mc), me).wait_recv()
        for cp in first + passed:
            cp.wait_send()
        for cp in mine:
            cp.wait()

    return pl.pallas_call(
        body, name=name, out_shape=[S((NDEV,) + x.shape, x.dtype) for x in xs],
        in_specs=_hbm_specs(na), out_specs=_hbm_specs(na),
        scratch_shapes=[pltpu.SemaphoreType.DMA((7 * na,)), pltpu.SemaphoreType.DMA((7 * na,)), pltpu.SemaphoreType.DMA((na,))],
    )(*xs)


def _rs_sibling(gs):
    na = len(gs)

    def body(*refs):
        g_refs, out_refs, send_sems, recv_sems = refs[:na], refs[na:2 * na], refs[2 * na], refs[2 * na + 1]
        mx, my, mc = _me()
        cps = [pltpu.make_async_remote_copy(
            src_ref=g_refs[a].at[2 * ch + 1 - mc], dst_ref=out_refs[a].at[ch], send_sem=send_sems.at[4 * a + ch],
            recv_sem=recv_sems.at[4 * a + ch], device_id=(mx, my, 1 - mc), device_id_type=MESH)
            for a in range(na) for ch in range(4)]
        for cp in cps:
            cp.start()
        for cp in cps:
            cp.wait_recv()
        for cp in cps:
            cp.wait_send()

    return pl.pallas_call(
        body, name="rs_sibling", out_shape=[S((4,) + g.shape[1:], g.dtype) for g in gs],
        in_specs=_hbm_specs(na), out_specs=_hbm_specs(na),
        scratch_shapes=[pltpu.SemaphoreType.DMA((4 * na,)), pltpu.SemaphoreType.DMA((4 * na,))],
    )(*gs)


def _side_allgather(xs):
    na = len(xs)

    def mk(x_refs, out_refs, sems):
        send_sems, recv_sems, local_sems = sems
        mx, my, mc = _me()
        me, sib = (mx, my, mc), (mx, my, 1 - mc)
        chips = [(1 - mx, my), (mx, 1 - my), (1 - mx, 1 - my)]

        def slab(a, px, py, pc):
            return out_refs[a].at[4 * px + 2 * py + pc]

        def copy(a, k, block, to, own=False):
            return pltpu.make_async_remote_copy(
                src_ref=x_refs[a] if own else slab(a, *block), dst_ref=slab(a, *block),
                send_sem=send_sems.at[7 * a + k], recv_sem=recv_sems.at[7 * a + k], device_id=to, device_id_type=MESH)

        mine = [pltpu.make_async_copy(x_refs[a], slab(a, *me), local_sems.at[a]) for a in range(na)]
        first = []
        for a in range(na):
            first += [copy(a, 0, me, sib, own=True)] + [copy(a, 1 + j, me, (*chip, mc), own=True) for j, chip in enumerate(chips)]
        return me, sib, chips, mc, copy, mine, first

    def start(x_refs, out_refs, sems):
        *_, mine, first = mk(x_refs, out_refs, sems)
        for cp in mine + first:
            cp.start()

    def finish(x_refs, out_refs, sems):
        me, sib, chips, mc, copy, mine, first = mk(x_refs, out_refs, sems)
        passed = []
        for j, chip in enumerate(chips):
            for a in range(na):
                copy(a, 1 + j, (*chip, mc), me).wait_recv()
                passed.append(copy(a, 4 + j, (*chip, mc), sib))
                passed[-1].start()
        for a in range(na):
            copy(a, 0, sib, me).wait_recv()
            for j, chip in enumerate(chips):
                copy(a, 4 + j, (*chip, 1 - mc), me).wait_recv()
        for cp in first + passed:
            cp.wait_send()
        for cp in mine:
            cp.wait()

    scratch = [pltpu.SemaphoreType.DMA((7 * na,)), pltpu.SemaphoreType.DMA((7 * na,)), pltpu.SemaphoreType.DMA((na,))]
    return _Side(xs, [S((NDEV,) + x.shape, x.dtype) for x in xs], scratch, start, finish)


def _side_exchange(arrs, nslot, out_slots, route):
    na = len(arrs)

    def copies(in_refs, out_refs, sems):
        send_sems, recv_sems = sems
        return [pltpu.make_async_remote_copy(
            src_ref=in_refs[a].at[src], dst_ref=out_refs[a].at[k], send_sem=send_sems.at[nslot * a + k],
            recv_sem=recv_sems.at[nslot * a + k], device_id=to, device_id_type=MESH)
            for a in range(na) for k, (src, to) in enumerate(route(*_me()))]

    def start(in_refs, out_refs, sems):
        for cp in copies(in_refs, out_refs, sems):
            cp.start()

    def finish(in_refs, out_refs, sems):
        cps = copies(in_refs, out_refs, sems)
        for cp in cps:
            cp.wait_recv()
        for cp in cps:
            cp.wait_send()

    scratch = [pltpu.SemaphoreType.DMA((nslot * na,)), pltpu.SemaphoreType.DMA((nslot * na,))]
    return _Side(arrs, [S((out_slots,) + x.shape[1:], x.dtype) for x in arrs], scratch, start, finish)


def _side_rs_sibling(gs):
    return _side_exchange(gs, 4, 4, lambda mx, my, mc: [(2 * ch + 1 - mc, (mx, my, 1 - mc)) for ch in range(4)])


def _side_rs_chips(ps):
    return _side_exchange(ps, 3, 3, lambda mx, my, mc: [(2 * cx + cy, (cx, cy, mc)) for cx, cy in
                                                        [(1 - mx, my), (mx, 1 - my), (1 - mx, 1 - my)]])


_COLW = 512
_COL_NAMES = ("l0_w_in", "l0_w_ff1", "l1_w_in", "l1_w_ff1")
_ROW_NAMES = ("l0_w_out", "l0_w_ff2", "l1_w_out", "l1_w_ff2")
_BIG_NAMES = _COL_NAMES + _ROW_NAMES


def _full_weight(gathered, name, ncols):
    if name in _ROW_NAMES:
        return gathered.reshape(-1, D)
    return gathered[:, :, :ncols].transpose(1, 0, 2).reshape(D, NDEV * ncols)


def _regroup_w_in0(w):
    main = jnp.concatenate([w[:, 0:1536], w[:, 1544:3080], w[:, 3088:3600]], axis=1)
    small = jnp.concatenate([w[:, 1536:1544], w[:, 3080:3088]], axis=1)
    return jnp.concatenate([main, small, jnp.zeros((D, ZW0 - 3584 - 16), w.dtype)], axis=1)


def _ungroup_w_in0(g):
    return jnp.concatenate([g[:, 0:1536], g[:, 3584:3592], g[:, 1536:3072], g[:, 3592:3600], g[:, 3072:3584]], axis=1)


def _col_slabs(g, ncols):
    g = g.reshape(D, NDEV, ncols).transpose(1, 0, 2)
    return jnp.pad(g, ((0, 0), (0, 0), (0, _COLW - ncols)))


def _sq(t):
    return t * t


def _epi_res_norm(acc, res, gain):
    y = acc + res
    h = _f_norm(y, gain)[0]
    return y, h, h


_RES_NORM_OUTS = [("tile", f32), ("tile", bf16), ("tile_t", bf16)]


def _epi_norm_bwd(acc, x, dres, gain):
    _, vjp = jax.vjp(lambda xx, gg: _f_norm(xx, gg)[0], x, gain)
    dx, dgain = vjp(acc)
    dx = dx + dres
    return dx, dgain, dx


_NORM_BWD_OUTS = [("tile", f32), ("rows", f32), ("tile_t", bf16)]


def _epi_loss(acc, res, tgt):
    e = acc + res - tgt
    dy = e * (1.0 / D)
    return dy, dy, jnp.sum(e * e, axis=0, keepdims=True)


_LOSS_OUTS = [("tile", f32), ("tile_t", bf16), ("rows", f32)]


def _mlp_fwd(tag, x, h, w1, w2, epi, extra, consts, outs):
    a = _mm(f"{tag}_ff1", h, w1, NN, bf16, epi=lambda acc: jnp.maximum(acc, 0.0))
    return _mm(f"{tag}_ff2", a, w2, NN, f32, a_fn=_sq, epi=epi, extra=(x,) + tuple(extra), consts=consts, outs=outs), a


def _mlp_bwd(tag, x, gain, w1, w2, ht, a, dy, dyt, side=None):
    da = _mm(f"{tag}_ff2_dx", dy, w2, NT, bf16, epi=lambda acc, av: acc * 2.0 * av.astype(f32), extra=(a,), side=side)
    if side is not None:
        da, side_res = da
    dw2 = _mm(f"{tag}_ff2_dw", dyt, a, NN, bf16, b_fn=_sq, out_t=True)
    res = _mm(f"{tag}_ff1_dx", da, w1, NT, f32, epi=_epi_norm_bwd, extra=(x, dy), consts=(gain,), outs=_NORM_BWD_OUTS)
    dw1 = _mm(f"{tag}_ff1_dw", ht, da, NN, bf16, slab=True)
    return (res, dw1, dw2) if side is None else ((res, dw1, dw2), side_res)


def _row(v):
    return v.reshape(1, -1).astype(f32)


_L0_REST = ("l0_w_ff1", "l0_w_out")
_L1_MIX = ("l0_w_ff2", "l1_w_in", "l1_w_out")
_L1_FFN = ("l1_w_ff1", "l1_w_ff2")
_GRAD_A = ("l1_w_ff1", "l1_w_ff2", "l1_w_out", "l1_w_in")
_GRAD_B = ("l0_w_ff1", "l0_w_ff2", "l0_w_out")
_IN_PLACE = ("l0_w_ff1", "l1_w_in", "l1_w_ff1")


def _train(x, tgt, args, mom, var, t):
    n = x.shape[0]
    nchunk = t // CH
    mx, my, mc = _me()
    dev, chip = 4 * mx + 2 * my + mc, 2 * mx + my
    core_idx = jnp.reshape(mc, (1,)).astype(jnp.int32)
    own_idx = jnp.stack([dev, chip]).astype(jnp.int32)
    ncols = {nm: args[nm].shape[1] for nm in _COL_NAMES}
    conv_cols = args["l0_gdn_conv"].shape[1]
    g, big, w = {}, {}, {}

    def send(nm):
        a = args[nm].astype(bf16)
        return jnp.pad(a, ((0, 0), (0, _COLW - ncols[nm]))) if nm in _COL_NAMES else a

    def take(names, gathered):
        for nm, arr in zip(names, gathered):
            w[nm] = arr if nm in _IN_PLACE else _full_weight(arr, nm, ncols.get(nm, 0))

    def by_dev(nm, ga):
        return ga if nm in _COL_NAMES else ga.reshape(NDEV, -1, D)

    def pair(names, gs, r1s):
        return [_pair_sum(f"rs_pair_sum_{nm}", core_idx, ga, r1) for nm, ga, r1 in zip(names, gs, r1s)]

    def adam(names, gs, r1s, r2s):
        for nm, ga, r1, r2 in zip(names, gs, r1s, r2s):
            big[nm] = _adamw_big(f"adamw_{nm}", own_idx, ga, r1, r2, args[nm], mom[nm], var[nm])


    li = jnp.arange(FOX_W)
    pm = jnp.where((li[:, None] // FOX_D) == (li[None, :] // FOX_D), 1.0 / FOX_D, 0.0).astype(f32)
    lane_head = jnp.arange(GDN_W) // HD
    sel = lambda first_lane: (jnp.arange(LANES)[:, None] == (first_lane + lane_head)[None, :]).astype(f32)
    e_beta, e_alpha = sel(FOX_H), sel(FOX_H + GDN_H)
    alog_b, dt_b = _row(jnp.repeat(args["l0_gdn_A_log"], HD)), _row(jnp.repeat(args["l0_gdn_dt_bias"], HD))
    gq_t, gk_t = _row(jnp.tile(args["l0_fox_q_norm"], FOX_H)), _row(jnp.tile(args["l0_fox_k_norm"], FOX_H))
    on0_t, on1_t = _row(jnp.tile(args["l0_gdn_o_norm"], GDN_H)), _row(jnp.tile(args["l1_hgrn_o_norm"], HG_H))
    fbias = jnp.pad(_row(args["l0_fox_f_bias"]), ((0, 0), (0, LANES - FOX_H)))
    g0m, g0f, g1m, g1f = (_row(args[k]) for k in ("l0_mix_norm", "l0_ffn_norm", "l1_mix_norm", "l1_ffn_norm"))
    lbl = args["hgrn_lb_logits"].astype(f32)

    first = [send("l0_w_in"), jnp.pad(args["l0_gdn_conv"], ((0, 4), (0, LANES * 2 - conv_cols)))]
    (h0, h0t), first = _tok_fwd("l0_mix_norm", _f_norm, [(x, D, 0)], [(g0m, None, 0)], [(D, bf16)], TOK_ROWS, also_t=(0,),
                                side=_side_allgather(first))
    take(("l0_w_in",), first[:1])
    w_in0 = _regroup_w_in0(w["l0_w_in"])
    wconv = first[1][:, :4, :conv_cols].transpose(1, 0, 2).reshape(4, NDEV * conv_cols)
    z0 = _mm("l0_in", h0, w_in0, NN, f32)
    qk_rows = [(z0, 2 * FOX_W, 0)]
    qk_consts = [(gq_t, None, 0), (gk_t, None, 0), (pm, None, 0)]
    qn, kn = _tok_fwd("fox_pre", _f_foxpre, qk_rows, qk_consts, [(FOX_W, f32)] * 2, TOK_ROWS)
    ccol, crow = _fox_gate_fwd(z0, fbias, t)
    fox_o, got = _fox_attn_fwd(qn, kn, z0, ccol, crow, t, side=_side_allgather([send(nm) for nm in _L0_REST]))
    take(_L0_REST, got)
    conv_rows, conv_consts = [(z0, LANES, Z0_GQKV)], [(wconv, LANES, 0)]
    (qkv,) = _tok_fwd("gdn_conv", _f_conv, conv_rows, conv_consts, [(LANES, f32)], t, ncb=12, with_j=True)
    gate_consts = [(e_beta, None, 0), (e_alpha, None, 0), (alog_b, None, 0), (dt_b, None, 0)]
    intra_rows = [(qkv, GDN_W, 0), (qkv, GDN_W, 1), (qkv, GDN_W, 2), (z0, LANES, Z0_SMALL)]
    intra, got = _tok_fwd("gdn_intra", _f_gdn_gated, intra_rows, gate_consts, [(GDN_W, f32)] * 7, 2 * CH,
                          side=_side_allgather([send(nm) for nm in _L1_MIX]))
    take(_L1_MIX, got)
    inter_rows = [(a, 0, 1) for a in intra[:6]]
    full_rows = inter_rows + [(fox_o, 0, 1), (z0, Z0_GG // 4, 1)]
    (cat0, gdn_hist), got = _scan_fwd("gdn_scan", _f_gdn_full, full_rows, GDN_H, nchunk, consts=(on0_t,), out_dtype=bf16, out_k=2,
                                      side=_side_allgather([send(nm) for nm in _L1_FFN[1:]]))
    take(_L1_FFN[1:], got)
    x1, hf0, hf0t = _mm("l0_out", cat0, w["l0_w_out"], NN, f32, epi=_epi_res_norm, extra=(x,), consts=(g0f,), outs=_RES_NORM_OUTS)
    (x2, h1, h1t), a0 = _mlp_fwd("l0", x1, hf0, w["l0_w_ff1"], w["l0_w_ff2"], _epi_res_norm, (), (g1m,), _RES_NORM_OUTS)

    z1 = _mm("l1_in", h1, w["l1_w_in"], NN, f32)
    (cat1, hg_hist), got = _scan_fwd("hgrn_scan", _f_hgrn_full, [(z1, 0, 4)], HG_H, nchunk, consts=(lbl, on1_t), out_dtype=bf16,
                                     side=_side_allgather([send(nm) for nm in _L1_FFN[:1]]))
    take(_L1_FFN[:1], got)
    x3, hf1, hf1t = _mm("l1_out", cat1, w["l1_w_out"], NN, f32, epi=_epi_res_norm, extra=(x2,), consts=(g1f,), outs=_RES_NORM_OUTS)
    (dy, dyt, loss_row), a1 = _mlp_fwd("l1", x3, hf1, w["l1_w_ff1"], w["l1_w_ff2"], _epi_loss, (tgt,), (), _LOSS_OUTS)

    (dx3, g["l1_ffn_norm"], dx3t), ga_ff1, ga_ff2 = _mlp_bwd("l1", x3, g1f, w["l1_w_ff1"], w["l1_w_ff2"], hf1t, a1, dy, dyt)
    dcat1 = _mm("l1_out_dx", dx3, w["l1_w_out"], NT, f32)
    ga_out = _mm("l1_out_dw", dx3t, cat1, NN, bf16, out_t=True)
    dz1, dlbl, don1 = _scan_bwd("hgrn_scan_bwd", _f_hgrn_full, [(z1, 0, 4)], hg_hist, dcat1, HG_H, nchunk, dtypes=[bf16],
                                consts=(lbl, on1_t))
    dx2, g["l1_mix_norm"], dx2t = _mm("l1_in_dx", dz1, w["l1_w_in"], NT, f32, epi=_epi_norm_bwd, extra=(x2, dx3), consts=(g1m,),
                                      outs=_NORM_BWD_OUTS)
    ga_in = _mm("l1_in_dw", h1t, dz1, NN, bf16, slab=True)
    g["l1_hgrn_o_norm"] = don1.reshape(HG_H, HD).sum(0)
    g["hgrn_lb_logits"] = dlbl
    gs_a = [by_dev(nm, ga) for nm, ga in zip(_GRAD_A, (ga_ff1, ga_ff2, ga_out, ga_in))]

    (dx1, g["l0_ffn_norm"], dx1t), gb_ff1, gb_ff2 = _mlp_bwd("l0", x1, g0f, w["l0_w_ff1"], w["l0_w_ff2"], hf0t, a0, dx2, dx2t)
    dcat0 = _mm("l0_out_dx", dx1, w["l0_w_out"], NT, f32)
    gb_out = _mm("l0_out_dw", dx1t, cat0, NN, bf16, out_t=True)
    gs_b = [by_dev(nm, ga) for nm, ga in zip(_GRAD_B, (gb_ff1, gb_ff2, gb_out))]
    (*dinter, dfox_o, dgg, don0), r1_a = _scan_bwd("gdn_scan_bwd", _f_gdn_full, full_rows, gdn_hist, dcat0, GDN_H, nchunk,
                                                   consts=(on0_t,), out_k=2, dtypes=[f32] * 7 + [bf16], cps=1,
                                                   side=_side_rs_sibling(gs_a))
    pairs_a = pair(_GRAD_A, gs_a, r1_a)
    (dqn, dkn, dfv, dcq, dck), got = _fox_attn_bwd(qn, kn, z0, ccol, crow, dfox_o, t,
                                                   side=_join_sides([_side_rs_chips(pairs_a), _side_rs_sibling(gs_b)]))
    r2_a, r1_b = got[:len(_GRAD_A)], got[len(_GRAD_A):]
    adam(_GRAD_A, gs_a, r1_a, r2_a)
    pairs_b = pair(_GRAD_B, gs_b, r1_b)
    (dqkv, dzs_g, dalog_b, ddt_b), r2_b = _tok_bwd(
        "gdn_intra_bwd", _f_gdn_gated_given, intra_rows + [(intra[6], GDN_W, 0)], gate_consts, [(a, GDN_W) for a in dinter],
        2 * CH, [0, 1, 2, 3], [2, 3], ncat=3, side=_side_rs_chips(pairs_b))
    adam(_GRAD_B, gs_b, r1_b, r2_b)
    dgqkv, dwconv = _tok_bwd("gdn_conv_bwd", _f_conv, conv_rows, conv_consts, [(dqkv, LANES)], t, [0], [0], ncb=12, with_j=True,
                             drow_dtype=bf16)
    dzs_f, dfb = _fox_gate_bwd(z0, fbias, dcq, dck, t)
    dzqk, dgq_t, dgk_t = _tok_bwd("fox_pre_bwd", _f_foxpre, qk_rows, qk_consts, [(dqn, FOX_W), (dkn, FOX_W)], TOK_ROWS, [0], [0, 1],
                                  drow_dtype=bf16)
    dz0 = jnp.concatenate([dzqk, dfv.astype(bf16), dgqkv, dgg, (dzs_g + dzs_f).astype(bf16), jnp.zeros((n, ZW0 - 3712), bf16)], axis=1)
    gs_c = [_col_slabs(_ungroup_w_in0(_mm("l0_in_dw", h0t, dz0, NN, bf16)), ncols["l0_w_in"])]
    r1_c = _rs_sibling(gs_c)
    pairs_c = pair(("l0_w_in",), gs_c, r1_c)
    (dx, g["l0_mix_norm"], _), r2_c = _mm("l0_in_dx", dz0, w_in0, NT, f32, epi=_epi_norm_bwd, extra=(x, dx1), consts=(g0m,),
                                          outs=_NORM_BWD_OUTS, side=_side_rs_chips(pairs_c))
    adam(("l0_w_in",), gs_c, r1_c, r2_c)
    g["l0_fox_q_norm"] = dgq_t.reshape(FOX_H, FOX_D).sum(0)
    g["l0_fox_k_norm"] = dgk_t.reshape(FOX_H, FOX_D).sum(0)
    g["l0_fox_f_bias"] = dfb[0, :FOX_H]
    g["l0_gdn_conv"] = dwconv
    g["l0_gdn_A_log"] = dalog_b.reshape(GDN_H, HD).sum(1)
    g["l0_gdn_dt_bias"] = ddt_b.reshape(GDN_H, HD).sum(1)
    g["l0_gdn_o_norm"] = don0.reshape(GDN_H, HD).sum(0)
    return loss_row, dx, g, big


_NAMES = ("l0_mix_norm", "l0_w_in", "l0_fox_q_norm", "l0_fox_k_norm", "l0_fox_f_bias", "l0_gdn_conv", "l0_gdn_A_log",
          "l0_gdn_dt_bias", "l0_gdn_o_norm", "l0_w_out", "l0_ffn_norm", "l0_w_ff1", "l0_w_ff2", "l1_mix_norm", "l1_w_in",
          "l1_hgrn_o_norm", "l1_w_out", "l1_ffn_norm", "l1_w_ff1", "l1_w_ff2", "hgrn_lb_logits")
_SMALL_NAMES = tuple(nm for nm in _NAMES if nm not in _BIG_NAMES)
_SMALL_ROWS = 16


def _pack_small(vals):
    flat = jnp.concatenate([vals[nm].reshape(-1).astype(f32) for nm in _SMALL_NAMES])
    return jnp.pad(flat, (0, _SMALL_ROWS * D - flat.shape[0])).reshape(_SMALL_ROWS, D)


def _unpack_small(packed, shapes):
    flat = packed.reshape(-1)
    out, off = {}, 0
    for nm in _SMALL_NAMES:
        size = 1
        for s in shapes[nm]:
            size *= s
        out[nm] = flat[off:off + size].reshape(shapes[nm])
        off += size
    return out, off


def kernel(x, l0_mix_norm, l0_w_in, l0_fox_q_norm, l0_fox_k_norm, l0_fox_f_bias, l0_gdn_conv, l0_gdn_A_log, l0_gdn_dt_bias, l0_gdn_o_norm, l0_w_out, l0_ffn_norm, l0_w_ff1, l0_w_ff2, l1_mix_norm, l1_w_in, l1_hgrn_o_norm, l1_w_out, l1_ffn_norm, l1_w_ff1, l1_w_ff2, hgrn_lb_logits, loss_target, m_l0_mix_norm, m_l0_w_in, m_l0_fox_q_norm, m_l0_fox_k_norm, m_l0_fox_f_bias, m_l0_gdn_conv, m_l0_gdn_A_log, m_l0_gdn_dt_bias, m_l0_gdn_o_norm, m_l0_w_out, m_l0_ffn_norm, m_l0_w_ff1, m_l0_w_ff2, m_l1_mix_norm, m_l1_w_in, m_l1_hgrn_o_norm, m_l1_w_out, m_l1_ffn_norm, m_l1_w_ff1, m_l1_w_ff2, m_hgrn_lb_logits, v_l0_mix_norm, v_l0_w_in, v_l0_fox_q_norm, v_l0_fox_k_norm, v_l0_fox_f_bias, v_l0_gdn_conv, v_l0_gdn_A_log, v_l0_gdn_dt_bias, v_l0_gdn_o_norm, v_l0_w_out, v_l0_ffn_norm, v_l0_w_ff1, v_l0_w_ff2, v_l1_mix_norm, v_l1_w_in, v_l1_hgrn_o_norm, v_l1_w_out, v_l1_ffn_norm, v_l1_w_ff1, v_l1_w_ff2, v_hgrn_lb_logits):
    args = dict(zip(_NAMES, (l0_mix_norm, l0_w_in, l0_fox_q_norm, l0_fox_k_norm, l0_fox_f_bias, l0_gdn_conv, l0_gdn_A_log, l0_gdn_dt_bias, l0_gdn_o_norm, l0_w_out, l0_ffn_norm, l0_w_ff1, l0_w_ff2, l1_mix_norm, l1_w_in, l1_hgrn_o_norm, l1_w_out, l1_ffn_norm, l1_w_ff1, l1_w_ff2, hgrn_lb_logits)))
    mom = dict(zip(_NAMES, (m_l0_mix_norm, m_l0_w_in, m_l0_fox_q_norm, m_l0_fox_k_norm, m_l0_fox_f_bias, m_l0_gdn_conv, m_l0_gdn_A_log, m_l0_gdn_dt_bias, m_l0_gdn_o_norm, m_l0_w_out, m_l0_ffn_norm, m_l0_w_ff1, m_l0_w_ff2, m_l1_mix_norm, m_l1_w_in, m_l1_hgrn_o_norm, m_l1_w_out, m_l1_ffn_norm, m_l1_w_ff1, m_l1_w_ff2, m_hgrn_lb_logits)))
    var = dict(zip(_NAMES, (v_l0_mix_norm, v_l0_w_in, v_l0_fox_q_norm, v_l0_fox_k_norm, v_l0_fox_f_bias, v_l0_gdn_conv, v_l0_gdn_A_log, v_l0_gdn_dt_bias, v_l0_gdn_o_norm, v_l0_w_out, v_l0_ffn_norm, v_l0_w_ff1, v_l0_w_ff2, v_l1_mix_norm, v_l1_w_in, v_l1_hgrn_o_norm, v_l1_w_out, v_l1_ffn_norm, v_l1_w_ff1, v_l1_w_ff2, v_hgrn_lb_logits)))
    nb, t, _ = x.shape
    dev = 4 * lax.axis_index("x") + 2 * lax.axis_index("y") + lax.axis_index("c")
    conv_cols = l0_gdn_conv.shape[1]
    loss_row, dx, g, big = _train(x.reshape(nb * t, D), loss_target.reshape(nb * t, D), args, mom, var, t)

    shapes = {nm: args[nm].shape for nm in _SMALL_NAMES}
    gsm = dict(g)
    gsm["l0_gdn_conv"] = jnp.zeros(shapes["l0_gdn_conv"], f32)
    packed = _pack_small(gsm)
    _, used = _unpack_small(packed, shapes)
    flat_extra = jnp.concatenate([jnp.sum(loss_row).reshape(1), g["l0_gdn_conv"].reshape(-1)])
    packed = packed.reshape(-1).at[used:used + flat_extra.shape[0]].set(flat_extra).reshape(_SMALL_ROWS, D)
    (parts,) = _allgather("ag_small", [packed])
    total = _sum_parts(parts).reshape(-1)
    loss = 0.5 * total[used] / D
    conv_g_full = total[used + 1:used + 1 + 4 * NDEV * conv_cols].reshape(4, NDEV * conv_cols)
    conv_g = lax.dynamic_slice(conv_g_full, (0, dev * conv_cols), (4, conv_cols))
    own_vals = {nm: jnp.zeros(shapes[nm], f32) for nm in _SMALL_NAMES}
    own_vals["l0_gdn_conv"] = conv_g
    own_mask = {nm: jnp.zeros(shapes[nm], f32) for nm in _SMALL_NAMES}
    own_mask["l0_gdn_conv"] = jnp.ones(shapes["l0_gdn_conv"], f32)
    small = _adamw_small(parts, _pack_small(args), _pack_small(mom), _pack_small(var), _pack_small(own_mask), _pack_small(own_vals))
    small = [_unpack_small(a, shapes)[0] for a in small]
    small[0]["l0_gdn_conv"] = conv_g

    outs = [loss, dx.reshape(nb, t, D)]
    for k in range(4):
        outs += [big[nm][k] if nm in _BIG_NAMES else small[k][nm] for nm in _NAMES]
    return tuple(outs)
```

```python
import functools

import jax
import jax.numpy as jnp
from jax import lax
from jax.experimental import pallas as pl
from jax.experimental.pallas import tpu as pltpu

f32, bf16 = jnp.float32, jnp.bfloat16
NN = (((1,), (0,)), ((), ()))
NT = (((1,), (1,)), ((), ()))
TN = (((0,), (0,)), ((), ()))
MESH = pl.DeviceIdType.MESH
S = jax.ShapeDtypeStruct

EPS = 1e-6
D = 1024
LANES = 128
FOX_H, FOX_D, FOX_W = 8, 64, 512
GDN_H, HD, GDN_W = 4, 128, 512
HG_H = 8
CH = 64
TOK_ROWS = 256
ZW0 = 3840
NDEV = 8
ADAM_LR, ADAM_B1, ADAM_B2, ADAM_EPS, ADAM_WD, ADAM_STEP = 0.001, 0.9, 0.999, 1e-08, 0.01, 10

Z0_FV, Z0_GQKV, Z0_GG, Z0_SMALL = 8, 12, 24, 28


def _dot(a, b, dims=NN):
    return lax.dot_general(a, b, dims, preferred_element_type=f32)


def _iota2(shape, axis):
    return lax.broadcasted_iota(jnp.int32, shape, axis)


def _split3(x):
    x1 = x.astype(bf16)
    r = x - x1.astype(f32)
    x2 = r.astype(bf16)
    return x1, x2, (r - x2.astype(f32)).astype(bf16)


def _dot_sel(a, b, dims=NN, exact_lhs=False):
    if exact_lhs:
        return sum(_dot(a.astype(bf16), piece, dims) for piece in _split3(b))
    return sum(_dot(piece, b.astype(bf16), dims) for piece in _split3(a))


@jax.custom_vjp
def _sel_rhs(a, b):
    return _dot_sel(a, b)


_sel_rhs.defvjp(lambda a, b: (_dot_sel(a, b), b), lambda b, g: (_dot_sel(g, b, NT), jnp.zeros_like(b)))


@jax.custom_vjp
def _sel_lhs(a, x):
    return _dot_sel(a, x, exact_lhs=True)


_sel_lhs.defvjp(lambda a, x: (_dot_sel(a, x, exact_lhs=True), a), lambda a, g: (jnp.zeros_like(a), _dot_sel(a, g, TN, exact_lhs=True)))


class _Side:
    def __init__(self, ins, out_shapes, scratch, start, finish):
        self.ins, self.out_shapes, self.scratch, self.start, self.finish = list(ins), list(out_shapes), list(scratch), start, finish


def _join_sides(sides):
    def split(refs, counts):
        out, off = [], 0
        for c in counts:
            out.append(refs[off:off + c])
            off += c
        return out

    ni, no, ns = ([len(getattr(sd, a)) for sd in sides] for a in ("ins", "out_shapes", "scratch"))

    def run(which):
        def go(ins, outs, sems):
            for sd, i, o, c in zip(sides, split(ins, ni), split(outs, no), split(sems, ns)):
                getattr(sd, which)(i, o, c)
        return go

    return _Side(sum((sd.ins for sd in sides), []), sum((sd.out_shapes for sd in sides), []),
                 sum((sd.scratch for sd in sides), []), run("start"), run("finish"))


def _pcall(side, body, *, name, grid, in_specs, out_specs, out_shape, scratch_shapes=(), compiler_params=None):
    if side is None:
        return pl.pallas_call(body, name=name, grid=grid, in_specs=in_specs, out_specs=out_specs, out_shape=out_shape,
                              scratch_shapes=scratch_shapes, compiler_params=compiler_params)
    single = not isinstance(out_shape, (list, tuple))
    ospecs, oshape = ([out_specs], [out_shape]) if single else (list(out_specs), list(out_shape))
    nin, nout, nscr = len(in_specs), len(ospecs), len(scratch_shapes)
    si, so = len(side.ins), len(side.out_shapes)

    def wrapped(*refs):
        o0 = nin + si
        c0 = o0 + nout + so
        sins, souts, ssems = refs[nin:o0], refs[o0 + nout:c0], refs[c0 + nscr:]
        ids = [pl.program_id(a) for a in range(len(grid))]
        first = functools.reduce(jnp.logical_and, [i == 0 for i in ids])
        last = functools.reduce(jnp.logical_and, [i == g - 1 for i, g in zip(ids, grid)])

        @pl.when(first)
        def _():
            side.start(sins, souts, ssems)

        body(*refs[:nin], *refs[o0:o0 + nout], *refs[c0:c0 + nscr])

        @pl.when(last)
        def _():
            side.finish(sins, souts, ssems)

    call = pl.pallas_call(
        wrapped, name=name, grid=grid, in_specs=list(in_specs) + _hbm_specs(si), out_specs=ospecs + _hbm_specs(so),
        out_shape=oshape + side.out_shapes, scratch_shapes=list(scratch_shapes) + side.scratch,
        compiler_params=pltpu.CompilerParams(dimension_semantics=("arbitrary",) * len(grid),
                                             vmem_limit_bytes=getattr(compiler_params, "vmem_limit_bytes", None)))

    def run(*args):
        res = call(*args, *side.ins)
        return (res[0] if single else list(res[:nout])), list(res[nout:])

    return run


def _tok_specs(rows, consts, tm):
    specs = []
    for (_, w, base) in rows:
        specs.append(pl.BlockSpec((tm, w), functools.partial(lambda j, i, b: (i, b + j), b=base)))
    for (arr, w, base) in consts:
        if w is None:
            specs.append(pl.BlockSpec(arr.shape, lambda j, i: (0, 0)))
        else:
            specs.append(pl.BlockSpec((arr.shape[0], w), functools.partial(lambda j, i, b: (0, b + j), b=base)))
    return specs


def _tok_fwd(name, f, rows, consts, outs, tm, ncb=1, with_j=False, also_t=(), side=None):
    n = rows[0][0].shape[0]
    nin = len(rows) + len(consts)
    nout = len(outs)

    def body(*refs):
        ins = [r[...] for r in refs[:nin]]
        vals = f(pl.program_id(0), *ins) if with_j else f(*ins)
        for r, v in zip(refs[nin:nin + nout], vals):
            r[...] = v.astype(r.dtype)
        for r, k in zip(refs[nin + nout:], also_t):
            r[...] = vals[k].T.astype(r.dtype)

    return _pcall(
        side, body, name=name, grid=(ncb, n // tm),
        in_specs=_tok_specs(rows, consts, tm),
        out_specs=[pl.BlockSpec((tm, w), lambda j, i: (i, j)) for (w, _) in outs]
        + [pl.BlockSpec((outs[k][0], tm), lambda j, i: (0, i)) for k in also_t],
        out_shape=[S((n, w * ncb), dt) for (w, dt) in outs] + [S((outs[k][0], n), bf16) for k in also_t],
        compiler_params=pltpu.CompilerParams(dimension_semantics=("parallel", "parallel")),
    )(*[r[0] for r in rows], *[c[0] for c in consts])


def _tok_bwd(name, f, rows, consts, cots, tm, drow, dconst, ncb=1, with_j=False, addto=None, also_t=(), drow_dtype=f32, side=None, ncat=0):
    n = rows[0][0].shape[0]
    nr, nc, nct = len(rows), len(consts), len(cots)
    addto = addto or {}
    add_keys = sorted(addto)
    nadd = len(add_keys)

    def body(*refs):
        ins = [r[...] for r in refs[:nr + nc]]
        cot = [r[...] for r in refs[nr + nc:nr + nc + nct]]
        adds = refs[nr + nc + nct:nr + nc + nct + nadd]
        outs = refs[nr + nc + nct + nadd:]
        pos = list(drow) + [nr + k for k in dconst]

        def g(*dargs):
            full = list(ins)
            for p, a in zip(pos, dargs):
                full[p] = a
            return tuple(f(pl.program_id(0), *full) if with_j else f(*full))

        vals, vjp = jax.vjp(g, *[ins[p] for p in pos])
        grads = vjp(tuple(c.astype(v.dtype) for c, v in zip(cot, vals)))
        off = 0
        for k in range(len(drow)):
            gk = grads[k]
            if k in addto:
                gk = gk + adds[add_keys.index(k)][...]
            if k < ncat:
                outs[0][:, off:off + gk.shape[1]] = gk.astype(outs[0].dtype)
                off += gk.shape[1]
            else:
                outs[k - skip][...] = gk.astype(outs[k - skip].dtype)
            if k in also_t:
                tref = outs[len(drow) - skip + len(dconst) + list(also_t).index(k)]
                tref[...] = gk.T.astype(tref.dtype)
        first = pl.program_id(1) == 0
        for k in range(len(dconst)):
            ref = outs[len(drow) - skip + k]

            @pl.when(first)
            def _():
                ref[...] = jnp.zeros_like(ref)

            ref[...] += grads[len(drow) + k]

    skip = max(ncat - 1, 0)
    in_specs = _tok_specs(rows, consts, tm)
    in_specs += [pl.BlockSpec((tm, w), lambda j, i: (i, j)) for (_, w) in cots]
    in_specs += [pl.BlockSpec((tm, rows[drow[k]][1]), lambda j, i: (i, j)) for k in add_keys]
    dts = drow_dtype if isinstance(drow_dtype, (list, tuple)) else [drow_dtype] * len(drow)
    widths = [rows[k][1] for k in drow]
    if ncat:
        widths, dts = [sum(widths[:ncat])] + widths[ncat:], [dts[0]] + list(dts[ncat:])
    out_specs = [pl.BlockSpec((tm, wd), lambda j, i: (i, j)) for wd in widths]
    out_shape = [S((n, wd * ncb), dt) for wd, dt in zip(widths, dts)]
    for k in dconst:
        arr, w, _ = consts[k]
        if w is None:
            out_specs.append(pl.BlockSpec(arr.shape, lambda j, i: (0, 0)))
            out_shape.append(S(arr.shape, f32))
        else:
            out_specs.append(pl.BlockSpec((arr.shape[0], w), lambda j, i: (0, j)))
            out_shape.append(S((arr.shape[0], w * ncb), f32))
    for k in also_t:
        out_specs.append(pl.BlockSpec((rows[drow[k]][1], tm), lambda j, i: (0, i)))
        out_shape.append(S((rows[drow[k]][1], n), bf16))
    return _pcall(
        side, body, name=name, grid=(ncb, n // tm), in_specs=in_specs, out_specs=out_specs, out_shape=out_shape,
        compiler_params=pltpu.CompilerParams(dimension_semantics=("parallel", "arbitrary")),
    )(*[r[0] for r in rows], *[c[0] for c in consts], *[c[0] for c in cots], *[addto[k] for k in add_keys])


SCAN_CHUNKS = 2


def _scan_fwd(name, f, rows, nh, nchunk, side=None, consts=(), out_dtype=f32, out_k=1):
    n = rows[0][0].shape[0]
    nb = n // (CH * nchunk)
    nin, nco = len(rows), len(consts)
    w = nh * HD
    cps = SCAN_CHUNKS

    def body(*refs):
        o_ref, hist_ref, st = refs[nin + nco], refs[nin + nco + 1], refs[nin + nco + 2]

        @pl.when(pl.program_id(0) == 0)
        def _():
            st[...] = jnp.zeros_like(st)

        cvals = [r[...] for r in refs[nin:nin + nco]]
        state = st[...]
        for sub in range(cps):
            rr = pl.ds(sub * CH, CH)
            hist_ref[sub] = state.astype(hist_ref.dtype)
            tiles = [r[:, rr, :].reshape(nb * CH, r.shape[2]) for r in refs[:nin]]
            o, state = f(*tiles, *cvals, state)
            o_ref[:, rr, :] = o.reshape(nb, CH, out_k * w).astype(o_ref.dtype)
        st[...] = state

    seq3 = lambda a: a.reshape(nb, nchunk * CH, a.shape[1])
    res = _pcall(
        side, body, name=name, grid=(nchunk // cps,),
        in_specs=[pl.BlockSpec((nb, cps * CH, k * w), functools.partial(lambda c, base: (0, c, base), base=b)) for (_, b, k) in rows]
        + [pl.BlockSpec(c.shape, lambda c: (0, 0)) for c in consts],
        out_specs=[pl.BlockSpec((nb, cps * CH, out_k * w), lambda c: (0, c, 0)), pl.BlockSpec((cps, nb * w, HD), lambda c: (c, 0, 0))],
        out_shape=[S((nb, nchunk * CH, out_k * w), out_dtype), S((nchunk, nb * w, HD), bf16)],
        scratch_shapes=[pltpu.VMEM((nb * w, HD), f32)],
        compiler_params=pltpu.CompilerParams(dimension_semantics=("arbitrary",)),
    )(*[seq3(r[0]) for r in rows], *consts)
    (o, hist), extra = (res, None) if side is None else res
    out = [o.reshape(n, out_k * w), hist]
    return out if side is None else (out, extra)


def _scan_bwd(name, f, rows, hist, do, nh, nchunk, side=None, dtypes=None, consts=(), out_k=1, cps=SCAN_CHUNKS):
    n = rows[0][0].shape[0]
    nb = n // (CH * nchunk)
    nin, nco = len(rows), len(consts)
    w = nh * HD
    nstep = nchunk // cps

    def body(*refs):
        hist_ref, do_ref = refs[nin + nco], refs[nin + nco + 1]
        outs = refs[nin + nco + 2:nin + nco + 2 + nin]
        couts = refs[nin + nco + 2 + nin:nin + nco + 2 + nin + nco]
        ds = refs[nin + nco + 2 + nin + nco]

        @pl.when(pl.program_id(0) == 0)
        def _():
            ds[...] = jnp.zeros_like(ds)
            for c in couts:
                c[...] = jnp.zeros_like(c)

        cvals = [r[...] for r in refs[nin:nin + nco]]
        dstate = ds[...]
        for sub in reversed(range(cps)):
            rr = pl.ds(sub * CH, CH)
            tiles = [r[:, rr, :].reshape(nb * CH, r.shape[2]) for r in refs[:nin]]
            _, vjp = jax.vjp(f, *tiles, *cvals, hist_ref[sub].astype(f32))
            grads = vjp((do_ref[:, rr, :].reshape(nb * CH, out_k * w), dstate))
            for r, gk in zip(outs, grads[:nin]):
                r[:, rr, :] = gk.reshape(nb, CH, r.shape[2]).astype(r.dtype)
            for c, gk in zip(couts, grads[nin:nin + nco]):
                c[...] += gk
            dstate = grads[nin + nco]
        ds[...] = dstate

    seq3 = lambda a: a.reshape(nb, nchunk * CH, a.shape[1])
    rev = lambda c, base: (0, nstep - 1 - c, base)
    res = _pcall(
        side, body, name=name, grid=(nstep,),
        in_specs=[pl.BlockSpec((nb, cps * CH, k * w), functools.partial(rev, base=b)) for (_, b, k) in rows]
        + [pl.BlockSpec(c.shape, lambda c: (0, 0)) for c in consts]
        + [pl.BlockSpec((cps, nb * w, HD), lambda c: (nstep - 1 - c, 0, 0)),
           pl.BlockSpec((nb, cps * CH, out_k * w), functools.partial(rev, base=0))],
        out_specs=[pl.BlockSpec((nb, cps * CH, k * w), functools.partial(rev, base=0)) for (_, _, k) in rows]
        + [pl.BlockSpec(c.shape, lambda c: (0, 0)) for c in consts],
        out_shape=[S((nb, nchunk * CH, k * w), dt) for (_, _, k), dt in zip(rows, dtypes or [f32] * nin)]
        + [S(c.shape, f32) for c in consts],
        scratch_shapes=[pltpu.VMEM((nb * w, HD), f32)],
        compiler_params=pltpu.CompilerParams(dimension_semantics=("arbitrary",)),
    )(*[seq3(r[0]) for r in rows], *consts, hist, seq3(do))
    outs, extra = (res, None) if side is None else res
    outs = [o.reshape(n, o.shape[2]) for o in outs[:nin]] + list(outs[nin:])
    return outs if side is None else (outs, extra)


_VMEM_LIMIT = 56 * 2 ** 20
_VMEM_TILE_BUDGET = 40 * 2 ** 20


def _mm_tiles(m, n, k, sa, sb, so, sx, a_f32, b_f32, tn_fixed):
    best = None
    for tm in (1024, 512, 256, 128, 64):
        for tn in ((tn_fixed,) if tn_fixed else (1024, 768, 512, 384, 256, 128)):
            if m % tm or n % tn:
                continue
            need = 2 * (tm * k * sa + k * tn * sb + tm * tn * (so + sx)) + tm * tn * 4
            need += tm * k * (2 if sa == 4 else 0) + k * tn * (2 if sb == 4 else 0)
            need += tm * k * (4 if a_f32 else 0) + k * tn * (4 if b_f32 else 0)
            if need <= _VMEM_TILE_BUDGET and (best is None or (tm * tn, tm) > best[0]):
                best = ((tm * tn, tm), tm, tn)
    return best[1], best[2]


def _mm(name, a, b, dims, out_dtype, a_fn=None, b_fn=None, epi=None, extra=(), consts=(), outs=None, out_t=False, slab=False,
        side=None):
    m, kk = a.shape
    gathered = b.ndim == 3
    if gathered:
        nn = NDEV * _COLW if dims is NN else b.shape[1]
    else:
        nn = b.shape[1] if dims is NN else b.shape[0]
    kinds = [("tile", out_dtype)] if outs is None else list(outs)
    so = sum(jnp.dtype(dt).itemsize for kd, dt in kinds if kd != "rows")
    sx = sum(e.dtype.itemsize for e in extra)
    full_rows = bool(consts) or any(kd == "rows" for kd, _ in kinds)
    tm, tn = _mm_tiles(m, nn, kk, a.dtype.itemsize, b.dtype.itemsize, so, sx,
                       a_fn is not None, b_fn is not None,
                       _COLW if slab else (nn if full_rows else None))
    gblocks = tn // _COLW if (gathered and dims is NN) else 0
    assert not gblocks or (outs is None and not extra and not out_t and not slab and b_fn is None)
    nex, nco = len(extra), len(consts)

    def body(a_ref, b_ref, *rest):
        av = a_ref[...]
        if a_fn is not None:
            av = a_fn(av.astype(f32))
        av = av.astype(bf16)
        if gblocks:
            for gb in range(gblocks):
                part = _dot(av, b_ref[gb], NN)
                rest[nex + nco][:, _COLW * gb:_COLW * (gb + 1)] = (part if epi is None else epi(part)).astype(rest[nex + nco].dtype)
            return
        if gathered:
            acc = sum(_dot(av[:, _COLW * d:_COLW * (d + 1)], b_ref[d], NT) for d in range(NDEV))
        else:
            bv = b_ref[...]
            if b_fn is not None:
                bv = b_fn(bv.astype(f32))
            acc = _dot(av, bv.astype(bf16), dims)
        if epi is not None:
            acc = epi(acc, *[r[...] for r in rest[:nex + nco]])
        vals = acc if isinstance(acc, tuple) else (acc,)
        for (kd, _), o_ref, val in zip(kinds, rest[nex + nco:], vals):
            if kd == "rows":
                @pl.when(pl.program_id(0) == 0)
                def _():
                    o_ref[...] = jnp.zeros_like(o_ref)

                o_ref[...] += val
            elif kd == "tile_t" or out_t:
                o_ref[...] = val.T.astype(o_ref.dtype)
            elif slab:
                o_ref[0] = val.astype(o_ref.dtype)
            else:
                o_ref[...] = val.astype(o_ref.dtype)

    if gathered:
        bspec = (pl.BlockSpec((gblocks, kk, _COLW), lambda i, j: (j, 0, 0)) if dims is NN
                 else pl.BlockSpec((NDEV, tn, _COLW), lambda i, j: (0, j, 0)))
    else:
        bspec = pl.BlockSpec((kk, tn), lambda i, j: (0, j)) if dims is NN else pl.BlockSpec((tn, kk), lambda i, j: (j, 0))
    out_specs, out_shape = [], []
    for kd, dt in kinds:
        if kd == "rows":
            out_specs.append(pl.BlockSpec((1, nn), lambda i, j: (0, 0)))
            out_shape.append(S((1, nn), dt))
        elif kd == "tile_t" or out_t:
            out_specs.append(pl.BlockSpec((tn, tm), lambda i, j: (j, i)))
            out_shape.append(S((nn, m), dt))
        elif slab:
            out_specs.append(pl.BlockSpec((1, tm, tn), lambda i, j: (j, i, 0)))
            out_shape.append(S((nn // tn, m, tn), dt))
        else:
            out_specs.append(pl.BlockSpec((tm, tn), lambda i, j: (i, j)))
            out_shape.append(S((m, nn), dt))
    if outs is None:
        out_specs, out_shape = out_specs[0], out_shape[0]
    sem = ("arbitrary", "arbitrary") if any(kd == "rows" for kd, _ in kinds) else ("parallel", "parallel")
    return _pcall(
        side, body, name=name, grid=(m // tm, nn // tn),
        in_specs=[pl.BlockSpec((tm, kk), lambda i, j: (i, 0)), bspec]
        + [pl.BlockSpec((tm, tn), lambda i, j: (i, j)) for _ in extra]
        + [pl.BlockSpec(c.shape, lambda i, j: (0, 0)) for c in consts],
        out_specs=out_specs, out_shape=out_shape,
        compiler_params=pltpu.CompilerParams(dimension_semantics=sem, vmem_limit_bytes=_VMEM_LIMIT),
    )(a, b, *extra, *consts)


def _f_norm(x, g):
    return (x * lax.rsqrt(jnp.mean(x * x, axis=-1, keepdims=True) + EPS) * g,)


def _f_foxpre(zqk, gq, gk, pm):
    def nrm(t, g):
        return t * lax.rsqrt(_sel_rhs(t * t, pm) + EPS) * g
    return nrm(zqk[:, :FOX_W], gq), nrm(zqk[:, FOX_W:], gk)


def _chunk_cumsum(x):
    n = x.shape[0]
    r, c = _iota2((n, n), 0), _iota2((n, n), 1)
    tri = jnp.logical_and(r >= c, (r // CH) == (c // CH)).astype(f32)
    return _sel_lhs(tri, x)


def _f_gdngate(zs, eb, ea, alog_b, dt_b):
    beta = jax.nn.sigmoid(_sel_rhs(zs, eb))
    la = -jnp.exp(alog_b) * jax.nn.softplus(_sel_rhs(zs, ea) + dt_b)
    return beta, _chunk_cumsum(la)


def _f_conv(j, x, w):
    t = x.shape[0]
    y = x * w[3:4, :]
    for jj in range(3):
        sh = 3 - jj
        xs = jnp.concatenate([jnp.zeros((sh, x.shape[1]), f32), x[:t - sh, :]], axis=0)
        y = y + xs * w[jj:jj + 1, :]
    y = jax.nn.silu(y)
    yn = y * lax.rsqrt(jnp.sum(y * y, axis=-1, keepdims=True) + EPS)
    return (jnp.where(j < 2 * GDN_H, yn, y),)


def _head_rms(o, nh):
    outs = []
    for h in range(nh):
        oh = o[:, HD * h:HD * (h + 1)]
        outs.append(oh * lax.rsqrt(jnp.mean(oh * oh, axis=-1, keepdims=True) + EPS))
    return jnp.concatenate(outs, axis=1)


def _f_post0(fox_o, o, gg, on):
    return (jnp.concatenate([fox_o, _head_rms(o, GDN_H) * on * jax.nn.silu(gg)], axis=1),)


def _f_post1(o, zg, on):
    return (_head_rms(o, HG_H) * on * jax.nn.silu(zg),)


def _f_hpre(zqf, lbl):
    lb = jax.nn.sigmoid(lbl[1:2, :] - lbl[0:1, :])
    fg = lb + (1.0 - lb) * jax.nn.sigmoid(zqf[:, D:])
    return jax.nn.silu(zqf[:, :D]), 1.0 - fg, _chunk_cumsum(jnp.log(fg))


def _dotb(a, b, dims=NN):
    return _dot(a.astype(bf16), b.astype(bf16), dims)


def _dot3(a, b):
    ah, bh = a.astype(bf16), b.astype(bf16)
    al, bl = (a - ah.astype(f32)).astype(bf16), (b - bh.astype(f32)).astype(bf16)
    return _dot(ah, bh) + (_dot(ah, bl) + _dot(al, bh))


def _split(t, nh):
    return [t[CH * ck:CH * (ck + 1), HD * h:HD * (h + 1)] for ck in range(t.shape[0] // CH) for h in range(nh)]


def _merge(units, nh):
    return jnp.concatenate([jnp.concatenate(units[i:i + nh], axis=1) for i in range(0, len(units), nh)], axis=0)


def _inv_impl(amats):
    n = amats[0].shape[0]
    eye = jnp.where(_iota2((n, n), 0) == _iota2((n, n), 1), 1.0, 0.0).astype(f32)
    xs, ps = [eye - a for a in amats], list(amats)
    for _ in range(max(1, (n - 1).bit_length()) - 1):
        ps = [_dotb(p, p) for p in ps]
        xs = [x + _dotb(x, p) for x, p in zip(xs, ps)]
    for _ in range(3):
        rs = [eye - x - _dot3(a, x) for a, x in zip(amats, xs)]
        xs = [x + _dotb(x, r) for x, r in zip(xs, rs)]
    return tuple(xs)


@jax.custom_vjp
def _inv_unit_lower(amats):
    return _inv_impl(amats)


def _inv_fwd(amats):
    xs = _inv_impl(amats)
    return xs, xs


def _inv_bwd(xs, dxs):
    return (tuple(-_dotb(_dotb(x, dx, TN), x, NT) for x, dx in zip(xs, dxs)),)


_inv_unit_lower.defvjp(_inv_fwd, _inv_bwd)


@jax.custom_vjp
def _inv_given(amats, xs):
    return xs


def _inv_given_fwd(amats, xs):
    return xs, xs


def _inv_given_bwd(xs, dxs):
    return _inv_bwd(xs, dxs)[0], tuple(jnp.zeros_like(x) for x in xs)


_inv_given.defvjp(_inv_given_fwd, _inv_given_bwd)


def _f_gdn_intra(q, k, v, bb, gb, tinv_p=None):
    qs, ks, vs, bs, gs = (_split(t, GDN_H) for t in (q, k, v, bb, gb))
    r, cc = _iota2((CH, CH), 0), _iota2((CH, CH), 1)
    causal, strict = r >= cc, r > cc
    beta, g, gl = [b[:, :1] for b in bs], [x[:, :1] for x in gs], [x[CH - 1:CH, :1] for x in gs]
    decay = [jnp.exp(jnp.where(causal, x[:, :CH] - x[:, :CH].T, -jnp.inf)) for x in gs]
    kb = [ki * bi for ki, bi in zip(ks, beta)]
    amat = [jnp.where(strict, _dotb(kbi, ki, NT) * di, 0.0) for kbi, ki, di in zip(kb, ks, decay)]
    if tinv_p is None:
        tinv = _inv_unit_lower(tuple(amat))
    else:
        tinv = _inv_given(tuple(amat), tuple(x[:, :CH] for x in _split(tinv_p, GDN_H)))
    rhs = [jnp.concatenate([vi * bi, kbi * jnp.exp(gi)], axis=1) for vi, bi, kbi, gi in zip(vs, beta, kb, g)]
    uw = [_dotb(ti, ri) for ti, ri in zip(tinv, rhs)]
    qsc = [qi * (HD ** -0.5) for qi in qs]
    qk = [jnp.where(causal, _dotb(qi, ki, NT) * di, 0.0) for qi, ki, di in zip(qsc, ks, decay)]
    outs = ([x[:, :HD] for x in uw], [x[:, HD:] for x in uw],
            [jnp.concatenate([x, jnp.zeros_like(x)], axis=1) for x in qk],
            [qi * jnp.exp(gi) for qi, gi in zip(qsc, g)],
            [ki * jnp.exp(gli - gi) for ki, gli, gi in zip(ks, gl, g)],
            [jnp.broadcast_to(gli, (CH, HD)) for gli in gl])
    if tinv_p is None:
        outs += ([jnp.concatenate([x, jnp.zeros_like(x)], axis=1) for x in tinv],)
    return tuple(_merge(o, GDN_H) for o in outs)


def _f_gdn_gated(q, k, v, zs, eb, ea, alog_b, dt_b):
    return _f_gdn_intra(q, k, v, *_f_gdngate(zs, eb, ea, alog_b, dt_b))


def _f_gdn_gated_given(q, k, v, zs, tinv_p, eb, ea, alog_b, dt_b):
    return _f_gdn_intra(q, k, v, *_f_gdngate(zs, eb, ea, alog_b, dt_b), tinv_p=tinv_p)


def _f_gdn_inter(u, w, qkp, qd, kd, glb, st):
    us, ws, qks, qds, kds, gls = (_split(t, GDN_H) for t in (u, w, qkp, qd, kd, glb))
    sts = [st[HD * i:HD * (i + 1), :] for i in range(len(us))]
    vn = [ui - _dotb(wi, si) for ui, wi, si in zip(us, ws, sts)]
    o = [_dotb(qi, si) + _dotb(xi[:, :CH], vi) for qi, si, xi, vi in zip(qds, sts, qks, vn)]
    s2 = [si * jnp.exp(gi[:1, :1]) + _dotb(ki, vi, TN) for si, gi, ki, vi in zip(sts, gls, kds, vn)]
    return _merge(o, GDN_H), jnp.concatenate(s2, axis=0)


def _f_gdn_full(u, w, qkp, qd, kd, glb, fox_o, gg, on, st):
    o, s2 = _f_gdn_inter(u, w, qkp, qd, kd, glb, st)
    return _f_post0(fox_o, o, gg, on)[0], s2


def _f_hgrn_chunk(q, k, b, v, st):
    qs, ks, bs, vs = (_split(t, HG_H) for t in (q, k, b, v))
    sts = [st[HD * i:HD * (i + 1), :] for i in range(len(qs))]
    causal = _iota2((CH, CH), 0) >= _iota2((CH, CH), 1)
    bl, bm = [x[CH - 1:CH, :] for x in bs], [x[CH // 2 - 1:CH // 2, :] for x in bs]
    a = [jnp.where(causal, _dotb(qi * jnp.exp(bi - mi), ki * jnp.exp(mi - bi), NT), 0.0)
         for qi, ki, bi, mi in zip(qs, ks, bs, bm)]
    o = [_dotb(qi * jnp.exp(bi), si, NT) + _dotb(ai, vi) for qi, bi, si, ai, vi in zip(qs, bs, sts, a, vs)]
    s2 = [si * jnp.exp(li) + _dotb(vi, ki * jnp.exp(li - bi), TN) for si, li, vi, ki, bi in zip(sts, bl, vs, ks, bs)]
    return _merge(o, HG_H), jnp.concatenate(s2, axis=0)


def _f_hgrn_full(z, lbl, on, st):
    o, s2 = _f_hgrn_chunk(*_f_hpre(z[:, :2 * D], lbl), z[:, 2 * D:3 * D], st)
    return _f_post1(o, z[:, 3 * D:], on)[0], s2


def _fox_gate_fwd(z0, fbias, t, tc=256):
    n = z0.shape[0]
    nt = t // tc

    def body(zs_ref, b_ref, ccol_ref, crow_ref, carry):
        @pl.when(pl.program_id(1) == 0)
        def _():
            carry[...] = jnp.zeros_like(carry)

        ls = jnp.where(_iota2((tc, LANES), 1) < FOX_H, jax.nn.log_sigmoid(zs_ref[...] + b_ref[...]), 0.0)
        tri = (_iota2((tc, tc), 0) >= _iota2((tc, tc), 1)).astype(f32)
        c = _dot_sel(tri, ls, exact_lhs=True) + carry[...]
        carry[...] = c[tc - 1:tc, :]
        ccol_ref[...] = c
        crow_ref[0] = c.T[:FOX_H, :]

    return pl.pallas_call(
        body, name="fox_gate_fwd", grid=(n // t, nt),
        in_specs=[pl.BlockSpec((tc, LANES), lambda b, i: (b * nt + i, Z0_SMALL)), pl.BlockSpec((1, LANES), lambda b, i: (0, 0))],
        out_specs=[pl.BlockSpec((tc, LANES), lambda b, i: (b * nt + i, 0)), pl.BlockSpec((1, FOX_H, tc), lambda b, i: (b, 0, i))],
        out_shape=[S((n, LANES), f32), S((n // t, FOX_H, t), f32)],
        scratch_shapes=[pltpu.VMEM((1, LANES), f32)],
        compiler_params=pltpu.CompilerParams(dimension_semantics=("parallel", "arbitrary")),
    )(z0, fbias)


def _fox_gate_bwd(z0, fbias, dcq, dck, t, tc=256):
    n = z0.shape[0]
    nt = t // tc

    def body(zs_ref, b_ref, dcq_ref, dck_ref, dz_ref, db_ref, carry):
        first = jnp.logical_and(pl.program_id(0) == 0, pl.program_id(1) == 0)

        @pl.when(pl.program_id(1) == 0)
        def _():
            carry[...] = jnp.zeros_like(carry)

        @pl.when(first)
        def _():
            db_ref[...] = jnp.zeros_like(db_ref)

        dc = dcq_ref[0] + dcq_ref[1] + dcq_ref[2] + dcq_ref[3]
        drow = dck_ref[0, 0] + dck_ref[1, 0] + dck_ref[2, 0] + dck_ref[3, 0]
        eye = (_iota2((FOX_H, LANES), 0) == _iota2((FOX_H, LANES), 1)).astype(f32)
        dc = dc + _dot_sel(drow, eye, TN)
        triu = (_iota2((tc, tc), 0) <= _iota2((tc, tc), 1)).astype(f32)
        dls = _dot_sel(triu, dc, exact_lhs=True) + carry[...]
        carry[...] = dls[0:1, :]
        x = zs_ref[...] + b_ref[...]
        dz = jnp.where(_iota2((tc, LANES), 1) < FOX_H, dls * jax.nn.sigmoid(-x), 0.0)
        dz_ref[...] = dz
        db_ref[...] += jnp.sum(dz, axis=0, keepdims=True)

    def rev(b, i):
        return b * nt + (nt - 1 - i)

    return pl.pallas_call(
        body, name="fox_gate_bwd", grid=(n // t, nt),
        in_specs=[pl.BlockSpec((tc, LANES), lambda b, i: (rev(b, i), Z0_SMALL)), pl.BlockSpec((1, LANES), lambda b, i: (0, 0)),
                  pl.BlockSpec((4, tc, LANES), lambda b, i: (0, rev(b, i), 0)),
                  pl.BlockSpec((4, 1, FOX_H, tc), lambda b, i: (0, b, 0, nt - 1 - i))],
        out_specs=[pl.BlockSpec((tc, LANES), lambda b, i: (rev(b, i), 0)), pl.BlockSpec((1, LANES), lambda b, i: (0, 0))],
        out_shape=[S((n, LANES), f32), S((1, LANES), f32)],
        scratch_shapes=[pltpu.VMEM((1, LANES), f32)],
        compiler_params=pltpu.CompilerParams(dimension_semantics=("arbitrary", "arbitrary")),
    )(z0, fbias, dcq, dck)


def _fox_scores(hh, p, i, tq, q, k, ccol, crow):
    kmax = k.shape[0]
    lane = _iota2((1, LANES), 1)
    mh = (lane // FOX_D) == hh
    h = 2 * p + hh
    qh = jnp.where(mh, q, 0.0).astype(bf16)
    s = _dot(qh, k, NT) * (FOX_D ** -0.5)
    cq = jnp.sum(jnp.where(lane == h, ccol, 0.0), axis=1, keepdims=True)
    ck = jnp.sum(jnp.where(_iota2((FOX_H, 1), 0) == h, crow, 0.0), axis=0, keepdims=True)
    causal = _iota2((1, kmax), 1) <= (i * tq + _iota2((tq, 1), 0))
    s = jnp.where(causal, s + cq - ck, -jnp.inf)
    pe = jnp.exp(s - jnp.max(s, axis=1, keepdims=True))
    return mh, qh, pe, jnp.sum(pe, axis=1, keepdims=True)


def _fox_attn_fwd(qn, kn, z0, ccol, crow, t, tq=256, side=None):
    n = qn.shape[0]
    nq = t // tq

    def body(q_ref, k_ref, v_ref, ccol_ref, crow_ref, o_ref):
        p = pl.program_id(1)
        k, v, crow = k_ref[...].astype(bf16), v_ref[...].astype(bf16), crow_ref[0]
        for i in range(nq):
            rows, kmax = pl.ds(i * tq, tq), (i + 1) * tq
            q, cc = q_ref[rows, :], ccol_ref[rows, :]
            acc = jnp.zeros((tq, LANES), f32)
            for hh in range(2):
                mh, _, pe, l = _fox_scores(hh, p, i, tq, q, k[:kmax], cc, crow[:, :kmax])
                acc = jnp.where(mh, _dot(pe.astype(bf16), v[:kmax]) / l, acc)
            o_ref[rows, :] = acc

    seq = lambda b, p: (b, p)
    return _pcall(
        side, body, name="fox_attn_fwd", grid=(n // t, FOX_H // 2),
        in_specs=[pl.BlockSpec((t, LANES), seq), pl.BlockSpec((t, LANES), seq), pl.BlockSpec((t, LANES), lambda b, p: (b, Z0_FV + p)),
                  pl.BlockSpec((t, LANES), lambda b, p: (b, 0)), pl.BlockSpec((1, FOX_H, t), lambda b, p: (b, 0, 0))],
        out_specs=pl.BlockSpec((t, LANES), seq),
        out_shape=S((n, FOX_W), f32),
        compiler_params=pltpu.CompilerParams(dimension_semantics=("parallel", "parallel")),
    )(qn, kn, z0, ccol, crow)


def _fox_attn_bwd(qn, kn, z0, ccol, crow, do, t, tq=256, side=None):
    n = qn.shape[0]
    nq = t // tq
    nb = n // t

    def body(q_ref, k_ref, v_ref, ccol_ref, crow_ref, do_ref, dq_ref, dk_ref, dv_ref, dcq_ref, dck_ref):
        p = pl.program_id(1)
        dk_ref[...] = jnp.zeros_like(dk_ref)
        dv_ref[...] = jnp.zeros_like(dv_ref)
        dck_ref[...] = jnp.zeros_like(dck_ref)
        kf, v, crow = k_ref[...], v_ref[...].astype(bf16), crow_ref[0]
        k = kf.astype(bf16)
        lane = _iota2((1, LANES), 1)
        sub = _iota2((FOX_H, 1), 0)
        scale = FOX_D ** -0.5
        for i in range(nq):
            rows, kmax = pl.ds(i * tq, tq), (i + 1) * tq
            q, cc, dout = q_ref[rows, :], ccol_ref[rows, :], do_ref[rows, :]
            dq = jnp.zeros((tq, LANES), f32)
            dcq = jnp.zeros((tq, LANES), f32)
            for hh in range(2):
                mh, qh, pe, l = _fox_scores(hh, p, i, tq, q, k[:kmax], cc, crow[:, :kmax])
                pr = pe / l
                doh = jnp.where(mh, dout, 0.0).astype(bf16)
                dp = _dot(doh, v[:kmax], NT)
                ds = pr * (dp - jnp.sum(pr * dp, axis=1, keepdims=True))
                dsb = ds.astype(bf16)
                dq = dq + _dot(dsb, jnp.where(mh, kf[:kmax], 0.0).astype(bf16)) * scale
                dk_ref[:kmax, :] += _dot(dsb, qh, TN) * scale
                dv_ref[:kmax, :] += _dot(pr.astype(bf16), doh, TN)
                h = 2 * p + hh
                dcq = dcq + jnp.where(lane == h, jnp.sum(ds, axis=1, keepdims=True), 0.0)
                dck_ref[0, 0, :, :kmax] += jnp.where(sub == h, -jnp.sum(ds, axis=0, keepdims=True), 0.0)
            dq_ref[rows, :] = dq
            dcq_ref[0, rows, :] = dcq

    seq = lambda b, p: (b, p)
    return _pcall(
        side, body, name="fox_attn_bwd", grid=(nb, FOX_H // 2),
        in_specs=[pl.BlockSpec((t, LANES), seq), pl.BlockSpec((t, LANES), seq), pl.BlockSpec((t, LANES), lambda b, p: (b, Z0_FV + p)),
                  pl.BlockSpec((t, LANES), lambda b, p: (b, 0)), pl.BlockSpec((1, FOX_H, t), lambda b, p: (b, 0, 0)),
                  pl.BlockSpec((t, LANES), seq)],
        out_specs=[pl.BlockSpec((t, LANES), seq), pl.BlockSpec((t, LANES), seq), pl.BlockSpec((t, LANES), seq),
                   pl.BlockSpec((1, t, LANES), lambda b, p: (p, b, 0)), pl.BlockSpec((1, 1, FOX_H, t), lambda b, p: (p, b, 0, 0))],
        out_shape=[S((n, FOX_W), f32), S((n, FOX_W), f32), S((n, FOX_W), f32), S((4, n, LANES), f32), S((4, nb, FOX_H, t), f32)],
        compiler_params=pltpu.CompilerParams(dimension_semantics=("parallel", "parallel")),
    )(qn, kn, z0, ccol, crow, do)


def _adamw_math(w, g, m, v):
    m = ADAM_B1 * m + (1.0 - ADAM_B1) * g
    v = ADAM_B2 * v + (1.0 - ADAM_B2) * (g * g)
    m_hat = m / (1.0 - ADAM_B1 ** ADAM_STEP)
    v_hat = v / (1.0 - ADAM_B2 ** ADAM_STEP)
    return -ADAM_LR * (m_hat / (jnp.sqrt(v_hat) + ADAM_EPS) + ADAM_WD * w), m, v


def _adamw_big(name, idx, gmine, recv1, recv2, w, m, v):
    r, wc = w.shape
    c = gmine.shape[2]
    tr = min(r, 256)

    def body(idx_ref, gm_ref, r1_ref, r2_ref, w_ref, m_ref, v_ref, g_ref, d_ref, nm_ref, nv_ref):
        g = gm_ref[0].astype(f32) + r1_ref[0].astype(f32)
        for k in range(3):
            g = g + r2_ref[k].astype(f32)
        g = g[:, :wc]
        d, nm, nv = _adamw_math(w_ref[...], g, m_ref[...], v_ref[...])
        g_ref[...] = g
        d_ref[...] = d
        nm_ref[...] = nm
        nv_ref[...] = nv

    row = pl.BlockSpec((tr, wc), lambda i, s: (i, 0))
    return pl.pallas_call(
        body, name=name,
        grid_spec=pltpu.PrefetchScalarGridSpec(
            num_scalar_prefetch=1, grid=(r // tr,),
            in_specs=[pl.BlockSpec((1, tr, c), lambda i, s: (s[0], i, 0)), pl.BlockSpec((1, tr, c), lambda i, s: (s[1], i, 0)),
                      pl.BlockSpec((3, tr, c), lambda i, s: (0, i, 0)), row, row, row],
            out_specs=[row, row, row, row]),
        out_shape=[S((r, wc), f32)] * 4,
        compiler_params=pltpu.CompilerParams(dimension_semantics=("parallel",)),
    )(idx, gmine, recv1, recv2, w, m, v)


def _pair_sum(name, idx, gmine, recv1):
    _, r, c = gmine.shape
    g4 = gmine.reshape(4, 2, r, c)

    def body(idx_ref, gm_ref, r1_ref, o_ref):
        o_ref[0] = (gm_ref[0, 0].astype(f32) + r1_ref[0].astype(f32)).astype(bf16)

    return pl.pallas_call(
        body, name=name,
        grid_spec=pltpu.PrefetchScalarGridSpec(
            num_scalar_prefetch=1, grid=(4,),
            in_specs=[pl.BlockSpec((1, 1, r, c), lambda ch, s: (ch, s[0], 0, 0)), pl.BlockSpec((1, r, c), lambda ch, s: (ch, 0, 0))],
            out_specs=pl.BlockSpec((1, r, c), lambda ch, s: (ch, 0, 0))),
        out_shape=S((4, r, c), bf16),
        compiler_params=pltpu.CompilerParams(dimension_semantics=("parallel",)),
    )(idx, g4, recv1)


def _adamw_small(parts, w, m, v, own_mask, own_g):
    def body(p_ref, w_ref, m_ref, v_ref, mask_ref, og_ref, g_ref, d_ref, nm_ref, nv_ref):
        g = p_ref[0]
        for k in range(1, NDEV):
            g = g + p_ref[k]
        g_ref[...] = g
        ge = jnp.where(mask_ref[...] > 0.5, og_ref[...], g)
        d, nm, nv = _adamw_math(w_ref[...], ge, m_ref[...], v_ref[...])
        d_ref[...] = d
        nm_ref[...] = nm
        nv_ref[...] = nv

    return pl.pallas_call(body, name="adamw_small", out_shape=[S(w.shape, f32)] * 4)(parts, w, m, v, own_mask, own_g)


def _sum_parts(parts):
    def body(p_ref, g_ref):
        g = p_ref[0]
        for k in range(1, NDEV):
            g = g + p_ref[k]
        g_ref[...] = g

    return pl.pallas_call(body, name="sum_parts", out_shape=S(parts.shape[1:], f32))(parts)


def _me():
    return lax.axis_index("x"), lax.axis_index("y"), lax.axis_index("c")


def _hbm_specs(n):
    return [pl.BlockSpec(memory_space=pl.ANY)] * n


def _allgather(name, xs):
    na = len(xs)

    def body(*refs):
        x_refs, out_refs = refs[:na], refs[na:2 * na]
        send_sems, recv_sems, local_sems = refs[2 * na:]
        mx, my, mc = _me()
        me, sib = (mx, my, mc), (mx, my, 1 - mc)
        chips = [(1 - mx, my), (mx, 1 - my), (1 - mx, 1 - my)]

        def slab(a, px, py, pc):
            return out_refs[a].at[4 * px + 2 * py + pc]

        def copy(a, k, block, to, own=False):
            return pltpu.make_async_remote_copy(
                src_ref=x_refs[a] if own else slab(a, *block), dst_ref=slab(a, *block),
                send_sem=send_sems.at[7 * a + k], recv_sem=recv_sems.at[7 * a + k], device_id=to, device_id_type=MESH)

        mine = [pltpu.make_async_copy(x_refs[a], slab(a, *me), local_sems.at[a]) for a in range(na)]
        first = []
        for a in range(na):
            mine[a].start()
            first += [copy(a, 0, me, sib, own=True)] + [copy(a, 1 + j, me, (*chip, mc), own=True) for j, chip in enumerate(chips)]
        for cp in first:
            cp.start()
        passed = []
        for j, chip in enumerate(chips):
            for a in range(na):
                copy(a, 1 + j, (*chip, mc), me).wait_recv()
                passed.append(copy(a, 4 + j, (*chip, mc), sib))
                passed[-1].start()
        for a in range(na):
            copy(a, 0, sib, me).wait_recv()
            for j, chip in enumerate(chips):
                copy(a, 4 + j, (*chip, 1 - mc), me).wait_recv()
        for cp in first + passed:
            cp.wait_send()
        for cp in mine:
            cp.wait()

    return pl.pallas_call(
        body, name=name, out_shape=[S((NDEV,) + x.shape, x.dtype) for x in xs],
        in_specs=_hbm_specs(na), out_specs=_hbm_specs(na),
        scratch_shapes=[pltpu.SemaphoreType.DMA((7 * na,)), pltpu.SemaphoreType.DMA((7 * na,)), pltpu.SemaphoreType.DMA((na,))],
    )(*xs)


def _rs_sibling(gs):
    na = len(gs)

    def body(*refs):
        g_refs, out_refs, send_sems, recv_sems = refs[:na], refs[na:2 * na], refs[2 * na], refs[2 * na + 1]
        mx, my, mc = _me()
        cps = [pltpu.make_async_remote_copy(
            src_ref=g_refs[a].at[2 * ch + 1 - mc], dst_ref=out_refs[a].at[ch], send_sem=send_sems.at[4 * a + ch],
            recv_sem=recv_sems.at[4 * a + ch], device_id=(mx, my, 1 - mc), device_id_type=MESH)
            for a in range(na) for ch in range(4)]
        for cp in cps:
            cp.start()
        for cp in cps:
            cp.wait_recv()
        for cp in cps:
            cp.wait_send()

    return pl.pallas_call(
        body, name="rs_sibling", out_shape=[S((4,) + g.shape[1:], g.dtype) for g in gs],
        in_specs=_hbm_specs(na), out_specs=_hbm_specs(na),
        scratch_shapes=[pltpu.SemaphoreType.DMA((4 * na,)), pltpu.SemaphoreType.DMA((4 * na,))],
    )(*gs)


def _side_allgather(xs):
    na = len(xs)

    def mk(x_refs, out_refs, sems):
        send_sems, recv_sems, local_sems = sems
        mx, my, mc = _me()
        me, sib = (mx, my, mc), (mx, my, 1 - mc)
        chips = [(1 - mx, my), (mx, 1 - my), (1 - mx, 1 - my)]

        def slab(a, px, py, pc):
            return out_refs[a].at[4 * px + 2 * py + pc]

        def copy(a, k, block, to, own=False):
            return pltpu.make_async_remote_copy(
                src_ref=x_refs[a] if own else slab(a, *block), dst_ref=slab(a, *block),
                send_sem=send_sems.at[7 * a + k], recv_sem=recv_sems.at[7 * a + k], device_id=to, device_id_type=MESH)

        mine = [pltpu.make_async_copy(x_refs[a], slab(a, *me), local_sems.at[a]) for a in range(na)]
        first = []
        for a in range(na):
            first += [copy(a, 0, me, sib, own=True)] + [copy(a, 1 + j, me, (*chip, mc), own=True) for j, chip in enumerate(chips)]
        return me, sib, chips, mc, copy, mine, first

    def start(x_refs, out_refs, sems):
        *_, mine, first = mk(x_refs, out_refs, sems)
        for cp in mine + first:
            cp.start()

    def finish(x_refs, out_refs, sems):
        me, sib, chips, mc, copy, mine, first = mk(x_refs, out_refs, sems)
        passed = []
        for j, chip in enumerate(chips):
            for a in range(na):
                copy(a, 1 + j, (*chip, mc), me).wait_recv()
                passed.append(copy(a, 4 + j, (*chip, mc), sib))
                passed[-1].start()
        for a in range(na):
            copy(a, 0, sib, me).wait_recv()
            for j, chip in enumerate(chips):
                copy(a, 4 + j, (*chip, 1 - mc), me).wait_recv()
        for cp in first + passed:
            cp.wait_send()
        for cp in mine:
            cp.wait()

    scratch = [pltpu.SemaphoreType.DMA((7 * na,)), pltpu.SemaphoreType.DMA((7 * na,)), pltpu.SemaphoreType.DMA((na,))]
    return _Side(xs, [S((NDEV,) + x.shape, x.dtype) for x in xs], scratch, start, finish)


def _side_exchange(arrs, nslot, out_slots, route):
    na = len(arrs)

    def copies(in_refs, out_refs, sems):
        send_sems, recv_sems = sems
        return [pltpu.make_async_remote_copy(
            src_ref=in_refs[a].at[src], dst_ref=out_refs[a].at[k], send_sem=send_sems.at[nslot * a + k],
            recv_sem=recv_sems.at[nslot * a + k], device_id=to, device_id_type=MESH)
            for a in range(na) for k, (src, to) in enumerate(route(*_me()))]

    def start(in_refs, out_refs, sems):
        for cp in copies(in_refs, out_refs, sems):
            cp.start()

    def finish(in_refs, out_refs, sems):
        cps = copies(in_refs, out_refs, sems)
        for cp in cps:
            cp.wait_recv()
        for cp in cps:
            cp.wait_send()

    scratch = [pltpu.SemaphoreType.DMA((nslot * na,)), pltpu.SemaphoreType.DMA((nslot * na,))]
    return _Side(arrs, [S((out_slots,) + x.shape[1:], x.dtype) for x in arrs], scratch, start, finish)


def _side_rs_sibling(gs):
    return _side_exchange(gs, 4, 4, lambda mx, my, mc: [(2 * ch + 1 - mc, (mx, my, 1 - mc)) for ch in range(4)])


def _side_rs_chips(ps):
    return _side_exchange(ps, 3, 3, lambda mx, my, mc: [(2 * cx + cy, (cx, cy, mc)) for cx, cy in
                                                        [(1 - mx, my), (mx, 1 - my), (1 - mx, 1 - my)]])


_COLW = 512
_COL_NAMES = ("l0_w_in", "l0_w_ff1", "l1_w_in", "l1_w_ff1")
_ROW_NAMES = ("l0_w_out", "l0_w_ff2", "l1_w_out", "l1_w_ff2")
_BIG_NAMES = _COL_NAMES + _ROW_NAMES


def _full_weight(gathered, name, ncols):
    if name in _ROW_NAMES:
        return gathered.reshape(-1, D)
    return gathered[:, :, :ncols].transpose(1, 0, 2).reshape(D, NDEV * ncols)


def _regroup_w_in0(w):
    main = jnp.concatenate([w[:, 0:1536], w[:, 1544:3080], w[:, 3088:3600]], axis=1)
    small = jnp.concatenate([w[:, 1536:1544], w[:, 3080:3088]], axis=1)
    return jnp.concatenate([main, small, jnp.zeros((D, ZW0 - 3584 - 16), w.dtype)], axis=1)


_W_IN0_RUNS = ((0, 1536, 0), (1536, 1544, 3584), (1544, 3080, 1536), (3080, 3088, 3592), (3088, 3600, 3072))


def _w_in0_slabs(g, ncols):
    slabs = []
    for d in range(NDEV):
        lo, hi = d * ncols, (d + 1) * ncols
        parts = [g[:, r + max(lo, a) - a:r + min(hi, b) - a] for a, b, r in _W_IN0_RUNS if max(lo, a) < min(hi, b)]
        slabs.append(jnp.concatenate(parts + [jnp.zeros((D, _COLW - ncols), g.dtype)], axis=1))
    return jnp.stack(slabs)


def _sq(t):
    return t * t


def _epi_res_norm(acc, res, gain):
    y = acc + res
    h = _f_norm(y, gain)[0]
    return y, h, h


_RES_NORM_OUTS = [("tile", f32), ("tile", bf16), ("tile_t", bf16)]


def _epi_norm_bwd(acc, x, dres, gain):
    _, vjp = jax.vjp(lambda xx, gg: _f_norm(xx, gg)[0], x, gain)
    dx, dgain = vjp(acc)
    dx = dx + dres
    return dx, dgain, dx


_NORM_BWD_OUTS = [("tile", f32), ("rows", f32), ("tile_t", bf16)]


def _epi_loss(acc, res, tgt):
    e = acc + res - tgt
    dy = e * (1.0 / D)
    return dy, dy, jnp.sum(e * e, axis=0, keepdims=True)


_LOSS_OUTS = [("tile", f32), ("tile_t", bf16), ("rows", f32)]


def _mlp_fwd(tag, x, h, w1, w2, epi, extra, consts, outs):
    a = _mm(f"{tag}_ff1", h, w1, NN, bf16, epi=lambda acc: jnp.maximum(acc, 0.0))
    return _mm(f"{tag}_ff2", a, w2, NN, f32, a_fn=_sq, epi=epi, extra=(x,) + tuple(extra), consts=consts, outs=outs), a


def _mlp_bwd(tag, x, gain, w1, w2, ht, a, dy, dyt, side=None):
    da = _mm(f"{tag}_ff2_dx", dy, w2, NT, bf16, epi=lambda acc, av: acc * 2.0 * av.astype(f32), extra=(a,), side=side)
    if side is not None:
        da, side_res = da
    dw2 = _mm(f"{tag}_ff2_dw", dyt, a, NN, bf16, b_fn=_sq, out_t=True)
    res = _mm(f"{tag}_ff1_dx", da, w1, NT, f32, epi=_epi_norm_bwd, extra=(x, dy), consts=(gain,), outs=_NORM_BWD_OUTS)
    dw1 = _mm(f"{tag}_ff1_dw", ht, da, NN, bf16, slab=True)
    return (res, dw1, dw2) if side is None else ((res, dw1, dw2), side_res)


def _row(v):
    return v.reshape(1, -1).astype(f32)


_L0_REST = ("l0_w_ff1", "l0_w_out")
_L1_MIX = ("l0_w_ff2", "l1_w_in", "l1_w_out")
_L1_FFN = ("l1_w_ff1", "l1_w_ff2")
_GRAD_A = ("l1_w_ff1", "l1_w_ff2", "l1_w_out", "l1_w_in")
_GRAD_B = ("l0_w_ff1", "l0_w_ff2", "l0_w_out")
_IN_PLACE = ("l0_w_ff1", "l1_w_in", "l1_w_ff1")


def _train(x, tgt, args, mom, var, t):
    n = x.shape[0]
    nchunk = t // CH
    mx, my, mc = _me()
    dev, chip = 4 * mx + 2 * my + mc, 2 * mx + my
    core_idx = jnp.reshape(mc, (1,)).astype(jnp.int32)
    own_idx = jnp.stack([dev, chip]).astype(jnp.int32)
    ncols = {nm: args[nm].shape[1] for nm in _COL_NAMES}
    conv_cols = args["l0_gdn_conv"].shape[1]
    g, big, w = {}, {}, {}

    def send(nm):
        a = args[nm].astype(bf16)
        return jnp.pad(a, ((0, 0), (0, _COLW - ncols[nm]))) if nm in _COL_NAMES else a

    def take(names, gathered):
        for nm, arr in zip(names, gathered):
            w[nm] = arr if nm in _IN_PLACE else _full_weight(arr, nm, ncols.get(nm, 0))

    def by_dev(nm, ga):
        return ga if nm in _COL_NAMES else ga.reshape(NDEV, -1, D)

    def pair(names, gs, r1s):
        return [_pair_sum(f"rs_pair_sum_{nm}", core_idx, ga, r1) for nm, ga, r1 in zip(names, gs, r1s)]

    def adam(names, gs, r1s, r2s):
        for nm, ga, r1, r2 in zip(names, gs, r1s, r2s):
            big[nm] = _adamw_big(f"adamw_{nm}", own_idx, ga, r1, r2, args[nm], mom[nm], var[nm])


    li = jnp.arange(FOX_W)
    pm = jnp.where((li[:, None] // FOX_D) == (li[None, :] // FOX_D), 1.0 / FOX_D, 0.0).astype(f32)
    lane_head = jnp.arange(GDN_W) // HD
    sel = lambda first_lane: (jnp.arange(LANES)[:, None] == (first_lane + lane_head)[None, :]).astype(f32)
    e_beta, e_alpha = sel(FOX_H), sel(FOX_H + GDN_H)
    alog_b, dt_b = _row(jnp.repeat(args["l0_gdn_A_log"], HD)), _row(jnp.repeat(args["l0_gdn_dt_bias"], HD))
    gq_t, gk_t = _row(jnp.tile(args["l0_fox_q_norm"], FOX_H)), _row(jnp.tile(args["l0_fox_k_norm"], FOX_H))
    on0_t, on1_t = _row(jnp.tile(args["l0_gdn_o_norm"], GDN_H)), _row(jnp.tile(args["l1_hgrn_o_norm"], HG_H))
    fbias = jnp.pad(_row(args["l0_fox_f_bias"]), ((0, 0), (0, LANES - FOX_H)))
    g0m, g0f, g1m, g1f = (_row(args[k]) for k in ("l0_mix_norm", "l0_ffn_norm", "l1_mix_norm", "l1_ffn_norm"))
    lbl = args["hgrn_lb_logits"].astype(f32)

    first = [send("l0_w_in"), jnp.pad(args["l0_gdn_conv"], ((0, 4), (0, LANES * 2 - conv_cols)))]
    (h0, h0t), first = _tok_fwd("l0_mix_norm", _f_norm, [(x, D, 0)], [(g0m, None, 0)], [(D, bf16)], TOK_ROWS, also_t=(0,),
                                side=_side_allgather(first))
    take(("l0_w_in",), first[:1])
    w_in0 = _regroup_w_in0(w["l0_w_in"])
    wconv = first[1][:, :4, :conv_cols].transpose(1, 0, 2).reshape(4, NDEV * conv_cols)
    z0 = _mm("l0_in", h0, w_in0, NN, f32)
    qk_rows = [(z0, 2 * FOX_W, 0)]
    qk_consts = [(gq_t, None, 0), (gk_t, None, 0), (pm, None, 0)]
    qn, kn = _tok_fwd("fox_pre", _f_foxpre, qk_rows, qk_consts, [(FOX_W, f32)] * 2, TOK_ROWS)
    ccol, crow = _fox_gate_fwd(z0, fbias, t)
    fox_o, got = _fox_attn_fwd(qn, kn, z0, ccol, crow, t, side=_side_allgather([send(nm) for nm in _L0_REST]))
    take(_L0_REST, got)
    conv_rows, conv_consts = [(z0, LANES, Z0_GQKV)], [(wconv, LANES, 0)]
    (qkv,) = _tok_fwd("gdn_conv", _f_conv, conv_rows, conv_consts, [(LANES, f32)], t, ncb=12, with_j=True)
    gate_consts = [(e_beta, None, 0), (e_alpha, None, 0), (alog_b, None, 0), (dt_b, None, 0)]
    intra_rows = [(qkv, GDN_W, 0), (qkv, GDN_W, 1), (qkv, GDN_W, 2), (z0, LANES, Z0_SMALL)]
    intra, got = _tok_fwd("gdn_intra", _f_gdn_gated, intra_rows, gate_consts, [(GDN_W, f32)] * 7, 2 * CH,
                          side=_side_allgather([send(nm) for nm in _L1_MIX]))
    take(_L1_MIX, got)
    inter_rows = [(a, 0, 1) for a in intra[:6]]
    full_rows = inter_rows + [(fox_o, 0, 1), (z0, Z0_GG // 4, 1)]
    (cat0, gdn_hist), got = _scan_fwd("gdn_scan", _f_gdn_full, full_rows, GDN_H, nchunk, consts=(on0_t,), out_dtype=bf16, out_k=2,
                                      side=_side_allgather([send(nm) for nm in _L1_FFN[1:]]))
    take(_L1_FFN[1:], got)
    x1, hf0, hf0t = _mm("l0_out", cat0, w["l0_w_out"], NN, f32, epi=_epi_res_norm, extra=(x,), consts=(g0f,), outs=_RES_NORM_OUTS)
    (x2, h1, h1t), a0 = _mlp_fwd("l0", x1, hf0, w["l0_w_ff1"], w["l0_w_ff2"], _epi_res_norm, (), (g1m,), _RES_NORM_OUTS)

    z1 = _mm("l1_in", h1, w["l1_w_in"], NN, f32)
    (cat1, hg_hist), got = _scan_fwd("hgrn_scan", _f_hgrn_full, [(z1, 0, 4)], HG_H, nchunk, consts=(lbl, on1_t), out_dtype=bf16,
                                     side=_side_allgather([send(nm) for nm in _L1_FFN[:1]]))
    take(_L1_FFN[:1], got)
    x3, hf1, hf1t = _mm("l1_out", cat1, w["l1_w_out"], NN, f32, epi=_epi_res_norm, extra=(x2,), consts=(g1f,), outs=_RES_NORM_OUTS)
    (dy, dyt, loss_row), a1 = _mlp_fwd("l1", x3, hf1, w["l1_w_ff1"], w["l1_w_ff2"], _epi_loss, (tgt,), (), _LOSS_OUTS)

    (dx3, g["l1_ffn_norm"], dx3t), ga_ff1, ga_ff2 = _mlp_bwd("l1", x3, g1f, w["l1_w_ff1"], w["l1_w_ff2"], hf1t, a1, dy, dyt)
    dcat1 = _mm("l1_out_dx", dx3, w["l1_w_out"], NT, f32)
    ga_out = _mm("l1_out_dw", dx3t, cat1, NN, bf16, out_t=True)
    dz1, dlbl, don1 = _scan_bwd("hgrn_scan_bwd", _f_hgrn_full, [(z1, 0, 4)], hg_hist, dcat1, HG_H, nchunk, dtypes=[bf16],
                                consts=(lbl, on1_t))
    dx2, g["l1_mix_norm"], dx2t = _mm("l1_in_dx", dz1, w["l1_w_in"], NT, f32, epi=_epi_norm_bwd, extra=(x2, dx3), consts=(g1m,),
                                      outs=_NORM_BWD_OUTS)
    ga_in = _mm("l1_in_dw", h1t, dz1, NN, bf16, slab=True)
    g["l1_hgrn_o_norm"] = don1.reshape(HG_H, HD).sum(0)
    g["hgrn_lb_logits"] = dlbl
    gs_a = [by_dev(nm, ga) for nm, ga in zip(_GRAD_A, (ga_ff1, ga_ff2, ga_out, ga_in))]

    (dx1, g["l0_ffn_norm"], dx1t), gb_ff1, gb_ff2 = _mlp_bwd("l0", x1, g0f, w["l0_w_ff1"], w["l0_w_ff2"], hf0t, a0, dx2, dx2t)
    dcat0 = _mm("l0_out_dx", dx1, w["l0_w_out"], NT, f32)
    gb_out = _mm("l0_out_dw", dx1t, cat0, NN, bf16, out_t=True)
    gs_b = [by_dev(nm, ga) for nm, ga in zip(_GRAD_B, (gb_ff1, gb_ff2, gb_out))]
    (*dinter, dfox_o, dgg, don0), r1_a = _scan_bwd("gdn_scan_bwd", _f_gdn_full, full_rows, gdn_hist, dcat0, GDN_H, nchunk,
                                                   consts=(on0_t,), out_k=2, dtypes=[f32] * 7 + [bf16], cps=1,
                                                   side=_side_rs_sibling(gs_a))
    pairs_a = pair(_GRAD_A, gs_a, r1_a)
    (dqn, dkn, dfv, dcq, dck), got = _fox_attn_bwd(qn, kn, z0, ccol, crow, dfox_o, t,
                                                   side=_join_sides([_side_rs_chips(pairs_a), _side_rs_sibling(gs_b)]))
    r2_a, r1_b = got[:len(_GRAD_A)], got[len(_GRAD_A):]
    adam(_GRAD_A, gs_a, r1_a, r2_a)
    pairs_b = pair(_GRAD_B, gs_b, r1_b)
    (dqkv, dzs_g, dalog_b, ddt_b), r2_b = _tok_bwd(
        "gdn_intra_bwd", _f_gdn_gated_given, intra_rows + [(intra[6], GDN_W, 0)], gate_consts, [(a, GDN_W) for a in dinter],
        2 * CH, [0, 1, 2, 3], [2, 3], ncat=3, side=_side_rs_chips(pairs_b))
    adam(_GRAD_B, gs_b, r1_b, r2_b)
    dgqkv, dwconv = _tok_bwd("gdn_conv_bwd", _f_conv, conv_rows, conv_consts, [(dqkv, LANES)], t, [0], [0], ncb=12, with_j=True,
                             drow_dtype=bf16)
    dzs_f, dfb = _fox_gate_bwd(z0, fbias, dcq, dck, t)
    dzqk, dgq_t, dgk_t = _tok_bwd("fox_pre_bwd", _f_foxpre, qk_rows, qk_consts, [(dqn, FOX_W), (dkn, FOX_W)], TOK_ROWS, [0], [0, 1],
                                  drow_dtype=bf16)
    dz0 = jnp.concatenate([dzqk, dfv.astype(bf16), dgqkv, dgg, (dzs_g + dzs_f).astype(bf16), jnp.zeros((n, ZW0 - 3712), bf16)], axis=1)
    gs_c = [_w_in0_slabs(_mm("l0_in_dw", h0t, dz0, NN, bf16), ncols["l0_w_in"])]
    r1_c = _rs_sibling(gs_c)
    pairs_c = pair(("l0_w_in",), gs_c, r1_c)
    (dx, g["l0_mix_norm"], _), r2_c = _mm("l0_in_dx", dz0, w_in0, NT, f32, epi=_epi_norm_bwd, extra=(x, dx1), consts=(g0m,),
                                          outs=_NORM_BWD_OUTS, side=_side_rs_chips(pairs_c))
    adam(("l0_w_in",), gs_c, r1_c, r2_c)
    g["l0_fox_q_norm"] = dgq_t.reshape(FOX_H, FOX_D).sum(0)
    g["l0_fox_k_norm"] = dgk_t.reshape(FOX_H, FOX_D).sum(0)
    g["l0_fox_f_bias"] = dfb[0, :FOX_H]
    g["l0_gdn_conv"] = dwconv
    g["l0_gdn_A_log"] = dalog_b.reshape(GDN_H, HD).sum(1)
    g["l0_gdn_dt_bias"] = ddt_b.reshape(GDN_H, HD).sum(1)
    g["l0_gdn_o_norm"] = don0.reshape(GDN_H, HD).sum(0)
    return loss_row, dx, g, big


_NAMES = ("l0_mix_norm", "l0_w_in", "l0_fox_q_norm", "l0_fox_k_norm", "l0_fox_f_bias", "l0_gdn_conv", "l0_gdn_A_log",
          "l0_gdn_dt_bias", "l0_gdn_o_norm", "l0_w_out", "l0_ffn_norm", "l0_w_ff1", "l0_w_ff2", "l1_mix_norm", "l1_w_in",
          "l1_hgrn_o_norm", "l1_w_out", "l1_ffn_norm", "l1_w_ff1", "l1_w_ff2", "hgrn_lb_logits")
_SMALL_NAMES = tuple(nm for nm in _NAMES if nm not in _BIG_NAMES)
_SMALL_ROWS = 16


def _pack_small(vals):
    flat = jnp.concatenate([vals[nm].reshape(-1).astype(f32) for nm in _SMALL_NAMES])
    return jnp.pad(flat, (0, _SMALL_ROWS * D - flat.shape[0])).reshape(_SMALL_ROWS, D)


def _unpack_small(packed, shapes):
    flat = packed.reshape(-1)
    out, off = {}, 0
    for nm in _SMALL_NAMES:
        size = 1
        for s in shapes[nm]:
            size *= s
        out[nm] = flat[off:off + size].reshape(shapes[nm])
        off += size
    return out, off


def kernel(x, l0_mix_norm, l0_w_in, l0_fox_q_norm, l0_fox_k_norm, l0_fox_f_bias, l0_gdn_conv, l0_gdn_A_log, l0_gdn_dt_bias, l0_gdn_o_norm, l0_w_out, l0_ffn_norm, l0_w_ff1, l0_w_ff2, l1_mix_norm, l1_w_in, l1_hgrn_o_norm, l1_w_out, l1_ffn_norm, l1_w_ff1, l1_w_ff2, hgrn_lb_logits, loss_target, m_l0_mix_norm, m_l0_w_in, m_l0_fox_q_norm, m_l0_fox_k_norm, m_l0_fox_f_bias, m_l0_gdn_conv, m_l0_gdn_A_log, m_l0_gdn_dt_bias, m_l0_gdn_o_norm, m_l0_w_out, m_l0_ffn_norm, m_l0_w_ff1, m_l0_w_ff2, m_l1_mix_norm, m_l1_w_in, m_l1_hgrn_o_norm, m_l1_w_out, m_l1_ffn_norm, m_l1_w_ff1, m_l1_w_ff2, m_hgrn_lb_logits, v_l0_mix_norm, v_l0_w_in, v_l0_fox_q_norm, v_l0_fox_k_norm, v_l0_fox_f_bias, v_l0_gdn_conv, v_l0_gdn_A_log, v_l0_gdn_dt_bias, v_l0_gdn_o_norm, v_l0_w_out, v_l0_ffn_norm, v_l0_w_ff1, v_l0_w_ff2, v_l1_mix_norm, v_l1_w_in, v_l1_hgrn_o_norm, v_l1_w_out, v_l1_ffn_norm, v_l1_w_ff1, v_l1_w_ff2, v_hgrn_lb_logits):
    args = dict(zip(_NAMES, (l0_mix_norm, l0_w_in, l0_fox_q_norm, l0_fox_k_norm, l0_fox_f_bias, l0_gdn_conv, l0_gdn_A_log, l0_gdn_dt_bias, l0_gdn_o_norm, l0_w_out, l0_ffn_norm, l0_w_ff1, l0_w_ff2, l1_mix_norm, l1_w_in, l1_hgrn_o_norm, l1_w_out, l1_ffn_norm, l1_w_ff1, l1_w_ff2, hgrn_lb_logits)))
    mom = dict(zip(_NAMES, (m_l0_mix_norm, m_l0_w_in, m_l0_fox_q_norm, m_l0_fox_k_norm, m_l0_fox_f_bias, m_l0_gdn_conv, m_l0_gdn_A_log, m_l0_gdn_dt_bias, m_l0_gdn_o_norm, m_l0_w_out, m_l0_ffn_norm, m_l0_w_ff1, m_l0_w_ff2, m_l1_mix_norm, m_l1_w_in, m_l1_hgrn_o_norm, m_l1_w_out, m_l1_ffn_norm, m_l1_w_ff1, m_l1_w_ff2, m_hgrn_lb_logits)))
    var = dict(zip(_NAMES, (v_l0_mix_norm, v_l0_w_in, v_l0_fox_q_norm, v_l0_fox_k_norm, v_l0_fox_f_bias, v_l0_gdn_conv, v_l0_gdn_A_log, v_l0_gdn_dt_bias, v_l0_gdn_o_norm, v_l0_w_out, v_l0_ffn_norm, v_l0_w_ff1, v_l0_w_ff2, v_l1_mix_norm, v_l1_w_in, v_l1_hgrn_o_norm, v_l1_w_out, v_l1_ffn_norm, v_l1_w_ff1, v_l1_w_ff2, v_hgrn_lb_logits)))
    nb, t, _ = x.shape
    dev = 4 * lax.axis_index("x") + 2 * lax.axis_index("y") + lax.axis_index("c")
    conv_cols = l0_gdn_conv.shape[1]
    loss_row, dx, g, big = _train(x.reshape(nb * t, D), loss_target.reshape(nb * t, D), args, mom, var, t)

    shapes = {nm: args[nm].shape for nm in _SMALL_NAMES}
    gsm = dict(g)
    gsm["l0_gdn_conv"] = jnp.zeros(shapes["l0_gdn_conv"], f32)
    packed = _pack_small(gsm)
    _, used = _unpack_small(packed, shapes)
    flat_extra = jnp.concatenate([jnp.sum(loss_row).reshape(1), g["l0_gdn_conv"].reshape(-1)])
    packed = packed.reshape(-1).at[used:used + flat_extra.shape[0]].set(flat_extra).reshape(_SMALL_ROWS, D)
    (parts,) = _allgather("ag_small", [packed])
    total = _sum_parts(parts).reshape(-1)
    loss = 0.5 * total[used] / D
    conv_g_full = total[used + 1:used + 1 + 4 * NDEV * conv_cols].reshape(4, NDEV * conv_cols)
    conv_g = lax.dynamic_slice(conv_g_full, (0, dev * conv_cols), (4, conv_cols))
    own_vals = {nm: jnp.zeros(shapes[nm], f32) for nm in _SMALL_NAMES}
    own_vals["l0_gdn_conv"] = conv_g
    own_mask = {nm: jnp.zeros(shapes[nm], f32) for nm in _SMALL_NAMES}
    own_mask["l0_gdn_conv"] = jnp.ones(shapes["l0_gdn_conv"], f32)
    small = _adamw_small(parts, _pack_small(args), _pack_small(mom), _pack_small(var), _pack_small(own_mask), _pack_small(own_vals))
    small = [_unpack_small(a, shapes)[0] for a in small]
    small[0]["l0_gdn_conv"] = conv_g

    outs = [loss, dx.reshape(nb, t, D)]
    for k in range(4):
        outs += [big[nm][k] if nm in _BIG_NAMES else small[k][nm] for nm in _NAMES]
    return tuple(outs)
```

```python
import functools

import jax
import jax.numpy as jnp
from jax import lax
from jax.experimental import pallas as pl
from jax.experimental.pallas import tpu as pltpu

f32, bf16 = jnp.float32, jnp.bfloat16
NN = (((1,), (0,)), ((), ()))
NT = (((1,), (1,)), ((), ()))
TN = (((0,), (0,)), ((), ()))
MESH = pl.DeviceIdType.MESH
S = jax.ShapeDtypeStruct

EPS = 1e-6
D = 1024
LANES = 128
FOX_H, FOX_D, FOX_W = 8, 64, 512
GDN_H, HD, GDN_W = 4, 128, 512
HG_H = 8
CH = 64
TOK_ROWS = 256
ZW0 = 3840
NDEV = 8
ADAM_LR, ADAM_B1, ADAM_B2, ADAM_EPS, ADAM_WD, ADAM_STEP = 0.001, 0.9, 0.999, 1e-08, 0.01, 10

Z0_FV, Z0_GQKV, Z0_GG, Z0_SMALL = 8, 12, 24, 28


def _dot(a, b, dims=NN):
    return lax.dot_general(a, b, dims, preferred_element_type=f32)


def _iota2(shape, axis):
    return lax.broadcasted_iota(jnp.int32, shape, axis)


def _split3(x):
    x1 = x.astype(bf16)
    r = x - x1.astype(f32)
    x2 = r.astype(bf16)
    return x1, x2, (r - x2.astype(f32)).astype(bf16)


def _dot_sel(a, b, dims=NN, exact_lhs=False):
    if exact_lhs:
        return sum(_dot(a.astype(bf16), piece, dims) for piece in _split3(b))
    return sum(_dot(piece, b.astype(bf16), dims) for piece in _split3(a))


@jax.custom_vjp
def _sel_rhs(a, b):
    return _dot_sel(a, b)


_sel_rhs.defvjp(lambda a, b: (_dot_sel(a, b), b), lambda b, g: (_dot_sel(g, b, NT), jnp.zeros_like(b)))


@jax.custom_vjp
def _sel_lhs(a, x):
    return _dot_sel(a, x, exact_lhs=True)


_sel_lhs.defvjp(lambda a, x: (_dot_sel(a, x, exact_lhs=True), a), lambda a, g: (jnp.zeros_like(a), _dot_sel(a, g, TN, exact_lhs=True)))


class _Side:
    def __init__(self, ins, out_shapes, scratch, start, finish):
        self.ins, self.out_shapes, self.scratch, self.start, self.finish = list(ins), list(out_shapes), list(scratch), start, finish


def _join_sides(sides):
    def split(refs, counts):
        out, off = [], 0
        for c in counts:
            out.append(refs[off:off + c])
            off += c
        return out

    ni, no, ns = ([len(getattr(sd, a)) for sd in sides] for a in ("ins", "out_shapes", "scratch"))

    def run(which):
        def go(ins, outs, sems):
            for sd, i, o, c in zip(sides, split(ins, ni), split(outs, no), split(sems, ns)):
                getattr(sd, which)(i, o, c)
        return go

    return _Side(sum((sd.ins for sd in sides), []), sum((sd.out_shapes for sd in sides), []),
                 sum((sd.scratch for sd in sides), []), run("start"), run("finish"))


def _pcall(side, body, *, name, grid, in_specs, out_specs, out_shape, scratch_shapes=(), compiler_params=None):
    if side is None:
        return pl.pallas_call(body, name=name, grid=grid, in_specs=in_specs, out_specs=out_specs, out_shape=out_shape,
                              scratch_shapes=scratch_shapes, compiler_params=compiler_params)
    single = not isinstance(out_shape, (list, tuple))
    ospecs, oshape = ([out_specs], [out_shape]) if single else (list(out_specs), list(out_shape))
    nin, nout, nscr = len(in_specs), len(ospecs), len(scratch_shapes)
    si, so = len(side.ins), len(side.out_shapes)

    def wrapped(*refs):
        o0 = nin + si
        c0 = o0 + nout + so
        sins, souts, ssems = refs[nin:o0], refs[o0 + nout:c0], refs[c0 + nscr:]
        ids = [pl.program_id(a) for a in range(len(grid))]
        first = functools.reduce(jnp.logical_and, [i == 0 for i in ids])
        last = functools.reduce(jnp.logical_and, [i == g - 1 for i, g in zip(ids, grid)])

        @pl.when(first)
        def _():
            side.start(sins, souts, ssems)

        body(*refs[:nin], *refs[o0:o0 + nout], *refs[c0:c0 + nscr])

        @pl.when(last)
        def _():
            side.finish(sins, souts, ssems)

    call = pl.pallas_call(
        wrapped, name=name, grid=grid, in_specs=list(in_specs) + _hbm_specs(si), out_specs=ospecs + _hbm_specs(so),
        out_shape=oshape + side.out_shapes, scratch_shapes=list(scratch_shapes) + side.scratch,
        compiler_params=pltpu.CompilerParams(dimension_semantics=("arbitrary",) * len(grid),
                                             vmem_limit_bytes=getattr(compiler_params, "vmem_limit_bytes", None)))

    def run(*args):
        res = call(*args, *side.ins)
        return (res[0] if single else list(res[:nout])), list(res[nout:])

    return run


def _tok_specs(rows, consts, tm):
    specs = []
    for (_, w, base) in rows:
        specs.append(pl.BlockSpec((tm, w), functools.partial(lambda j, i, b: (i, b + j), b=base)))
    for (arr, w, base) in consts:
        if w is None:
            specs.append(pl.BlockSpec(arr.shape, lambda j, i: (0, 0)))
        else:
            specs.append(pl.BlockSpec((arr.shape[0], w), functools.partial(lambda j, i, b: (0, b + j), b=base)))
    return specs


def _tok_fwd(name, f, rows, consts, outs, tm, ncb=1, with_j=False, also_t=(), side=None):
    n = rows[0][0].shape[0]
    nin = len(rows) + len(consts)
    nout = len(outs)

    def body(*refs):
        ins = [r[...] for r in refs[:nin]]
        vals = f(pl.program_id(0), *ins) if with_j else f(*ins)
        for r, v in zip(refs[nin:nin + nout], vals):
            r[...] = v.astype(r.dtype)
        for r, k in zip(refs[nin + nout:], also_t):
            r[...] = vals[k].T.astype(r.dtype)

    return _pcall(
        side, body, name=name, grid=(ncb, n // tm),
        in_specs=_tok_specs(rows, consts, tm),
        out_specs=[pl.BlockSpec((tm, w), lambda j, i: (i, j)) for (w, _) in outs]
        + [pl.BlockSpec((outs[k][0], tm), lambda j, i: (0, i)) for k in also_t],
        out_shape=[S((n, w * ncb), dt) for (w, dt) in outs] + [S((outs[k][0], n), bf16) for k in also_t],
        compiler_params=pltpu.CompilerParams(dimension_semantics=("parallel", "parallel")),
    )(*[r[0] for r in rows], *[c[0] for c in consts])


def _tok_bwd(name, f, rows, consts, cots, tm, drow, dconst, ncb=1, with_j=False, addto=None, also_t=(), drow_dtype=f32, side=None, ncat=0):
    n = rows[0][0].shape[0]
    nr, nc, nct = len(rows), len(consts), len(cots)
    addto = addto or {}
    add_keys = sorted(addto)
    nadd = len(add_keys)

    def body(*refs):
        ins = [r[...] for r in refs[:nr + nc]]
        cot = [r[...] for r in refs[nr + nc:nr + nc + nct]]
        adds = refs[nr + nc + nct:nr + nc + nct + nadd]
        outs = refs[nr + nc + nct + nadd:]
        pos = list(drow) + [nr + k for k in dconst]

        def g(*dargs):
            full = list(ins)
            for p, a in zip(pos, dargs):
                full[p] = a
            return tuple(f(pl.program_id(0), *full) if with_j else f(*full))

        vals, vjp = jax.vjp(g, *[ins[p] for p in pos])
        grads = vjp(tuple(c.astype(v.dtype) for c, v in zip(cot, vals)))
        off = 0
        for k in range(len(drow)):
            gk = grads[k]
            if k in addto:
                gk = gk + adds[add_keys.index(k)][...]
            if k < ncat:
                outs[0][:, off:off + gk.shape[1]] = gk.astype(outs[0].dtype)
                off += gk.shape[1]
            else:
                outs[k - skip][...] = gk.astype(outs[k - skip].dtype)
            if k in also_t:
                tref = outs[len(drow) - skip + len(dconst) + list(also_t).index(k)]
                tref[...] = gk.T.astype(tref.dtype)
        first = pl.program_id(1) == 0
        for k in range(len(dconst)):
            ref = outs[len(drow) - skip + k]

            @pl.when(first)
            def _():
                ref[...] = jnp.zeros_like(ref)

            ref[...] += grads[len(drow) + k]

    skip = max(ncat - 1, 0)
    in_specs = _tok_specs(rows, consts, tm)
    in_specs += [pl.BlockSpec((tm, w), lambda j, i: (i, j)) for (_, w) in cots]
    in_specs += [pl.BlockSpec((tm, rows[drow[k]][1]), lambda j, i: (i, j)) for k in add_keys]
    dts = drow_dtype if isinstance(drow_dtype, (list, tuple)) else [drow_dtype] * len(drow)
    widths = [rows[k][1] for k in drow]
    if ncat:
        widths, dts = [sum(widths[:ncat])] + widths[ncat:], [dts[0]] + list(dts[ncat:])
    out_specs = [pl.BlockSpec((tm, wd), lambda j, i: (i, j)) for wd in widths]
    out_shape = [S((n, wd * ncb), dt) for wd, dt in zip(widths, dts)]
    for k in dconst:
        arr, w, _ = consts[k]
        if w is None:
            out_specs.append(pl.BlockSpec(arr.shape, lambda j, i: (0, 0)))
            out_shape.append(S(arr.shape, f32))
        else:
            out_specs.append(pl.BlockSpec((arr.shape[0], w), lambda j, i: (0, j)))
            out_shape.append(S((arr.shape[0], w * ncb), f32))
    for k in also_t:
        out_specs.append(pl.BlockSpec((rows[drow[k]][1], tm), lambda j, i: (0, i)))
        out_shape.append(S((rows[drow[k]][1], n), bf16))
    return _pcall(
        side, body, name=name, grid=(ncb, n // tm), in_specs=in_specs, out_specs=out_specs, out_shape=out_shape,
        compiler_params=pltpu.CompilerParams(dimension_semantics=("parallel", "arbitrary")),
    )(*[r[0] for r in rows], *[c[0] for c in consts], *[c[0] for c in cots], *[addto[k] for k in add_keys])


SCAN_CHUNKS = 2


def _scan_fwd(name, f, rows, nh, nchunk, side=None, consts=(), out_dtype=f32, out_k=1):
    n = rows[0][0].shape[0]
    nb = n // (CH * nchunk)
    nin, nco = len(rows), len(consts)
    w = nh * HD
    cps = SCAN_CHUNKS

    def body(*refs):
        o_ref, hist_ref, st = refs[nin + nco], refs[nin + nco + 1], refs[nin + nco + 2]

        @pl.when(pl.program_id(0) == 0)
        def _():
            st[...] = jnp.zeros_like(st)

        cvals = [r[...] for r in refs[nin:nin + nco]]
        state = st[...]
        for sub in range(cps):
            rr = pl.ds(sub * CH, CH)
            hist_ref[sub] = state.astype(hist_ref.dtype)
            tiles = [r[:, rr, :].reshape(nb * CH, r.shape[2]) for r in refs[:nin]]
            o, state = f(*tiles, *cvals, state)
            o_ref[:, rr, :] = o.reshape(nb, CH, out_k * w).astype(o_ref.dtype)
        st[...] = state

    seq3 = lambda a: a.reshape(nb, nchunk * CH, a.shape[1])
    res = _pcall(
        side, body, name=name, grid=(nchunk // cps,),
        in_specs=[pl.BlockSpec((nb, cps * CH, k * w), functools.partial(lambda c, base: (0, c, base), base=b)) for (_, b, k) in rows]
        + [pl.BlockSpec(c.shape, lambda c: (0, 0)) for c in consts],
        out_specs=[pl.BlockSpec((nb, cps * CH, out_k * w), lambda c: (0, c, 0)), pl.BlockSpec((cps, nb * w, HD), lambda c: (c, 0, 0))],
        out_shape=[S((nb, nchunk * CH, out_k * w), out_dtype), S((nchunk, nb * w, HD), bf16)],
        scratch_shapes=[pltpu.VMEM((nb * w, HD), f32)],
        compiler_params=pltpu.CompilerParams(dimension_semantics=("arbitrary",)),
    )(*[seq3(r[0]) for r in rows], *consts)
    (o, hist), extra = (res, None) if side is None else res
    out = [o.reshape(n, out_k * w), hist]
    return out if side is None else (out, extra)


def _scan_bwd(name, f, rows, hist, do, nh, nchunk, side=None, dtypes=None, consts=(), out_k=1, cps=SCAN_CHUNKS):
    n = rows[0][0].shape[0]
    nb = n // (CH * nchunk)
    nin, nco = len(rows), len(consts)
    w = nh * HD
    nstep = nchunk // cps

    def body(*refs):
        hist_ref, do_ref = refs[nin + nco], refs[nin + nco + 1]
        outs = refs[nin + nco + 2:nin + nco + 2 + nin]
        couts = refs[nin + nco + 2 + nin:nin + nco + 2 + nin + nco]
        ds = refs[nin + nco + 2 + nin + nco]

        @pl.when(pl.program_id(0) == 0)
        def _():
            ds[...] = jnp.zeros_like(ds)
            for c in couts:
                c[...] = jnp.zeros_like(c)

        cvals = [r[...] for r in refs[nin:nin + nco]]
        dstate = ds[...]
        for sub in reversed(range(cps)):
            rr = pl.ds(sub * CH, CH)
            tiles = [r[:, rr, :].reshape(nb * CH, r.shape[2]) for r in refs[:nin]]
            _, vjp = jax.vjp(f, *tiles, *cvals, hist_ref[sub].astype(f32))
            grads = vjp((do_ref[:, rr, :].reshape(nb * CH, out_k * w), dstate))
            for r, gk in zip(outs, grads[:nin]):
                r[:, rr, :] = gk.reshape(nb, CH, r.shape[2]).astype(r.dtype)
            for c, gk in zip(couts, grads[nin:nin + nco]):
                c[...] += gk
            dstate = grads[nin + nco]
        ds[...] = dstate

    seq3 = lambda a: a.reshape(nb, nchunk * CH, a.shape[1])
    rev = lambda c, base: (0, nstep - 1 - c, base)
    res = _pcall(
        side, body, name=name, grid=(nstep,),
        in_specs=[pl.BlockSpec((nb, cps * CH, k * w), functools.partial(rev, base=b)) for (_, b, k) in rows]
        + [pl.BlockSpec(c.shape, lambda c: (0, 0)) for c in consts]
        + [pl.BlockSpec((cps, nb * w, HD), lambda c: (nstep - 1 - c, 0, 0)),
           pl.BlockSpec((nb, cps * CH, out_k * w), functools.partial(rev, base=0))],
        out_specs=[pl.BlockSpec((nb, cps * CH, k * w), functools.partial(rev, base=0)) for (_, _, k) in rows]
        + [pl.BlockSpec(c.shape, lambda c: (0, 0)) for c in consts],
        out_shape=[S((nb, nchunk * CH, k * w), dt) for (_, _, k), dt in zip(rows, dtypes or [f32] * nin)]
        + [S(c.shape, f32) for c in consts],
        scratch_shapes=[pltpu.VMEM((nb * w, HD), f32)],
        compiler_params=pltpu.CompilerParams(dimension_semantics=("arbitrary",)),
    )(*[seq3(r[0]) for r in rows], *consts, hist, seq3(do))
    outs, extra = (res, None) if side is None else res
    outs = [o.reshape(n, o.shape[2]) for o in outs[:nin]] + list(outs[nin:])
    return outs if side is None else (outs, extra)


_VMEM_LIMIT = 56 * 2 ** 20
_VMEM_TILE_BUDGET = 40 * 2 ** 20


def _mm_tiles(m, n, k, sa, sb, so, sx, a_f32, b_f32, tn_fixed):
    best = None
    for tm in (1024, 512, 256, 128, 64):
        for tn in ((tn_fixed,) if tn_fixed else (1024, 768, 512, 384, 256, 128)):
            if m % tm or n % tn:
                continue
            need = 2 * (tm * k * sa + k * tn * sb + tm * tn * (so + sx)) + tm * tn * 4
            need += tm * k * (2 if sa == 4 else 0) + k * tn * (2 if sb == 4 else 0)
            need += tm * k * (4 if a_f32 else 0) + k * tn * (4 if b_f32 else 0)
            if need <= _VMEM_TILE_BUDGET and (best is None or (tm * tn, tm) > best[0]):
                best = ((tm * tn, tm), tm, tn)
    return best[1], best[2]


def _mm(name, a, b, dims, out_dtype, a_fn=None, b_fn=None, epi=None, extra=(), consts=(), outs=None, out_t=False, slab=False,
        side=None):
    m, kk = a.shape
    gathered = b.ndim == 3
    if gathered:
        nn = NDEV * _COLW if dims is NN else b.shape[1]
    else:
        nn = b.shape[1] if dims is NN else b.shape[0]
    kinds = [("tile", out_dtype)] if outs is None else list(outs)
    so = sum(jnp.dtype(dt).itemsize for kd, dt in kinds if kd != "rows")
    sx = sum(e.dtype.itemsize for e in extra)
    full_rows = bool(consts) or any(kd == "rows" for kd, _ in kinds)
    tm, tn = _mm_tiles(m, nn, kk, a.dtype.itemsize, b.dtype.itemsize, so, sx,
                       a_fn is not None, b_fn is not None,
                       _COLW if slab else (nn if full_rows else None))
    gblocks = tn // _COLW if (gathered and dims is NN) else 0
    assert not gblocks or (outs is None and not extra and not out_t and not slab and b_fn is None)
    nex, nco = len(extra), len(consts)

    def body(a_ref, b_ref, *rest):
        av = a_ref[...]
        if a_fn is not None:
            av = a_fn(av.astype(f32))
        av = av.astype(bf16)
        if gblocks:
            for gb in range(gblocks):
                part = _dot(av, b_ref[gb], NN)
                rest[nex + nco][:, _COLW * gb:_COLW * (gb + 1)] = (part if epi is None else epi(part)).astype(rest[nex + nco].dtype)
            return
        if gathered:
            acc = sum(_dot(av[:, _COLW * d:_COLW * (d + 1)], b_ref[d], NT) for d in range(NDEV))
        else:
            bv = b_ref[...]
            if b_fn is not None:
                bv = b_fn(bv.astype(f32))
            acc = _dot(av, bv.astype(bf16), dims)
        if epi is not None:
            acc = epi(acc, *[r[...] for r in rest[:nex + nco]])
        vals = acc if isinstance(acc, tuple) else (acc,)
        for (kd, _), o_ref, val in zip(kinds, rest[nex + nco:], vals):
            if kd == "rows":
                @pl.when(pl.program_id(0) == 0)
                def _():
                    o_ref[...] = jnp.zeros_like(o_ref)

                o_ref[...] += val
            elif kd == "tile_t" or out_t:
                o_ref[...] = val.T.astype(o_ref.dtype)
            elif slab:
                o_ref[0] = val.astype(o_ref.dtype)
            else:
                o_ref[...] = val.astype(o_ref.dtype)

    if gathered:
        bspec = (pl.BlockSpec((gblocks, kk, _COLW), lambda i, j: (j, 0, 0)) if dims is NN
                 else pl.BlockSpec((NDEV, tn, _COLW), lambda i, j: (0, j, 0)))
    else:
        bspec = pl.BlockSpec((kk, tn), lambda i, j: (0, j)) if dims is NN else pl.BlockSpec((tn, kk), lambda i, j: (j, 0))
    out_specs, out_shape = [], []
    for kd, dt in kinds:
        if kd == "rows":
            out_specs.append(pl.BlockSpec((1, nn), lambda i, j: (0, 0)))
            out_shape.append(S((1, nn), dt))
        elif kd == "tile_t" or out_t:
            out_specs.append(pl.BlockSpec((tn, tm), lambda i, j: (j, i)))
            out_shape.append(S((nn, m), dt))
        elif slab:
            out_specs.append(pl.BlockSpec((1, tm, tn), lambda i, j: (j, i, 0)))
            out_shape.append(S((nn // tn, m, tn), dt))
        else:
            out_specs.append(pl.BlockSpec((tm, tn), lambda i, j: (i, j)))
            out_shape.append(S((m, nn), dt))
    if outs is None:
        out_specs, out_shape = out_specs[0], out_shape[0]
    sem = ("arbitrary", "arbitrary") if any(kd == "rows" for kd, _ in kinds) else ("parallel", "parallel")
    return _pcall(
        side, body, name=name, grid=(m // tm, nn // tn),
        in_specs=[pl.BlockSpec((tm, kk), lambda i, j: (i, 0)), bspec]
        + [pl.BlockSpec((tm, tn), lambda i, j: (i, j)) for _ in extra]
        + [pl.BlockSpec(c.shape, lambda i, j: (0, 0)) for c in consts],
        out_specs=out_specs, out_shape=out_shape,
        compiler_params=pltpu.CompilerParams(dimension_semantics=sem, vmem_limit_bytes=_VMEM_LIMIT),
    )(a, b, *extra, *consts)


def _f_norm(x, g):
    return (x * lax.rsqrt(jnp.mean(x * x, axis=-1, keepdims=True) + EPS) * g,)


def _f_foxpre(zqk, gq, gk, pm):
    def nrm(t, g):
        return t * lax.rsqrt(_sel_rhs(t * t, pm) + EPS) * g
    return nrm(zqk[:, :FOX_W], gq), nrm(zqk[:, FOX_W:], gk)


def _chunk_cumsum(x):
    n = x.shape[0]
    r, c = _iota2((n, n), 0), _iota2((n, n), 1)
    tri = jnp.logical_and(r >= c, (r // CH) == (c // CH)).astype(f32)
    return _sel_lhs(tri, x)


def _f_gdngate(zs, eb, ea, alog_b, dt_b):
    beta = jax.nn.sigmoid(_sel_rhs(zs, eb))
    la = -jnp.exp(alog_b) * jax.nn.softplus(_sel_rhs(zs, ea) + dt_b)
    return beta, _chunk_cumsum(la)


def _f_conv(j, x, w):
    t = x.shape[0]
    y = x * w[3:4, :]
    for jj in range(3):
        sh = 3 - jj
        xs = jnp.concatenate([jnp.zeros((sh, x.shape[1]), f32), x[:t - sh, :]], axis=0)
        y = y + xs * w[jj:jj + 1, :]
    y = jax.nn.silu(y)
    yn = y * lax.rsqrt(jnp.sum(y * y, axis=-1, keepdims=True) + EPS)
    return (jnp.where(j < 2 * GDN_H, yn, y),)


def _head_rms(o, nh):
    outs = []
    for h in range(nh):
        oh = o[:, HD * h:HD * (h + 1)]
        outs.append(oh * lax.rsqrt(jnp.mean(oh * oh, axis=-1, keepdims=True) + EPS))
    return jnp.concatenate(outs, axis=1)


def _f_post0(fox_o, o, gg, on):
    return (jnp.concatenate([fox_o, _head_rms(o, GDN_H) * on * jax.nn.silu(gg)], axis=1),)


def _f_post1(o, zg, on):
    return (_head_rms(o, HG_H) * on * jax.nn.silu(zg),)


def _f_hpre(zqf, lbl):
    lb = jax.nn.sigmoid(lbl[1:2, :] - lbl[0:1, :])
    fg = lb + (1.0 - lb) * jax.nn.sigmoid(zqf[:, D:])
    return jax.nn.silu(zqf[:, :D]), 1.0 - fg, _chunk_cumsum(jnp.log(fg))


def _dotb(a, b, dims=NN):
    return _dot(a.astype(bf16), b.astype(bf16), dims)


def _dot3(a, b):
    ah, bh = a.astype(bf16), b.astype(bf16)
    al, bl = (a - ah.astype(f32)).astype(bf16), (b - bh.astype(f32)).astype(bf16)
    return _dot(ah, bh) + (_dot(ah, bl) + _dot(al, bh))


def _split(t, nh):
    return [t[CH * ck:CH * (ck + 1), HD * h:HD * (h + 1)] for ck in range(t.shape[0] // CH) for h in range(nh)]


def _merge(units, nh):
    return jnp.concatenate([jnp.concatenate(units[i:i + nh], axis=1) for i in range(0, len(units), nh)], axis=0)


def _inv_impl(amats):
    n = amats[0].shape[0]
    eye = jnp.where(_iota2((n, n), 0) == _iota2((n, n), 1), 1.0, 0.0).astype(f32)
    xs, ps = [eye - a for a in amats], list(amats)
    for _ in range(max(1, (n - 1).bit_length()) - 1):
        ps = [_dotb(p, p) for p in ps]
        xs = [x + _dotb(x, p) for x, p in zip(xs, ps)]
    for _ in range(3):
        rs = [eye - x - _dot3(a, x) for a, x in zip(amats, xs)]
        xs = [x + _dotb(x, r) for x, r in zip(xs, rs)]
    return tuple(xs)


@jax.custom_vjp
def _inv_unit_lower(amats):
    return _inv_impl(amats)


def _inv_fwd(amats):
    xs = _inv_impl(amats)
    return xs, xs


def _inv_bwd(xs, dxs):
    return (tuple(-_dotb(_dotb(x, dx, TN), x, NT) for x, dx in zip(xs, dxs)),)


_inv_unit_lower.defvjp(_inv_fwd, _inv_bwd)


@jax.custom_vjp
def _inv_given(amats, xs):
    return xs


def _inv_given_fwd(amats, xs):
    return xs, xs


def _inv_given_bwd(xs, dxs):
    return _inv_bwd(xs, dxs)[0], tuple(jnp.zeros_like(x) for x in xs)


_inv_given.defvjp(_inv_given_fwd, _inv_given_bwd)


def _f_gdn_intra(q, k, v, bb, gb, tinv_p=None):
    qs, ks, vs, bs, gs = (_split(t, GDN_H) for t in (q, k, v, bb, gb))
    r, cc = _iota2((CH, CH), 0), _iota2((CH, CH), 1)
    causal, strict = r >= cc, r > cc
    beta, g, gl = [b[:, :1] for b in bs], [x[:, :1] for x in gs], [x[CH - 1:CH, :1] for x in gs]
    decay = [jnp.exp(jnp.where(causal, x[:, :CH] - x[:, :CH].T, -jnp.inf)) for x in gs]
    kb = [ki * bi for ki, bi in zip(ks, beta)]
    amat = [jnp.where(strict, _dotb(kbi, ki, NT) * di, 0.0) for kbi, ki, di in zip(kb, ks, decay)]
    if tinv_p is None:
        tinv = _inv_unit_lower(tuple(amat))
    else:
        tinv = _inv_given(tuple(amat), tuple(x[:, :CH] for x in _split(tinv_p, GDN_H)))
    rhs = [jnp.concatenate([vi * bi, kbi * jnp.exp(gi)], axis=1) for vi, bi, kbi, gi in zip(vs, beta, kb, g)]
    uw = [_dotb(ti, ri) for ti, ri in zip(tinv, rhs)]
    qsc = [qi * (HD ** -0.5) for qi in qs]
    qk = [jnp.where(causal, _dotb(qi, ki, NT) * di, 0.0) for qi, ki, di in zip(qsc, ks, decay)]
    outs = ([x[:, :HD] for x in uw], [x[:, HD:] for x in uw],
            [jnp.concatenate([x, jnp.zeros_like(x)], axis=1) for x in qk],
            [qi * jnp.exp(gi) for qi, gi in zip(qsc, g)],
            [ki * jnp.exp(gli - gi) for ki, gli, gi in zip(ks, gl, g)],
            [jnp.broadcast_to(gli, (CH, HD)) for gli in gl])
    if tinv_p is None:
        outs += ([jnp.concatenate([x, jnp.zeros_like(x)], axis=1) for x in tinv],)
    return tuple(_merge(o, GDN_H) for o in outs)


def _f_gdn_gated(q, k, v, zs, eb, ea, alog_b, dt_b):
    return _f_gdn_intra(q, k, v, *_f_gdngate(zs, eb, ea, alog_b, dt_b))


def _f_gdn_gated_given(q, k, v, zs, tinv_p, eb, ea, alog_b, dt_b):
    return _f_gdn_intra(q, k, v, *_f_gdngate(zs, eb, ea, alog_b, dt_b), tinv_p=tinv_p)


def _f_gdn_inter(u, w, qkp, qd, kd, glb, st):
    us, ws, qks, qds, kds, gls = (_split(t, GDN_H) for t in (u, w, qkp, qd, kd, glb))
    sts = [st[HD * i:HD * (i + 1), :] for i in range(len(us))]
    vn = [ui - _dotb(wi, si) for ui, wi, si in zip(us, ws, sts)]
    o = [_dotb(qi, si) + _dotb(xi[:, :CH], vi) for qi, si, xi, vi in zip(qds, sts, qks, vn)]
    s2 = [si * jnp.exp(gi[:1, :1]) + _dotb(ki, vi, TN) for si, gi, ki, vi in zip(sts, gls, kds, vn)]
    return _merge(o, GDN_H), jnp.concatenate(s2, axis=0)


def _f_gdn_full(u, w, qkp, qd, kd, glb, fox_o, gg, on, st):
    o, s2 = _f_gdn_inter(u, w, qkp, qd, kd, glb, st)
    return _f_post0(fox_o, o, gg, on)[0], s2


def _f_hgrn_chunk(q, k, b, v, st):
    qs, ks, bs, vs = (_split(t, HG_H) for t in (q, k, b, v))
    sts = [st[HD * i:HD * (i + 1), :] for i in range(len(qs))]
    causal = _iota2((CH, CH), 0) >= _iota2((CH, CH), 1)
    bl, bm = [x[CH - 1:CH, :] for x in bs], [x[CH // 2 - 1:CH // 2, :] for x in bs]
    a = [jnp.where(causal, _dotb(qi * jnp.exp(bi - mi), ki * jnp.exp(mi - bi), NT), 0.0)
         for qi, ki, bi, mi in zip(qs, ks, bs, bm)]
    o = [_dotb(qi * jnp.exp(bi), si, NT) + _dotb(ai, vi) for qi, bi, si, ai, vi in zip(qs, bs, sts, a, vs)]
    s2 = [si * jnp.exp(li) + _dotb(vi, ki * jnp.exp(li - bi), TN) for si, li, vi, ki, bi in zip(sts, bl, vs, ks, bs)]
    return _merge(o, HG_H), jnp.concatenate(s2, axis=0)


def _f_hgrn_full(z, lbl, on, st):
    o, s2 = _f_hgrn_chunk(*_f_hpre(z[:, :2 * D], lbl), z[:, 2 * D:3 * D], st)
    return _f_post1(o, z[:, 3 * D:], on)[0], s2


def _fox_gate_fwd(z0, fbias, t, tc=256):
    n = z0.shape[0]
    nt = t // tc

    def body(zs_ref, b_ref, ccol_ref, crow_ref, carry):
        @pl.when(pl.program_id(1) == 0)
        def _():
            carry[...] = jnp.zeros_like(carry)

        ls = jnp.where(_iota2((tc, LANES), 1) < FOX_H, jax.nn.log_sigmoid(zs_ref[...] + b_ref[...]), 0.0)
        tri = (_iota2((tc, tc), 0) >= _iota2((tc, tc), 1)).astype(f32)
        c = _dot_sel(tri, ls, exact_lhs=True) + carry[...]
        carry[...] = c[tc - 1:tc, :]
        ccol_ref[...] = c
        crow_ref[0] = c.T[:FOX_H, :]

    return pl.pallas_call(
        body, name="fox_gate_fwd", grid=(n // t, nt),
        in_specs=[pl.BlockSpec((tc, LANES), lambda b, i: (b * nt + i, Z0_SMALL)), pl.BlockSpec((1, LANES), lambda b, i: (0, 0))],
        out_specs=[pl.BlockSpec((tc, LANES), lambda b, i: (b * nt + i, 0)), pl.BlockSpec((1, FOX_H, tc), lambda b, i: (b, 0, i))],
        out_shape=[S((n, LANES), f32), S((n // t, FOX_H, t), f32)],
        scratch_shapes=[pltpu.VMEM((1, LANES), f32)],
        compiler_params=pltpu.CompilerParams(dimension_semantics=("parallel", "arbitrary")),
    )(z0, fbias)


def _fox_gate_bwd(z0, fbias, dcq, dck, t, tc=256):
    n = z0.shape[0]
    nt = t // tc

    def body(zs_ref, b_ref, dcq_ref, dck_ref, dz_ref, db_ref, carry):
        first = jnp.logical_and(pl.program_id(0) == 0, pl.program_id(1) == 0)

        @pl.when(pl.program_id(1) == 0)
        def _():
            carry[...] = jnp.zeros_like(carry)

        @pl.when(first)
        def _():
            db_ref[...] = jnp.zeros_like(db_ref)

        dc = dcq_ref[0] + dcq_ref[1] + dcq_ref[2] + dcq_ref[3]
        drow = dck_ref[0, 0] + dck_ref[1, 0] + dck_ref[2, 0] + dck_ref[3, 0]
        eye = (_iota2((FOX_H, LANES), 0) == _iota2((FOX_H, LANES), 1)).astype(f32)
        dc = dc + _dot_sel(drow, eye, TN)
        triu = (_iota2((tc, tc), 0) <= _iota2((tc, tc), 1)).astype(f32)
        dls = _dot_sel(triu, dc, exact_lhs=True) + carry[...]
        carry[...] = dls[0:1, :]
        x = zs_ref[...] + b_ref[...]
        dz = jnp.where(_iota2((tc, LANES), 1) < FOX_H, dls * jax.nn.sigmoid(-x), 0.0)
        dz_ref[...] = dz
        db_ref[...] += jnp.sum(dz, axis=0, keepdims=True)

    def rev(b, i):
        return b * nt + (nt - 1 - i)

    return pl.pallas_call(
        body, name="fox_gate_bwd", grid=(n // t, nt),
        in_specs=[pl.BlockSpec((tc, LANES), lambda b, i: (rev(b, i), Z0_SMALL)), pl.BlockSpec((1, LANES), lambda b, i: (0, 0)),
                  pl.BlockSpec((4, tc, LANES), lambda b, i: (0, rev(b, i), 0)),
                  pl.BlockSpec((4, 1, FOX_H, tc), lambda b, i: (0, b, 0, nt - 1 - i))],
        out_specs=[pl.BlockSpec((tc, LANES), lambda b, i: (rev(b, i), 0)), pl.BlockSpec((1, LANES), lambda b, i: (0, 0))],
        out_shape=[S((n, LANES), f32), S((1, LANES), f32)],
        scratch_shapes=[pltpu.VMEM((1, LANES), f32)],
        compiler_params=pltpu.CompilerParams(dimension_semantics=("arbitrary", "arbitrary")),
    )(z0, fbias, dcq, dck)


def _fox_scores(hh, p, i, tq, q, k, ccol, crow):
    kmax = k.shape[0]
    lane = _iota2((1, LANES), 1)
    mh = (lane // FOX_D) == hh
    h = 2 * p + hh
    qh = jnp.where(mh, q, 0.0).astype(bf16)
    s = _dot(qh, k, NT) * (FOX_D ** -0.5)
    cq = jnp.sum(jnp.where(lane == h, ccol, 0.0), axis=1, keepdims=True)
    ck = jnp.sum(jnp.where(_iota2((FOX_H, 1), 0) == h, crow, 0.0), axis=0, keepdims=True)
    causal = _iota2((1, kmax), 1) <= (i * tq + _iota2((tq, 1), 0))
    s = jnp.where(causal, s + cq - ck, -jnp.inf)
    pe = jnp.exp(s - jnp.max(s, axis=1, keepdims=True))
    return mh, qh, pe, jnp.sum(pe, axis=1, keepdims=True)


def _fox_attn_fwd(qn, kn, z0, ccol, crow, t, tq=256, side=None):
    n = qn.shape[0]
    nq = t // tq

    def body(q_ref, k_ref, v_ref, ccol_ref, crow_ref, o_ref):
        p = pl.program_id(1)
        k, v, crow = k_ref[...].astype(bf16), v_ref[...].astype(bf16), crow_ref[0]
        for i in range(nq):
            rows, kmax = pl.ds(i * tq, tq), (i + 1) * tq
            q, cc = q_ref[rows, :], ccol_ref[rows, :]
            acc = jnp.zeros((tq, LANES), f32)
            for hh in range(2):
                mh, _, pe, l = _fox_scores(hh, p, i, tq, q, k[:kmax], cc, crow[:, :kmax])
                acc = jnp.where(mh, _dot(pe.astype(bf16), v[:kmax]) / l, acc)
            o_ref[rows, :] = acc

    seq = lambda b, p: (b, p)
    return _pcall(
        side, body, name="fox_attn_fwd", grid=(n // t, FOX_H // 2),
        in_specs=[pl.BlockSpec((t, LANES), seq), pl.BlockSpec((t, LANES), seq), pl.BlockSpec((t, LANES), lambda b, p: (b, Z0_FV + p)),
                  pl.BlockSpec((t, LANES), lambda b, p: (b, 0)), pl.BlockSpec((1, FOX_H, t), lambda b, p: (b, 0, 0))],
        out_specs=pl.BlockSpec((t, LANES), seq),
        out_shape=S((n, FOX_W), f32),
        compiler_params=pltpu.CompilerParams(dimension_semantics=("parallel", "parallel")),
    )(qn, kn, z0, ccol, crow)


def _fox_attn_bwd(qn, kn, z0, ccol, crow, do, t, tq=256, side=None):
    n = qn.shape[0]
    nq = t // tq
    nb = n // t

    def body(q_ref, k_ref, v_ref, ccol_ref, crow_ref, do_ref, dq_ref, dk_ref, dv_ref, dcq_ref, dck_ref):
        p = pl.program_id(1)
        dk_ref[...] = jnp.zeros_like(dk_ref)
        dv_ref[...] = jnp.zeros_like(dv_ref)
        dck_ref[...] = jnp.zeros_like(dck_ref)
        kf, v, crow = k_ref[...], v_ref[...].astype(bf16), crow_ref[0]
        k = kf.astype(bf16)
        lane = _iota2((1, LANES), 1)
        sub = _iota2((FOX_H, 1), 0)
        scale = FOX_D ** -0.5
        for i in range(nq):
            rows, kmax = pl.ds(i * tq, tq), (i + 1) * tq
            q, cc, dout = q_ref[rows, :], ccol_ref[rows, :], do_ref[rows, :]
            dq = jnp.zeros((tq, LANES), f32)
            dcq = jnp.zeros((tq, LANES), f32)
            for hh in range(2):
                mh, qh, pe, l = _fox_scores(hh, p, i, tq, q, k[:kmax], cc, crow[:, :kmax])
                pr = pe / l
                doh = jnp.where(mh, dout, 0.0).astype(bf16)
                dp = _dot(doh, v[:kmax], NT)
                ds = pr * (dp - jnp.sum(pr * dp, axis=1, keepdims=True))
                dsb = ds.astype(bf16)
                dq = dq + _dot(dsb, jnp.where(mh, kf[:kmax], 0.0).astype(bf16)) * scale
                dk_ref[:kmax, :] += _dot(dsb, qh, TN) * scale
                dv_ref[:kmax, :] += _dot(pr.astype(bf16), doh, TN)
                h = 2 * p + hh
                dcq = dcq + jnp.where(lane == h, jnp.sum(ds, axis=1, keepdims=True), 0.0)
                dck_ref[0, 0, :, :kmax] += jnp.where(sub == h, -jnp.sum(ds, axis=0, keepdims=True), 0.0)
            dq_ref[rows, :] = dq
            dcq_ref[0, rows, :] = dcq

    seq = lambda b, p: (b, p)
    return _pcall(
        side, body, name="fox_attn_bwd", grid=(nb, FOX_H // 2),
        in_specs=[pl.BlockSpec((t, LANES), seq), pl.BlockSpec((t, LANES), seq), pl.BlockSpec((t, LANES), lambda b, p: (b, Z0_FV + p)),
                  pl.BlockSpec((t, LANES), lambda b, p: (b, 0)), pl.BlockSpec((1, FOX_H, t), lambda b, p: (b, 0, 0)),
                  pl.BlockSpec((t, LANES), seq)],
        out_specs=[pl.BlockSpec((t, LANES), seq), pl.BlockSpec((t, LANES), seq), pl.BlockSpec((t, LANES), seq),
                   pl.BlockSpec((1, t, LANES), lambda b, p: (p, b, 0)), pl.BlockSpec((1, 1, FOX_H, t), lambda b, p: (p, b, 0, 0))],
        out_shape=[S((n, FOX_W), f32), S((n, FOX_W), f32), S((n, FOX_W), f32), S((4, n, LANES), f32), S((4, nb, FOX_H, t), f32)],
        compiler_params=pltpu.CompilerParams(dimension_semantics=("parallel", "parallel")),
    )(qn, kn, z0, ccol, crow, do)


def _adamw_math(w, g, m, v):
    m = ADAM_B1 * m + (1.0 - ADAM_B1) * g
    v = ADAM_B2 * v + (1.0 - ADAM_B2) * (g * g)
    m_hat = m / (1.0 - ADAM_B1 ** ADAM_STEP)
    v_hat = v / (1.0 - ADAM_B2 ** ADAM_STEP)
    return -ADAM_LR * (m_hat / (jnp.sqrt(v_hat) + ADAM_EPS) + ADAM_WD * w), m, v


def _adamw_big(name, idx, gmine, recv1, recv2, w, m, v):
    r, wc = w.shape
    c = gmine.shape[2]
    tr = min(r, 256)

    def body(idx_ref, gm_ref, r1_ref, r2_ref, w_ref, m_ref, v_ref, g_ref, d_ref, nm_ref, nv_ref):
        g = gm_ref[0].astype(f32) + r1_ref[0].astype(f32)
        for k in range(3):
            g = g + r2_ref[k].astype(f32)
        g = g[:, :wc]
        d, nm, nv = _adamw_math(w_ref[...], g, m_ref[...], v_ref[...])
        g_ref[...] = g
        d_ref[...] = d
        nm_ref[...] = nm
        nv_ref[...] = nv

    row = pl.BlockSpec((tr, wc), lambda i, s: (i, 0))
    return pl.pallas_call(
        body, name=name,
        grid_spec=pltpu.PrefetchScalarGridSpec(
            num_scalar_prefetch=1, grid=(r // tr,),
            in_specs=[pl.BlockSpec((1, tr, c), lambda i, s: (s[0], i, 0)), pl.BlockSpec((1, tr, c), lambda i, s: (s[1], i, 0)),
                      pl.BlockSpec((3, tr, c), lambda i, s: (0, i, 0)), row, row, row],
            out_specs=[row, row, row, row]),
        out_shape=[S((r, wc), f32)] * 4,
        compiler_params=pltpu.CompilerParams(dimension_semantics=("parallel",)),
    )(idx, gmine, recv1, recv2, w, m, v)


def _pair_sum(name, idx, gmine, recv1):
    _, r, c = gmine.shape
    g4 = gmine.reshape(4, 2, r, c)

    def body(idx_ref, gm_ref, r1_ref, o_ref):
        o_ref[0] = (gm_ref[0, 0].astype(f32) + r1_ref[0].astype(f32)).astype(bf16)

    return pl.pallas_call(
        body, name=name,
        grid_spec=pltpu.PrefetchScalarGridSpec(
            num_scalar_prefetch=1, grid=(4,),
            in_specs=[pl.BlockSpec((1, 1, r, c), lambda ch, s: (ch, s[0], 0, 0)), pl.BlockSpec((1, r, c), lambda ch, s: (ch, 0, 0))],
            out_specs=pl.BlockSpec((1, r, c), lambda ch, s: (ch, 0, 0))),
        out_shape=S((4, r, c), bf16),
        compiler_params=pltpu.CompilerParams(dimension_semantics=("parallel",)),
    )(idx, g4, recv1)


def _adamw_small(parts, w, m, v, own_mask, own_g):
    def body(p_ref, w_ref, m_ref, v_ref, mask_ref, og_ref, g_ref, d_ref, nm_ref, nv_ref):
        g = p_ref[0]
        for k in range(1, NDEV):
            g = g + p_ref[k]
        g_ref[...] = g
        ge = jnp.where(mask_ref[...] > 0.5, og_ref[...], g)
        d, nm, nv = _adamw_math(w_ref[...], ge, m_ref[...], v_ref[...])
        d_ref[...] = d
        nm_ref[...] = nm
        nv_ref[...] = nv

    return pl.pallas_call(body, name="adamw_small", out_shape=[S(w.shape, f32)] * 4)(parts, w, m, v, own_mask, own_g)


def _sum_parts(parts):
    def body(p_ref, g_ref):
        g = p_ref[0]
        for k in range(1, NDEV):
            g = g + p_ref[k]
        g_ref[...] = g

    return pl.pallas_call(body, name="sum_parts", out_shape=S(parts.shape[1:], f32))(parts)


def _me():
    return lax.axis_index("x"), lax.axis_index("y"), lax.axis_index("c")


def _hbm_specs(n):
    return [pl.BlockSpec(memory_space=pl.ANY)] * n


def _allgather(name, xs):
    na = len(xs)

    def body(*refs):
        x_refs, out_refs = refs[:na], refs[na:2 * na]
        send_sems, recv_sems, local_sems = refs[2 * na:]
        mx, my, mc = _me()
        me, sib = (mx, my, mc), (mx, my, 1 - mc)
        chips = [(1 - mx, my), (mx, 1 - my), (1 - mx, 1 - my)]

        def slab(a, px, py, pc):
            return out_refs[a].at[4 * px + 2 * py + pc]

        def copy(a, k, block, to, own=False):
            return pltpu.make_async_remote_copy(
                src_ref=x_refs[a] if own else slab(a, *block), dst_ref=slab(a, *block),
                send_sem=send_sems.at[7 * a + k], recv_sem=recv_sems.at[7 * a + k], device_id=to, device_id_type=MESH)

        mine = [pltpu.make_async_copy(x_refs[a], slab(a, *me), local_sems.at[a]) for a in range(na)]
        first = []
        for a in range(na):
            mine[a].start()
            first += [copy(a, 0, me, sib, own=True)] + [copy(a, 1 + j, me, (*chip, mc), own=True) for j, chip in enumerate(chips)]
        for cp in first:
            cp.start()
        passed = []
        for j, chip in enumerate(chips):
            for a in range(na):
                copy(a, 1 + j, (*chip, mc), me).wait_recv()
                passed.append(copy(a, 4 + j, (*chip, mc), sib))
                passed[-1].start()
        for a in range(na):
            copy(a, 0, sib, me).wait_recv()
            for j, chip in enumerate(chips):
                copy(a, 4 + j, (*chip, 1 - mc), me).wait_recv()
        for cp in first + passed:
            cp.wait_send()
        for cp in mine:
            cp.wait()

    return pl.pallas_call(
        body, name=name, out_shape=[S((NDEV,) + x.shape, x.dtype) for x in xs],
        in_specs=_hbm_specs(na), out_specs=_hbm_specs(na),
        scratch_shapes=[pltpu.SemaphoreType.DMA((7 * na,)), pltpu.SemaphoreType.DMA((7 * na,)), pltpu.SemaphoreType.DMA((na,))],
    )(*xs)


def _rs_sibling(gs):
    na = len(gs)

    def body(*refs):
        g_refs, out_refs, send_sems, recv_sems = refs[:na], refs[na:2 * na], refs[2 * na], refs[2 * na + 1]
        mx, my, mc = _me()
        cps = [pltpu.make_async_remote_copy(
            src_ref=g_refs[a].at[2 * ch + 1 - mc], dst_ref=out_refs[a].at[ch], send_sem=send_sems.at[4 * a + ch],
            recv_sem=recv_sems.at[4 * a + ch], device_id=(mx, my, 1 - mc), device_id_type=MESH)
            for a in range(na) for ch in range(4)]
        for cp in cps:
            cp.start()
        for cp in cps:
            cp.wait_recv()
        for cp in cps:
            cp.wait_send()

    return pl.pallas_call(
        body, name="rs_sibling", out_shape=[S((4,) + g.shape[1:], g.dtype) for g in gs],
        in_specs=_hbm_specs(na), out_specs=_hbm_specs(na),
        scratch_shapes=[pltpu.SemaphoreType.DMA((4 * na,)), pltpu.SemaphoreType.DMA((4 * na,))],
    )(*gs)


def _side_allgather(xs):
    na = len(xs)

    def mk(x_refs, out_refs, sems):
        send_sems, recv_sems, local_sems = sems
        mx, my, mc = _me()
        me, sib = (mx, my, mc), (mx, my, 1 - mc)
        chips = [(1 - mx, my), (mx, 1 - my), (1 - mx, 1 - my)]

        def slab(a, px, py, pc):
            return out_refs[a].at[4 * px + 2 * py + pc]

        def copy(a, k, block, to, own=False):
            return pltpu.make_async_remote_copy(
                src_ref=x_refs[a] if own else slab(a, *block), dst_ref=slab(a, *block),
                send_sem=send_sems.at[7 * a + k], recv_sem=recv_sems.at[7 * a + k], device_id=to, device_id_type=MESH)

        mine = [pltpu.make_async_copy(x_refs[a], slab(a, *me), local_sems.at[a]) for a in range(na)]
        first = []
        for a in range(na):
            first += [copy(a, 0, me, sib, own=True)] + [copy(a, 1 + j, me, (*chip, mc), own=True) for j, chip in enumerate(chips)]
        return me, sib, chips, mc, copy, mine, first

    def start(x_refs, out_refs, sems):
        *_, mine, first = mk(x_refs, out_refs, sems)
        for cp in mine + first:
            cp.start()

    def finish(x_refs, out_refs, sems):
        me, sib, chips, mc, copy, mine, first = mk(x_refs, out_refs, sems)
        passed = []
        for j, chip in enumerate(chips):
            for a in range(na):
                copy(a, 1 + j, (*chip, mc), me).wait_recv()
                passed.append(copy(a, 4 + j, (*chip, mc), sib))
                passed[-1].start()
        for a in range(na):
            copy(a, 0, sib, me).wait_recv()
            for j, chip in enumerate(chips):
                copy(a, 4 + j, (*chip, 1 - mc), me).wait_recv()
        for cp in first + passed:
            cp.wait_send()
        for cp in mine:
            cp.wait()

    scratch = [pltpu.SemaphoreType.DMA((7 * na,)), pltpu.SemaphoreType.DMA((7 * na,)), pltpu.SemaphoreType.DMA((na,))]
    return _Side(xs, [S((NDEV,) + x.shape, x.dtype) for x in xs], scratch, start, finish)


def _side_exchange(arrs, nslot, out_slots, route):
    na = len(arrs)

    def copies(in_refs, out_refs, sems):
        send_sems, recv_sems = sems
        return [pltpu.make_async_remote_copy(
            src_ref=in_refs[a].at[src], dst_ref=out_refs[a].at[k], send_sem=send_sems.at[nslot * a + k],
            recv_sem=recv_sems.at[nslot * a + k], device_id=to, device_id_type=MESH)
            for a in range(na) for k, (src, to) in enumerate(route(*_me()))]

    def start(in_refs, out_refs, sems):
        for cp in copies(in_refs, out_refs, sems):
            cp.start()

    def finish(in_refs, out_refs, sems):
        cps = copies(in_refs, out_refs, sems)
        for cp in cps:
            cp.wait_recv()
        for cp in cps:
            cp.wait_send()

    scratch = [pltpu.SemaphoreType.DMA((nslot * na,)), pltpu.SemaphoreType.DMA((nslot * na,))]
    return _Side(arrs, [S((out_slots,) + x.shape[1:], x.dtype) for x in arrs], scratch, start, finish)


def _side_rs_sibling(gs):
    return _side_exchange(gs, 4, 4, lambda mx, my, mc: [(2 * ch + 1 - mc, (mx, my, 1 - mc)) for ch in range(4)])


def _side_rs_chips(ps):
    return _side_exchange(ps, 3, 3, lambda mx, my, mc: [(2 * cx + cy, (cx, cy, mc)) for cx, cy in
                                                        [(1 - mx, my), (mx, 1 - my), (1 - mx, 1 - my)]])


_COLW = 512
_COL_NAMES = ("l0_w_in", "l0_w_ff1", "l1_w_in", "l1_w_ff1")
_ROW_NAMES = ("l0_w_out", "l0_w_ff2", "l1_w_out", "l1_w_ff2")
_BIG_NAMES = _COL_NAMES + _ROW_NAMES


_W_IN0_RUNS = ((0, 1536, 0), (1536, 1544, 3584), (1544, 3080, 1536), (3080, 3088, 3592), (3088, 3600, 3072))


def _w_in0_slabs(g, ncols):
    slabs = []
    for d in range(NDEV):
        lo, hi = d * ncols, (d + 1) * ncols
        parts = [g[:, r + max(lo, a) - a:r + min(hi, b) - a] for a, b, r in _W_IN0_RUNS if max(lo, a) < min(hi, b)]
        slabs.append(jnp.concatenate(parts + [jnp.zeros((D, _COLW - ncols), g.dtype)], axis=1))
    return jnp.stack(slabs)


def _w_in0_regrouped(gathered, ncols):
    parts = []
    for a, b, _ in sorted(_W_IN0_RUNS, key=lambda run: run[2]):
        for d in range(a // ncols, (b - 1) // ncols + 1):
            parts.append(gathered[d, :, max(a, d * ncols) - d * ncols:min(b, (d + 1) * ncols) - d * ncols])
    used = sum(p.shape[1] for p in parts)
    return jnp.concatenate(parts + [jnp.zeros((D, ZW0 - used), gathered.dtype)], axis=1)


def _sq(t):
    return t * t


def _epi_res_norm(acc, res, gain):
    y = acc + res
    h = _f_norm(y, gain)[0]
    return y, h, h


_RES_NORM_OUTS = [("tile", f32), ("tile", bf16), ("tile_t", bf16)]


def _epi_norm_bwd(acc, x, dres, gain):
    _, vjp = jax.vjp(lambda xx, gg: _f_norm(xx, gg)[0], x, gain)
    dx, dgain = vjp(acc)
    dx = dx + dres
    return dx, dgain, dx


_NORM_BWD_OUTS = [("tile", f32), ("rows", f32), ("tile_t", bf16)]


def _epi_loss(acc, res, tgt):
    e = acc + res - tgt
    dy = e * (1.0 / D)
    return dy, dy, jnp.sum(e * e, axis=0, keepdims=True)


_LOSS_OUTS = [("tile", f32), ("tile_t", bf16), ("rows", f32)]


def _mlp_fwd(tag, x, h, w1, w2, epi, extra, consts, outs):
    a = _mm(f"{tag}_ff1", h, w1, NN, bf16, epi=lambda acc: jnp.maximum(acc, 0.0))
    return _mm(f"{tag}_ff2", a, w2, NN, f32, a_fn=_sq, epi=epi, extra=(x,) + tuple(extra), consts=consts, outs=outs), a


def _mlp_bwd(tag, x, gain, w1, w2, ht, a, dy, dyt, side=None):
    da = _mm(f"{tag}_ff2_dx", dy, w2, NT, bf16, epi=lambda acc, av: acc * 2.0 * av.astype(f32), extra=(a,), side=side)
    if side is not None:
        da, side_res = da
    dw2 = _mm(f"{tag}_ff2_dw", dyt, a, NN, bf16, b_fn=_sq, out_t=True)
    res = _mm(f"{tag}_ff1_dx", da, w1, NT, f32, epi=_epi_norm_bwd, extra=(x, dy), consts=(gain,), outs=_NORM_BWD_OUTS)
    dw1 = _mm(f"{tag}_ff1_dw", ht, da, NN, bf16, slab=True)
    return (res, dw1, dw2) if side is None else ((res, dw1, dw2), side_res)


def _row(v):
    return v.reshape(1, -1).astype(f32)


_L0_REST = ("l0_w_ff1", "l0_w_out")
_L1_MIX = ("l0_w_ff2", "l1_w_in", "l1_w_out")
_L1_FFN = ("l1_w_ff1", "l1_w_ff2")
_GRAD_A = ("l1_w_ff1", "l1_w_ff2", "l1_w_out", "l1_w_in")
_GRAD_B = ("l0_w_ff1", "l0_w_ff2", "l0_w_out")
_IN_PLACE = ("l0_w_ff1", "l1_w_in", "l1_w_ff1")


def _train(x, tgt, args, mom, var, t):
    n = x.shape[0]
    nchunk = t // CH
    mx, my, mc = _me()
    dev, chip = 4 * mx + 2 * my + mc, 2 * mx + my
    core_idx = jnp.reshape(mc, (1,)).astype(jnp.int32)
    own_idx = jnp.stack([dev, chip]).astype(jnp.int32)
    ncols = {nm: args[nm].shape[1] for nm in _COL_NAMES}
    conv_cols = args["l0_gdn_conv"].shape[1]
    g, big, w = {}, {}, {}

    def send(nm):
        a = args[nm].astype(bf16)
        return jnp.pad(a, ((0, 0), (0, _COLW - ncols[nm]))) if nm in _COL_NAMES else a

    def take(names, gathered):
        for nm, arr in zip(names, gathered):
            w[nm] = arr if nm in _IN_PLACE else arr.reshape(-1, D)

    def by_dev(nm, ga):
        return ga if nm in _COL_NAMES else ga.reshape(NDEV, -1, D)

    def pair(names, gs, r1s):
        return [_pair_sum(f"rs_pair_sum_{nm}", core_idx, ga, r1) for nm, ga, r1 in zip(names, gs, r1s)]

    def adam(names, gs, r1s, r2s):
        for nm, ga, r1, r2 in zip(names, gs, r1s, r2s):
            big[nm] = _adamw_big(f"adamw_{nm}", own_idx, ga, r1, r2, args[nm], mom[nm], var[nm])


    li = jnp.arange(FOX_W)
    pm = jnp.where((li[:, None] // FOX_D) == (li[None, :] // FOX_D), 1.0 / FOX_D, 0.0).astype(f32)
    lane_head = jnp.arange(GDN_W) // HD
    sel = lambda first_lane: (jnp.arange(LANES)[:, None] == (first_lane + lane_head)[None, :]).astype(f32)
    e_beta, e_alpha = sel(FOX_H), sel(FOX_H + GDN_H)
    alog_b, dt_b = _row(jnp.repeat(args["l0_gdn_A_log"], HD)), _row(jnp.repeat(args["l0_gdn_dt_bias"], HD))
    gq_t, gk_t = _row(jnp.tile(args["l0_fox_q_norm"], FOX_H)), _row(jnp.tile(args["l0_fox_k_norm"], FOX_H))
    on0_t, on1_t = _row(jnp.tile(args["l0_gdn_o_norm"], GDN_H)), _row(jnp.tile(args["l1_hgrn_o_norm"], HG_H))
    fbias = jnp.pad(_row(args["l0_fox_f_bias"]), ((0, 0), (0, LANES - FOX_H)))
    g0m, g0f, g1m, g1f = (_row(args[k]) for k in ("l0_mix_norm", "l0_ffn_norm", "l1_mix_norm", "l1_ffn_norm"))
    lbl = args["hgrn_lb_logits"].astype(f32)

    first = [send("l0_w_in"), jnp.pad(args["l0_gdn_conv"], ((0, 4), (0, LANES * 2 - conv_cols)))]
    (h0, h0t), first = _tok_fwd("l0_mix_norm", _f_norm, [(x, D, 0)], [(g0m, None, 0)], [(D, bf16)], TOK_ROWS, also_t=(0,),
                                side=_side_allgather(first))
    w_in0 = _w_in0_regrouped(first[0], ncols["l0_w_in"])
    wconv = first[1][:, :4, :conv_cols].transpose(1, 0, 2).reshape(4, NDEV * conv_cols)
    z0 = _mm("l0_in", h0, w_in0, NN, f32)
    qk_rows = [(z0, 2 * FOX_W, 0)]
    qk_consts = [(gq_t, None, 0), (gk_t, None, 0), (pm, None, 0)]
    qn, kn = _tok_fwd("fox_pre", _f_foxpre, qk_rows, qk_consts, [(FOX_W, f32)] * 2, TOK_ROWS)
    ccol, crow = _fox_gate_fwd(z0, fbias, t)
    fox_o, got = _fox_attn_fwd(qn, kn, z0, ccol, crow, t, side=_side_allgather([send(nm) for nm in _L0_REST]))
    take(_L0_REST, got)
    conv_rows, conv_consts = [(z0, LANES, Z0_GQKV)], [(wconv, LANES, 0)]
    (qkv,) = _tok_fwd("gdn_conv", _f_conv, conv_rows, conv_consts, [(LANES, f32)], t, ncb=12, with_j=True)
    gate_consts = [(e_beta, None, 0), (e_alpha, None, 0), (alog_b, None, 0), (dt_b, None, 0)]
    intra_rows = [(qkv, GDN_W, 0), (qkv, GDN_W, 1), (qkv, GDN_W, 2), (z0, LANES, Z0_SMALL)]
    intra, got = _tok_fwd("gdn_intra", _f_gdn_gated, intra_rows, gate_consts, [(GDN_W, f32)] * 7, 2 * CH,
                          side=_side_allgather([send(nm) for nm in _L1_MIX]))
    take(_L1_MIX, got)
    inter_rows = [(a, 0, 1) for a in intra[:6]]
    full_rows = inter_rows + [(fox_o, 0, 1), (z0, Z0_GG // 4, 1)]
    (cat0, gdn_hist), got = _scan_fwd("gdn_scan", _f_gdn_full, full_rows, GDN_H, nchunk, consts=(on0_t,), out_dtype=bf16, out_k=2,
                                      side=_side_allgather([send(nm) for nm in _L1_FFN[1:]]))
    take(_L1_FFN[1:], got)
    x1, hf0, hf0t = _mm("l0_out", cat0, w["l0_w_out"], NN, f32, epi=_epi_res_norm, extra=(x,), consts=(g0f,), outs=_RES_NORM_OUTS)
    (x2, h1, h1t), a0 = _mlp_fwd("l0", x1, hf0, w["l0_w_ff1"], w["l0_w_ff2"], _epi_res_norm, (), (g1m,), _RES_NORM_OUTS)

    z1 = _mm("l1_in", h1, w["l1_w_in"], NN, f32)
    (cat1, hg_hist), got = _scan_fwd("hgrn_scan", _f_hgrn_full, [(z1, 0, 4)], HG_H, nchunk, consts=(lbl, on1_t), out_dtype=bf16,
                                     side=_side_allgather([send(nm) for nm in _L1_FFN[:1]]))
    take(_L1_FFN[:1], got)
    x3, hf1, hf1t = _mm("l1_out", cat1, w["l1_w_out"], NN, f32, epi=_epi_res_norm, extra=(x2,), consts=(g1f,), outs=_RES_NORM_OUTS)
    (dy, dyt, loss_row), a1 = _mlp_fwd("l1", x3, hf1, w["l1_w_ff1"], w["l1_w_ff2"], _epi_loss, (tgt,), (), _LOSS_OUTS)

    (dx3, g["l1_ffn_norm"], dx3t), ga_ff1, ga_ff2 = _mlp_bwd("l1", x3, g1f, w["l1_w_ff1"], w["l1_w_ff2"], hf1t, a1, dy, dyt)
    dcat1 = _mm("l1_out_dx", dx3, w["l1_w_out"], NT, f32)
    ga_out = _mm("l1_out_dw", dx3t, cat1, NN, bf16, out_t=True)
    dz1, dlbl, don1 = _scan_bwd("hgrn_scan_bwd", _f_hgrn_full, [(z1, 0, 4)], hg_hist, dcat1, HG_H, nchunk, dtypes=[bf16],
                                consts=(lbl, on1_t))
    dx2, g["l1_mix_norm"], dx2t = _mm("l1_in_dx", dz1, w["l1_w_in"], NT, f32, epi=_epi_norm_bwd, extra=(x2, dx3), consts=(g1m,),
                                      outs=_NORM_BWD_OUTS)
    ga_in = _mm("l1_in_dw", h1t, dz1, NN, bf16, slab=True)
    g["l1_hgrn_o_norm"] = don1.reshape(HG_H, HD).sum(0)
    g["hgrn_lb_logits"] = dlbl
    gs_a = [by_dev(nm, ga) for nm, ga in zip(_GRAD_A, (ga_ff1, ga_ff2, ga_out, ga_in))]

    (dx1, g["l0_ffn_norm"], dx1t), gb_ff1, gb_ff2 = _mlp_bwd("l0", x1, g0f, w["l0_w_ff1"], w["l0_w_ff2"], hf0t, a0, dx2, dx2t)
    dcat0 = _mm("l0_out_dx", dx1, w["l0_w_out"], NT, f32)
    gb_out = _mm("l0_out_dw", dx1t, cat0, NN, bf16, out_t=True)
    gs_b = [by_dev(nm, ga) for nm, ga in zip(_GRAD_B, (gb_ff1, gb_ff2, gb_out))]
    (*dinter, dfox_o, dgg, don0), r1_a = _scan_bwd("gdn_scan_bwd", _f_gdn_full, full_rows, gdn_hist, dcat0, GDN_H, nchunk,
                                                   consts=(on0_t,), out_k=2, dtypes=[f32] * 7 + [bf16], cps=1,
                                                   side=_side_rs_sibling(gs_a))
    pairs_a = pair(_GRAD_A, gs_a, r1_a)
    (dqn, dkn, dfv, dcq, dck), got = _fox_attn_bwd(qn, kn, z0, ccol, crow, dfox_o, t,
                                                   side=_join_sides([_side_rs_chips(pairs_a), _side_rs_sibling(gs_b)]))
    r2_a, r1_b = got[:len(_GRAD_A)], got[len(_GRAD_A):]
    adam(_GRAD_A, gs_a, r1_a, r2_a)
    pairs_b = pair(_GRAD_B, gs_b, r1_b)
    (dqkv, dzs_g, dalog_b, ddt_b), r2_b = _tok_bwd(
        "gdn_intra_bwd", _f_gdn_gated_given, intra_rows + [(intra[6], GDN_W, 0)], gate_consts, [(a, GDN_W) for a in dinter],
        2 * CH, [0, 1, 2, 3], [2, 3], ncat=3, side=_side_rs_chips(pairs_b))
    adam(_GRAD_B, gs_b, r1_b, r2_b)
    dgqkv, dwconv = _tok_bwd("gdn_conv_bwd", _f_conv, conv_rows, conv_consts, [(dqkv, LANES)], t, [0], [0], ncb=12, with_j=True,
                             drow_dtype=bf16)
    dzs_f, dfb = _fox_gate_bwd(z0, fbias, dcq, dck, t)
    dzqk, dgq_t, dgk_t = _tok_bwd("fox_pre_bwd", _f_foxpre, qk_rows, qk_consts, [(dqn, FOX_W), (dkn, FOX_W)], TOK_ROWS, [0], [0, 1],
                                  drow_dtype=bf16)
    dz0 = jnp.concatenate([dzqk, dfv.astype(bf16), dgqkv, dgg, (dzs_g + dzs_f).astype(bf16), jnp.zeros((n, ZW0 - 3712), bf16)], axis=1)
    gs_c = [_w_in0_slabs(_mm("l0_in_dw", h0t, dz0, NN, bf16), ncols["l0_w_in"])]
    r1_c = _rs_sibling(gs_c)
    pairs_c = pair(("l0_w_in",), gs_c, r1_c)
    (dx, g["l0_mix_norm"], _), r2_c = _mm("l0_in_dx", dz0, w_in0, NT, f32, epi=_epi_norm_bwd, extra=(x, dx1), consts=(g0m,),
                                          outs=_NORM_BWD_OUTS, side=_side_rs_chips(pairs_c))
    adam(("l0_w_in",), gs_c, r1_c, r2_c)
    g["l0_fox_q_norm"] = dgq_t.reshape(FOX_H, FOX_D).sum(0)
    g["l0_fox_k_norm"] = dgk_t.reshape(FOX_H, FOX_D).sum(0)
    g["l0_fox_f_bias"] = dfb[0, :FOX_H]
    g["l0_gdn_conv"] = dwconv
    g["l0_gdn_A_log"] = dalog_b.reshape(GDN_H, HD).sum(1)
    g["l0_gdn_dt_bias"] = ddt_b.reshape(GDN_H, HD).sum(1)
    g["l0_gdn_o_norm"] = don0.reshape(GDN_H, HD).sum(0)
    return loss_row, dx, g, big


_NAMES = ("l0_mix_norm", "l0_w_in", "l0_fox_q_norm", "l0_fox_k_norm", "l0_fox_f_bias", "l0_gdn_conv", "l0_gdn_A_log",
          "l0_gdn_dt_bias", "l0_gdn_o_norm", "l0_w_out", "l0_ffn_norm", "l0_w_ff1", "l0_w_ff2", "l1_mix_norm", "l1_w_in",
          "l1_hgrn_o_norm", "l1_w_out", "l1_ffn_norm", "l1_w_ff1", "l1_w_ff2", "hgrn_lb_logits")
_SMALL_NAMES = tuple(nm for nm in _NAMES if nm not in _BIG_NAMES)
_SMALL_ROWS = 16


def _pack_small(vals):
    flat = jnp.concatenate([vals[nm].reshape(-1).astype(f32) for nm in _SMALL_NAMES])
    return jnp.pad(flat, (0, _SMALL_ROWS * D - flat.shape[0])).reshape(_SMALL_ROWS, D)


def _unpack_small(packed, shapes):
    flat = packed.reshape(-1)
    out, off = {}, 0
    for nm in _SMALL_NAMES:
        size = 1
        for s in shapes[nm]:
            size *= s
        out[nm] = flat[off:off + size].reshape(shapes[nm])
        off += size
    return out, off


def kernel(x, l0_mix_norm, l0_w_in, l0_fox_q_norm, l0_fox_k_norm, l0_fox_f_bias, l0_gdn_conv, l0_gdn_A_log, l0_gdn_dt_bias, l0_gdn_o_norm, l0_w_out, l0_ffn_norm, l0_w_ff1, l0_w_ff2, l1_mix_norm, l1_w_in, l1_hgrn_o_norm, l1_w_out, l1_ffn_norm, l1_w_ff1, l1_w_ff2, hgrn_lb_logits, loss_target, m_l0_mix_norm, m_l0_w_in, m_l0_fox_q_norm, m_l0_fox_k_norm, m_l0_fox_f_bias, m_l0_gdn_conv, m_l0_gdn_A_log, m_l0_gdn_dt_bias, m_l0_gdn_o_norm, m_l0_w_out, m_l0_ffn_norm, m_l0_w_ff1, m_l0_w_ff2, m_l1_mix_norm, m_l1_w_in, m_l1_hgrn_o_norm, m_l1_w_out, m_l1_ffn_norm, m_l1_w_ff1, m_l1_w_ff2, m_hgrn_lb_logits, v_l0_mix_norm, v_l0_w_in, v_l0_fox_q_norm, v_l0_fox_k_norm, v_l0_fox_f_bias, v_l0_gdn_conv, v_l0_gdn_A_log, v_l0_gdn_dt_bias, v_l0_gdn_o_norm, v_l0_w_out, v_l0_ffn_norm, v_l0_w_ff1, v_l0_w_ff2, v_l1_mix_norm, v_l1_w_in, v_l1_hgrn_o_norm, v_l1_w_out, v_l1_ffn_norm, v_l1_w_ff1, v_l1_w_ff2, v_hgrn_lb_logits):
    args = dict(zip(_NAMES, (l0_mix_norm, l0_w_in, l0_fox_q_norm, l0_fox_k_norm, l0_fox_f_bias, l0_gdn_conv, l0_gdn_A_log, l0_gdn_dt_bias, l0_gdn_o_norm, l0_w_out, l0_ffn_norm, l0_w_ff1, l0_w_ff2, l1_mix_norm, l1_w_in, l1_hgrn_o_norm, l1_w_out, l1_ffn_norm, l1_w_ff1, l1_w_ff2, hgrn_lb_logits)))
    mom = dict(zip(_NAMES, (m_l0_mix_norm, m_l0_w_in, m_l0_fox_q_norm, m_l0_fox_k_norm, m_l0_fox_f_bias, m_l0_gdn_conv, m_l0_gdn_A_log, m_l0_gdn_dt_bias, m_l0_gdn_o_norm, m_l0_w_out, m_l0_ffn_norm, m_l0_w_ff1, m_l0_w_ff2, m_l1_mix_norm, m_l1_w_in, m_l1_hgrn_o_norm, m_l1_w_out, m_l1_ffn_norm, m_l1_w_ff1, m_l1_w_ff2, m_hgrn_lb_logits)))
    var = dict(zip(_NAMES, (v_l0_mix_norm, v_l0_w_in, v_l0_fox_q_norm, v_l0_fox_k_norm, v_l0_fox_f_bias, v_l0_gdn_conv, v_l0_gdn_A_log, v_l0_gdn_dt_bias, v_l0_gdn_o_norm, v_l0_w_out, v_l0_ffn_norm, v_l0_w_ff1, v_l0_w_ff2, v_l1_mix_norm, v_l1_w_in, v_l1_hgrn_o_norm, v_l1_w_out, v_l1_ffn_norm, v_l1_w_ff1, v_l1_w_ff2, v_hgrn_lb_logits)))
    nb, t, _ = x.shape
    dev = 4 * lax.axis_index("x") + 2 * lax.axis_index("y") + lax.axis_index("c")
    conv_cols = l0_gdn_conv.shape[1]
    loss_row, dx, g, big = _train(x.reshape(nb * t, D), loss_target.reshape(nb * t, D), args, mom, var, t)

    shapes = {nm: args[nm].shape for nm in _SMALL_NAMES}
    gsm = dict(g)
    gsm["l0_gdn_conv"] = jnp.zeros(shapes["l0_gdn_conv"], f32)
    packed = _pack_small(gsm)
    _, used = _unpack_small(packed, shapes)
    flat_extra = jnp.concatenate([jnp.sum(loss_row).reshape(1), g["l0_gdn_conv"].reshape(-1)])
    packed = packed.reshape(-1).at[used:used + flat_extra.shape[0]].set(flat_extra).reshape(_SMALL_ROWS, D)
    (parts,) = _allgather("ag_small", [packed])
    total = _sum_parts(parts).reshape(-1)
    loss = 0.5 * total[used] / D
    conv_g_full = total[used + 1:used + 1 + 4 * NDEV * conv_cols].reshape(4, NDEV * conv_cols)
    conv_g = lax.dynamic_slice(conv_g_full, (0, dev * conv_cols), (4, conv_cols))
    own_vals = {nm: jnp.zeros(shapes[nm], f32) for nm in _SMALL_NAMES}
    own_vals["l0_gdn_conv"] = conv_g
    own_mask = {nm: jnp.zeros(shapes[nm], f32) for nm in _SMALL_NAMES}
    own_mask["l0_gdn_conv"] = jnp.ones(shapes["l0_gdn_conv"], f32)
    small = _adamw_small(parts, _pack_small(args), _pack_small(mom), _pack_small(var), _pack_small(own_mask), _pack_small(own_vals))
    small = [_unpack_small(a, shapes)[0] for a in small]
    small[0]["l0_gdn_conv"] = conv_g

    outs = [loss, dx.reshape(nb, t, D)]
    for k in range(4):
        outs += [big[nm][k] if nm in _BIG_NAMES else small[k][nm] for nm in _NAMES]
    return tuple(outs)
```

```python
import functools

import jax
import jax.numpy as jnp
from jax import lax
from jax.experimental import pallas as pl
from jax.experimental.pallas import tpu as pltpu

f32, bf16 = jnp.float32, jnp.bfloat16
NN = (((1,), (0,)), ((), ()))
NT = (((1,), (1,)), ((), ()))
TN = (((0,), (0,)), ((), ()))
MESH = pl.DeviceIdType.MESH
S = jax.ShapeDtypeStruct

EPS = 1e-6
D = 1024
LANES = 128
FOX_H, FOX_D, FOX_W = 8, 64, 512
GDN_H, HD, GDN_W = 4, 128, 512
HG_H = 8
CH = 64
TOK_ROWS = 256
ZW0 = 3840
NDEV = 8
ADAM_LR, ADAM_B1, ADAM_B2, ADAM_EPS, ADAM_WD, ADAM_STEP = 0.001, 0.9, 0.999, 1e-08, 0.01, 10

Z0_FV, Z0_GQKV, Z0_GG, Z0_SMALL = 8, 12, 24, 28


def _dot(a, b, dims=NN):
    return lax.dot_general(a, b, dims, preferred_element_type=f32)


def _iota2(shape, axis):
    return lax.broadcasted_iota(jnp.int32, shape, axis)


def _split3(x):
    x1 = x.astype(bf16)
    r = x - x1.astype(f32)
    x2 = r.astype(bf16)
    return x1, x2, (r - x2.astype(f32)).astype(bf16)


def _dot_sel(a, b, dims=NN, exact_lhs=False):
    if exact_lhs:
        return sum(_dot(a.astype(bf16), piece, dims) for piece in _split3(b))
    return sum(_dot(piece, b.astype(bf16), dims) for piece in _split3(a))


@jax.custom_vjp
def _sel_rhs(a, b):
    return _dot_sel(a, b)


_sel_rhs.defvjp(lambda a, b: (_dot_sel(a, b), b), lambda b, g: (_dot_sel(g, b, NT), jnp.zeros_like(b)))


@jax.custom_vjp
def _sel_lhs(a, x):
    return _dot_sel(a, x, exact_lhs=True)


_sel_lhs.defvjp(lambda a, x: (_dot_sel(a, x, exact_lhs=True), a), lambda a, g: (jnp.zeros_like(a), _dot_sel(a, g, TN, exact_lhs=True)))


class _Side:
    def __init__(self, ins, out_shapes, scratch, start, finish):
        self.ins, self.out_shapes, self.scratch, self.start, self.finish = list(ins), list(out_shapes), list(scratch), start, finish


def _join_sides(sides):
    def split(refs, counts):
        out, off = [], 0
        for c in counts:
            out.append(refs[off:off + c])
            off += c
        return out

    ni, no, ns = ([len(getattr(sd, a)) for sd in sides] for a in ("ins", "out_shapes", "scratch"))

    def run(which):
        def go(ins, outs, sems):
            for sd, i, o, c in zip(sides, split(ins, ni), split(outs, no), split(sems, ns)):
                getattr(sd, which)(i, o, c)
        return go

    return _Side(sum((sd.ins for sd in sides), []), sum((sd.out_shapes for sd in sides), []),
                 sum((sd.scratch for sd in sides), []), run("start"), run("finish"))


def _pcall(side, body, *, name, grid, in_specs, out_specs, out_shape, scratch_shapes=(), compiler_params=None):
    if side is None:
        return pl.pallas_call(body, name=name, grid=grid, in_specs=in_specs, out_specs=out_specs, out_shape=out_shape,
                              scratch_shapes=scratch_shapes, compiler_params=compiler_params)
    single = not isinstance(out_shape, (list, tuple))
    ospecs, oshape = ([out_specs], [out_shape]) if single else (list(out_specs), list(out_shape))
    nin, nout, nscr = len(in_specs), len(ospecs), len(scratch_shapes)
    si, so = len(side.ins), len(side.out_shapes)

    def wrapped(*refs):
        o0 = nin + si
        c0 = o0 + nout + so
        sins, souts, ssems = refs[nin:o0], refs[o0 + nout:c0], refs[c0 + nscr:]
        ids = [pl.program_id(a) for a in range(len(grid))]
        first = functools.reduce(jnp.logical_and, [i == 0 for i in ids])
        last = functools.reduce(jnp.logical_and, [i == g - 1 for i, g in zip(ids, grid)])

        @pl.when(first)
        def _():
            side.start(sins, souts, ssems)

        body(*refs[:nin], *refs[o0:o0 + nout], *refs[c0:c0 + nscr])

        @pl.when(last)
        def _():
            side.finish(sins, souts, ssems)

    call = pl.pallas_call(
        wrapped, name=name, grid=grid, in_specs=list(in_specs) + _hbm_specs(si), out_specs=ospecs + _hbm_specs(so),
        out_shape=oshape + side.out_shapes, scratch_shapes=list(scratch_shapes) + side.scratch,
        compiler_params=pltpu.CompilerParams(dimension_semantics=("arbitrary",) * len(grid),
                                             vmem_limit_bytes=getattr(compiler_params, "vmem_limit_bytes", None)))

    def run(*args):
        res = call(*args, *side.ins)
        return (res[0] if single else list(res[:nout])), list(res[nout:])

    return run


def _tok_specs(rows, consts, tm):
    specs = []
    for (_, w, base) in rows:
        specs.append(pl.BlockSpec((tm, w), functools.partial(lambda j, i, b: (i, b + j), b=base)))
    for (arr, w, base) in consts:
        if w is None:
            specs.append(pl.BlockSpec(arr.shape, lambda j, i: (0, 0)))
        else:
            specs.append(pl.BlockSpec((arr.shape[0], w), functools.partial(lambda j, i, b: (0, b + j), b=base)))
    return specs


def _tok_fwd(name, f, rows, consts, outs, tm, ncb=1, with_j=False, also_t=(), side=None):
    n = rows[0][0].shape[0]
    nin = len(rows) + len(consts)
    nout = len(outs)

    def body(*refs):
        ins = [r[...] for r in refs[:nin]]
        vals = f(pl.program_id(0), *ins) if with_j else f(*ins)
        for r, v in zip(refs[nin:nin + nout], vals):
            r[...] = v.astype(r.dtype)
        for r, k in zip(refs[nin + nout:], also_t):
            r[...] = vals[k].T.astype(r.dtype)

    return _pcall(
        side, body, name=name, grid=(ncb, n // tm),
        in_specs=_tok_specs(rows, consts, tm),
        out_specs=[pl.BlockSpec((tm, w), lambda j, i: (i, j)) for (w, _) in outs]
        + [pl.BlockSpec((outs[k][0], tm), lambda j, i: (0, i)) for k in also_t],
        out_shape=[S((n, w * ncb), dt) for (w, dt) in outs] + [S((outs[k][0], n), bf16) for k in also_t],
        compiler_params=pltpu.CompilerParams(dimension_semantics=("parallel", "parallel")),
    )(*[r[0] for r in rows], *[c[0] for c in consts])


def _tok_bwd(name, f, rows, consts, cots, tm, drow, dconst, ncb=1, with_j=False, addto=None, also_t=(), drow_dtype=f32, side=None, ncat=0):
    n = rows[0][0].shape[0]
    nr, nc, nct = len(rows), len(consts), len(cots)
    addto = addto or {}
    add_keys = sorted(addto)
    nadd = len(add_keys)

    def body(*refs):
        ins = [r[...] for r in refs[:nr + nc]]
        cot = [r[...] for r in refs[nr + nc:nr + nc + nct]]
        adds = refs[nr + nc + nct:nr + nc + nct + nadd]
        outs = refs[nr + nc + nct + nadd:]
        pos = list(drow) + [nr + k for k in dconst]

        def g(*dargs):
            full = list(ins)
            for p, a in zip(pos, dargs):
                full[p] = a
            return tuple(f(pl.program_id(0), *full) if with_j else f(*full))

        vals, vjp = jax.vjp(g, *[ins[p] for p in pos])
        grads = vjp(tuple(c.astype(v.dtype) for c, v in zip(cot, vals)))
        off = 0
        for k in range(len(drow)):
            gk = grads[k]
            if k in addto:
                gk = gk + adds[add_keys.index(k)][...]
            if k < ncat:
                outs[0][:, off:off + gk.shape[1]] = gk.astype(outs[0].dtype)
                off += gk.shape[1]
            else:
                outs[k - skip][...] = gk.astype(outs[k - skip].dtype)
            if k in also_t:
                tref = outs[len(drow) - skip + len(dconst) + list(also_t).index(k)]
                tref[...] = gk.T.astype(tref.dtype)
        first = pl.program_id(1) == 0
        for k in range(len(dconst)):
            ref = outs[len(drow) - skip + k]

            @pl.when(first)
            def _():
                ref[...] = jnp.zeros_like(ref)

            ref[...] += grads[len(drow) + k]

    skip = max(ncat - 1, 0)
    in_specs = _tok_specs(rows, consts, tm)
    in_specs += [pl.BlockSpec((tm, w), lambda j, i: (i, j)) for (_, w) in cots]
    in_specs += [pl.BlockSpec((tm, rows[drow[k]][1]), lambda j, i: (i, j)) for k in add_keys]
    dts = drow_dtype if isinstance(drow_dtype, (list, tuple)) else [drow_dtype] * len(drow)
    widths = [rows[k][1] for k in drow]
    if ncat:
        widths, dts = [sum(widths[:ncat])] + widths[ncat:], [dts[0]] + list(dts[ncat:])
    out_specs = [pl.BlockSpec((tm, wd), lambda j, i: (i, j)) for wd in widths]
    out_shape = [S((n, wd * ncb), dt) for wd, dt in zip(widths, dts)]
    for k in dconst:
        arr, w, _ = consts[k]
        if w is None:
            out_specs.append(pl.BlockSpec(arr.shape, lambda j, i: (0, 0)))
            out_shape.append(S(arr.shape, f32))
        else:
            out_specs.append(pl.BlockSpec((arr.shape[0], w), lambda j, i: (0, j)))
            out_shape.append(S((arr.shape[0], w * ncb), f32))
    for k in also_t:
        out_specs.append(pl.BlockSpec((rows[drow[k]][1], tm), lambda j, i: (0, i)))
        out_shape.append(S((rows[drow[k]][1], n), bf16))
    return _pcall(
        side, body, name=name, grid=(ncb, n // tm), in_specs=in_specs, out_specs=out_specs, out_shape=out_shape,
        compiler_params=pltpu.CompilerParams(dimension_semantics=("parallel", "arbitrary")),
    )(*[r[0] for r in rows], *[c[0] for c in consts], *[c[0] for c in cots], *[addto[k] for k in add_keys])


SCAN_CHUNKS = 2


def _scan_fwd(name, f, rows, nh, nchunk, side=None, consts=(), out_dtype=f32, out_k=1):
    n = rows[0][0].shape[0]
    nb = n // (CH * nchunk)
    nin, nco = len(rows), len(consts)
    w = nh * HD
    cps = SCAN_CHUNKS

    def body(*refs):
        o_ref, hist_ref, st = refs[nin + nco], refs[nin + nco + 1], refs[nin + nco + 2]

        @pl.when(pl.program_id(0) == 0)
        def _():
            st[...] = jnp.zeros_like(st)

        cvals = [r[...] for r in refs[nin:nin + nco]]
        state = st[...]
        for sub in range(cps):
            rr = pl.ds(sub * CH, CH)
            hist_ref[sub] = state.astype(hist_ref.dtype)
            tiles = [r[:, rr, :].reshape(nb * CH, r.shape[2]) for r in refs[:nin]]
            o, state = f(*tiles, *cvals, state)
            o_ref[:, rr, :] = o.reshape(nb, CH, out_k * w).astype(o_ref.dtype)
        st[...] = state

    seq3 = lambda a: a.reshape(nb, nchunk * CH, a.shape[1])
    res = _pcall(
        side, body, name=name, grid=(nchunk // cps,),
        in_specs=[pl.BlockSpec((nb, cps * CH, k * w), functools.partial(lambda c, base: (0, c, base), base=b)) for (_, b, k) in rows]
        + [pl.BlockSpec(c.shape, lambda c: (0, 0)) for c in consts],
        out_specs=[pl.BlockSpec((nb, cps * CH, out_k * w), lambda c: (0, c, 0)), pl.BlockSpec((cps, nb * w, HD), lambda c: (c, 0, 0))],
        out_shape=[S((nb, nchunk * CH, out_k * w), out_dtype), S((nchunk, nb * w, HD), bf16)],
        scratch_shapes=[pltpu.VMEM((nb * w, HD), f32)],
        compiler_params=pltpu.CompilerParams(dimension_semantics=("arbitrary",)),
    )(*[seq3(r[0]) for r in rows], *consts)
    (o, hist), extra = (res, None) if side is None else res
    out = [o.reshape(n, out_k * w), hist]
    return out if side is None else (out, extra)


def _scan_bwd(name, f, rows, hist, do, nh, nchunk, side=None, dtypes=None, consts=(), out_k=1, cps=SCAN_CHUNKS):
    n = rows[0][0].shape[0]
    nb = n // (CH * nchunk)
    nin, nco = len(rows), len(consts)
    w = nh * HD
    nstep = nchunk // cps

    def body(*refs):
        hist_ref, do_ref = refs[nin + nco], refs[nin + nco + 1]
        outs = refs[nin + nco + 2:nin + nco + 2 + nin]
        couts = refs[nin + nco + 2 + nin:nin + nco + 2 + nin + nco]
        ds = refs[nin + nco + 2 + nin + nco]

        @pl.when(pl.program_id(0) == 0)
        def _():
            ds[...] = jnp.zeros_like(ds)
            for c in couts:
                c[...] = jnp.zeros_like(c)

        cvals = [r[...] for r in refs[nin:nin + nco]]
        dstate = ds[...]
        for sub in reversed(range(cps)):
            rr = pl.ds(sub * CH, CH)
            tiles = [r[:, rr, :].reshape(nb * CH, r.shape[2]) for r in refs[:nin]]
            _, vjp = jax.vjp(f, *tiles, *cvals, hist_ref[sub].astype(f32))
            grads = vjp((do_ref[:, rr, :].reshape(nb * CH, out_k * w), dstate))
            for r, gk in zip(outs, grads[:nin]):
                r[:, rr, :] = gk.reshape(nb, CH, r.shape[2]).astype(r.dtype)
            for c, gk in zip(couts, grads[nin:nin + nco]):
                c[...] += gk
            dstate = grads[nin + nco]
        ds[...] = dstate

    seq3 = lambda a: a.reshape(nb, nchunk * CH, a.shape[1])
    rev = lambda c, base: (0, nstep - 1 - c, base)
    res = _pcall(
        side, body, name=name, grid=(nstep,),
        in_specs=[pl.BlockSpec((nb, cps * CH, k * w), functools.partial(rev, base=b)) for (_, b, k) in rows]
        + [pl.BlockSpec(c.shape, lambda c: (0, 0)) for c in consts]
        + [pl.BlockSpec((cps, nb * w, HD), lambda c: (nstep - 1 - c, 0, 0)),
           pl.BlockSpec((nb, cps * CH, out_k * w), functools.partial(rev, base=0))],
        out_specs=[pl.BlockSpec((nb, cps * CH, k * w), functools.partial(rev, base=0)) for (_, _, k) in rows]
        + [pl.BlockSpec(c.shape, lambda c: (0, 0)) for c in consts],
        out_shape=[S((nb, nchunk * CH, k * w), dt) for (_, _, k), dt in zip(rows, dtypes or [f32] * nin)]
        + [S(c.shape, f32) for c in consts],
        scratch_shapes=[pltpu.VMEM((nb * w, HD), f32)],
        compiler_params=pltpu.CompilerParams(dimension_semantics=("arbitrary",)),
    )(*[seq3(r[0]) for r in rows], *consts, hist, seq3(do))
    outs, extra = (res, None) if side is None else res
    outs = [o.reshape(n, o.shape[2]) for o in outs[:nin]] + list(outs[nin:])
    return outs if side is None else (outs, extra)


_VMEM_LIMIT = 56 * 2 ** 20
_VMEM_TILE_BUDGET = 40 * 2 ** 20


def _mm_tiles(m, n, k, sa, sb, so, sx, a_f32, b_f32, tn_fixed):
    best = None
    for tm in (1024, 512, 256, 128, 64):
        for tn in ((tn_fixed,) if tn_fixed else (1024, 768, 512, 384, 256, 128)):
            if m % tm or n % tn:
                continue
            need = 2 * (tm * k * sa + k * tn * sb + tm * tn * (so + sx)) + tm * tn * 4
            need += tm * k * (2 if sa == 4 else 0) + k * tn * (2 if sb == 4 else 0)
            need += tm * k * (4 if a_f32 else 0) + k * tn * (4 if b_f32 else 0)
            if need <= _VMEM_TILE_BUDGET and (best is None or (tm * tn, tm) > best[0]):
                best = ((tm * tn, tm), tm, tn)
    return best[1], best[2]


def _mm(name, a, b, dims, out_dtype, a_fn=None, b_fn=None, epi=None, extra=(), consts=(), outs=None, out_t=False, slab=False,
        side=None):
    m, kk = a.shape
    gathered = b.ndim == 3
    if gathered:
        nn = NDEV * _COLW if dims is NN else b.shape[1]
    else:
        nn = b.shape[1] if dims is NN else b.shape[0]
    kinds = [("tile", out_dtype)] if outs is None else list(outs)
    so = sum(jnp.dtype(dt).itemsize for kd, dt in kinds if kd != "rows")
    sx = sum(e.dtype.itemsize for e in extra)
    full_rows = bool(consts) or any(kd == "rows" for kd, _ in kinds)
    tm, tn = _mm_tiles(m, nn, kk, a.dtype.itemsize, b.dtype.itemsize, so, sx,
                       a_fn is not None, b_fn is not None,
                       _COLW if slab else (nn if full_rows else None))
    gblocks = tn // _COLW if (gathered and dims is NN) else 0
    assert not gblocks or (outs is None and not extra and not out_t and not slab and b_fn is None)
    nex, nco = len(extra), len(consts)

    def body(a_ref, b_ref, *rest):
        av = a_ref[...]
        if a_fn is not None:
            av = a_fn(av.astype(f32))
        av = av.astype(bf16)
        if gblocks:
            for gb in range(gblocks):
                part = _dot(av, b_ref[gb], NN)
                rest[nex + nco][:, _COLW * gb:_COLW * (gb + 1)] = (part if epi is None else epi(part)).astype(rest[nex + nco].dtype)
            return
        if gathered:
            acc = sum(_dot(av[:, _COLW * d:_COLW * (d + 1)], b_ref[d], NT) for d in range(NDEV))
        else:
            bv = b_ref[...]
            if b_fn is not None:
                bv = b_fn(bv.astype(f32))
            acc = _dot(av, bv.astype(bf16), dims)
        if epi is not None:
            acc = epi(acc, *[r[...] for r in rest[:nex + nco]])
        vals = acc if isinstance(acc, tuple) else (acc,)
        for (kd, _), o_ref, val in zip(kinds, rest[nex + nco:], vals):
            if kd == "rows":
                @pl.when(pl.program_id(0) == 0)
                def _():
                    o_ref[...] = jnp.zeros_like(o_ref)

                o_ref[...] += val
            elif kd == "tile_t" or out_t:
                o_ref[...] = val.T.astype(o_ref.dtype)
            elif slab:
                o_ref[0] = val.astype(o_ref.dtype)
            else:
                o_ref[...] = val.astype(o_ref.dtype)

    if gathered:
        bspec = (pl.BlockSpec((gblocks, kk, _COLW), lambda i, j: (j, 0, 0)) if dims is NN
                 else pl.BlockSpec((NDEV, tn, _COLW), lambda i, j: (0, j, 0)))
    else:
        bspec = pl.BlockSpec((kk, tn), lambda i, j: (0, j)) if dims is NN else pl.BlockSpec((tn, kk), lambda i, j: (j, 0))
    out_specs, out_shape = [], []
    for kd, dt in kinds:
        if kd == "rows":
            out_specs.append(pl.BlockSpec((1, nn), lambda i, j: (0, 0)))
            out_shape.append(S((1, nn), dt))
        elif kd == "tile_t" or out_t:
            out_specs.append(pl.BlockSpec((tn, tm), lambda i, j: (j, i)))
            out_shape.append(S((nn, m), dt))
        elif slab:
            out_specs.append(pl.BlockSpec((1, tm, tn), lambda i, j: (j, i, 0)))
            out_shape.append(S((nn // tn, m, tn), dt))
        else:
            out_specs.append(pl.BlockSpec((tm, tn), lambda i, j: (i, j)))
            out_shape.append(S((m, nn), dt))
    if outs is None:
        out_specs, out_shape = out_specs[0], out_shape[0]
    sem = ("arbitrary", "arbitrary") if any(kd == "rows" for kd, _ in kinds) else ("parallel", "parallel")
    return _pcall(
        side, body, name=name, grid=(m // tm, nn // tn),
        in_specs=[pl.BlockSpec((tm, kk), lambda i, j: (i, 0)), bspec]
        + [pl.BlockSpec((tm, tn), lambda i, j: (i, j)) for _ in extra]
        + [pl.BlockSpec(c.shape, lambda i, j: (0, 0)) for c in consts],
        out_specs=out_specs, out_shape=out_shape,
        compiler_params=pltpu.CompilerParams(dimension_semantics=sem, vmem_limit_bytes=_VMEM_LIMIT),
    )(a, b, *extra, *consts)


def _f_norm(x, g):
    return (x * lax.rsqrt(jnp.mean(x * x, axis=-1, keepdims=True) + EPS) * g,)


def _f_foxpre(zqk, gq, gk, pm):
    def nrm(t, g):
        return t * lax.rsqrt(_sel_rhs(t * t, pm) + EPS) * g
    return nrm(zqk[:, :FOX_W], gq), nrm(zqk[:, FOX_W:], gk)


def _chunk_cumsum(x):
    n = x.shape[0]
    r, c = _iota2((n, n), 0), _iota2((n, n), 1)
    tri = jnp.logical_and(r >= c, (r // CH) == (c // CH)).astype(f32)
    return _sel_lhs(tri, x)


def _f_gdngate(zs, eb, ea, alog_b, dt_b):
    beta = jax.nn.sigmoid(_sel_rhs(zs, eb))
    la = -jnp.exp(alog_b) * jax.nn.softplus(_sel_rhs(zs, ea) + dt_b)
    return beta, _chunk_cumsum(la)


def _f_conv(j, x, w):
    t = x.shape[0]
    y = x * w[3:4, :]
    for jj in range(3):
        sh = 3 - jj
        xs = jnp.concatenate([jnp.zeros((sh, x.shape[1]), f32), x[:t - sh, :]], axis=0)
        y = y + xs * w[jj:jj + 1, :]
    y = jax.nn.silu(y)
    yn = y * lax.rsqrt(jnp.sum(y * y, axis=-1, keepdims=True) + EPS)
    return (jnp.where(j < 2 * GDN_H, yn, y),)


def _head_rms(o, nh):
    outs = []
    for h in range(nh):
        oh = o[:, HD * h:HD * (h + 1)]
        outs.append(oh * lax.rsqrt(jnp.mean(oh * oh, axis=-1, keepdims=True) + EPS))
    return jnp.concatenate(outs, axis=1)


def _f_post0(fox_o, o, gg, on):
    return (jnp.concatenate([fox_o, _head_rms(o, GDN_H) * on * jax.nn.silu(gg)], axis=1),)


def _f_post1(o, zg, on):
    return (_head_rms(o, HG_H) * on * jax.nn.silu(zg),)


def _f_hpre(zqf, lbl):
    lb = jax.nn.sigmoid(lbl[1:2, :] - lbl[0:1, :])
    fg = lb + (1.0 - lb) * jax.nn.sigmoid(zqf[:, D:])
    return jax.nn.silu(zqf[:, :D]), 1.0 - fg, _chunk_cumsum(jnp.log(fg))


def _dotb(a, b, dims=NN):
    return _dot(a.astype(bf16), b.astype(bf16), dims)


def _dot3(a, b):
    ah, bh = a.astype(bf16), b.astype(bf16)
    al, bl = (a - ah.astype(f32)).astype(bf16), (b - bh.astype(f32)).astype(bf16)
    return _dot(ah, bh) + (_dot(ah, bl) + _dot(al, bh))


def _split(t, nh):
    return [t[CH * ck:CH * (ck + 1), HD * h:HD * (h + 1)] for ck in range(t.shape[0] // CH) for h in range(nh)]


def _merge(units, nh):
    return jnp.concatenate([jnp.concatenate(units[i:i + nh], axis=1) for i in range(0, len(units), nh)], axis=0)


def _inv_impl(amats):
    n = amats[0].shape[0]
    eye = jnp.where(_iota2((n, n), 0) == _iota2((n, n), 1), 1.0, 0.0).astype(f32)
    xs, ps = [eye - a for a in amats], list(amats)
    for _ in range(max(1, (n - 1).bit_length()) - 1):
        ps = [_dotb(p, p) for p in ps]
        xs = [x + _dotb(x, p) for x, p in zip(xs, ps)]
    for _ in range(3):
        rs = [eye - x - _dot3(a, x) for a, x in zip(amats, xs)]
        xs = [x + _dotb(x, r) for x, r in zip(xs, rs)]
    return tuple(xs)


@jax.custom_vjp
def _inv_unit_lower(amats):
    return _inv_impl(amats)


def _inv_fwd(amats):
    xs = _inv_impl(amats)
    return xs, xs


def _inv_bwd(xs, dxs):
    return (tuple(-_dotb(_dotb(x, dx, TN), x, NT) for x, dx in zip(xs, dxs)),)


_inv_unit_lower.defvjp(_inv_fwd, _inv_bwd)


@jax.custom_vjp
def _inv_given(amats, xs):
    return xs


def _inv_given_fwd(amats, xs):
    return xs, xs


def _inv_given_bwd(xs, dxs):
    return _inv_bwd(xs, dxs)[0], tuple(jnp.zeros_like(x) for x in xs)


_inv_given.defvjp(_inv_given_fwd, _inv_given_bwd)


def _f_gdn_intra(q, k, v, bb, gb, tinv_p=None):
    qs, ks, vs, bs, gs = (_split(t, GDN_H) for t in (q, k, v, bb, gb))
    r, cc = _iota2((CH, CH), 0), _iota2((CH, CH), 1)
    causal, strict = r >= cc, r > cc
    beta, g, gl = [b[:, :1] for b in bs], [x[:, :1] for x in gs], [x[CH - 1:CH, :1] for x in gs]
    decay = [jnp.exp(jnp.where(causal, x[:, :CH] - x[:, :CH].T, -jnp.inf)) for x in gs]
    kb = [ki * bi for ki, bi in zip(ks, beta)]
    amat = [jnp.where(strict, _dotb(kbi, ki, NT) * di, 0.0) for kbi, ki, di in zip(kb, ks, decay)]
    if tinv_p is None:
        tinv = _inv_unit_lower(tuple(amat))
    else:
        tinv = _inv_given(tuple(amat), tuple(x[:, :CH] for x in _split(tinv_p, GDN_H)))
    rhs = [jnp.concatenate([vi * bi, kbi * jnp.exp(gi)], axis=1) for vi, bi, kbi, gi in zip(vs, beta, kb, g)]
    uw = [_dotb(ti, ri) for ti, ri in zip(tinv, rhs)]
    qsc = [qi * (HD ** -0.5) for qi in qs]
    qk = [jnp.where(causal, _dotb(qi, ki, NT) * di, 0.0) for qi, ki, di in zip(qsc, ks, decay)]
    outs = ([x[:, :HD] for x in uw], [x[:, HD:] for x in uw],
            [jnp.concatenate([x, jnp.zeros_like(x)], axis=1) for x in qk],
            [qi * jnp.exp(gi) for qi, gi in zip(qsc, g)],
            [ki * jnp.exp(gli - gi) for ki, gli, gi in zip(ks, gl, g)],
            [jnp.broadcast_to(gli, (CH, HD)) for gli in gl])
    if tinv_p is None:
        outs += ([jnp.concatenate([x, jnp.zeros_like(x)], axis=1) for x in tinv],)
    return tuple(_merge(o, GDN_H) for o in outs)


def _f_gdn_gated(q, k, v, zs, eb, ea, alog_b, dt_b):
    return _f_gdn_intra(q, k, v, *_f_gdngate(zs, eb, ea, alog_b, dt_b))


def _f_gdn_gated_given(q, k, v, zs, tinv_p, eb, ea, alog_b, dt_b):
    return _f_gdn_intra(q, k, v, *_f_gdngate(zs, eb, ea, alog_b, dt_b), tinv_p=tinv_p)


def _f_gdn_inter(u, w, qkp, qd, kd, glb, st):
    us, ws, qks, qds, kds, gls = (_split(t, GDN_H) for t in (u, w, qkp, qd, kd, glb))
    sts = [st[HD * i:HD * (i + 1), :] for i in range(len(us))]
    vn = [ui - _dotb(wi, si) for ui, wi, si in zip(us, ws, sts)]
    o = [_dotb(qi, si) + _dotb(xi[:, :CH], vi) for qi, si, xi, vi in zip(qds, sts, qks, vn)]
    s2 = [si * jnp.exp(gi[:1, :1]) + _dotb(ki, vi, TN) for si, gi, ki, vi in zip(sts, gls, kds, vn)]
    return _merge(o, GDN_H), jnp.concatenate(s2, axis=0)


def _f_gdn_full(u, w, qkp, qd, kd, glb, fox_o, gg, on, st):
    o, s2 = _f_gdn_inter(u, w, qkp, qd, kd, glb, st)
    return _f_post0(fox_o, o, gg, on)[0], s2


def _f_hgrn_chunk(q, k, b, v, st):
    qs, ks, bs, vs = (_split(t, HG_H) for t in (q, k, b, v))
    sts = [st[HD * i:HD * (i + 1), :] for i in range(len(qs))]
    causal = _iota2((CH, CH), 0) >= _iota2((CH, CH), 1)
    bl, bm = [x[CH - 1:CH, :] for x in bs], [x[CH // 2 - 1:CH // 2, :] for x in bs]
    a = [jnp.where(causal, _dotb(qi * jnp.exp(bi - mi), ki * jnp.exp(mi - bi), NT), 0.0)
         for qi, ki, bi, mi in zip(qs, ks, bs, bm)]
    o = [_dotb(qi * jnp.exp(bi), si, NT) + _dotb(ai, vi) for qi, bi, si, ai, vi in zip(qs, bs, sts, a, vs)]
    s2 = [si * jnp.exp(li) + _dotb(vi, ki * jnp.exp(li - bi), TN) for si, li, vi, ki, bi in zip(sts, bl, vs, ks, bs)]
    return _merge(o, HG_H), jnp.concatenate(s2, axis=0)


def _f_hgrn_full(z, lbl, on, st):
    o, s2 = _f_hgrn_chunk(*_f_hpre(z[:, :2 * D], lbl), z[:, 2 * D:3 * D], st)
    return _f_post1(o, z[:, 3 * D:], on)[0], s2


def _fox_gate_fwd(z0, fbias, t, tc=256):
    n = z0.shape[0]
    nt = t // tc

    def body(zs_ref, b_ref, ccol_ref, crow_ref, carry):
        @pl.when(pl.program_id(1) == 0)
        def _():
            carry[...] = jnp.zeros_like(carry)

        ls = jnp.where(_iota2((tc, LANES), 1) < FOX_H, jax.nn.log_sigmoid(zs_ref[...] + b_ref[...]), 0.0)
        tri = (_iota2((tc, tc), 0) >= _iota2((tc, tc), 1)).astype(f32)
        c = _dot_sel(tri, ls, exact_lhs=True) + carry[...]
        carry[...] = c[tc - 1:tc, :]
        ccol_ref[...] = c
        crow_ref[0] = c.T[:FOX_H, :]

    return pl.pallas_call(
        body, name="fox_gate_fwd", grid=(n // t, nt),
        in_specs=[pl.BlockSpec((tc, LANES), lambda b, i: (b * nt + i, Z0_SMALL)), pl.BlockSpec((1, LANES), lambda b, i: (0, 0))],
        out_specs=[pl.BlockSpec((tc, LANES), lambda b, i: (b * nt + i, 0)), pl.BlockSpec((1, FOX_H, tc), lambda b, i: (b, 0, i))],
        out_shape=[S((n, LANES), f32), S((n // t, FOX_H, t), f32)],
        scratch_shapes=[pltpu.VMEM((1, LANES), f32)],
        compiler_params=pltpu.CompilerParams(dimension_semantics=("parallel", "arbitrary")),
    )(z0, fbias)


def _fox_gate_bwd(z0, fbias, dcq, dck, t, tc=256):
    n = z0.shape[0]
    nt = t // tc

    def body(zs_ref, b_ref, dcq_ref, dck_ref, dz_ref, db_ref, carry):
        first = jnp.logical_and(pl.program_id(0) == 0, pl.program_id(1) == 0)

        @pl.when(pl.program_id(1) == 0)
        def _():
            carry[...] = jnp.zeros_like(carry)

        @pl.when(first)
        def _():
            db_ref[...] = jnp.zeros_like(db_ref)

        dc = dcq_ref[0] + dcq_ref[1] + dcq_ref[2] + dcq_ref[3]
        drow = dck_ref[0, 0] + dck_ref[1, 0] + dck_ref[2, 0] + dck_ref[3, 0]
        eye = (_iota2((FOX_H, LANES), 0) == _iota2((FOX_H, LANES), 1)).astype(f32)
        dc = dc + _dot_sel(drow, eye, TN)
        triu = (_iota2((tc, tc), 0) <= _iota2((tc, tc), 1)).astype(f32)
        dls = _dot_sel(triu, dc, exact_lhs=True) + carry[...]
        carry[...] = dls[0:1, :]
        x = zs_ref[...] + b_ref[...]
        dz = jnp.where(_iota2((tc, LANES), 1) < FOX_H, dls * jax.nn.sigmoid(-x), 0.0)
        dz_ref[...] = dz
        db_ref[...] += jnp.sum(dz, axis=0, keepdims=True)

    def rev(b, i):
        return b * nt + (nt - 1 - i)

    return pl.pallas_call(
        body, name="fox_gate_bwd", grid=(n // t, nt),
        in_specs=[pl.BlockSpec((tc, LANES), lambda b, i: (rev(b, i), Z0_SMALL)), pl.BlockSpec((1, LANES), lambda b, i: (0, 0)),
                  pl.BlockSpec((4, tc, LANES), lambda b, i: (0, rev(b, i), 0)),
                  pl.BlockSpec((4, 1, FOX_H, tc), lambda b, i: (0, b, 0, nt - 1 - i))],
        out_specs=[pl.BlockSpec((tc, LANES), lambda b, i: (rev(b, i), 0)), pl.BlockSpec((1, LANES), lambda b, i: (0, 0))],
        out_shape=[S((n, LANES), f32), S((1, LANES), f32)],
        scratch_shapes=[pltpu.VMEM((1, LANES), f32)],
        compiler_params=pltpu.CompilerParams(dimension_semantics=("arbitrary", "arbitrary")),
    )(z0, fbias, dcq, dck)


def _fox_scores(hh, p, i, tq, q, k, ccol, crow):
    kmax = k.shape[0]
    lane = _iota2((1, LANES), 1)
    mh = (lane // FOX_D) == hh
    h = 2 * p + hh
    qh = jnp.where(mh, q, 0.0).astype(bf16)
    s = _dot(qh, k, NT) * (FOX_D ** -0.5)
    cq = jnp.sum(jnp.where(lane == h, ccol, 0.0), axis=1, keepdims=True)
    ck = jnp.sum(jnp.where(_iota2((FOX_H, 1), 0) == h, crow, 0.0), axis=0, keepdims=True)
    causal = _iota2((1, kmax), 1) <= (i * tq + _iota2((tq, 1), 0))
    s = jnp.where(causal, s + cq - ck, -jnp.inf)
    pe = jnp.exp(s - jnp.max(s, axis=1, keepdims=True))
    return mh, qh, pe, jnp.sum(pe, axis=1, keepdims=True)


def _fox_attn_fwd(qn, kn, z0, ccol, crow, t, tq=256, side=None):
    n = qn.shape[0]
    nq = t // tq

    def body(q_ref, k_ref, v_ref, ccol_ref, crow_ref, o_ref):
        p = pl.program_id(1)
        k, v, crow = k_ref[...].astype(bf16), v_ref[...].astype(bf16), crow_ref[0]
        for i in range(nq):
            rows, kmax = pl.ds(i * tq, tq), (i + 1) * tq
            q, cc = q_ref[rows, :], ccol_ref[rows, :]
            acc = jnp.zeros((tq, LANES), f32)
            for hh in range(2):
                mh, _, pe, l = _fox_scores(hh, p, i, tq, q, k[:kmax], cc, crow[:, :kmax])
                acc = jnp.where(mh, _dot(pe.astype(bf16), v[:kmax]) / l, acc)
            o_ref[rows, :] = acc

    seq = lambda b, p: (b, p)
    return _pcall(
        side, body, name="fox_attn_fwd", grid=(n // t, FOX_H // 2),
        in_specs=[pl.BlockSpec((t, LANES), seq), pl.BlockSpec((t, LANES), seq), pl.BlockSpec((t, LANES), lambda b, p: (b, Z0_FV + p)),
                  pl.BlockSpec((t, LANES), lambda b, p: (b, 0)), pl.BlockSpec((1, FOX_H, t), lambda b, p: (b, 0, 0))],
        out_specs=pl.BlockSpec((t, LANES), seq),
        out_shape=S((n, FOX_W), f32),
        compiler_params=pltpu.CompilerParams(dimension_semantics=("parallel", "parallel")),
    )(qn, kn, z0, ccol, crow)


def _fox_attn_bwd(qn, kn, z0, ccol, crow, do, t, tq=256, side=None):
    n = qn.shape[0]
    nq = t // tq
    nb = n // t

    def body(q_ref, k_ref, v_ref, ccol_ref, crow_ref, do_ref, dq_ref, dk_ref, dv_ref, dcq_ref, dck_ref):
        p = pl.program_id(1)
        dk_ref[...] = jnp.zeros_like(dk_ref)
        dv_ref[...] = jnp.zeros_like(dv_ref)
        dck_ref[...] = jnp.zeros_like(dck_ref)
        kf, v, crow = k_ref[...], v_ref[...].astype(bf16), crow_ref[0]
        k = kf.astype(bf16)
        lane = _iota2((1, LANES), 1)
        sub = _iota2((FOX_H, 1), 0)
        scale = FOX_D ** -0.5
        for i in range(nq):
            rows, kmax = pl.ds(i * tq, tq), (i + 1) * tq
            q, cc, dout = q_ref[rows, :], ccol_ref[rows, :], do_ref[rows, :]
            dq = jnp.zeros((tq, LANES), f32)
            dcq = jnp.zeros((tq, LANES), f32)
            for hh in range(2):
                mh, qh, pe, l = _fox_scores(hh, p, i, tq, q, k[:kmax], cc, crow[:, :kmax])
                pr = pe / l
                doh = jnp.where(mh, dout, 0.0).astype(bf16)
                dp = _dot(doh, v[:kmax], NT)
                ds = pr * (dp - jnp.sum(pr * dp, axis=1, keepdims=True))
                dsb = ds.astype(bf16)
                dq = dq + _dot(dsb, jnp.where(mh, kf[:kmax], 0.0).astype(bf16)) * scale
                dk_ref[:kmax, :] += _dot(dsb, qh, TN) * scale
                dv_ref[:kmax, :] += _dot(pr.astype(bf16), doh, TN)
                h = 2 * p + hh
                dcq = dcq + jnp.where(lane == h, jnp.sum(ds, axis=1, keepdims=True), 0.0)
                dck_ref[0, 0, :, :kmax] += jnp.where(sub == h, -jnp.sum(ds, axis=0, keepdims=True), 0.0)
            dq_ref[rows, :] = dq
            dcq_ref[0, rows, :] = dcq

    seq = lambda b, p: (b, p)
    return _pcall(
        side, body, name="fox_attn_bwd", grid=(nb, FOX_H // 2),
        in_specs=[pl.BlockSpec((t, LANES), seq), pl.BlockSpec((t, LANES), seq), pl.BlockSpec((t, LANES), lambda b, p: (b, Z0_FV + p)),
                  pl.BlockSpec((t, LANES), lambda b, p: (b, 0)), pl.BlockSpec((1, FOX_H, t), lambda b, p: (b, 0, 0)),
                  pl.BlockSpec((t, LANES), seq)],
        out_specs=[pl.BlockSpec((t, LANES), seq), pl.BlockSpec((t, LANES), seq), pl.BlockSpec((t, LANES), seq),
                   pl.BlockSpec((1, t, LANES), lambda b, p: (p, b, 0)), pl.BlockSpec((1, 1, FOX_H, t), lambda b, p: (p, b, 0, 0))],
        out_shape=[S((n, FOX_W), f32), S((n, FOX_W), f32), S((n, FOX_W), f32), S((4, n, LANES), f32), S((4, nb, FOX_H, t), f32)],
        compiler_params=pltpu.CompilerParams(dimension_semantics=("parallel", "parallel")),
    )(qn, kn, z0, ccol, crow, do)


def _adamw_math(w, g, m, v):
    m = ADAM_B1 * m + (1.0 - ADAM_B1) * g
    v = ADAM_B2 * v + (1.0 - ADAM_B2) * (g * g)
    m_hat = m / (1.0 - ADAM_B1 ** ADAM_STEP)
    v_hat = v / (1.0 - ADAM_B2 ** ADAM_STEP)
    return -ADAM_LR * (m_hat / (jnp.sqrt(v_hat) + ADAM_EPS) + ADAM_WD * w), m, v


def _adamw_big(name, idx, gmine, recv1, recv2, w, m, v):
    r, wc = w.shape
    c = gmine.shape[2]
    tr = min(r, 256)

    def body(idx_ref, gm_ref, r1_ref, r2_ref, w_ref, m_ref, v_ref, g_ref, d_ref, nm_ref, nv_ref):
        g = gm_ref[0].astype(f32) + r1_ref[0].astype(f32)
        for k in range(3):
            g = g + r2_ref[k].astype(f32)
        g = g[:, :wc]
        d, nm, nv = _adamw_math(w_ref[...], g, m_ref[...], v_ref[...])
        g_ref[...] = g
        d_ref[...] = d
        nm_ref[...] = nm
        nv_ref[...] = nv

    row = pl.BlockSpec((tr, wc), lambda i, s: (i, 0))
    return pl.pallas_call(
        body, name=name,
        grid_spec=pltpu.PrefetchScalarGridSpec(
            num_scalar_prefetch=1, grid=(r // tr,),
            in_specs=[pl.BlockSpec((1, tr, c), lambda i, s: (s[0], i, 0)), pl.BlockSpec((1, tr, c), lambda i, s: (s[1], i, 0)),
                      pl.BlockSpec((3, tr, c), lambda i, s: (0, i, 0)), row, row, row],
            out_specs=[row, row, row, row]),
        out_shape=[S((r, wc), f32)] * 4,
        compiler_params=pltpu.CompilerParams(dimension_semantics=("parallel",)),
    )(idx, gmine, recv1, recv2, w, m, v)


def _pair_sum(name, idx, gmine, recv1):
    _, r, c = gmine.shape
    g4 = gmine.reshape(4, 2, r, c)

    def body(idx_ref, gm_ref, r1_ref, o_ref):
        o_ref[0] = (gm_ref[0, 0].astype(f32) + r1_ref[0].astype(f32)).astype(bf16)

    return pl.pallas_call(
        body, name=name,
        grid_spec=pltpu.PrefetchScalarGridSpec(
            num_scalar_prefetch=1, grid=(4,),
            in_specs=[pl.BlockSpec((1, 1, r, c), lambda ch, s: (ch, s[0], 0, 0)), pl.BlockSpec((1, r, c), lambda ch, s: (ch, 0, 0))],
            out_specs=pl.BlockSpec((1, r, c), lambda ch, s: (ch, 0, 0))),
        out_shape=S((4, r, c), bf16),
        compiler_params=pltpu.CompilerParams(dimension_semantics=("parallel",)),
    )(idx, g4, recv1)


def _adamw_small(parts, w, m, v, own_mask, own_g):
    def body(p_ref, w_ref, m_ref, v_ref, mask_ref, og_ref, g_ref, d_ref, nm_ref, nv_ref):
        g = p_ref[0]
        for k in range(1, NDEV):
            g = g + p_ref[k]
        g_ref[...] = g
        ge = jnp.where(mask_ref[...] > 0.5, og_ref[...], g)
        d, nm, nv = _adamw_math(w_ref[...], ge, m_ref[...], v_ref[...])
        d_ref[...] = d
        nm_ref[...] = nm
        nv_ref[...] = nv

    return pl.pallas_call(body, name="adamw_small", out_shape=[S(w.shape, f32)] * 4)(parts, w, m, v, own_mask, own_g)


def _sum_parts(parts):
    def body(p_ref, g_ref):
        g = p_ref[0]
        for k in range(1, NDEV):
            g = g + p_ref[k]
        g_ref[...] = g

    return pl.pallas_call(body, name="sum_parts", out_shape=S(parts.shape[1:], f32))(parts)


def _me():
    return lax.axis_index("x"), lax.axis_index("y"), lax.axis_index("c")


def _hbm_specs(n):
    return [pl.BlockSpec(memory_space=pl.ANY)] * n


def _allgather(name, xs):
    na = len(xs)

    def body(*refs):
        x_refs, out_refs = refs[:na], refs[na:2 * na]
        send_sems, recv_sems, local_sems = refs[2 * na:]
        mx, my, mc = _me()
        me, sib = (mx, my, mc), (mx, my, 1 - mc)
        chips = [(1 - mx, my), (mx, 1 - my), (1 - mx, 1 - my)]

        def slab(a, px, py, pc):
            return out_refs[a].at[4 * px + 2 * py + pc]

        def copy(a, k, block, to, own=False):
            return pltpu.make_async_remote_copy(
                src_ref=x_refs[a] if own else slab(a, *block), dst_ref=slab(a, *block),
                send_sem=send_sems.at[7 * a + k], recv_sem=recv_sems.at[7 * a + k], device_id=to, device_id_type=MESH)

        mine = [pltpu.make_async_copy(x_refs[a], slab(a, *me), local_sems.at[a]) for a in range(na)]
        first = []
        for a in range(na):
            mine[a].start()
            first += [copy(a, 0, me, sib, own=True)] + [copy(a, 1 + j, me, (*chip, mc), own=True) for j, chip in enumerate(chips)]
        for cp in first:
            cp.start()
        passed = []
        for j, chip in enumerate(chips):
            for a in range(na):
                copy(a, 1 + j, (*chip, mc), me).wait_recv()
                passed.append(copy(a, 4 + j, (*chip, mc), sib))
                passed[-1].start()
        for a in range(na):
            copy(a, 0, sib, me).wait_recv()
            for j, chip in enumerate(chips):
                copy(a, 4 + j, (*chip, 1 - mc), me).wait_recv()
        for cp in first + passed:
            cp.wait_send()
        for cp in mine:
            cp.wait()

    return pl.pallas_call(
        body, name=name, out_shape=[S((NDEV,) + x.shape, x.dtype) for x in xs],
        in_specs=_hbm_specs(na), out_specs=_hbm_specs(na),
        scratch_shapes=[pltpu.SemaphoreType.DMA((7 * na,)), pltpu.SemaphoreType.DMA((7 * na,)), pltpu.SemaphoreType.DMA((na,))],
    )(*xs)


def _rs_sibling(gs):
    na = len(gs)

    def body(*refs):
        g_refs, out_refs, send_sems, recv_sems = refs[:na], refs[na:2 * na], refs[2 * na], refs[2 * na + 1]
        mx, my, mc = _me()
        cps = [pltpu.make_async_remote_copy(
            src_ref=g_refs[a].at[2 * ch + 1 - mc], dst_ref=out_refs[a].at[ch], send_sem=send_sems.at[4 * a + ch],
            recv_sem=recv_sems.at[4 * a + ch], device_id=(mx, my, 1 - mc), device_id_type=MESH)
            for a in range(na) for ch in range(4)]
        for cp in cps:
            cp.start()
        for cp in cps:
            cp.wait_recv()
        for cp in cps:
            cp.wait_send()

    return pl.pallas_call(
        body, name="rs_sibling", out_shape=[S((4,) + g.shape[1:], g.dtype) for g in gs],
        in_specs=_hbm_specs(na), out_specs=_hbm_specs(na),
        scratch_shapes=[pltpu.SemaphoreType.DMA((4 * na,)), pltpu.SemaphoreType.DMA((4 * na,))],
    )(*gs)


def _side_allgather(xs):
    na = len(xs)

    def mk(x_refs, out_refs, sems):
        send_sems, recv_sems, local_sems = sems
        mx, my, mc = _me()
        me, sib = (mx, my, mc), (mx, my, 1 - mc)
        chips = [(1 - mx, my), (mx, 1 - my), (1 - mx, 1 - my)]

        def slab(a, px, py, pc):
            return out_refs[a].at[4 * px + 2 * py + pc]

        def copy(a, k, block, to, own=False):
            return pltpu.make_async_remote_copy(
                src_ref=x_refs[a] if own else slab(a, *block), dst_ref=slab(a, *block),
                send_sem=send_sems.at[7 * a + k], recv_sem=recv_sems.at[7 * a + k], device_id=to, device_id_type=MESH)

        mine = [pltpu.make_async_copy(x_refs[a], slab(a, *me), local_sems.at[a]) for a in range(na)]
        first = []
        for a in range(na):
            first += [copy(a, 0, me, sib, own=True)] + [copy(a, 1 + j, me, (*chip, mc), own=True) for j, chip in enumerate(chips)]
        return me, sib, chips, mc, copy, mine, first

    def start(x_refs, out_refs, sems):
        *_, mine, first = mk(x_refs, out_refs, sems)
        for cp in mine + first:
            cp.start()

    def finish(x_refs, out_refs, sems):
        me, sib, chips, mc, copy, mine, first = mk(x_refs, out_refs, sems)
        passed = []
        for j, chip in enumerate(chips):
            for a in range(na):
                copy(a, 1 + j, (*chip, mc), me).wait_recv()
                passed.append(copy(a, 4 + j, (*chip, mc), sib))
                passed[-1].start()
        for a in range(na):
            copy(a, 0, sib, me).wait_recv()
            for j, chip in enumerate(chips):
                copy(a, 4 + j, (*chip, 1 - mc), me).wait_recv()
        for cp in first + passed:
            cp.wait_send()
        for cp in mine:
            cp.wait()

    scratch = [pltpu.SemaphoreType.DMA((7 * na,)), pltpu.SemaphoreType.DMA((7 * na,)), pltpu.SemaphoreType.DMA((na,))]
    return _Side(xs, [S((NDEV,) + x.shape, x.dtype) for x in xs], scratch, start, finish)


def _side_exchange(arrs, nslot, out_slots, route):
    na = len(arrs)

    def copies(in_refs, out_refs, sems):
        send_sems, recv_sems = sems
        return [pltpu.make_async_remote_copy(
            src_ref=in_refs[a].at[src], dst_ref=out_refs[a].at[k], send_sem=send_sems.at[nslot * a + k],
            recv_sem=recv_sems.at[nslot * a + k], device_id=to, device_id_type=MESH)
            for a in range(na) for k, (src, to) in enumerate(route(*_me()))]

    def start(in_refs, out_refs, sems):
        for cp in copies(in_refs, out_refs, sems):
            cp.start()

    def finish(in_refs, out_refs, sems):
        cps = copies(in_refs, out_refs, sems)
        for cp in cps:
            cp.wait_recv()
        for cp in cps:
            cp.wait_send()

    scratch = [pltpu.SemaphoreType.DMA((nslot * na,)), pltpu.SemaphoreType.DMA((nslot * na,))]
    return _Side(arrs, [S((out_slots,) + x.shape[1:], x.dtype) for x in arrs], scratch, start, finish)


def _side_rs_sibling(gs):
    return _side_exchange(gs, 4, 4, lambda mx, my, mc: [(2 * ch + 1 - mc, (mx, my, 1 - mc)) for ch in range(4)])


def _side_rs_chips(ps):
    return _side_exchange(ps, 3, 3, lambda mx, my, mc: [(2 * cx + cy, (cx, cy, mc)) for cx, cy in
                                                        [(1 - mx, my), (mx, 1 - my), (1 - mx, 1 - my)]])


_COLW = 512
_COL_NAMES = ("l0_w_in", "l0_w_ff1", "l1_w_in", "l1_w_ff1")
_ROW_NAMES = ("l0_w_out", "l0_w_ff2", "l1_w_out", "l1_w_ff2")
_BIG_NAMES = _COL_NAMES + _ROW_NAMES


_W_IN0_RUNS = ((0, 1536, 0), (1536, 1544, 3584), (1544, 3080, 1536), (3080, 3088, 3592), (3088, 3600, 3072))


def _w_in0_slabs(g, ncols):
    slabs = []
    for d in range(NDEV):
        lo, hi = d * ncols, (d + 1) * ncols
        parts = [g[:, r + max(lo, a) - a:r + min(hi, b) - a] for a, b, r in _W_IN0_RUNS if max(lo, a) < min(hi, b)]
        slabs.append(jnp.concatenate(parts + [jnp.zeros((D, _COLW - ncols), g.dtype)], axis=1))
    return jnp.stack(slabs)


def _w_in0_regrouped(gathered, ncols):
    parts = []
    for a, b, _ in sorted(_W_IN0_RUNS, key=lambda run: run[2]):
        for d in range(a // ncols, (b - 1) // ncols + 1):
            parts.append(gathered[d, :, max(a, d * ncols) - d * ncols:min(b, (d + 1) * ncols) - d * ncols])
    used = sum(p.shape[1] for p in parts)
    return jnp.concatenate(parts + [jnp.zeros((D, ZW0 - used), gathered.dtype)], axis=1)


def _sq(t):
    return t * t


def _epi_res_norm(acc, res, gain):
    y = acc + res
    h = _f_norm(y, gain)[0]
    return y, h, h


_RES_NORM_OUTS = [("tile", f32), ("tile", bf16), ("tile_t", bf16)]


def _epi_norm_bwd(acc, x, dres, gain):
    _, vjp = jax.vjp(lambda xx, gg: _f_norm(xx, gg)[0], x, gain)
    dx, dgain = vjp(acc)
    dx = dx + dres
    return dx, dgain, dx


_NORM_BWD_OUTS = [("tile", f32), ("rows", f32), ("tile_t", bf16)]


def _epi_loss(acc, res, tgt):
    e = acc + res - tgt
    dy = e * (1.0 / D)
    return dy, dy, jnp.sum(e * e, axis=0, keepdims=True)


_LOSS_OUTS = [("tile", f32), ("tile_t", bf16), ("rows", f32)]


def _mlp_fwd(tag, x, h, w1, w2, epi, extra, consts, outs):
    a = _mm(f"{tag}_ff1", h, w1, NN, bf16, epi=lambda acc: jnp.maximum(acc, 0.0))
    return _mm(f"{tag}_ff2", a, w2, NN, f32, a_fn=_sq, epi=epi, extra=(x,) + tuple(extra), consts=consts, outs=outs), a


def _mlp_bwd(tag, x, gain, w1, w2, ht, a, dy, dyt, side=None):
    da = _mm(f"{tag}_ff2_dx", dy, w2, NT, bf16, epi=lambda acc, av: acc * 2.0 * av.astype(f32), extra=(a,), side=side)
    if side is not None:
        da, side_res = da
    dw2 = _mm(f"{tag}_ff2_dw", dyt, a, NN, bf16, b_fn=_sq, out_t=True)
    res = _mm(f"{tag}_ff1_dx", da, w1, NT, f32, epi=_epi_norm_bwd, extra=(x, dy), consts=(gain,), outs=_NORM_BWD_OUTS)
    dw1 = _mm(f"{tag}_ff1_dw", ht, da, NN, bf16, slab=True)
    return (res, dw1, dw2) if side is None else ((res, dw1, dw2), side_res)


def _row(v):
    return v.reshape(1, -1).astype(f32)


_L0_REST = ("l0_w_ff1", "l0_w_out")
_L1_MIX = ("l0_w_ff2", "l1_w_in", "l1_w_out")
_L1_FFN = ("l1_w_ff1", "l1_w_ff2")
_GRAD_A = ("l1_w_ff1", "l1_w_ff2", "l1_w_out", "l1_w_in")
_GRAD_B = ("l0_w_ff1", "l0_w_ff2", "l0_w_out")
_IN_PLACE = ("l0_w_ff1", "l1_w_in", "l1_w_ff1")


def _train(x, tgt, args, mom, var, t):
    n = x.shape[0]
    nchunk = t // CH
    mx, my, mc = _me()
    dev, chip = 4 * mx + 2 * my + mc, 2 * mx + my
    core_idx = jnp.reshape(mc, (1,)).astype(jnp.int32)
    own_idx = jnp.stack([dev, chip]).astype(jnp.int32)
    ncols = {nm: args[nm].shape[1] for nm in _COL_NAMES}
    conv_cols = args["l0_gdn_conv"].shape[1]
    g, big, w = {}, {}, {}

    def send(nm):
        a = args[nm].astype(bf16)
        return jnp.pad(a, ((0, 0), (0, _COLW - ncols[nm]))) if nm in _COL_NAMES else a

    def take(names, gathered):
        for nm, arr in zip(names, gathered):
            w[nm] = arr if nm in _IN_PLACE else arr.reshape(-1, D)

    def by_dev(nm, ga):
        return ga if nm in _COL_NAMES else ga.reshape(NDEV, -1, D)

    def pair(names, gs, r1s):
        return [_pair_sum(f"rs_pair_sum_{nm}", core_idx, ga, r1) for nm, ga, r1 in zip(names, gs, r1s)]

    def adam(names, gs, r1s, r2s):
        for nm, ga, r1, r2 in zip(names, gs, r1s, r2s):
            big[nm] = _adamw_big(f"adamw_{nm}", own_idx, ga, r1, r2, args[nm], mom[nm], var[nm])


    li = jnp.arange(FOX_W)
    pm = jnp.where((li[:, None] // FOX_D) == (li[None, :] // FOX_D), 1.0 / FOX_D, 0.0).astype(f32)
    lane_head = jnp.arange(GDN_W) // HD
    sel = lambda first_lane: (jnp.arange(LANES)[:, None] == (first_lane + lane_head)[None, :]).astype(f32)
    e_beta, e_alpha = sel(FOX_H), sel(FOX_H + GDN_H)
    alog_b, dt_b = _row(jnp.repeat(args["l0_gdn_A_log"], HD)), _row(jnp.repeat(args["l0_gdn_dt_bias"], HD))
    gq_t, gk_t = _row(jnp.tile(args["l0_fox_q_norm"], FOX_H)), _row(jnp.tile(args["l0_fox_k_norm"], FOX_H))
    on0_t, on1_t = _row(jnp.tile(args["l0_gdn_o_norm"], GDN_H)), _row(jnp.tile(args["l1_hgrn_o_norm"], HG_H))
    fbias = jnp.pad(_row(args["l0_fox_f_bias"]), ((0, 0), (0, LANES - FOX_H)))
    g0m, g0f, g1m, g1f = (_row(args[k]) for k in ("l0_mix_norm", "l0_ffn_norm", "l1_mix_norm", "l1_ffn_norm"))
    lbl = args["hgrn_lb_logits"].astype(f32)

    first = [send("l0_w_in"), jnp.pad(args["l0_gdn_conv"], ((0, 4), (0, LANES * 2 - conv_cols)))]
    (h0, h0t), first = _tok_fwd("l0_mix_norm", _f_norm, [(x, D, 0)], [(g0m, None, 0)], [(D, bf16)], TOK_ROWS, also_t=(0,),
                                side=_side_allgather(first))
    w_in0 = _w_in0_regrouped(first[0], ncols["l0_w_in"])
    wconv = first[1][:, :4, :conv_cols].transpose(1, 0, 2).reshape(4, NDEV * conv_cols)
    z0 = _mm("l0_in", h0, w_in0, NN, f32)
    qk_rows = [(z0, 2 * FOX_W, 0)]
    qk_consts = [(gq_t, None, 0), (gk_t, None, 0), (pm, None, 0)]
    qn, kn = _tok_fwd("fox_pre", _f_foxpre, qk_rows, qk_consts, [(FOX_W, f32)] * 2, TOK_ROWS)
    ccol, crow = _fox_gate_fwd(z0, fbias, t)
    fox_o, got = _fox_attn_fwd(qn, kn, z0, ccol, crow, t, side=_side_allgather([send(nm) for nm in _L0_REST]))
    take(_L0_REST, got)
    conv_rows, conv_consts = [(z0, LANES, Z0_GQKV)], [(wconv, LANES, 0)]
    (qkv,) = _tok_fwd("gdn_conv", _f_conv, conv_rows, conv_consts, [(LANES, f32)], t, ncb=12, with_j=True)
    gate_consts = [(e_beta, None, 0), (e_alpha, None, 0), (alog_b, None, 0), (dt_b, None, 0)]
    intra_rows = [(qkv, GDN_W, 0), (qkv, GDN_W, 1), (qkv, GDN_W, 2), (z0, LANES, Z0_SMALL)]
    intra, got = _tok_fwd("gdn_intra", _f_gdn_gated, intra_rows, gate_consts, [(GDN_W, f32)] * 7, 2 * CH,
                          side=_side_allgather([send(nm) for nm in _L1_MIX]))
    take(_L1_MIX, got)
    inter_rows = [(a, 0, 1) for a in intra[:6]]
    full_rows = inter_rows + [(fox_o, 0, 1), (z0, Z0_GG // 4, 1)]
    (cat0, gdn_hist), got = _scan_fwd("gdn_scan", _f_gdn_full, full_rows, GDN_H, nchunk, consts=(on0_t,), out_dtype=bf16, out_k=2,
                                      side=_side_allgather([send(nm) for nm in _L1_FFN[1:]]))
    take(_L1_FFN[1:], got)
    x1, hf0, hf0t = _mm("l0_out", cat0, w["l0_w_out"], NN, f32, epi=_epi_res_norm, extra=(x,), consts=(g0f,), outs=_RES_NORM_OUTS)
    (x2, h1, h1t), a0 = _mlp_fwd("l0", x1, hf0, w["l0_w_ff1"], w["l0_w_ff2"], _epi_res_norm, (), (g1m,), _RES_NORM_OUTS)

    z1 = _mm("l1_in", h1, w["l1_w_in"], NN, f32)
    (cat1, hg_hist), got = _scan_fwd("hgrn_scan", _f_hgrn_full, [(z1, 0, 4)], HG_H, nchunk, consts=(lbl, on1_t), out_dtype=bf16,
                                     side=_side_allgather([send(nm) for nm in _L1_FFN[:1]]))
    take(_L1_FFN[:1], got)
    x3, hf1, hf1t = _mm("l1_out", cat1, w["l1_w_out"], NN, f32, epi=_epi_res_norm, extra=(x2,), consts=(g1f,), outs=_RES_NORM_OUTS)
    (dy, dyt, loss_row), a1 = _mlp_fwd("l1", x3, hf1, w["l1_w_ff1"], w["l1_w_ff2"], _epi_loss, (tgt,), (), _LOSS_OUTS)

    (dx3, g["l1_ffn_norm"], dx3t), ga_ff1, ga_ff2 = _mlp_bwd("l1", x3, g1f, w["l1_w_ff1"], w["l1_w_ff2"], hf1t, a1, dy, dyt)
    dcat1 = _mm("l1_out_dx", dx3, w["l1_w_out"], NT, f32)
    ga_out = _mm("l1_out_dw", dx3t, cat1, NN, bf16, out_t=True)
    dz1, dlbl, don1 = _scan_bwd("hgrn_scan_bwd", _f_hgrn_full, [(z1, 0, 4)], hg_hist, dcat1, HG_H, nchunk, dtypes=[bf16],
                                consts=(lbl, on1_t))
    dx2, g["l1_mix_norm"], dx2t = _mm("l1_in_dx", dz1, w["l1_w_in"], NT, f32, epi=_epi_norm_bwd, extra=(x2, dx3), consts=(g1m,),
                                      outs=_NORM_BWD_OUTS)
    ga_in = _mm("l1_in_dw", h1t, dz1, NN, bf16, slab=True)
    g["l1_hgrn_o_norm"] = don1.reshape(HG_H, HD).sum(0)
    g["hgrn_lb_logits"] = dlbl
    gs_a = [by_dev(nm, ga) for nm, ga in zip(_GRAD_A, (ga_ff1, ga_ff2, ga_out, ga_in))]

    (dx1, g["l0_ffn_norm"], dx1t), gb_ff1, gb_ff2 = _mlp_bwd("l0", x1, g0f, w["l0_w_ff1"], w["l0_w_ff2"], hf0t, a0, dx2, dx2t)
    dcat0 = _mm("l0_out_dx", dx1, w["l0_w_out"], NT, f32)
    gb_out = _mm("l0_out_dw", dx1t, cat0, NN, bf16, out_t=True)
    gs_b = [by_dev(nm, ga) for nm, ga in zip(_GRAD_B, (gb_ff1, gb_ff2, gb_out))]
    (*dinter, dfox_o, dgg, don0), r1_a = _scan_bwd("gdn_scan_bwd", _f_gdn_full, full_rows, gdn_hist, dcat0, GDN_H, nchunk,
                                                   consts=(on0_t,), out_k=2, dtypes=[f32] * 7 + [bf16], cps=1,
                                                   side=_side_rs_sibling(gs_a))
    pairs_a = pair(_GRAD_A, gs_a, r1_a)
    (dqn, dkn, dfv, dcq, dck), got = _fox_attn_bwd(qn, kn, z0, ccol, crow, dfox_o, t,
                                                   side=_join_sides([_side_rs_chips(pairs_a), _side_rs_sibling(gs_b)]))
    r2_a, r1_b = got[:len(_GRAD_A)], got[len(_GRAD_A):]
    adam(_GRAD_A, gs_a, r1_a, r2_a)
    pairs_b = pair(_GRAD_B, gs_b, r1_b)
    (dqkv, dzs_g, dalog_b, ddt_b), r2_b = _tok_bwd(
        "gdn_intra_bwd", _f_gdn_gated_given, intra_rows + [(intra[6], GDN_W, 0)], gate_consts, [(a, GDN_W) for a in dinter],
        2 * CH, [0, 1, 2, 3], [2, 3], ncat=3, side=_side_rs_chips(pairs_b))
    adam(_GRAD_B, gs_b, r1_b, r2_b)
    dgqkv, dwconv = _tok_bwd("gdn_conv_bwd", _f_conv, conv_rows, conv_consts, [(dqkv, LANES)], t, [0], [0], ncb=12, with_j=True,
                             drow_dtype=bf16)
    dzs_f, dfb = _fox_gate_bwd(z0, fbias, dcq, dck, t)
    dzqk, dgq_t, dgk_t = _tok_bwd("fox_pre_bwd", _f_foxpre, qk_rows, qk_consts, [(dqn, FOX_W), (dkn, FOX_W)], TOK_ROWS, [0], [0, 1],
                                  drow_dtype=bf16)
    dz0 = jnp.concatenate([dzqk, dfv.astype(bf16), dgqkv, dgg, (dzs_g + dzs_f).astype(bf16), jnp.zeros((n, ZW0 - 3712), bf16)], axis=1)
    gs_c = [_w_in0_slabs(_mm("l0_in_dw", h0t, dz0, NN, bf16), ncols["l0_w_in"])]
    r1_c = _rs_sibling(gs_c)
    pairs_c = pair(("l0_w_in",), gs_c, r1_c)
    (dx, g["l0_mix_norm"]), r2_c = _mm("l0_in_dx", dz0, w_in0, NT, f32, epi=lambda *a: _epi_norm_bwd(*a)[:2], extra=(x, dx1),
                                       consts=(g0m,), outs=_NORM_BWD_OUTS[:2], side=_side_rs_chips(pairs_c))
    adam(("l0_w_in",), gs_c, r1_c, r2_c)
    g["l0_fox_q_norm"] = dgq_t.reshape(FOX_H, FOX_D).sum(0)
    g["l0_fox_k_norm"] = dgk_t.reshape(FOX_H, FOX_D).sum(0)
    g["l0_fox_f_bias"] = dfb[0, :FOX_H]
    g["l0_gdn_conv"] = dwconv
    g["l0_gdn_A_log"] = dalog_b.reshape(GDN_H, HD).sum(1)
    g["l0_gdn_dt_bias"] = ddt_b.reshape(GDN_H, HD).sum(1)
    g["l0_gdn_o_norm"] = don0.reshape(GDN_H, HD).sum(0)
    return loss_row, dx, g, big


_NAMES = ("l0_mix_norm", "l0_w_in", "l0_fox_q_norm", "l0_fox_k_norm", "l0_fox_f_bias", "l0_gdn_conv", "l0_gdn_A_log",
          "l0_gdn_dt_bias", "l0_gdn_o_norm", "l0_w_out", "l0_ffn_norm", "l0_w_ff1", "l0_w_ff2", "l1_mix_norm", "l1_w_in",
          "l1_hgrn_o_norm", "l1_w_out", "l1_ffn_norm", "l1_w_ff1", "l1_w_ff2", "hgrn_lb_logits")
_SMALL_NAMES = tuple(nm for nm in _NAMES if nm not in _BIG_NAMES)
_SMALL_ROWS = 16


def _pack_small(vals):
    flat = jnp.concatenate([vals[nm].reshape(-1).astype(f32) for nm in _SMALL_NAMES])
    return jnp.pad(flat, (0, _SMALL_ROWS * D - flat.shape[0])).reshape(_SMALL_ROWS, D)


def _unpack_small(packed, shapes):
    flat = packed.reshape(-1)
    out, off = {}, 0
    for nm in _SMALL_NAMES:
        size = 1
        for s in shapes[nm]:
            size *= s
        out[nm] = flat[off:off + size].reshape(shapes[nm])
        off += size
    return out, off


def kernel(x, l0_mix_norm, l0_w_in, l0_fox_q_norm, l0_fox_k_norm, l0_fox_f_bias, l0_gdn_conv, l0_gdn_A_log, l0_gdn_dt_bias, l0_gdn_o_norm, l0_w_out, l0_ffn_norm, l0_w_ff1, l0_w_ff2, l1_mix_norm, l1_w_in, l1_hgrn_o_norm, l1_w_out, l1_ffn_norm, l1_w_ff1, l1_w_ff2, hgrn_lb_logits, loss_target, m_l0_mix_norm, m_l0_w_in, m_l0_fox_q_norm, m_l0_fox_k_norm, m_l0_fox_f_bias, m_l0_gdn_conv, m_l0_gdn_A_log, m_l0_gdn_dt_bias, m_l0_gdn_o_norm, m_l0_w_out, m_l0_ffn_norm, m_l0_w_ff1, m_l0_w_ff2, m_l1_mix_norm, m_l1_w_in, m_l1_hgrn_o_norm, m_l1_w_out, m_l1_ffn_norm, m_l1_w_ff1, m_l1_w_ff2, m_hgrn_lb_logits, v_l0_mix_norm, v_l0_w_in, v_l0_fox_q_norm, v_l0_fox_k_norm, v_l0_fox_f_bias, v_l0_gdn_conv, v_l0_gdn_A_log, v_l0_gdn_dt_bias, v_l0_gdn_o_norm, v_l0_w_out, v_l0_ffn_norm, v_l0_w_ff1, v_l0_w_ff2, v_l1_mix_norm, v_l1_w_in, v_l1_hgrn_o_norm, v_l1_w_out, v_l1_ffn_norm, v_l1_w_ff1, v_l1_w_ff2, v_hgrn_lb_logits):
    args = dict(zip(_NAMES, (l0_mix_norm, l0_w_in, l0_fox_q_norm, l0_fox_k_norm, l0_fox_f_bias, l0_gdn_conv, l0_gdn_A_log, l0_gdn_dt_bias, l0_gdn_o_norm, l0_w_out, l0_ffn_norm, l0_w_ff1, l0_w_ff2, l1_mix_norm, l1_w_in, l1_hgrn_o_norm, l1_w_out, l1_ffn_norm, l1_w_ff1, l1_w_ff2, hgrn_lb_logits)))
    mom = dict(zip(_NAMES, (m_l0_mix_norm, m_l0_w_in, m_l0_fox_q_norm, m_l0_fox_k_norm, m_l0_fox_f_bias, m_l0_gdn_conv, m_l0_gdn_A_log, m_l0_gdn_dt_bias, m_l0_gdn_o_norm, m_l0_w_out, m_l0_ffn_norm, m_l0_w_ff1, m_l0_w_ff2, m_l1_mix_norm, m_l1_w_in, m_l1_hgrn_o_norm, m_l1_w_out, m_l1_ffn_norm, m_l1_w_ff1, m_l1_w_ff2, m_hgrn_lb_logits)))
    var = dict(zip(_NAMES, (v_l0_mix_norm, v_l0_w_in, v_l0_fox_q_norm, v_l0_fox_k_norm, v_l0_fox_f_bias, v_l0_gdn_conv, v_l0_gdn_A_log, v_l0_gdn_dt_bias, v_l0_gdn_o_norm, v_l0_w_out, v_l0_ffn_norm, v_l0_w_ff1, v_l0_w_ff2, v_l1_mix_norm, v_l1_w_in, v_l1_hgrn_o_norm, v_l1_w_out, v_l1_ffn_norm, v_l1_w_ff1, v_l1_w_ff2, v_hgrn_lb_logits)))
    nb, t, _ = x.shape
    dev = 4 * lax.axis_index("x") + 2 * lax.axis_index("y") + lax.axis_index("c")
    conv_cols = l0_gdn_conv.shape[1]
    loss_row, dx, g, big = _train(x.reshape(nb * t, D), loss_target.reshape(nb * t, D), args, mom, var, t)

    shapes = {nm: args[nm].shape for nm in _SMALL_NAMES}
    gsm = dict(g)
    gsm["l0_gdn_conv"] = jnp.zeros(shapes["l0_gdn_conv"], f32)
    packed = _pack_small(gsm)
    _, used = _unpack_small(packed, shapes)
    flat_extra = jnp.concatenate([jnp.sum(loss_row).reshape(1), g["l0_gdn_conv"].reshape(-1)])
    packed = packed.reshape(-1).at[used:used + flat_extra.shape[0]].set(flat_extra).reshape(_SMALL_ROWS, D)
    (parts,) = _allgather("ag_small", [packed])
    total = _sum_parts(parts).reshape(-1)
    loss = 0.5 * total[used] / D
    conv_g_full = total[used + 1:used + 1 + 4 * NDEV * conv_cols].reshape(4, NDEV * conv_cols)
    conv_g = lax.dynamic_slice(conv_g_full, (0, dev * conv_cols), (4, conv_cols))
    own_vals = {nm: jnp.zeros(shapes[nm], f32) for nm in _SMALL_NAMES}
    own_vals["l0_gdn_conv"] = conv_g
    own_mask = {nm: jnp.zeros(shapes[nm], f32) for nm in _SMALL_NAMES}
    own_mask["l0_gdn_conv"] = jnp.ones(shapes["l0_gdn_conv"], f32)
    small = _adamw_small(parts, _pack_small(args), _pack_small(mom), _pack_small(var), _pack_small(own_mask), _pack_small(own_vals))
    small = [_unpack_small(a, shapes)[0] for a in small]
    small[0]["l0_gdn_conv"] = conv_g

    outs = [loss, dx.reshape(nb, t, D)]
    for k in range(4):
        outs += [big[nm][k] if nm in _BIG_NAMES else small[k][nm] for nm in _NAMES]
    return tuple(outs)
```

```python
import functools

import jax
import jax.numpy as jnp
from jax import lax
from jax.experimental import pallas as pl
from jax.experimental.pallas import tpu as pltpu

f32, bf16 = jnp.float32, jnp.bfloat16
NN = (((1,), (0,)), ((), ()))
NT = (((1,), (1,)), ((), ()))
TN = (((0,), (0,)), ((), ()))
MESH = pl.DeviceIdType.MESH
S = jax.ShapeDtypeStruct

EPS = 1e-6
D = 1024
LANES = 128
FOX_H, FOX_D, FOX_W = 8, 64, 512
GDN_H, HD, GDN_W = 4, 128, 512
HG_H = 8
CH = 64
TOK_ROWS = 256
ZW0 = 3840
NDEV = 8
ADAM_LR, ADAM_B1, ADAM_B2, ADAM_EPS, ADAM_WD, ADAM_STEP = 0.001, 0.9, 0.999, 1e-08, 0.01, 10

Z0_FV, Z0_GQKV, Z0_GG, Z0_SMALL = 8, 12, 24, 28


def _dot(a, b, dims=NN):
    return lax.dot_general(a, b, dims, preferred_element_type=f32)


def _iota2(shape, axis):
    return lax.broadcasted_iota(jnp.int32, shape, axis)


def _split3(x):
    x1 = x.astype(bf16)
    r = x - x1.astype(f32)
    x2 = r.astype(bf16)
    return x1, x2, (r - x2.astype(f32)).astype(bf16)


def _dot_sel(a, b, dims=NN, exact_lhs=False):
    if exact_lhs:
        return sum(_dot(a.astype(bf16), piece, dims) for piece in _split3(b))
    return sum(_dot(piece, b.astype(bf16), dims) for piece in _split3(a))


@jax.custom_vjp
def _sel_rhs(a, b):
    return _dot_sel(a, b)


_sel_rhs.defvjp(lambda a, b: (_dot_sel(a, b), b), lambda b, g: (_dot_sel(g, b, NT), jnp.zeros_like(b)))


@jax.custom_vjp
def _sel_lhs(a, x):
    return _dot_sel(a, x, exact_lhs=True)


_sel_lhs.defvjp(lambda a, x: (_dot_sel(a, x, exact_lhs=True), a), lambda a, g: (jnp.zeros_like(a), _dot_sel(a, g, TN, exact_lhs=True)))


class _Side:
    def __init__(self, ins, out_shapes, scratch, start, finish):
        self.ins, self.out_shapes, self.scratch, self.start, self.finish = list(ins), list(out_shapes), list(scratch), start, finish


def _join_sides(sides):
    def split(refs, counts):
        out, off = [], 0
        for c in counts:
            out.append(refs[off:off + c])
            off += c
        return out

    ni, no, ns = ([len(getattr(sd, a)) for sd in sides] for a in ("ins", "out_shapes", "scratch"))

    def run(which):
        def go(ins, outs, sems):
            for sd, i, o, c in zip(sides, split(ins, ni), split(outs, no), split(sems, ns)):
                getattr(sd, which)(i, o, c)
        return go

    return _Side(sum((sd.ins for sd in sides), []), sum((sd.out_shapes for sd in sides), []),
                 sum((sd.scratch for sd in sides), []), run("start"), run("finish"))


def _pcall(side, body, *, name, grid, in_specs, out_specs, out_shape, scratch_shapes=(), compiler_params=None):
    if side is None:
        return pl.pallas_call(body, name=name, grid=grid, in_specs=in_specs, out_specs=out_specs, out_shape=out_shape,
                              scratch_shapes=scratch_shapes, compiler_params=compiler_params)
    single = not isinstance(out_shape, (list, tuple))
    ospecs, oshape = ([out_specs], [out_shape]) if single else (list(out_specs), list(out_shape))
    nin, nout, nscr = len(in_specs), len(ospecs), len(scratch_shapes)
    si, so = len(side.ins), len(side.out_shapes)

    def wrapped(*refs):
        o0 = nin + si
        c0 = o0 + nout + so
        sins, souts, ssems = refs[nin:o0], refs[o0 + nout:c0], refs[c0 + nscr:]
        ids = [pl.program_id(a) for a in range(len(grid))]
        first = functools.reduce(jnp.logical_and, [i == 0 for i in ids])
        last = functools.reduce(jnp.logical_and, [i == g - 1 for i, g in zip(ids, grid)])

        @pl.when(first)
        def _():
            side.start(sins, souts, ssems)

        body(*refs[:nin], *refs[o0:o0 + nout], *refs[c0:c0 + nscr])

        @pl.when(last)
        def _():
            side.finish(sins, souts, ssems)

    call = pl.pallas_call(
        wrapped, name=name, grid=grid, in_specs=list(in_specs) + _hbm_specs(si), out_specs=ospecs + _hbm_specs(so),
        out_shape=oshape + side.out_shapes, scratch_shapes=list(scratch_shapes) + side.scratch,
        compiler_params=pltpu.CompilerParams(dimension_semantics=("arbitrary",) * len(grid),
                                             vmem_limit_bytes=getattr(compiler_params, "vmem_limit_bytes", None)))

    def run(*args):
        res = call(*args, *side.ins)
        return (res[0] if single else list(res[:nout])), list(res[nout:])

    return run


def _tok_specs(rows, consts, tm):
    specs = []
    for (_, w, base) in rows:
        specs.append(pl.BlockSpec((tm, w), functools.partial(lambda j, i, b: (i, b + j), b=base)))
    for (arr, w, base) in consts:
        if w is None:
            specs.append(pl.BlockSpec(arr.shape, lambda j, i: (0, 0)))
        else:
            specs.append(pl.BlockSpec((arr.shape[0], w), functools.partial(lambda j, i, b: (0, b + j), b=base)))
    return specs


def _tok_fwd(name, f, rows, consts, outs, tm, ncb=1, with_j=False, also_t=(), side=None):
    n = rows[0][0].shape[0]
    nin = len(rows) + len(consts)
    nout = len(outs)

    def body(*refs):
        ins = [r[...] for r in refs[:nin]]
        vals = f(pl.program_id(0), *ins) if with_j else f(*ins)
        for r, v in zip(refs[nin:nin + nout], vals):
            r[...] = v.astype(r.dtype)
        for r, k in zip(refs[nin + nout:], also_t):
            r[...] = vals[k].T.astype(r.dtype)

    return _pcall(
        side, body, name=name, grid=(ncb, n // tm),
        in_specs=_tok_specs(rows, consts, tm),
        out_specs=[pl.BlockSpec((tm, w), lambda j, i: (i, j)) for (w, _) in outs]
        + [pl.BlockSpec((outs[k][0], tm), lambda j, i: (0, i)) for k in also_t],
        out_shape=[S((n, w * ncb), dt) for (w, dt) in outs] + [S((outs[k][0], n), bf16) for k in also_t],
        compiler_params=pltpu.CompilerParams(dimension_semantics=("parallel", "parallel")),
    )(*[r[0] for r in rows], *[c[0] for c in consts])


def _tok_bwd(name, f, rows, consts, cots, tm, drow, dconst, ncb=1, with_j=False, addto=None, also_t=(), drow_dtype=f32, side=None, ncat=0):
    n = rows[0][0].shape[0]
    nr, nc, nct = len(rows), len(consts), len(cots)
    addto = addto or {}
    add_keys = sorted(addto)
    nadd = len(add_keys)

    def body(*refs):
        ins = [r[...] for r in refs[:nr + nc]]
        cot = [r[...] for r in refs[nr + nc:nr + nc + nct]]
        adds = refs[nr + nc + nct:nr + nc + nct + nadd]
        outs = refs[nr + nc + nct + nadd:]
        pos = list(drow) + [nr + k for k in dconst]

        def g(*dargs):
            full = list(ins)
            for p, a in zip(pos, dargs):
                full[p] = a
            return tuple(f(pl.program_id(0), *full) if with_j else f(*full))

        vals, vjp = jax.vjp(g, *[ins[p] for p in pos])
        grads = vjp(tuple(c.astype(v.dtype) for c, v in zip(cot, vals)))
        off = 0
        for k in range(len(drow)):
            gk = grads[k]
            if k in addto:
                gk = gk + adds[add_keys.index(k)][...]
            if k < ncat:
                outs[0][:, off:off + gk.shape[1]] = gk.astype(outs[0].dtype)
                off += gk.shape[1]
            else:
                outs[k - skip][...] = gk.astype(outs[k - skip].dtype)
            if k in also_t:
                tref = outs[len(drow) - skip + len(dconst) + list(also_t).index(k)]
                tref[...] = gk.T.astype(tref.dtype)
        first = pl.program_id(1) == 0
        for k in range(len(dconst)):
            ref = outs[len(drow) - skip + k]

            @pl.when(first)
            def _():
                ref[...] = jnp.zeros_like(ref)

            ref[...] += grads[len(drow) + k]

    skip = max(ncat - 1, 0)
    in_specs = _tok_specs(rows, consts, tm)
    in_specs += [pl.BlockSpec((tm, w), lambda j, i: (i, j)) for (_, w) in cots]
    in_specs += [pl.BlockSpec((tm, rows[drow[k]][1]), lambda j, i: (i, j)) for k in add_keys]
    dts = drow_dtype if isinstance(drow_dtype, (list, tuple)) else [drow_dtype] * len(drow)
    widths = [rows[k][1] for k in drow]
    if ncat:
        widths, dts = [sum(widths[:ncat])] + widths[ncat:], [dts[0]] + list(dts[ncat:])
    out_specs = [pl.BlockSpec((tm, wd), lambda j, i: (i, j)) for wd in widths]
    out_shape = [S((n, wd * ncb), dt) for wd, dt in zip(widths, dts)]
    for k in dconst:
        arr, w, _ = consts[k]
        if w is None:
            out_specs.append(pl.BlockSpec(arr.shape, lambda j, i: (0, 0)))
            out_shape.append(S(arr.shape, f32))
        else:
            out_specs.append(pl.BlockSpec((arr.shape[0], w), lambda j, i: (0, j)))
            out_shape.append(S((arr.shape[0], w * ncb), f32))
    for k in also_t:
        out_specs.append(pl.BlockSpec((rows[drow[k]][1], tm), lambda j, i: (0, i)))
        out_shape.append(S((rows[drow[k]][1], n), bf16))
    return _pcall(
        side, body, name=name, grid=(ncb, n // tm), in_specs=in_specs, out_specs=out_specs, out_shape=out_shape,
        compiler_params=pltpu.CompilerParams(dimension_semantics=("parallel", "arbitrary")),
    )(*[r[0] for r in rows], *[c[0] for c in consts], *[c[0] for c in cots], *[addto[k] for k in add_keys])


SCAN_CHUNKS = 2


def _scan_fwd(name, f, rows, nh, nchunk, side=None, consts=(), out_dtype=f32, out_k=1):
    n = rows[0][0].shape[0]
    nb = n // (CH * nchunk)
    nin, nco = len(rows), len(consts)
    w = nh * HD
    cps = SCAN_CHUNKS

    def body(*refs):
        o_ref, hist_ref, st = refs[nin + nco], refs[nin + nco + 1], refs[nin + nco + 2]

        @pl.when(pl.program_id(0) == 0)
        def _():
            st[...] = jnp.zeros_like(st)

        cvals = [r[...] for r in refs[nin:nin + nco]]
        state = st[...]
        for sub in range(cps):
            rr = pl.ds(sub * CH, CH)
            hist_ref[sub] = state.astype(hist_ref.dtype)
            tiles = [r[:, rr, :].reshape(nb * CH, r.shape[2]) for r in refs[:nin]]
            o, state = f(*tiles, *cvals, state)
            o_ref[:, rr, :] = o.reshape(nb, CH, out_k * w).astype(o_ref.dtype)
        st[...] = state

    seq3 = lambda a: a.reshape(nb, nchunk * CH, a.shape[1])
    res = _pcall(
        side, body, name=name, grid=(nchunk // cps,),
        in_specs=[pl.BlockSpec((nb, cps * CH, k * w), functools.partial(lambda c, base: (0, c, base), base=b)) for (_, b, k) in rows]
        + [pl.BlockSpec(c.shape, lambda c: (0, 0)) for c in consts],
        out_specs=[pl.BlockSpec((nb, cps * CH, out_k * w), lambda c: (0, c, 0)), pl.BlockSpec((cps, nb * w, HD), lambda c: (c, 0, 0))],
        out_shape=[S((nb, nchunk * CH, out_k * w), out_dtype), S((nchunk, nb * w, HD), bf16)],
        scratch_shapes=[pltpu.VMEM((nb * w, HD), f32)],
        compiler_params=pltpu.CompilerParams(dimension_semantics=("arbitrary",)),
    )(*[seq3(r[0]) for r in rows], *consts)
    (o, hist), extra = (res, None) if side is None else res
    out = [o.reshape(n, out_k * w), hist]
    return out if side is None else (out, extra)


def _scan_bwd(name, f, rows, hist, do, nh, nchunk, side=None, dtypes=None, consts=(), out_k=1, cps=SCAN_CHUNKS):
    n = rows[0][0].shape[0]
    nb = n // (CH * nchunk)
    nin, nco = len(rows), len(consts)
    w = nh * HD
    nstep = nchunk // cps

    def body(*refs):
        hist_ref, do_ref = refs[nin + nco], refs[nin + nco + 1]
        outs = refs[nin + nco + 2:nin + nco + 2 + nin]
        couts = refs[nin + nco + 2 + nin:nin + nco + 2 + nin + nco]
        ds = refs[nin + nco + 2 + nin + nco]

        @pl.when(pl.program_id(0) == 0)
        def _():
            ds[...] = jnp.zeros_like(ds)
            for c in couts:
                c[...] = jnp.zeros_like(c)

        cvals = [r[...] for r in refs[nin:nin + nco]]
        dstate = ds[...]
        for sub in reversed(range(cps)):
            rr = pl.ds(sub * CH, CH)
            tiles = [r[:, rr, :].reshape(nb * CH, r.shape[2]) for r in refs[:nin]]
            _, vjp = jax.vjp(f, *tiles, *cvals, hist_ref[sub].astype(f32))
            grads = vjp((do_ref[:, rr, :].reshape(nb * CH, out_k * w), dstate))
            for r, gk in zip(outs, grads[:nin]):
                r[:, rr, :] = gk.reshape(nb, CH, r.shape[2]).astype(r.dtype)
            for c, gk in zip(couts, grads[nin:nin + nco]):
                c[...] += gk
            dstate = grads[nin + nco]
        ds[...] = dstate

    seq3 = lambda a: a.reshape(nb, nchunk * CH, a.shape[1])
    rev = lambda c, base: (0, nstep - 1 - c, base)
    res = _pcall(
        side, body, name=name, grid=(nstep,),
        in_specs=[pl.BlockSpec((nb, cps * CH, k * w), functools.partial(rev, base=b)) for (_, b, k) in rows]
        + [pl.BlockSpec(c.shape, lambda c: (0, 0)) for c in consts]
        + [pl.BlockSpec((cps, nb * w, HD), lambda c: (nstep - 1 - c, 0, 0)),
           pl.BlockSpec((nb, cps * CH, out_k * w), functools.partial(rev, base=0))],
        out_specs=[pl.BlockSpec((nb, cps * CH, k * w), functools.partial(rev, base=0)) for (_, _, k) in rows]
        + [pl.BlockSpec(c.shape, lambda c: (0, 0)) for c in consts],
        out_shape=[S((nb, nchunk * CH, k * w), dt) for (_, _, k), dt in zip(rows, dtypes or [f32] * nin)]
        + [S(c.shape, f32) for c in consts],
        scratch_shapes=[pltpu.VMEM((nb * w, HD), f32)],
        compiler_params=pltpu.CompilerParams(dimension_semantics=("arbitrary",)),
    )(*[seq3(r[0]) for r in rows], *consts, hist, seq3(do))
    outs, extra = (res, None) if side is None else res
    outs = [o.reshape(n, o.shape[2]) for o in outs[:nin]] + list(outs[nin:])
    return outs if side is None else (outs, extra)


_VMEM_LIMIT = 56 * 2 ** 20
_VMEM_TILE_BUDGET = 40 * 2 ** 20


def _mm_tiles(m, n, k, sa, sb, so, sx, a_f32, b_f32, tn_fixed):
    best = None
    for tm in (1024, 512, 256, 128, 64):
        for tn in ((tn_fixed,) if tn_fixed else (1024, 768, 512, 384, 256, 128)):
            if m % tm or n % tn:
                continue
            need = 2 * (tm * k * sa + k * tn * sb + tm * tn * (so + sx)) + tm * tn * 4
            need += tm * k * (2 if sa == 4 else 0) + k * tn * (2 if sb == 4 else 0)
            need += tm * k * (4 if a_f32 else 0) + k * tn * (4 if b_f32 else 0)
            if need <= _VMEM_TILE_BUDGET and (best is None or (tm * tn, tm) > best[0]):
                best = ((tm * tn, tm), tm, tn)
    return best[1], best[2]


def _mm(name, a, b, dims, out_dtype, a_fn=None, b_fn=None, epi=None, extra=(), consts=(), outs=None, out_t=False, slab=False,
        side=None):
    m, kk = a.shape
    gathered = b.ndim == 3
    if gathered:
        nn = NDEV * _COLW if dims is NN else b.shape[1]
    else:
        nn = b.shape[1] if dims is NN else b.shape[0]
    kinds = [("tile", out_dtype)] if outs is None else list(outs)
    so = sum(jnp.dtype(dt).itemsize for kd, dt in kinds if kd != "rows")
    sx = sum(e.dtype.itemsize for e in extra)
    full_rows = bool(consts) or any(kd == "rows" for kd, _ in kinds)
    tm, tn = _mm_tiles(m, nn, kk, a.dtype.itemsize, b.dtype.itemsize, so, sx,
                       a_fn is not None, b_fn is not None,
                       _COLW if slab else (nn if full_rows else None))
    gblocks = tn // _COLW if (gathered and dims is NN) else 0
    assert not gblocks or (outs is None and not extra and not out_t and not slab and b_fn is None)
    nex, nco = len(extra), len(consts)

    def body(a_ref, b_ref, *rest):
        av = a_ref[...]
        if a_fn is not None:
            av = a_fn(av.astype(f32))
        av = av.astype(bf16)
        if gblocks:
            for gb in range(gblocks):
                part = _dot(av, b_ref[gb], NN)
                rest[nex + nco][:, _COLW * gb:_COLW * (gb + 1)] = (part if epi is None else epi(part)).astype(rest[nex + nco].dtype)
            return
        if gathered:
            acc = sum(_dot(av[:, _COLW * d:_COLW * (d + 1)], b_ref[d], NT) for d in range(NDEV))
        else:
            bv = b_ref[...]
            if b_fn is not None:
                bv = b_fn(bv.astype(f32))
            acc = _dot(av, bv.astype(bf16), dims)
        if epi is not None:
            acc = epi(acc, *[r[...] for r in rest[:nex + nco]])
        vals = acc if isinstance(acc, tuple) else (acc,)
        for (kd, _), o_ref, val in zip(kinds, rest[nex + nco:], vals):
            if kd == "rows":
                @pl.when(pl.program_id(0) == 0)
                def _():
                    o_ref[...] = jnp.zeros_like(o_ref)

                o_ref[...] += val
            elif kd == "tile_t" or out_t:
                o_ref[...] = val.T.astype(o_ref.dtype)
            elif slab:
                o_ref[0] = val.astype(o_ref.dtype)
            else:
                o_ref[...] = val.astype(o_ref.dtype)

    if gathered:
        bspec = (pl.BlockSpec((gblocks, kk, _COLW), lambda i, j: (j, 0, 0)) if dims is NN
                 else pl.BlockSpec((NDEV, tn, _COLW), lambda i, j: (0, j, 0)))
    else:
        bspec = pl.BlockSpec((kk, tn), lambda i, j: (0, j)) if dims is NN else pl.BlockSpec((tn, kk), lambda i, j: (j, 0))
    out_specs, out_shape = [], []
    for kd, dt in kinds:
        if kd == "rows":
            out_specs.append(pl.BlockSpec((1, nn), lambda i, j: (0, 0)))
            out_shape.append(S((1, nn), dt))
        elif kd == "tile_t" or out_t:
            out_specs.append(pl.BlockSpec((tn, tm), lambda i, j: (j, i)))
            out_shape.append(S((nn, m), dt))
        elif slab:
            out_specs.append(pl.BlockSpec((1, tm, tn), lambda i, j: (j, i, 0)))
            out_shape.append(S((nn // tn, m, tn), dt))
        else:
            out_specs.append(pl.BlockSpec((tm, tn), lambda i, j: (i, j)))
            out_shape.append(S((m, nn), dt))
    if outs is None:
        out_specs, out_shape = out_specs[0], out_shape[0]
    sem = ("arbitrary", "arbitrary") if any(kd == "rows" for kd, _ in kinds) else ("parallel", "parallel")
    return _pcall(
        side, body, name=name, grid=(m // tm, nn // tn),
        in_specs=[pl.BlockSpec((tm, kk), lambda i, j: (i, 0)), bspec]
        + [pl.BlockSpec((tm, tn), lambda i, j: (i, j)) for _ in extra]
        + [pl.BlockSpec(c.shape, lambda i, j: (0, 0)) for c in consts],
        out_specs=out_specs, out_shape=out_shape,
        compiler_params=pltpu.CompilerParams(dimension_semantics=sem, vmem_limit_bytes=_VMEM_LIMIT),
    )(a, b, *extra, *consts)


def _f_norm(x, g):
    return (x * lax.rsqrt(jnp.mean(x * x, axis=-1, keepdims=True) + EPS) * g,)


def _f_foxpre(zqk, gq, gk, pm):
    def nrm(t, g):
        return t * lax.rsqrt(_sel_rhs(t * t, pm) + EPS) * g
    return nrm(zqk[:, :FOX_W], gq), nrm(zqk[:, FOX_W:], gk)


def _chunk_cumsum(x):
    n = x.shape[0]
    r, c = _iota2((n, n), 0), _iota2((n, n), 1)
    tri = jnp.logical_and(r >= c, (r // CH) == (c // CH)).astype(f32)
    return _sel_lhs(tri, x)


def _f_gdngate(zs, eb, ea, alog_b, dt_b):
    beta = jax.nn.sigmoid(_sel_rhs(zs, eb))
    la = -jnp.exp(alog_b) * jax.nn.softplus(_sel_rhs(zs, ea) + dt_b)
    return beta, _chunk_cumsum(la)


def _f_conv(j, x, w):
    t = x.shape[0]
    y = x * w[3:4, :]
    for jj in range(3):
        sh = 3 - jj
        xs = jnp.concatenate([jnp.zeros((sh, x.shape[1]), f32), x[:t - sh, :]], axis=0)
        y = y + xs * w[jj:jj + 1, :]
    y = jax.nn.silu(y)
    yn = y * lax.rsqrt(jnp.sum(y * y, axis=-1, keepdims=True) + EPS)
    return (jnp.where(j < 2 * GDN_H, yn, y),)


def _head_rms(o, nh):
    outs = []
    for h in range(nh):
        oh = o[:, HD * h:HD * (h + 1)]
        outs.append(oh * lax.rsqrt(jnp.mean(oh * oh, axis=-1, keepdims=True) + EPS))
    return jnp.concatenate(outs, axis=1)


def _f_post0(fox_o, o, gg, on):
    return (jnp.concatenate([fox_o, _head_rms(o, GDN_H) * on * jax.nn.silu(gg)], axis=1),)


def _f_post1(o, zg, on):
    return (_head_rms(o, HG_H) * on * jax.nn.silu(zg),)


def _f_hpre(zqf, lbl):
    lb = jax.nn.sigmoid(lbl[1:2, :] - lbl[0:1, :])
    fg = lb + (1.0 - lb) * jax.nn.sigmoid(zqf[:, D:])
    return jax.nn.silu(zqf[:, :D]), 1.0 - fg, _chunk_cumsum(jnp.log(fg))


def _dotb(a, b, dims=NN):
    return _dot(a.astype(bf16), b.astype(bf16), dims)


def _dot3(a, b):
    ah, bh = a.astype(bf16), b.astype(bf16)
    al, bl = (a - ah.astype(f32)).astype(bf16), (b - bh.astype(f32)).astype(bf16)
    return _dot(ah, bh) + (_dot(ah, bl) + _dot(al, bh))


def _split(t, nh):
    return [t[CH * ck:CH * (ck + 1), HD * h:HD * (h + 1)] for ck in range(t.shape[0] // CH) for h in range(nh)]


def _merge(units, nh):
    return jnp.concatenate([jnp.concatenate(units[i:i + nh], axis=1) for i in range(0, len(units), nh)], axis=0)


def _inv_impl(amats):
    n = amats[0].shape[0]
    eye = jnp.where(_iota2((n, n), 0) == _iota2((n, n), 1), 1.0, 0.0).astype(f32)
    xs, ps = [eye - a for a in amats], list(amats)
    for _ in range(max(1, (n - 1).bit_length()) - 1):
        ps = [_dotb(p, p) for p in ps]
        xs = [x + _dotb(x, p) for x, p in zip(xs, ps)]
    for _ in range(3):
        rs = [eye - x - _dot3(a, x) for a, x in zip(amats, xs)]
        xs = [x + _dotb(x, r) for x, r in zip(xs, rs)]
    return tuple(xs)


@jax.custom_vjp
def _inv_unit_lower(amats):
    return _inv_impl(amats)


def _inv_fwd(amats):
    xs = _inv_impl(amats)
    return xs, xs


def _inv_bwd(xs, dxs):
    return (tuple(-_dotb(_dotb(x, dx, TN), x, NT) for x, dx in zip(xs, dxs)),)


_inv_unit_lower.defvjp(_inv_fwd, _inv_bwd)


@jax.custom_vjp
def _inv_given(amats, xs):
    return xs


def _inv_given_fwd(amats, xs):
    return xs, xs


def _inv_given_bwd(xs, dxs):
    return _inv_bwd(xs, dxs)[0], tuple(jnp.zeros_like(x) for x in xs)


_inv_given.defvjp(_inv_given_fwd, _inv_given_bwd)


def _f_gdn_intra(q, k, v, bb, gb, tinv_p=None):
    qs, ks, vs, bs, gs = (_split(t, GDN_H) for t in (q, k, v, bb, gb))
    r, cc = _iota2((CH, CH), 0), _iota2((CH, CH), 1)
    causal, strict = r >= cc, r > cc
    beta, g, gl = [b[:, :1] for b in bs], [x[:, :1] for x in gs], [x[CH - 1:CH, :1] for x in gs]
    decay = [jnp.exp(jnp.where(causal, x[:, :CH] - x[:, :CH].T, -jnp.inf)) for x in gs]
    kb = [ki * bi for ki, bi in zip(ks, beta)]
    amat = [jnp.where(strict, _dotb(kbi, ki, NT) * di, 0.0) for kbi, ki, di in zip(kb, ks, decay)]
    if tinv_p is None:
        tinv = _inv_unit_lower(tuple(amat))
    else:
        tinv = _inv_given(tuple(amat), tuple(x[:, :CH] for x in _split(tinv_p, GDN_H)))
    rhs = [jnp.concatenate([vi * bi, kbi * jnp.exp(gi)], axis=1) for vi, bi, kbi, gi in zip(vs, beta, kb, g)]
    uw = [_dotb(ti, ri) for ti, ri in zip(tinv, rhs)]
    qsc = [qi * (HD ** -0.5) for qi in qs]
    qk = [jnp.where(causal, _dotb(qi, ki, NT) * di, 0.0) for qi, ki, di in zip(qsc, ks, decay)]
    outs = ([x[:, :HD] for x in uw], [x[:, HD:] for x in uw],
            [jnp.concatenate([x, jnp.zeros_like(x)], axis=1) for x in qk],
            [qi * jnp.exp(gi) for qi, gi in zip(qsc, g)],
            [ki * jnp.exp(gli - gi) for ki, gli, gi in zip(ks, gl, g)],
            [jnp.broadcast_to(gli, (CH, HD)) for gli in gl])
    if tinv_p is None:
        outs += ([jnp.concatenate([x, jnp.zeros_like(x)], axis=1) for x in tinv],)
    return tuple(_merge(o, GDN_H) for o in outs)


def _f_gdn_gated(q, k, v, zs, eb, ea, alog_b, dt_b):
    return _f_gdn_intra(q, k, v, *_f_gdngate(zs, eb, ea, alog_b, dt_b))


def _f_gdn_gated_given(q, k, v, zs, tinv_p, eb, ea, alog_b, dt_b):
    return _f_gdn_intra(q, k, v, *_f_gdngate(zs, eb, ea, alog_b, dt_b), tinv_p=tinv_p)


def _f_gdn_inter(u, w, qkp, qd, kd, glb, st):
    us, ws, qks, qds, kds, gls = (_split(t, GDN_H) for t in (u, w, qkp, qd, kd, glb))
    sts = [st[HD * i:HD * (i + 1), :] for i in range(len(us))]
    vn = [ui - _dotb(wi, si) for ui, wi, si in zip(us, ws, sts)]
    o = [_dotb(qi, si) + _dotb(xi[:, :CH], vi) for qi, si, xi, vi in zip(qds, sts, qks, vn)]
    s2 = [si * jnp.exp(gi[:1, :1]) + _dotb(ki, vi, TN) for si, gi, ki, vi in zip(sts, gls, kds, vn)]
    return _merge(o, GDN_H), jnp.concatenate(s2, axis=0)


def _f_gdn_full(u, w, qkp, qd, kd, glb, fox_o, gg, on, st):
    o, s2 = _f_gdn_inter(u, w, qkp, qd, kd, glb, st)
    return _f_post0(fox_o, o, gg, on)[0], s2


def _f_hgrn_chunk(q, k, b, v, st):
    qs, ks, bs, vs = (_split(t, HG_H) for t in (q, k, b, v))
    sts = [st[HD * i:HD * (i + 1), :] for i in range(len(qs))]
    causal = _iota2((CH, CH), 0) >= _iota2((CH, CH), 1)
    bl, bm = [x[CH - 1:CH, :] for x in bs], [x[CH // 2 - 1:CH // 2, :] for x in bs]
    a = [jnp.where(causal, _dotb(qi * jnp.exp(bi - mi), ki * jnp.exp(mi - bi), NT), 0.0)
         for qi, ki, bi, mi in zip(qs, ks, bs, bm)]
    o = [_dotb(qi * jnp.exp(bi), si, NT) + _dotb(ai, vi) for qi, bi, si, ai, vi in zip(qs, bs, sts, a, vs)]
    s2 = [si * jnp.exp(li) + _dotb(vi, ki * jnp.exp(li - bi), TN) for si, li, vi, ki, bi in zip(sts, bl, vs, ks, bs)]
    return _merge(o, HG_H), jnp.concatenate(s2, axis=0)


def _f_hgrn_full(z, lbl, on, st):
    o, s2 = _f_hgrn_chunk(*_f_hpre(z[:, :2 * D], lbl), z[:, 2 * D:3 * D], st)
    return _f_post1(o, z[:, 3 * D:], on)[0], s2


def _fox_gate_fwd(z0, fbias, t, tc=256):
    n = z0.shape[0]
    nt = t // tc

    def body(zs_ref, b_ref, ccol_ref, crow_ref, carry):
        @pl.when(pl.program_id(1) == 0)
        def _():
            carry[...] = jnp.zeros_like(carry)

        ls = jnp.where(_iota2((tc, LANES), 1) < FOX_H, jax.nn.log_sigmoid(zs_ref[...] + b_ref[...]), 0.0)
        tri = (_iota2((tc, tc), 0) >= _iota2((tc, tc), 1)).astype(f32)
        c = _dot_sel(tri, ls, exact_lhs=True) + carry[...]
        carry[...] = c[tc - 1:tc, :]
        ccol_ref[...] = c
        crow_ref[0] = c.T[:FOX_H, :]

    return pl.pallas_call(
        body, name="fox_gate_fwd", grid=(n // t, nt),
        in_specs=[pl.BlockSpec((tc, LANES), lambda b, i: (b * nt + i, Z0_SMALL)), pl.BlockSpec((1, LANES), lambda b, i: (0, 0))],
        out_specs=[pl.BlockSpec((tc, LANES), lambda b, i: (b * nt + i, 0)), pl.BlockSpec((1, FOX_H, tc), lambda b, i: (b, 0, i))],
        out_shape=[S((n, LANES), f32), S((n // t, FOX_H, t), f32)],
        scratch_shapes=[pltpu.VMEM((1, LANES), f32)],
        compiler_params=pltpu.CompilerParams(dimension_semantics=("parallel", "arbitrary")),
    )(z0, fbias)


def _fox_gate_bwd(z0, fbias, dcq, dck, t, tc=256):
    n = z0.shape[0]
    nt = t // tc

    def body(zs_ref, b_ref, dcq_ref, dck_ref, dz_ref, db_ref, carry):
        first = jnp.logical_and(pl.program_id(0) == 0, pl.program_id(1) == 0)

        @pl.when(pl.program_id(1) == 0)
        def _():
            carry[...] = jnp.zeros_like(carry)

        @pl.when(first)
        def _():
            db_ref[...] = jnp.zeros_like(db_ref)

        dc = dcq_ref[0] + dcq_ref[1] + dcq_ref[2] + dcq_ref[3]
        drow = dck_ref[0, 0] + dck_ref[1, 0] + dck_ref[2, 0] + dck_ref[3, 0]
        eye = (_iota2((FOX_H, LANES), 0) == _iota2((FOX_H, LANES), 1)).astype(f32)
        dc = dc + _dot_sel(drow, eye, TN)
        triu = (_iota2((tc, tc), 0) <= _iota2((tc, tc), 1)).astype(f32)
        dls = _dot_sel(triu, dc, exact_lhs=True) + carry[...]
        carry[...] = dls[0:1, :]
        x = zs_ref[...] + b_ref[...]
        dz = jnp.where(_iota2((tc, LANES), 1) < FOX_H, dls * jax.nn.sigmoid(-x), 0.0)
        dz_ref[...] = dz
        db_ref[...] += jnp.sum(dz, axis=0, keepdims=True)

    def rev(b, i):
        return b * nt + (nt - 1 - i)

    return pl.pallas_call(
        body, name="fox_gate_bwd", grid=(n // t, nt),
        in_specs=[pl.BlockSpec((tc, LANES), lambda b, i: (rev(b, i), Z0_SMALL)), pl.BlockSpec((1, LANES), lambda b, i: (0, 0)),
                  pl.BlockSpec((4, tc, LANES), lambda b, i: (0, rev(b, i), 0)),
                  pl.BlockSpec((4, 1, FOX_H, tc), lambda b, i: (0, b, 0, nt - 1 - i))],
        out_specs=[pl.BlockSpec((tc, LANES), lambda b, i: (rev(b, i), 0)), pl.BlockSpec((1, LANES), lambda b, i: (0, 0))],
        out_shape=[S((n, LANES), f32), S((1, LANES), f32)],
        scratch_shapes=[pltpu.VMEM((1, LANES), f32)],
        compiler_params=pltpu.CompilerParams(dimension_semantics=("arbitrary", "arbitrary")),
    )(z0, fbias, dcq, dck)


def _fox_scores(hh, p, i, tq, q, k, ccol, crow):
    kmax = k.shape[0]
    lane = _iota2((1, LANES), 1)
    mh = (lane // FOX_D) == hh
    h = 2 * p + hh
    qh = jnp.where(mh, q, 0.0).astype(bf16)
    s = _dot(qh, k, NT) * (FOX_D ** -0.5)
    cq = jnp.sum(jnp.where(lane == h, ccol, 0.0), axis=1, keepdims=True)
    ck = jnp.sum(jnp.where(_iota2((FOX_H, 1), 0) == h, crow, 0.0), axis=0, keepdims=True)
    causal = _iota2((1, kmax), 1) <= (i * tq + _iota2((tq, 1), 0))
    s = jnp.where(causal, s + cq - ck, -jnp.inf)
    pe = jnp.exp(s - jnp.max(s, axis=1, keepdims=True))
    return mh, qh, pe, jnp.sum(pe, axis=1, keepdims=True)


def _fox_attn_fwd(qn, kn, z0, ccol, crow, t, tq=256, side=None):
    n = qn.shape[0]
    nq = t // tq

    def body(q_ref, k_ref, v_ref, ccol_ref, crow_ref, o_ref):
        p = pl.program_id(1)
        k, v, crow = k_ref[...].astype(bf16), v_ref[...].astype(bf16), crow_ref[0]
        for i in range(nq):
            rows, kmax = pl.ds(i * tq, tq), (i + 1) * tq
            q, cc = q_ref[rows, :], ccol_ref[rows, :]
            acc = jnp.zeros((tq, LANES), f32)
            for hh in range(2):
                mh, _, pe, l = _fox_scores(hh, p, i, tq, q, k[:kmax], cc, crow[:, :kmax])
                acc = jnp.where(mh, _dot(pe.astype(bf16), v[:kmax]) / l, acc)
            o_ref[rows, :] = acc

    seq = lambda b, p: (b, p)
    return _pcall(
        side, body, name="fox_attn_fwd", grid=(n // t, FOX_H // 2),
        in_specs=[pl.BlockSpec((t, LANES), seq), pl.BlockSpec((t, LANES), seq), pl.BlockSpec((t, LANES), lambda b, p: (b, Z0_FV + p)),
                  pl.BlockSpec((t, LANES), lambda b, p: (b, 0)), pl.BlockSpec((1, FOX_H, t), lambda b, p: (b, 0, 0))],
        out_specs=pl.BlockSpec((t, LANES), seq),
        out_shape=S((n, FOX_W), f32),
        compiler_params=pltpu.CompilerParams(dimension_semantics=("parallel", "parallel")),
    )(qn, kn, z0, ccol, crow)


def _fox_attn_bwd(qn, kn, z0, ccol, crow, do, t, tq=256, side=None):
    n = qn.shape[0]
    nq = t // tq
    nb = n // t

    def body(q_ref, k_ref, v_ref, ccol_ref, crow_ref, do_ref, dq_ref, dk_ref, dv_ref, dcq_ref, dck_ref):
        p = pl.program_id(1)
        dk_ref[...] = jnp.zeros_like(dk_ref)
        dv_ref[...] = jnp.zeros_like(dv_ref)
        dck_ref[...] = jnp.zeros_like(dck_ref)
        kf, v, crow = k_ref[...], v_ref[...].astype(bf16), crow_ref[0]
        k = kf.astype(bf16)
        lane = _iota2((1, LANES), 1)
        sub = _iota2((FOX_H, 1), 0)
        scale = FOX_D ** -0.5
        for i in range(nq):
            rows, kmax = pl.ds(i * tq, tq), (i + 1) * tq
            q, cc, dout = q_ref[rows, :], ccol_ref[rows, :], do_ref[rows, :]
            dq = jnp.zeros((tq, LANES), f32)
            dcq = jnp.zeros((tq, LANES), f32)
            for hh in range(2):
                mh, qh, pe, l = _fox_scores(hh, p, i, tq, q, k[:kmax], cc, crow[:, :kmax])
                pr = pe / l
                doh = jnp.where(mh, dout, 0.0).astype(bf16)
                dp = _dot(doh, v[:kmax], NT)
                ds = pr * (dp - jnp.sum(pr * dp, axis=1, keepdims=True))
                dsb = ds.astype(bf16)
                dq = dq + _dot(dsb, jnp.where(mh, kf[:kmax], 0.0).astype(bf16)) * scale
                dk_ref[:kmax, :] += _dot(dsb, qh, TN) * scale
                dv_ref[:kmax, :] += _dot(pr.astype(bf16), doh, TN)
                h = 2 * p + hh
                dcq = dcq + jnp.where(lane == h, jnp.sum(ds, axis=1, keepdims=True), 0.0)
                dck_ref[0, 0, :, :kmax] += jnp.where(sub == h, -jnp.sum(ds, axis=0, keepdims=True), 0.0)
            dq_ref[rows, :] = dq
            dcq_ref[0, rows, :] = dcq

    seq = lambda b, p: (b, p)
    return _pcall(
        side, body, name="fox_attn_bwd", grid=(nb, FOX_H // 2),
        in_specs=[pl.BlockSpec((t, LANES), seq), pl.BlockSpec((t, LANES), seq), pl.BlockSpec((t, LANES), lambda b, p: (b, Z0_FV + p)),
                  pl.BlockSpec((t, LANES), lambda b, p: (b, 0)), pl.BlockSpec((1, FOX_H, t), lambda b, p: (b, 0, 0)),
                  pl.BlockSpec((t, LANES), seq)],
        out_specs=[pl.BlockSpec((t, LANES), seq), pl.BlockSpec((t, LANES), seq), pl.BlockSpec((t, LANES), seq),
                   pl.BlockSpec((1, t, LANES), lambda b, p: (p, b, 0)), pl.BlockSpec((1, 1, FOX_H, t), lambda b, p: (p, b, 0, 0))],
        out_shape=[S((n, FOX_W), f32), S((n, FOX_W), f32), S((n, FOX_W), f32), S((4, n, LANES), f32), S((4, nb, FOX_H, t), f32)],
        compiler_params=pltpu.CompilerParams(dimension_semantics=("parallel", "parallel")),
    )(qn, kn, z0, ccol, crow, do)


def _adamw_math(w, g, m, v):
    m = ADAM_B1 * m + (1.0 - ADAM_B1) * g
    v = ADAM_B2 * v + (1.0 - ADAM_B2) * (g * g)
    m_hat = m / (1.0 - ADAM_B1 ** ADAM_STEP)
    v_hat = v / (1.0 - ADAM_B2 ** ADAM_STEP)
    return -ADAM_LR * (m_hat / (jnp.sqrt(v_hat) + ADAM_EPS) + ADAM_WD * w), m, v


_ADAMW_STEPS = 4


def _adamw_big(name, idx, gs, r1s, r2s, ws, ms, vs):
    na = len(ws)

    def body(idx_ref, *refs):
        for k in range(na):
            gm_ref, r1_ref, r2_ref, w_ref, m_ref, v_ref = refs[6 * k:6 * k + 6]
            g_ref, d_ref, nm_ref, nv_ref = refs[6 * na + 4 * k:6 * na + 4 * k + 4]
            g = gm_ref[0].astype(f32) + r1_ref[0].astype(f32)
            for j in range(3):
                g = g + r2_ref[j].astype(f32)
            g = g[:, :w_ref.shape[1]]
            d, nm, nv = _adamw_math(w_ref[...], g, m_ref[...], v_ref[...])
            g_ref[...] = g
            d_ref[...] = d
            nm_ref[...] = nm
            nv_ref[...] = nv

    in_specs, out_specs, out_shape, operands = [], [], [], []
    for ga, r1, r2, w, m, v in zip(gs, r1s, r2s, ws, ms, vs):
        r, wc = w.shape
        c = ga.shape[2]
        tr = r // _ADAMW_STEPS
        row = pl.BlockSpec((tr, wc), lambda i, s: (i, 0))
        in_specs += [pl.BlockSpec((1, tr, c), lambda i, s: (s[0], i, 0)), pl.BlockSpec((1, tr, c), lambda i, s: (s[1], i, 0)),
                     pl.BlockSpec((3, tr, c), lambda i, s: (0, i, 0)), row, row, row]
        out_specs += [row] * 4
        out_shape += [S((r, wc), f32)] * 4
        operands += [ga, r1, r2, w, m, v]
    res = pl.pallas_call(
        body, name=name,
        grid_spec=pltpu.PrefetchScalarGridSpec(num_scalar_prefetch=1, grid=(_ADAMW_STEPS,), in_specs=in_specs, out_specs=out_specs),
        out_shape=out_shape,
        compiler_params=pltpu.CompilerParams(dimension_semantics=("parallel",), vmem_limit_bytes=_VMEM_LIMIT),
    )(idx, *operands)
    return [res[4 * k:4 * k + 4] for k in range(na)]


def _pair_sum(name, idx, gs, r1s):
    na = len(gs)

    def body(idx_ref, *refs):
        for gm_ref, r1_ref, o_ref in zip(refs[:na], refs[na:2 * na], refs[2 * na:]):
            o_ref[0] = (gm_ref[0, 0].astype(f32) + r1_ref[0].astype(f32)).astype(bf16)

    shapes = [ga.shape[1:] for ga in gs]
    res = pl.pallas_call(
        body, name=name,
        grid_spec=pltpu.PrefetchScalarGridSpec(
            num_scalar_prefetch=1, grid=(4,),
            in_specs=[pl.BlockSpec((1, 1, r, c), lambda ch, s: (ch, s[0], 0, 0)) for r, c in shapes]
            + [pl.BlockSpec((1, r, c), lambda ch, s: (ch, 0, 0)) for r, c in shapes],
            out_specs=[pl.BlockSpec((1, r, c), lambda ch, s: (ch, 0, 0)) for r, c in shapes]),
        out_shape=[S((4, r, c), bf16) for r, c in shapes],
        compiler_params=pltpu.CompilerParams(dimension_semantics=("parallel",), vmem_limit_bytes=_VMEM_LIMIT),
    )(idx, *[ga.reshape(4, 2, *ga.shape[1:]) for ga in gs], *r1s)
    return list(res)


def _adamw_small(parts, w, m, v, own_mask, own_g):
    def body(p_ref, w_ref, m_ref, v_ref, mask_ref, og_ref, g_ref, d_ref, nm_ref, nv_ref):
        g = p_ref[0]
        for k in range(1, NDEV):
            g = g + p_ref[k]
        g_ref[...] = g
        ge = jnp.where(mask_ref[...] > 0.5, og_ref[...], g)
        d, nm, nv = _adamw_math(w_ref[...], ge, m_ref[...], v_ref[...])
        d_ref[...] = d
        nm_ref[...] = nm
        nv_ref[...] = nv

    return pl.pallas_call(body, name="adamw_small", out_shape=[S(w.shape, f32)] * 4)(parts, w, m, v, own_mask, own_g)


def _sum_parts(parts):
    def body(p_ref, g_ref):
        g = p_ref[0]
        for k in range(1, NDEV):
            g = g + p_ref[k]
        g_ref[...] = g

    return pl.pallas_call(body, name="sum_parts", out_shape=S(parts.shape[1:], f32))(parts)


def _me():
    return lax.axis_index("x"), lax.axis_index("y"), lax.axis_index("c")


def _hbm_specs(n):
    return [pl.BlockSpec(memory_space=pl.ANY)] * n


def _allgather(name, xs):
    na = len(xs)

    def body(*refs):
        x_refs, out_refs = refs[:na], refs[na:2 * na]
        send_sems, recv_sems, local_sems = refs[2 * na:]
        mx, my, mc = _me()
        me, sib = (mx, my, mc), (mx, my, 1 - mc)
        chips = [(1 - mx, my), (mx, 1 - my), (1 - mx, 1 - my)]

        def slab(a, px, py, pc):
            return out_refs[a].at[4 * px + 2 * py + pc]

        def copy(a, k, block, to, own=False):
            return pltpu.make_async_remote_copy(
                src_ref=x_refs[a] if own else slab(a, *block), dst_ref=slab(a, *block),
                send_sem=send_sems.at[7 * a + k], recv_sem=recv_sems.at[7 * a + k], device_id=to, device_id_type=MESH)

        mine = [pltpu.make_async_copy(x_refs[a], slab(a, *me), local_sems.at[a]) for a in range(na)]
        first = []
        for a in range(na):
            mine[a].start()
            first += [copy(a, 0, me, sib, own=True)] + [copy(a, 1 + j, me, (*chip, mc), own=True) for j, chip in enumerate(chips)]
        for cp in first:
            cp.start()
        passed = []
        for j, chip in enumerate(chips):
            for a in range(na):
                copy(a, 1 + j, (*chip, mc), me).wait_recv()
                passed.append(copy(a, 4 + j, (*chip, mc), sib))
                passed[-1].start()
        for a in range(na):
            copy(a, 0, sib, me).wait_recv()
            for j, chip in enumerate(chips):
                copy(a, 4 + j, (*chip, 1 - mc), me).wait_recv()
        for cp in first + passed:
            cp.wait_send()
        for cp in mine:
            cp.wait()

    return pl.pallas_call(
        body, name=name, out_shape=[S((NDEV,) + x.shape, x.dtype) for x in xs],
        in_specs=_hbm_specs(na), out_specs=_hbm_specs(na),
        scratch_shapes=[pltpu.SemaphoreType.DMA((7 * na,)), pltpu.SemaphoreType.DMA((7 * na,)), pltpu.SemaphoreType.DMA((na,))],
    )(*xs)


def _rs_sibling(gs):
    na = len(gs)

    def body(*refs):
        g_refs, out_refs, send_sems, recv_sems = refs[:na], refs[na:2 * na], refs[2 * na], refs[2 * na + 1]
        mx, my, mc = _me()
        cps = [pltpu.make_async_remote_copy(
            src_ref=g_refs[a].at[2 * ch + 1 - mc], dst_ref=out_refs[a].at[ch], send_sem=send_sems.at[4 * a + ch],
            recv_sem=recv_sems.at[4 * a + ch], device_id=(mx, my, 1 - mc), device_id_type=MESH)
            for a in range(na) for ch in range(4)]
        for cp in cps:
            cp.start()
        for cp in cps:
            cp.wait_recv()
        for cp in cps:
            cp.wait_send()

    return pl.pallas_call(
        body, name="rs_sibling", out_shape=[S((4,) + g.shape[1:], g.dtype) for g in gs],
        in_specs=_hbm_specs(na), out_specs=_hbm_specs(na),
        scratch_shapes=[pltpu.SemaphoreType.DMA((4 * na,)), pltpu.SemaphoreType.DMA((4 * na,))],
    )(*gs)


def _side_allgather(xs):
    na = len(xs)

    def mk(x_refs, out_refs, sems):
        send_sems, recv_sems, local_sems = sems
        mx, my, mc = _me()
        me, sib = (mx, my, mc), (mx, my, 1 - mc)
        chips = [(1 - mx, my), (mx, 1 - my), (1 - mx, 1 - my)]

        def slab(a, px, py, pc):
            return out_refs[a].at[4 * px + 2 * py + pc]

        def copy(a, k, block, to, own=False):
            return pltpu.make_async_remote_copy(
                src_ref=x_refs[a] if own else slab(a, *block), dst_ref=slab(a, *block),
                send_sem=send_sems.at[7 * a + k], recv_sem=recv_sems.at[7 * a + k], device_id=to, device_id_type=MESH)

        mine = [pltpu.make_async_copy(x_refs[a], slab(a, *me), local_sems.at[a]) for a in range(na)]
        first = []
        for a in range(na):
            first += [copy(a, 0, me, sib, own=True)] + [copy(a, 1 + j, me, (*chip, mc), own=True) for j, chip in enumerate(chips)]
        return me, sib, chips, mc, copy, mine, first

    def start(x_refs, out_refs, sems):
        *_, mine, first = mk(x_refs, out_refs, sems)
        for cp in mine + first:
            cp.start()

    def finish(x_refs, out_refs, sems):
        me, sib, chips, mc, copy, mine, first = mk(x_refs, out_refs, sems)
        passed = []
        for j, chip in enumerate(chips):
            for a in range(na):
                copy(a, 1 + j, (*chip, mc), me).wait_recv()
                passed.append(copy(a, 4 + j, (*chip, mc), sib))
                passed[-1].start()
        for a in range(na):
            copy(a, 0, sib, me).wait_recv()
            for j, chip in enumerate(chips):
                copy(a, 4 + j, (*chip, 1 - mc), me).wait_recv()
        for cp in first + passed:
            cp.wait_send()
        for cp in mine:
            cp.wait()

    scratch = [pltpu.SemaphoreType.DMA((7 * na,)), pltpu.SemaphoreType.DMA((7 * na,)), pltpu.SemaphoreType.DMA((na,))]
    return _Side(xs, [S((NDEV,) + x.shape, x.dtype) for x in xs], scratch, start, finish)


def _side_exchange(arrs, nslot, out_slots, route):
    na = len(arrs)

    def copies(in_refs, out_refs, sems):
        send_sems, recv_sems = sems
        return [pltpu.make_async_remote_copy(
            src_ref=in_refs[a].at[src], dst_ref=out_refs[a].at[k], send_sem=send_sems.at[nslot * a + k],
            recv_sem=recv_sems.at[nslot * a + k], device_id=to, device_id_type=MESH)
            for a in range(na) for k, (src, to) in enumerate(route(*_me()))]

    def start(in_refs, out_refs, sems):
        for cp in copies(in_refs, out_refs, sems):
            cp.start()

    def finish(in_refs, out_refs, sems):
        cps = copies(in_refs, out_refs, sems)
        for cp in cps:
            cp.wait_recv()
        for cp in cps:
            cp.wait_send()

    scratch = [pltpu.SemaphoreType.DMA((nslot * na,)), pltpu.SemaphoreType.DMA((nslot * na,))]
    return _Side(arrs, [S((out_slots,) + x.shape[1:], x.dtype) for x in arrs], scratch, start, finish)


def _side_rs_sibling(gs):
    return _side_exchange(gs, 4, 4, lambda mx, my, mc: [(2 * ch + 1 - mc, (mx, my, 1 - mc)) for ch in range(4)])


def _side_rs_chips(ps):
    return _side_exchange(ps, 3, 3, lambda mx, my, mc: [(2 * cx + cy, (cx, cy, mc)) for cx, cy in
                                                        [(1 - mx, my), (mx, 1 - my), (1 - mx, 1 - my)]])


_COLW = 512
_COL_NAMES = ("l0_w_in", "l0_w_ff1", "l1_w_in", "l1_w_ff1")
_ROW_NAMES = ("l0_w_out", "l0_w_ff2", "l1_w_out", "l1_w_ff2")
_BIG_NAMES = _COL_NAMES + _ROW_NAMES


_W_IN0_RUNS = ((0, 1536, 0), (1536, 1544, 3584), (1544, 3080, 1536), (3080, 3088, 3592), (3088, 3600, 3072))


def _w_in0_slabs(g, ncols):
    slabs = []
    for d in range(NDEV):
        lo, hi = d * ncols, (d + 1) * ncols
        parts = [g[:, r + max(lo, a) - a:r + min(hi, b) - a] for a, b, r in _W_IN0_RUNS if max(lo, a) < min(hi, b)]
        slabs.append(jnp.concatenate(parts + [jnp.zeros((D, _COLW - ncols), g.dtype)], axis=1))
    return jnp.stack(slabs)


def _w_in0_regrouped(gathered, ncols):
    parts = []
    for a, b, _ in sorted(_W_IN0_RUNS, key=lambda run: run[2]):
        for d in range(a // ncols, (b - 1) // ncols + 1):
            parts.append(gathered[d, :, max(a, d * ncols) - d * ncols:min(b, (d + 1) * ncols) - d * ncols])
    used = sum(p.shape[1] for p in parts)
    return jnp.concatenate(parts + [jnp.zeros((D, ZW0 - used), gathered.dtype)], axis=1)


def _sq(t):
    return t * t


def _epi_res_norm(acc, res, gain):
    y = acc + res
    h = _f_norm(y, gain)[0]
    return y, h, h


_RES_NORM_OUTS = [("tile", f32), ("tile", bf16), ("tile_t", bf16)]


def _epi_norm_bwd(acc, x, dres, gain):
    _, vjp = jax.vjp(lambda xx, gg: _f_norm(xx, gg)[0], x, gain)
    dx, dgain = vjp(acc)
    dx = dx + dres
    return dx, dgain, dx


_NORM_BWD_OUTS = [("tile", f32), ("rows", f32), ("tile_t", bf16)]


def _epi_loss(acc, res, tgt):
    e = acc + res - tgt
    dy = e * (1.0 / D)
    return dy, dy, jnp.sum(e * e, axis=0, keepdims=True)


_LOSS_OUTS = [("tile", f32), ("tile_t", bf16), ("rows", f32)]


def _mlp_fwd(tag, x, h, w1, w2, epi, extra, consts, outs):
    a = _mm(f"{tag}_ff1", h, w1, NN, bf16, epi=lambda acc: jnp.maximum(acc, 0.0))
    return _mm(f"{tag}_ff2", a, w2, NN, f32, a_fn=_sq, epi=epi, extra=(x,) + tuple(extra), consts=consts, outs=outs), a


def _mlp_bwd(tag, x, gain, w1, w2, ht, a, dy, dyt, side=None):
    da = _mm(f"{tag}_ff2_dx", dy, w2, NT, bf16, epi=lambda acc, av: acc * 2.0 * av.astype(f32), extra=(a,), side=side)
    if side is not None:
        da, side_res = da
    dw2 = _mm(f"{tag}_ff2_dw", dyt, a, NN, bf16, b_fn=_sq, out_t=True)
    res = _mm(f"{tag}_ff1_dx", da, w1, NT, f32, epi=_epi_norm_bwd, extra=(x, dy), consts=(gain,), outs=_NORM_BWD_OUTS)
    dw1 = _mm(f"{tag}_ff1_dw", ht, da, NN, bf16, slab=True)
    return (res, dw1, dw2) if side is None else ((res, dw1, dw2), side_res)


def _row(v):
    return v.reshape(1, -1).astype(f32)


_L0_REST = ("l0_w_ff1", "l0_w_out")
_L1_MIX = ("l0_w_ff2", "l1_w_in", "l1_w_out")
_L1_FFN = ("l1_w_ff1", "l1_w_ff2")
_GRAD_A = ("l1_w_ff1", "l1_w_ff2", "l1_w_out", "l1_w_in")
_GRAD_B = ("l0_w_ff1", "l0_w_ff2", "l0_w_out")
_IN_PLACE = ("l0_w_ff1", "l1_w_in", "l1_w_ff1")


def _train(x, tgt, args, mom, var, t):
    n = x.shape[0]
    nchunk = t // CH
    mx, my, mc = _me()
    dev, chip = 4 * mx + 2 * my + mc, 2 * mx + my
    core_idx = jnp.reshape(mc, (1,)).astype(jnp.int32)
    own_idx = jnp.stack([dev, chip]).astype(jnp.int32)
    ncols = {nm: args[nm].shape[1] for nm in _COL_NAMES}
    conv_cols = args["l0_gdn_conv"].shape[1]
    g, big, w = {}, {}, {}

    def send(nm):
        a = args[nm].astype(bf16)
        return jnp.pad(a, ((0, 0), (0, _COLW - ncols[nm]))) if nm in _COL_NAMES else a

    def take(names, gathered):
        for nm, arr in zip(names, gathered):
            w[nm] = arr if nm in _IN_PLACE else arr.reshape(-1, D)

    def by_dev(nm, ga):
        return ga if nm in _COL_NAMES else ga.reshape(NDEV, -1, D)

    def pair(names, gs, r1s):
        return _pair_sum(f"rs_pair_sum_{names[0]}_group", core_idx, gs, r1s)

    def adam(names, gs, r1s, r2s):
        res = _adamw_big(f"adamw_{names[0]}_group", own_idx, gs, r1s, r2s, *[[src[nm] for nm in names] for src in (args, mom, var)])
        big.update(zip(names, res))


    li = jnp.arange(FOX_W)
    pm = jnp.where((li[:, None] // FOX_D) == (li[None, :] // FOX_D), 1.0 / FOX_D, 0.0).astype(f32)
    lane_head = jnp.arange(GDN_W) // HD
    sel = lambda first_lane: (jnp.arange(LANES)[:, None] == (first_lane + lane_head)[None, :]).astype(f32)
    e_beta, e_alpha = sel(FOX_H), sel(FOX_H + GDN_H)
    alog_b, dt_b = _row(jnp.repeat(args["l0_gdn_A_log"], HD)), _row(jnp.repeat(args["l0_gdn_dt_bias"], HD))
    gq_t, gk_t = _row(jnp.tile(args["l0_fox_q_norm"], FOX_H)), _row(jnp.tile(args["l0_fox_k_norm"], FOX_H))
    on0_t, on1_t = _row(jnp.tile(args["l0_gdn_o_norm"], GDN_H)), _row(jnp.tile(args["l1_hgrn_o_norm"], HG_H))
    fbias = jnp.pad(_row(args["l0_fox_f_bias"]), ((0, 0), (0, LANES - FOX_H)))
    g0m, g0f, g1m, g1f = (_row(args[k]) for k in ("l0_mix_norm", "l0_ffn_norm", "l1_mix_norm", "l1_ffn_norm"))
    lbl = args["hgrn_lb_logits"].astype(f32)

    first = [send("l0_w_in"), jnp.pad(args["l0_gdn_conv"], ((0, 4), (0, LANES * 2 - conv_cols)))]
    (h0, h0t), first = _tok_fwd("l0_mix_norm", _f_norm, [(x, D, 0)], [(g0m, None, 0)], [(D, bf16)], TOK_ROWS, also_t=(0,),
                                side=_side_allgather(first))
    w_in0 = _w_in0_regrouped(first[0], ncols["l0_w_in"])
    wconv = first[1][:, :4, :conv_cols].transpose(1, 0, 2).reshape(4, NDEV * conv_cols)
    z0 = _mm("l0_in", h0, w_in0, NN, f32)
    qk_rows = [(z0, 2 * FOX_W, 0)]
    qk_consts = [(gq_t, None, 0), (gk_t, None, 0), (pm, None, 0)]
    qn, kn = _tok_fwd("fox_pre", _f_foxpre, qk_rows, qk_consts, [(FOX_W, f32)] * 2, TOK_ROWS)
    ccol, crow = _fox_gate_fwd(z0, fbias, t)
    fox_o, got = _fox_attn_fwd(qn, kn, z0, ccol, crow, t, side=_side_allgather([send(nm) for nm in _L0_REST]))
    take(_L0_REST, got)
    conv_rows, conv_consts = [(z0, LANES, Z0_GQKV)], [(wconv, LANES, 0)]
    (qkv,) = _tok_fwd("gdn_conv", _f_conv, conv_rows, conv_consts, [(LANES, f32)], t, ncb=12, with_j=True)
    gate_consts = [(e_beta, None, 0), (e_alpha, None, 0), (alog_b, None, 0), (dt_b, None, 0)]
    intra_rows = [(qkv, GDN_W, 0), (qkv, GDN_W, 1), (qkv, GDN_W, 2), (z0, LANES, Z0_SMALL)]
    intra, got = _tok_fwd("gdn_intra", _f_gdn_gated, intra_rows, gate_consts, [(GDN_W, f32)] * 7, 2 * CH,
                          side=_side_allgather([send(nm) for nm in _L1_MIX]))
    take(_L1_MIX, got)
    inter_rows = [(a, 0, 1) for a in intra[:6]]
    full_rows = inter_rows + [(fox_o, 0, 1), (z0, Z0_GG // 4, 1)]
    (cat0, gdn_hist), got = _scan_fwd("gdn_scan", _f_gdn_full, full_rows, GDN_H, nchunk, consts=(on0_t,), out_dtype=bf16, out_k=2,
                                      side=_side_allgather([send(nm) for nm in _L1_FFN[1:]]))
    take(_L1_FFN[1:], got)
    x1, hf0, hf0t = _mm("l0_out", cat0, w["l0_w_out"], NN, f32, epi=_epi_res_norm, extra=(x,), consts=(g0f,), outs=_RES_NORM_OUTS)
    (x2, h1, h1t), a0 = _mlp_fwd("l0", x1, hf0, w["l0_w_ff1"], w["l0_w_ff2"], _epi_res_norm, (), (g1m,), _RES_NORM_OUTS)

    z1 = _mm("l1_in", h1, w["l1_w_in"], NN, f32)
    (cat1, hg_hist), got = _scan_fwd("hgrn_scan", _f_hgrn_full, [(z1, 0, 4)], HG_H, nchunk, consts=(lbl, on1_t), out_dtype=bf16,
                                     side=_side_allgather([send(nm) for nm in _L1_FFN[:1]]))
    take(_L1_FFN[:1], got)
    x3, hf1, hf1t = _mm("l1_out", cat1, w["l1_w_out"], NN, f32, epi=_epi_res_norm, extra=(x2,), consts=(g1f,), outs=_RES_NORM_OUTS)
    (dy, dyt, loss_row), a1 = _mlp_fwd("l1", x3, hf1, w["l1_w_ff1"], w["l1_w_ff2"], _epi_loss, (tgt,), (), _LOSS_OUTS)

    (dx3, g["l1_ffn_norm"], dx3t), ga_ff1, ga_ff2 = _mlp_bwd("l1", x3, g1f, w["l1_w_ff1"], w["l1_w_ff2"], hf1t, a1, dy, dyt)
    dcat1 = _mm("l1_out_dx", dx3, w["l1_w_out"], NT, f32)
    ga_out = _mm("l1_out_dw", dx3t, cat1, NN, bf16, out_t=True)
    dz1, dlbl, don1 = _scan_bwd("hgrn_scan_bwd", _f_hgrn_full, [(z1, 0, 4)], hg_hist, dcat1, HG_H, nchunk, dtypes=[bf16],
                                consts=(lbl, on1_t))
    dx2, g["l1_mix_norm"], dx2t = _mm("l1_in_dx", dz1, w["l1_w_in"], NT, f32, epi=_epi_norm_bwd, extra=(x2, dx3), consts=(g1m,),
                                      outs=_NORM_BWD_OUTS)
    ga_in = _mm("l1_in_dw", h1t, dz1, NN, bf16, slab=True)
    g["l1_hgrn_o_norm"] = don1.reshape(HG_H, HD).sum(0)
    g["hgrn_lb_logits"] = dlbl
    gs_a = [by_dev(nm, ga) for nm, ga in zip(_GRAD_A, (ga_ff1, ga_ff2, ga_out, ga_in))]

    (dx1, g["l0_ffn_norm"], dx1t), gb_ff1, gb_ff2 = _mlp_bwd("l0", x1, g0f, w["l0_w_ff1"], w["l0_w_ff2"], hf0t, a0, dx2, dx2t)
    dcat0 = _mm("l0_out_dx", dx1, w["l0_w_out"], NT, f32)
    gb_out = _mm("l0_out_dw", dx1t, cat0, NN, bf16, out_t=True)
    gs_b = [by_dev(nm, ga) for nm, ga in zip(_GRAD_B, (gb_ff1, gb_ff2, gb_out))]
    (*dinter, dfox_o, dgg, don0), r1_a = _scan_bwd("gdn_scan_bwd", _f_gdn_full, full_rows, gdn_hist, dcat0, GDN_H, nchunk,
                                                   consts=(on0_t,), out_k=2, dtypes=[f32] * 7 + [bf16], cps=1,
                                                   side=_side_rs_sibling(gs_a))
    pairs_a = pair(_GRAD_A, gs_a, r1_a)
    (dqn, dkn, dfv, dcq, dck), got = _fox_attn_bwd(qn, kn, z0, ccol, crow, dfox_o, t,
                                                   side=_join_sides([_side_rs_chips(pairs_a), _side_rs_sibling(gs_b)]))
    r2_a, r1_b = got[:len(_GRAD_A)], got[len(_GRAD_A):]
    adam(_GRAD_A, gs_a, r1_a, r2_a)
    pairs_b = pair(_GRAD_B, gs_b, r1_b)
    (dqkv, dzs_g, dalog_b, ddt_b), r2_b = _tok_bwd(
        "gdn_intra_bwd", _f_gdn_gated_given, intra_rows + [(intra[6], GDN_W, 0)], gate_consts, [(a, GDN_W) for a in dinter],
        2 * CH, [0, 1, 2, 3], [2, 3], ncat=3, side=_side_rs_chips(pairs_b))
    adam(_GRAD_B, gs_b, r1_b, r2_b)
    dgqkv, dwconv = _tok_bwd("gdn_conv_bwd", _f_conv, conv_rows, conv_consts, [(dqkv, LANES)], t, [0], [0], ncb=12, with_j=True,
                             drow_dtype=bf16)
    dzs_f, dfb = _fox_gate_bwd(z0, fbias, dcq, dck, t)
    dzqk, dgq_t, dgk_t = _tok_bwd("fox_pre_bwd", _f_foxpre, qk_rows, qk_consts, [(dqn, FOX_W), (dkn, FOX_W)], TOK_ROWS, [0], [0, 1],
                                  drow_dtype=bf16)
    dz0 = jnp.concatenate([dzqk, dfv.astype(bf16), dgqkv, dgg, (dzs_g + dzs_f).astype(bf16), jnp.zeros((n, ZW0 - 3712), bf16)], axis=1)
    gs_c = [_w_in0_slabs(_mm("l0_in_dw", h0t, dz0, NN, bf16), ncols["l0_w_in"])]
    r1_c = _rs_sibling(gs_c)
    pairs_c = pair(("l0_w_in",), gs_c, r1_c)
    (dx, g["l0_mix_norm"]), r2_c = _mm("l0_in_dx", dz0, w_in0, NT, f32, epi=lambda *a: _epi_norm_bwd(*a)[:2], extra=(x, dx1),
                                       consts=(g0m,), outs=_NORM_BWD_OUTS[:2], side=_side_rs_chips(pairs_c))
    adam(("l0_w_in",), gs_c, r1_c, r2_c)
    g["l0_fox_q_norm"] = dgq_t.reshape(FOX_H, FOX_D).sum(0)
    g["l0_fox_k_norm"] = dgk_t.reshape(FOX_H, FOX_D).sum(0)
    g["l0_fox_f_bias"] = dfb[0, :FOX_H]
    g["l0_gdn_conv"] = dwconv
    g["l0_gdn_A_log"] = dalog_b.reshape(GDN_H, HD).sum(1)
    g["l0_gdn_dt_bias"] = ddt_b.reshape(GDN_H, HD).sum(1)
    g["l0_gdn_o_norm"] = don0.reshape(GDN_H, HD).sum(0)
    return loss_row, dx, g, big


_NAMES = ("l0_mix_norm", "l0_w_in", "l0_fox_q_norm", "l0_fox_k_norm", "l0_fox_f_bias", "l0_gdn_conv", "l0_gdn_A_log",
          "l0_gdn_dt_bias", "l0_gdn_o_norm", "l0_w_out", "l0_ffn_norm", "l0_w_ff1", "l0_w_ff2", "l1_mix_norm", "l1_w_in",
          "l1_hgrn_o_norm", "l1_w_out", "l1_ffn_norm", "l1_w_ff1", "l1_w_ff2", "hgrn_lb_logits")
_SMALL_NAMES = tuple(nm for nm in _NAMES if nm not in _BIG_NAMES)
_SMALL_ROWS = 16


def _pack_small(vals):
    flat = jnp.concatenate([vals[nm].reshape(-1).astype(f32) for nm in _SMALL_NAMES])
    return jnp.pad(flat, (0, _SMALL_ROWS * D - flat.shape[0])).reshape(_SMALL_ROWS, D)


def _unpack_small(packed, shapes):
    flat = packed.reshape(-1)
    out, off = {}, 0
    for nm in _SMALL_NAMES:
        size = 1
        for s in shapes[nm]:
            size *= s
        out[nm] = flat[off:off + size].reshape(shapes[nm])
        off += size
    return out, off


def kernel(x, l0_mix_norm, l0_w_in, l0_fox_q_norm, l0_fox_k_norm, l0_fox_f_bias, l0_gdn_conv, l0_gdn_A_log, l0_gdn_dt_bias, l0_gdn_o_norm, l0_w_out, l0_ffn_norm, l0_w_ff1, l0_w_ff2, l1_mix_norm, l1_w_in, l1_hgrn_o_norm, l1_w_out, l1_ffn_norm, l1_w_ff1, l1_w_ff2, hgrn_lb_logits, loss_target, m_l0_mix_norm, m_l0_w_in, m_l0_fox_q_norm, m_l0_fox_k_norm, m_l0_fox_f_bias, m_l0_gdn_conv, m_l0_gdn_A_log, m_l0_gdn_dt_bias, m_l0_gdn_o_norm, m_l0_w_out, m_l0_ffn_norm, m_l0_w_ff1, m_l0_w_ff2, m_l1_mix_norm, m_l1_w_in, m_l1_hgrn_o_norm, m_l1_w_out, m_l1_ffn_norm, m_l1_w_ff1, m_l1_w_ff2, m_hgrn_lb_logits, v_l0_mix_norm, v_l0_w_in, v_l0_fox_q_norm, v_l0_fox_k_norm, v_l0_fox_f_bias, v_l0_gdn_conv, v_l0_gdn_A_log, v_l0_gdn_dt_bias, v_l0_gdn_o_norm, v_l0_w_out, v_l0_ffn_norm, v_l0_w_ff1, v_l0_w_ff2, v_l1_mix_norm, v_l1_w_in, v_l1_hgrn_o_norm, v_l1_w_out, v_l1_ffn_norm, v_l1_w_ff1, v_l1_w_ff2, v_hgrn_lb_logits):
    args = dict(zip(_NAMES, (l0_mix_norm, l0_w_in, l0_fox_q_norm, l0_fox_k_norm, l0_fox_f_bias, l0_gdn_conv, l0_gdn_A_log, l0_gdn_dt_bias, l0_gdn_o_norm, l0_w_out, l0_ffn_norm, l0_w_ff1, l0_w_ff2, l1_mix_norm, l1_w_in, l1_hgrn_o_norm, l1_w_out, l1_ffn_norm, l1_w_ff1, l1_w_ff2, hgrn_lb_logits)))
    mom = dict(zip(_NAMES, (m_l0_mix_norm, m_l0_w_in, m_l0_fox_q_norm, m_l0_fox_k_norm, m_l0_fox_f_bias, m_l0_gdn_conv, m_l0_gdn_A_log, m_l0_gdn_dt_bias, m_l0_gdn_o_norm, m_l0_w_out, m_l0_ffn_norm, m_l0_w_ff1, m_l0_w_ff2, m_l1_mix_norm, m_l1_w_in, m_l1_hgrn_o_norm, m_l1_w_out, m_l1_ffn_norm, m_l1_w_ff1, m_l1_w_ff2, m_hgrn_lb_logits)))
    var = dict(zip(_NAMES, (v_l0_mix_norm, v_l0_w_in, v_l0_fox_q_norm, v_l0_fox_k_norm, v_l0_fox_f_bias, v_l0_gdn_conv, v_l0_gdn_A_log, v_l0_gdn_dt_bias, v_l0_gdn_o_norm, v_l0_w_out, v_l0_ffn_norm, v_l0_w_ff1, v_l0_w_ff2, v_l1_mix_norm, v_l1_w_in, v_l1_hgrn_o_norm, v_l1_w_out, v_l1_ffn_norm, v_l1_w_ff1, v_l1_w_ff2, v_hgrn_lb_logits)))
    nb, t, _ = x.shape
    dev = 4 * lax.axis_index("x") + 2 * lax.axis_index("y") + lax.axis_index("c")
    conv_cols = l0_gdn_conv.shape[1]
    loss_row, dx, g, big = _train(x.reshape(nb * t, D), loss_target.reshape(nb * t, D), args, mom, var, t)

    shapes = {nm: args[nm].shape for nm in _SMALL_NAMES}
    gsm = dict(g)
    gsm["l0_gdn_conv"] = jnp.zeros(shapes["l0_gdn_conv"], f32)
    packed = _pack_small(gsm)
    _, used = _unpack_small(packed, shapes)
    flat_extra = jnp.concatenate([jnp.sum(loss_row).reshape(1), g["l0_gdn_conv"].reshape(-1)])
    packed = packed.reshape(-1).at[used:used + flat_extra.shape[0]].set(flat_extra).reshape(_SMALL_ROWS, D)
    (parts,) = _allgather("ag_small", [packed])
    total = _sum_parts(parts).reshape(-1)
    loss = 0.5 * total[used] / D
    conv_g_full = total[used + 1:used + 1 + 4 * NDEV * conv_cols].reshape(4, NDEV * conv_cols)
    conv_g = lax.dynamic_slice(conv_g_full, (0, dev * conv_cols), (4, conv_cols))
    own_vals = {nm: jnp.zeros(shapes[nm], f32) for nm in _SMALL_NAMES}
    own_vals["l0_gdn_conv"] = conv_g
    own_mask = {nm: jnp.zeros(shapes[nm], f32) for nm in _SMALL_NAMES}
    own_mask["l0_gdn_conv"] = jnp.ones(shapes["l0_gdn_conv"], f32)
    small = _adamw_small(parts, _pack_small(args), _pack_small(mom), _pack_small(var), _pack_small(own_mask), _pack_small(own_vals))
    small = [_unpack_small(a, shapes)[0] for a in small]
    small[0]["l0_gdn_conv"] = conv_g

    outs = [loss, dx.reshape(nb, t, D)]
    for k in range(4):
        outs += [big[nm][k] if nm in _BIG_NAMES else small[k][nm] for nm in _NAMES]
    return tuple(outs)
```

```python
import functools

import jax
import jax.numpy as jnp
from jax import lax
from jax.experimental import pallas as pl
from jax.experimental.pallas import tpu as pltpu

f32, bf16 = jnp.float32, jnp.bfloat16
NN = (((1,), (0,)), ((), ()))
NT = (((1,), (1,)), ((), ()))
TN = (((0,), (0,)), ((), ()))
MESH = pl.DeviceIdType.MESH
S = jax.ShapeDtypeStruct

EPS = 1e-6
D = 1024
LANES = 128
FOX_H, FOX_D, FOX_W = 8, 64, 512
GDN_H, HD, GDN_W = 4, 128, 512
HG_H = 8
CH = 64
TOK_ROWS = 256
ZW0 = 3840
NDEV = 8
ADAM_LR, ADAM_B1, ADAM_B2, ADAM_EPS, ADAM_WD, ADAM_STEP = 0.001, 0.9, 0.999, 1e-08, 0.01, 10

Z0_FV, Z0_GQKV, Z0_GG, Z0_SMALL = 8, 12, 24, 28


def _dot(a, b, dims=NN):
    return lax.dot_general(a, b, dims, preferred_element_type=f32)


def _iota2(shape, axis):
    return lax.broadcasted_iota(jnp.int32, shape, axis)


def _split3(x):
    x1 = x.astype(bf16)
    r = x - x1.astype(f32)
    x2 = r.astype(bf16)
    return x1, x2, (r - x2.astype(f32)).astype(bf16)


def _dot_sel(a, b, dims=NN, exact_lhs=False):
    if exact_lhs:
        return sum(_dot(a.astype(bf16), piece, dims) for piece in _split3(b))
    return sum(_dot(piece, b.astype(bf16), dims) for piece in _split3(a))


@jax.custom_vjp
def _sel_rhs(a, b):
    return _dot_sel(a, b)


_sel_rhs.defvjp(lambda a, b: (_dot_sel(a, b), b), lambda b, g: (_dot_sel(g, b, NT), jnp.zeros_like(b)))


@jax.custom_vjp
def _sel_lhs(a, x):
    return _dot_sel(a, x, exact_lhs=True)


_sel_lhs.defvjp(lambda a, x: (_dot_sel(a, x, exact_lhs=True), a), lambda a, g: (jnp.zeros_like(a), _dot_sel(a, g, TN, exact_lhs=True)))


class _Side:
    def __init__(self, ins, out_shapes, scratch, start, finish):
        self.ins, self.out_shapes, self.scratch, self.start, self.finish = list(ins), list(out_shapes), list(scratch), start, finish


def _join_sides(sides):
    def split(refs, counts):
        out, off = [], 0
        for c in counts:
            out.append(refs[off:off + c])
            off += c
        return out

    ni, no, ns = ([len(getattr(sd, a)) for sd in sides] for a in ("ins", "out_shapes", "scratch"))

    def run(which):
        def go(ins, outs, sems):
            for sd, i, o, c in zip(sides, split(ins, ni), split(outs, no), split(sems, ns)):
                getattr(sd, which)(i, o, c)
        return go

    return _Side(sum((sd.ins for sd in sides), []), sum((sd.out_shapes for sd in sides), []),
                 sum((sd.scratch for sd in sides), []), run("start"), run("finish"))


def _pcall(side, body, *, name, grid, in_specs, out_specs, out_shape, scratch_shapes=(), compiler_params=None, aliases=None):
    if side is None:
        return pl.pallas_call(body, name=name, grid=grid, in_specs=in_specs, out_specs=out_specs, out_shape=out_shape,
                              scratch_shapes=scratch_shapes, compiler_params=compiler_params, input_output_aliases=aliases or {})
    assert not aliases
    single = not isinstance(out_shape, (list, tuple))
    ospecs, oshape = ([out_specs], [out_shape]) if single else (list(out_specs), list(out_shape))
    nin, nout, nscr = len(in_specs), len(ospecs), len(scratch_shapes)
    si, so = len(side.ins), len(side.out_shapes)

    def wrapped(*refs):
        o0 = nin + si
        c0 = o0 + nout + so
        sins, souts, ssems = refs[nin:o0], refs[o0 + nout:c0], refs[c0 + nscr:]
        ids = [pl.program_id(a) for a in range(len(grid))]
        first = functools.reduce(jnp.logical_and, [i == 0 for i in ids])
        last = functools.reduce(jnp.logical_and, [i == g - 1 for i, g in zip(ids, grid)])

        @pl.when(first)
        def _():
            side.start(sins, souts, ssems)

        body(*refs[:nin], *refs[o0:o0 + nout], *refs[c0:c0 + nscr])

        @pl.when(last)
        def _():
            side.finish(sins, souts, ssems)

    call = pl.pallas_call(
        wrapped, name=name, grid=grid, in_specs=list(in_specs) + _hbm_specs(si), out_specs=ospecs + _hbm_specs(so),
        out_shape=oshape + side.out_shapes, scratch_shapes=list(scratch_shapes) + side.scratch,
        compiler_params=pltpu.CompilerParams(dimension_semantics=("arbitrary",) * len(grid),
                                             vmem_limit_bytes=getattr(compiler_params, "vmem_limit_bytes", None)))

    def run(*args):
        res = call(*args, *side.ins)
        return (res[0] if single else list(res[:nout])), list(res[nout:])

    return run


def _tok_specs(rows, consts, tm):
    specs = []
    for (_, w, base) in rows:
        specs.append(pl.BlockSpec((tm, w), functools.partial(lambda j, i, b: (i, b + j), b=base)))
    for (arr, w, base) in consts:
        if w is None:
            specs.append(pl.BlockSpec(arr.shape, lambda j, i: (0, 0)))
        else:
            specs.append(pl.BlockSpec((arr.shape[0], w), functools.partial(lambda j, i, b: (0, b + j), b=base)))
    return specs


def _tok_fwd(name, f, rows, consts, outs, tm, ncb=1, with_j=False, also_t=(), side=None):
    n = rows[0][0].shape[0]
    nin = len(rows) + len(consts)
    nout = len(outs)

    def body(*refs):
        ins = [r[...] for r in refs[:nin]]
        vals = f(pl.program_id(0), *ins) if with_j else f(*ins)
        for r, v in zip(refs[nin:nin + nout], vals):
            r[...] = v.astype(r.dtype)
        for r, k in zip(refs[nin + nout:], also_t):
            r[...] = vals[k].T.astype(r.dtype)

    return _pcall(
        side, body, name=name, grid=(ncb, n // tm),
        in_specs=_tok_specs(rows, consts, tm),
        out_specs=[pl.BlockSpec((tm, w), lambda j, i: (i, j)) for (w, _) in outs]
        + [pl.BlockSpec((outs[k][0], tm), lambda j, i: (0, i)) for k in also_t],
        out_shape=[S((n, w * ncb), dt) for (w, dt) in outs] + [S((outs[k][0], n), bf16) for k in also_t],
        compiler_params=pltpu.CompilerParams(dimension_semantics=("parallel", "parallel")),
    )(*[r[0] for r in rows], *[c[0] for c in consts])


def _tok_bwd(name, f, rows, consts, cots, tm, drow, dconst, ncb=1, with_j=False, addto=None, also_t=(), drow_dtype=f32, side=None, ncat=0,
             into=None):
    n = rows[0][0].shape[0]
    nr, nc, nct = len(rows), len(consts), len(cots)
    nbuf = int(into is not None and into[0] is not None)
    addto = addto or {}
    add_keys = sorted(addto)
    nadd = len(add_keys)

    def body(*refs):
        ins = [r[...] for r in refs[:nr + nc]]
        cot = [r[...] for r in refs[nr + nc:nr + nc + nct]]
        adds = refs[nr + nc + nct:nr + nc + nct + nadd]
        outs = refs[nr + nc + nct + nadd + nbuf:]
        pos = list(drow) + [nr + k for k in dconst]

        def g(*dargs):
            full = list(ins)
            for p, a in zip(pos, dargs):
                full[p] = a
            return tuple(f(pl.program_id(0), *full) if with_j else f(*full))

        vals, vjp = jax.vjp(g, *[ins[p] for p in pos])
        grads = vjp(tuple(c.astype(v.dtype) for c, v in zip(cot, vals)))
        off = 0
        for k in range(len(drow)):
            gk = grads[k]
            if k in addto:
                gk = gk + adds[add_keys.index(k)][...]
            if k < ncat:
                outs[0][:, off:off + gk.shape[1]] = gk.astype(outs[0].dtype)
                off += gk.shape[1]
            else:
                outs[k - skip][...] = gk.astype(outs[k - skip].dtype)
            if k in also_t:
                tref = outs[len(drow) - skip + len(dconst) + list(also_t).index(k)]
                tref[...] = gk.T.astype(tref.dtype)
        first = pl.program_id(1) == 0
        for k in range(len(dconst)):
            ref = outs[len(drow) - skip + k]

            @pl.when(first)
            def _():
                ref[...] = jnp.zeros_like(ref)

            ref[...] += grads[len(drow) + k]

    skip = max(ncat - 1, 0)
    in_specs = _tok_specs(rows, consts, tm)
    in_specs += [pl.BlockSpec((tm, w), lambda j, i: (i, j)) for (_, w) in cots]
    in_specs += [pl.BlockSpec((tm, rows[drow[k]][1]), lambda j, i: (i, j)) for k in add_keys]
    dts = drow_dtype if isinstance(drow_dtype, (list, tuple)) else [drow_dtype] * len(drow)
    widths = [rows[k][1] for k in drow]
    if ncat:
        widths, dts = [sum(widths[:ncat])] + widths[ncat:], [dts[0]] + list(dts[ncat:])
    out_specs = [pl.BlockSpec((tm, wd), lambda j, i: (i, j)) for wd in widths]
    out_shape = [S((n, wd * ncb), dt) for wd, dt in zip(widths, dts)]
    bufs = []
    if into is not None:
        buf, total, col0 = into
        out_specs[0] = pl.BlockSpec((tm, widths[0]), functools.partial(lambda j, i, b0: (i, j + b0), b0=col0 // widths[0]))
        out_shape[0] = S((n, total), dts[0])
        if nbuf:
            in_specs += _hbm_specs(1)
            bufs = [buf]
    for k in dconst:
        arr, w, _ = consts[k]
        if w is None:
            out_specs.append(pl.BlockSpec(arr.shape, lambda j, i: (0, 0)))
            out_shape.append(S(arr.shape, f32))
        else:
            out_specs.append(pl.BlockSpec((arr.shape[0], w), lambda j, i: (0, j)))
            out_shape.append(S((arr.shape[0], w * ncb), f32))
    for k in also_t:
        out_specs.append(pl.BlockSpec((rows[drow[k]][1], tm), lambda j, i: (0, i)))
        out_shape.append(S((rows[drow[k]][1], n), bf16))
    return _pcall(
        side, body, name=name, grid=(ncb, n // tm), in_specs=in_specs, out_specs=out_specs, out_shape=out_shape,
        compiler_params=pltpu.CompilerParams(dimension_semantics=("parallel", "arbitrary")),
        aliases={len(in_specs) - 1: 0} if nbuf else None,
    )(*[r[0] for r in rows], *[c[0] for c in consts], *[c[0] for c in cots], *[addto[k] for k in add_keys], *bufs)


SCAN_CHUNKS = 2


def _scan_fwd(name, f, rows, nh, nchunk, side=None, consts=(), out_dtype=f32, out_k=1):
    n = rows[0][0].shape[0]
    nb = n // (CH * nchunk)
    nin, nco = len(rows), len(consts)
    w = nh * HD
    cps = SCAN_CHUNKS

    def body(*refs):
        o_ref, hist_ref, st = refs[nin + nco], refs[nin + nco + 1], refs[nin + nco + 2]

        @pl.when(pl.program_id(0) == 0)
        def _():
            st[...] = jnp.zeros_like(st)

        cvals = [r[...] for r in refs[nin:nin + nco]]
        state = st[...]
        for sub in range(cps):
            rr = pl.ds(sub * CH, CH)
            hist_ref[sub] = state.astype(hist_ref.dtype)
            tiles = [r[:, rr, :].reshape(nb * CH, r.shape[2]) for r in refs[:nin]]
            o, state = f(*tiles, *cvals, state)
            o_ref[:, rr, :] = o.reshape(nb, CH, out_k * w).astype(o_ref.dtype)
        st[...] = state

    seq3 = lambda a: a.reshape(nb, nchunk * CH, a.shape[1])
    res = _pcall(
        side, body, name=name, grid=(nchunk // cps,),
        in_specs=[pl.BlockSpec((nb, cps * CH, k * w), functools.partial(lambda c, base: (0, c, base), base=b)) for (_, b, k) in rows]
        + [pl.BlockSpec(c.shape, lambda c: (0, 0)) for c in consts],
        out_specs=[pl.BlockSpec((nb, cps * CH, out_k * w), lambda c: (0, c, 0)), pl.BlockSpec((cps, nb * w, HD), lambda c: (c, 0, 0))],
        out_shape=[S((nb, nchunk * CH, out_k * w), out_dtype), S((nchunk, nb * w, HD), bf16)],
        scratch_shapes=[pltpu.VMEM((nb * w, HD), f32)],
        compiler_params=pltpu.CompilerParams(dimension_semantics=("arbitrary",)),
    )(*[seq3(r[0]) for r in rows], *consts)
    (o, hist), extra = (res, None) if side is None else res
    out = [o.reshape(n, out_k * w), hist]
    return out if side is None else (out, extra)


def _scan_bwd(name, f, rows, hist, do, nh, nchunk, side=None, dtypes=None, consts=(), out_k=1, cps=SCAN_CHUNKS):
    n = rows[0][0].shape[0]
    nb = n // (CH * nchunk)
    nin, nco = len(rows), len(consts)
    w = nh * HD
    nstep = nchunk // cps

    def body(*refs):
        hist_ref, do_ref = refs[nin + nco], refs[nin + nco + 1]
        outs = refs[nin + nco + 2:nin + nco + 2 + nin]
        couts = refs[nin + nco + 2 + nin:nin + nco + 2 + nin + nco]
        ds = refs[nin + nco + 2 + nin + nco]

        @pl.when(pl.program_id(0) == 0)
        def _():
            ds[...] = jnp.zeros_like(ds)
            for c in couts:
                c[...] = jnp.zeros_like(c)

        cvals = [r[...] for r in refs[nin:nin + nco]]
        dstate = ds[...]
        for sub in reversed(range(cps)):
            rr = pl.ds(sub * CH, CH)
            tiles = [r[:, rr, :].reshape(nb * CH, r.shape[2]) for r in refs[:nin]]
            _, vjp = jax.vjp(f, *tiles, *cvals, hist_ref[sub].astype(f32))
            grads = vjp((do_ref[:, rr, :].reshape(nb * CH, out_k * w), dstate))
            for r, gk in zip(outs, grads[:nin]):
                r[:, rr, :] = gk.reshape(nb, CH, r.shape[2]).astype(r.dtype)
            for c, gk in zip(couts, grads[nin:nin + nco]):
                c[...] += gk
            dstate = grads[nin + nco]
        ds[...] = dstate

    seq3 = lambda a: a.reshape(nb, nchunk * CH, a.shape[1])
    rev = lambda c, base: (0, nstep - 1 - c, base)
    res = _pcall(
        side, body, name=name, grid=(nstep,),
        in_specs=[pl.BlockSpec((nb, cps * CH, k * w), functools.partial(rev, base=b)) for (_, b, k) in rows]
        + [pl.BlockSpec(c.shape, lambda c: (0, 0)) for c in consts]
        + [pl.BlockSpec((cps, nb * w, HD), lambda c: (nstep - 1 - c, 0, 0)),
           pl.BlockSpec((nb, cps * CH, out_k * w), functools.partial(rev, base=0))],
        out_specs=[pl.BlockSpec((nb, cps * CH, k * w), functools.partial(rev, base=0)) for (_, _, k) in rows]
        + [pl.BlockSpec(c.shape, lambda c: (0, 0)) for c in consts],
        out_shape=[S((nb, nchunk * CH, k * w), dt) for (_, _, k), dt in zip(rows, dtypes or [f32] * nin)]
        + [S(c.shape, f32) for c in consts],
        scratch_shapes=[pltpu.VMEM((nb * w, HD), f32)],
        compiler_params=pltpu.CompilerParams(dimension_semantics=("arbitrary",)),
    )(*[seq3(r[0]) for r in rows], *consts, hist, seq3(do))
    outs, extra = (res, None) if side is None else res
    outs = [o.reshape(n, o.shape[2]) for o in outs[:nin]] + list(outs[nin:])
    return outs if side is None else (outs, extra)


_VMEM_LIMIT = 56 * 2 ** 20
_VMEM_TILE_BUDGET = 40 * 2 ** 20


def _mm_tiles(m, n, k, sa, sb, so, sx, a_f32, b_f32, tn_fixed):
    best = None
    for tm in (1024, 512, 256, 128, 64):
        for tn in ((tn_fixed,) if tn_fixed else (1024, 768, 512, 384, 256, 128)):
            if m % tm or n % tn:
                continue
            need = 2 * (tm * k * sa + k * tn * sb + tm * tn * (so + sx)) + tm * tn * 4
            need += tm * k * (2 if sa == 4 else 0) + k * tn * (2 if sb == 4 else 0)
            need += tm * k * (4 if a_f32 else 0) + k * tn * (4 if b_f32 else 0)
            if need <= _VMEM_TILE_BUDGET and (best is None or (tm * tn, tm) > best[0]):
                best = ((tm * tn, tm), tm, tn)
    return best[1], best[2]


def _mm(name, a, b, dims, out_dtype, a_fn=None, b_fn=None, epi=None, extra=(), consts=(), outs=None, out_t=False, slab=False,
        side=None):
    m, kk = a.shape
    gathered = b.ndim == 3
    if gathered:
        nn = NDEV * _COLW if dims is NN else b.shape[1]
    else:
        nn = b.shape[1] if dims is NN else b.shape[0]
    kinds = [("tile", out_dtype)] if outs is None else list(outs)
    so = sum(jnp.dtype(dt).itemsize for kd, dt in kinds if kd != "rows")
    sx = sum(e.dtype.itemsize for e in extra)
    full_rows = bool(consts) or any(kd == "rows" for kd, _ in kinds)
    tm, tn = _mm_tiles(m, nn, kk, a.dtype.itemsize, b.dtype.itemsize, so, sx,
                       a_fn is not None, b_fn is not None,
                       _COLW if slab else (nn if full_rows else None))
    gblocks = tn // _COLW if (gathered and dims is NN) else 0
    assert not gblocks or (outs is None and not extra and not out_t and not slab and b_fn is None)
    nex, nco = len(extra), len(consts)

    def body(a_ref, b_ref, *rest):
        av = a_ref[...]
        if a_fn is not None:
            av = a_fn(av.astype(f32))
        av = av.astype(bf16)
        if gblocks:
            for gb in range(gblocks):
                part = _dot(av, b_ref[gb], NN)
                rest[nex + nco][:, _COLW * gb:_COLW * (gb + 1)] = (part if epi is None else epi(part)).astype(rest[nex + nco].dtype)
            return
        if gathered:
            acc = sum(_dot(av[:, _COLW * d:_COLW * (d + 1)], b_ref[d], NT) for d in range(NDEV))
        else:
            bv = b_ref[...]
            if b_fn is not None:
                bv = b_fn(bv.astype(f32))
            acc = _dot(av, bv.astype(bf16), dims)
        if epi is not None:
            acc = epi(acc, *[r[...] for r in rest[:nex + nco]])
        vals = acc if isinstance(acc, tuple) else (acc,)
        for (kd, _), o_ref, val in zip(kinds, rest[nex + nco:], vals):
            if kd == "rows":
                @pl.when(pl.program_id(0) == 0)
                def _():
                    o_ref[...] = jnp.zeros_like(o_ref)

                o_ref[...] += val
            elif kd == "tile_t" or out_t:
                o_ref[...] = val.T.astype(o_ref.dtype)
            elif slab:
                o_ref[0] = val.astype(o_ref.dtype)
            else:
                o_ref[...] = val.astype(o_ref.dtype)

    if gathered:
        bspec = (pl.BlockSpec((gblocks, kk, _COLW), lambda i, j: (j, 0, 0)) if dims is NN
                 else pl.BlockSpec((NDEV, tn, _COLW), lambda i, j: (0, j, 0)))
    else:
        bspec = pl.BlockSpec((kk, tn), lambda i, j: (0, j)) if dims is NN else pl.BlockSpec((tn, kk), lambda i, j: (j, 0))
    out_specs, out_shape = [], []
    for kd, dt in kinds:
        if kd == "rows":
            out_specs.append(pl.BlockSpec((1, nn), lambda i, j: (0, 0)))
            out_shape.append(S((1, nn), dt))
        elif kd == "tile_t" or out_t:
            out_specs.append(pl.BlockSpec((tn, tm), lambda i, j: (j, i)))
            out_shape.append(S((nn, m), dt))
        elif slab:
            out_specs.append(pl.BlockSpec((1, tm, tn), lambda i, j: (j, i, 0)))
            out_shape.append(S((nn // tn, m, tn), dt))
        else:
            out_specs.append(pl.BlockSpec((tm, tn), lambda i, j: (i, j)))
            out_shape.append(S((m, nn), dt))
    if outs is None:
        out_specs, out_shape = out_specs[0], out_shape[0]
    sem = ("arbitrary", "arbitrary") if any(kd == "rows" for kd, _ in kinds) else ("parallel", "parallel")
    return _pcall(
        side, body, name=name, grid=(m // tm, nn // tn),
        in_specs=[pl.BlockSpec((tm, kk), lambda i, j: (i, 0)), bspec]
        + [pl.BlockSpec((tm, tn), lambda i, j: (i, j)) for _ in extra]
        + [pl.BlockSpec(c.shape, lambda i, j: (0, 0)) for c in consts],
        out_specs=out_specs, out_shape=out_shape,
        compiler_params=pltpu.CompilerParams(dimension_semantics=sem, vmem_limit_bytes=_VMEM_LIMIT),
    )(a, b, *extra, *consts)


def _f_norm(x, g):
    return (x * lax.rsqrt(jnp.mean(x * x, axis=-1, keepdims=True) + EPS) * g,)


def _f_foxpre(zqk, gq, gk, pm):
    def nrm(t, g):
        return t * lax.rsqrt(_sel_rhs(t * t, pm) + EPS) * g
    return nrm(zqk[:, :FOX_W], gq), nrm(zqk[:, FOX_W:], gk)


def _chunk_cumsum(x):
    n = x.shape[0]
    r, c = _iota2((n, n), 0), _iota2((n, n), 1)
    tri = jnp.logical_and(r >= c, (r // CH) == (c // CH)).astype(f32)
    return _sel_lhs(tri, x)


def _f_gdngate(zs, eb, ea, alog_b, dt_b):
    beta = jax.nn.sigmoid(_sel_rhs(zs, eb))
    la = -jnp.exp(alog_b) * jax.nn.softplus(_sel_rhs(zs, ea) + dt_b)
    return beta, _chunk_cumsum(la)


def _f_conv(j, x, w):
    t = x.shape[0]
    y = x * w[3:4, :]
    for jj in range(3):
        sh = 3 - jj
        xs = jnp.concatenate([jnp.zeros((sh, x.shape[1]), f32), x[:t - sh, :]], axis=0)
        y = y + xs * w[jj:jj + 1, :]
    y = jax.nn.silu(y)
    yn = y * lax.rsqrt(jnp.sum(y * y, axis=-1, keepdims=True) + EPS)
    return (jnp.where(j < 2 * GDN_H, yn, y),)


def _head_rms(o, nh):
    outs = []
    for h in range(nh):
        oh = o[:, HD * h:HD * (h + 1)]
        outs.append(oh * lax.rsqrt(jnp.mean(oh * oh, axis=-1, keepdims=True) + EPS))
    return jnp.concatenate(outs, axis=1)


def _f_post0(fox_o, o, gg, on):
    return (jnp.concatenate([fox_o, _head_rms(o, GDN_H) * on * jax.nn.silu(gg)], axis=1),)


def _f_post1(o, zg, on):
    return (_head_rms(o, HG_H) * on * jax.nn.silu(zg),)


def _f_hpre(zqf, lbl):
    lb = jax.nn.sigmoid(lbl[1:2, :] - lbl[0:1, :])
    fg = lb + (1.0 - lb) * jax.nn.sigmoid(zqf[:, D:])
    return jax.nn.silu(zqf[:, :D]), 1.0 - fg, _chunk_cumsum(jnp.log(fg))


def _dotb(a, b, dims=NN):
    return _dot(a.astype(bf16), b.astype(bf16), dims)


def _dot3(a, b):
    ah, bh = a.astype(bf16), b.astype(bf16)
    al, bl = (a - ah.astype(f32)).astype(bf16), (b - bh.astype(f32)).astype(bf16)
    return _dot(ah, bh) + (_dot(ah, bl) + _dot(al, bh))


def _split(t, nh):
    return [t[CH * ck:CH * (ck + 1), HD * h:HD * (h + 1)] for ck in range(t.shape[0] // CH) for h in range(nh)]


def _merge(units, nh):
    return jnp.concatenate([jnp.concatenate(units[i:i + nh], axis=1) for i in range(0, len(units), nh)], axis=0)


def _inv_impl(amats):
    n = amats[0].shape[0]
    eye = jnp.where(_iota2((n, n), 0) == _iota2((n, n), 1), 1.0, 0.0).astype(f32)
    xs, ps = [eye - a for a in amats], list(amats)
    for _ in range(max(1, (n - 1).bit_length()) - 1):
        ps = [_dotb(p, p) for p in ps]
        xs = [x + _dotb(x, p) for x, p in zip(xs, ps)]
    for _ in range(3):
        rs = [eye - x - _dot3(a, x) for a, x in zip(amats, xs)]
        xs = [x + _dotb(x, r) for x, r in zip(xs, rs)]
    return tuple(xs)


@jax.custom_vjp
def _inv_unit_lower(amats):
    return _inv_impl(amats)


def _inv_fwd(amats):
    xs = _inv_impl(amats)
    return xs, xs


def _inv_bwd(xs, dxs):
    return (tuple(-_dotb(_dotb(x, dx, TN), x, NT) for x, dx in zip(xs, dxs)),)


_inv_unit_lower.defvjp(_inv_fwd, _inv_bwd)


@jax.custom_vjp
def _inv_given(amats, xs):
    return xs


def _inv_given_fwd(amats, xs):
    return xs, xs


def _inv_given_bwd(xs, dxs):
    return _inv_bwd(xs, dxs)[0], tuple(jnp.zeros_like(x) for x in xs)


_inv_given.defvjp(_inv_given_fwd, _inv_given_bwd)


def _f_gdn_intra(q, k, v, bb, gb, tinv_p=None):
    qs, ks, vs, bs, gs = (_split(t, GDN_H) for t in (q, k, v, bb, gb))
    r, cc = _iota2((CH, CH), 0), _iota2((CH, CH), 1)
    causal, strict = r >= cc, r > cc
    beta, g, gl = [b[:, :1] for b in bs], [x[:, :1] for x in gs], [x[CH - 1:CH, :1] for x in gs]
    decay = [jnp.exp(jnp.where(causal, x[:, :CH] - x[:, :CH].T, -jnp.inf)) for x in gs]
    kb = [ki * bi for ki, bi in zip(ks, beta)]
    amat = [jnp.where(strict, _dotb(kbi, ki, NT) * di, 0.0) for kbi, ki, di in zip(kb, ks, decay)]
    if tinv_p is None:
        tinv = _inv_unit_lower(tuple(amat))
    else:
        tinv = _inv_given(tuple(amat), tuple(x[:, :CH] for x in _split(tinv_p, GDN_H)))
    rhs = [jnp.concatenate([vi * bi, kbi * jnp.exp(gi)], axis=1) for vi, bi, kbi, gi in zip(vs, beta, kb, g)]
    uw = [_dotb(ti, ri) for ti, ri in zip(tinv, rhs)]
    qsc = [qi * (HD ** -0.5) for qi in qs]
    qk = [jnp.where(causal, _dotb(qi, ki, NT) * di, 0.0) for qi, ki, di in zip(qsc, ks, decay)]
    outs = ([x[:, :HD] for x in uw], [x[:, HD:] for x in uw],
            [jnp.concatenate([x, jnp.zeros_like(x)], axis=1) for x in qk],
            [qi * jnp.exp(gi) for qi, gi in zip(qsc, g)],
            [ki * jnp.exp(gli - gi) for ki, gli, gi in zip(ks, gl, g)],
            [jnp.broadcast_to(gli, (CH, HD)) for gli in gl])
    if tinv_p is None:
        outs += ([jnp.concatenate([x, jnp.zeros_like(x)], axis=1) for x in tinv],)
    return tuple(_merge(o, GDN_H) for o in outs)


def _f_gdn_gated(q, k, v, zs, eb, ea, alog_b, dt_b):
    return _f_gdn_intra(q, k, v, *_f_gdngate(zs, eb, ea, alog_b, dt_b))


def _f_gdn_gated_given(q, k, v, zs, tinv_p, eb, ea, alog_b, dt_b):
    return _f_gdn_intra(q, k, v, *_f_gdngate(zs, eb, ea, alog_b, dt_b), tinv_p=tinv_p)


def _f_gdn_inter(u, w, qkp, qd, kd, glb, st):
    us, ws, qks, qds, kds, gls = (_split(t, GDN_H) for t in (u, w, qkp, qd, kd, glb))
    sts = [st[HD * i:HD * (i + 1), :] for i in range(len(us))]
    vn = [ui - _dotb(wi, si) for ui, wi, si in zip(us, ws, sts)]
    o = [_dotb(qi, si) + _dotb(xi[:, :CH], vi) for qi, si, xi, vi in zip(qds, sts, qks, vn)]
    s2 = [si * jnp.exp(gi[:1, :1]) + _dotb(ki, vi, TN) for si, gi, ki, vi in zip(sts, gls, kds, vn)]
    return _merge(o, GDN_H), jnp.concatenate(s2, axis=0)


def _f_gdn_full(u, w, qkp, qd, kd, glb, fox_o, gg, on, st):
    o, s2 = _f_gdn_inter(u, w, qkp, qd, kd, glb, st)
    return _f_post0(fox_o, o, gg, on)[0], s2


def _f_hgrn_chunk(q, k, b, v, st):
    qs, ks, bs, vs = (_split(t, HG_H) for t in (q, k, b, v))
    sts = [st[HD * i:HD * (i + 1), :] for i in range(len(qs))]
    causal = _iota2((CH, CH), 0) >= _iota2((CH, CH), 1)
    bl, bm = [x[CH - 1:CH, :] for x in bs], [x[CH // 2 - 1:CH // 2, :] for x in bs]
    a = [jnp.where(causal, _dotb(qi * jnp.exp(bi - mi), ki * jnp.exp(mi - bi), NT), 0.0)
         for qi, ki, bi, mi in zip(qs, ks, bs, bm)]
    o = [_dotb(qi * jnp.exp(bi), si, NT) + _dotb(ai, vi) for qi, bi, si, ai, vi in zip(qs, bs, sts, a, vs)]
    s2 = [si * jnp.exp(li) + _dotb(vi, ki * jnp.exp(li - bi), TN) for si, li, vi, ki, bi in zip(sts, bl, vs, ks, bs)]
    return _merge(o, HG_H), jnp.concatenate(s2, axis=0)


def _f_hgrn_full(z, lbl, on, st):
    o, s2 = _f_hgrn_chunk(*_f_hpre(z[:, :2 * D], lbl), z[:, 2 * D:3 * D], st)
    return _f_post1(o, z[:, 3 * D:], on)[0], s2


def _fox_gate_fwd(z0, fbias, t, tc=256):
    n = z0.shape[0]
    nt = t // tc

    def body(zs_ref, b_ref, ccol_ref, crow_ref, carry):
        @pl.when(pl.program_id(1) == 0)
        def _():
            carry[...] = jnp.zeros_like(carry)

        ls = jnp.where(_iota2((tc, LANES), 1) < FOX_H, jax.nn.log_sigmoid(zs_ref[...] + b_ref[...]), 0.0)
        tri = (_iota2((tc, tc), 0) >= _iota2((tc, tc), 1)).astype(f32)
        c = _dot_sel(tri, ls, exact_lhs=True) + carry[...]
        carry[...] = c[tc - 1:tc, :]
        ccol_ref[...] = c
        crow_ref[0] = c.T[:FOX_H, :]

    return pl.pallas_call(
        body, name="fox_gate_fwd", grid=(n // t, nt),
        in_specs=[pl.BlockSpec((tc, LANES), lambda b, i: (b * nt + i, Z0_SMALL)), pl.BlockSpec((1, LANES), lambda b, i: (0, 0))],
        out_specs=[pl.BlockSpec((tc, LANES), lambda b, i: (b * nt + i, 0)), pl.BlockSpec((1, FOX_H, tc), lambda b, i: (b, 0, i))],
        out_shape=[S((n, LANES), f32), S((n // t, FOX_H, t), f32)],
        scratch_shapes=[pltpu.VMEM((1, LANES), f32)],
        compiler_params=pltpu.CompilerParams(dimension_semantics=("parallel", "arbitrary")),
    )(z0, fbias)


def _fox_gate_bwd(z0, fbias, dcq, dck, t, tc=256):
    n = z0.shape[0]
    nt = t // tc

    def body(zs_ref, b_ref, dcq_ref, dck_ref, dz_ref, db_ref, carry):
        first = jnp.logical_and(pl.program_id(0) == 0, pl.program_id(1) == 0)

        @pl.when(pl.program_id(1) == 0)
        def _():
            carry[...] = jnp.zeros_like(carry)

        @pl.when(first)
        def _():
            db_ref[...] = jnp.zeros_like(db_ref)

        dc = dcq_ref[0] + dcq_ref[1] + dcq_ref[2] + dcq_ref[3]
        drow = dck_ref[0, 0] + dck_ref[1, 0] + dck_ref[2, 0] + dck_ref[3, 0]
        eye = (_iota2((FOX_H, LANES), 0) == _iota2((FOX_H, LANES), 1)).astype(f32)
        dc = dc + _dot_sel(drow, eye, TN)
        triu = (_iota2((tc, tc), 0) <= _iota2((tc, tc), 1)).astype(f32)
        dls = _dot_sel(triu, dc, exact_lhs=True) + carry[...]
        carry[...] = dls[0:1, :]
        x = zs_ref[...] + b_ref[...]
        dz = jnp.where(_iota2((tc, LANES), 1) < FOX_H, dls * jax.nn.sigmoid(-x), 0.0)
        dz_ref[...] = dz
        db_ref[...] += jnp.sum(dz, axis=0, keepdims=True)

    def rev(b, i):
        return b * nt + (nt - 1 - i)

    return pl.pallas_call(
        body, name="fox_gate_bwd", grid=(n // t, nt),
        in_specs=[pl.BlockSpec((tc, LANES), lambda b, i: (rev(b, i), Z0_SMALL)), pl.BlockSpec((1, LANES), lambda b, i: (0, 0)),
                  pl.BlockSpec((4, tc, LANES), lambda b, i: (0, rev(b, i), 0)),
                  pl.BlockSpec((4, 1, FOX_H, tc), lambda b, i: (0, b, 0, nt - 1 - i))],
        out_specs=[pl.BlockSpec((tc, LANES), lambda b, i: (rev(b, i), 0)), pl.BlockSpec((1, LANES), lambda b, i: (0, 0))],
        out_shape=[S((n, LANES), f32), S((1, LANES), f32)],
        scratch_shapes=[pltpu.VMEM((1, LANES), f32)],
        compiler_params=pltpu.CompilerParams(dimension_semantics=("arbitrary", "arbitrary")),
    )(z0, fbias, dcq, dck)


def _fox_scores(hh, p, i, tq, q, k, ccol, crow):
    kmax = k.shape[0]
    lane = _iota2((1, LANES), 1)
    mh = (lane // FOX_D) == hh
    h = 2 * p + hh
    qh = jnp.where(mh, q, 0.0).astype(bf16)
    s = _dot(qh, k, NT) * (FOX_D ** -0.5)
    cq = jnp.sum(jnp.where(lane == h, ccol, 0.0), axis=1, keepdims=True)
    ck = jnp.sum(jnp.where(_iota2((FOX_H, 1), 0) == h, crow, 0.0), axis=0, keepdims=True)
    causal = _iota2((1, kmax), 1) <= (i * tq + _iota2((tq, 1), 0))
    s = jnp.where(causal, s + cq - ck, -jnp.inf)
    pe = jnp.exp(s - jnp.max(s, axis=1, keepdims=True))
    return mh, qh, pe, jnp.sum(pe, axis=1, keepdims=True)


def _fox_attn_fwd(qn, kn, z0, ccol, crow, t, tq=256, side=None):
    n = qn.shape[0]
    nq = t // tq

    def body(q_ref, k_ref, v_ref, ccol_ref, crow_ref, o_ref):
        p = pl.program_id(1)
        k, v, crow = k_ref[...].astype(bf16), v_ref[...].astype(bf16), crow_ref[0]
        for i in range(nq):
            rows, kmax = pl.ds(i * tq, tq), (i + 1) * tq
            q, cc = q_ref[rows, :], ccol_ref[rows, :]
            acc = jnp.zeros((tq, LANES), f32)
            for hh in range(2):
                mh, _, pe, l = _fox_scores(hh, p, i, tq, q, k[:kmax], cc, crow[:, :kmax])
                acc = jnp.where(mh, _dot(pe.astype(bf16), v[:kmax]) / l, acc)
            o_ref[rows, :] = acc

    seq = lambda b, p: (b, p)
    return _pcall(
        side, body, name="fox_attn_fwd", grid=(n // t, FOX_H // 2),
        in_specs=[pl.BlockSpec((t, LANES), seq), pl.BlockSpec((t, LANES), seq), pl.BlockSpec((t, LANES), lambda b, p: (b, Z0_FV + p)),
                  pl.BlockSpec((t, LANES), lambda b, p: (b, 0)), pl.BlockSpec((1, FOX_H, t), lambda b, p: (b, 0, 0))],
        out_specs=pl.BlockSpec((t, LANES), seq),
        out_shape=S((n, FOX_W), f32),
        compiler_params=pltpu.CompilerParams(dimension_semantics=("parallel", "parallel")),
    )(qn, kn, z0, ccol, crow)


def _fox_attn_bwd(qn, kn, z0, ccol, crow, do, t, tq=256, side=None):
    n = qn.shape[0]
    nq = t // tq
    nb = n // t

    def body(q_ref, k_ref, v_ref, ccol_ref, crow_ref, do_ref, dq_ref, dk_ref, dv_ref, dcq_ref, dck_ref):
        p = pl.program_id(1)
        dk_ref[...] = jnp.zeros_like(dk_ref)
        dv_ref[...] = jnp.zeros_like(dv_ref)
        dck_ref[...] = jnp.zeros_like(dck_ref)
        kf, v, crow = k_ref[...], v_ref[...].astype(bf16), crow_ref[0]
        k = kf.astype(bf16)
        lane = _iota2((1, LANES), 1)
        sub = _iota2((FOX_H, 1), 0)
        scale = FOX_D ** -0.5
        for i in range(nq):
            rows, kmax = pl.ds(i * tq, tq), (i + 1) * tq
            q, cc, dout = q_ref[rows, :], ccol_ref[rows, :], do_ref[rows, :]
            dq = jnp.zeros((tq, LANES), f32)
            dcq = jnp.zeros((tq, LANES), f32)
            for hh in range(2):
                mh, qh, pe, l = _fox_scores(hh, p, i, tq, q, k[:kmax], cc, crow[:, :kmax])
                pr = pe / l
                doh = jnp.where(mh, dout, 0.0).astype(bf16)
                dp = _dot(doh, v[:kmax], NT)
                ds = pr * (dp - jnp.sum(pr * dp, axis=1, keepdims=True))
                dsb = ds.astype(bf16)
                dq = dq + _dot(dsb, jnp.where(mh, kf[:kmax], 0.0).astype(bf16)) * scale
                dk_ref[:kmax, :] += _dot(dsb, qh, TN) * scale
                dv_ref[:kmax, :] += _dot(pr.astype(bf16), doh, TN)
                h = 2 * p + hh
                dcq = dcq + jnp.where(lane == h, jnp.sum(ds, axis=1, keepdims=True), 0.0)
                dck_ref[0, 0, :, :kmax] += jnp.where(sub == h, -jnp.sum(ds, axis=0, keepdims=True), 0.0)
            dq_ref[rows, :] = dq
            dcq_ref[0, rows, :] = dcq

    seq = lambda b, p: (b, p)
    return _pcall(
        side, body, name="fox_attn_bwd", grid=(nb, FOX_H // 2),
        in_specs=[pl.BlockSpec((t, LANES), seq), pl.BlockSpec((t, LANES), seq), pl.BlockSpec((t, LANES), lambda b, p: (b, Z0_FV + p)),
                  pl.BlockSpec((t, LANES), lambda b, p: (b, 0)), pl.BlockSpec((1, FOX_H, t), lambda b, p: (b, 0, 0)),
                  pl.BlockSpec((t, LANES), seq)],
        out_specs=[pl.BlockSpec((t, LANES), seq), pl.BlockSpec((t, LANES), seq), pl.BlockSpec((t, LANES), seq),
                   pl.BlockSpec((1, t, LANES), lambda b, p: (p, b, 0)), pl.BlockSpec((1, 1, FOX_H, t), lambda b, p: (p, b, 0, 0))],
        out_shape=[S((n, FOX_W), f32), S((n, FOX_W), f32), S((n, FOX_W), f32), S((4, n, LANES), f32), S((4, nb, FOX_H, t), f32)],
        compiler_params=pltpu.CompilerParams(dimension_semantics=("parallel", "parallel")),
    )(qn, kn, z0, ccol, crow, do)


def _adamw_math(w, g, m, v):
    m = ADAM_B1 * m + (1.0 - ADAM_B1) * g
    v = ADAM_B2 * v + (1.0 - ADAM_B2) * (g * g)
    m_hat = m / (1.0 - ADAM_B1 ** ADAM_STEP)
    v_hat = v / (1.0 - ADAM_B2 ** ADAM_STEP)
    return -ADAM_LR * (m_hat / (jnp.sqrt(v_hat) + ADAM_EPS) + ADAM_WD * w), m, v


_ADAMW_STEPS = 4


def _adamw_big(name, idx, gs, r1s, r2s, ws, ms, vs):
    na = len(ws)

    def body(idx_ref, *refs):
        for k in range(na):
            gm_ref, r1_ref, r2_ref, w_ref, m_ref, v_ref = refs[6 * k:6 * k + 6]
            g_ref, d_ref, nm_ref, nv_ref = refs[6 * na + 4 * k:6 * na + 4 * k + 4]
            g = gm_ref[0].astype(f32) + r1_ref[0].astype(f32)
            for j in range(3):
                g = g + r2_ref[j].astype(f32)
            g = g[:, :w_ref.shape[1]]
            d, nm, nv = _adamw_math(w_ref[...], g, m_ref[...], v_ref[...])
            g_ref[...] = g
            d_ref[...] = d
            nm_ref[...] = nm
            nv_ref[...] = nv

    in_specs, out_specs, out_shape, operands = [], [], [], []
    for ga, r1, r2, w, m, v in zip(gs, r1s, r2s, ws, ms, vs):
        r, wc = w.shape
        c = ga.shape[2]
        tr = r // _ADAMW_STEPS
        row = pl.BlockSpec((tr, wc), lambda i, s: (i, 0))
        in_specs += [pl.BlockSpec((1, tr, c), lambda i, s: (s[0], i, 0)), pl.BlockSpec((1, tr, c), lambda i, s: (s[1], i, 0)),
                     pl.BlockSpec((3, tr, c), lambda i, s: (0, i, 0)), row, row, row]
        out_specs += [row] * 4
        out_shape += [S((r, wc), f32)] * 4
        operands += [ga, r1, r2, w, m, v]
    res = pl.pallas_call(
        body, name=name,
        grid_spec=pltpu.PrefetchScalarGridSpec(num_scalar_prefetch=1, grid=(_ADAMW_STEPS,), in_specs=in_specs, out_specs=out_specs),
        out_shape=out_shape,
        compiler_params=pltpu.CompilerParams(dimension_semantics=("parallel",), vmem_limit_bytes=_VMEM_LIMIT),
    )(idx, *operands)
    return [res[4 * k:4 * k + 4] for k in range(na)]


def _pair_sum(name, idx, gs, r1s):
    na = len(gs)

    def body(idx_ref, *refs):
        for gm_ref, r1_ref, o_ref in zip(refs[:na], refs[na:2 * na], refs[2 * na:]):
            o_ref[0] = (gm_ref[0, 0].astype(f32) + r1_ref[0].astype(f32)).astype(bf16)

    shapes = [ga.shape[1:] for ga in gs]
    res = pl.pallas_call(
        body, name=name,
        grid_spec=pltpu.PrefetchScalarGridSpec(
            num_scalar_prefetch=1, grid=(4,),
            in_specs=[pl.BlockSpec((1, 1, r, c), lambda ch, s: (ch, s[0], 0, 0)) for r, c in shapes]
            + [pl.BlockSpec((1, r, c), lambda ch, s: (ch, 0, 0)) for r, c in shapes],
            out_specs=[pl.BlockSpec((1, r, c), lambda ch, s: (ch, 0, 0)) for r, c in shapes]),
        out_shape=[S((4, r, c), bf16) for r, c in shapes],
        compiler_params=pltpu.CompilerParams(dimension_semantics=("parallel",), vmem_limit_bytes=_VMEM_LIMIT),
    )(idx, *[ga.reshape(4, 2, *ga.shape[1:]) for ga in gs], *r1s)
    return list(res)


def _adamw_small(parts, w, m, v, own_mask, own_g):
    def body(p_ref, w_ref, m_ref, v_ref, mask_ref, og_ref, g_ref, d_ref, nm_ref, nv_ref):
        g = p_ref[0]
        for k in range(1, NDEV):
            g = g + p_ref[k]
        g_ref[...] = g
        ge = jnp.where(mask_ref[...] > 0.5, og_ref[...], g)
        d, nm, nv = _adamw_math(w_ref[...], ge, m_ref[...], v_ref[...])
        d_ref[...] = d
        nm_ref[...] = nm
        nv_ref[...] = nv

    return pl.pallas_call(body, name="adamw_small", out_shape=[S(w.shape, f32)] * 4)(parts, w, m, v, own_mask, own_g)


def _sum_parts(parts):
    def body(p_ref, g_ref):
        g = p_ref[0]
        for k in range(1, NDEV):
            g = g + p_ref[k]
        g_ref[...] = g

    return pl.pallas_call(body, name="sum_parts", out_shape=S(parts.shape[1:], f32))(parts)


def _me():
    return lax.axis_index("x"), lax.axis_index("y"), lax.axis_index("c")


def _hbm_specs(n):
    return [pl.BlockSpec(memory_space=pl.ANY)] * n


def _allgather(name, xs):
    na = len(xs)

    def body(*refs):
        x_refs, out_refs = refs[:na], refs[na:2 * na]
        send_sems, recv_sems, local_sems = refs[2 * na:]
        mx, my, mc = _me()
        me, sib = (mx, my, mc), (mx, my, 1 - mc)
        chips = [(1 - mx, my), (mx, 1 - my), (1 - mx, 1 - my)]

        def slab(a, px, py, pc):
            return out_refs[a].at[4 * px + 2 * py + pc]

        def copy(a, k, block, to, own=False):
            return pltpu.make_async_remote_copy(
                src_ref=x_refs[a] if own else slab(a, *block), dst_ref=slab(a, *block),
                send_sem=send_sems.at[7 * a + k], recv_sem=recv_sems.at[7 * a + k], device_id=to, device_id_type=MESH)

        mine = [pltpu.make_async_copy(x_refs[a], slab(a, *me), local_sems.at[a]) for a in range(na)]
        first = []
        for a in range(na):
            mine[a].start()
            first += [copy(a, 0, me, sib, own=True)] + [copy(a, 1 + j, me, (*chip, mc), own=True) for j, chip in enumerate(chips)]
        for cp in first:
            cp.start()
        passed = []
        for j, chip in enumerate(chips):
            for a in range(na):
                copy(a, 1 + j, (*chip, mc), me).wait_recv()
                passed.append(copy(a, 4 + j, (*chip, mc), sib))
                passed[-1].start()
        for a in range(na):
            copy(a, 0, sib, me).wait_recv()
            for j, chip in enumerate(chips):
                copy(a, 4 + j, (*chip, 1 - mc), me).wait_recv()
        for cp in first + passed:
            cp.wait_send()
        for cp in mine:
            cp.wait()

    return pl.pallas_call(
        body, name=name, out_shape=[S((NDEV,) + x.shape, x.dtype) for x in xs],
        in_specs=_hbm_specs(na), out_specs=_hbm_specs(na),
        scratch_shapes=[pltpu.SemaphoreType.DMA((7 * na,)), pltpu.SemaphoreType.DMA((7 * na,)), pltpu.SemaphoreType.DMA((na,))],
    )(*xs)


def _rs_sibling(gs):
    na = len(gs)

    def body(*refs):
        g_refs, out_refs, send_sems, recv_sems = refs[:na], refs[na:2 * na], refs[2 * na], refs[2 * na + 1]
        mx, my, mc = _me()
        cps = [pltpu.make_async_remote_copy(
            src_ref=g_refs[a].at[2 * ch + 1 - mc], dst_ref=out_refs[a].at[ch], send_sem=send_sems.at[4 * a + ch],
            recv_sem=recv_sems.at[4 * a + ch], device_id=(mx, my, 1 - mc), device_id_type=MESH)
            for a in range(na) for ch in range(4)]
        for cp in cps:
            cp.start()
        for cp in cps:
            cp.wait_recv()
        for cp in cps:
            cp.wait_send()

    return pl.pallas_call(
        body, name="rs_sibling", out_shape=[S((4,) + g.shape[1:], g.dtype) for g in gs],
        in_specs=_hbm_specs(na), out_specs=_hbm_specs(na),
        scratch_shapes=[pltpu.SemaphoreType.DMA((4 * na,)), pltpu.SemaphoreType.DMA((4 * na,))],
    )(*gs)


def _side_allgather(xs):
    na = len(xs)

    def mk(x_refs, out_refs, sems):
        send_sems, recv_sems, local_sems = sems
        mx, my, mc = _me()
        me, sib = (mx, my, mc), (mx, my, 1 - mc)
        chips = [(1 - mx, my), (mx, 1 - my), (1 - mx, 1 - my)]

        def slab(a, px, py, pc):
            return out_refs[a].at[4 * px + 2 * py + pc]

        def copy(a, k, block, to, own=False):
            return pltpu.make_async_remote_copy(
                src_ref=x_refs[a] if own else slab(a, *block), dst_ref=slab(a, *block),
                send_sem=send_sems.at[7 * a + k], recv_sem=recv_sems.at[7 * a + k], device_id=to, device_id_type=MESH)

        mine = [pltpu.make_async_copy(x_refs[a], slab(a, *me), local_sems.at[a]) for a in range(na)]
        first = []
        for a in range(na):
            first += [copy(a, 0, me, sib, own=True)] + [copy(a, 1 + j, me, (*chip, mc), own=True) for j, chip in enumerate(chips)]
        return me, sib, chips, mc, copy, mine, first

    def start(x_refs, out_refs, sems):
        *_, mine, first = mk(x_refs, out_refs, sems)
        for cp in mine + first:
            cp.start()

    def finish(x_refs, out_refs, sems):
        me, sib, chips, mc, copy, mine, first = mk(x_refs, out_refs, sems)
        passed = []
        for j, chip in enumerate(chips):
            for a in range(na):
                copy(a, 1 + j, (*chip, mc), me).wait_recv()
                passed.append(copy(a, 4 + j, (*chip, mc), sib))
                passed[-1].start()
        for a in range(na):
            copy(a, 0, sib, me).wait_recv()
            for j, chip in enumerate(chips):
                copy(a, 4 + j, (*chip, 1 - mc), me).wait_recv()
        for cp in first + passed:
            cp.wait_send()
        for cp in mine:
            cp.wait()

    scratch = [pltpu.SemaphoreType.DMA((7 * na,)), pltpu.SemaphoreType.DMA((7 * na,)), pltpu.SemaphoreType.DMA((na,))]
    return _Side(xs, [S((NDEV,) + x.shape, x.dtype) for x in xs], scratch, start, finish)


def _side_exchange(arrs, nslot, out_slots, route):
    na = len(arrs)

    def copies(in_refs, out_refs, sems):
        send_sems, recv_sems = sems
        return [pltpu.make_async_remote_copy(
            src_ref=in_refs[a].at[src], dst_ref=out_refs[a].at[k], send_sem=send_sems.at[nslot * a + k],
            recv_sem=recv_sems.at[nslot * a + k], device_id=to, device_id_type=MESH)
            for a in range(na) for k, (src, to) in enumerate(route(*_me()))]

    def start(in_refs, out_refs, sems):
        for cp in copies(in_refs, out_refs, sems):
            cp.start()

    def finish(in_refs, out_refs, sems):
        cps = copies(in_refs, out_refs, sems)
        for cp in cps:
            cp.wait_recv()
        for cp in cps:
            cp.wait_send()

    scratch = [pltpu.SemaphoreType.DMA((nslot * na,)), pltpu.SemaphoreType.DMA((nslot * na,))]
    return _Side(arrs, [S((out_slots,) + x.shape[1:], x.dtype) for x in arrs], scratch, start, finish)


def _side_rs_sibling(gs):
    return _side_exchange(gs, 4, 4, lambda mx, my, mc: [(2 * ch + 1 - mc, (mx, my, 1 - mc)) for ch in range(4)])


def _side_rs_chips(ps):
    return _side_exchange(ps, 3, 3, lambda mx, my, mc: [(2 * cx + cy, (cx, cy, mc)) for cx, cy in
                                                        [(1 - mx, my), (mx, 1 - my), (1 - mx, 1 - my)]])


_COLW = 512
_COL_NAMES = ("l0_w_in", "l0_w_ff1", "l1_w_in", "l1_w_ff1")
_ROW_NAMES = ("l0_w_out", "l0_w_ff2", "l1_w_out", "l1_w_ff2")
_BIG_NAMES = _COL_NAMES + _ROW_NAMES


_W_IN0_RUNS = ((0, 1536, 0), (1536, 1544, 3584), (1544, 3080, 1536), (3080, 3088, 3592), (3088, 3600, 3072))


def _w_in0_slabs(g, ncols):
    slabs = []
    for d in range(NDEV):
        lo, hi = d * ncols, (d + 1) * ncols
        parts = [g[:, r + max(lo, a) - a:r + min(hi, b) - a] for a, b, r in _W_IN0_RUNS if max(lo, a) < min(hi, b)]
        slabs.append(jnp.concatenate(parts + [jnp.zeros((D, _COLW - ncols), g.dtype)], axis=1))
    return jnp.stack(slabs)


def _w_in0_regrouped(gathered, ncols):
    parts = []
    for a, b, _ in sorted(_W_IN0_RUNS, key=lambda run: run[2]):
        for d in range(a // ncols, (b - 1) // ncols + 1):
            parts.append(gathered[d, :, max(a, d * ncols) - d * ncols:min(b, (d + 1) * ncols) - d * ncols])
    used = sum(p.shape[1] for p in parts)
    return jnp.concatenate(parts + [jnp.zeros((D, ZW0 - used), gathered.dtype)], axis=1)


def _sq(t):
    return t * t


def _epi_res_norm(acc, res, gain):
    y = acc + res
    h = _f_norm(y, gain)[0]
    return y, h, h


_RES_NORM_OUTS = [("tile", f32), ("tile", bf16), ("tile_t", bf16)]


def _epi_norm_bwd(acc, x, dres, gain):
    _, vjp = jax.vjp(lambda xx, gg: _f_norm(xx, gg)[0], x, gain)
    dx, dgain = vjp(acc)
    dx = dx + dres
    return dx, dgain, dx


_NORM_BWD_OUTS = [("tile", f32), ("rows", f32), ("tile_t", bf16)]


def _epi_loss(acc, res, tgt):
    e = acc + res - tgt
    dy = e * (1.0 / D)
    return dy, dy, jnp.sum(e * e, axis=0, keepdims=True)


_LOSS_OUTS = [("tile", f32), ("tile_t", bf16), ("rows", f32)]


def _mlp_fwd(tag, x, h, w1, w2, epi, extra, consts, outs):
    a = _mm(f"{tag}_ff1", h, w1, NN, bf16, epi=lambda acc: jnp.maximum(acc, 0.0))
    return _mm(f"{tag}_ff2", a, w2, NN, f32, a_fn=_sq, epi=epi, extra=(x,) + tuple(extra), consts=consts, outs=outs), a


def _mlp_bwd(tag, x, gain, w1, w2, ht, a, dy, dyt, side=None):
    da = _mm(f"{tag}_ff2_dx", dy, w2, NT, bf16, epi=lambda acc, av: acc * 2.0 * av.astype(f32), extra=(a,), side=side)
    if side is not None:
        da, side_res = da
    dw2 = _mm(f"{tag}_ff2_dw", dyt, a, NN, bf16, b_fn=_sq, out_t=True)
    res = _mm(f"{tag}_ff1_dx", da, w1, NT, f32, epi=_epi_norm_bwd, extra=(x, dy), consts=(gain,), outs=_NORM_BWD_OUTS)
    dw1 = _mm(f"{tag}_ff1_dw", ht, da, NN, bf16, slab=True)
    return (res, dw1, dw2) if side is None else ((res, dw1, dw2), side_res)


def _row(v):
    return v.reshape(1, -1).astype(f32)


_L0_REST = ("l0_w_ff1", "l0_w_out")
_L1_MIX = ("l0_w_ff2", "l1_w_in", "l1_w_out")
_L1_FFN = ("l1_w_ff1", "l1_w_ff2")
_GRAD_A = ("l1_w_ff1", "l1_w_ff2", "l1_w_out", "l1_w_in")
_GRAD_B = ("l0_w_ff1", "l0_w_ff2", "l0_w_out")
_IN_PLACE = ("l0_w_ff1", "l1_w_in", "l1_w_ff1")


def _train(x, tgt, args, mom, var, t):
    n = x.shape[0]
    nchunk = t // CH
    mx, my, mc = _me()
    dev, chip = 4 * mx + 2 * my + mc, 2 * mx + my
    core_idx = jnp.reshape(mc, (1,)).astype(jnp.int32)
    own_idx = jnp.stack([dev, chip]).astype(jnp.int32)
    ncols = {nm: args[nm].shape[1] for nm in _COL_NAMES}
    conv_cols = args["l0_gdn_conv"].shape[1]
    g, big, w = {}, {}, {}

    def send(nm):
        a = args[nm].astype(bf16)
        return jnp.pad(a, ((0, 0), (0, _COLW - ncols[nm]))) if nm in _COL_NAMES else a

    def take(names, gathered):
        for nm, arr in zip(names, gathered):
            w[nm] = arr if nm in _IN_PLACE else arr.reshape(-1, D)

    def by_dev(nm, ga):
        return ga if nm in _COL_NAMES else ga.reshape(NDEV, -1, D)

    def pair(names, gs, r1s):
        return _pair_sum(f"rs_pair_sum_{names[0]}_group", core_idx, gs, r1s)

    def adam(names, gs, r1s, r2s):
        res = _adamw_big(f"adamw_{names[0]}_group", own_idx, gs, r1s, r2s, *[[src[nm] for nm in names] for src in (args, mom, var)])
        big.update(zip(names, res))


    li = jnp.arange(FOX_W)
    pm = jnp.where((li[:, None] // FOX_D) == (li[None, :] // FOX_D), 1.0 / FOX_D, 0.0).astype(f32)
    lane_head = jnp.arange(GDN_W) // HD
    sel = lambda first_lane: (jnp.arange(LANES)[:, None] == (first_lane + lane_head)[None, :]).astype(f32)
    e_beta, e_alpha = sel(FOX_H), sel(FOX_H + GDN_H)
    alog_b, dt_b = _row(jnp.repeat(args["l0_gdn_A_log"], HD)), _row(jnp.repeat(args["l0_gdn_dt_bias"], HD))
    gq_t, gk_t = _row(jnp.tile(args["l0_fox_q_norm"], FOX_H)), _row(jnp.tile(args["l0_fox_k_norm"], FOX_H))
    on0_t, on1_t = _row(jnp.tile(args["l0_gdn_o_norm"], GDN_H)), _row(jnp.tile(args["l1_hgrn_o_norm"], HG_H))
    fbias = jnp.pad(_row(args["l0_fox_f_bias"]), ((0, 0), (0, LANES - FOX_H)))
    g0m, g0f, g1m, g1f = (_row(args[k]) for k in ("l0_mix_norm", "l0_ffn_norm", "l1_mix_norm", "l1_ffn_norm"))
    lbl = args["hgrn_lb_logits"].astype(f32)

    first = [send("l0_w_in"), jnp.pad(args["l0_gdn_conv"], ((0, 4), (0, LANES * 2 - conv_cols)))]
    (h0, h0t), first = _tok_fwd("l0_mix_norm", _f_norm, [(x, D, 0)], [(g0m, None, 0)], [(D, bf16)], TOK_ROWS, also_t=(0,),
                                side=_side_allgather(first))
    w_in0 = _w_in0_regrouped(first[0], ncols["l0_w_in"])
    wconv = first[1][:, :4, :conv_cols].transpose(1, 0, 2).reshape(4, NDEV * conv_cols)
    z0 = _mm("l0_in", h0, w_in0, NN, f32)
    qk_rows = [(z0, 2 * FOX_W, 0)]
    qk_consts = [(gq_t, None, 0), (gk_t, None, 0), (pm, None, 0)]
    qn, kn = _tok_fwd("fox_pre", _f_foxpre, qk_rows, qk_consts, [(FOX_W, f32)] * 2, TOK_ROWS)
    ccol, crow = _fox_gate_fwd(z0, fbias, t)
    fox_o, got = _fox_attn_fwd(qn, kn, z0, ccol, crow, t, side=_side_allgather([send(nm) for nm in _L0_REST]))
    take(_L0_REST, got)
    conv_rows, conv_consts = [(z0, LANES, Z0_GQKV)], [(wconv, LANES, 0)]
    (qkv,) = _tok_fwd("gdn_conv", _f_conv, conv_rows, conv_consts, [(LANES, f32)], t, ncb=12, with_j=True)
    gate_consts = [(e_beta, None, 0), (e_alpha, None, 0), (alog_b, None, 0), (dt_b, None, 0)]
    intra_rows = [(qkv, GDN_W, 0), (qkv, GDN_W, 1), (qkv, GDN_W, 2), (z0, LANES, Z0_SMALL)]
    intra, got = _tok_fwd("gdn_intra", _f_gdn_gated, intra_rows, gate_consts, [(GDN_W, f32)] * 7, 2 * CH,
                          side=_side_allgather([send(nm) for nm in _L1_MIX]))
    take(_L1_MIX, got)
    inter_rows = [(a, 0, 1) for a in intra[:6]]
    full_rows = inter_rows + [(fox_o, 0, 1), (z0, Z0_GG // 4, 1)]
    (cat0, gdn_hist), got = _scan_fwd("gdn_scan", _f_gdn_full, full_rows, GDN_H, nchunk, consts=(on0_t,), out_dtype=bf16, out_k=2,
                                      side=_side_allgather([send(nm) for nm in _L1_FFN[1:]]))
    take(_L1_FFN[1:], got)
    x1, hf0, hf0t = _mm("l0_out", cat0, w["l0_w_out"], NN, f32, epi=_epi_res_norm, extra=(x,), consts=(g0f,), outs=_RES_NORM_OUTS)
    (x2, h1, h1t), a0 = _mlp_fwd("l0", x1, hf0, w["l0_w_ff1"], w["l0_w_ff2"], _epi_res_norm, (), (g1m,), _RES_NORM_OUTS)

    z1 = _mm("l1_in", h1, w["l1_w_in"], NN, f32)
    (cat1, hg_hist), got = _scan_fwd("hgrn_scan", _f_hgrn_full, [(z1, 0, 4)], HG_H, nchunk, consts=(lbl, on1_t), out_dtype=bf16,
                                     side=_side_allgather([send(nm) for nm in _L1_FFN[:1]]))
    take(_L1_FFN[:1], got)
    x3, hf1, hf1t = _mm("l1_out", cat1, w["l1_w_out"], NN, f32, epi=_epi_res_norm, extra=(x2,), consts=(g1f,), outs=_RES_NORM_OUTS)
    (dy, dyt, loss_row), a1 = _mlp_fwd("l1", x3, hf1, w["l1_w_ff1"], w["l1_w_ff2"], _epi_loss, (tgt,), (), _LOSS_OUTS)

    (dx3, g["l1_ffn_norm"], dx3t), ga_ff1, ga_ff2 = _mlp_bwd("l1", x3, g1f, w["l1_w_ff1"], w["l1_w_ff2"], hf1t, a1, dy, dyt)
    dcat1 = _mm("l1_out_dx", dx3, w["l1_w_out"], NT, f32)
    ga_out = _mm("l1_out_dw", dx3t, cat1, NN, bf16, out_t=True)
    dz1, dlbl, don1 = _scan_bwd("hgrn_scan_bwd", _f_hgrn_full, [(z1, 0, 4)], hg_hist, dcat1, HG_H, nchunk, dtypes=[bf16],
                                consts=(lbl, on1_t))
    dx2, g["l1_mix_norm"], dx2t = _mm("l1_in_dx", dz1, w["l1_w_in"], NT, f32, epi=_epi_norm_bwd, extra=(x2, dx3), consts=(g1m,),
                                      outs=_NORM_BWD_OUTS)
    ga_in = _mm("l1_in_dw", h1t, dz1, NN, bf16, slab=True)
    g["l1_hgrn_o_norm"] = don1.reshape(HG_H, HD).sum(0)
    g["hgrn_lb_logits"] = dlbl
    gs_a = [by_dev(nm, ga) for nm, ga in zip(_GRAD_A, (ga_ff1, ga_ff2, ga_out, ga_in))]

    (dx1, g["l0_ffn_norm"], dx1t), gb_ff1, gb_ff2 = _mlp_bwd("l0", x1, g0f, w["l0_w_ff1"], w["l0_w_ff2"], hf0t, a0, dx2, dx2t)
    dcat0 = _mm("l0_out_dx", dx1, w["l0_w_out"], NT, f32)
    gb_out = _mm("l0_out_dw", dx1t, cat0, NN, bf16, out_t=True)
    gs_b = [by_dev(nm, ga) for nm, ga in zip(_GRAD_B, (gb_ff1, gb_ff2, gb_out))]
    (*dinter, dfox_o, dgg, don0), r1_a = _scan_bwd("gdn_scan_bwd", _f_gdn_full, full_rows, gdn_hist, dcat0, GDN_H, nchunk,
                                                   consts=(on0_t,), out_k=2, dtypes=[f32] * 7 + [bf16], cps=1,
                                                   side=_side_rs_sibling(gs_a))
    pairs_a = pair(_GRAD_A, gs_a, r1_a)
    (dqn, dkn, dfv, dcq, dck), got = _fox_attn_bwd(qn, kn, z0, ccol, crow, dfox_o, t,
                                                   side=_join_sides([_side_rs_chips(pairs_a), _side_rs_sibling(gs_b)]))
    r2_a, r1_b = got[:len(_GRAD_A)], got[len(_GRAD_A):]
    adam(_GRAD_A, gs_a, r1_a, r2_a)
    pairs_b = pair(_GRAD_B, gs_b, r1_b)
    (dqkv, dzs_g, dalog_b, ddt_b), r2_b = _tok_bwd(
        "gdn_intra_bwd", _f_gdn_gated_given, intra_rows + [(intra[6], GDN_W, 0)], gate_consts, [(a, GDN_W) for a in dinter],
        2 * CH, [0, 1, 2, 3], [2, 3], ncat=3, side=_side_rs_chips(pairs_b))
    adam(_GRAD_B, gs_b, r1_b, r2_b)
    dz0, dgq_t, dgk_t = _tok_bwd("fox_pre_bwd", _f_foxpre, qk_rows, qk_consts, [(dqn, FOX_W), (dkn, FOX_W)], TOK_ROWS, [0], [0, 1],
                                 drow_dtype=bf16, into=(None, ZW0, 0))
    dz0, dwconv = _tok_bwd("gdn_conv_bwd", _f_conv, conv_rows, conv_consts, [(dqkv, LANES)], t, [0], [0], ncb=12, with_j=True,
                           drow_dtype=bf16, into=(dz0, ZW0, Z0_GQKV * LANES))
    dzs_f, dfb = _fox_gate_bwd(z0, fbias, dcq, dck, t)
    dz0 = dz0.at[:, Z0_FV * LANES:Z0_GQKV * LANES].set(dfv.astype(bf16)).at[:, Z0_GG * LANES:Z0_SMALL * LANES].set(dgg)
    dz0 = dz0.at[:, Z0_SMALL * LANES:].set(jnp.pad((dzs_g + dzs_f).astype(bf16), ((0, 0), (0, ZW0 - (Z0_SMALL + 1) * LANES))))
    gs_c = [_w_in0_slabs(_mm("l0_in_dw", h0t, dz0, NN, bf16), ncols["l0_w_in"])]
    r1_c = _rs_sibling(gs_c)
    pairs_c = pair(("l0_w_in",), gs_c, r1_c)
    (dx, g["l0_mix_norm"]), r2_c = _mm("l0_in_dx", dz0, w_in0, NT, f32, epi=lambda *a: _epi_norm_bwd(*a)[:2], extra=(x, dx1),
                                       consts=(g0m,), outs=_NORM_BWD_OUTS[:2], side=_side_rs_chips(pairs_c))
    adam(("l0_w_in",), gs_c, r1_c, r2_c)
    g["l0_fox_q_norm"] = dgq_t.reshape(FOX_H, FOX_D).sum(0)
    g["l0_fox_k_norm"] = dgk_t.reshape(FOX_H, FOX_D).sum(0)
    g["l0_fox_f_bias"] = dfb[0, :FOX_H]
    g["l0_gdn_conv"] = dwconv
    g["l0_gdn_A_log"] = dalog_b.reshape(GDN_H, HD).sum(1)
    g["l0_gdn_dt_bias"] = ddt_b.reshape(GDN_H, HD).sum(1)
    g["l0_gdn_o_norm"] = don0.reshape(GDN_H, HD).sum(0)
    return loss_row, dx, g, big


_NAMES = ("l0_mix_norm", "l0_w_in", "l0_fox_q_norm", "l0_fox_k_norm", "l0_fox_f_bias", "l0_gdn_conv", "l0_gdn_A_log",
          "l0_gdn_dt_bias", "l0_gdn_o_norm", "l0_w_out", "l0_ffn_norm", "l0_w_ff1", "l0_w_ff2", "l1_mix_norm", "l1_w_in",
          "l1_hgrn_o_norm", "l1_w_out", "l1_ffn_norm", "l1_w_ff1", "l1_w_ff2", "hgrn_lb_logits")
_SMALL_NAMES = tuple(nm for nm in _NAMES if nm not in _BIG_NAMES)
_SMALL_ROWS = 16


def _pack_small(vals):
    flat = jnp.concatenate([vals[nm].reshape(-1).astype(f32) for nm in _SMALL_NAMES])
    return jnp.pad(flat, (0, _SMALL_ROWS * D - flat.shape[0])).reshape(_SMALL_ROWS, D)


def _unpack_small(packed, shapes):
    flat = packed.reshape(-1)
    out, off = {}, 0
    for nm in _SMALL_NAMES:
        size = 1
        for s in shapes[nm]:
            size *= s
        out[nm] = flat[off:off + size].reshape(shapes[nm])
        off += size
    return out, off


def kernel(x, l0_mix_norm, l0_w_in, l0_fox_q_norm, l0_fox_k_norm, l0_fox_f_bias, l0_gdn_conv, l0_gdn_A_log, l0_gdn_dt_bias, l0_gdn_o_norm, l0_w_out, l0_ffn_norm, l0_w_ff1, l0_w_ff2, l1_mix_norm, l1_w_in, l1_hgrn_o_norm, l1_w_out, l1_ffn_norm, l1_w_ff1, l1_w_ff2, hgrn_lb_logits, loss_target, m_l0_mix_norm, m_l0_w_in, m_l0_fox_q_norm, m_l0_fox_k_norm, m_l0_fox_f_bias, m_l0_gdn_conv, m_l0_gdn_A_log, m_l0_gdn_dt_bias, m_l0_gdn_o_norm, m_l0_w_out, m_l0_ffn_norm, m_l0_w_ff1, m_l0_w_ff2, m_l1_mix_norm, m_l1_w_in, m_l1_hgrn_o_norm, m_l1_w_out, m_l1_ffn_norm, m_l1_w_ff1, m_l1_w_ff2, m_hgrn_lb_logits, v_l0_mix_norm, v_l0_w_in, v_l0_fox_q_norm, v_l0_fox_k_norm, v_l0_fox_f_bias, v_l0_gdn_conv, v_l0_gdn_A_log, v_l0_gdn_dt_bias, v_l0_gdn_o_norm, v_l0_w_out, v_l0_ffn_norm, v_l0_w_ff1, v_l0_w_ff2, v_l1_mix_norm, v_l1_w_in, v_l1_hgrn_o_norm, v_l1_w_out, v_l1_ffn_norm, v_l1_w_ff1, v_l1_w_ff2, v_hgrn_lb_logits):
    args = dict(zip(_NAMES, (l0_mix_norm, l0_w_in, l0_fox_q_norm, l0_fox_k_norm, l0_fox_f_bias, l0_gdn_conv, l0_gdn_A_log, l0_gdn_dt_bias, l0_gdn_o_norm, l0_w_out, l0_ffn_norm, l0_w_ff1, l0_w_ff2, l1_mix_norm, l1_w_in, l1_hgrn_o_norm, l1_w_out, l1_ffn_norm, l1_w_ff1, l1_w_ff2, hgrn_lb_logits)))
    mom = dict(zip(_NAMES, (m_l0_mix_norm, m_l0_w_in, m_l0_fox_q_norm, m_l0_fox_k_norm, m_l0_fox_f_bias, m_l0_gdn_conv, m_l0_gdn_A_log, m_l0_gdn_dt_bias, m_l0_gdn_o_norm, m_l0_w_out, m_l0_ffn_norm, m_l0_w_ff1, m_l0_w_ff2, m_l1_mix_norm, m_l1_w_in, m_l1_hgrn_o_norm, m_l1_w_out, m_l1_ffn_norm, m_l1_w_ff1, m_l1_w_ff2, m_hgrn_lb_logits)))
    var = dict(zip(_NAMES, (v_l0_mix_norm, v_l0_w_in, v_l0_fox_q_norm, v_l0_fox_k_norm, v_l0_fox_f_bias, v_l0_gdn_conv, v_l0_gdn_A_log, v_l0_gdn_dt_bias, v_l0_gdn_o_norm, v_l0_w_out, v_l0_ffn_norm, v_l0_w_ff1, v_l0_w_ff2, v_l1_mix_norm, v_l1_w_in, v_l1_hgrn_o_norm, v_l1_w_out, v_l1_ffn_norm, v_l1_w_ff1, v_l1_w_ff2, v_hgrn_lb_logits)))
    nb, t, _ = x.shape
    dev = 4 * lax.axis_index("x") + 2 * lax.axis_index("y") + lax.axis_index("c")
    conv_cols = l0_gdn_conv.shape[1]
    loss_row, dx, g, big = _train(x.reshape(nb * t, D), loss_target.reshape(nb * t, D), args, mom, var, t)

    shapes = {nm: args[nm].shape for nm in _SMALL_NAMES}
    gsm = dict(g)
    gsm["l0_gdn_conv"] = jnp.zeros(shapes["l0_gdn_conv"], f32)
    packed = _pack_small(gsm)
    _, used = _unpack_small(packed, shapes)
    flat_extra = jnp.concatenate([jnp.sum(loss_row).reshape(1), g["l0_gdn_conv"].reshape(-1)])
    packed = packed.reshape(-1).at[used:used + flat_extra.shape[0]].set(flat_extra).reshape(_SMALL_ROWS, D)
    (parts,) = _allgather("ag_small", [packed])
    total = _sum_parts(parts).reshape(-1)
    loss = 0.5 * total[used] / D
    conv_g_full = total[used + 1:used + 1 + 4 * NDEV * conv_cols].reshape(4, NDEV * conv_cols)
    conv_g = lax.dynamic_slice(conv_g_full, (0, dev * conv_cols), (4, conv_cols))
    own_vals = {nm: jnp.zeros(shapes[nm], f32) for nm in _SMALL_NAMES}
    own_vals["l0_gdn_conv"] = conv_g
    own_mask = {nm: jnp.zeros(shapes[nm], f32) for nm in _SMALL_NAMES}
    own_mask["l0_gdn_conv"] = jnp.ones(shapes["l0_gdn_conv"], f32)
    small = _adamw_small(parts, _pack_small(args), _pack_small(mom), _pack_small(var), _pack_small(own_mask), _pack_small(own_vals))
    small = [_unpack_small(a, shapes)[0] for a in small]
    small[0]["l0_gdn_conv"] = conv_g

    outs = [loss, dx.reshape(nb, t, D)]
    for k in range(4):
        outs += [big[nm][k] if nm in _BIG_NAMES else small[k][nm] for nm in _NAMES]
    return tuple(outs)
```

```python
import functools

import jax
import jax.numpy as jnp
from jax import lax
from jax.experimental import pallas as pl
from jax.experimental.pallas import tpu as pltpu

f32, bf16 = jnp.float32, jnp.bfloat16
NN = (((1,), (0,)), ((), ()))
NT = (((1,), (1,)), ((), ()))
TN = (((0,), (0,)), ((), ()))
MESH = pl.DeviceIdType.MESH
S = jax.ShapeDtypeStruct

EPS = 1e-6
D = 1024
LANES = 128
FOX_H, FOX_D, FOX_W = 8, 64, 512
GDN_H, HD, GDN_W = 4, 128, 512
HG_H = 8
CH = 64
TOK_ROWS = 256
ZW0 = 3840
NDEV = 8
ADAM_LR, ADAM_B1, ADAM_B2, ADAM_EPS, ADAM_WD, ADAM_STEP = 0.001, 0.9, 0.999, 1e-08, 0.01, 10

Z0_FV, Z0_GQKV, Z0_GG, Z0_SMALL = 8, 12, 24, 28


def _dot(a, b, dims=NN):
    return lax.dot_general(a, b, dims, preferred_element_type=f32)


def _iota2(shape, axis):
    return lax.broadcasted_iota(jnp.int32, shape, axis)


def _split3(x):
    x1 = x.astype(bf16)
    r = x - x1.astype(f32)
    x2 = r.astype(bf16)
    return x1, x2, (r - x2.astype(f32)).astype(bf16)


def _dot_sel(a, b, dims=NN, exact_lhs=False):
    if exact_lhs:
        return sum(_dot(a.astype(bf16), piece, dims) for piece in _split3(b))
    return sum(_dot(piece, b.astype(bf16), dims) for piece in _split3(a))


@jax.custom_vjp
def _sel_rhs(a, b):
    return _dot_sel(a, b)


_sel_rhs.defvjp(lambda a, b: (_dot_sel(a, b), b), lambda b, g: (_dot_sel(g, b, NT), jnp.zeros_like(b)))


@jax.custom_vjp
def _sel_lhs(a, x):
    return _dot_sel(a, x, exact_lhs=True)


_sel_lhs.defvjp(lambda a, x: (_dot_sel(a, x, exact_lhs=True), a), lambda a, g: (jnp.zeros_like(a), _dot_sel(a, g, TN, exact_lhs=True)))


class _Side:
    def __init__(self, ins, out_shapes, scratch, start, finish):
        self.ins, self.out_shapes, self.scratch, self.start, self.finish = list(ins), list(out_shapes), list(scratch), start, finish


def _join_sides(sides):
    def split(refs, counts):
        out, off = [], 0
        for c in counts:
            out.append(refs[off:off + c])
            off += c
        return out

    ni, no, ns = ([len(getattr(sd, a)) for sd in sides] for a in ("ins", "out_shapes", "scratch"))

    def run(which):
        def go(ins, outs, sems):
            for sd, i, o, c in zip(sides, split(ins, ni), split(outs, no), split(sems, ns)):
                getattr(sd, which)(i, o, c)
        return go

    return _Side(sum((sd.ins for sd in sides), []), sum((sd.out_shapes for sd in sides), []),
                 sum((sd.scratch for sd in sides), []), run("start"), run("finish"))


def _pcall(side, body, *, name, grid, in_specs, out_specs, out_shape, scratch_shapes=(), compiler_params=None, aliases=None):
    if side is None:
        return pl.pallas_call(body, name=name, grid=grid, in_specs=in_specs, out_specs=out_specs, out_shape=out_shape,
                              scratch_shapes=scratch_shapes, compiler_params=compiler_params, input_output_aliases=aliases or {})
    assert not aliases
    single = not isinstance(out_shape, (list, tuple))
    ospecs, oshape = ([out_specs], [out_shape]) if single else (list(out_specs), list(out_shape))
    nin, nout, nscr = len(in_specs), len(ospecs), len(scratch_shapes)
    si, so = len(side.ins), len(side.out_shapes)

    def wrapped(*refs):
        o0 = nin + si
        c0 = o0 + nout + so
        sins, souts, ssems = refs[nin:o0], refs[o0 + nout:c0], refs[c0 + nscr:]
        ids = [pl.program_id(a) for a in range(len(grid))]
        first = functools.reduce(jnp.logical_and, [i == 0 for i in ids])
        last = functools.reduce(jnp.logical_and, [i == g - 1 for i, g in zip(ids, grid)])

        @pl.when(first)
        def _():
            side.start(sins, souts, ssems)

        body(*refs[:nin], *refs[o0:o0 + nout], *refs[c0:c0 + nscr])

        @pl.when(last)
        def _():
            side.finish(sins, souts, ssems)

    call = pl.pallas_call(
        wrapped, name=name, grid=grid, in_specs=list(in_specs) + _hbm_specs(si), out_specs=ospecs + _hbm_specs(so),
        out_shape=oshape + side.out_shapes, scratch_shapes=list(scratch_shapes) + side.scratch,
        compiler_params=pltpu.CompilerParams(dimension_semantics=("arbitrary",) * len(grid),
                                             vmem_limit_bytes=getattr(compiler_params, "vmem_limit_bytes", None)))

    def run(*args):
        res = call(*args, *side.ins)
        return (res[0] if single else list(res[:nout])), list(res[nout:])

    return run


def _tok_specs(rows, consts, tm):
    specs = []
    for (_, w, base) in rows:
        specs.append(pl.BlockSpec((tm, w), functools.partial(lambda j, i, b: (i, b + j), b=base)))
    for (arr, w, base) in consts:
        if w is None:
            specs.append(pl.BlockSpec(arr.shape, lambda j, i: (0, 0)))
        else:
            specs.append(pl.BlockSpec((arr.shape[0], w), functools.partial(lambda j, i, b: (0, b + j), b=base)))
    return specs


def _tok_fwd(name, f, rows, consts, outs, tm, ncb=1, with_j=False, also_t=(), side=None):
    n = rows[0][0].shape[0]
    nin = len(rows) + len(consts)
    nout = len(outs)

    def body(*refs):
        ins = [r[...] for r in refs[:nin]]
        vals = f(pl.program_id(0), *ins) if with_j else f(*ins)
        for r, v in zip(refs[nin:nin + nout], vals):
            r[...] = v.astype(r.dtype)
        for r, k in zip(refs[nin + nout:], also_t):
            r[...] = vals[k].T.astype(r.dtype)

    return _pcall(
        side, body, name=name, grid=(ncb, n // tm),
        in_specs=_tok_specs(rows, consts, tm),
        out_specs=[pl.BlockSpec((tm, w), lambda j, i: (i, j)) for (w, _) in outs]
        + [pl.BlockSpec((outs[k][0], tm), lambda j, i: (0, i)) for k in also_t],
        out_shape=[S((n, w * ncb), dt) for (w, dt) in outs] + [S((outs[k][0], n), bf16) for k in also_t],
        compiler_params=pltpu.CompilerParams(dimension_semantics=("parallel", "parallel")),
    )(*[r[0] for r in rows], *[c[0] for c in consts])


def _tok_bwd(name, f, rows, consts, cots, tm, drow, dconst, ncb=1, with_j=False, addto=None, also_t=(), drow_dtype=f32, side=None, ncat=0,
             into=None):
    n = rows[0][0].shape[0]
    nr, nc, nct = len(rows), len(consts), len(cots)
    nbuf = int(into is not None and into[0] is not None)
    addto = addto or {}
    add_keys = sorted(addto)
    nadd = len(add_keys)

    def body(*refs):
        ins = [r[...] for r in refs[:nr + nc]]
        cot = [r[...] for r in refs[nr + nc:nr + nc + nct]]
        adds = refs[nr + nc + nct:nr + nc + nct + nadd]
        outs = refs[nr + nc + nct + nadd + nbuf:]
        pos = list(drow) + [nr + k for k in dconst]

        def g(*dargs):
            full = list(ins)
            for p, a in zip(pos, dargs):
                full[p] = a
            return tuple(f(pl.program_id(0), *full) if with_j else f(*full))

        vals, vjp = jax.vjp(g, *[ins[p] for p in pos])
        grads = vjp(tuple(c.astype(v.dtype) for c, v in zip(cot, vals)))
        off = 0
        for k in range(len(drow)):
            gk = grads[k]
            if k in addto:
                gk = gk + adds[add_keys.index(k)][...]
            if k < ncat:
                outs[0][:, off:off + gk.shape[1]] = gk.astype(outs[0].dtype)
                off += gk.shape[1]
            else:
                outs[k - skip][...] = gk.astype(outs[k - skip].dtype)
            if k in also_t:
                tref = outs[len(drow) - skip + len(dconst) + list(also_t).index(k)]
                tref[...] = gk.T.astype(tref.dtype)
        first = pl.program_id(1) == 0
        for k in range(len(dconst)):
            ref = outs[len(drow) - skip + k]

            @pl.when(first)
            def _():
                ref[...] = jnp.zeros_like(ref)

            ref[...] += grads[len(drow) + k]

    skip = max(ncat - 1, 0)
    in_specs = _tok_specs(rows, consts, tm)
    in_specs += [pl.BlockSpec((tm, w), lambda j, i: (i, j)) for (_, w) in cots]
    in_specs += [pl.BlockSpec((tm, rows[drow[k]][1]), lambda j, i: (i, j)) for k in add_keys]
    dts = drow_dtype if isinstance(drow_dtype, (list, tuple)) else [drow_dtype] * len(drow)
    widths = [rows[k][1] for k in drow]
    if ncat:
        widths, dts = [sum(widths[:ncat])] + widths[ncat:], [dts[0]] + list(dts[ncat:])
    out_specs = [pl.BlockSpec((tm, wd), lambda j, i: (i, j)) for wd in widths]
    out_shape = [S((n, wd * ncb), dt) for wd, dt in zip(widths, dts)]
    bufs = []
    if into is not None:
        buf, total, col0 = into
        out_specs[0] = pl.BlockSpec((tm, widths[0]), functools.partial(lambda j, i, b0: (i, j + b0), b0=col0 // widths[0]))
        out_shape[0] = S((n, total), dts[0])
        if nbuf:
            in_specs += _hbm_specs(1)
            bufs = [buf]
    for k in dconst:
        arr, w, _ = consts[k]
        if w is None:
            out_specs.append(pl.BlockSpec(arr.shape, lambda j, i: (0, 0)))
            out_shape.append(S(arr.shape, f32))
        else:
            out_specs.append(pl.BlockSpec((arr.shape[0], w), lambda j, i: (0, j)))
            out_shape.append(S((arr.shape[0], w * ncb), f32))
    for k in also_t:
        out_specs.append(pl.BlockSpec((rows[drow[k]][1], tm), lambda j, i: (0, i)))
        out_shape.append(S((rows[drow[k]][1], n), bf16))
    return _pcall(
        side, body, name=name, grid=(ncb, n // tm), in_specs=in_specs, out_specs=out_specs, out_shape=out_shape,
        compiler_params=pltpu.CompilerParams(dimension_semantics=("parallel", "arbitrary")),
        aliases={len(in_specs) - 1: 0} if nbuf else None,
    )(*[r[0] for r in rows], *[c[0] for c in consts], *[c[0] for c in cots], *[addto[k] for k in add_keys], *bufs)


SCAN_CHUNKS = 2


def _scan_fwd(name, f, rows, nh, nchunk, side=None, consts=(), out_dtype=f32, out_k=1):
    n = rows[0][0].shape[0]
    nb = n // (CH * nchunk)
    nin, nco = len(rows), len(consts)
    w = nh * HD
    cps = SCAN_CHUNKS

    def body(*refs):
        o_ref, hist_ref, st = refs[nin + nco], refs[nin + nco + 1], refs[nin + nco + 2]

        @pl.when(pl.program_id(0) == 0)
        def _():
            st[...] = jnp.zeros_like(st)

        cvals = [r[...] for r in refs[nin:nin + nco]]
        state = st[...]
        for sub in range(cps):
            rr = pl.ds(sub * CH, CH)
            hist_ref[sub] = state.astype(hist_ref.dtype)
            tiles = [r[:, rr, :].reshape(nb * CH, r.shape[2]) for r in refs[:nin]]
            o, state = f(*tiles, *cvals, state)
            o_ref[:, rr, :] = o.reshape(nb, CH, out_k * w).astype(o_ref.dtype)
        st[...] = state

    seq3 = lambda a: a.reshape(nb, nchunk * CH, a.shape[1])
    res = _pcall(
        side, body, name=name, grid=(nchunk // cps,),
        in_specs=[pl.BlockSpec((nb, cps * CH, k * w), functools.partial(lambda c, base: (0, c, base), base=b)) for (_, b, k) in rows]
        + [pl.BlockSpec(c.shape, lambda c: (0, 0)) for c in consts],
        out_specs=[pl.BlockSpec((nb, cps * CH, out_k * w), lambda c: (0, c, 0)), pl.BlockSpec((cps, nb * w, HD), lambda c: (c, 0, 0))],
        out_shape=[S((nb, nchunk * CH, out_k * w), out_dtype), S((nchunk, nb * w, HD), bf16)],
        scratch_shapes=[pltpu.VMEM((nb * w, HD), f32)],
        compiler_params=pltpu.CompilerParams(dimension_semantics=("arbitrary",)),
    )(*[seq3(r[0]) for r in rows], *consts)
    (o, hist), extra = (res, None) if side is None else res
    out = [o.reshape(n, out_k * w), hist]
    return out if side is None else (out, extra)


def _scan_bwd(name, f, rows, hist, do, nh, nchunk, side=None, dtypes=None, consts=(), out_k=1, cps=SCAN_CHUNKS):
    n = rows[0][0].shape[0]
    nb = n // (CH * nchunk)
    nin, nco = len(rows), len(consts)
    w = nh * HD
    nstep = nchunk // cps

    def body(*refs):
        hist_ref, do_ref = refs[nin + nco], refs[nin + nco + 1]
        outs = refs[nin + nco + 2:nin + nco + 2 + nin]
        couts = refs[nin + nco + 2 + nin:nin + nco + 2 + nin + nco]
        ds = refs[nin + nco + 2 + nin + nco]

        @pl.when(pl.program_id(0) == 0)
        def _():
            ds[...] = jnp.zeros_like(ds)
            for c in couts:
                c[...] = jnp.zeros_like(c)

        cvals = [r[...] for r in refs[nin:nin + nco]]
        dstate = ds[...]
        for sub in reversed(range(cps)):
            rr = pl.ds(sub * CH, CH)
            tiles = [r[:, rr, :].reshape(nb * CH, r.shape[2]) for r in refs[:nin]]
            _, vjp = jax.vjp(f, *tiles, *cvals, hist_ref[sub].astype(f32))
            grads = vjp((do_ref[:, rr, :].reshape(nb * CH, out_k * w), dstate))
            for r, gk in zip(outs, grads[:nin]):
                r[:, rr, :] = gk.reshape(nb, CH, r.shape[2]).astype(r.dtype)
            for c, gk in zip(couts, grads[nin:nin + nco]):
                c[...] += gk
            dstate = grads[nin + nco]
        ds[...] = dstate

    seq3 = lambda a: a.reshape(nb, nchunk * CH, a.shape[1])
    rev = lambda c, base: (0, nstep - 1 - c, base)
    res = _pcall(
        side, body, name=name, grid=(nstep,),
        in_specs=[pl.BlockSpec((nb, cps * CH, k * w), functools.partial(rev, base=b)) for (_, b, k) in rows]
        + [pl.BlockSpec(c.shape, lambda c: (0, 0)) for c in consts]
        + [pl.BlockSpec((cps, nb * w, HD), lambda c: (nstep - 1 - c, 0, 0)),
           pl.BlockSpec((nb, cps * CH, out_k * w), functools.partial(rev, base=0))],
        out_specs=[pl.BlockSpec((nb, cps * CH, k * w), functools.partial(rev, base=0)) for (_, _, k) in rows]
        + [pl.BlockSpec(c.shape, lambda c: (0, 0)) for c in consts],
        out_shape=[S((nb, nchunk * CH, k * w), dt) for (_, _, k), dt in zip(rows, dtypes or [f32] * nin)]
        + [S(c.shape, f32) for c in consts],
        scratch_shapes=[pltpu.VMEM((nb * w, HD), f32)],
        compiler_params=pltpu.CompilerParams(dimension_semantics=("arbitrary",)),
    )(*[seq3(r[0]) for r in rows], *consts, hist, seq3(do))
    outs, extra = (res, None) if side is None else res
    outs = [o.reshape(n, o.shape[2]) for o in outs[:nin]] + list(outs[nin:])
    return outs if side is None else (outs, extra)


_VMEM_LIMIT = 56 * 2 ** 20
_VMEM_TILE_BUDGET = 40 * 2 ** 20


def _mm_tiles(m, n, k, sa, sb, so, sx, a_f32, b_f32, tn_fixed):
    best = None
    for tm in (1024, 512, 256, 128, 64):
        for tn in ((tn_fixed,) if tn_fixed else (1024, 768, 512, 384, 256, 128)):
            if m % tm or n % tn:
                continue
            need = 2 * (tm * k * sa + k * tn * sb + tm * tn * (so + sx)) + tm * tn * 4
            need += tm * k * (2 if sa == 4 else 0) + k * tn * (2 if sb == 4 else 0)
            need += tm * k * (4 if a_f32 else 0) + k * tn * (4 if b_f32 else 0)
            if need <= _VMEM_TILE_BUDGET and (best is None or (tm * tn, tm) > best[0]):
                best = ((tm * tn, tm), tm, tn)
    return best[1], best[2]


def _mm(name, a, b, dims, out_dtype, a_fn=None, b_fn=None, epi=None, extra=(), consts=(), outs=None, out_t=False, slab=False,
        side=None):
    m, kk = a.shape
    gathered = b.ndim == 3
    if gathered:
        nn = NDEV * _COLW if dims is NN else b.shape[1]
    else:
        nn = b.shape[1] if dims is NN else b.shape[0]
    kinds = [("tile", out_dtype)] if outs is None else list(outs)
    so = sum(jnp.dtype(dt).itemsize for kd, dt in kinds if kd != "rows")
    sx = sum(e.dtype.itemsize for e in extra)
    full_rows = bool(consts) or any(kd == "rows" for kd, _ in kinds)
    tm, tn = _mm_tiles(m, nn, kk, a.dtype.itemsize, b.dtype.itemsize, so, sx,
                       a_fn is not None, b_fn is not None,
                       _COLW if slab else (nn if full_rows else None))
    gblocks = tn // _COLW if (gathered and dims is NN) else 0
    assert not gblocks or (outs is None and not extra and not out_t and not slab and b_fn is None)
    nex, nco = len(extra), len(consts)

    def body(a_ref, b_ref, *rest):
        av = a_ref[...]
        if a_fn is not None:
            av = a_fn(av.astype(f32))
        av = av.astype(bf16)
        if gblocks:
            for gb in range(gblocks):
                part = _dot(av, b_ref[gb], NN)
                rest[nex + nco][:, _COLW * gb:_COLW * (gb + 1)] = (part if epi is None else epi(part)).astype(rest[nex + nco].dtype)
            return
        if gathered:
            acc = sum(_dot(av[:, _COLW * d:_COLW * (d + 1)], b_ref[d], NT) for d in range(NDEV))
        else:
            bv = b_ref[...]
            if b_fn is not None:
                bv = b_fn(bv.astype(f32))
            acc = _dot(av, bv.astype(bf16), dims)
        if epi is not None:
            acc = epi(acc, *[r[...] for r in rest[:nex + nco]])
        vals = acc if isinstance(acc, tuple) else (acc,)
        for (kd, _), o_ref, val in zip(kinds, rest[nex + nco:], vals):
            if kd == "rows":
                @pl.when(pl.program_id(0) == 0)
                def _():
                    o_ref[...] = jnp.zeros_like(o_ref)

                o_ref[...] += val
            elif kd == "tile_t" or out_t:
                o_ref[...] = val.T.astype(o_ref.dtype)
            elif slab:
                o_ref[0] = val.astype(o_ref.dtype)
            else:
                o_ref[...] = val.astype(o_ref.dtype)

    if gathered:
        bspec = (pl.BlockSpec((gblocks, kk, _COLW), lambda i, j: (j, 0, 0)) if dims is NN
                 else pl.BlockSpec((NDEV, tn, _COLW), lambda i, j: (0, j, 0)))
    else:
        bspec = pl.BlockSpec((kk, tn), lambda i, j: (0, j)) if dims is NN else pl.BlockSpec((tn, kk), lambda i, j: (j, 0))
    out_specs, out_shape = [], []
    for kd, dt in kinds:
        if kd == "rows":
            out_specs.append(pl.BlockSpec((1, nn), lambda i, j: (0, 0)))
            out_shape.append(S((1, nn), dt))
        elif kd == "tile_t" or out_t:
            out_specs.append(pl.BlockSpec((tn, tm), lambda i, j: (j, i)))
            out_shape.append(S((nn, m), dt))
        elif slab:
            out_specs.append(pl.BlockSpec((1, tm, tn), lambda i, j: (j, i, 0)))
            out_shape.append(S((nn // tn, m, tn), dt))
        else:
            out_specs.append(pl.BlockSpec((tm, tn), lambda i, j: (i, j)))
            out_shape.append(S((m, nn), dt))
    if outs is None:
        out_specs, out_shape = out_specs[0], out_shape[0]
    sem = ("arbitrary", "arbitrary") if any(kd == "rows" for kd, _ in kinds) else ("parallel", "parallel")
    return _pcall(
        side, body, name=name, grid=(m // tm, nn // tn),
        in_specs=[pl.BlockSpec((tm, kk), lambda i, j: (i, 0)), bspec]
        + [pl.BlockSpec((tm, tn), lambda i, j: (i, j)) for _ in extra]
        + [pl.BlockSpec(c.shape, lambda i, j: (0, 0)) for c in consts],
        out_specs=out_specs, out_shape=out_shape,
        compiler_params=pltpu.CompilerParams(dimension_semantics=sem, vmem_limit_bytes=_VMEM_LIMIT),
    )(a, b, *extra, *consts)


def _f_norm(x, g):
    return (x * lax.rsqrt(jnp.mean(x * x, axis=-1, keepdims=True) + EPS) * g,)


def _f_foxpre(zqk, gq, gk, pm):
    def nrm(t, g):
        return t * lax.rsqrt(_sel_rhs(t * t, pm) + EPS) * g
    return nrm(zqk[:, :FOX_W], gq), nrm(zqk[:, FOX_W:], gk)


def _chunk_cumsum(x):
    n = x.shape[0]
    r, c = _iota2((n, n), 0), _iota2((n, n), 1)
    tri = jnp.logical_and(r >= c, (r // CH) == (c // CH)).astype(f32)
    return _sel_lhs(tri, x)


def _f_gdngate(zs, eb, ea, alog_b, dt_b):
    beta = jax.nn.sigmoid(_sel_rhs(zs, eb))
    la = -jnp.exp(alog_b) * jax.nn.softplus(_sel_rhs(zs, ea) + dt_b)
    return beta, _chunk_cumsum(la)


def _f_conv(j, x, w):
    t = x.shape[0]
    y = x * w[3:4, :]
    for jj in range(3):
        sh = 3 - jj
        xs = jnp.concatenate([jnp.zeros((sh, x.shape[1]), f32), x[:t - sh, :]], axis=0)
        y = y + xs * w[jj:jj + 1, :]
    y = jax.nn.silu(y)
    yn = y * lax.rsqrt(jnp.sum(y * y, axis=-1, keepdims=True) + EPS)
    return (jnp.where(j < 2 * GDN_H, yn, y),)


def _head_rms(o, nh):
    outs = []
    for h in range(nh):
        oh = o[:, HD * h:HD * (h + 1)]
        outs.append(oh * lax.rsqrt(jnp.mean(oh * oh, axis=-1, keepdims=True) + EPS))
    return jnp.concatenate(outs, axis=1)


def _f_post0(fox_o, o, gg, on):
    return (jnp.concatenate([fox_o, _head_rms(o, GDN_H) * on * jax.nn.silu(gg)], axis=1),)


def _f_post1(o, zg, on):
    return (_head_rms(o, HG_H) * on * jax.nn.silu(zg),)


def _f_hpre(zqf, lbl):
    lb = jax.nn.sigmoid(lbl[1:2, :] - lbl[0:1, :])
    fg = lb + (1.0 - lb) * jax.nn.sigmoid(zqf[:, D:])
    return jax.nn.silu(zqf[:, :D]), 1.0 - fg, _chunk_cumsum(jnp.log(fg))


def _dotb(a, b, dims=NN):
    return _dot(a.astype(bf16), b.astype(bf16), dims)


def _dot3(a, b):
    ah, bh = a.astype(bf16), b.astype(bf16)
    al, bl = (a - ah.astype(f32)).astype(bf16), (b - bh.astype(f32)).astype(bf16)
    return _dot(ah, bh) + (_dot(ah, bl) + _dot(al, bh))


def _split(t, nh):
    return [t[CH * ck:CH * (ck + 1), HD * h:HD * (h + 1)] for ck in range(t.shape[0] // CH) for h in range(nh)]


def _merge(units, nh):
    return jnp.concatenate([jnp.concatenate(units[i:i + nh], axis=1) for i in range(0, len(units), nh)], axis=0)


def _inv_impl(amats):
    n = amats[0].shape[0]
    eye = jnp.where(_iota2((n, n), 0) == _iota2((n, n), 1), 1.0, 0.0).astype(f32)
    xs, ps = [eye - a for a in amats], list(amats)
    for _ in range(max(1, (n - 1).bit_length()) - 1):
        ps = [_dotb(p, p) for p in ps]
        xs = [x + _dotb(x, p) for x, p in zip(xs, ps)]
    for _ in range(3):
        rs = [eye - x - _dot3(a, x) for a, x in zip(amats, xs)]
        xs = [x + _dotb(x, r) for x, r in zip(xs, rs)]
    return tuple(xs)


@jax.custom_vjp
def _inv_unit_lower(amats):
    return _inv_impl(amats)


def _inv_fwd(amats):
    xs = _inv_impl(amats)
    return xs, xs


def _inv_bwd(xs, dxs):
    return (tuple(-_dotb(_dotb(x, dx, TN), x, NT) for x, dx in zip(xs, dxs)),)


_inv_unit_lower.defvjp(_inv_fwd, _inv_bwd)


@jax.custom_vjp
def _inv_given(amats, xs):
    return xs


def _inv_given_fwd(amats, xs):
    return xs, xs


def _inv_given_bwd(xs, dxs):
    return _inv_bwd(xs, dxs)[0], tuple(jnp.zeros_like(x) for x in xs)


_inv_given.defvjp(_inv_given_fwd, _inv_given_bwd)


def _f_gdn_intra(q, k, v, bb, gb, tinv_p=None):
    qs, ks, vs, bs, gs = (_split(t, GDN_H) for t in (q, k, v, bb, gb))
    r, cc = _iota2((CH, CH), 0), _iota2((CH, CH), 1)
    causal, strict = r >= cc, r > cc
    beta, g, gl = [b[:, :1] for b in bs], [x[:, :1] for x in gs], [x[CH - 1:CH, :1] for x in gs]
    decay = [jnp.exp(jnp.where(causal, x[:, :CH] - x[:, :CH].T, -jnp.inf)) for x in gs]
    kb = [ki * bi for ki, bi in zip(ks, beta)]
    amat = [jnp.where(strict, _dotb(kbi, ki, NT) * di, 0.0) for kbi, ki, di in zip(kb, ks, decay)]
    if tinv_p is None:
        tinv = _inv_unit_lower(tuple(amat))
    else:
        tinv = _inv_given(tuple(amat), tuple(x[:, :CH] for x in _split(tinv_p, GDN_H)))
    rhs = [jnp.concatenate([vi * bi, kbi * jnp.exp(gi)], axis=1) for vi, bi, kbi, gi in zip(vs, beta, kb, g)]
    uw = [_dotb(ti, ri) for ti, ri in zip(tinv, rhs)]
    qsc = [qi * (HD ** -0.5) for qi in qs]
    qk = [jnp.where(causal, _dotb(qi, ki, NT) * di, 0.0) for qi, ki, di in zip(qsc, ks, decay)]
    outs = ([x[:, :HD] for x in uw], [x[:, HD:] for x in uw],
            [jnp.concatenate([x, jnp.zeros_like(x)], axis=1) for x in qk],
            [qi * jnp.exp(gi) for qi, gi in zip(qsc, g)],
            [ki * jnp.exp(gli - gi) for ki, gli, gi in zip(ks, gl, g)],
            [jnp.broadcast_to(gli, (CH, HD)) for gli in gl])
    if tinv_p is None:
        outs += ([jnp.concatenate([x, jnp.zeros_like(x)], axis=1) for x in tinv],)
    return tuple(_merge(o, GDN_H) for o in outs)


def _f_gdn_gated(q, k, v, zs, eb, ea, alog_b, dt_b):
    return _f_gdn_intra(q, k, v, *_f_gdngate(zs, eb, ea, alog_b, dt_b))


def _f_gdn_gated_given(q, k, v, zs, tinv_p, eb, ea, alog_b, dt_b):
    return _f_gdn_intra(q, k, v, *_f_gdngate(zs, eb, ea, alog_b, dt_b), tinv_p=tinv_p)


def _f_gdn_inter(u, w, qkp, qd, kd, glb, st):
    us, ws, qks, qds, kds, gls = (_split(t, GDN_H) for t in (u, w, qkp, qd, kd, glb))
    sts = [st[HD * i:HD * (i + 1), :] for i in range(len(us))]
    vn = [ui - _dotb(wi, si) for ui, wi, si in zip(us, ws, sts)]
    o = [_dotb(qi, si) + _dotb(xi[:, :CH], vi) for qi, si, xi, vi in zip(qds, sts, qks, vn)]
    s2 = [si * jnp.exp(gi[:1, :1]) + _dotb(ki, vi, TN) for si, gi, ki, vi in zip(sts, gls, kds, vn)]
    return _merge(o, GDN_H), jnp.concatenate(s2, axis=0)


def _f_gdn_full(u, w, qkp, qd, kd, glb, fox_o, gg, on, st):
    o, s2 = _f_gdn_inter(u, w, qkp, qd, kd, glb, st)
    return _f_post0(fox_o, o, gg, on)[0], s2


def _f_hgrn_chunk(q, k, b, v, st):
    qs, ks, bs, vs = (_split(t, HG_H) for t in (q, k, b, v))
    sts = [st[HD * i:HD * (i + 1), :] for i in range(len(qs))]
    causal = _iota2((CH, CH), 0) >= _iota2((CH, CH), 1)
    bl, bm = [x[CH - 1:CH, :] for x in bs], [x[CH // 2 - 1:CH // 2, :] for x in bs]
    a = [jnp.where(causal, _dotb(qi * jnp.exp(bi - mi), ki * jnp.exp(mi - bi), NT), 0.0)
         for qi, ki, bi, mi in zip(qs, ks, bs, bm)]
    o = [_dotb(qi * jnp.exp(bi), si, NT) + _dotb(ai, vi) for qi, bi, si, ai, vi in zip(qs, bs, sts, a, vs)]
    s2 = [si * jnp.exp(li) + _dotb(vi, ki * jnp.exp(li - bi), TN) for si, li, vi, ki, bi in zip(sts, bl, vs, ks, bs)]
    return _merge(o, HG_H), jnp.concatenate(s2, axis=0)


def _f_hgrn_full(z, lbl, on, st):
    o, s2 = _f_hgrn_chunk(*_f_hpre(z[:, :2 * D], lbl), z[:, 2 * D:3 * D], st)
    return _f_post1(o, z[:, 3 * D:], on)[0], s2


def _fox_gate_fwd(z0, fbias, t, tc=256):
    n = z0.shape[0]
    nt = t // tc

    def body(zs_ref, b_ref, ccol_ref, crow_ref, carry):
        @pl.when(pl.program_id(1) == 0)
        def _():
            carry[...] = jnp.zeros_like(carry)

        ls = jnp.where(_iota2((tc, LANES), 1) < FOX_H, jax.nn.log_sigmoid(zs_ref[...] + b_ref[...]), 0.0)
        tri = (_iota2((tc, tc), 0) >= _iota2((tc, tc), 1)).astype(f32)
        c = _dot_sel(tri, ls, exact_lhs=True) + carry[...]
        carry[...] = c[tc - 1:tc, :]
        ccol_ref[...] = c
        crow_ref[0] = c.T[:FOX_H, :]

    return pl.pallas_call(
        body, name="fox_gate_fwd", grid=(n // t, nt),
        in_specs=[pl.BlockSpec((tc, LANES), lambda b, i: (b * nt + i, Z0_SMALL)), pl.BlockSpec((1, LANES), lambda b, i: (0, 0))],
        out_specs=[pl.BlockSpec((tc, LANES), lambda b, i: (b * nt + i, 0)), pl.BlockSpec((1, FOX_H, tc), lambda b, i: (b, 0, i))],
        out_shape=[S((n, LANES), f32), S((n // t, FOX_H, t), f32)],
        scratch_shapes=[pltpu.VMEM((1, LANES), f32)],
        compiler_params=pltpu.CompilerParams(dimension_semantics=("parallel", "arbitrary")),
    )(z0, fbias)


def _fox_gate_bwd(z0, fbias, dcq, dck, t, tc=256):
    n = z0.shape[0]
    nt = t // tc

    def body(zs_ref, b_ref, dcq_ref, dck_ref, dz_ref, db_ref, carry):
        first = jnp.logical_and(pl.program_id(0) == 0, pl.program_id(1) == 0)

        @pl.when(pl.program_id(1) == 0)
        def _():
            carry[...] = jnp.zeros_like(carry)

        @pl.when(first)
        def _():
            db_ref[...] = jnp.zeros_like(db_ref)

        dc = dcq_ref[0] + dcq_ref[1] + dcq_ref[2] + dcq_ref[3]
        drow = dck_ref[0, 0] + dck_ref[1, 0] + dck_ref[2, 0] + dck_ref[3, 0]
        eye = (_iota2((FOX_H, LANES), 0) == _iota2((FOX_H, LANES), 1)).astype(f32)
        dc = dc + _dot_sel(drow, eye, TN)
        triu = (_iota2((tc, tc), 0) <= _iota2((tc, tc), 1)).astype(f32)
        dls = _dot_sel(triu, dc, exact_lhs=True) + carry[...]
        carry[...] = dls[0:1, :]
        x = zs_ref[...] + b_ref[...]
        dz = jnp.where(_iota2((tc, LANES), 1) < FOX_H, dls * jax.nn.sigmoid(-x), 0.0)
        dz_ref[...] = dz
        db_ref[...] += jnp.sum(dz, axis=0, keepdims=True)

    def rev(b, i):
        return b * nt + (nt - 1 - i)

    return pl.pallas_call(
        body, name="fox_gate_bwd", grid=(n // t, nt),
        in_specs=[pl.BlockSpec((tc, LANES), lambda b, i: (rev(b, i), Z0_SMALL)), pl.BlockSpec((1, LANES), lambda b, i: (0, 0)),
                  pl.BlockSpec((4, tc, LANES), lambda b, i: (0, rev(b, i), 0)),
                  pl.BlockSpec((4, 1, FOX_H, tc), lambda b, i: (0, b, 0, nt - 1 - i))],
        out_specs=[pl.BlockSpec((tc, LANES), lambda b, i: (rev(b, i), 0)), pl.BlockSpec((1, LANES), lambda b, i: (0, 0))],
        out_shape=[S((n, LANES), f32), S((1, LANES), f32)],
        scratch_shapes=[pltpu.VMEM((1, LANES), f32)],
        compiler_params=pltpu.CompilerParams(dimension_semantics=("arbitrary", "arbitrary")),
    )(z0, fbias, dcq, dck)


def _fox_scores(hh, p, i, tq, q, k, ccol, crow):
    kmax = k.shape[0]
    lane = _iota2((1, LANES), 1)
    mh = (lane // FOX_D) == hh
    h = 2 * p + hh
    qh = jnp.where(mh, q, 0.0).astype(bf16)
    s = _dot(qh, k, NT) * (FOX_D ** -0.5)
    cq = jnp.sum(jnp.where(lane == h, ccol, 0.0), axis=1, keepdims=True)
    ck = jnp.sum(jnp.where(_iota2((FOX_H, 1), 0) == h, crow, 0.0), axis=0, keepdims=True)
    causal = _iota2((1, kmax), 1) <= (i * tq + _iota2((tq, 1), 0))
    s = jnp.where(causal, s + cq - ck, -jnp.inf)
    pe = jnp.exp(s - jnp.max(s, axis=1, keepdims=True))
    return mh, qh, pe, jnp.sum(pe, axis=1, keepdims=True)


def _fox_attn_fwd(qn, kn, z0, ccol, crow, t, tq=256, side=None):
    n = qn.shape[0]
    nq = t // tq

    def body(q_ref, k_ref, v_ref, ccol_ref, crow_ref, o_ref):
        p = pl.program_id(1)
        k, v, crow = k_ref[...].astype(bf16), v_ref[...].astype(bf16), crow_ref[0]
        for i in range(nq):
            rows, kmax = pl.ds(i * tq, tq), (i + 1) * tq
            q, cc = q_ref[rows, :], ccol_ref[rows, :]
            acc = jnp.zeros((tq, LANES), f32)
            for hh in range(2):
                mh, _, pe, l = _fox_scores(hh, p, i, tq, q, k[:kmax], cc, crow[:, :kmax])
                acc = jnp.where(mh, _dot(pe.astype(bf16), v[:kmax]) / l, acc)
            o_ref[rows, :] = acc

    seq = lambda b, p: (b, p)
    return _pcall(
        side, body, name="fox_attn_fwd", grid=(n // t, FOX_H // 2),
        in_specs=[pl.BlockSpec((t, LANES), seq), pl.BlockSpec((t, LANES), seq), pl.BlockSpec((t, LANES), lambda b, p: (b, Z0_FV + p)),
                  pl.BlockSpec((t, LANES), lambda b, p: (b, 0)), pl.BlockSpec((1, FOX_H, t), lambda b, p: (b, 0, 0))],
        out_specs=pl.BlockSpec((t, LANES), seq),
        out_shape=S((n, FOX_W), f32),
        compiler_params=pltpu.CompilerParams(dimension_semantics=("parallel", "parallel")),
    )(qn, kn, z0, ccol, crow)


def _fox_attn_bwd(qn, kn, z0, ccol, crow, do, t, tq=256, side=None):
    n = qn.shape[0]
    nq = t // tq
    nb = n // t

    def body(q_ref, k_ref, v_ref, ccol_ref, crow_ref, do_ref, dq_ref, dk_ref, dz_ref, dcq_ref, dck_ref, dv_ref):
        p = pl.program_id(1)
        dk_ref[...] = jnp.zeros_like(dk_ref)
        dv_ref[...] = jnp.zeros_like(dv_ref)
        dck_ref[...] = jnp.zeros_like(dck_ref)
        kf, v, crow = k_ref[...], v_ref[...].astype(bf16), crow_ref[0]
        k = kf.astype(bf16)
        lane = _iota2((1, LANES), 1)
        sub = _iota2((FOX_H, 1), 0)
        scale = FOX_D ** -0.5
        for i in range(nq):
            rows, kmax = pl.ds(i * tq, tq), (i + 1) * tq
            q, cc, dout = q_ref[rows, :], ccol_ref[rows, :], do_ref[rows, :]
            dq = jnp.zeros((tq, LANES), f32)
            dcq = jnp.zeros((tq, LANES), f32)
            for hh in range(2):
                mh, qh, pe, l = _fox_scores(hh, p, i, tq, q, k[:kmax], cc, crow[:, :kmax])
                pr = pe / l
                doh = jnp.where(mh, dout, 0.0).astype(bf16)
                dp = _dot(doh, v[:kmax], NT)
                ds = pr * (dp - jnp.sum(pr * dp, axis=1, keepdims=True))
                dsb = ds.astype(bf16)
                dq = dq + _dot(dsb, jnp.where(mh, kf[:kmax], 0.0).astype(bf16)) * scale
                dk_ref[:kmax, :] += _dot(dsb, qh, TN) * scale
                dv_ref[:kmax, :] += _dot(pr.astype(bf16), doh, TN)
                h = 2 * p + hh
                dcq = dcq + jnp.where(lane == h, jnp.sum(ds, axis=1, keepdims=True), 0.0)
                dck_ref[0, 0, :, :kmax] += jnp.where(sub == h, -jnp.sum(ds, axis=0, keepdims=True), 0.0)
            dq_ref[rows, :] = dq
            dcq_ref[0, rows, :] = dcq
        dz_ref[...] = dv_ref[...].astype(bf16)

    seq = lambda b, p: (b, p)
    return _pcall(
        side, body, name="fox_attn_bwd", grid=(nb, FOX_H // 2), scratch_shapes=[pltpu.VMEM((t, LANES), f32)],
        in_specs=[pl.BlockSpec((t, LANES), seq), pl.BlockSpec((t, LANES), seq), pl.BlockSpec((t, LANES), lambda b, p: (b, Z0_FV + p)),
                  pl.BlockSpec((t, LANES), lambda b, p: (b, 0)), pl.BlockSpec((1, FOX_H, t), lambda b, p: (b, 0, 0)),
                  pl.BlockSpec((t, LANES), seq)],
        out_specs=[pl.BlockSpec((t, LANES), seq), pl.BlockSpec((t, LANES), seq), pl.BlockSpec((t, LANES), lambda b, p: (b, Z0_FV + p)),
                   pl.BlockSpec((1, t, LANES), lambda b, p: (p, b, 0)), pl.BlockSpec((1, 1, FOX_H, t), lambda b, p: (p, b, 0, 0))],
        out_shape=[S((n, FOX_W), f32), S((n, FOX_W), f32), S((n, ZW0), bf16), S((4, n, LANES), f32), S((4, nb, FOX_H, t), f32)],
        compiler_params=pltpu.CompilerParams(dimension_semantics=("parallel", "parallel")),
    )(qn, kn, z0, ccol, crow, do)


def _adamw_math(w, g, m, v):
    m = ADAM_B1 * m + (1.0 - ADAM_B1) * g
    v = ADAM_B2 * v + (1.0 - ADAM_B2) * (g * g)
    m_hat = m / (1.0 - ADAM_B1 ** ADAM_STEP)
    v_hat = v / (1.0 - ADAM_B2 ** ADAM_STEP)
    return -ADAM_LR * (m_hat / (jnp.sqrt(v_hat) + ADAM_EPS) + ADAM_WD * w), m, v


_ADAMW_STEPS = 4


def _adamw_big(name, idx, gs, r1s, r2s, ws, ms, vs):
    na = len(ws)

    def body(idx_ref, *refs):
        for k in range(na):
            gm_ref, r1_ref, r2_ref, w_ref, m_ref, v_ref = refs[6 * k:6 * k + 6]
            g_ref, d_ref, nm_ref, nv_ref = refs[6 * na + 4 * k:6 * na + 4 * k + 4]
            g = gm_ref[0].astype(f32) + r1_ref[0].astype(f32)
            for j in range(3):
                g = g + r2_ref[j].astype(f32)
            g = g[:, :w_ref.shape[1]]
            d, nm, nv = _adamw_math(w_ref[...], g, m_ref[...], v_ref[...])
            g_ref[...] = g
            d_ref[...] = d
            nm_ref[...] = nm
            nv_ref[...] = nv

    in_specs, out_specs, out_shape, operands = [], [], [], []
    for ga, r1, r2, w, m, v in zip(gs, r1s, r2s, ws, ms, vs):
        r, wc = w.shape
        c = ga.shape[2]
        tr = r // _ADAMW_STEPS
        row = pl.BlockSpec((tr, wc), lambda i, s: (i, 0))
        in_specs += [pl.BlockSpec((1, tr, c), lambda i, s: (s[0], i, 0)), pl.BlockSpec((1, tr, c), lambda i, s: (s[1], i, 0)),
                     pl.BlockSpec((3, tr, c), lambda i, s: (0, i, 0)), row, row, row]
        out_specs += [row] * 4
        out_shape += [S((r, wc), f32)] * 4
        operands += [ga, r1, r2, w, m, v]
    res = pl.pallas_call(
        body, name=name,
        grid_spec=pltpu.PrefetchScalarGridSpec(num_scalar_prefetch=1, grid=(_ADAMW_STEPS,), in_specs=in_specs, out_specs=out_specs),
        out_shape=out_shape,
        compiler_params=pltpu.CompilerParams(dimension_semantics=("parallel",), vmem_limit_bytes=_VMEM_LIMIT),
    )(idx, *operands)
    return [res[4 * k:4 * k + 4] for k in range(na)]


def _pair_sum(name, idx, gs, r1s):
    na = len(gs)

    def body(idx_ref, *refs):
        for gm_ref, r1_ref, o_ref in zip(refs[:na], refs[na:2 * na], refs[2 * na:]):
            o_ref[0] = (gm_ref[0, 0].astype(f32) + r1_ref[0].astype(f32)).astype(bf16)

    shapes = [ga.shape[1:] for ga in gs]
    res = pl.pallas_call(
        body, name=name,
        grid_spec=pltpu.PrefetchScalarGridSpec(
            num_scalar_prefetch=1, grid=(4,),
            in_specs=[pl.BlockSpec((1, 1, r, c), lambda ch, s: (ch, s[0], 0, 0)) for r, c in shapes]
            + [pl.BlockSpec((1, r, c), lambda ch, s: (ch, 0, 0)) for r, c in shapes],
            out_specs=[pl.BlockSpec((1, r, c), lambda ch, s: (ch, 0, 0)) for r, c in shapes]),
        out_shape=[S((4, r, c), bf16) for r, c in shapes],
        compiler_params=pltpu.CompilerParams(dimension_semantics=("parallel",), vmem_limit_bytes=_VMEM_LIMIT),
    )(idx, *[ga.reshape(4, 2, *ga.shape[1:]) for ga in gs], *r1s)
    return list(res)


def _adamw_small(parts, w, m, v, own_mask, own_g):
    def body(p_ref, w_ref, m_ref, v_ref, mask_ref, og_ref, g_ref, d_ref, nm_ref, nv_ref):
        g = p_ref[0]
        for k in range(1, NDEV):
            g = g + p_ref[k]
        g_ref[...] = g
        ge = jnp.where(mask_ref[...] > 0.5, og_ref[...], g)
        d, nm, nv = _adamw_math(w_ref[...], ge, m_ref[...], v_ref[...])
        d_ref[...] = d
        nm_ref[...] = nm
        nv_ref[...] = nv

    return pl.pallas_call(body, name="adamw_small", out_shape=[S(w.shape, f32)] * 4)(parts, w, m, v, own_mask, own_g)


def _sum_parts(parts):
    def body(p_ref, g_ref):
        g = p_ref[0]
        for k in range(1, NDEV):
            g = g + p_ref[k]
        g_ref[...] = g

    return pl.pallas_call(body, name="sum_parts", out_shape=S(parts.shape[1:], f32))(parts)


def _me():
    return lax.axis_index("x"), lax.axis_index("y"), lax.axis_index("c")


def _hbm_specs(n):
    return [pl.BlockSpec(memory_space=pl.ANY)] * n


def _allgather(name, xs):
    na = len(xs)

    def body(*refs):
        x_refs, out_refs = refs[:na], refs[na:2 * na]
        send_sems, recv_sems, local_sems = refs[2 * na:]
        mx, my, mc = _me()
        me, sib = (mx, my, mc), (mx, my, 1 - mc)
        chips = [(1 - mx, my), (mx, 1 - my), (1 - mx, 1 - my)]

        def slab(a, px, py, pc):
            return out_refs[a].at[4 * px + 2 * py + pc]

        def copy(a, k, block, to, own=False):
            return pltpu.make_async_remote_copy(
                src_ref=x_refs[a] if own else slab(a, *block), dst_ref=slab(a, *block),
                send_sem=send_sems.at[7 * a + k], recv_sem=recv_sems.at[7 * a + k], device_id=to, device_id_type=MESH)

        mine = [pltpu.make_async_copy(x_refs[a], slab(a, *me), local_sems.at[a]) for a in range(na)]
        first = []
        for a in range(na):
            mine[a].start()
            first += [copy(a, 0, me, sib, own=True)] + [copy(a, 1 + j, me, (*chip, mc), own=True) for j, chip in enumerate(chips)]
        for cp in first:
            cp.start()
        passed = []
        for j, chip in enumerate(chips):
            for a in range(na):
                copy(a, 1 + j, (*chip, mc), me).wait_recv()
                passed.append(copy(a, 4 + j, (*chip, mc), sib))
                passed[-1].start()
        for a in range(na):
            copy(a, 0, sib, me).wait_recv()
            for j, chip in enumerate(chips):
                copy(a, 4 + j, (*chip, 1 - mc), me).wait_recv()
        for cp in first + passed:
            cp.wait_send()
        for cp in mine:
            cp.wait()

    return pl.pallas_call(
        body, name=name, out_shape=[S((NDEV,) + x.shape, x.dtype) for x in xs],
        in_specs=_hbm_specs(na), out_specs=_hbm_specs(na),
        scratch_shapes=[pltpu.SemaphoreType.DMA((7 * na,)), pltpu.SemaphoreType.DMA((7 * na,)), pltpu.SemaphoreType.DMA((na,))],
    )(*xs)


def _rs_sibling(gs):
    na = len(gs)

    def body(*refs):
        g_refs, out_refs, send_sems, recv_sems = refs[:na], refs[na:2 * na], refs[2 * na], refs[2 * na + 1]
        mx, my, mc = _me()
        cps = [pltpu.make_async_remote_copy(
            src_ref=g_refs[a].at[2 * ch + 1 - mc], dst_ref=out_refs[a].at[ch], send_sem=send_sems.at[4 * a + ch],
            recv_sem=recv_sems.at[4 * a + ch], device_id=(mx, my, 1 - mc), device_id_type=MESH)
            for a in range(na) for ch in range(4)]
        for cp in cps:
            cp.start()
        for cp in cps:
            cp.wait_recv()
        for cp in cps:
            cp.wait_send()

    return pl.pallas_call(
        body, name="rs_sibling", out_shape=[S((4,) + g.shape[1:], g.dtype) for g in gs],
        in_specs=_hbm_specs(na), out_specs=_hbm_specs(na),
        scratch_shapes=[pltpu.SemaphoreType.DMA((4 * na,)), pltpu.SemaphoreType.DMA((4 * na,))],
    )(*gs)


def _side_allgather(xs):
    na = len(xs)

    def mk(x_refs, out_refs, sems):
        send_sems, recv_sems, local_sems = sems
        mx, my, mc = _me()
        me, sib = (mx, my, mc), (mx, my, 1 - mc)
        chips = [(1 - mx, my), (mx, 1 - my), (1 - mx, 1 - my)]

        def slab(a, px, py, pc):
            return out_refs[a].at[4 * px + 2 * py + pc]

        def copy(a, k, block, to, own=False):
            return pltpu.make_async_remote_copy(
                src_ref=x_refs[a] if own else slab(a, *block), dst_ref=slab(a, *block),
                send_sem=send_sems.at[7 * a + k], recv_sem=recv_sems.at[7 * a + k], device_id=to, device_id_type=MESH)

        mine = [pltpu.make_async_copy(x_refs[a], slab(a, *me), local_sems.at[a]) for a in range(na)]
        first = []
        for a in range(na):
            first += [copy(a, 0, me, sib, own=True)] + [copy(a, 1 + j, me, (*chip, mc), own=True) for j, chip in enumerate(chips)]
        return me, sib, chips, mc, copy, mine, first

    def start(x_refs, out_refs, sems):
        *_, mine, first = mk(x_refs, out_refs, sems)
        for cp in mine + first:
            cp.start()

    def finish(x_refs, out_refs, sems):
        me, sib, chips, mc, copy, mine, first = mk(x_refs, out_refs, sems)
        passed = []
        for j, chip in enumerate(chips):
            for a in range(na):
                copy(a, 1 + j, (*chip, mc), me).wait_recv()
                passed.append(copy(a, 4 + j, (*chip, mc), sib))
                passed[-1].start()
        for a in range(na):
            copy(a, 0, sib, me).wait_recv()
            for j, chip in enumerate(chips):
                copy(a, 4 + j, (*chip, 1 - mc), me).wait_recv()
        for cp in first + passed:
            cp.wait_send()
        for cp in mine:
            cp.wait()

    scratch = [pltpu.SemaphoreType.DMA((7 * na,)), pltpu.SemaphoreType.DMA((7 * na,)), pltpu.SemaphoreType.DMA((na,))]
    return _Side(xs, [S((NDEV,) + x.shape, x.dtype) for x in xs], scratch, start, finish)


def _side_exchange(arrs, nslot, out_slots, route):
    na = len(arrs)

    def copies(in_refs, out_refs, sems):
        send_sems, recv_sems = sems
        return [pltpu.make_async_remote_copy(
            src_ref=in_refs[a].at[src], dst_ref=out_refs[a].at[k], send_sem=send_sems.at[nslot * a + k],
            recv_sem=recv_sems.at[nslot * a + k], device_id=to, device_id_type=MESH)
            for a in range(na) for k, (src, to) in enumerate(route(*_me()))]

    def start(in_refs, out_refs, sems):
        for cp in copies(in_refs, out_refs, sems):
            cp.start()

    def finish(in_refs, out_refs, sems):
        cps = copies(in_refs, out_refs, sems)
        for cp in cps:
            cp.wait_recv()
        for cp in cps:
            cp.wait_send()

    scratch = [pltpu.SemaphoreType.DMA((nslot * na,)), pltpu.SemaphoreType.DMA((nslot * na,))]
    return _Side(arrs, [S((out_slots,) + x.shape[1:], x.dtype) for x in arrs], scratch, start, finish)


def _side_rs_sibling(gs):
    return _side_exchange(gs, 4, 4, lambda mx, my, mc: [(2 * ch + 1 - mc, (mx, my, 1 - mc)) for ch in range(4)])


def _side_rs_chips(ps):
    return _side_exchange(ps, 3, 3, lambda mx, my, mc: [(2 * cx + cy, (cx, cy, mc)) for cx, cy in
                                                        [(1 - mx, my), (mx, 1 - my), (1 - mx, 1 - my)]])


_COLW = 512
_COL_NAMES = ("l0_w_in", "l0_w_ff1", "l1_w_in", "l1_w_ff1")
_ROW_NAMES = ("l0_w_out", "l0_w_ff2", "l1_w_out", "l1_w_ff2")
_BIG_NAMES = _COL_NAMES + _ROW_NAMES


_W_IN0_RUNS = ((0, 1536, 0), (1536, 1544, 3584), (1544, 3080, 1536), (3080, 3088, 3592), (3088, 3600, 3072))


def _w_in0_slabs(g, ncols):
    slabs = []
    for d in range(NDEV):
        lo, hi = d * ncols, (d + 1) * ncols
        parts = [g[:, r + max(lo, a) - a:r + min(hi, b) - a] for a, b, r in _W_IN0_RUNS if max(lo, a) < min(hi, b)]
        slabs.append(jnp.concatenate(parts + [jnp.zeros((D, _COLW - ncols), g.dtype)], axis=1))
    return jnp.stack(slabs)


def _w_in0_regrouped(gathered, ncols):
    parts = []
    for a, b, _ in sorted(_W_IN0_RUNS, key=lambda run: run[2]):
        for d in range(a // ncols, (b - 1) // ncols + 1):
            parts.append(gathered[d, :, max(a, d * ncols) - d * ncols:min(b, (d + 1) * ncols) - d * ncols])
    used = sum(p.shape[1] for p in parts)
    return jnp.concatenate(parts + [jnp.zeros((D, ZW0 - used), gathered.dtype)], axis=1)


def _sq(t):
    return t * t


def _epi_res_norm(acc, res, gain):
    y = acc + res
    h = _f_norm(y, gain)[0]
    return y, h, h


_RES_NORM_OUTS = [("tile", f32), ("tile", bf16), ("tile_t", bf16)]


def _epi_norm_bwd(acc, x, dres, gain):
    _, vjp = jax.vjp(lambda xx, gg: _f_norm(xx, gg)[0], x, gain)
    dx, dgain = vjp(acc)
    dx = dx + dres
    return dx, dgain, dx


_NORM_BWD_OUTS = [("tile", f32), ("rows", f32), ("tile_t", bf16)]


def _epi_loss(acc, res, tgt):
    e = acc + res - tgt
    dy = e * (1.0 / D)
    return dy, dy, jnp.sum(e * e, axis=0, keepdims=True)


_LOSS_OUTS = [("tile", f32), ("tile_t", bf16), ("rows", f32)]


def _mlp_fwd(tag, x, h, w1, w2, epi, extra, consts, outs):
    a = _mm(f"{tag}_ff1", h, w1, NN, bf16, epi=lambda acc: jnp.maximum(acc, 0.0))
    return _mm(f"{tag}_ff2", a, w2, NN, f32, a_fn=_sq, epi=epi, extra=(x,) + tuple(extra), consts=consts, outs=outs), a


def _mlp_bwd(tag, x, gain, w1, w2, ht, a, dy, dyt, side=None):
    da = _mm(f"{tag}_ff2_dx", dy, w2, NT, bf16, epi=lambda acc, av: acc * 2.0 * av.astype(f32), extra=(a,), side=side)
    if side is not None:
        da, side_res = da
    dw2 = _mm(f"{tag}_ff2_dw", dyt, a, NN, bf16, b_fn=_sq, out_t=True)
    res = _mm(f"{tag}_ff1_dx", da, w1, NT, f32, epi=_epi_norm_bwd, extra=(x, dy), consts=(gain,), outs=_NORM_BWD_OUTS)
    dw1 = _mm(f"{tag}_ff1_dw", ht, da, NN, bf16, slab=True)
    return (res, dw1, dw2) if side is None else ((res, dw1, dw2), side_res)


def _row(v):
    return v.reshape(1, -1).astype(f32)


_L0_REST = ("l0_w_ff1", "l0_w_out")
_L1_MIX = ("l0_w_ff2", "l1_w_in", "l1_w_out")
_L1_FFN = ("l1_w_ff1", "l1_w_ff2")
_GRAD_A = ("l1_w_ff1", "l1_w_ff2", "l1_w_out", "l1_w_in")
_GRAD_B = ("l0_w_ff1", "l0_w_ff2", "l0_w_out")
_IN_PLACE = ("l0_w_ff1", "l1_w_in", "l1_w_ff1")


def _train(x, tgt, args, mom, var, t):
    n = x.shape[0]
    nchunk = t // CH
    mx, my, mc = _me()
    dev, chip = 4 * mx + 2 * my + mc, 2 * mx + my
    core_idx = jnp.reshape(mc, (1,)).astype(jnp.int32)
    own_idx = jnp.stack([dev, chip]).astype(jnp.int32)
    ncols = {nm: args[nm].shape[1] for nm in _COL_NAMES}
    conv_cols = args["l0_gdn_conv"].shape[1]
    g, big, w = {}, {}, {}

    def send(nm):
        a = args[nm].astype(bf16)
        return jnp.pad(a, ((0, 0), (0, _COLW - ncols[nm]))) if nm in _COL_NAMES else a

    def take(names, gathered):
        for nm, arr in zip(names, gathered):
            w[nm] = arr if nm in _IN_PLACE else arr.reshape(-1, D)

    def by_dev(nm, ga):
        return ga if nm in _COL_NAMES else ga.reshape(NDEV, -1, D)

    def pair(names, gs, r1s):
        return _pair_sum(f"rs_pair_sum_{names[0]}_group", core_idx, gs, r1s)

    def adam(names, gs, r1s, r2s):
        res = _adamw_big(f"adamw_{names[0]}_group", own_idx, gs, r1s, r2s, *[[src[nm] for nm in names] for src in (args, mom, var)])
        big.update(zip(names, res))


    li = jnp.arange(FOX_W)
    pm = jnp.where((li[:, None] // FOX_D) == (li[None, :] // FOX_D), 1.0 / FOX_D, 0.0).astype(f32)
    lane_head = jnp.arange(GDN_W) // HD
    sel = lambda first_lane: (jnp.arange(LANES)[:, None] == (first_lane + lane_head)[None, :]).astype(f32)
    e_beta, e_alpha = sel(FOX_H), sel(FOX_H + GDN_H)
    alog_b, dt_b = _row(jnp.repeat(args["l0_gdn_A_log"], HD)), _row(jnp.repeat(args["l0_gdn_dt_bias"], HD))
    gq_t, gk_t = _row(jnp.tile(args["l0_fox_q_norm"], FOX_H)), _row(jnp.tile(args["l0_fox_k_norm"], FOX_H))
    on0_t, on1_t = _row(jnp.tile(args["l0_gdn_o_norm"], GDN_H)), _row(jnp.tile(args["l1_hgrn_o_norm"], HG_H))
    fbias = jnp.pad(_row(args["l0_fox_f_bias"]), ((0, 0), (0, LANES - FOX_H)))
    g0m, g0f, g1m, g1f = (_row(args[k]) for k in ("l0_mix_norm", "l0_ffn_norm", "l1_mix_norm", "l1_ffn_norm"))
    lbl = args["hgrn_lb_logits"].astype(f32)

    first = [send("l0_w_in"), jnp.pad(args["l0_gdn_conv"], ((0, 4), (0, LANES * 2 - conv_cols)))]
    (h0, h0t), first = _tok_fwd("l0_mix_norm", _f_norm, [(x, D, 0)], [(g0m, None, 0)], [(D, bf16)], TOK_ROWS, also_t=(0,),
                                side=_side_allgather(first))
    w_in0 = _w_in0_regrouped(first[0], ncols["l0_w_in"])
    wconv = first[1][:, :4, :conv_cols].transpose(1, 0, 2).reshape(4, NDEV * conv_cols)
    z0 = _mm("l0_in", h0, w_in0, NN, f32)
    qk_rows = [(z0, 2 * FOX_W, 0)]
    qk_consts = [(gq_t, None, 0), (gk_t, None, 0), (pm, None, 0)]
    qn, kn = _tok_fwd("fox_pre", _f_foxpre, qk_rows, qk_consts, [(FOX_W, f32)] * 2, TOK_ROWS)
    ccol, crow = _fox_gate_fwd(z0, fbias, t)
    fox_o, got = _fox_attn_fwd(qn, kn, z0, ccol, crow, t, side=_side_allgather([send(nm) for nm in _L0_REST]))
    take(_L0_REST, got)
    conv_rows, conv_consts = [(z0, LANES, Z0_GQKV)], [(wconv, LANES, 0)]
    (qkv,) = _tok_fwd("gdn_conv", _f_conv, conv_rows, conv_consts, [(LANES, f32)], t, ncb=12, with_j=True)
    gate_consts = [(e_beta, None, 0), (e_alpha, None, 0), (alog_b, None, 0), (dt_b, None, 0)]
    intra_rows = [(qkv, GDN_W, 0), (qkv, GDN_W, 1), (qkv, GDN_W, 2), (z0, LANES, Z0_SMALL)]
    intra, got = _tok_fwd("gdn_intra", _f_gdn_gated, intra_rows, gate_consts, [(GDN_W, f32)] * 7, 2 * CH,
                          side=_side_allgather([send(nm) for nm in _L1_MIX]))
    take(_L1_MIX, got)
    inter_rows = [(a, 0, 1) for a in intra[:6]]
    full_rows = inter_rows + [(fox_o, 0, 1), (z0, Z0_GG // 4, 1)]
    (cat0, gdn_hist), got = _scan_fwd("gdn_scan", _f_gdn_full, full_rows, GDN_H, nchunk, consts=(on0_t,), out_dtype=bf16, out_k=2,
                                      side=_side_allgather([send(nm) for nm in _L1_FFN[1:]]))
    take(_L1_FFN[1:], got)
    x1, hf0, hf0t = _mm("l0_out", cat0, w["l0_w_out"], NN, f32, epi=_epi_res_norm, extra=(x,), consts=(g0f,), outs=_RES_NORM_OUTS)
    (x2, h1, h1t), a0 = _mlp_fwd("l0", x1, hf0, w["l0_w_ff1"], w["l0_w_ff2"], _epi_res_norm, (), (g1m,), _RES_NORM_OUTS)

    z1 = _mm("l1_in", h1, w["l1_w_in"], NN, f32)
    (cat1, hg_hist), got = _scan_fwd("hgrn_scan", _f_hgrn_full, [(z1, 0, 4)], HG_H, nchunk, consts=(lbl, on1_t), out_dtype=bf16,
                                     side=_side_allgather([send(nm) for nm in _L1_FFN[:1]]))
    take(_L1_FFN[:1], got)
    x3, hf1, hf1t = _mm("l1_out", cat1, w["l1_w_out"], NN, f32, epi=_epi_res_norm, extra=(x2,), consts=(g1f,), outs=_RES_NORM_OUTS)
    (dy, dyt, loss_row), a1 = _mlp_fwd("l1", x3, hf1, w["l1_w_ff1"], w["l1_w_ff2"], _epi_loss, (tgt,), (), _LOSS_OUTS)

    (dx3, g["l1_ffn_norm"], dx3t), ga_ff1, ga_ff2 = _mlp_bwd("l1", x3, g1f, w["l1_w_ff1"], w["l1_w_ff2"], hf1t, a1, dy, dyt)
    dcat1 = _mm("l1_out_dx", dx3, w["l1_w_out"], NT, f32)
    ga_out = _mm("l1_out_dw", dx3t, cat1, NN, bf16, out_t=True)
    dz1, dlbl, don1 = _scan_bwd("hgrn_scan_bwd", _f_hgrn_full, [(z1, 0, 4)], hg_hist, dcat1, HG_H, nchunk, dtypes=[bf16],
                                consts=(lbl, on1_t))
    dx2, g["l1_mix_norm"], dx2t = _mm("l1_in_dx", dz1, w["l1_w_in"], NT, f32, epi=_epi_norm_bwd, extra=(x2, dx3), consts=(g1m,),
                                      outs=_NORM_BWD_OUTS)
    ga_in = _mm("l1_in_dw", h1t, dz1, NN, bf16, slab=True)
    g["l1_hgrn_o_norm"] = don1.reshape(HG_H, HD).sum(0)
    g["hgrn_lb_logits"] = dlbl
    gs_a = [by_dev(nm, ga) for nm, ga in zip(_GRAD_A, (ga_ff1, ga_ff2, ga_out, ga_in))]

    (dx1, g["l0_ffn_norm"], dx1t), gb_ff1, gb_ff2 = _mlp_bwd("l0", x1, g0f, w["l0_w_ff1"], w["l0_w_ff2"], hf0t, a0, dx2, dx2t)
    dcat0 = _mm("l0_out_dx", dx1, w["l0_w_out"], NT, f32)
    gb_out = _mm("l0_out_dw", dx1t, cat0, NN, bf16, out_t=True)
    gs_b = [by_dev(nm, ga) for nm, ga in zip(_GRAD_B, (gb_ff1, gb_ff2, gb_out))]
    (*dinter, dfox_o, dgg, don0), r1_a = _scan_bwd("gdn_scan_bwd", _f_gdn_full, full_rows, gdn_hist, dcat0, GDN_H, nchunk,
                                                   consts=(on0_t,), out_k=2, dtypes=[f32] * 7 + [bf16], cps=1,
                                                   side=_side_rs_sibling(gs_a))
    pairs_a = pair(_GRAD_A, gs_a, r1_a)
    (dqn, dkn, dz0, dcq, dck), got = _fox_attn_bwd(qn, kn, z0, ccol, crow, dfox_o, t,
                                                   side=_join_sides([_side_rs_chips(pairs_a), _side_rs_sibling(gs_b)]))
    r2_a, r1_b = got[:len(_GRAD_A)], got[len(_GRAD_A):]
    adam(_GRAD_A, gs_a, r1_a, r2_a)
    pairs_b = pair(_GRAD_B, gs_b, r1_b)
    (dqkv, dzs_g, dalog_b, ddt_b), r2_b = _tok_bwd(
        "gdn_intra_bwd", _f_gdn_gated_given, intra_rows + [(intra[6], GDN_W, 0)], gate_consts, [(a, GDN_W) for a in dinter],
        2 * CH, [0, 1, 2, 3], [2, 3], ncat=3, side=_side_rs_chips(pairs_b))
    adam(_GRAD_B, gs_b, r1_b, r2_b)
    dz0, dgq_t, dgk_t = _tok_bwd("fox_pre_bwd", _f_foxpre, qk_rows, qk_consts, [(dqn, FOX_W), (dkn, FOX_W)], TOK_ROWS, [0], [0, 1],
                                 drow_dtype=bf16, into=(dz0, ZW0, 0))
    dz0, dwconv = _tok_bwd("gdn_conv_bwd", _f_conv, conv_rows, conv_consts, [(dqkv, LANES)], t, [0], [0], ncb=12, with_j=True,
                           drow_dtype=bf16, into=(dz0, ZW0, Z0_GQKV * LANES))
    dzs_f, dfb = _fox_gate_bwd(z0, fbias, dcq, dck, t)
    dz0 = dz0.at[:, Z0_GG * LANES:Z0_SMALL * LANES].set(dgg)
    dz0 = dz0.at[:, Z0_SMALL * LANES:].set(jnp.pad((dzs_g + dzs_f).astype(bf16), ((0, 0), (0, ZW0 - (Z0_SMALL + 1) * LANES))))
    gs_c = [_w_in0_slabs(_mm("l0_in_dw", h0t, dz0, NN, bf16), ncols["l0_w_in"])]
    r1_c = _rs_sibling(gs_c)
    pairs_c = pair(("l0_w_in",), gs_c, r1_c)
    (dx, g["l0_mix_norm"]), r2_c = _mm("l0_in_dx", dz0, w_in0, NT, f32, epi=lambda *a: _epi_norm_bwd(*a)[:2], extra=(x, dx1),
                                       consts=(g0m,), outs=_NORM_BWD_OUTS[:2], side=_side_rs_chips(pairs_c))
    adam(("l0_w_in",), gs_c, r1_c, r2_c)
    g["l0_fox_q_norm"] = dgq_t.reshape(FOX_H, FOX_D).sum(0)
    g["l0_fox_k_norm"] = dgk_t.reshape(FOX_H, FOX_D).sum(0)
    g["l0_fox_f_bias"] = dfb[0, :FOX_H]
    g["l0_gdn_conv"] = dwconv
    g["l0_gdn_A_log"] = dalog_b.reshape(GDN_H, HD).sum(1)
    g["l0_gdn_dt_bias"] = ddt_b.reshape(GDN_H, HD).sum(1)
    g["l0_gdn_o_norm"] = don0.reshape(GDN_H, HD).sum(0)
    return loss_row, dx, g, big


_NAMES = ("l0_mix_norm", "l0_w_in", "l0_fox_q_norm", "l0_fox_k_norm", "l0_fox_f_bias", "l0_gdn_conv", "l0_gdn_A_log",
          "l0_gdn_dt_bias", "l0_gdn_o_norm", "l0_w_out", "l0_ffn_norm", "l0_w_ff1", "l0_w_ff2", "l1_mix_norm", "l1_w_in",
          "l1_hgrn_o_norm", "l1_w_out", "l1_ffn_norm", "l1_w_ff1", "l1_w_ff2", "hgrn_lb_logits")
_SMALL_NAMES = tuple(nm for nm in _NAMES if nm not in _BIG_NAMES)
_SMALL_ROWS = 16


def _pack_small(vals):
    flat = jnp.concatenate([vals[nm].reshape(-1).astype(f32) for nm in _SMALL_NAMES])
    return jnp.pad(flat, (0, _SMALL_ROWS * D - flat.shape[0])).reshape(_SMALL_ROWS, D)


def _unpack_small(packed, shapes):
    flat = packed.reshape(-1)
    out, off = {}, 0
    for nm in _SMALL_NAMES:
        size = 1
        for s in shapes[nm]:
            size *= s
        out[nm] = flat[off:off + size].reshape(shapes[nm])
        off += size
    return out, off


def kernel(x, l0_mix_norm, l0_w_in, l0_fox_q_norm, l0_fox_k_norm, l0_fox_f_bias, l0_gdn_conv, l0_gdn_A_log, l0_gdn_dt_bias, l0_gdn_o_norm, l0_w_out, l0_ffn_norm, l0_w_ff1, l0_w_ff2, l1_mix_norm, l1_w_in, l1_hgrn_o_norm, l1_w_out, l1_ffn_norm, l1_w_ff1, l1_w_ff2, hgrn_lb_logits, loss_target, m_l0_mix_norm, m_l0_w_in, m_l0_fox_q_norm, m_l0_fox_k_norm, m_l0_fox_f_bias, m_l0_gdn_conv, m_l0_gdn_A_log, m_l0_gdn_dt_bias, m_l0_gdn_o_norm, m_l0_w_out, m_l0_ffn_norm, m_l0_w_ff1, m_l0_w_ff2, m_l1_mix_norm, m_l1_w_in, m_l1_hgrn_o_norm, m_l1_w_out, m_l1_ffn_norm, m_l1_w_ff1, m_l1_w_ff2, m_hgrn_lb_logits, v_l0_mix_norm, v_l0_w_in, v_l0_fox_q_norm, v_l0_fox_k_norm, v_l0_fox_f_bias, v_l0_gdn_conv, v_l0_gdn_A_log, v_l0_gdn_dt_bias, v_l0_gdn_o_norm, v_l0_w_out, v_l0_ffn_norm, v_l0_w_ff1, v_l0_w_ff2, v_l1_mix_norm, v_l1_w_in, v_l1_hgrn_o_norm, v_l1_w_out, v_l1_ffn_norm, v_l1_w_ff1, v_l1_w_ff2, v_hgrn_lb_logits):
    args = dict(zip(_NAMES, (l0_mix_norm, l0_w_in, l0_fox_q_norm, l0_fox_k_norm, l0_fox_f_bias, l0_gdn_conv, l0_gdn_A_log, l0_gdn_dt_bias, l0_gdn_o_norm, l0_w_out, l0_ffn_norm, l0_w_ff1, l0_w_ff2, l1_mix_norm, l1_w_in, l1_hgrn_o_norm, l1_w_out, l1_ffn_norm, l1_w_ff1, l1_w_ff2, hgrn_lb_logits)))
    mom = dict(zip(_NAMES, (m_l0_mix_norm, m_l0_w_in, m_l0_fox_q_norm, m_l0_fox_k_norm, m_l0_fox_f_bias, m_l0_gdn_conv, m_l0_gdn_A_log, m_l0_gdn_dt_bias, m_l0_gdn_o_norm, m_l0_w_out, m_l0_ffn_norm, m_l0_w_ff1, m_l0_w_ff2, m_l1_mix_norm, m_l1_w_in, m_l1_hgrn_o_norm, m_l1_w_out, m_l1_ffn_norm, m_l1_w_ff1, m_l1_w_ff2, m_hgrn_lb_logits)))
    var = dict(zip(_NAMES, (v_l0_mix_norm, v_l0_w_in, v_l0_fox_q_norm, v_l0_fox_k_norm, v_l0_fox_f_bias, v_l0_gdn_conv, v_l0_gdn_A_log, v_l0_gdn_dt_bias, v_l0_gdn_o_norm, v_l0_w_out, v_l0_ffn_norm, v_l0_w_ff1, v_l0_w_ff2, v_l1_mix_norm, v_l1_w_in, v_l1_hgrn_o_norm, v_l1_w_out, v_l1_ffn_norm, v_l1_w_ff1, v_l1_w_ff2, v_hgrn_lb_logits)))
    nb, t, _ = x.shape
    dev = 4 * lax.axis_index("x") + 2 * lax.axis_index("y") + lax.axis_index("c")
    conv_cols = l0_gdn_conv.shape[1]
    loss_row, dx, g, big = _train(x.reshape(nb * t, D), loss_target.reshape(nb * t, D), args, mom, var, t)

    shapes = {nm: args[nm].shape for nm in _SMALL_NAMES}
    gsm = dict(g)
    gsm["l0_gdn_conv"] = jnp.zeros(shapes["l0_gdn_conv"], f32)
    packed = _pack_small(gsm)
    _, used = _unpack_small(packed, shapes)
    flat_extra = jnp.concatenate([jnp.sum(loss_row).reshape(1), g["l0_gdn_conv"].reshape(-1)])
    packed = packed.reshape(-1).at[used:used + flat_extra.shape[0]].set(flat_extra).reshape(_SMALL_ROWS, D)
    (parts,) = _allgather("ag_small", [packed])
    total = _sum_parts(parts).reshape(-1)
    loss = 0.5 * total[used] / D
    conv_g_full = total[used + 1:used + 1 + 4 * NDEV * conv_cols].reshape(4, NDEV * conv_cols)
    conv_g = lax.dynamic_slice(conv_g_full, (0, dev * conv_cols), (4, conv_cols))
    own_vals = {nm: jnp.zeros(shapes[nm], f32) for nm in _SMALL_NAMES}
    own_vals["l0_gdn_conv"] = conv_g
    own_mask = {nm: jnp.zeros(shapes[nm], f32) for nm in _SMALL_NAMES}
    own_mask["l0_gdn_conv"] = jnp.ones(shapes["l0_gdn_conv"], f32)
    small = _adamw_small(parts, _pack_small(args), _pack_small(mom), _pack_small(var), _pack_small(own_mask), _pack_small(own_vals))
    small = [_unpack_small(a, shapes)[0] for a in small]
    small[0]["l0_gdn_conv"] = conv_g

    outs = [loss, dx.reshape(nb, t, D)]
    for k in range(4):
        outs += [big[nm][k] if nm in _BIG_NAMES else small[k][nm] for nm in _NAMES]
    return tuple(outs)
```
